```python
import math
import jax
import jax.numpy as jnp
from jax import lax
import numpy as np

D_MODEL = 1024
BATCH = 16
SEQ = 2048
DEPTH = 1

N_META = 16
GDN_HEADS = 4
GDN_HEAD_DIM = 128
GDN_WIDTH = GDN_HEADS * GDN_HEAD_DIM
GDN_CONV = 4
CHUNK = 64
SC_WIDTH = D_MODEL - GDN_WIDTH
SC_GROUPS = 8
SC_CONV = 3
MIX_WIDTH = GDN_WIDTH + SC_WIDTH
D_FF = -(-8 * D_MODEL // (3 * 256)) * 256
IN_SPLITS = (GDN_WIDTH, GDN_WIDTH, GDN_WIDTH, GDN_WIDTH, GDN_HEADS, GDN_HEADS, SC_WIDTH, SC_WIDTH, SC_WIDTH)
IN_WIDTH = sum(IN_SPLITS)
EPS = 1e-6

kernel_name = 'hymba_gdn_shortconv_block'


def rms_norm(x, w):
    xf = x.astype(jnp.float32)
    y = xf * lax.rsqrt(jnp.mean(xf * xf, axis=-1, keepdims=True) + EPS)
    return (y * w.astype(jnp.float32)).astype(x.dtype)


def l2_normalize(x):
    xf = x.astype(jnp.float32)
    return xf * lax.rsqrt(jnp.sum(xf * xf, axis=-1, keepdims=True) + EPS)


def causal_depthwise_conv(x, w):
    k_width = w.shape[0]
    seq_len = x.shape[1]
    xp = jnp.pad(x, ((0, 0), (k_width - 1, 0), (0, 0)))
    return sum(xp[:, i:i + seq_len] * w[i].astype(x.dtype) for i in range(k_width))


def chunked_gated_delta_rule(q, k, v, g, beta):
    b, seq_len, n_heads, dk = q.shape
    dv = v.shape[-1]
    pad = (-seq_len) % CHUNK
    f32 = jnp.float32

    def to_chunks(t):
        t = jnp.pad(t.astype(f32), ((0, 0), (pad, 0)) + ((0, 0),) * (t.ndim - 2))
        n = t.shape[1] // CHUNK
        t = t.reshape((b, n, CHUNK) + t.shape[2:])
        return jnp.moveaxis(t, 3, 1)

    qc, kc, vc, g_raw, bc = (to_chunks(t) for t in (q, k, v, g, beta))
    gc = jnp.cumsum(g_raw, axis=-1)
    idx = jnp.arange(CHUNK)
    incl = idx[:, None] >= idx[None, :]
    strict = idx[:, None] > idx[None, :]
    diff = gc[..., :, None] - gc[..., None, :]
    decay = jnp.where(incl, jnp.exp(jnp.where(incl, diff, 0.0)), 0.0)
    kb = kc * bc[..., None]
    a = jnp.where(strict, jnp.einsum('bhnid,bhnjd->bhnij', kb, kc) * decay, 0.0)
    eye = jnp.eye(CHUNK, dtype=f32)
    t_inv = lax.linalg.triangular_solve(a + eye, jnp.broadcast_to(eye, a.shape), left_side=True, lower=True)
    u = jnp.einsum('bhnij,bhnje->bhnie', t_inv, vc * bc[..., None])
    w = jnp.einsum('bhnij,bhnjd->bhnid', t_inv, kb * jnp.exp(gc)[..., None])
    qk = jnp.where(incl, jnp.einsum('bhnid,bhnjd->bhnij', qc, kc) * decay, 0.0)
    q_dec = qc * jnp.exp(gc)[..., None]
    k_dec = kc * jnp.exp(gc[..., -1:] - gc)[..., None]
    g_last = jnp.exp(gc[..., -1])

    def step(state, xs):
        q_i, k_i, u_i, w_i, qk_i, gl_i = xs
        v_new = u_i - jnp.einsum('bhcd,bhde->bhce', w_i, state)
        o_i = jnp.einsum('bhcd,bhde->bhce', q_i, state) + jnp.einsum('bhij,bhje->bhie', qk_i, v_new)
        state = state * gl_i[..., None, None] + jnp.einsum('bhcd,bhce->bhde', k_i, v_new)
        return state, o_i

    xs = tuple(jnp.moveaxis(t, 2, 0) for t in (q_dec, k_dec, u, w, qk, g_last))
    state0 = jnp.zeros((b, n_heads, dk, dv), f32)
    _, o = lax.scan(step, state0, xs)
    o = jnp.moveaxis(o, 0, 2).reshape(b, n_heads, -1, dv)
    return jnp.transpose(o, (0, 2, 1, 3))[:, pad:]


def token_mixer(u, w_in, conv_qkv, a_log, dt_bias, gdn_norm, conv_sc, w_out):
    b, seq_len, _ = u.shape
    f32 = jnp.float32
    proj = u @ w_in
    cuts = [int(c) for c in np.cumsum(IN_SPLITS)[:-1]]
    q, k, v, z, b_logit, a_logit, sc_x, sc_b, sc_c = jnp.split(proj, cuts, axis=-1)

    qkv = jax.nn.silu(causal_depthwise_conv(jnp.concatenate([q, k, v], axis=-1), conv_qkv))
    q, k, v = (t.reshape(b, seq_len, GDN_HEADS, GDN_HEAD_DIM) for t in jnp.split(qkv, 3, axis=-1))
    q = l2_normalize(q) * (GDN_HEAD_DIM ** -0.5)
    k = l2_normalize(k)
    beta = jax.nn.sigmoid(b_logit.astype(f32))
    g = -jnp.exp(a_log.astype(f32)) * jax.nn.softplus(a_logit.astype(f32) + dt_bias.astype(f32))
    o = chunked_gated_delta_rule(q, k, v, g, beta)
    gate = jax.nn.silu(z.astype(f32)).reshape(b, seq_len, GDN_HEADS, GDN_HEAD_DIM)
    o = (rms_norm(o, gdn_norm) * gate).astype(u.dtype).reshape(b, seq_len, GDN_WIDTH)

    y_sc = sc_b * causal_depthwise_conv(sc_c * sc_x, conv_sc)

    return jnp.concatenate([o, y_sc], axis=-1) @ w_out


def swiglu(u, w_gate, w_up, w_down):
    return (jax.nn.silu(u @ w_gate) * (u @ w_up)) @ w_down


def _fwd_setup_inputs(seed: int = 0) -> dict:
    key = jax.random.key(seed)
    ks = jax.random.split(key, 17)
    f32 = jnp.float32

    def nrm(k, shape, scale):
        return jax.random.normal(k, shape, f32) * scale

    def gain(k, width):
        return 1.0 + 0.02 * jax.random.normal(k, (DEPTH, width), f32)

    dt = jnp.exp(jax.random.uniform(ks[9], (DEPTH, GDN_HEADS), f32, math.log(1e-3), math.log(1e-1)))
    return {
        'x': nrm(ks[0], (BATCH, SEQ, D_MODEL), 1.0),
        'meta_tokens': nrm(ks[1], (N_META, D_MODEL), 1.0),
        'mix_pre_norm': gain(ks[2], D_MODEL),
        'mix_post_norm': gain(ks[3], D_MODEL),
        'ffn_pre_norm': gain(ks[4], D_MODEL),
        'ffn_post_norm': gain(ks[5], D_MODEL),
        'w_in': nrm(ks[6], (DEPTH, D_MODEL, IN_WIDTH), D_MODEL ** -0.5),
        'conv_qkv': nrm(ks[7], (DEPTH, GDN_CONV, 3 * GDN_WIDTH), GDN_CONV ** -0.5),
        'a_log': jnp.log(jax.random.uniform(ks[8], (DEPTH, GDN_HEADS), f32, 1.0, 16.0)),
        'dt_bias': dt + jnp.log(-jnp.expm1(-dt)),
        'gdn_norm': gain(ks[10], GDN_HEAD_DIM),
        'conv_sc': nrm(ks[11], (DEPTH, SC_CONV, SC_WIDTH), SC_CONV ** -0.5),
        'w_out': nrm(ks[12], (DEPTH, MIX_WIDTH, D_MODEL), MIX_WIDTH ** -0.5),
        'w_gate': nrm(ks[13], (DEPTH, D_MODEL, D_FF), D_MODEL ** -0.5),
        'w_up': nrm(ks[14], (DEPTH, D_MODEL, D_FF), D_MODEL ** -0.5),
        'w_down': nrm(ks[15], (DEPTH, D_FF, D_MODEL), D_FF ** -0.5),
    }


def _fwd_reference(x, meta_tokens, mix_pre_norm, mix_post_norm, ffn_pre_norm, ffn_post_norm, w_in, conv_qkv,
              a_log, dt_bias, gdn_norm, conv_sc, w_out, w_gate, w_up, w_down):
    b = x.shape[0]
    meta = jnp.broadcast_to(meta_tokens.astype(x.dtype)[None], (b, N_META, D_MODEL))
    h = jnp.concatenate([meta, x], axis=1)
    for l in range(DEPTH):
        mix = token_mixer(rms_norm(h, mix_pre_norm[l]), w_in[l], conv_qkv[l], a_log[l], dt_bias[l],
                          gdn_norm[l], conv_sc[l], w_out[l])
        h = h + rms_norm(mix, mix_post_norm[l])
        ffn = swiglu(rms_norm(h, ffn_pre_norm[l]), w_gate[l], w_up[l], w_down[l])
        h = h + rms_norm(ffn, ffn_post_norm[l])
    return h[:, N_META:]


import jax as _jax
import jax.numpy as _jnp

TWIN_FORMAT = 'train_step'
FWD_PARAMS = ['x', 'meta_tokens', 'mix_pre_norm', 'mix_post_norm', 'ffn_pre_norm', 'ffn_post_norm', 'w_in', 'conv_qkv', 'a_log', 'dt_bias', 'gdn_norm', 'conv_sc', 'w_out', 'w_gate', 'w_up', 'w_down']
TWIN_WEIGHTS = ['meta_tokens', 'mix_pre_norm', 'mix_post_norm', 'ffn_pre_norm', 'ffn_post_norm', 'w_in', 'conv_qkv', 'a_log', 'dt_bias', 'gdn_norm', 'conv_sc', 'w_out', 'w_gate', 'w_up', 'w_down']
TWIN_DIFF_INPUT = 'x'
TWIN_INPUTS = ['x', 'meta_tokens', 'mix_pre_norm', 'mix_post_norm', 'ffn_pre_norm', 'ffn_post_norm', 'w_in', 'conv_qkv', 'a_log', 'dt_bias', 'gdn_norm', 'conv_sc', 'w_out', 'w_gate', 'w_up', 'w_down', 'loss_target', 'm_meta_tokens', 'm_mix_pre_norm', 'm_mix_post_norm', 'm_ffn_pre_norm', 'm_ffn_post_norm', 'm_w_in', 'm_conv_qkv', 'm_a_log', 'm_dt_bias', 'm_gdn_norm', 'm_conv_sc', 'm_w_out', 'm_w_gate', 'm_w_up', 'm_w_down', 'v_meta_tokens', 'v_mix_pre_norm', 'v_mix_post_norm', 'v_ffn_pre_norm', 'v_ffn_post_norm', 'v_w_in', 'v_conv_qkv', 'v_a_log', 'v_dt_bias', 'v_gdn_norm', 'v_conv_sc', 'v_w_out', 'v_w_gate', 'v_w_up', 'v_w_down']
TWIN_OUTPUTS = ['loss', 'grad_x', 'grad_meta_tokens', 'grad_mix_pre_norm', 'grad_mix_post_norm', 'grad_ffn_pre_norm', 'grad_ffn_post_norm', 'grad_w_in', 'grad_conv_qkv', 'grad_a_log', 'grad_dt_bias', 'grad_gdn_norm', 'grad_conv_sc', 'grad_w_out', 'grad_w_gate', 'grad_w_up', 'grad_w_down', 'delta_meta_tokens', 'delta_mix_pre_norm', 'delta_mix_post_norm', 'delta_ffn_pre_norm', 'delta_ffn_post_norm', 'delta_w_in', 'delta_conv_qkv', 'delta_a_log', 'delta_dt_bias', 'delta_gdn_norm', 'delta_conv_sc', 'delta_w_out', 'delta_w_gate', 'delta_w_up', 'delta_w_down', 'new_m_meta_tokens', 'new_m_mix_pre_norm', 'new_m_mix_post_norm', 'new_m_ffn_pre_norm', 'new_m_ffn_post_norm', 'new_m_w_in', 'new_m_conv_qkv', 'new_m_a_log', 'new_m_dt_bias', 'new_m_gdn_norm', 'new_m_conv_sc', 'new_m_w_out', 'new_m_w_gate', 'new_m_w_up', 'new_m_w_down', 'new_v_meta_tokens', 'new_v_mix_pre_norm', 'new_v_mix_post_norm', 'new_v_ffn_pre_norm', 'new_v_ffn_post_norm', 'new_v_w_in', 'new_v_conv_qkv', 'new_v_a_log', 'new_v_dt_bias', 'new_v_gdn_norm', 'new_v_conv_sc', 'new_v_w_out', 'new_v_w_gate', 'new_v_w_up', 'new_v_w_down']
TWIN_LEAF_KINDS = {'loss': 'loss', 'grad_x': 'grad_x', 'grad_meta_tokens': 'grad_w', 'grad_mix_pre_norm': 'grad_w', 'grad_mix_post_norm': 'grad_w', 'grad_ffn_pre_norm': 'grad_w', 'grad_ffn_post_norm': 'grad_w', 'grad_w_in': 'grad_w', 'grad_conv_qkv': 'grad_w', 'grad_a_log': 'grad_w', 'grad_dt_bias': 'grad_w', 'grad_gdn_norm': 'grad_w', 'grad_conv_sc': 'grad_w', 'grad_w_out': 'grad_w', 'grad_w_gate': 'grad_w', 'grad_w_up': 'grad_w', 'grad_w_down': 'grad_w', 'delta_meta_tokens': 'delta_w', 'delta_mix_pre_norm': 'delta_w', 'delta_mix_post_norm': 'delta_w', 'delta_ffn_pre_norm': 'delta_w', 'delta_ffn_post_norm': 'delta_w', 'delta_w_in': 'delta_w', 'delta_conv_qkv': 'delta_w', 'delta_a_log': 'delta_w', 'delta_dt_bias': 'delta_w', 'delta_gdn_norm': 'delta_w', 'delta_conv_sc': 'delta_w', 'delta_w_out': 'delta_w', 'delta_w_gate': 'delta_w', 'delta_w_up': 'delta_w', 'delta_w_down': 'delta_w', 'new_m_meta_tokens': 'new_m', 'new_m_mix_pre_norm': 'new_m', 'new_m_mix_post_norm': 'new_m', 'new_m_ffn_pre_norm': 'new_m', 'new_m_ffn_post_norm': 'new_m', 'new_m_w_in': 'new_m', 'new_m_conv_qkv': 'new_m', 'new_m_a_log': 'new_m', 'new_m_dt_bias': 'new_m', 'new_m_gdn_norm': 'new_m', 'new_m_conv_sc': 'new_m', 'new_m_w_out': 'new_m', 'new_m_w_gate': 'new_m', 'new_m_w_up': 'new_m', 'new_m_w_down': 'new_m', 'new_v_meta_tokens': 'new_v', 'new_v_mix_pre_norm': 'new_v', 'new_v_mix_post_norm': 'new_v', 'new_v_ffn_pre_norm': 'new_v', 'new_v_ffn_post_norm': 'new_v', 'new_v_w_in': 'new_v', 'new_v_conv_qkv': 'new_v', 'new_v_a_log': 'new_v', 'new_v_dt_bias': 'new_v', 'new_v_gdn_norm': 'new_v', 'new_v_conv_sc': 'new_v', 'new_v_w_out': 'new_v', 'new_v_w_gate': 'new_v', 'new_v_w_up': 'new_v', 'new_v_w_down': 'new_v'}


def _forward(args):
    return _fwd_reference(*[args[k] for k in FWD_PARAMS])


def _output_shape():
    out = _jax.eval_shape(lambda: _forward(_fwd_setup_inputs(0)))
    return out.shape, out.dtype

N_MICROBATCH = 1
ADAM_LR = 0.001
ADAM_B1 = 0.9
ADAM_B2 = 0.999
ADAM_EPS = 1e-08
ADAM_WD = 0.01
ADAM_STEP = 10
PER_EXAMPLE_BATCH_AXIS = {'x': 0, 'loss_target': 0}
SHARED_INPUTS = []
_WEIGHT_DTYPES = {'meta_tokens': _jnp.float32, 'mix_pre_norm': _jnp.float32, 'mix_post_norm': _jnp.float32, 'ffn_pre_norm': _jnp.float32, 'ffn_post_norm': _jnp.float32, 'w_in': _jnp.float32, 'conv_qkv': _jnp.float32, 'a_log': _jnp.float32, 'dt_bias': _jnp.float32, 'gdn_norm': _jnp.float32, 'conv_sc': _jnp.float32, 'w_out': _jnp.float32, 'w_gate': _jnp.float32, 'w_up': _jnp.float32, 'w_down': _jnp.float32}
MOMENT_SCALE = {'meta_tokens': 1.832347e-02, 'mix_pre_norm': 7.931488e-01, 'mix_post_norm': 3.194994e+01, 'ffn_pre_norm': 6.202777e-01, 'ffn_post_norm': 3.203677e+01, 'w_in': 4.128690e-01, 'conv_qkv': 2.913322e-01, 'a_log': 1.889964e+00, 'dt_bias': 1.849209e+00, 'gdn_norm': 9.420842e-01, 'conv_sc': 6.032470e-01, 'w_out': 5.723449e-01, 'w_gate': 2.047657e-01, 'w_up': 2.952198e-01, 'w_down': 4.895483e-01}


def _to_microbatches(a, axis):
    t = _jnp.moveaxis(a, axis, 0)
    t = t.reshape((N_MICROBATCH, t.shape[0] // N_MICROBATCH) + t.shape[1:])
    return _jnp.moveaxis(t, 1, axis + 1)


def setup_inputs(seed: int = 0) -> dict:
    inp = _fwd_setup_inputs(seed)
    key = _jax.random.fold_in(_jax.random.key(seed), 7919)
    shape, _ = _output_shape()
    out = dict(inp)
    out["loss_target"] = _jax.random.normal(_jax.random.fold_in(key, 0), shape, _jnp.float32)
    for i, name in enumerate(TWIN_WEIGHTS):
        w = inp[name].astype(_jnp.float32)
        if MOMENT_SCALE is None:
            s = _jnp.sqrt(_jnp.mean(_jnp.square(w)) + 1e-30)
        else:
            s = MOMENT_SCALE[name]
        km, kv = _jax.random.split(_jax.random.fold_in(key, i + 1))
        out[name] = w
        out["m_" + name] = s * _jax.random.normal(km, w.shape, _jnp.float32)
        out["v_" + name] = (s * s) * _jax.random.uniform(kv, w.shape, _jnp.float32, 0.5, 1.5)
    if N_MICROBATCH > 1:
        for name, axis in PER_EXAMPLE_BATCH_AXIS.items():
            out[name] = _to_microbatches(out[name], axis)
    return {'x': out['x'], 'meta_tokens': out['meta_tokens'], 'mix_pre_norm': out['mix_pre_norm'], 'mix_post_norm': out['mix_post_norm'], 'ffn_pre_norm': out['ffn_pre_norm'], 'ffn_post_norm': out['ffn_post_norm'], 'w_in': out['w_in'], 'conv_qkv': out['conv_qkv'], 'a_log': out['a_log'], 'dt_bias': out['dt_bias'], 'gdn_norm': out['gdn_norm'], 'conv_sc': out['conv_sc'], 'w_out': out['w_out'], 'w_gate': out['w_gate'], 'w_up': out['w_up'], 'w_down': out['w_down'], 'loss_target': out['loss_target'], 'm_meta_tokens': out['m_meta_tokens'], 'm_mix_pre_norm': out['m_mix_pre_norm'], 'm_mix_post_norm': out['m_mix_post_norm'], 'm_ffn_pre_norm': out['m_ffn_pre_norm'], 'm_ffn_post_norm': out['m_ffn_post_norm'], 'm_w_in': out['m_w_in'], 'm_conv_qkv': out['m_conv_qkv'], 'm_a_log': out['m_a_log'], 'm_dt_bias': out['m_dt_bias'], 'm_gdn_norm': out['m_gdn_norm'], 'm_conv_sc': out['m_conv_sc'], 'm_w_out': out['m_w_out'], 'm_w_gate': out['m_w_gate'], 'm_w_up': out['m_w_up'], 'm_w_down': out['m_w_down'], 'v_meta_tokens': out['v_meta_tokens'], 'v_mix_pre_norm': out['v_mix_pre_norm'], 'v_mix_post_norm': out['v_mix_post_norm'], 'v_ffn_pre_norm': out['v_ffn_pre_norm'], 'v_ffn_post_norm': out['v_ffn_post_norm'], 'v_w_in': out['v_w_in'], 'v_conv_qkv': out['v_conv_qkv'], 'v_a_log': out['v_a_log'], 'v_dt_bias': out['v_dt_bias'], 'v_gdn_norm': out['v_gdn_norm'], 'v_conv_sc': out['v_conv_sc'], 'v_w_out': out['v_w_out'], 'v_w_gate': out['v_w_gate'], 'v_w_up': out['v_w_up'], 'v_w_down': out['v_w_down']}


def _loss(weights, diff, rest, loss_target):
    with _jax.named_scope("forward"):
        args = {**rest, TWIN_DIFF_INPUT: diff, **{k: w.astype(_WEIGHT_DTYPES[k]) for k, w in weights.items()}}
        y = _forward(args)
    with _jax.named_scope("loss_head"):
        err = _jnp.square(y.astype(_jnp.float32) - loss_target)
        return 0.5 * _jnp.sum(_jnp.mean(err, axis=-1)) if err.ndim else 0.5 * err


def _adamw(w, g, m, v):
    m = ADAM_B1 * m + (1.0 - ADAM_B1) * g
    v = ADAM_B2 * v + (1.0 - ADAM_B2) * _jnp.square(g)
    m_hat = m / (1.0 - ADAM_B1 ** ADAM_STEP)
    v_hat = v / (1.0 - ADAM_B2 ** ADAM_STEP)
    delta = -ADAM_LR * (m_hat / (_jnp.sqrt(v_hat) + ADAM_EPS) + ADAM_WD * w)
    return delta, m, v


def reference(x, meta_tokens, mix_pre_norm, mix_post_norm, ffn_pre_norm, ffn_post_norm, w_in, conv_qkv, a_log, dt_bias, gdn_norm, conv_sc, w_out, w_gate, w_up, w_down, loss_target, m_meta_tokens, m_mix_pre_norm, m_mix_post_norm, m_ffn_pre_norm, m_ffn_post_norm, m_w_in, m_conv_qkv, m_a_log, m_dt_bias, m_gdn_norm, m_conv_sc, m_w_out, m_w_gate, m_w_up, m_w_down, v_meta_tokens, v_mix_pre_norm, v_mix_post_norm, v_ffn_pre_norm, v_ffn_post_norm, v_w_in, v_conv_qkv, v_a_log, v_dt_bias, v_gdn_norm, v_conv_sc, v_w_out, v_w_gate, v_w_up, v_w_down):
    given = dict(x=x, meta_tokens=meta_tokens, mix_pre_norm=mix_pre_norm, mix_post_norm=mix_post_norm, ffn_pre_norm=ffn_pre_norm, ffn_post_norm=ffn_post_norm, w_in=w_in, conv_qkv=conv_qkv, a_log=a_log, dt_bias=dt_bias, gdn_norm=gdn_norm, conv_sc=conv_sc, w_out=w_out, w_gate=w_gate, w_up=w_up, w_down=w_down, loss_target=loss_target, m_meta_tokens=m_meta_tokens, m_mix_pre_norm=m_mix_pre_norm, m_mix_post_norm=m_mix_post_norm, m_ffn_pre_norm=m_ffn_pre_norm, m_ffn_post_norm=m_ffn_post_norm, m_w_in=m_w_in, m_conv_qkv=m_conv_qkv, m_a_log=m_a_log, m_dt_bias=m_dt_bias, m_gdn_norm=m_gdn_norm, m_conv_sc=m_conv_sc, m_w_out=m_w_out, m_w_gate=m_w_gate, m_w_up=m_w_up, m_w_down=m_w_down, v_meta_tokens=v_meta_tokens, v_mix_pre_norm=v_mix_pre_norm, v_mix_post_norm=v_mix_post_norm, v_ffn_pre_norm=v_ffn_pre_norm, v_ffn_post_norm=v_ffn_post_norm, v_w_in=v_w_in, v_conv_qkv=v_conv_qkv, v_a_log=v_a_log, v_dt_bias=v_dt_bias, v_gdn_norm=v_gdn_norm, v_conv_sc=v_conv_sc, v_w_out=v_w_out, v_w_gate=v_w_gate, v_w_up=v_w_up, v_w_down=v_w_down)
    weights = {n: given[n] for n in TWIN_WEIGHTS}
    shared = {n: given[n] for n in SHARED_INPUTS}
    per_example = {n: given[n] for n in ['x']}
    grad_fn = _jax.value_and_grad(_loss, argnums=(0, 1))

    def one_microbatch(ex, loss_target):
        ex = dict(ex)
        diff = ex.pop(TWIN_DIFF_INPUT)
        return grad_fn(weights, diff, {**shared, **ex}, loss_target)

    if N_MICROBATCH == 1:
        loss, (grad_w, grad_x) = one_microbatch(per_example, given["loss_target"])
    else:
        def body(carry, xs):
            loss_sum, grad_sum = carry
            l_k, (gw_k, gx_k) = one_microbatch(xs[0], xs[1])
            with _jax.named_scope("update"):
                return (loss_sum + l_k, _jax.tree.map(_jnp.add, grad_sum, gw_k)), gx_k

        init = (_jnp.zeros((), _jnp.float32), _jax.tree.map(_jnp.zeros_like, weights))
        (loss, grad_w), grad_x = _jax.lax.scan(body, init, (per_example, given["loss_target"]))
    with _jax.named_scope("update"):
        delta_w, new_m, new_v = {}, {}, {}
        for n in TWIN_WEIGHTS:
            delta_w[n], new_m[n], new_v[n] = _adamw(weights[n], grad_w[n], given["m_" + n], given["v_" + n])
    return (loss, grad_x, *[grad_w[n] for n in TWIN_WEIGHTS], *[delta_w[n] for n in TWIN_WEIGHTS],
            *[new_m[n] for n in TWIN_WEIGHTS], *[new_v[n] for n in TWIN_WEIGHTS])
```

```python
import functools

import jax
import jax.numpy as jnp
from jax import lax
from jax.experimental import pallas as pl
from jax.experimental.pallas import tpu as pltpu

F32 = jnp.float32
BF16 = jnp.bfloat16
MXU_DTYPE = jnp.bfloat16
MESH = pl.DeviceIdType.MESH

D_MODEL = 1024
N_META = 16
HEADS = 4
HEAD_DIM = 128
GDN_WIDTH = HEADS * HEAD_DIM
GDN_CONV = 4
CHUNK = 64
SC_WIDTH = D_MODEL - GDN_WIDTH
SC_CONV = 3
D_FF = 2816
IN_WIDTH = 4 * GDN_WIDTH + 2 * HEADS + 3 * SC_WIDTH
IN_PAD = 3840
BA_COL = (4 * GDN_WIDTH + 3 * SC_WIDTH) // 128
EPS = 1e-6
LANES = 128
N_CHIPS = 4
VMEM_LIMIT = 48 * 2 ** 20

ADAM_LR = 0.001
ADAM_B1 = 0.9
ADAM_B2 = 0.999
ADAM_EPS = 1e-08
ADAM_WD = 0.01
ADAM_STEP = 10


def _pick(n, candidates):
    for c in candidates:
        if n % c == 0:
            return c
    return n


def _row_tile(n):
    return _pick(n, (352, 256, 176, 128, 64, 32, 16, 8))


def _params(*sem):
    return pltpu.CompilerParams(dimension_semantics=sem, vmem_limit_bytes=VMEM_LIMIT)


def _sigmoid(x):
    return 1.0 / (1.0 + jnp.exp(-x))


def _softplus(x):
    return jnp.maximum(x, 0.0) + jnp.log(1.0 + jnp.exp(-jnp.abs(x)))


def _dsilu(x, s):
    return s * (1.0 + x * (1.0 - s))


def _mm(a, b, mode, out_dtype, name):
    if mode == "tn":
        k_dim, m_dim = a.shape
    else:
        m_dim, k_dim = a.shape
    n_dim = b.shape[0] if mode == "nt" else b.shape[1]
    tm = _pick(m_dim, (1408, 1024, 512, 256, 128) if mode == "tn" else (1056, 1024, 704, 512, 256, 128))
    tn = _pick(n_dim, (1408, 1280, 1024, 768, 512, 256, 128))
    tk = _pick(k_dim, (1408, 1280, 1056, 1024, 512, 256, 128))
    nk = k_dim // tk
    if mode == "nn":
        a_spec = pl.BlockSpec((tm, tk), lambda i, j, k: (i, k))
        b_spec = pl.BlockSpec((tk, tn), lambda i, j, k: (k, j))
        dims = (((1,), (0,)), ((), ()))
    elif mode == "nt":
        a_spec = pl.BlockSpec((tm, tk), lambda i, j, k: (i, k))
        b_spec = pl.BlockSpec((tn, tk), lambda i, j, k: (j, k))
        dims = (((1,), (1,)), ((), ()))
    else:
        a_spec = pl.BlockSpec((tk, tm), lambda i, j, k: (k, i))
        b_spec = pl.BlockSpec((tk, tn), lambda i, j, k: (k, j))
        dims = (((0,), (0,)), ((), ()))

    def body(a_ref, b_ref, o_ref, acc_ref):
        k = pl.program_id(2)
        p = lax.dot_general(a_ref[...], b_ref[...], dims, preferred_element_type=F32)

        @pl.when(k == 0)
        def _():
            acc_ref[...] = p

        @pl.when(k > 0)
        def _():
            acc_ref[...] += p

        @pl.when(k == nk - 1)
        def _():
            o_ref[...] = acc_ref[...].astype(out_dtype)

    return pl.pallas_call(
        body, name=name,
        out_shape=jax.ShapeDtypeStruct((m_dim, n_dim), out_dtype),
        grid=(m_dim // tm, n_dim // tn, nk),
        in_specs=[a_spec, b_spec],
        out_specs=pl.BlockSpec((tm, tn), lambda i, j, k: (i, j)),
        scratch_shapes=[pltpu.VMEM((tm, tn), F32)],
        compiler_params=_params("parallel", "parallel", "arbitrary"),
    )(a, b)


def _rms_apply(x, w):
    r = lax.rsqrt(jnp.mean(x * x, axis=-1, keepdims=True) + EPS)
    return x * r * w


def _rms_bwd(x, w, dy):
    r = lax.rsqrt(jnp.mean(x * x, axis=-1, keepdims=True) + EPS)
    xh = x * r
    dyw = dy * w
    dx = r * (dyw - xh * jnp.mean(dyw * xh, axis=-1, keepdims=True))
    return dx, jnp.sum(dy * xh, axis=0, keepdims=True)


def _accumulate(ref, first, value):
    @pl.when(first)
    def _():
        ref[...] = value

    @pl.when(jnp.logical_not(first))
    def _():
        ref[...] += value


def _rows(tr, width):
    return pl.BlockSpec((tr, width), lambda i: (i, 0))


def _vec(width):
    return pl.BlockSpec((1, width), lambda i: (0, 0))


def _rms_fwd(h, w, name):
    n, d = h.shape
    tr = _row_tile(n)

    def body(h_ref, w_ref, u_ref):
        u_ref[...] = _rms_apply(h_ref[...], w_ref[...]).astype(u_ref.dtype)

    return pl.pallas_call(
        body, name=name, out_shape=jax.ShapeDtypeStruct((n, d), MXU_DTYPE), grid=(n // tr,),
        in_specs=[_rows(tr, d), _vec(d)], out_specs=_rows(tr, d), compiler_params=_params("parallel"),
    )(h, w)


def _mix_residual(h0, mix, w_post, w_pre):
    n, d = h0.shape
    tr = _row_tile(n)

    def body(h0_ref, mix_ref, wpost_ref, wpre_ref, h1_ref, u2_ref):
        h1 = h0_ref[...] + _rms_apply(mix_ref[...], wpost_ref[...])
        h1_ref[...] = h1
        u2_ref[...] = _rms_apply(h1, wpre_ref[...]).astype(u2_ref.dtype)

    return pl.pallas_call(
        body, name="mix_residual",
        out_shape=(jax.ShapeDtypeStruct((n, d), F32), jax.ShapeDtypeStruct((n, d), MXU_DTYPE)), grid=(n // tr,),
        in_specs=[_rows(tr, d), _rows(tr, d), _vec(d), _vec(d)], out_specs=(_rows(tr, d), _rows(tr, d)),
        compiler_params=_params("parallel"),
    )(h0, mix, w_post, w_pre)


def _swiglu_act(gu):
    n = gu.shape[0]
    tr = _pick(n, (176, 128, 64, 32, 16, 8))

    def body(gu_ref, act_ref):
        gate = gu_ref[:, :D_FF]
        act_ref[...] = (gate * _sigmoid(gate) * gu_ref[:, D_FF:]).astype(act_ref.dtype)

    return pl.pallas_call(
        body, name="swiglu_act", out_shape=jax.ShapeDtypeStruct((n, D_FF), MXU_DTYPE), grid=(n // tr,),
        in_specs=[_rows(tr, 2 * D_FF)], out_specs=_rows(tr, D_FF), compiler_params=_params("parallel"),
    )(gu)


def _swiglu_bwd(gu, dact):
    n = gu.shape[0]
    tr = _pick(n, (176, 128, 64, 32, 16, 8))

    def body(gu_ref, dact_ref, dgu_ref):
        gate = gu_ref[:, :D_FF]
        up = gu_ref[:, D_FF:]
        s = _sigmoid(gate)
        da = dact_ref[...]
        dgu_ref[:, :D_FF] = (da * up * _dsilu(gate, s)).astype(dgu_ref.dtype)
        dgu_ref[:, D_FF:] = (da * gate * s).astype(dgu_ref.dtype)

    return pl.pallas_call(
        body, name="swiglu_bwd", out_shape=jax.ShapeDtypeStruct((n, 2 * D_FF), MXU_DTYPE), grid=(n // tr,),
        in_specs=[_rows(tr, 2 * D_FF), _rows(tr, D_FF)], out_specs=_rows(tr, 2 * D_FF),
        compiler_params=_params("parallel"),
    )(gu, dact)


def _loss_head(h1, ffn, w_post, target, rows_per_seq, x_offset):
    n, d = h1.shape
    tr = _row_tile(rows_per_seq)
    tiles_per_seq = rows_per_seq // tr

    def body(h1_ref, ffn_ref, w_ref, t_ref, dh2_ref, dffn_ref, dw_ref, sq_ref):
        i = pl.program_id(0)
        w = w_ref[...]
        f = ffn_ref[...]
        r = lax.rsqrt(jnp.mean(f * f, axis=-1, keepdims=True) + EPS)
        fh = f * r
        row = lax.rem(i, tiles_per_seq) * tr + lax.broadcasted_iota(jnp.int32, (tr, 1), 0)
        err = jnp.where(row >= x_offset, h1_ref[...] + fh * w - t_ref[...], 0.0)
        dh2 = err * (1.0 / d)
        dh2_ref[...] = dh2
        dyw = dh2 * w
        dffn_ref[...] = (r * (dyw - fh * jnp.mean(dyw * fh, axis=-1, keepdims=True))).astype(dffn_ref.dtype)
        _accumulate(dw_ref, i == 0, jnp.sum(dh2 * fh, axis=0, keepdims=True))
        _accumulate(sq_ref, i == 0, jnp.sum(jnp.sum(err * err, axis=1, keepdims=True), axis=0, keepdims=True))

    return pl.pallas_call(
        body, name="loss_head",
        out_shape=(jax.ShapeDtypeStruct((n, d), F32), jax.ShapeDtypeStruct((n, d), MXU_DTYPE),
                   jax.ShapeDtypeStruct((1, d), F32), jax.ShapeDtypeStruct((1, 1), F32)),
        grid=(n // tr,),
        in_specs=[_rows(tr, d), _rows(tr, d), _vec(d), _rows(tr, d)],
        out_specs=(_rows(tr, d), _rows(tr, d), _vec(d), _vec(1)),
        compiler_params=_params("arbitrary"),
    )(h1, ffn, w_post, target)


def _mid_bwd(h1, mix, w_mix_post, w_ffn_pre, dh2, du2):
    n, d = h1.shape
    tr = _row_tile(n)

    def body(h1_ref, mix_ref, wpost_ref, wpre_ref, dh2_ref, du2_ref, dh1_ref, dmix_ref, dwpre_ref, dwpost_ref):
        i = pl.program_id(0)
        dx, dwpre = _rms_bwd(h1_ref[...], wpre_ref[...], du2_ref[...])
        dh1 = dh2_ref[...] + dx
        dh1_ref[...] = dh1
        dmix, dwpost = _rms_bwd(mix_ref[...], wpost_ref[...], dh1)
        dmix_ref[...] = dmix.astype(dmix_ref.dtype)
        _accumulate(dwpre_ref, i == 0, dwpre)
        _accumulate(dwpost_ref, i == 0, dwpost)

    return pl.pallas_call(
        body, name="mid_bwd",
        out_shape=(jax.ShapeDtypeStruct((n, d), F32), jax.ShapeDtypeStruct((n, d), MXU_DTYPE),
                   jax.ShapeDtypeStruct((1, d), F32), jax.ShapeDtypeStruct((1, d), F32)),
        grid=(n // tr,),
        in_specs=[_rows(tr, d), _rows(tr, d), _vec(d), _vec(d), _rows(tr, d), _rows(tr, d)],
        out_specs=(_rows(tr, d), _rows(tr, d), _vec(d), _vec(d)),
        compiler_params=_params("arbitrary"),
    )(h1, mix, w_mix_post, w_ffn_pre, dh2, du2)


def _in_bwd(h0, w_pre, dh1, du1):
    n, d = h0.shape
    tr = _row_tile(n)

    def body(h0_ref, w_ref, dh1_ref, du1_ref, dh0_ref, dw_ref):
        dx, dw = _rms_bwd(h0_ref[...], w_ref[...], du1_ref[...])
        dh0_ref[...] = dh1_ref[...] + dx
        _accumulate(dw_ref, pl.program_id(0) == 0, dw)

    return pl.pallas_call(
        body, name="in_bwd",
        out_shape=(jax.ShapeDtypeStruct((n, d), F32), jax.ShapeDtypeStruct((1, d), F32)), grid=(n // tr,),
        in_specs=[_rows(tr, d), _vec(d), _rows(tr, d), _rows(tr, d)], out_specs=(_rows(tr, d), _vec(d)),
        compiler_params=_params("arbitrary"),
    )(h0, w_pre, dh1, du1)


def _lane_is(lo, hi):
    lane = lax.broadcasted_iota(jnp.int32, (1, LANES), 1)
    return jnp.logical_and(lane >= lo, lane < hi)


def _gates_fwd(proj, a_log_l, dt_bias_l, rows_per_seq, pad_rows):
    n = proj.shape[0]
    tr = _row_tile(rows_per_seq)
    tiles_per_seq = rows_per_seq // tr

    def body(p_ref, a_ref, dt_ref, o_ref):
        x = p_ref[...]
        row = lax.rem(pl.program_id(0), tiles_per_seq) * tr + lax.broadcasted_iota(jnp.int32, (tr, 1), 0)
        g = -jnp.exp(a_ref[...]) * _softplus(x + dt_ref[...])
        val = jnp.where(_lane_is(0, HEADS), _sigmoid(x), jnp.where(_lane_is(HEADS, 2 * HEADS), g, 0.0))
        o_ref[...] = jnp.where(row >= pad_rows, val, 0.0)

    return pl.pallas_call(
        body, name="gates_fwd", out_shape=jax.ShapeDtypeStruct((n, LANES), F32), grid=(n // tr,),
        in_specs=[pl.BlockSpec((tr, LANES), lambda i: (i, BA_COL)), _vec(LANES), _vec(LANES)],
        out_specs=_rows(tr, LANES), compiler_params=_params("parallel"),
    )(proj, a_log_l, dt_bias_l)


def _gates_bwd(proj, dbg, a_log_l, dt_bias_l, rows_per_seq, pad_rows):
    n = proj.shape[0]
    tr = _row_tile(rows_per_seq)
    tiles_per_seq = rows_per_seq // tr

    def body(p_ref, d_ref, a_ref, dt_ref, dx_ref, da_ref, ddt_ref):
        i = pl.program_id(0)
        x = p_ref[...]
        d = d_ref[...]
        row = lax.rem(i, tiles_per_seq) * tr + lax.broadcasted_iota(jnp.int32, (tr, 1), 0)
        live = row >= pad_rows
        beta = _sigmoid(x)
        ea = jnp.exp(a_ref[...])
        xa = x + dt_ref[...]
        g = -ea * _softplus(xa)
        is_g = _lane_is(HEADS, 2 * HEADS)
        d_alogit = jnp.where(jnp.logical_and(live, is_g), d * (-ea) * _sigmoid(xa), 0.0)
        d_blogit = jnp.where(jnp.logical_and(live, _lane_is(0, HEADS)), d * beta * (1.0 - beta), 0.0)
        dx_ref[:, :LANES] = (d_alogit + d_blogit).astype(dx_ref.dtype)
        dx_ref[:, LANES:] = jnp.zeros((tr, LANES), dx_ref.dtype)
        _accumulate(da_ref, i == 0, jnp.sum(jnp.where(jnp.logical_and(live, is_g), d * g, 0.0), axis=0, keepdims=True))
        _accumulate(ddt_ref, i == 0, jnp.sum(d_alogit, axis=0, keepdims=True))

    return pl.pallas_call(
        body, name="gates_bwd",
        out_shape=(jax.ShapeDtypeStruct((n, 2 * LANES), MXU_DTYPE), jax.ShapeDtypeStruct((1, LANES), F32),
                   jax.ShapeDtypeStruct((1, LANES), F32)),
        grid=(n // tr,),
        in_specs=[pl.BlockSpec((tr, LANES), lambda i: (i, BA_COL)), _rows(tr, LANES), _vec(LANES), _vec(LANES)],
        out_specs=(_rows(tr, 2 * LANES), _vec(LANES), _vec(LANES)),
        compiler_params=_params("arbitrary"),
    )(proj, dbg, a_log_l, dt_bias_l)


def _shift_down(x, k):
    return x if k == 0 else pltpu.roll(x, k, 0)


def _shift_up(x, k):
    return x if k == 0 else pltpu.roll(x, x.shape[0] - k, 0)


def _causal_conv(x, w, width):
    acc = w[width - 1:width, :] * x
    for i in range(width - 1):
        acc = acc + w[i:i + 1, :] * _shift_down(x, width - 1 - i)
    return acc


def _seq_head(rs, col0):
    return pl.BlockSpec((rs, LANES), lambda j, b: (b, col0 + j))


def _live_rows(rs, pad_rows):
    return lax.broadcasted_iota(jnp.int32, (rs, 1), 0) >= pad_rows


def _qkv_fwd(proj, conv_w, kind, rs, pad_rows):
    n = proj.shape[0]
    col0 = {"q": 0, "k": HEADS, "v": 2 * HEADS}[kind]

    def body(p_ref, w_ref, o_ref):
        c = _causal_conv(p_ref[...], w_ref[...], GDN_CONV)
        s = c * _sigmoid(c)
        if kind != "v":
            s = s * lax.rsqrt(jnp.sum(s * s, axis=-1, keepdims=True) + EPS)
        if kind == "q":
            s = s * (HEAD_DIM ** -0.5)
        o_ref[...] = jnp.where(_live_rows(rs, pad_rows), s, 0.0)

    return pl.pallas_call(
        body, name="qkv_fwd_" + kind, out_shape=jax.ShapeDtypeStruct((n, GDN_WIDTH), F32), grid=(HEADS, n // rs),
        in_specs=[_seq_head(rs, col0), pl.BlockSpec((GDN_CONV, LANES), lambda j, b: (0, col0 + j))],
        out_specs=_seq_head(rs, 0), compiler_params=_params("parallel", "parallel"),
    )(proj, conv_w)


def _qkv_bwd(dy, proj, conv_w, kind, rs, pad_rows):
    n = proj.shape[0]
    col0 = {"q": 0, "k": HEADS, "v": 2 * HEADS}[kind]

    def body(dy_ref, p_ref, w_ref, dp_ref, dw_ref):
        pre = p_ref[...]
        w = w_ref[...]
        c = _causal_conv(pre, w, GDN_CONV)
        sg = _sigmoid(c)
        s = c * sg
        ds = dy_ref[...]
        if kind == "q":
            ds = ds * (HEAD_DIM ** -0.5)
        if kind != "v":
            r = lax.rsqrt(jnp.sum(s * s, axis=-1, keepdims=True) + EPS)
            sh = s * r
            ds = r * (ds - sh * jnp.sum(ds * sh, axis=-1, keepdims=True))
        dc = jnp.where(_live_rows(rs, pad_rows), ds * _dsilu(c, sg), 0.0)
        dpre = w[GDN_CONV - 1:GDN_CONV, :] * dc
        for i in range(GDN_CONV - 1):
            dpre = dpre + w[i:i + 1, :] * _shift_up(dc, GDN_CONV - 1 - i)
        dp_ref[...] = dpre.astype(dp_ref.dtype)
        dw = jnp.concatenate(
            [jnp.sum(dc * _shift_down(pre, GDN_CONV - 1 - i), axis=0, keepdims=True) for i in range(GDN_CONV)], axis=0)
        _accumulate(dw_ref, pl.program_id(1) == 0, dw)

    return pl.pallas_call(
        body, name="qkv_bwd_" + kind,
        out_shape=(jax.ShapeDtypeStruct((n, GDN_WIDTH), MXU_DTYPE), jax.ShapeDtypeStruct((GDN_CONV, GDN_WIDTH), F32)),
        grid=(HEADS, n // rs),
        in_specs=[_seq_head(rs, 0), _seq_head(rs, col0), pl.BlockSpec((GDN_CONV, LANES), lambda j, b: (0, col0 + j))],
        out_specs=(_seq_head(rs, 0), pl.BlockSpec((GDN_CONV, LANES), lambda j, b: (0, j))),
        compiler_params=_params("parallel", "arbitrary"),
    )(dy, proj, conv_w)


SC_COL = 4 * HEADS


def _sc_fwd(proj, conv_w, rs):
    n = proj.shape[0]

    def body(x_ref, b_ref, c_ref, w_ref, y_ref):
        y_ref[...] = (b_ref[...] * _causal_conv(c_ref[...] * x_ref[...], w_ref[...], SC_CONV)).astype(y_ref.dtype)

    return pl.pallas_call(
        body, name="sc_fwd", out_shape=jax.ShapeDtypeStruct((n, SC_WIDTH), MXU_DTYPE), grid=(HEADS, n // rs),
        in_specs=[_seq_head(rs, SC_COL), _seq_head(rs, SC_COL + 4), _seq_head(rs, SC_COL + 8),
                  pl.BlockSpec((SC_CONV, LANES), lambda j, b: (0, j))],
        out_specs=_seq_head(rs, 0), compiler_params=_params("parallel", "parallel"),
    )(proj, proj, proj, conv_w)


def _sc_bwd(dcat, proj, conv_w, rs):
    n = proj.shape[0]

    def body(dy_ref, x_ref, b_ref, c_ref, w_ref, dx_ref, db_ref, dc_ref, dw_ref):
        w = w_ref[...]
        x = x_ref[...]
        cc = c_ref[...]
        u = cc * x
        dy = dy_ref[...]
        db_ref[...] = (dy * _causal_conv(u, w, SC_CONV)).astype(db_ref.dtype)
        dcv = dy * b_ref[...]
        du = w[SC_CONV - 1:SC_CONV, :] * dcv
        for i in range(SC_CONV - 1):
            du = du + w[i:i + 1, :] * _shift_up(dcv, SC_CONV - 1 - i)
        dx_ref[...] = (du * cc).astype(dx_ref.dtype)
        dc_ref[...] = (du * x).astype(dc_ref.dtype)
        dw = jnp.concatenate(
            [jnp.sum(dcv * _shift_down(u, SC_CONV - 1 - i), axis=0, keepdims=True) for i in range(SC_CONV)], axis=0)
        _accumulate(dw_ref, pl.program_id(1) == 0, dw)

    piece = jax.ShapeDtypeStruct((n, SC_WIDTH), MXU_DTYPE)
    return pl.pallas_call(
        body, name="sc_bwd", out_shape=(piece, piece, piece, jax.ShapeDtypeStruct((SC_CONV, SC_WIDTH), F32)),
        grid=(HEADS, n // rs),
        in_specs=[_seq_head(rs, HEADS), _seq_head(rs, SC_COL), _seq_head(rs, SC_COL + 4), _seq_head(rs, SC_COL + 8),
                  pl.BlockSpec((SC_CONV, LANES), lambda j, b: (0, j))],
        out_specs=(_seq_head(rs, 0), _seq_head(rs, 0), _seq_head(rs, 0),
                   pl.BlockSpec((SC_CONV, LANES), lambda j, b: (0, j))),
        compiler_params=_params("parallel", "arbitrary"),
    )(dcat, proj, proj, proj, conv_w)


Z_COL = 3 * HEADS


def _gate_fwd(o, proj, gdn_norm, rs):
    n = proj.shape[0]

    def body(o_ref, z_ref, w_ref, y_ref):
        z = z_ref[...]
        y_ref[...] = (_rms_apply(o_ref[...], w_ref[...]) * z * _sigmoid(z)).astype(y_ref.dtype)

    return pl.pallas_call(
        body, name="gate_fwd", out_shape=jax.ShapeDtypeStruct((n, GDN_WIDTH), MXU_DTYPE), grid=(HEADS, n // rs),
        in_specs=[_seq_head(rs, 0), _seq_head(rs, Z_COL), pl.BlockSpec((1, LANES), lambda j, b: (0, 0))],
        out_specs=_seq_head(rs, 0), compiler_params=_params("parallel", "parallel"),
    )(o, proj, gdn_norm)


def _gate_bwd(dcat, o, proj, gdn_norm, rs):
    n = proj.shape[0]

    def body(dy_ref, o_ref, z_ref, w_ref, do_ref, dz_ref, dw_ref):
        z = z_ref[...]
        w = w_ref[...]
        o = o_ref[...]
        dy = dy_ref[...]
        s = _sigmoid(z)
        dz_ref[...] = (dy * _rms_apply(o, w) * _dsilu(z, s)).astype(dz_ref.dtype)
        do, dw = _rms_bwd(o, w, dy * z * s)
        do_ref[...] = do
        _accumulate(dw_ref, jnp.logical_and(pl.program_id(0) == 0, pl.program_id(1) == 0), dw)

    return pl.pallas_call(
        body, name="gate_bwd",
        out_shape=(jax.ShapeDtypeStruct((n, GDN_WIDTH), F32), jax.ShapeDtypeStruct((n, GDN_WIDTH), MXU_DTYPE),
                   jax.ShapeDtypeStruct((1, LANES), F32)),
        grid=(HEADS, n // rs),
        in_specs=[_seq_head(rs, 0), _seq_head(rs, 0), _seq_head(rs, Z_COL), pl.BlockSpec((1, LANES), lambda j, b: (0, 0))],
        out_specs=(_seq_head(rs, 0), _seq_head(rs, 0), pl.BlockSpec((1, LANES), lambda j, b: (0, 0))),
        compiler_params=_params("arbitrary", "arbitrary"),
    )(dcat, o, proj, gdn_norm)


def _dot(a, b):
    return jnp.dot(a.astype(MXU_DTYPE), b.astype(MXU_DTYPE), preferred_element_type=F32)


def _dot_nt(a, b):
    return lax.dot_general(a.astype(MXU_DTYPE), b.astype(MXU_DTYPE), (((1,), (1,)), ((), ())),
                           preferred_element_type=F32)


def _dot_tn(a, b):
    return lax.dot_general(a.astype(MXU_DTYPE), b.astype(MXU_DTYPE), (((0,), (0,)), ((), ())),
                           preferred_element_type=F32)


def _dot_exact(a, b):
    return jnp.dot(a, b, precision=lax.Precision.HIGHEST, preferred_element_type=F32)


def _unit_lower_inverse(a, eye):
    inv = eye - a
    power = a
    span = 2
    while span < CHUNK:
        power = _dot_exact(power, power)
        inv = inv + _dot_exact(inv, power)
        span *= 2
    return inv


def _chunk_masks():
    ii = lax.broadcasted_iota(jnp.int32, (CHUNK, CHUNK), 0)
    jj = lax.broadcasted_iota(jnp.int32, (CHUNK, CHUNK), 1)
    return ii, jj


def _chunk_decay(g_row, g_col, ii, jj):
    incl = ii >= jj
    gc_col = jnp.sum(jnp.where(incl, g_row, 0.0), axis=1, keepdims=True)
    gc_row = jnp.sum(jnp.where(ii <= jj, g_col, 0.0), axis=0, keepdims=True)
    g_total = jnp.sum(g_row, axis=1, keepdims=True)
    decay = jnp.where(incl, jnp.exp(jnp.where(incl, gc_col - gc_row, 0.0)), 0.0)
    return gc_col, g_total, decay


def _bh_spec(block):
    return pl.BlockSpec((None, None) + block, lambda b, h: (b, h) + (0,) * len(block))


def _gdn_fwd(q, k, v, g_row, g_col, beta_col, rs):
    n = q.shape[0]
    batch = n // rs
    chunks = rs // CHUNK

    def body(q_ref, k_ref, v_ref, gr_ref, gc_ref, bc_ref, o_ref, s_ref, t_ref):
        ii, jj = _chunk_masks()
        incl = ii >= jj
        eye = (ii == jj).astype(F32)

        def chunk(c, state):
            r0 = pl.multiple_of(c * CHUNK, CHUNK)
            rows = pl.ds(r0, CHUNK)
            qc, kc, vc = q_ref[rows, :], k_ref[rows, :], v_ref[rows, :]
            beta = bc_ref[rows, :]
            gc_col, g_total, decay = _chunk_decay(gr_ref[pl.ds(c, 1), :], gc_ref[rows, :], ii, jj)
            kb = kc * beta
            a = jnp.where(ii > jj, _dot_nt(kb, kc) * decay, 0.0)
            t_inv = _unit_lower_inverse(a, eye)
            eg = jnp.exp(gc_col)
            u = _dot(t_inv, vc * beta)
            w = _dot(t_inv, kb * eg)
            qk = jnp.where(incl, _dot_nt(qc, kc) * decay, 0.0)
            s_ref[c] = state
            t_ref[c] = t_inv
            v_new = u - _dot(w, state)
            o_ref[rows, :] = _dot(qc * eg, state) + _dot(qk, v_new)
            k_dec = kc * jnp.exp(g_total - gc_col)
            return state * jnp.exp(g_total) + _dot_tn(k_dec, v_new)

        lax.fori_loop(0, chunks, chunk, jnp.zeros((HEAD_DIM, HEAD_DIM), F32))

    head = lambda col0: pl.BlockSpec((rs, LANES), lambda b, h: (b, col0 + h))
    return pl.pallas_call(
        body, name="gdn_fwd",
        out_shape=(jax.ShapeDtypeStruct((n, GDN_WIDTH), F32),
                   jax.ShapeDtypeStruct((batch, HEADS, chunks, HEAD_DIM, HEAD_DIM), F32),
                   jax.ShapeDtypeStruct((batch, HEADS, chunks, CHUNK, CHUNK), F32)),
        grid=(batch, HEADS),
        in_specs=[head(0), head(0), head(0), _bh_spec((chunks, CHUNK)), _bh_spec((rs, 1)), _bh_spec((rs, 1))],
        out_specs=(head(0), _bh_spec((chunks, HEAD_DIM, HEAD_DIM)), _bh_spec((chunks, CHUNK, CHUNK))),
        compiler_params=_params("parallel", "parallel"),
    )(q, k, v, g_row, g_col, beta_col)


def _gdn_bwd(do, q, k, v, g_row, g_col, beta_col, states, t_invs, rs):
    n = q.shape[0]
    batch = n // rs
    chunks = rs // CHUNK

    def body(do_ref, q_ref, k_ref, v_ref, gr_ref, gc_ref, bc_ref, s_ref, t_ref, dq_ref, dk_ref, dv_ref, dg_ref, db_ref):
        ii, jj = _chunk_masks()
        incl = ii >= jj
        eye = (ii == jj).astype(F32)

        def rowsum(x):
            return jnp.sum(x, axis=1, keepdims=True)

        def chunk(step, d_state):
            c = chunks - 1 - step
            r0 = pl.multiple_of(c * CHUNK, CHUNK)
            rows = pl.ds(r0, CHUNK)
            qc, kc, vc, doc = q_ref[rows, :], k_ref[rows, :], v_ref[rows, :], do_ref[rows, :]
            beta = bc_ref[rows, :]
            state = s_ref[c]
            t_inv = t_ref[c]
            gc_col, g_total, decay = _chunk_decay(gr_ref[pl.ds(c, 1), :], gc_ref[rows, :], ii, jj)
            kb = kc * beta
            vb = vc * beta
            eg = jnp.exp(gc_col)
            kbg = kb * eg
            a = jnp.where(ii > jj, _dot_nt(kb, kc) * decay, 0.0)
            qk = jnp.where(incl, _dot_nt(qc, kc) * decay, 0.0)
            w = _dot(t_inv, kbg)
            q_dec = qc * eg
            ek = jnp.exp(g_total - gc_col)
            k_dec = kc * ek
            g_last = jnp.exp(g_total)
            v_new = _dot(t_inv, vb) - _dot(w, state)
            dv_new = _dot_tn(qk, doc) + _dot(k_dec, d_state)
            dqk = jnp.where(incl, _dot_nt(doc, v_new), 0.0)
            dq_dec = _dot_nt(doc, state)
            dk_dec = _dot_nt(v_new, d_state)
            dg_last = jnp.sum(rowsum(state * d_state), axis=0, keepdims=True)
            d_state_new = _dot_tn(q_dec, doc) + g_last * d_state - _dot_tn(w, dv_new)
            dw = -_dot_nt(dv_new, state)
            dt = _dot_nt(dv_new, vb) + _dot_nt(dw, kbg)
            dvb = _dot_tn(t_inv, dv_new)
            dkbg = _dot_tn(t_inv, dw)
            da = -jnp.where(ii > jj, _dot_nt(_dot_tn(t_inv, dt), t_inv), 0.0)
            dm_a = da * decay
            dm_qk = dqk * decay
            e = da * a + dqk * qk
            dkb = _dot(dm_a, kc) + dkbg * eg
            dk_ref[rows, :] = _dot_tn(dm_a, kb) + _dot_tn(dm_qk, qc) + dk_dec * ek + dkb * beta
            dq_ref[rows, :] = _dot(dm_qk, kc) + dq_dec * eg
            dv_ref[rows, :] = dvb * beta
            db_ref[rows, :] = rowsum(dkb * kc + dvb * vc)
            col_e = rowsum(jnp.where(ii == jj, jnp.sum(e, axis=0, keepdims=True), 0.0))
            dgc = rowsum(e) - col_e + rowsum(dq_dec * q_dec - dk_dec * k_dec + dkbg * kbg)
            d_total = jnp.sum(rowsum(dk_dec * k_dec), axis=0, keepdims=True) + dg_last * g_last
            dg_ref[pl.ds(c, 1), :] = jnp.sum(jnp.where(incl, dgc, 0.0), axis=0, keepdims=True) + d_total
            return d_state_new

        lax.fori_loop(0, chunks, chunk, jnp.zeros((HEAD_DIM, HEAD_DIM), F32))

    head = lambda col0: pl.BlockSpec((rs, LANES), lambda b, h: (b, col0 + h))
    grad = jax.ShapeDtypeStruct((n, GDN_WIDTH), F32)
    return pl.pallas_call(
        body, name="gdn_bwd",
        out_shape=(grad, grad, grad, jax.ShapeDtypeStruct((batch, HEADS, chunks, CHUNK), F32),
                   jax.ShapeDtypeStruct((batch, HEADS, rs, 1), F32)),
        grid=(batch, HEADS),
        in_specs=[head(0), head(0), head(0), head(0), _bh_spec((chunks, CHUNK)), _bh_spec((rs, 1)), _bh_spec((rs, 1)),
                  _bh_spec((chunks, HEAD_DIM, HEAD_DIM)), _bh_spec((chunks, CHUNK, CHUNK))],
        out_specs=(head(0), head(0), head(0), _bh_spec((chunks, CHUNK)), _bh_spec((rs, 1))),
        compiler_params=_params("parallel", "parallel"),
    )(do, q, k, v, g_row, g_col, beta_col, states, t_invs)


def _lane_vec(vals, offset):
    k = vals.shape[1]
    return jnp.pad(vals, ((0, 0), (offset, LANES - offset - k)))


def _to_heads(cols, batch, rs):
    return jnp.transpose(cols.reshape(batch, rs, HEADS), (0, 2, 1))


def _local_step(x, target, meta, norms, w_in_p, conv_qkv, a_log, dt_bias, gdn_norm, conv_sc, w_out, w_gu, w_down):
    batch, seq, d = x.shape
    tokens = N_META + seq
    pad_rows = (-tokens) % CHUNK
    rs = tokens + pad_rows
    x_offset = pad_rows + N_META
    n = batch * rs
    chunks = rs // CHUNK
    w_mix_pre, w_mix_post, w_ffn_pre, w_ffn_post = norms

    head = jnp.concatenate([jnp.zeros((pad_rows, d), F32), meta], axis=0)
    h0 = jnp.concatenate([jnp.broadcast_to(head[None], (batch, x_offset, d)), x], axis=1).reshape(n, d)
    target_p = jnp.pad(target, ((0, 0), (x_offset, 0), (0, 0))).reshape(n, d)
    a_log_l = _lane_vec(a_log, HEADS)
    dt_bias_l = _lane_vec(dt_bias, HEADS)

    u1 = _rms_fwd(h0, w_mix_pre, "rms_mix_pre")
    proj = _mm(u1, w_in_p, "nn", F32, "mm_proj")
    q = _qkv_fwd(proj, conv_qkv, "q", rs, pad_rows)
    k = _qkv_fwd(proj, conv_qkv, "k", rs, pad_rows)
    v = _qkv_fwd(proj, conv_qkv, "v", rs, pad_rows)
    bg = _gates_fwd(proj, a_log_l, dt_bias_l, rs, pad_rows)
    beta_col = _to_heads(bg[:, :HEADS], batch, rs)[..., None]
    g_heads = _to_heads(bg[:, HEADS:2 * HEADS], batch, rs)
    g_col = g_heads[..., None]
    g_row = g_heads.reshape(batch, HEADS, chunks, CHUNK)
    o, states, t_invs = _gdn_fwd(q, k, v, g_row, g_col, beta_col, rs)
    o_gated = _gate_fwd(o, proj, gdn_norm, rs)
    y_sc = _sc_fwd(proj, conv_sc, rs)
    cat = jnp.concatenate([o_gated, y_sc], axis=1)
    mix = _mm(cat, w_out, "nn", F32, "mm_mix")
    h1, u2 = _mix_residual(h0, mix, w_mix_post, w_ffn_pre)
    gu = _mm(u2, w_gu, "nn", F32, "mm_gate_up")
    act = _swiglu_act(gu)
    ffn = _mm(act, w_down, "nn", F32, "mm_down")

    dh2, dffn, d_ffn_post, sq = _loss_head(h1, ffn, w_ffn_post, target_p, rs, x_offset)
    dact = _mm(dffn, w_down, "nt", F32, "mm_dact")
    d_w_down = _mm(act, dffn, "tn", F32, "mm_dw_down")
    dgu = _swiglu_bwd(gu, dact)
    d_w_gu = _mm(u2, dgu, "tn", F32, "mm_dw_gate_up")
    du2 = _mm(dgu, w_gu, "nt", F32, "mm_du2")
    dh1, dmix, d_ffn_pre, d_mix_post = _mid_bwd(h1, mix, w_mix_post, w_ffn_pre, dh2, du2)
    dcat = _mm(dmix, w_out, "nt", F32, "mm_dcat")
    d_w_out = _mm(cat, dmix, "tn", F32, "mm_dw_out")
    do, dz, d_gdn_norm = _gate_bwd(dcat, o, proj, gdn_norm, rs)
    dscx, dscb, dscc, d_conv_sc = _sc_bwd(dcat, proj, conv_sc, rs)
    dq, dk, dv, dg_row, dbeta_col = _gdn_bwd(do, q, k, v, g_row, g_col, beta_col, states, t_invs, rs)
    dpq, dwq = _qkv_bwd(dq, proj, conv_qkv, "q", rs, pad_rows)
    dpk, dwk = _qkv_bwd(dk, proj, conv_qkv, "k", rs, pad_rows)
    dpv, dwv = _qkv_bwd(dv, proj, conv_qkv, "v", rs, pad_rows)
    d_conv_qkv = jnp.concatenate([dwq, dwk, dwv], axis=1)
    dbeta = jnp.transpose(dbeta_col[..., 0], (0, 2, 1)).reshape(n, HEADS)
    dg = jnp.transpose(dg_row.reshape(batch, HEADS, rs), (0, 2, 1)).reshape(n, HEADS)
    dbg = jnp.pad(jnp.concatenate([dbeta, dg], axis=1), ((0, 0), (0, LANES - 2 * HEADS)))
    dba, d_a_log_l, d_dt_bias_l = _gates_bwd(proj, dbg, a_log_l, dt_bias_l, rs, pad_rows)
    dproj = jnp.concatenate([dpq, dpk, dpv, dz, dscx, dscb, dscc, dba], axis=1)
    d_w_in_p = _mm(u1, dproj, "tn", F32, "mm_dw_in")
    du1 = _mm(dproj, w_in_p, "nt", F32, "mm_du1")
    dh0, d_mix_pre = _in_bwd(h0, w_mix_pre, dh1, du1)

    dh0 = dh0.reshape(batch, rs, d)
    grads = dict(
        meta_tokens=jnp.sum(dh0[:, pad_rows:x_offset], axis=0),
        mix_pre_norm=d_mix_pre, mix_post_norm=d_mix_post, ffn_pre_norm=d_ffn_pre, ffn_post_norm=d_ffn_post,
        w_in_p=d_w_in_p, conv_qkv=d_conv_qkv,
        a_log=d_a_log_l[:, HEADS:2 * HEADS], dt_bias=d_dt_bias_l[:, HEADS:2 * HEADS],
        gdn_norm=d_gdn_norm, conv_sc=d_conv_sc, w_out=d_w_out, w_gu=d_w_gu, w_down=d_w_down,
    )
    return sq, dh0[:, x_offset:], grads


def _to_padded_in(w_in):
    lo, hi = 4 * GDN_WIDTH, 4 * GDN_WIDTH + 2 * HEADS
    pad = jnp.zeros((w_in.shape[0], IN_PAD - IN_WIDTH), w_in.dtype)
    return jnp.concatenate([w_in[:, :lo], w_in[:, hi:], w_in[:, lo:hi], pad], axis=1)


def _from_padded_in(w_in_p):
    lo, hi = 4 * GDN_WIDTH, IN_WIDTH - 2 * HEADS
    return jnp.concatenate([w_in_p[:, :lo], w_in_p[:, hi:IN_WIDTH], w_in_p[:, lo:hi]], axis=1)


HALF_ROWS = 13312
SMALL_ROWS = 48
REDUCE_ROWS = 224


def _shard_pieces(d):
    return (("w_in", (d, IN_WIDTH // N_CHIPS)), ("w_out", (D_MODEL // N_CHIPS, d)), ("w_gate", (d, D_FF // N_CHIPS)),
            ("w_up", (d, D_FF // N_CHIPS)), ("w_down", (D_FF // N_CHIPS, d)))


def _flatten(pieces, rows):
    flat = jnp.concatenate([p.reshape(-1) for p in pieces])
    return jnp.pad(flat, (0, rows * LANES - flat.shape[0])).reshape(rows, LANES)


def _unflatten(flat, shapes):
    flat = flat.reshape(-1)
    out, at = [], 0
    for shape in shapes:
        size = shape[0] * shape[1]
        out.append(flat[at:at + size].reshape(shape))
        at += size
    return out


def _hbm():
    return pl.BlockSpec(memory_space=pl.ANY)


def _place():
    x, y, c = lax.axis_index("x"), lax.axis_index("y"), lax.axis_index("c")
    chips = ((1 - x, y), (x, 1 - y), (1 - x, 1 - y))
    return x, y, c, chips


def _gather_weights(w_flat, s_flat):
    def body(w_ref, s_ref, wall_ref, sall_ref, send_sems, recv_sems, local_sems):
        x, y, c, chips = _place()
        mine = 2 * x + y
        sibling = (x, y, 1 - c)

        def big(k, src, chip, half, to):
            return pltpu.make_async_remote_copy(src_ref=src, dst_ref=wall_ref.at[chip, half], send_sem=send_sems.at[k],
                                                recv_sem=recv_sems.at[k], device_id=to, device_id_type=MESH)

        def small(k, to):
            return pltpu.make_async_remote_copy(src_ref=s_ref, dst_ref=sall_ref.at[mine], send_sem=send_sems.at[k],
                                                recv_sem=recv_sems.at[k], device_id=to, device_id_type=MESH)

        own_w = pltpu.make_async_copy(w_ref, wall_ref.at[mine], local_sems.at[0])
        own_s = pltpu.make_async_copy(s_ref, sall_ref.at[mine], local_sems.at[1])
        own_w.start()
        own_s.start()
        first = []
        for j, (cx, cy) in enumerate(chips):
            first.append(big(j, w_ref.at[c], mine, c, (cx, cy, c)))
            first.append(small(3 + j, (cx, cy, c)))
        for cp in first:
            cp.start()
        passed = []
        for j, (cx, cy) in enumerate(chips):
            theirs = 2 * cx + cy
            big(j, w_ref.at[c], theirs, c, sibling).wait_recv()
            passed.append(big(6 + j, wall_ref.at[theirs, c], theirs, c, sibling))
            passed[j].start()
        for j, (cx, cy) in enumerate(chips):
            big(6 + j, w_ref.at[c], 2 * cx + cy, 1 - c, sibling).wait_recv()
            small(3 + j, sibling).wait_recv()
        for cp in first + passed:
            cp.wait_send()
        own_w.wait()
        own_s.wait()

    return pl.pallas_call(
        body, name="gather_weights",
        out_shape=(jax.ShapeDtypeStruct((N_CHIPS,) + w_flat.shape, w_flat.dtype),
                   jax.ShapeDtypeStruct((N_CHIPS,) + s_flat.shape, s_flat.dtype)),
        in_specs=[_hbm(), _hbm()], out_specs=(_hbm(), _hbm()),
        scratch_shapes=[pltpu.SemaphoreType.DMA((9,)), pltpu.SemaphoreType.DMA((9,)), pltpu.SemaphoreType.DMA((2,))],
    )(w_flat, s_flat)


def _exchange_siblings(g_flat, small):
    def body(g_ref, s_ref, got_ref, sall_ref, send_sems, recv_sems, local_sem):
        x, y, c, _ = _place()
        me = 4 * x + 2 * y + c
        own = pltpu.make_async_copy(s_ref, sall_ref.at[me], local_sem)
        own.start()
        copies = []
        for k in range(7):
            dx, dy, dc = ((k + 1) >> 2) & 1, ((k + 1) >> 1) & 1, (k + 1) & 1
            peer = (1 - x if dx else x, 1 - y if dy else y, 1 - c if dc else c)
            copies.append(pltpu.make_async_remote_copy(
                src_ref=s_ref, dst_ref=sall_ref.at[me], send_sem=send_sems.at[k], recv_sem=recv_sems.at[k],
                device_id=peer, device_id_type=MESH))
        copies.append(pltpu.make_async_remote_copy(
            src_ref=g_ref.at[1 - c], dst_ref=got_ref, send_sem=send_sems.at[7], recv_sem=recv_sems.at[7],
            device_id=(x, y, 1 - c), device_id_type=MESH))
        for cp in copies:
            cp.start()
        for cp in copies:
            cp.wait_recv()
        for cp in copies:
            cp.wait_send()
        own.wait()

    return pl.pallas_call(
        body, name="exchange_siblings",
        out_shape=(jax.ShapeDtypeStruct(g_flat.shape[1:], F32), jax.ShapeDtypeStruct((8,) + small.shape, F32)),
        in_specs=[_hbm(), _hbm()], out_specs=(_hbm(), _hbm()),
        scratch_shapes=[pltpu.SemaphoreType.DMA((8,)), pltpu.SemaphoreType.DMA((8,)), pltpu.SemaphoreType.DMA],
    )(g_flat, small)


def _exchange_chips(part):
    def body(p_ref, got_ref, send_sems, recv_sems):
        x, y, c, chips = _place()
        copies = [pltpu.make_async_remote_copy(
            src_ref=p_ref.at[2 * cx + cy], dst_ref=got_ref.at[j], send_sem=send_sems.at[j], recv_sem=recv_sems.at[j],
            device_id=(cx, cy, c), device_id_type=MESH) for j, (cx, cy) in enumerate(chips)]
        for cp in copies:
            cp.start()
        for cp in copies:
            cp.wait_recv()
        for cp in copies:
            cp.wait_send()

    return pl.pallas_call(
        body, name="exchange_chips", out_shape=jax.ShapeDtypeStruct((3,) + part.shape[1:], part.dtype),
        in_specs=[_hbm()], out_specs=_hbm(),
        scratch_shapes=[pltpu.SemaphoreType.DMA((3,)), pltpu.SemaphoreType.DMA((3,))],
    )(part)


def _share_halves(total):
    def body(t_ref, full_ref, send_sem, recv_sem, local_sem):
        x, y, c, _ = _place()
        own = pltpu.make_async_copy(t_ref, full_ref.at[c], local_sem)
        own.start()
        cp = pltpu.make_async_remote_copy(src_ref=t_ref, dst_ref=full_ref.at[c], send_sem=send_sem, recv_sem=recv_sem,
                                          device_id=(x, y, 1 - c), device_id_type=MESH)
        cp.start()
        cp.wait_recv()
        cp.wait_send()
        own.wait()

    return pl.pallas_call(
        body, name="share_halves", out_shape=jax.ShapeDtypeStruct((2,) + total.shape, total.dtype),
        in_specs=[_hbm()], out_specs=_hbm(),
        scratch_shapes=[pltpu.SemaphoreType.DMA, pltpu.SemaphoreType.DMA, pltpu.SemaphoreType.DMA],
    )(total)


def _flat_tile(rows):
    return _pick(rows, (3328, 1024, 512, 256, 128, 64, 32, 16))


def _add_sibling(g_flat, got, core):
    _, chips, rows, _ = g_flat.shape
    tr = _flat_tile(rows)

    def body(core_ref, g_ref, r_ref, sum_ref, send_ref):
        s = g_ref[...] + r_ref[...]
        sum_ref[...] = s
        send_ref[...] = s.astype(send_ref.dtype)

    block = pl.BlockSpec((None, tr, LANES), lambda p, i, core_ref: (p, i, 0))
    return pl.pallas_call(
        body, name="add_sibling",
        out_shape=(jax.ShapeDtypeStruct((chips, rows, LANES), F32), jax.ShapeDtypeStruct((chips, rows, LANES), BF16)),
        grid_spec=pltpu.PrefetchScalarGridSpec(
            num_scalar_prefetch=1, grid=(chips, rows // tr),
            in_specs=[pl.BlockSpec((None, None, tr, LANES), lambda p, i, core_ref: (core_ref[0], p, i, 0)), block],
            out_specs=(block, block)),
        compiler_params=_params("parallel", "parallel"),
    )(core, g_flat, got)


def _add_chips(part, got, chip):
    _, rows, _ = part.shape
    tr = _flat_tile(rows)

    def body(chip_ref, p_ref, r_ref, o_ref):
        o_ref[...] = ((p_ref[...] + r_ref[0].astype(F32)) + r_ref[1].astype(F32)) + r_ref[2].astype(F32)

    return pl.pallas_call(
        body, name="add_chips", out_shape=jax.ShapeDtypeStruct((rows, LANES), F32),
        grid_spec=pltpu.PrefetchScalarGridSpec(
            num_scalar_prefetch=1, grid=(rows // tr,),
            in_specs=[pl.BlockSpec((None, tr, LANES), lambda i, chip_ref: (chip_ref[0], i, 0)),
                      pl.BlockSpec((3, tr, LANES), lambda i, chip_ref: (0, i, 0))],
            out_specs=pl.BlockSpec((tr, LANES), lambda i, chip_ref: (i, 0))),
        compiler_params=_params("parallel"),
    )(chip, part, got)


def _sum_devices(small_all):
    def body(s_ref, o_ref):
        acc = s_ref[0]
        for k in range(1, 8):
            acc = acc + s_ref[k]
        o_ref[...] = acc

    return pl.pallas_call(body, name="sum_devices", out_shape=jax.ShapeDtypeStruct(small_all.shape[1:], F32))(small_all)


def _adamw(w, g, m, v, name):
    rows, cols = w.shape
    tr = _pick(rows, (256, 352, 176, 128, 64, 32, 16, 8))

    def body(w_ref, g_ref, m_ref, v_ref, d_ref, nm_ref, nv_ref):
        g = g_ref[...]
        m = ADAM_B1 * m_ref[...] + (1.0 - ADAM_B1) * g
        v = ADAM_B2 * v_ref[...] + (1.0 - ADAM_B2) * (g * g)
        m_hat = m / (1.0 - ADAM_B1 ** ADAM_STEP)
        v_hat = v / (1.0 - ADAM_B2 ** ADAM_STEP)
        d_ref[...] = -ADAM_LR * (m_hat / (jnp.sqrt(v_hat) + ADAM_EPS) + ADAM_WD * w_ref[...])
        nm_ref[...] = m
        nv_ref[...] = v

    block = pl.BlockSpec((tr, cols), lambda i: (i, 0))
    shape = jax.ShapeDtypeStruct((rows, cols), F32)
    return pl.pallas_call(
        body, name="adamw_" + name, out_shape=(shape, shape, shape), grid=(rows // tr,),
        in_specs=[block] * 4, out_specs=(block,) * 3, compiler_params=_params("parallel"),
    )(w, g, m, v)


WEIGHTS = ("meta_tokens", "mix_pre_norm", "mix_post_norm", "ffn_pre_norm", "ffn_post_norm", "w_in", "conv_qkv", "a_log",
           "dt_bias", "gdn_norm", "conv_sc", "w_out", "w_gate", "w_up", "w_down")


def kernel(x, meta_tokens, mix_pre_norm, mix_post_norm, ffn_pre_norm, ffn_post_norm, w_in, conv_qkv, a_log, dt_bias, gdn_norm, conv_sc, w_out, w_gate, w_up, w_down, loss_target, m_meta_tokens, m_mix_pre_norm, m_mix_post_norm, m_ffn_pre_norm, m_ffn_post_norm, m_w_in, m_conv_qkv, m_a_log, m_dt_bias, m_gdn_norm, m_conv_sc, m_w_out, m_w_gate, m_w_up, m_w_down, v_meta_tokens, v_mix_pre_norm, v_mix_post_norm, v_ffn_pre_norm, v_ffn_post_norm, v_w_in, v_conv_qkv, v_a_log, v_dt_bias, v_gdn_norm, v_conv_sc, v_w_out, v_w_gate, v_w_up, v_w_down):
    d = x.shape[-1]
    two_d = lambda a: a.reshape(a.shape[-2:])
    weights = dict(zip(WEIGHTS, (meta_tokens, mix_pre_norm, mix_post_norm, ffn_pre_norm, ffn_post_norm, w_in, conv_qkv, a_log,
                                 dt_bias, gdn_norm, conv_sc, w_out, w_gate, w_up, w_down)))
    m_in = dict(zip(WEIGHTS, (m_meta_tokens, m_mix_pre_norm, m_mix_post_norm, m_ffn_pre_norm, m_ffn_post_norm, m_w_in, m_conv_qkv,
                              m_a_log, m_dt_bias, m_gdn_norm, m_conv_sc, m_w_out, m_w_gate, m_w_up, m_w_down)))
    v_in = dict(zip(WEIGHTS, (v_meta_tokens, v_mix_pre_norm, v_mix_post_norm, v_ffn_pre_norm, v_ffn_post_norm, v_w_in, v_conv_qkv,
                              v_a_log, v_dt_bias, v_gdn_norm, v_conv_sc, v_w_out, v_w_gate, v_w_up, v_w_down)))
    core = lax.axis_index("c")
    chip = 2 * lax.axis_index("x") + lax.axis_index("y")
    pieces = _shard_pieces(d)
    shapes = [shape for _, shape in pieces]
    small_shapes = [two_d(weights[n]).shape for n in ("conv_qkv", "conv_sc", "meta_tokens")]

    w_flat = _flatten([two_d(weights[n]).astype(MXU_DTYPE) for n, _ in pieces], 2 * HALF_ROWS).reshape(2, HALF_ROWS, LANES)
    s_flat = _flatten([two_d(weights[n]) for n in ("conv_qkv", "conv_sc", "meta_tokens")], SMALL_ROWS)
    w_all, s_all = _gather_weights(w_flat, s_flat)
    per_chip = [_unflatten(w_all[p], shapes) for p in range(N_CHIPS)]
    full = {n: jnp.concatenate([per_chip[p][i] for p in range(N_CHIPS)], axis=0 if n in ("w_out", "w_down") else 1)
            for i, (n, _) in enumerate(pieces)}
    small_chip = [_unflatten(s_all[p], small_shapes) for p in range(N_CHIPS)]
    conv_qkv_full, conv_sc_full, meta_full = (jnp.concatenate([small_chip[p][i] for p in range(N_CHIPS)], axis=1)
                                              for i in range(3))

    sq, grad_x, g = _local_step(
        x, loss_target, meta_full, (mix_pre_norm, mix_post_norm, ffn_pre_norm, ffn_post_norm), _to_padded_in(full["w_in"]),
        conv_qkv_full, a_log, dt_bias, gdn_norm, conv_sc_full, full["w_out"], jnp.concatenate([full["w_gate"], full["w_up"]], axis=1),
        full["w_down"])

    d_w_in = _from_padded_in(g["w_in_p"])
    by_chip = []
    for p in range(N_CHIPS):
        cols = lambda a, width: a[:, p * width:(p + 1) * width]
        rows = lambda a, height: a[p * height:(p + 1) * height]
        by_chip.append(_flatten([cols(d_w_in, shapes[0][1]), rows(g["w_out"], shapes[1][0]), cols(g["w_gu"][:, :D_FF], shapes[2][1]),
                                 cols(g["w_gu"][:, D_FF:], shapes[3][1]), rows(g["w_down"], shapes[4][0])], 2 * HALF_ROWS))
    g_flat = jnp.stack(by_chip).reshape(N_CHIPS, 2, HALF_ROWS, LANES).transpose(1, 0, 2, 3)
    scalars = jnp.concatenate([g["a_log"], g["dt_bias"], sq], axis=1)
    small = _flatten([g["mix_pre_norm"], g["mix_post_norm"], g["ffn_pre_norm"], g["ffn_post_norm"],
                      jnp.pad(scalars, ((0, 0), (0, LANES - scalars.shape[1]))), g["gdn_norm"], g["conv_qkv"], g["conv_sc"],
                      g["meta_tokens"]], REDUCE_ROWS)
    got_sibling, small_all = _exchange_siblings(g_flat, small)
    part, part_send = _add_sibling(g_flat, got_sibling, core.reshape(1).astype(jnp.int32))
    got_chips = _exchange_chips(part_send)
    total = _add_chips(part, got_chips, chip.reshape(1).astype(jnp.int32))
    g_shard = _share_halves(total)
    reduced = _sum_devices(small_all)

    grads = dict(zip([n for n, _ in pieces], _unflatten(g_shard, shapes)))
    r = reduced.reshape(-1)
    at = 0
    for n in ("mix_pre_norm", "mix_post_norm", "ffn_pre_norm", "ffn_post_norm"):
        grads[n] = r[at:at + d].reshape(1, d)
        at += d
    grads["a_log"] = r[at:at + HEADS].reshape(1, HEADS)
    grads["dt_bias"] = r[at + HEADS:at + 2 * HEADS].reshape(1, HEADS)
    loss = (0.5 / d) * r[at + 2 * HEADS]
    at += LANES
    grads["gdn_norm"] = r[at:at + HEAD_DIM].reshape(1, HEAD_DIM)
    at += HEAD_DIM
    for n, shape in (("conv_qkv", (GDN_CONV, 3 * GDN_WIDTH)), ("conv_sc", (SC_CONV, SC_WIDTH)), ("meta_tokens", (N_META, d))):
        full_grad = r[at:at + shape[0] * shape[1]].reshape(shape)
        at += shape[0] * shape[1]
        width = shape[1] // N_CHIPS
        grads[n] = lax.dynamic_slice_in_dim(full_grad, chip * width, width, axis=1)

    out_g, out_d, out_m, out_v = [], [], [], []
    for n in WEIGHTS:
        shape = weights[n].shape
        delta, new_m, new_v = _adamw(two_d(weights[n]), grads[n], two_d(m_in[n]), two_d(v_in[n]), n)
        out_g.append(grads[n].reshape(shape))
        out_d.append(delta.reshape(shape))
        out_m.append(new_m.reshape(shape))
        out_v.append(new_v.reshape(shape))
    return (loss, grad_x, *out_g, *out_d, *out_m, *out_v)
```

```python
import functools

import jax
import jax.numpy as jnp
from jax import lax
from jax.experimental import pallas as pl
from jax.experimental.pallas import tpu as pltpu

F32 = jnp.float32
BF16 = jnp.bfloat16
MXU_DTYPE = jnp.bfloat16
MESH = pl.DeviceIdType.MESH

D_MODEL = 1024
N_META = 16
HEADS = 4
HEAD_DIM = 128
GDN_WIDTH = HEADS * HEAD_DIM
GDN_CONV = 4
CHUNK = 64
SC_WIDTH = D_MODEL - GDN_WIDTH
SC_CONV = 3
D_FF = 2816
IN_WIDTH = 4 * GDN_WIDTH + 2 * HEADS + 3 * SC_WIDTH
IN_PAD = 3840
BA_COL = (4 * GDN_WIDTH + 3 * SC_WIDTH) // 128
EPS = 1e-6
LANES = 128
N_CHIPS = 4
VMEM_LIMIT = 48 * 2 ** 20

ADAM_LR = 0.001
ADAM_B1 = 0.9
ADAM_B2 = 0.999
ADAM_EPS = 1e-08
ADAM_WD = 0.01
ADAM_STEP = 10


def _pick(n, candidates):
    for c in candidates:
        if n % c == 0:
            return c
    return n


def _row_tile(n):
    return _pick(n, (352, 256, 176, 128, 64, 32, 16, 8))


def _params(*sem):
    return pltpu.CompilerParams(dimension_semantics=sem, vmem_limit_bytes=VMEM_LIMIT)


def _sigmoid(x):
    return 1.0 / (1.0 + jnp.exp(-x))


def _softplus(x):
    return jnp.maximum(x, 0.0) + jnp.log(1.0 + jnp.exp(-jnp.abs(x)))


def _dsilu(x, s):
    return s * (1.0 + x * (1.0 - s))


def _mm(a, b, mode, out_dtype, name):
    if mode == "tn":
        k_dim, m_dim = a.shape
    else:
        m_dim, k_dim = a.shape
    n_dim = b.shape[0] if mode == "nt" else b.shape[1]
    tm = _pick(m_dim, (1408, 1024, 512, 256, 128) if mode == "tn" else (1056, 1024, 704, 512, 256, 128))
    tn = _pick(n_dim, (1408, 1280, 1024, 768, 512, 256, 128))
    tk = _pick(k_dim, (1408, 1280, 1056, 1024, 512, 256, 128))
    nk = k_dim // tk
    if mode == "nn":
        a_spec = pl.BlockSpec((tm, tk), lambda i, j, k: (i, k))
        b_spec = pl.BlockSpec((tk, tn), lambda i, j, k: (k, j))
        dims = (((1,), (0,)), ((), ()))
    elif mode == "nt":
        a_spec = pl.BlockSpec((tm, tk), lambda i, j, k: (i, k))
        b_spec = pl.BlockSpec((tn, tk), lambda i, j, k: (j, k))
        dims = (((1,), (1,)), ((), ()))
    else:
        a_spec = pl.BlockSpec((tk, tm), lambda i, j, k: (k, i))
        b_spec = pl.BlockSpec((tk, tn), lambda i, j, k: (k, j))
        dims = (((0,), (0,)), ((), ()))

    def body(a_ref, b_ref, o_ref, acc_ref):
        k = pl.program_id(2)
        p = lax.dot_general(a_ref[...], b_ref[...], dims, preferred_element_type=F32)

        @pl.when(k == 0)
        def _():
            acc_ref[...] = p

        @pl.when(k > 0)
        def _():
            acc_ref[...] += p

        @pl.when(k == nk - 1)
        def _():
            o_ref[...] = acc_ref[...].astype(out_dtype)

    return pl.pallas_call(
        body, name=name,
        out_shape=jax.ShapeDtypeStruct((m_dim, n_dim), out_dtype),
        grid=(m_dim // tm, n_dim // tn, nk),
        in_specs=[a_spec, b_spec],
        out_specs=pl.BlockSpec((tm, tn), lambda i, j, k: (i, j)),
        scratch_shapes=[pltpu.VMEM((tm, tn), F32)],
        compiler_params=_params("parallel", "parallel", "arbitrary"),
    )(a, b)


def _rms_apply(x, w):
    r = lax.rsqrt(jnp.mean(x * x, axis=-1, keepdims=True) + EPS)
    return x * r * w


def _rms_bwd(x, w, dy):
    r = lax.rsqrt(jnp.mean(x * x, axis=-1, keepdims=True) + EPS)
    xh = x * r
    dyw = dy * w
    dx = r * (dyw - xh * jnp.mean(dyw * xh, axis=-1, keepdims=True))
    return dx, jnp.sum(dy * xh, axis=0, keepdims=True)


def _accumulate(ref, first, value):
    @pl.when(first)
    def _():
        ref[...] = value

    @pl.when(jnp.logical_not(first))
    def _():
        ref[...] += value


def _rows(tr, width):
    return pl.BlockSpec((tr, width), lambda i: (i, 0))


def _vec(width):
    return pl.BlockSpec((1, width), lambda i: (0, 0))


def _rms_fwd(h, w, name):
    n, d = h.shape
    tr = _row_tile(n)

    def body(h_ref, w_ref, u_ref):
        u_ref[...] = _rms_apply(h_ref[...], w_ref[...]).astype(u_ref.dtype)

    return pl.pallas_call(
        body, name=name, out_shape=jax.ShapeDtypeStruct((n, d), MXU_DTYPE), grid=(n // tr,),
        in_specs=[_rows(tr, d), _vec(d)], out_specs=_rows(tr, d), compiler_params=_params("parallel"),
    )(h, w)


def _mix_residual(h0, mix, w_post, w_pre):
    n, d = h0.shape
    tr = _row_tile(n)

    def body(h0_ref, mix_ref, wpost_ref, wpre_ref, h1_ref, u2_ref):
        h1 = h0_ref[...] + _rms_apply(mix_ref[...], wpost_ref[...])
        h1_ref[...] = h1
        u2_ref[...] = _rms_apply(h1, wpre_ref[...]).astype(u2_ref.dtype)

    return pl.pallas_call(
        body, name="mix_residual",
        out_shape=(jax.ShapeDtypeStruct((n, d), F32), jax.ShapeDtypeStruct((n, d), MXU_DTYPE)), grid=(n // tr,),
        in_specs=[_rows(tr, d), _rows(tr, d), _vec(d), _vec(d)], out_specs=(_rows(tr, d), _rows(tr, d)),
        compiler_params=_params("parallel"),
    )(h0, mix, w_post, w_pre)


def _swiglu_act(gu):
    n = gu.shape[0]
    tr = _pick(n, (176, 128, 64, 32, 16, 8))

    def body(gu_ref, act_ref):
        gate = gu_ref[:, :D_FF]
        act_ref[...] = (gate * _sigmoid(gate) * gu_ref[:, D_FF:]).astype(act_ref.dtype)

    return pl.pallas_call(
        body, name="swiglu_act", out_shape=jax.ShapeDtypeStruct((n, D_FF), MXU_DTYPE), grid=(n // tr,),
        in_specs=[_rows(tr, 2 * D_FF)], out_specs=_rows(tr, D_FF), compiler_params=_params("parallel"),
    )(gu)


def _swiglu_bwd(gu, dact):
    n = gu.shape[0]
    tr = _pick(n, (176, 128, 64, 32, 16, 8))

    def body(gu_ref, dact_ref, dgu_ref):
        gate = gu_ref[:, :D_FF]
        up = gu_ref[:, D_FF:]
        s = _sigmoid(gate)
        da = dact_ref[...]
        dgu_ref[:, :D_FF] = (da * up * _dsilu(gate, s)).astype(dgu_ref.dtype)
        dgu_ref[:, D_FF:] = (da * gate * s).astype(dgu_ref.dtype)

    return pl.pallas_call(
        body, name="swiglu_bwd", out_shape=jax.ShapeDtypeStruct((n, 2 * D_FF), MXU_DTYPE), grid=(n // tr,),
        in_specs=[_rows(tr, 2 * D_FF), _rows(tr, D_FF)], out_specs=_rows(tr, 2 * D_FF),
        compiler_params=_params("parallel"),
    )(gu, dact)


def _loss_head(h1, ffn, w_post, target, rows_per_seq, x_offset):
    n, d = h1.shape
    tr = _row_tile(rows_per_seq)
    tiles_per_seq = rows_per_seq // tr

    def body(h1_ref, ffn_ref, w_ref, t_ref, dh2_ref, dffn_ref, dw_ref, sq_ref):
        i = pl.program_id(0)
        w = w_ref[...]
        f = ffn_ref[...]
        r = lax.rsqrt(jnp.mean(f * f, axis=-1, keepdims=True) + EPS)
        fh = f * r
        row = lax.rem(i, tiles_per_seq) * tr + lax.broadcasted_iota(jnp.int32, (tr, 1), 0)
        err = jnp.where(row >= x_offset, h1_ref[...] + fh * w - t_ref[...], 0.0)
        dh2 = err * (1.0 / d)
        dh2_ref[...] = dh2
        dyw = dh2 * w
        dffn_ref[...] = (r * (dyw - fh * jnp.mean(dyw * fh, axis=-1, keepdims=True))).astype(dffn_ref.dtype)
        _accumulate(dw_ref, i == 0, jnp.sum(dh2 * fh, axis=0, keepdims=True))
        _accumulate(sq_ref, i == 0, jnp.sum(jnp.sum(err * err, axis=1, keepdims=True), axis=0, keepdims=True))

    return pl.pallas_call(
        body, name="loss_head",
        out_shape=(jax.ShapeDtypeStruct((n, d), F32), jax.ShapeDtypeStruct((n, d), MXU_DTYPE),
                   jax.ShapeDtypeStruct((1, d), F32), jax.ShapeDtypeStruct((1, 1), F32)),
        grid=(n // tr,),
        in_specs=[_rows(tr, d), _rows(tr, d), _vec(d), _rows(tr, d)],
        out_specs=(_rows(tr, d), _rows(tr, d), _vec(d), _vec(1)),
        compiler_params=_params("arbitrary"),
    )(h1, ffn, w_post, target)


def _mid_bwd(h1, mix, w_mix_post, w_ffn_pre, dh2, du2):
    n, d = h1.shape
    tr = _row_tile(n)

    def body(h1_ref, mix_ref, wpost_ref, wpre_ref, dh2_ref, du2_ref, dh1_ref, dmix_ref, dwpre_ref, dwpost_ref):
        i = pl.program_id(0)
        dx, dwpre = _rms_bwd(h1_ref[...], wpre_ref[...], du2_ref[...])
        dh1 = dh2_ref[...] + dx
        dh1_ref[...] = dh1
        dmix, dwpost = _rms_bwd(mix_ref[...], wpost_ref[...], dh1)
        dmix_ref[...] = dmix.astype(dmix_ref.dtype)
        _accumulate(dwpre_ref, i == 0, dwpre)
        _accumulate(dwpost_ref, i == 0, dwpost)

    return pl.pallas_call(
        body, name="mid_bwd",
        out_shape=(jax.ShapeDtypeStruct((n, d), F32), jax.ShapeDtypeStruct((n, d), MXU_DTYPE),
                   jax.ShapeDtypeStruct((1, d), F32), jax.ShapeDtypeStruct((1, d), F32)),
        grid=(n // tr,),
        in_specs=[_rows(tr, d), _rows(tr, d), _vec(d), _vec(d), _rows(tr, d), _rows(tr, d)],
        out_specs=(_rows(tr, d), _rows(tr, d), _vec(d), _vec(d)),
        compiler_params=_params("arbitrary"),
    )(h1, mix, w_mix_post, w_ffn_pre, dh2, du2)


def _in_bwd(h0, w_pre, dh1, du1):
    n, d = h0.shape
    tr = _row_tile(n)

    def body(h0_ref, w_ref, dh1_ref, du1_ref, dh0_ref, dw_ref):
        dx, dw = _rms_bwd(h0_ref[...], w_ref[...], du1_ref[...])
        dh0_ref[...] = dh1_ref[...] + dx
        _accumulate(dw_ref, pl.program_id(0) == 0, dw)

    return pl.pallas_call(
        body, name="in_bwd",
        out_shape=(jax.ShapeDtypeStruct((n, d), F32), jax.ShapeDtypeStruct((1, d), F32)), grid=(n // tr,),
        in_specs=[_rows(tr, d), _vec(d), _rows(tr, d), _rows(tr, d)], out_specs=(_rows(tr, d), _vec(d)),
        compiler_params=_params("arbitrary"),
    )(h0, w_pre, dh1, du1)


def _lane_is(lo, hi):
    lane = lax.broadcasted_iota(jnp.int32, (1, LANES), 1)
    return jnp.logical_and(lane >= lo, lane < hi)


def _gates_fwd(proj, a_log_l, dt_bias_l, rows_per_seq, pad_rows):
    n = proj.shape[0]
    tr = _row_tile(rows_per_seq)
    tiles_per_seq = rows_per_seq // tr

    def body(p_ref, a_ref, dt_ref, o_ref):
        x = p_ref[...]
        row = lax.rem(pl.program_id(0), tiles_per_seq) * tr + lax.broadcasted_iota(jnp.int32, (tr, 1), 0)
        g = -jnp.exp(a_ref[...]) * _softplus(x + dt_ref[...])
        val = jnp.where(_lane_is(0, HEADS), _sigmoid(x), jnp.where(_lane_is(HEADS, 2 * HEADS), g, 0.0))
        o_ref[...] = jnp.where(row >= pad_rows, val, 0.0)

    return pl.pallas_call(
        body, name="gates_fwd", out_shape=jax.ShapeDtypeStruct((n, LANES), F32), grid=(n // tr,),
        in_specs=[pl.BlockSpec((tr, LANES), lambda i: (i, BA_COL)), _vec(LANES), _vec(LANES)],
        out_specs=_rows(tr, LANES), compiler_params=_params("parallel"),
    )(proj, a_log_l, dt_bias_l)


def _gates_bwd(proj, dbg, a_log_l, dt_bias_l, rows_per_seq, pad_rows):
    n = proj.shape[0]
    tr = _row_tile(rows_per_seq)
    tiles_per_seq = rows_per_seq // tr

    def body(p_ref, d_ref, a_ref, dt_ref, dx_ref, da_ref, ddt_ref):
        i = pl.program_id(0)
        x = p_ref[...]
        d = d_ref[...]
        row = lax.rem(i, tiles_per_seq) * tr + lax.broadcasted_iota(jnp.int32, (tr, 1), 0)
        live = row >= pad_rows
        beta = _sigmoid(x)
        ea = jnp.exp(a_ref[...])
        xa = x + dt_ref[...]
        g = -ea * _softplus(xa)
        is_g = _lane_is(HEADS, 2 * HEADS)
        d_alogit = jnp.where(jnp.logical_and(live, is_g), d * (-ea) * _sigmoid(xa), 0.0)
        d_blogit = jnp.where(jnp.logical_and(live, _lane_is(0, HEADS)), d * beta * (1.0 - beta), 0.0)
        dx_ref[:, :LANES] = (d_alogit + d_blogit).astype(dx_ref.dtype)
        dx_ref[:, LANES:] = jnp.zeros((tr, LANES), dx_ref.dtype)
        _accumulate(da_ref, i == 0, jnp.sum(jnp.where(jnp.logical_and(live, is_g), d * g, 0.0), axis=0, keepdims=True))
        _accumulate(ddt_ref, i == 0, jnp.sum(d_alogit, axis=0, keepdims=True))

    return pl.pallas_call(
        body, name="gates_bwd",
        out_shape=(jax.ShapeDtypeStruct((n, 2 * LANES), MXU_DTYPE), jax.ShapeDtypeStruct((1, LANES), F32),
                   jax.ShapeDtypeStruct((1, LANES), F32)),
        grid=(n // tr,),
        in_specs=[pl.BlockSpec((tr, LANES), lambda i: (i, BA_COL)), _rows(tr, LANES), _vec(LANES), _vec(LANES)],
        out_specs=(_rows(tr, 2 * LANES), _vec(LANES), _vec(LANES)),
        compiler_params=_params("arbitrary"),
    )(proj, dbg, a_log_l, dt_bias_l)


def _shift_down(x, k):
    return x if k == 0 else pltpu.roll(x, k, 0)


def _shift_up(x, k):
    return x if k == 0 else pltpu.roll(x, x.shape[0] - k, 0)


def _causal_conv(x, w, width):
    acc = w[width - 1:width, :] * x
    for i in range(width - 1):
        acc = acc + w[i:i + 1, :] * _shift_down(x, width - 1 - i)
    return acc


def _seq_head(rs, col0):
    return pl.BlockSpec((rs, LANES), lambda j, b: (b, col0 + j))


def _live_rows(rs, pad_rows):
    return lax.broadcasted_iota(jnp.int32, (rs, 1), 0) >= pad_rows


def _qkv_fwd(proj, conv_w, kind, rs, pad_rows):
    n = proj.shape[0]
    col0 = {"q": 0, "k": HEADS, "v": 2 * HEADS}[kind]

    def body(p_ref, w_ref, o_ref):
        c = _causal_conv(p_ref[...], w_ref[...], GDN_CONV)
        s = c * _sigmoid(c)
        if kind != "v":
            s = s * lax.rsqrt(jnp.sum(s * s, axis=-1, keepdims=True) + EPS)
        if kind == "q":
            s = s * (HEAD_DIM ** -0.5)
        o_ref[...] = jnp.where(_live_rows(rs, pad_rows), s, 0.0)

    return pl.pallas_call(
        body, name="qkv_fwd_" + kind, out_shape=jax.ShapeDtypeStruct((n, GDN_WIDTH), F32), grid=(HEADS, n // rs),
        in_specs=[_seq_head(rs, col0), pl.BlockSpec((GDN_CONV, LANES), lambda j, b: (0, col0 + j))],
        out_specs=_seq_head(rs, 0), compiler_params=_params("parallel", "parallel"),
    )(proj, conv_w)


def _qkv_bwd(dy, proj, conv_w, kind, rs, pad_rows):
    n = proj.shape[0]
    col0 = {"q": 0, "k": HEADS, "v": 2 * HEADS}[kind]

    def body(dy_ref, p_ref, w_ref, dp_ref, dw_ref):
        pre = p_ref[...]
        w = w_ref[...]
        c = _causal_conv(pre, w, GDN_CONV)
        sg = _sigmoid(c)
        s = c * sg
        ds = dy_ref[...]
        if kind == "q":
            ds = ds * (HEAD_DIM ** -0.5)
        if kind != "v":
            r = lax.rsqrt(jnp.sum(s * s, axis=-1, keepdims=True) + EPS)
            sh = s * r
            ds = r * (ds - sh * jnp.sum(ds * sh, axis=-1, keepdims=True))
        dc = jnp.where(_live_rows(rs, pad_rows), ds * _dsilu(c, sg), 0.0)
        dpre = w[GDN_CONV - 1:GDN_CONV, :] * dc
        for i in range(GDN_CONV - 1):
            dpre = dpre + w[i:i + 1, :] * _shift_up(dc, GDN_CONV - 1 - i)
        dp_ref[...] = dpre.astype(dp_ref.dtype)
        dw = jnp.concatenate(
            [jnp.sum(dc * _shift_down(pre, GDN_CONV - 1 - i), axis=0, keepdims=True) for i in range(GDN_CONV)], axis=0)
        _accumulate(dw_ref, pl.program_id(1) == 0, dw)

    return pl.pallas_call(
        body, name="qkv_bwd_" + kind,
        out_shape=(jax.ShapeDtypeStruct((n, GDN_WIDTH), MXU_DTYPE), jax.ShapeDtypeStruct((GDN_CONV, GDN_WIDTH), F32)),
        grid=(HEADS, n // rs),
        in_specs=[_seq_head(rs, 0), _seq_head(rs, col0), pl.BlockSpec((GDN_CONV, LANES), lambda j, b: (0, col0 + j))],
        out_specs=(_seq_head(rs, 0), pl.BlockSpec((GDN_CONV, LANES), lambda j, b: (0, j))),
        compiler_params=_params("parallel", "arbitrary"),
    )(dy, proj, conv_w)


SC_COL = 4 * HEADS


def _sc_fwd(proj, conv_w, rs):
    n = proj.shape[0]

    def body(x_ref, b_ref, c_ref, w_ref, y_ref):
        y_ref[...] = (b_ref[...] * _causal_conv(c_ref[...] * x_ref[...], w_ref[...], SC_CONV)).astype(y_ref.dtype)

    return pl.pallas_call(
        body, name="sc_fwd", out_shape=jax.ShapeDtypeStruct((n, SC_WIDTH), MXU_DTYPE), grid=(HEADS, n // rs),
        in_specs=[_seq_head(rs, SC_COL), _seq_head(rs, SC_COL + 4), _seq_head(rs, SC_COL + 8),
                  pl.BlockSpec((SC_CONV, LANES), lambda j, b: (0, j))],
        out_specs=_seq_head(rs, 0), compiler_params=_params("parallel", "parallel"),
    )(proj, proj, proj, conv_w)


def _sc_bwd(dcat, proj, conv_w, rs):
    n = proj.shape[0]

    def body(dy_ref, x_ref, b_ref, c_ref, w_ref, dx_ref, db_ref, dc_ref, dw_ref):
        w = w_ref[...]
        x = x_ref[...]
        cc = c_ref[...]
        u = cc * x
        dy = dy_ref[...]
        db_ref[...] = (dy * _causal_conv(u, w, SC_CONV)).astype(db_ref.dtype)
        dcv = dy * b_ref[...]
        du = w[SC_CONV - 1:SC_CONV, :] * dcv
        for i in range(SC_CONV - 1):
            du = du + w[i:i + 1, :] * _shift_up(dcv, SC_CONV - 1 - i)
        dx_ref[...] = (du * cc).astype(dx_ref.dtype)
        dc_ref[...] = (du * x).astype(dc_ref.dtype)
        dw = jnp.concatenate(
            [jnp.sum(dcv * _shift_down(u, SC_CONV - 1 - i), axis=0, keepdims=True) for i in range(SC_CONV)], axis=0)
        _accumulate(dw_ref, pl.program_id(1) == 0, dw)

    piece = jax.ShapeDtypeStruct((n, SC_WIDTH), MXU_DTYPE)
    return pl.pallas_call(
        body, name="sc_bwd", out_shape=(piece, piece, piece, jax.ShapeDtypeStruct((SC_CONV, SC_WIDTH), F32)),
        grid=(HEADS, n // rs),
        in_specs=[_seq_head(rs, HEADS), _seq_head(rs, SC_COL), _seq_head(rs, SC_COL + 4), _seq_head(rs, SC_COL + 8),
                  pl.BlockSpec((SC_CONV, LANES), lambda j, b: (0, j))],
        out_specs=(_seq_head(rs, 0), _seq_head(rs, 0), _seq_head(rs, 0),
                   pl.BlockSpec((SC_CONV, LANES), lambda j, b: (0, j))),
        compiler_params=_params("parallel", "arbitrary"),
    )(dcat, proj, proj, proj, conv_w)


Z_COL = 3 * HEADS


def _gate_fwd(o, proj, gdn_norm, rs):
    n = proj.shape[0]

    def body(o_ref, z_ref, w_ref, y_ref):
        z = z_ref[...]
        y_ref[...] = (_rms_apply(o_ref[...], w_ref[...]) * z * _sigmoid(z)).astype(y_ref.dtype)

    return pl.pallas_call(
        body, name="gate_fwd", out_shape=jax.ShapeDtypeStruct((n, GDN_WIDTH), MXU_DTYPE), grid=(HEADS, n // rs),
        in_specs=[_seq_head(rs, 0), _seq_head(rs, Z_COL), pl.BlockSpec((1, LANES), lambda j, b: (0, 0))],
        out_specs=_seq_head(rs, 0), compiler_params=_params("parallel", "parallel"),
    )(o, proj, gdn_norm)


def _gate_bwd(dcat, o, proj, gdn_norm, rs):
    n = proj.shape[0]

    def body(dy_ref, o_ref, z_ref, w_ref, do_ref, dz_ref, dw_ref):
        z = z_ref[...]
        w = w_ref[...]
        o = o_ref[...]
        dy = dy_ref[...]
        s = _sigmoid(z)
        dz_ref[...] = (dy * _rms_apply(o, w) * _dsilu(z, s)).astype(dz_ref.dtype)
        do, dw = _rms_bwd(o, w, dy * z * s)
        do_ref[...] = do
        _accumulate(dw_ref, jnp.logical_and(pl.program_id(0) == 0, pl.program_id(1) == 0), dw)

    return pl.pallas_call(
        body, name="gate_bwd",
        out_shape=(jax.ShapeDtypeStruct((n, GDN_WIDTH), F32), jax.ShapeDtypeStruct((n, GDN_WIDTH), MXU_DTYPE),
                   jax.ShapeDtypeStruct((1, LANES), F32)),
        grid=(HEADS, n // rs),
        in_specs=[_seq_head(rs, 0), _seq_head(rs, 0), _seq_head(rs, Z_COL), pl.BlockSpec((1, LANES), lambda j, b: (0, 0))],
        out_specs=(_seq_head(rs, 0), _seq_head(rs, 0), pl.BlockSpec((1, LANES), lambda j, b: (0, 0))),
        compiler_params=_params("arbitrary", "arbitrary"),
    )(dcat, o, proj, gdn_norm)


def _dot(a, b):
    return jnp.dot(a.astype(MXU_DTYPE), b.astype(MXU_DTYPE), preferred_element_type=F32)


def _dot_nt(a, b):
    return lax.dot_general(a.astype(MXU_DTYPE), b.astype(MXU_DTYPE), (((1,), (1,)), ((), ())),
                           preferred_element_type=F32)


def _dot_tn(a, b):
    return lax.dot_general(a.astype(MXU_DTYPE), b.astype(MXU_DTYPE), (((0,), (0,)), ((), ())),
                           preferred_element_type=F32)


def _dot_exact(a, b):
    return jnp.dot(a, b, precision=lax.Precision.HIGHEST, preferred_element_type=F32)


def _unit_lower_inverse(a, eye):
    inv = eye - a
    power = a
    span = 2
    while span < CHUNK:
        power = _dot_exact(power, power)
        inv = inv + _dot_exact(inv, power)
        span *= 2
    return inv


def _chunk_masks():
    ii = lax.broadcasted_iota(jnp.int32, (CHUNK, CHUNK), 0)
    jj = lax.broadcasted_iota(jnp.int32, (CHUNK, CHUNK), 1)
    return ii, jj


def _chunk_decay(g_col, ii, jj):
    incl = ii >= jj
    g_row = jnp.sum(jnp.where(ii == jj, g_col, 0.0), axis=0, keepdims=True)
    gc_col = jnp.sum(jnp.where(incl, g_row, 0.0), axis=1, keepdims=True)
    gc_row = jnp.sum(jnp.where(ii <= jj, g_col, 0.0), axis=0, keepdims=True)
    g_total = jnp.sum(g_row, axis=1, keepdims=True)
    decay = jnp.where(incl, jnp.exp(jnp.where(incl, gc_col - gc_row, 0.0)), 0.0)
    return gc_col, g_total, decay


def _gdn_segments(rs):
    chunks = rs // CHUNK
    seg_chunks = _pick(chunks, (11, 8, 4, 2))
    return chunks, seg_chunks, chunks // seg_chunks


def _gdn_fwd(q, k, v, bg, rs):
    n = q.shape[0]
    batch = n // rs
    chunks, seg_chunks, segs = _gdn_segments(rs)
    seg_rows = seg_chunks * CHUNK

    def body(q_ref, k_ref, v_ref, bg_ref, o_ref, s_ref, t_ref, state_ref):
        @pl.when(pl.program_id(1) == 0)
        def _():
            state_ref[...] = jnp.zeros_like(state_ref)

        ii, jj = _chunk_masks()
        incl = ii >= jj
        eye = (ii == jj).astype(F32)

        def chunk(c, carry):
            rows = pl.ds(pl.multiple_of(c * CHUNK, CHUNK), CHUNK)
            bgc = bg_ref[rows, :]
            for h in range(HEADS):
                lanes = slice(h * HEAD_DIM, (h + 1) * HEAD_DIM)
                qc, kc, vc = q_ref[rows, lanes], k_ref[rows, lanes], v_ref[rows, lanes]
                beta = bgc[:, h:h + 1]
                state = state_ref[h]
                gc_col, g_total, decay = _chunk_decay(bgc[:, HEADS + h:HEADS + h + 1], ii, jj)
                kb = kc * beta
                a = jnp.where(ii > jj, _dot_nt(kb, kc) * decay, 0.0)
                t_inv = _unit_lower_inverse(a, eye)
                eg = jnp.exp(gc_col)
                u = _dot(t_inv, vc * beta)
                w = _dot(t_inv, kb * eg)
                qk = jnp.where(incl, _dot_nt(qc, kc) * decay, 0.0)
                s_ref[h, c] = state
                t_ref[h, c] = t_inv
                v_new = u - _dot(w, state)
                o_ref[rows, lanes] = _dot(qc * eg, state) + _dot(qk, v_new)
                k_dec = kc * jnp.exp(g_total - gc_col)
                state_ref[h] = state * jnp.exp(g_total) + _dot_tn(k_dec, v_new)
            return carry

        lax.fori_loop(0, seg_chunks, chunk, 0)

    rows_spec = lambda width: pl.BlockSpec((seg_rows, width), lambda b, s: (b * segs + s, 0))
    per_chunk = lambda r, c: pl.BlockSpec((None, HEADS, seg_chunks, r, c), lambda b, s: (b, 0, s, 0, 0))
    return pl.pallas_call(
        body, name="gdn_fwd",
        out_shape=(jax.ShapeDtypeStruct((n, GDN_WIDTH), F32),
                   jax.ShapeDtypeStruct((batch, HEADS, chunks, HEAD_DIM, HEAD_DIM), F32),
                   jax.ShapeDtypeStruct((batch, HEADS, chunks, CHUNK, CHUNK), F32)),
        grid=(batch, segs),
        in_specs=[rows_spec(GDN_WIDTH), rows_spec(GDN_WIDTH), rows_spec(GDN_WIDTH), rows_spec(LANES)],
        out_specs=(rows_spec(GDN_WIDTH), per_chunk(HEAD_DIM, HEAD_DIM), per_chunk(CHUNK, CHUNK)),
        scratch_shapes=[pltpu.VMEM((HEADS, HEAD_DIM, HEAD_DIM), F32)],
        compiler_params=_params("parallel", "arbitrary"),
    )(q, k, v, bg)


def _gdn_bwd(do, q, k, v, bg, states, t_invs, rs):
    n = q.shape[0]
    batch = n // rs
    chunks, seg_chunks, segs = _gdn_segments(rs)
    seg_rows = seg_chunks * CHUNK

    def body(do_ref, q_ref, k_ref, v_ref, bg_ref, s_ref, t_ref, dq_ref, dk_ref, dv_ref, dbg_ref, dstate_ref):
        @pl.when(pl.program_id(1) == 0)
        def _():
            dstate_ref[...] = jnp.zeros_like(dstate_ref)

        ii, jj = _chunk_masks()
        incl = ii >= jj
        lane = lax.broadcasted_iota(jnp.int32, (1, LANES), 1)

        def rowsum(x):
            return jnp.sum(x, axis=1, keepdims=True)

        def chunk(step, carry):
            c = seg_chunks - 1 - step
            rows = pl.ds(pl.multiple_of(c * CHUNK, CHUNK), CHUNK)
            bgc = bg_ref[rows, :]
            dbg = jnp.zeros((CHUNK, LANES), F32)
            for h in range(HEADS):
                lanes = slice(h * HEAD_DIM, (h + 1) * HEAD_DIM)
                qc, kc, vc, doc = q_ref[rows, lanes], k_ref[rows, lanes], v_ref[rows, lanes], do_ref[rows, lanes]
                beta = bgc[:, h:h + 1]
                state = s_ref[h, c]
                t_inv = t_ref[h, c]
                d_state = dstate_ref[h]
                gc_col, g_total, decay = _chunk_decay(bgc[:, HEADS + h:HEADS + h + 1], ii, jj)
                kb = kc * beta
                vb = vc * beta
                eg = jnp.exp(gc_col)
                kbg = kb * eg
                a = jnp.where(ii > jj, _dot_nt(kb, kc) * decay, 0.0)
                qk = jnp.where(incl, _dot_nt(qc, kc) * decay, 0.0)
                w = _dot(t_inv, kbg)
                q_dec = qc * eg
                ek = jnp.exp(g_total - gc_col)
                k_dec = kc * ek
                g_last = jnp.exp(g_total)
                v_new = _dot(t_inv, vb) - _dot(w, state)
                dv_new = _dot_tn(qk, doc) + _dot(k_dec, d_state)
                dqk = jnp.where(incl, _dot_nt(doc, v_new), 0.0)
                dq_dec = _dot_nt(doc, state)
                dk_dec = _dot_nt(v_new, d_state)
                dg_last = jnp.sum(rowsum(state * d_state), axis=0, keepdims=True)
                dstate_ref[h] = _dot_tn(q_dec, doc) + g_last * d_state - _dot_tn(w, dv_new)
                dw = -_dot_nt(dv_new, state)
                dt = _dot_nt(dv_new, vb) + _dot_nt(dw, kbg)
                dvb = _dot_tn(t_inv, dv_new)
                dkbg = _dot_tn(t_inv, dw)
                da = -jnp.where(ii > jj, _dot_nt(_dot_tn(t_inv, dt), t_inv), 0.0)
                dm_a = da * decay
                dm_qk = dqk * decay
                e = da * a + dqk * qk
                dkb = _dot(dm_a, kc) + dkbg * eg
                dk_ref[rows, lanes] = _dot_tn(dm_a, kb) + _dot_tn(dm_qk, qc) + dk_dec * ek + dkb * beta
                dq_ref[rows, lanes] = _dot(dm_qk, kc) + dq_dec * eg
                dv_ref[rows, lanes] = dvb * beta
                dbeta = rowsum(dkb * kc + dvb * vc)
                col_e = rowsum(jnp.where(ii == jj, jnp.sum(e, axis=0, keepdims=True), 0.0))
                dgc = rowsum(e) - col_e + rowsum(dq_dec * q_dec - dk_dec * k_dec + dkbg * kbg)
                dgc_row = jnp.sum(jnp.where(ii == jj, dgc, 0.0), axis=0, keepdims=True)
                d_total = jnp.sum(rowsum(dk_dec * k_dec), axis=0, keepdims=True) + dg_last * g_last
                dg = rowsum(jnp.where(jj >= ii, dgc_row, 0.0)) + d_total
                dbg = dbg + jnp.where(lane == h, dbeta, 0.0) + jnp.where(lane == HEADS + h, dg, 0.0)
            dbg_ref[rows, :] = dbg
            return carry

        lax.fori_loop(0, seg_chunks, chunk, 0)

    rows_spec = lambda width: pl.BlockSpec((seg_rows, width), lambda b, s: (b * segs + segs - 1 - s, 0))
    per_chunk = lambda r, c: pl.BlockSpec((None, HEADS, seg_chunks, r, c), lambda b, s: (b, 0, segs - 1 - s, 0, 0))
    grad = jax.ShapeDtypeStruct((n, GDN_WIDTH), F32)
    wide = rows_spec(GDN_WIDTH)
    return pl.pallas_call(
        body, name="gdn_bwd",
        out_shape=(grad, grad, grad, jax.ShapeDtypeStruct((n, LANES), F32)),
        grid=(batch, segs),
        in_specs=[wide, wide, wide, wide, rows_spec(LANES), per_chunk(HEAD_DIM, HEAD_DIM), per_chunk(CHUNK, CHUNK)],
        out_specs=(wide, wide, wide, rows_spec(LANES)),
        scratch_shapes=[pltpu.VMEM((HEADS, HEAD_DIM, HEAD_DIM), F32)],
        compiler_params=_params("parallel", "arbitrary"),
    )(do, q, k, v, bg, states, t_invs)


def _lane_vec(vals, offset):
    k = vals.shape[1]
    return jnp.pad(vals, ((0, 0), (offset, LANES - offset - k)))


def _local_step(x, target, meta, norms, w_in_p, conv_qkv, a_log, dt_bias, gdn_norm, conv_sc, w_out, w_gu, w_down):
    batch, seq, d = x.shape
    tokens = N_META + seq
    pad_rows = (-tokens) % CHUNK
    rs = tokens + pad_rows
    x_offset = pad_rows + N_META
    n = batch * rs
    w_mix_pre, w_mix_post, w_ffn_pre, w_ffn_post = norms

    head = jnp.concatenate([jnp.zeros((pad_rows, d), F32), meta], axis=0)
    h0 = jnp.concatenate([jnp.broadcast_to(head[None], (batch, x_offset, d)), x], axis=1).reshape(n, d)
    target_p = jnp.pad(target, ((0, 0), (x_offset, 0), (0, 0))).reshape(n, d)
    a_log_l = _lane_vec(a_log, HEADS)
    dt_bias_l = _lane_vec(dt_bias, HEADS)

    u1 = _rms_fwd(h0, w_mix_pre, "rms_mix_pre")
    proj = _mm(u1, w_in_p, "nn", F32, "mm_proj")
    q = _qkv_fwd(proj, conv_qkv, "q", rs, pad_rows)
    k = _qkv_fwd(proj, conv_qkv, "k", rs, pad_rows)
    v = _qkv_fwd(proj, conv_qkv, "v", rs, pad_rows)
    bg = _gates_fwd(proj, a_log_l, dt_bias_l, rs, pad_rows)
    o, states, t_invs = _gdn_fwd(q, k, v, bg, rs)
    o_gated = _gate_fwd(o, proj, gdn_norm, rs)
    y_sc = _sc_fwd(proj, conv_sc, rs)
    cat = jnp.concatenate([o_gated, y_sc], axis=1)
    mix = _mm(cat, w_out, "nn", F32, "mm_mix")
    h1, u2 = _mix_residual(h0, mix, w_mix_post, w_ffn_pre)
    gu = _mm(u2, w_gu, "nn", F32, "mm_gate_up")
    act = _swiglu_act(gu)
    ffn = _mm(act, w_down, "nn", F32, "mm_down")

    dh2, dffn, d_ffn_post, sq = _loss_head(h1, ffn, w_ffn_post, target_p, rs, x_offset)
    dact = _mm(dffn, w_down, "nt", F32, "mm_dact")
    d_w_down = _mm(act, dffn, "tn", F32, "mm_dw_down")
    dgu = _swiglu_bwd(gu, dact)
    d_w_gu = _mm(u2, dgu, "tn", F32, "mm_dw_gate_up")
    du2 = _mm(dgu, w_gu, "nt", F32, "mm_du2")
    dh1, dmix, d_ffn_pre, d_mix_post = _mid_bwd(h1, mix, w_mix_post, w_ffn_pre, dh2, du2)
    dcat = _mm(dmix, w_out, "nt", F32, "mm_dcat")
    d_w_out = _mm(cat, dmix, "tn", F32, "mm_dw_out")
    do, dz, d_gdn_norm = _gate_bwd(dcat, o, proj, gdn_norm, rs)
    dscx, dscb, dscc, d_conv_sc = _sc_bwd(dcat, proj, conv_sc, rs)
    dq, dk, dv, dbg = _gdn_bwd(do, q, k, v, bg, states, t_invs, rs)
    dpq, dwq = _qkv_bwd(dq, proj, conv_qkv, "q", rs, pad_rows)
    dpk, dwk = _qkv_bwd(dk, proj, conv_qkv, "k", rs, pad_rows)
    dpv, dwv = _qkv_bwd(dv, proj, conv_qkv, "v", rs, pad_rows)
    d_conv_qkv = jnp.concatenate([dwq, dwk, dwv], axis=1)
    dba, d_a_log_l, d_dt_bias_l = _gates_bwd(proj, dbg, a_log_l, dt_bias_l, rs, pad_rows)
    dproj = jnp.concatenate([dpq, dpk, dpv, dz, dscx, dscb, dscc, dba], axis=1)
    d_w_in_p = _mm(u1, dproj, "tn", F32, "mm_dw_in")
    du1 = _mm(dproj, w_in_p, "nt", F32, "mm_du1")
    dh0, d_mix_pre = _in_bwd(h0, w_mix_pre, dh1, du1)

    dh0 = dh0.reshape(batch, rs, d)
    grads = dict(
        meta_tokens=jnp.sum(dh0[:, pad_rows:x_offset], axis=0),
        mix_pre_norm=d_mix_pre, mix_post_norm=d_mix_post, ffn_pre_norm=d_ffn_pre, ffn_post_norm=d_ffn_post,
        w_in_p=d_w_in_p, conv_qkv=d_conv_qkv,
        a_log=d_a_log_l[:, HEADS:2 * HEADS], dt_bias=d_dt_bias_l[:, HEADS:2 * HEADS],
        gdn_norm=d_gdn_norm, conv_sc=d_conv_sc, w_out=d_w_out, w_gu=d_w_gu, w_down=d_w_down,
    )
    return sq, dh0[:, x_offset:], grads


def _to_padded_in(w_in):
    lo, hi = 4 * GDN_WIDTH, 4 * GDN_WIDTH + 2 * HEADS
    pad = jnp.zeros((w_in.shape[0], IN_PAD - IN_WIDTH), w_in.dtype)
    return jnp.concatenate([w_in[:, :lo], w_in[:, hi:], w_in[:, lo:hi], pad], axis=1)


def _from_padded_in(w_in_p):
    lo, hi = 4 * GDN_WIDTH, IN_WIDTH - 2 * HEADS
    return jnp.concatenate([w_in_p[:, :lo], w_in_p[:, hi:IN_WIDTH], w_in_p[:, lo:hi]], axis=1)


HALF_ROWS = 13312
SMALL_ROWS = 48
REDUCE_ROWS = 224


def _shard_pieces(d):
    return (("w_in", (d, IN_WIDTH // N_CHIPS)), ("w_out", (D_MODEL // N_CHIPS, d)), ("w_gate", (d, D_FF // N_CHIPS)),
            ("w_up", (d, D_FF // N_CHIPS)), ("w_down", (D_FF // N_CHIPS, d)))


def _flatten(pieces, rows):
    flat = jnp.concatenate([p.reshape(-1) for p in pieces])
    return jnp.pad(flat, (0, rows * LANES - flat.shape[0])).reshape(rows, LANES)


def _unflatten(flat, shapes):
    flat = flat.reshape(-1)
    out, at = [], 0
    for shape in shapes:
        size = shape[0] * shape[1]
        out.append(flat[at:at + size].reshape(shape))
        at += size
    return out


def _hbm():
    return pl.BlockSpec(memory_space=pl.ANY)


def _place():
    x, y, c = lax.axis_index("x"), lax.axis_index("y"), lax.axis_index("c")
    chips = ((1 - x, y), (x, 1 - y), (1 - x, 1 - y))
    return x, y, c, chips


def _gather_weights(w_flat, s_flat):
    def body(w_ref, s_ref, wall_ref, sall_ref, send_sems, recv_sems, local_sem):
        x, y, c, chips = _place()
        mine = 2 * x + y
        sibling = (x, y, 1 - c)

        def big(k, src, chip, half, to):
            return pltpu.make_async_remote_copy(src_ref=src, dst_ref=wall_ref.at[chip, half], send_sem=send_sems.at[k],
                                                recv_sem=recv_sems.at[k], device_id=to, device_id_type=MESH)

        def small(k, to):
            return pltpu.make_async_remote_copy(src_ref=s_ref, dst_ref=sall_ref.at[mine], send_sem=send_sems.at[k],
                                                recv_sem=recv_sems.at[k], device_id=to, device_id_type=MESH)

        own_w = pltpu.make_async_remote_copy(src_ref=w_ref, dst_ref=wall_ref.at[mine], send_sem=send_sems.at[9],
                                             recv_sem=recv_sems.at[9], device_id=sibling, device_id_type=MESH)
        own_s = pltpu.make_async_copy(s_ref, sall_ref.at[mine], local_sem)
        own_w.start()
        own_s.start()
        first = []
        for j, (cx, cy) in enumerate(chips):
            first.append(big(j, w_ref.at[c], mine, c, (cx, cy, c)))
            first.append(small(3 + j, (cx, cy, c)))
        for cp in first:
            cp.start()
        passed = []
        for j, (cx, cy) in enumerate(chips):
            theirs = 2 * cx + cy
            big(j, w_ref.at[c], theirs, c, sibling).wait_recv()
            passed.append(big(6 + j, wall_ref.at[theirs, c], theirs, c, sibling))
            passed[j].start()
        for j, (cx, cy) in enumerate(chips):
            big(6 + j, w_ref.at[c], 2 * cx + cy, 1 - c, sibling).wait_recv()
            small(3 + j, sibling).wait_recv()
        own_w.wait_recv()
        for cp in first + passed + [own_w]:
            cp.wait_send()
        own_s.wait()

    return pl.pallas_call(
        body, name="gather_weights",
        out_shape=(jax.ShapeDtypeStruct((N_CHIPS,) + w_flat.shape, w_flat.dtype),
                   jax.ShapeDtypeStruct((N_CHIPS,) + s_flat.shape, s_flat.dtype)),
        in_specs=[_hbm(), _hbm()], out_specs=(_hbm(), _hbm()),
        scratch_shapes=[pltpu.SemaphoreType.DMA((10,)), pltpu.SemaphoreType.DMA((10,)), pltpu.SemaphoreType.DMA],
    )(w_flat, s_flat)


def _exchange_siblings(g_flat, small):
    def body(g_ref, s_ref, got_ref, sall_ref, send_sems, recv_sems, local_sem):
        x, y, c, _ = _place()
        me = 4 * x + 2 * y + c
        own = pltpu.make_async_copy(s_ref, sall_ref.at[me], local_sem)
        own.start()
        copies = []
        for k in range(7):
            dx, dy, dc = ((k + 1) >> 2) & 1, ((k + 1) >> 1) & 1, (k + 1) & 1
            peer = (1 - x if dx else x, 1 - y if dy else y, 1 - c if dc else c)
            copies.append(pltpu.make_async_remote_copy(
                src_ref=s_ref, dst_ref=sall_ref.at[me], send_sem=send_sems.at[k], recv_sem=recv_sems.at[k],
                device_id=peer, device_id_type=MESH))
        copies.append(pltpu.make_async_remote_copy(
            src_ref=g_ref.at[1 - c], dst_ref=got_ref, send_sem=send_sems.at[7], recv_sem=recv_sems.at[7],
            device_id=(x, y, 1 - c), device_id_type=MESH))
        for cp in copies:
            cp.start()
        for cp in copies:
            cp.wait_recv()
        for cp in copies:
            cp.wait_send()
        own.wait()

    return pl.pallas_call(
        body, name="exchange_siblings",
        out_shape=(jax.ShapeDtypeStruct(g_flat.shape[1:], F32), jax.ShapeDtypeStruct((8,) + small.shape, F32)),
        in_specs=[_hbm(), _hbm()], out_specs=(_hbm(), _hbm()),
        scratch_shapes=[pltpu.SemaphoreType.DMA((8,)), pltpu.SemaphoreType.DMA((8,)), pltpu.SemaphoreType.DMA],
    )(g_flat, small)


def _exchange_chips(part):
    def body(p_ref, got_ref, send_sems, recv_sems):
        x, y, c, chips = _place()
        copies = [pltpu.make_async_remote_copy(
            src_ref=p_ref.at[2 * cx + cy], dst_ref=got_ref.at[j], send_sem=send_sems.at[j], recv_sem=recv_sems.at[j],
            device_id=(cx, cy, c), device_id_type=MESH) for j, (cx, cy) in enumerate(chips)]
        for cp in copies:
            cp.start()
        for cp in copies:
            cp.wait_recv()
        for cp in copies:
            cp.wait_send()

    return pl.pallas_call(
        body, name="exchange_chips", out_shape=jax.ShapeDtypeStruct((3,) + part.shape[1:], part.dtype),
        in_specs=[_hbm()], out_specs=_hbm(),
        scratch_shapes=[pltpu.SemaphoreType.DMA((3,)), pltpu.SemaphoreType.DMA((3,))],
    )(part)


def _share_halves(halves):
    def body(h_ref, full_ref, send_sem, recv_sem):
        x, y, c, _ = _place()
        cp = pltpu.make_async_remote_copy(src_ref=h_ref.at[c], dst_ref=full_ref.at[c], send_sem=send_sem, recv_sem=recv_sem,
                                          device_id=(x, y, 1 - c), device_id_type=MESH)
        cp.start()
        cp.wait_recv()
        cp.wait_send()

    return pl.pallas_call(
        body, name="share_halves", out_shape=jax.ShapeDtypeStruct(halves.shape, halves.dtype),
        in_specs=[_hbm()], out_specs=_hbm(), input_output_aliases={0: 0},
        scratch_shapes=[pltpu.SemaphoreType.DMA, pltpu.SemaphoreType.DMA],
    )(halves)


def _flat_tile(rows):
    return _pick(rows, (3328, 1024, 512, 256, 128, 64, 32, 16))


def _add_sibling(g_flat, got, core):
    _, chips, rows, _ = g_flat.shape
    tr = _flat_tile(rows)

    def body(core_ref, g_ref, r_ref, sum_ref, send_ref):
        s = g_ref[...] + r_ref[...]
        sum_ref[...] = s
        send_ref[...] = s.astype(send_ref.dtype)

    block = pl.BlockSpec((None, tr, LANES), lambda p, i, core_ref: (p, i, 0))
    return pl.pallas_call(
        body, name="add_sibling",
        out_shape=(jax.ShapeDtypeStruct((chips, rows, LANES), F32), jax.ShapeDtypeStruct((chips, rows, LANES), BF16)),
        grid_spec=pltpu.PrefetchScalarGridSpec(
            num_scalar_prefetch=1, grid=(chips, rows // tr),
            in_specs=[pl.BlockSpec((None, None, tr, LANES), lambda p, i, core_ref: (core_ref[0], p, i, 0)), block],
            out_specs=(block, block)),
        compiler_params=_params("parallel", "parallel"),
    )(core, g_flat, got)


def _add_chips(part, got, chip_core):
    _, rows, _ = part.shape
    tr = _flat_tile(rows)

    def body(place_ref, p_ref, r_ref, o_ref):
        o_ref[...] = ((p_ref[...] + r_ref[0].astype(F32)) + r_ref[1].astype(F32)) + r_ref[2].astype(F32)

    return pl.pallas_call(
        body, name="add_chips", out_shape=jax.ShapeDtypeStruct((2, rows, LANES), F32),
        grid_spec=pltpu.PrefetchScalarGridSpec(
            num_scalar_prefetch=1, grid=(rows // tr,),
            in_specs=[pl.BlockSpec((None, tr, LANES), lambda i, place_ref: (place_ref[0], i, 0)),
                      pl.BlockSpec((3, tr, LANES), lambda i, place_ref: (0, i, 0))],
            out_specs=pl.BlockSpec((None, tr, LANES), lambda i, place_ref: (place_ref[1], i, 0))),
        compiler_params=_params("parallel"),
    )(chip_core, part, got)


def _sum_devices(small_all):
    def body(s_ref, o_ref):
        acc = s_ref[0]
        for k in range(1, 8):
            acc = acc + s_ref[k]
        o_ref[...] = acc

    return pl.pallas_call(body, name="sum_devices", out_shape=jax.ShapeDtypeStruct(small_all.shape[1:], F32))(small_all)


def _adamw(w, g, m, v, name):
    rows, cols = w.shape
    tr = _pick(rows, (256, 352, 176, 128, 64, 32, 16, 8))

    def body(w_ref, g_ref, m_ref, v_ref, d_ref, nm_ref, nv_ref):
        g = g_ref[...]
        m = ADAM_B1 * m_ref[...] + (1.0 - ADAM_B1) * g
        v = ADAM_B2 * v_ref[...] + (1.0 - ADAM_B2) * (g * g)
        m_hat = m / (1.0 - ADAM_B1 ** ADAM_STEP)
        v_hat = v / (1.0 - ADAM_B2 ** ADAM_STEP)
        d_ref[...] = -ADAM_LR * (m_hat / (jnp.sqrt(v_hat) + ADAM_EPS) + ADAM_WD * w_ref[...])
        nm_ref[...] = m
        nv_ref[...] = v

    block = pl.BlockSpec((tr, cols), lambda i: (i, 0))
    shape = jax.ShapeDtypeStruct((rows, cols), F32)
    return pl.pallas_call(
        body, name="adamw_" + name, out_shape=(shape, shape, shape), grid=(rows // tr,),
        in_specs=[block] * 4, out_specs=(block,) * 3, compiler_params=_params("parallel"),
    )(w, g, m, v)


WEIGHTS = ("meta_tokens", "mix_pre_norm", "mix_post_norm", "ffn_pre_norm", "ffn_post_norm", "w_in", "conv_qkv", "a_log",
           "dt_bias", "gdn_norm", "conv_sc", "w_out", "w_gate", "w_up", "w_down")


def kernel(x, meta_tokens, mix_pre_norm, mix_post_norm, ffn_pre_norm, ffn_post_norm, w_in, conv_qkv, a_log, dt_bias, gdn_norm, conv_sc, w_out, w_gate, w_up, w_down, loss_target, m_meta_tokens, m_mix_pre_norm, m_mix_post_norm, m_ffn_pre_norm, m_ffn_post_norm, m_w_in, m_conv_qkv, m_a_log, m_dt_bias, m_gdn_norm, m_conv_sc, m_w_out, m_w_gate, m_w_up, m_w_down, v_meta_tokens, v_mix_pre_norm, v_mix_post_norm, v_ffn_pre_norm, v_ffn_post_norm, v_w_in, v_conv_qkv, v_a_log, v_dt_bias, v_gdn_norm, v_conv_sc, v_w_out, v_w_gate, v_w_up, v_w_down):
    d = x.shape[-1]
    two_d = lambda a: a.reshape(a.shape[-2:])
    weights = dict(zip(WEIGHTS, (meta_tokens, mix_pre_norm, mix_post_norm, ffn_pre_norm, ffn_post_norm, w_in, conv_qkv, a_log,
                                 dt_bias, gdn_norm, conv_sc, w_out, w_gate, w_up, w_down)))
    m_in = dict(zip(WEIGHTS, (m_meta_tokens, m_mix_pre_norm, m_mix_post_norm, m_ffn_pre_norm, m_ffn_post_norm, m_w_in, m_conv_qkv,
                              m_a_log, m_dt_bias, m_gdn_norm, m_conv_sc, m_w_out, m_w_gate, m_w_up, m_w_down)))
    v_in = dict(zip(WEIGHTS, (v_meta_tokens, v_mix_pre_norm, v_mix_post_norm, v_ffn_pre_norm, v_ffn_post_norm, v_w_in, v_conv_qkv,
                              v_a_log, v_dt_bias, v_gdn_norm, v_conv_sc, v_w_out, v_w_gate, v_w_up, v_w_down)))
    core = lax.axis_index("c")
    chip = 2 * lax.axis_index("x") + lax.axis_index("y")
    pieces = _shard_pieces(d)
    shapes = [shape for _, shape in pieces]
    small_shapes = [two_d(weights[n]).shape for n in ("conv_qkv", "conv_sc", "meta_tokens")]

    w_flat = _flatten([two_d(weights[n]).astype(MXU_DTYPE) for n, _ in pieces], 2 * HALF_ROWS).reshape(2, HALF_ROWS, LANES)
    s_flat = _flatten([two_d(weights[n]) for n in ("conv_qkv", "conv_sc", "meta_tokens")], SMALL_ROWS)
    w_all, s_all = _gather_weights(w_flat, s_flat)
    per_chip = [_unflatten(w_all[p], shapes) for p in range(N_CHIPS)]
    full = {n: jnp.concatenate([per_chip[p][i] for p in range(N_CHIPS)], axis=0 if n in ("w_out", "w_down") else 1)
            for i, (n, _) in enumerate(pieces)}
    small_chip = [_unflatten(s_all[p], small_shapes) for p in range(N_CHIPS)]
    conv_qkv_full, conv_sc_full, meta_full = (jnp.concatenate([small_chip[p][i] for p in range(N_CHIPS)], axis=1)
                                              for i in range(3))

    sq, grad_x, g = _local_step(
        x, loss_target, meta_full, (mix_pre_norm, mix_post_norm, ffn_pre_norm, ffn_post_norm), _to_padded_in(full["w_in"]),
        conv_qkv_full, a_log, dt_bias, gdn_norm, conv_sc_full, full["w_out"], jnp.concatenate([full["w_gate"], full["w_up"]], axis=1),
        full["w_down"])

    d_w_in = _from_padded_in(g["w_in_p"])
    by_chip = []
    for p in range(N_CHIPS):
        cols = lambda a, width: a[:, p * width:(p + 1) * width]
        rows = lambda a, height: a[p * height:(p + 1) * height]
        by_chip.append(_flatten([cols(d_w_in, shapes[0][1]), rows(g["w_out"], shapes[1][0]), cols(g["w_gu"][:, :D_FF], shapes[2][1]),
                                 cols(g["w_gu"][:, D_FF:], shapes[3][1]), rows(g["w_down"], shapes[4][0])], 2 * HALF_ROWS))
    g_flat = jnp.stack(by_chip).reshape(N_CHIPS, 2, HALF_ROWS, LANES).transpose(1, 0, 2, 3)
    scalars = jnp.concatenate([g["a_log"], g["dt_bias"], sq], axis=1)
    small = _flatten([g["mix_pre_norm"], g["mix_post_norm"], g["ffn_pre_norm"], g["ffn_post_norm"],
                      jnp.pad(scalars, ((0, 0), (0, LANES - scalars.shape[1]))), g["gdn_norm"], g["conv_qkv"], g["conv_sc"],
                      g["meta_tokens"]], REDUCE_ROWS)
    got_sibling, small_all = _exchange_siblings(g_flat, small)
    part, part_send = _add_sibling(g_flat, got_sibling, core.reshape(1).astype(jnp.int32))
    got_chips = _exchange_chips(part_send)
    g_shard = _share_halves(_add_chips(part, got_chips, jnp.stack([chip, core]).astype(jnp.int32)))
    reduced = _sum_devices(small_all)

    grads = dict(zip([n for n, _ in pieces], _unflatten(g_shard, shapes)))
    r = reduced.reshape(-1)
    at = 0
    for n in ("mix_pre_norm", "mix_post_norm", "ffn_pre_norm", "ffn_post_norm"):
        grads[n] = r[at:at + d].reshape(1, d)
        at += d
    grads["a_log"] = r[at:at + HEADS].reshape(1, HEADS)
    grads["dt_bias"] = r[at + HEADS:at + 2 * HEADS].reshape(1, HEADS)
    loss = (0.5 / d) * r[at + 2 * HEADS]
    at += LANES
    grads["gdn_norm"] = r[at:at + HEAD_DIM].reshape(1, HEAD_DIM)
    at += HEAD_DIM
    for n, shape in (("conv_qkv", (GDN_CONV, 3 * GDN_WIDTH)), ("conv_sc", (SC_CONV, SC_WIDTH)), ("meta_tokens", (N_META, d))):
        full_grad = r[at:at + shape[0] * shape[1]].reshape(shape)
        at += shape[0] * shape[1]
        width = shape[1] // N_CHIPS
        grads[n] = lax.dynamic_slice_in_dim(full_grad, chip * width, width, axis=1)

    out_g, out_d, out_m, out_v = [], [], [], []
    for n in WEIGHTS:
        shape = weights[n].shape
        delta, new_m, new_v = _adamw(two_d(weights[n]), grads[n], two_d(m_in[n]), two_d(v_in[n]), n)
        out_g.append(grads[n].reshape(shape))
        out_d.append(delta.reshape(shape))
        out_m.append(new_m.reshape(shape))
        out_v.append(new_v.reshape(shape))
    return (loss, grad_x, *out_g, *out_d, *out_m, *out_v)
```

```python
import functools

import jax
import jax.numpy as jnp
from jax import lax
from jax.experimental import pallas as pl
from jax.experimental.pallas import tpu as pltpu

F32 = jnp.float32
BF16 = jnp.bfloat16
MXU_DTYPE = jnp.bfloat16
MESH = pl.DeviceIdType.MESH

D_MODEL = 1024
N_META = 16
HEADS = 4
HEAD_DIM = 128
GDN_WIDTH = HEADS * HEAD_DIM
GDN_CONV = 4
CHUNK = 64
SC_WIDTH = D_MODEL - GDN_WIDTH
SC_CONV = 3
D_FF = 2816
IN_WIDTH = 4 * GDN_WIDTH + 2 * HEADS + 3 * SC_WIDTH
IN_PAD = 3840
BA_COL = (4 * GDN_WIDTH + 3 * SC_WIDTH) // 128
EPS = 1e-6
LANES = 128
N_CHIPS = 4
VMEM_LIMIT = 48 * 2 ** 20

ADAM_LR = 0.001
ADAM_B1 = 0.9
ADAM_B2 = 0.999
ADAM_EPS = 1e-08
ADAM_WD = 0.01
ADAM_STEP = 10


def _pick(n, candidates):
    for c in candidates:
        if n % c == 0:
            return c
    return n


def _row_tile(n):
    return _pick(n, (352, 256, 176, 128, 64, 32, 16, 8))


def _params(*sem):
    return pltpu.CompilerParams(dimension_semantics=sem, vmem_limit_bytes=VMEM_LIMIT)


def _sigmoid(x):
    return 1.0 / (1.0 + jnp.exp(-x))


def _softplus(x):
    return jnp.maximum(x, 0.0) + jnp.log(1.0 + jnp.exp(-jnp.abs(x)))


def _dsilu(x, s):
    return s * (1.0 + x * (1.0 - s))


def _mm(a, b, mode, out_dtype, name):
    if mode == "tn":
        k_dim, m_dim = a.shape
    else:
        m_dim, k_dim = a.shape
    n_dim = b.shape[0] if mode == "nt" else b.shape[1]
    tm = _pick(m_dim, (1408, 1024, 512, 256, 128) if mode == "tn" else (1056, 1024, 704, 512, 256, 128))
    tn = _pick(n_dim, (1408, 1280, 1024, 768, 512, 256, 128))
    tk = _pick(k_dim, (1408, 1280, 1056, 1024, 512, 256, 128))
    nk = k_dim // tk
    if mode == "nn":
        a_spec = pl.BlockSpec((tm, tk), lambda i, j, k: (i, k))
        b_spec = pl.BlockSpec((tk, tn), lambda i, j, k: (k, j))
        dims = (((1,), (0,)), ((), ()))
    elif mode == "nt":
        a_spec = pl.BlockSpec((tm, tk), lambda i, j, k: (i, k))
        b_spec = pl.BlockSpec((tn, tk), lambda i, j, k: (j, k))
        dims = (((1,), (1,)), ((), ()))
    else:
        a_spec = pl.BlockSpec((tk, tm), lambda i, j, k: (k, i))
        b_spec = pl.BlockSpec((tk, tn), lambda i, j, k: (k, j))
        dims = (((0,), (0,)), ((), ()))

    def body(a_ref, b_ref, o_ref, acc_ref):
        k = pl.program_id(2)
        p = lax.dot_general(a_ref[...], b_ref[...], dims, preferred_element_type=F32)

        @pl.when(k == 0)
        def _():
            acc_ref[...] = p

        @pl.when(k > 0)
        def _():
            acc_ref[...] += p

        @pl.when(k == nk - 1)
        def _():
            o_ref[...] = acc_ref[...].astype(out_dtype)

    return pl.pallas_call(
        body, name=name,
        out_shape=jax.ShapeDtypeStruct((m_dim, n_dim), out_dtype),
        grid=(m_dim // tm, n_dim // tn, nk),
        in_specs=[a_spec, b_spec],
        out_specs=pl.BlockSpec((tm, tn), lambda i, j, k: (i, j)),
        scratch_shapes=[pltpu.VMEM((tm, tn), F32)],
        compiler_params=_params("parallel", "parallel", "arbitrary"),
    )(a, b)


def _rms_apply(x, w):
    r = lax.rsqrt(jnp.mean(x * x, axis=-1, keepdims=True) + EPS)
    return x * r * w


def _rms_bwd(x, w, dy):
    r = lax.rsqrt(jnp.mean(x * x, axis=-1, keepdims=True) + EPS)
    xh = x * r
    dyw = dy * w
    dx = r * (dyw - xh * jnp.mean(dyw * xh, axis=-1, keepdims=True))
    return dx, jnp.sum(dy * xh, axis=0, keepdims=True)


def _accumulate(ref, first, value):
    @pl.when(first)
    def _():
        ref[...] = value

    @pl.when(jnp.logical_not(first))
    def _():
        ref[...] += value


def _rows(tr, width):
    return pl.BlockSpec((tr, width), lambda i: (i, 0))


def _vec(width):
    return pl.BlockSpec((1, width), lambda i: (0, 0))


def _rms_fwd(h, w, name):
    n, d = h.shape
    tr = _row_tile(n)

    def body(h_ref, w_ref, u_ref):
        u_ref[...] = _rms_apply(h_ref[...], w_ref[...]).astype(u_ref.dtype)

    return pl.pallas_call(
        body, name=name, out_shape=jax.ShapeDtypeStruct((n, d), MXU_DTYPE), grid=(n // tr,),
        in_specs=[_rows(tr, d), _vec(d)], out_specs=_rows(tr, d), compiler_params=_params("parallel"),
    )(h, w)


def _mix_residual(h0, mix, w_post, w_pre):
    n, d = h0.shape
    tr = _row_tile(n)

    def body(h0_ref, mix_ref, wpost_ref, wpre_ref, h1_ref, u2_ref):
        h1 = h0_ref[...] + _rms_apply(mix_ref[...], wpost_ref[...])
        h1_ref[...] = h1
        u2_ref[...] = _rms_apply(h1, wpre_ref[...]).astype(u2_ref.dtype)

    return pl.pallas_call(
        body, name="mix_residual",
        out_shape=(jax.ShapeDtypeStruct((n, d), F32), jax.ShapeDtypeStruct((n, d), MXU_DTYPE)), grid=(n // tr,),
        in_specs=[_rows(tr, d), _rows(tr, d), _vec(d), _vec(d)], out_specs=(_rows(tr, d), _rows(tr, d)),
        compiler_params=_params("parallel"),
    )(h0, mix, w_post, w_pre)


def _swiglu_act(gu):
    n = gu.shape[0]
    tr = _pick(n, (176, 128, 64, 32, 16, 8))

    def body(gu_ref, act_ref):
        gate = gu_ref[:, :D_FF]
        act_ref[...] = (gate * _sigmoid(gate) * gu_ref[:, D_FF:]).astype(act_ref.dtype)

    return pl.pallas_call(
        body, name="swiglu_act", out_shape=jax.ShapeDtypeStruct((n, D_FF), MXU_DTYPE), grid=(n // tr,),
        in_specs=[_rows(tr, 2 * D_FF)], out_specs=_rows(tr, D_FF), compiler_params=_params("parallel"),
    )(gu)


def _swiglu_bwd(gu, dact):
    n = gu.shape[0]
    tr = _pick(n, (176, 128, 64, 32, 16, 8))

    def body(gu_ref, dact_ref, dgu_ref):
        gate = gu_ref[:, :D_FF]
        up = gu_ref[:, D_FF:]
        s = _sigmoid(gate)
        da = dact_ref[...]
        dgu_ref[:, :D_FF] = (da * up * _dsilu(gate, s)).astype(dgu_ref.dtype)
        dgu_ref[:, D_FF:] = (da * gate * s).astype(dgu_ref.dtype)

    return pl.pallas_call(
        body, name="swiglu_bwd", out_shape=jax.ShapeDtypeStruct((n, 2 * D_FF), MXU_DTYPE), grid=(n // tr,),
        in_specs=[_rows(tr, 2 * D_FF), _rows(tr, D_FF)], out_specs=_rows(tr, 2 * D_FF),
        compiler_params=_params("parallel"),
    )(gu, dact)


def _loss_head(h1, ffn, w_post, target, rows_per_seq, x_offset):
    n, d = h1.shape
    tr = _row_tile(rows_per_seq)
    tiles_per_seq = rows_per_seq // tr

    def body(h1_ref, ffn_ref, w_ref, t_ref, dh2_ref, dffn_ref, dw_ref, sq_ref):
        i = pl.program_id(0)
        w = w_ref[...]
        f = ffn_ref[...]
        r = lax.rsqrt(jnp.mean(f * f, axis=-1, keepdims=True) + EPS)
        fh = f * r
        row = lax.rem(i, tiles_per_seq) * tr + lax.broadcasted_iota(jnp.int32, (tr, 1), 0)
        err = jnp.where(row >= x_offset, h1_ref[...] + fh * w - t_ref[...], 0.0)
        dh2 = err * (1.0 / d)
        dh2_ref[...] = dh2
        dyw = dh2 * w
        dffn_ref[...] = (r * (dyw - fh * jnp.mean(dyw * fh, axis=-1, keepdims=True))).astype(dffn_ref.dtype)
        _accumulate(dw_ref, i == 0, jnp.sum(dh2 * fh, axis=0, keepdims=True))
        _accumulate(sq_ref, i == 0, jnp.sum(jnp.sum(err * err, axis=1, keepdims=True), axis=0, keepdims=True))

    return pl.pallas_call(
        body, name="loss_head",
        out_shape=(jax.ShapeDtypeStruct((n, d), F32), jax.ShapeDtypeStruct((n, d), MXU_DTYPE),
                   jax.ShapeDtypeStruct((1, d), F32), jax.ShapeDtypeStruct((1, 1), F32)),
        grid=(n // tr,),
        in_specs=[_rows(tr, d), _rows(tr, d), _vec(d), _rows(tr, d)],
        out_specs=(_rows(tr, d), _rows(tr, d), _vec(d), _vec(1)),
        compiler_params=_params("arbitrary"),
    )(h1, ffn, w_post, target)


def _mid_bwd(h1, mix, w_mix_post, w_ffn_pre, dh2, du2):
    n, d = h1.shape
    tr = _row_tile(n)

    def body(h1_ref, mix_ref, wpost_ref, wpre_ref, dh2_ref, du2_ref, dh1_ref, dmix_ref, dwpre_ref, dwpost_ref):
        i = pl.program_id(0)
        dx, dwpre = _rms_bwd(h1_ref[...], wpre_ref[...], du2_ref[...])
        dh1 = dh2_ref[...] + dx
        dh1_ref[...] = dh1
        dmix, dwpost = _rms_bwd(mix_ref[...], wpost_ref[...], dh1)
        dmix_ref[...] = dmix.astype(dmix_ref.dtype)
        _accumulate(dwpre_ref, i == 0, dwpre)
        _accumulate(dwpost_ref, i == 0, dwpost)

    return pl.pallas_call(
        body, name="mid_bwd",
        out_shape=(jax.ShapeDtypeStruct((n, d), F32), jax.ShapeDtypeStruct((n, d), MXU_DTYPE),
                   jax.ShapeDtypeStruct((1, d), F32), jax.ShapeDtypeStruct((1, d), F32)),
        grid=(n // tr,),
        in_specs=[_rows(tr, d), _rows(tr, d), _vec(d), _vec(d), _rows(tr, d), _rows(tr, d)],
        out_specs=(_rows(tr, d), _rows(tr, d), _vec(d), _vec(d)),
        compiler_params=_params("arbitrary"),
    )(h1, mix, w_mix_post, w_ffn_pre, dh2, du2)


def _in_bwd(h0, w_pre, dh1, du1):
    n, d = h0.shape
    tr = _row_tile(n)

    def body(h0_ref, w_ref, dh1_ref, du1_ref, dh0_ref, dw_ref):
        dx, dw = _rms_bwd(h0_ref[...], w_ref[...], du1_ref[...])
        dh0_ref[...] = dh1_ref[...] + dx
        _accumulate(dw_ref, pl.program_id(0) == 0, dw)

    return pl.pallas_call(
        body, name="in_bwd",
        out_shape=(jax.ShapeDtypeStruct((n, d), F32), jax.ShapeDtypeStruct((1, d), F32)), grid=(n // tr,),
        in_specs=[_rows(tr, d), _vec(d), _rows(tr, d), _rows(tr, d)], out_specs=(_rows(tr, d), _vec(d)),
        compiler_params=_params("arbitrary"),
    )(h0, w_pre, dh1, du1)


def _lane_is(lo, hi):
    lane = lax.broadcasted_iota(jnp.int32, (1, LANES), 1)
    return jnp.logical_and(lane >= lo, lane < hi)


def _gates_fwd(proj, a_log_l, dt_bias_l, rows_per_seq, pad_rows):
    n = proj.shape[0]
    tr = _row_tile(rows_per_seq)
    tiles_per_seq = rows_per_seq // tr

    def body(p_ref, a_ref, dt_ref, o_ref):
        x = p_ref[...]
        row = lax.rem(pl.program_id(0), tiles_per_seq) * tr + lax.broadcasted_iota(jnp.int32, (tr, 1), 0)
        g = -jnp.exp(a_ref[...]) * _softplus(x + dt_ref[...])
        val = jnp.where(_lane_is(0, HEADS), _sigmoid(x), jnp.where(_lane_is(HEADS, 2 * HEADS), g, 0.0))
        o_ref[...] = jnp.where(row >= pad_rows, val, 0.0)

    return pl.pallas_call(
        body, name="gates_fwd", out_shape=jax.ShapeDtypeStruct((n, LANES), F32), grid=(n // tr,),
        in_specs=[pl.BlockSpec((tr, LANES), lambda i: (i, BA_COL)), _vec(LANES), _vec(LANES)],
        out_specs=_rows(tr, LANES), compiler_params=_params("parallel"),
    )(proj, a_log_l, dt_bias_l)


def _gates_bwd(proj, dbg, a_log_l, dt_bias_l, rows_per_seq, pad_rows):
    n = proj.shape[0]
    tr = _row_tile(rows_per_seq)
    tiles_per_seq = rows_per_seq // tr

    def body(p_ref, d_ref, a_ref, dt_ref, dx_ref, da_ref, ddt_ref):
        i = pl.program_id(0)
        x = p_ref[...]
        d = d_ref[...]
        row = lax.rem(i, tiles_per_seq) * tr + lax.broadcasted_iota(jnp.int32, (tr, 1), 0)
        live = row >= pad_rows
        beta = _sigmoid(x)
        ea = jnp.exp(a_ref[...])
        xa = x + dt_ref[...]
        g = -ea * _softplus(xa)
        is_g = _lane_is(HEADS, 2 * HEADS)
        d_alogit = jnp.where(jnp.logical_and(live, is_g), d * (-ea) * _sigmoid(xa), 0.0)
        d_blogit = jnp.where(jnp.logical_and(live, _lane_is(0, HEADS)), d * beta * (1.0 - beta), 0.0)
        dx_ref[:, :LANES] = (d_alogit + d_blogit).astype(dx_ref.dtype)
        dx_ref[:, LANES:] = jnp.zeros((tr, LANES), dx_ref.dtype)
        _accumulate(da_ref, i == 0, jnp.sum(jnp.where(jnp.logical_and(live, is_g), d * g, 0.0), axis=0, keepdims=True))
        _accumulate(ddt_ref, i == 0, jnp.sum(d_alogit, axis=0, keepdims=True))

    return pl.pallas_call(
        body, name="gates_bwd",
        out_shape=(jax.ShapeDtypeStruct((n, 2 * LANES), MXU_DTYPE), jax.ShapeDtypeStruct((1, LANES), F32),
                   jax.ShapeDtypeStruct((1, LANES), F32)),
        grid=(n // tr,),
        in_specs=[pl.BlockSpec((tr, LANES), lambda i: (i, BA_COL)), _rows(tr, LANES), _vec(LANES), _vec(LANES)],
        out_specs=(_rows(tr, 2 * LANES), _vec(LANES), _vec(LANES)),
        compiler_params=_params("arbitrary"),
    )(proj, dbg, a_log_l, dt_bias_l)


def _shift_down(x, k):
    return x if k == 0 else pltpu.roll(x, k, 0)


def _shift_up(x, k):
    return x if k == 0 else pltpu.roll(x, x.shape[0] - k, 0)


def _causal_conv(x, w, width):
    acc = w[width - 1:width, :] * x
    for i in range(width - 1):
        acc = acc + w[i:i + 1, :] * _shift_down(x, width - 1 - i)
    return acc


def _seq_head(rs, col0):
    return pl.BlockSpec((rs, LANES), lambda j, b: (b, col0 + j))


def _live_rows(rs, pad_rows):
    return lax.broadcasted_iota(jnp.int32, (rs, 1), 0) >= pad_rows


def _qkv_fwd(proj, conv_w, kind, rs, pad_rows):
    n = proj.shape[0]
    col0 = {"q": 0, "k": HEADS, "v": 2 * HEADS}[kind]

    def body(p_ref, w_ref, o_ref):
        c = _causal_conv(p_ref[...], w_ref[...], GDN_CONV)
        s = c * _sigmoid(c)
        if kind != "v":
            s = s * lax.rsqrt(jnp.sum(s * s, axis=-1, keepdims=True) + EPS)
        if kind == "q":
            s = s * (HEAD_DIM ** -0.5)
        o_ref[...] = jnp.where(_live_rows(rs, pad_rows), s, 0.0)

    return pl.pallas_call(
        body, name="qkv_fwd_" + kind, out_shape=jax.ShapeDtypeStruct((n, GDN_WIDTH), F32), grid=(HEADS, n // rs),
        in_specs=[_seq_head(rs, col0), pl.BlockSpec((GDN_CONV, LANES), lambda j, b: (0, col0 + j))],
        out_specs=_seq_head(rs, 0), compiler_params=_params("parallel", "parallel"),
    )(proj, conv_w)


def _qkv_bwd(dy, proj, conv_w, kind, rs, pad_rows):
    n = proj.shape[0]
    col0 = {"q": 0, "k": HEADS, "v": 2 * HEADS}[kind]

    def body(dy_ref, p_ref, w_ref, dp_ref, dw_ref):
        pre = p_ref[...]
        w = w_ref[...]
        c = _causal_conv(pre, w, GDN_CONV)
        sg = _sigmoid(c)
        s = c * sg
        ds = dy_ref[...]
        if kind == "q":
            ds = ds * (HEAD_DIM ** -0.5)
        if kind != "v":
            r = lax.rsqrt(jnp.sum(s * s, axis=-1, keepdims=True) + EPS)
            sh = s * r
            ds = r * (ds - sh * jnp.sum(ds * sh, axis=-1, keepdims=True))
        dc = jnp.where(_live_rows(rs, pad_rows), ds * _dsilu(c, sg), 0.0)
        dpre = w[GDN_CONV - 1:GDN_CONV, :] * dc
        for i in range(GDN_CONV - 1):
            dpre = dpre + w[i:i + 1, :] * _shift_up(dc, GDN_CONV - 1 - i)
        dp_ref[...] = dpre.astype(dp_ref.dtype)
        dw = jnp.concatenate(
            [jnp.sum(dc * _shift_down(pre, GDN_CONV - 1 - i), axis=0, keepdims=True) for i in range(GDN_CONV)], axis=0)
        _accumulate(dw_ref, pl.program_id(1) == 0, dw)

    return pl.pallas_call(
        body, name="qkv_bwd_" + kind,
        out_shape=(jax.ShapeDtypeStruct((n, GDN_WIDTH), MXU_DTYPE), jax.ShapeDtypeStruct((GDN_CONV, GDN_WIDTH), F32)),
        grid=(HEADS, n // rs),
        in_specs=[_seq_head(rs, 0), _seq_head(rs, col0), pl.BlockSpec((GDN_CONV, LANES), lambda j, b: (0, col0 + j))],
        out_specs=(_seq_head(rs, 0), pl.BlockSpec((GDN_CONV, LANES), lambda j, b: (0, j))),
        compiler_params=_params("parallel", "arbitrary"),
    )(dy, proj, conv_w)


SC_COL = 4 * HEADS


def _sc_fwd(proj, conv_w, rs):
    n = proj.shape[0]

    def body(x_ref, b_ref, c_ref, w_ref, y_ref):
        y_ref[...] = (b_ref[...] * _causal_conv(c_ref[...] * x_ref[...], w_ref[...], SC_CONV)).astype(y_ref.dtype)

    return pl.pallas_call(
        body, name="sc_fwd", out_shape=jax.ShapeDtypeStruct((n, SC_WIDTH), MXU_DTYPE), grid=(HEADS, n // rs),
        in_specs=[_seq_head(rs, SC_COL), _seq_head(rs, SC_COL + 4), _seq_head(rs, SC_COL + 8),
                  pl.BlockSpec((SC_CONV, LANES), lambda j, b: (0, j))],
        out_specs=_seq_head(rs, 0), compiler_params=_params("parallel", "parallel"),
    )(proj, proj, proj, conv_w)


def _sc_bwd(dcat, proj, conv_w, rs):
    n = proj.shape[0]

    def body(dy_ref, x_ref, b_ref, c_ref, w_ref, dx_ref, db_ref, dc_ref, dw_ref):
        w = w_ref[...]
        x = x_ref[...]
        cc = c_ref[...]
        u = cc * x
        dy = dy_ref[...]
        db_ref[...] = (dy * _causal_conv(u, w, SC_CONV)).astype(db_ref.dtype)
        dcv = dy * b_ref[...]
        du = w[SC_CONV - 1:SC_CONV, :] * dcv
        for i in range(SC_CONV - 1):
            du = du + w[i:i + 1, :] * _shift_up(dcv, SC_CONV - 1 - i)
        dx_ref[...] = (du * cc).astype(dx_ref.dtype)
        dc_ref[...] = (du * x).astype(dc_ref.dtype)
        dw = jnp.concatenate(
            [jnp.sum(dcv * _shift_down(u, SC_CONV - 1 - i), axis=0, keepdims=True) for i in range(SC_CONV)], axis=0)
        _accumulate(dw_ref, pl.program_id(1) == 0, dw)

    piece = jax.ShapeDtypeStruct((n, SC_WIDTH), MXU_DTYPE)
    return pl.pallas_call(
        body, name="sc_bwd", out_shape=(piece, piece, piece, jax.ShapeDtypeStruct((SC_CONV, SC_WIDTH), F32)),
        grid=(HEADS, n // rs),
        in_specs=[_seq_head(rs, HEADS), _seq_head(rs, SC_COL), _seq_head(rs, SC_COL + 4), _seq_head(rs, SC_COL + 8),
                  pl.BlockSpec((SC_CONV, LANES), lambda j, b: (0, j))],
        out_specs=(_seq_head(rs, 0), _seq_head(rs, 0), _seq_head(rs, 0),
                   pl.BlockSpec((SC_CONV, LANES), lambda j, b: (0, j))),
        compiler_params=_params("parallel", "arbitrary"),
    )(dcat, proj, proj, proj, conv_w)


Z_COL = 3 * HEADS


def _gate_fwd(o, proj, gdn_norm, rs):
    n = proj.shape[0]

    def body(o_ref, z_ref, w_ref, y_ref):
        z = z_ref[...]
        y_ref[...] = (_rms_apply(o_ref[...], w_ref[...]) * z * _sigmoid(z)).astype(y_ref.dtype)

    return pl.pallas_call(
        body, name="gate_fwd", out_shape=jax.ShapeDtypeStruct((n, GDN_WIDTH), MXU_DTYPE), grid=(HEADS, n // rs),
        in_specs=[_seq_head(rs, 0), _seq_head(rs, Z_COL), pl.BlockSpec((1, LANES), lambda j, b: (0, 0))],
        out_specs=_seq_head(rs, 0), compiler_params=_params("parallel", "parallel"),
    )(o, proj, gdn_norm)


def _gate_bwd(dcat, o, proj, gdn_norm, rs):
    n = proj.shape[0]

    def body(dy_ref, o_ref, z_ref, w_ref, do_ref, dz_ref, dw_ref):
        z = z_ref[...]
        w = w_ref[...]
        o = o_ref[...]
        dy = dy_ref[...]
        s = _sigmoid(z)
        dz_ref[...] = (dy * _rms_apply(o, w) * _dsilu(z, s)).astype(dz_ref.dtype)
        do, dw = _rms_bwd(o, w, dy * z * s)
        do_ref[...] = do
        _accumulate(dw_ref, jnp.logical_and(pl.program_id(0) == 0, pl.program_id(1) == 0), dw)

    return pl.pallas_call(
        body, name="gate_bwd",
        out_shape=(jax.ShapeDtypeStruct((n, GDN_WIDTH), F32), jax.ShapeDtypeStruct((n, GDN_WIDTH), MXU_DTYPE),
                   jax.ShapeDtypeStruct((1, LANES), F32)),
        grid=(HEADS, n // rs),
        in_specs=[_seq_head(rs, 0), _seq_head(rs, 0), _seq_head(rs, Z_COL), pl.BlockSpec((1, LANES), lambda j, b: (0, 0))],
        out_specs=(_seq_head(rs, 0), _seq_head(rs, 0), pl.BlockSpec((1, LANES), lambda j, b: (0, 0))),
        compiler_params=_params("arbitrary", "arbitrary"),
    )(dcat, o, proj, gdn_norm)


def _dot(a, b):
    return jnp.dot(a.astype(MXU_DTYPE), b.astype(MXU_DTYPE), preferred_element_type=F32)


def _dot_nt(a, b):
    return lax.dot_general(a.astype(MXU_DTYPE), b.astype(MXU_DTYPE), (((1,), (1,)), ((), ())),
                           preferred_element_type=F32)


def _dot_tn(a, b):
    return lax.dot_general(a.astype(MXU_DTYPE), b.astype(MXU_DTYPE), (((0,), (0,)), ((), ())),
                           preferred_element_type=F32)


def _split(x):
    hi = x.astype(MXU_DTYPE)
    return hi, (x - hi.astype(F32)).astype(MXU_DTYPE)


def _dot_split(a, b):
    mm = functools.partial(jnp.dot, preferred_element_type=F32)
    return mm(a[0], b[0]) + (mm(a[0], b[1]) + mm(a[1], b[0]))


def _unit_lower_inverses(mats, eye):
    inv = [eye - a for a in mats]
    power = [_split(a) for a in mats]
    span = 2
    while span < CHUNK:
        power = [_split(_dot_split(p, p)) for p in power]
        inv = [i + _dot_split(_split(i), p) for i, p in zip(inv, power)]
        span *= 2
    return inv


def _chunk_masks():
    ii = lax.broadcasted_iota(jnp.int32, (CHUNK, CHUNK), 0)
    jj = lax.broadcasted_iota(jnp.int32, (CHUNK, CHUNK), 1)
    return ii, jj


def _chunk_decay(g_col, ii, jj):
    incl = ii >= jj
    g_row = jnp.sum(jnp.where(ii == jj, g_col, 0.0), axis=0, keepdims=True)
    gc_col = jnp.sum(jnp.where(incl, g_row, 0.0), axis=1, keepdims=True)
    gc_row = jnp.sum(jnp.where(ii <= jj, g_col, 0.0), axis=0, keepdims=True)
    g_total = jnp.sum(g_row, axis=1, keepdims=True)
    decay = jnp.where(incl, jnp.exp(jnp.where(incl, gc_col - gc_row, 0.0)), 0.0)
    return gc_col, g_total, decay


def _gdn_segments(rs, candidates):
    chunks = rs // CHUNK
    seg_chunks = _pick(chunks, candidates)
    return chunks, seg_chunks, chunks // seg_chunks


def _head_lanes(h):
    return slice(h * HEAD_DIM, (h + 1) * HEAD_DIM)


def _gdn_fwd(q, k, v, bg, rs):
    n = q.shape[0]
    batch = n // rs
    chunks, seg_chunks, segs = _gdn_segments(rs, (11, 8, 4, 2))
    seg_rows = seg_chunks * CHUNK
    chains = [(b, h) for b in range(batch) for h in range(HEADS)]
    each = lambda f, *lists: [f(*args) for args in zip(*lists)]

    def body(q_ref, k_ref, v_ref, bg_ref, o_ref, s_ref, t_ref, state_ref):
        @pl.when(pl.program_id(0) == 0)
        def _():
            state_ref[...] = jnp.zeros_like(state_ref)

        ii, jj = _chunk_masks()
        incl = ii >= jj
        eye = (ii == jj).astype(F32)

        def chunk(c, carry):
            rows = pl.ds(pl.multiple_of(c * CHUNK, CHUNK), CHUNK)
            bgc = [bg_ref[b, rows, :] for b in range(batch)]
            qc = [q_ref[b, rows, _head_lanes(h)] for b, h in chains]
            kc = [k_ref[b, rows, _head_lanes(h)] for b, h in chains]
            vc = [v_ref[b, rows, _head_lanes(h)] for b, h in chains]
            beta = [bgc[b][:, h:h + 1] for b, h in chains]
            state = [state_ref[b, h] for b, h in chains]
            dec = [_chunk_decay(bgc[b][:, HEADS + h:HEADS + h + 1], ii, jj) for b, h in chains]
            gc_col, g_total, decay = ([d[i] for d in dec] for i in range(3))
            kb = each(lambda x, y: x * y, kc, beta)
            a = each(lambda x, y, d: jnp.where(ii > jj, _dot_nt(x, y) * d, 0.0), kb, kc, decay)
            t_inv = _unit_lower_inverses(a, eye)
            eg = [jnp.exp(g) for g in gc_col]
            u = each(lambda t, x, y: _dot(t, x * y), t_inv, vc, beta)
            w = each(lambda t, x, e: _dot(t, x * e), t_inv, kb, eg)
            qk = each(lambda x, y, d: jnp.where(incl, _dot_nt(x, y) * d, 0.0), qc, kc, decay)
            v_new = each(lambda x, y, s: x - _dot(y, s), u, w, state)
            o = each(lambda x, e, s, m, vn: _dot(x * e, s) + _dot(m, vn), qc, eg, state, qk, v_new)
            new_state = each(lambda s, gt, x, g, vn: s * jnp.exp(gt) + _dot_tn(x * jnp.exp(gt - g), vn),
                             state, g_total, kc, gc_col, v_new)
            for i, (b, h) in enumerate(chains):
                s_ref[b, h, c] = state[i]
                t_ref[b, h, c] = t_inv[i]
                o_ref[b, rows, _head_lanes(h)] = o[i]
                state_ref[b, h] = new_state[i]
            return carry

        lax.fori_loop(0, seg_chunks, chunk, 0)

    rows_spec = lambda width: pl.BlockSpec((batch, seg_rows, width), lambda s: (0, s, 0))
    per_chunk = lambda r, c: pl.BlockSpec((batch, HEADS, seg_chunks, r, c), lambda s: (0, 0, s, 0, 0))
    as_seqs = lambda a: a.reshape(batch, rs, a.shape[-1])
    o, states, t_invs = pl.pallas_call(
        body, name="gdn_fwd",
        out_shape=(jax.ShapeDtypeStruct((batch, rs, GDN_WIDTH), F32),
                   jax.ShapeDtypeStruct((batch, HEADS, chunks, HEAD_DIM, HEAD_DIM), F32),
                   jax.ShapeDtypeStruct((batch, HEADS, chunks, CHUNK, CHUNK), F32)),
        grid=(segs,),
        in_specs=[rows_spec(GDN_WIDTH), rows_spec(GDN_WIDTH), rows_spec(GDN_WIDTH), rows_spec(LANES)],
        out_specs=(rows_spec(GDN_WIDTH), per_chunk(HEAD_DIM, HEAD_DIM), per_chunk(CHUNK, CHUNK)),
        scratch_shapes=[pltpu.VMEM((batch, HEADS, HEAD_DIM, HEAD_DIM), F32)],
        compiler_params=_params("arbitrary"),
    )(as_seqs(q), as_seqs(k), as_seqs(v), as_seqs(bg))
    return o.reshape(n, GDN_WIDTH), states, t_invs


def _gdn_bwd(do, q, k, v, bg, states, t_invs, rs):
    n = q.shape[0]
    batch = n // rs
    chunks, seg_chunks, segs = _gdn_segments(rs, (3, 4, 2))
    seg_rows = seg_chunks * CHUNK
    chains = [(b, h) for b in range(batch) for h in range(HEADS)]
    each = lambda f, *lists: [f(*args) for args in zip(*lists)]

    def body(do_ref, q_ref, k_ref, v_ref, bg_ref, s_ref, t_ref, dq_ref, dk_ref, dv_ref, dbg_ref, dstate_ref):
        @pl.when(pl.program_id(0) == 0)
        def _():
            dstate_ref[...] = jnp.zeros_like(dstate_ref)

        ii, jj = _chunk_masks()
        incl = ii >= jj
        strict = ii > jj
        lane = lax.broadcasted_iota(jnp.int32, (1, LANES), 1)

        def rowsum(x):
            return jnp.sum(x, axis=1, keepdims=True)

        def total(x):
            return jnp.sum(rowsum(x), axis=0, keepdims=True)

        def chunk(step, carry):
            c = seg_chunks - 1 - step
            rows = pl.ds(pl.multiple_of(c * CHUNK, CHUNK), CHUNK)
            bgc = [bg_ref[b, rows, :] for b in range(batch)]
            qc = [q_ref[b, rows, _head_lanes(h)] for b, h in chains]
            kc = [k_ref[b, rows, _head_lanes(h)] for b, h in chains]
            vc = [v_ref[b, rows, _head_lanes(h)] for b, h in chains]
            doc = [do_ref[b, rows, _head_lanes(h)] for b, h in chains]
            beta = [bgc[b][:, h:h + 1] for b, h in chains]
            state = [s_ref[b, h, c] for b, h in chains]
            t_inv = [t_ref[b, h, c] for b, h in chains]
            d_state = [dstate_ref[b, h] for b, h in chains]
            dec = [_chunk_decay(bgc[b][:, HEADS + h:HEADS + h + 1], ii, jj) for b, h in chains]
            gc_col, g_total, decay = ([d[i] for d in dec] for i in range(3))
            kb = each(lambda x, y: x * y, kc, beta)
            vb = each(lambda x, y: x * y, vc, beta)
            eg = [jnp.exp(g) for g in gc_col]
            kbg = each(lambda x, y: x * y, kb, eg)
            a = each(lambda x, y, d: jnp.where(strict, _dot_nt(x, y) * d, 0.0), kb, kc, decay)
            qk = each(lambda x, y, d: jnp.where(incl, _dot_nt(x, y) * d, 0.0), qc, kc, decay)
            w = each(_dot, t_inv, kbg)
            u = each(_dot, t_inv, vb)
            q_dec = each(lambda x, y: x * y, qc, eg)
            ek = each(lambda gt, g: jnp.exp(gt - g), g_total, gc_col)
            k_dec = each(lambda x, y: x * y, kc, ek)
            g_last = [jnp.exp(gt) for gt in g_total]
            v_new = each(lambda x, y, s: x - _dot(y, s), u, w, state)
            dv_new = each(lambda m, d, x, ds: _dot_tn(m, d) + _dot(x, ds), qk, doc, k_dec, d_state)
            dqk = each(lambda d, vn: jnp.where(incl, _dot_nt(d, vn), 0.0), doc, v_new)
            dq_dec = each(_dot_nt, doc, state)
            dk_dec = each(_dot_nt, v_new, d_state)
            dg_last = each(lambda s, ds: total(s * ds), state, d_state)
            new_d_state = each(lambda x, d, gl, ds, y, dvn: _dot_tn(x, d) + gl * ds - _dot_tn(y, dvn),
                               q_dec, doc, g_last, d_state, w, dv_new)
            dw = each(lambda dvn, s: -_dot_nt(dvn, s), dv_new, state)
            dt = each(lambda dvn, x, y, z: _dot_nt(dvn, x) + _dot_nt(y, z), dv_new, vb, dw, kbg)
            dvb = each(_dot_tn, t_inv, dv_new)
            dkbg = each(_dot_tn, t_inv, dw)
            t_dt = each(_dot_tn, t_inv, dt)
            da = each(lambda x, t: -jnp.where(strict, _dot_nt(x, t), 0.0), t_dt, t_inv)
            dm_a = each(lambda x, y: x * y, da, decay)
            dm_qk = each(lambda x, y: x * y, dqk, decay)
            e = each(lambda x, y, z, t: x * y + z * t, da, a, dqk, qk)
            dkb = each(lambda m, x, y, z: _dot(m, x) + y * z, dm_a, kc, dkbg, eg)
            dk = each(lambda m, x, m2, y, z, t, p, bt: _dot_tn(m, x) + _dot_tn(m2, y) + z * t + p * bt,
                      dm_a, kb, dm_qk, qc, dk_dec, ek, dkb, beta)
            dq = each(lambda m, x, y, z: _dot(m, x) + y * z, dm_qk, kc, dq_dec, eg)
            dbeta = each(lambda x, y, z, t: rowsum(x * y + z * t), dkb, kc, dvb, vc)
            dgc = each(lambda x, p, pd, r, rd, s, sd: rowsum(x) - rowsum(jnp.where(ii == jj, jnp.sum(x, axis=0, keepdims=True), 0.0))
                       + rowsum(p * pd - r * rd + s * sd), e, dq_dec, q_dec, dk_dec, k_dec, dkbg, kbg)
            d_total = each(lambda r, rd, x, gl: total(r * rd) + x * gl, dk_dec, k_dec, dg_last, g_last)
            dg = each(lambda x, t: rowsum(jnp.where(jj >= ii, jnp.sum(jnp.where(ii == jj, x, 0.0), axis=0, keepdims=True), 0.0)) + t,
                      dgc, d_total)
            dbg = [jnp.zeros((CHUNK, LANES), F32) for _ in range(batch)]
            for i, (b, h) in enumerate(chains):
                dstate_ref[b, h] = new_d_state[i]
                dk_ref[b, rows, _head_lanes(h)] = dk[i]
                dq_ref[b, rows, _head_lanes(h)] = dq[i]
                dv_ref[b, rows, _head_lanes(h)] = dvb[i] * beta[i]
                dbg[b] = dbg[b] + jnp.where(lane == h, dbeta[i], 0.0) + jnp.where(lane == HEADS + h, dg[i], 0.0)
            for b in range(batch):
                dbg_ref[b, rows, :] = dbg[b]
            return carry

        lax.fori_loop(0, seg_chunks, chunk, 0)

    rows_spec = lambda width: pl.BlockSpec((batch, seg_rows, width), lambda s: (0, segs - 1 - s, 0))
    per_chunk = lambda r, c: pl.BlockSpec((batch, HEADS, seg_chunks, r, c), lambda s: (0, 0, segs - 1 - s, 0, 0))
    as_seqs = lambda a: a.reshape(batch, rs, a.shape[-1])
    grad = jax.ShapeDtypeStruct((batch, rs, GDN_WIDTH), F32)
    wide = rows_spec(GDN_WIDTH)
    dq, dk, dv, dbg = pl.pallas_call(
        body, name="gdn_bwd",
        out_shape=(grad, grad, grad, jax.ShapeDtypeStruct((batch, rs, LANES), F32)),
        grid=(segs,),
        in_specs=[wide, wide, wide, wide, rows_spec(LANES), per_chunk(HEAD_DIM, HEAD_DIM), per_chunk(CHUNK, CHUNK)],
        out_specs=(wide, wide, wide, rows_spec(LANES)),
        scratch_shapes=[pltpu.VMEM((batch, HEADS, HEAD_DIM, HEAD_DIM), F32)],
        compiler_params=_params("arbitrary"),
    )(as_seqs(do), as_seqs(q), as_seqs(k), as_seqs(v), as_seqs(bg), states, t_invs)
    return dq.reshape(n, GDN_WIDTH), dk.reshape(n, GDN_WIDTH), dv.reshape(n, GDN_WIDTH), dbg.reshape(n, LANES)


def _lane_vec(vals, offset):
    k = vals.shape[1]
    return jnp.pad(vals, ((0, 0), (offset, LANES - offset - k)))


def _local_step(x, target, meta, norms, w_in_p, conv_qkv, a_log, dt_bias, gdn_norm, conv_sc, w_out, w_gu, w_down):
    batch, seq, d = x.shape
    tokens = N_META + seq
    pad_rows = (-tokens) % CHUNK
    rs = tokens + pad_rows
    x_offset = pad_rows + N_META
    n = batch * rs
    w_mix_pre, w_mix_post, w_ffn_pre, w_ffn_post = norms

    head = jnp.concatenate([jnp.zeros((pad_rows, d), F32), meta], axis=0)
    h0 = jnp.concatenate([jnp.broadcast_to(head[None], (batch, x_offset, d)), x], axis=1).reshape(n, d)
    target_p = jnp.pad(target, ((0, 0), (x_offset, 0), (0, 0))).reshape(n, d)
    a_log_l = _lane_vec(a_log, HEADS)
    dt_bias_l = _lane_vec(dt_bias, HEADS)

    u1 = _rms_fwd(h0, w_mix_pre, "rms_mix_pre")
    proj = _mm(u1, w_in_p, "nn", F32, "mm_proj")
    q = _qkv_fwd(proj, conv_qkv, "q", rs, pad_rows)
    k = _qkv_fwd(proj, conv_qkv, "k", rs, pad_rows)
    v = _qkv_fwd(proj, conv_qkv, "v", rs, pad_rows)
    bg = _gates_fwd(proj, a_log_l, dt_bias_l, rs, pad_rows)
    o, states, t_invs = _gdn_fwd(q, k, v, bg, rs)
    o_gated = _gate_fwd(o, proj, gdn_norm, rs)
    y_sc = _sc_fwd(proj, conv_sc, rs)
    cat = jnp.concatenate([o_gated, y_sc], axis=1)
    mix = _mm(cat, w_out, "nn", F32, "mm_mix")
    h1, u2 = _mix_residual(h0, mix, w_mix_post, w_ffn_pre)
    gu = _mm(u2, w_gu, "nn", F32, "mm_gate_up")
    act = _swiglu_act(gu)
    ffn = _mm(act, w_down, "nn", F32, "mm_down")

    dh2, dffn, d_ffn_post, sq = _loss_head(h1, ffn, w_ffn_post, target_p, rs, x_offset)
    dact = _mm(dffn, w_down, "nt", F32, "mm_dact")
    d_w_down = _mm(act, dffn, "tn", F32, "mm_dw_down")
    dgu = _swiglu_bwd(gu, dact)
    d_w_gu = _mm(u2, dgu, "tn", F32, "mm_dw_gate_up")
    du2 = _mm(dgu, w_gu, "nt", F32, "mm_du2")
    dh1, dmix, d_ffn_pre, d_mix_post = _mid_bwd(h1, mix, w_mix_post, w_ffn_pre, dh2, du2)
    dcat = _mm(dmix, w_out, "nt", F32, "mm_dcat")
    d_w_out = _mm(cat, dmix, "tn", F32, "mm_dw_out")
    do, dz, d_gdn_norm = _gate_bwd(dcat, o, proj, gdn_norm, rs)
    dscx, dscb, dscc, d_conv_sc = _sc_bwd(dcat, proj, conv_sc, rs)
    dq, dk, dv, dbg = _gdn_bwd(do, q, k, v, bg, states, t_invs, rs)
    dpq, dwq = _qkv_bwd(dq, proj, conv_qkv, "q", rs, pad_rows)
    dpk, dwk = _qkv_bwd(dk, proj, conv_qkv, "k", rs, pad_rows)
    dpv, dwv = _qkv_bwd(dv, proj, conv_qkv, "v", rs, pad_rows)
    d_conv_qkv = jnp.concatenate([dwq, dwk, dwv], axis=1)
    dba, d_a_log_l, d_dt_bias_l = _gates_bwd(proj, dbg, a_log_l, dt_bias_l, rs, pad_rows)
    dproj = jnp.concatenate([dpq, dpk, dpv, dz, dscx, dscb, dscc, dba], axis=1)
    d_w_in_p = _mm(u1, dproj, "tn", F32, "mm_dw_in")
    du1 = _mm(dproj, w_in_p, "nt", F32, "mm_du1")
    dh0, d_mix_pre = _in_bwd(h0, w_mix_pre, dh1, du1)

    dh0 = dh0.reshape(batch, rs, d)
    grads = dict(
        meta_tokens=jnp.sum(dh0[:, pad_rows:x_offset], axis=0),
        mix_pre_norm=d_mix_pre, mix_post_norm=d_mix_post, ffn_pre_norm=d_ffn_pre, ffn_post_norm=d_ffn_post,
        w_in_p=d_w_in_p, conv_qkv=d_conv_qkv,
        a_log=d_a_log_l[:, HEADS:2 * HEADS], dt_bias=d_dt_bias_l[:, HEADS:2 * HEADS],
        gdn_norm=d_gdn_norm, conv_sc=d_conv_sc, w_out=d_w_out, w_gu=d_w_gu, w_down=d_w_down,
    )
    return sq, dh0[:, x_offset:], grads


def _to_padded_in(w_in):
    lo, hi = 4 * GDN_WIDTH, 4 * GDN_WIDTH + 2 * HEADS
    pad = jnp.zeros((w_in.shape[0], IN_PAD - IN_WIDTH), w_in.dtype)
    return jnp.concatenate([w_in[:, :lo], w_in[:, hi:], w_in[:, lo:hi], pad], axis=1)


def _from_padded_in(w_in_p):
    lo, hi = 4 * GDN_WIDTH, IN_WIDTH - 2 * HEADS
    return jnp.concatenate([w_in_p[:, :lo], w_in_p[:, hi:IN_WIDTH], w_in_p[:, lo:hi]], axis=1)


HALF_ROWS = 13312
SMALL_ROWS = 48
REDUCE_ROWS = 224


def _shard_pieces(d):
    return (("w_in", (d, IN_WIDTH // N_CHIPS)), ("w_out", (D_MODEL // N_CHIPS, d)), ("w_gate", (d, D_FF // N_CHIPS)),
            ("w_up", (d, D_FF // N_CHIPS)), ("w_down", (D_FF // N_CHIPS, d)))


def _flatten(pieces, rows):
    flat = jnp.concatenate([p.reshape(-1) for p in pieces])
    return jnp.pad(flat, (0, rows * LANES - flat.shape[0])).reshape(rows, LANES)


def _unflatten(flat, shapes):
    flat = flat.reshape(-1)
    out, at = [], 0
    for shape in shapes:
        size = shape[0] * shape[1]
        out.append(flat[at:at + size].reshape(shape))
        at += size
    return out


def _hbm():
    return pl.BlockSpec(memory_space=pl.ANY)


def _place():
    x, y, c = lax.axis_index("x"), lax.axis_index("y"), lax.axis_index("c")
    chips = ((1 - x, y), (x, 1 - y), (1 - x, 1 - y))
    return x, y, c, chips


def _gather_weights(w_flat, s_flat):
    def body(w_ref, s_ref, wall_ref, sall_ref, send_sems, recv_sems, local_sem):
        x, y, c, chips = _place()
        mine = 2 * x + y
        sibling = (x, y, 1 - c)

        def big(k, src, chip, half, to):
            return pltpu.make_async_remote_copy(src_ref=src, dst_ref=wall_ref.at[chip, half], send_sem=send_sems.at[k],
                                                recv_sem=recv_sems.at[k], device_id=to, device_id_type=MESH)

        def small(k, to):
            return pltpu.make_async_remote_copy(src_ref=s_ref, dst_ref=sall_ref.at[mine], send_sem=send_sems.at[k],
                                                recv_sem=recv_sems.at[k], device_id=to, device_id_type=MESH)

        own_w = pltpu.make_async_remote_copy(src_ref=w_ref, dst_ref=wall_ref.at[mine], send_sem=send_sems.at[9],
                                             recv_sem=recv_sems.at[9], device_id=sibling, device_id_type=MESH)
        own_s = pltpu.make_async_copy(s_ref, sall_ref.at[mine], local_sem)
        own_w.start()
        own_s.start()
        first = []
        for j, (cx, cy) in enumerate(chips):
            first.append(big(j, w_ref.at[c], mine, c, (cx, cy, c)))
            first.append(small(3 + j, (cx, cy, c)))
        for cp in first:
            cp.start()
        passed = []
        for j, (cx, cy) in enumerate(chips):
            theirs = 2 * cx + cy
            big(j, w_ref.at[c], theirs, c, sibling).wait_recv()
            passed.append(big(6 + j, wall_ref.at[theirs, c], theirs, c, sibling))
            passed[j].start()
        for j, (cx, cy) in enumerate(chips):
            big(6 + j, w_ref.at[c], 2 * cx + cy, 1 - c, sibling).wait_recv()
            small(3 + j, sibling).wait_recv()
        own_w.wait_recv()
        for cp in first + passed + [own_w]:
            cp.wait_send()
        own_s.wait()

    return pl.pallas_call(
        body, name="gather_weights",
        out_shape=(jax.ShapeDtypeStruct((N_CHIPS,) + w_flat.shape, w_flat.dtype),
                   jax.ShapeDtypeStruct((N_CHIPS,) + s_flat.shape, s_flat.dtype)),
        in_specs=[_hbm(), _hbm()], out_specs=(_hbm(), _hbm()),
        scratch_shapes=[pltpu.SemaphoreType.DMA((10,)), pltpu.SemaphoreType.DMA((10,)), pltpu.SemaphoreType.DMA],
    )(w_flat, s_flat)


def _exchange_siblings(g_flat, small):
    def body(g_ref, s_ref, got_ref, sall_ref, send_sems, recv_sems, local_sem):
        x, y, c, _ = _place()
        me = 4 * x + 2 * y + c
        own = pltpu.make_async_copy(s_ref, sall_ref.at[me], local_sem)
        own.start()
        copies = []
        for k in range(7):
            dx, dy, dc = ((k + 1) >> 2) & 1, ((k + 1) >> 1) & 1, (k + 1) & 1
            peer = (1 - x if dx else x, 1 - y if dy else y, 1 - c if dc else c)
            copies.append(pltpu.make_async_remote_copy(
                src_ref=s_ref, dst_ref=sall_ref.at[me], send_sem=send_sems.at[k], recv_sem=recv_sems.at[k],
                device_id=peer, device_id_type=MESH))
        copies.append(pltpu.make_async_remote_copy(
            src_ref=g_ref.at[1 - c], dst_ref=got_ref, send_sem=send_sems.at[7], recv_sem=recv_sems.at[7],
            device_id=(x, y, 1 - c), device_id_type=MESH))
        for cp in copies:
            cp.start()
        for cp in copies:
            cp.wait_recv()
        for cp in copies:
            cp.wait_send()
        own.wait()

    return pl.pallas_call(
        body, name="exchange_siblings",
        out_shape=(jax.ShapeDtypeStruct(g_flat.shape[1:], F32), jax.ShapeDtypeStruct((8,) + small.shape, F32)),
        in_specs=[_hbm(), _hbm()], out_specs=(_hbm(), _hbm()),
        scratch_shapes=[pltpu.SemaphoreType.DMA((8,)), pltpu.SemaphoreType.DMA((8,)), pltpu.SemaphoreType.DMA],
    )(g_flat, small)


def _exchange_chips(part):
    def body(p_ref, got_ref, send_sems, recv_sems):
        x, y, c, chips = _place()
        copies = [pltpu.make_async_remote_copy(
            src_ref=p_ref.at[2 * cx + cy], dst_ref=got_ref.at[j], send_sem=send_sems.at[j], recv_sem=recv_sems.at[j],
            device_id=(cx, cy, c), device_id_type=MESH) for j, (cx, cy) in enumerate(chips)]
        for cp in copies:
            cp.start()
        for cp in copies:
            cp.wait_recv()
        for cp in copies:
            cp.wait_send()

    return pl.pallas_call(
        body, name="exchange_chips", out_shape=jax.ShapeDtypeStruct((3,) + part.shape[1:], part.dtype),
        in_specs=[_hbm()], out_specs=_hbm(),
        scratch_shapes=[pltpu.SemaphoreType.DMA((3,)), pltpu.SemaphoreType.DMA((3,))],
    )(part)


def _share_halves(halves):
    def body(h_ref, full_ref, send_sem, recv_sem):
        x, y, c, _ = _place()
        cp = pltpu.make_async_remote_copy(src_ref=h_ref.at[c], dst_ref=full_ref.at[c], send_sem=send_sem, recv_sem=recv_sem,
                                          device_id=(x, y, 1 - c), device_id_type=MESH)
        cp.start()
        cp.wait_recv()
        cp.wait_send()

    return pl.pallas_call(
        body, name="share_halves", out_shape=jax.ShapeDtypeStruct(halves.shape, halves.dtype),
        in_specs=[_hbm()], out_specs=_hbm(), input_output_aliases={0: 0},
        scratch_shapes=[pltpu.SemaphoreType.DMA, pltpu.SemaphoreType.DMA],
    )(halves)


def _flat_tile(rows):
    return _pick(rows, (3328, 1024, 512, 256, 128, 64, 32, 16))


def _add_sibling(g_flat, got, core):
    _, chips, rows, _ = g_flat.shape
    tr = _flat_tile(rows)

    def body(core_ref, g_ref, r_ref, sum_ref, send_ref):
        s = g_ref[...] + r_ref[...]
        sum_ref[...] = s
        send_ref[...] = s.astype(send_ref.dtype)

    block = pl.BlockSpec((None, tr, LANES), lambda p, i, core_ref: (p, i, 0))
    return pl.pallas_call(
        body, name="add_sibling",
        out_shape=(jax.ShapeDtypeStruct((chips, rows, LANES), F32), jax.ShapeDtypeStruct((chips, rows, LANES), BF16)),
        grid_spec=pltpu.PrefetchScalarGridSpec(
            num_scalar_prefetch=1, grid=(chips, rows // tr),
            in_specs=[pl.BlockSpec((None, None, tr, LANES), lambda p, i, core_ref: (core_ref[0], p, i, 0)), block],
            out_specs=(block, block)),
        compiler_params=_params("parallel", "parallel"),
    )(core, g_flat, got)


def _add_chips(part, got, chip_core):
    _, rows, _ = part.shape
    tr = _flat_tile(rows)

    def body(place_ref, p_ref, r_ref, o_ref):
        o_ref[...] = ((p_ref[...] + r_ref[0].astype(F32)) + r_ref[1].astype(F32)) + r_ref[2].astype(F32)

    return pl.pallas_call(
        body, name="add_chips", out_shape=jax.ShapeDtypeStruct((2, rows, LANES), F32),
        grid_spec=pltpu.PrefetchScalarGridSpec(
            num_scalar_prefetch=1, grid=(rows // tr,),
            in_specs=[pl.BlockSpec((None, tr, LANES), lambda i, place_ref: (place_ref[0], i, 0)),
                      pl.BlockSpec((3, tr, LANES), lambda i, place_ref: (0, i, 0))],
            out_specs=pl.BlockSpec((None, tr, LANES), lambda i, place_ref: (place_ref[1], i, 0))),
        compiler_params=_params("parallel"),
    )(chip_core, part, got)


def _sum_devices(small_all):
    def body(s_ref, o_ref):
        acc = s_ref[0]
        for k in range(1, 8):
            acc = acc + s_ref[k]
        o_ref[...] = acc

    return pl.pallas_call(body, name="sum_devices", out_shape=jax.ShapeDtypeStruct(small_all.shape[1:], F32))(small_all)


def _adamw(w, g, m, v, name):
    rows, cols = w.shape
    tr = _pick(rows, (256, 352, 176, 128, 64, 32, 16, 8))

    def body(w_ref, g_ref, m_ref, v_ref, d_ref, nm_ref, nv_ref):
        g = g_ref[...]
        m = ADAM_B1 * m_ref[...] + (1.0 - ADAM_B1) * g
        v = ADAM_B2 * v_ref[...] + (1.0 - ADAM_B2) * (g * g)
        m_hat = m / (1.0 - ADAM_B1 ** ADAM_STEP)
        v_hat = v / (1.0 - ADAM_B2 ** ADAM_STEP)
        d_ref[...] = -ADAM_LR * (m_hat / (jnp.sqrt(v_hat) + ADAM_EPS) + ADAM_WD * w_ref[...])
        nm_ref[...] = m
        nv_ref[...] = v

    block = pl.BlockSpec((tr, cols), lambda i: (i, 0))
    shape = jax.ShapeDtypeStruct((rows, cols), F32)
    return pl.pallas_call(
        body, name="adamw_" + name, out_shape=(shape, shape, shape), grid=(rows // tr,),
        in_specs=[block] * 4, out_specs=(block,) * 3, compiler_params=_params("parallel"),
    )(w, g, m, v)


WEIGHTS = ("meta_tokens", "mix_pre_norm", "mix_post_norm", "ffn_pre_norm", "ffn_post_norm", "w_in", "conv_qkv", "a_log",
           "dt_bias", "gdn_norm", "conv_sc", "w_out", "w_gate", "w_up", "w_down")


def kernel(x, meta_tokens, mix_pre_norm, mix_post_norm, ffn_pre_norm, ffn_post_norm, w_in, conv_qkv, a_log, dt_bias, gdn_norm, conv_sc, w_out, w_gate, w_up, w_down, loss_target, m_meta_tokens, m_mix_pre_norm, m_mix_post_norm, m_ffn_pre_norm, m_ffn_post_norm, m_w_in, m_conv_qkv, m_a_log, m_dt_bias, m_gdn_norm, m_conv_sc, m_w_out, m_w_gate, m_w_up, m_w_down, v_meta_tokens, v_mix_pre_norm, v_mix_post_norm, v_ffn_pre_norm, v_ffn_post_norm, v_w_in, v_conv_qkv, v_a_log, v_dt_bias, v_gdn_norm, v_conv_sc, v_w_out, v_w_gate, v_w_up, v_w_down):
    d = x.shape[-1]
    two_d = lambda a: a.reshape(a.shape[-2:])
    weights = dict(zip(WEIGHTS, (meta_tokens, mix_pre_norm, mix_post_norm, ffn_pre_norm, ffn_post_norm, w_in, conv_qkv, a_log,
                                 dt_bias, gdn_norm, conv_sc, w_out, w_gate, w_up, w_down)))
    m_in = dict(zip(WEIGHTS, (m_meta_tokens, m_mix_pre_norm, m_mix_post_norm, m_ffn_pre_norm, m_ffn_post_norm, m_w_in, m_conv_qkv,
                              m_a_log, m_dt_bias, m_gdn_norm, m_conv_sc, m_w_out, m_w_gate, m_w_up, m_w_down)))
    v_in = dict(zip(WEIGHTS, (v_meta_tokens, v_mix_pre_norm, v_mix_post_norm, v_ffn_pre_norm, v_ffn_post_norm, v_w_in, v_conv_qkv,
                              v_a_log, v_dt_bias, v_gdn_norm, v_conv_sc, v_w_out, v_w_gate, v_w_up, v_w_down)))
    core = lax.axis_index("c")
    chip = 2 * lax.axis_index("x") + lax.axis_index("y")
    pieces = _shard_pieces(d)
    shapes = [shape for _, shape in pieces]
    small_shapes = [two_d(weights[n]).shape for n in ("conv_qkv", "conv_sc", "meta_tokens")]

    w_flat = _flatten([two_d(weights[n]).astype(MXU_DTYPE) for n, _ in pieces], 2 * HALF_ROWS).reshape(2, HALF_ROWS, LANES)
    s_flat = _flatten([two_d(weights[n]) for n in ("conv_qkv", "conv_sc", "meta_tokens")], SMALL_ROWS)
    w_all, s_all = _gather_weights(w_flat, s_flat)
    per_chip = [_unflatten(w_all[p], shapes) for p in range(N_CHIPS)]
    full = {n: jnp.concatenate([per_chip[p][i] for p in range(N_CHIPS)], axis=0 if n in ("w_out", "w_down") else 1)
            for i, (n, _) in enumerate(pieces)}
    small_chip = [_unflatten(s_all[p], small_shapes) for p in range(N_CHIPS)]
    conv_qkv_full, conv_sc_full, meta_full = (jnp.concatenate([small_chip[p][i] for p in range(N_CHIPS)], axis=1)
                                              for i in range(3))

    sq, grad_x, g = _local_step(
        x, loss_target, meta_full, (mix_pre_norm, mix_post_norm, ffn_pre_norm, ffn_post_norm), _to_padded_in(full["w_in"]),
        conv_qkv_full, a_log, dt_bias, gdn_norm, conv_sc_full, full["w_out"], jnp.concatenate([full["w_gate"], full["w_up"]], axis=1),
        full["w_down"])

    d_w_in = _from_padded_in(g["w_in_p"])
    by_chip = []
    for p in range(N_CHIPS):
        cols = lambda a, width: a[:, p * width:(p + 1) * width]
        rows = lambda a, height: a[p * height:(p + 1) * height]
        by_chip.append(_flatten([cols(d_w_in, shapes[0][1]), rows(g["w_out"], shapes[1][0]), cols(g["w_gu"][:, :D_FF], shapes[2][1]),
                                 cols(g["w_gu"][:, D_FF:], shapes[3][1]), rows(g["w_down"], shapes[4][0])], 2 * HALF_ROWS))
    g_flat = jnp.stack(by_chip).reshape(N_CHIPS, 2, HALF_ROWS, LANES).transpose(1, 0, 2, 3)
    scalars = jnp.concatenate([g["a_log"], g["dt_bias"], sq], axis=1)
    small = _flatten([g["mix_pre_norm"], g["mix_post_norm"], g["ffn_pre_norm"], g["ffn_post_norm"],
                      jnp.pad(scalars, ((0, 0), (0, LANES - scalars.shape[1]))), g["gdn_norm"], g["conv_qkv"], g["conv_sc"],
                      g["meta_tokens"]], REDUCE_ROWS)
    got_sibling, small_all = _exchange_siblings(g_flat, small)
    part, part_send = _add_sibling(g_flat, got_sibling, core.reshape(1).astype(jnp.int32))
    got_chips = _exchange_chips(part_send)
    g_shard = _share_halves(_add_chips(part, got_chips, jnp.stack([chip, core]).astype(jnp.int32)))
    reduced = _sum_devices(small_all)

    grads = dict(zip([n for n, _ in pieces], _unflatten(g_shard, shapes)))
    r = reduced.reshape(-1)
    at = 0
    for n in ("mix_pre_norm", "mix_post_norm", "ffn_pre_norm", "ffn_post_norm"):
        grads[n] = r[at:at + d].reshape(1, d)
        at += d
    grads["a_log"] = r[at:at + HEADS].reshape(1, HEADS)
    grads["dt_bias"] = r[at + HEADS:at + 2 * HEADS].reshape(1, HEADS)
    loss = (0.5 / d) * r[at + 2 * HEADS]
    at += LANES
    grads["gdn_norm"] = r[at:at + HEAD_DIM].reshape(1, HEAD_DIM)
    at += HEAD_DIM
    for n, shape in (("conv_qkv", (GDN_CONV, 3 * GDN_WIDTH)), ("conv_sc", (SC_CONV, SC_WIDTH)), ("meta_tokens", (N_META, d))):
        full_grad = r[at:at + shape[0] * shape[1]].reshape(shape)
        at += shape[0] * shape[1]
        width = shape[1] // N_CHIPS
        grads[n] = lax.dynamic_slice_in_dim(full_grad, chip * width, width, axis=1)

    out_g, out_d, out_m, out_v = [], [], [], []
    for n in WEIGHTS:
        shape = weights[n].shape
        delta, new_m, new_v = _adamw(two_d(weights[n]), grads[n], two_d(m_in[n]), two_d(v_in[n]), n)
        out_g.append(grads[n].reshape(shape))
        out_d.append(delta.reshape(shape))
        out_m.append(new_m.reshape(shape))
        out_v.append(new_v.reshape(shape))
    return (loss, grad_x, *out_g, *out_d, *out_m, *out_v)
```

```python
import functools

import jax
import jax.numpy as jnp
from jax import lax
from jax.experimental import pallas as pl
from jax.experimental.pallas import tpu as pltpu

F32 = jnp.float32
BF16 = jnp.bfloat16
MXU_DTYPE = jnp.bfloat16
MESH = pl.DeviceIdType.MESH

D_MODEL = 1024
N_META = 16
HEADS = 4
HEAD_DIM = 128
GDN_WIDTH = HEADS * HEAD_DIM
GDN_CONV = 4
CHUNK = 64
SC_WIDTH = D_MODEL - GDN_WIDTH
SC_CONV = 3
D_FF = 2816
IN_WIDTH = 4 * GDN_WIDTH + 2 * HEADS + 3 * SC_WIDTH
IN_PAD = 3840
BA_COL = (4 * GDN_WIDTH + 3 * SC_WIDTH) // 128
EPS = 1e-6
LANES = 128
N_CHIPS = 4
VMEM_LIMIT = 48 * 2 ** 20

ADAM_LR = 0.001
ADAM_B1 = 0.9
ADAM_B2 = 0.999
ADAM_EPS = 1e-08
ADAM_WD = 0.01
ADAM_STEP = 10


def _pick(n, candidates):
    for c in candidates:
        if n % c == 0:
            return c
    return n


def _row_tile(n):
    return _pick(n, (352, 256, 176, 128, 64, 32, 16, 8))


def _params(*sem):
    return pltpu.CompilerParams(dimension_semantics=sem, vmem_limit_bytes=VMEM_LIMIT)


def _sigmoid(x):
    return 1.0 / (1.0 + jnp.exp(-x))


def _softplus(x):
    return jnp.maximum(x, 0.0) + jnp.log(1.0 + jnp.exp(-jnp.abs(x)))


def _dsilu(x, s):
    return s * (1.0 + x * (1.0 - s))


def _mm(a, b, mode, out_dtype, name, init=None):
    if mode == "tn":
        k_dim, m_dim = a.shape
    else:
        m_dim, k_dim = a.shape
    n_dim = b.shape[0] if mode == "nt" else b.shape[1]
    tm = _pick(m_dim, (1408, 1024, 512, 256, 128) if mode == "tn" else (1056, 1024, 704, 512, 256, 128))
    tn = _pick(n_dim, (1408, 1280, 1024, 768, 512, 256, 128))
    tk = _pick(k_dim, (1408, 1280, 1056, 1024, 512, 256, 128))
    nk = k_dim // tk
    if mode == "nn":
        a_spec = pl.BlockSpec((tm, tk), lambda i, j, k: (i, k))
        b_spec = pl.BlockSpec((tk, tn), lambda i, j, k: (k, j))
        dims = (((1,), (0,)), ((), ()))
    elif mode == "nt":
        a_spec = pl.BlockSpec((tm, tk), lambda i, j, k: (i, k))
        b_spec = pl.BlockSpec((tn, tk), lambda i, j, k: (j, k))
        dims = (((1,), (1,)), ((), ()))
    else:
        a_spec = pl.BlockSpec((tk, tm), lambda i, j, k: (k, i))
        b_spec = pl.BlockSpec((tk, tn), lambda i, j, k: (k, j))
        dims = (((0,), (0,)), ((), ()))

    out_spec = pl.BlockSpec((tm, tn), lambda i, j, k: (i, j))

    def body(a_ref, b_ref, *rest):
        o_ref, acc_ref = rest[-2:]
        k = pl.program_id(2)
        p = lax.dot_general(a_ref[...], b_ref[...], dims, preferred_element_type=F32)

        @pl.when(k == 0)
        def _():
            acc_ref[...] = p if init is None else rest[0][...] + p

        @pl.when(k > 0)
        def _():
            acc_ref[...] += p

        @pl.when(k == nk - 1)
        def _():
            o_ref[...] = acc_ref[...].astype(out_dtype)

    return pl.pallas_call(
        body, name=name,
        out_shape=jax.ShapeDtypeStruct((m_dim, n_dim), out_dtype),
        grid=(m_dim // tm, n_dim // tn, nk),
        in_specs=[a_spec, b_spec] + ([] if init is None else [out_spec]),
        out_specs=out_spec,
        scratch_shapes=[pltpu.VMEM((tm, tn), F32)],
        compiler_params=_params("parallel", "parallel", "arbitrary"),
    )(*((a, b) if init is None else (a, b, init)))


def _rms_apply(x, w):
    r = lax.rsqrt(jnp.mean(x * x, axis=-1, keepdims=True) + EPS)
    return x * r * w


def _rms_bwd(x, w, dy):
    r = lax.rsqrt(jnp.mean(x * x, axis=-1, keepdims=True) + EPS)
    xh = x * r
    dyw = dy * w
    dx = r * (dyw - xh * jnp.mean(dyw * xh, axis=-1, keepdims=True))
    return dx, jnp.sum(dy * xh, axis=0, keepdims=True)


def _accumulate(ref, first, value):
    @pl.when(first)
    def _():
        ref[...] = value

    @pl.when(jnp.logical_not(first))
    def _():
        ref[...] += value


def _rows(tr, width):
    return pl.BlockSpec((tr, width), lambda i: (i, 0))


def _vec(width):
    return pl.BlockSpec((1, width), lambda i: (0, 0))


def _rms_fwd(h, w, name):
    n, d = h.shape
    tr = _row_tile(n)

    def body(h_ref, w_ref, u_ref):
        u_ref[...] = _rms_apply(h_ref[...], w_ref[...]).astype(u_ref.dtype)

    return pl.pallas_call(
        body, name=name, out_shape=jax.ShapeDtypeStruct((n, d), MXU_DTYPE), grid=(n // tr,),
        in_specs=[_rows(tr, d), _vec(d)], out_specs=_rows(tr, d), compiler_params=_params("parallel"),
    )(h, w)


def _mix_residual(h0, mix, w_post, w_pre):
    n, d = h0.shape
    tr = _row_tile(n)

    def body(h0_ref, mix_ref, wpost_ref, wpre_ref, h1_ref, u2_ref):
        h1 = h0_ref[...] + _rms_apply(mix_ref[...], wpost_ref[...])
        h1_ref[...] = h1
        u2_ref[...] = _rms_apply(h1, wpre_ref[...]).astype(u2_ref.dtype)

    return pl.pallas_call(
        body, name="mix_residual",
        out_shape=(jax.ShapeDtypeStruct((n, d), F32), jax.ShapeDtypeStruct((n, d), MXU_DTYPE)), grid=(n // tr,),
        in_specs=[_rows(tr, d), _rows(tr, d), _vec(d), _vec(d)], out_specs=(_rows(tr, d), _rows(tr, d)),
        compiler_params=_params("parallel"),
    )(h0, mix, w_post, w_pre)


def _swiglu_act(gate, up):
    n = gate.shape[0]
    tr = _pick(n, (176, 128, 64, 32, 16, 8))

    def body(g_ref, u_ref, act_ref):
        g = g_ref[...]
        act_ref[...] = (g * _sigmoid(g) * u_ref[...]).astype(act_ref.dtype)

    return pl.pallas_call(
        body, name="swiglu_act", out_shape=jax.ShapeDtypeStruct((n, D_FF), MXU_DTYPE), grid=(n // tr,),
        in_specs=[_rows(tr, D_FF), _rows(tr, D_FF)], out_specs=_rows(tr, D_FF), compiler_params=_params("parallel"),
    )(gate, up)


def _swiglu_bwd(gate, up, dact):
    n = gate.shape[0]
    tr = _pick(n, (176, 128, 64, 32, 16, 8))

    def body(g_ref, u_ref, dact_ref, dg_ref, du_ref):
        g = g_ref[...]
        s = _sigmoid(g)
        da = dact_ref[...]
        dg_ref[...] = (da * u_ref[...] * _dsilu(g, s)).astype(dg_ref.dtype)
        du_ref[...] = (da * g * s).astype(du_ref.dtype)

    shape = jax.ShapeDtypeStruct((n, D_FF), MXU_DTYPE)
    return pl.pallas_call(
        body, name="swiglu_bwd", out_shape=(shape, shape), grid=(n // tr,),
        in_specs=[_rows(tr, D_FF)] * 3, out_specs=(_rows(tr, D_FF), _rows(tr, D_FF)),
        compiler_params=_params("parallel"),
    )(gate, up, dact)


def _loss_head(h1, ffn, w_post, target, rows_per_seq, x_offset):
    n, d = h1.shape
    tr = _row_tile(rows_per_seq)
    tiles_per_seq = rows_per_seq // tr

    def body(h1_ref, ffn_ref, w_ref, t_ref, dh2_ref, dffn_ref, dw_ref, sq_ref):
        i = pl.program_id(0)
        w = w_ref[...]
        f = ffn_ref[...]
        r = lax.rsqrt(jnp.mean(f * f, axis=-1, keepdims=True) + EPS)
        fh = f * r
        row = lax.rem(i, tiles_per_seq) * tr + lax.broadcasted_iota(jnp.int32, (tr, 1), 0)
        err = jnp.where(row >= x_offset, h1_ref[...] + fh * w - t_ref[...], 0.0)
        dh2 = err * (1.0 / d)
        dh2_ref[...] = dh2
        dyw = dh2 * w
        dffn_ref[...] = (r * (dyw - fh * jnp.mean(dyw * fh, axis=-1, keepdims=True))).astype(dffn_ref.dtype)
        _accumulate(dw_ref, i == 0, jnp.sum(dh2 * fh, axis=0, keepdims=True))
        _accumulate(sq_ref, i == 0, jnp.sum(jnp.sum(err * err, axis=1, keepdims=True), axis=0, keepdims=True))

    return pl.pallas_call(
        body, name="loss_head",
        out_shape=(jax.ShapeDtypeStruct((n, d), F32), jax.ShapeDtypeStruct((n, d), MXU_DTYPE),
                   jax.ShapeDtypeStruct((1, d), F32), jax.ShapeDtypeStruct((1, 1), F32)),
        grid=(n // tr,),
        in_specs=[_rows(tr, d), _rows(tr, d), _vec(d), _rows(tr, d)],
        out_specs=(_rows(tr, d), _rows(tr, d), _vec(d), _vec(1)),
        compiler_params=_params("arbitrary"),
    )(h1, ffn, w_post, target)


def _mid_bwd(h1, mix, w_mix_post, w_ffn_pre, dh2, du2):
    n, d = h1.shape
    tr = _row_tile(n)

    def body(h1_ref, mix_ref, wpost_ref, wpre_ref, dh2_ref, du2_ref, dh1_ref, dmix_ref, dwpre_ref, dwpost_ref):
        i = pl.program_id(0)
        dx, dwpre = _rms_bwd(h1_ref[...], wpre_ref[...], du2_ref[...])
        dh1 = dh2_ref[...] + dx
        dh1_ref[...] = dh1
        dmix, dwpost = _rms_bwd(mix_ref[...], wpost_ref[...], dh1)
        dmix_ref[...] = dmix.astype(dmix_ref.dtype)
        _accumulate(dwpre_ref, i == 0, dwpre)
        _accumulate(dwpost_ref, i == 0, dwpost)

    return pl.pallas_call(
        body, name="mid_bwd",
        out_shape=(jax.ShapeDtypeStruct((n, d), F32), jax.ShapeDtypeStruct((n, d), MXU_DTYPE),
                   jax.ShapeDtypeStruct((1, d), F32), jax.ShapeDtypeStruct((1, d), F32)),
        grid=(n // tr,),
        in_specs=[_rows(tr, d), _rows(tr, d), _vec(d), _vec(d), _rows(tr, d), _rows(tr, d)],
        out_specs=(_rows(tr, d), _rows(tr, d), _vec(d), _vec(d)),
        compiler_params=_params("arbitrary"),
    )(h1, mix, w_mix_post, w_ffn_pre, dh2, du2)


def _in_bwd(h0, w_pre, dh1, du1):
    n, d = h0.shape
    tr = _row_tile(n)

    def body(h0_ref, w_ref, dh1_ref, du1_ref, dh0_ref, dw_ref):
        dx, dw = _rms_bwd(h0_ref[...], w_ref[...], du1_ref[...])
        dh0_ref[...] = dh1_ref[...] + dx
        _accumulate(dw_ref, pl.program_id(0) == 0, dw)

    return pl.pallas_call(
        body, name="in_bwd",
        out_shape=(jax.ShapeDtypeStruct((n, d), F32), jax.ShapeDtypeStruct((1, d), F32)), grid=(n // tr,),
        in_specs=[_rows(tr, d), _vec(d), _rows(tr, d), _rows(tr, d)], out_specs=(_rows(tr, d), _vec(d)),
        compiler_params=_params("arbitrary"),
    )(h0, w_pre, dh1, du1)


def _lane_is(lo, hi):
    lane = lax.broadcasted_iota(jnp.int32, (1, LANES), 1)
    return jnp.logical_and(lane >= lo, lane < hi)


def _gates_fwd(proj, a_log_l, dt_bias_l, rows_per_seq, pad_rows):
    n = proj.shape[0]
    tr = _row_tile(rows_per_seq)
    tiles_per_seq = rows_per_seq // tr

    def body(p_ref, a_ref, dt_ref, o_ref):
        x = p_ref[...]
        row = lax.rem(pl.program_id(0), tiles_per_seq) * tr + lax.broadcasted_iota(jnp.int32, (tr, 1), 0)
        g = -jnp.exp(a_ref[...]) * _softplus(x + dt_ref[...])
        val = jnp.where(_lane_is(0, HEADS), _sigmoid(x), jnp.where(_lane_is(HEADS, 2 * HEADS), g, 0.0))
        o_ref[...] = jnp.where(row >= pad_rows, val, 0.0)

    return pl.pallas_call(
        body, name="gates_fwd", out_shape=jax.ShapeDtypeStruct((n, LANES), F32), grid=(n // tr,),
        in_specs=[pl.BlockSpec((tr, LANES), lambda i: (i, BA_COL)), _vec(LANES), _vec(LANES)],
        out_specs=_rows(tr, LANES), compiler_params=_params("parallel"),
    )(proj, a_log_l, dt_bias_l)


def _gates_bwd(proj, dbg, a_log_l, dt_bias_l, rows_per_seq, pad_rows):
    n = proj.shape[0]
    tr = _row_tile(rows_per_seq)
    tiles_per_seq = rows_per_seq // tr

    def body(p_ref, d_ref, a_ref, dt_ref, dx_ref, da_ref, ddt_ref):
        i = pl.program_id(0)
        x = p_ref[...]
        d = d_ref[...]
        row = lax.rem(i, tiles_per_seq) * tr + lax.broadcasted_iota(jnp.int32, (tr, 1), 0)
        live = row >= pad_rows
        beta = _sigmoid(x)
        ea = jnp.exp(a_ref[...])
        xa = x + dt_ref[...]
        g = -ea * _softplus(xa)
        is_g = _lane_is(HEADS, 2 * HEADS)
        d_alogit = jnp.where(jnp.logical_and(live, is_g), d * (-ea) * _sigmoid(xa), 0.0)
        d_blogit = jnp.where(jnp.logical_and(live, _lane_is(0, HEADS)), d * beta * (1.0 - beta), 0.0)
        dx_ref[:, :LANES] = (d_alogit + d_blogit).astype(dx_ref.dtype)
        dx_ref[:, LANES:] = jnp.zeros((tr, LANES), dx_ref.dtype)
        _accumulate(da_ref, i == 0, jnp.sum(jnp.where(jnp.logical_and(live, is_g), d * g, 0.0), axis=0, keepdims=True))
        _accumulate(ddt_ref, i == 0, jnp.sum(d_alogit, axis=0, keepdims=True))

    return pl.pallas_call(
        body, name="gates_bwd",
        out_shape=(jax.ShapeDtypeStruct((n, 2 * LANES), MXU_DTYPE), jax.ShapeDtypeStruct((1, LANES), F32),
                   jax.ShapeDtypeStruct((1, LANES), F32)),
        grid=(n // tr,),
        in_specs=[pl.BlockSpec((tr, LANES), lambda i: (i, BA_COL)), _rows(tr, LANES), _vec(LANES), _vec(LANES)],
        out_specs=(_rows(tr, 2 * LANES), _vec(LANES), _vec(LANES)),
        compiler_params=_params("arbitrary"),
    )(proj, dbg, a_log_l, dt_bias_l)


def _shift_down(x, k):
    return x if k == 0 else pltpu.roll(x, k, 0)


def _shift_up(x, k):
    return x if k == 0 else pltpu.roll(x, x.shape[0] - k, 0)


def _causal_conv(x, w, width):
    acc = w[width - 1:width, :] * x
    for i in range(width - 1):
        acc = acc + w[i:i + 1, :] * _shift_down(x, width - 1 - i)
    return acc


def _seq_head(rs, col0):
    return pl.BlockSpec((rs, LANES), lambda j, b: (b, col0 + j))


def _live_rows(rs, pad_rows):
    return lax.broadcasted_iota(jnp.int32, (rs, 1), 0) >= pad_rows


def _qkv_fwd(proj, conv_w, kind, rs, pad_rows):
    n = proj.shape[0]
    col0 = {"q": 0, "k": HEADS, "v": 2 * HEADS}[kind]

    def body(p_ref, w_ref, o_ref):
        c = _causal_conv(p_ref[...], w_ref[...], GDN_CONV)
        s = c * _sigmoid(c)
        if kind != "v":
            s = s * lax.rsqrt(jnp.sum(s * s, axis=-1, keepdims=True) + EPS)
        if kind == "q":
            s = s * (HEAD_DIM ** -0.5)
        o_ref[...] = jnp.where(_live_rows(rs, pad_rows), s, 0.0)

    return pl.pallas_call(
        body, name="qkv_fwd_" + kind, out_shape=jax.ShapeDtypeStruct((n, GDN_WIDTH), F32), grid=(HEADS, n // rs),
        in_specs=[_seq_head(rs, col0), pl.BlockSpec((GDN_CONV, LANES), lambda j, b: (0, col0 + j))],
        out_specs=_seq_head(rs, 0), compiler_params=_params("parallel", "parallel"),
    )(proj, conv_w)


def _qkv_bwd(dy, proj, conv_w, kind, rs, pad_rows):
    n = proj.shape[0]
    col0 = {"q": 0, "k": HEADS, "v": 2 * HEADS}[kind]

    def body(dy_ref, p_ref, w_ref, dp_ref, dw_ref):
        pre = p_ref[...]
        w = w_ref[...]
        c = _causal_conv(pre, w, GDN_CONV)
        sg = _sigmoid(c)
        s = c * sg
        ds = dy_ref[...]
        if kind == "q":
            ds = ds * (HEAD_DIM ** -0.5)
        if kind != "v":
            r = lax.rsqrt(jnp.sum(s * s, axis=-1, keepdims=True) + EPS)
            sh = s * r
            ds = r * (ds - sh * jnp.sum(ds * sh, axis=-1, keepdims=True))
        dc = jnp.where(_live_rows(rs, pad_rows), ds * _dsilu(c, sg), 0.0)
        dpre = w[GDN_CONV - 1:GDN_CONV, :] * dc
        for i in range(GDN_CONV - 1):
            dpre = dpre + w[i:i + 1, :] * _shift_up(dc, GDN_CONV - 1 - i)
        dp_ref[...] = dpre.astype(dp_ref.dtype)
        dw = jnp.concatenate(
            [jnp.sum(dc * _shift_down(pre, GDN_CONV - 1 - i), axis=0, keepdims=True) for i in range(GDN_CONV)], axis=0)
        _accumulate(dw_ref, pl.program_id(1) == 0, dw)

    return pl.pallas_call(
        body, name="qkv_bwd_" + kind,
        out_shape=(jax.ShapeDtypeStruct((n, GDN_WIDTH), MXU_DTYPE), jax.ShapeDtypeStruct((GDN_CONV, GDN_WIDTH), F32)),
        grid=(HEADS, n // rs),
        in_specs=[_seq_head(rs, 0), _seq_head(rs, col0), pl.BlockSpec((GDN_CONV, LANES), lambda j, b: (0, col0 + j))],
        out_specs=(_seq_head(rs, 0), pl.BlockSpec((GDN_CONV, LANES), lambda j, b: (0, j))),
        compiler_params=_params("parallel", "arbitrary"),
    )(dy, proj, conv_w)


SC_COL = 4 * HEADS


def _sc_fwd(proj, conv_w, rs):
    n = proj.shape[0]

    def body(x_ref, b_ref, c_ref, w_ref, y_ref):
        y_ref[...] = (b_ref[...] * _causal_conv(c_ref[...] * x_ref[...], w_ref[...], SC_CONV)).astype(y_ref.dtype)

    return pl.pallas_call(
        body, name="sc_fwd", out_shape=jax.ShapeDtypeStruct((n, SC_WIDTH), MXU_DTYPE), grid=(HEADS, n // rs),
        in_specs=[_seq_head(rs, SC_COL), _seq_head(rs, SC_COL + 4), _seq_head(rs, SC_COL + 8),
                  pl.BlockSpec((SC_CONV, LANES), lambda j, b: (0, j))],
        out_specs=_seq_head(rs, 0), compiler_params=_params("parallel", "parallel"),
    )(proj, proj, proj, conv_w)


def _sc_bwd(dcat, proj, conv_w, rs):
    n = proj.shape[0]

    def body(dy_ref, x_ref, b_ref, c_ref, w_ref, dx_ref, db_ref, dc_ref, dw_ref):
        w = w_ref[...]
        x = x_ref[...]
        cc = c_ref[...]
        u = cc * x
        dy = dy_ref[...]
        db_ref[...] = (dy * _causal_conv(u, w, SC_CONV)).astype(db_ref.dtype)
        dcv = dy * b_ref[...]
        du = w[SC_CONV - 1:SC_CONV, :] * dcv
        for i in range(SC_CONV - 1):
            du = du + w[i:i + 1, :] * _shift_up(dcv, SC_CONV - 1 - i)
        dx_ref[...] = (du * cc).astype(dx_ref.dtype)
        dc_ref[...] = (du * x).astype(dc_ref.dtype)
        dw = jnp.concatenate(
            [jnp.sum(dcv * _shift_down(u, SC_CONV - 1 - i), axis=0, keepdims=True) for i in range(SC_CONV)], axis=0)
        _accumulate(dw_ref, pl.program_id(1) == 0, dw)

    piece = jax.ShapeDtypeStruct((n, SC_WIDTH), MXU_DTYPE)
    return pl.pallas_call(
        body, name="sc_bwd", out_shape=(piece, piece, piece, jax.ShapeDtypeStruct((SC_CONV, SC_WIDTH), F32)),
        grid=(HEADS, n // rs),
        in_specs=[_seq_head(rs, HEADS), _seq_head(rs, SC_COL), _seq_head(rs, SC_COL + 4), _seq_head(rs, SC_COL + 8),
                  pl.BlockSpec((SC_CONV, LANES), lambda j, b: (0, j))],
        out_specs=(_seq_head(rs, 0), _seq_head(rs, 0), _seq_head(rs, 0),
                   pl.BlockSpec((SC_CONV, LANES), lambda j, b: (0, j))),
        compiler_params=_params("parallel", "arbitrary"),
    )(dcat, proj, proj, proj, conv_w)


Z_COL = 3 * HEADS


def _gate_fwd(o, proj, gdn_norm, rs):
    n = proj.shape[0]

    def body(o_ref, z_ref, w_ref, y_ref):
        z = z_ref[...]
        y_ref[...] = (_rms_apply(o_ref[...], w_ref[...]) * z * _sigmoid(z)).astype(y_ref.dtype)

    return pl.pallas_call(
        body, name="gate_fwd", out_shape=jax.ShapeDtypeStruct((n, GDN_WIDTH), MXU_DTYPE), grid=(HEADS, n // rs),
        in_specs=[_seq_head(rs, 0), _seq_head(rs, Z_COL), pl.BlockSpec((1, LANES), lambda j, b: (0, 0))],
        out_specs=_seq_head(rs, 0), compiler_params=_params("parallel", "parallel"),
    )(o, proj, gdn_norm)


def _gate_bwd(dcat, o, proj, gdn_norm, rs):
    n = proj.shape[0]

    def body(dy_ref, o_ref, z_ref, w_ref, do_ref, dz_ref, dw_ref):
        z = z_ref[...]
        w = w_ref[...]
        o = o_ref[...]
        dy = dy_ref[...]
        s = _sigmoid(z)
        dz_ref[...] = (dy * _rms_apply(o, w) * _dsilu(z, s)).astype(dz_ref.dtype)
        do, dw = _rms_bwd(o, w, dy * z * s)
        do_ref[...] = do
        _accumulate(dw_ref, jnp.logical_and(pl.program_id(0) == 0, pl.program_id(1) == 0), dw)

    return pl.pallas_call(
        body, name="gate_bwd",
        out_shape=(jax.ShapeDtypeStruct((n, GDN_WIDTH), F32), jax.ShapeDtypeStruct((n, GDN_WIDTH), MXU_DTYPE),
                   jax.ShapeDtypeStruct((1, LANES), F32)),
        grid=(HEADS, n // rs),
        in_specs=[_seq_head(rs, 0), _seq_head(rs, 0), _seq_head(rs, Z_COL), pl.BlockSpec((1, LANES), lambda j, b: (0, 0))],
        out_specs=(_seq_head(rs, 0), _seq_head(rs, 0), pl.BlockSpec((1, LANES), lambda j, b: (0, 0))),
        compiler_params=_params("arbitrary", "arbitrary"),
    )(dcat, o, proj, gdn_norm)


def _dot(a, b):
    return jnp.dot(a.astype(MXU_DTYPE), b.astype(MXU_DTYPE), preferred_element_type=F32)


def _dot_nt(a, b):
    return lax.dot_general(a.astype(MXU_DTYPE), b.astype(MXU_DTYPE), (((1,), (1,)), ((), ())),
                           preferred_element_type=F32)


def _dot_tn(a, b):
    return lax.dot_general(a.astype(MXU_DTYPE), b.astype(MXU_DTYPE), (((0,), (0,)), ((), ())),
                           preferred_element_type=F32)


def _split(x):
    hi = x.astype(MXU_DTYPE)
    return hi, (x - hi.astype(F32)).astype(MXU_DTYPE)


def _dot_split(a, b):
    mm = functools.partial(jnp.dot, preferred_element_type=F32)
    return mm(a[0], b[0]) + (mm(a[0], b[1]) + mm(a[1], b[0]))


def _unit_lower_inverses(mats, eye):
    inv = [eye - a for a in mats]
    power = [_split(a) for a in mats]
    span = 2
    while span < CHUNK:
        power = [_split(_dot_split(p, p)) for p in power]
        inv = [i + _dot_split(_split(i), p) for i, p in zip(inv, power)]
        span *= 2
    return inv


def _chunk_masks():
    ii = lax.broadcasted_iota(jnp.int32, (CHUNK, CHUNK), 0)
    jj = lax.broadcasted_iota(jnp.int32, (CHUNK, CHUNK), 1)
    return ii, jj


def _chunk_decay(g_col, ii, jj):
    incl = ii >= jj
    g_row = jnp.sum(jnp.where(ii == jj, g_col, 0.0), axis=0, keepdims=True)
    gc_col = jnp.sum(jnp.where(incl, g_row, 0.0), axis=1, keepdims=True)
    gc_row = jnp.sum(jnp.where(ii <= jj, g_col, 0.0), axis=0, keepdims=True)
    g_total = jnp.sum(g_row, axis=1, keepdims=True)
    decay = jnp.where(incl, jnp.exp(jnp.where(incl, gc_col - gc_row, 0.0)), 0.0)
    return gc_col, g_total, decay


def _gdn_segments(rs, candidates):
    chunks = rs // CHUNK
    seg_chunks = _pick(chunks, candidates)
    return chunks, seg_chunks, chunks // seg_chunks


def _head_lanes(h):
    return slice(h * HEAD_DIM, (h + 1) * HEAD_DIM)


def _gdn_fwd(q, k, v, bg, rs):
    n = q.shape[0]
    batch = n // rs
    chunks, seg_chunks, segs = _gdn_segments(rs, (11, 8, 4, 2))
    seg_rows = seg_chunks * CHUNK
    chains = [(b, h) for b in range(batch) for h in range(HEADS)]
    each = lambda f, *lists: [f(*args) for args in zip(*lists)]

    def body(q_ref, k_ref, v_ref, bg_ref, o_ref, s_ref, t_ref, state_ref):
        @pl.when(pl.program_id(0) == 0)
        def _():
            state_ref[...] = jnp.zeros_like(state_ref)

        ii, jj = _chunk_masks()
        incl = ii >= jj
        eye = (ii == jj).astype(F32)

        def chunk(c, carry):
            rows = pl.ds(pl.multiple_of(c * CHUNK, CHUNK), CHUNK)
            bgc = [bg_ref[b, rows, :] for b in range(batch)]
            qc = [q_ref[b, rows, _head_lanes(h)] for b, h in chains]
            kc = [k_ref[b, rows, _head_lanes(h)] for b, h in chains]
            vc = [v_ref[b, rows, _head_lanes(h)] for b, h in chains]
            beta = [bgc[b][:, h:h + 1] for b, h in chains]
            state = [state_ref[b, h] for b, h in chains]
            dec = [_chunk_decay(bgc[b][:, HEADS + h:HEADS + h + 1], ii, jj) for b, h in chains]
            gc_col, g_total, decay = ([d[i] for d in dec] for i in range(3))
            kb = each(lambda x, y: x * y, kc, beta)
            a = each(lambda x, y, d: jnp.where(ii > jj, _dot_nt(x, y) * d, 0.0), kb, kc, decay)
            t_inv = _unit_lower_inverses(a, eye)
            eg = [jnp.exp(g) for g in gc_col]
            u = each(lambda t, x, y: _dot(t, x * y), t_inv, vc, beta)
            w = each(lambda t, x, e: _dot(t, x * e), t_inv, kb, eg)
            qk = each(lambda x, y, d: jnp.where(incl, _dot_nt(x, y) * d, 0.0), qc, kc, decay)
            v_new = each(lambda x, y, s: x - _dot(y, s), u, w, state)
            o = each(lambda x, e, s, m, vn: _dot(x * e, s) + _dot(m, vn), qc, eg, state, qk, v_new)
            new_state = each(lambda s, gt, x, g, vn: s * jnp.exp(gt) + _dot_tn(x * jnp.exp(gt - g), vn),
                             state, g_total, kc, gc_col, v_new)
            for i, (b, h) in enumerate(chains):
                s_ref[b, h, c] = state[i]
                t_ref[b, h, c] = t_inv[i]
                o_ref[b, rows, _head_lanes(h)] = o[i]
                state_ref[b, h] = new_state[i]
            return carry

        lax.fori_loop(0, seg_chunks, chunk, 0)

    rows_spec = lambda width: pl.BlockSpec((batch, seg_rows, width), lambda s: (0, s, 0))
    per_chunk = lambda r, c: pl.BlockSpec((batch, HEADS, seg_chunks, r, c), lambda s: (0, 0, s, 0, 0))
    as_seqs = lambda a: a.reshape(batch, rs, a.shape[-1])
    o, states, t_invs = pl.pallas_call(
        body, name="gdn_fwd",
        out_shape=(jax.ShapeDtypeStruct((batch, rs, GDN_WIDTH), F32),
                   jax.ShapeDtypeStruct((batch, HEADS, chunks, HEAD_DIM, HEAD_DIM), F32),
                   jax.ShapeDtypeStruct((batch, HEADS, chunks, CHUNK, CHUNK), F32)),
        grid=(segs,),
        in_specs=[rows_spec(GDN_WIDTH), rows_spec(GDN_WIDTH), rows_spec(GDN_WIDTH), rows_spec(LANES)],
        out_specs=(rows_spec(GDN_WIDTH), per_chunk(HEAD_DIM, HEAD_DIM), per_chunk(CHUNK, CHUNK)),
        scratch_shapes=[pltpu.VMEM((batch, HEADS, HEAD_DIM, HEAD_DIM), F32)],
        compiler_params=_params("arbitrary"),
    )(as_seqs(q), as_seqs(k), as_seqs(v), as_seqs(bg))
    return o.reshape(n, GDN_WIDTH), states, t_invs


def _gdn_bwd(do, q, k, v, bg, states, t_invs, rs):
    n = q.shape[0]
    batch = n // rs
    chunks, seg_chunks, segs = _gdn_segments(rs, (3, 4, 2))
    seg_rows = seg_chunks * CHUNK
    chains = [(b, h) for b in range(batch) for h in range(HEADS)]
    each = lambda f, *lists: [f(*args) for args in zip(*lists)]

    def body(do_ref, q_ref, k_ref, v_ref, bg_ref, s_ref, t_ref, dq_ref, dk_ref, dv_ref, dbg_ref, dstate_ref):
        @pl.when(pl.program_id(0) == 0)
        def _():
            dstate_ref[...] = jnp.zeros_like(dstate_ref)

        ii, jj = _chunk_masks()
        incl = ii >= jj
        strict = ii > jj
        lane = lax.broadcasted_iota(jnp.int32, (1, LANES), 1)

        def rowsum(x):
            return jnp.sum(x, axis=1, keepdims=True)

        def total(x):
            return jnp.sum(rowsum(x), axis=0, keepdims=True)

        def chunk(step, carry):
            c = seg_chunks - 1 - step
            rows = pl.ds(pl.multiple_of(c * CHUNK, CHUNK), CHUNK)
            bgc = [bg_ref[b, rows, :] for b in range(batch)]
            qc = [q_ref[b, rows, _head_lanes(h)] for b, h in chains]
            kc = [k_ref[b, rows, _head_lanes(h)] for b, h in chains]
            vc = [v_ref[b, rows, _head_lanes(h)] for b, h in chains]
            doc = [do_ref[b, rows, _head_lanes(h)] for b, h in chains]
            beta = [bgc[b][:, h:h + 1] for b, h in chains]
            state = [s_ref[b, h, c] for b, h in chains]
            t_inv = [t_ref[b, h, c] for b, h in chains]
            d_state = [dstate_ref[b, h] for b, h in chains]
            dec = [_chunk_decay(bgc[b][:, HEADS + h:HEADS + h + 1], ii, jj) for b, h in chains]
            gc_col, g_total, decay = ([d[i] for d in dec] for i in range(3))
            kb = each(lambda x, y: x * y, kc, beta)
            vb = each(lambda x, y: x * y, vc, beta)
            eg = [jnp.exp(g) for g in gc_col]
            kbg = each(lambda x, y: x * y, kb, eg)
            a = each(lambda x, y, d: jnp.where(strict, _dot_nt(x, y) * d, 0.0), kb, kc, decay)
            qk = each(lambda x, y, d: jnp.where(incl, _dot_nt(x, y) * d, 0.0), qc, kc, decay)
            w = each(_dot, t_inv, kbg)
            u = each(_dot, t_inv, vb)
            q_dec = each(lambda x, y: x * y, qc, eg)
            ek = each(lambda gt, g: jnp.exp(gt - g), g_total, gc_col)
            k_dec = each(lambda x, y: x * y, kc, ek)
            g_last = [jnp.exp(gt) for gt in g_total]
            v_new = each(lambda x, y, s: x - _dot(y, s), u, w, state)
            dv_new = each(lambda m, d, x, ds: _dot_tn(m, d) + _dot(x, ds), qk, doc, k_dec, d_state)
            dqk = each(lambda d, vn: jnp.where(incl, _dot_nt(d, vn), 0.0), doc, v_new)
            dq_dec = each(_dot_nt, doc, state)
            dk_dec = each(_dot_nt, v_new, d_state)
            dg_last = each(lambda s, ds: total(s * ds), state, d_state)
            new_d_state = each(lambda x, d, gl, ds, y, dvn: _dot_tn(x, d) + gl * ds - _dot_tn(y, dvn),
                               q_dec, doc, g_last, d_state, w, dv_new)
            dw = each(lambda dvn, s: -_dot_nt(dvn, s), dv_new, state)
            dt = each(lambda dvn, x, y, z: _dot_nt(dvn, x) + _dot_nt(y, z), dv_new, vb, dw, kbg)
            dvb = each(_dot_tn, t_inv, dv_new)
            dkbg = each(_dot_tn, t_inv, dw)
            t_dt = each(_dot_tn, t_inv, dt)
            da = each(lambda x, t: -jnp.where(strict, _dot_nt(x, t), 0.0), t_dt, t_inv)
            dm_a = each(lambda x, y: x * y, da, decay)
            dm_qk = each(lambda x, y: x * y, dqk, decay)
            e = each(lambda x, y, z, t: x * y + z * t, da, a, dqk, qk)
            dkb = each(lambda m, x, y, z: _dot(m, x) + y * z, dm_a, kc, dkbg, eg)
            dk = each(lambda m, x, m2, y, z, t, p, bt: _dot_tn(m, x) + _dot_tn(m2, y) + z * t + p * bt,
                      dm_a, kb, dm_qk, qc, dk_dec, ek, dkb, beta)
            dq = each(lambda m, x, y, z: _dot(m, x) + y * z, dm_qk, kc, dq_dec, eg)
            dbeta = each(lambda x, y, z, t: rowsum(x * y + z * t), dkb, kc, dvb, vc)
            dgc = each(lambda x, p, pd, r, rd, s, sd: rowsum(x) - rowsum(jnp.where(ii == jj, jnp.sum(x, axis=0, keepdims=True), 0.0))
                       + rowsum(p * pd - r * rd + s * sd), e, dq_dec, q_dec, dk_dec, k_dec, dkbg, kbg)
            d_total = each(lambda r, rd, x, gl: total(r * rd) + x * gl, dk_dec, k_dec, dg_last, g_last)
            dg = each(lambda x, t: rowsum(jnp.where(jj >= ii, jnp.sum(jnp.where(ii == jj, x, 0.0), axis=0, keepdims=True), 0.0)) + t,
                      dgc, d_total)
            dbg = [jnp.zeros((CHUNK, LANES), F32) for _ in range(batch)]
            for i, (b, h) in enumerate(chains):
                dstate_ref[b, h] = new_d_state[i]
                dk_ref[b, rows, _head_lanes(h)] = dk[i]
                dq_ref[b, rows, _head_lanes(h)] = dq[i]
                dv_ref[b, rows, _head_lanes(h)] = dvb[i] * beta[i]
                dbg[b] = dbg[b] + jnp.where(lane == h, dbeta[i], 0.0) + jnp.where(lane == HEADS + h, dg[i], 0.0)
            for b in range(batch):
                dbg_ref[b, rows, :] = dbg[b]
            return carry

        lax.fori_loop(0, seg_chunks, chunk, 0)

    rows_spec = lambda width: pl.BlockSpec((batch, seg_rows, width), lambda s: (0, segs - 1 - s, 0))
    per_chunk = lambda r, c: pl.BlockSpec((batch, HEADS, seg_chunks, r, c), lambda s: (0, 0, segs - 1 - s, 0, 0))
    as_seqs = lambda a: a.reshape(batch, rs, a.shape[-1])
    grad = jax.ShapeDtypeStruct((batch, rs, GDN_WIDTH), F32)
    wide = rows_spec(GDN_WIDTH)
    dq, dk, dv, dbg = pl.pallas_call(
        body, name="gdn_bwd",
        out_shape=(grad, grad, grad, jax.ShapeDtypeStruct((batch, rs, LANES), F32)),
        grid=(segs,),
        in_specs=[wide, wide, wide, wide, rows_spec(LANES), per_chunk(HEAD_DIM, HEAD_DIM), per_chunk(CHUNK, CHUNK)],
        out_specs=(wide, wide, wide, rows_spec(LANES)),
        scratch_shapes=[pltpu.VMEM((batch, HEADS, HEAD_DIM, HEAD_DIM), F32)],
        compiler_params=_params("arbitrary"),
    )(as_seqs(do), as_seqs(q), as_seqs(k), as_seqs(v), as_seqs(bg), states, t_invs)
    return dq.reshape(n, GDN_WIDTH), dk.reshape(n, GDN_WIDTH), dv.reshape(n, GDN_WIDTH), dbg.reshape(n, LANES)


def _lane_vec(vals, offset):
    k = vals.shape[1]
    return jnp.pad(vals, ((0, 0), (offset, LANES - offset - k)))


def _local_step(x, target, meta, norms, w_in_t, conv_qkv, a_log, dt_bias, gdn_norm, conv_sc, w_out, w_gate_t, w_up_t, w_down):
    batch, seq, d = x.shape
    tokens = N_META + seq
    pad_rows = (-tokens) % CHUNK
    rs = tokens + pad_rows
    x_offset = pad_rows + N_META
    n = batch * rs
    w_mix_pre, w_mix_post, w_ffn_pre, w_ffn_post = norms

    head = jnp.concatenate([jnp.zeros((pad_rows, d), F32), meta], axis=0)
    h0 = jnp.concatenate([jnp.broadcast_to(head[None], (batch, x_offset, d)), x], axis=1).reshape(n, d)
    target_p = jnp.pad(target, ((0, 0), (x_offset, 0), (0, 0))).reshape(n, d)
    a_log_l = _lane_vec(a_log, HEADS)
    dt_bias_l = _lane_vec(dt_bias, HEADS)

    u1 = _rms_fwd(h0, w_mix_pre, "rms_mix_pre")
    proj = _mm(u1, w_in_t, "nt", F32, "mm_proj")
    q = _qkv_fwd(proj, conv_qkv, "q", rs, pad_rows)
    k = _qkv_fwd(proj, conv_qkv, "k", rs, pad_rows)
    v = _qkv_fwd(proj, conv_qkv, "v", rs, pad_rows)
    bg = _gates_fwd(proj, a_log_l, dt_bias_l, rs, pad_rows)
    o, states, t_invs = _gdn_fwd(q, k, v, bg, rs)
    o_gated = _gate_fwd(o, proj, gdn_norm, rs)
    y_sc = _sc_fwd(proj, conv_sc, rs)
    cat = jnp.concatenate([o_gated, y_sc], axis=1)
    mix = _mm(cat, w_out, "nn", F32, "mm_mix")
    h1, u2 = _mix_residual(h0, mix, w_mix_post, w_ffn_pre)
    gate = _mm(u2, w_gate_t, "nt", F32, "mm_gate")
    up = _mm(u2, w_up_t, "nt", F32, "mm_up")
    act = _swiglu_act(gate, up)
    ffn = _mm(act, w_down, "nn", F32, "mm_down")

    dh2, dffn, d_ffn_post, sq = _loss_head(h1, ffn, w_ffn_post, target_p, rs, x_offset)
    dact = _mm(dffn, w_down, "nt", F32, "mm_dact")
    d_w_down = _mm(act, dffn, "tn", F32, "mm_dw_down")
    dgate, dup = _swiglu_bwd(gate, up, dact)
    d_w_gate_t = _mm(dgate, u2, "tn", F32, "mm_dw_gate")
    d_w_up_t = _mm(dup, u2, "tn", F32, "mm_dw_up")
    du2 = _mm(dup, w_up_t, "nn", F32, "mm_du2_up", init=_mm(dgate, w_gate_t, "nn", F32, "mm_du2_gate"))
    dh1, dmix, d_ffn_pre, d_mix_post = _mid_bwd(h1, mix, w_mix_post, w_ffn_pre, dh2, du2)
    dcat = _mm(dmix, w_out, "nt", F32, "mm_dcat")
    d_w_out = _mm(cat, dmix, "tn", F32, "mm_dw_out")
    do, dz, d_gdn_norm = _gate_bwd(dcat, o, proj, gdn_norm, rs)
    dscx, dscb, dscc, d_conv_sc = _sc_bwd(dcat, proj, conv_sc, rs)
    dq, dk, dv, dbg = _gdn_bwd(do, q, k, v, bg, states, t_invs, rs)
    dpq, dwq = _qkv_bwd(dq, proj, conv_qkv, "q", rs, pad_rows)
    dpk, dwk = _qkv_bwd(dk, proj, conv_qkv, "k", rs, pad_rows)
    dpv, dwv = _qkv_bwd(dv, proj, conv_qkv, "v", rs, pad_rows)
    d_conv_qkv = jnp.concatenate([dwq, dwk, dwv], axis=1)
    dba, d_a_log_l, d_dt_bias_l = _gates_bwd(proj, dbg, a_log_l, dt_bias_l, rs, pad_rows)
    dproj = jnp.concatenate([dpq, dpk, dpv, dz, dscx, dscb, dscc, dba], axis=1)
    d_w_in_t = _mm(dproj, u1, "tn", F32, "mm_dw_in")
    du1 = _mm(dproj, w_in_t, "nn", F32, "mm_du1")
    dh0, d_mix_pre = _in_bwd(h0, w_mix_pre, dh1, du1)

    dh0 = dh0.reshape(batch, rs, d)
    grads = dict(
        meta_tokens=jnp.sum(dh0[:, pad_rows:x_offset], axis=0),
        mix_pre_norm=d_mix_pre, mix_post_norm=d_mix_post, ffn_pre_norm=d_ffn_pre, ffn_post_norm=d_ffn_post,
        w_in=d_w_in_t, conv_qkv=d_conv_qkv,
        a_log=d_a_log_l[:, HEADS:2 * HEADS], dt_bias=d_dt_bias_l[:, HEADS:2 * HEADS],
        gdn_norm=d_gdn_norm, conv_sc=d_conv_sc, w_out=d_w_out, w_gate=d_w_gate_t, w_up=d_w_up_t, w_down=d_w_down,
    )
    return sq, dh0[:, x_offset:], grads


def _to_padded_in(w_in_t):
    lo, hi = 4 * GDN_WIDTH, 4 * GDN_WIDTH + 2 * HEADS
    pad = jnp.zeros((IN_PAD - IN_WIDTH, w_in_t.shape[1]), w_in_t.dtype)
    return jnp.concatenate([w_in_t[:lo], w_in_t[hi:], w_in_t[lo:hi], pad], axis=0)


def _from_padded_in(w_in_p):
    lo, hi = 4 * GDN_WIDTH, IN_WIDTH - 2 * HEADS
    return jnp.concatenate([w_in_p[:lo], w_in_p[hi:IN_WIDTH], w_in_p[lo:hi]], axis=0)


MATRICES = ("w_in", "w_out", "w_gate", "w_up", "w_down")
IN_SHARD = IN_WIDTH // N_CHIPS
IN_SHARD_PAD = 928
PIECE_ROWS = dict(w_in=IN_SHARD_PAD, w_out=D_MODEL // N_CHIPS, w_gate=D_FF // N_CHIPS, w_up=D_FF // N_CHIPS,
                  w_down=D_FF // N_CHIPS)
SMALL_ROWS = 48
REDUCE_ROWS = 224


def _flatten(pieces, rows):
    flat = jnp.concatenate([p.reshape(-1) for p in pieces])
    return jnp.pad(flat, (0, rows * LANES - flat.shape[0])).reshape(rows, LANES)


def _unflatten(flat, shapes):
    flat = flat.reshape(-1)
    out, at = [], 0
    for shape in shapes:
        size = shape[0] * shape[1]
        out.append(flat[at:at + size].reshape(shape))
        at += size
    return out


def _stored(w):
    return jnp.transpose(w, (2, 0, 1)).reshape(-1, LANES)


def _unstored(flat, d):
    return jnp.transpose(flat.reshape(-1, 1, d), (1, 2, 0))


def _hbm():
    return pl.BlockSpec(memory_space=pl.ANY)


def _place():
    x, y, c = lax.axis_index("x"), lax.axis_index("y"), lax.axis_index("c")
    chips = ((1 - x, y), (x, 1 - y), (1 - x, 1 - y))
    return x, y, c, chips


def _gather_weights(pieces, s_flat):
    count = len(pieces)

    def body(*refs):
        w_refs, s_ref = refs[:count], refs[count]
        out_refs, sall_ref = refs[count + 1:2 * count + 1], refs[2 * count + 1]
        send_sems, recv_sems, local_sem = refs[2 * count + 2:]
        x, y, c, chips = _place()
        mine = 2 * x + y
        sibling = (x, y, 1 - c)

        def copy(k, src, dst, to):
            return pltpu.make_async_remote_copy(src_ref=src, dst_ref=dst, send_sem=send_sems.at[k], recv_sem=recv_sems.at[k],
                                                device_id=to, device_id_type=MESH)

        own_s = pltpu.make_async_copy(s_ref, sall_ref.at[mine], local_sem)
        own_s.start()
        started = []
        for j, (cx, cy) in enumerate(chips):
            for i, (w, o) in enumerate(zip(w_refs, out_refs)):
                started.append(copy(7 * i + 1 + j, w.at[c], o.at[mine, c], (cx, cy, c)))
            started.append(copy(7 * count + j, s_ref, sall_ref.at[mine], (cx, cy, c)))
        for i, (w, o) in enumerate(zip(w_refs, out_refs)):
            started.append(copy(7 * i, w, o.at[mine], sibling))
        for cp in started:
            cp.start()
        for j, (cx, cy) in enumerate(chips):
            theirs = 2 * cx + cy
            for i, (w, o) in enumerate(zip(w_refs, out_refs)):
                copy(7 * i + 1 + j, w.at[c], o.at[theirs, c], sibling).wait_recv()
                started.append(copy(7 * i + 4 + j, o.at[theirs, c], o.at[theirs, c], sibling))
                started[-1].start()
        for i, (w, o) in enumerate(zip(w_refs, out_refs)):
            copy(7 * i, w, o.at[mine], sibling).wait_recv()
            for j, (cx, cy) in enumerate(chips):
                copy(7 * i + 4 + j, w.at[c], o.at[2 * cx + cy, 1 - c], sibling).wait_recv()
        for j in range(3):
            copy(7 * count + j, s_ref, sall_ref.at[mine], sibling).wait_recv()
        for cp in started:
            cp.wait_send()
        own_s.wait()

    sems = 7 * count + 3
    return pl.pallas_call(
        body, name="gather_weights",
        out_shape=tuple(jax.ShapeDtypeStruct((N_CHIPS,) + p.shape, p.dtype) for p in pieces)
        + (jax.ShapeDtypeStruct((N_CHIPS,) + s_flat.shape, s_flat.dtype),),
        in_specs=[_hbm()] * (count + 1), out_specs=(_hbm(),) * (count + 1),
        scratch_shapes=[pltpu.SemaphoreType.DMA((sems,)), pltpu.SemaphoreType.DMA((sems,)), pltpu.SemaphoreType.DMA],
    )(*pieces, s_flat)


def _exchange_siblings(grads, small):
    count = len(grads)

    def body(*refs):
        g_refs, s_ref = refs[:count], refs[count]
        got_refs, sall_ref = refs[count + 1:2 * count + 1], refs[2 * count + 1]
        send_sems, recv_sems, local_sem = refs[2 * count + 2:]
        x, y, c, _ = _place()
        me = 4 * x + 2 * y + c
        own = pltpu.make_async_copy(s_ref, sall_ref.at[me], local_sem)
        own.start()
        copies = []
        for k in range(7):
            dx, dy, dc = ((k + 1) >> 2) & 1, ((k + 1) >> 1) & 1, (k + 1) & 1
            peer = (1 - x if dx else x, 1 - y if dy else y, 1 - c if dc else c)
            copies.append(pltpu.make_async_remote_copy(
                src_ref=s_ref, dst_ref=sall_ref.at[me], send_sem=send_sems.at[k], recv_sem=recv_sems.at[k],
                device_id=peer, device_id_type=MESH))
        for i, (g, got) in enumerate(zip(g_refs, got_refs)):
            copies.append(pltpu.make_async_remote_copy(
                src_ref=g.at[:, 1 - c], dst_ref=got, send_sem=send_sems.at[7 + i], recv_sem=recv_sems.at[7 + i],
                device_id=(x, y, 1 - c), device_id_type=MESH))
        for cp in copies:
            cp.start()
        for cp in copies:
            cp.wait_recv()
        for cp in copies:
            cp.wait_send()
        own.wait()

    return pl.pallas_call(
        body, name="exchange_siblings",
        out_shape=tuple(jax.ShapeDtypeStruct((g.shape[0],) + g.shape[2:], F32) for g in grads)
        + (jax.ShapeDtypeStruct((8,) + small.shape, F32),),
        in_specs=[_hbm()] * (count + 1), out_specs=(_hbm(),) * (count + 1),
        scratch_shapes=[pltpu.SemaphoreType.DMA((7 + count,)), pltpu.SemaphoreType.DMA((7 + count,)), pltpu.SemaphoreType.DMA],
    )(*grads, small)


def _exchange_chips(parts):
    count = len(parts)

    def body(*refs):
        p_refs, got_refs = refs[:count], refs[count:2 * count]
        send_sems, recv_sems = refs[2 * count:]
        x, y, c, chips = _place()
        copies = [pltpu.make_async_remote_copy(
            src_ref=p.at[2 * cx + cy], dst_ref=got.at[j], send_sem=send_sems.at[3 * i + j], recv_sem=recv_sems.at[3 * i + j],
            device_id=(cx, cy, c), device_id_type=MESH)
            for j, (cx, cy) in enumerate(chips) for i, (p, got) in enumerate(zip(p_refs, got_refs))]
        for cp in copies:
            cp.start()
        for cp in copies:
            cp.wait_recv()
        for cp in copies:
            cp.wait_send()

    return pl.pallas_call(
        body, name="exchange_chips", out_shape=tuple(jax.ShapeDtypeStruct((3,) + p.shape[1:], p.dtype) for p in parts),
        in_specs=[_hbm()] * count, out_specs=(_hbm(),) * count,
        scratch_shapes=[pltpu.SemaphoreType.DMA((3 * count,)), pltpu.SemaphoreType.DMA((3 * count,))],
    )(*parts)


def _share_halves(halves):
    count = len(halves)

    def body(*refs):
        h_refs, full_refs = refs[:count], refs[count:2 * count]
        send_sems, recv_sems = refs[2 * count:]
        x, y, c, _ = _place()
        copies = [pltpu.make_async_remote_copy(src_ref=h.at[c], dst_ref=full.at[c], send_sem=send_sems.at[i],
                                               recv_sem=recv_sems.at[i], device_id=(x, y, 1 - c), device_id_type=MESH)
                  for i, (h, full) in enumerate(zip(h_refs, full_refs))]
        for cp in copies:
            cp.start()
        for cp in copies:
            cp.wait_recv()
        for cp in copies:
            cp.wait_send()

    return pl.pallas_call(
        body, name="share_halves", out_shape=tuple(jax.ShapeDtypeStruct(h.shape, h.dtype) for h in halves),
        in_specs=[_hbm()] * count, out_specs=(_hbm(),) * count, input_output_aliases={i: i for i in range(count)},
        scratch_shapes=[pltpu.SemaphoreType.DMA((count,)), pltpu.SemaphoreType.DMA((count,))],
    )(*halves)


def _add_sibling(grad, got, core, name):
    chips, _, rows, cols = grad.shape

    def body(core_ref, g_ref, r_ref, sum_ref, send_ref):
        s = g_ref[...] + r_ref[...]
        sum_ref[...] = s
        send_ref[...] = s.astype(send_ref.dtype)

    block = pl.BlockSpec((None, rows, cols), lambda p, core_ref: (p, 0, 0))
    return pl.pallas_call(
        body, name="add_sibling_" + name,
        out_shape=(jax.ShapeDtypeStruct((chips, rows, cols), F32), jax.ShapeDtypeStruct((chips, rows, cols), BF16)),
        grid_spec=pltpu.PrefetchScalarGridSpec(
            num_scalar_prefetch=1, grid=(chips,),
            in_specs=[pl.BlockSpec((None, None, rows, cols), lambda p, core_ref: (p, core_ref[0], 0, 0)), block],
            out_specs=(block, block)),
        compiler_params=_params("parallel"),
    )(core, grad, got)


def _add_chips(part, got, chip_core, name):
    _, rows, cols = part.shape
    tr = rows // 2 if rows % 32 == 0 else rows

    def body(place_ref, p_ref, r_ref, o_ref):
        o_ref[...] = ((p_ref[...] + r_ref[0].astype(F32)) + r_ref[1].astype(F32)) + r_ref[2].astype(F32)

    return pl.pallas_call(
        body, name="add_chips_" + name, out_shape=jax.ShapeDtypeStruct((2, rows, cols), F32),
        grid_spec=pltpu.PrefetchScalarGridSpec(
            num_scalar_prefetch=1, grid=(rows // tr,),
            in_specs=[pl.BlockSpec((None, tr, cols), lambda i, place_ref: (place_ref[0], i, 0)),
                      pl.BlockSpec((3, tr, cols), lambda i, place_ref: (0, i, 0))],
            out_specs=pl.BlockSpec((None, tr, cols), lambda i, place_ref: (place_ref[1], i, 0))),
        compiler_params=_params("parallel"),
    )(chip_core, part, got)


def _sum_devices(small_all):
    def body(s_ref, o_ref):
        acc = s_ref[0]
        for k in range(1, 8):
            acc = acc + s_ref[k]
        o_ref[...] = acc

    return pl.pallas_call(body, name="sum_devices", out_shape=jax.ShapeDtypeStruct(small_all.shape[1:], F32))(small_all)


def _adamw(w, g, m, v, name):
    rows, cols = w.shape
    tr = _pick(rows, (3592, 256, 352, 176, 128, 64, 32, 16, 8))

    def body(w_ref, g_ref, m_ref, v_ref, d_ref, nm_ref, nv_ref):
        g = g_ref[...]
        m = ADAM_B1 * m_ref[...] + (1.0 - ADAM_B1) * g
        v = ADAM_B2 * v_ref[...] + (1.0 - ADAM_B2) * (g * g)
        m_hat = m / (1.0 - ADAM_B1 ** ADAM_STEP)
        v_hat = v / (1.0 - ADAM_B2 ** ADAM_STEP)
        d_ref[...] = -ADAM_LR * (m_hat / (jnp.sqrt(v_hat) + ADAM_EPS) + ADAM_WD * w_ref[...])
        nm_ref[...] = m
        nv_ref[...] = v

    block = pl.BlockSpec((tr, cols), lambda i: (i, 0))
    shape = jax.ShapeDtypeStruct((rows, cols), F32)
    return pl.pallas_call(
        body, name="adamw_" + name, out_shape=(shape, shape, shape), grid=(rows // tr,),
        in_specs=[block] * 4, out_specs=(block,) * 3, compiler_params=_params("parallel"),
    )(w, g, m, v)


WEIGHTS = ("meta_tokens", "mix_pre_norm", "mix_post_norm", "ffn_pre_norm", "ffn_post_norm", "w_in", "conv_qkv", "a_log",
           "dt_bias", "gdn_norm", "conv_sc", "w_out", "w_gate", "w_up", "w_down")


def kernel(x, meta_tokens, mix_pre_norm, mix_post_norm, ffn_pre_norm, ffn_post_norm, w_in, conv_qkv, a_log, dt_bias, gdn_norm, conv_sc, w_out, w_gate, w_up, w_down, loss_target, m_meta_tokens, m_mix_pre_norm, m_mix_post_norm, m_ffn_pre_norm, m_ffn_post_norm, m_w_in, m_conv_qkv, m_a_log, m_dt_bias, m_gdn_norm, m_conv_sc, m_w_out, m_w_gate, m_w_up, m_w_down, v_meta_tokens, v_mix_pre_norm, v_mix_post_norm, v_ffn_pre_norm, v_ffn_post_norm, v_w_in, v_conv_qkv, v_a_log, v_dt_bias, v_gdn_norm, v_conv_sc, v_w_out, v_w_gate, v_w_up, v_w_down):
    d = x.shape[-1]
    two_d = lambda a: a.reshape(a.shape[-2:])
    weights = dict(zip(WEIGHTS, (meta_tokens, mix_pre_norm, mix_post_norm, ffn_pre_norm, ffn_post_norm, w_in, conv_qkv, a_log,
                                 dt_bias, gdn_norm, conv_sc, w_out, w_gate, w_up, w_down)))
    m_in = dict(zip(WEIGHTS, (m_meta_tokens, m_mix_pre_norm, m_mix_post_norm, m_ffn_pre_norm, m_ffn_post_norm, m_w_in, m_conv_qkv,
                              m_a_log, m_dt_bias, m_gdn_norm, m_conv_sc, m_w_out, m_w_gate, m_w_up, m_w_down)))
    v_in = dict(zip(WEIGHTS, (v_meta_tokens, v_mix_pre_norm, v_mix_post_norm, v_ffn_pre_norm, v_ffn_post_norm, v_w_in, v_conv_qkv,
                              v_a_log, v_dt_bias, v_gdn_norm, v_conv_sc, v_w_out, v_w_gate, v_w_up, v_w_down)))
    core = lax.axis_index("c")
    chip = 2 * lax.axis_index("x") + lax.axis_index("y")
    core_arg = core.reshape(1).astype(jnp.int32)
    chip_core = jnp.stack([chip, core]).astype(jnp.int32)
    small_shapes = [two_d(weights[n]).shape for n in ("conv_qkv", "conv_sc", "meta_tokens")]
    halves = lambda a: a.reshape(a.shape[:-2] + (2, a.shape[-2] // 2, d))
    whole = lambda a: a.reshape(a.shape[:-3] + (2 * a.shape[-2], d))

    def stored(params, n):
        if n == "w_in":
            return _stored(params[n]).reshape(IN_SHARD, d)
        return two_d(params[n]).T if n in ("w_gate", "w_up") else two_d(params[n])

    shard = {n: stored(weights, n).astype(MXU_DTYPE) for n in MATRICES}
    shard["w_in"] = jnp.pad(shard["w_in"], ((0, IN_SHARD_PAD - IN_SHARD), (0, 0)))
    s_flat = _flatten([two_d(weights[n]) for n in ("conv_qkv", "conv_sc", "meta_tokens")], SMALL_ROWS)
    *gathered, s_all = _gather_weights([halves(shard[n]) for n in MATRICES], s_flat)
    full = {n: whole(a) for n, a in zip(MATRICES, gathered)}
    w_in_t = _to_padded_in(full["w_in"][:, :IN_SHARD].reshape(IN_WIDTH, d))
    small_chip = [_unflatten(s_all[p], small_shapes) for p in range(N_CHIPS)]
    conv_qkv_full, conv_sc_full, meta_full = (jnp.concatenate([small_chip[p][i] for p in range(N_CHIPS)], axis=1)
                                              for i in range(3))

    sq, grad_x, g = _local_step(
        x, loss_target, meta_full, (mix_pre_norm, mix_post_norm, ffn_pre_norm, ffn_post_norm), w_in_t, conv_qkv_full, a_log,
        dt_bias, gdn_norm, conv_sc_full, full["w_out"].reshape(-1, d), full["w_gate"].reshape(-1, d), full["w_up"].reshape(-1, d),
        full["w_down"].reshape(-1, d))

    g["w_in"] = jnp.pad(_from_padded_in(g["w_in"]).reshape(N_CHIPS, IN_SHARD, d), ((0, 0), (0, IN_SHARD_PAD - IN_SHARD), (0, 0)))
    by_chip = [halves(g[n].reshape(N_CHIPS, PIECE_ROWS[n], d)) for n in MATRICES]
    scalars = jnp.concatenate([g["a_log"], g["dt_bias"], sq], axis=1)
    small = _flatten([g["mix_pre_norm"], g["mix_post_norm"], g["ffn_pre_norm"], g["ffn_post_norm"],
                      jnp.pad(scalars, ((0, 0), (0, LANES - scalars.shape[1]))), g["gdn_norm"], g["conv_qkv"], g["conv_sc"],
                      g["meta_tokens"]], REDUCE_ROWS)
    *got_sibling, small_all = _exchange_siblings(by_chip, small)
    sums = [_add_sibling(a, b, core_arg, n) for n, a, b in zip(MATRICES, by_chip, got_sibling)]
    got_chips = _exchange_chips([send for _, send in sums])
    totals = [_add_chips(part, got, chip_core, n) for n, (part, _), got in zip(MATRICES, sums, got_chips)]
    grads = {n: whole(a) for n, a in zip(MATRICES, _share_halves(totals))}
    reduced = _sum_devices(small_all)

    r = reduced.reshape(-1)
    at = 0
    for n in ("mix_pre_norm", "mix_post_norm", "ffn_pre_norm", "ffn_post_norm"):
        grads[n] = r[at:at + d].reshape(1, d)
        at += d
    grads["a_log"] = r[at:at + HEADS].reshape(1, HEADS)
    grads["dt_bias"] = r[at + HEADS:at + 2 * HEADS].reshape(1, HEADS)
    loss = (0.5 / d) * r[at + 2 * HEADS]
    at += LANES
    grads["gdn_norm"] = r[at:at + HEAD_DIM].reshape(1, HEAD_DIM)
    at += HEAD_DIM
    for n, shape in (("conv_qkv", (GDN_CONV, 3 * GDN_WIDTH)), ("conv_sc", (SC_CONV, SC_WIDTH)), ("meta_tokens", (N_META, d))):
        full_grad = r[at:at + shape[0] * shape[1]].reshape(shape)
        at += shape[0] * shape[1]
        width = shape[1] // N_CHIPS
        grads[n] = lax.dynamic_slice_in_dim(full_grad, chip * width, width, axis=1)

    outs = [[], [], [], []]
    for n in WEIGHTS:
        shape = weights[n].shape
        grad = grads[n]
        if n == "w_in":
            to_kernel, from_kernel = _stored, functools.partial(_unstored, d=d)
            grad = grad[:IN_SHARD].reshape(-1, LANES)
        elif n in ("w_gate", "w_up"):
            to_kernel, from_kernel = (lambda a: two_d(a).T), (lambda a: a.T.reshape(shape))
        else:
            to_kernel, from_kernel = two_d, (lambda a: a.reshape(shape))
        delta, new_m, new_v = _adamw(to_kernel(weights[n]), grad, to_kernel(m_in[n]), to_kernel(v_in[n]), n)
        for out, a in zip(outs, (grad, delta, new_m, new_v)):
            out.append(from_kernel(a))
    return (loss, grad_x, *outs[0], *outs[1], *outs[2], *outs[3])
```

```python
import functools

import jax
import jax.numpy as jnp
from jax import lax
from jax.experimental import pallas as pl
from jax.experimental.pallas import tpu as pltpu

F32 = jnp.float32
BF16 = jnp.bfloat16
MXU_DTYPE = jnp.bfloat16
MESH = pl.DeviceIdType.MESH

D_MODEL = 1024
N_META = 16
HEADS = 4
HEAD_DIM = 128
GDN_WIDTH = HEADS * HEAD_DIM
GDN_CONV = 4
CHUNK = 64
SC_WIDTH = D_MODEL - GDN_WIDTH
SC_CONV = 3
D_FF = 2816
IN_WIDTH = 4 * GDN_WIDTH + 2 * HEADS + 3 * SC_WIDTH
IN_PAD = 3840
BA_COL = (4 * GDN_WIDTH + 3 * SC_WIDTH) // 128
EPS = 1e-6
LANES = 128
N_CHIPS = 4
VMEM_LIMIT = 48 * 2 ** 20

ADAM_LR = 0.001
ADAM_B1 = 0.9
ADAM_B2 = 0.999
ADAM_EPS = 1e-08
ADAM_WD = 0.01
ADAM_STEP = 10


def _pick(n, candidates):
    for c in candidates:
        if n % c == 0:
            return c
    return n


def _row_tile(n):
    return _pick(n, (352, 256, 176, 128, 64, 32, 16, 8))


def _params(*sem):
    return pltpu.CompilerParams(dimension_semantics=sem, vmem_limit_bytes=VMEM_LIMIT)


def _sigmoid(x):
    return 1.0 / (1.0 + jnp.exp(-x))


def _softplus(x):
    return jnp.maximum(x, 0.0) + jnp.log(1.0 + jnp.exp(-jnp.abs(x)))


def _dsilu(x, s):
    return s * (1.0 + x * (1.0 - s))


def _mm(a, b, mode, out_dtype, name, init=None):
    if mode == "tn":
        k_dim, m_dim = a.shape
    else:
        m_dim, k_dim = a.shape
    n_dim = b.shape[0] if mode == "nt" else b.shape[1]
    tm = _pick(m_dim, (1408, 1280, 1024, 512, 256, 128) if mode == "tn" else (1056, 1024, 704, 512, 256, 128))
    tn = _pick(n_dim, (1408, 1280, 1024, 768, 512, 256, 128))
    tk = _pick(k_dim, (1408, 1280, 1056, 1024, 512, 256, 128))
    nk = k_dim // tk
    if mode == "nn":
        a_spec = pl.BlockSpec((tm, tk), lambda i, j, k: (i, k))
        b_spec = pl.BlockSpec((tk, tn), lambda i, j, k: (k, j))
        dims = (((1,), (0,)), ((), ()))
    elif mode == "nt":
        a_spec = pl.BlockSpec((tm, tk), lambda i, j, k: (i, k))
        b_spec = pl.BlockSpec((tn, tk), lambda i, j, k: (j, k))
        dims = (((1,), (1,)), ((), ()))
    else:
        a_spec = pl.BlockSpec((tk, tm), lambda i, j, k: (k, i))
        b_spec = pl.BlockSpec((tk, tn), lambda i, j, k: (k, j))
        dims = (((0,), (0,)), ((), ()))

    out_spec = pl.BlockSpec((tm, tn), lambda i, j, k: (i, j))

    def body(a_ref, b_ref, *rest):
        o_ref, acc_ref = rest[-2:]
        k = pl.program_id(2)
        p = lax.dot_general(a_ref[...], b_ref[...], dims, preferred_element_type=F32)

        @pl.when(k == 0)
        def _():
            acc_ref[...] = p if init is None else rest[0][...] + p

        @pl.when(k > 0)
        def _():
            acc_ref[...] += p

        @pl.when(k == nk - 1)
        def _():
            o_ref[...] = acc_ref[...].astype(out_dtype)

    return pl.pallas_call(
        body, name=name,
        out_shape=jax.ShapeDtypeStruct((m_dim, n_dim), out_dtype),
        grid=(m_dim // tm, n_dim // tn, nk),
        in_specs=[a_spec, b_spec] + ([] if init is None else [out_spec]),
        out_specs=out_spec,
        scratch_shapes=[pltpu.VMEM((tm, tn), F32)],
        compiler_params=_params("parallel", "parallel", "arbitrary"),
    )(*((a, b) if init is None else (a, b, init)))


def _rms_apply(x, w):
    r = lax.rsqrt(jnp.mean(x * x, axis=-1, keepdims=True) + EPS)
    return x * r * w


def _rms_bwd(x, w, dy):
    r = lax.rsqrt(jnp.mean(x * x, axis=-1, keepdims=True) + EPS)
    xh = x * r
    dyw = dy * w
    dx = r * (dyw - xh * jnp.mean(dyw * xh, axis=-1, keepdims=True))
    return dx, jnp.sum(dy * xh, axis=0, keepdims=True)


def _accumulate(ref, first, value):
    @pl.when(first)
    def _():
        ref[...] = value

    @pl.when(jnp.logical_not(first))
    def _():
        ref[...] += value


def _rows(tr, width):
    return pl.BlockSpec((tr, width), lambda i: (i, 0))


def _vec(width):
    return pl.BlockSpec((1, width), lambda i: (0, 0))


def _rms_fwd(h, w, name):
    n, d = h.shape
    tr = _row_tile(n)

    def body(h_ref, w_ref, u_ref):
        u_ref[...] = _rms_apply(h_ref[...], w_ref[...]).astype(u_ref.dtype)

    return pl.pallas_call(
        body, name=name, out_shape=jax.ShapeDtypeStruct((n, d), MXU_DTYPE), grid=(n // tr,),
        in_specs=[_rows(tr, d), _vec(d)], out_specs=_rows(tr, d), compiler_params=_params("parallel"),
    )(h, w)


def _mix_residual(h0, mix, w_post, w_pre):
    n, d = h0.shape
    tr = _row_tile(n)

    def body(h0_ref, mix_ref, wpost_ref, wpre_ref, h1_ref, u2_ref):
        h1 = h0_ref[...] + _rms_apply(mix_ref[...], wpost_ref[...])
        h1_ref[...] = h1
        u2_ref[...] = _rms_apply(h1, wpre_ref[...]).astype(u2_ref.dtype)

    return pl.pallas_call(
        body, name="mix_residual",
        out_shape=(jax.ShapeDtypeStruct((n, d), F32), jax.ShapeDtypeStruct((n, d), MXU_DTYPE)), grid=(n // tr,),
        in_specs=[_rows(tr, d), _rows(tr, d), _vec(d), _vec(d)], out_specs=(_rows(tr, d), _rows(tr, d)),
        compiler_params=_params("parallel"),
    )(h0, mix, w_post, w_pre)


def _swiglu_act(gate, up):
    n = gate.shape[0]
    tr = _pick(n, (176, 128, 64, 32, 16, 8))

    def body(g_ref, u_ref, act_ref):
        g = g_ref[...]
        act_ref[...] = (g * _sigmoid(g) * u_ref[...]).astype(act_ref.dtype)

    return pl.pallas_call(
        body, name="swiglu_act", out_shape=jax.ShapeDtypeStruct((n, D_FF), MXU_DTYPE), grid=(n // tr,),
        in_specs=[_rows(tr, D_FF), _rows(tr, D_FF)], out_specs=_rows(tr, D_FF), compiler_params=_params("parallel"),
    )(gate, up)


def _swiglu_bwd(gate, up, dact):
    n = gate.shape[0]
    tr = _pick(n, (176, 128, 64, 32, 16, 8))

    def body(g_ref, u_ref, dact_ref, dg_ref, du_ref):
        g = g_ref[...]
        s = _sigmoid(g)
        da = dact_ref[...]
        dg_ref[...] = (da * u_ref[...] * _dsilu(g, s)).astype(dg_ref.dtype)
        du_ref[...] = (da * g * s).astype(du_ref.dtype)

    shape = jax.ShapeDtypeStruct((n, D_FF), MXU_DTYPE)
    return pl.pallas_call(
        body, name="swiglu_bwd", out_shape=(shape, shape), grid=(n // tr,),
        in_specs=[_rows(tr, D_FF)] * 3, out_specs=(_rows(tr, D_FF), _rows(tr, D_FF)),
        compiler_params=_params("parallel"),
    )(gate, up, dact)


def _loss_head(h1, ffn, w_post, target, rows_per_seq, x_offset):
    n, d = h1.shape
    tr = _row_tile(rows_per_seq)
    tiles_per_seq = rows_per_seq // tr

    def body(h1_ref, ffn_ref, w_ref, t_ref, dh2_ref, dffn_ref, dw_ref, sq_ref):
        i = pl.program_id(0)
        w = w_ref[...]
        f = ffn_ref[...]
        r = lax.rsqrt(jnp.mean(f * f, axis=-1, keepdims=True) + EPS)
        fh = f * r
        row = lax.rem(i, tiles_per_seq) * tr + lax.broadcasted_iota(jnp.int32, (tr, 1), 0)
        err = jnp.where(row >= x_offset, h1_ref[...] + fh * w - t_ref[...], 0.0)
        dh2 = err * (1.0 / d)
        dh2_ref[...] = dh2
        dyw = dh2 * w
        dffn_ref[...] = (r * (dyw - fh * jnp.mean(dyw * fh, axis=-1, keepdims=True))).astype(dffn_ref.dtype)
        _accumulate(dw_ref, i == 0, jnp.sum(dh2 * fh, axis=0, keepdims=True))
        _accumulate(sq_ref, i == 0, jnp.sum(jnp.sum(err * err, axis=1, keepdims=True), axis=0, keepdims=True))

    return pl.pallas_call(
        body, name="loss_head",
        out_shape=(jax.ShapeDtypeStruct((n, d), F32), jax.ShapeDtypeStruct((n, d), MXU_DTYPE),
                   jax.ShapeDtypeStruct((1, d), F32), jax.ShapeDtypeStruct((1, 1), F32)),
        grid=(n // tr,),
        in_specs=[_rows(tr, d), _rows(tr, d), _vec(d), _rows(tr, d)],
        out_specs=(_rows(tr, d), _rows(tr, d), _vec(d), _vec(1)),
        compiler_params=_params("arbitrary"),
    )(h1, ffn, w_post, target)


def _mid_bwd(h1, mix, w_mix_post, w_ffn_pre, dh2, du2):
    n, d = h1.shape
    tr = _row_tile(n)

    def body(h1_ref, mix_ref, wpost_ref, wpre_ref, dh2_ref, du2_ref, dh1_ref, dmix_ref, dwpre_ref, dwpost_ref):
        i = pl.program_id(0)
        dx, dwpre = _rms_bwd(h1_ref[...], wpre_ref[...], du2_ref[...])
        dh1 = dh2_ref[...] + dx
        dh1_ref[...] = dh1
        dmix, dwpost = _rms_bwd(mix_ref[...], wpost_ref[...], dh1)
        dmix_ref[...] = dmix.astype(dmix_ref.dtype)
        _accumulate(dwpre_ref, i == 0, dwpre)
        _accumulate(dwpost_ref, i == 0, dwpost)

    return pl.pallas_call(
        body, name="mid_bwd",
        out_shape=(jax.ShapeDtypeStruct((n, d), F32), jax.ShapeDtypeStruct((n, d), MXU_DTYPE),
                   jax.ShapeDtypeStruct((1, d), F32), jax.ShapeDtypeStruct((1, d), F32)),
        grid=(n // tr,),
        in_specs=[_rows(tr, d), _rows(tr, d), _vec(d), _vec(d), _rows(tr, d), _rows(tr, d)],
        out_specs=(_rows(tr, d), _rows(tr, d), _vec(d), _vec(d)),
        compiler_params=_params("arbitrary"),
    )(h1, mix, w_mix_post, w_ffn_pre, dh2, du2)


def _in_bwd(h0, w_pre, dh1, du1):
    n, d = h0.shape
    tr = _row_tile(n)

    def body(h0_ref, w_ref, dh1_ref, du1_ref, dh0_ref, dw_ref):
        dx, dw = _rms_bwd(h0_ref[...], w_ref[...], du1_ref[...])
        dh0_ref[...] = dh1_ref[...] + dx
        _accumulate(dw_ref, pl.program_id(0) == 0, dw)

    return pl.pallas_call(
        body, name="in_bwd",
        out_shape=(jax.ShapeDtypeStruct((n, d), F32), jax.ShapeDtypeStruct((1, d), F32)), grid=(n // tr,),
        in_specs=[_rows(tr, d), _vec(d), _rows(tr, d), _rows(tr, d)], out_specs=(_rows(tr, d), _vec(d)),
        compiler_params=_params("arbitrary"),
    )(h0, w_pre, dh1, du1)


def _lane_is(lo, hi):
    lane = lax.broadcasted_iota(jnp.int32, (1, LANES), 1)
    return jnp.logical_and(lane >= lo, lane < hi)


def _gates_fwd(proj, a_log_l, dt_bias_l, rows_per_seq, pad_rows):
    n = proj.shape[0]
    tr = _row_tile(rows_per_seq)
    tiles_per_seq = rows_per_seq // tr

    def body(p_ref, a_ref, dt_ref, o_ref):
        x = p_ref[...]
        row = lax.rem(pl.program_id(0), tiles_per_seq) * tr + lax.broadcasted_iota(jnp.int32, (tr, 1), 0)
        g = -jnp.exp(a_ref[...]) * _softplus(x + dt_ref[...])
        val = jnp.where(_lane_is(0, HEADS), _sigmoid(x), jnp.where(_lane_is(HEADS, 2 * HEADS), g, 0.0))
        o_ref[...] = jnp.where(row >= pad_rows, val, 0.0)

    return pl.pallas_call(
        body, name="gates_fwd", out_shape=jax.ShapeDtypeStruct((n, LANES), F32), grid=(n // tr,),
        in_specs=[pl.BlockSpec((tr, LANES), lambda i: (i, BA_COL)), _vec(LANES), _vec(LANES)],
        out_specs=_rows(tr, LANES), compiler_params=_params("parallel"),
    )(proj, a_log_l, dt_bias_l)


def _gates_bwd(proj, dbg, a_log_l, dt_bias_l, rows_per_seq, pad_rows):
    n = proj.shape[0]
    tr = _row_tile(rows_per_seq)
    tiles_per_seq = rows_per_seq // tr

    def body(p_ref, d_ref, a_ref, dt_ref, dx_ref, da_ref, ddt_ref):
        i = pl.program_id(0)
        x = p_ref[...]
        d = d_ref[...]
        row = lax.rem(i, tiles_per_seq) * tr + lax.broadcasted_iota(jnp.int32, (tr, 1), 0)
        live = row >= pad_rows
        beta = _sigmoid(x)
        ea = jnp.exp(a_ref[...])
        xa = x + dt_ref[...]
        g = -ea * _softplus(xa)
        is_g = _lane_is(HEADS, 2 * HEADS)
        d_alogit = jnp.where(jnp.logical_and(live, is_g), d * (-ea) * _sigmoid(xa), 0.0)
        d_blogit = jnp.where(jnp.logical_and(live, _lane_is(0, HEADS)), d * beta * (1.0 - beta), 0.0)
        dx_ref[:, :LANES] = (d_alogit + d_blogit).astype(dx_ref.dtype)
        dx_ref[:, LANES:] = jnp.zeros((tr, LANES), dx_ref.dtype)
        _accumulate(da_ref, i == 0, jnp.sum(jnp.where(jnp.logical_and(live, is_g), d * g, 0.0), axis=0, keepdims=True))
        _accumulate(ddt_ref, i == 0, jnp.sum(d_alogit, axis=0, keepdims=True))

    return pl.pallas_call(
        body, name="gates_bwd",
        out_shape=(jax.ShapeDtypeStruct((n, 2 * LANES), MXU_DTYPE), jax.ShapeDtypeStruct((1, LANES), F32),
                   jax.ShapeDtypeStruct((1, LANES), F32)),
        grid=(n // tr,),
        in_specs=[pl.BlockSpec((tr, LANES), lambda i: (i, BA_COL)), _rows(tr, LANES), _vec(LANES), _vec(LANES)],
        out_specs=(_rows(tr, 2 * LANES), _vec(LANES), _vec(LANES)),
        compiler_params=_params("arbitrary"),
    )(proj, dbg, a_log_l, dt_bias_l)


def _shift_down(x, k):
    return x if k == 0 else pltpu.roll(x, k, 0)


def _shift_up(x, k):
    return x if k == 0 else pltpu.roll(x, x.shape[0] - k, 0)


def _causal_conv(x, w, width):
    acc = w[width - 1:width, :] * x
    for i in range(width - 1):
        acc = acc + w[i:i + 1, :] * _shift_down(x, width - 1 - i)
    return acc


def _seq_head(rs, col0):
    return pl.BlockSpec((rs, LANES), lambda j, b: (b, col0 + j))


def _live_rows(rs, pad_rows):
    return lax.broadcasted_iota(jnp.int32, (rs, 1), 0) >= pad_rows


def _qkv_fwd(proj, conv_w, kind, rs, pad_rows):
    n = proj.shape[0]
    col0 = {"q": 0, "k": HEADS, "v": 2 * HEADS}[kind]

    def body(p_ref, w_ref, o_ref):
        c = _causal_conv(p_ref[...], w_ref[...], GDN_CONV)
        s = c * _sigmoid(c)
        if kind != "v":
            s = s * lax.rsqrt(jnp.sum(s * s, axis=-1, keepdims=True) + EPS)
        if kind == "q":
            s = s * (HEAD_DIM ** -0.5)
        o_ref[...] = jnp.where(_live_rows(rs, pad_rows), s, 0.0)

    return pl.pallas_call(
        body, name="qkv_fwd_" + kind, out_shape=jax.ShapeDtypeStruct((n, GDN_WIDTH), F32), grid=(HEADS, n // rs),
        in_specs=[_seq_head(rs, col0), pl.BlockSpec((GDN_CONV, LANES), lambda j, b: (0, col0 + j))],
        out_specs=_seq_head(rs, 0), compiler_params=_params("parallel", "parallel"),
    )(proj, conv_w)


def _qkv_bwd(dy, proj, conv_w, kind, rs, pad_rows):
    n = proj.shape[0]
    col0 = {"q": 0, "k": HEADS, "v": 2 * HEADS}[kind]

    def body(dy_ref, p_ref, w_ref, dp_ref, dw_ref):
        pre = p_ref[...]
        w = w_ref[...]
        c = _causal_conv(pre, w, GDN_CONV)
        sg = _sigmoid(c)
        s = c * sg
        ds = dy_ref[...]
        if kind == "q":
            ds = ds * (HEAD_DIM ** -0.5)
        if kind != "v":
            r = lax.rsqrt(jnp.sum(s * s, axis=-1, keepdims=True) + EPS)
            sh = s * r
            ds = r * (ds - sh * jnp.sum(ds * sh, axis=-1, keepdims=True))
        dc = jnp.where(_live_rows(rs, pad_rows), ds * _dsilu(c, sg), 0.0)
        dpre = w[GDN_CONV - 1:GDN_CONV, :] * dc
        for i in range(GDN_CONV - 1):
            dpre = dpre + w[i:i + 1, :] * _shift_up(dc, GDN_CONV - 1 - i)
        dp_ref[...] = dpre.astype(dp_ref.dtype)
        dw = jnp.concatenate(
            [jnp.sum(dc * _shift_down(pre, GDN_CONV - 1 - i), axis=0, keepdims=True) for i in range(GDN_CONV)], axis=0)
        _accumulate(dw_ref, pl.program_id(1) == 0, dw)

    return pl.pallas_call(
        body, name="qkv_bwd_" + kind,
        out_shape=(jax.ShapeDtypeStruct((n, GDN_WIDTH), MXU_DTYPE), jax.ShapeDtypeStruct((GDN_CONV, GDN_WIDTH), F32)),
        grid=(HEADS, n // rs),
        in_specs=[_seq_head(rs, 0), _seq_head(rs, col0), pl.BlockSpec((GDN_CONV, LANES), lambda j, b: (0, col0 + j))],
        out_specs=(_seq_head(rs, 0), pl.BlockSpec((GDN_CONV, LANES), lambda j, b: (0, j))),
        compiler_params=_params("parallel", "arbitrary"),
    )(dy, proj, conv_w)


SC_COL = 4 * HEADS


def _sc_fwd(proj, conv_w, rs):
    n = proj.shape[0]

    def body(x_ref, b_ref, c_ref, w_ref, y_ref):
        y_ref[...] = (b_ref[...] * _causal_conv(c_ref[...] * x_ref[...], w_ref[...], SC_CONV)).astype(y_ref.dtype)

    return pl.pallas_call(
        body, name="sc_fwd", out_shape=jax.ShapeDtypeStruct((n, SC_WIDTH), MXU_DTYPE), grid=(HEADS, n // rs),
        in_specs=[_seq_head(rs, SC_COL), _seq_head(rs, SC_COL + 4), _seq_head(rs, SC_COL + 8),
                  pl.BlockSpec((SC_CONV, LANES), lambda j, b: (0, j))],
        out_specs=_seq_head(rs, 0), compiler_params=_params("parallel", "parallel"),
    )(proj, proj, proj, conv_w)


def _sc_bwd(dcat, proj, conv_w, rs):
    n = proj.shape[0]

    def body(dy_ref, x_ref, b_ref, c_ref, w_ref, dx_ref, db_ref, dc_ref, dw_ref):
        w = w_ref[...]
        x = x_ref[...]
        cc = c_ref[...]
        u = cc * x
        dy = dy_ref[...]
        db_ref[...] = (dy * _causal_conv(u, w, SC_CONV)).astype(db_ref.dtype)
        dcv = dy * b_ref[...]
        du = w[SC_CONV - 1:SC_CONV, :] * dcv
        for i in range(SC_CONV - 1):
            du = du + w[i:i + 1, :] * _shift_up(dcv, SC_CONV - 1 - i)
        dx_ref[...] = (du * cc).astype(dx_ref.dtype)
        dc_ref[...] = (du * x).astype(dc_ref.dtype)
        dw = jnp.concatenate(
            [jnp.sum(dcv * _shift_down(u, SC_CONV - 1 - i), axis=0, keepdims=True) for i in range(SC_CONV)], axis=0)
        _accumulate(dw_ref, pl.program_id(1) == 0, dw)

    piece = jax.ShapeDtypeStruct((n, SC_WIDTH), MXU_DTYPE)
    return pl.pallas_call(
        body, name="sc_bwd", out_shape=(piece, piece, piece, jax.ShapeDtypeStruct((SC_CONV, SC_WIDTH), F32)),
        grid=(HEADS, n // rs),
        in_specs=[_seq_head(rs, HEADS), _seq_head(rs, SC_COL), _seq_head(rs, SC_COL + 4), _seq_head(rs, SC_COL + 8),
                  pl.BlockSpec((SC_CONV, LANES), lambda j, b: (0, j))],
        out_specs=(_seq_head(rs, 0), _seq_head(rs, 0), _seq_head(rs, 0),
                   pl.BlockSpec((SC_CONV, LANES), lambda j, b: (0, j))),
        compiler_params=_params("parallel", "arbitrary"),
    )(dcat, proj, proj, proj, conv_w)


Z_COL = 3 * HEADS


def _gate_fwd(o, proj, gdn_norm, rs):
    n = proj.shape[0]

    def body(o_ref, z_ref, w_ref, y_ref):
        z = z_ref[...]
        y_ref[...] = (_rms_apply(o_ref[...], w_ref[...]) * z * _sigmoid(z)).astype(y_ref.dtype)

    return pl.pallas_call(
        body, name="gate_fwd", out_shape=jax.ShapeDtypeStruct((n, GDN_WIDTH), MXU_DTYPE), grid=(HEADS, n // rs),
        in_specs=[_seq_head(rs, 0), _seq_head(rs, Z_COL), pl.BlockSpec((1, LANES), lambda j, b: (0, 0))],
        out_specs=_seq_head(rs, 0), compiler_params=_params("parallel", "parallel"),
    )(o, proj, gdn_norm)


def _gate_bwd(dcat, o, proj, gdn_norm, rs):
    n = proj.shape[0]

    def body(dy_ref, o_ref, z_ref, w_ref, do_ref, dz_ref, dw_ref):
        z = z_ref[...]
        w = w_ref[...]
        o = o_ref[...]
        dy = dy_ref[...]
        s = _sigmoid(z)
        dz_ref[...] = (dy * _rms_apply(o, w) * _dsilu(z, s)).astype(dz_ref.dtype)
        do, dw = _rms_bwd(o, w, dy * z * s)
        do_ref[...] = do
        _accumulate(dw_ref, jnp.logical_and(pl.program_id(0) == 0, pl.program_id(1) == 0), dw)

    return pl.pallas_call(
        body, name="gate_bwd",
        out_shape=(jax.ShapeDtypeStruct((n, GDN_WIDTH), F32), jax.ShapeDtypeStruct((n, GDN_WIDTH), MXU_DTYPE),
                   jax.ShapeDtypeStruct((1, LANES), F32)),
        grid=(HEADS, n // rs),
        in_specs=[_seq_head(rs, 0), _seq_head(rs, 0), _seq_head(rs, Z_COL), pl.BlockSpec((1, LANES), lambda j, b: (0, 0))],
        out_specs=(_seq_head(rs, 0), _seq_head(rs, 0), pl.BlockSpec((1, LANES), lambda j, b: (0, 0))),
        compiler_params=_params("arbitrary", "arbitrary"),
    )(dcat, o, proj, gdn_norm)


def _dot(a, b):
    return jnp.dot(a.astype(MXU_DTYPE), b.astype(MXU_DTYPE), preferred_element_type=F32)


def _dot_nt(a, b):
    return lax.dot_general(a.astype(MXU_DTYPE), b.astype(MXU_DTYPE), (((1,), (1,)), ((), ())),
                           preferred_element_type=F32)


def _dot_tn(a, b):
    return lax.dot_general(a.astype(MXU_DTYPE), b.astype(MXU_DTYPE), (((0,), (0,)), ((), ())),
                           preferred_element_type=F32)


def _split(x):
    hi = x.astype(MXU_DTYPE)
    return hi, (x - hi.astype(F32)).astype(MXU_DTYPE)


def _dot_split(a, b):
    mm = functools.partial(jnp.dot, preferred_element_type=F32)
    return mm(a[0], b[0]) + (mm(a[0], b[1]) + mm(a[1], b[0]))


def _unit_lower_inverses(mats, eye):
    inv = [eye - a for a in mats]
    power = [_split(a) for a in mats]
    span = 2
    while span < CHUNK:
        power = [_split(_dot_split(p, p)) for p in power]
        inv = [i + _dot_split(_split(i), p) for i, p in zip(inv, power)]
        span *= 2
    return inv


def _chunk_masks():
    ii = lax.broadcasted_iota(jnp.int32, (CHUNK, CHUNK), 0)
    jj = lax.broadcasted_iota(jnp.int32, (CHUNK, CHUNK), 1)
    return ii, jj


def _chunk_decay(g_col, ii, jj):
    incl = ii >= jj
    g_row = jnp.sum(jnp.where(ii == jj, g_col, 0.0), axis=0, keepdims=True)
    gc_col = jnp.sum(jnp.where(incl, g_row, 0.0), axis=1, keepdims=True)
    gc_row = jnp.sum(jnp.where(ii <= jj, g_col, 0.0), axis=0, keepdims=True)
    g_total = jnp.sum(g_row, axis=1, keepdims=True)
    decay = jnp.where(incl, jnp.exp(jnp.where(incl, gc_col - gc_row, 0.0)), 0.0)
    return gc_col, g_total, decay


def _gdn_segments(rs, candidates):
    chunks = rs // CHUNK
    seg_chunks = _pick(chunks, candidates)
    return chunks, seg_chunks, chunks // seg_chunks


def _head_lanes(h):
    return slice(h * HEAD_DIM, (h + 1) * HEAD_DIM)


def _gdn_fwd(q, k, v, bg, rs, pieces):
    n = q.shape[0]
    batch = n // rs
    chunks, seg_chunks, segs = _gdn_segments(rs, (11, 8, 4, 2))
    seg_rows = seg_chunks * CHUNK
    chains = [(b, h) for b in range(batch) for h in range(HEADS)]
    each = lambda f, *lists: [f(*args) for args in zip(*lists)]
    count = len(pieces)

    def body(q_ref, k_ref, v_ref, bg_ref, *rest):
        w_refs, (o_ref, s_ref, t_ref), out_refs = rest[:count], rest[count:count + 3], rest[count + 3:2 * count + 3]
        state_ref, send_sems, recv_sems = rest[2 * count + 3:]
        gather = _gather_copies(w_refs, out_refs, send_sems, recv_sems)

        @pl.when(pl.program_id(0) == 0)
        def _():
            state_ref[...] = jnp.zeros_like(state_ref)
            for cp in gather[0]:
                cp.start()

        ii, jj = _chunk_masks()
        incl = ii >= jj
        eye = (ii == jj).astype(F32)

        def chunk(c, carry):
            rows = pl.ds(pl.multiple_of(c * CHUNK, CHUNK), CHUNK)
            bgc = [bg_ref[b, rows, :] for b in range(batch)]
            qc = [q_ref[b, rows, _head_lanes(h)] for b, h in chains]
            kc = [k_ref[b, rows, _head_lanes(h)] for b, h in chains]
            vc = [v_ref[b, rows, _head_lanes(h)] for b, h in chains]
            beta = [bgc[b][:, h:h + 1] for b, h in chains]
            state = [state_ref[b, h] for b, h in chains]
            dec = [_chunk_decay(bgc[b][:, HEADS + h:HEADS + h + 1], ii, jj) for b, h in chains]
            gc_col, g_total, decay = ([d[i] for d in dec] for i in range(3))
            kb = each(lambda x, y: x * y, kc, beta)
            a = each(lambda x, y, d: jnp.where(ii > jj, _dot_nt(x, y) * d, 0.0), kb, kc, decay)
            t_inv = _unit_lower_inverses(a, eye)
            eg = [jnp.exp(g) for g in gc_col]
            u = each(lambda t, x, y: _dot(t, x * y), t_inv, vc, beta)
            w = each(lambda t, x, e: _dot(t, x * e), t_inv, kb, eg)
            qk = each(lambda x, y, d: jnp.where(incl, _dot_nt(x, y) * d, 0.0), qc, kc, decay)
            v_new = each(lambda x, y, s: x - _dot(y, s), u, w, state)
            o = each(lambda x, e, s, m, vn: _dot(x * e, s) + _dot(m, vn), qc, eg, state, qk, v_new)
            new_state = each(lambda s, gt, x, g, vn: s * jnp.exp(gt) + _dot_tn(x * jnp.exp(gt - g), vn),
                             state, g_total, kc, gc_col, v_new)
            for i, (b, h) in enumerate(chains):
                s_ref[b, h, c] = state[i]
                t_ref[b, h, c] = t_inv[i]
                o_ref[b, rows, _head_lanes(h)] = o[i]
                state_ref[b, h] = new_state[i]
            return carry

        lax.fori_loop(0, seg_chunks, chunk, 0)

        @pl.when(pl.program_id(0) == segs - 1)
        def _():
            _gather_finish(gather)

    rows_spec = lambda width: pl.BlockSpec((batch, seg_rows, width), lambda s: (0, s, 0))
    per_chunk = lambda r, c: pl.BlockSpec((batch, HEADS, seg_chunks, r, c), lambda s: (0, 0, s, 0, 0))
    as_seqs = lambda a: a.reshape(batch, rs, a.shape[-1])
    sems = GATHER_SEMS * count
    o, states, t_invs, *gathered = pl.pallas_call(
        body, name="gdn_fwd",
        out_shape=(jax.ShapeDtypeStruct((batch, rs, GDN_WIDTH), F32),
                   jax.ShapeDtypeStruct((batch, HEADS, chunks, HEAD_DIM, HEAD_DIM), F32),
                   jax.ShapeDtypeStruct((batch, HEADS, chunks, CHUNK, CHUNK), F32))
        + tuple(jax.ShapeDtypeStruct((N_CHIPS,) + p.shape, p.dtype) for p in pieces),
        grid=(segs,),
        in_specs=[rows_spec(GDN_WIDTH), rows_spec(GDN_WIDTH), rows_spec(GDN_WIDTH), rows_spec(LANES)] + [_hbm()] * count,
        out_specs=(rows_spec(GDN_WIDTH), per_chunk(HEAD_DIM, HEAD_DIM), per_chunk(CHUNK, CHUNK)) + (_hbm(),) * count,
        scratch_shapes=[pltpu.VMEM((batch, HEADS, HEAD_DIM, HEAD_DIM), F32), pltpu.SemaphoreType.DMA((sems,)),
                        pltpu.SemaphoreType.DMA((sems,))],
        compiler_params=_params("arbitrary"),
    )(as_seqs(q), as_seqs(k), as_seqs(v), as_seqs(bg), *pieces)
    return o.reshape(n, GDN_WIDTH), states, t_invs, gathered


def _gdn_bwd(do, q, k, v, bg, states, t_invs, rs, parts):
    n = q.shape[0]
    batch = n // rs
    chunks, seg_chunks, segs = _gdn_segments(rs, (3, 4, 2))
    seg_rows = seg_chunks * CHUNK
    chains = [(b, h) for b in range(batch) for h in range(HEADS)]
    each = lambda f, *lists: [f(*args) for args in zip(*lists)]
    count = len(parts)

    def body(do_ref, q_ref, k_ref, v_ref, bg_ref, s_ref, t_ref, *rest):
        p_refs, (dq_ref, dk_ref, dv_ref, dbg_ref), got_refs = rest[:count], rest[count:count + 4], rest[count + 4:2 * count + 4]
        dstate_ref, send_sems, recv_sems = rest[2 * count + 4:]
        exchange = _chip_copies(p_refs, got_refs, send_sems, recv_sems)

        @pl.when(pl.program_id(0) == 0)
        def _():
            dstate_ref[...] = jnp.zeros_like(dstate_ref)
            for cp in exchange:
                cp.start()

        ii, jj = _chunk_masks()
        incl = ii >= jj
        strict = ii > jj
        lane = lax.broadcasted_iota(jnp.int32, (1, LANES), 1)

        def rowsum(x):
            return jnp.sum(x, axis=1, keepdims=True)

        def total(x):
            return jnp.sum(rowsum(x), axis=0, keepdims=True)

        def chunk(step, carry):
            c = seg_chunks - 1 - step
            rows = pl.ds(pl.multiple_of(c * CHUNK, CHUNK), CHUNK)
            bgc = [bg_ref[b, rows, :] for b in range(batch)]
            qc = [q_ref[b, rows, _head_lanes(h)] for b, h in chains]
            kc = [k_ref[b, rows, _head_lanes(h)] for b, h in chains]
            vc = [v_ref[b, rows, _head_lanes(h)] for b, h in chains]
            doc = [do_ref[b, rows, _head_lanes(h)] for b, h in chains]
            beta = [bgc[b][:, h:h + 1] for b, h in chains]
            state = [s_ref[b, h, c] for b, h in chains]
            t_inv = [t_ref[b, h, c] for b, h in chains]
            d_state = [dstate_ref[b, h] for b, h in chains]
            dec = [_chunk_decay(bgc[b][:, HEADS + h:HEADS + h + 1], ii, jj) for b, h in chains]
            gc_col, g_total, decay = ([d[i] for d in dec] for i in range(3))
            kb = each(lambda x, y: x * y, kc, beta)
            vb = each(lambda x, y: x * y, vc, beta)
            eg = [jnp.exp(g) for g in gc_col]
            kbg = each(lambda x, y: x * y, kb, eg)
            a = each(lambda x, y, d: jnp.where(strict, _dot_nt(x, y) * d, 0.0), kb, kc, decay)
            qk = each(lambda x, y, d: jnp.where(incl, _dot_nt(x, y) * d, 0.0), qc, kc, decay)
            w = each(_dot, t_inv, kbg)
            u = each(_dot, t_inv, vb)
            q_dec = each(lambda x, y: x * y, qc, eg)
            ek = each(lambda gt, g: jnp.exp(gt - g), g_total, gc_col)
            k_dec = each(lambda x, y: x * y, kc, ek)
            g_last = [jnp.exp(gt) for gt in g_total]
            v_new = each(lambda x, y, s: x - _dot(y, s), u, w, state)
            dv_new = each(lambda m, d, x, ds: _dot_tn(m, d) + _dot(x, ds), qk, doc, k_dec, d_state)
            dqk = each(lambda d, vn: jnp.where(incl, _dot_nt(d, vn), 0.0), doc, v_new)
            dq_dec = each(_dot_nt, doc, state)
            dk_dec = each(_dot_nt, v_new, d_state)
            dg_last = each(lambda s, ds: total(s * ds), state, d_state)
            new_d_state = each(lambda x, d, gl, ds, y, dvn: _dot_tn(x, d) + gl * ds - _dot_tn(y, dvn),
                               q_dec, doc, g_last, d_state, w, dv_new)
            dw = each(lambda dvn, s: -_dot_nt(dvn, s), dv_new, state)
            dt = each(lambda dvn, x, y, z: _dot_nt(dvn, x) + _dot_nt(y, z), dv_new, vb, dw, kbg)
            dvb = each(_dot_tn, t_inv, dv_new)
            dkbg = each(_dot_tn, t_inv, dw)
            t_dt = each(_dot_tn, t_inv, dt)
            da = each(lambda x, t: -jnp.where(strict, _dot_nt(x, t), 0.0), t_dt, t_inv)
            dm_a = each(lambda x, y: x * y, da, decay)
            dm_qk = each(lambda x, y: x * y, dqk, decay)
            e = each(lambda x, y, z, t: x * y + z * t, da, a, dqk, qk)
            dkb = each(lambda m, x, y, z: _dot(m, x) + y * z, dm_a, kc, dkbg, eg)
            dk = each(lambda m, x, m2, y, z, t, p, bt: _dot_tn(m, x) + _dot_tn(m2, y) + z * t + p * bt,
                      dm_a, kb, dm_qk, qc, dk_dec, ek, dkb, beta)
            dq = each(lambda m, x, y, z: _dot(m, x) + y * z, dm_qk, kc, dq_dec, eg)
            dbeta = each(lambda x, y, z, t: rowsum(x * y + z * t), dkb, kc, dvb, vc)
            dgc = each(lambda x, p, pd, r, rd, s, sd: rowsum(x) - rowsum(jnp.where(ii == jj, jnp.sum(x, axis=0, keepdims=True), 0.0))
                       + rowsum(p * pd - r * rd + s * sd), e, dq_dec, q_dec, dk_dec, k_dec, dkbg, kbg)
            d_total = each(lambda r, rd, x, gl: total(r * rd) + x * gl, dk_dec, k_dec, dg_last, g_last)
            dg = each(lambda x, t: rowsum(jnp.where(jj >= ii, jnp.sum(jnp.where(ii == jj, x, 0.0), axis=0, keepdims=True), 0.0)) + t,
                      dgc, d_total)
            dbg = [jnp.zeros((CHUNK, LANES), F32) for _ in range(batch)]
            for i, (b, h) in enumerate(chains):
                dstate_ref[b, h] = new_d_state[i]
                dk_ref[b, rows, _head_lanes(h)] = dk[i]
                dq_ref[b, rows, _head_lanes(h)] = dq[i]
                dv_ref[b, rows, _head_lanes(h)] = dvb[i] * beta[i]
                dbg[b] = dbg[b] + jnp.where(lane == h, dbeta[i], 0.0) + jnp.where(lane == HEADS + h, dg[i], 0.0)
            for b in range(batch):
                dbg_ref[b, rows, :] = dbg[b]
            return carry

        lax.fori_loop(0, seg_chunks, chunk, 0)

        @pl.when(pl.program_id(0) == segs - 1)
        def _():
            for cp in exchange:
                cp.wait_recv()
            for cp in exchange:
                cp.wait_send()

    rows_spec = lambda width: pl.BlockSpec((batch, seg_rows, width), lambda s: (0, segs - 1 - s, 0))
    per_chunk = lambda r, c: pl.BlockSpec((batch, HEADS, seg_chunks, r, c), lambda s: (0, 0, segs - 1 - s, 0, 0))
    as_seqs = lambda a: a.reshape(batch, rs, a.shape[-1])
    grad = jax.ShapeDtypeStruct((batch, rs, GDN_WIDTH), F32)
    wide = rows_spec(GDN_WIDTH)
    dq, dk, dv, dbg, *got = pl.pallas_call(
        body, name="gdn_bwd",
        out_shape=(grad, grad, grad, jax.ShapeDtypeStruct((batch, rs, LANES), F32))
        + tuple(jax.ShapeDtypeStruct((3,) + p.shape[1:], p.dtype) for p in parts),
        grid=(segs,),
        in_specs=[wide, wide, wide, wide, rows_spec(LANES), per_chunk(HEAD_DIM, HEAD_DIM), per_chunk(CHUNK, CHUNK)]
        + [_hbm()] * count,
        out_specs=(wide, wide, wide, rows_spec(LANES)) + (_hbm(),) * count,
        scratch_shapes=[pltpu.VMEM((batch, HEADS, HEAD_DIM, HEAD_DIM), F32), pltpu.SemaphoreType.DMA((3 * count,)),
                        pltpu.SemaphoreType.DMA((3 * count,))],
        compiler_params=_params("arbitrary"),
    )(as_seqs(do), as_seqs(q), as_seqs(k), as_seqs(v), as_seqs(bg), states, t_invs, *parts)
    return dq.reshape(n, GDN_WIDTH), dk.reshape(n, GDN_WIDTH), dv.reshape(n, GDN_WIDTH), dbg.reshape(n, LANES), got


def _lane_vec(vals, offset):
    k = vals.shape[1]
    return jnp.pad(vals, ((0, 0), (offset, LANES - offset - k)))


LATER = ("w_out", "w_gate", "w_up", "w_down")


def _halves(a):
    return a.reshape(a.shape[:-2] + (2, a.shape[-2] // 2, a.shape[-1]))


def _local_step(x, target, meta, norms, w_in_t, conv_qkv, a_log, dt_bias, gdn_norm, conv_sc, later_shards, core_arg):
    batch, seq, d = x.shape
    tokens = N_META + seq
    pad_rows = (-tokens) % CHUNK
    rs = tokens + pad_rows
    x_offset = pad_rows + N_META
    n = batch * rs
    w_mix_pre, w_mix_post, w_ffn_pre, w_ffn_post = norms

    head = jnp.concatenate([jnp.zeros((pad_rows, d), F32), meta], axis=0)
    h0 = jnp.concatenate([jnp.broadcast_to(head[None], (batch, x_offset, d)), x], axis=1).reshape(n, d)
    target_p = jnp.pad(target, ((0, 0), (x_offset, 0), (0, 0))).reshape(n, d)
    a_log_l = _lane_vec(a_log, HEADS)
    dt_bias_l = _lane_vec(dt_bias, HEADS)

    u1 = _rms_fwd(h0, w_mix_pre, "rms_mix_pre")
    proj = _mm(u1, w_in_t, "nt", F32, "mm_proj")
    q = _qkv_fwd(proj, conv_qkv, "q", rs, pad_rows)
    k = _qkv_fwd(proj, conv_qkv, "k", rs, pad_rows)
    v = _qkv_fwd(proj, conv_qkv, "v", rs, pad_rows)
    bg = _gates_fwd(proj, a_log_l, dt_bias_l, rs, pad_rows)
    o, states, t_invs, gathered = _gdn_fwd(q, k, v, bg, rs, later_shards)
    w_out, w_gate_t, w_up_t, w_down = (a.reshape(-1, d) for a in gathered)
    o_gated = _gate_fwd(o, proj, gdn_norm, rs)
    y_sc = _sc_fwd(proj, conv_sc, rs)
    cat = jnp.concatenate([o_gated, y_sc], axis=1)
    mix = _mm(cat, w_out, "nn", F32, "mm_mix")
    h1, u2 = _mix_residual(h0, mix, w_mix_post, w_ffn_pre)
    gate = _mm(u2, w_gate_t, "nt", F32, "mm_gate")
    up = _mm(u2, w_up_t, "nt", F32, "mm_up")
    act = _swiglu_act(gate, up)
    ffn = _mm(act, w_down, "nn", F32, "mm_down")

    dh2, dffn, d_ffn_post, sq = _loss_head(h1, ffn, w_ffn_post, target_p, rs, x_offset)
    dact = _mm(dffn, w_down, "nt", F32, "mm_dact")
    d_w_down = _mm(act, dffn, "tn", F32, "mm_dw_down")
    dgate, dup = _swiglu_bwd(gate, up, dact)
    d_w_gate_t = _mm(dgate, u2, "tn", F32, "mm_dw_gate")
    d_w_up_t = _mm(dup, u2, "tn", F32, "mm_dw_up")
    du2 = _mm(dup, w_up_t, "nn", F32, "mm_du2_up", init=_mm(dgate, w_gate_t, "nn", F32, "mm_du2_gate"))
    dh1, dmix, d_ffn_pre, d_mix_post = _mid_bwd(h1, mix, w_mix_post, w_ffn_pre, dh2, du2)
    dcat = _mm(dmix, w_out, "nt", F32, "mm_dcat")
    d_w_out = _mm(cat, dmix, "tn", F32, "mm_dw_out")
    do, dz, d_gdn_norm = _gate_bwd(dcat, o, proj, gdn_norm, rs)
    dscx, dscb, dscc, d_conv_sc = _sc_bwd(dcat, proj, conv_sc, rs)
    by_chip = [_halves(g.reshape(N_CHIPS, -1, d)) for g in (d_w_out, d_w_gate_t, d_w_up_t, d_w_down)]
    sums = [_add_sibling(a, b, core_arg, name) for name, a, b in zip(LATER, by_chip, _exchange_siblings(by_chip))]
    dq, dk, dv, dbg, got_chips = _gdn_bwd(do, q, k, v, bg, states, t_invs, rs, [send for _, send in sums])
    dpq, dwq = _qkv_bwd(dq, proj, conv_qkv, "q", rs, pad_rows)
    dpk, dwk = _qkv_bwd(dk, proj, conv_qkv, "k", rs, pad_rows)
    dpv, dwv = _qkv_bwd(dv, proj, conv_qkv, "v", rs, pad_rows)
    d_conv_qkv = jnp.concatenate([dwq, dwk, dwv], axis=1)
    dba, d_a_log_l, d_dt_bias_l = _gates_bwd(proj, dbg, a_log_l, dt_bias_l, rs, pad_rows)
    dproj = jnp.concatenate([dpq, dpk, dpv, dz, dscx, dscb, dscc, dba], axis=1)
    d_w_in_t = _mm(dproj, u1, "tn", F32, "mm_dw_in")
    du1 = _mm(dproj, w_in_t, "nn", F32, "mm_du1")
    dh0, d_mix_pre = _in_bwd(h0, w_mix_pre, dh1, du1)

    dh0 = dh0.reshape(batch, rs, d)
    grads = dict(
        meta_tokens=jnp.sum(dh0[:, pad_rows:x_offset], axis=0),
        mix_pre_norm=d_mix_pre, mix_post_norm=d_mix_post, ffn_pre_norm=d_ffn_pre, ffn_post_norm=d_ffn_post,
        w_in=d_w_in_t, conv_qkv=d_conv_qkv,
        a_log=d_a_log_l[:, HEADS:2 * HEADS], dt_bias=d_dt_bias_l[:, HEADS:2 * HEADS],
        gdn_norm=d_gdn_norm, conv_sc=d_conv_sc,
    )
    return sq, dh0[:, x_offset:], grads, [(part, got) for (part, _), got in zip(sums, got_chips)]


def _to_padded_in(w_in_t):
    lo, hi = 4 * GDN_WIDTH, 4 * GDN_WIDTH + 2 * HEADS
    pad = jnp.zeros((IN_PAD - IN_WIDTH, w_in_t.shape[1]), w_in_t.dtype)
    return jnp.concatenate([w_in_t[:lo], w_in_t[hi:], w_in_t[lo:hi], pad], axis=0)


def _from_padded_in(w_in_p):
    lo, hi = 4 * GDN_WIDTH, IN_WIDTH - 2 * HEADS
    return jnp.concatenate([w_in_p[:lo], w_in_p[hi:IN_WIDTH], w_in_p[lo:hi]], axis=0)


MATRICES = ("w_in", "w_out", "w_gate", "w_up", "w_down")
IN_SHARD = IN_WIDTH // N_CHIPS
IN_SHARD_PAD = 928
SMALL_ROWS = 48
REDUCE_ROWS = 224


def _flatten(pieces, rows):
    flat = jnp.concatenate([p.reshape(-1) for p in pieces])
    return jnp.pad(flat, (0, rows * LANES - flat.shape[0])).reshape(rows, LANES)


def _unflatten(flat, shapes):
    flat = flat.reshape(-1)
    out, at = [], 0
    for shape in shapes:
        size = shape[0] * shape[1]
        out.append(flat[at:at + size].reshape(shape))
        at += size
    return out


def _stored(w):
    return jnp.transpose(w, (2, 0, 1)).reshape(-1, LANES)


def _unstored(flat, d):
    return jnp.transpose(flat.reshape(-1, 1, d), (1, 2, 0))


def _hbm():
    return pl.BlockSpec(memory_space=pl.ANY)


def _place():
    x, y, c = lax.axis_index("x"), lax.axis_index("y"), lax.axis_index("c")
    chips = ((1 - x, y), (x, 1 - y), (1 - x, 1 - y))
    return x, y, c, chips


def _remote(src, dst, send_sems, recv_sems, k, to):
    return pltpu.make_async_remote_copy(src_ref=src, dst_ref=dst, send_sem=send_sems.at[k], recv_sem=recv_sems.at[k],
                                        device_id=to, device_id_type=MESH)


GATHER_SEMS = 7


def _gather_copies(w_refs, out_refs, send_sems, recv_sems):
    x, y, c, chips = _place()
    mine = 2 * x + y
    sibling = (x, y, 1 - c)
    copy = functools.partial(_remote, send_sems=send_sems, recv_sems=recv_sems)
    direct, landed, passing, from_sibling = [], [], [], []
    for i, (w, o) in enumerate(zip(w_refs, out_refs)):
        k = GATHER_SEMS * i
        direct.append(copy(w, o.at[mine], k=k, to=sibling))
        from_sibling.append(copy(w, o.at[mine], k=k, to=sibling))
        for j, (cx, cy) in enumerate(chips):
            theirs = 2 * cx + cy
            direct.append(copy(w.at[c], o.at[mine, c], k=k + 1 + j, to=(cx, cy, c)))
            landed.append(copy(w.at[c], o.at[theirs, c], k=k + 1 + j, to=sibling))
            passing.append(copy(o.at[theirs, c], o.at[theirs, c], k=k + 4 + j, to=sibling))
            from_sibling.append(copy(w.at[c], o.at[theirs, 1 - c], k=k + 4 + j, to=sibling))
    return direct, landed, passing, from_sibling


def _gather_finish(copies):
    direct, landed, passing, from_sibling = copies
    for arrival, forward in zip(landed, passing):
        arrival.wait_recv()
        forward.start()
    for arrival in from_sibling:
        arrival.wait_recv()
    for cp in direct + passing:
        cp.wait_send()


def _gather_weights(pieces, s_flat):
    count = len(pieces)

    def body(*refs):
        w_refs, s_ref = refs[:count], refs[count]
        out_refs, sall_ref = refs[count + 1:2 * count + 1], refs[2 * count + 1]
        send_sems, recv_sems, local_sem = refs[2 * count + 2:]
        x, y, c, chips = _place()
        mine = 2 * x + y
        own_s = pltpu.make_async_copy(s_ref, sall_ref.at[mine], local_sem)
        own_s.start()
        small = [_remote(s_ref, sall_ref.at[mine], send_sems, recv_sems, GATHER_SEMS * count + j, (cx, cy, c))
                 for j, (cx, cy) in enumerate(chips)]
        copies = _gather_copies(w_refs, out_refs, send_sems, recv_sems)
        for cp in small + copies[0]:
            cp.start()
        _gather_finish(copies)
        for cp in small:
            cp.wait_recv()
        for cp in small:
            cp.wait_send()
        own_s.wait()

    sems = GATHER_SEMS * count + 3
    return pl.pallas_call(
        body, name="gather_weights",
        out_shape=tuple(jax.ShapeDtypeStruct((N_CHIPS,) + p.shape, p.dtype) for p in pieces)
        + (jax.ShapeDtypeStruct((N_CHIPS,) + s_flat.shape, s_flat.dtype),),
        in_specs=[_hbm()] * (count + 1), out_specs=(_hbm(),) * (count + 1),
        scratch_shapes=[pltpu.SemaphoreType.DMA((sems,)), pltpu.SemaphoreType.DMA((sems,)), pltpu.SemaphoreType.DMA],
    )(*pieces, s_flat)


def _exchange_siblings(grads, small=None):
    count = len(grads)
    extra = 0 if small is None else 1

    def body(*refs):
        g_refs = refs[:count]
        got_refs = refs[count + extra:2 * count + extra]
        send_sems, recv_sems = refs[2 * (count + extra):2 * (count + extra) + 2]
        x, y, c, _ = _place()
        copies = [_remote(g.at[:, 1 - c], got, send_sems, recv_sems, i, (x, y, 1 - c))
                  for i, (g, got) in enumerate(zip(g_refs, got_refs))]
        if small is not None:
            s_ref, sall_ref, local_sem = refs[count], refs[2 * count + 1], refs[-1]
            me = 4 * x + 2 * y + c
            own = pltpu.make_async_copy(s_ref, sall_ref.at[me], local_sem)
            own.start()
            for k in range(7):
                dx, dy, dc = ((k + 1) >> 2) & 1, ((k + 1) >> 1) & 1, (k + 1) & 1
                peer = (1 - x if dx else x, 1 - y if dy else y, 1 - c if dc else c)
                copies.append(_remote(s_ref, sall_ref.at[me], send_sems, recv_sems, count + k, peer))
        for cp in copies:
            cp.start()
        for cp in copies:
            cp.wait_recv()
        for cp in copies:
            cp.wait_send()
        if small is not None:
            own.wait()

    sems = count + 7 * extra
    return pl.pallas_call(
        body, name="exchange_siblings" + ("" if small is None else "_small"),
        out_shape=tuple(jax.ShapeDtypeStruct((g.shape[0],) + g.shape[2:], F32) for g in grads)
        + (() if small is None else (jax.ShapeDtypeStruct((8,) + small.shape, F32),)),
        in_specs=[_hbm()] * (count + extra), out_specs=(_hbm(),) * (count + extra),
        scratch_shapes=[pltpu.SemaphoreType.DMA((sems,)), pltpu.SemaphoreType.DMA((sems,))]
        + ([] if small is None else [pltpu.SemaphoreType.DMA]),
    )(*grads, *(() if small is None else (small,)))


def _chip_copies(p_refs, got_refs, send_sems, recv_sems):
    x, y, c, chips = _place()
    return [_remote(p.at[2 * cx + cy], got.at[j], send_sems, recv_sems, 3 * i + j, (cx, cy, c))
            for i, (p, got) in enumerate(zip(p_refs, got_refs)) for j, (cx, cy) in enumerate(chips)]


def _exchange_chips(parts):
    count = len(parts)

    def body(*refs):
        copies = _chip_copies(refs[:count], refs[count:2 * count], *refs[2 * count:])
        for cp in copies:
            cp.start()
        for cp in copies:
            cp.wait_recv()
        for cp in copies:
            cp.wait_send()

    return pl.pallas_call(
        body, name="exchange_chips", out_shape=tuple(jax.ShapeDtypeStruct((3,) + p.shape[1:], p.dtype) for p in parts),
        in_specs=[_hbm()] * count, out_specs=(_hbm(),) * count,
        scratch_shapes=[pltpu.SemaphoreType.DMA((3 * count,)), pltpu.SemaphoreType.DMA((3 * count,))],
    )(*parts)


def _share_halves(halves):
    count = len(halves)

    def body(*refs):
        h_refs, full_refs = refs[:count], refs[count:2 * count]
        send_sems, recv_sems = refs[2 * count:]
        x, y, c, _ = _place()
        copies = [pltpu.make_async_remote_copy(src_ref=h.at[c], dst_ref=full.at[c], send_sem=send_sems.at[i],
                                               recv_sem=recv_sems.at[i], device_id=(x, y, 1 - c), device_id_type=MESH)
                  for i, (h, full) in enumerate(zip(h_refs, full_refs))]
        for cp in copies:
            cp.start()
        for cp in copies:
            cp.wait_recv()
        for cp in copies:
            cp.wait_send()

    return pl.pallas_call(
        body, name="share_halves", out_shape=tuple(jax.ShapeDtypeStruct(h.shape, h.dtype) for h in halves),
        in_specs=[_hbm()] * count, out_specs=(_hbm(),) * count, input_output_aliases={i: i for i in range(count)},
        scratch_shapes=[pltpu.SemaphoreType.DMA((count,)), pltpu.SemaphoreType.DMA((count,))],
    )(*halves)


def _add_sibling(grad, got, core, name):
    chips, _, rows, cols = grad.shape

    def body(core_ref, g_ref, r_ref, sum_ref, send_ref):
        s = g_ref[...] + r_ref[...]
        sum_ref[...] = s
        send_ref[...] = s.astype(send_ref.dtype)

    block = pl.BlockSpec((None, rows, cols), lambda p, core_ref: (p, 0, 0))
    return pl.pallas_call(
        body, name="add_sibling_" + name,
        out_shape=(jax.ShapeDtypeStruct((chips, rows, cols), F32), jax.ShapeDtypeStruct((chips, rows, cols), BF16)),
        grid_spec=pltpu.PrefetchScalarGridSpec(
            num_scalar_prefetch=1, grid=(chips,),
            in_specs=[pl.BlockSpec((None, None, rows, cols), lambda p, core_ref: (p, core_ref[0], 0, 0)), block],
            out_specs=(block, block)),
        compiler_params=_params("parallel"),
    )(core, grad, got)


def _add_chips(part, got, chip_core, name):
    _, rows, cols = part.shape
    tr = rows // 2 if rows % 32 == 0 else rows

    def body(place_ref, p_ref, r_ref, o_ref):
        o_ref[...] = ((p_ref[...] + r_ref[0].astype(F32)) + r_ref[1].astype(F32)) + r_ref[2].astype(F32)

    return pl.pallas_call(
        body, name="add_chips_" + name, out_shape=jax.ShapeDtypeStruct((2, rows, cols), F32),
        grid_spec=pltpu.PrefetchScalarGridSpec(
            num_scalar_prefetch=1, grid=(rows // tr,),
            in_specs=[pl.BlockSpec((None, tr, cols), lambda i, place_ref: (place_ref[0], i, 0)),
                      pl.BlockSpec((3, tr, cols), lambda i, place_ref: (0, i, 0))],
            out_specs=pl.BlockSpec((None, tr, cols), lambda i, place_ref: (place_ref[1], i, 0))),
        compiler_params=_params("parallel"),
    )(chip_core, part, got)


def _sum_devices(small_all):
    def body(s_ref, o_ref):
        acc = s_ref[0]
        for k in range(1, 8):
            acc = acc + s_ref[k]
        o_ref[...] = acc

    return pl.pallas_call(body, name="sum_devices", out_shape=jax.ShapeDtypeStruct(small_all.shape[1:], F32))(small_all)


def _adamw(w, g, m, v, name):
    rows, cols = w.shape
    tr = _pick(rows, (3592, 256, 352, 176, 128, 64, 32, 16, 8))

    def body(w_ref, g_ref, m_ref, v_ref, d_ref, nm_ref, nv_ref):
        g = g_ref[...]
        m = ADAM_B1 * m_ref[...] + (1.0 - ADAM_B1) * g
        v = ADAM_B2 * v_ref[...] + (1.0 - ADAM_B2) * (g * g)
        m_hat = m / (1.0 - ADAM_B1 ** ADAM_STEP)
        v_hat = v / (1.0 - ADAM_B2 ** ADAM_STEP)
        d_ref[...] = -ADAM_LR * (m_hat / (jnp.sqrt(v_hat) + ADAM_EPS) + ADAM_WD * w_ref[...])
        nm_ref[...] = m
        nv_ref[...] = v

    block = pl.BlockSpec((tr, cols), lambda i: (i, 0))
    shape = jax.ShapeDtypeStruct((rows, cols), F32)
    return pl.pallas_call(
        body, name="adamw_" + name, out_shape=(shape, shape, shape), grid=(rows // tr,),
        in_specs=[block] * 4, out_specs=(block,) * 3, compiler_params=_params("parallel"),
    )(w, g, m, v)


WEIGHTS = ("meta_tokens", "mix_pre_norm", "mix_post_norm", "ffn_pre_norm", "ffn_post_norm", "w_in", "conv_qkv", "a_log",
           "dt_bias", "gdn_norm", "conv_sc", "w_out", "w_gate", "w_up", "w_down")


def kernel(x, meta_tokens, mix_pre_norm, mix_post_norm, ffn_pre_norm, ffn_post_norm, w_in, conv_qkv, a_log, dt_bias, gdn_norm, conv_sc, w_out, w_gate, w_up, w_down, loss_target, m_meta_tokens, m_mix_pre_norm, m_mix_post_norm, m_ffn_pre_norm, m_ffn_post_norm, m_w_in, m_conv_qkv, m_a_log, m_dt_bias, m_gdn_norm, m_conv_sc, m_w_out, m_w_gate, m_w_up, m_w_down, v_meta_tokens, v_mix_pre_norm, v_mix_post_norm, v_ffn_pre_norm, v_ffn_post_norm, v_w_in, v_conv_qkv, v_a_log, v_dt_bias, v_gdn_norm, v_conv_sc, v_w_out, v_w_gate, v_w_up, v_w_down):
    d = x.shape[-1]
    two_d = lambda a: a.reshape(a.shape[-2:])
    weights = dict(zip(WEIGHTS, (meta_tokens, mix_pre_norm, mix_post_norm, ffn_pre_norm, ffn_post_norm, w_in, conv_qkv, a_log,
                                 dt_bias, gdn_norm, conv_sc, w_out, w_gate, w_up, w_down)))
    m_in = dict(zip(WEIGHTS, (m_meta_tokens, m_mix_pre_norm, m_mix_post_norm, m_ffn_pre_norm, m_ffn_post_norm, m_w_in, m_conv_qkv,
                              m_a_log, m_dt_bias, m_gdn_norm, m_conv_sc, m_w_out, m_w_gate, m_w_up, m_w_down)))
    v_in = dict(zip(WEIGHTS, (v_meta_tokens, v_mix_pre_norm, v_mix_post_norm, v_ffn_pre_norm, v_ffn_post_norm, v_w_in, v_conv_qkv,
                              v_a_log, v_dt_bias, v_gdn_norm, v_conv_sc, v_w_out, v_w_gate, v_w_up, v_w_down)))
    core = lax.axis_index("c")
    chip = 2 * lax.axis_index("x") + lax.axis_index("y")
    core_arg = core.reshape(1).astype(jnp.int32)
    chip_core = jnp.stack([chip, core]).astype(jnp.int32)
    small_shapes = [two_d(weights[n]).shape for n in ("conv_qkv", "conv_sc", "meta_tokens")]
    whole = lambda a: a.reshape(a.shape[:-3] + (2 * a.shape[-2], d))

    def stored(params, n):
        if n == "w_in":
            return _stored(params[n]).reshape(IN_SHARD, d)
        return two_d(params[n]).T if n in ("w_gate", "w_up") else two_d(params[n])

    shard = {n: stored(weights, n).astype(MXU_DTYPE) for n in MATRICES}
    shard["w_in"] = jnp.pad(shard["w_in"], ((0, IN_SHARD_PAD - IN_SHARD), (0, 0)))
    s_flat = _flatten([two_d(weights[n]) for n in ("conv_qkv", "conv_sc", "meta_tokens")], SMALL_ROWS)
    w_in_all, s_all = _gather_weights([_halves(shard["w_in"])], s_flat)
    w_in_t = _to_padded_in(whole(w_in_all)[:, :IN_SHARD].reshape(IN_WIDTH, d))
    small_chip = [_unflatten(s_all[p], small_shapes) for p in range(N_CHIPS)]
    conv_qkv_full, conv_sc_full, meta_full = (jnp.concatenate([small_chip[p][i] for p in range(N_CHIPS)], axis=1)
                                              for i in range(3))

    sq, grad_x, g, later = _local_step(
        x, loss_target, meta_full, (mix_pre_norm, mix_post_norm, ffn_pre_norm, ffn_post_norm), w_in_t, conv_qkv_full, a_log,
        dt_bias, gdn_norm, conv_sc_full, [_halves(shard[n]) for n in LATER], core_arg)

    g_in = jnp.pad(_from_padded_in(g["w_in"]).reshape(N_CHIPS, IN_SHARD, d), ((0, 0), (0, IN_SHARD_PAD - IN_SHARD), (0, 0)))
    scalars = jnp.concatenate([g["a_log"], g["dt_bias"], sq], axis=1)
    small = _flatten([g["mix_pre_norm"], g["mix_post_norm"], g["ffn_pre_norm"], g["ffn_post_norm"],
                      jnp.pad(scalars, ((0, 0), (0, LANES - scalars.shape[1]))), g["gdn_norm"], g["conv_qkv"], g["conv_sc"],
                      g["meta_tokens"]], REDUCE_ROWS)
    got_sibling, small_all = _exchange_siblings([_halves(g_in)], small)
    part_in, send_in = _add_sibling(_halves(g_in), got_sibling, core_arg, "w_in")
    sums = dict(zip(LATER, later), w_in=(part_in, _exchange_chips([send_in])[0]))
    totals = [_add_chips(*sums[n], chip_core, n) for n in MATRICES]
    grads = {n: whole(a) for n, a in zip(MATRICES, _share_halves(totals))}
    reduced = _sum_devices(small_all)

    r = reduced.reshape(-1)
    at = 0
    for n in ("mix_pre_norm", "mix_post_norm", "ffn_pre_norm", "ffn_post_norm"):
        grads[n] = r[at:at + d].reshape(1, d)
        at += d
    grads["a_log"] = r[at:at + HEADS].reshape(1, HEADS)
    grads["dt_bias"] = r[at + HEADS:at + 2 * HEADS].reshape(1, HEADS)
    loss = (0.5 / d) * r[at + 2 * HEADS]
    at += LANES
    grads["gdn_norm"] = r[at:at + HEAD_DIM].reshape(1, HEAD_DIM)
    at += HEAD_DIM
    for n, shape in (("conv_qkv", (GDN_CONV, 3 * GDN_WIDTH)), ("conv_sc", (SC_CONV, SC_WIDTH)), ("meta_tokens", (N_META, d))):
        full_grad = r[at:at + shape[0] * shape[1]].reshape(shape)
        at += shape[0] * shape[1]
        width = shape[1] // N_CHIPS
        grads[n] = lax.dynamic_slice_in_dim(full_grad, chip * width, width, axis=1)

    outs = [[], [], [], []]
    for n in WEIGHTS:
        shape = weights[n].shape
        grad = grads[n]
        if n == "w_in":
            to_kernel, from_kernel = _stored, functools.partial(_unstored, d=d)
            grad = grad[:IN_SHARD].reshape(-1, LANES)
        elif n in ("w_gate", "w_up"):
            to_kernel, from_kernel = (lambda a: two_d(a).T), (lambda a: a.T.reshape(shape))
        else:
            to_kernel, from_kernel = two_d, (lambda a: a.reshape(shape))
        delta, new_m, new_v = _adamw(to_kernel(weights[n]), grad, to_kernel(m_in[n]), to_kernel(v_in[n]), n)
        for out, a in zip(outs, (grad, delta, new_m, new_v)):
            out.append(from_kernel(a))
    return (loss, grad_x, *outs[0], *outs[1], *outs[2], *outs[3])
```

```python
import functools

import jax
import jax.numpy as jnp
from jax import lax
from jax.experimental import pallas as pl
from jax.experimental.pallas import tpu as pltpu

F32 = jnp.float32
BF16 = jnp.bfloat16
MXU_DTYPE = jnp.bfloat16
MESH = pl.DeviceIdType.MESH

D_MODEL = 1024
N_META = 16
HEADS = 4
HEAD_DIM = 128
GDN_WIDTH = HEADS * HEAD_DIM
GDN_CONV = 4
CHUNK = 64
SC_WIDTH = D_MODEL - GDN_WIDTH
SC_CONV = 3
D_FF = 2816
IN_WIDTH = 4 * GDN_WIDTH + 2 * HEADS + 3 * SC_WIDTH
IN_PAD = 3840
BA_COL = (4 * GDN_WIDTH + 3 * SC_WIDTH) // 128
EPS = 1e-6
LANES = 128
N_CHIPS = 4
VMEM_LIMIT = 48 * 2 ** 20

ADAM_LR = 0.001
ADAM_B1 = 0.9
ADAM_B2 = 0.999
ADAM_EPS = 1e-08
ADAM_WD = 0.01
ADAM_STEP = 10


def _pick(n, candidates):
    for c in candidates:
        if n % c == 0:
            return c
    return n


def _row_tile(n):
    return _pick(n, (352, 256, 176, 128, 64, 32, 16, 8))


def _params(*sem):
    return pltpu.CompilerParams(dimension_semantics=sem, vmem_limit_bytes=VMEM_LIMIT)


def _sigmoid(x):
    return 1.0 / (1.0 + jnp.exp(-x))


def _softplus(x):
    return jnp.maximum(x, 0.0) + jnp.log(1.0 + jnp.exp(-jnp.abs(x)))


def _dsilu(x, s):
    return s * (1.0 + x * (1.0 - s))


def _mm(a, b, mode, out_dtype, name, init=None, tiles=None):
    if mode == "tn":
        k_dim, m_dim = a.shape
    else:
        m_dim, k_dim = a.shape
    n_dim = b.shape[0] if mode == "nt" else b.shape[1]
    tm = _pick(m_dim, (1408, 1280, 1024, 512, 256, 128) if mode == "tn" else (1056, 1024, 704, 512, 256, 128))
    tn = _pick(n_dim, (1408, 1280, 1024, 768, 512, 256, 128))
    tk = _pick(k_dim, (1408, 1280, 1056, 1024, 512, 256, 128))
    if tiles is not None:
        tm, tn, tk = tiles
    nk = k_dim // tk
    if mode == "nn":
        a_spec = pl.BlockSpec((tm, tk), lambda i, j, k: (i, k))
        b_spec = pl.BlockSpec((tk, tn), lambda i, j, k: (k, j))
        dims = (((1,), (0,)), ((), ()))
    elif mode == "nt":
        a_spec = pl.BlockSpec((tm, tk), lambda i, j, k: (i, k))
        b_spec = pl.BlockSpec((tn, tk), lambda i, j, k: (j, k))
        dims = (((1,), (1,)), ((), ()))
    else:
        a_spec = pl.BlockSpec((tk, tm), lambda i, j, k: (k, i))
        b_spec = pl.BlockSpec((tk, tn), lambda i, j, k: (k, j))
        dims = (((0,), (0,)), ((), ()))

    out_spec = pl.BlockSpec((tm, tn), lambda i, j, k: (i, j))

    def body(a_ref, b_ref, *rest):
        o_ref, acc_ref = rest[-2:]
        k = pl.program_id(2)
        p = lax.dot_general(a_ref[...], b_ref[...], dims, preferred_element_type=F32)

        @pl.when(k == 0)
        def _():
            acc_ref[...] = p if init is None else rest[0][...] + p

        @pl.when(k > 0)
        def _():
            acc_ref[...] += p

        @pl.when(k == nk - 1)
        def _():
            o_ref[...] = acc_ref[...].astype(out_dtype)

    return pl.pallas_call(
        body, name=name,
        out_shape=jax.ShapeDtypeStruct((m_dim, n_dim), out_dtype),
        grid=(m_dim // tm, n_dim // tn, nk),
        in_specs=[a_spec, b_spec] + ([] if init is None else [out_spec]),
        out_specs=out_spec,
        scratch_shapes=[pltpu.VMEM((tm, tn), F32)],
        compiler_params=_params("parallel", "parallel", "arbitrary"),
    )(*((a, b) if init is None else (a, b, init)))


def _rms_apply(x, w):
    r = lax.rsqrt(jnp.mean(x * x, axis=-1, keepdims=True) + EPS)
    return x * r * w


def _rms_bwd(x, w, dy):
    r = lax.rsqrt(jnp.mean(x * x, axis=-1, keepdims=True) + EPS)
    xh = x * r
    dyw = dy * w
    dx = r * (dyw - xh * jnp.mean(dyw * xh, axis=-1, keepdims=True))
    return dx, jnp.sum(dy * xh, axis=0, keepdims=True)


def _accumulate(ref, first, value):
    @pl.when(first)
    def _():
        ref[...] = value

    @pl.when(jnp.logical_not(first))
    def _():
        ref[...] += value


def _rows(tr, width):
    return pl.BlockSpec((tr, width), lambda i: (i, 0))


def _vec(width):
    return pl.BlockSpec((1, width), lambda i: (0, 0))


def _rms_fwd(h, w, name):
    n, d = h.shape
    tr = _row_tile(n)

    def body(h_ref, w_ref, u_ref):
        u_ref[...] = _rms_apply(h_ref[...], w_ref[...]).astype(u_ref.dtype)

    return pl.pallas_call(
        body, name=name, out_shape=jax.ShapeDtypeStruct((n, d), MXU_DTYPE), grid=(n // tr,),
        in_specs=[_rows(tr, d), _vec(d)], out_specs=_rows(tr, d), compiler_params=_params("parallel"),
    )(h, w)


def _mix_residual(h0, mix, w_post, w_pre):
    n, d = h0.shape
    tr = _row_tile(n)

    def body(h0_ref, mix_ref, wpost_ref, wpre_ref, h1_ref, u2_ref):
        h1 = h0_ref[...] + _rms_apply(mix_ref[...], wpost_ref[...])
        h1_ref[...] = h1
        u2_ref[...] = _rms_apply(h1, wpre_ref[...]).astype(u2_ref.dtype)

    return pl.pallas_call(
        body, name="mix_residual",
        out_shape=(jax.ShapeDtypeStruct((n, d), F32), jax.ShapeDtypeStruct((n, d), MXU_DTYPE)), grid=(n // tr,),
        in_specs=[_rows(tr, d), _rows(tr, d), _vec(d), _vec(d)], out_specs=(_rows(tr, d), _rows(tr, d)),
        compiler_params=_params("parallel"),
    )(h0, mix, w_post, w_pre)


NT_DIMS = (((1,), (1,)), ((), ()))


def _ffn_tiles(n):
    return _pick(n, (704, 512, 256, 128)), _pick(D_FF, (1408, 256, 128))


def _swiglu_fwd(u, w_gate_t, w_up_t):
    n, d = u.shape
    tm, tn = _ffn_tiles(n)

    def body(u_ref, wg_ref, wu_ref, g_ref, up_ref, act_ref):
        a = u_ref[...]
        g = lax.dot_general(a, wg_ref[...], NT_DIMS, preferred_element_type=F32)
        up = lax.dot_general(a, wu_ref[...], NT_DIMS, preferred_element_type=F32)
        g_ref[...] = g
        up_ref[...] = up
        act_ref[...] = (g * _sigmoid(g) * up).astype(act_ref.dtype)

    tile = pl.BlockSpec((tm, tn), lambda j, i: (i, j))
    weight = pl.BlockSpec((tn, d), lambda j, i: (j, 0))
    wide = jax.ShapeDtypeStruct((n, D_FF), F32)
    return pl.pallas_call(
        body, name="swiglu_fwd", out_shape=(wide, wide, jax.ShapeDtypeStruct((n, D_FF), MXU_DTYPE)),
        grid=(D_FF // tn, n // tm),
        in_specs=[pl.BlockSpec((tm, d), lambda j, i: (i, 0)), weight, weight], out_specs=(tile, tile, tile),
        compiler_params=_params("parallel", "parallel"),
    )(u, w_gate_t, w_up_t)


def _swiglu_bwd(dffn, w_down, gate, up):
    n, d = dffn.shape
    tm, tn = _ffn_tiles(n)

    def body(dy_ref, w_ref, g_ref, u_ref, dg_ref, du_ref):
        da = lax.dot_general(dy_ref[...], w_ref[...], NT_DIMS, preferred_element_type=F32)
        g = g_ref[...]
        s = _sigmoid(g)
        dg_ref[...] = (da * u_ref[...] * _dsilu(g, s)).astype(dg_ref.dtype)
        du_ref[...] = (da * g * s).astype(du_ref.dtype)

    tile = pl.BlockSpec((tm, tn), lambda j, i: (i, j))
    shape = jax.ShapeDtypeStruct((n, D_FF), MXU_DTYPE)
    return pl.pallas_call(
        body, name="swiglu_bwd", out_shape=(shape, shape), grid=(D_FF // tn, n // tm),
        in_specs=[pl.BlockSpec((tm, d), lambda j, i: (i, 0)), pl.BlockSpec((tn, d), lambda j, i: (j, 0)), tile, tile],
        out_specs=(tile, tile), compiler_params=_params("parallel", "parallel"),
    )(dffn, w_down, gate, up)


def _loss_head(h1, ffn, w_post, target, rows_per_seq, x_offset):
    n, d = h1.shape
    tr = _row_tile(rows_per_seq)
    tiles_per_seq = rows_per_seq // tr

    def body(h1_ref, ffn_ref, w_ref, t_ref, dh2_ref, dffn_ref, dw_ref, sq_ref):
        i = pl.program_id(0)
        w = w_ref[...]
        f = ffn_ref[...]
        r = lax.rsqrt(jnp.mean(f * f, axis=-1, keepdims=True) + EPS)
        fh = f * r
        row = lax.rem(i, tiles_per_seq) * tr + lax.broadcasted_iota(jnp.int32, (tr, 1), 0)
        err = jnp.where(row >= x_offset, h1_ref[...] + fh * w - t_ref[...], 0.0)
        dh2 = err * (1.0 / d)
        dh2_ref[...] = dh2
        dyw = dh2 * w
        dffn_ref[...] = (r * (dyw - fh * jnp.mean(dyw * fh, axis=-1, keepdims=True))).astype(dffn_ref.dtype)
        _accumulate(dw_ref, i == 0, jnp.sum(dh2 * fh, axis=0, keepdims=True))
        _accumulate(sq_ref, i == 0, jnp.sum(jnp.sum(err * err, axis=1, keepdims=True), axis=0, keepdims=True))

    return pl.pallas_call(
        body, name="loss_head",
        out_shape=(jax.ShapeDtypeStruct((n, d), F32), jax.ShapeDtypeStruct((n, d), MXU_DTYPE),
                   jax.ShapeDtypeStruct((1, d), F32), jax.ShapeDtypeStruct((1, 1), F32)),
        grid=(n // tr,),
        in_specs=[_rows(tr, d), _rows(tr, d), _vec(d), _rows(tr, d)],
        out_specs=(_rows(tr, d), _rows(tr, d), _vec(d), _vec(1)),
        compiler_params=_params("arbitrary"),
    )(h1, ffn, w_post, target)


def _mid_bwd(h1, mix, w_mix_post, w_ffn_pre, dh2, du2):
    n, d = h1.shape
    tr = _row_tile(n)

    def body(h1_ref, mix_ref, wpost_ref, wpre_ref, dh2_ref, du2_ref, dh1_ref, dmix_ref, dwpre_ref, dwpost_ref):
        i = pl.program_id(0)
        dx, dwpre = _rms_bwd(h1_ref[...], wpre_ref[...], du2_ref[...])
        dh1 = dh2_ref[...] + dx
        dh1_ref[...] = dh1
        dmix, dwpost = _rms_bwd(mix_ref[...], wpost_ref[...], dh1)
        dmix_ref[...] = dmix.astype(dmix_ref.dtype)
        _accumulate(dwpre_ref, i == 0, dwpre)
        _accumulate(dwpost_ref, i == 0, dwpost)

    return pl.pallas_call(
        body, name="mid_bwd",
        out_shape=(jax.ShapeDtypeStruct((n, d), F32), jax.ShapeDtypeStruct((n, d), MXU_DTYPE),
                   jax.ShapeDtypeStruct((1, d), F32), jax.ShapeDtypeStruct((1, d), F32)),
        grid=(n // tr,),
        in_specs=[_rows(tr, d), _rows(tr, d), _vec(d), _vec(d), _rows(tr, d), _rows(tr, d)],
        out_specs=(_rows(tr, d), _rows(tr, d), _vec(d), _vec(d)),
        compiler_params=_params("arbitrary"),
    )(h1, mix, w_mix_post, w_ffn_pre, dh2, du2)


def _in_bwd(h0, w_pre, dh1, du1):
    n, d = h0.shape
    tr = _row_tile(n)

    def body(h0_ref, w_ref, dh1_ref, du1_ref, dh0_ref, dw_ref):
        dx, dw = _rms_bwd(h0_ref[...], w_ref[...], du1_ref[...])
        dh0_ref[...] = dh1_ref[...] + dx
        _accumulate(dw_ref, pl.program_id(0) == 0, dw)

    return pl.pallas_call(
        body, name="in_bwd",
        out_shape=(jax.ShapeDtypeStruct((n, d), F32), jax.ShapeDtypeStruct((1, d), F32)), grid=(n // tr,),
        in_specs=[_rows(tr, d), _vec(d), _rows(tr, d), _rows(tr, d)], out_specs=(_rows(tr, d), _vec(d)),
        compiler_params=_params("arbitrary"),
    )(h0, w_pre, dh1, du1)


def _lane_is(lo, hi):
    lane = lax.broadcasted_iota(jnp.int32, (1, LANES), 1)
    return jnp.logical_and(lane >= lo, lane < hi)


def _gates_fwd(proj, a_log_l, dt_bias_l, rows_per_seq, pad_rows):
    n = proj.shape[0]
    tr = _row_tile(rows_per_seq)
    tiles_per_seq = rows_per_seq // tr

    def body(p_ref, a_ref, dt_ref, o_ref):
        x = p_ref[...]
        row = lax.rem(pl.program_id(0), tiles_per_seq) * tr + lax.broadcasted_iota(jnp.int32, (tr, 1), 0)
        g = -jnp.exp(a_ref[...]) * _softplus(x + dt_ref[...])
        val = jnp.where(_lane_is(0, HEADS), _sigmoid(x), jnp.where(_lane_is(HEADS, 2 * HEADS), g, 0.0))
        o_ref[...] = jnp.where(row >= pad_rows, val, 0.0)

    return pl.pallas_call(
        body, name="gates_fwd", out_shape=jax.ShapeDtypeStruct((n, LANES), F32), grid=(n // tr,),
        in_specs=[pl.BlockSpec((tr, LANES), lambda i: (i, BA_COL)), _vec(LANES), _vec(LANES)],
        out_specs=_rows(tr, LANES), compiler_params=_params("parallel"),
    )(proj, a_log_l, dt_bias_l)


def _gates_bwd(proj, dbg, a_log_l, dt_bias_l, rows_per_seq, pad_rows):
    n = proj.shape[0]
    tr = _row_tile(rows_per_seq)
    tiles_per_seq = rows_per_seq // tr

    def body(p_ref, d_ref, a_ref, dt_ref, dx_ref, da_ref, ddt_ref):
        i = pl.program_id(0)
        x = p_ref[...]
        d = d_ref[...]
        row = lax.rem(i, tiles_per_seq) * tr + lax.broadcasted_iota(jnp.int32, (tr, 1), 0)
        live = row >= pad_rows
        beta = _sigmoid(x)
        ea = jnp.exp(a_ref[...])
        xa = x + dt_ref[...]
        g = -ea * _softplus(xa)
        is_g = _lane_is(HEADS, 2 * HEADS)
        d_alogit = jnp.where(jnp.logical_and(live, is_g), d * (-ea) * _sigmoid(xa), 0.0)
        d_blogit = jnp.where(jnp.logical_and(live, _lane_is(0, HEADS)), d * beta * (1.0 - beta), 0.0)
        dx_ref[:, :LANES] = (d_alogit + d_blogit).astype(dx_ref.dtype)
        dx_ref[:, LANES:] = jnp.zeros((tr, LANES), dx_ref.dtype)
        _accumulate(da_ref, i == 0, jnp.sum(jnp.where(jnp.logical_and(live, is_g), d * g, 0.0), axis=0, keepdims=True))
        _accumulate(ddt_ref, i == 0, jnp.sum(d_alogit, axis=0, keepdims=True))

    return pl.pallas_call(
        body, name="gates_bwd",
        out_shape=(jax.ShapeDtypeStruct((n, 2 * LANES), MXU_DTYPE), jax.ShapeDtypeStruct((1, LANES), F32),
                   jax.ShapeDtypeStruct((1, LANES), F32)),
        grid=(n // tr,),
        in_specs=[pl.BlockSpec((tr, LANES), lambda i: (i, BA_COL)), _rows(tr, LANES), _vec(LANES), _vec(LANES)],
        out_specs=(_rows(tr, 2 * LANES), _vec(LANES), _vec(LANES)),
        compiler_params=_params("arbitrary"),
    )(proj, dbg, a_log_l, dt_bias_l)


def _shift_down(x, k):
    return x if k == 0 else pltpu.roll(x, k, 0)


def _shift_up(x, k):
    return x if k == 0 else pltpu.roll(x, x.shape[0] - k, 0)


def _causal_conv(x, w, width):
    acc = w[width - 1:width, :] * x
    for i in range(width - 1):
        acc = acc + w[i:i + 1, :] * _shift_down(x, width - 1 - i)
    return acc


def _seq_head(rs, col0):
    return pl.BlockSpec((rs, LANES), lambda j, b: (b, col0 + j))


def _live_rows(rs, pad_rows):
    return lax.broadcasted_iota(jnp.int32, (rs, 1), 0) >= pad_rows


def _qkv_fwd(proj, conv_w, kind, rs, pad_rows):
    n = proj.shape[0]
    col0 = {"q": 0, "k": HEADS, "v": 2 * HEADS}[kind]

    def body(p_ref, w_ref, o_ref):
        c = _causal_conv(p_ref[...], w_ref[...], GDN_CONV)
        s = c * _sigmoid(c)
        if kind != "v":
            s = s * lax.rsqrt(jnp.sum(s * s, axis=-1, keepdims=True) + EPS)
        if kind == "q":
            s = s * (HEAD_DIM ** -0.5)
        o_ref[...] = jnp.where(_live_rows(rs, pad_rows), s, 0.0)

    return pl.pallas_call(
        body, name="qkv_fwd_" + kind, out_shape=jax.ShapeDtypeStruct((n, GDN_WIDTH), F32), grid=(HEADS, n // rs),
        in_specs=[_seq_head(rs, col0), pl.BlockSpec((GDN_CONV, LANES), lambda j, b: (0, col0 + j))],
        out_specs=_seq_head(rs, 0), compiler_params=_params("parallel", "parallel"),
    )(proj, conv_w)


def _qkv_bwd(dy, proj, conv_w, kind, rs, pad_rows):
    n = proj.shape[0]
    col0 = {"q": 0, "k": HEADS, "v": 2 * HEADS}[kind]

    def body(dy_ref, p_ref, w_ref, dp_ref, dw_ref):
        pre = p_ref[...]
        w = w_ref[...]
        c = _causal_conv(pre, w, GDN_CONV)
        sg = _sigmoid(c)
        s = c * sg
        ds = dy_ref[...]
        if kind == "q":
            ds = ds * (HEAD_DIM ** -0.5)
        if kind != "v":
            r = lax.rsqrt(jnp.sum(s * s, axis=-1, keepdims=True) + EPS)
            sh = s * r
            ds = r * (ds - sh * jnp.sum(ds * sh, axis=-1, keepdims=True))
        dc = jnp.where(_live_rows(rs, pad_rows), ds * _dsilu(c, sg), 0.0)
        dpre = w[GDN_CONV - 1:GDN_CONV, :] * dc
        for i in range(GDN_CONV - 1):
            dpre = dpre + w[i:i + 1, :] * _shift_up(dc, GDN_CONV - 1 - i)
        dp_ref[...] = dpre.astype(dp_ref.dtype)
        dw = jnp.concatenate(
            [jnp.sum(dc * _shift_down(pre, GDN_CONV - 1 - i), axis=0, keepdims=True) for i in range(GDN_CONV)], axis=0)
        _accumulate(dw_ref, pl.program_id(1) == 0, dw)

    return pl.pallas_call(
        body, name="qkv_bwd_" + kind,
        out_shape=(jax.ShapeDtypeStruct((n, GDN_WIDTH), MXU_DTYPE), jax.ShapeDtypeStruct((GDN_CONV, GDN_WIDTH), F32)),
        grid=(HEADS, n // rs),
        in_specs=[_seq_head(rs, 0), _seq_head(rs, col0), pl.BlockSpec((GDN_CONV, LANES), lambda j, b: (0, col0 + j))],
        out_specs=(_seq_head(rs, 0), pl.BlockSpec((GDN_CONV, LANES), lambda j, b: (0, j))),
        compiler_params=_params("parallel", "arbitrary"),
    )(dy, proj, conv_w)


SC_COL = 4 * HEADS


def _sc_fwd(proj, conv_w, rs):
    n = proj.shape[0]

    def body(x_ref, b_ref, c_ref, w_ref, y_ref):
        y_ref[...] = (b_ref[...] * _causal_conv(c_ref[...] * x_ref[...], w_ref[...], SC_CONV)).astype(y_ref.dtype)

    return pl.pallas_call(
        body, name="sc_fwd", out_shape=jax.ShapeDtypeStruct((n, SC_WIDTH), MXU_DTYPE), grid=(HEADS, n // rs),
        in_specs=[_seq_head(rs, SC_COL), _seq_head(rs, SC_COL + 4), _seq_head(rs, SC_COL + 8),
                  pl.BlockSpec((SC_CONV, LANES), lambda j, b: (0, j))],
        out_specs=_seq_head(rs, 0), compiler_params=_params("parallel", "parallel"),
    )(proj, proj, proj, conv_w)


def _sc_bwd(dcat, proj, conv_w, rs):
    n = proj.shape[0]

    def body(dy_ref, x_ref, b_ref, c_ref, w_ref, dx_ref, db_ref, dc_ref, dw_ref):
        w = w_ref[...]
        x = x_ref[...]
        cc = c_ref[...]
        u = cc * x
        dy = dy_ref[...]
        db_ref[...] = (dy * _causal_conv(u, w, SC_CONV)).astype(db_ref.dtype)
        dcv = dy * b_ref[...]
        du = w[SC_CONV - 1:SC_CONV, :] * dcv
        for i in range(SC_CONV - 1):
            du = du + w[i:i + 1, :] * _shift_up(dcv, SC_CONV - 1 - i)
        dx_ref[...] = (du * cc).astype(dx_ref.dtype)
        dc_ref[...] = (du * x).astype(dc_ref.dtype)
        dw = jnp.concatenate(
            [jnp.sum(dcv * _shift_down(u, SC_CONV - 1 - i), axis=0, keepdims=True) for i in range(SC_CONV)], axis=0)
        _accumulate(dw_ref, pl.program_id(1) == 0, dw)

    piece = jax.ShapeDtypeStruct((n, SC_WIDTH), MXU_DTYPE)
    return pl.pallas_call(
        body, name="sc_bwd", out_shape=(piece, piece, piece, jax.ShapeDtypeStruct((SC_CONV, SC_WIDTH), F32)),
        grid=(HEADS, n // rs),
        in_specs=[_seq_head(rs, HEADS), _seq_head(rs, SC_COL), _seq_head(rs, SC_COL + 4), _seq_head(rs, SC_COL + 8),
                  pl.BlockSpec((SC_CONV, LANES), lambda j, b: (0, j))],
        out_specs=(_seq_head(rs, 0), _seq_head(rs, 0), _seq_head(rs, 0),
                   pl.BlockSpec((SC_CONV, LANES), lambda j, b: (0, j))),
        compiler_params=_params("parallel", "arbitrary"),
    )(dcat, proj, proj, proj, conv_w)


Z_COL = 3 * HEADS


def _gate_fwd(o, proj, gdn_norm, rs):
    n = proj.shape[0]

    def body(o_ref, z_ref, w_ref, y_ref):
        z = z_ref[...]
        y_ref[...] = (_rms_apply(o_ref[...], w_ref[...]) * z * _sigmoid(z)).astype(y_ref.dtype)

    return pl.pallas_call(
        body, name="gate_fwd", out_shape=jax.ShapeDtypeStruct((n, GDN_WIDTH), MXU_DTYPE), grid=(HEADS, n // rs),
        in_specs=[_seq_head(rs, 0), _seq_head(rs, Z_COL), pl.BlockSpec((1, LANES), lambda j, b: (0, 0))],
        out_specs=_seq_head(rs, 0), compiler_params=_params("parallel", "parallel"),
    )(o, proj, gdn_norm)


def _gate_bwd(dcat, o, proj, gdn_norm, rs):
    n = proj.shape[0]

    def body(dy_ref, o_ref, z_ref, w_ref, do_ref, dz_ref, dw_ref):
        z = z_ref[...]
        w = w_ref[...]
        o = o_ref[...]
        dy = dy_ref[...]
        s = _sigmoid(z)
        dz_ref[...] = (dy * _rms_apply(o, w) * _dsilu(z, s)).astype(dz_ref.dtype)
        do, dw = _rms_bwd(o, w, dy * z * s)
        do_ref[...] = do
        _accumulate(dw_ref, jnp.logical_and(pl.program_id(0) == 0, pl.program_id(1) == 0), dw)

    return pl.pallas_call(
        body, name="gate_bwd",
        out_shape=(jax.ShapeDtypeStruct((n, GDN_WIDTH), F32), jax.ShapeDtypeStruct((n, GDN_WIDTH), MXU_DTYPE),
                   jax.ShapeDtypeStruct((1, LANES), F32)),
        grid=(HEADS, n // rs),
        in_specs=[_seq_head(rs, 0), _seq_head(rs, 0), _seq_head(rs, Z_COL), pl.BlockSpec((1, LANES), lambda j, b: (0, 0))],
        out_specs=(_seq_head(rs, 0), _seq_head(rs, 0), pl.BlockSpec((1, LANES), lambda j, b: (0, 0))),
        compiler_params=_params("arbitrary", "arbitrary"),
    )(dcat, o, proj, gdn_norm)


def _dot(a, b):
    return jnp.dot(a.astype(MXU_DTYPE), b.astype(MXU_DTYPE), preferred_element_type=F32)


def _dot_nt(a, b):
    return lax.dot_general(a.astype(MXU_DTYPE), b.astype(MXU_DTYPE), (((1,), (1,)), ((), ())),
                           preferred_element_type=F32)


def _dot_tn(a, b):
    return lax.dot_general(a.astype(MXU_DTYPE), b.astype(MXU_DTYPE), (((0,), (0,)), ((), ())),
                           preferred_element_type=F32)


def _split(x):
    hi = x.astype(MXU_DTYPE)
    return hi, (x - hi.astype(F32)).astype(MXU_DTYPE)


def _dot_split(a, b):
    mm = functools.partial(jnp.dot, preferred_element_type=F32)
    return mm(a[0], b[0]) + (mm(a[0], b[1]) + mm(a[1], b[0]))


def _unit_lower_inverses(mats, eye):
    inv = [eye - a for a in mats]
    power = [_split(a) for a in mats]
    span = 2
    while span < CHUNK:
        power = [_split(_dot_split(p, p)) for p in power]
        inv = [i + _dot_split(_split(i), p) for i, p in zip(inv, power)]
        span *= 2
    return inv


def _chunk_masks():
    ii = lax.broadcasted_iota(jnp.int32, (CHUNK, CHUNK), 0)
    jj = lax.broadcasted_iota(jnp.int32, (CHUNK, CHUNK), 1)
    return ii, jj


def _chunk_decay(g_col, ii, jj):
    incl = ii >= jj
    g_row = jnp.sum(jnp.where(ii == jj, g_col, 0.0), axis=0, keepdims=True)
    gc_col = jnp.sum(jnp.where(incl, g_row, 0.0), axis=1, keepdims=True)
    gc_row = jnp.sum(jnp.where(ii <= jj, g_col, 0.0), axis=0, keepdims=True)
    g_total = jnp.sum(g_row, axis=1, keepdims=True)
    decay = jnp.where(incl, jnp.exp(jnp.where(incl, gc_col - gc_row, 0.0)), 0.0)
    return gc_col, g_total, decay


def _gdn_segments(rs, candidates):
    chunks = rs // CHUNK
    seg_chunks = _pick(chunks, candidates)
    return chunks, seg_chunks, chunks // seg_chunks


def _head_lanes(h):
    return slice(h * HEAD_DIM, (h + 1) * HEAD_DIM)


def _gdn_fwd(q, k, v, bg, rs, pieces):
    n = q.shape[0]
    batch = n // rs
    chunks, seg_chunks, segs = _gdn_segments(rs, (11, 8, 4, 2))
    seg_rows = seg_chunks * CHUNK
    chains = [(b, h) for b in range(batch) for h in range(HEADS)]
    each = lambda f, *lists: [f(*args) for args in zip(*lists)]
    count = len(pieces)

    def body(q_ref, k_ref, v_ref, bg_ref, *rest):
        w_refs, (o_ref, s_ref, t_ref), out_refs = rest[:count], rest[count:count + 3], rest[count + 3:2 * count + 3]
        state_ref, send_sems, recv_sems = rest[2 * count + 3:]
        gather = _gather_copies(w_refs, out_refs, send_sems, recv_sems)

        @pl.when(pl.program_id(0) == 0)
        def _():
            state_ref[...] = jnp.zeros_like(state_ref)
            for cp in gather[0]:
                cp.start()

        ii, jj = _chunk_masks()
        incl = ii >= jj
        eye = (ii == jj).astype(F32)

        def chunk(c, carry):
            rows = pl.ds(pl.multiple_of(c * CHUNK, CHUNK), CHUNK)
            bgc = [bg_ref[b, rows, :] for b in range(batch)]
            qc = [q_ref[b, rows, _head_lanes(h)] for b, h in chains]
            kc = [k_ref[b, rows, _head_lanes(h)] for b, h in chains]
            vc = [v_ref[b, rows, _head_lanes(h)] for b, h in chains]
            beta = [bgc[b][:, h:h + 1] for b, h in chains]
            state = [state_ref[b, h] for b, h in chains]
            dec = [_chunk_decay(bgc[b][:, HEADS + h:HEADS + h + 1], ii, jj) for b, h in chains]
            gc_col, g_total, decay = ([d[i] for d in dec] for i in range(3))
            kb = each(lambda x, y: x * y, kc, beta)
            a = each(lambda x, y, d: jnp.where(ii > jj, _dot_nt(x, y) * d, 0.0), kb, kc, decay)
            t_inv = _unit_lower_inverses(a, eye)
            eg = [jnp.exp(g) for g in gc_col]
            u = each(lambda t, x, y: _dot(t, x * y), t_inv, vc, beta)
            w = each(lambda t, x, e: _dot(t, x * e), t_inv, kb, eg)
            qk = each(lambda x, y, d: jnp.where(incl, _dot_nt(x, y) * d, 0.0), qc, kc, decay)
            v_new = each(lambda x, y, s: x - _dot(y, s), u, w, state)
            o = each(lambda x, e, s, m, vn: _dot(x * e, s) + _dot(m, vn), qc, eg, state, qk, v_new)
            new_state = each(lambda s, gt, x, g, vn: s * jnp.exp(gt) + _dot_tn(x * jnp.exp(gt - g), vn),
                             state, g_total, kc, gc_col, v_new)
            for i, (b, h) in enumerate(chains):
                s_ref[b, h, c] = state[i]
                t_ref[b, h, c] = t_inv[i]
                o_ref[b, rows, _head_lanes(h)] = o[i]
                state_ref[b, h] = new_state[i]
            return carry

        lax.fori_loop(0, seg_chunks, chunk, 0)

        @pl.when(pl.program_id(0) == segs - 1)
        def _():
            _gather_finish(gather)

    rows_spec = lambda width: pl.BlockSpec((batch, seg_rows, width), lambda s: (0, s, 0))
    per_chunk = lambda r, c: pl.BlockSpec((batch, HEADS, seg_chunks, r, c), lambda s: (0, 0, s, 0, 0))
    as_seqs = lambda a: a.reshape(batch, rs, a.shape[-1])
    sems = GATHER_SEMS * count
    o, states, t_invs, *gathered = pl.pallas_call(
        body, name="gdn_fwd",
        out_shape=(jax.ShapeDtypeStruct((batch, rs, GDN_WIDTH), F32),
                   jax.ShapeDtypeStruct((batch, HEADS, chunks, HEAD_DIM, HEAD_DIM), F32),
                   jax.ShapeDtypeStruct((batch, HEADS, chunks, CHUNK, CHUNK), F32))
        + tuple(jax.ShapeDtypeStruct((N_CHIPS,) + p.shape, p.dtype) for p in pieces),
        grid=(segs,),
        in_specs=[rows_spec(GDN_WIDTH), rows_spec(GDN_WIDTH), rows_spec(GDN_WIDTH), rows_spec(LANES)] + [_hbm()] * count,
        out_specs=(rows_spec(GDN_WIDTH), per_chunk(HEAD_DIM, HEAD_DIM), per_chunk(CHUNK, CHUNK)) + (_hbm(),) * count,
        scratch_shapes=[pltpu.VMEM((batch, HEADS, HEAD_DIM, HEAD_DIM), F32), pltpu.SemaphoreType.DMA((sems,)),
                        pltpu.SemaphoreType.DMA((sems,))],
        compiler_params=_params("arbitrary"),
    )(as_seqs(q), as_seqs(k), as_seqs(v), as_seqs(bg), *pieces)
    return o.reshape(n, GDN_WIDTH), states, t_invs, gathered


def _gdn_bwd(do, q, k, v, bg, states, t_invs, rs, parts):
    n = q.shape[0]
    batch = n // rs
    chunks, seg_chunks, segs = _gdn_segments(rs, (3, 4, 2))
    seg_rows = seg_chunks * CHUNK
    chains = [(b, h) for b in range(batch) for h in range(HEADS)]
    each = lambda f, *lists: [f(*args) for args in zip(*lists)]
    count = len(parts)

    def body(do_ref, q_ref, k_ref, v_ref, bg_ref, s_ref, t_ref, *rest):
        p_refs, (dq_ref, dk_ref, dv_ref, dbg_ref), got_refs = rest[:count], rest[count:count + 4], rest[count + 4:2 * count + 4]
        dstate_ref, send_sems, recv_sems = rest[2 * count + 4:]
        exchange = _chip_copies(p_refs, got_refs, send_sems, recv_sems)

        @pl.when(pl.program_id(0) == 0)
        def _():
            dstate_ref[...] = jnp.zeros_like(dstate_ref)
            for cp in exchange:
                cp.start()

        ii, jj = _chunk_masks()
        incl = ii >= jj
        strict = ii > jj
        lane = lax.broadcasted_iota(jnp.int32, (1, LANES), 1)

        def rowsum(x):
            return jnp.sum(x, axis=1, keepdims=True)

        def total(x):
            return jnp.sum(rowsum(x), axis=0, keepdims=True)

        def chunk(step, carry):
            c = seg_chunks - 1 - step
            rows = pl.ds(pl.multiple_of(c * CHUNK, CHUNK), CHUNK)
            bgc = [bg_ref[b, rows, :] for b in range(batch)]
            qc = [q_ref[b, rows, _head_lanes(h)] for b, h in chains]
            kc = [k_ref[b, rows, _head_lanes(h)] for b, h in chains]
            vc = [v_ref[b, rows, _head_lanes(h)] for b, h in chains]
            doc = [do_ref[b, rows, _head_lanes(h)] for b, h in chains]
            beta = [bgc[b][:, h:h + 1] for b, h in chains]
            state = [s_ref[b, h, c] for b, h in chains]
            t_inv = [t_ref[b, h, c] for b, h in chains]
            d_state = [dstate_ref[b, h] for b, h in chains]
            dec = [_chunk_decay(bgc[b][:, HEADS + h:HEADS + h + 1], ii, jj) for b, h in chains]
            gc_col, g_total, decay = ([d[i] for d in dec] for i in range(3))
            kb = each(lambda x, y: x * y, kc, beta)
            vb = each(lambda x, y: x * y, vc, beta)
            eg = [jnp.exp(g) for g in gc_col]
            kbg = each(lambda x, y: x * y, kb, eg)
            a = each(lambda x, y, d: jnp.where(strict, _dot_nt(x, y) * d, 0.0), kb, kc, decay)
            qk = each(lambda x, y, d: jnp.where(incl, _dot_nt(x, y) * d, 0.0), qc, kc, decay)
            w = each(_dot, t_inv, kbg)
            u = each(_dot, t_inv, vb)
            q_dec = each(lambda x, y: x * y, qc, eg)
            ek = each(lambda gt, g: jnp.exp(gt - g), g_total, gc_col)
            k_dec = each(lambda x, y: x * y, kc, ek)
            g_last = [jnp.exp(gt) for gt in g_total]
            v_new = each(lambda x, y, s: x - _dot(y, s), u, w, state)
            dv_new = each(lambda m, d, x, ds: _dot_tn(m, d) + _dot(x, ds), qk, doc, k_dec, d_state)
            dqk = each(lambda d, vn: jnp.where(incl, _dot_nt(d, vn), 0.0), doc, v_new)
            dq_dec = each(_dot_nt, doc, state)
            dk_dec = each(_dot_nt, v_new, d_state)
            dg_last = each(lambda s, ds: total(s * ds), state, d_state)
            new_d_state = each(lambda x, d, gl, ds, y, dvn: _dot_tn(x, d) + gl * ds - _dot_tn(y, dvn),
                               q_dec, doc, g_last, d_state, w, dv_new)
            dw = each(lambda dvn, s: -_dot_nt(dvn, s), dv_new, state)
            dt = each(lambda dvn, x, y, z: _dot_nt(dvn, x) + _dot_nt(y, z), dv_new, vb, dw, kbg)
            dvb = each(_dot_tn, t_inv, dv_new)
            dkbg = each(_dot_tn, t_inv, dw)
            t_dt = each(_dot_tn, t_inv, dt)
            da = each(lambda x, t: -jnp.where(strict, _dot_nt(x, t), 0.0), t_dt, t_inv)
            dm_a = each(lambda x, y: x * y, da, decay)
            dm_qk = each(lambda x, y: x * y, dqk, decay)
            e = each(lambda x, y, z, t: x * y + z * t, da, a, dqk, qk)
            dkb = each(lambda m, x, y, z: _dot(m, x) + y * z, dm_a, kc, dkbg, eg)
            dk = each(lambda m, x, m2, y, z, t, p, bt: _dot_tn(m, x) + _dot_tn(m2, y) + z * t + p * bt,
                      dm_a, kb, dm_qk, qc, dk_dec, ek, dkb, beta)
            dq = each(lambda m, x, y, z: _dot(m, x) + y * z, dm_qk, kc, dq_dec, eg)
            dbeta = each(lambda x, y, z, t: rowsum(x * y + z * t), dkb, kc, dvb, vc)
            dgc = each(lambda x, p, pd, r, rd, s, sd: rowsum(x) - rowsum(jnp.where(ii == jj, jnp.sum(x, axis=0, keepdims=True), 0.0))
                       + rowsum(p * pd - r * rd + s * sd), e, dq_dec, q_dec, dk_dec, k_dec, dkbg, kbg)
            d_total = each(lambda r, rd, x, gl: total(r * rd) + x * gl, dk_dec, k_dec, dg_last, g_last)
            dg = each(lambda x, t: rowsum(jnp.where(jj >= ii, jnp.sum(jnp.where(ii == jj, x, 0.0), axis=0, keepdims=True), 0.0)) + t,
                      dgc, d_total)
            dbg = [jnp.zeros((CHUNK, LANES), F32) for _ in range(batch)]
            for i, (b, h) in enumerate(chains):
                dstate_ref[b, h] = new_d_state[i]
                dk_ref[b, rows, _head_lanes(h)] = dk[i]
                dq_ref[b, rows, _head_lanes(h)] = dq[i]
                dv_ref[b, rows, _head_lanes(h)] = dvb[i] * beta[i]
                dbg[b] = dbg[b] + jnp.where(lane == h, dbeta[i], 0.0) + jnp.where(lane == HEADS + h, dg[i], 0.0)
            for b in range(batch):
                dbg_ref[b, rows, :] = dbg[b]
            return carry

        lax.fori_loop(0, seg_chunks, chunk, 0)

        @pl.when(pl.program_id(0) == segs - 1)
        def _():
            for cp in exchange:
                cp.wait_recv()
            for cp in exchange:
                cp.wait_send()

    rows_spec = lambda width: pl.BlockSpec((batch, seg_rows, width), lambda s: (0, segs - 1 - s, 0))
    per_chunk = lambda r, c: pl.BlockSpec((batch, HEADS, seg_chunks, r, c), lambda s: (0, 0, segs - 1 - s, 0, 0))
    as_seqs = lambda a: a.reshape(batch, rs, a.shape[-1])
    grad = jax.ShapeDtypeStruct((batch, rs, GDN_WIDTH), F32)
    wide = rows_spec(GDN_WIDTH)
    dq, dk, dv, dbg, *got = pl.pallas_call(
        body, name="gdn_bwd",
        out_shape=(grad, grad, grad, jax.ShapeDtypeStruct((batch, rs, LANES), F32))
        + tuple(jax.ShapeDtypeStruct((3,) + p.shape[1:], p.dtype) for p in parts),
        grid=(segs,),
        in_specs=[wide, wide, wide, wide, rows_spec(LANES), per_chunk(HEAD_DIM, HEAD_DIM), per_chunk(CHUNK, CHUNK)]
        + [_hbm()] * count,
        out_specs=(wide, wide, wide, rows_spec(LANES)) + (_hbm(),) * count,
        scratch_shapes=[pltpu.VMEM((batch, HEADS, HEAD_DIM, HEAD_DIM), F32), pltpu.SemaphoreType.DMA((3 * count,)),
                        pltpu.SemaphoreType.DMA((3 * count,))],
        compiler_params=_params("arbitrary"),
    )(as_seqs(do), as_seqs(q), as_seqs(k), as_seqs(v), as_seqs(bg), states, t_invs, *parts)
    return dq.reshape(n, GDN_WIDTH), dk.reshape(n, GDN_WIDTH), dv.reshape(n, GDN_WIDTH), dbg.reshape(n, LANES), got


def _lane_vec(vals, offset):
    k = vals.shape[1]
    return jnp.pad(vals, ((0, 0), (offset, LANES - offset - k)))


LATER = ("w_out", "w_gate", "w_up", "w_down")


def _halves(a):
    return a.reshape(a.shape[:-2] + (2, a.shape[-2] // 2, a.shape[-1]))


def _local_step(x, target, meta, norms, w_in_t, conv_qkv, a_log, dt_bias, gdn_norm, conv_sc, later_shards, core_arg):
    batch, seq, d = x.shape
    tokens = N_META + seq
    pad_rows = (-tokens) % CHUNK
    rs = tokens + pad_rows
    x_offset = pad_rows + N_META
    n = batch * rs
    w_mix_pre, w_mix_post, w_ffn_pre, w_ffn_post = norms

    head = jnp.concatenate([jnp.zeros((pad_rows, d), F32), meta], axis=0)
    h0 = jnp.concatenate([jnp.broadcast_to(head[None], (batch, x_offset, d)), x], axis=1).reshape(n, d)
    target_p = jnp.pad(target, ((0, 0), (x_offset, 0), (0, 0))).reshape(n, d)
    a_log_l = _lane_vec(a_log, HEADS)
    dt_bias_l = _lane_vec(dt_bias, HEADS)

    u1 = _rms_fwd(h0, w_mix_pre, "rms_mix_pre")
    proj = _mm(u1, w_in_t, "nt", F32, "mm_proj")
    q = _qkv_fwd(proj, conv_qkv, "q", rs, pad_rows)
    k = _qkv_fwd(proj, conv_qkv, "k", rs, pad_rows)
    v = _qkv_fwd(proj, conv_qkv, "v", rs, pad_rows)
    bg = _gates_fwd(proj, a_log_l, dt_bias_l, rs, pad_rows)
    o, states, t_invs, gathered = _gdn_fwd(q, k, v, bg, rs, later_shards)
    w_out, w_gate_t, w_up_t, w_down = (a.reshape(-1, d) for a in gathered)
    o_gated = _gate_fwd(o, proj, gdn_norm, rs)
    y_sc = _sc_fwd(proj, conv_sc, rs)
    cat = jnp.concatenate([o_gated, y_sc], axis=1)
    mix = _mm(cat, w_out, "nn", F32, "mm_mix")
    h1, u2 = _mix_residual(h0, mix, w_mix_post, w_ffn_pre)
    gate, up, act = _swiglu_fwd(u2, w_gate_t, w_up_t)
    ffn = _mm(act, w_down, "nn", F32, "mm_down")

    dh2, dffn, d_ffn_post, sq = _loss_head(h1, ffn, w_ffn_post, target_p, rs, x_offset)
    d_w_down = _mm(act, dffn, "tn", F32, "mm_dw_down")
    dgate, dup = _swiglu_bwd(dffn, w_down, gate, up)
    d_w_gate_t = _mm(dgate, u2, "tn", F32, "mm_dw_gate")
    big = n % 2112 == 0
    d_w_up_t = _mm(dup, u2, "tn", F32, "mm_dw_up", tiles=(1408, 1024, 2112) if big else None)
    du2 = _mm(dup, w_up_t, "nn", F32, "mm_du2_up", init=_mm(dgate, w_gate_t, "nn", F32, "mm_du2_gate"),
              tiles=(2112, 512, 1408) if big else None)
    dh1, dmix, d_ffn_pre, d_mix_post = _mid_bwd(h1, mix, w_mix_post, w_ffn_pre, dh2, du2)
    dcat = _mm(dmix, w_out, "nt", F32, "mm_dcat")
    d_w_out = _mm(cat, dmix, "tn", F32, "mm_dw_out")
    do, dz, d_gdn_norm = _gate_bwd(dcat, o, proj, gdn_norm, rs)
    dscx, dscb, dscc, d_conv_sc = _sc_bwd(dcat, proj, conv_sc, rs)
    by_chip = [_halves(g.reshape(N_CHIPS, -1, d)) for g in (d_w_out, d_w_gate_t, d_w_up_t, d_w_down)]
    sums = [_add_sibling(a, b, core_arg, name) for name, a, b in zip(LATER, by_chip, _exchange_siblings(by_chip))]
    dq, dk, dv, dbg, got_chips = _gdn_bwd(do, q, k, v, bg, states, t_invs, rs, [send for _, send in sums])
    dpq, dwq = _qkv_bwd(dq, proj, conv_qkv, "q", rs, pad_rows)
    dpk, dwk = _qkv_bwd(dk, proj, conv_qkv, "k", rs, pad_rows)
    dpv, dwv = _qkv_bwd(dv, proj, conv_qkv, "v", rs, pad_rows)
    d_conv_qkv = jnp.concatenate([dwq, dwk, dwv], axis=1)
    dba, d_a_log_l, d_dt_bias_l = _gates_bwd(proj, dbg, a_log_l, dt_bias_l, rs, pad_rows)
    dproj = jnp.concatenate([dpq, dpk, dpv, dz, dscx, dscb, dscc, dba], axis=1)
    d_w_in_t = _mm(dproj, u1, "tn", F32, "mm_dw_in")
    du1 = _mm(dproj, w_in_t, "nn", F32, "mm_du1")
    dh0, d_mix_pre = _in_bwd(h0, w_mix_pre, dh1, du1)

    dh0 = dh0.reshape(batch, rs, d)
    grads = dict(
        meta_tokens=jnp.sum(dh0[:, pad_rows:x_offset], axis=0),
        mix_pre_norm=d_mix_pre, mix_post_norm=d_mix_post, ffn_pre_norm=d_ffn_pre, ffn_post_norm=d_ffn_post,
        w_in=d_w_in_t, conv_qkv=d_conv_qkv,
        a_log=d_a_log_l[:, HEADS:2 * HEADS], dt_bias=d_dt_bias_l[:, HEADS:2 * HEADS],
        gdn_norm=d_gdn_norm, conv_sc=d_conv_sc,
    )
    return sq, dh0[:, x_offset:], grads, [(part, got) for (part, _), got in zip(sums, got_chips)]


def _to_padded_in(w_in_t):
    lo, hi = 4 * GDN_WIDTH, 4 * GDN_WIDTH + 2 * HEADS
    pad = jnp.zeros((IN_PAD - IN_WIDTH, w_in_t.shape[1]), w_in_t.dtype)
    return jnp.concatenate([w_in_t[:lo], w_in_t[hi:], w_in_t[lo:hi], pad], axis=0)


def _from_padded_in(w_in_p):
    lo, hi = 4 * GDN_WIDTH, IN_WIDTH - 2 * HEADS
    return jnp.concatenate([w_in_p[:lo], w_in_p[hi:IN_WIDTH], w_in_p[lo:hi]], axis=0)


MATRICES = ("w_in", "w_out", "w_gate", "w_up", "w_down")
IN_SHARD = IN_WIDTH // N_CHIPS
IN_SHARD_PAD = 928
SMALL_ROWS = 48
REDUCE_ROWS = 224


def _flatten(pieces, rows):
    flat = jnp.concatenate([p.reshape(-1) for p in pieces])
    return jnp.pad(flat, (0, rows * LANES - flat.shape[0])).reshape(rows, LANES)


def _unflatten(flat, shapes):
    flat = flat.reshape(-1)
    out, at = [], 0
    for shape in shapes:
        size = shape[0] * shape[1]
        out.append(flat[at:at + size].reshape(shape))
        at += size
    return out


def _stored(w):
    return jnp.transpose(w, (2, 0, 1)).reshape(-1, LANES)


def _unstored(flat, d):
    return jnp.transpose(flat.reshape(-1, 1, d), (1, 2, 0))


def _hbm():
    return pl.BlockSpec(memory_space=pl.ANY)


def _place():
    x, y, c = lax.axis_index("x"), lax.axis_index("y"), lax.axis_index("c")
    chips = ((1 - x, y), (x, 1 - y), (1 - x, 1 - y))
    return x, y, c, chips


def _remote(src, dst, send_sems, recv_sems, k, to):
    return pltpu.make_async_remote_copy(src_ref=src, dst_ref=dst, send_sem=send_sems.at[k], recv_sem=recv_sems.at[k],
                                        device_id=to, device_id_type=MESH)


GATHER_SEMS = 7


def _gather_copies(w_refs, out_refs, send_sems, recv_sems):
    x, y, c, chips = _place()
    mine = 2 * x + y
    sibling = (x, y, 1 - c)
    copy = functools.partial(_remote, send_sems=send_sems, recv_sems=recv_sems)
    direct, landed, passing, from_sibling = [], [], [], []
    for i, (w, o) in enumerate(zip(w_refs, out_refs)):
        k = GATHER_SEMS * i
        direct.append(copy(w, o.at[mine], k=k, to=sibling))
        from_sibling.append(copy(w, o.at[mine], k=k, to=sibling))
        for j, (cx, cy) in enumerate(chips):
            theirs = 2 * cx + cy
            direct.append(copy(w.at[c], o.at[mine, c], k=k + 1 + j, to=(cx, cy, c)))
            landed.append(copy(w.at[c], o.at[theirs, c], k=k + 1 + j, to=sibling))
            passing.append(copy(o.at[theirs, c], o.at[theirs, c], k=k + 4 + j, to=sibling))
            from_sibling.append(copy(w.at[c], o.at[theirs, 1 - c], k=k + 4 + j, to=sibling))
    return direct, landed, passing, from_sibling


def _gather_finish(copies):
    direct, landed, passing, from_sibling = copies
    for arrival, forward in zip(landed, passing):
        arrival.wait_recv()
        forward.start()
    for arrival in from_sibling:
        arrival.wait_recv()
    for cp in direct + passing:
        cp.wait_send()


def _gather_weights(pieces, s_flat):
    count = len(pieces)

    def body(*refs):
        w_refs, s_ref = refs[:count], refs[count]
        out_refs, sall_ref = refs[count + 1:2 * count + 1], refs[2 * count + 1]
        send_sems, recv_sems, local_sem = refs[2 * count + 2:]
        x, y, c, chips = _place()
        mine = 2 * x + y
        own_s = pltpu.make_async_copy(s_ref, sall_ref.at[mine], local_sem)
        own_s.start()
        small = [_remote(s_ref, sall_ref.at[mine], send_sems, recv_sems, GATHER_SEMS * count + j, (cx, cy, c))
                 for j, (cx, cy) in enumerate(chips)]
        copies = _gather_copies(w_refs, out_refs, send_sems, recv_sems)
        for cp in small + copies[0]:
            cp.start()
        _gather_finish(copies)
        for cp in small:
            cp.wait_recv()
        for cp in small:
            cp.wait_send()
        own_s.wait()

    sems = GATHER_SEMS * count + 3
    return pl.pallas_call(
        body, name="gather_weights",
        out_shape=tuple(jax.ShapeDtypeStruct((N_CHIPS,) + p.shape, p.dtype) for p in pieces)
        + (jax.ShapeDtypeStruct((N_CHIPS,) + s_flat.shape, s_flat.dtype),),
        in_specs=[_hbm()] * (count + 1), out_specs=(_hbm(),) * (count + 1),
        scratch_shapes=[pltpu.SemaphoreType.DMA((sems,)), pltpu.SemaphoreType.DMA((sems,)), pltpu.SemaphoreType.DMA],
    )(*pieces, s_flat)


def _exchange_siblings(grads, small=None):
    count = len(grads)
    extra = 0 if small is None else 1

    def body(*refs):
        g_refs = refs[:count]
        got_refs = refs[count + extra:2 * count + extra]
        send_sems, recv_sems = refs[2 * (count + extra):2 * (count + extra) + 2]
        x, y, c, _ = _place()
        copies = [_remote(g.at[:, 1 - c], got, send_sems, recv_sems, i, (x, y, 1 - c))
                  for i, (g, got) in enumerate(zip(g_refs, got_refs))]
        if small is not None:
            s_ref, sall_ref, local_sem = refs[count], refs[2 * count + 1], refs[-1]
            me = 4 * x + 2 * y + c
            own = pltpu.make_async_copy(s_ref, sall_ref.at[me], local_sem)
            own.start()
            for k in range(7):
                dx, dy, dc = ((k + 1) >> 2) & 1, ((k + 1) >> 1) & 1, (k + 1) & 1
                peer = (1 - x if dx else x, 1 - y if dy else y, 1 - c if dc else c)
                copies.append(_remote(s_ref, sall_ref.at[me], send_sems, recv_sems, count + k, peer))
        for cp in copies:
            cp.start()
        for cp in copies:
            cp.wait_recv()
        for cp in copies:
            cp.wait_send()
        if small is not None:
            own.wait()

    sems = count + 7 * extra
    return pl.pallas_call(
        body, name="exchange_siblings" + ("" if small is None else "_small"),
        out_shape=tuple(jax.ShapeDtypeStruct((g.shape[0],) + g.shape[2:], F32) for g in grads)
        + (() if small is None else (jax.ShapeDtypeStruct((8,) + small.shape, F32),)),
        in_specs=[_hbm()] * (count + extra), out_specs=(_hbm(),) * (count + extra),
        scratch_shapes=[pltpu.SemaphoreType.DMA((sems,)), pltpu.SemaphoreType.DMA((sems,))]
        + ([] if small is None else [pltpu.SemaphoreType.DMA]),
    )(*grads, *(() if small is None else (small,)))


def _chip_copies(p_refs, got_refs, send_sems, recv_sems):
    x, y, c, chips = _place()
    return [_remote(p.at[2 * cx + cy], got.at[j], send_sems, recv_sems, 3 * i + j, (cx, cy, c))
            for i, (p, got) in enumerate(zip(p_refs, got_refs)) for j, (cx, cy) in enumerate(chips)]


def _exchange_chips(parts):
    count = len(parts)

    def body(*refs):
        copies = _chip_copies(refs[:count], refs[count:2 * count], *refs[2 * count:])
        for cp in copies:
            cp.start()
        for cp in copies:
            cp.wait_recv()
        for cp in copies:
            cp.wait_send()

    return pl.pallas_call(
        body, name="exchange_chips", out_shape=tuple(jax.ShapeDtypeStruct((3,) + p.shape[1:], p.dtype) for p in parts),
        in_specs=[_hbm()] * count, out_specs=(_hbm(),) * count,
        scratch_shapes=[pltpu.SemaphoreType.DMA((3 * count,)), pltpu.SemaphoreType.DMA((3 * count,))],
    )(*parts)


def _share_halves(halves):
    count = len(halves)

    def body(*refs):
        h_refs, full_refs = refs[:count], refs[count:2 * count]
        send_sems, recv_sems = refs[2 * count:]
        x, y, c, _ = _place()
        copies = [pltpu.make_async_remote_copy(src_ref=h.at[c], dst_ref=full.at[c], send_sem=send_sems.at[i],
                                               recv_sem=recv_sems.at[i], device_id=(x, y, 1 - c), device_id_type=MESH)
                  for i, (h, full) in enumerate(zip(h_refs, full_refs))]
        for cp in copies:
            cp.start()
        for cp in copies:
            cp.wait_recv()
        for cp in copies:
            cp.wait_send()

    return pl.pallas_call(
        body, name="share_halves", out_shape=tuple(jax.ShapeDtypeStruct(h.shape, h.dtype) for h in halves),
        in_specs=[_hbm()] * count, out_specs=(_hbm(),) * count, input_output_aliases={i: i for i in range(count)},
        scratch_shapes=[pltpu.SemaphoreType.DMA((count,)), pltpu.SemaphoreType.DMA((count,))],
    )(*halves)


def _add_sibling(grad, got, core, name):
    chips, _, rows, cols = grad.shape

    def body(core_ref, g_ref, r_ref, sum_ref, send_ref):
        s = g_ref[...] + r_ref[...]
        sum_ref[...] = s
        send_ref[...] = s.astype(send_ref.dtype)

    block = pl.BlockSpec((None, rows, cols), lambda p, core_ref: (p, 0, 0))
    return pl.pallas_call(
        body, name="add_sibling_" + name,
        out_shape=(jax.ShapeDtypeStruct((chips, rows, cols), F32), jax.ShapeDtypeStruct((chips, rows, cols), BF16)),
        grid_spec=pltpu.PrefetchScalarGridSpec(
            num_scalar_prefetch=1, grid=(chips,),
            in_specs=[pl.BlockSpec((None, None, rows, cols), lambda p, core_ref: (p, core_ref[0], 0, 0)), block],
            out_specs=(block, block)),
        compiler_params=_params("parallel"),
    )(core, grad, got)


def _add_chips(part, got, chip_core, name):
    _, rows, cols = part.shape
    tr = rows // 2 if rows % 32 == 0 else rows

    def body(place_ref, p_ref, r_ref, o_ref):
        o_ref[...] = ((p_ref[...] + r_ref[0].astype(F32)) + r_ref[1].astype(F32)) + r_ref[2].astype(F32)

    return pl.pallas_call(
        body, name="add_chips_" + name, out_shape=jax.ShapeDtypeStruct((2, rows, cols), F32),
        grid_spec=pltpu.PrefetchScalarGridSpec(
            num_scalar_prefetch=1, grid=(rows // tr,),
            in_specs=[pl.BlockSpec((None, tr, cols), lambda i, place_ref: (place_ref[0], i, 0)),
                      pl.BlockSpec((3, tr, cols), lambda i, place_ref: (0, i, 0))],
            out_specs=pl.BlockSpec((None, tr, cols), lambda i, place_ref: (place_ref[1], i, 0))),
        compiler_params=_params("parallel"),
    )(chip_core, part, got)


def _sum_devices(small_all):
    def body(s_ref, o_ref):
        acc = s_ref[0]
        for k in range(1, 8):
            acc = acc + s_ref[k]
        o_ref[...] = acc

    return pl.pallas_call(body, name="sum_devices", out_shape=jax.ShapeDtypeStruct(small_all.shape[1:], F32))(small_all)


def _adamw(w, g, m, v, name):
    rows, cols = w.shape
    tr = _pick(rows, (3592, 256, 352, 176, 128, 64, 32, 16, 8))

    def body(w_ref, g_ref, m_ref, v_ref, d_ref, nm_ref, nv_ref):
        g = g_ref[...]
        m = ADAM_B1 * m_ref[...] + (1.0 - ADAM_B1) * g
        v = ADAM_B2 * v_ref[...] + (1.0 - ADAM_B2) * (g * g)
        m_hat = m / (1.0 - ADAM_B1 ** ADAM_STEP)
        v_hat = v / (1.0 - ADAM_B2 ** ADAM_STEP)
        d_ref[...] = -ADAM_LR * (m_hat / (jnp.sqrt(v_hat) + ADAM_EPS) + ADAM_WD * w_ref[...])
        nm_ref[...] = m
        nv_ref[...] = v

    block = pl.BlockSpec((tr, cols), lambda i: (i, 0))
    shape = jax.ShapeDtypeStruct((rows, cols), F32)
    return pl.pallas_call(
        body, name="adamw_" + name, out_shape=(shape, shape, shape), grid=(rows // tr,),
        in_specs=[block] * 4, out_specs=(block,) * 3, compiler_params=_params("parallel"),
    )(w, g, m, v)


WEIGHTS = ("meta_tokens", "mix_pre_norm", "mix_post_norm", "ffn_pre_norm", "ffn_post_norm", "w_in", "conv_qkv", "a_log",
           "dt_bias", "gdn_norm", "conv_sc", "w_out", "w_gate", "w_up", "w_down")


def kernel(x, meta_tokens, mix_pre_norm, mix_post_norm, ffn_pre_norm, ffn_post_norm, w_in, conv_qkv, a_log, dt_bias, gdn_norm, conv_sc, w_out, w_gate, w_up, w_down, loss_target, m_meta_tokens, m_mix_pre_norm, m_mix_post_norm, m_ffn_pre_norm, m_ffn_post_norm, m_w_in, m_conv_qkv, m_a_log, m_dt_bias, m_gdn_norm, m_conv_sc, m_w_out, m_w_gate, m_w_up, m_w_down, v_meta_tokens, v_mix_pre_norm, v_mix_post_norm, v_ffn_pre_norm, v_ffn_post_norm, v_w_in, v_conv_qkv, v_a_log, v_dt_bias, v_gdn_norm, v_conv_sc, v_w_out, v_w_gate, v_w_up, v_w_down):
    d = x.shape[-1]
    two_d = lambda a: a.reshape(a.shape[-2:])
    weights = dict(zip(WEIGHTS, (meta_tokens, mix_pre_norm, mix_post_norm, ffn_pre_norm, ffn_post_norm, w_in, conv_qkv, a_log,
                                 dt_bias, gdn_norm, conv_sc, w_out, w_gate, w_up, w_down)))
    m_in = dict(zip(WEIGHTS, (m_meta_tokens, m_mix_pre_norm, m_mix_post_norm, m_ffn_pre_norm, m_ffn_post_norm, m_w_in, m_conv_qkv,
                              m_a_log, m_dt_bias, m_gdn_norm, m_conv_sc, m_w_out, m_w_gate, m_w_up, m_w_down)))
    v_in = dict(zip(WEIGHTS, (v_meta_tokens, v_mix_pre_norm, v_mix_post_norm, v_ffn_pre_norm, v_ffn_post_norm, v_w_in, v_conv_qkv,
                              v_a_log, v_dt_bias, v_gdn_norm, v_conv_sc, v_w_out, v_w_gate, v_w_up, v_w_down)))
    core = lax.axis_index("c")
    chip = 2 * lax.axis_index("x") + lax.axis_index("y")
    core_arg = core.reshape(1).astype(jnp.int32)
    chip_core = jnp.stack([chip, core]).astype(jnp.int32)
    small_shapes = [two_d(weights[n]).shape for n in ("conv_qkv", "conv_sc", "meta_tokens")]
    whole = lambda a: a.reshape(a.shape[:-3] + (2 * a.shape[-2], d))

    def stored(params, n):
        if n == "w_in":
            return _stored(params[n]).reshape(IN_SHARD, d)
        return two_d(params[n]).T if n in ("w_gate", "w_up") else two_d(params[n])

    shard = {n: stored(weights, n).astype(MXU_DTYPE) for n in MATRICES}
    shard["w_in"] = jnp.pad(shard["w_in"], ((0, IN_SHARD_PAD - IN_SHARD), (0, 0)))
    s_flat = _flatten([two_d(weights[n]) for n in ("conv_qkv", "conv_sc", "meta_tokens")], SMALL_ROWS)
    w_in_all, s_all = _gather_weights([_halves(shard["w_in"])], s_flat)
    w_in_t = _to_padded_in(whole(w_in_all)[:, :IN_SHARD].reshape(IN_WIDTH, d))
    small_chip = [_unflatten(s_all[p], small_shapes) for p in range(N_CHIPS)]
    conv_qkv_full, conv_sc_full, meta_full = (jnp.concatenate([small_chip[p][i] for p in range(N_CHIPS)], axis=1)
                                              for i in range(3))

    sq, grad_x, g, later = _local_step(
        x, loss_target, meta_full, (mix_pre_norm, mix_post_norm, ffn_pre_norm, ffn_post_norm), w_in_t, conv_qkv_full, a_log,
        dt_bias, gdn_norm, conv_sc_full, [_halves(shard[n]) for n in LATER], core_arg)

    g_in = jnp.pad(_from_padded_in(g["w_in"]).reshape(N_CHIPS, IN_SHARD, d), ((0, 0), (0, IN_SHARD_PAD - IN_SHARD), (0, 0)))
    scalars = jnp.concatenate([g["a_log"], g["dt_bias"], sq], axis=1)
    small = _flatten([g["mix_pre_norm"], g["mix_post_norm"], g["ffn_pre_norm"], g["ffn_post_norm"],
                      jnp.pad(scalars, ((0, 0), (0, LANES - scalars.shape[1]))), g["gdn_norm"], g["conv_qkv"], g["conv_sc"],
                      g["meta_tokens"]], REDUCE_ROWS)
    got_sibling, small_all = _exchange_siblings([_halves(g_in)], small)
    part_in, send_in = _add_sibling(_halves(g_in), got_sibling, core_arg, "w_in")
    sums = dict(zip(LATER, later), w_in=(part_in, _exchange_chips([send_in])[0]))
    totals = [_add_chips(*sums[n], chip_core, n) for n in MATRICES]
    grads = {n: whole(a) for n, a in zip(MATRICES, _share_halves(totals))}
    reduced = _sum_devices(small_all)

    r = reduced.reshape(-1)
    at = 0
    for n in ("mix_pre_norm", "mix_post_norm", "ffn_pre_norm", "ffn_post_norm"):
        grads[n] = r[at:at + d].reshape(1, d)
        at += d
    grads["a_log"] = r[at:at + HEADS].reshape(1, HEADS)
    grads["dt_bias"] = r[at + HEADS:at + 2 * HEADS].reshape(1, HEADS)
    loss = (0.5 / d) * r[at + 2 * HEADS]
    at += LANES
    grads["gdn_norm"] = r[at:at + HEAD_DIM].reshape(1, HEAD_DIM)
    at += HEAD_DIM
    for n, shape in (("conv_qkv", (GDN_CONV, 3 * GDN_WIDTH)), ("conv_sc", (SC_CONV, SC_WIDTH)), ("meta_tokens", (N_META, d))):
        full_grad = r[at:at + shape[0] * shape[1]].reshape(shape)
        at += shape[0] * shape[1]
        width = shape[1] // N_CHIPS
        grads[n] = lax.dynamic_slice_in_dim(full_grad, chip * width, width, axis=1)

    outs = [[], [], [], []]
    for n in WEIGHTS:
        shape = weights[n].shape
        grad = grads[n]
        if n == "w_in":
            to_kernel, from_kernel = _stored, functools.partial(_unstored, d=d)
            grad = grad[:IN_SHARD].reshape(-1, LANES)
        elif n in ("w_gate", "w_up"):
            to_kernel, from_kernel = (lambda a: two_d(a).T), (lambda a: a.T.reshape(shape))
        else:
            to_kernel, from_kernel = two_d, (lambda a: a.reshape(shape))
        delta, new_m, new_v = _adamw(to_kernel(weights[n]), grad, to_kernel(m_in[n]), to_kernel(v_in[n]), n)
        for out, a in zip(outs, (grad, delta, new_m, new_v)):
            out.append(from_kernel(a))
    return (loss, grad_x, *outs[0], *outs[1], *outs[2], *outs[3])
```

```python
import functools

import jax
import jax.numpy as jnp
from jax import lax
from jax.experimental import pallas as pl
from jax.experimental.pallas import tpu as pltpu

F32 = jnp.float32
BF16 = jnp.bfloat16
MXU_DTYPE = jnp.bfloat16
MESH = pl.DeviceIdType.MESH

D_MODEL = 1024
N_META = 16
HEADS = 4
HEAD_DIM = 128
GDN_WIDTH = HEADS * HEAD_DIM
GDN_CONV = 4
CHUNK = 64
SC_WIDTH = D_MODEL - GDN_WIDTH
SC_CONV = 3
D_FF = 2816
IN_WIDTH = 4 * GDN_WIDTH + 2 * HEADS + 3 * SC_WIDTH
IN_PAD = 3840
BA_COL = (4 * GDN_WIDTH + 3 * SC_WIDTH) // 128
EPS = 1e-6
LANES = 128
N_CHIPS = 4
VMEM_LIMIT = 48 * 2 ** 20

ADAM_LR = 0.001
ADAM_B1 = 0.9
ADAM_B2 = 0.999
ADAM_EPS = 1e-08
ADAM_WD = 0.01
ADAM_STEP = 10


def _pick(n, candidates):
    for c in candidates:
        if n % c == 0:
            return c
    return n


def _row_tile(n):
    return _pick(n, (352, 256, 176, 128, 64, 32, 16, 8))


def _params(*sem):
    return pltpu.CompilerParams(dimension_semantics=sem, vmem_limit_bytes=VMEM_LIMIT)


def _sigmoid(x):
    return 1.0 / (1.0 + jnp.exp(-x))


def _softplus(x):
    return jnp.maximum(x, 0.0) + jnp.log(1.0 + jnp.exp(-jnp.abs(x)))


def _dsilu(x, s):
    return s * (1.0 + x * (1.0 - s))


def _mm(a, b, mode, out_dtype, name, init=None):
    if mode == "tn":
        k_dim, m_dim = a.shape
    else:
        m_dim, k_dim = a.shape
    n_dim = b.shape[0] if mode == "nt" else b.shape[1]
    tm = _pick(m_dim, (1408, 1280, 1024, 512, 256, 128) if mode == "tn" else (1056, 1024, 704, 512, 256, 128))
    tn = _pick(n_dim, (1408, 1280, 1024, 768, 512, 256, 128))
    tk = _pick(k_dim, (1408, 1280, 1056, 1024, 512, 256, 128))
    nk = k_dim // tk
    if mode == "nn":
        a_spec = pl.BlockSpec((tm, tk), lambda i, j, k: (i, k))
        b_spec = pl.BlockSpec((tk, tn), lambda i, j, k: (k, j))
        dims = (((1,), (0,)), ((), ()))
    elif mode == "nt":
        a_spec = pl.BlockSpec((tm, tk), lambda i, j, k: (i, k))
        b_spec = pl.BlockSpec((tn, tk), lambda i, j, k: (j, k))
        dims = (((1,), (1,)), ((), ()))
    else:
        a_spec = pl.BlockSpec((tk, tm), lambda i, j, k: (k, i))
        b_spec = pl.BlockSpec((tk, tn), lambda i, j, k: (k, j))
        dims = (((0,), (0,)), ((), ()))

    out_spec = pl.BlockSpec((tm, tn), lambda i, j, k: (i, j))

    def body(a_ref, b_ref, *rest):
        o_ref, acc_ref = rest[-2:]
        k = pl.program_id(2)
        p = lax.dot_general(a_ref[...], b_ref[...], dims, preferred_element_type=F32)

        @pl.when(k == 0)
        def _():
            acc_ref[...] = p if init is None else rest[0][...] + p

        @pl.when(k > 0)
        def _():
            acc_ref[...] += p

        @pl.when(k == nk - 1)
        def _():
            o_ref[...] = acc_ref[...].astype(out_dtype)

    return pl.pallas_call(
        body, name=name,
        out_shape=jax.ShapeDtypeStruct((m_dim, n_dim), out_dtype),
        grid=(m_dim // tm, n_dim // tn, nk),
        in_specs=[a_spec, b_spec] + ([] if init is None else [out_spec]),
        out_specs=out_spec,
        scratch_shapes=[pltpu.VMEM((tm, tn), F32)],
        compiler_params=_params("parallel", "parallel", "arbitrary"),
    )(*((a, b) if init is None else (a, b, init)))


def _rms_apply(x, w):
    r = lax.rsqrt(jnp.mean(x * x, axis=-1, keepdims=True) + EPS)
    return x * r * w


def _rms_bwd(x, w, dy):
    r = lax.rsqrt(jnp.mean(x * x, axis=-1, keepdims=True) + EPS)
    xh = x * r
    dyw = dy * w
    dx = r * (dyw - xh * jnp.mean(dyw * xh, axis=-1, keepdims=True))
    return dx, jnp.sum(dy * xh, axis=0, keepdims=True)


def _accumulate(ref, first, value):
    @pl.when(first)
    def _():
        ref[...] = value

    @pl.when(jnp.logical_not(first))
    def _():
        ref[...] += value


def _rows(tr, width):
    return pl.BlockSpec((tr, width), lambda i: (i, 0))


def _vec(width):
    return pl.BlockSpec((1, width), lambda i: (0, 0))


def _rms_fwd(h, w, name):
    n, d = h.shape
    tr = _row_tile(n)

    def body(h_ref, w_ref, u_ref):
        u_ref[...] = _rms_apply(h_ref[...], w_ref[...]).astype(u_ref.dtype)

    return pl.pallas_call(
        body, name=name, out_shape=jax.ShapeDtypeStruct((n, d), MXU_DTYPE), grid=(n // tr,),
        in_specs=[_rows(tr, d), _vec(d)], out_specs=_rows(tr, d), compiler_params=_params("parallel"),
    )(h, w)


def _mix_residual(h0, mix, w_post, w_pre):
    n, d = h0.shape
    tr = _row_tile(n)

    def body(h0_ref, mix_ref, wpost_ref, wpre_ref, h1_ref, u2_ref):
        h1 = h0_ref[...] + _rms_apply(mix_ref[...], wpost_ref[...])
        h1_ref[...] = h1
        u2_ref[...] = _rms_apply(h1, wpre_ref[...]).astype(u2_ref.dtype)

    return pl.pallas_call(
        body, name="mix_residual",
        out_shape=(jax.ShapeDtypeStruct((n, d), F32), jax.ShapeDtypeStruct((n, d), MXU_DTYPE)), grid=(n // tr,),
        in_specs=[_rows(tr, d), _rows(tr, d), _vec(d), _vec(d)], out_specs=(_rows(tr, d), _rows(tr, d)),
        compiler_params=_params("parallel"),
    )(h0, mix, w_post, w_pre)


NT_DIMS = (((1,), (1,)), ((), ()))


def _ffn_tiles(n):
    return _pick(n, (704, 512, 256, 128)), _pick(D_FF, (1408, 256, 128))


def _swiglu_fwd(u, w_gate_t, w_up_t):
    n, d = u.shape
    tm, tn = _ffn_tiles(n)

    def body(u_ref, wg_ref, wu_ref, g_ref, up_ref, act_ref):
        a = u_ref[...]
        g = lax.dot_general(a, wg_ref[...], NT_DIMS, preferred_element_type=F32)
        up = lax.dot_general(a, wu_ref[...], NT_DIMS, preferred_element_type=F32)
        g_ref[...] = g
        up_ref[...] = up
        act_ref[...] = (g * _sigmoid(g) * up).astype(act_ref.dtype)

    tile = pl.BlockSpec((tm, tn), lambda j, i: (i, j))
    weight = pl.BlockSpec((tn, d), lambda j, i: (j, 0))
    wide = jax.ShapeDtypeStruct((n, D_FF), F32)
    return pl.pallas_call(
        body, name="swiglu_fwd", out_shape=(wide, wide, jax.ShapeDtypeStruct((n, D_FF), MXU_DTYPE)),
        grid=(D_FF // tn, n // tm),
        in_specs=[pl.BlockSpec((tm, d), lambda j, i: (i, 0)), weight, weight], out_specs=(tile, tile, tile),
        compiler_params=_params("parallel", "parallel"),
    )(u, w_gate_t, w_up_t)


def _swiglu_bwd(dffn, w_down, gate, up):
    n, d = dffn.shape
    tm, tn = _ffn_tiles(n)

    def body(dy_ref, w_ref, g_ref, u_ref, dg_ref, du_ref):
        da = lax.dot_general(dy_ref[...], w_ref[...], NT_DIMS, preferred_element_type=F32)
        g = g_ref[...]
        s = _sigmoid(g)
        dg_ref[...] = (da * u_ref[...] * _dsilu(g, s)).astype(dg_ref.dtype)
        du_ref[...] = (da * g * s).astype(du_ref.dtype)

    tile = pl.BlockSpec((tm, tn), lambda j, i: (i, j))
    shape = jax.ShapeDtypeStruct((n, D_FF), MXU_DTYPE)
    return pl.pallas_call(
        body, name="swiglu_bwd", out_shape=(shape, shape), grid=(D_FF // tn, n // tm),
        in_specs=[pl.BlockSpec((tm, d), lambda j, i: (i, 0)), pl.BlockSpec((tn, d), lambda j, i: (j, 0)), tile, tile],
        out_specs=(tile, tile), compiler_params=_params("parallel", "parallel"),
    )(dffn, w_down, gate, up)


def _loss_head(h1, ffn, w_post, target, rows_per_seq, x_offset):
    n, d = h1.shape
    tr = _row_tile(rows_per_seq)
    tiles_per_seq = rows_per_seq // tr

    def body(h1_ref, ffn_ref, w_ref, t_ref, dh2_ref, dffn_ref, dw_ref, sq_ref):
        i = pl.program_id(0)
        w = w_ref[...]
        f = ffn_ref[...]
        r = lax.rsqrt(jnp.mean(f * f, axis=-1, keepdims=True) + EPS)
        fh = f * r
        row = lax.rem(i, tiles_per_seq) * tr + lax.broadcasted_iota(jnp.int32, (tr, 1), 0)
        err = jnp.where(row >= x_offset, h1_ref[...] + fh * w - t_ref[...], 0.0)
        dh2 = err * (1.0 / d)
        dh2_ref[...] = dh2
        dyw = dh2 * w
        dffn_ref[...] = (r * (dyw - fh * jnp.mean(dyw * fh, axis=-1, keepdims=True))).astype(dffn_ref.dtype)
        _accumulate(dw_ref, i == 0, jnp.sum(dh2 * fh, axis=0, keepdims=True))
        _accumulate(sq_ref, i == 0, jnp.sum(jnp.sum(err * err, axis=1, keepdims=True), axis=0, keepdims=True))

    return pl.pallas_call(
        body, name="loss_head",
        out_shape=(jax.ShapeDtypeStruct((n, d), F32), jax.ShapeDtypeStruct((n, d), MXU_DTYPE),
                   jax.ShapeDtypeStruct((1, d), F32), jax.ShapeDtypeStruct((1, 1), F32)),
        grid=(n // tr,),
        in_specs=[_rows(tr, d), _rows(tr, d), _vec(d), _rows(tr, d)],
        out_specs=(_rows(tr, d), _rows(tr, d), _vec(d), _vec(1)),
        compiler_params=_params("arbitrary"),
    )(h1, ffn, w_post, target)


def _mid_bwd(h1, mix, w_mix_post, w_ffn_pre, dh2, du2):
    n, d = h1.shape
    tr = _row_tile(n)

    def body(h1_ref, mix_ref, wpost_ref, wpre_ref, dh2_ref, du2_ref, dh1_ref, dmix_ref, dwpre_ref, dwpost_ref):
        i = pl.program_id(0)
        dx, dwpre = _rms_bwd(h1_ref[...], wpre_ref[...], du2_ref[...])
        dh1 = dh2_ref[...] + dx
        dh1_ref[...] = dh1
        dmix, dwpost = _rms_bwd(mix_ref[...], wpost_ref[...], dh1)
        dmix_ref[...] = dmix.astype(dmix_ref.dtype)
        _accumulate(dwpre_ref, i == 0, dwpre)
        _accumulate(dwpost_ref, i == 0, dwpost)

    return pl.pallas_call(
        body, name="mid_bwd",
        out_shape=(jax.ShapeDtypeStruct((n, d), F32), jax.ShapeDtypeStruct((n, d), MXU_DTYPE),
                   jax.ShapeDtypeStruct((1, d), F32), jax.ShapeDtypeStruct((1, d), F32)),
        grid=(n // tr,),
        in_specs=[_rows(tr, d), _rows(tr, d), _vec(d), _vec(d), _rows(tr, d), _rows(tr, d)],
        out_specs=(_rows(tr, d), _rows(tr, d), _vec(d), _vec(d)),
        compiler_params=_params("arbitrary"),
    )(h1, mix, w_mix_post, w_ffn_pre, dh2, du2)


def _in_bwd(h0, w_pre, dh1, du1):
    n, d = h0.shape
    tr = _row_tile(n)

    def body(h0_ref, w_ref, dh1_ref, du1_ref, dh0_ref, dw_ref):
        dx, dw = _rms_bwd(h0_ref[...], w_ref[...], du1_ref[...])
        dh0_ref[...] = dh1_ref[...] + dx
        _accumulate(dw_ref, pl.program_id(0) == 0, dw)

    return pl.pallas_call(
        body, name="in_bwd",
        out_shape=(jax.ShapeDtypeStruct((n, d), F32), jax.ShapeDtypeStruct((1, d), F32)), grid=(n // tr,),
        in_specs=[_rows(tr, d), _vec(d), _rows(tr, d), _rows(tr, d)], out_specs=(_rows(tr, d), _vec(d)),
        compiler_params=_params("arbitrary"),
    )(h0, w_pre, dh1, du1)


def _lane_is(lo, hi):
    lane = lax.broadcasted_iota(jnp.int32, (1, LANES), 1)
    return jnp.logical_and(lane >= lo, lane < hi)


def _gates_fwd(proj, a_log_l, dt_bias_l, rows_per_seq, pad_rows):
    n = proj.shape[0]
    tr = _row_tile(rows_per_seq)
    tiles_per_seq = rows_per_seq // tr

    def body(p_ref, a_ref, dt_ref, o_ref):
        x = p_ref[...]
        row = lax.rem(pl.program_id(0), tiles_per_seq) * tr + lax.broadcasted_iota(jnp.int32, (tr, 1), 0)
        g = -jnp.exp(a_ref[...]) * _softplus(x + dt_ref[...])
        val = jnp.where(_lane_is(0, HEADS), _sigmoid(x), jnp.where(_lane_is(HEADS, 2 * HEADS), g, 0.0))
        o_ref[...] = jnp.where(row >= pad_rows, val, 0.0)

    return pl.pallas_call(
        body, name="gates_fwd", out_shape=jax.ShapeDtypeStruct((n, LANES), F32), grid=(n // tr,),
        in_specs=[pl.BlockSpec((tr, LANES), lambda i: (i, BA_COL)), _vec(LANES), _vec(LANES)],
        out_specs=_rows(tr, LANES), compiler_params=_params("parallel"),
    )(proj, a_log_l, dt_bias_l)


def _gates_bwd(proj, dbg, a_log_l, dt_bias_l, rows_per_seq, pad_rows):
    n = proj.shape[0]
    tr = _row_tile(rows_per_seq)
    tiles_per_seq = rows_per_seq // tr

    def body(p_ref, d_ref, a_ref, dt_ref, dx_ref, da_ref, ddt_ref):
        i = pl.program_id(0)
        x = p_ref[...]
        d = d_ref[...]
        row = lax.rem(i, tiles_per_seq) * tr + lax.broadcasted_iota(jnp.int32, (tr, 1), 0)
        live = row >= pad_rows
        beta = _sigmoid(x)
        ea = jnp.exp(a_ref[...])
        xa = x + dt_ref[...]
        g = -ea * _softplus(xa)
        is_g = _lane_is(HEADS, 2 * HEADS)
        d_alogit = jnp.where(jnp.logical_and(live, is_g), d * (-ea) * _sigmoid(xa), 0.0)
        d_blogit = jnp.where(jnp.logical_and(live, _lane_is(0, HEADS)), d * beta * (1.0 - beta), 0.0)
        dx_ref[:, :LANES] = (d_alogit + d_blogit).astype(dx_ref.dtype)
        dx_ref[:, LANES:] = jnp.zeros((tr, LANES), dx_ref.dtype)
        _accumulate(da_ref, i == 0, jnp.sum(jnp.where(jnp.logical_and(live, is_g), d * g, 0.0), axis=0, keepdims=True))
        _accumulate(ddt_ref, i == 0, jnp.sum(d_alogit, axis=0, keepdims=True))

    return pl.pallas_call(
        body, name="gates_bwd",
        out_shape=(jax.ShapeDtypeStruct((n, 2 * LANES), MXU_DTYPE), jax.ShapeDtypeStruct((1, LANES), F32),
                   jax.ShapeDtypeStruct((1, LANES), F32)),
        grid=(n // tr,),
        in_specs=[pl.BlockSpec((tr, LANES), lambda i: (i, BA_COL)), _rows(tr, LANES), _vec(LANES), _vec(LANES)],
        out_specs=(_rows(tr, 2 * LANES), _vec(LANES), _vec(LANES)),
        compiler_params=_params("arbitrary"),
    )(proj, dbg, a_log_l, dt_bias_l)


def _shift_down(x, k):
    return x if k == 0 else pltpu.roll(x, k, 0)


def _shift_up(x, k):
    return x if k == 0 else pltpu.roll(x, x.shape[0] - k, 0)


def _causal_conv(x, w, width):
    acc = w[width - 1:width, :] * x
    for i in range(width - 1):
        acc = acc + w[i:i + 1, :] * _shift_down(x, width - 1 - i)
    return acc


def _seq_head(rs, col0):
    return pl.BlockSpec((rs, LANES), lambda j, b: (b, col0 + j))


def _live_rows(rs, pad_rows):
    return lax.broadcasted_iota(jnp.int32, (rs, 1), 0) >= pad_rows


def _qkv_fwd(proj, conv_w, kind, rs, pad_rows):
    n = proj.shape[0]
    col0 = {"q": 0, "k": HEADS, "v": 2 * HEADS}[kind]

    def body(p_ref, w_ref, o_ref):
        c = _causal_conv(p_ref[...], w_ref[...], GDN_CONV)
        s = c * _sigmoid(c)
        if kind != "v":
            s = s * lax.rsqrt(jnp.sum(s * s, axis=-1, keepdims=True) + EPS)
        if kind == "q":
            s = s * (HEAD_DIM ** -0.5)
        o_ref[...] = jnp.where(_live_rows(rs, pad_rows), s, 0.0)

    return pl.pallas_call(
        body, name="qkv_fwd_" + kind, out_shape=jax.ShapeDtypeStruct((n, GDN_WIDTH), F32), grid=(HEADS, n // rs),
        in_specs=[_seq_head(rs, col0), pl.BlockSpec((GDN_CONV, LANES), lambda j, b: (0, col0 + j))],
        out_specs=_seq_head(rs, 0), compiler_params=_params("parallel", "parallel"),
    )(proj, conv_w)


def _qkv_bwd(dy, proj, conv_w, kind, rs, pad_rows):
    n = proj.shape[0]
    col0 = {"q": 0, "k": HEADS, "v": 2 * HEADS}[kind]

    def body(dy_ref, p_ref, w_ref, dp_ref, dw_ref):
        pre = p_ref[...]
        w = w_ref[...]
        c = _causal_conv(pre, w, GDN_CONV)
        sg = _sigmoid(c)
        s = c * sg
        ds = dy_ref[...]
        if kind == "q":
            ds = ds * (HEAD_DIM ** -0.5)
        if kind != "v":
            r = lax.rsqrt(jnp.sum(s * s, axis=-1, keepdims=True) + EPS)
            sh = s * r
            ds = r * (ds - sh * jnp.sum(ds * sh, axis=-1, keepdims=True))
        dc = jnp.where(_live_rows(rs, pad_rows), ds * _dsilu(c, sg), 0.0)
        dpre = w[GDN_CONV - 1:GDN_CONV, :] * dc
        for i in range(GDN_CONV - 1):
            dpre = dpre + w[i:i + 1, :] * _shift_up(dc, GDN_CONV - 1 - i)
        dp_ref[...] = dpre.astype(dp_ref.dtype)
        dw = jnp.concatenate(
            [jnp.sum(dc * _shift_down(pre, GDN_CONV - 1 - i), axis=0, keepdims=True) for i in range(GDN_CONV)], axis=0)
        _accumulate(dw_ref, pl.program_id(1) == 0, dw)

    return pl.pallas_call(
        body, name="qkv_bwd_" + kind,
        out_shape=(jax.ShapeDtypeStruct((n, GDN_WIDTH), MXU_DTYPE), jax.ShapeDtypeStruct((GDN_CONV, GDN_WIDTH), F32)),
        grid=(HEADS, n // rs),
        in_specs=[_seq_head(rs, 0), _seq_head(rs, col0), pl.BlockSpec((GDN_CONV, LANES), lambda j, b: (0, col0 + j))],
        out_specs=(_seq_head(rs, 0), pl.BlockSpec((GDN_CONV, LANES), lambda j, b: (0, j))),
        compiler_params=_params("parallel", "arbitrary"),
    )(dy, proj, conv_w)


SC_COL = 4 * HEADS


def _sc_fwd(proj, conv_w, rs):
    n = proj.shape[0]

    def body(x_ref, b_ref, c_ref, w_ref, y_ref):
        y_ref[...] = (b_ref[...] * _causal_conv(c_ref[...] * x_ref[...], w_ref[...], SC_CONV)).astype(y_ref.dtype)

    return pl.pallas_call(
        body, name="sc_fwd", out_shape=jax.ShapeDtypeStruct((n, SC_WIDTH), MXU_DTYPE), grid=(HEADS, n // rs),
        in_specs=[_seq_head(rs, SC_COL), _seq_head(rs, SC_COL + 4), _seq_head(rs, SC_COL + 8),
                  pl.BlockSpec((SC_CONV, LANES), lambda j, b: (0, j))],
        out_specs=_seq_head(rs, 0), compiler_params=_params("parallel", "parallel"),
    )(proj, proj, proj, conv_w)


def _sc_bwd(dcat, proj, conv_w, rs):
    n = proj.shape[0]

    def body(dy_ref, x_ref, b_ref, c_ref, w_ref, dx_ref, db_ref, dc_ref, dw_ref):
        w = w_ref[...]
        x = x_ref[...]
        cc = c_ref[...]
        u = cc * x
        dy = dy_ref[...]
        db_ref[...] = (dy * _causal_conv(u, w, SC_CONV)).astype(db_ref.dtype)
        dcv = dy * b_ref[...]
        du = w[SC_CONV - 1:SC_CONV, :] * dcv
        for i in range(SC_CONV - 1):
            du = du + w[i:i + 1, :] * _shift_up(dcv, SC_CONV - 1 - i)
        dx_ref[...] = (du * cc).astype(dx_ref.dtype)
        dc_ref[...] = (du * x).astype(dc_ref.dtype)
        dw = jnp.concatenate(
            [jnp.sum(dcv * _shift_down(u, SC_CONV - 1 - i), axis=0, keepdims=True) for i in range(SC_CONV)], axis=0)
        _accumulate(dw_ref, pl.program_id(1) == 0, dw)

    piece = jax.ShapeDtypeStruct((n, SC_WIDTH), MXU_DTYPE)
    return pl.pallas_call(
        body, name="sc_bwd", out_shape=(piece, piece, piece, jax.ShapeDtypeStruct((SC_CONV, SC_WIDTH), F32)),
        grid=(HEADS, n // rs),
        in_specs=[_seq_head(rs, HEADS), _seq_head(rs, SC_COL), _seq_head(rs, SC_COL + 4), _seq_head(rs, SC_COL + 8),
                  pl.BlockSpec((SC_CONV, LANES), lambda j, b: (0, j))],
        out_specs=(_seq_head(rs, 0), _seq_head(rs, 0), _seq_head(rs, 0),
                   pl.BlockSpec((SC_CONV, LANES), lambda j, b: (0, j))),
        compiler_params=_params("parallel", "arbitrary"),
    )(dcat, proj, proj, proj, conv_w)


Z_COL = 3 * HEADS


def _gate_fwd(o, proj, gdn_norm, rs):
    n = proj.shape[0]

    def body(o_ref, z_ref, w_ref, y_ref):
        z = z_ref[...]
        y_ref[...] = (_rms_apply(o_ref[...], w_ref[...]) * z * _sigmoid(z)).astype(y_ref.dtype)

    return pl.pallas_call(
        body, name="gate_fwd", out_shape=jax.ShapeDtypeStruct((n, GDN_WIDTH), MXU_DTYPE), grid=(HEADS, n // rs),
        in_specs=[_seq_head(rs, 0), _seq_head(rs, Z_COL), pl.BlockSpec((1, LANES), lambda j, b: (0, 0))],
        out_specs=_seq_head(rs, 0), compiler_params=_params("parallel", "parallel"),
    )(o, proj, gdn_norm)


def _gate_bwd(dcat, o, proj, gdn_norm, rs):
    n = proj.shape[0]

    def body(dy_ref, o_ref, z_ref, w_ref, do_ref, dz_ref, dw_ref):
        z = z_ref[...]
        w = w_ref[...]
        o = o_ref[...]
        dy = dy_ref[...]
        s = _sigmoid(z)
        dz_ref[...] = (dy * _rms_apply(o, w) * _dsilu(z, s)).astype(dz_ref.dtype)
        do, dw = _rms_bwd(o, w, dy * z * s)
        do_ref[...] = do
        _accumulate(dw_ref, jnp.logical_and(pl.program_id(0) == 0, pl.program_id(1) == 0), dw)

    return pl.pallas_call(
        body, name="gate_bwd",
        out_shape=(jax.ShapeDtypeStruct((n, GDN_WIDTH), F32), jax.ShapeDtypeStruct((n, GDN_WIDTH), MXU_DTYPE),
                   jax.ShapeDtypeStruct((1, LANES), F32)),
        grid=(HEADS, n // rs),
        in_specs=[_seq_head(rs, 0), _seq_head(rs, 0), _seq_head(rs, Z_COL), pl.BlockSpec((1, LANES), lambda j, b: (0, 0))],
        out_specs=(_seq_head(rs, 0), _seq_head(rs, 0), pl.BlockSpec((1, LANES), lambda j, b: (0, 0))),
        compiler_params=_params("arbitrary", "arbitrary"),
    )(dcat, o, proj, gdn_norm)


def _dot(a, b):
    return jnp.dot(a.astype(MXU_DTYPE), b.astype(MXU_DTYPE), preferred_element_type=F32)


def _dot_nt(a, b):
    return lax.dot_general(a.astype(MXU_DTYPE), b.astype(MXU_DTYPE), (((1,), (1,)), ((), ())),
                           preferred_element_type=F32)


def _dot_tn(a, b):
    return lax.dot_general(a.astype(MXU_DTYPE), b.astype(MXU_DTYPE), (((0,), (0,)), ((), ())),
                           preferred_element_type=F32)


def _split(x):
    hi = x.astype(MXU_DTYPE)
    return hi, (x - hi.astype(F32)).astype(MXU_DTYPE)


def _dot_split(a, b):
    mm = functools.partial(jnp.dot, preferred_element_type=F32)
    return mm(a[0], b[0]) + (mm(a[0], b[1]) + mm(a[1], b[0]))


def _unit_lower_inverses(mats, eye):
    inv = [eye - a for a in mats]
    power = [_split(a) for a in mats]
    span = 2
    while span < CHUNK:
        power = [_split(_dot_split(p, p)) for p in power]
        inv = [i + _dot_split(_split(i), p) for i, p in zip(inv, power)]
        span *= 2
    return inv


def _chunk_masks():
    ii = lax.broadcasted_iota(jnp.int32, (CHUNK, CHUNK), 0)
    jj = lax.broadcasted_iota(jnp.int32, (CHUNK, CHUNK), 1)
    return ii, jj


def _chunk_decay(g_col, ii, jj):
    incl = ii >= jj
    g_row = jnp.sum(jnp.where(ii == jj, g_col, 0.0), axis=0, keepdims=True)
    gc_col = jnp.sum(jnp.where(incl, g_row, 0.0), axis=1, keepdims=True)
    gc_row = jnp.sum(jnp.where(ii <= jj, g_col, 0.0), axis=0, keepdims=True)
    g_total = jnp.sum(g_row, axis=1, keepdims=True)
    decay = jnp.where(incl, jnp.exp(jnp.where(incl, gc_col - gc_row, 0.0)), 0.0)
    return gc_col, g_total, decay


def _gdn_segments(rs, candidates):
    chunks = rs // CHUNK
    seg_chunks = _pick(chunks, candidates)
    return chunks, seg_chunks, chunks // seg_chunks


def _head_lanes(h):
    return slice(h * HEAD_DIM, (h + 1) * HEAD_DIM)


def _gdn_fwd(q, k, v, bg, rs, pieces):
    n = q.shape[0]
    batch = n // rs
    chunks, seg_chunks, segs = _gdn_segments(rs, (11, 8, 4, 2))
    seg_rows = seg_chunks * CHUNK
    chains = [(b, h) for b in range(batch) for h in range(HEADS)]
    each = lambda f, *lists: [f(*args) for args in zip(*lists)]
    count = len(pieces)

    def body(q_ref, k_ref, v_ref, bg_ref, *rest):
        w_refs, (o_ref, s_ref, t_ref), out_refs = rest[:count], rest[count:count + 3], rest[count + 3:2 * count + 3]
        state_ref, send_sems, recv_sems = rest[2 * count + 3:]
        gather = _gather_copies(w_refs, out_refs, send_sems, recv_sems)

        @pl.when(pl.program_id(0) == 0)
        def _():
            state_ref[...] = jnp.zeros_like(state_ref)
            for cp in gather[0]:
                cp.start()

        ii, jj = _chunk_masks()
        incl = ii >= jj
        eye = (ii == jj).astype(F32)

        def chunk(c, carry):
            rows = pl.ds(pl.multiple_of(c * CHUNK, CHUNK), CHUNK)
            bgc = [bg_ref[b, rows, :] for b in range(batch)]
            qc = [q_ref[b, rows, _head_lanes(h)] for b, h in chains]
            kc = [k_ref[b, rows, _head_lanes(h)] for b, h in chains]
            vc = [v_ref[b, rows, _head_lanes(h)] for b, h in chains]
            beta = [bgc[b][:, h:h + 1] for b, h in chains]
            state = [state_ref[b, h] for b, h in chains]
            dec = [_chunk_decay(bgc[b][:, HEADS + h:HEADS + h + 1], ii, jj) for b, h in chains]
            gc_col, g_total, decay = ([d[i] for d in dec] for i in range(3))
            kb = each(lambda x, y: x * y, kc, beta)
            a = each(lambda x, y, d: jnp.where(ii > jj, _dot_nt(x, y) * d, 0.0), kb, kc, decay)
            t_inv = _unit_lower_inverses(a, eye)
            eg = [jnp.exp(g) for g in gc_col]
            u = each(lambda t, x, y: _dot(t, x * y), t_inv, vc, beta)
            w = each(lambda t, x, e: _dot(t, x * e), t_inv, kb, eg)
            qk = each(lambda x, y, d: jnp.where(incl, _dot_nt(x, y) * d, 0.0), qc, kc, decay)
            v_new = each(lambda x, y, s: x - _dot(y, s), u, w, state)
            o = each(lambda x, e, s, m, vn: _dot(x * e, s) + _dot(m, vn), qc, eg, state, qk, v_new)
            new_state = each(lambda s, gt, x, g, vn: s * jnp.exp(gt) + _dot_tn(x * jnp.exp(gt - g), vn),
                             state, g_total, kc, gc_col, v_new)
            for i, (b, h) in enumerate(chains):
                s_ref[b, h, c] = state[i]
                t_ref[b, h, c] = t_inv[i]
                o_ref[b, rows, _head_lanes(h)] = o[i]
                state_ref[b, h] = new_state[i]
            return carry

        lax.fori_loop(0, seg_chunks, chunk, 0)

        @pl.when(pl.program_id(0) == segs - 1)
        def _():
            _gather_finish(gather)

    rows_spec = lambda width: pl.BlockSpec((batch, seg_rows, width), lambda s: (0, s, 0))
    per_chunk = lambda r, c: pl.BlockSpec((batch, HEADS, seg_chunks, r, c), lambda s: (0, 0, s, 0, 0))
    as_seqs = lambda a: a.reshape(batch, rs, a.shape[-1])
    sems = GATHER_SEMS * count
    o, states, t_invs, *gathered = pl.pallas_call(
        body, name="gdn_fwd",
        out_shape=(jax.ShapeDtypeStruct((batch, rs, GDN_WIDTH), F32),
                   jax.ShapeDtypeStruct((batch, HEADS, chunks, HEAD_DIM, HEAD_DIM), F32),
                   jax.ShapeDtypeStruct((batch, HEADS, chunks, CHUNK, CHUNK), F32))
        + tuple(jax.ShapeDtypeStruct((N_CHIPS,) + p.shape, p.dtype) for p in pieces),
        grid=(segs,),
        in_specs=[rows_spec(GDN_WIDTH), rows_spec(GDN_WIDTH), rows_spec(GDN_WIDTH), rows_spec(LANES)] + [_hbm()] * count,
        out_specs=(rows_spec(GDN_WIDTH), per_chunk(HEAD_DIM, HEAD_DIM), per_chunk(CHUNK, CHUNK)) + (_hbm(),) * count,
        scratch_shapes=[pltpu.VMEM((batch, HEADS, HEAD_DIM, HEAD_DIM), F32), pltpu.SemaphoreType.DMA((sems,)),
                        pltpu.SemaphoreType.DMA((sems,))],
        compiler_params=_params("arbitrary"),
    )(as_seqs(q), as_seqs(k), as_seqs(v), as_seqs(bg), *pieces)
    return o.reshape(n, GDN_WIDTH), states, t_invs, gathered


def _gdn_bwd(do, q, k, v, bg, states, t_invs, rs, parts):
    n = q.shape[0]
    batch = n // rs
    chunks, seg_chunks, segs = _gdn_segments(rs, (3, 4, 2))
    seg_rows = seg_chunks * CHUNK
    chains = [(b, h) for b in range(batch) for h in range(HEADS)]
    each = lambda f, *lists: [f(*args) for args in zip(*lists)]
    count = len(parts)

    def body(do_ref, q_ref, k_ref, v_ref, bg_ref, s_ref, t_ref, *rest):
        p_refs, (dq_ref, dk_ref, dv_ref, dbg_ref), got_refs = rest[:count], rest[count:count + 4], rest[count + 4:2 * count + 4]
        dstate_ref, send_sems, recv_sems = rest[2 * count + 4:]
        exchange = _chip_copies(p_refs, got_refs, send_sems, recv_sems)

        @pl.when(pl.program_id(0) == 0)
        def _():
            dstate_ref[...] = jnp.zeros_like(dstate_ref)
            for cp in exchange:
                cp.start()

        ii, jj = _chunk_masks()
        incl = ii >= jj
        strict = ii > jj
        lane = lax.broadcasted_iota(jnp.int32, (1, LANES), 1)

        def rowsum(x):
            return jnp.sum(x, axis=1, keepdims=True)

        def total(x):
            return jnp.sum(rowsum(x), axis=0, keepdims=True)

        def chunk(step, carry):
            c = seg_chunks - 1 - step
            rows = pl.ds(pl.multiple_of(c * CHUNK, CHUNK), CHUNK)
            bgc = [bg_ref[b, rows, :] for b in range(batch)]
            qc = [q_ref[b, rows, _head_lanes(h)] for b, h in chains]
            kc = [k_ref[b, rows, _head_lanes(h)] for b, h in chains]
            vc = [v_ref[b, rows, _head_lanes(h)] for b, h in chains]
            doc = [do_ref[b, rows, _head_lanes(h)] for b, h in chains]
            beta = [bgc[b][:, h:h + 1] for b, h in chains]
            state = [s_ref[b, h, c] for b, h in chains]
            t_inv = [t_ref[b, h, c] for b, h in chains]
            d_state = [dstate_ref[b, h] for b, h in chains]
            dec = [_chunk_decay(bgc[b][:, HEADS + h:HEADS + h + 1], ii, jj) for b, h in chains]
            gc_col, g_total, decay = ([d[i] for d in dec] for i in range(3))
            kb = each(lambda x, y: x * y, kc, beta)
            vb = each(lambda x, y: x * y, vc, beta)
            eg = [jnp.exp(g) for g in gc_col]
            kbg = each(lambda x, y: x * y, kb, eg)
            a = each(lambda x, y, d: jnp.where(strict, _dot_nt(x, y) * d, 0.0), kb, kc, decay)
            qk = each(lambda x, y, d: jnp.where(incl, _dot_nt(x, y) * d, 0.0), qc, kc, decay)
            w = each(_dot, t_inv, kbg)
            u = each(_dot, t_inv, vb)
            q_dec = each(lambda x, y: x * y, qc, eg)
            ek = each(lambda gt, g: jnp.exp(gt - g), g_total, gc_col)
            k_dec = each(lambda x, y: x * y, kc, ek)
            g_last = [jnp.exp(gt) for gt in g_total]
            v_new = each(lambda x, y, s: x - _dot(y, s), u, w, state)
            dv_new = each(lambda m, d, x, ds: _dot_tn(m, d) + _dot(x, ds), qk, doc, k_dec, d_state)
            dqk = each(lambda d, vn: jnp.where(incl, _dot_nt(d, vn), 0.0), doc, v_new)
            dq_dec = each(_dot_nt, doc, state)
            dk_dec = each(_dot_nt, v_new, d_state)
            dg_last = each(lambda s, ds: total(s * ds), state, d_state)
            new_d_state = each(lambda x, d, gl, ds, y, dvn: _dot_tn(x, d) + gl * ds - _dot_tn(y, dvn),
                               q_dec, doc, g_last, d_state, w, dv_new)
            dw = each(lambda dvn, s: -_dot_nt(dvn, s), dv_new, state)
            dt = each(lambda dvn, x, y, z: _dot_nt(dvn, x) + _dot_nt(y, z), dv_new, vb, dw, kbg)
            dvb = each(_dot_tn, t_inv, dv_new)
            dkbg = each(_dot_tn, t_inv, dw)
            t_dt = each(_dot_tn, t_inv, dt)
            da = each(lambda x, t: -jnp.where(strict, _dot_nt(x, t), 0.0), t_dt, t_inv)
            dm_a = each(lambda x, y: x * y, da, decay)
            dm_qk = each(lambda x, y: x * y, dqk, decay)
            e = each(lambda x, y, z, t: x * y + z * t, da, a, dqk, qk)
            dkb = each(lambda m, x, y, z: _dot(m, x) + y * z, dm_a, kc, dkbg, eg)
            dk = each(lambda m, x, m2, y, z, t, p, bt: _dot_tn(m, x) + _dot_tn(m2, y) + z * t + p * bt,
                      dm_a, kb, dm_qk, qc, dk_dec, ek, dkb, beta)
            dq = each(lambda m, x, y, z: _dot(m, x) + y * z, dm_qk, kc, dq_dec, eg)
            dbeta = each(lambda x, y, z, t: rowsum(x * y + z * t), dkb, kc, dvb, vc)
            dgc = each(lambda x, p, pd, r, rd, s, sd: rowsum(x) - rowsum(jnp.where(ii == jj, jnp.sum(x, axis=0, keepdims=True), 0.0))
                       + rowsum(p * pd - r * rd + s * sd), e, dq_dec, q_dec, dk_dec, k_dec, dkbg, kbg)
            d_total = each(lambda r, rd, x, gl: total(r * rd) + x * gl, dk_dec, k_dec, dg_last, g_last)
            dg = each(lambda x, t: rowsum(jnp.where(jj >= ii, jnp.sum(jnp.where(ii == jj, x, 0.0), axis=0, keepdims=True), 0.0)) + t,
                      dgc, d_total)
            dbg = [jnp.zeros((CHUNK, LANES), F32) for _ in range(batch)]
            for i, (b, h) in enumerate(chains):
                dstate_ref[b, h] = new_d_state[i]
                dk_ref[b, rows, _head_lanes(h)] = dk[i]
                dq_ref[b, rows, _head_lanes(h)] = dq[i]
                dv_ref[b, rows, _head_lanes(h)] = dvb[i] * beta[i]
                dbg[b] = dbg[b] + jnp.where(lane == h, dbeta[i], 0.0) + jnp.where(lane == HEADS + h, dg[i], 0.0)
            for b in range(batch):
                dbg_ref[b, rows, :] = dbg[b]
            return carry

        lax.fori_loop(0, seg_chunks, chunk, 0)

        @pl.when(pl.program_id(0) == segs - 1)
        def _():
            for cp in exchange:
                cp.wait_recv()
            for cp in exchange:
                cp.wait_send()

    rows_spec = lambda width: pl.BlockSpec((batch, seg_rows, width), lambda s: (0, segs - 1 - s, 0))
    per_chunk = lambda r, c: pl.BlockSpec((batch, HEADS, seg_chunks, r, c), lambda s: (0, 0, segs - 1 - s, 0, 0))
    as_seqs = lambda a: a.reshape(batch, rs, a.shape[-1])
    grad = jax.ShapeDtypeStruct((batch, rs, GDN_WIDTH), F32)
    wide = rows_spec(GDN_WIDTH)
    dq, dk, dv, dbg, *got = pl.pallas_call(
        body, name="gdn_bwd",
        out_shape=(grad, grad, grad, jax.ShapeDtypeStruct((batch, rs, LANES), F32))
        + tuple(jax.ShapeDtypeStruct((3,) + p.shape[1:], p.dtype) for p in parts),
        grid=(segs,),
        in_specs=[wide, wide, wide, wide, rows_spec(LANES), per_chunk(HEAD_DIM, HEAD_DIM), per_chunk(CHUNK, CHUNK)]
        + [_hbm()] * count,
        out_specs=(wide, wide, wide, rows_spec(LANES)) + (_hbm(),) * count,
        scratch_shapes=[pltpu.VMEM((batch, HEADS, HEAD_DIM, HEAD_DIM), F32), pltpu.SemaphoreType.DMA((3 * count,)),
                        pltpu.SemaphoreType.DMA((3 * count,))],
        compiler_params=_params("arbitrary"),
    )(as_seqs(do), as_seqs(q), as_seqs(k), as_seqs(v), as_seqs(bg), states, t_invs, *parts)
    return dq.reshape(n, GDN_WIDTH), dk.reshape(n, GDN_WIDTH), dv.reshape(n, GDN_WIDTH), dbg.reshape(n, LANES), got


def _lane_vec(vals, offset):
    k = vals.shape[1]
    return jnp.pad(vals, ((0, 0), (offset, LANES - offset - k)))


LATER = ("w_out", "w_gate", "w_up", "w_down")


def _halves(a):
    return a.reshape(a.shape[:-2] + (2, a.shape[-2] // 2, a.shape[-1]))


def _local_step(x, target, meta, norms, w_in_t, conv_qkv, a_log, dt_bias, gdn_norm, conv_sc, later_shards, core_arg):
    batch, seq, d = x.shape
    tokens = N_META + seq
    pad_rows = (-tokens) % CHUNK
    rs = tokens + pad_rows
    x_offset = pad_rows + N_META
    n = batch * rs
    w_mix_pre, w_mix_post, w_ffn_pre, w_ffn_post = norms

    head = jnp.concatenate([jnp.zeros((pad_rows, d), F32), meta], axis=0)
    h0 = jnp.concatenate([jnp.broadcast_to(head[None], (batch, x_offset, d)), x], axis=1).reshape(n, d)
    target_p = jnp.pad(target, ((0, 0), (x_offset, 0), (0, 0))).reshape(n, d)
    a_log_l = _lane_vec(a_log, HEADS)
    dt_bias_l = _lane_vec(dt_bias, HEADS)

    u1 = _rms_fwd(h0, w_mix_pre, "rms_mix_pre")
    proj = _mm(u1, w_in_t, "nt", F32, "mm_proj")
    q = _qkv_fwd(proj, conv_qkv, "q", rs, pad_rows)
    k = _qkv_fwd(proj, conv_qkv, "k", rs, pad_rows)
    v = _qkv_fwd(proj, conv_qkv, "v", rs, pad_rows)
    bg = _gates_fwd(proj, a_log_l, dt_bias_l, rs, pad_rows)
    o, states, t_invs, gathered = _gdn_fwd(q, k, v, bg, rs, later_shards)
    w_out, w_gate_t, w_up_t, w_down = (a.reshape(-1, d) for a in gathered)
    o_gated = _gate_fwd(o, proj, gdn_norm, rs)
    y_sc = _sc_fwd(proj, conv_sc, rs)
    cat = jnp.concatenate([o_gated, y_sc], axis=1)
    mix = _mm(cat, w_out, "nn", F32, "mm_mix")
    h1, u2 = _mix_residual(h0, mix, w_mix_post, w_ffn_pre)
    gate, up, act = _swiglu_fwd(u2, w_gate_t, w_up_t)
    ffn = _mm(act, w_down, "nn", F32, "mm_down")

    dh2, dffn, d_ffn_post, sq = _loss_head(h1, ffn, w_ffn_post, target_p, rs, x_offset)
    d_w_down = _mm(act, dffn, "tn", F32, "mm_dw_down")
    dgate, dup = _swiglu_bwd(dffn, w_down, gate, up)
    d_w_gate_t = _mm(dgate, u2, "tn", F32, "mm_dw_gate")
    d_w_up_t = _mm(dup, u2, "tn", F32, "mm_dw_up")
    du2 = _mm(dup, w_up_t, "nn", F32, "mm_du2_up", init=_mm(dgate, w_gate_t, "nn", F32, "mm_du2_gate"))
    dh1, dmix, d_ffn_pre, d_mix_post = _mid_bwd(h1, mix, w_mix_post, w_ffn_pre, dh2, du2)
    dcat = _mm(dmix, w_out, "nt", F32, "mm_dcat")
    d_w_out = _mm(cat, dmix, "tn", F32, "mm_dw_out")
    do, dz, d_gdn_norm = _gate_bwd(dcat, o, proj, gdn_norm, rs)
    dscx, dscb, dscc, d_conv_sc = _sc_bwd(dcat, proj, conv_sc, rs)
    by_chip = [_halves(g.reshape(N_CHIPS, -1, d)) for g in (d_w_out, d_w_gate_t, d_w_up_t, d_w_down)]
    sums = [_add_sibling(a, b, core_arg, name) for name, a, b in zip(LATER, by_chip, _exchange_siblings(by_chip))]
    dq, dk, dv, dbg, got_chips = _gdn_bwd(do, q, k, v, bg, states, t_invs, rs, [send for _, send in sums])
    dpq, dwq = _qkv_bwd(dq, proj, conv_qkv, "q", rs, pad_rows)
    dpk, dwk = _qkv_bwd(dk, proj, conv_qkv, "k", rs, pad_rows)
    dpv, dwv = _qkv_bwd(dv, proj, conv_qkv, "v", rs, pad_rows)
    d_conv_qkv = jnp.concatenate([dwq, dwk, dwv], axis=1)
    dba, d_a_log_l, d_dt_bias_l = _gates_bwd(proj, dbg, a_log_l, dt_bias_l, rs, pad_rows)
    dproj = jnp.concatenate([dpq, dpk, dpv, dz, dscx, dscb, dscc, dba], axis=1)
    d_w_in_t = _mm(dproj, u1, "tn", F32, "mm_dw_in")
    du1 = _mm(dproj, w_in_t, "nn", F32, "mm_du1")
    dh0, d_mix_pre = _in_bwd(h0, w_mix_pre, dh1, du1)

    dh0 = dh0.reshape(batch, rs, d)
    grads = dict(
        meta_tokens=jnp.sum(dh0[:, pad_rows:x_offset], axis=0),
        mix_pre_norm=d_mix_pre, mix_post_norm=d_mix_post, ffn_pre_norm=d_ffn_pre, ffn_post_norm=d_ffn_post,
        w_in=d_w_in_t, conv_qkv=d_conv_qkv,
        a_log=d_a_log_l[:, HEADS:2 * HEADS], dt_bias=d_dt_bias_l[:, HEADS:2 * HEADS],
        gdn_norm=d_gdn_norm, conv_sc=d_conv_sc,
    )
    return sq, dh0[:, x_offset:], grads, [(part, got) for (part, _), got in zip(sums, got_chips)]


MATRICES = ("w_in", "w_out", "w_gate", "w_up", "w_down")
IN_SHARD = IN_WIDTH // N_CHIPS
IN_SHARD_PAD = 928


def _in_to_kernel_order(by_chip):
    w_t = by_chip[:, :IN_SHARD].reshape(IN_WIDTH, by_chip.shape[-1])
    lo, hi = 4 * GDN_WIDTH, 4 * GDN_WIDTH + 2 * HEADS
    return jnp.concatenate([w_t[:lo], w_t[hi:], w_t[lo:hi], jnp.zeros((IN_PAD - IN_WIDTH, w_t.shape[1]), w_t.dtype)], axis=0)


def _in_from_kernel_order(g_t):
    lo, hi = 4 * GDN_WIDTH, IN_WIDTH - 2 * HEADS
    g = jnp.concatenate([g_t[:lo], g_t[hi:IN_WIDTH], g_t[lo:hi]], axis=0).reshape(N_CHIPS, IN_SHARD, g_t.shape[-1])
    return jnp.pad(g, ((0, 0), (0, IN_SHARD_PAD - IN_SHARD), (0, 0)))


PACK_LANES = 3 * GDN_WIDTH
PACKED = dict(mix_pre_norm=(0, 1, D_MODEL), mix_post_norm=(1, 1, D_MODEL), ffn_pre_norm=(2, 1, D_MODEL),
              ffn_post_norm=(3, 1, D_MODEL), a_log=(4, 1, HEADS), dt_bias=(5, 1, HEADS), loss=(6, 1, 1),
              gdn_norm=(7, 1, HEAD_DIM), conv_qkv=(8, GDN_CONV, 3 * GDN_WIDTH), conv_sc=(16, SC_CONV, SC_WIDTH),
              meta_tokens=(32, N_META, D_MODEL))
PACK_ROWS = 48
SHARDED_SMALL = ("conv_qkv", "conv_sc", "meta_tokens")


def _pack_small(values):
    names = list(PACKED)

    def body(*refs):
        out_ref = refs[-1]
        out_ref[...] = jnp.zeros_like(out_ref)
        for name, ref in zip(names, refs):
            row, rows, lanes = PACKED[name]
            out_ref[row:row + rows, :lanes] = ref[...]

    return pl.pallas_call(body, name="pack_small", out_shape=jax.ShapeDtypeStruct((PACK_ROWS, PACK_LANES), F32))(
        *[values[name] for name in names])


def _sum_devices(packed_all, chip):
    names = list(PACKED)

    def body(chip_ref, all_ref, *rest):
        shard_refs, out_refs = rest[:len(SHARDED_SMALL)], rest[len(SHARDED_SMALL):]

        def total(ref, rows, lanes):
            acc = ref[0, rows, lanes]
            for k in range(1, 8):
                acc = acc + ref[k, rows, lanes]
            return acc

        for name, out in zip(names, out_refs):
            row, rows, lanes = PACKED[name]
            if name in SHARDED_SMALL:
                out[...] = total(shard_refs[SHARDED_SMALL.index(name)], slice(0, rows), slice(None))
            else:
                out[...] = total(all_ref, slice(row, row + rows), slice(0, lanes))

    def shard_spec(name):
        row, rows, lanes = PACKED[name]
        height = max(rows, 8)
        assert row % height == 0
        return pl.BlockSpec((8, height, lanes // N_CHIPS), lambda i, chip_ref: (0, row // height, chip_ref[0]))

    def out_shape(name):
        _, rows, lanes = PACKED[name]
        return jax.ShapeDtypeStruct((rows, lanes // N_CHIPS if name in SHARDED_SMALL else lanes), F32)

    whole = lambda shape: pl.BlockSpec(shape, lambda i, chip_ref: (0,) * len(shape))
    outs = pl.pallas_call(
        body, name="sum_devices", out_shape=tuple(out_shape(n) for n in names),
        grid_spec=pltpu.PrefetchScalarGridSpec(
            num_scalar_prefetch=1, grid=(1,),
            in_specs=[whole(packed_all.shape)] + [shard_spec(n) for n in SHARDED_SMALL],
            out_specs=tuple(whole(out_shape(n).shape) for n in names)),
    )(chip, packed_all, *[packed_all] * len(SHARDED_SMALL))
    return dict(zip(names, outs))


def _hbm():
    return pl.BlockSpec(memory_space=pl.ANY)


def _place():
    x, y, c = lax.axis_index("x"), lax.axis_index("y"), lax.axis_index("c")
    chips = ((1 - x, y), (x, 1 - y), (1 - x, 1 - y))
    return x, y, c, chips


def _remote(src, dst, send_sems, recv_sems, k, to):
    return pltpu.make_async_remote_copy(src_ref=src, dst_ref=dst, send_sem=send_sems.at[k], recv_sem=recv_sems.at[k],
                                        device_id=to, device_id_type=MESH)


GATHER_SEMS = 7


def _gather_copies(w_refs, out_refs, send_sems, recv_sems):
    x, y, c, chips = _place()
    mine = 2 * x + y
    sibling = (x, y, 1 - c)
    copy = functools.partial(_remote, send_sems=send_sems, recv_sems=recv_sems)
    direct, landed, passing, from_sibling = [], [], [], []
    for i, (w, o) in enumerate(zip(w_refs, out_refs)):
        k = GATHER_SEMS * i
        direct.append(copy(w, o.at[mine], k=k, to=sibling))
        from_sibling.append(copy(w, o.at[mine], k=k, to=sibling))
        for j, (cx, cy) in enumerate(chips):
            theirs = 2 * cx + cy
            direct.append(copy(w.at[c], o.at[mine, c], k=k + 1 + j, to=(cx, cy, c)))
            landed.append(copy(w.at[c], o.at[theirs, c], k=k + 1 + j, to=sibling))
            passing.append(copy(o.at[theirs, c], o.at[theirs, c], k=k + 4 + j, to=sibling))
            from_sibling.append(copy(w.at[c], o.at[theirs, 1 - c], k=k + 4 + j, to=sibling))
    return direct, landed, passing, from_sibling


def _gather_finish(copies):
    direct, landed, passing, from_sibling = copies
    for arrival, forward in zip(landed, passing):
        arrival.wait_recv()
        forward.start()
    for arrival in from_sibling:
        arrival.wait_recv()
    for cp in direct + passing:
        cp.wait_send()


def _gather_weights(pieces, smalls):
    count, extra = len(pieces), len(smalls)
    total = count + extra

    def body(*refs):
        w_refs, s_refs = refs[:count], refs[count:total]
        out_refs, sall_refs = refs[total:total + count], refs[total + count:2 * total]
        send_sems, recv_sems, local_sems = refs[2 * total:]
        x, y, c, chips = _place()
        mine = 2 * x + y
        own = [pltpu.make_async_copy(s, sall.at[mine], local_sems.at[i]) for i, (s, sall) in enumerate(zip(s_refs, sall_refs))]
        small = [_remote(s, sall.at[mine], send_sems, recv_sems, GATHER_SEMS * count + 3 * i + j, (cx, cy, c))
                 for i, (s, sall) in enumerate(zip(s_refs, sall_refs)) for j, (cx, cy) in enumerate(chips)]
        copies = _gather_copies(w_refs, out_refs, send_sems, recv_sems)
        for cp in own + small + copies[0]:
            cp.start()
        _gather_finish(copies)
        for cp in small:
            cp.wait_recv()
        for cp in small:
            cp.wait_send()
        for cp in own:
            cp.wait()

    sems = GATHER_SEMS * count + 3 * extra
    return pl.pallas_call(
        body, name="gather_weights",
        out_shape=tuple(jax.ShapeDtypeStruct((N_CHIPS,) + p.shape, p.dtype) for p in list(pieces) + list(smalls)),
        in_specs=[_hbm()] * total, out_specs=(_hbm(),) * total,
        scratch_shapes=[pltpu.SemaphoreType.DMA((sems,)), pltpu.SemaphoreType.DMA((sems,)), pltpu.SemaphoreType.DMA((extra,))],
    )(*pieces, *smalls)


def _exchange_siblings(grads, small=None):
    count = len(grads)
    extra = 0 if small is None else 1

    def body(*refs):
        g_refs = refs[:count]
        got_refs = refs[count + extra:2 * count + extra]
        send_sems, recv_sems = refs[2 * (count + extra):2 * (count + extra) + 2]
        x, y, c, _ = _place()
        copies = [_remote(g.at[:, 1 - c], got, send_sems, recv_sems, i, (x, y, 1 - c))
                  for i, (g, got) in enumerate(zip(g_refs, got_refs))]
        if small is not None:
            s_ref, sall_ref, local_sem = refs[count], refs[2 * count + 1], refs[-1]
            me = 4 * x + 2 * y + c
            own = pltpu.make_async_copy(s_ref, sall_ref.at[me], local_sem)
            own.start()
            for k in range(7):
                dx, dy, dc = ((k + 1) >> 2) & 1, ((k + 1) >> 1) & 1, (k + 1) & 1
                peer = (1 - x if dx else x, 1 - y if dy else y, 1 - c if dc else c)
                copies.append(_remote(s_ref, sall_ref.at[me], send_sems, recv_sems, count + k, peer))
        for cp in copies:
            cp.start()
        for cp in copies:
            cp.wait_recv()
        for cp in copies:
            cp.wait_send()
        if small is not None:
            own.wait()

    sems = count + 7 * extra
    return pl.pallas_call(
        body, name="exchange_siblings" + ("" if small is None else "_small"),
        out_shape=tuple(jax.ShapeDtypeStruct((g.shape[0],) + g.shape[2:], F32) for g in grads)
        + (() if small is None else (jax.ShapeDtypeStruct((8,) + small.shape, F32),)),
        in_specs=[_hbm()] * (count + extra), out_specs=(_hbm(),) * (count + extra),
        scratch_shapes=[pltpu.SemaphoreType.DMA((sems,)), pltpu.SemaphoreType.DMA((sems,))]
        + ([] if small is None else [pltpu.SemaphoreType.DMA]),
    )(*grads, *(() if small is None else (small,)))


def _chip_copies(p_refs, got_refs, send_sems, recv_sems):
    x, y, c, chips = _place()
    return [_remote(p.at[2 * cx + cy], got.at[j], send_sems, recv_sems, 3 * i + j, (cx, cy, c))
            for i, (p, got) in enumerate(zip(p_refs, got_refs)) for j, (cx, cy) in enumerate(chips)]


def _exchange_chips(parts):
    count = len(parts)

    def body(*refs):
        copies = _chip_copies(refs[:count], refs[count:2 * count], *refs[2 * count:])
        for cp in copies:
            cp.start()
        for cp in copies:
            cp.wait_recv()
        for cp in copies:
            cp.wait_send()

    return pl.pallas_call(
        body, name="exchange_chips", out_shape=tuple(jax.ShapeDtypeStruct((3,) + p.shape[1:], p.dtype) for p in parts),
        in_specs=[_hbm()] * count, out_specs=(_hbm(),) * count,
        scratch_shapes=[pltpu.SemaphoreType.DMA((3 * count,)), pltpu.SemaphoreType.DMA((3 * count,))],
    )(*parts)


def _share_halves(halves):
    count = len(halves)

    def body(*refs):
        h_refs, full_refs = refs[:count], refs[count:2 * count]
        send_sems, recv_sems = refs[2 * count:]
        x, y, c, _ = _place()
        copies = [pltpu.make_async_remote_copy(src_ref=h.at[c], dst_ref=full.at[c], send_sem=send_sems.at[i],
                                               recv_sem=recv_sems.at[i], device_id=(x, y, 1 - c), device_id_type=MESH)
                  for i, (h, full) in enumerate(zip(h_refs, full_refs))]
        for cp in copies:
            cp.start()
        for cp in copies:
            cp.wait_recv()
        for cp in copies:
            cp.wait_send()

    return pl.pallas_call(
        body, name="share_halves", out_shape=tuple(jax.ShapeDtypeStruct(h.shape, h.dtype) for h in halves),
        in_specs=[_hbm()] * count, out_specs=(_hbm(),) * count, input_output_aliases={i: i for i in range(count)},
        scratch_shapes=[pltpu.SemaphoreType.DMA((count,)), pltpu.SemaphoreType.DMA((count,))],
    )(*halves)


def _add_sibling(grad, got, core, name):
    chips, _, rows, cols = grad.shape

    def body(core_ref, g_ref, r_ref, sum_ref, send_ref):
        s = g_ref[...] + r_ref[...]
        sum_ref[...] = s
        send_ref[...] = s.astype(send_ref.dtype)

    block = pl.BlockSpec((None, rows, cols), lambda p, core_ref: (p, 0, 0))
    return pl.pallas_call(
        body, name="add_sibling_" + name,
        out_shape=(jax.ShapeDtypeStruct((chips, rows, cols), F32), jax.ShapeDtypeStruct((chips, rows, cols), BF16)),
        grid_spec=pltpu.PrefetchScalarGridSpec(
            num_scalar_prefetch=1, grid=(chips,),
            in_specs=[pl.BlockSpec((None, None, rows, cols), lambda p, core_ref: (p, core_ref[0], 0, 0)), block],
            out_specs=(block, block)),
        compiler_params=_params("parallel"),
    )(core, grad, got)


def _add_chips(part, got, chip_core, name):
    _, rows, cols = part.shape
    tr = rows // 2 if rows % 32 == 0 else rows

    def body(place_ref, p_ref, r_ref, o_ref):
        o_ref[...] = ((p_ref[...] + r_ref[0].astype(F32)) + r_ref[1].astype(F32)) + r_ref[2].astype(F32)

    return pl.pallas_call(
        body, name="add_chips_" + name, out_shape=jax.ShapeDtypeStruct((2, rows, cols), F32),
        grid_spec=pltpu.PrefetchScalarGridSpec(
            num_scalar_prefetch=1, grid=(rows // tr,),
            in_specs=[pl.BlockSpec((None, tr, cols), lambda i, place_ref: (place_ref[0], i, 0)),
                      pl.BlockSpec((3, tr, cols), lambda i, place_ref: (0, i, 0))],
            out_specs=pl.BlockSpec((None, tr, cols), lambda i, place_ref: (place_ref[1], i, 0))),
        compiler_params=_params("parallel"),
    )(chip_core, part, got)


def _adamw(w, g, m, v, name):
    rows, cols = w.shape
    tr = _pick(rows, (3592, 256, 352, 176, 128, 64, 32, 16, 8))

    def body(w_ref, g_ref, m_ref, v_ref, d_ref, nm_ref, nv_ref):
        g = g_ref[...]
        m = ADAM_B1 * m_ref[...] + (1.0 - ADAM_B1) * g
        v = ADAM_B2 * v_ref[...] + (1.0 - ADAM_B2) * (g * g)
        m_hat = m / (1.0 - ADAM_B1 ** ADAM_STEP)
        v_hat = v / (1.0 - ADAM_B2 ** ADAM_STEP)
        d_ref[...] = -ADAM_LR * (m_hat / (jnp.sqrt(v_hat) + ADAM_EPS) + ADAM_WD * w_ref[...])
        nm_ref[...] = m
        nv_ref[...] = v

    block = pl.BlockSpec((tr, cols), lambda i: (i, 0))
    shape = jax.ShapeDtypeStruct((rows, cols), F32)
    return pl.pallas_call(
        body, name="adamw_" + name, out_shape=(shape, shape, shape), grid=(rows // tr,),
        in_specs=[block] * 4, out_specs=(block,) * 3, compiler_params=_params("parallel"),
    )(w, g, m, v)


WEIGHTS = ("meta_tokens", "mix_pre_norm", "mix_post_norm", "ffn_pre_norm", "ffn_post_norm", "w_in", "conv_qkv", "a_log",
           "dt_bias", "gdn_norm", "conv_sc", "w_out", "w_gate", "w_up", "w_down")


def kernel(x, meta_tokens, mix_pre_norm, mix_post_norm, ffn_pre_norm, ffn_post_norm, w_in, conv_qkv, a_log, dt_bias, gdn_norm, conv_sc, w_out, w_gate, w_up, w_down, loss_target, m_meta_tokens, m_mix_pre_norm, m_mix_post_norm, m_ffn_pre_norm, m_ffn_post_norm, m_w_in, m_conv_qkv, m_a_log, m_dt_bias, m_gdn_norm, m_conv_sc, m_w_out, m_w_gate, m_w_up, m_w_down, v_meta_tokens, v_mix_pre_norm, v_mix_post_norm, v_ffn_pre_norm, v_ffn_post_norm, v_w_in, v_conv_qkv, v_a_log, v_dt_bias, v_gdn_norm, v_conv_sc, v_w_out, v_w_gate, v_w_up, v_w_down):
    d = x.shape[-1]
    two_d = lambda a: a.reshape(a.shape[-2:])
    weights = dict(zip(WEIGHTS, (meta_tokens, mix_pre_norm, mix_post_norm, ffn_pre_norm, ffn_post_norm, w_in, conv_qkv, a_log,
                                 dt_bias, gdn_norm, conv_sc, w_out, w_gate, w_up, w_down)))
    m_in = dict(zip(WEIGHTS, (m_meta_tokens, m_mix_pre_norm, m_mix_post_norm, m_ffn_pre_norm, m_ffn_post_norm, m_w_in, m_conv_qkv,
                              m_a_log, m_dt_bias, m_gdn_norm, m_conv_sc, m_w_out, m_w_gate, m_w_up, m_w_down)))
    v_in = dict(zip(WEIGHTS, (v_meta_tokens, v_mix_pre_norm, v_mix_post_norm, v_ffn_pre_norm, v_ffn_post_norm, v_w_in, v_conv_qkv,
                              v_a_log, v_dt_bias, v_gdn_norm, v_conv_sc, v_w_out, v_w_gate, v_w_up, v_w_down)))
    core = lax.axis_index("c")
    chip = 2 * lax.axis_index("x") + lax.axis_index("y")
    core_arg = core.reshape(1).astype(jnp.int32)
    chip_core = jnp.stack([chip, core]).astype(jnp.int32)
    whole = lambda a: a.reshape(a.shape[:-3] + (2 * a.shape[-2], d))
    by_rows = lambda n, a: two_d(a).T if n in ("w_in", "w_gate", "w_up") else two_d(a)

    shard = {n: by_rows(n, weights[n]).astype(MXU_DTYPE) for n in MATRICES}
    shard["w_in"] = jnp.pad(shard["w_in"], ((0, IN_SHARD_PAD - IN_SHARD), (0, 0)))
    w_in_all, *small_all = _gather_weights([_halves(shard["w_in"])], [two_d(weights[n]) for n in SHARDED_SMALL])
    w_in_t = _in_to_kernel_order(whole(w_in_all))
    conv_qkv_full, conv_sc_full, meta_full = (jnp.concatenate([a[p] for p in range(N_CHIPS)], axis=1) for a in small_all)

    sq, grad_x, g, later = _local_step(
        x, loss_target, meta_full, (mix_pre_norm, mix_post_norm, ffn_pre_norm, ffn_post_norm), w_in_t, conv_qkv_full, a_log,
        dt_bias, gdn_norm, conv_sc_full, [_halves(shard[n]) for n in LATER], core_arg)

    g_in = _halves(_in_from_kernel_order(g.pop("w_in")))
    got_sibling, packed_all = _exchange_siblings([g_in], _pack_small(dict(g, loss=sq)))
    part_in, send_in = _add_sibling(g_in, got_sibling, core_arg, "w_in")
    sums = dict(zip(LATER, later), w_in=(part_in, _exchange_chips([send_in])[0]))
    totals = [_add_chips(*sums[n], chip_core, n) for n in MATRICES]
    grads = {n: whole(a) for n, a in zip(MATRICES, _share_halves(totals))}
    grads["w_in"] = grads["w_in"][:IN_SHARD]
    grads.update(_sum_devices(packed_all, chip.reshape(1).astype(jnp.int32)))
    loss = (0.5 / d) * grads.pop("loss")[0, 0]

    outs = [[], [], [], []]
    for n in WEIGHTS:
        shape = weights[n].shape
        delta, new_m, new_v = _adamw(by_rows(n, weights[n]), grads[n], by_rows(n, m_in[n]), by_rows(n, v_in[n]), n)
        for out, a in zip(outs, (grads[n], delta, new_m, new_v)):
            out.append((a.T if n in ("w_in", "w_gate", "w_up") else a).reshape(shape))
    return (loss, grad_x, *outs[0], *outs[1], *outs[2], *outs[3])
```

```python
import functools

import jax
import jax.numpy as jnp
from jax import lax
from jax.experimental import pallas as pl
from jax.experimental.pallas import tpu as pltpu

F32 = jnp.float32
BF16 = jnp.bfloat16
MXU_DTYPE = jnp.bfloat16
MESH = pl.DeviceIdType.MESH

D_MODEL = 1024
N_META = 16
HEADS = 4
HEAD_DIM = 128
GDN_WIDTH = HEADS * HEAD_DIM
GDN_CONV = 4
CHUNK = 64
SC_WIDTH = D_MODEL - GDN_WIDTH
SC_CONV = 3
D_FF = 2816
IN_WIDTH = 4 * GDN_WIDTH + 2 * HEADS + 3 * SC_WIDTH
IN_PAD = 3840
BA_COL = (4 * GDN_WIDTH + 3 * SC_WIDTH) // 128
EPS = 1e-6
LANES = 128
N_CHIPS = 4
VMEM_LIMIT = 48 * 2 ** 20

ADAM_LR = 0.001
ADAM_B1 = 0.9
ADAM_B2 = 0.999
ADAM_EPS = 1e-08
ADAM_WD = 0.01
ADAM_STEP = 10


def _pick(n, candidates):
    for c in candidates:
        if n % c == 0:
            return c
    return n


def _row_tile(n):
    return _pick(n, (352, 256, 176, 128, 64, 32, 16, 8))


def _params(*sem):
    return pltpu.CompilerParams(dimension_semantics=sem, vmem_limit_bytes=VMEM_LIMIT)


def _sigmoid(x):
    return 1.0 / (1.0 + jnp.exp(-x))


def _softplus(x):
    return jnp.maximum(x, 0.0) + jnp.log(1.0 + jnp.exp(-jnp.abs(x)))


def _dsilu(x, s):
    return s * (1.0 + x * (1.0 - s))


def _mm(a, b, mode, out_dtype, name, init=None, exchange=None):
    if mode == "tn":
        k_dim, m_dim = a.shape
    else:
        m_dim, k_dim = a.shape
    n_dim = b.shape[0] if mode == "nt" else b.shape[1]
    tm = _pick(m_dim, (1408, 1280, 1024, 512, 256, 128) if mode == "tn" else (1056, 1024, 704, 512, 256, 128))
    tn = _pick(n_dim, (1408, 1280, 1024, 768, 512, 256, 128))
    tk = _pick(k_dim, (1408, 1280, 1056, 1024, 512, 256, 128))
    nk = k_dim // tk
    if mode == "nn":
        a_spec = pl.BlockSpec((tm, tk), lambda i, j, k: (i, k))
        b_spec = pl.BlockSpec((tk, tn), lambda i, j, k: (k, j))
        dims = (((1,), (0,)), ((), ()))
    elif mode == "nt":
        a_spec = pl.BlockSpec((tm, tk), lambda i, j, k: (i, k))
        b_spec = pl.BlockSpec((tn, tk), lambda i, j, k: (j, k))
        dims = (((1,), (1,)), ((), ()))
    else:
        a_spec = pl.BlockSpec((tk, tm), lambda i, j, k: (k, i))
        b_spec = pl.BlockSpec((tk, tn), lambda i, j, k: (k, j))
        dims = (((0,), (0,)), ((), ()))

    out_spec = pl.BlockSpec((tm, tn), lambda i, j, k: (i, j))
    grid = (m_dim // tm, n_dim // tn, nk)
    parts = () if exchange is None else tuple(exchange)
    count = len(parts)
    first_in = 2 if init is None else 3

    def body(a_ref, b_ref, *rest):
        o_ref = rest[first_in - 2 + count]
        acc_ref = rest[first_in - 1 + 2 * count]
        k = pl.program_id(2)
        step = (pl.program_id(0) * grid[1] + pl.program_id(1)) * nk + k
        if count:
            copies = _chip_copies(rest[first_in - 2:first_in - 2 + count], rest[first_in - 1 + count:first_in - 1 + 2 * count],
                                  *rest[first_in + 2 * count:])

            @pl.when(step == 0)
            def _():
                for cp in copies:
                    cp.start()

        p = lax.dot_general(a_ref[...], b_ref[...], dims, preferred_element_type=F32)

        @pl.when(k == 0)
        def _():
            acc_ref[...] = p if init is None else rest[0][...] + p

        @pl.when(k > 0)
        def _():
            acc_ref[...] += p

        @pl.when(k == nk - 1)
        def _():
            o_ref[...] = acc_ref[...].astype(out_dtype)

        if count:
            @pl.when(step == grid[0] * grid[1] * nk - 1)
            def _():
                for cp in copies:
                    cp.wait_recv()
                for cp in copies:
                    cp.wait_send()

    out = pl.pallas_call(
        body, name=name,
        out_shape=(jax.ShapeDtypeStruct((m_dim, n_dim), out_dtype),)
        + tuple(jax.ShapeDtypeStruct((3,) + p.shape[1:], p.dtype) for p in parts),
        grid=grid,
        in_specs=[a_spec, b_spec] + ([] if init is None else [out_spec]) + [_hbm()] * count,
        out_specs=(out_spec,) + (_hbm(),) * count,
        scratch_shapes=[pltpu.VMEM((tm, tn), F32)]
        + ([pltpu.SemaphoreType.DMA((3 * count,)), pltpu.SemaphoreType.DMA((3 * count,))] if count else []),
        compiler_params=_params(*(("arbitrary",) * 3 if count else ("parallel", "parallel", "arbitrary"))),
    )(a, b, *(() if init is None else (init,)), *parts)
    return out[0] if not count else out


def _rms_apply(x, w):
    r = lax.rsqrt(jnp.mean(x * x, axis=-1, keepdims=True) + EPS)
    return x * r * w


def _rms_bwd(x, w, dy):
    r = lax.rsqrt(jnp.mean(x * x, axis=-1, keepdims=True) + EPS)
    xh = x * r
    dyw = dy * w
    dx = r * (dyw - xh * jnp.mean(dyw * xh, axis=-1, keepdims=True))
    return dx, jnp.sum(dy * xh, axis=0, keepdims=True)


def _accumulate(ref, first, value):
    @pl.when(first)
    def _():
        ref[...] = value

    @pl.when(jnp.logical_not(first))
    def _():
        ref[...] += value


def _rows(tr, width):
    return pl.BlockSpec((tr, width), lambda i: (i, 0))


def _vec(width):
    return pl.BlockSpec((1, width), lambda i: (0, 0))


def _rms_fwd(h, w, name):
    n, d = h.shape
    tr = _row_tile(n)

    def body(h_ref, w_ref, u_ref):
        u_ref[...] = _rms_apply(h_ref[...], w_ref[...]).astype(u_ref.dtype)

    return pl.pallas_call(
        body, name=name, out_shape=jax.ShapeDtypeStruct((n, d), MXU_DTYPE), grid=(n // tr,),
        in_specs=[_rows(tr, d), _vec(d)], out_specs=_rows(tr, d), compiler_params=_params("parallel"),
    )(h, w)


def _mix_residual(h0, mix, w_post, w_pre):
    n, d = h0.shape
    tr = _row_tile(n)

    def body(h0_ref, mix_ref, wpost_ref, wpre_ref, h1_ref, u2_ref):
        h1 = h0_ref[...] + _rms_apply(mix_ref[...], wpost_ref[...])
        h1_ref[...] = h1
        u2_ref[...] = _rms_apply(h1, wpre_ref[...]).astype(u2_ref.dtype)

    return pl.pallas_call(
        body, name="mix_residual",
        out_shape=(jax.ShapeDtypeStruct((n, d), F32), jax.ShapeDtypeStruct((n, d), MXU_DTYPE)), grid=(n // tr,),
        in_specs=[_rows(tr, d), _rows(tr, d), _vec(d), _vec(d)], out_specs=(_rows(tr, d), _rows(tr, d)),
        compiler_params=_params("parallel"),
    )(h0, mix, w_post, w_pre)


NT_DIMS = (((1,), (1,)), ((), ()))


def _ffn_tiles(n):
    return _pick(n, (704, 512, 256, 128)), _pick(D_FF, (1408, 256, 128))


def _swiglu_fwd(u, w_gate_t, w_up_t):
    n, d = u.shape
    tm, tn = _ffn_tiles(n)

    def body(u_ref, wg_ref, wu_ref, g_ref, up_ref, act_ref):
        a = u_ref[...]
        g = lax.dot_general(a, wg_ref[...], NT_DIMS, preferred_element_type=F32)
        up = lax.dot_general(a, wu_ref[...], NT_DIMS, preferred_element_type=F32)
        g_ref[...] = g
        up_ref[...] = up
        act_ref[...] = (g * _sigmoid(g) * up).astype(act_ref.dtype)

    tile = pl.BlockSpec((tm, tn), lambda j, i: (i, j))
    weight = pl.BlockSpec((tn, d), lambda j, i: (j, 0))
    wide = jax.ShapeDtypeStruct((n, D_FF), F32)
    return pl.pallas_call(
        body, name="swiglu_fwd", out_shape=(wide, wide, jax.ShapeDtypeStruct((n, D_FF), MXU_DTYPE)),
        grid=(D_FF // tn, n // tm),
        in_specs=[pl.BlockSpec((tm, d), lambda j, i: (i, 0)), weight, weight], out_specs=(tile, tile, tile),
        compiler_params=_params("parallel", "parallel"),
    )(u, w_gate_t, w_up_t)


def _swiglu_bwd(dffn, w_down, gate, up):
    n, d = dffn.shape
    tm, tn = _ffn_tiles(n)

    def body(dy_ref, w_ref, g_ref, u_ref, dg_ref, du_ref):
        da = lax.dot_general(dy_ref[...], w_ref[...], NT_DIMS, preferred_element_type=F32)
        g = g_ref[...]
        s = _sigmoid(g)
        dg_ref[...] = (da * u_ref[...] * _dsilu(g, s)).astype(dg_ref.dtype)
        du_ref[...] = (da * g * s).astype(du_ref.dtype)

    tile = pl.BlockSpec((tm, tn), lambda j, i: (i, j))
    shape = jax.ShapeDtypeStruct((n, D_FF), MXU_DTYPE)
    return pl.pallas_call(
        body, name="swiglu_bwd", out_shape=(shape, shape), grid=(D_FF // tn, n // tm),
        in_specs=[pl.BlockSpec((tm, d), lambda j, i: (i, 0)), pl.BlockSpec((tn, d), lambda j, i: (j, 0)), tile, tile],
        out_specs=(tile, tile), compiler_params=_params("parallel", "parallel"),
    )(dffn, w_down, gate, up)


def _loss_head(h1, ffn, w_post, target, rows_per_seq, x_offset):
    n, d = h1.shape
    tr = _row_tile(rows_per_seq)
    tiles_per_seq = rows_per_seq // tr

    def body(h1_ref, ffn_ref, w_ref, t_ref, dh2_ref, dffn_ref, dw_ref, sq_ref):
        i = pl.program_id(0)
        w = w_ref[...]
        f = ffn_ref[...]
        r = lax.rsqrt(jnp.mean(f * f, axis=-1, keepdims=True) + EPS)
        fh = f * r
        row = lax.rem(i, tiles_per_seq) * tr + lax.broadcasted_iota(jnp.int32, (tr, 1), 0)
        err = jnp.where(row >= x_offset, h1_ref[...] + fh * w - t_ref[...], 0.0)
        dh2 = err * (1.0 / d)
        dh2_ref[...] = dh2
        dyw = dh2 * w
        dffn_ref[...] = (r * (dyw - fh * jnp.mean(dyw * fh, axis=-1, keepdims=True))).astype(dffn_ref.dtype)
        _accumulate(dw_ref, i == 0, jnp.sum(dh2 * fh, axis=0, keepdims=True))
        _accumulate(sq_ref, i == 0, jnp.sum(jnp.sum(err * err, axis=1, keepdims=True), axis=0, keepdims=True))

    return pl.pallas_call(
        body, name="loss_head",
        out_shape=(jax.ShapeDtypeStruct((n, d), F32), jax.ShapeDtypeStruct((n, d), MXU_DTYPE),
                   jax.ShapeDtypeStruct((1, d), F32), jax.ShapeDtypeStruct((1, 1), F32)),
        grid=(n // tr,),
        in_specs=[_rows(tr, d), _rows(tr, d), _vec(d), _rows(tr, d)],
        out_specs=(_rows(tr, d), _rows(tr, d), _vec(d), _vec(1)),
        compiler_params=_params("arbitrary"),
    )(h1, ffn, w_post, target)


def _mid_bwd(h1, mix, w_mix_post, w_ffn_pre, dh2, du2, grads):
    n, d = h1.shape
    tr = _row_tile(n)
    count = len(grads)

    def body(h1_ref, mix_ref, wpost_ref, wpre_ref, dh2_ref, du2_ref, *rest):
        g_refs, (dh1_ref, dmix_ref, dwpre_ref, dwpost_ref), got_refs = rest[:count], rest[count:count + 4], rest[count + 4:2 * count + 4]
        exchange = _sibling_copies(g_refs, got_refs, *rest[2 * count + 4:])
        i = pl.program_id(0)

        @pl.when(i == 0)
        def _():
            for cp in exchange:
                cp.start()

        dx, dwpre = _rms_bwd(h1_ref[...], wpre_ref[...], du2_ref[...])
        dh1 = dh2_ref[...] + dx
        dh1_ref[...] = dh1
        dmix, dwpost = _rms_bwd(mix_ref[...], wpost_ref[...], dh1)
        dmix_ref[...] = dmix.astype(dmix_ref.dtype)
        _accumulate(dwpre_ref, i == 0, dwpre)
        _accumulate(dwpost_ref, i == 0, dwpost)

        @pl.when(i == n // tr - 1)
        def _():
            for cp in exchange:
                cp.wait_recv()
            for cp in exchange:
                cp.wait_send()

    dh1, dmix, dwpre, dwpost, *got = pl.pallas_call(
        body, name="mid_bwd",
        out_shape=(jax.ShapeDtypeStruct((n, d), F32), jax.ShapeDtypeStruct((n, d), MXU_DTYPE),
                   jax.ShapeDtypeStruct((1, d), F32), jax.ShapeDtypeStruct((1, d), F32))
        + tuple(jax.ShapeDtypeStruct((g.shape[0],) + g.shape[2:], F32) for g in grads),
        grid=(n // tr,),
        in_specs=[_rows(tr, d), _rows(tr, d), _vec(d), _vec(d), _rows(tr, d), _rows(tr, d)] + [_hbm()] * count,
        out_specs=(_rows(tr, d), _rows(tr, d), _vec(d), _vec(d)) + (_hbm(),) * count,
        scratch_shapes=[pltpu.SemaphoreType.DMA((count,)), pltpu.SemaphoreType.DMA((count,))],
        compiler_params=_params("arbitrary"),
    )(h1, mix, w_mix_post, w_ffn_pre, dh2, du2, *grads)
    return dh1, dmix, dwpre, dwpost, got


def _in_bwd(h0, w_pre, dh1, du1):
    n, d = h0.shape
    tr = _row_tile(n)

    def body(h0_ref, w_ref, dh1_ref, du1_ref, dh0_ref, dw_ref):
        dx, dw = _rms_bwd(h0_ref[...], w_ref[...], du1_ref[...])
        dh0_ref[...] = dh1_ref[...] + dx
        _accumulate(dw_ref, pl.program_id(0) == 0, dw)

    return pl.pallas_call(
        body, name="in_bwd",
        out_shape=(jax.ShapeDtypeStruct((n, d), F32), jax.ShapeDtypeStruct((1, d), F32)), grid=(n // tr,),
        in_specs=[_rows(tr, d), _vec(d), _rows(tr, d), _rows(tr, d)], out_specs=(_rows(tr, d), _vec(d)),
        compiler_params=_params("arbitrary"),
    )(h0, w_pre, dh1, du1)


def _lane_is(lo, hi):
    lane = lax.broadcasted_iota(jnp.int32, (1, LANES), 1)
    return jnp.logical_and(lane >= lo, lane < hi)


def _gates_fwd(proj, a_log_l, dt_bias_l, rows_per_seq, pad_rows):
    n = proj.shape[0]
    tr = _row_tile(rows_per_seq)
    tiles_per_seq = rows_per_seq // tr

    def body(p_ref, a_ref, dt_ref, o_ref):
        x = p_ref[...]
        row = lax.rem(pl.program_id(0), tiles_per_seq) * tr + lax.broadcasted_iota(jnp.int32, (tr, 1), 0)
        g = -jnp.exp(a_ref[...]) * _softplus(x + dt_ref[...])
        val = jnp.where(_lane_is(0, HEADS), _sigmoid(x), jnp.where(_lane_is(HEADS, 2 * HEADS), g, 0.0))
        o_ref[...] = jnp.where(row >= pad_rows, val, 0.0)

    return pl.pallas_call(
        body, name="gates_fwd", out_shape=jax.ShapeDtypeStruct((n, LANES), F32), grid=(n // tr,),
        in_specs=[pl.BlockSpec((tr, LANES), lambda i: (i, BA_COL)), _vec(LANES), _vec(LANES)],
        out_specs=_rows(tr, LANES), compiler_params=_params("parallel"),
    )(proj, a_log_l, dt_bias_l)


def _gates_bwd(proj, dbg, a_log_l, dt_bias_l, rows_per_seq, pad_rows):
    n = proj.shape[0]
    tr = _row_tile(rows_per_seq)
    tiles_per_seq = rows_per_seq // tr

    def body(p_ref, d_ref, a_ref, dt_ref, dx_ref, da_ref, ddt_ref):
        i = pl.program_id(0)
        x = p_ref[...]
        d = d_ref[...]
        row = lax.rem(i, tiles_per_seq) * tr + lax.broadcasted_iota(jnp.int32, (tr, 1), 0)
        live = row >= pad_rows
        beta = _sigmoid(x)
        ea = jnp.exp(a_ref[...])
        xa = x + dt_ref[...]
        g = -ea * _softplus(xa)
        is_g = _lane_is(HEADS, 2 * HEADS)
        d_alogit = jnp.where(jnp.logical_and(live, is_g), d * (-ea) * _sigmoid(xa), 0.0)
        d_blogit = jnp.where(jnp.logical_and(live, _lane_is(0, HEADS)), d * beta * (1.0 - beta), 0.0)
        dx_ref[:, :LANES] = (d_alogit + d_blogit).astype(dx_ref.dtype)
        dx_ref[:, LANES:] = jnp.zeros((tr, LANES), dx_ref.dtype)
        _accumulate(da_ref, i == 0, jnp.sum(jnp.where(jnp.logical_and(live, is_g), d * g, 0.0), axis=0, keepdims=True))
        _accumulate(ddt_ref, i == 0, jnp.sum(d_alogit, axis=0, keepdims=True))

    return pl.pallas_call(
        body, name="gates_bwd",
        out_shape=(jax.ShapeDtypeStruct((n, 2 * LANES), MXU_DTYPE), jax.ShapeDtypeStruct((1, LANES), F32),
                   jax.ShapeDtypeStruct((1, LANES), F32)),
        grid=(n // tr,),
        in_specs=[pl.BlockSpec((tr, LANES), lambda i: (i, BA_COL)), _rows(tr, LANES), _vec(LANES), _vec(LANES)],
        out_specs=(_rows(tr, 2 * LANES), _vec(LANES), _vec(LANES)),
        compiler_params=_params("arbitrary"),
    )(proj, dbg, a_log_l, dt_bias_l)


def _shift_down(x, k):
    return x if k == 0 else pltpu.roll(x, k, 0)


def _shift_up(x, k):
    return x if k == 0 else pltpu.roll(x, x.shape[0] - k, 0)


def _causal_conv(x, w, width):
    acc = w[width - 1:width, :] * x
    for i in range(width - 1):
        acc = acc + w[i:i + 1, :] * _shift_down(x, width - 1 - i)
    return acc


def _seq_head(rs, col0):
    return pl.BlockSpec((rs, LANES), lambda j, b: (b, col0 + j))


def _live_rows(rs, pad_rows):
    return lax.broadcasted_iota(jnp.int32, (rs, 1), 0) >= pad_rows


def _qkv_fwd(proj, conv_w, kind, rs, pad_rows):
    n = proj.shape[0]
    col0 = {"q": 0, "k": HEADS, "v": 2 * HEADS}[kind]

    def body(p_ref, w_ref, o_ref):
        c = _causal_conv(p_ref[...], w_ref[...], GDN_CONV)
        s = c * _sigmoid(c)
        if kind != "v":
            s = s * lax.rsqrt(jnp.sum(s * s, axis=-1, keepdims=True) + EPS)
        if kind == "q":
            s = s * (HEAD_DIM ** -0.5)
        o_ref[...] = jnp.where(_live_rows(rs, pad_rows), s, 0.0)

    return pl.pallas_call(
        body, name="qkv_fwd_" + kind, out_shape=jax.ShapeDtypeStruct((n, GDN_WIDTH), F32), grid=(HEADS, n // rs),
        in_specs=[_seq_head(rs, col0), pl.BlockSpec((GDN_CONV, LANES), lambda j, b: (0, col0 + j))],
        out_specs=_seq_head(rs, 0), compiler_params=_params("parallel", "parallel"),
    )(proj, conv_w)


def _qkv_bwd(dy, proj, conv_w, kind, rs, pad_rows):
    n = proj.shape[0]
    col0 = {"q": 0, "k": HEADS, "v": 2 * HEADS}[kind]

    def body(dy_ref, p_ref, w_ref, dp_ref, dw_ref):
        pre = p_ref[...]
        w = w_ref[...]
        c = _causal_conv(pre, w, GDN_CONV)
        sg = _sigmoid(c)
        s = c * sg
        ds = dy_ref[...]
        if kind == "q":
            ds = ds * (HEAD_DIM ** -0.5)
        if kind != "v":
            r = lax.rsqrt(jnp.sum(s * s, axis=-1, keepdims=True) + EPS)
            sh = s * r
            ds = r * (ds - sh * jnp.sum(ds * sh, axis=-1, keepdims=True))
        dc = jnp.where(_live_rows(rs, pad_rows), ds * _dsilu(c, sg), 0.0)
        dpre = w[GDN_CONV - 1:GDN_CONV, :] * dc
        for i in range(GDN_CONV - 1):
            dpre = dpre + w[i:i + 1, :] * _shift_up(dc, GDN_CONV - 1 - i)
        dp_ref[...] = dpre.astype(dp_ref.dtype)
        dw = jnp.concatenate(
            [jnp.sum(dc * _shift_down(pre, GDN_CONV - 1 - i), axis=0, keepdims=True) for i in range(GDN_CONV)], axis=0)
        _accumulate(dw_ref, pl.program_id(1) == 0, dw)

    return pl.pallas_call(
        body, name="qkv_bwd_" + kind,
        out_shape=(jax.ShapeDtypeStruct((n, GDN_WIDTH), MXU_DTYPE), jax.ShapeDtypeStruct((GDN_CONV, GDN_WIDTH), F32)),
        grid=(HEADS, n // rs),
        in_specs=[_seq_head(rs, 0), _seq_head(rs, col0), pl.BlockSpec((GDN_CONV, LANES), lambda j, b: (0, col0 + j))],
        out_specs=(_seq_head(rs, 0), pl.BlockSpec((GDN_CONV, LANES), lambda j, b: (0, j))),
        compiler_params=_params("parallel", "arbitrary"),
    )(dy, proj, conv_w)


SC_COL = 4 * HEADS


def _sc_fwd(proj, conv_w, rs):
    n = proj.shape[0]

    def body(x_ref, b_ref, c_ref, w_ref, y_ref):
        y_ref[...] = (b_ref[...] * _causal_conv(c_ref[...] * x_ref[...], w_ref[...], SC_CONV)).astype(y_ref.dtype)

    return pl.pallas_call(
        body, name="sc_fwd", out_shape=jax.ShapeDtypeStruct((n, SC_WIDTH), MXU_DTYPE), grid=(HEADS, n // rs),
        in_specs=[_seq_head(rs, SC_COL), _seq_head(rs, SC_COL + 4), _seq_head(rs, SC_COL + 8),
                  pl.BlockSpec((SC_CONV, LANES), lambda j, b: (0, j))],
        out_specs=_seq_head(rs, 0), compiler_params=_params("parallel", "parallel"),
    )(proj, proj, proj, conv_w)


def _sc_bwd(dcat, proj, conv_w, rs):
    n = proj.shape[0]

    def body(dy_ref, x_ref, b_ref, c_ref, w_ref, dx_ref, db_ref, dc_ref, dw_ref):
        w = w_ref[...]
        x = x_ref[...]
        cc = c_ref[...]
        u = cc * x
        dy = dy_ref[...]
        db_ref[...] = (dy * _causal_conv(u, w, SC_CONV)).astype(db_ref.dtype)
        dcv = dy * b_ref[...]
        du = w[SC_CONV - 1:SC_CONV, :] * dcv
        for i in range(SC_CONV - 1):
            du = du + w[i:i + 1, :] * _shift_up(dcv, SC_CONV - 1 - i)
        dx_ref[...] = (du * cc).astype(dx_ref.dtype)
        dc_ref[...] = (du * x).astype(dc_ref.dtype)
        dw = jnp.concatenate(
            [jnp.sum(dcv * _shift_down(u, SC_CONV - 1 - i), axis=0, keepdims=True) for i in range(SC_CONV)], axis=0)
        _accumulate(dw_ref, pl.program_id(1) == 0, dw)

    piece = jax.ShapeDtypeStruct((n, SC_WIDTH), MXU_DTYPE)
    return pl.pallas_call(
        body, name="sc_bwd", out_shape=(piece, piece, piece, jax.ShapeDtypeStruct((SC_CONV, SC_WIDTH), F32)),
        grid=(HEADS, n // rs),
        in_specs=[_seq_head(rs, HEADS), _seq_head(rs, SC_COL), _seq_head(rs, SC_COL + 4), _seq_head(rs, SC_COL + 8),
                  pl.BlockSpec((SC_CONV, LANES), lambda j, b: (0, j))],
        out_specs=(_seq_head(rs, 0), _seq_head(rs, 0), _seq_head(rs, 0),
                   pl.BlockSpec((SC_CONV, LANES), lambda j, b: (0, j))),
        compiler_params=_params("parallel", "arbitrary"),
    )(dcat, proj, proj, proj, conv_w)


Z_COL = 3 * HEADS


def _gate_fwd(o, proj, gdn_norm, rs):
    n = proj.shape[0]

    def body(o_ref, z_ref, w_ref, y_ref):
        z = z_ref[...]
        y_ref[...] = (_rms_apply(o_ref[...], w_ref[...]) * z * _sigmoid(z)).astype(y_ref.dtype)

    return pl.pallas_call(
        body, name="gate_fwd", out_shape=jax.ShapeDtypeStruct((n, GDN_WIDTH), MXU_DTYPE), grid=(HEADS, n // rs),
        in_specs=[_seq_head(rs, 0), _seq_head(rs, Z_COL), pl.BlockSpec((1, LANES), lambda j, b: (0, 0))],
        out_specs=_seq_head(rs, 0), compiler_params=_params("parallel", "parallel"),
    )(o, proj, gdn_norm)


def _gate_bwd(dcat, o, proj, gdn_norm, rs):
    n = proj.shape[0]

    def body(dy_ref, o_ref, z_ref, w_ref, do_ref, dz_ref, dw_ref):
        z = z_ref[...]
        w = w_ref[...]
        o = o_ref[...]
        dy = dy_ref[...]
        s = _sigmoid(z)
        dz_ref[...] = (dy * _rms_apply(o, w) * _dsilu(z, s)).astype(dz_ref.dtype)
        do, dw = _rms_bwd(o, w, dy * z * s)
        do_ref[...] = do
        _accumulate(dw_ref, jnp.logical_and(pl.program_id(0) == 0, pl.program_id(1) == 0), dw)

    return pl.pallas_call(
        body, name="gate_bwd",
        out_shape=(jax.ShapeDtypeStruct((n, GDN_WIDTH), F32), jax.ShapeDtypeStruct((n, GDN_WIDTH), MXU_DTYPE),
                   jax.ShapeDtypeStruct((1, LANES), F32)),
        grid=(HEADS, n // rs),
        in_specs=[_seq_head(rs, 0), _seq_head(rs, 0), _seq_head(rs, Z_COL), pl.BlockSpec((1, LANES), lambda j, b: (0, 0))],
        out_specs=(_seq_head(rs, 0), _seq_head(rs, 0), pl.BlockSpec((1, LANES), lambda j, b: (0, 0))),
        compiler_params=_params("arbitrary", "arbitrary"),
    )(dcat, o, proj, gdn_norm)


def _dot(a, b):
    return jnp.dot(a.astype(MXU_DTYPE), b.astype(MXU_DTYPE), preferred_element_type=F32)


def _dot_nt(a, b):
    return lax.dot_general(a.astype(MXU_DTYPE), b.astype(MXU_DTYPE), (((1,), (1,)), ((), ())),
                           preferred_element_type=F32)


def _dot_tn(a, b):
    return lax.dot_general(a.astype(MXU_DTYPE), b.astype(MXU_DTYPE), (((0,), (0,)), ((), ())),
                           preferred_element_type=F32)


def _split(x):
    hi = x.astype(MXU_DTYPE)
    return hi, (x - hi.astype(F32)).astype(MXU_DTYPE)


def _dot_split(a, b):
    mm = functools.partial(jnp.dot, preferred_element_type=F32)
    return mm(a[0], b[0]) + (mm(a[0], b[1]) + mm(a[1], b[0]))


def _unit_lower_inverses(mats, eye):
    inv = [eye - a for a in mats]
    power = [_split(a) for a in mats]
    span = 2
    while span < CHUNK:
        power = [_split(_dot_split(p, p)) for p in power]
        inv = [i + _dot_split(_split(i), p) for i, p in zip(inv, power)]
        span *= 2
    return inv


def _chunk_masks():
    ii = lax.broadcasted_iota(jnp.int32, (CHUNK, CHUNK), 0)
    jj = lax.broadcasted_iota(jnp.int32, (CHUNK, CHUNK), 1)
    return ii, jj


def _chunk_decay(g_col, ii, jj):
    incl = ii >= jj
    g_row = jnp.sum(jnp.where(ii == jj, g_col, 0.0), axis=0, keepdims=True)
    gc_col = jnp.sum(jnp.where(incl, g_row, 0.0), axis=1, keepdims=True)
    gc_row = jnp.sum(jnp.where(ii <= jj, g_col, 0.0), axis=0, keepdims=True)
    g_total = jnp.sum(g_row, axis=1, keepdims=True)
    decay = jnp.where(incl, jnp.exp(jnp.where(incl, gc_col - gc_row, 0.0)), 0.0)
    return gc_col, g_total, decay


def _gdn_segments(rs, candidates):
    chunks = rs // CHUNK
    seg_chunks = _pick(chunks, candidates)
    return chunks, seg_chunks, chunks // seg_chunks


def _head_lanes(h):
    return slice(h * HEAD_DIM, (h + 1) * HEAD_DIM)


def _gdn_fwd(q, k, v, bg, rs, pieces):
    n = q.shape[0]
    batch = n // rs
    chunks, seg_chunks, segs = _gdn_segments(rs, (11, 8, 4, 2))
    seg_rows = seg_chunks * CHUNK
    chains = [(b, h) for b in range(batch) for h in range(HEADS)]
    each = lambda f, *lists: [f(*args) for args in zip(*lists)]
    count = len(pieces)

    def body(q_ref, k_ref, v_ref, bg_ref, *rest):
        w_refs, (o_ref, s_ref, t_ref), out_refs = rest[:count], rest[count:count + 3], rest[count + 3:2 * count + 3]
        state_ref, send_sems, recv_sems = rest[2 * count + 3:]
        gather = _gather_copies(w_refs, out_refs, send_sems, recv_sems)

        @pl.when(pl.program_id(0) == 0)
        def _():
            state_ref[...] = jnp.zeros_like(state_ref)
            for cp in gather[0]:
                cp.start()

        ii, jj = _chunk_masks()
        incl = ii >= jj
        eye = (ii == jj).astype(F32)

        def chunk(c, carry):
            rows = pl.ds(pl.multiple_of(c * CHUNK, CHUNK), CHUNK)
            bgc = [bg_ref[b, rows, :] for b in range(batch)]
            qc = [q_ref[b, rows, _head_lanes(h)] for b, h in chains]
            kc = [k_ref[b, rows, _head_lanes(h)] for b, h in chains]
            vc = [v_ref[b, rows, _head_lanes(h)] for b, h in chains]
            beta = [bgc[b][:, h:h + 1] for b, h in chains]
            state = [state_ref[b, h] for b, h in chains]
            dec = [_chunk_decay(bgc[b][:, HEADS + h:HEADS + h + 1], ii, jj) for b, h in chains]
            gc_col, g_total, decay = ([d[i] for d in dec] for i in range(3))
            kb = each(lambda x, y: x * y, kc, beta)
            a = each(lambda x, y, d: jnp.where(ii > jj, _dot_nt(x, y) * d, 0.0), kb, kc, decay)
            t_inv = _unit_lower_inverses(a, eye)
            eg = [jnp.exp(g) for g in gc_col]
            u = each(lambda t, x, y: _dot(t, x * y), t_inv, vc, beta)
            w = each(lambda t, x, e: _dot(t, x * e), t_inv, kb, eg)
            qk = each(lambda x, y, d: jnp.where(incl, _dot_nt(x, y) * d, 0.0), qc, kc, decay)
            v_new = each(lambda x, y, s: x - _dot(y, s), u, w, state)
            o = each(lambda x, e, s, m, vn: _dot(x * e, s) + _dot(m, vn), qc, eg, state, qk, v_new)
            new_state = each(lambda s, gt, x, g, vn: s * jnp.exp(gt) + _dot_tn(x * jnp.exp(gt - g), vn),
                             state, g_total, kc, gc_col, v_new)
            for i, (b, h) in enumerate(chains):
                s_ref[b, h, c] = state[i]
                t_ref[b, h, c] = t_inv[i]
                o_ref[b, rows, _head_lanes(h)] = o[i]
                state_ref[b, h] = new_state[i]
            return carry

        lax.fori_loop(0, seg_chunks, chunk, 0)

        @pl.when(pl.program_id(0) == segs - 1)
        def _():
            _gather_finish(gather)

    rows_spec = lambda width: pl.BlockSpec((batch, seg_rows, width), lambda s: (0, s, 0))
    per_chunk = lambda r, c: pl.BlockSpec((batch, HEADS, seg_chunks, r, c), lambda s: (0, 0, s, 0, 0))
    as_seqs = lambda a: a.reshape(batch, rs, a.shape[-1])
    sems = GATHER_SEMS * count
    o, states, t_invs, *gathered = pl.pallas_call(
        body, name="gdn_fwd",
        out_shape=(jax.ShapeDtypeStruct((batch, rs, GDN_WIDTH), F32),
                   jax.ShapeDtypeStruct((batch, HEADS, chunks, HEAD_DIM, HEAD_DIM), F32),
                   jax.ShapeDtypeStruct((batch, HEADS, chunks, CHUNK, CHUNK), F32))
        + tuple(jax.ShapeDtypeStruct((N_CHIPS,) + p.shape, p.dtype) for p in pieces),
        grid=(segs,),
        in_specs=[rows_spec(GDN_WIDTH), rows_spec(GDN_WIDTH), rows_spec(GDN_WIDTH), rows_spec(LANES)] + [_hbm()] * count,
        out_specs=(rows_spec(GDN_WIDTH), per_chunk(HEAD_DIM, HEAD_DIM), per_chunk(CHUNK, CHUNK)) + (_hbm(),) * count,
        scratch_shapes=[pltpu.VMEM((batch, HEADS, HEAD_DIM, HEAD_DIM), F32), pltpu.SemaphoreType.DMA((sems,)),
                        pltpu.SemaphoreType.DMA((sems,))],
        compiler_params=_params("arbitrary"),
    )(as_seqs(q), as_seqs(k), as_seqs(v), as_seqs(bg), *pieces)
    return o.reshape(n, GDN_WIDTH), states, t_invs, gathered


def _gdn_bwd(do, q, k, v, bg, states, t_invs, rs, parts):
    n = q.shape[0]
    batch = n // rs
    chunks, seg_chunks, segs = _gdn_segments(rs, (3, 4, 2))
    seg_rows = seg_chunks * CHUNK
    chains = [(b, h) for b in range(batch) for h in range(HEADS)]
    each = lambda f, *lists: [f(*args) for args in zip(*lists)]
    count = len(parts)

    def body(do_ref, q_ref, k_ref, v_ref, bg_ref, s_ref, t_ref, *rest):
        p_refs, (dq_ref, dk_ref, dv_ref, dbg_ref), got_refs = rest[:count], rest[count:count + 4], rest[count + 4:2 * count + 4]
        dstate_ref, send_sems, recv_sems = rest[2 * count + 4:]
        exchange = _chip_copies(p_refs, got_refs, send_sems, recv_sems)

        @pl.when(pl.program_id(0) == 0)
        def _():
            dstate_ref[...] = jnp.zeros_like(dstate_ref)
            for cp in exchange:
                cp.start()

        ii, jj = _chunk_masks()
        incl = ii >= jj
        strict = ii > jj
        lane = lax.broadcasted_iota(jnp.int32, (1, LANES), 1)

        def rowsum(x):
            return jnp.sum(x, axis=1, keepdims=True)

        def total(x):
            return jnp.sum(rowsum(x), axis=0, keepdims=True)

        def chunk(step, carry):
            c = seg_chunks - 1 - step
            rows = pl.ds(pl.multiple_of(c * CHUNK, CHUNK), CHUNK)
            bgc = [bg_ref[b, rows, :] for b in range(batch)]
            qc = [q_ref[b, rows, _head_lanes(h)] for b, h in chains]
            kc = [k_ref[b, rows, _head_lanes(h)] for b, h in chains]
            vc = [v_ref[b, rows, _head_lanes(h)] for b, h in chains]
            doc = [do_ref[b, rows, _head_lanes(h)] for b, h in chains]
            beta = [bgc[b][:, h:h + 1] for b, h in chains]
            state = [s_ref[b, h, c] for b, h in chains]
            t_inv = [t_ref[b, h, c] for b, h in chains]
            d_state = [dstate_ref[b, h] for b, h in chains]
            dec = [_chunk_decay(bgc[b][:, HEADS + h:HEADS + h + 1], ii, jj) for b, h in chains]
            gc_col, g_total, decay = ([d[i] for d in dec] for i in range(3))
            kb = each(lambda x, y: x * y, kc, beta)
            vb = each(lambda x, y: x * y, vc, beta)
            eg = [jnp.exp(g) for g in gc_col]
            kbg = each(lambda x, y: x * y, kb, eg)
            a = each(lambda x, y, d: jnp.where(strict, _dot_nt(x, y) * d, 0.0), kb, kc, decay)
            qk = each(lambda x, y, d: jnp.where(incl, _dot_nt(x, y) * d, 0.0), qc, kc, decay)
            w = each(_dot, t_inv, kbg)
            u = each(_dot, t_inv, vb)
            q_dec = each(lambda x, y: x * y, qc, eg)
            ek = each(lambda gt, g: jnp.exp(gt - g), g_total, gc_col)
            k_dec = each(lambda x, y: x * y, kc, ek)
            g_last = [jnp.exp(gt) for gt in g_total]
            v_new = each(lambda x, y, s: x - _dot(y, s), u, w, state)
            dv_new = each(lambda m, d, x, ds: _dot_tn(m, d) + _dot(x, ds), qk, doc, k_dec, d_state)
            dqk = each(lambda d, vn: jnp.where(incl, _dot_nt(d, vn), 0.0), doc, v_new)
            dq_dec = each(_dot_nt, doc, state)
            dk_dec = each(_dot_nt, v_new, d_state)
            dg_last = each(lambda s, ds: total(s * ds), state, d_state)
            new_d_state = each(lambda x, d, gl, ds, y, dvn: _dot_tn(x, d) + gl * ds - _dot_tn(y, dvn),
                               q_dec, doc, g_last, d_state, w, dv_new)
            dw = each(lambda dvn, s: -_dot_nt(dvn, s), dv_new, state)
            dt = each(lambda dvn, x, y, z: _dot_nt(dvn, x) + _dot_nt(y, z), dv_new, vb, dw, kbg)
            dvb = each(_dot_tn, t_inv, dv_new)
            dkbg = each(_dot_tn, t_inv, dw)
            t_dt = each(_dot_tn, t_inv, dt)
            da = each(lambda x, t: -jnp.where(strict, _dot_nt(x, t), 0.0), t_dt, t_inv)
            dm_a = each(lambda x, y: x * y, da, decay)
            dm_qk = each(lambda x, y: x * y, dqk, decay)
            e = each(lambda x, y, z, t: x * y + z * t, da, a, dqk, qk)
            dkb = each(lambda m, x, y, z: _dot(m, x) + y * z, dm_a, kc, dkbg, eg)
            dk = each(lambda m, x, m2, y, z, t, p, bt: _dot_tn(m, x) + _dot_tn(m2, y) + z * t + p * bt,
                      dm_a, kb, dm_qk, qc, dk_dec, ek, dkb, beta)
            dq = each(lambda m, x, y, z: _dot(m, x) + y * z, dm_qk, kc, dq_dec, eg)
            dbeta = each(lambda x, y, z, t: rowsum(x * y + z * t), dkb, kc, dvb, vc)
            dgc = each(lambda x, p, pd, r, rd, s, sd: rowsum(x) - rowsum(jnp.where(ii == jj, jnp.sum(x, axis=0, keepdims=True), 0.0))
                       + rowsum(p * pd - r * rd + s * sd), e, dq_dec, q_dec, dk_dec, k_dec, dkbg, kbg)
            d_total = each(lambda r, rd, x, gl: total(r * rd) + x * gl, dk_dec, k_dec, dg_last, g_last)
            dg = each(lambda x, t: rowsum(jnp.where(jj >= ii, jnp.sum(jnp.where(ii == jj, x, 0.0), axis=0, keepdims=True), 0.0)) + t,
                      dgc, d_total)
            dbg = [jnp.zeros((CHUNK, LANES), F32) for _ in range(batch)]
            for i, (b, h) in enumerate(chains):
                dstate_ref[b, h] = new_d_state[i]
                dk_ref[b, rows, _head_lanes(h)] = dk[i]
                dq_ref[b, rows, _head_lanes(h)] = dq[i]
                dv_ref[b, rows, _head_lanes(h)] = dvb[i] * beta[i]
                dbg[b] = dbg[b] + jnp.where(lane == h, dbeta[i], 0.0) + jnp.where(lane == HEADS + h, dg[i], 0.0)
            for b in range(batch):
                dbg_ref[b, rows, :] = dbg[b]
            return carry

        lax.fori_loop(0, seg_chunks, chunk, 0)

        @pl.when(pl.program_id(0) == segs - 1)
        def _():
            for cp in exchange:
                cp.wait_recv()
            for cp in exchange:
                cp.wait_send()

    rows_spec = lambda width: pl.BlockSpec((batch, seg_rows, width), lambda s: (0, segs - 1 - s, 0))
    per_chunk = lambda r, c: pl.BlockSpec((batch, HEADS, seg_chunks, r, c), lambda s: (0, 0, segs - 1 - s, 0, 0))
    as_seqs = lambda a: a.reshape(batch, rs, a.shape[-1])
    grad = jax.ShapeDtypeStruct((batch, rs, GDN_WIDTH), F32)
    wide = rows_spec(GDN_WIDTH)
    dq, dk, dv, dbg, *got = pl.pallas_call(
        body, name="gdn_bwd",
        out_shape=(grad, grad, grad, jax.ShapeDtypeStruct((batch, rs, LANES), F32))
        + tuple(jax.ShapeDtypeStruct((3,) + p.shape[1:], p.dtype) for p in parts),
        grid=(segs,),
        in_specs=[wide, wide, wide, wide, rows_spec(LANES), per_chunk(HEAD_DIM, HEAD_DIM), per_chunk(CHUNK, CHUNK)]
        + [_hbm()] * count,
        out_specs=(wide, wide, wide, rows_spec(LANES)) + (_hbm(),) * count,
        scratch_shapes=[pltpu.VMEM((batch, HEADS, HEAD_DIM, HEAD_DIM), F32), pltpu.SemaphoreType.DMA((3 * count,)),
                        pltpu.SemaphoreType.DMA((3 * count,))],
        compiler_params=_params("arbitrary"),
    )(as_seqs(do), as_seqs(q), as_seqs(k), as_seqs(v), as_seqs(bg), states, t_invs, *parts)
    return dq.reshape(n, GDN_WIDTH), dk.reshape(n, GDN_WIDTH), dv.reshape(n, GDN_WIDTH), dbg.reshape(n, LANES), got


def _lane_vec(vals, offset):
    k = vals.shape[1]
    return jnp.pad(vals, ((0, 0), (offset, LANES - offset - k)))


LATER = ("w_out", "w_gate", "w_up", "w_down")


def _halves(a):
    return a.reshape(a.shape[:-2] + (2, a.shape[-2] // 2, a.shape[-1]))


def _local_step(x, target, meta, norms, w_in_t, conv_qkv, a_log, dt_bias, gdn_norm, conv_sc, later_shards, core_arg):
    batch, seq, d = x.shape
    tokens = N_META + seq
    pad_rows = (-tokens) % CHUNK
    rs = tokens + pad_rows
    x_offset = pad_rows + N_META
    n = batch * rs
    w_mix_pre, w_mix_post, w_ffn_pre, w_ffn_post = norms

    head = jnp.concatenate([jnp.zeros((pad_rows, d), F32), meta], axis=0)
    h0 = jnp.concatenate([jnp.broadcast_to(head[None], (batch, x_offset, d)), x], axis=1).reshape(n, d)
    target_p = jnp.pad(target, ((0, 0), (x_offset, 0), (0, 0))).reshape(n, d)
    a_log_l = _lane_vec(a_log, HEADS)
    dt_bias_l = _lane_vec(dt_bias, HEADS)

    u1 = _rms_fwd(h0, w_mix_pre, "rms_mix_pre")
    proj = _mm(u1, w_in_t, "nt", F32, "mm_proj")
    q = _qkv_fwd(proj, conv_qkv, "q", rs, pad_rows)
    k = _qkv_fwd(proj, conv_qkv, "k", rs, pad_rows)
    v = _qkv_fwd(proj, conv_qkv, "v", rs, pad_rows)
    bg = _gates_fwd(proj, a_log_l, dt_bias_l, rs, pad_rows)
    o, states, t_invs, gathered = _gdn_fwd(q, k, v, bg, rs, later_shards)
    w_out, w_gate_t, w_up_t, w_down = (a.reshape(-1, d) for a in gathered)
    o_gated = _gate_fwd(o, proj, gdn_norm, rs)
    y_sc = _sc_fwd(proj, conv_sc, rs)
    cat = jnp.concatenate([o_gated, y_sc], axis=1)
    mix = _mm(cat, w_out, "nn", F32, "mm_mix")
    h1, u2 = _mix_residual(h0, mix, w_mix_post, w_ffn_pre)
    gate, up, act = _swiglu_fwd(u2, w_gate_t, w_up_t)
    ffn = _mm(act, w_down, "nn", F32, "mm_down")

    dh2, dffn, d_ffn_post, sq = _loss_head(h1, ffn, w_ffn_post, target_p, rs, x_offset)
    d_w_down = _mm(act, dffn, "tn", F32, "mm_dw_down")
    dgate, dup = _swiglu_bwd(dffn, w_down, gate, up)
    d_w_gate_t = _mm(dgate, u2, "tn", F32, "mm_dw_gate")
    d_w_up_t = _mm(dup, u2, "tn", F32, "mm_dw_up")
    du2 = _mm(dup, w_up_t, "nn", F32, "mm_du2_up", init=_mm(dgate, w_gate_t, "nn", F32, "mm_du2_gate"))
    by_chip = [_halves(g.reshape(N_CHIPS, -1, d)) for g in (d_w_gate_t, d_w_up_t, d_w_down)]
    dh1, dmix, d_ffn_pre, d_mix_post, got_sibling = _mid_bwd(h1, mix, w_mix_post, w_ffn_pre, dh2, du2, by_chip)
    dcat = _mm(dmix, w_out, "nt", F32, "mm_dcat")
    d_w_out = _halves(_mm(cat, dmix, "tn", F32, "mm_dw_out").reshape(N_CHIPS, -1, d))
    by_chip, got_sibling = [d_w_out] + by_chip, list(_exchange_siblings([d_w_out])) + got_sibling
    sums = [_add_sibling(a, b, core_arg, name) for name, a, b in zip(LATER, by_chip, got_sibling)]
    do, dz, d_gdn_norm = _gate_bwd(dcat, o, proj, gdn_norm, rs)
    dscx, dscb, dscc, d_conv_sc = _sc_bwd(dcat, proj, conv_sc, rs)
    dq, dk, dv, dbg, got_chips = _gdn_bwd(do, q, k, v, bg, states, t_invs, rs, [send for _, send in sums])
    dpq, dwq = _qkv_bwd(dq, proj, conv_qkv, "q", rs, pad_rows)
    dpk, dwk = _qkv_bwd(dk, proj, conv_qkv, "k", rs, pad_rows)
    dpv, dwv = _qkv_bwd(dv, proj, conv_qkv, "v", rs, pad_rows)
    d_conv_qkv = jnp.concatenate([dwq, dwk, dwv], axis=1)
    dba, d_a_log_l, d_dt_bias_l = _gates_bwd(proj, dbg, a_log_l, dt_bias_l, rs, pad_rows)
    dproj = jnp.concatenate([dpq, dpk, dpv, dz, dscx, dscb, dscc, dba], axis=1)
    g_in = _halves(_in_from_kernel_order(_mm(dproj, u1, "tn", F32, "mm_dw_in")))
    sums.insert(0, _add_sibling(g_in, _exchange_siblings([g_in])[0], core_arg, "w_in"))
    du1, got_in = _mm(dproj, w_in_t, "nn", F32, "mm_du1", exchange=[sums[0][1]])
    got_chips.insert(0, got_in)
    dh0, d_mix_pre = _in_bwd(h0, w_mix_pre, dh1, du1)

    dh0 = dh0.reshape(batch, rs, d)
    grads = dict(
        meta_tokens=jnp.sum(dh0[:, pad_rows:x_offset], axis=0),
        mix_pre_norm=d_mix_pre, mix_post_norm=d_mix_post, ffn_pre_norm=d_ffn_pre, ffn_post_norm=d_ffn_post,
        conv_qkv=d_conv_qkv,
        a_log=d_a_log_l[:, HEADS:2 * HEADS], dt_bias=d_dt_bias_l[:, HEADS:2 * HEADS],
        gdn_norm=d_gdn_norm, conv_sc=d_conv_sc,
    )
    return sq, dh0[:, x_offset:], grads, [(part, got) for (part, _), got in zip(sums, got_chips)]


MATRICES = ("w_in", "w_out", "w_gate", "w_up", "w_down")
IN_SHARD = IN_WIDTH // N_CHIPS
IN_SHARD_PAD = 928


def _in_to_kernel_order(by_chip):
    w_t = by_chip[:, :IN_SHARD].reshape(IN_WIDTH, by_chip.shape[-1])
    lo, hi = 4 * GDN_WIDTH, 4 * GDN_WIDTH + 2 * HEADS
    return jnp.concatenate([w_t[:lo], w_t[hi:], w_t[lo:hi], jnp.zeros((IN_PAD - IN_WIDTH, w_t.shape[1]), w_t.dtype)], axis=0)


def _in_from_kernel_order(g_t):
    lo, hi = 4 * GDN_WIDTH, IN_WIDTH - 2 * HEADS
    g = jnp.concatenate([g_t[:lo], g_t[hi:IN_WIDTH], g_t[lo:hi]], axis=0).reshape(N_CHIPS, IN_SHARD, g_t.shape[-1])
    return jnp.pad(g, ((0, 0), (0, IN_SHARD_PAD - IN_SHARD), (0, 0)))


PACK_LANES = 3 * GDN_WIDTH
PACKED = dict(mix_pre_norm=(0, 1, D_MODEL), mix_post_norm=(1, 1, D_MODEL), ffn_pre_norm=(2, 1, D_MODEL),
              ffn_post_norm=(3, 1, D_MODEL), a_log=(4, 1, HEADS), dt_bias=(5, 1, HEADS), loss=(6, 1, 1),
              gdn_norm=(7, 1, HEAD_DIM), conv_qkv=(8, GDN_CONV, 3 * GDN_WIDTH), conv_sc=(16, SC_CONV, SC_WIDTH),
              meta_tokens=(32, N_META, D_MODEL))
PACK_ROWS = 48
SHARDED_SMALL = ("conv_qkv", "conv_sc", "meta_tokens")


def _pack_small(values):
    names = list(PACKED)

    def body(*refs):
        out_ref = refs[-1]
        out_ref[...] = jnp.zeros_like(out_ref)
        for name, ref in zip(names, refs):
            row, rows, lanes = PACKED[name]
            out_ref[row:row + rows, :lanes] = ref[...]

    return pl.pallas_call(body, name="pack_small", out_shape=jax.ShapeDtypeStruct((PACK_ROWS, PACK_LANES), F32))(
        *[values[name] for name in names])


def _sum_devices(packed_all, chip):
    names = list(PACKED)

    def body(chip_ref, all_ref, *rest):
        shard_refs, out_refs = rest[:len(SHARDED_SMALL)], rest[len(SHARDED_SMALL):]

        def total(ref, rows, lanes):
            acc = ref[0, rows, lanes]
            for k in range(1, 8):
                acc = acc + ref[k, rows, lanes]
            return acc

        for name, out in zip(names, out_refs):
            row, rows, lanes = PACKED[name]
            if name in SHARDED_SMALL:
                out[...] = total(shard_refs[SHARDED_SMALL.index(name)], slice(0, rows), slice(None))
            else:
                out[...] = total(all_ref, slice(row, row + rows), slice(0, lanes))

    def shard_spec(name):
        row, rows, lanes = PACKED[name]
        height = max(rows, 8)
        assert row % height == 0
        return pl.BlockSpec((8, height, lanes // N_CHIPS), lambda i, chip_ref: (0, row // height, chip_ref[0]))

    def out_shape(name):
        _, rows, lanes = PACKED[name]
        return jax.ShapeDtypeStruct((rows, lanes // N_CHIPS if name in SHARDED_SMALL else lanes), F32)

    whole = lambda shape: pl.BlockSpec(shape, lambda i, chip_ref: (0,) * len(shape))
    outs = pl.pallas_call(
        body, name="sum_devices", out_shape=tuple(out_shape(n) for n in names),
        grid_spec=pltpu.PrefetchScalarGridSpec(
            num_scalar_prefetch=1, grid=(1,),
            in_specs=[whole(packed_all.shape)] + [shard_spec(n) for n in SHARDED_SMALL],
            out_specs=tuple(whole(out_shape(n).shape) for n in names)),
    )(chip, packed_all, *[packed_all] * len(SHARDED_SMALL))
    return dict(zip(names, outs))


def _hbm():
    return pl.BlockSpec(memory_space=pl.ANY)


def _place():
    x, y, c = lax.axis_index("x"), lax.axis_index("y"), lax.axis_index("c")
    chips = ((1 - x, y), (x, 1 - y), (1 - x, 1 - y))
    return x, y, c, chips


def _remote(src, dst, send_sems, recv_sems, k, to):
    return pltpu.make_async_remote_copy(src_ref=src, dst_ref=dst, send_sem=send_sems.at[k], recv_sem=recv_sems.at[k],
                                        device_id=to, device_id_type=MESH)


GATHER_SEMS = 7


def _gather_copies(w_refs, out_refs, send_sems, recv_sems):
    x, y, c, chips = _place()
    mine = 2 * x + y
    sibling = (x, y, 1 - c)
    copy = functools.partial(_remote, send_sems=send_sems, recv_sems=recv_sems)
    direct, landed, passing, from_sibling = [], [], [], []
    for i, (w, o) in enumerate(zip(w_refs, out_refs)):
        k = GATHER_SEMS * i
        direct.append(copy(w, o.at[mine], k=k, to=sibling))
        from_sibling.append(copy(w, o.at[mine], k=k, to=sibling))
        for j, (cx, cy) in enumerate(chips):
            theirs = 2 * cx + cy
            direct.append(copy(w.at[c], o.at[mine, c], k=k + 1 + j, to=(cx, cy, c)))
            landed.append(copy(w.at[c], o.at[theirs, c], k=k + 1 + j, to=sibling))
            passing.append(copy(o.at[theirs, c], o.at[theirs, c], k=k + 4 + j, to=sibling))
            from_sibling.append(copy(w.at[c], o.at[theirs, 1 - c], k=k + 4 + j, to=sibling))
    return direct, landed, passing, from_sibling


def _gather_finish(copies):
    direct, landed, passing, from_sibling = copies
    for arrival, forward in zip(landed, passing):
        arrival.wait_recv()
        forward.start()
    for arrival in from_sibling:
        arrival.wait_recv()
    for cp in direct + passing:
        cp.wait_send()


def _gather_weights(pieces, smalls):
    count, extra = len(pieces), len(smalls)
    total = count + extra

    def body(*refs):
        w_refs, s_refs = refs[:count], refs[count:total]
        out_refs, sall_refs = refs[total:total + count], refs[total + count:2 * total]
        send_sems, recv_sems, local_sems = refs[2 * total:]
        x, y, c, chips = _place()
        mine = 2 * x + y
        own = [pltpu.make_async_copy(s, sall.at[mine], local_sems.at[i]) for i, (s, sall) in enumerate(zip(s_refs, sall_refs))]
        small = [_remote(s, sall.at[mine], send_sems, recv_sems, GATHER_SEMS * count + 3 * i + j, (cx, cy, c))
                 for i, (s, sall) in enumerate(zip(s_refs, sall_refs)) for j, (cx, cy) in enumerate(chips)]
        copies = _gather_copies(w_refs, out_refs, send_sems, recv_sems)
        for cp in own + small + copies[0]:
            cp.start()
        _gather_finish(copies)
        for cp in small:
            cp.wait_recv()
        for cp in small:
            cp.wait_send()
        for cp in own:
            cp.wait()

    sems = GATHER_SEMS * count + 3 * extra
    return pl.pallas_call(
        body, name="gather_weights",
        out_shape=tuple(jax.ShapeDtypeStruct((N_CHIPS,) + p.shape, p.dtype) for p in list(pieces) + list(smalls)),
        in_specs=[_hbm()] * total, out_specs=(_hbm(),) * total,
        scratch_shapes=[pltpu.SemaphoreType.DMA((sems,)), pltpu.SemaphoreType.DMA((sems,)), pltpu.SemaphoreType.DMA((extra,))],
    )(*pieces, *smalls)


def _sibling_copies(g_refs, got_refs, send_sems, recv_sems):
    x, y, c, _ = _place()
    return [_remote(g.at[:, 1 - c], got, send_sems, recv_sems, i, (x, y, 1 - c)) for i, (g, got) in enumerate(zip(g_refs, got_refs))]


def _exchange_siblings(grads, small=None):
    count = len(grads)
    extra = 0 if small is None else 1

    def body(*refs):
        g_refs = refs[:count]
        got_refs = refs[count + extra:2 * count + extra]
        send_sems, recv_sems = refs[2 * (count + extra):2 * (count + extra) + 2]
        x, y, c, _ = _place()
        copies = _sibling_copies(g_refs, got_refs, send_sems, recv_sems)
        if small is not None:
            s_ref, sall_ref, local_sem = refs[count], refs[2 * count + 1], refs[-1]
            me = 4 * x + 2 * y + c
            own = pltpu.make_async_copy(s_ref, sall_ref.at[me], local_sem)
            own.start()
            for k in range(7):
                dx, dy, dc = ((k + 1) >> 2) & 1, ((k + 1) >> 1) & 1, (k + 1) & 1
                peer = (1 - x if dx else x, 1 - y if dy else y, 1 - c if dc else c)
                copies.append(_remote(s_ref, sall_ref.at[me], send_sems, recv_sems, count + k, peer))
        for cp in copies:
            cp.start()
        for cp in copies:
            cp.wait_recv()
        for cp in copies:
            cp.wait_send()
        if small is not None:
            own.wait()

    sems = count + 7 * extra
    return pl.pallas_call(
        body, name="exchange_siblings" + ("" if small is None else "_small"),
        out_shape=tuple(jax.ShapeDtypeStruct((g.shape[0],) + g.shape[2:], F32) for g in grads)
        + (() if small is None else (jax.ShapeDtypeStruct((8,) + small.shape, F32),)),
        in_specs=[_hbm()] * (count + extra), out_specs=(_hbm(),) * (count + extra),
        scratch_shapes=[pltpu.SemaphoreType.DMA((sems,)), pltpu.SemaphoreType.DMA((sems,))]
        + ([] if small is None else [pltpu.SemaphoreType.DMA]),
    )(*grads, *(() if small is None else (small,)))


def _chip_copies(p_refs, got_refs, send_sems, recv_sems):
    x, y, c, chips = _place()
    return [_remote(p.at[2 * cx + cy], got.at[j], send_sems, recv_sems, 3 * i + j, (cx, cy, c))
            for i, (p, got) in enumerate(zip(p_refs, got_refs)) for j, (cx, cy) in enumerate(chips)]


def _share_halves(halves):
    count = len(halves)

    def body(*refs):
        h_refs, full_refs = refs[:count], refs[count:2 * count]
        send_sems, recv_sems = refs[2 * count:]
        x, y, c, _ = _place()
        copies = [pltpu.make_async_remote_copy(src_ref=h.at[c], dst_ref=full.at[c], send_sem=send_sems.at[i],
                                               recv_sem=recv_sems.at[i], device_id=(x, y, 1 - c), device_id_type=MESH)
                  for i, (h, full) in enumerate(zip(h_refs, full_refs))]
        for cp in copies:
            cp.start()
        for cp in copies:
            cp.wait_recv()
        for cp in copies:
            cp.wait_send()

    return pl.pallas_call(
        body, name="share_halves", out_shape=tuple(jax.ShapeDtypeStruct(h.shape, h.dtype) for h in halves),
        in_specs=[_hbm()] * count, out_specs=(_hbm(),) * count, input_output_aliases={i: i for i in range(count)},
        scratch_shapes=[pltpu.SemaphoreType.DMA((count,)), pltpu.SemaphoreType.DMA((count,))],
    )(*halves)


def _add_sibling(grad, got, core, name):
    chips, _, rows, cols = grad.shape

    def body(core_ref, g_ref, r_ref, sum_ref, send_ref):
        s = g_ref[...] + r_ref[...]
        sum_ref[...] = s
        send_ref[...] = s.astype(send_ref.dtype)

    block = pl.BlockSpec((None, rows, cols), lambda p, core_ref: (p, 0, 0))
    return pl.pallas_call(
        body, name="add_sibling_" + name,
        out_shape=(jax.ShapeDtypeStruct((chips, rows, cols), F32), jax.ShapeDtypeStruct((chips, rows, cols), BF16)),
        grid_spec=pltpu.PrefetchScalarGridSpec(
            num_scalar_prefetch=1, grid=(chips,),
            in_specs=[pl.BlockSpec((None, None, rows, cols), lambda p, core_ref: (p, core_ref[0], 0, 0)), block],
            out_specs=(block, block)),
        compiler_params=_params("parallel"),
    )(core, grad, got)


def _add_chips(part, got, chip_core, name):
    _, rows, cols = part.shape
    tr = rows // 2 if rows % 32 == 0 else rows

    def body(place_ref, p_ref, r_ref, o_ref):
        o_ref[...] = ((p_ref[...] + r_ref[0].astype(F32)) + r_ref[1].astype(F32)) + r_ref[2].astype(F32)

    return pl.pallas_call(
        body, name="add_chips_" + name, out_shape=jax.ShapeDtypeStruct((2, rows, cols), F32),
        grid_spec=pltpu.PrefetchScalarGridSpec(
            num_scalar_prefetch=1, grid=(rows // tr,),
            in_specs=[pl.BlockSpec((None, tr, cols), lambda i, place_ref: (place_ref[0], i, 0)),
                      pl.BlockSpec((3, tr, cols), lambda i, place_ref: (0, i, 0))],
            out_specs=pl.BlockSpec((None, tr, cols), lambda i, place_ref: (place_ref[1], i, 0))),
        compiler_params=_params("parallel"),
    )(chip_core, part, got)


def _adamw(w, g, m, v, name):
    rows, cols = w.shape
    tr = _pick(rows, (3592, 256, 352, 176, 128, 64, 32, 16, 8))

    def body(w_ref, g_ref, m_ref, v_ref, d_ref, nm_ref, nv_ref):
        g = g_ref[...]
        m = ADAM_B1 * m_ref[...] + (1.0 - ADAM_B1) * g
        v = ADAM_B2 * v_ref[...] + (1.0 - ADAM_B2) * (g * g)
        m_hat = m / (1.0 - ADAM_B1 ** ADAM_STEP)
        v_hat = v / (1.0 - ADAM_B2 ** ADAM_STEP)
        d_ref[...] = -ADAM_LR * (m_hat / (jnp.sqrt(v_hat) + ADAM_EPS) + ADAM_WD * w_ref[...])
        nm_ref[...] = m
        nv_ref[...] = v

    block = pl.BlockSpec((tr, cols), lambda i: (i, 0))
    shape = jax.ShapeDtypeStruct((rows, cols), F32)
    return pl.pallas_call(
        body, name="adamw_" + name, out_shape=(shape, shape, shape), grid=(rows // tr,),
        in_specs=[block] * 4, out_specs=(block,) * 3, compiler_params=_params("parallel"),
    )(w, g, m, v)


WEIGHTS = ("meta_tokens", "mix_pre_norm", "mix_post_norm", "ffn_pre_norm", "ffn_post_norm", "w_in", "conv_qkv", "a_log",
           "dt_bias", "gdn_norm", "conv_sc", "w_out", "w_gate", "w_up", "w_down")


def kernel(x, meta_tokens, mix_pre_norm, mix_post_norm, ffn_pre_norm, ffn_post_norm, w_in, conv_qkv, a_log, dt_bias, gdn_norm, conv_sc, w_out, w_gate, w_up, w_down, loss_target, m_meta_tokens, m_mix_pre_norm, m_mix_post_norm, m_ffn_pre_norm, m_ffn_post_norm, m_w_in, m_conv_qkv, m_a_log, m_dt_bias, m_gdn_norm, m_conv_sc, m_w_out, m_w_gate, m_w_up, m_w_down, v_meta_tokens, v_mix_pre_norm, v_mix_post_norm, v_ffn_pre_norm, v_ffn_post_norm, v_w_in, v_conv_qkv, v_a_log, v_dt_bias, v_gdn_norm, v_conv_sc, v_w_out, v_w_gate, v_w_up, v_w_down):
    d = x.shape[-1]
    two_d = lambda a: a.reshape(a.shape[-2:])
    weights = dict(zip(WEIGHTS, (meta_tokens, mix_pre_norm, mix_post_norm, ffn_pre_norm, ffn_post_norm, w_in, conv_qkv, a_log,
                                 dt_bias, gdn_norm, conv_sc, w_out, w_gate, w_up, w_down)))
    m_in = dict(zip(WEIGHTS, (m_meta_tokens, m_mix_pre_norm, m_mix_post_norm, m_ffn_pre_norm, m_ffn_post_norm, m_w_in, m_conv_qkv,
                              m_a_log, m_dt_bias, m_gdn_norm, m_conv_sc, m_w_out, m_w_gate, m_w_up, m_w_down)))
    v_in = dict(zip(WEIGHTS, (v_meta_tokens, v_mix_pre_norm, v_mix_post_norm, v_ffn_pre_norm, v_ffn_post_norm, v_w_in, v_conv_qkv,
                              v_a_log, v_dt_bias, v_gdn_norm, v_conv_sc, v_w_out, v_w_gate, v_w_up, v_w_down)))
    core = lax.axis_index("c")
    chip = 2 * lax.axis_index("x") + lax.axis_index("y")
    core_arg = core.reshape(1).astype(jnp.int32)
    chip_core = jnp.stack([chip, core]).astype(jnp.int32)
    whole = lambda a: a.reshape(a.shape[:-3] + (2 * a.shape[-2], d))
    by_rows = lambda n, a: two_d(a).T if n in ("w_in", "w_gate", "w_up") else two_d(a)

    shard = {n: by_rows(n, weights[n]).astype(MXU_DTYPE) for n in MATRICES}
    shard["w_in"] = jnp.pad(shard["w_in"], ((0, IN_SHARD_PAD - IN_SHARD), (0, 0)))
    w_in_all, *small_all = _gather_weights([_halves(shard["w_in"])], [two_d(weights[n]) for n in SHARDED_SMALL])
    w_in_t = _in_to_kernel_order(whole(w_in_all))
    conv_qkv_full, conv_sc_full, meta_full = (jnp.concatenate([a[p] for p in range(N_CHIPS)], axis=1) for a in small_all)

    sq, grad_x, g, sums = _local_step(
        x, loss_target, meta_full, (mix_pre_norm, mix_post_norm, ffn_pre_norm, ffn_post_norm), w_in_t, conv_qkv_full, a_log,
        dt_bias, gdn_norm, conv_sc_full, [_halves(shard[n]) for n in LATER], core_arg)

    (packed_all,) = _exchange_siblings([], _pack_small(dict(g, loss=sq)))
    totals = [_add_chips(part, got, chip_core, n) for n, (part, got) in zip(MATRICES, sums)]
    grads = {n: whole(a) for n, a in zip(MATRICES, _share_halves(totals))}
    grads["w_in"] = grads["w_in"][:IN_SHARD]
    grads.update(_sum_devices(packed_all, chip.reshape(1).astype(jnp.int32)))
    loss = (0.5 / d) * grads.pop("loss")[0, 0]

    outs = [[], [], [], []]
    for n in WEIGHTS:
        shape = weights[n].shape
        delta, new_m, new_v = _adamw(by_rows(n, weights[n]), grads[n], by_rows(n, m_in[n]), by_rows(n, v_in[n]), n)
        for out, a in zip(outs, (grads[n], delta, new_m, new_v)):
            out.append((a.T if n in ("w_in", "w_gate", "w_up") else a).reshape(shape))
    return (loss, grad_x, *outs[0], *outs[1], *outs[2], *outs[3])
```

```python
import functools

import jax
import jax.numpy as jnp
from jax import lax
from jax.experimental import pallas as pl
from jax.experimental.pallas import tpu as pltpu

F32 = jnp.float32
BF16 = jnp.bfloat16
MXU_DTYPE = jnp.bfloat16
MESH = pl.DeviceIdType.MESH

D_MODEL = 1024
N_META = 16
HEADS = 4
HEAD_DIM = 128
GDN_WIDTH = HEADS * HEAD_DIM
GDN_CONV = 4
CHUNK = 64
SC_WIDTH = D_MODEL - GDN_WIDTH
SC_CONV = 3
D_FF = 2816
IN_WIDTH = 4 * GDN_WIDTH + 2 * HEADS + 3 * SC_WIDTH
IN_PAD = 3840
BA_COL = (4 * GDN_WIDTH + 3 * SC_WIDTH) // 128
EPS = 1e-6
LANES = 128
N_CHIPS = 4
VMEM_LIMIT = 48 * 2 ** 20

ADAM_LR = 0.001
ADAM_B1 = 0.9
ADAM_B2 = 0.999
ADAM_EPS = 1e-08
ADAM_WD = 0.01
ADAM_STEP = 10


def _pick(n, candidates):
    for c in candidates:
        if n % c == 0:
            return c
    return n


def _row_tile(n):
    return _pick(n, (352, 256, 176, 128, 64, 32, 16, 8))


def _params(*sem):
    return pltpu.CompilerParams(dimension_semantics=sem, vmem_limit_bytes=VMEM_LIMIT)


def _sigmoid(x):
    return 0.5 * jnp.tanh(0.5 * x) + 0.5


def _softplus(x):
    return jnp.maximum(x, 0.0) + jnp.log(1.0 + jnp.exp(-jnp.abs(x)))


def _dsilu(x, s):
    return s * (1.0 + x * (1.0 - s))


def _mm(a, b, mode, out_dtype, name, init=None, exchange=None):
    if mode == "tn":
        k_dim, m_dim = a.shape
    else:
        m_dim, k_dim = a.shape
    n_dim = b.shape[0] if mode == "nt" else b.shape[1]
    tm = _pick(m_dim, (1408, 1280, 1024, 512, 256, 128) if mode == "tn" else (1056, 1024, 704, 512, 256, 128))
    tn = _pick(n_dim, (1408, 1280, 1024, 768, 512, 256, 128))
    tk = _pick(k_dim, (1408, 1280, 1056, 1024, 512, 256, 128))
    nk = k_dim // tk
    if mode == "nn":
        a_spec = pl.BlockSpec((tm, tk), lambda i, j, k: (i, k))
        b_spec = pl.BlockSpec((tk, tn), lambda i, j, k: (k, j))
        dims = (((1,), (0,)), ((), ()))
    elif mode == "nt":
        a_spec = pl.BlockSpec((tm, tk), lambda i, j, k: (i, k))
        b_spec = pl.BlockSpec((tn, tk), lambda i, j, k: (j, k))
        dims = (((1,), (1,)), ((), ()))
    else:
        a_spec = pl.BlockSpec((tk, tm), lambda i, j, k: (k, i))
        b_spec = pl.BlockSpec((tk, tn), lambda i, j, k: (k, j))
        dims = (((0,), (0,)), ((), ()))

    out_spec = pl.BlockSpec((tm, tn), lambda i, j, k: (i, j))
    grid = (m_dim // tm, n_dim // tn, nk)
    parts = () if exchange is None else tuple(exchange)
    count = len(parts)
    first_in = 2 if init is None else 3

    def body(a_ref, b_ref, *rest):
        o_ref = rest[first_in - 2 + count]
        acc_ref = rest[first_in - 1 + 2 * count]
        k = pl.program_id(2)
        step = (pl.program_id(0) * grid[1] + pl.program_id(1)) * nk + k
        if count:
            copies = _chip_copies(rest[first_in - 2:first_in - 2 + count], rest[first_in - 1 + count:first_in - 1 + 2 * count],
                                  *rest[first_in + 2 * count:])

            @pl.when(step == 0)
            def _():
                for cp in copies:
                    cp.start()

        p = lax.dot_general(a_ref[...], b_ref[...], dims, preferred_element_type=F32)

        @pl.when(k == 0)
        def _():
            acc_ref[...] = p if init is None else rest[0][...] + p

        @pl.when(k > 0)
        def _():
            acc_ref[...] += p

        @pl.when(k == nk - 1)
        def _():
            o_ref[...] = acc_ref[...].astype(out_dtype)

        if count:
            @pl.when(step == grid[0] * grid[1] * nk - 1)
            def _():
                for cp in copies:
                    cp.wait_recv()
                for cp in copies:
                    cp.wait_send()

    out = pl.pallas_call(
        body, name=name,
        out_shape=(jax.ShapeDtypeStruct((m_dim, n_dim), out_dtype),)
        + tuple(jax.ShapeDtypeStruct((3,) + p.shape[1:], p.dtype) for p in parts),
        grid=grid,
        in_specs=[a_spec, b_spec] + ([] if init is None else [out_spec]) + [_hbm()] * count,
        out_specs=(out_spec,) + (_hbm(),) * count,
        scratch_shapes=[pltpu.VMEM((tm, tn), F32)]
        + ([pltpu.SemaphoreType.DMA((3 * count,)), pltpu.SemaphoreType.DMA((3 * count,))] if count else []),
        compiler_params=_params(*(("arbitrary",) * 3 if count else ("parallel", "parallel", "arbitrary"))),
    )(a, b, *(() if init is None else (init,)), *parts)
    return out[0] if not count else out


def _rms_apply(x, w):
    r = lax.rsqrt(jnp.mean(x * x, axis=-1, keepdims=True) + EPS)
    return x * r * w


def _rms_bwd(x, w, dy):
    r = lax.rsqrt(jnp.mean(x * x, axis=-1, keepdims=True) + EPS)
    xh = x * r
    dyw = dy * w
    dx = r * (dyw - xh * jnp.mean(dyw * xh, axis=-1, keepdims=True))
    return dx, jnp.sum(dy * xh, axis=0, keepdims=True)


def _accumulate(ref, first, value):
    @pl.when(first)
    def _():
        ref[...] = value

    @pl.when(jnp.logical_not(first))
    def _():
        ref[...] += value


def _rows(tr, width):
    return pl.BlockSpec((tr, width), lambda i: (i, 0))


def _vec(width):
    return pl.BlockSpec((1, width), lambda i: (0, 0))


def _rms_fwd(h, w, name):
    n, d = h.shape
    tr = _row_tile(n)

    def body(h_ref, w_ref, u_ref):
        u_ref[...] = _rms_apply(h_ref[...], w_ref[...]).astype(u_ref.dtype)

    return pl.pallas_call(
        body, name=name, out_shape=jax.ShapeDtypeStruct((n, d), MXU_DTYPE), grid=(n // tr,),
        in_specs=[_rows(tr, d), _vec(d)], out_specs=_rows(tr, d), compiler_params=_params("parallel"),
    )(h, w)


def _mix_residual(h0, mix, w_post, w_pre):
    n, d = h0.shape
    tr = _row_tile(n)

    def body(h0_ref, mix_ref, wpost_ref, wpre_ref, h1_ref, u2_ref):
        h1 = h0_ref[...] + _rms_apply(mix_ref[...], wpost_ref[...])
        h1_ref[...] = h1
        u2_ref[...] = _rms_apply(h1, wpre_ref[...]).astype(u2_ref.dtype)

    return pl.pallas_call(
        body, name="mix_residual",
        out_shape=(jax.ShapeDtypeStruct((n, d), F32), jax.ShapeDtypeStruct((n, d), MXU_DTYPE)), grid=(n // tr,),
        in_specs=[_rows(tr, d), _rows(tr, d), _vec(d), _vec(d)], out_specs=(_rows(tr, d), _rows(tr, d)),
        compiler_params=_params("parallel"),
    )(h0, mix, w_post, w_pre)


NT_DIMS = (((1,), (1,)), ((), ()))


def _ffn_tiles(n):
    return _pick(n, (704, 512, 256, 128)), _pick(D_FF, (1408, 256, 128))


def _swiglu_fwd(u, w_gate_t, w_up_t):
    n, d = u.shape
    tm, tn = _ffn_tiles(n)

    def body(u_ref, wg_ref, wu_ref, g_ref, up_ref, act_ref):
        a = u_ref[...]
        g = lax.dot_general(a, wg_ref[...], NT_DIMS, preferred_element_type=F32)
        up = lax.dot_general(a, wu_ref[...], NT_DIMS, preferred_element_type=F32)
        g_ref[...] = g
        up_ref[...] = up
        act_ref[...] = (g * _sigmoid(g) * up).astype(act_ref.dtype)

    tile = pl.BlockSpec((tm, tn), lambda j, i: (i, j))
    weight = pl.BlockSpec((tn, d), lambda j, i: (j, 0))
    wide = jax.ShapeDtypeStruct((n, D_FF), F32)
    return pl.pallas_call(
        body, name="swiglu_fwd", out_shape=(wide, wide, jax.ShapeDtypeStruct((n, D_FF), MXU_DTYPE)),
        grid=(D_FF // tn, n // tm),
        in_specs=[pl.BlockSpec((tm, d), lambda j, i: (i, 0)), weight, weight], out_specs=(tile, tile, tile),
        compiler_params=_params("parallel", "parallel"),
    )(u, w_gate_t, w_up_t)


def _swiglu_bwd(dffn, w_down, gate, up):
    n, d = dffn.shape
    tm, tn = _ffn_tiles(n)

    def body(dy_ref, w_ref, g_ref, u_ref, dg_ref, du_ref):
        da = lax.dot_general(dy_ref[...], w_ref[...], NT_DIMS, preferred_element_type=F32)
        g = g_ref[...]
        s = _sigmoid(g)
        dg_ref[...] = (da * u_ref[...] * _dsilu(g, s)).astype(dg_ref.dtype)
        du_ref[...] = (da * g * s).astype(du_ref.dtype)

    tile = pl.BlockSpec((tm, tn), lambda j, i: (i, j))
    shape = jax.ShapeDtypeStruct((n, D_FF), MXU_DTYPE)
    return pl.pallas_call(
        body, name="swiglu_bwd", out_shape=(shape, shape), grid=(D_FF // tn, n // tm),
        in_specs=[pl.BlockSpec((tm, d), lambda j, i: (i, 0)), pl.BlockSpec((tn, d), lambda j, i: (j, 0)), tile, tile],
        out_specs=(tile, tile), compiler_params=_params("parallel", "parallel"),
    )(dffn, w_down, gate, up)


def _loss_head(h1, ffn, w_post, target, rows_per_seq, x_offset):
    n, d = h1.shape
    tr = _row_tile(rows_per_seq)
    tiles_per_seq = rows_per_seq // tr

    def body(h1_ref, ffn_ref, w_ref, t_ref, dh2_ref, dffn_ref, dw_ref, sq_ref):
        i = pl.program_id(0)
        w = w_ref[...]
        f = ffn_ref[...]
        r = lax.rsqrt(jnp.mean(f * f, axis=-1, keepdims=True) + EPS)
        fh = f * r
        row = lax.rem(i, tiles_per_seq) * tr + lax.broadcasted_iota(jnp.int32, (tr, 1), 0)
        err = jnp.where(row >= x_offset, h1_ref[...] + fh * w - t_ref[...], 0.0)
        dh2 = err * (1.0 / d)
        dh2_ref[...] = dh2
        dyw = dh2 * w
        dffn_ref[...] = (r * (dyw - fh * jnp.mean(dyw * fh, axis=-1, keepdims=True))).astype(dffn_ref.dtype)
        _accumulate(dw_ref, i == 0, jnp.sum(dh2 * fh, axis=0, keepdims=True))
        _accumulate(sq_ref, i == 0, jnp.sum(jnp.sum(err * err, axis=1, keepdims=True), axis=0, keepdims=True))

    return pl.pallas_call(
        body, name="loss_head",
        out_shape=(jax.ShapeDtypeStruct((n, d), F32), jax.ShapeDtypeStruct((n, d), MXU_DTYPE),
                   jax.ShapeDtypeStruct((1, d), F32), jax.ShapeDtypeStruct((1, 1), F32)),
        grid=(n // tr,),
        in_specs=[_rows(tr, d), _rows(tr, d), _vec(d), _rows(tr, d)],
        out_specs=(_rows(tr, d), _rows(tr, d), _vec(d), _vec(1)),
        compiler_params=_params("arbitrary"),
    )(h1, ffn, w_post, target)


def _mid_bwd(h1, mix, w_mix_post, w_ffn_pre, dh2, du2, grads):
    n, d = h1.shape
    tr = _row_tile(n)
    count = len(grads)

    def body(h1_ref, mix_ref, wpost_ref, wpre_ref, dh2_ref, du2_ref, *rest):
        g_refs, (dh1_ref, dmix_ref, dwpre_ref, dwpost_ref), got_refs = rest[:count], rest[count:count + 4], rest[count + 4:2 * count + 4]
        exchange = _sibling_copies(g_refs, got_refs, *rest[2 * count + 4:])
        i = pl.program_id(0)

        @pl.when(i == 0)
        def _():
            for cp in exchange:
                cp.start()

        dx, dwpre = _rms_bwd(h1_ref[...], wpre_ref[...], du2_ref[...])
        dh1 = dh2_ref[...] + dx
        dh1_ref[...] = dh1
        dmix, dwpost = _rms_bwd(mix_ref[...], wpost_ref[...], dh1)
        dmix_ref[...] = dmix.astype(dmix_ref.dtype)
        _accumulate(dwpre_ref, i == 0, dwpre)
        _accumulate(dwpost_ref, i == 0, dwpost)

        @pl.when(i == n // tr - 1)
        def _():
            for cp in exchange:
                cp.wait_recv()
            for cp in exchange:
                cp.wait_send()

    dh1, dmix, dwpre, dwpost, *got = pl.pallas_call(
        body, name="mid_bwd",
        out_shape=(jax.ShapeDtypeStruct((n, d), F32), jax.ShapeDtypeStruct((n, d), MXU_DTYPE),
                   jax.ShapeDtypeStruct((1, d), F32), jax.ShapeDtypeStruct((1, d), F32))
        + tuple(jax.ShapeDtypeStruct((g.shape[0],) + g.shape[2:], F32) for g in grads),
        grid=(n // tr,),
        in_specs=[_rows(tr, d), _rows(tr, d), _vec(d), _vec(d), _rows(tr, d), _rows(tr, d)] + [_hbm()] * count,
        out_specs=(_rows(tr, d), _rows(tr, d), _vec(d), _vec(d)) + (_hbm(),) * count,
        scratch_shapes=[pltpu.SemaphoreType.DMA((count,)), pltpu.SemaphoreType.DMA((count,))],
        compiler_params=_params("arbitrary"),
    )(h1, mix, w_mix_post, w_ffn_pre, dh2, du2, *grads)
    return dh1, dmix, dwpre, dwpost, got


def _in_bwd(h0, w_pre, dh1, du1):
    n, d = h0.shape
    tr = _row_tile(n)

    def body(h0_ref, w_ref, dh1_ref, du1_ref, dh0_ref, dw_ref):
        dx, dw = _rms_bwd(h0_ref[...], w_ref[...], du1_ref[...])
        dh0_ref[...] = dh1_ref[...] + dx
        _accumulate(dw_ref, pl.program_id(0) == 0, dw)

    return pl.pallas_call(
        body, name="in_bwd",
        out_shape=(jax.ShapeDtypeStruct((n, d), F32), jax.ShapeDtypeStruct((1, d), F32)), grid=(n // tr,),
        in_specs=[_rows(tr, d), _vec(d), _rows(tr, d), _rows(tr, d)], out_specs=(_rows(tr, d), _vec(d)),
        compiler_params=_params("arbitrary"),
    )(h0, w_pre, dh1, du1)


def _lane_is(lo, hi):
    lane = lax.broadcasted_iota(jnp.int32, (1, LANES), 1)
    return jnp.logical_and(lane >= lo, lane < hi)


def _gates_fwd(proj, a_log_l, dt_bias_l, rows_per_seq, pad_rows):
    n = proj.shape[0]
    tr = _row_tile(rows_per_seq)
    tiles_per_seq = rows_per_seq // tr

    def body(p_ref, a_ref, dt_ref, o_ref):
        x = p_ref[...]
        row = lax.rem(pl.program_id(0), tiles_per_seq) * tr + lax.broadcasted_iota(jnp.int32, (tr, 1), 0)
        g = -jnp.exp(a_ref[...]) * _softplus(x + dt_ref[...])
        val = jnp.where(_lane_is(0, HEADS), _sigmoid(x), jnp.where(_lane_is(HEADS, 2 * HEADS), g, 0.0))
        o_ref[...] = jnp.where(row >= pad_rows, val, 0.0)

    return pl.pallas_call(
        body, name="gates_fwd", out_shape=jax.ShapeDtypeStruct((n, LANES), F32), grid=(n // tr,),
        in_specs=[pl.BlockSpec((tr, LANES), lambda i: (i, BA_COL)), _vec(LANES), _vec(LANES)],
        out_specs=_rows(tr, LANES), compiler_params=_params("parallel"),
    )(proj, a_log_l, dt_bias_l)


def _gates_bwd(proj, dbg, a_log_l, dt_bias_l, rows_per_seq, pad_rows):
    n = proj.shape[0]
    tr = _row_tile(rows_per_seq)
    tiles_per_seq = rows_per_seq // tr

    def body(p_ref, d_ref, a_ref, dt_ref, dx_ref, da_ref, ddt_ref):
        i = pl.program_id(0)
        x = p_ref[...]
        d = d_ref[...]
        row = lax.rem(i, tiles_per_seq) * tr + lax.broadcasted_iota(jnp.int32, (tr, 1), 0)
        live = row >= pad_rows
        beta = _sigmoid(x)
        ea = jnp.exp(a_ref[...])
        xa = x + dt_ref[...]
        g = -ea * _softplus(xa)
        is_g = _lane_is(HEADS, 2 * HEADS)
        d_alogit = jnp.where(jnp.logical_and(live, is_g), d * (-ea) * _sigmoid(xa), 0.0)
        d_blogit = jnp.where(jnp.logical_and(live, _lane_is(0, HEADS)), d * beta * (1.0 - beta), 0.0)
        dx_ref[:, :LANES] = (d_alogit + d_blogit).astype(dx_ref.dtype)
        dx_ref[:, LANES:] = jnp.zeros((tr, LANES), dx_ref.dtype)
        _accumulate(da_ref, i == 0, jnp.sum(jnp.where(jnp.logical_and(live, is_g), d * g, 0.0), axis=0, keepdims=True))
        _accumulate(ddt_ref, i == 0, jnp.sum(d_alogit, axis=0, keepdims=True))

    return pl.pallas_call(
        body, name="gates_bwd",
        out_shape=(jax.ShapeDtypeStruct((n, 2 * LANES), MXU_DTYPE), jax.ShapeDtypeStruct((1, LANES), F32),
                   jax.ShapeDtypeStruct((1, LANES), F32)),
        grid=(n // tr,),
        in_specs=[pl.BlockSpec((tr, LANES), lambda i: (i, BA_COL)), _rows(tr, LANES), _vec(LANES), _vec(LANES)],
        out_specs=(_rows(tr, 2 * LANES), _vec(LANES), _vec(LANES)),
        compiler_params=_params("arbitrary"),
    )(proj, dbg, a_log_l, dt_bias_l)


HALO = 8


def _halo_scratch(rs):
    return pltpu.VMEM((rs + 2 * HALO, LANES), F32)


def _stage(ref, x):
    rs = x.shape[0]
    ref[0:HALO, :] = jnp.zeros((HALO, LANES), F32)
    ref[HALO + rs:, :] = jnp.zeros((HALO, LANES), F32)
    ref[HALO:HALO + rs, :] = x


def _shifted(ref, k, rs):
    return ref[pl.ds(HALO - k, rs), :]


def _causal_conv(x, x_staged, w, width):
    acc = w[width - 1:width, :] * x
    for i in range(width - 1):
        acc = acc + w[i:i + 1, :] * _shifted(x_staged, width - 1 - i, x.shape[0])
    return acc


def _anti_causal_conv(dy, dy_staged, w, width):
    acc = w[width - 1:width, :] * dy
    for i in range(width - 1):
        acc = acc + w[i:i + 1, :] * _shifted(dy_staged, -(width - 1 - i), dy.shape[0])
    return acc


def _conv_weight_grad(dy, x, x_staged, width):
    taps = [_shifted(x_staged, width - 1 - i, x.shape[0]) for i in range(width - 1)] + [x]
    return jnp.concatenate([jnp.sum(dy * tap, axis=0, keepdims=True) for tap in taps], axis=0)


def _seq_head(rs, col0):
    return pl.BlockSpec((rs, LANES), lambda j, b: (b, col0 + j))


def _qkv_fwd(proj, conv_w, kind, rs):
    n = proj.shape[0]
    col0 = {"q": 0, "k": HEADS, "v": 2 * HEADS}[kind]

    def body(p_ref, w_ref, o_ref, staged):
        pre = p_ref[...]
        _stage(staged, pre)
        c = _causal_conv(pre, staged, w_ref[...], GDN_CONV)
        s = c * _sigmoid(c)
        if kind != "v":
            s = s * lax.rsqrt(jnp.sum(s * s, axis=-1, keepdims=True) + EPS)
        if kind == "q":
            s = s * (HEAD_DIM ** -0.5)
        o_ref[...] = s

    return pl.pallas_call(
        body, name="qkv_fwd_" + kind, out_shape=jax.ShapeDtypeStruct((n, GDN_WIDTH), F32), grid=(HEADS, n // rs),
        in_specs=[_seq_head(rs, col0), pl.BlockSpec((GDN_CONV, LANES), lambda j, b: (0, col0 + j))],
        out_specs=_seq_head(rs, 0), scratch_shapes=[_halo_scratch(rs)], compiler_params=_params("parallel", "parallel"),
    )(proj, conv_w)


def _qkv_bwd(dy, proj, conv_w, kind, rs):
    n = proj.shape[0]
    col0 = {"q": 0, "k": HEADS, "v": 2 * HEADS}[kind]

    def body(dy_ref, p_ref, w_ref, dp_ref, dw_ref, pre_staged, dc_staged):
        pre = p_ref[...]
        w = w_ref[...]
        _stage(pre_staged, pre)
        c = _causal_conv(pre, pre_staged, w, GDN_CONV)
        sg = _sigmoid(c)
        s = c * sg
        ds = dy_ref[...]
        if kind == "q":
            ds = ds * (HEAD_DIM ** -0.5)
        if kind != "v":
            r = lax.rsqrt(jnp.sum(s * s, axis=-1, keepdims=True) + EPS)
            sh = s * r
            ds = r * (ds - sh * jnp.sum(ds * sh, axis=-1, keepdims=True))
        dc = ds * _dsilu(c, sg)
        _stage(dc_staged, dc)
        dp_ref[...] = _anti_causal_conv(dc, dc_staged, w, GDN_CONV).astype(dp_ref.dtype)
        _accumulate(dw_ref, pl.program_id(1) == 0, _conv_weight_grad(dc, pre, pre_staged, GDN_CONV))

    return pl.pallas_call(
        body, name="qkv_bwd_" + kind,
        out_shape=(jax.ShapeDtypeStruct((n, GDN_WIDTH), MXU_DTYPE), jax.ShapeDtypeStruct((GDN_CONV, GDN_WIDTH), F32)),
        grid=(HEADS, n // rs),
        in_specs=[_seq_head(rs, 0), _seq_head(rs, col0), pl.BlockSpec((GDN_CONV, LANES), lambda j, b: (0, col0 + j))],
        out_specs=(_seq_head(rs, 0), pl.BlockSpec((GDN_CONV, LANES), lambda j, b: (0, j))),
        scratch_shapes=[_halo_scratch(rs), _halo_scratch(rs)],
        compiler_params=_params("parallel", "arbitrary"),
    )(dy, proj, conv_w)


SC_COL = 4 * HEADS


def _sc_fwd(proj, conv_w, rs):
    n = proj.shape[0]

    def body(x_ref, b_ref, c_ref, w_ref, y_ref, staged):
        u = c_ref[...] * x_ref[...]
        _stage(staged, u)
        y_ref[...] = (b_ref[...] * _causal_conv(u, staged, w_ref[...], SC_CONV)).astype(y_ref.dtype)

    return pl.pallas_call(
        body, name="sc_fwd", out_shape=jax.ShapeDtypeStruct((n, SC_WIDTH), MXU_DTYPE), grid=(HEADS, n // rs),
        in_specs=[_seq_head(rs, SC_COL), _seq_head(rs, SC_COL + 4), _seq_head(rs, SC_COL + 8),
                  pl.BlockSpec((SC_CONV, LANES), lambda j, b: (0, j))],
        out_specs=_seq_head(rs, 0), scratch_shapes=[_halo_scratch(rs)], compiler_params=_params("parallel", "parallel"),
    )(proj, proj, proj, conv_w)


def _sc_bwd(dcat, proj, conv_w, rs):
    n = proj.shape[0]

    def body(dy_ref, x_ref, b_ref, c_ref, w_ref, dx_ref, db_ref, dc_ref, dw_ref, u_staged, dcv_staged):
        w = w_ref[...]
        x = x_ref[...]
        cc = c_ref[...]
        u = cc * x
        _stage(u_staged, u)
        dy = dy_ref[...]
        db_ref[...] = (dy * _causal_conv(u, u_staged, w, SC_CONV)).astype(db_ref.dtype)
        dcv = dy * b_ref[...]
        _stage(dcv_staged, dcv)
        du = _anti_causal_conv(dcv, dcv_staged, w, SC_CONV)
        dx_ref[...] = (du * cc).astype(dx_ref.dtype)
        dc_ref[...] = (du * x).astype(dc_ref.dtype)
        _accumulate(dw_ref, pl.program_id(1) == 0, _conv_weight_grad(dcv, u, u_staged, SC_CONV))

    piece = jax.ShapeDtypeStruct((n, SC_WIDTH), MXU_DTYPE)
    return pl.pallas_call(
        body, name="sc_bwd", out_shape=(piece, piece, piece, jax.ShapeDtypeStruct((SC_CONV, SC_WIDTH), F32)),
        grid=(HEADS, n // rs),
        in_specs=[_seq_head(rs, HEADS), _seq_head(rs, SC_COL), _seq_head(rs, SC_COL + 4), _seq_head(rs, SC_COL + 8),
                  pl.BlockSpec((SC_CONV, LANES), lambda j, b: (0, j))],
        out_specs=(_seq_head(rs, 0), _seq_head(rs, 0), _seq_head(rs, 0),
                   pl.BlockSpec((SC_CONV, LANES), lambda j, b: (0, j))),
        scratch_shapes=[_halo_scratch(rs), _halo_scratch(rs)],
        compiler_params=_params("parallel", "arbitrary"),
    )(dcat, proj, proj, proj, conv_w)


Z_COL = 3 * HEADS


def _gate_fwd(o, proj, gdn_norm, rs):
    n = proj.shape[0]

    def body(o_ref, z_ref, w_ref, y_ref):
        z = z_ref[...]
        y_ref[...] = (_rms_apply(o_ref[...], w_ref[...]) * z * _sigmoid(z)).astype(y_ref.dtype)

    return pl.pallas_call(
        body, name="gate_fwd", out_shape=jax.ShapeDtypeStruct((n, GDN_WIDTH), MXU_DTYPE), grid=(HEADS, n // rs),
        in_specs=[_seq_head(rs, 0), _seq_head(rs, Z_COL), pl.BlockSpec((1, LANES), lambda j, b: (0, 0))],
        out_specs=_seq_head(rs, 0), compiler_params=_params("parallel", "parallel"),
    )(o, proj, gdn_norm)


def _gate_bwd(dcat, o, proj, gdn_norm, rs):
    n = proj.shape[0]

    def body(dy_ref, o_ref, z_ref, w_ref, do_ref, dz_ref, dw_ref):
        z = z_ref[...]
        w = w_ref[...]
        o = o_ref[...]
        dy = dy_ref[...]
        s = _sigmoid(z)
        dz_ref[...] = (dy * _rms_apply(o, w) * _dsilu(z, s)).astype(dz_ref.dtype)
        do, dw = _rms_bwd(o, w, dy * z * s)
        do_ref[...] = do
        _accumulate(dw_ref, jnp.logical_and(pl.program_id(0) == 0, pl.program_id(1) == 0), dw)

    return pl.pallas_call(
        body, name="gate_bwd",
        out_shape=(jax.ShapeDtypeStruct((n, GDN_WIDTH), F32), jax.ShapeDtypeStruct((n, GDN_WIDTH), MXU_DTYPE),
                   jax.ShapeDtypeStruct((1, LANES), F32)),
        grid=(HEADS, n // rs),
        in_specs=[_seq_head(rs, 0), _seq_head(rs, 0), _seq_head(rs, Z_COL), pl.BlockSpec((1, LANES), lambda j, b: (0, 0))],
        out_specs=(_seq_head(rs, 0), _seq_head(rs, 0), pl.BlockSpec((1, LANES), lambda j, b: (0, 0))),
        compiler_params=_params("arbitrary", "arbitrary"),
    )(dcat, o, proj, gdn_norm)


def _dot(a, b):
    return jnp.dot(a.astype(MXU_DTYPE), b.astype(MXU_DTYPE), preferred_element_type=F32)


def _dot_nt(a, b):
    return lax.dot_general(a.astype(MXU_DTYPE), b.astype(MXU_DTYPE), (((1,), (1,)), ((), ())),
                           preferred_element_type=F32)


def _dot_tn(a, b):
    return lax.dot_general(a.astype(MXU_DTYPE), b.astype(MXU_DTYPE), (((0,), (0,)), ((), ())),
                           preferred_element_type=F32)


def _split(x):
    hi = x.astype(MXU_DTYPE)
    return hi, (x - hi.astype(F32)).astype(MXU_DTYPE)


def _dot_split(a, b):
    mm = functools.partial(jnp.dot, preferred_element_type=F32)
    return mm(a[0], b[0]) + (mm(a[0], b[1]) + mm(a[1], b[0]))


def _unit_lower_inverses(mats, eye):
    inv = [eye - a for a in mats]
    power = [_split(a) for a in mats]
    span = 2
    while span < CHUNK:
        power = [_split(_dot_split(p, p)) for p in power]
        inv = [i + _dot_split(_split(i), p) for i, p in zip(inv, power)]
        span *= 2
    return inv


def _chunk_masks():
    ii = lax.broadcasted_iota(jnp.int32, (CHUNK, CHUNK), 0)
    jj = lax.broadcasted_iota(jnp.int32, (CHUNK, CHUNK), 1)
    return ii, jj


def _chunk_decay(g_col, ii, jj):
    incl = ii >= jj
    g_row = jnp.sum(jnp.where(ii == jj, g_col, 0.0), axis=0, keepdims=True)
    gc_col = jnp.sum(jnp.where(incl, g_row, 0.0), axis=1, keepdims=True)
    gc_row = jnp.sum(jnp.where(ii <= jj, g_col, 0.0), axis=0, keepdims=True)
    g_total = jnp.sum(g_row, axis=1, keepdims=True)
    decay = jnp.where(incl, jnp.exp(jnp.where(incl, gc_col - gc_row, 0.0)), 0.0)
    return gc_col, g_total, decay


def _gdn_segments(rs, candidates):
    chunks = rs // CHUNK
    seg_chunks = _pick(chunks, candidates)
    return chunks, seg_chunks, chunks // seg_chunks


def _head_lanes(h):
    return slice(h * HEAD_DIM, (h + 1) * HEAD_DIM)


def _gdn_fwd(q, k, v, bg, rs, pieces):
    n = q.shape[0]
    batch = n // rs
    chunks, seg_chunks, segs = _gdn_segments(rs, (11, 8, 4, 2))
    seg_rows = seg_chunks * CHUNK
    chains = [(b, h) for b in range(batch) for h in range(HEADS)]
    each = lambda f, *lists: [f(*args) for args in zip(*lists)]
    count = len(pieces)

    def body(q_ref, k_ref, v_ref, bg_ref, *rest):
        w_refs, (o_ref, s_ref, t_ref), out_refs = rest[:count], rest[count:count + 3], rest[count + 3:2 * count + 3]
        state_ref, send_sems, recv_sems = rest[2 * count + 3:]
        gather = _gather_copies(w_refs, out_refs, send_sems, recv_sems)

        @pl.when(pl.program_id(0) == 0)
        def _():
            state_ref[...] = jnp.zeros_like(state_ref)
            for cp in gather[0]:
                cp.start()

        ii, jj = _chunk_masks()
        incl = ii >= jj
        eye = (ii == jj).astype(F32)

        def chunk(c, carry):
            rows = pl.ds(pl.multiple_of(c * CHUNK, CHUNK), CHUNK)
            bgc = [bg_ref[b, rows, :] for b in range(batch)]
            qc = [q_ref[b, rows, _head_lanes(h)] for b, h in chains]
            kc = [k_ref[b, rows, _head_lanes(h)] for b, h in chains]
            vc = [v_ref[b, rows, _head_lanes(h)] for b, h in chains]
            beta = [bgc[b][:, h:h + 1] for b, h in chains]
            state = [state_ref[b, h] for b, h in chains]
            dec = [_chunk_decay(bgc[b][:, HEADS + h:HEADS + h + 1], ii, jj) for b, h in chains]
            gc_col, g_total, decay = ([d[i] for d in dec] for i in range(3))
            kb = each(lambda x, y: x * y, kc, beta)
            a = each(lambda x, y, d: jnp.where(ii > jj, _dot_nt(x, y) * d, 0.0), kb, kc, decay)
            t_inv = _unit_lower_inverses(a, eye)
            eg = [jnp.exp(g) for g in gc_col]
            u = each(lambda t, x, y: _dot(t, x * y), t_inv, vc, beta)
            w = each(lambda t, x, e: _dot(t, x * e), t_inv, kb, eg)
            qk = each(lambda x, y, d: jnp.where(incl, _dot_nt(x, y) * d, 0.0), qc, kc, decay)
            v_new = each(lambda x, y, s: x - _dot(y, s), u, w, state)
            o = each(lambda x, e, s, m, vn: _dot(x * e, s) + _dot(m, vn), qc, eg, state, qk, v_new)
            new_state = each(lambda s, gt, x, g, vn: s * jnp.exp(gt) + _dot_tn(x * jnp.exp(gt - g), vn),
                             state, g_total, kc, gc_col, v_new)
            for i, (b, h) in enumerate(chains):
                s_ref[b, h, c] = state[i]
                t_ref[b, h, c] = t_inv[i]
                o_ref[b, rows, _head_lanes(h)] = o[i]
                state_ref[b, h] = new_state[i]
            return carry

        lax.fori_loop(0, seg_chunks, chunk, 0)

        @pl.when(pl.program_id(0) == segs - 1)
        def _():
            _gather_finish(gather)

    rows_spec = lambda width: pl.BlockSpec((batch, seg_rows, width), lambda s: (0, s, 0))
    per_chunk = lambda r, c: pl.BlockSpec((batch, HEADS, seg_chunks, r, c), lambda s: (0, 0, s, 0, 0))
    as_seqs = lambda a: a.reshape(batch, rs, a.shape[-1])
    sems = GATHER_SEMS * count
    o, states, t_invs, *gathered = pl.pallas_call(
        body, name="gdn_fwd",
        out_shape=(jax.ShapeDtypeStruct((batch, rs, GDN_WIDTH), F32),
                   jax.ShapeDtypeStruct((batch, HEADS, chunks, HEAD_DIM, HEAD_DIM), F32),
                   jax.ShapeDtypeStruct((batch, HEADS, chunks, CHUNK, CHUNK), F32))
        + tuple(jax.ShapeDtypeStruct((N_CHIPS,) + p.shape, p.dtype) for p in pieces),
        grid=(segs,),
        in_specs=[rows_spec(GDN_WIDTH), rows_spec(GDN_WIDTH), rows_spec(GDN_WIDTH), rows_spec(LANES)] + [_hbm()] * count,
        out_specs=(rows_spec(GDN_WIDTH), per_chunk(HEAD_DIM, HEAD_DIM), per_chunk(CHUNK, CHUNK)) + (_hbm(),) * count,
        scratch_shapes=[pltpu.VMEM((batch, HEADS, HEAD_DIM, HEAD_DIM), F32), pltpu.SemaphoreType.DMA((sems,)),
                        pltpu.SemaphoreType.DMA((sems,))],
        compiler_params=_params("arbitrary"),
    )(as_seqs(q), as_seqs(k), as_seqs(v), as_seqs(bg), *pieces)
    return o.reshape(n, GDN_WIDTH), states, t_invs, gathered


def _gdn_bwd(do, q, k, v, bg, states, t_invs, rs, parts):
    n = q.shape[0]
    batch = n // rs
    chunks, seg_chunks, segs = _gdn_segments(rs, (3, 4, 2))
    seg_rows = seg_chunks * CHUNK
    chains = [(b, h) for b in range(batch) for h in range(HEADS)]
    each = lambda f, *lists: [f(*args) for args in zip(*lists)]
    count = len(parts)

    def body(do_ref, q_ref, k_ref, v_ref, bg_ref, s_ref, t_ref, *rest):
        p_refs, (dq_ref, dk_ref, dv_ref, dbg_ref), got_refs = rest[:count], rest[count:count + 4], rest[count + 4:2 * count + 4]
        dstate_ref, send_sems, recv_sems = rest[2 * count + 4:]
        exchange = _chip_copies(p_refs, got_refs, send_sems, recv_sems)

        @pl.when(pl.program_id(0) == 0)
        def _():
            dstate_ref[...] = jnp.zeros_like(dstate_ref)
            for cp in exchange:
                cp.start()

        ii, jj = _chunk_masks()
        incl = ii >= jj
        strict = ii > jj
        lane = lax.broadcasted_iota(jnp.int32, (1, LANES), 1)

        def rowsum(x):
            return jnp.sum(x, axis=1, keepdims=True)

        def total(x):
            return jnp.sum(rowsum(x), axis=0, keepdims=True)

        def chunk(step, carry):
            c = seg_chunks - 1 - step
            rows = pl.ds(pl.multiple_of(c * CHUNK, CHUNK), CHUNK)
            bgc = [bg_ref[b, rows, :] for b in range(batch)]
            qc = [q_ref[b, rows, _head_lanes(h)] for b, h in chains]
            kc = [k_ref[b, rows, _head_lanes(h)] for b, h in chains]
            vc = [v_ref[b, rows, _head_lanes(h)] for b, h in chains]
            doc = [do_ref[b, rows, _head_lanes(h)] for b, h in chains]
            beta = [bgc[b][:, h:h + 1] for b, h in chains]
            state = [s_ref[b, h, c] for b, h in chains]
            t_inv = [t_ref[b, h, c] for b, h in chains]
            d_state = [dstate_ref[b, h] for b, h in chains]
            dec = [_chunk_decay(bgc[b][:, HEADS + h:HEADS + h + 1], ii, jj) for b, h in chains]
            gc_col, g_total, decay = ([d[i] for d in dec] for i in range(3))
            kb = each(lambda x, y: x * y, kc, beta)
            vb = each(lambda x, y: x * y, vc, beta)
            eg = [jnp.exp(g) for g in gc_col]
            kbg = each(lambda x, y: x * y, kb, eg)
            a = each(lambda x, y, d: jnp.where(strict, _dot_nt(x, y) * d, 0.0), kb, kc, decay)
            qk = each(lambda x, y, d: jnp.where(incl, _dot_nt(x, y) * d, 0.0), qc, kc, decay)
            w = each(_dot, t_inv, kbg)
            u = each(_dot, t_inv, vb)
            q_dec = each(lambda x, y: x * y, qc, eg)
            ek = each(lambda gt, g: jnp.exp(gt - g), g_total, gc_col)
            k_dec = each(lambda x, y: x * y, kc, ek)
            g_last = [jnp.exp(gt) for gt in g_total]
            v_new = each(lambda x, y, s: x - _dot(y, s), u, w, state)
            dv_new = each(lambda m, d, x, ds: _dot_tn(m, d) + _dot(x, ds), qk, doc, k_dec, d_state)
            dqk = each(lambda d, vn: jnp.where(incl, _dot_nt(d, vn), 0.0), doc, v_new)
            dq_dec = each(_dot_nt, doc, state)
            dk_dec = each(_dot_nt, v_new, d_state)
            dg_last = each(lambda s, ds: total(s * ds), state, d_state)
            new_d_state = each(lambda x, d, gl, ds, y, dvn: _dot_tn(x, d) + gl * ds - _dot_tn(y, dvn),
                               q_dec, doc, g_last, d_state, w, dv_new)
            dw = each(lambda dvn, s: -_dot_nt(dvn, s), dv_new, state)
            dt = each(lambda dvn, x, y, z: _dot_nt(dvn, x) + _dot_nt(y, z), dv_new, vb, dw, kbg)
            dvb = each(_dot_tn, t_inv, dv_new)
            dkbg = each(_dot_tn, t_inv, dw)
            t_dt = each(_dot_tn, t_inv, dt)
            da = each(lambda x, t: -jnp.where(strict, _dot_nt(x, t), 0.0), t_dt, t_inv)
            dm_a = each(lambda x, y: x * y, da, decay)
            dm_qk = each(lambda x, y: x * y, dqk, decay)
            e = each(lambda x, y, z, t: x * y + z * t, da, a, dqk, qk)
            dkb = each(lambda m, x, y, z: _dot(m, x) + y * z, dm_a, kc, dkbg, eg)
            dk = each(lambda m, x, m2, y, z, t, p, bt: _dot_tn(m, x) + _dot_tn(m2, y) + z * t + p * bt,
                      dm_a, kb, dm_qk, qc, dk_dec, ek, dkb, beta)
            dq = each(lambda m, x, y, z: _dot(m, x) + y * z, dm_qk, kc, dq_dec, eg)
            dbeta = each(lambda x, y, z, t: rowsum(x * y + z * t), dkb, kc, dvb, vc)
            dgc = each(lambda x, p, pd, r, rd, s, sd: rowsum(x) - rowsum(jnp.where(ii == jj, jnp.sum(x, axis=0, keepdims=True), 0.0))
                       + rowsum(p * pd - r * rd + s * sd), e, dq_dec, q_dec, dk_dec, k_dec, dkbg, kbg)
            d_total = each(lambda r, rd, x, gl: total(r * rd) + x * gl, dk_dec, k_dec, dg_last, g_last)
            dg = each(lambda x, t: rowsum(jnp.where(jj >= ii, jnp.sum(jnp.where(ii == jj, x, 0.0), axis=0, keepdims=True), 0.0)) + t,
                      dgc, d_total)
            dbg = [jnp.zeros((CHUNK, LANES), F32) for _ in range(batch)]
            for i, (b, h) in enumerate(chains):
                dstate_ref[b, h] = new_d_state[i]
                dk_ref[b, rows, _head_lanes(h)] = dk[i]
                dq_ref[b, rows, _head_lanes(h)] = dq[i]
                dv_ref[b, rows, _head_lanes(h)] = dvb[i] * beta[i]
                dbg[b] = dbg[b] + jnp.where(lane == h, dbeta[i], 0.0) + jnp.where(lane == HEADS + h, dg[i], 0.0)
            for b in range(batch):
                dbg_ref[b, rows, :] = dbg[b]
            return carry

        lax.fori_loop(0, seg_chunks, chunk, 0)

        @pl.when(pl.program_id(0) == segs - 1)
        def _():
            for cp in exchange:
                cp.wait_recv()
            for cp in exchange:
                cp.wait_send()

    rows_spec = lambda width: pl.BlockSpec((batch, seg_rows, width), lambda s: (0, segs - 1 - s, 0))
    per_chunk = lambda r, c: pl.BlockSpec((batch, HEADS, seg_chunks, r, c), lambda s: (0, 0, segs - 1 - s, 0, 0))
    as_seqs = lambda a: a.reshape(batch, rs, a.shape[-1])
    grad = jax.ShapeDtypeStruct((batch, rs, GDN_WIDTH), F32)
    wide = rows_spec(GDN_WIDTH)
    dq, dk, dv, dbg, *got = pl.pallas_call(
        body, name="gdn_bwd",
        out_shape=(grad, grad, grad, jax.ShapeDtypeStruct((batch, rs, LANES), F32))
        + tuple(jax.ShapeDtypeStruct((3,) + p.shape[1:], p.dtype) for p in parts),
        grid=(segs,),
        in_specs=[wide, wide, wide, wide, rows_spec(LANES), per_chunk(HEAD_DIM, HEAD_DIM), per_chunk(CHUNK, CHUNK)]
        + [_hbm()] * count,
        out_specs=(wide, wide, wide, rows_spec(LANES)) + (_hbm(),) * count,
        scratch_shapes=[pltpu.VMEM((batch, HEADS, HEAD_DIM, HEAD_DIM), F32), pltpu.SemaphoreType.DMA((3 * count,)),
                        pltpu.SemaphoreType.DMA((3 * count,))],
        compiler_params=_params("arbitrary"),
    )(as_seqs(do), as_seqs(q), as_seqs(k), as_seqs(v), as_seqs(bg), states, t_invs, *parts)
    return dq.reshape(n, GDN_WIDTH), dk.reshape(n, GDN_WIDTH), dv.reshape(n, GDN_WIDTH), dbg.reshape(n, LANES), got


def _lane_vec(vals, offset):
    k = vals.shape[1]
    return jnp.pad(vals, ((0, 0), (offset, LANES - offset - k)))


LATER = ("w_out", "w_gate", "w_up", "w_down")


def _halves(a):
    return a.reshape(a.shape[:-2] + (2, a.shape[-2] // 2, a.shape[-1]))


def _local_step(x, target, meta, norms, w_in_t, conv_qkv, a_log, dt_bias, gdn_norm, conv_sc, later_shards, core_arg):
    batch, seq, d = x.shape
    tokens = N_META + seq
    pad_rows = (-tokens) % CHUNK
    rs = tokens + pad_rows
    x_offset = pad_rows + N_META
    n = batch * rs
    w_mix_pre, w_mix_post, w_ffn_pre, w_ffn_post = norms

    head = jnp.concatenate([jnp.zeros((pad_rows, d), F32), meta], axis=0)
    h0 = jnp.concatenate([jnp.broadcast_to(head[None], (batch, x_offset, d)), x], axis=1).reshape(n, d)
    target_p = jnp.pad(target, ((0, 0), (x_offset, 0), (0, 0))).reshape(n, d)
    a_log_l = _lane_vec(a_log, HEADS)
    dt_bias_l = _lane_vec(dt_bias, HEADS)

    u1 = _rms_fwd(h0, w_mix_pre, "rms_mix_pre")
    proj = _mm(u1, w_in_t, "nt", F32, "mm_proj")
    q = _qkv_fwd(proj, conv_qkv, "q", rs)
    k = _qkv_fwd(proj, conv_qkv, "k", rs)
    v = _qkv_fwd(proj, conv_qkv, "v", rs)
    bg = _gates_fwd(proj, a_log_l, dt_bias_l, rs, pad_rows)
    o, states, t_invs, gathered = _gdn_fwd(q, k, v, bg, rs, later_shards)
    w_out, w_gate_t, w_up_t, w_down = (a.reshape(-1, d) for a in gathered)
    o_gated = _gate_fwd(o, proj, gdn_norm, rs)
    y_sc = _sc_fwd(proj, conv_sc, rs)
    cat = jnp.concatenate([o_gated, y_sc], axis=1)
    mix = _mm(cat, w_out, "nn", F32, "mm_mix")
    h1, u2 = _mix_residual(h0, mix, w_mix_post, w_ffn_pre)
    gate, up, act = _swiglu_fwd(u2, w_gate_t, w_up_t)
    ffn = _mm(act, w_down, "nn", F32, "mm_down")

    dh2, dffn, d_ffn_post, sq = _loss_head(h1, ffn, w_ffn_post, target_p, rs, x_offset)
    d_w_down = _mm(act, dffn, "tn", F32, "mm_dw_down")
    dgate, dup = _swiglu_bwd(dffn, w_down, gate, up)
    d_w_gate_t = _mm(dgate, u2, "tn", F32, "mm_dw_gate")
    d_w_up_t = _mm(dup, u2, "tn", F32, "mm_dw_up")
    du2 = _mm(dup, w_up_t, "nn", F32, "mm_du2_up", init=_mm(dgate, w_gate_t, "nn", F32, "mm_du2_gate"))
    by_chip = [_halves(g.reshape(N_CHIPS, -1, d)) for g in (d_w_gate_t, d_w_up_t, d_w_down)]
    dh1, dmix, d_ffn_pre, d_mix_post, got_sibling = _mid_bwd(h1, mix, w_mix_post, w_ffn_pre, dh2, du2, by_chip)
    dcat = _mm(dmix, w_out, "nt", F32, "mm_dcat")
    d_w_out = _halves(_mm(cat, dmix, "tn", F32, "mm_dw_out").reshape(N_CHIPS, -1, d))
    by_chip, got_sibling = [d_w_out] + by_chip, list(_exchange_siblings([d_w_out])) + got_sibling
    sums = [_add_sibling(a, b, core_arg, name) for name, a, b in zip(LATER, by_chip, got_sibling)]
    do, dz, d_gdn_norm = _gate_bwd(dcat, o, proj, gdn_norm, rs)
    dscx, dscb, dscc, d_conv_sc = _sc_bwd(dcat, proj, conv_sc, rs)
    dq, dk, dv, dbg, got_chips = _gdn_bwd(do, q, k, v, bg, states, t_invs, rs, [send for _, send in sums])
    dpq, dwq = _qkv_bwd(dq, proj, conv_qkv, "q", rs)
    dpk, dwk = _qkv_bwd(dk, proj, conv_qkv, "k", rs)
    dpv, dwv = _qkv_bwd(dv, proj, conv_qkv, "v", rs)
    d_conv_qkv = jnp.concatenate([dwq, dwk, dwv], axis=1)
    dba, d_a_log_l, d_dt_bias_l = _gates_bwd(proj, dbg, a_log_l, dt_bias_l, rs, pad_rows)
    dproj = jnp.concatenate([dpq, dpk, dpv, dz, dscx, dscb, dscc, dba], axis=1)
    g_in = _halves(_in_from_kernel_order(_mm(dproj, u1, "tn", F32, "mm_dw_in")))
    sums.insert(0, _add_sibling(g_in, _exchange_siblings([g_in])[0], core_arg, "w_in"))
    du1, got_in = _mm(dproj, w_in_t, "nn", F32, "mm_du1", exchange=[sums[0][1]])
    got_chips.insert(0, got_in)
    dh0, d_mix_pre = _in_bwd(h0, w_mix_pre, dh1, du1)

    dh0 = dh0.reshape(batch, rs, d)
    grads = dict(
        meta_tokens=jnp.sum(dh0[:, pad_rows:x_offset], axis=0),
        mix_pre_norm=d_mix_pre, mix_post_norm=d_mix_post, ffn_pre_norm=d_ffn_pre, ffn_post_norm=d_ffn_post,
        conv_qkv=d_conv_qkv,
        a_log=d_a_log_l[:, HEADS:2 * HEADS], dt_bias=d_dt_bias_l[:, HEADS:2 * HEADS],
        gdn_norm=d_gdn_norm, conv_sc=d_conv_sc,
    )
    return sq, dh0[:, x_offset:], grads, [(part, got) for (part, _), got in zip(sums, got_chips)]


MATRICES = ("w_in", "w_out", "w_gate", "w_up", "w_down")
IN_SHARD = IN_WIDTH // N_CHIPS
IN_SHARD_PAD = 928


def _in_to_kernel_order(by_chip):
    w_t = by_chip[:, :IN_SHARD].reshape(IN_WIDTH, by_chip.shape[-1])
    lo, hi = 4 * GDN_WIDTH, 4 * GDN_WIDTH + 2 * HEADS
    return jnp.concatenate([w_t[:lo], w_t[hi:], w_t[lo:hi], jnp.zeros((IN_PAD - IN_WIDTH, w_t.shape[1]), w_t.dtype)], axis=0)


def _in_from_kernel_order(g_t):
    lo, hi = 4 * GDN_WIDTH, IN_WIDTH - 2 * HEADS
    g = jnp.concatenate([g_t[:lo], g_t[hi:IN_WIDTH], g_t[lo:hi]], axis=0).reshape(N_CHIPS, IN_SHARD, g_t.shape[-1])
    return jnp.pad(g, ((0, 0), (0, IN_SHARD_PAD - IN_SHARD), (0, 0)))


PACK_LANES = 3 * GDN_WIDTH
PACKED = dict(mix_pre_norm=(0, 1, D_MODEL), mix_post_norm=(1, 1, D_MODEL), ffn_pre_norm=(2, 1, D_MODEL),
              ffn_post_norm=(3, 1, D_MODEL), a_log=(4, 1, HEADS), dt_bias=(5, 1, HEADS), loss=(6, 1, 1),
              gdn_norm=(7, 1, HEAD_DIM), conv_qkv=(8, GDN_CONV, 3 * GDN_WIDTH), conv_sc=(16, SC_CONV, SC_WIDTH),
              meta_tokens=(32, N_META, D_MODEL))
PACK_ROWS = 48
SHARDED_SMALL = ("conv_qkv", "conv_sc", "meta_tokens")


def _pack_small(values):
    names = list(PACKED)

    def body(*refs):
        out_ref = refs[-1]
        out_ref[...] = jnp.zeros_like(out_ref)
        for name, ref in zip(names, refs):
            row, rows, lanes = PACKED[name]
            out_ref[row:row + rows, :lanes] = ref[...]

    return pl.pallas_call(body, name="pack_small", out_shape=jax.ShapeDtypeStruct((PACK_ROWS, PACK_LANES), F32))(
        *[values[name] for name in names])


def _sum_devices(packed_all, chip):
    names = list(PACKED)

    def body(chip_ref, all_ref, *rest):
        shard_refs, out_refs = rest[:len(SHARDED_SMALL)], rest[len(SHARDED_SMALL):]

        def total(ref, rows, lanes):
            acc = ref[0, rows, lanes]
            for k in range(1, 8):
                acc = acc + ref[k, rows, lanes]
            return acc

        for name, out in zip(names, out_refs):
            row, rows, lanes = PACKED[name]
            if name in SHARDED_SMALL:
                out[...] = total(shard_refs[SHARDED_SMALL.index(name)], slice(0, rows), slice(None))
            else:
                out[...] = total(all_ref, slice(row, row + rows), slice(0, lanes))

    def shard_spec(name):
        row, rows, lanes = PACKED[name]
        height = max(rows, 8)
        assert row % height == 0
        return pl.BlockSpec((8, height, lanes // N_CHIPS), lambda i, chip_ref: (0, row // height, chip_ref[0]))

    def out_shape(name):
        _, rows, lanes = PACKED[name]
        return jax.ShapeDtypeStruct((rows, lanes // N_CHIPS if name in SHARDED_SMALL else lanes), F32)

    whole = lambda shape: pl.BlockSpec(shape, lambda i, chip_ref: (0,) * len(shape))
    outs = pl.pallas_call(
        body, name="sum_devices", out_shape=tuple(out_shape(n) for n in names),
        grid_spec=pltpu.PrefetchScalarGridSpec(
            num_scalar_prefetch=1, grid=(1,),
            in_specs=[whole(packed_all.shape)] + [shard_spec(n) for n in SHARDED_SMALL],
            out_specs=tuple(whole(out_shape(n).shape) for n in names)),
    )(chip, packed_all, *[packed_all] * len(SHARDED_SMALL))
    return dict(zip(names, outs))


def _hbm():
    return pl.BlockSpec(memory_space=pl.ANY)


def _place():
    x, y, c = lax.axis_index("x"), lax.axis_index("y"), lax.axis_index("c")
    chips = ((1 - x, y), (x, 1 - y), (1 - x, 1 - y))
    return x, y, c, chips


def _remote(src, dst, send_sems, recv_sems, k, to):
    return pltpu.make_async_remote_copy(src_ref=src, dst_ref=dst, send_sem=send_sems.at[k], recv_sem=recv_sems.at[k],
                                        device_id=to, device_id_type=MESH)


GATHER_SEMS = 7


def _gather_copies(w_refs, out_refs, send_sems, recv_sems):
    x, y, c, chips = _place()
    mine = 2 * x + y
    sibling = (x, y, 1 - c)
    copy = functools.partial(_remote, send_sems=send_sems, recv_sems=recv_sems)
    direct, landed, passing, from_sibling = [], [], [], []
    for i, (w, o) in enumerate(zip(w_refs, out_refs)):
        k = GATHER_SEMS * i
        direct.append(copy(w, o.at[mine], k=k, to=sibling))
        from_sibling.append(copy(w, o.at[mine], k=k, to=sibling))
        for j, (cx, cy) in enumerate(chips):
            theirs = 2 * cx + cy
            direct.append(copy(w.at[c], o.at[mine, c], k=k + 1 + j, to=(cx, cy, c)))
            landed.append(copy(w.at[c], o.at[theirs, c], k=k + 1 + j, to=sibling))
            passing.append(copy(o.at[theirs, c], o.at[theirs, c], k=k + 4 + j, to=sibling))
            from_sibling.append(copy(w.at[c], o.at[theirs, 1 - c], k=k + 4 + j, to=sibling))
    return direct, landed, passing, from_sibling


def _gather_finish(copies):
    direct, landed, passing, from_sibling = copies
    for arrival, forward in zip(landed, passing):
        arrival.wait_recv()
        forward.start()
    for arrival in from_sibling:
        arrival.wait_recv()
    for cp in direct + passing:
        cp.wait_send()


def _gather_weights(pieces, smalls):
    count, extra = len(pieces), len(smalls)
    total = count + extra

    def body(*refs):
        w_refs, s_refs = refs[:count], refs[count:total]
        out_refs, sall_refs = refs[total:total + count], refs[total + count:2 * total]
        send_sems, recv_sems, local_sems = refs[2 * total:]
        x, y, c, chips = _place()
        mine = 2 * x + y
        own = [pltpu.make_async_copy(s, sall.at[mine], local_sems.at[i]) for i, (s, sall) in enumerate(zip(s_refs, sall_refs))]
        small = [_remote(s, sall.at[mine], send_sems, recv_sems, GATHER_SEMS * count + 3 * i + j, (cx, cy, c))
                 for i, (s, sall) in enumerate(zip(s_refs, sall_refs)) for j, (cx, cy) in enumerate(chips)]
        copies = _gather_copies(w_refs, out_refs, send_sems, recv_sems)
        for cp in own + small + copies[0]:
            cp.start()
        _gather_finish(copies)
        for cp in small:
            cp.wait_recv()
        for cp in small:
            cp.wait_send()
        for cp in own:
            cp.wait()

    sems = GATHER_SEMS * count + 3 * extra
    return pl.pallas_call(
        body, name="gather_weights",
        out_shape=tuple(jax.ShapeDtypeStruct((N_CHIPS,) + p.shape, p.dtype) for p in list(pieces) + list(smalls)),
        in_specs=[_hbm()] * total, out_specs=(_hbm(),) * total,
        scratch_shapes=[pltpu.SemaphoreType.DMA((sems,)), pltpu.SemaphoreType.DMA((sems,)), pltpu.SemaphoreType.DMA((extra,))],
    )(*pieces, *smalls)


def _sibling_copies(g_refs, got_refs, send_sems, recv_sems):
    x, y, c, _ = _place()
    return [_remote(g.at[:, 1 - c], got, send_sems, recv_sems, i, (x, y, 1 - c)) for i, (g, got) in enumerate(zip(g_refs, got_refs))]


def _exchange_siblings(grads, small=None):
    count = len(grads)
    extra = 0 if small is None else 1

    def body(*refs):
        g_refs = refs[:count]
        got_refs = refs[count + extra:2 * count + extra]
        send_sems, recv_sems = refs[2 * (count + extra):2 * (count + extra) + 2]
        x, y, c, _ = _place()
        copies = _sibling_copies(g_refs, got_refs, send_sems, recv_sems)
        if small is not None:
            s_ref, sall_ref, local_sem = refs[count], refs[2 * count + 1], refs[-1]
            me = 4 * x + 2 * y + c
            own = pltpu.make_async_copy(s_ref, sall_ref.at[me], local_sem)
            own.start()
            for k in range(7):
                dx, dy, dc = ((k + 1) >> 2) & 1, ((k + 1) >> 1) & 1, (k + 1) & 1
                peer = (1 - x if dx else x, 1 - y if dy else y, 1 - c if dc else c)
                copies.append(_remote(s_ref, sall_ref.at[me], send_sems, recv_sems, count + k, peer))
        for cp in copies:
            cp.start()
        for cp in copies:
            cp.wait_recv()
        for cp in copies:
            cp.wait_send()
        if small is not None:
            own.wait()

    sems = count + 7 * extra
    return pl.pallas_call(
        body, name="exchange_siblings" + ("" if small is None else "_small"),
        out_shape=tuple(jax.ShapeDtypeStruct((g.shape[0],) + g.shape[2:], F32) for g in grads)
        + (() if small is None else (jax.ShapeDtypeStruct((8,) + small.shape, F32),)),
        in_specs=[_hbm()] * (count + extra), out_specs=(_hbm(),) * (count + extra),
        scratch_shapes=[pltpu.SemaphoreType.DMA((sems,)), pltpu.SemaphoreType.DMA((sems,))]
        + ([] if small is None else [pltpu.SemaphoreType.DMA]),
    )(*grads, *(() if small is None else (small,)))


def _chip_copies(p_refs, got_refs, send_sems, recv_sems):
    x, y, c, chips = _place()
    return [_remote(p.at[2 * cx + cy], got.at[j], send_sems, recv_sems, 3 * i + j, (cx, cy, c))
            for i, (p, got) in enumerate(zip(p_refs, got_refs)) for j, (cx, cy) in enumerate(chips)]


def _share_halves(halves):
    count = len(halves)

    def body(*refs):
        h_refs, full_refs = refs[:count], refs[count:2 * count]
        send_sems, recv_sems = refs[2 * count:]
        x, y, c, _ = _place()
        copies = [pltpu.make_async_remote_copy(src_ref=h.at[c], dst_ref=full.at[c], send_sem=send_sems.at[i],
                                               recv_sem=recv_sems.at[i], device_id=(x, y, 1 - c), device_id_type=MESH)
                  for i, (h, full) in enumerate(zip(h_refs, full_refs))]
        for cp in copies:
            cp.start()
        for cp in copies:
            cp.wait_recv()
        for cp in copies:
            cp.wait_send()

    return pl.pallas_call(
        body, name="share_halves", out_shape=tuple(jax.ShapeDtypeStruct(h.shape, h.dtype) for h in halves),
        in_specs=[_hbm()] * count, out_specs=(_hbm(),) * count, input_output_aliases={i: i for i in range(count)},
        scratch_shapes=[pltpu.SemaphoreType.DMA((count,)), pltpu.SemaphoreType.DMA((count,))],
    )(*halves)


def _add_sibling(grad, got, core, name):
    chips, _, rows, cols = grad.shape

    def body(core_ref, g_ref, r_ref, sum_ref, send_ref):
        s = g_ref[...] + r_ref[...]
        sum_ref[...] = s
        send_ref[...] = s.astype(send_ref.dtype)

    block = pl.BlockSpec((None, rows, cols), lambda p, core_ref: (p, 0, 0))
    return pl.pallas_call(
        body, name="add_sibling_" + name,
        out_shape=(jax.ShapeDtypeStruct((chips, rows, cols), F32), jax.ShapeDtypeStruct((chips, rows, cols), BF16)),
        grid_spec=pltpu.PrefetchScalarGridSpec(
            num_scalar_prefetch=1, grid=(chips,),
            in_specs=[pl.BlockSpec((None, None, rows, cols), lambda p, core_ref: (p, core_ref[0], 0, 0)), block],
            out_specs=(block, block)),
        compiler_params=_params("parallel"),
    )(core, grad, got)


def _add_chips(part, got, chip_core, name):
    _, rows, cols = part.shape
    tr = rows // 2 if rows % 32 == 0 else rows

    def body(place_ref, p_ref, r_ref, o_ref):
        o_ref[...] = ((p_ref[...] + r_ref[0].astype(F32)) + r_ref[1].astype(F32)) + r_ref[2].astype(F32)

    return pl.pallas_call(
        body, name="add_chips_" + name, out_shape=jax.ShapeDtypeStruct((2, rows, cols), F32),
        grid_spec=pltpu.PrefetchScalarGridSpec(
            num_scalar_prefetch=1, grid=(rows // tr,),
            in_specs=[pl.BlockSpec((None, tr, cols), lambda i, place_ref: (place_ref[0], i, 0)),
                      pl.BlockSpec((3, tr, cols), lambda i, place_ref: (0, i, 0))],
            out_specs=pl.BlockSpec((None, tr, cols), lambda i, place_ref: (place_ref[1], i, 0))),
        compiler_params=_params("parallel"),
    )(chip_core, part, got)


def _adamw(w, g, m, v, name):
    rows, cols = w.shape
    tr = _pick(rows, (3592, 256, 352, 176, 128, 64, 32, 16, 8))

    def body(w_ref, g_ref, m_ref, v_ref, d_ref, nm_ref, nv_ref):
        g = g_ref[...]
        m = ADAM_B1 * m_ref[...] + (1.0 - ADAM_B1) * g
        v = ADAM_B2 * v_ref[...] + (1.0 - ADAM_B2) * (g * g)
        m_hat = m / (1.0 - ADAM_B1 ** ADAM_STEP)
        v_hat = v / (1.0 - ADAM_B2 ** ADAM_STEP)
        d_ref[...] = -ADAM_LR * (m_hat / (jnp.sqrt(v_hat) + ADAM_EPS) + ADAM_WD * w_ref[...])
        nm_ref[...] = m
        nv_ref[...] = v

    block = pl.BlockSpec((tr, cols), lambda i: (i, 0))
    shape = jax.ShapeDtypeStruct((rows, cols), F32)
    return pl.pallas_call(
        body, name="adamw_" + name, out_shape=(shape, shape, shape), grid=(rows // tr,),
        in_specs=[block] * 4, out_specs=(block,) * 3, compiler_params=_params("parallel"),
    )(w, g, m, v)


WEIGHTS = ("meta_tokens", "mix_pre_norm", "mix_post_norm", "ffn_pre_norm", "ffn_post_norm", "w_in", "conv_qkv", "a_log",
           "dt_bias", "gdn_norm", "conv_sc", "w_out", "w_gate", "w_up", "w_down")


def kernel(x, meta_tokens, mix_pre_norm, mix_post_norm, ffn_pre_norm, ffn_post_norm, w_in, conv_qkv, a_log, dt_bias, gdn_norm, conv_sc, w_out, w_gate, w_up, w_down, loss_target, m_meta_tokens, m_mix_pre_norm, m_mix_post_norm, m_ffn_pre_norm, m_ffn_post_norm, m_w_in, m_conv_qkv, m_a_log, m_dt_bias, m_gdn_norm, m_conv_sc, m_w_out, m_w_gate, m_w_up, m_w_down, v_meta_tokens, v_mix_pre_norm, v_mix_post_norm, v_ffn_pre_norm, v_ffn_post_norm, v_w_in, v_conv_qkv, v_a_log, v_dt_bias, v_gdn_norm, v_conv_sc, v_w_out, v_w_gate, v_w_up, v_w_down):
    d = x.shape[-1]
    two_d = lambda a: a.reshape(a.shape[-2:])
    weights = dict(zip(WEIGHTS, (meta_tokens, mix_pre_norm, mix_post_norm, ffn_pre_norm, ffn_post_norm, w_in, conv_qkv, a_log,
                                 dt_bias, gdn_norm, conv_sc, w_out, w_gate, w_up, w_down)))
    m_in = dict(zip(WEIGHTS, (m_meta_tokens, m_mix_pre_norm, m_mix_post_norm, m_ffn_pre_norm, m_ffn_post_norm, m_w_in, m_conv_qkv,
                              m_a_log, m_dt_bias, m_gdn_norm, m_conv_sc, m_w_out, m_w_gate, m_w_up, m_w_down)))
    v_in = dict(zip(WEIGHTS, (v_meta_tokens, v_mix_pre_norm, v_mix_post_norm, v_ffn_pre_norm, v_ffn_post_norm, v_w_in, v_conv_qkv,
                              v_a_log, v_dt_bias, v_gdn_norm, v_conv_sc, v_w_out, v_w_gate, v_w_up, v_w_down)))
    core = lax.axis_index("c")
    chip = 2 * lax.axis_index("x") + lax.axis_index("y")
    core_arg = core.reshape(1).astype(jnp.int32)
    chip_core = jnp.stack([chip, core]).astype(jnp.int32)
    whole = lambda a: a.reshape(a.shape[:-3] + (2 * a.shape[-2], d))
    by_rows = lambda n, a: two_d(a).T if n in ("w_in", "w_gate", "w_up") else two_d(a)

    shard = {n: by_rows(n, weights[n]).astype(MXU_DTYPE) for n in MATRICES}
    shard["w_in"] = jnp.pad(shard["w_in"], ((0, IN_SHARD_PAD - IN_SHARD), (0, 0)))
    w_in_all, *small_all = _gather_weights([_halves(shard["w_in"])], [two_d(weights[n]) for n in SHARDED_SMALL])
    w_in_t = _in_to_kernel_order(whole(w_in_all))
    conv_qkv_full, conv_sc_full, meta_full = (jnp.concatenate([a[p] for p in range(N_CHIPS)], axis=1) for a in small_all)

    sq, grad_x, g, sums = _local_step(
        x, loss_target, meta_full, (mix_pre_norm, mix_post_norm, ffn_pre_norm, ffn_post_norm), w_in_t, conv_qkv_full, a_log,
        dt_bias, gdn_norm, conv_sc_full, [_halves(shard[n]) for n in LATER], core_arg)

    (packed_all,) = _exchange_siblings([], _pack_small(dict(g, loss=sq)))
    totals = [_add_chips(part, got, chip_core, n) for n, (part, got) in zip(MATRICES, sums)]
    grads = {n: whole(a) for n, a in zip(MATRICES, _share_halves(totals))}
    grads["w_in"] = grads["w_in"][:IN_SHARD]
    grads.update(_sum_devices(packed_all, chip.reshape(1).astype(jnp.int32)))
    loss = (0.5 / d) * grads.pop("loss")[0, 0]

    outs = [[], [], [], []]
    for n in WEIGHTS:
        shape = weights[n].shape
        delta, new_m, new_v = _adamw(by_rows(n, weights[n]), grads[n], by_rows(n, m_in[n]), by_rows(n, v_in[n]), n)
        for out, a in zip(outs, (grads[n], delta, new_m, new_v)):
            out.append((a.T if n in ("w_in", "w_gate", "w_up") else a).reshape(shape))
    return (loss, grad_x, *outs[0], *outs[1], *outs[2], *outs[3])
```

```python
import functools

import jax
import jax.numpy as jnp
from jax import lax
from jax.experimental import pallas as pl
from jax.experimental.pallas import tpu as pltpu

F32 = jnp.float32
BF16 = jnp.bfloat16
MXU_DTYPE = jnp.bfloat16
MESH = pl.DeviceIdType.MESH

D_MODEL = 1024
N_META = 16
HEADS = 4
HEAD_DIM = 128
GDN_WIDTH = HEADS * HEAD_DIM
GDN_CONV = 4
CHUNK = 64
SC_WIDTH = D_MODEL - GDN_WIDTH
SC_CONV = 3
D_FF = 2816
IN_WIDTH = 4 * GDN_WIDTH + 2 * HEADS + 3 * SC_WIDTH
IN_PAD = 3840
BA_COL = (4 * GDN_WIDTH + 3 * SC_WIDTH) // 128
EPS = 1e-6
LANES = 128
N_CHIPS = 4
VMEM_LIMIT = 48 * 2 ** 20

ADAM_LR = 0.001
ADAM_B1 = 0.9
ADAM_B2 = 0.999
ADAM_EPS = 1e-08
ADAM_WD = 0.01
ADAM_STEP = 10


def _pick(n, candidates):
    for c in candidates:
        if n % c == 0:
            return c
    return n


def _row_tile(n):
    return _pick(n, (352, 256, 176, 128, 64, 32, 16, 8))


def _params(*sem):
    return pltpu.CompilerParams(dimension_semantics=sem, vmem_limit_bytes=VMEM_LIMIT)


def _sigmoid(x):
    return 0.5 * jnp.tanh(0.5 * x) + 0.5


def _softplus(x):
    return jnp.maximum(x, 0.0) + jnp.log(1.0 + jnp.exp(-jnp.abs(x)))


def _dsilu(x, s):
    return s * (1.0 + x * (1.0 - s))


def _mm(a, b, mode, out_dtype, name, init=None, exchange=None):
    if mode == "tn":
        k_dim, m_dim = a.shape
    else:
        m_dim, k_dim = a.shape
    n_dim = b.shape[0] if mode == "nt" else b.shape[1]
    tm = _pick(m_dim, (1408, 1280, 1024, 512, 256, 128) if mode == "tn" else (1056, 1024, 704, 512, 256, 128))
    tn = _pick(n_dim, (1408, 1280, 1024, 768, 512, 256, 128))
    tk = _pick(k_dim, (1408, 1280, 1056, 1024, 512, 256, 128))
    nk = k_dim // tk
    if mode == "nn":
        a_spec = pl.BlockSpec((tm, tk), lambda i, j, k: (i, k))
        b_spec = pl.BlockSpec((tk, tn), lambda i, j, k: (k, j))
        dims = (((1,), (0,)), ((), ()))
    elif mode == "nt":
        a_spec = pl.BlockSpec((tm, tk), lambda i, j, k: (i, k))
        b_spec = pl.BlockSpec((tn, tk), lambda i, j, k: (j, k))
        dims = (((1,), (1,)), ((), ()))
    else:
        a_spec = pl.BlockSpec((tk, tm), lambda i, j, k: (k, i))
        b_spec = pl.BlockSpec((tk, tn), lambda i, j, k: (k, j))
        dims = (((0,), (0,)), ((), ()))

    out_spec = pl.BlockSpec((tm, tn), lambda i, j, k: (i, j))
    grid = (m_dim // tm, n_dim // tn, nk)
    parts = () if exchange is None else tuple(exchange)
    count = len(parts)
    first_in = 2 if init is None else 3

    def body(a_ref, b_ref, *rest):
        o_ref = rest[first_in - 2 + count]
        acc_ref = rest[first_in - 1 + 2 * count]
        k = pl.program_id(2)
        step = (pl.program_id(0) * grid[1] + pl.program_id(1)) * nk + k
        if count:
            copies = _chip_copies(rest[first_in - 2:first_in - 2 + count], rest[first_in - 1 + count:first_in - 1 + 2 * count],
                                  *rest[first_in + 2 * count:])

            @pl.when(step == 0)
            def _():
                for cp in copies:
                    cp.start()

        p = lax.dot_general(a_ref[...], b_ref[...], dims, preferred_element_type=F32)

        @pl.when(k == 0)
        def _():
            acc_ref[...] = p if init is None else rest[0][...] + p

        @pl.when(k > 0)
        def _():
            acc_ref[...] += p

        @pl.when(k == nk - 1)
        def _():
            o_ref[...] = acc_ref[...].astype(out_dtype)

        if count:
            @pl.when(step == grid[0] * grid[1] * nk - 1)
            def _():
                for cp in copies:
                    cp.wait_recv()
                for cp in copies:
                    cp.wait_send()

    out = pl.pallas_call(
        body, name=name,
        out_shape=(jax.ShapeDtypeStruct((m_dim, n_dim), out_dtype),)
        + tuple(jax.ShapeDtypeStruct((3,) + p.shape[1:], p.dtype) for p in parts),
        grid=grid,
        in_specs=[a_spec, b_spec] + ([] if init is None else [out_spec]) + [_hbm()] * count,
        out_specs=(out_spec,) + (_hbm(),) * count,
        scratch_shapes=[pltpu.VMEM((tm, tn), F32)]
        + ([pltpu.SemaphoreType.DMA((3 * count,)), pltpu.SemaphoreType.DMA((3 * count,))] if count else []),
        compiler_params=_params(*(("arbitrary",) * 3 if count else ("parallel", "parallel", "arbitrary"))),
    )(a, b, *(() if init is None else (init,)), *parts)
    return out[0] if not count else out


def _rms_apply(x, w):
    r = lax.rsqrt(jnp.mean(x * x, axis=-1, keepdims=True) + EPS)
    return x * r * w


def _rms_bwd(x, w, dy):
    r = lax.rsqrt(jnp.mean(x * x, axis=-1, keepdims=True) + EPS)
    xh = x * r
    dyw = dy * w
    dx = r * (dyw - xh * jnp.mean(dyw * xh, axis=-1, keepdims=True))
    return dx, jnp.sum(dy * xh, axis=0, keepdims=True)


def _accumulate(ref, first, value):
    @pl.when(first)
    def _():
        ref[...] = value

    @pl.when(jnp.logical_not(first))
    def _():
        ref[...] += value


def _rows(tr, width):
    return pl.BlockSpec((tr, width), lambda i: (i, 0))


def _vec(width):
    return pl.BlockSpec((1, width), lambda i: (0, 0))


def _rms_fwd(h, w, name):
    n, d = h.shape
    tr = _row_tile(n)

    def body(h_ref, w_ref, u_ref):
        u_ref[...] = _rms_apply(h_ref[...], w_ref[...]).astype(u_ref.dtype)

    return pl.pallas_call(
        body, name=name, out_shape=jax.ShapeDtypeStruct((n, d), MXU_DTYPE), grid=(n // tr,),
        in_specs=[_rows(tr, d), _vec(d)], out_specs=_rows(tr, d), compiler_params=_params("parallel"),
    )(h, w)


def _mix_residual(h0, mix, w_post, w_pre):
    n, d = h0.shape
    tr = _row_tile(n)

    def body(h0_ref, mix_ref, wpost_ref, wpre_ref, h1_ref, u2_ref):
        h1 = h0_ref[...] + _rms_apply(mix_ref[...], wpost_ref[...])
        h1_ref[...] = h1
        u2_ref[...] = _rms_apply(h1, wpre_ref[...]).astype(u2_ref.dtype)

    return pl.pallas_call(
        body, name="mix_residual",
        out_shape=(jax.ShapeDtypeStruct((n, d), F32), jax.ShapeDtypeStruct((n, d), MXU_DTYPE)), grid=(n // tr,),
        in_specs=[_rows(tr, d), _rows(tr, d), _vec(d), _vec(d)], out_specs=(_rows(tr, d), _rows(tr, d)),
        compiler_params=_params("parallel"),
    )(h0, mix, w_post, w_pre)


NT_DIMS = (((1,), (1,)), ((), ()))


def _ffn_tiles(n):
    return _pick(n, (704, 512, 256, 128)), _pick(D_FF, (1408, 256, 128))


def _swiglu_fwd(u, w_gate_t, w_up_t):
    n, d = u.shape
    tm, tn = _ffn_tiles(n)

    def body(u_ref, wg_ref, wu_ref, g_ref, up_ref, act_ref):
        a = u_ref[...]
        g = lax.dot_general(a, wg_ref[...], NT_DIMS, preferred_element_type=F32)
        up = lax.dot_general(a, wu_ref[...], NT_DIMS, preferred_element_type=F32)
        g_ref[...] = g
        up_ref[...] = up
        act_ref[...] = (g * _sigmoid(g) * up).astype(act_ref.dtype)

    tile = pl.BlockSpec((tm, tn), lambda j, i: (i, j))
    weight = pl.BlockSpec((tn, d), lambda j, i: (j, 0))
    wide = jax.ShapeDtypeStruct((n, D_FF), F32)
    return pl.pallas_call(
        body, name="swiglu_fwd", out_shape=(wide, wide, jax.ShapeDtypeStruct((n, D_FF), MXU_DTYPE)),
        grid=(D_FF // tn, n // tm),
        in_specs=[pl.BlockSpec((tm, d), lambda j, i: (i, 0)), weight, weight], out_specs=(tile, tile, tile),
        compiler_params=_params("parallel", "parallel"),
    )(u, w_gate_t, w_up_t)


def _swiglu_bwd(dffn, w_down, gate, up):
    n, d = dffn.shape
    tm, tn = _ffn_tiles(n)

    def body(dy_ref, w_ref, g_ref, u_ref, dg_ref, du_ref):
        da = lax.dot_general(dy_ref[...], w_ref[...], NT_DIMS, preferred_element_type=F32)
        g = g_ref[...]
        s = _sigmoid(g)
        dg_ref[...] = (da * u_ref[...] * _dsilu(g, s)).astype(dg_ref.dtype)
        du_ref[...] = (da * g * s).astype(du_ref.dtype)

    tile = pl.BlockSpec((tm, tn), lambda j, i: (i, j))
    shape = jax.ShapeDtypeStruct((n, D_FF), MXU_DTYPE)
    return pl.pallas_call(
        body, name="swiglu_bwd", out_shape=(shape, shape), grid=(D_FF // tn, n // tm),
        in_specs=[pl.BlockSpec((tm, d), lambda j, i: (i, 0)), pl.BlockSpec((tn, d), lambda j, i: (j, 0)), tile, tile],
        out_specs=(tile, tile), compiler_params=_params("parallel", "parallel"),
    )(dffn, w_down, gate, up)


def _loss_head(h1, ffn, w_post, target, rows_per_seq, x_offset):
    n, d = h1.shape
    tr = _row_tile(rows_per_seq)
    tiles_per_seq = rows_per_seq // tr

    def body(h1_ref, ffn_ref, w_ref, t_ref, dh2_ref, dffn_ref, dw_ref, sq_ref):
        i = pl.program_id(0)
        w = w_ref[...]
        f = ffn_ref[...]
        r = lax.rsqrt(jnp.mean(f * f, axis=-1, keepdims=True) + EPS)
        fh = f * r
        row = lax.rem(i, tiles_per_seq) * tr + lax.broadcasted_iota(jnp.int32, (tr, 1), 0)
        err = jnp.where(row >= x_offset, h1_ref[...] + fh * w - t_ref[...], 0.0)
        dh2 = err * (1.0 / d)
        dh2_ref[...] = dh2
        dyw = dh2 * w
        dffn_ref[...] = (r * (dyw - fh * jnp.mean(dyw * fh, axis=-1, keepdims=True))).astype(dffn_ref.dtype)
        _accumulate(dw_ref, i == 0, jnp.sum(dh2 * fh, axis=0, keepdims=True))
        _accumulate(sq_ref, i == 0, jnp.sum(jnp.sum(err * err, axis=1, keepdims=True), axis=0, keepdims=True))

    return pl.pallas_call(
        body, name="loss_head",
        out_shape=(jax.ShapeDtypeStruct((n, d), F32), jax.ShapeDtypeStruct((n, d), MXU_DTYPE),
                   jax.ShapeDtypeStruct((1, d), F32), jax.ShapeDtypeStruct((1, 1), F32)),
        grid=(n // tr,),
        in_specs=[_rows(tr, d), _rows(tr, d), _vec(d), _rows(tr, d)],
        out_specs=(_rows(tr, d), _rows(tr, d), _vec(d), _vec(1)),
        compiler_params=_params("arbitrary"),
    )(h1, ffn, w_post, target)


def _mid_bwd(h1, mix, w_mix_post, w_ffn_pre, dh2, du2, grads):
    n, d = h1.shape
    tr = _row_tile(n)
    count = len(grads)

    def body(h1_ref, mix_ref, wpost_ref, wpre_ref, dh2_ref, du2_ref, *rest):
        g_refs, (dh1_ref, dmix_ref, dwpre_ref, dwpost_ref), got_refs = rest[:count], rest[count:count + 4], rest[count + 4:2 * count + 4]
        exchange = _sibling_copies(g_refs, got_refs, *rest[2 * count + 4:])
        i = pl.program_id(0)

        @pl.when(i == 0)
        def _():
            for cp in exchange:
                cp.start()

        dx, dwpre = _rms_bwd(h1_ref[...], wpre_ref[...], du2_ref[...])
        dh1 = dh2_ref[...] + dx
        dh1_ref[...] = dh1
        dmix, dwpost = _rms_bwd(mix_ref[...], wpost_ref[...], dh1)
        dmix_ref[...] = dmix.astype(dmix_ref.dtype)
        _accumulate(dwpre_ref, i == 0, dwpre)
        _accumulate(dwpost_ref, i == 0, dwpost)

        @pl.when(i == n // tr - 1)
        def _():
            for cp in exchange:
                cp.wait_recv()
            for cp in exchange:
                cp.wait_send()

    dh1, dmix, dwpre, dwpost, *got = pl.pallas_call(
        body, name="mid_bwd",
        out_shape=(jax.ShapeDtypeStruct((n, d), F32), jax.ShapeDtypeStruct((n, d), MXU_DTYPE),
                   jax.ShapeDtypeStruct((1, d), F32), jax.ShapeDtypeStruct((1, d), F32))
        + tuple(jax.ShapeDtypeStruct((g.shape[0],) + g.shape[2:], F32) for g in grads),
        grid=(n // tr,),
        in_specs=[_rows(tr, d), _rows(tr, d), _vec(d), _vec(d), _rows(tr, d), _rows(tr, d)] + [_hbm()] * count,
        out_specs=(_rows(tr, d), _rows(tr, d), _vec(d), _vec(d)) + (_hbm(),) * count,
        scratch_shapes=[pltpu.SemaphoreType.DMA((count,)), pltpu.SemaphoreType.DMA((count,))],
        compiler_params=_params("arbitrary"),
    )(h1, mix, w_mix_post, w_ffn_pre, dh2, du2, *grads)
    return dh1, dmix, dwpre, dwpost, got


def _in_bwd(h0, w_pre, dh1, du1):
    n, d = h0.shape
    tr = _row_tile(n)

    def body(h0_ref, w_ref, dh1_ref, du1_ref, dh0_ref, dw_ref):
        dx, dw = _rms_bwd(h0_ref[...], w_ref[...], du1_ref[...])
        dh0_ref[...] = dh1_ref[...] + dx
        _accumulate(dw_ref, pl.program_id(0) == 0, dw)

    return pl.pallas_call(
        body, name="in_bwd",
        out_shape=(jax.ShapeDtypeStruct((n, d), F32), jax.ShapeDtypeStruct((1, d), F32)), grid=(n // tr,),
        in_specs=[_rows(tr, d), _vec(d), _rows(tr, d), _rows(tr, d)], out_specs=(_rows(tr, d), _vec(d)),
        compiler_params=_params("arbitrary"),
    )(h0, w_pre, dh1, du1)


def _lane_is(lo, hi):
    lane = lax.broadcasted_iota(jnp.int32, (1, LANES), 1)
    return jnp.logical_and(lane >= lo, lane < hi)


def _gates_fwd(proj, a_log_l, dt_bias_l, rows_per_seq, pad_rows):
    n = proj.shape[0]
    tr = _row_tile(rows_per_seq)
    tiles_per_seq = rows_per_seq // tr

    def body(p_ref, a_ref, dt_ref, o_ref):
        x = p_ref[...]
        row = lax.rem(pl.program_id(0), tiles_per_seq) * tr + lax.broadcasted_iota(jnp.int32, (tr, 1), 0)
        g = -jnp.exp(a_ref[...]) * _softplus(x + dt_ref[...])
        val = jnp.where(_lane_is(0, HEADS), _sigmoid(x), jnp.where(_lane_is(HEADS, 2 * HEADS), g, 0.0))
        o_ref[...] = jnp.where(row >= pad_rows, val, 0.0)

    return pl.pallas_call(
        body, name="gates_fwd", out_shape=jax.ShapeDtypeStruct((n, LANES), F32), grid=(n // tr,),
        in_specs=[pl.BlockSpec((tr, LANES), lambda i: (i, BA_COL)), _vec(LANES), _vec(LANES)],
        out_specs=_rows(tr, LANES), compiler_params=_params("parallel"),
    )(proj, a_log_l, dt_bias_l)


def _gates_bwd(proj, dbg, a_log_l, dt_bias_l, rows_per_seq, pad_rows):
    n = proj.shape[0]
    tr = _row_tile(rows_per_seq)
    tiles_per_seq = rows_per_seq // tr

    def body(p_ref, d_ref, a_ref, dt_ref, dx_ref, da_ref, ddt_ref):
        i = pl.program_id(0)
        x = p_ref[...]
        d = d_ref[...]
        row = lax.rem(i, tiles_per_seq) * tr + lax.broadcasted_iota(jnp.int32, (tr, 1), 0)
        live = row >= pad_rows
        beta = _sigmoid(x)
        ea = jnp.exp(a_ref[...])
        xa = x + dt_ref[...]
        g = -ea * _softplus(xa)
        is_g = _lane_is(HEADS, 2 * HEADS)
        d_alogit = jnp.where(jnp.logical_and(live, is_g), d * (-ea) * _sigmoid(xa), 0.0)
        d_blogit = jnp.where(jnp.logical_and(live, _lane_is(0, HEADS)), d * beta * (1.0 - beta), 0.0)
        dx_ref[:, :LANES] = (d_alogit + d_blogit).astype(dx_ref.dtype)
        dx_ref[:, LANES:] = jnp.zeros((tr, LANES), dx_ref.dtype)
        _accumulate(da_ref, i == 0, jnp.sum(jnp.where(jnp.logical_and(live, is_g), d * g, 0.0), axis=0, keepdims=True))
        _accumulate(ddt_ref, i == 0, jnp.sum(d_alogit, axis=0, keepdims=True))

    return pl.pallas_call(
        body, name="gates_bwd",
        out_shape=(jax.ShapeDtypeStruct((n, 2 * LANES), MXU_DTYPE), jax.ShapeDtypeStruct((1, LANES), F32),
                   jax.ShapeDtypeStruct((1, LANES), F32)),
        grid=(n // tr,),
        in_specs=[pl.BlockSpec((tr, LANES), lambda i: (i, BA_COL)), _rows(tr, LANES), _vec(LANES), _vec(LANES)],
        out_specs=(_rows(tr, 2 * LANES), _vec(LANES), _vec(LANES)),
        compiler_params=_params("arbitrary"),
    )(proj, dbg, a_log_l, dt_bias_l)


HALO = 8


def _halo_scratch(rs):
    return pltpu.VMEM((rs + 2 * HALO, LANES), F32)


def _stage(ref, x):
    rs = x.shape[0]
    ref[0:HALO, :] = jnp.zeros((HALO, LANES), F32)
    ref[HALO + rs:, :] = jnp.zeros((HALO, LANES), F32)
    ref[HALO:HALO + rs, :] = x


def _shifted(ref, k, rs):
    return ref[pl.ds(HALO - k, rs), :]


def _causal_conv(x, x_staged, w, width):
    acc = w[width - 1:width, :] * x
    for i in range(width - 1):
        acc = acc + w[i:i + 1, :] * _shifted(x_staged, width - 1 - i, x.shape[0])
    return acc


def _anti_causal_conv(dy, dy_staged, w, width):
    acc = w[width - 1:width, :] * dy
    for i in range(width - 1):
        acc = acc + w[i:i + 1, :] * _shifted(dy_staged, -(width - 1 - i), dy.shape[0])
    return acc


def _conv_weight_grad(dy, x, x_staged, width):
    taps = [_shifted(x_staged, width - 1 - i, x.shape[0]) for i in range(width - 1)] + [x]
    return jnp.concatenate([jnp.sum(dy * tap, axis=0, keepdims=True) for tap in taps], axis=0)


def _seq_cols(rs, col0, heads):
    return pl.BlockSpec((rs, heads * LANES), lambda j, b: (b, col0 // heads + j))


def _tap_cols(width, col0, heads):
    return pl.BlockSpec((width, heads * LANES), lambda j, b: (0, col0 // heads + j))


def _lanes_of(h):
    return slice(h * LANES, (h + 1) * LANES)


def _qkv_fwd(proj, conv_w, kind, rs):
    n = proj.shape[0]
    col0 = {"q": 0, "k": HEADS, "v": 2 * HEADS}[kind]
    hb = HEADS

    def body(p_ref, w_ref, o_ref, staged):
        for h in range(hb):
            pre = p_ref[:, _lanes_of(h)]
            _stage(staged, pre)
            c = _causal_conv(pre, staged, w_ref[:, _lanes_of(h)], GDN_CONV)
            s = c * _sigmoid(c)
            if kind != "v":
                s = s * lax.rsqrt(jnp.sum(s * s, axis=-1, keepdims=True) + EPS)
            if kind == "q":
                s = s * (HEAD_DIM ** -0.5)
            o_ref[:, _lanes_of(h)] = s

    return pl.pallas_call(
        body, name="qkv_fwd_" + kind, out_shape=jax.ShapeDtypeStruct((n, GDN_WIDTH), F32), grid=(HEADS // hb, n // rs),
        in_specs=[_seq_cols(rs, col0, hb), _tap_cols(GDN_CONV, col0, hb)],
        out_specs=_seq_cols(rs, 0, hb), scratch_shapes=[_halo_scratch(rs)], compiler_params=_params("parallel", "parallel"),
    )(proj, conv_w)


def _qkv_bwd(dy, proj, conv_w, kind, rs):
    n = proj.shape[0]
    col0 = {"q": 0, "k": HEADS, "v": 2 * HEADS}[kind]
    hb = HEADS

    def body(dy_ref, p_ref, w_ref, dp_ref, dw_ref, pre_staged, dc_staged):
        for h in range(hb):
            lanes = _lanes_of(h)
            pre = p_ref[:, lanes]
            w = w_ref[:, lanes]
            _stage(pre_staged, pre)
            c = _causal_conv(pre, pre_staged, w, GDN_CONV)
            sg = _sigmoid(c)
            s = c * sg
            ds = dy_ref[:, lanes]
            if kind == "q":
                ds = ds * (HEAD_DIM ** -0.5)
            if kind != "v":
                r = lax.rsqrt(jnp.sum(s * s, axis=-1, keepdims=True) + EPS)
                sh = s * r
                ds = r * (ds - sh * jnp.sum(ds * sh, axis=-1, keepdims=True))
            dc = ds * _dsilu(c, sg)
            _stage(dc_staged, dc)
            dp_ref[:, lanes] = _anti_causal_conv(dc, dc_staged, w, GDN_CONV).astype(dp_ref.dtype)
            _accumulate(dw_ref.at[:, lanes], pl.program_id(1) == 0, _conv_weight_grad(dc, pre, pre_staged, GDN_CONV))

    return pl.pallas_call(
        body, name="qkv_bwd_" + kind,
        out_shape=(jax.ShapeDtypeStruct((n, GDN_WIDTH), MXU_DTYPE), jax.ShapeDtypeStruct((GDN_CONV, GDN_WIDTH), F32)),
        grid=(HEADS // hb, n // rs),
        in_specs=[_seq_cols(rs, 0, hb), _seq_cols(rs, col0, hb), _tap_cols(GDN_CONV, col0, hb)],
        out_specs=(_seq_cols(rs, 0, hb), _tap_cols(GDN_CONV, 0, hb)),
        scratch_shapes=[_halo_scratch(rs), _halo_scratch(rs)],
        compiler_params=_params("parallel", "arbitrary"),
    )(dy, proj, conv_w)


SC_COL = 4 * HEADS


def _sc_fwd(proj, conv_w, rs):
    n = proj.shape[0]

    hb = 2

    def body(x_ref, b_ref, c_ref, w_ref, y_ref, staged):
        for h in range(hb):
            lanes = _lanes_of(h)
            u = c_ref[:, lanes] * x_ref[:, lanes]
            _stage(staged, u)
            y_ref[:, lanes] = (b_ref[:, lanes] * _causal_conv(u, staged, w_ref[:, lanes], SC_CONV)).astype(y_ref.dtype)

    return pl.pallas_call(
        body, name="sc_fwd", out_shape=jax.ShapeDtypeStruct((n, SC_WIDTH), MXU_DTYPE), grid=(HEADS // hb, n // rs),
        in_specs=[_seq_cols(rs, SC_COL, hb), _seq_cols(rs, SC_COL + 4, hb), _seq_cols(rs, SC_COL + 8, hb),
                  _tap_cols(SC_CONV, 0, hb)],
        out_specs=_seq_cols(rs, 0, hb), scratch_shapes=[_halo_scratch(rs)], compiler_params=_params("parallel", "parallel"),
    )(proj, proj, proj, conv_w)


def _sc_bwd(dcat, proj, conv_w, rs):
    n = proj.shape[0]
    hb = 2

    def body(dy_ref, x_ref, b_ref, c_ref, w_ref, dx_ref, db_ref, dc_ref, dw_ref, u_staged, dcv_staged):
        for h in range(hb):
            lanes = _lanes_of(h)
            w = w_ref[:, lanes]
            x = x_ref[:, lanes]
            cc = c_ref[:, lanes]
            u = cc * x
            _stage(u_staged, u)
            dy = dy_ref[:, lanes]
            db_ref[:, lanes] = (dy * _causal_conv(u, u_staged, w, SC_CONV)).astype(db_ref.dtype)
            dcv = dy * b_ref[:, lanes]
            _stage(dcv_staged, dcv)
            du = _anti_causal_conv(dcv, dcv_staged, w, SC_CONV)
            dx_ref[:, lanes] = (du * cc).astype(dx_ref.dtype)
            dc_ref[:, lanes] = (du * x).astype(dc_ref.dtype)
            _accumulate(dw_ref.at[:, lanes], pl.program_id(1) == 0, _conv_weight_grad(dcv, u, u_staged, SC_CONV))

    piece = jax.ShapeDtypeStruct((n, SC_WIDTH), MXU_DTYPE)
    return pl.pallas_call(
        body, name="sc_bwd", out_shape=(piece, piece, piece, jax.ShapeDtypeStruct((SC_CONV, SC_WIDTH), F32)),
        grid=(HEADS // hb, n // rs),
        in_specs=[_seq_cols(rs, HEADS, hb), _seq_cols(rs, SC_COL, hb), _seq_cols(rs, SC_COL + 4, hb),
                  _seq_cols(rs, SC_COL + 8, hb), _tap_cols(SC_CONV, 0, hb)],
        out_specs=(_seq_cols(rs, 0, hb), _seq_cols(rs, 0, hb), _seq_cols(rs, 0, hb), _tap_cols(SC_CONV, 0, hb)),
        scratch_shapes=[_halo_scratch(rs), _halo_scratch(rs)],
        compiler_params=_params("parallel", "arbitrary"),
    )(dcat, proj, proj, proj, conv_w)


Z_COL = 3 * HEADS


def _gate_fwd(o, proj, gdn_norm, rs):
    n = proj.shape[0]

    hb = HEADS

    def body(o_ref, z_ref, w_ref, y_ref):
        for h in range(hb):
            lanes = _lanes_of(h)
            z = z_ref[:, lanes]
            y_ref[:, lanes] = (_rms_apply(o_ref[:, lanes], w_ref[...]) * z * _sigmoid(z)).astype(y_ref.dtype)

    return pl.pallas_call(
        body, name="gate_fwd", out_shape=jax.ShapeDtypeStruct((n, GDN_WIDTH), MXU_DTYPE), grid=(HEADS // hb, n // rs),
        in_specs=[_seq_cols(rs, 0, hb), _seq_cols(rs, Z_COL, hb), pl.BlockSpec((1, LANES), lambda j, b: (0, 0))],
        out_specs=_seq_cols(rs, 0, hb), compiler_params=_params("parallel", "parallel"),
    )(o, proj, gdn_norm)


def _gate_bwd(dcat, o, proj, gdn_norm, rs):
    n = proj.shape[0]
    hb = 2

    def body(dy_ref, o_ref, z_ref, w_ref, do_ref, dz_ref, dw_ref):
        w = w_ref[...]
        dw_step = jnp.zeros((1, LANES), F32)
        for h in range(hb):
            lanes = _lanes_of(h)
            z = z_ref[:, lanes]
            o = o_ref[:, lanes]
            dy = dy_ref[:, lanes]
            s = _sigmoid(z)
            dz_ref[:, lanes] = (dy * _rms_apply(o, w) * _dsilu(z, s)).astype(dz_ref.dtype)
            do, dw = _rms_bwd(o, w, dy * z * s)
            do_ref[:, lanes] = do
            dw_step = dw_step + dw
        _accumulate(dw_ref, jnp.logical_and(pl.program_id(0) == 0, pl.program_id(1) == 0), dw_step)

    return pl.pallas_call(
        body, name="gate_bwd",
        out_shape=(jax.ShapeDtypeStruct((n, GDN_WIDTH), F32), jax.ShapeDtypeStruct((n, GDN_WIDTH), MXU_DTYPE),
                   jax.ShapeDtypeStruct((1, LANES), F32)),
        grid=(HEADS // hb, n // rs),
        in_specs=[_seq_cols(rs, 0, hb), _seq_cols(rs, 0, hb), _seq_cols(rs, Z_COL, hb), pl.BlockSpec((1, LANES), lambda j, b: (0, 0))],
        out_specs=(_seq_cols(rs, 0, hb), _seq_cols(rs, 0, hb), pl.BlockSpec((1, LANES), lambda j, b: (0, 0))),
        compiler_params=_params("arbitrary", "arbitrary"),
    )(dcat, o, proj, gdn_norm)


def _dot(a, b):
    return jnp.dot(a.astype(MXU_DTYPE), b.astype(MXU_DTYPE), preferred_element_type=F32)


def _dot_nt(a, b):
    return lax.dot_general(a.astype(MXU_DTYPE), b.astype(MXU_DTYPE), (((1,), (1,)), ((), ())),
                           preferred_element_type=F32)


def _dot_tn(a, b):
    return lax.dot_general(a.astype(MXU_DTYPE), b.astype(MXU_DTYPE), (((0,), (0,)), ((), ())),
                           preferred_element_type=F32)


def _split(x):
    hi = x.astype(MXU_DTYPE)
    return hi, (x - hi.astype(F32)).astype(MXU_DTYPE)


def _dot_split(a, b):
    mm = functools.partial(jnp.dot, preferred_element_type=F32)
    return mm(a[0], b[0]) + (mm(a[0], b[1]) + mm(a[1], b[0]))


def _unit_lower_inverses(mats, eye):
    inv = [eye - a for a in mats]
    power = [_split(a) for a in mats]
    span = 2
    while span < CHUNK:
        power = [_split(_dot_split(p, p)) for p in power]
        inv = [i + _dot_split(_split(i), p) for i, p in zip(inv, power)]
        span *= 2
    return inv


def _chunk_masks():
    ii = lax.broadcasted_iota(jnp.int32, (CHUNK, CHUNK), 0)
    jj = lax.broadcasted_iota(jnp.int32, (CHUNK, CHUNK), 1)
    return ii, jj


def _chunk_decay(g_col, ii, jj):
    incl = ii >= jj
    g_row = jnp.sum(jnp.where(ii == jj, g_col, 0.0), axis=0, keepdims=True)
    gc_col = jnp.sum(jnp.where(incl, g_row, 0.0), axis=1, keepdims=True)
    gc_row = jnp.sum(jnp.where(ii <= jj, g_col, 0.0), axis=0, keepdims=True)
    g_total = jnp.sum(g_row, axis=1, keepdims=True)
    decay = jnp.where(incl, jnp.exp(jnp.where(incl, gc_col - gc_row, 0.0)), 0.0)
    return gc_col, g_total, decay


def _gdn_segments(rs, candidates):
    chunks = rs // CHUNK
    seg_chunks = _pick(chunks, candidates)
    return chunks, seg_chunks, chunks // seg_chunks


def _head_lanes(h):
    return slice(h * HEAD_DIM, (h + 1) * HEAD_DIM)


def _gdn_fwd(q, k, v, bg, rs, pieces):
    n = q.shape[0]
    batch = n // rs
    chunks, seg_chunks, segs = _gdn_segments(rs, (11, 8, 4, 2))
    seg_rows = seg_chunks * CHUNK
    chains = [(b, h) for b in range(batch) for h in range(HEADS)]
    each = lambda f, *lists: [f(*args) for args in zip(*lists)]
    count = len(pieces)

    def body(q_ref, k_ref, v_ref, bg_ref, *rest):
        w_refs, (o_ref, s_ref, t_ref), out_refs = rest[:count], rest[count:count + 3], rest[count + 3:2 * count + 3]
        state_ref, send_sems, recv_sems = rest[2 * count + 3:]
        gather = _gather_copies(w_refs, out_refs, send_sems, recv_sems)

        @pl.when(pl.program_id(0) == 0)
        def _():
            state_ref[...] = jnp.zeros_like(state_ref)
            for cp in gather[0]:
                cp.start()

        ii, jj = _chunk_masks()
        incl = ii >= jj
        eye = (ii == jj).astype(F32)

        def chunk(c, carry):
            rows = pl.ds(pl.multiple_of(c * CHUNK, CHUNK), CHUNK)
            bgc = [bg_ref[b, rows, :] for b in range(batch)]
            qc = [q_ref[b, rows, _head_lanes(h)] for b, h in chains]
            kc = [k_ref[b, rows, _head_lanes(h)] for b, h in chains]
            vc = [v_ref[b, rows, _head_lanes(h)] for b, h in chains]
            beta = [bgc[b][:, h:h + 1] for b, h in chains]
            state = [state_ref[b, h] for b, h in chains]
            dec = [_chunk_decay(bgc[b][:, HEADS + h:HEADS + h + 1], ii, jj) for b, h in chains]
            gc_col, g_total, decay = ([d[i] for d in dec] for i in range(3))
            kb = each(lambda x, y: x * y, kc, beta)
            a = each(lambda x, y, d: jnp.where(ii > jj, _dot_nt(x, y) * d, 0.0), kb, kc, decay)
            t_inv = _unit_lower_inverses(a, eye)
            eg = [jnp.exp(g) for g in gc_col]
            u = each(lambda t, x, y: _dot(t, x * y), t_inv, vc, beta)
            w = each(lambda t, x, e: _dot(t, x * e), t_inv, kb, eg)
            qk = each(lambda x, y, d: jnp.where(incl, _dot_nt(x, y) * d, 0.0), qc, kc, decay)
            v_new = each(lambda x, y, s: x - _dot(y, s), u, w, state)
            o = each(lambda x, e, s, m, vn: _dot(x * e, s) + _dot(m, vn), qc, eg, state, qk, v_new)
            new_state = each(lambda s, gt, x, g, vn: s * jnp.exp(gt) + _dot_tn(x * jnp.exp(gt - g), vn),
                             state, g_total, kc, gc_col, v_new)
            for i, (b, h) in enumerate(chains):
                s_ref[b, h, c] = state[i]
                t_ref[b, h, c] = t_inv[i]
                o_ref[b, rows, _head_lanes(h)] = o[i]
                state_ref[b, h] = new_state[i]
            return carry

        lax.fori_loop(0, seg_chunks, chunk, 0)

        @pl.when(pl.program_id(0) == segs - 1)
        def _():
            _gather_finish(gather)

    rows_spec = lambda width: pl.BlockSpec((batch, seg_rows, width), lambda s: (0, s, 0))
    per_chunk = lambda r, c: pl.BlockSpec((batch, HEADS, seg_chunks, r, c), lambda s: (0, 0, s, 0, 0))
    as_seqs = lambda a: a.reshape(batch, rs, a.shape[-1])
    sems = GATHER_SEMS * count
    o, states, t_invs, *gathered = pl.pallas_call(
        body, name="gdn_fwd",
        out_shape=(jax.ShapeDtypeStruct((batch, rs, GDN_WIDTH), F32),
                   jax.ShapeDtypeStruct((batch, HEADS, chunks, HEAD_DIM, HEAD_DIM), F32),
                   jax.ShapeDtypeStruct((batch, HEADS, chunks, CHUNK, CHUNK), F32))
        + tuple(jax.ShapeDtypeStruct((N_CHIPS,) + p.shape, p.dtype) for p in pieces),
        grid=(segs,),
        in_specs=[rows_spec(GDN_WIDTH), rows_spec(GDN_WIDTH), rows_spec(GDN_WIDTH), rows_spec(LANES)] + [_hbm()] * count,
        out_specs=(rows_spec(GDN_WIDTH), per_chunk(HEAD_DIM, HEAD_DIM), per_chunk(CHUNK, CHUNK)) + (_hbm(),) * count,
        scratch_shapes=[pltpu.VMEM((batch, HEADS, HEAD_DIM, HEAD_DIM), F32), pltpu.SemaphoreType.DMA((sems,)),
                        pltpu.SemaphoreType.DMA((sems,))],
        compiler_params=_params("arbitrary"),
    )(as_seqs(q), as_seqs(k), as_seqs(v), as_seqs(bg), *pieces)
    return o.reshape(n, GDN_WIDTH), states, t_invs, gathered


def _gdn_bwd(do, q, k, v, bg, states, t_invs, rs, parts):
    n = q.shape[0]
    batch = n // rs
    chunks, seg_chunks, segs = _gdn_segments(rs, (3, 4, 2))
    seg_rows = seg_chunks * CHUNK
    chains = [(b, h) for b in range(batch) for h in range(HEADS)]
    each = lambda f, *lists: [f(*args) for args in zip(*lists)]
    count = len(parts)

    def body(do_ref, q_ref, k_ref, v_ref, bg_ref, s_ref, t_ref, *rest):
        p_refs, (dq_ref, dk_ref, dv_ref, dbg_ref), got_refs = rest[:count], rest[count:count + 4], rest[count + 4:2 * count + 4]
        dstate_ref, send_sems, recv_sems = rest[2 * count + 4:]
        exchange = _chip_copies(p_refs, got_refs, send_sems, recv_sems)

        @pl.when(pl.program_id(0) == 0)
        def _():
            dstate_ref[...] = jnp.zeros_like(dstate_ref)
            for cp in exchange:
                cp.start()

        ii, jj = _chunk_masks()
        incl = ii >= jj
        strict = ii > jj
        lane = lax.broadcasted_iota(jnp.int32, (1, LANES), 1)

        def rowsum(x):
            return jnp.sum(x, axis=1, keepdims=True)

        def total(x):
            return jnp.sum(rowsum(x), axis=0, keepdims=True)

        def chunk(step, carry):
            c = seg_chunks - 1 - step
            rows = pl.ds(pl.multiple_of(c * CHUNK, CHUNK), CHUNK)
            bgc = [bg_ref[b, rows, :] for b in range(batch)]
            qc = [q_ref[b, rows, _head_lanes(h)] for b, h in chains]
            kc = [k_ref[b, rows, _head_lanes(h)] for b, h in chains]
            vc = [v_ref[b, rows, _head_lanes(h)] for b, h in chains]
            doc = [do_ref[b, rows, _head_lanes(h)] for b, h in chains]
            beta = [bgc[b][:, h:h + 1] for b, h in chains]
            state = [s_ref[b, h, c] for b, h in chains]
            t_inv = [t_ref[b, h, c] for b, h in chains]
            d_state = [dstate_ref[b, h] for b, h in chains]
            dec = [_chunk_decay(bgc[b][:, HEADS + h:HEADS + h + 1], ii, jj) for b, h in chains]
            gc_col, g_total, decay = ([d[i] for d in dec] for i in range(3))
            kb = each(lambda x, y: x * y, kc, beta)
            vb = each(lambda x, y: x * y, vc, beta)
            eg = [jnp.exp(g) for g in gc_col]
            kbg = each(lambda x, y: x * y, kb, eg)
            a = each(lambda x, y, d: jnp.where(strict, _dot_nt(x, y) * d, 0.0), kb, kc, decay)
            qk = each(lambda x, y, d: jnp.where(incl, _dot_nt(x, y) * d, 0.0), qc, kc, decay)
            w = each(_dot, t_inv, kbg)
            u = each(_dot, t_inv, vb)
            q_dec = each(lambda x, y: x * y, qc, eg)
            ek = each(lambda gt, g: jnp.exp(gt - g), g_total, gc_col)
            k_dec = each(lambda x, y: x * y, kc, ek)
            g_last = [jnp.exp(gt) for gt in g_total]
            v_new = each(lambda x, y, s: x - _dot(y, s), u, w, state)
            dv_new = each(lambda m, d, x, ds: _dot_tn(m, d) + _dot(x, ds), qk, doc, k_dec, d_state)
            dqk = each(lambda d, vn: jnp.where(incl, _dot_nt(d, vn), 0.0), doc, v_new)
            dq_dec = each(_dot_nt, doc, state)
            dk_dec = each(_dot_nt, v_new, d_state)
            dg_last = each(lambda s, ds: total(s * ds), state, d_state)
            new_d_state = each(lambda x, d, gl, ds, y, dvn: _dot_tn(x, d) + gl * ds - _dot_tn(y, dvn),
                               q_dec, doc, g_last, d_state, w, dv_new)
            dw = each(lambda dvn, s: -_dot_nt(dvn, s), dv_new, state)
            dt = each(lambda dvn, x, y, z: _dot_nt(dvn, x) + _dot_nt(y, z), dv_new, vb, dw, kbg)
            dvb = each(_dot_tn, t_inv, dv_new)
            dkbg = each(_dot_tn, t_inv, dw)
            t_dt = each(_dot_tn, t_inv, dt)
            da = each(lambda x, t: -jnp.where(strict, _dot_nt(x, t), 0.0), t_dt, t_inv)
            dm_a = each(lambda x, y: x * y, da, decay)
            dm_qk = each(lambda x, y: x * y, dqk, decay)
            e = each(lambda x, y, z, t: x * y + z * t, da, a, dqk, qk)
            dkb = each(lambda m, x, y, z: _dot(m, x) + y * z, dm_a, kc, dkbg, eg)
            dk = each(lambda m, x, m2, y, z, t, p, bt: _dot_tn(m, x) + _dot_tn(m2, y) + z * t + p * bt,
                      dm_a, kb, dm_qk, qc, dk_dec, ek, dkb, beta)
            dq = each(lambda m, x, y, z: _dot(m, x) + y * z, dm_qk, kc, dq_dec, eg)
            dbeta = each(lambda x, y, z, t: rowsum(x * y + z * t), dkb, kc, dvb, vc)
            dgc = each(lambda x, p, pd, r, rd, s, sd: rowsum(x) - rowsum(jnp.where(ii == jj, jnp.sum(x, axis=0, keepdims=True), 0.0))
                       + rowsum(p * pd - r * rd + s * sd), e, dq_dec, q_dec, dk_dec, k_dec, dkbg, kbg)
            d_total = each(lambda r, rd, x, gl: total(r * rd) + x * gl, dk_dec, k_dec, dg_last, g_last)
            dg = each(lambda x, t: rowsum(jnp.where(jj >= ii, jnp.sum(jnp.where(ii == jj, x, 0.0), axis=0, keepdims=True), 0.0)) + t,
                      dgc, d_total)
            dbg = [jnp.zeros((CHUNK, LANES), F32) for _ in range(batch)]
            for i, (b, h) in enumerate(chains):
                dstate_ref[b, h] = new_d_state[i]
                dk_ref[b, rows, _head_lanes(h)] = dk[i]
                dq_ref[b, rows, _head_lanes(h)] = dq[i]
                dv_ref[b, rows, _head_lanes(h)] = dvb[i] * beta[i]
                dbg[b] = dbg[b] + jnp.where(lane == h, dbeta[i], 0.0) + jnp.where(lane == HEADS + h, dg[i], 0.0)
            for b in range(batch):
                dbg_ref[b, rows, :] = dbg[b]
            return carry

        lax.fori_loop(0, seg_chunks, chunk, 0)

        @pl.when(pl.program_id(0) == segs - 1)
        def _():
            for cp in exchange:
                cp.wait_recv()
            for cp in exchange:
                cp.wait_send()

    rows_spec = lambda width: pl.BlockSpec((batch, seg_rows, width), lambda s: (0, segs - 1 - s, 0))
    per_chunk = lambda r, c: pl.BlockSpec((batch, HEADS, seg_chunks, r, c), lambda s: (0, 0, segs - 1 - s, 0, 0))
    as_seqs = lambda a: a.reshape(batch, rs, a.shape[-1])
    grad = jax.ShapeDtypeStruct((batch, rs, GDN_WIDTH), F32)
    wide = rows_spec(GDN_WIDTH)
    dq, dk, dv, dbg, *got = pl.pallas_call(
        body, name="gdn_bwd",
        out_shape=(grad, grad, grad, jax.ShapeDtypeStruct((batch, rs, LANES), F32))
        + tuple(jax.ShapeDtypeStruct((3,) + p.shape[1:], p.dtype) for p in parts),
        grid=(segs,),
        in_specs=[wide, wide, wide, wide, rows_spec(LANES), per_chunk(HEAD_DIM, HEAD_DIM), per_chunk(CHUNK, CHUNK)]
        + [_hbm()] * count,
        out_specs=(wide, wide, wide, rows_spec(LANES)) + (_hbm(),) * count,
        scratch_shapes=[pltpu.VMEM((batch, HEADS, HEAD_DIM, HEAD_DIM), F32), pltpu.SemaphoreType.DMA((3 * count,)),
                        pltpu.SemaphoreType.DMA((3 * count,))],
        compiler_params=_params("arbitrary"),
    )(as_seqs(do), as_seqs(q), as_seqs(k), as_seqs(v), as_seqs(bg), states, t_invs, *parts)
    return dq.reshape(n, GDN_WIDTH), dk.reshape(n, GDN_WIDTH), dv.reshape(n, GDN_WIDTH), dbg.reshape(n, LANES), got


def _lane_vec(vals, offset):
    k = vals.shape[1]
    return jnp.pad(vals, ((0, 0), (offset, LANES - offset - k)))


LATER = ("w_out", "w_gate", "w_up", "w_down")


def _halves(a):
    return a.reshape(a.shape[:-2] + (2, a.shape[-2] // 2, a.shape[-1]))


def _local_step(x, target, meta, norms, w_in_t, conv_qkv, a_log, dt_bias, gdn_norm, conv_sc, later_shards, core_arg):
    batch, seq, d = x.shape
    tokens = N_META + seq
    pad_rows = (-tokens) % CHUNK
    rs = tokens + pad_rows
    x_offset = pad_rows + N_META
    n = batch * rs
    w_mix_pre, w_mix_post, w_ffn_pre, w_ffn_post = norms

    head = jnp.concatenate([jnp.zeros((pad_rows, d), F32), meta], axis=0)
    h0 = jnp.concatenate([jnp.broadcast_to(head[None], (batch, x_offset, d)), x], axis=1).reshape(n, d)
    target_p = jnp.pad(target, ((0, 0), (x_offset, 0), (0, 0))).reshape(n, d)
    a_log_l = _lane_vec(a_log, HEADS)
    dt_bias_l = _lane_vec(dt_bias, HEADS)

    u1 = _rms_fwd(h0, w_mix_pre, "rms_mix_pre")
    proj = _mm(u1, w_in_t, "nt", F32, "mm_proj")
    q = _qkv_fwd(proj, conv_qkv, "q", rs)
    k = _qkv_fwd(proj, conv_qkv, "k", rs)
    v = _qkv_fwd(proj, conv_qkv, "v", rs)
    bg = _gates_fwd(proj, a_log_l, dt_bias_l, rs, pad_rows)
    o, states, t_invs, gathered = _gdn_fwd(q, k, v, bg, rs, later_shards)
    w_out, w_gate_t, w_up_t, w_down = (a.reshape(-1, d) for a in gathered)
    o_gated = _gate_fwd(o, proj, gdn_norm, rs)
    y_sc = _sc_fwd(proj, conv_sc, rs)
    cat = jnp.concatenate([o_gated, y_sc], axis=1)
    mix = _mm(cat, w_out, "nn", F32, "mm_mix")
    h1, u2 = _mix_residual(h0, mix, w_mix_post, w_ffn_pre)
    gate, up, act = _swiglu_fwd(u2, w_gate_t, w_up_t)
    ffn = _mm(act, w_down, "nn", F32, "mm_down")

    dh2, dffn, d_ffn_post, sq = _loss_head(h1, ffn, w_ffn_post, target_p, rs, x_offset)
    d_w_down = _mm(act, dffn, "tn", F32, "mm_dw_down")
    dgate, dup = _swiglu_bwd(dffn, w_down, gate, up)
    d_w_gate_t = _mm(dgate, u2, "tn", F32, "mm_dw_gate")
    d_w_up_t = _mm(dup, u2, "tn", F32, "mm_dw_up")
    du2 = _mm(dup, w_up_t, "nn", F32, "mm_du2_up", init=_mm(dgate, w_gate_t, "nn", F32, "mm_du2_gate"))
    by_chip = [_halves(g.reshape(N_CHIPS, -1, d)) for g in (d_w_gate_t, d_w_up_t, d_w_down)]
    dh1, dmix, d_ffn_pre, d_mix_post, got_sibling = _mid_bwd(h1, mix, w_mix_post, w_ffn_pre, dh2, du2, by_chip)
    dcat = _mm(dmix, w_out, "nt", F32, "mm_dcat")
    d_w_out = _halves(_mm(cat, dmix, "tn", F32, "mm_dw_out").reshape(N_CHIPS, -1, d))
    by_chip, got_sibling = [d_w_out] + by_chip, list(_exchange_siblings([d_w_out])) + got_sibling
    sums = [_add_sibling(a, b, core_arg, name) for name, a, b in zip(LATER, by_chip, got_sibling)]
    do, dz, d_gdn_norm = _gate_bwd(dcat, o, proj, gdn_norm, rs)
    dscx, dscb, dscc, d_conv_sc = _sc_bwd(dcat, proj, conv_sc, rs)
    dq, dk, dv, dbg, got_chips = _gdn_bwd(do, q, k, v, bg, states, t_invs, rs, [send for _, send in sums])
    dpq, dwq = _qkv_bwd(dq, proj, conv_qkv, "q", rs)
    dpk, dwk = _qkv_bwd(dk, proj, conv_qkv, "k", rs)
    dpv, dwv = _qkv_bwd(dv, proj, conv_qkv, "v", rs)
    d_conv_qkv = jnp.concatenate([dwq, dwk, dwv], axis=1)
    dba, d_a_log_l, d_dt_bias_l = _gates_bwd(proj, dbg, a_log_l, dt_bias_l, rs, pad_rows)
    dproj = jnp.concatenate([dpq, dpk, dpv, dz, dscx, dscb, dscc, dba], axis=1)
    g_in = _halves(_in_from_kernel_order(_mm(dproj, u1, "tn", F32, "mm_dw_in")))
    sums.insert(0, _add_sibling(g_in, _exchange_siblings([g_in])[0], core_arg, "w_in"))
    du1, got_in = _mm(dproj, w_in_t, "nn", F32, "mm_du1", exchange=[sums[0][1]])
    got_chips.insert(0, got_in)
    dh0, d_mix_pre = _in_bwd(h0, w_mix_pre, dh1, du1)

    dh0 = dh0.reshape(batch, rs, d)
    grads = dict(
        meta_tokens=jnp.sum(dh0[:, pad_rows:x_offset], axis=0),
        mix_pre_norm=d_mix_pre, mix_post_norm=d_mix_post, ffn_pre_norm=d_ffn_pre, ffn_post_norm=d_ffn_post,
        conv_qkv=d_conv_qkv,
        a_log=d_a_log_l[:, HEADS:2 * HEADS], dt_bias=d_dt_bias_l[:, HEADS:2 * HEADS],
        gdn_norm=d_gdn_norm, conv_sc=d_conv_sc,
    )
    return sq, dh0[:, x_offset:], grads, [(part, got) for (part, _), got in zip(sums, got_chips)]


MATRICES = ("w_in", "w_out", "w_gate", "w_up", "w_down")
IN_SHARD = IN_WIDTH // N_CHIPS
IN_SHARD_PAD = 928


def _in_to_kernel_order(by_chip):
    w_t = by_chip[:, :IN_SHARD].reshape(IN_WIDTH, by_chip.shape[-1])
    lo, hi = 4 * GDN_WIDTH, 4 * GDN_WIDTH + 2 * HEADS
    return jnp.concatenate([w_t[:lo], w_t[hi:], w_t[lo:hi], jnp.zeros((IN_PAD - IN_WIDTH, w_t.shape[1]), w_t.dtype)], axis=0)


def _in_from_kernel_order(g_t):
    lo, hi = 4 * GDN_WIDTH, IN_WIDTH - 2 * HEADS
    g = jnp.concatenate([g_t[:lo], g_t[hi:IN_WIDTH], g_t[lo:hi]], axis=0).reshape(N_CHIPS, IN_SHARD, g_t.shape[-1])
    return jnp.pad(g, ((0, 0), (0, IN_SHARD_PAD - IN_SHARD), (0, 0)))


PACK_LANES = 3 * GDN_WIDTH
PACKED = dict(mix_pre_norm=(0, 1, D_MODEL), mix_post_norm=(1, 1, D_MODEL), ffn_pre_norm=(2, 1, D_MODEL),
              ffn_post_norm=(3, 1, D_MODEL), a_log=(4, 1, HEADS), dt_bias=(5, 1, HEADS), loss=(6, 1, 1),
              gdn_norm=(7, 1, HEAD_DIM), conv_qkv=(8, GDN_CONV, 3 * GDN_WIDTH), conv_sc=(16, SC_CONV, SC_WIDTH),
              meta_tokens=(32, N_META, D_MODEL))
PACK_ROWS = 48
SHARDED_SMALL = ("conv_qkv", "conv_sc", "meta_tokens")


def _pack_small(values):
    names = list(PACKED)

    def body(*refs):
        out_ref = refs[-1]
        out_ref[...] = jnp.zeros_like(out_ref)
        for name, ref in zip(names, refs):
            row, rows, lanes = PACKED[name]
            out_ref[row:row + rows, :lanes] = ref[...]

    return pl.pallas_call(body, name="pack_small", out_shape=jax.ShapeDtypeStruct((PACK_ROWS, PACK_LANES), F32))(
        *[values[name] for name in names])


def _sum_devices(packed_all, chip):
    names = list(PACKED)

    def body(chip_ref, all_ref, *rest):
        shard_refs, out_refs = rest[:len(SHARDED_SMALL)], rest[len(SHARDED_SMALL):]

        def total(ref, rows, lanes):
            acc = ref[0, rows, lanes]
            for k in range(1, 8):
                acc = acc + ref[k, rows, lanes]
            return acc

        for name, out in zip(names, out_refs):
            row, rows, lanes = PACKED[name]
            if name in SHARDED_SMALL:
                out[...] = total(shard_refs[SHARDED_SMALL.index(name)], slice(0, rows), slice(None))
            else:
                out[...] = total(all_ref, slice(row, row + rows), slice(0, lanes))

    def shard_spec(name):
        row, rows, lanes = PACKED[name]
        height = max(rows, 8)
        assert row % height == 0
        return pl.BlockSpec((8, height, lanes // N_CHIPS), lambda i, chip_ref: (0, row // height, chip_ref[0]))

    def out_shape(name):
        _, rows, lanes = PACKED[name]
        return jax.ShapeDtypeStruct((rows, lanes // N_CHIPS if name in SHARDED_SMALL else lanes), F32)

    whole = lambda shape: pl.BlockSpec(shape, lambda i, chip_ref: (0,) * len(shape))
    outs = pl.pallas_call(
        body, name="sum_devices", out_shape=tuple(out_shape(n) for n in names),
        grid_spec=pltpu.PrefetchScalarGridSpec(
            num_scalar_prefetch=1, grid=(1,),
            in_specs=[whole(packed_all.shape)] + [shard_spec(n) for n in SHARDED_SMALL],
            out_specs=tuple(whole(out_shape(n).shape) for n in names)),
    )(chip, packed_all, *[packed_all] * len(SHARDED_SMALL))
    return dict(zip(names, outs))


def _hbm():
    return pl.BlockSpec(memory_space=pl.ANY)


def _place():
    x, y, c = lax.axis_index("x"), lax.axis_index("y"), lax.axis_index("c")
    chips = ((1 - x, y), (x, 1 - y), (1 - x, 1 - y))
    return x, y, c, chips


def _remote(src, dst, send_sems, recv_sems, k, to):
    return pltpu.make_async_remote_copy(src_ref=src, dst_ref=dst, send_sem=send_sems.at[k], recv_sem=recv_sems.at[k],
                                        device_id=to, device_id_type=MESH)


GATHER_SEMS = 7


def _gather_copies(w_refs, out_refs, send_sems, recv_sems):
    x, y, c, chips = _place()
    mine = 2 * x + y
    sibling = (x, y, 1 - c)
    copy = functools.partial(_remote, send_sems=send_sems, recv_sems=recv_sems)
    direct, landed, passing, from_sibling = [], [], [], []
    for i, (w, o) in enumerate(zip(w_refs, out_refs)):
        k = GATHER_SEMS * i
        direct.append(copy(w, o.at[mine], k=k, to=sibling))
        from_sibling.append(copy(w, o.at[mine], k=k, to=sibling))
        for j, (cx, cy) in enumerate(chips):
            theirs = 2 * cx + cy
            direct.append(copy(w.at[c], o.at[mine, c], k=k + 1 + j, to=(cx, cy, c)))
            landed.append(copy(w.at[c], o.at[theirs, c], k=k + 1 + j, to=sibling))
            passing.append(copy(o.at[theirs, c], o.at[theirs, c], k=k + 4 + j, to=sibling))
            from_sibling.append(copy(w.at[c], o.at[theirs, 1 - c], k=k + 4 + j, to=sibling))
    return direct, landed, passing, from_sibling


def _gather_finish(copies):
    direct, landed, passing, from_sibling = copies
    for arrival, forward in zip(landed, passing):
        arrival.wait_recv()
        forward.start()
    for arrival in from_sibling:
        arrival.wait_recv()
    for cp in direct + passing:
        cp.wait_send()


def _gather_weights(pieces, smalls):
    count, extra = len(pieces), len(smalls)
    total = count + extra

    def body(*refs):
        w_refs, s_refs = refs[:count], refs[count:total]
        out_refs, sall_refs = refs[total:total + count], refs[total + count:2 * total]
        send_sems, recv_sems, local_sems = refs[2 * total:]
        x, y, c, chips = _place()
        mine = 2 * x + y
        own = [pltpu.make_async_copy(s, sall.at[mine], local_sems.at[i]) for i, (s, sall) in enumerate(zip(s_refs, sall_refs))]
        small = [_remote(s, sall.at[mine], send_sems, recv_sems, GATHER_SEMS * count + 3 * i + j, (cx, cy, c))
                 for i, (s, sall) in enumerate(zip(s_refs, sall_refs)) for j, (cx, cy) in enumerate(chips)]
        copies = _gather_copies(w_refs, out_refs, send_sems, recv_sems)
        for cp in own + small + copies[0]:
            cp.start()
        _gather_finish(copies)
        for cp in small:
            cp.wait_recv()
        for cp in small:
            cp.wait_send()
        for cp in own:
            cp.wait()

    sems = GATHER_SEMS * count + 3 * extra
    return pl.pallas_call(
        body, name="gather_weights",
        out_shape=tuple(jax.ShapeDtypeStruct((N_CHIPS,) + p.shape, p.dtype) for p in list(pieces) + list(smalls)),
        in_specs=[_hbm()] * total, out_specs=(_hbm(),) * total,
        scratch_shapes=[pltpu.SemaphoreType.DMA((sems,)), pltpu.SemaphoreType.DMA((sems,)), pltpu.SemaphoreType.DMA((extra,))],
    )(*pieces, *smalls)


def _sibling_copies(g_refs, got_refs, send_sems, recv_sems):
    x, y, c, _ = _place()
    return [_remote(g.at[:, 1 - c], got, send_sems, recv_sems, i, (x, y, 1 - c)) for i, (g, got) in enumerate(zip(g_refs, got_refs))]


def _exchange_siblings(grads, small=None):
    count = len(grads)
    extra = 0 if small is None else 1

    def body(*refs):
        g_refs = refs[:count]
        got_refs = refs[count + extra:2 * count + extra]
        send_sems, recv_sems = refs[2 * (count + extra):2 * (count + extra) + 2]
        x, y, c, _ = _place()
        copies = _sibling_copies(g_refs, got_refs, send_sems, recv_sems)
        if small is not None:
            s_ref, sall_ref, local_sem = refs[count], refs[2 * count + 1], refs[-1]
            me = 4 * x + 2 * y + c
            own = pltpu.make_async_copy(s_ref, sall_ref.at[me], local_sem)
            own.start()
            for k in range(7):
                dx, dy, dc = ((k + 1) >> 2) & 1, ((k + 1) >> 1) & 1, (k + 1) & 1
                peer = (1 - x if dx else x, 1 - y if dy else y, 1 - c if dc else c)
                copies.append(_remote(s_ref, sall_ref.at[me], send_sems, recv_sems, count + k, peer))
        for cp in copies:
            cp.start()
        for cp in copies:
            cp.wait_recv()
        for cp in copies:
            cp.wait_send()
        if small is not None:
            own.wait()

    sems = count + 7 * extra
    return pl.pallas_call(
        body, name="exchange_siblings" + ("" if small is None else "_small"),
        out_shape=tuple(jax.ShapeDtypeStruct((g.shape[0],) + g.shape[2:], F32) for g in grads)
        + (() if small is None else (jax.ShapeDtypeStruct((8,) + small.shape, F32),)),
        in_specs=[_hbm()] * (count + extra), out_specs=(_hbm(),) * (count + extra),
        scratch_shapes=[pltpu.SemaphoreType.DMA((sems,)), pltpu.SemaphoreType.DMA((sems,))]
        + ([] if small is None else [pltpu.SemaphoreType.DMA]),
    )(*grads, *(() if small is None else (small,)))


def _chip_copies(p_refs, got_refs, send_sems, recv_sems):
    x, y, c, chips = _place()
    return [_remote(p.at[2 * cx + cy], got.at[j], send_sems, recv_sems, 3 * i + j, (cx, cy, c))
            for i, (p, got) in enumerate(zip(p_refs, got_refs)) for j, (cx, cy) in enumerate(chips)]


def _share_halves(halves):
    count = len(halves)

    def body(*refs):
        h_refs, full_refs = refs[:count], refs[count:2 * count]
        send_sems, recv_sems = refs[2 * count:]
        x, y, c, _ = _place()
        copies = [pltpu.make_async_remote_copy(src_ref=h.at[c], dst_ref=full.at[c], send_sem=send_sems.at[i],
                                               recv_sem=recv_sems.at[i], device_id=(x, y, 1 - c), device_id_type=MESH)
                  for i, (h, full) in enumerate(zip(h_refs, full_refs))]
        for cp in copies:
            cp.start()
        for cp in copies:
            cp.wait_recv()
        for cp in copies:
            cp.wait_send()

    return pl.pallas_call(
        body, name="share_halves", out_shape=tuple(jax.ShapeDtypeStruct(h.shape, h.dtype) for h in halves),
        in_specs=[_hbm()] * count, out_specs=(_hbm(),) * count, input_output_aliases={i: i for i in range(count)},
        scratch_shapes=[pltpu.SemaphoreType.DMA((count,)), pltpu.SemaphoreType.DMA((count,))],
    )(*halves)


def _add_sibling(grad, got, core, name):
    chips, _, rows, cols = grad.shape

    def body(core_ref, g_ref, r_ref, sum_ref, send_ref):
        s = g_ref[...] + r_ref[...]
        sum_ref[...] = s
        send_ref[...] = s.astype(send_ref.dtype)

    block = pl.BlockSpec((None, rows, cols), lambda p, core_ref: (p, 0, 0))
    return pl.pallas_call(
        body, name="add_sibling_" + name,
        out_shape=(jax.ShapeDtypeStruct((chips, rows, cols), F32), jax.ShapeDtypeStruct((chips, rows, cols), BF16)),
        grid_spec=pltpu.PrefetchScalarGridSpec(
            num_scalar_prefetch=1, grid=(chips,),
            in_specs=[pl.BlockSpec((None, None, rows, cols), lambda p, core_ref: (p, core_ref[0], 0, 0)), block],
            out_specs=(block, block)),
        compiler_params=_params("parallel"),
    )(core, grad, got)


def _add_chips(part, got, chip_core, name):
    _, rows, cols = part.shape
    tr = rows // 2 if rows % 32 == 0 else rows

    def body(place_ref, p_ref, r_ref, o_ref):
        o_ref[...] = ((p_ref[...] + r_ref[0].astype(F32)) + r_ref[1].astype(F32)) + r_ref[2].astype(F32)

    return pl.pallas_call(
        body, name="add_chips_" + name, out_shape=jax.ShapeDtypeStruct((2, rows, cols), F32),
        grid_spec=pltpu.PrefetchScalarGridSpec(
            num_scalar_prefetch=1, grid=(rows // tr,),
            in_specs=[pl.BlockSpec((None, tr, cols), lambda i, place_ref: (place_ref[0], i, 0)),
                      pl.BlockSpec((3, tr, cols), lambda i, place_ref: (0, i, 0))],
            out_specs=pl.BlockSpec((None, tr, cols), lambda i, place_ref: (place_ref[1], i, 0))),
        compiler_params=_params("parallel"),
    )(chip_core, part, got)


def _adamw(w, g, m, v, name):
    rows, cols = w.shape
    tr = _pick(rows, (3592, 256, 352, 176, 128, 64, 32, 16, 8))

    def body(w_ref, g_ref, m_ref, v_ref, d_ref, nm_ref, nv_ref):
        g = g_ref[...]
        m = ADAM_B1 * m_ref[...] + (1.0 - ADAM_B1) * g
        v = ADAM_B2 * v_ref[...] + (1.0 - ADAM_B2) * (g * g)
        m_hat = m / (1.0 - ADAM_B1 ** ADAM_STEP)
        v_hat = v / (1.0 - ADAM_B2 ** ADAM_STEP)
        d_ref[...] = -ADAM_LR * (m_hat / (jnp.sqrt(v_hat) + ADAM_EPS) + ADAM_WD * w_ref[...])
        nm_ref[...] = m
        nv_ref[...] = v

    block = pl.BlockSpec((tr, cols), lambda i: (i, 0))
    shape = jax.ShapeDtypeStruct((rows, cols), F32)
    return pl.pallas_call(
        body, name="adamw_" + name, out_shape=(shape, shape, shape), grid=(rows // tr,),
        in_specs=[block] * 4, out_specs=(block,) * 3, compiler_params=_params("parallel"),
    )(w, g, m, v)


WEIGHTS = ("meta_tokens", "mix_pre_norm", "mix_post_norm", "ffn_pre_norm", "ffn_post_norm", "w_in", "conv_qkv", "a_log",
           "dt_bias", "gdn_norm", "conv_sc", "w_out", "w_gate", "w_up", "w_down")


def kernel(x, meta_tokens, mix_pre_norm, mix_post_norm, ffn_pre_norm, ffn_post_norm, w_in, conv_qkv, a_log, dt_bias, gdn_norm, conv_sc, w_out, w_gate, w_up, w_down, loss_target, m_meta_tokens, m_mix_pre_norm, m_mix_post_norm, m_ffn_pre_norm, m_ffn_post_norm, m_w_in, m_conv_qkv, m_a_log, m_dt_bias, m_gdn_norm, m_conv_sc, m_w_out, m_w_gate, m_w_up, m_w_down, v_meta_tokens, v_mix_pre_norm, v_mix_post_norm, v_ffn_pre_norm, v_ffn_post_norm, v_w_in, v_conv_qkv, v_a_log, v_dt_bias, v_gdn_norm, v_conv_sc, v_w_out, v_w_gate, v_w_up, v_w_down):
    d = x.shape[-1]
    two_d = lambda a: a.reshape(a.shape[-2:])
    weights = dict(zip(WEIGHTS, (meta_tokens, mix_pre_norm, mix_post_norm, ffn_pre_norm, ffn_post_norm, w_in, conv_qkv, a_log,
                                 dt_bias, gdn_norm, conv_sc, w_out, w_gate, w_up, w_down)))
    m_in = dict(zip(WEIGHTS, (m_meta_tokens, m_mix_pre_norm, m_mix_post_norm, m_ffn_pre_norm, m_ffn_post_norm, m_w_in, m_conv_qkv,
                              m_a_log, m_dt_bias, m_gdn_norm, m_conv_sc, m_w_out, m_w_gate, m_w_up, m_w_down)))
    v_in = dict(zip(WEIGHTS, (v_meta_tokens, v_mix_pre_norm, v_mix_post_norm, v_ffn_pre_norm, v_ffn_post_norm, v_w_in, v_conv_qkv,
                              v_a_log, v_dt_bias, v_gdn_norm, v_conv_sc, v_w_out, v_w_gate, v_w_up, v_w_down)))
    core = lax.axis_index("c")
    chip = 2 * lax.axis_index("x") + lax.axis_index("y")
    core_arg = core.reshape(1).astype(jnp.int32)
    chip_core = jnp.stack([chip, core]).astype(jnp.int32)
    whole = lambda a: a.reshape(a.shape[:-3] + (2 * a.shape[-2], d))
    by_rows = lambda n, a: two_d(a).T if n in ("w_in", "w_gate", "w_up") else two_d(a)

    shard = {n: by_rows(n, weights[n]).astype(MXU_DTYPE) for n in MATRICES}
    shard["w_in"] = jnp.pad(shard["w_in"], ((0, IN_SHARD_PAD - IN_SHARD), (0, 0)))
    w_in_all, *small_all = _gather_weights([_halves(shard["w_in"])], [two_d(weights[n]) for n in SHARDED_SMALL])
    w_in_t = _in_to_kernel_order(whole(w_in_all))
    conv_qkv_full, conv_sc_full, meta_full = (jnp.concatenate([a[p] for p in range(N_CHIPS)], axis=1) for a in small_all)

    sq, grad_x, g, sums = _local_step(
        x, loss_target, meta_full, (mix_pre_norm, mix_post_norm, ffn_pre_norm, ffn_post_norm), w_in_t, conv_qkv_full, a_log,
        dt_bias, gdn_norm, conv_sc_full, [_halves(shard[n]) for n in LATER], core_arg)

    (packed_all,) = _exchange_siblings([], _pack_small(dict(g, loss=sq)))
    totals = [_add_chips(part, got, chip_core, n) for n, (part, got) in zip(MATRICES, sums)]
    grads = {n: whole(a) for n, a in zip(MATRICES, _share_halves(totals))}
    grads["w_in"] = grads["w_in"][:IN_SHARD]
    grads.update(_sum_devices(packed_all, chip.reshape(1).astype(jnp.int32)))
    loss = (0.5 / d) * grads.pop("loss")[0, 0]

    outs = [[], [], [], []]
    for n in WEIGHTS:
        shape = weights[n].shape
        delta, new_m, new_v = _adamw(by_rows(n, weights[n]), grads[n], by_rows(n, m_in[n]), by_rows(n, v_in[n]), n)
        for out, a in zip(outs, (grads[n], delta, new_m, new_v)):
            out.append((a.T if n in ("w_in", "w_gate", "w_up") else a).reshape(shape))
    return (loss, grad_x, *outs[0], *outs[1], *outs[2], *outs[3])
```

```python
import functools

import jax
import jax.numpy as jnp
from jax import lax
from jax.experimental import pallas as pl
from jax.experimental.pallas import tpu as pltpu

F32 = jnp.float32
BF16 = jnp.bfloat16
MXU_DTYPE = jnp.bfloat16
MESH = pl.DeviceIdType.MESH

D_MODEL = 1024
N_META = 16
HEADS = 4
HEAD_DIM = 128
GDN_WIDTH = HEADS * HEAD_DIM
GDN_CONV = 4
CHUNK = 64
SC_WIDTH = D_MODEL - GDN_WIDTH
SC_CONV = 3
D_FF = 2816
IN_WIDTH = 4 * GDN_WIDTH + 2 * HEADS + 3 * SC_WIDTH
IN_PAD = 3840
BA_COL = (4 * GDN_WIDTH + 3 * SC_WIDTH) // 128
EPS = 1e-6
LANES = 128
N_CHIPS = 4
VMEM_LIMIT = 48 * 2 ** 20

ADAM_LR = 0.001
ADAM_B1 = 0.9
ADAM_B2 = 0.999
ADAM_EPS = 1e-08
ADAM_WD = 0.01
ADAM_STEP = 10


def _pick(n, candidates):
    for c in candidates:
        if n % c == 0:
            return c
    return n


def _row_tile(n):
    return _pick(n, (352, 256, 176, 128, 64, 32, 16, 8))


def _params(*sem):
    return pltpu.CompilerParams(dimension_semantics=sem, vmem_limit_bytes=VMEM_LIMIT)


def _sigmoid(x):
    return 0.5 * jnp.tanh(0.5 * x) + 0.5


def _softplus(x):
    return jnp.maximum(x, 0.0) + jnp.log(1.0 + jnp.exp(-jnp.abs(x)))


def _dsilu(x, s):
    return s * (1.0 + x * (1.0 - s))


def _mm(a, b, mode, out_dtype, name, init=None, exchange=None):
    if mode == "tn":
        k_dim, m_dim = a.shape
    else:
        m_dim, k_dim = a.shape
    n_dim = b.shape[0] if mode == "nt" else b.shape[1]
    tm = _pick(m_dim, (1408, 1280, 1024, 512, 256, 128) if mode == "tn" else (1056, 1024, 704, 512, 256, 128))
    tn = _pick(n_dim, (1408, 1280, 1024, 768, 512, 256, 128))
    tk = _pick(k_dim, (1408, 1280, 1056, 1024, 512, 256, 128))
    nk = k_dim // tk
    if mode == "nn":
        a_spec = pl.BlockSpec((tm, tk), lambda i, j, k: (i, k))
        b_spec = pl.BlockSpec((tk, tn), lambda i, j, k: (k, j))
        dims = (((1,), (0,)), ((), ()))
    elif mode == "nt":
        a_spec = pl.BlockSpec((tm, tk), lambda i, j, k: (i, k))
        b_spec = pl.BlockSpec((tn, tk), lambda i, j, k: (j, k))
        dims = (((1,), (1,)), ((), ()))
    else:
        a_spec = pl.BlockSpec((tk, tm), lambda i, j, k: (k, i))
        b_spec = pl.BlockSpec((tk, tn), lambda i, j, k: (k, j))
        dims = (((0,), (0,)), ((), ()))

    out_spec = pl.BlockSpec((tm, tn), lambda i, j, k: (i, j))
    grid = (m_dim // tm, n_dim // tn, nk)
    parts = () if exchange is None else tuple(exchange)
    count = len(parts)
    first_in = 2 if init is None else 3

    assert out_dtype == F32

    def body(a_ref, b_ref, *rest):
        o_ref = rest[first_in - 2 + count]
        k = pl.program_id(2)
        step = (pl.program_id(0) * grid[1] + pl.program_id(1)) * nk + k
        if count:
            copies = _chip_copies(rest[first_in - 2:first_in - 2 + count], rest[first_in - 1 + count:first_in - 1 + 2 * count],
                                  *rest[first_in - 1 + 2 * count:])

            @pl.when(step == 0)
            def _():
                for cp in copies:
                    cp.start()

        p = lax.dot_general(a_ref[...], b_ref[...], dims, preferred_element_type=F32)
        if nk == 1:
            o_ref[...] = p if init is None else rest[0][...] + p
        else:
            @pl.when(k == 0)
            def _():
                o_ref[...] = p if init is None else rest[0][...] + p

            @pl.when(k > 0)
            def _():
                o_ref[...] += p

        if count:
            @pl.when(step == grid[0] * grid[1] * nk - 1)
            def _():
                for cp in copies:
                    cp.wait_recv()
                for cp in copies:
                    cp.wait_send()

    out = pl.pallas_call(
        body, name=name,
        out_shape=(jax.ShapeDtypeStruct((m_dim, n_dim), out_dtype),)
        + tuple(jax.ShapeDtypeStruct((3,) + p.shape[1:], p.dtype) for p in parts),
        grid=grid,
        in_specs=[a_spec, b_spec] + ([] if init is None else [out_spec]) + [_hbm()] * count,
        out_specs=(out_spec,) + (_hbm(),) * count,
        scratch_shapes=[pltpu.SemaphoreType.DMA((3 * count,)), pltpu.SemaphoreType.DMA((3 * count,))] if count else [],
        compiler_params=_params(*(("arbitrary",) * 3 if count else ("parallel", "parallel", "arbitrary"))),
    )(a, b, *(() if init is None else (init,)), *parts)
    return out[0] if not count else out


def _rms_apply(x, w):
    r = lax.rsqrt(jnp.mean(x * x, axis=-1, keepdims=True) + EPS)
    return x * r * w


def _rms_bwd(x, w, dy):
    r = lax.rsqrt(jnp.mean(x * x, axis=-1, keepdims=True) + EPS)
    xh = x * r
    dyw = dy * w
    dx = r * (dyw - xh * jnp.mean(dyw * xh, axis=-1, keepdims=True))
    return dx, jnp.sum(dy * xh, axis=0, keepdims=True)


def _accumulate(ref, first, value):
    @pl.when(first)
    def _():
        ref[...] = value

    @pl.when(jnp.logical_not(first))
    def _():
        ref[...] += value


def _rows(tr, width):
    return pl.BlockSpec((tr, width), lambda i: (i, 0))


def _vec(width):
    return pl.BlockSpec((1, width), lambda i: (0, 0))


def _rms_fwd(h, w, name):
    n, d = h.shape
    tr = _row_tile(n)

    def body(h_ref, w_ref, u_ref):
        u_ref[...] = _rms_apply(h_ref[...], w_ref[...]).astype(u_ref.dtype)

    return pl.pallas_call(
        body, name=name, out_shape=jax.ShapeDtypeStruct((n, d), MXU_DTYPE), grid=(n // tr,),
        in_specs=[_rows(tr, d), _vec(d)], out_specs=_rows(tr, d), compiler_params=_params("parallel"),
    )(h, w)


def _mix_residual(h0, mix, w_post, w_pre):
    n, d = h0.shape
    tr = _row_tile(n)

    def body(h0_ref, mix_ref, wpost_ref, wpre_ref, h1_ref, u2_ref):
        h1 = h0_ref[...] + _rms_apply(mix_ref[...], wpost_ref[...])
        h1_ref[...] = h1
        u2_ref[...] = _rms_apply(h1, wpre_ref[...]).astype(u2_ref.dtype)

    return pl.pallas_call(
        body, name="mix_residual",
        out_shape=(jax.ShapeDtypeStruct((n, d), F32), jax.ShapeDtypeStruct((n, d), MXU_DTYPE)), grid=(n // tr,),
        in_specs=[_rows(tr, d), _rows(tr, d), _vec(d), _vec(d)], out_specs=(_rows(tr, d), _rows(tr, d)),
        compiler_params=_params("parallel"),
    )(h0, mix, w_post, w_pre)


NT_DIMS = (((1,), (1,)), ((), ()))


def _ffn_tiles(n):
    return _pick(n, (704, 512, 256, 128)), _pick(D_FF, (1408, 256, 128))


def _swiglu_fwd(u, w_gate_t, w_up_t):
    n, d = u.shape
    tm, tn = _ffn_tiles(n)

    def body(u_ref, wg_ref, wu_ref, g_ref, up_ref, act_ref):
        a = u_ref[...]
        g = lax.dot_general(a, wg_ref[...], NT_DIMS, preferred_element_type=F32)
        up = lax.dot_general(a, wu_ref[...], NT_DIMS, preferred_element_type=F32)
        g_ref[...] = g
        up_ref[...] = up
        act_ref[...] = (g * _sigmoid(g) * up).astype(act_ref.dtype)

    tile = pl.BlockSpec((tm, tn), lambda j, i: (i, j))
    weight = pl.BlockSpec((tn, d), lambda j, i: (j, 0))
    wide = jax.ShapeDtypeStruct((n, D_FF), F32)
    return pl.pallas_call(
        body, name="swiglu_fwd", out_shape=(wide, wide, jax.ShapeDtypeStruct((n, D_FF), MXU_DTYPE)),
        grid=(D_FF // tn, n // tm),
        in_specs=[pl.BlockSpec((tm, d), lambda j, i: (i, 0)), weight, weight], out_specs=(tile, tile, tile),
        compiler_params=_params("parallel", "parallel"),
    )(u, w_gate_t, w_up_t)


def _swiglu_bwd(dffn, w_down, gate, up):
    n, d = dffn.shape
    tm, tn = _ffn_tiles(n)

    def body(dy_ref, w_ref, g_ref, u_ref, dg_ref, du_ref):
        da = lax.dot_general(dy_ref[...], w_ref[...], NT_DIMS, preferred_element_type=F32)
        g = g_ref[...]
        s = _sigmoid(g)
        dg_ref[...] = (da * u_ref[...] * _dsilu(g, s)).astype(dg_ref.dtype)
        du_ref[...] = (da * g * s).astype(du_ref.dtype)

    tile = pl.BlockSpec((tm, tn), lambda j, i: (i, j))
    shape = jax.ShapeDtypeStruct((n, D_FF), MXU_DTYPE)
    return pl.pallas_call(
        body, name="swiglu_bwd", out_shape=(shape, shape), grid=(D_FF // tn, n // tm),
        in_specs=[pl.BlockSpec((tm, d), lambda j, i: (i, 0)), pl.BlockSpec((tn, d), lambda j, i: (j, 0)), tile, tile],
        out_specs=(tile, tile), compiler_params=_params("parallel", "parallel"),
    )(dffn, w_down, gate, up)


def _loss_head(h1, ffn, w_post, target, rows_per_seq, x_offset):
    n, d = h1.shape
    tr = _row_tile(rows_per_seq)
    tiles_per_seq = rows_per_seq // tr

    def body(h1_ref, ffn_ref, w_ref, t_ref, dh2_ref, dffn_ref, dw_ref, sq_ref):
        i = pl.program_id(0)
        w = w_ref[...]
        f = ffn_ref[...]
        r = lax.rsqrt(jnp.mean(f * f, axis=-1, keepdims=True) + EPS)
        fh = f * r
        row = lax.rem(i, tiles_per_seq) * tr + lax.broadcasted_iota(jnp.int32, (tr, 1), 0)
        err = jnp.where(row >= x_offset, h1_ref[...] + fh * w - t_ref[...], 0.0)
        dh2 = err * (1.0 / d)
        dh2_ref[...] = dh2
        dyw = dh2 * w
        dffn_ref[...] = (r * (dyw - fh * jnp.mean(dyw * fh, axis=-1, keepdims=True))).astype(dffn_ref.dtype)
        _accumulate(dw_ref, i == 0, jnp.sum(dh2 * fh, axis=0, keepdims=True))
        _accumulate(sq_ref, i == 0, jnp.sum(jnp.sum(err * err, axis=1, keepdims=True), axis=0, keepdims=True))

    return pl.pallas_call(
        body, name="loss_head",
        out_shape=(jax.ShapeDtypeStruct((n, d), F32), jax.ShapeDtypeStruct((n, d), MXU_DTYPE),
                   jax.ShapeDtypeStruct((1, d), F32), jax.ShapeDtypeStruct((1, 1), F32)),
        grid=(n // tr,),
        in_specs=[_rows(tr, d), _rows(tr, d), _vec(d), _rows(tr, d)],
        out_specs=(_rows(tr, d), _rows(tr, d), _vec(d), _vec(1)),
        compiler_params=_params("arbitrary"),
    )(h1, ffn, w_post, target)


def _mid_bwd(h1, mix, w_mix_post, w_ffn_pre, dh2, du2, grads):
    n, d = h1.shape
    tr = _row_tile(n)
    count = len(grads)

    def body(h1_ref, mix_ref, wpost_ref, wpre_ref, dh2_ref, du2_ref, *rest):
        g_refs, (dh1_ref, dmix_ref, dwpre_ref, dwpost_ref), got_refs = rest[:count], rest[count:count + 4], rest[count + 4:2 * count + 4]
        exchange = _sibling_copies(g_refs, got_refs, *rest[2 * count + 4:])
        i = pl.program_id(0)

        @pl.when(i == 0)
        def _():
            for cp in exchange:
                cp.start()

        dx, dwpre = _rms_bwd(h1_ref[...], wpre_ref[...], du2_ref[...])
        dh1 = dh2_ref[...] + dx
        dh1_ref[...] = dh1
        dmix, dwpost = _rms_bwd(mix_ref[...], wpost_ref[...], dh1)
        dmix_ref[...] = dmix.astype(dmix_ref.dtype)
        _accumulate(dwpre_ref, i == 0, dwpre)
        _accumulate(dwpost_ref, i == 0, dwpost)

        @pl.when(i == n // tr - 1)
        def _():
            for cp in exchange:
                cp.wait_recv()
            for cp in exchange:
                cp.wait_send()

    dh1, dmix, dwpre, dwpost, *got = pl.pallas_call(
        body, name="mid_bwd",
        out_shape=(jax.ShapeDtypeStruct((n, d), F32), jax.ShapeDtypeStruct((n, d), MXU_DTYPE),
                   jax.ShapeDtypeStruct((1, d), F32), jax.ShapeDtypeStruct((1, d), F32))
        + tuple(jax.ShapeDtypeStruct((g.shape[0],) + g.shape[2:], F32) for g in grads),
        grid=(n // tr,),
        in_specs=[_rows(tr, d), _rows(tr, d), _vec(d), _vec(d), _rows(tr, d), _rows(tr, d)] + [_hbm()] * count,
        out_specs=(_rows(tr, d), _rows(tr, d), _vec(d), _vec(d)) + (_hbm(),) * count,
        scratch_shapes=[pltpu.SemaphoreType.DMA((count,)), pltpu.SemaphoreType.DMA((count,))],
        compiler_params=_params("arbitrary"),
    )(h1, mix, w_mix_post, w_ffn_pre, dh2, du2, *grads)
    return dh1, dmix, dwpre, dwpost, got


def _in_bwd(h0, w_pre, dh1, du1):
    n, d = h0.shape
    tr = _row_tile(n)

    def body(h0_ref, w_ref, dh1_ref, du1_ref, dh0_ref, dw_ref):
        dx, dw = _rms_bwd(h0_ref[...], w_ref[...], du1_ref[...])
        dh0_ref[...] = dh1_ref[...] + dx
        _accumulate(dw_ref, pl.program_id(0) == 0, dw)

    return pl.pallas_call(
        body, name="in_bwd",
        out_shape=(jax.ShapeDtypeStruct((n, d), F32), jax.ShapeDtypeStruct((1, d), F32)), grid=(n // tr,),
        in_specs=[_rows(tr, d), _vec(d), _rows(tr, d), _rows(tr, d)], out_specs=(_rows(tr, d), _vec(d)),
        compiler_params=_params("arbitrary"),
    )(h0, w_pre, dh1, du1)


def _lane_is(lo, hi):
    lane = lax.broadcasted_iota(jnp.int32, (1, LANES), 1)
    return jnp.logical_and(lane >= lo, lane < hi)


def _gates_fwd(proj, a_log_l, dt_bias_l, rows_per_seq, pad_rows):
    n = proj.shape[0]
    tr = _row_tile(rows_per_seq)
    tiles_per_seq = rows_per_seq // tr

    def body(p_ref, a_ref, dt_ref, o_ref):
        x = p_ref[...]
        row = lax.rem(pl.program_id(0), tiles_per_seq) * tr + lax.broadcasted_iota(jnp.int32, (tr, 1), 0)
        g = -jnp.exp(a_ref[...]) * _softplus(x + dt_ref[...])
        val = jnp.where(_lane_is(0, HEADS), _sigmoid(x), jnp.where(_lane_is(HEADS, 2 * HEADS), g, 0.0))
        o_ref[...] = jnp.where(row >= pad_rows, val, 0.0)

    return pl.pallas_call(
        body, name="gates_fwd", out_shape=jax.ShapeDtypeStruct((n, LANES), F32), grid=(n // tr,),
        in_specs=[pl.BlockSpec((tr, LANES), lambda i: (i, BA_COL)), _vec(LANES), _vec(LANES)],
        out_specs=_rows(tr, LANES), compiler_params=_params("parallel"),
    )(proj, a_log_l, dt_bias_l)


def _gates_bwd(proj, dbg, a_log_l, dt_bias_l, rows_per_seq, pad_rows):
    n = proj.shape[0]
    tr = _row_tile(rows_per_seq)
    tiles_per_seq = rows_per_seq // tr

    def body(p_ref, d_ref, a_ref, dt_ref, dx_ref, da_ref, ddt_ref):
        i = pl.program_id(0)
        x = p_ref[...]
        d = d_ref[...]
        row = lax.rem(i, tiles_per_seq) * tr + lax.broadcasted_iota(jnp.int32, (tr, 1), 0)
        live = row >= pad_rows
        beta = _sigmoid(x)
        ea = jnp.exp(a_ref[...])
        xa = x + dt_ref[...]
        g = -ea * _softplus(xa)
        is_g = _lane_is(HEADS, 2 * HEADS)
        d_alogit = jnp.where(jnp.logical_and(live, is_g), d * (-ea) * _sigmoid(xa), 0.0)
        d_blogit = jnp.where(jnp.logical_and(live, _lane_is(0, HEADS)), d * beta * (1.0 - beta), 0.0)
        dx_ref[:, :LANES] = (d_alogit + d_blogit).astype(dx_ref.dtype)
        dx_ref[:, LANES:] = jnp.zeros((tr, LANES), dx_ref.dtype)
        _accumulate(da_ref, i == 0, jnp.sum(jnp.where(jnp.logical_and(live, is_g), d * g, 0.0), axis=0, keepdims=True))
        _accumulate(ddt_ref, i == 0, jnp.sum(d_alogit, axis=0, keepdims=True))

    return pl.pallas_call(
        body, name="gates_bwd",
        out_shape=(jax.ShapeDtypeStruct((n, 2 * LANES), MXU_DTYPE), jax.ShapeDtypeStruct((1, LANES), F32),
                   jax.ShapeDtypeStruct((1, LANES), F32)),
        grid=(n // tr,),
        in_specs=[pl.BlockSpec((tr, LANES), lambda i: (i, BA_COL)), _rows(tr, LANES), _vec(LANES), _vec(LANES)],
        out_specs=(_rows(tr, 2 * LANES), _vec(LANES), _vec(LANES)),
        compiler_params=_params("arbitrary"),
    )(proj, dbg, a_log_l, dt_bias_l)


HALO = 8


def _halo_scratch(rs):
    return pltpu.VMEM((rs + 2 * HALO, LANES), F32)


def _stage(ref, x):
    rs = x.shape[0]
    ref[0:HALO, :] = jnp.zeros((HALO, LANES), F32)
    ref[HALO + rs:, :] = jnp.zeros((HALO, LANES), F32)
    ref[HALO:HALO + rs, :] = x


def _shifted(ref, k, rs):
    return ref[pl.ds(HALO - k, rs), :]


def _causal_conv(x, x_staged, w, width):
    acc = w[width - 1:width, :] * x
    for i in range(width - 1):
        acc = acc + w[i:i + 1, :] * _shifted(x_staged, width - 1 - i, x.shape[0])
    return acc


def _anti_causal_conv(dy, dy_staged, w, width):
    acc = w[width - 1:width, :] * dy
    for i in range(width - 1):
        acc = acc + w[i:i + 1, :] * _shifted(dy_staged, -(width - 1 - i), dy.shape[0])
    return acc


def _conv_weight_grad(dy, x, x_staged, width):
    taps = [_shifted(x_staged, width - 1 - i, x.shape[0]) for i in range(width - 1)] + [x]
    return jnp.concatenate([jnp.sum(dy * tap, axis=0, keepdims=True) for tap in taps], axis=0)


def _seq_cols(rs, col0, heads):
    return pl.BlockSpec((rs, heads * LANES), lambda j, b: (b, col0 // heads + j))


def _tap_cols(width, col0, heads):
    return pl.BlockSpec((width, heads * LANES), lambda j, b: (0, col0 // heads + j))


def _lanes_of(h):
    return slice(h * LANES, (h + 1) * LANES)


def _qkv_fwd(proj, conv_w, kind, rs):
    n = proj.shape[0]
    col0 = {"q": 0, "k": HEADS, "v": 2 * HEADS}[kind]
    hb = HEADS

    def body(p_ref, w_ref, o_ref, staged):
        for h in range(hb):
            pre = p_ref[:, _lanes_of(h)]
            _stage(staged, pre)
            c = _causal_conv(pre, staged, w_ref[:, _lanes_of(h)], GDN_CONV)
            s = c * _sigmoid(c)
            if kind != "v":
                s = s * lax.rsqrt(jnp.sum(s * s, axis=-1, keepdims=True) + EPS)
            if kind == "q":
                s = s * (HEAD_DIM ** -0.5)
            o_ref[:, _lanes_of(h)] = s

    return pl.pallas_call(
        body, name="qkv_fwd_" + kind, out_shape=jax.ShapeDtypeStruct((n, GDN_WIDTH), F32), grid=(HEADS // hb, n // rs),
        in_specs=[_seq_cols(rs, col0, hb), _tap_cols(GDN_CONV, col0, hb)],
        out_specs=_seq_cols(rs, 0, hb), scratch_shapes=[_halo_scratch(rs)], compiler_params=_params("parallel", "parallel"),
    )(proj, conv_w)


def _qkv_bwd(dy, proj, conv_w, kind, rs):
    n = proj.shape[0]
    col0 = {"q": 0, "k": HEADS, "v": 2 * HEADS}[kind]
    hb = HEADS

    def body(dy_ref, p_ref, w_ref, dp_ref, dw_ref, pre_staged, dc_staged):
        for h in range(hb):
            lanes = _lanes_of(h)
            pre = p_ref[:, lanes]
            w = w_ref[:, lanes]
            _stage(pre_staged, pre)
            c = _causal_conv(pre, pre_staged, w, GDN_CONV)
            sg = _sigmoid(c)
            s = c * sg
            ds = dy_ref[:, lanes]
            if kind == "q":
                ds = ds * (HEAD_DIM ** -0.5)
            if kind != "v":
                r = lax.rsqrt(jnp.sum(s * s, axis=-1, keepdims=True) + EPS)
                sh = s * r
                ds = r * (ds - sh * jnp.sum(ds * sh, axis=-1, keepdims=True))
            dc = ds * _dsilu(c, sg)
            _stage(dc_staged, dc)
            dp_ref[:, lanes] = _anti_causal_conv(dc, dc_staged, w, GDN_CONV).astype(dp_ref.dtype)
            _accumulate(dw_ref.at[:, lanes], pl.program_id(1) == 0, _conv_weight_grad(dc, pre, pre_staged, GDN_CONV))

    return pl.pallas_call(
        body, name="qkv_bwd_" + kind,
        out_shape=(jax.ShapeDtypeStruct((n, GDN_WIDTH), MXU_DTYPE), jax.ShapeDtypeStruct((GDN_CONV, GDN_WIDTH), F32)),
        grid=(HEADS // hb, n // rs),
        in_specs=[_seq_cols(rs, 0, hb), _seq_cols(rs, col0, hb), _tap_cols(GDN_CONV, col0, hb)],
        out_specs=(_seq_cols(rs, 0, hb), _tap_cols(GDN_CONV, 0, hb)),
        scratch_shapes=[_halo_scratch(rs), _halo_scratch(rs)],
        compiler_params=_params("parallel", "arbitrary"),
    )(dy, proj, conv_w)


SC_COL = 4 * HEADS


def _sc_fwd(proj, conv_w, rs):
    n = proj.shape[0]

    hb = 2

    def body(x_ref, b_ref, c_ref, w_ref, y_ref, staged):
        for h in range(hb):
            lanes = _lanes_of(h)
            u = c_ref[:, lanes] * x_ref[:, lanes]
            _stage(staged, u)
            y_ref[:, lanes] = (b_ref[:, lanes] * _causal_conv(u, staged, w_ref[:, lanes], SC_CONV)).astype(y_ref.dtype)

    return pl.pallas_call(
        body, name="sc_fwd", out_shape=jax.ShapeDtypeStruct((n, SC_WIDTH), MXU_DTYPE), grid=(HEADS // hb, n // rs),
        in_specs=[_seq_cols(rs, SC_COL, hb), _seq_cols(rs, SC_COL + 4, hb), _seq_cols(rs, SC_COL + 8, hb),
                  _tap_cols(SC_CONV, 0, hb)],
        out_specs=_seq_cols(rs, 0, hb), scratch_shapes=[_halo_scratch(rs)], compiler_params=_params("parallel", "parallel"),
    )(proj, proj, proj, conv_w)


def _sc_bwd(dcat, proj, conv_w, rs):
    n = proj.shape[0]
    hb = 2

    def body(dy_ref, x_ref, b_ref, c_ref, w_ref, dx_ref, db_ref, dc_ref, dw_ref, u_staged, dcv_staged):
        for h in range(hb):
            lanes = _lanes_of(h)
            w = w_ref[:, lanes]
            x = x_ref[:, lanes]
            cc = c_ref[:, lanes]
            u = cc * x
            _stage(u_staged, u)
            dy = dy_ref[:, lanes]
            db_ref[:, lanes] = (dy * _causal_conv(u, u_staged, w, SC_CONV)).astype(db_ref.dtype)
            dcv = dy * b_ref[:, lanes]
            _stage(dcv_staged, dcv)
            du = _anti_causal_conv(dcv, dcv_staged, w, SC_CONV)
            dx_ref[:, lanes] = (du * cc).astype(dx_ref.dtype)
            dc_ref[:, lanes] = (du * x).astype(dc_ref.dtype)
            _accumulate(dw_ref.at[:, lanes], pl.program_id(1) == 0, _conv_weight_grad(dcv, u, u_staged, SC_CONV))

    piece = jax.ShapeDtypeStruct((n, SC_WIDTH), MXU_DTYPE)
    return pl.pallas_call(
        body, name="sc_bwd", out_shape=(piece, piece, piece, jax.ShapeDtypeStruct((SC_CONV, SC_WIDTH), F32)),
        grid=(HEADS // hb, n // rs),
        in_specs=[_seq_cols(rs, HEADS, hb), _seq_cols(rs, SC_COL, hb), _seq_cols(rs, SC_COL + 4, hb),
                  _seq_cols(rs, SC_COL + 8, hb), _tap_cols(SC_CONV, 0, hb)],
        out_specs=(_seq_cols(rs, 0, hb), _seq_cols(rs, 0, hb), _seq_cols(rs, 0, hb), _tap_cols(SC_CONV, 0, hb)),
        scratch_shapes=[_halo_scratch(rs), _halo_scratch(rs)],
        compiler_params=_params("parallel", "arbitrary"),
    )(dcat, proj, proj, proj, conv_w)


Z_COL = 3 * HEADS


def _gate_fwd(o, proj, gdn_norm, rs):
    n = proj.shape[0]

    hb = HEADS

    def body(o_ref, z_ref, w_ref, y_ref):
        for h in range(hb):
            lanes = _lanes_of(h)
            z = z_ref[:, lanes]
            y_ref[:, lanes] = (_rms_apply(o_ref[:, lanes], w_ref[...]) * z * _sigmoid(z)).astype(y_ref.dtype)

    return pl.pallas_call(
        body, name="gate_fwd", out_shape=jax.ShapeDtypeStruct((n, GDN_WIDTH), MXU_DTYPE), grid=(HEADS // hb, n // rs),
        in_specs=[_seq_cols(rs, 0, hb), _seq_cols(rs, Z_COL, hb), pl.BlockSpec((1, LANES), lambda j, b: (0, 0))],
        out_specs=_seq_cols(rs, 0, hb), compiler_params=_params("parallel", "parallel"),
    )(o, proj, gdn_norm)


def _gate_bwd(dcat, o, proj, gdn_norm, rs):
    n = proj.shape[0]
    hb = 2

    def body(dy_ref, o_ref, z_ref, w_ref, do_ref, dz_ref, dw_ref):
        w = w_ref[...]
        dw_step = jnp.zeros((1, LANES), F32)
        for h in range(hb):
            lanes = _lanes_of(h)
            z = z_ref[:, lanes]
            o = o_ref[:, lanes]
            dy = dy_ref[:, lanes]
            s = _sigmoid(z)
            dz_ref[:, lanes] = (dy * _rms_apply(o, w) * _dsilu(z, s)).astype(dz_ref.dtype)
            do, dw = _rms_bwd(o, w, dy * z * s)
            do_ref[:, lanes] = do
            dw_step = dw_step + dw
        _accumulate(dw_ref, jnp.logical_and(pl.program_id(0) == 0, pl.program_id(1) == 0), dw_step)

    return pl.pallas_call(
        body, name="gate_bwd",
        out_shape=(jax.ShapeDtypeStruct((n, GDN_WIDTH), F32), jax.ShapeDtypeStruct((n, GDN_WIDTH), MXU_DTYPE),
                   jax.ShapeDtypeStruct((1, LANES), F32)),
        grid=(HEADS // hb, n // rs),
        in_specs=[_seq_cols(rs, 0, hb), _seq_cols(rs, 0, hb), _seq_cols(rs, Z_COL, hb), pl.BlockSpec((1, LANES), lambda j, b: (0, 0))],
        out_specs=(_seq_cols(rs, 0, hb), _seq_cols(rs, 0, hb), pl.BlockSpec((1, LANES), lambda j, b: (0, 0))),
        compiler_params=_params("arbitrary", "arbitrary"),
    )(dcat, o, proj, gdn_norm)


def _dot(a, b):
    return jnp.dot(a.astype(MXU_DTYPE), b.astype(MXU_DTYPE), preferred_element_type=F32)


def _dot_nt(a, b):
    return lax.dot_general(a.astype(MXU_DTYPE), b.astype(MXU_DTYPE), (((1,), (1,)), ((), ())),
                           preferred_element_type=F32)


def _dot_tn(a, b):
    return lax.dot_general(a.astype(MXU_DTYPE), b.astype(MXU_DTYPE), (((0,), (0,)), ((), ())),
                           preferred_element_type=F32)


def _split(x):
    hi = x.astype(MXU_DTYPE)
    return hi, (x - hi.astype(F32)).astype(MXU_DTYPE)


def _dot_split(a, b):
    mm = functools.partial(jnp.dot, preferred_element_type=F32)
    return mm(a[0], b[0]) + (mm(a[0], b[1]) + mm(a[1], b[0]))


def _unit_lower_inverses(mats, eye):
    inv = [eye - a for a in mats]
    power = [_split(a) for a in mats]
    span = 2
    while span < CHUNK:
        power = [_split(_dot_split(p, p)) for p in power]
        inv = [i + _dot_split(_split(i), p) for i, p in zip(inv, power)]
        span *= 2
    return inv


def _chunk_masks():
    ii = lax.broadcasted_iota(jnp.int32, (CHUNK, CHUNK), 0)
    jj = lax.broadcasted_iota(jnp.int32, (CHUNK, CHUNK), 1)
    return ii, jj


def _chunk_decay(g_col, ii, jj):
    incl = ii >= jj
    g_row = jnp.sum(jnp.where(ii == jj, g_col, 0.0), axis=0, keepdims=True)
    gc_col = jnp.sum(jnp.where(incl, g_row, 0.0), axis=1, keepdims=True)
    gc_row = jnp.sum(jnp.where(ii <= jj, g_col, 0.0), axis=0, keepdims=True)
    g_total = jnp.sum(g_row, axis=1, keepdims=True)
    decay = jnp.where(incl, jnp.exp(jnp.where(incl, gc_col - gc_row, 0.0)), 0.0)
    return gc_col, g_total, decay


def _gdn_segments(rs, candidates):
    chunks = rs // CHUNK
    seg_chunks = _pick(chunks, candidates)
    return chunks, seg_chunks, chunks // seg_chunks


def _head_lanes(h):
    return slice(h * HEAD_DIM, (h + 1) * HEAD_DIM)


def _gdn_fwd(q, k, v, bg, rs, pieces):
    n = q.shape[0]
    batch = n // rs
    chunks, seg_chunks, segs = _gdn_segments(rs, (11, 8, 4, 2))
    seg_rows = seg_chunks * CHUNK
    chains = [(b, h) for b in range(batch) for h in range(HEADS)]
    each = lambda f, *lists: [f(*args) for args in zip(*lists)]
    count = len(pieces)

    def body(q_ref, k_ref, v_ref, bg_ref, *rest):
        w_refs, (o_ref, s_ref, t_ref), out_refs = rest[:count], rest[count:count + 3], rest[count + 3:2 * count + 3]
        state_ref, send_sems, recv_sems = rest[2 * count + 3:]
        gather = _gather_copies(w_refs, out_refs, send_sems, recv_sems)

        @pl.when(pl.program_id(0) == 0)
        def _():
            state_ref[...] = jnp.zeros_like(state_ref)
            for cp in gather[0]:
                cp.start()

        ii, jj = _chunk_masks()
        incl = ii >= jj
        eye = (ii == jj).astype(F32)

        def chunk(c, carry):
            rows = pl.ds(pl.multiple_of(c * CHUNK, CHUNK), CHUNK)
            bgc = [bg_ref[b, rows, :] for b in range(batch)]
            qc = [q_ref[b, rows, _head_lanes(h)] for b, h in chains]
            kc = [k_ref[b, rows, _head_lanes(h)] for b, h in chains]
            vc = [v_ref[b, rows, _head_lanes(h)] for b, h in chains]
            beta = [bgc[b][:, h:h + 1] for b, h in chains]
            state = [state_ref[b, h] for b, h in chains]
            dec = [_chunk_decay(bgc[b][:, HEADS + h:HEADS + h + 1], ii, jj) for b, h in chains]
            gc_col, g_total, decay = ([d[i] for d in dec] for i in range(3))
            kb = each(lambda x, y: x * y, kc, beta)
            a = each(lambda x, y, d: jnp.where(ii > jj, _dot_nt(x, y) * d, 0.0), kb, kc, decay)
            t_inv = _unit_lower_inverses(a, eye)
            eg = [jnp.exp(g) for g in gc_col]
            u = each(lambda t, x, y: _dot(t, x * y), t_inv, vc, beta)
            w = each(lambda t, x, e: _dot(t, x * e), t_inv, kb, eg)
            qk = each(lambda x, y, d: jnp.where(incl, _dot_nt(x, y) * d, 0.0), qc, kc, decay)
            v_new = each(lambda x, y, s: x - _dot(y, s), u, w, state)
            o = each(lambda x, e, s, m, vn: _dot(x * e, s) + _dot(m, vn), qc, eg, state, qk, v_new)
            new_state = each(lambda s, gt, x, g, vn: s * jnp.exp(gt) + _dot_tn(x * jnp.exp(gt - g), vn),
                             state, g_total, kc, gc_col, v_new)
            for i, (b, h) in enumerate(chains):
                s_ref[b, h, c] = state[i]
                t_ref[b, h, c] = t_inv[i]
                o_ref[b, rows, _head_lanes(h)] = o[i]
                state_ref[b, h] = new_state[i]
            return carry

        lax.fori_loop(0, seg_chunks, chunk, 0)

        @pl.when(pl.program_id(0) == segs - 1)
        def _():
            _gather_finish(gather)

    rows_spec = lambda width: pl.BlockSpec((batch, seg_rows, width), lambda s: (0, s, 0))
    per_chunk = lambda r, c: pl.BlockSpec((batch, HEADS, seg_chunks, r, c), lambda s: (0, 0, s, 0, 0))
    as_seqs = lambda a: a.reshape(batch, rs, a.shape[-1])
    sems = GATHER_SEMS * count
    o, states, t_invs, *gathered = pl.pallas_call(
        body, name="gdn_fwd",
        out_shape=(jax.ShapeDtypeStruct((batch, rs, GDN_WIDTH), F32),
                   jax.ShapeDtypeStruct((batch, HEADS, chunks, HEAD_DIM, HEAD_DIM), F32),
                   jax.ShapeDtypeStruct((batch, HEADS, chunks, CHUNK, CHUNK), F32))
        + tuple(jax.ShapeDtypeStruct((N_CHIPS,) + p.shape, p.dtype) for p in pieces),
        grid=(segs,),
        in_specs=[rows_spec(GDN_WIDTH), rows_spec(GDN_WIDTH), rows_spec(GDN_WIDTH), rows_spec(LANES)] + [_hbm()] * count,
        out_specs=(rows_spec(GDN_WIDTH), per_chunk(HEAD_DIM, HEAD_DIM), per_chunk(CHUNK, CHUNK)) + (_hbm(),) * count,
        scratch_shapes=[pltpu.VMEM((batch, HEADS, HEAD_DIM, HEAD_DIM), F32), pltpu.SemaphoreType.DMA((sems,)),
                        pltpu.SemaphoreType.DMA((sems,))],
        compiler_params=_params("arbitrary"),
    )(as_seqs(q), as_seqs(k), as_seqs(v), as_seqs(bg), *pieces)
    return o.reshape(n, GDN_WIDTH), states, t_invs, gathered


def _gdn_bwd(do, q, k, v, bg, states, t_invs, rs, parts):
    n = q.shape[0]
    batch = n // rs
    chunks, seg_chunks, segs = _gdn_segments(rs, (3, 4, 2))
    seg_rows = seg_chunks * CHUNK
    chains = [(b, h) for b in range(batch) for h in range(HEADS)]
    each = lambda f, *lists: [f(*args) for args in zip(*lists)]
    count = len(parts)

    def body(do_ref, q_ref, k_ref, v_ref, bg_ref, s_ref, t_ref, *rest):
        p_refs, (dq_ref, dk_ref, dv_ref, dbg_ref), got_refs = rest[:count], rest[count:count + 4], rest[count + 4:2 * count + 4]
        dstate_ref, send_sems, recv_sems = rest[2 * count + 4:]
        exchange = _chip_copies(p_refs, got_refs, send_sems, recv_sems)

        @pl.when(pl.program_id(0) == 0)
        def _():
            dstate_ref[...] = jnp.zeros_like(dstate_ref)
            for cp in exchange:
                cp.start()

        ii, jj = _chunk_masks()
        incl = ii >= jj
        strict = ii > jj
        lane = lax.broadcasted_iota(jnp.int32, (1, LANES), 1)

        def rowsum(x):
            return jnp.sum(x, axis=1, keepdims=True)

        def total(x):
            return jnp.sum(rowsum(x), axis=0, keepdims=True)

        def chunk(step, carry):
            c = seg_chunks - 1 - step
            rows = pl.ds(pl.multiple_of(c * CHUNK, CHUNK), CHUNK)
            bgc = [bg_ref[b, rows, :] for b in range(batch)]
            qc = [q_ref[b, rows, _head_lanes(h)] for b, h in chains]
            kc = [k_ref[b, rows, _head_lanes(h)] for b, h in chains]
            vc = [v_ref[b, rows, _head_lanes(h)] for b, h in chains]
            doc = [do_ref[b, rows, _head_lanes(h)] for b, h in chains]
            beta = [bgc[b][:, h:h + 1] for b, h in chains]
            state = [s_ref[b, h, c] for b, h in chains]
            t_inv = [t_ref[b, h, c] for b, h in chains]
            d_state = [dstate_ref[b, h] for b, h in chains]
            dec = [_chunk_decay(bgc[b][:, HEADS + h:HEADS + h + 1], ii, jj) for b, h in chains]
            gc_col, g_total, decay = ([d[i] for d in dec] for i in range(3))
            kb = each(lambda x, y: x * y, kc, beta)
            vb = each(lambda x, y: x * y, vc, beta)
            eg = [jnp.exp(g) for g in gc_col]
            kbg = each(lambda x, y: x * y, kb, eg)
            a = each(lambda x, y, d: jnp.where(strict, _dot_nt(x, y) * d, 0.0), kb, kc, decay)
            qk = each(lambda x, y, d: jnp.where(incl, _dot_nt(x, y) * d, 0.0), qc, kc, decay)
            w = each(_dot, t_inv, kbg)
            u = each(_dot, t_inv, vb)
            q_dec = each(lambda x, y: x * y, qc, eg)
            ek = each(lambda gt, g: jnp.exp(gt - g), g_total, gc_col)
            k_dec = each(lambda x, y: x * y, kc, ek)
            g_last = [jnp.exp(gt) for gt in g_total]
            v_new = each(lambda x, y, s: x - _dot(y, s), u, w, state)
            dv_new = each(lambda m, d, x, ds: _dot_tn(m, d) + _dot(x, ds), qk, doc, k_dec, d_state)
            dqk = each(lambda d, vn: jnp.where(incl, _dot_nt(d, vn), 0.0), doc, v_new)
            dq_dec = each(_dot_nt, doc, state)
            dk_dec = each(_dot_nt, v_new, d_state)
            dg_last = each(lambda s, ds: total(s * ds), state, d_state)
            new_d_state = each(lambda x, d, gl, ds, y, dvn: _dot_tn(x, d) + gl * ds - _dot_tn(y, dvn),
                               q_dec, doc, g_last, d_state, w, dv_new)
            dw = each(lambda dvn, s: -_dot_nt(dvn, s), dv_new, state)
            dt = each(lambda dvn, x, y, z: _dot_nt(dvn, x) + _dot_nt(y, z), dv_new, vb, dw, kbg)
            dvb = each(_dot_tn, t_inv, dv_new)
            dkbg = each(_dot_tn, t_inv, dw)
            t_dt = each(_dot_tn, t_inv, dt)
            da = each(lambda x, t: -jnp.where(strict, _dot_nt(x, t), 0.0), t_dt, t_inv)
            dm_a = each(lambda x, y: x * y, da, decay)
            dm_qk = each(lambda x, y: x * y, dqk, decay)
            e = each(lambda x, y, z, t: x * y + z * t, da, a, dqk, qk)
            dkb = each(lambda m, x, y, z: _dot(m, x) + y * z, dm_a, kc, dkbg, eg)
            dk = each(lambda m, x, m2, y, z, t, p, bt: _dot_tn(m, x) + _dot_tn(m2, y) + z * t + p * bt,
                      dm_a, kb, dm_qk, qc, dk_dec, ek, dkb, beta)
            dq = each(lambda m, x, y, z: _dot(m, x) + y * z, dm_qk, kc, dq_dec, eg)
            dbeta = each(lambda x, y, z, t: rowsum(x * y + z * t), dkb, kc, dvb, vc)
            dgc = each(lambda x, p, pd, r, rd, s, sd: rowsum(x) - rowsum(jnp.where(ii == jj, jnp.sum(x, axis=0, keepdims=True), 0.0))
                       + rowsum(p * pd - r * rd + s * sd), e, dq_dec, q_dec, dk_dec, k_dec, dkbg, kbg)
            d_total = each(lambda r, rd, x, gl: total(r * rd) + x * gl, dk_dec, k_dec, dg_last, g_last)
            dg = each(lambda x, t: rowsum(jnp.where(jj >= ii, jnp.sum(jnp.where(ii == jj, x, 0.0), axis=0, keepdims=True), 0.0)) + t,
                      dgc, d_total)
            dbg = [jnp.zeros((CHUNK, LANES), F32) for _ in range(batch)]
            for i, (b, h) in enumerate(chains):
                dstate_ref[b, h] = new_d_state[i]
                dk_ref[b, rows, _head_lanes(h)] = dk[i]
                dq_ref[b, rows, _head_lanes(h)] = dq[i]
                dv_ref[b, rows, _head_lanes(h)] = dvb[i] * beta[i]
                dbg[b] = dbg[b] + jnp.where(lane == h, dbeta[i], 0.0) + jnp.where(lane == HEADS + h, dg[i], 0.0)
            for b in range(batch):
                dbg_ref[b, rows, :] = dbg[b]
            return carry

        lax.fori_loop(0, seg_chunks, chunk, 0)

        @pl.when(pl.program_id(0) == segs - 1)
        def _():
            for cp in exchange:
                cp.wait_recv()
            for cp in exchange:
                cp.wait_send()

    rows_spec = lambda width: pl.BlockSpec((batch, seg_rows, width), lambda s: (0, segs - 1 - s, 0))
    per_chunk = lambda r, c: pl.BlockSpec((batch, HEADS, seg_chunks, r, c), lambda s: (0, 0, segs - 1 - s, 0, 0))
    as_seqs = lambda a: a.reshape(batch, rs, a.shape[-1])
    grad = jax.ShapeDtypeStruct((batch, rs, GDN_WIDTH), F32)
    wide = rows_spec(GDN_WIDTH)
    dq, dk, dv, dbg, *got = pl.pallas_call(
        body, name="gdn_bwd",
        out_shape=(grad, grad, grad, jax.ShapeDtypeStruct((batch, rs, LANES), F32))
        + tuple(jax.ShapeDtypeStruct((3,) + p.shape[1:], p.dtype) for p in parts),
        grid=(segs,),
        in_specs=[wide, wide, wide, wide, rows_spec(LANES), per_chunk(HEAD_DIM, HEAD_DIM), per_chunk(CHUNK, CHUNK)]
        + [_hbm()] * count,
        out_specs=(wide, wide, wide, rows_spec(LANES)) + (_hbm(),) * count,
        scratch_shapes=[pltpu.VMEM((batch, HEADS, HEAD_DIM, HEAD_DIM), F32), pltpu.SemaphoreType.DMA((3 * count,)),
                        pltpu.SemaphoreType.DMA((3 * count,))],
        compiler_params=_params("arbitrary"),
    )(as_seqs(do), as_seqs(q), as_seqs(k), as_seqs(v), as_seqs(bg), states, t_invs, *parts)
    return dq.reshape(n, GDN_WIDTH), dk.reshape(n, GDN_WIDTH), dv.reshape(n, GDN_WIDTH), dbg.reshape(n, LANES), got


def _lane_vec(vals, offset):
    k = vals.shape[1]
    return jnp.pad(vals, ((0, 0), (offset, LANES - offset - k)))


LATER = ("w_out", "w_gate", "w_up", "w_down")


def _halves(a):
    return a.reshape(a.shape[:-2] + (2, a.shape[-2] // 2, a.shape[-1]))


def _local_step(x, target, meta, norms, w_in_t, conv_qkv, a_log, dt_bias, gdn_norm, conv_sc, later_shards, core_arg):
    batch, seq, d = x.shape
    tokens = N_META + seq
    pad_rows = (-tokens) % CHUNK
    rs = tokens + pad_rows
    x_offset = pad_rows + N_META
    n = batch * rs
    w_mix_pre, w_mix_post, w_ffn_pre, w_ffn_post = norms

    head = jnp.concatenate([jnp.zeros((pad_rows, d), F32), meta], axis=0)
    h0 = jnp.concatenate([jnp.broadcast_to(head[None], (batch, x_offset, d)), x], axis=1).reshape(n, d)
    target_p = jnp.pad(target, ((0, 0), (x_offset, 0), (0, 0))).reshape(n, d)
    a_log_l = _lane_vec(a_log, HEADS)
    dt_bias_l = _lane_vec(dt_bias, HEADS)

    u1 = _rms_fwd(h0, w_mix_pre, "rms_mix_pre")
    proj = _mm(u1, w_in_t, "nt", F32, "mm_proj")
    q = _qkv_fwd(proj, conv_qkv, "q", rs)
    k = _qkv_fwd(proj, conv_qkv, "k", rs)
    v = _qkv_fwd(proj, conv_qkv, "v", rs)
    bg = _gates_fwd(proj, a_log_l, dt_bias_l, rs, pad_rows)
    o, states, t_invs, gathered = _gdn_fwd(q, k, v, bg, rs, later_shards)
    w_out, w_gate_t, w_up_t, w_down = (a.reshape(-1, d) for a in gathered)
    o_gated = _gate_fwd(o, proj, gdn_norm, rs)
    y_sc = _sc_fwd(proj, conv_sc, rs)
    cat = jnp.concatenate([o_gated, y_sc], axis=1)
    mix = _mm(cat, w_out, "nn", F32, "mm_mix")
    h1, u2 = _mix_residual(h0, mix, w_mix_post, w_ffn_pre)
    gate, up, act = _swiglu_fwd(u2, w_gate_t, w_up_t)
    ffn = _mm(act, w_down, "nn", F32, "mm_down")

    dh2, dffn, d_ffn_post, sq = _loss_head(h1, ffn, w_ffn_post, target_p, rs, x_offset)
    d_w_down = _mm(act, dffn, "tn", F32, "mm_dw_down")
    dgate, dup = _swiglu_bwd(dffn, w_down, gate, up)
    d_w_gate_t = _mm(dgate, u2, "tn", F32, "mm_dw_gate")
    d_w_up_t = _mm(dup, u2, "tn", F32, "mm_dw_up")
    du2 = _mm(dup, w_up_t, "nn", F32, "mm_du2_up", init=_mm(dgate, w_gate_t, "nn", F32, "mm_du2_gate"))
    by_chip = [_halves(g.reshape(N_CHIPS, -1, d)) for g in (d_w_gate_t, d_w_up_t, d_w_down)]
    dh1, dmix, d_ffn_pre, d_mix_post, got_sibling = _mid_bwd(h1, mix, w_mix_post, w_ffn_pre, dh2, du2, by_chip)
    dcat = _mm(dmix, w_out, "nt", F32, "mm_dcat")
    d_w_out = _halves(_mm(cat, dmix, "tn", F32, "mm_dw_out").reshape(N_CHIPS, -1, d))
    by_chip, got_sibling = [d_w_out] + by_chip, list(_exchange_siblings([d_w_out])) + got_sibling
    sums = [_add_sibling(a, b, core_arg, name) for name, a, b in zip(LATER, by_chip, got_sibling)]
    do, dz, d_gdn_norm = _gate_bwd(dcat, o, proj, gdn_norm, rs)
    dscx, dscb, dscc, d_conv_sc = _sc_bwd(dcat, proj, conv_sc, rs)
    dq, dk, dv, dbg, got_chips = _gdn_bwd(do, q, k, v, bg, states, t_invs, rs, [send for _, send in sums])
    dpq, dwq = _qkv_bwd(dq, proj, conv_qkv, "q", rs)
    dpk, dwk = _qkv_bwd(dk, proj, conv_qkv, "k", rs)
    dpv, dwv = _qkv_bwd(dv, proj, conv_qkv, "v", rs)
    d_conv_qkv = jnp.concatenate([dwq, dwk, dwv], axis=1)
    dba, d_a_log_l, d_dt_bias_l = _gates_bwd(proj, dbg, a_log_l, dt_bias_l, rs, pad_rows)
    dproj = jnp.concatenate([dpq, dpk, dpv, dz, dscx, dscb, dscc, dba], axis=1)
    g_in = _halves(_in_from_kernel_order(_mm(dproj, u1, "tn", F32, "mm_dw_in")))
    sums.insert(0, _add_sibling(g_in, _exchange_siblings([g_in])[0], core_arg, "w_in"))
    du1, got_in = _mm(dproj, w_in_t, "nn", F32, "mm_du1", exchange=[sums[0][1]])
    got_chips.insert(0, got_in)
    dh0, d_mix_pre = _in_bwd(h0, w_mix_pre, dh1, du1)

    dh0 = dh0.reshape(batch, rs, d)
    grads = dict(
        meta_tokens=jnp.sum(dh0[:, pad_rows:x_offset], axis=0),
        mix_pre_norm=d_mix_pre, mix_post_norm=d_mix_post, ffn_pre_norm=d_ffn_pre, ffn_post_norm=d_ffn_post,
        conv_qkv=d_conv_qkv,
        a_log=d_a_log_l[:, HEADS:2 * HEADS], dt_bias=d_dt_bias_l[:, HEADS:2 * HEADS],
        gdn_norm=d_gdn_norm, conv_sc=d_conv_sc,
    )
    return sq, dh0[:, x_offset:], grads, [(part, got) for (part, _), got in zip(sums, got_chips)]


MATRICES = ("w_in", "w_out", "w_gate", "w_up", "w_down")
IN_SHARD = IN_WIDTH // N_CHIPS
IN_SHARD_PAD = 928


def _in_to_kernel_order(by_chip):
    w_t = by_chip[:, :IN_SHARD].reshape(IN_WIDTH, by_chip.shape[-1])
    lo, hi = 4 * GDN_WIDTH, 4 * GDN_WIDTH + 2 * HEADS
    return jnp.concatenate([w_t[:lo], w_t[hi:], w_t[lo:hi], jnp.zeros((IN_PAD - IN_WIDTH, w_t.shape[1]), w_t.dtype)], axis=0)


def _in_from_kernel_order(g_t):
    lo, hi = 4 * GDN_WIDTH, IN_WIDTH - 2 * HEADS
    g = jnp.concatenate([g_t[:lo], g_t[hi:IN_WIDTH], g_t[lo:hi]], axis=0).reshape(N_CHIPS, IN_SHARD, g_t.shape[-1])
    return jnp.pad(g, ((0, 0), (0, IN_SHARD_PAD - IN_SHARD), (0, 0)))


PACK_LANES = 3 * GDN_WIDTH
PACKED = dict(mix_pre_norm=(0, 1, 0, D_MODEL), mix_post_norm=(1, 1, 0, D_MODEL), ffn_pre_norm=(2, 1, 0, D_MODEL),
              ffn_post_norm=(3, 1, 0, D_MODEL), a_log=(4, 1, 0, HEADS), dt_bias=(5, 1, 0, HEADS), loss=(6, 1, 0, 1),
              gdn_norm=(7, 1, 0, HEAD_DIM), conv_qkv=(8, GDN_CONV, 0, 3 * GDN_WIDTH), conv_sc=(0, SC_CONV, D_MODEL, SC_WIDTH),
              meta_tokens=(16, N_META, 0, D_MODEL))
PACK_ROWS = 32
SHARDED_SMALL = ("conv_qkv", "conv_sc", "meta_tokens")


def _pack_small(values):
    names = list(PACKED)

    def body(*refs):
        out_ref = refs[-1]
        out_ref[...] = jnp.zeros_like(out_ref)
        for name, ref in zip(names, refs):
            row, rows, lane0, lanes = PACKED[name]
            out_ref[row:row + rows, lane0:lane0 + lanes] = ref[...]

    return pl.pallas_call(body, name="pack_small", out_shape=jax.ShapeDtypeStruct((PACK_ROWS, PACK_LANES), F32))(
        *[values[name] for name in names])


def _sum_devices(packed_all, chip):
    names = list(PACKED)

    def body(chip_ref, all_ref, *rest):
        shard_refs, out_refs = rest[:len(SHARDED_SMALL)], rest[len(SHARDED_SMALL):]

        def total(ref, rows, lanes):
            acc = ref[0, rows, lanes]
            for k in range(1, 8):
                acc = acc + ref[k, rows, lanes]
            return acc

        for name, out in zip(names, out_refs):
            row, rows, lane0, lanes = PACKED[name]
            if name in SHARDED_SMALL:
                out[...] = total(shard_refs[SHARDED_SMALL.index(name)], slice(0, rows), slice(None))
            else:
                out[...] = total(all_ref, slice(row, row + rows), slice(lane0, lane0 + lanes))

    def shard_spec(name):
        row, rows, lane0, lanes = PACKED[name]
        height, width = max(rows, 8), lanes // N_CHIPS
        assert row % height == 0 and lane0 % width == 0
        return pl.BlockSpec((8, height, width), lambda i, chip_ref: (0, row // height, lane0 // width + chip_ref[0]))

    def out_shape(name):
        _, rows, _, lanes = PACKED[name]
        return jax.ShapeDtypeStruct((rows, lanes // N_CHIPS if name in SHARDED_SMALL else lanes), F32)

    whole = lambda shape: pl.BlockSpec(shape, lambda i, chip_ref: (0,) * len(shape))
    outs = pl.pallas_call(
        body, name="sum_devices", out_shape=tuple(out_shape(n) for n in names),
        grid_spec=pltpu.PrefetchScalarGridSpec(
            num_scalar_prefetch=1, grid=(1,),
            in_specs=[whole(packed_all.shape)] + [shard_spec(n) for n in SHARDED_SMALL],
            out_specs=tuple(whole(out_shape(n).shape) for n in names)),
    )(chip, packed_all, *[packed_all] * len(SHARDED_SMALL))
    return dict(zip(names, outs))


def _hbm():
    return pl.BlockSpec(memory_space=pl.ANY)


def _place():
    x, y, c = lax.axis_index("x"), lax.axis_index("y"), lax.axis_index("c")
    chips = ((1 - x, y), (x, 1 - y), (1 - x, 1 - y))
    return x, y, c, chips


def _remote(src, dst, send_sems, recv_sems, k, to):
    return pltpu.make_async_remote_copy(src_ref=src, dst_ref=dst, send_sem=send_sems.at[k], recv_sem=recv_sems.at[k],
                                        device_id=to, device_id_type=MESH)


GATHER_SEMS = 7


def _gather_copies(w_refs, out_refs, send_sems, recv_sems):
    x, y, c, chips = _place()
    mine = 2 * x + y
    sibling = (x, y, 1 - c)
    copy = functools.partial(_remote, send_sems=send_sems, recv_sems=recv_sems)
    direct, landed, passing, from_sibling = [], [], [], []
    for i, (w, o) in enumerate(zip(w_refs, out_refs)):
        k = GATHER_SEMS * i
        direct.append(copy(w, o.at[mine], k=k, to=sibling))
        from_sibling.append(copy(w, o.at[mine], k=k, to=sibling))
        for j, (cx, cy) in enumerate(chips):
            theirs = 2 * cx + cy
            direct.append(copy(w.at[c], o.at[mine, c], k=k + 1 + j, to=(cx, cy, c)))
            landed.append(copy(w.at[c], o.at[theirs, c], k=k + 1 + j, to=sibling))
            passing.append(copy(o.at[theirs, c], o.at[theirs, c], k=k + 4 + j, to=sibling))
            from_sibling.append(copy(w.at[c], o.at[theirs, 1 - c], k=k + 4 + j, to=sibling))
    return direct, landed, passing, from_sibling


def _gather_finish(copies):
    direct, landed, passing, from_sibling = copies
    for arrival, forward in zip(landed, passing):
        arrival.wait_recv()
        forward.start()
    for arrival in from_sibling:
        arrival.wait_recv()
    for cp in direct + passing:
        cp.wait_send()


def _gather_weights(pieces, smalls):
    count, extra = len(pieces), len(smalls)
    total = count + extra

    def body(*refs):
        w_refs, s_refs = refs[:count], refs[count:total]
        out_refs, sall_refs = refs[total:total + count], refs[total + count:2 * total]
        send_sems, recv_sems, local_sems = refs[2 * total:]
        x, y, c, chips = _place()
        mine = 2 * x + y
        own = [pltpu.make_async_copy(s, sall.at[mine], local_sems.at[i]) for i, (s, sall) in enumerate(zip(s_refs, sall_refs))]
        small = [_remote(s, sall.at[mine], send_sems, recv_sems, GATHER_SEMS * count + 3 * i + j, (cx, cy, c))
                 for i, (s, sall) in enumerate(zip(s_refs, sall_refs)) for j, (cx, cy) in enumerate(chips)]
        copies = _gather_copies(w_refs, out_refs, send_sems, recv_sems)
        for cp in own + small + copies[0]:
            cp.start()
        _gather_finish(copies)
        for cp in small:
            cp.wait_recv()
        for cp in small:
            cp.wait_send()
        for cp in own:
            cp.wait()

    sems = GATHER_SEMS * count + 3 * extra
    return pl.pallas_call(
        body, name="gather_weights",
        out_shape=tuple(jax.ShapeDtypeStruct((N_CHIPS,) + p.shape, p.dtype) for p in list(pieces) + list(smalls)),
        in_specs=[_hbm()] * total, out_specs=(_hbm(),) * total,
        scratch_shapes=[pltpu.SemaphoreType.DMA((sems,)), pltpu.SemaphoreType.DMA((sems,)), pltpu.SemaphoreType.DMA((extra,))],
    )(*pieces, *smalls)


def _sibling_copies(g_refs, got_refs, send_sems, recv_sems):
    x, y, c, _ = _place()
    return [_remote(g.at[:, 1 - c], got, send_sems, recv_sems, i, (x, y, 1 - c)) for i, (g, got) in enumerate(zip(g_refs, got_refs))]


def _exchange_siblings(grads, small=None):
    count = len(grads)
    extra = 0 if small is None else 1

    def body(*refs):
        g_refs = refs[:count]
        got_refs = refs[count + extra:2 * count + extra]
        send_sems, recv_sems = refs[2 * (count + extra):2 * (count + extra) + 2]
        x, y, c, _ = _place()
        copies = _sibling_copies(g_refs, got_refs, send_sems, recv_sems)
        if small is not None:
            s_ref, sall_ref, local_sem = refs[count], refs[2 * count + 1], refs[-1]
            me = 4 * x + 2 * y + c
            own = pltpu.make_async_copy(s_ref, sall_ref.at[me], local_sem)
            own.start()
            for k in range(7):
                dx, dy, dc = ((k + 1) >> 2) & 1, ((k + 1) >> 1) & 1, (k + 1) & 1
                peer = (1 - x if dx else x, 1 - y if dy else y, 1 - c if dc else c)
                copies.append(_remote(s_ref, sall_ref.at[me], send_sems, recv_sems, count + k, peer))
        for cp in copies:
            cp.start()
        for cp in copies:
            cp.wait_recv()
        for cp in copies:
            cp.wait_send()
        if small is not None:
            own.wait()

    sems = count + 7 * extra
    return pl.pallas_call(
        body, name="exchange_siblings" + ("" if small is None else "_small"),
        out_shape=tuple(jax.ShapeDtypeStruct((g.shape[0],) + g.shape[2:], F32) for g in grads)
        + (() if small is None else (jax.ShapeDtypeStruct((8,) + small.shape, F32),)),
        in_specs=[_hbm()] * (count + extra), out_specs=(_hbm(),) * (count + extra),
        scratch_shapes=[pltpu.SemaphoreType.DMA((sems,)), pltpu.SemaphoreType.DMA((sems,))]
        + ([] if small is None else [pltpu.SemaphoreType.DMA]),
    )(*grads, *(() if small is None else (small,)))


def _chip_copies(p_refs, got_refs, send_sems, recv_sems):
    x, y, c, chips = _place()
    return [_remote(p.at[2 * cx + cy], got.at[j], send_sems, recv_sems, 3 * i + j, (cx, cy, c))
            for i, (p, got) in enumerate(zip(p_refs, got_refs)) for j, (cx, cy) in enumerate(chips)]


def _share_halves(halves):
    count = len(halves)

    def body(*refs):
        h_refs, full_refs = refs[:count], refs[count:2 * count]
        send_sems, recv_sems = refs[2 * count:]
        x, y, c, _ = _place()
        copies = [pltpu.make_async_remote_copy(src_ref=h.at[c], dst_ref=full.at[c], send_sem=send_sems.at[i],
                                               recv_sem=recv_sems.at[i], device_id=(x, y, 1 - c), device_id_type=MESH)
                  for i, (h, full) in enumerate(zip(h_refs, full_refs))]
        for cp in copies:
            cp.start()
        for cp in copies:
            cp.wait_recv()
        for cp in copies:
            cp.wait_send()

    return pl.pallas_call(
        body, name="share_halves", out_shape=tuple(jax.ShapeDtypeStruct(h.shape, h.dtype) for h in halves),
        in_specs=[_hbm()] * count, out_specs=(_hbm(),) * count, input_output_aliases={i: i for i in range(count)},
        scratch_shapes=[pltpu.SemaphoreType.DMA((count,)), pltpu.SemaphoreType.DMA((count,))],
    )(*halves)


def _add_sibling(grad, got, core, name):
    chips, _, rows, cols = grad.shape

    def body(core_ref, g_ref, r_ref, sum_ref, send_ref):
        s = g_ref[...] + r_ref[...]
        sum_ref[...] = s
        send_ref[...] = s.astype(send_ref.dtype)

    block = pl.BlockSpec((None, rows, cols), lambda p, core_ref: (p, 0, 0))
    return pl.pallas_call(
        body, name="add_sibling_" + name,
        out_shape=(jax.ShapeDtypeStruct((chips, rows, cols), F32), jax.ShapeDtypeStruct((chips, rows, cols), BF16)),
        grid_spec=pltpu.PrefetchScalarGridSpec(
            num_scalar_prefetch=1, grid=(chips,),
            in_specs=[pl.BlockSpec((None, None, rows, cols), lambda p, core_ref: (p, core_ref[0], 0, 0)), block],
            out_specs=(block, block)),
        compiler_params=_params("parallel"),
    )(core, grad, got)


def _add_chips(part, got, chip_core, name):
    _, rows, cols = part.shape
    tr = rows // 2 if rows % 32 == 0 else rows

    def body(place_ref, p_ref, r_ref, o_ref):
        o_ref[...] = ((p_ref[...] + r_ref[0].astype(F32)) + r_ref[1].astype(F32)) + r_ref[2].astype(F32)

    return pl.pallas_call(
        body, name="add_chips_" + name, out_shape=jax.ShapeDtypeStruct((2, rows, cols), F32),
        grid_spec=pltpu.PrefetchScalarGridSpec(
            num_scalar_prefetch=1, grid=(rows // tr,),
            in_specs=[pl.BlockSpec((None, tr, cols), lambda i, place_ref: (place_ref[0], i, 0)),
                      pl.BlockSpec((3, tr, cols), lambda i, place_ref: (0, i, 0))],
            out_specs=pl.BlockSpec((None, tr, cols), lambda i, place_ref: (place_ref[1], i, 0))),
        compiler_params=_params("parallel"),
    )(chip_core, part, got)


def _adamw(w, g, m, v, name):
    rows, cols = w.shape
    tr = _pick(rows, (3592, 256, 352, 176, 128, 64, 32, 16, 8))

    def body(w_ref, g_ref, m_ref, v_ref, d_ref, nm_ref, nv_ref):
        g = g_ref[...]
        m = ADAM_B1 * m_ref[...] + (1.0 - ADAM_B1) * g
        v = ADAM_B2 * v_ref[...] + (1.0 - ADAM_B2) * (g * g)
        m_hat = m / (1.0 - ADAM_B1 ** ADAM_STEP)
        v_hat = v / (1.0 - ADAM_B2 ** ADAM_STEP)
        d_ref[...] = -ADAM_LR * (m_hat / (jnp.sqrt(v_hat) + ADAM_EPS) + ADAM_WD * w_ref[...])
        nm_ref[...] = m
        nv_ref[...] = v

    block = pl.BlockSpec((tr, cols), lambda i: (i, 0))
    shape = jax.ShapeDtypeStruct((rows, cols), F32)
    return pl.pallas_call(
        body, name="adamw_" + name, out_shape=(shape, shape, shape), grid=(rows // tr,),
        in_specs=[block] * 4, out_specs=(block,) * 3, compiler_params=_params("parallel"),
    )(w, g, m, v)


WEIGHTS = ("meta_tokens", "mix_pre_norm", "mix_post_norm", "ffn_pre_norm", "ffn_post_norm", "w_in", "conv_qkv", "a_log",
           "dt_bias", "gdn_norm", "conv_sc", "w_out", "w_gate", "w_up", "w_down")


def kernel(x, meta_tokens, mix_pre_norm, mix_post_norm, ffn_pre_norm, ffn_post_norm, w_in, conv_qkv, a_log, dt_bias, gdn_norm, conv_sc, w_out, w_gate, w_up, w_down, loss_target, m_meta_tokens, m_mix_pre_norm, m_mix_post_norm, m_ffn_pre_norm, m_ffn_post_norm, m_w_in, m_conv_qkv, m_a_log, m_dt_bias, m_gdn_norm, m_conv_sc, m_w_out, m_w_gate, m_w_up, m_w_down, v_meta_tokens, v_mix_pre_norm, v_mix_post_norm, v_ffn_pre_norm, v_ffn_post_norm, v_w_in, v_conv_qkv, v_a_log, v_dt_bias, v_gdn_norm, v_conv_sc, v_w_out, v_w_gate, v_w_up, v_w_down):
    d = x.shape[-1]
    two_d = lambda a: a.reshape(a.shape[-2:])
    weights = dict(zip(WEIGHTS, (meta_tokens, mix_pre_norm, mix_post_norm, ffn_pre_norm, ffn_post_norm, w_in, conv_qkv, a_log,
                                 dt_bias, gdn_norm, conv_sc, w_out, w_gate, w_up, w_down)))
    m_in = dict(zip(WEIGHTS, (m_meta_tokens, m_mix_pre_norm, m_mix_post_norm, m_ffn_pre_norm, m_ffn_post_norm, m_w_in, m_conv_qkv,
                              m_a_log, m_dt_bias, m_gdn_norm, m_conv_sc, m_w_out, m_w_gate, m_w_up, m_w_down)))
    v_in = dict(zip(WEIGHTS, (v_meta_tokens, v_mix_pre_norm, v_mix_post_norm, v_ffn_pre_norm, v_ffn_post_norm, v_w_in, v_conv_qkv,
                              v_a_log, v_dt_bias, v_gdn_norm, v_conv_sc, v_w_out, v_w_gate, v_w_up, v_w_down)))
    core = lax.axis_index("c")
    chip = 2 * lax.axis_index("x") + lax.axis_index("y")
    core_arg = core.reshape(1).astype(jnp.int32)
    chip_core = jnp.stack([chip, core]).astype(jnp.int32)
    whole = lambda a: a.reshape(a.shape[:-3] + (2 * a.shape[-2], d))
    by_rows = lambda n, a: two_d(a).T if n in ("w_in", "w_gate", "w_up") else two_d(a)

    shard = {n: by_rows(n, weights[n]).astype(MXU_DTYPE) for n in MATRICES}
    shard["w_in"] = jnp.pad(shard["w_in"], ((0, IN_SHARD_PAD - IN_SHARD), (0, 0)))
    w_in_all, *small_all = _gather_weights([_halves(shard["w_in"])], [two_d(weights[n]) for n in SHARDED_SMALL])
    w_in_t = _in_to_kernel_order(whole(w_in_all))
    conv_qkv_full, conv_sc_full, meta_full = (jnp.concatenate([a[p] for p in range(N_CHIPS)], axis=1) for a in small_all)

    sq, grad_x, g, sums = _local_step(
        x, loss_target, meta_full, (mix_pre_norm, mix_post_norm, ffn_pre_norm, ffn_post_norm), w_in_t, conv_qkv_full, a_log,
        dt_bias, gdn_norm, conv_sc_full, [_halves(shard[n]) for n in LATER], core_arg)

    (packed_all,) = _exchange_siblings([], _pack_small(dict(g, loss=sq)))
    totals = [_add_chips(part, got, chip_core, n) for n, (part, got) in zip(MATRICES, sums)]
    grads = {n: whole(a) for n, a in zip(MATRICES, _share_halves(totals))}
    grads["w_in"] = grads["w_in"][:IN_SHARD]
    grads.update(_sum_devices(packed_all, chip.reshape(1).astype(jnp.int32)))
    loss = (0.5 / d) * grads.pop("loss")[0, 0]

    outs = [[], [], [], []]
    for n in WEIGHTS:
        shape = weights[n].shape
        delta, new_m, new_v = _adamw(by_rows(n, weights[n]), grads[n], by_rows(n, m_in[n]), by_rows(n, v_in[n]), n)
        for out, a in zip(outs, (grads[n], delta, new_m, new_v)):
            out.append((a.T if n in ("w_in", "w_gate", "w_up") else a).reshape(shape))
    return (loss, grad_x, *outs[0], *outs[1], *outs[2], *outs[3])
```

```python
import functools

import jax
import jax.numpy as jnp
from jax import lax
from jax.experimental import pallas as pl
from jax.experimental.pallas import tpu as pltpu

F32 = jnp.float32
BF16 = jnp.bfloat16
MXU_DTYPE = jnp.bfloat16
MESH = pl.DeviceIdType.MESH

D_MODEL = 1024
N_META = 16
HEADS = 4
HEAD_DIM = 128
GDN_WIDTH = HEADS * HEAD_DIM
GDN_CONV = 4
CHUNK = 64
SC_WIDTH = D_MODEL - GDN_WIDTH
SC_CONV = 3
D_FF = 2816
IN_WIDTH = 4 * GDN_WIDTH + 2 * HEADS + 3 * SC_WIDTH
IN_PAD = 3840
BA_COL = (4 * GDN_WIDTH + 3 * SC_WIDTH) // 128
EPS = 1e-6
LANES = 128
N_CHIPS = 4
VMEM_LIMIT = 48 * 2 ** 20

ADAM_LR = 0.001
ADAM_B1 = 0.9
ADAM_B2 = 0.999
ADAM_EPS = 1e-08
ADAM_WD = 0.01
ADAM_STEP = 10


def _pick(n, candidates):
    for c in candidates:
        if n % c == 0:
            return c
    return n


def _row_tile(n):
    return _pick(n, (352, 256, 176, 128, 64, 32, 16, 8))


def _params(*sem):
    return pltpu.CompilerParams(dimension_semantics=sem, vmem_limit_bytes=VMEM_LIMIT)


def _sigmoid(x):
    return 0.5 * jnp.tanh(0.5 * x) + 0.5


def _softplus(x):
    return jnp.maximum(x, 0.0) + jnp.log(1.0 + jnp.exp(-jnp.abs(x)))


def _dsilu(x, s):
    return s * (1.0 + x * (1.0 - s))


def _mm(a, b, mode, out_dtype, name, init=None, exchange=None):
    if mode == "tn":
        k_dim, m_dim = a.shape
    else:
        m_dim, k_dim = a.shape
    n_dim = b.shape[0] if mode == "nt" else b.shape[1]
    rows = (1056, 1024, 704, 512, 256, 128) if init is not None else (2112, 1056, 1024, 704, 512, 256, 128)
    tm = _pick(m_dim, (1408, 1280, 1024, 512, 256, 128) if mode == "tn" else rows)
    tn = _pick(n_dim, (1408, 1280, 1024, 768, 512, 256, 128))
    tk = _pick(k_dim, (1408, 1280, 1056, 1024, 512, 256, 128))
    nk = k_dim // tk
    if mode == "nn":
        a_spec = pl.BlockSpec((tm, tk), lambda i, j, k: (i, k))
        b_spec = pl.BlockSpec((tk, tn), lambda i, j, k: (k, j))
        dims = (((1,), (0,)), ((), ()))
    elif mode == "nt":
        a_spec = pl.BlockSpec((tm, tk), lambda i, j, k: (i, k))
        b_spec = pl.BlockSpec((tn, tk), lambda i, j, k: (j, k))
        dims = (((1,), (1,)), ((), ()))
    else:
        a_spec = pl.BlockSpec((tk, tm), lambda i, j, k: (k, i))
        b_spec = pl.BlockSpec((tk, tn), lambda i, j, k: (k, j))
        dims = (((0,), (0,)), ((), ()))

    out_spec = pl.BlockSpec((tm, tn), lambda i, j, k: (i, j))
    grid = (m_dim // tm, n_dim // tn, nk)
    parts = () if exchange is None else tuple(exchange)
    count = len(parts)
    first_in = 2 if init is None else 3

    assert out_dtype == F32

    def body(a_ref, b_ref, *rest):
        o_ref = rest[first_in - 2 + count]
        k = pl.program_id(2)
        step = (pl.program_id(0) * grid[1] + pl.program_id(1)) * nk + k
        if count:
            copies = _chip_copies(rest[first_in - 2:first_in - 2 + count], rest[first_in - 1 + count:first_in - 1 + 2 * count],
                                  *rest[first_in - 1 + 2 * count:])

            @pl.when(step == 0)
            def _():
                for cp in copies:
                    cp.start()

        p = lax.dot_general(a_ref[...], b_ref[...], dims, preferred_element_type=F32)
        if nk == 1:
            o_ref[...] = p if init is None else rest[0][...] + p
        else:
            @pl.when(k == 0)
            def _():
                o_ref[...] = p if init is None else rest[0][...] + p

            @pl.when(k > 0)
            def _():
                o_ref[...] += p

        if count:
            @pl.when(step == grid[0] * grid[1] * nk - 1)
            def _():
                for cp in copies:
                    cp.wait_recv()
                for cp in copies:
                    cp.wait_send()

    out = pl.pallas_call(
        body, name=name,
        out_shape=(jax.ShapeDtypeStruct((m_dim, n_dim), out_dtype),)
        + tuple(jax.ShapeDtypeStruct((3,) + p.shape[1:], p.dtype) for p in parts),
        grid=grid,
        in_specs=[a_spec, b_spec] + ([] if init is None else [out_spec]) + [_hbm()] * count,
        out_specs=(out_spec,) + (_hbm(),) * count,
        scratch_shapes=[pltpu.SemaphoreType.DMA((3 * count,)), pltpu.SemaphoreType.DMA((3 * count,))] if count else [],
        compiler_params=_params(*(("arbitrary",) * 3 if count else ("parallel", "parallel", "arbitrary"))),
    )(a, b, *(() if init is None else (init,)), *parts)
    return out[0] if not count else out


def _rms_apply(x, w):
    r = lax.rsqrt(jnp.mean(x * x, axis=-1, keepdims=True) + EPS)
    return x * r * w


def _rms_bwd(x, w, dy):
    r = lax.rsqrt(jnp.mean(x * x, axis=-1, keepdims=True) + EPS)
    xh = x * r
    dyw = dy * w
    dx = r * (dyw - xh * jnp.mean(dyw * xh, axis=-1, keepdims=True))
    return dx, jnp.sum(dy * xh, axis=0, keepdims=True)


def _accumulate(ref, first, value):
    @pl.when(first)
    def _():
        ref[...] = value

    @pl.when(jnp.logical_not(first))
    def _():
        ref[...] += value


def _rows(tr, width):
    return pl.BlockSpec((tr, width), lambda i: (i, 0))


def _vec(width):
    return pl.BlockSpec((1, width), lambda i: (0, 0))


def _rms_fwd(h, w, name):
    n, d = h.shape
    tr = _row_tile(n)

    def body(h_ref, w_ref, u_ref):
        u_ref[...] = _rms_apply(h_ref[...], w_ref[...]).astype(u_ref.dtype)

    return pl.pallas_call(
        body, name=name, out_shape=jax.ShapeDtypeStruct((n, d), MXU_DTYPE), grid=(n // tr,),
        in_specs=[_rows(tr, d), _vec(d)], out_specs=_rows(tr, d), compiler_params=_params("parallel"),
    )(h, w)


def _mix_residual(h0, mix, w_post, w_pre):
    n, d = h0.shape
    tr = _row_tile(n)

    def body(h0_ref, mix_ref, wpost_ref, wpre_ref, h1_ref, u2_ref):
        h1 = h0_ref[...] + _rms_apply(mix_ref[...], wpost_ref[...])
        h1_ref[...] = h1
        u2_ref[...] = _rms_apply(h1, wpre_ref[...]).astype(u2_ref.dtype)

    return pl.pallas_call(
        body, name="mix_residual",
        out_shape=(jax.ShapeDtypeStruct((n, d), F32), jax.ShapeDtypeStruct((n, d), MXU_DTYPE)), grid=(n // tr,),
        in_specs=[_rows(tr, d), _rows(tr, d), _vec(d), _vec(d)], out_specs=(_rows(tr, d), _rows(tr, d)),
        compiler_params=_params("parallel"),
    )(h0, mix, w_post, w_pre)


NT_DIMS = (((1,), (1,)), ((), ()))


def _ffn_tiles(n):
    return _pick(n, (704, 512, 256, 128)), _pick(D_FF, (1408, 256, 128))


def _swiglu_fwd(u, w_gate_t, w_up_t, w_next):
    n, d = u.shape
    tm, tn = _ffn_tiles(n)
    grid = (D_FF // tn, n // tm)

    def body(u_ref, wg_ref, wu_ref, wn_ref, g_ref, up_ref, act_ref, wall_ref, send_sems, recv_sems):
        gather = _gather_copies([wn_ref], [wall_ref], send_sems, recv_sems)
        step = pl.program_id(0) * grid[1] + pl.program_id(1)

        @pl.when(step == 0)
        def _():
            for cp in gather[0]:
                cp.start()

        a = u_ref[...]
        g = lax.dot_general(a, wg_ref[...], NT_DIMS, preferred_element_type=F32)
        up = lax.dot_general(a, wu_ref[...], NT_DIMS, preferred_element_type=F32)
        g_ref[...] = g
        up_ref[...] = up
        act_ref[...] = (g * _sigmoid(g) * up).astype(act_ref.dtype)

        @pl.when(step == grid[0] * grid[1] - 1)
        def _():
            _gather_finish(gather)

    tile = pl.BlockSpec((tm, tn), lambda j, i: (i, j))
    weight = pl.BlockSpec((tn, d), lambda j, i: (j, 0))
    wide = jax.ShapeDtypeStruct((n, D_FF), F32)
    return pl.pallas_call(
        body, name="swiglu_fwd",
        out_shape=(wide, wide, jax.ShapeDtypeStruct((n, D_FF), MXU_DTYPE),
                   jax.ShapeDtypeStruct((N_CHIPS,) + w_next.shape, w_next.dtype)),
        grid=grid,
        in_specs=[pl.BlockSpec((tm, d), lambda j, i: (i, 0)), weight, weight, _hbm()], out_specs=(tile, tile, tile, _hbm()),
        scratch_shapes=[pltpu.SemaphoreType.DMA((GATHER_SEMS,)), pltpu.SemaphoreType.DMA((GATHER_SEMS,))],
        compiler_params=_params("arbitrary", "arbitrary"),
    )(u, w_gate_t, w_up_t, w_next)


def _swiglu_bwd(dffn, w_down, gate, up):
    n, d = dffn.shape
    tm, tn = _ffn_tiles(n)

    def body(dy_ref, w_ref, g_ref, u_ref, dg_ref, du_ref):
        da = lax.dot_general(dy_ref[...], w_ref[...], NT_DIMS, preferred_element_type=F32)
        g = g_ref[...]
        s = _sigmoid(g)
        dg_ref[...] = (da * u_ref[...] * _dsilu(g, s)).astype(dg_ref.dtype)
        du_ref[...] = (da * g * s).astype(du_ref.dtype)

    tile = pl.BlockSpec((tm, tn), lambda j, i: (i, j))
    shape = jax.ShapeDtypeStruct((n, D_FF), MXU_DTYPE)
    return pl.pallas_call(
        body, name="swiglu_bwd", out_shape=(shape, shape), grid=(D_FF // tn, n // tm),
        in_specs=[pl.BlockSpec((tm, d), lambda j, i: (i, 0)), pl.BlockSpec((tn, d), lambda j, i: (j, 0)), tile, tile],
        out_specs=(tile, tile), compiler_params=_params("parallel", "parallel"),
    )(dffn, w_down, gate, up)


def _loss_head(h1, ffn, w_post, target, rows_per_seq, x_offset):
    n, d = h1.shape
    tr = _row_tile(rows_per_seq)
    tiles_per_seq = rows_per_seq // tr

    def body(h1_ref, ffn_ref, w_ref, t_ref, dh2_ref, dffn_ref, dw_ref, sq_ref):
        i = pl.program_id(0)
        w = w_ref[...]
        f = ffn_ref[...]
        r = lax.rsqrt(jnp.mean(f * f, axis=-1, keepdims=True) + EPS)
        fh = f * r
        row = lax.rem(i, tiles_per_seq) * tr + lax.broadcasted_iota(jnp.int32, (tr, 1), 0)
        err = jnp.where(row >= x_offset, h1_ref[...] + fh * w - t_ref[...], 0.0)
        dh2 = err * (1.0 / d)
        dh2_ref[...] = dh2
        dyw = dh2 * w
        dffn_ref[...] = (r * (dyw - fh * jnp.mean(dyw * fh, axis=-1, keepdims=True))).astype(dffn_ref.dtype)
        _accumulate(dw_ref, i == 0, jnp.sum(dh2 * fh, axis=0, keepdims=True))
        _accumulate(sq_ref, i == 0, jnp.sum(jnp.sum(err * err, axis=1, keepdims=True), axis=0, keepdims=True))

    return pl.pallas_call(
        body, name="loss_head",
        out_shape=(jax.ShapeDtypeStruct((n, d), F32), jax.ShapeDtypeStruct((n, d), MXU_DTYPE),
                   jax.ShapeDtypeStruct((1, d), F32), jax.ShapeDtypeStruct((1, 1), F32)),
        grid=(n // tr,),
        in_specs=[_rows(tr, d), _rows(tr, d), _vec(d), _rows(tr, d)],
        out_specs=(_rows(tr, d), _rows(tr, d), _vec(d), _vec(1)),
        compiler_params=_params("arbitrary"),
    )(h1, ffn, w_post, target)


def _mid_bwd(h1, mix, w_mix_post, w_ffn_pre, dh2, du2, grads):
    n, d = h1.shape
    tr = _row_tile(n)
    count = len(grads)

    def body(h1_ref, mix_ref, wpost_ref, wpre_ref, dh2_ref, du2_ref, *rest):
        g_refs, (dh1_ref, dmix_ref, dwpre_ref, dwpost_ref), got_refs = rest[:count], rest[count:count + 4], rest[count + 4:2 * count + 4]
        exchange = _sibling_copies(g_refs, got_refs, *rest[2 * count + 4:])
        i = pl.program_id(0)

        @pl.when(i == 0)
        def _():
            for cp in exchange:
                cp.start()

        dx, dwpre = _rms_bwd(h1_ref[...], wpre_ref[...], du2_ref[...])
        dh1 = dh2_ref[...] + dx
        dh1_ref[...] = dh1
        dmix, dwpost = _rms_bwd(mix_ref[...], wpost_ref[...], dh1)
        dmix_ref[...] = dmix.astype(dmix_ref.dtype)
        _accumulate(dwpre_ref, i == 0, dwpre)
        _accumulate(dwpost_ref, i == 0, dwpost)

        @pl.when(i == n // tr - 1)
        def _():
            for cp in exchange:
                cp.wait_recv()
            for cp in exchange:
                cp.wait_send()

    dh1, dmix, dwpre, dwpost, *got = pl.pallas_call(
        body, name="mid_bwd",
        out_shape=(jax.ShapeDtypeStruct((n, d), F32), jax.ShapeDtypeStruct((n, d), MXU_DTYPE),
                   jax.ShapeDtypeStruct((1, d), F32), jax.ShapeDtypeStruct((1, d), F32))
        + tuple(jax.ShapeDtypeStruct((g.shape[0],) + g.shape[2:], F32) for g in grads),
        grid=(n // tr,),
        in_specs=[_rows(tr, d), _rows(tr, d), _vec(d), _vec(d), _rows(tr, d), _rows(tr, d)] + [_hbm()] * count,
        out_specs=(_rows(tr, d), _rows(tr, d), _vec(d), _vec(d)) + (_hbm(),) * count,
        scratch_shapes=[pltpu.SemaphoreType.DMA((count,)), pltpu.SemaphoreType.DMA((count,))],
        compiler_params=_params("arbitrary"),
    )(h1, mix, w_mix_post, w_ffn_pre, dh2, du2, *grads)
    return dh1, dmix, dwpre, dwpost, got


def _in_bwd(h0, w_pre, dh1, du1):
    n, d = h0.shape
    tr = _row_tile(n)

    def body(h0_ref, w_ref, dh1_ref, du1_ref, dh0_ref, dw_ref):
        dx, dw = _rms_bwd(h0_ref[...], w_ref[...], du1_ref[...])
        dh0_ref[...] = dh1_ref[...] + dx
        _accumulate(dw_ref, pl.program_id(0) == 0, dw)

    return pl.pallas_call(
        body, name="in_bwd",
        out_shape=(jax.ShapeDtypeStruct((n, d), F32), jax.ShapeDtypeStruct((1, d), F32)), grid=(n // tr,),
        in_specs=[_rows(tr, d), _vec(d), _rows(tr, d), _rows(tr, d)], out_specs=(_rows(tr, d), _vec(d)),
        compiler_params=_params("arbitrary"),
    )(h0, w_pre, dh1, du1)


def _lane_is(lo, hi):
    lane = lax.broadcasted_iota(jnp.int32, (1, LANES), 1)
    return jnp.logical_and(lane >= lo, lane < hi)


def _gates_fwd(proj, a_log_l, dt_bias_l, rows_per_seq, pad_rows):
    n = proj.shape[0]
    tr = _row_tile(rows_per_seq)
    tiles_per_seq = rows_per_seq // tr

    def body(p_ref, a_ref, dt_ref, o_ref):
        x = p_ref[...]
        row = lax.rem(pl.program_id(0), tiles_per_seq) * tr + lax.broadcasted_iota(jnp.int32, (tr, 1), 0)
        g = -jnp.exp(a_ref[...]) * _softplus(x + dt_ref[...])
        val = jnp.where(_lane_is(0, HEADS), _sigmoid(x), jnp.where(_lane_is(HEADS, 2 * HEADS), g, 0.0))
        o_ref[...] = jnp.where(row >= pad_rows, val, 0.0)

    return pl.pallas_call(
        body, name="gates_fwd", out_shape=jax.ShapeDtypeStruct((n, LANES), F32), grid=(n // tr,),
        in_specs=[pl.BlockSpec((tr, LANES), lambda i: (i, BA_COL)), _vec(LANES), _vec(LANES)],
        out_specs=_rows(tr, LANES), compiler_params=_params("parallel"),
    )(proj, a_log_l, dt_bias_l)


def _gates_bwd(proj, dbg, a_log_l, dt_bias_l, rows_per_seq, pad_rows):
    n = proj.shape[0]
    tr = _row_tile(rows_per_seq)
    tiles_per_seq = rows_per_seq // tr

    def body(p_ref, d_ref, a_ref, dt_ref, dx_ref, da_ref, ddt_ref):
        i = pl.program_id(0)
        x = p_ref[...]
        d = d_ref[...]
        row = lax.rem(i, tiles_per_seq) * tr + lax.broadcasted_iota(jnp.int32, (tr, 1), 0)
        live = row >= pad_rows
        beta = _sigmoid(x)
        ea = jnp.exp(a_ref[...])
        xa = x + dt_ref[...]
        g = -ea * _softplus(xa)
        is_g = _lane_is(HEADS, 2 * HEADS)
        d_alogit = jnp.where(jnp.logical_and(live, is_g), d * (-ea) * _sigmoid(xa), 0.0)
        d_blogit = jnp.where(jnp.logical_and(live, _lane_is(0, HEADS)), d * beta * (1.0 - beta), 0.0)
        dx_ref[:, :LANES] = (d_alogit + d_blogit).astype(dx_ref.dtype)
        dx_ref[:, LANES:] = jnp.zeros((tr, LANES), dx_ref.dtype)
        _accumulate(da_ref, i == 0, jnp.sum(jnp.where(jnp.logical_and(live, is_g), d * g, 0.0), axis=0, keepdims=True))
        _accumulate(ddt_ref, i == 0, jnp.sum(d_alogit, axis=0, keepdims=True))

    return pl.pallas_call(
        body, name="gates_bwd",
        out_shape=(jax.ShapeDtypeStruct((n, 2 * LANES), MXU_DTYPE), jax.ShapeDtypeStruct((1, LANES), F32),
                   jax.ShapeDtypeStruct((1, LANES), F32)),
        grid=(n // tr,),
        in_specs=[pl.BlockSpec((tr, LANES), lambda i: (i, BA_COL)), _rows(tr, LANES), _vec(LANES), _vec(LANES)],
        out_specs=(_rows(tr, 2 * LANES), _vec(LANES), _vec(LANES)),
        compiler_params=_params("arbitrary"),
    )(proj, dbg, a_log_l, dt_bias_l)


HALO = 8


def _halo_scratch(rs):
    return pltpu.VMEM((rs + 2 * HALO, LANES), F32)


def _stage(ref, x):
    rs = x.shape[0]
    ref[0:HALO, :] = jnp.zeros((HALO, LANES), F32)
    ref[HALO + rs:, :] = jnp.zeros((HALO, LANES), F32)
    ref[HALO:HALO + rs, :] = x


def _shifted(ref, k, rs):
    return ref[pl.ds(HALO - k, rs), :]


def _causal_conv(x, x_staged, w, width):
    acc = w[width - 1:width, :] * x
    for i in range(width - 1):
        acc = acc + w[i:i + 1, :] * _shifted(x_staged, width - 1 - i, x.shape[0])
    return acc


def _anti_causal_conv(dy, dy_staged, w, width):
    acc = w[width - 1:width, :] * dy
    for i in range(width - 1):
        acc = acc + w[i:i + 1, :] * _shifted(dy_staged, -(width - 1 - i), dy.shape[0])
    return acc


def _conv_weight_grad(dy, x, x_staged, width):
    taps = [_shifted(x_staged, width - 1 - i, x.shape[0]) for i in range(width - 1)] + [x]
    return jnp.concatenate([jnp.sum(dy * tap, axis=0, keepdims=True) for tap in taps], axis=0)


def _seq_cols(rs, col0, heads):
    return pl.BlockSpec((rs, heads * LANES), lambda j, b: (b, col0 // heads + j))


def _tap_cols(width, col0, heads):
    return pl.BlockSpec((width, heads * LANES), lambda j, b: (0, col0 // heads + j))


def _lanes_of(h):
    return slice(h * LANES, (h + 1) * LANES)


def _qkv_fwd(proj, conv_w, kind, rs):
    n = proj.shape[0]
    col0 = {"q": 0, "k": HEADS, "v": 2 * HEADS}[kind]
    hb = HEADS

    def body(p_ref, w_ref, o_ref, staged):
        for h in range(hb):
            pre = p_ref[:, _lanes_of(h)]
            _stage(staged, pre)
            c = _causal_conv(pre, staged, w_ref[:, _lanes_of(h)], GDN_CONV)
            s = c * _sigmoid(c)
            if kind != "v":
                s = s * lax.rsqrt(jnp.sum(s * s, axis=-1, keepdims=True) + EPS)
            if kind == "q":
                s = s * (HEAD_DIM ** -0.5)
            o_ref[:, _lanes_of(h)] = s

    return pl.pallas_call(
        body, name="qkv_fwd_" + kind, out_shape=jax.ShapeDtypeStruct((n, GDN_WIDTH), F32), grid=(HEADS // hb, n // rs),
        in_specs=[_seq_cols(rs, col0, hb), _tap_cols(GDN_CONV, col0, hb)],
        out_specs=_seq_cols(rs, 0, hb), scratch_shapes=[_halo_scratch(rs)], compiler_params=_params("parallel", "parallel"),
    )(proj, conv_w)


def _qkv_bwd(dy, proj, conv_w, kind, rs):
    n = proj.shape[0]
    col0 = {"q": 0, "k": HEADS, "v": 2 * HEADS}[kind]
    hb = HEADS

    def body(dy_ref, p_ref, w_ref, dp_ref, dw_ref, pre_staged, dc_staged):
        for h in range(hb):
            lanes = _lanes_of(h)
            pre = p_ref[:, lanes]
            w = w_ref[:, lanes]
            _stage(pre_staged, pre)
            c = _causal_conv(pre, pre_staged, w, GDN_CONV)
            sg = _sigmoid(c)
            s = c * sg
            ds = dy_ref[:, lanes]
            if kind == "q":
                ds = ds * (HEAD_DIM ** -0.5)
            if kind != "v":
                r = lax.rsqrt(jnp.sum(s * s, axis=-1, keepdims=True) + EPS)
                sh = s * r
                ds = r * (ds - sh * jnp.sum(ds * sh, axis=-1, keepdims=True))
            dc = ds * _dsilu(c, sg)
            _stage(dc_staged, dc)
            dp_ref[:, lanes] = _anti_causal_conv(dc, dc_staged, w, GDN_CONV).astype(dp_ref.dtype)
            _accumulate(dw_ref.at[:, lanes], pl.program_id(1) == 0, _conv_weight_grad(dc, pre, pre_staged, GDN_CONV))

    return pl.pallas_call(
        body, name="qkv_bwd_" + kind,
        out_shape=(jax.ShapeDtypeStruct((n, GDN_WIDTH), MXU_DTYPE), jax.ShapeDtypeStruct((GDN_CONV, GDN_WIDTH), F32)),
        grid=(HEADS // hb, n // rs),
        in_specs=[_seq_cols(rs, 0, hb), _seq_cols(rs, col0, hb), _tap_cols(GDN_CONV, col0, hb)],
        out_specs=(_seq_cols(rs, 0, hb), _tap_cols(GDN_CONV, 0, hb)),
        scratch_shapes=[_halo_scratch(rs), _halo_scratch(rs)],
        compiler_params=_params("parallel", "arbitrary"),
    )(dy, proj, conv_w)


SC_COL = 4 * HEADS


def _sc_fwd(proj, conv_w, rs):
    n = proj.shape[0]

    hb = 2

    def body(x_ref, b_ref, c_ref, w_ref, y_ref, staged):
        for h in range(hb):
            lanes = _lanes_of(h)
            u = c_ref[:, lanes] * x_ref[:, lanes]
            _stage(staged, u)
            y_ref[:, lanes] = (b_ref[:, lanes] * _causal_conv(u, staged, w_ref[:, lanes], SC_CONV)).astype(y_ref.dtype)

    return pl.pallas_call(
        body, name="sc_fwd", out_shape=jax.ShapeDtypeStruct((n, SC_WIDTH), MXU_DTYPE), grid=(HEADS // hb, n // rs),
        in_specs=[_seq_cols(rs, SC_COL, hb), _seq_cols(rs, SC_COL + 4, hb), _seq_cols(rs, SC_COL + 8, hb),
                  _tap_cols(SC_CONV, 0, hb)],
        out_specs=_seq_cols(rs, 0, hb), scratch_shapes=[_halo_scratch(rs)], compiler_params=_params("parallel", "parallel"),
    )(proj, proj, proj, conv_w)


def _sc_bwd(dcat, proj, conv_w, rs):
    n = proj.shape[0]
    hb = 2

    def body(dy_ref, x_ref, b_ref, c_ref, w_ref, dx_ref, db_ref, dc_ref, dw_ref, u_staged, dcv_staged):
        for h in range(hb):
            lanes = _lanes_of(h)
            w = w_ref[:, lanes]
            x = x_ref[:, lanes]
            cc = c_ref[:, lanes]
            u = cc * x
            _stage(u_staged, u)
            dy = dy_ref[:, lanes]
            db_ref[:, lanes] = (dy * _causal_conv(u, u_staged, w, SC_CONV)).astype(db_ref.dtype)
            dcv = dy * b_ref[:, lanes]
            _stage(dcv_staged, dcv)
            du = _anti_causal_conv(dcv, dcv_staged, w, SC_CONV)
            dx_ref[:, lanes] = (du * cc).astype(dx_ref.dtype)
            dc_ref[:, lanes] = (du * x).astype(dc_ref.dtype)
            _accumulate(dw_ref.at[:, lanes], pl.program_id(1) == 0, _conv_weight_grad(dcv, u, u_staged, SC_CONV))

    piece = jax.ShapeDtypeStruct((n, SC_WIDTH), MXU_DTYPE)
    return pl.pallas_call(
        body, name="sc_bwd", out_shape=(piece, piece, piece, jax.ShapeDtypeStruct((SC_CONV, SC_WIDTH), F32)),
        grid=(HEADS // hb, n // rs),
        in_specs=[_seq_cols(rs, HEADS, hb), _seq_cols(rs, SC_COL, hb), _seq_cols(rs, SC_COL + 4, hb),
                  _seq_cols(rs, SC_COL + 8, hb), _tap_cols(SC_CONV, 0, hb)],
        out_specs=(_seq_cols(rs, 0, hb), _seq_cols(rs, 0, hb), _seq_cols(rs, 0, hb), _tap_cols(SC_CONV, 0, hb)),
        scratch_shapes=[_halo_scratch(rs), _halo_scratch(rs)],
        compiler_params=_params("parallel", "arbitrary"),
    )(dcat, proj, proj, proj, conv_w)


Z_COL = 3 * HEADS


def _gate_fwd(o, proj, gdn_norm, rs):
    n = proj.shape[0]

    hb = HEADS

    def body(o_ref, z_ref, w_ref, y_ref):
        for h in range(hb):
            lanes = _lanes_of(h)
            z = z_ref[:, lanes]
            y_ref[:, lanes] = (_rms_apply(o_ref[:, lanes], w_ref[...]) * z * _sigmoid(z)).astype(y_ref.dtype)

    return pl.pallas_call(
        body, name="gate_fwd", out_shape=jax.ShapeDtypeStruct((n, GDN_WIDTH), MXU_DTYPE), grid=(HEADS // hb, n // rs),
        in_specs=[_seq_cols(rs, 0, hb), _seq_cols(rs, Z_COL, hb), pl.BlockSpec((1, LANES), lambda j, b: (0, 0))],
        out_specs=_seq_cols(rs, 0, hb), compiler_params=_params("parallel", "parallel"),
    )(o, proj, gdn_norm)


def _gate_bwd(dcat, o, proj, gdn_norm, rs):
    n = proj.shape[0]
    hb = 2

    def body(dy_ref, o_ref, z_ref, w_ref, do_ref, dz_ref, dw_ref):
        w = w_ref[...]
        dw_step = jnp.zeros((1, LANES), F32)
        for h in range(hb):
            lanes = _lanes_of(h)
            z = z_ref[:, lanes]
            o = o_ref[:, lanes]
            dy = dy_ref[:, lanes]
            s = _sigmoid(z)
            dz_ref[:, lanes] = (dy * _rms_apply(o, w) * _dsilu(z, s)).astype(dz_ref.dtype)
            do, dw = _rms_bwd(o, w, dy * z * s)
            do_ref[:, lanes] = do
            dw_step = dw_step + dw
        _accumulate(dw_ref, jnp.logical_and(pl.program_id(0) == 0, pl.program_id(1) == 0), dw_step)

    return pl.pallas_call(
        body, name="gate_bwd",
        out_shape=(jax.ShapeDtypeStruct((n, GDN_WIDTH), F32), jax.ShapeDtypeStruct((n, GDN_WIDTH), MXU_DTYPE),
                   jax.ShapeDtypeStruct((1, LANES), F32)),
        grid=(HEADS // hb, n // rs),
        in_specs=[_seq_cols(rs, 0, hb), _seq_cols(rs, 0, hb), _seq_cols(rs, Z_COL, hb), pl.BlockSpec((1, LANES), lambda j, b: (0, 0))],
        out_specs=(_seq_cols(rs, 0, hb), _seq_cols(rs, 0, hb), pl.BlockSpec((1, LANES), lambda j, b: (0, 0))),
        compiler_params=_params("arbitrary", "arbitrary"),
    )(dcat, o, proj, gdn_norm)


def _dot(a, b):
    return jnp.dot(a.astype(MXU_DTYPE), b.astype(MXU_DTYPE), preferred_element_type=F32)


def _dot_nt(a, b):
    return lax.dot_general(a.astype(MXU_DTYPE), b.astype(MXU_DTYPE), (((1,), (1,)), ((), ())),
                           preferred_element_type=F32)


def _dot_tn(a, b):
    return lax.dot_general(a.astype(MXU_DTYPE), b.astype(MXU_DTYPE), (((0,), (0,)), ((), ())),
                           preferred_element_type=F32)


def _split(x):
    hi = x.astype(MXU_DTYPE)
    return hi, (x - hi.astype(F32)).astype(MXU_DTYPE)


def _dot_split(a, b):
    mm = functools.partial(jnp.dot, preferred_element_type=F32)
    return mm(a[0], b[0]) + (mm(a[0], b[1]) + mm(a[1], b[0]))


def _unit_lower_inverses(mats, eye):
    inv = [eye - a for a in mats]
    power = [_split(a) for a in mats]
    span = 2
    while span < CHUNK:
        power = [_split(_dot_split(p, p)) for p in power]
        inv = [i + _dot_split(_split(i), p) for i, p in zip(inv, power)]
        span *= 2
    return inv


def _chunk_masks():
    ii = lax.broadcasted_iota(jnp.int32, (CHUNK, CHUNK), 0)
    jj = lax.broadcasted_iota(jnp.int32, (CHUNK, CHUNK), 1)
    return ii, jj


def _chunk_decay(g_col, ii, jj):
    incl = ii >= jj
    g_row = jnp.sum(jnp.where(ii == jj, g_col, 0.0), axis=0, keepdims=True)
    gc_col = jnp.sum(jnp.where(incl, g_row, 0.0), axis=1, keepdims=True)
    gc_row = jnp.sum(jnp.where(ii <= jj, g_col, 0.0), axis=0, keepdims=True)
    g_total = jnp.sum(g_row, axis=1, keepdims=True)
    decay = jnp.where(incl, jnp.exp(jnp.where(incl, gc_col - gc_row, 0.0)), 0.0)
    return gc_col, g_total, decay


def _gdn_segments(rs, candidates):
    chunks = rs // CHUNK
    seg_chunks = _pick(chunks, candidates)
    return chunks, seg_chunks, chunks // seg_chunks


def _head_lanes(h):
    return slice(h * HEAD_DIM, (h + 1) * HEAD_DIM)


def _gdn_fwd(q, k, v, bg, rs, pieces):
    n = q.shape[0]
    batch = n // rs
    chunks, seg_chunks, segs = _gdn_segments(rs, (11, 8, 4, 2))
    seg_rows = seg_chunks * CHUNK
    chains = [(b, h) for b in range(batch) for h in range(HEADS)]
    each = lambda f, *lists: [f(*args) for args in zip(*lists)]
    count = len(pieces)

    def body(q_ref, k_ref, v_ref, bg_ref, *rest):
        w_refs, (o_ref, s_ref, t_ref), out_refs = rest[:count], rest[count:count + 3], rest[count + 3:2 * count + 3]
        state_ref, send_sems, recv_sems = rest[2 * count + 3:]
        gather = _gather_copies(w_refs, out_refs, send_sems, recv_sems)

        @pl.when(pl.program_id(0) == 0)
        def _():
            state_ref[...] = jnp.zeros_like(state_ref)
            for cp in gather[0]:
                cp.start()

        ii, jj = _chunk_masks()
        incl = ii >= jj
        eye = (ii == jj).astype(F32)

        def chunk(c, carry):
            rows = pl.ds(pl.multiple_of(c * CHUNK, CHUNK), CHUNK)
            bgc = [bg_ref[b, rows, :] for b in range(batch)]
            qc = [q_ref[b, rows, _head_lanes(h)] for b, h in chains]
            kc = [k_ref[b, rows, _head_lanes(h)] for b, h in chains]
            vc = [v_ref[b, rows, _head_lanes(h)] for b, h in chains]
            beta = [bgc[b][:, h:h + 1] for b, h in chains]
            state = [state_ref[b, h] for b, h in chains]
            dec = [_chunk_decay(bgc[b][:, HEADS + h:HEADS + h + 1], ii, jj) for b, h in chains]
            gc_col, g_total, decay = ([d[i] for d in dec] for i in range(3))
            kb = each(lambda x, y: x * y, kc, beta)
            a = each(lambda x, y, d: jnp.where(ii > jj, _dot_nt(x, y) * d, 0.0), kb, kc, decay)
            t_inv = _unit_lower_inverses(a, eye)
            eg = [jnp.exp(g) for g in gc_col]
            u = each(lambda t, x, y: _dot(t, x * y), t_inv, vc, beta)
            w = each(lambda t, x, e: _dot(t, x * e), t_inv, kb, eg)
            qk = each(lambda x, y, d: jnp.where(incl, _dot_nt(x, y) * d, 0.0), qc, kc, decay)
            v_new = each(lambda x, y, s: x - _dot(y, s), u, w, state)
            o = each(lambda x, e, s, m, vn: _dot(x * e, s) + _dot(m, vn), qc, eg, state, qk, v_new)
            new_state = each(lambda s, gt, x, g, vn: s * jnp.exp(gt) + _dot_tn(x * jnp.exp(gt - g), vn),
                             state, g_total, kc, gc_col, v_new)
            for i, (b, h) in enumerate(chains):
                s_ref[b, h, c] = state[i]
                t_ref[b, h, c] = t_inv[i]
                o_ref[b, rows, _head_lanes(h)] = o[i]
                state_ref[b, h] = new_state[i]
            return carry

        lax.fori_loop(0, seg_chunks, chunk, 0)

        @pl.when(pl.program_id(0) == segs - 1)
        def _():
            _gather_finish(gather)

    rows_spec = lambda width: pl.BlockSpec((batch, seg_rows, width), lambda s: (0, s, 0))
    per_chunk = lambda r, c: pl.BlockSpec((batch, HEADS, seg_chunks, r, c), lambda s: (0, 0, s, 0, 0))
    as_seqs = lambda a: a.reshape(batch, rs, a.shape[-1])
    sems = GATHER_SEMS * count
    o, states, t_invs, *gathered = pl.pallas_call(
        body, name="gdn_fwd",
        out_shape=(jax.ShapeDtypeStruct((batch, rs, GDN_WIDTH), F32),
                   jax.ShapeDtypeStruct((batch, HEADS, chunks, HEAD_DIM, HEAD_DIM), F32),
                   jax.ShapeDtypeStruct((batch, HEADS, chunks, CHUNK, CHUNK), F32))
        + tuple(jax.ShapeDtypeStruct((N_CHIPS,) + p.shape, p.dtype) for p in pieces),
        grid=(segs,),
        in_specs=[rows_spec(GDN_WIDTH), rows_spec(GDN_WIDTH), rows_spec(GDN_WIDTH), rows_spec(LANES)] + [_hbm()] * count,
        out_specs=(rows_spec(GDN_WIDTH), per_chunk(HEAD_DIM, HEAD_DIM), per_chunk(CHUNK, CHUNK)) + (_hbm(),) * count,
        scratch_shapes=[pltpu.VMEM((batch, HEADS, HEAD_DIM, HEAD_DIM), F32), pltpu.SemaphoreType.DMA((sems,)),
                        pltpu.SemaphoreType.DMA((sems,))],
        compiler_params=_params("arbitrary"),
    )(as_seqs(q), as_seqs(k), as_seqs(v), as_seqs(bg), *pieces)
    return o.reshape(n, GDN_WIDTH), states, t_invs, gathered


def _gdn_bwd(do, q, k, v, bg, states, t_invs, rs, parts):
    n = q.shape[0]
    batch = n // rs
    chunks, seg_chunks, segs = _gdn_segments(rs, (3, 4, 2))
    seg_rows = seg_chunks * CHUNK
    chains = [(b, h) for b in range(batch) for h in range(HEADS)]
    each = lambda f, *lists: [f(*args) for args in zip(*lists)]
    count = len(parts)

    def body(do_ref, q_ref, k_ref, v_ref, bg_ref, s_ref, t_ref, *rest):
        p_refs, (dq_ref, dk_ref, dv_ref, dbg_ref), got_refs = rest[:count], rest[count:count + 4], rest[count + 4:2 * count + 4]
        dstate_ref, send_sems, recv_sems = rest[2 * count + 4:]
        exchange = _chip_copies(p_refs, got_refs, send_sems, recv_sems)

        @pl.when(pl.program_id(0) == 0)
        def _():
            dstate_ref[...] = jnp.zeros_like(dstate_ref)
            for cp in exchange:
                cp.start()

        ii, jj = _chunk_masks()
        incl = ii >= jj
        strict = ii > jj
        lane = lax.broadcasted_iota(jnp.int32, (1, LANES), 1)

        def rowsum(x):
            return jnp.sum(x, axis=1, keepdims=True)

        def total(x):
            return jnp.sum(rowsum(x), axis=0, keepdims=True)

        def chunk(step, carry):
            c = seg_chunks - 1 - step
            rows = pl.ds(pl.multiple_of(c * CHUNK, CHUNK), CHUNK)
            bgc = [bg_ref[b, rows, :] for b in range(batch)]
            qc = [q_ref[b, rows, _head_lanes(h)] for b, h in chains]
            kc = [k_ref[b, rows, _head_lanes(h)] for b, h in chains]
            vc = [v_ref[b, rows, _head_lanes(h)] for b, h in chains]
            doc = [do_ref[b, rows, _head_lanes(h)] for b, h in chains]
            beta = [bgc[b][:, h:h + 1] for b, h in chains]
            state = [s_ref[b, h, c] for b, h in chains]
            t_inv = [t_ref[b, h, c] for b, h in chains]
            d_state = [dstate_ref[b, h] for b, h in chains]
            dec = [_chunk_decay(bgc[b][:, HEADS + h:HEADS + h + 1], ii, jj) for b, h in chains]
            gc_col, g_total, decay = ([d[i] for d in dec] for i in range(3))
            kb = each(lambda x, y: x * y, kc, beta)
            vb = each(lambda x, y: x * y, vc, beta)
            eg = [jnp.exp(g) for g in gc_col]
            kbg = each(lambda x, y: x * y, kb, eg)
            a = each(lambda x, y, d: jnp.where(strict, _dot_nt(x, y) * d, 0.0), kb, kc, decay)
            qk = each(lambda x, y, d: jnp.where(incl, _dot_nt(x, y) * d, 0.0), qc, kc, decay)
            w = each(_dot, t_inv, kbg)
            u = each(_dot, t_inv, vb)
            q_dec = each(lambda x, y: x * y, qc, eg)
            ek = each(lambda gt, g: jnp.exp(gt - g), g_total, gc_col)
            k_dec = each(lambda x, y: x * y, kc, ek)
            g_last = [jnp.exp(gt) for gt in g_total]
            v_new = each(lambda x, y, s: x - _dot(y, s), u, w, state)
            dv_new = each(lambda m, d, x, ds: _dot_tn(m, d) + _dot(x, ds), qk, doc, k_dec, d_state)
            dqk = each(lambda d, vn: jnp.where(incl, _dot_nt(d, vn), 0.0), doc, v_new)
            dq_dec = each(_dot_nt, doc, state)
            dk_dec = each(_dot_nt, v_new, d_state)
            dg_last = each(lambda s, ds: total(s * ds), state, d_state)
            new_d_state = each(lambda x, d, gl, ds, y, dvn: _dot_tn(x, d) + gl * ds - _dot_tn(y, dvn),
                               q_dec, doc, g_last, d_state, w, dv_new)
            dw = each(lambda dvn, s: -_dot_nt(dvn, s), dv_new, state)
            dt = each(lambda dvn, x, y, z: _dot_nt(dvn, x) + _dot_nt(y, z), dv_new, vb, dw, kbg)
            dvb = each(_dot_tn, t_inv, dv_new)
            dkbg = each(_dot_tn, t_inv, dw)
            t_dt = each(_dot_tn, t_inv, dt)
            da = each(lambda x, t: -jnp.where(strict, _dot_nt(x, t), 0.0), t_dt, t_inv)
            dm_a = each(lambda x, y: x * y, da, decay)
            dm_qk = each(lambda x, y: x * y, dqk, decay)
            e = each(lambda x, y, z, t: x * y + z * t, da, a, dqk, qk)
            dkb = each(lambda m, x, y, z: _dot(m, x) + y * z, dm_a, kc, dkbg, eg)
            dk = each(lambda m, x, m2, y, z, t, p, bt: _dot_tn(m, x) + _dot_tn(m2, y) + z * t + p * bt,
                      dm_a, kb, dm_qk, qc, dk_dec, ek, dkb, beta)
            dq = each(lambda m, x, y, z: _dot(m, x) + y * z, dm_qk, kc, dq_dec, eg)
            dbeta = each(lambda x, y, z, t: rowsum(x * y + z * t), dkb, kc, dvb, vc)
            dgc = each(lambda x, p, pd, r, rd, s, sd: rowsum(x) - rowsum(jnp.where(ii == jj, jnp.sum(x, axis=0, keepdims=True), 0.0))
                       + rowsum(p * pd - r * rd + s * sd), e, dq_dec, q_dec, dk_dec, k_dec, dkbg, kbg)
            d_total = each(lambda r, rd, x, gl: total(r * rd) + x * gl, dk_dec, k_dec, dg_last, g_last)
            dg = each(lambda x, t: rowsum(jnp.where(jj >= ii, jnp.sum(jnp.where(ii == jj, x, 0.0), axis=0, keepdims=True), 0.0)) + t,
                      dgc, d_total)
            dbg = [jnp.zeros((CHUNK, LANES), F32) for _ in range(batch)]
            for i, (b, h) in enumerate(chains):
                dstate_ref[b, h] = new_d_state[i]
                dk_ref[b, rows, _head_lanes(h)] = dk[i]
                dq_ref[b, rows, _head_lanes(h)] = dq[i]
                dv_ref[b, rows, _head_lanes(h)] = dvb[i] * beta[i]
                dbg[b] = dbg[b] + jnp.where(lane == h, dbeta[i], 0.0) + jnp.where(lane == HEADS + h, dg[i], 0.0)
            for b in range(batch):
                dbg_ref[b, rows, :] = dbg[b]
            return carry

        lax.fori_loop(0, seg_chunks, chunk, 0)

        @pl.when(pl.program_id(0) == segs - 1)
        def _():
            for cp in exchange:
                cp.wait_recv()
            for cp in exchange:
                cp.wait_send()

    rows_spec = lambda width: pl.BlockSpec((batch, seg_rows, width), lambda s: (0, segs - 1 - s, 0))
    per_chunk = lambda r, c: pl.BlockSpec((batch, HEADS, seg_chunks, r, c), lambda s: (0, 0, segs - 1 - s, 0, 0))
    as_seqs = lambda a: a.reshape(batch, rs, a.shape[-1])
    grad = jax.ShapeDtypeStruct((batch, rs, GDN_WIDTH), F32)
    wide = rows_spec(GDN_WIDTH)
    dq, dk, dv, dbg, *got = pl.pallas_call(
        body, name="gdn_bwd",
        out_shape=(grad, grad, grad, jax.ShapeDtypeStruct((batch, rs, LANES), F32))
        + tuple(jax.ShapeDtypeStruct((3,) + p.shape[1:], p.dtype) for p in parts),
        grid=(segs,),
        in_specs=[wide, wide, wide, wide, rows_spec(LANES), per_chunk(HEAD_DIM, HEAD_DIM), per_chunk(CHUNK, CHUNK)]
        + [_hbm()] * count,
        out_specs=(wide, wide, wide, rows_spec(LANES)) + (_hbm(),) * count,
        scratch_shapes=[pltpu.VMEM((batch, HEADS, HEAD_DIM, HEAD_DIM), F32), pltpu.SemaphoreType.DMA((3 * count,)),
                        pltpu.SemaphoreType.DMA((3 * count,))],
        compiler_params=_params("arbitrary"),
    )(as_seqs(do), as_seqs(q), as_seqs(k), as_seqs(v), as_seqs(bg), states, t_invs, *parts)
    return dq.reshape(n, GDN_WIDTH), dk.reshape(n, GDN_WIDTH), dv.reshape(n, GDN_WIDTH), dbg.reshape(n, LANES), got


def _lane_vec(vals, offset):
    k = vals.shape[1]
    return jnp.pad(vals, ((0, 0), (offset, LANES - offset - k)))


LATER = ("w_out", "w_gate", "w_up", "w_down")


def _halves(a):
    return a.reshape(a.shape[:-2] + (2, a.shape[-2] // 2, a.shape[-1]))


def _local_step(x, target, meta, norms, w_in_t, conv_qkv, a_log, dt_bias, gdn_norm, conv_sc, later_shards, core_arg):
    batch, seq, d = x.shape
    tokens = N_META + seq
    pad_rows = (-tokens) % CHUNK
    rs = tokens + pad_rows
    x_offset = pad_rows + N_META
    n = batch * rs
    w_mix_pre, w_mix_post, w_ffn_pre, w_ffn_post = norms

    head = jnp.concatenate([jnp.zeros((pad_rows, d), F32), meta], axis=0)
    h0 = jnp.concatenate([jnp.broadcast_to(head[None], (batch, x_offset, d)), x], axis=1).reshape(n, d)
    target_p = jnp.pad(target, ((0, 0), (x_offset, 0), (0, 0))).reshape(n, d)
    a_log_l = _lane_vec(a_log, HEADS)
    dt_bias_l = _lane_vec(dt_bias, HEADS)

    u1 = _rms_fwd(h0, w_mix_pre, "rms_mix_pre")
    proj = _mm(u1, w_in_t, "nt", F32, "mm_proj")
    q = _qkv_fwd(proj, conv_qkv, "q", rs)
    k = _qkv_fwd(proj, conv_qkv, "k", rs)
    v = _qkv_fwd(proj, conv_qkv, "v", rs)
    bg = _gates_fwd(proj, a_log_l, dt_bias_l, rs, pad_rows)
    o, states, t_invs, gathered = _gdn_fwd(q, k, v, bg, rs, later_shards[:3])
    w_out, w_gate_t, w_up_t = (a.reshape(-1, d) for a in gathered)
    o_gated = _gate_fwd(o, proj, gdn_norm, rs)
    y_sc = _sc_fwd(proj, conv_sc, rs)
    cat = jnp.concatenate([o_gated, y_sc], axis=1)
    mix = _mm(cat, w_out, "nn", F32, "mm_mix")
    h1, u2 = _mix_residual(h0, mix, w_mix_post, w_ffn_pre)
    gate, up, act, w_down = _swiglu_fwd(u2, w_gate_t, w_up_t, later_shards[3])
    w_down = w_down.reshape(-1, d)
    ffn = _mm(act, w_down, "nn", F32, "mm_down")

    dh2, dffn, d_ffn_post, sq = _loss_head(h1, ffn, w_ffn_post, target_p, rs, x_offset)
    d_w_down = _mm(act, dffn, "tn", F32, "mm_dw_down")
    dgate, dup = _swiglu_bwd(dffn, w_down, gate, up)
    d_w_gate_t = _mm(dgate, u2, "tn", F32, "mm_dw_gate")
    d_w_up_t = _mm(dup, u2, "tn", F32, "mm_dw_up")
    du2 = _mm(dup, w_up_t, "nn", F32, "mm_du2_up", init=_mm(dgate, w_gate_t, "nn", F32, "mm_du2_gate"))
    by_chip = [_halves(g.reshape(N_CHIPS, -1, d)) for g in (d_w_gate_t, d_w_up_t, d_w_down)]
    dh1, dmix, d_ffn_pre, d_mix_post, got_sibling = _mid_bwd(h1, mix, w_mix_post, w_ffn_pre, dh2, du2, by_chip)
    dcat = _mm(dmix, w_out, "nt", F32, "mm_dcat")
    d_w_out = _halves(_mm(cat, dmix, "tn", F32, "mm_dw_out").reshape(N_CHIPS, -1, d))
    by_chip, got_sibling = [d_w_out] + by_chip, list(_exchange_siblings([d_w_out])) + got_sibling
    sums = [_add_sibling(a, b, core_arg, name) for name, a, b in zip(LATER, by_chip, got_sibling)]
    do, dz, d_gdn_norm = _gate_bwd(dcat, o, proj, gdn_norm, rs)
    dscx, dscb, dscc, d_conv_sc = _sc_bwd(dcat, proj, conv_sc, rs)
    dq, dk, dv, dbg, got_chips = _gdn_bwd(do, q, k, v, bg, states, t_invs, rs, [send for _, send in sums[:3]])
    dpq, dwq = _qkv_bwd(dq, proj, conv_qkv, "q", rs)
    dpk, dwk = _qkv_bwd(dk, proj, conv_qkv, "k", rs)
    dpv, dwv = _qkv_bwd(dv, proj, conv_qkv, "v", rs)
    d_conv_qkv = jnp.concatenate([dwq, dwk, dwv], axis=1)
    dba, d_a_log_l, d_dt_bias_l = _gates_bwd(proj, dbg, a_log_l, dt_bias_l, rs, pad_rows)
    dproj = jnp.concatenate([dpq, dpk, dpv, dz, dscx, dscb, dscc, dba], axis=1)
    d_w_in_t, got_down = _mm(dproj, u1, "tn", F32, "mm_dw_in", exchange=[sums[3][1]])
    got_chips.append(got_down)
    g_in = _halves(_in_from_kernel_order(d_w_in_t))
    sums.insert(0, _add_sibling(g_in, _exchange_siblings([g_in])[0], core_arg, "w_in"))
    du1, got_in = _mm(dproj, w_in_t, "nn", F32, "mm_du1", exchange=[sums[0][1]])
    got_chips.insert(0, got_in)
    dh0, d_mix_pre = _in_bwd(h0, w_mix_pre, dh1, du1)

    dh0 = dh0.reshape(batch, rs, d)
    grads = dict(
        meta_tokens=jnp.sum(dh0[:, pad_rows:x_offset], axis=0),
        mix_pre_norm=d_mix_pre, mix_post_norm=d_mix_post, ffn_pre_norm=d_ffn_pre, ffn_post_norm=d_ffn_post,
        conv_qkv=d_conv_qkv,
        a_log=d_a_log_l[:, HEADS:2 * HEADS], dt_bias=d_dt_bias_l[:, HEADS:2 * HEADS],
        gdn_norm=d_gdn_norm, conv_sc=d_conv_sc,
    )
    return sq, dh0[:, x_offset:], grads, [(part, got) for (part, _), got in zip(sums, got_chips)]


MATRICES = ("w_in", "w_out", "w_gate", "w_up", "w_down")
IN_SHARD = IN_WIDTH // N_CHIPS
IN_SHARD_PAD = 928


def _in_to_kernel_order(by_chip):
    w_t = by_chip[:, :IN_SHARD].reshape(IN_WIDTH, by_chip.shape[-1])
    lo, hi = 4 * GDN_WIDTH, 4 * GDN_WIDTH + 2 * HEADS
    return jnp.concatenate([w_t[:lo], w_t[hi:], w_t[lo:hi], jnp.zeros((IN_PAD - IN_WIDTH, w_t.shape[1]), w_t.dtype)], axis=0)


def _in_from_kernel_order(g_t):
    lo, hi = 4 * GDN_WIDTH, IN_WIDTH - 2 * HEADS
    g = jnp.concatenate([g_t[:lo], g_t[hi:IN_WIDTH], g_t[lo:hi]], axis=0).reshape(N_CHIPS, IN_SHARD, g_t.shape[-1])
    return jnp.pad(g, ((0, 0), (0, IN_SHARD_PAD - IN_SHARD), (0, 0)))


PACK_LANES = 3 * GDN_WIDTH
PACKED = dict(mix_pre_norm=(0, 1, 0, D_MODEL), mix_post_norm=(1, 1, 0, D_MODEL), ffn_pre_norm=(2, 1, 0, D_MODEL),
              ffn_post_norm=(3, 1, 0, D_MODEL), a_log=(4, 1, 0, HEADS), dt_bias=(5, 1, 0, HEADS), loss=(6, 1, 0, 1),
              gdn_norm=(7, 1, 0, HEAD_DIM), conv_qkv=(8, GDN_CONV, 0, 3 * GDN_WIDTH), conv_sc=(0, SC_CONV, D_MODEL, SC_WIDTH),
              meta_tokens=(16, N_META, 0, D_MODEL))
PACK_ROWS = 32
SHARDED_SMALL = ("conv_qkv", "conv_sc", "meta_tokens")


def _pack_small(values):
    names = list(PACKED)

    def body(*refs):
        out_ref = refs[-1]
        out_ref[...] = jnp.zeros_like(out_ref)
        for name, ref in zip(names, refs):
            row, rows, lane0, lanes = PACKED[name]
            out_ref[row:row + rows, lane0:lane0 + lanes] = ref[...]

    return pl.pallas_call(body, name="pack_small", out_shape=jax.ShapeDtypeStruct((PACK_ROWS, PACK_LANES), F32))(
        *[values[name] for name in names])


def _sum_devices(packed_all, chip):
    names = list(PACKED)

    def body(chip_ref, all_ref, *rest):
        shard_refs, out_refs = rest[:len(SHARDED_SMALL)], rest[len(SHARDED_SMALL):]

        def total(ref, rows, lanes):
            acc = ref[0, rows, lanes]
            for k in range(1, 8):
                acc = acc + ref[k, rows, lanes]
            return acc

        for name, out in zip(names, out_refs):
            row, rows, lane0, lanes = PACKED[name]
            if name in SHARDED_SMALL:
                out[...] = total(shard_refs[SHARDED_SMALL.index(name)], slice(0, rows), slice(None))
            else:
                out[...] = total(all_ref, slice(row, row + rows), slice(lane0, lane0 + lanes))

    def shard_spec(name):
        row, rows, lane0, lanes = PACKED[name]
        height, width = max(rows, 8), lanes // N_CHIPS
        assert row % height == 0 and lane0 % width == 0
        return pl.BlockSpec((8, height, width), lambda i, chip_ref: (0, row // height, lane0 // width + chip_ref[0]))

    def out_shape(name):
        _, rows, _, lanes = PACKED[name]
        return jax.ShapeDtypeStruct((rows, lanes // N_CHIPS if name in SHARDED_SMALL else lanes), F32)

    whole = lambda shape: pl.BlockSpec(shape, lambda i, chip_ref: (0,) * len(shape))
    outs = pl.pallas_call(
        body, name="sum_devices", out_shape=tuple(out_shape(n) for n in names),
        grid_spec=pltpu.PrefetchScalarGridSpec(
            num_scalar_prefetch=1, grid=(1,),
            in_specs=[whole(packed_all.shape)] + [shard_spec(n) for n in SHARDED_SMALL],
            out_specs=tuple(whole(out_shape(n).shape) for n in names)),
    )(chip, packed_all, *[packed_all] * len(SHARDED_SMALL))
    return dict(zip(names, outs))


def _hbm():
    return pl.BlockSpec(memory_space=pl.ANY)


def _place():
    x, y, c = lax.axis_index("x"), lax.axis_index("y"), lax.axis_index("c")
    chips = ((1 - x, y), (x, 1 - y), (1 - x, 1 - y))
    return x, y, c, chips


def _remote(src, dst, send_sems, recv_sems, k, to):
    return pltpu.make_async_remote_copy(src_ref=src, dst_ref=dst, send_sem=send_sems.at[k], recv_sem=recv_sems.at[k],
                                        device_id=to, device_id_type=MESH)


GATHER_SEMS = 7


def _gather_copies(w_refs, out_refs, send_sems, recv_sems):
    x, y, c, chips = _place()
    mine = 2 * x + y
    sibling = (x, y, 1 - c)
    copy = functools.partial(_remote, send_sems=send_sems, recv_sems=recv_sems)
    direct, landed, passing, from_sibling = [], [], [], []
    for i, (w, o) in enumerate(zip(w_refs, out_refs)):
        k = GATHER_SEMS * i
        direct.append(copy(w, o.at[mine], k=k, to=sibling))
        from_sibling.append(copy(w, o.at[mine], k=k, to=sibling))
        for j, (cx, cy) in enumerate(chips):
            theirs = 2 * cx + cy
            direct.append(copy(w.at[c], o.at[mine, c], k=k + 1 + j, to=(cx, cy, c)))
            landed.append(copy(w.at[c], o.at[theirs, c], k=k + 1 + j, to=sibling))
            passing.append(copy(o.at[theirs, c], o.at[theirs, c], k=k + 4 + j, to=sibling))
            from_sibling.append(copy(w.at[c], o.at[theirs, 1 - c], k=k + 4 + j, to=sibling))
    return direct, landed, passing, from_sibling


def _gather_finish(copies):
    direct, landed, passing, from_sibling = copies
    for arrival, forward in zip(landed, passing):
        arrival.wait_recv()
        forward.start()
    for arrival in from_sibling:
        arrival.wait_recv()
    for cp in direct + passing:
        cp.wait_send()


def _gather_weights(pieces, smalls):
    count, extra = len(pieces), len(smalls)
    total = count + extra

    def body(*refs):
        w_refs, s_refs = refs[:count], refs[count:total]
        out_refs, sall_refs = refs[total:total + count], refs[total + count:2 * total]
        send_sems, recv_sems, local_sems = refs[2 * total:]
        x, y, c, chips = _place()
        mine = 2 * x + y
        own = [pltpu.make_async_copy(s, sall.at[mine], local_sems.at[i]) for i, (s, sall) in enumerate(zip(s_refs, sall_refs))]
        small = [_remote(s, sall.at[mine], send_sems, recv_sems, GATHER_SEMS * count + 3 * i + j, (cx, cy, c))
                 for i, (s, sall) in enumerate(zip(s_refs, sall_refs)) for j, (cx, cy) in enumerate(chips)]
        copies = _gather_copies(w_refs, out_refs, send_sems, recv_sems)
        for cp in own + small + copies[0]:
            cp.start()
        _gather_finish(copies)
        for cp in small:
            cp.wait_recv()
        for cp in small:
            cp.wait_send()
        for cp in own:
            cp.wait()

    sems = GATHER_SEMS * count + 3 * extra
    return pl.pallas_call(
        body, name="gather_weights",
        out_shape=tuple(jax.ShapeDtypeStruct((N_CHIPS,) + p.shape, p.dtype) for p in list(pieces) + list(smalls)),
        in_specs=[_hbm()] * total, out_specs=(_hbm(),) * total,
        scratch_shapes=[pltpu.SemaphoreType.DMA((sems,)), pltpu.SemaphoreType.DMA((sems,)), pltpu.SemaphoreType.DMA((extra,))],
    )(*pieces, *smalls)


def _sibling_copies(g_refs, got_refs, send_sems, recv_sems):
    x, y, c, _ = _place()
    return [_remote(g.at[:, 1 - c], got, send_sems, recv_sems, i, (x, y, 1 - c)) for i, (g, got) in enumerate(zip(g_refs, got_refs))]


def _exchange_siblings(grads, small=None):
    count = len(grads)
    extra = 0 if small is None else 1

    def body(*refs):
        g_refs = refs[:count]
        got_refs = refs[count + extra:2 * count + extra]
        send_sems, recv_sems = refs[2 * (count + extra):2 * (count + extra) + 2]
        x, y, c, _ = _place()
        copies = _sibling_copies(g_refs, got_refs, send_sems, recv_sems)
        if small is not None:
            s_ref, sall_ref, local_sem = refs[count], refs[2 * count + 1], refs[-1]
            me = 4 * x + 2 * y + c
            own = pltpu.make_async_copy(s_ref, sall_ref.at[me], local_sem)
            own.start()
            for k in range(7):
                dx, dy, dc = ((k + 1) >> 2) & 1, ((k + 1) >> 1) & 1, (k + 1) & 1
                peer = (1 - x if dx else x, 1 - y if dy else y, 1 - c if dc else c)
                copies.append(_remote(s_ref, sall_ref.at[me], send_sems, recv_sems, count + k, peer))
        for cp in copies:
            cp.start()
        for cp in copies:
            cp.wait_recv()
        for cp in copies:
            cp.wait_send()
        if small is not None:
            own.wait()

    sems = count + 7 * extra
    return pl.pallas_call(
        body, name="exchange_siblings" + ("" if small is None else "_small"),
        out_shape=tuple(jax.ShapeDtypeStruct((g.shape[0],) + g.shape[2:], F32) for g in grads)
        + (() if small is None else (jax.ShapeDtypeStruct((8,) + small.shape, F32),)),
        in_specs=[_hbm()] * (count + extra), out_specs=(_hbm(),) * (count + extra),
        scratch_shapes=[pltpu.SemaphoreType.DMA((sems,)), pltpu.SemaphoreType.DMA((sems,))]
        + ([] if small is None else [pltpu.SemaphoreType.DMA]),
    )(*grads, *(() if small is None else (small,)))


def _chip_copies(p_refs, got_refs, send_sems, recv_sems):
    x, y, c, chips = _place()
    return [_remote(p.at[2 * cx + cy], got.at[j], send_sems, recv_sems, 3 * i + j, (cx, cy, c))
            for i, (p, got) in enumerate(zip(p_refs, got_refs)) for j, (cx, cy) in enumerate(chips)]


def _share_halves(halves):
    count = len(halves)

    def body(*refs):
        h_refs, full_refs = refs[:count], refs[count:2 * count]
        send_sems, recv_sems = refs[2 * count:]
        x, y, c, _ = _place()
        copies = [pltpu.make_async_remote_copy(src_ref=h.at[c], dst_ref=full.at[c], send_sem=send_sems.at[i],
                                               recv_sem=recv_sems.at[i], device_id=(x, y, 1 - c), device_id_type=MESH)
                  for i, (h, full) in enumerate(zip(h_refs, full_refs))]
        for cp in copies:
            cp.start()
        for cp in copies:
            cp.wait_recv()
        for cp in copies:
            cp.wait_send()

    return pl.pallas_call(
        body, name="share_halves", out_shape=tuple(jax.ShapeDtypeStruct(h.shape, h.dtype) for h in halves),
        in_specs=[_hbm()] * count, out_specs=(_hbm(),) * count, input_output_aliases={i: i for i in range(count)},
        scratch_shapes=[pltpu.SemaphoreType.DMA((count,)), pltpu.SemaphoreType.DMA((count,))],
    )(*halves)


def _add_sibling(grad, got, core, name):
    chips, _, rows, cols = grad.shape

    def body(core_ref, g_ref, r_ref, sum_ref, send_ref):
        s = g_ref[...] + r_ref[...]
        sum_ref[...] = s
        send_ref[...] = s.astype(send_ref.dtype)

    block = pl.BlockSpec((None, rows, cols), lambda p, core_ref: (p, 0, 0))
    return pl.pallas_call(
        body, name="add_sibling_" + name,
        out_shape=(jax.ShapeDtypeStruct((chips, rows, cols), F32), jax.ShapeDtypeStruct((chips, rows, cols), BF16)),
        grid_spec=pltpu.PrefetchScalarGridSpec(
            num_scalar_prefetch=1, grid=(chips,),
            in_specs=[pl.BlockSpec((None, None, rows, cols), lambda p, core_ref: (p, core_ref[0], 0, 0)), block],
            out_specs=(block, block)),
        compiler_params=_params("parallel"),
    )(core, grad, got)


def _add_chips(part, got, chip_core, name):
    _, rows, cols = part.shape
    tr = rows // 2 if rows % 32 == 0 else rows

    def body(place_ref, p_ref, r_ref, o_ref):
        o_ref[...] = ((p_ref[...] + r_ref[0].astype(F32)) + r_ref[1].astype(F32)) + r_ref[2].astype(F32)

    return pl.pallas_call(
        body, name="add_chips_" + name, out_shape=jax.ShapeDtypeStruct((2, rows, cols), F32),
        grid_spec=pltpu.PrefetchScalarGridSpec(
            num_scalar_prefetch=1, grid=(rows // tr,),
            in_specs=[pl.BlockSpec((None, tr, cols), lambda i, place_ref: (place_ref[0], i, 0)),
                      pl.BlockSpec((3, tr, cols), lambda i, place_ref: (0, i, 0))],
            out_specs=pl.BlockSpec((None, tr, cols), lambda i, place_ref: (place_ref[1], i, 0))),
        compiler_params=_params("parallel"),
    )(chip_core, part, got)


def _adamw(w, g, m, v, name):
    rows, cols = w.shape
    tr = _pick(rows, (3592, 256, 352, 176, 128, 64, 32, 16, 8))

    def body(w_ref, g_ref, m_ref, v_ref, d_ref, nm_ref, nv_ref):
        g = g_ref[...]
        m = ADAM_B1 * m_ref[...] + (1.0 - ADAM_B1) * g
        v = ADAM_B2 * v_ref[...] + (1.0 - ADAM_B2) * (g * g)
        m_hat = m / (1.0 - ADAM_B1 ** ADAM_STEP)
        v_hat = v / (1.0 - ADAM_B2 ** ADAM_STEP)
        d_ref[...] = -ADAM_LR * (m_hat / (jnp.sqrt(v_hat) + ADAM_EPS) + ADAM_WD * w_ref[...])
        nm_ref[...] = m
        nv_ref[...] = v

    block = pl.BlockSpec((tr, cols), lambda i: (i, 0))
    shape = jax.ShapeDtypeStruct((rows, cols), F32)
    return pl.pallas_call(
        body, name="adamw_" + name, out_shape=(shape, shape, shape), grid=(rows // tr,),
        in_specs=[block] * 4, out_specs=(block,) * 3, compiler_params=_params("parallel"),
    )(w, g, m, v)


WEIGHTS = ("meta_tokens", "mix_pre_norm", "mix_post_norm", "ffn_pre_norm", "ffn_post_norm", "w_in", "conv_qkv", "a_log",
           "dt_bias", "gdn_norm", "conv_sc", "w_out", "w_gate", "w_up", "w_down")


def kernel(x, meta_tokens, mix_pre_norm, mix_post_norm, ffn_pre_norm, ffn_post_norm, w_in, conv_qkv, a_log, dt_bias, gdn_norm, conv_sc, w_out, w_gate, w_up, w_down, loss_target, m_meta_tokens, m_mix_pre_norm, m_mix_post_norm, m_ffn_pre_norm, m_ffn_post_norm, m_w_in, m_conv_qkv, m_a_log, m_dt_bias, m_gdn_norm, m_conv_sc, m_w_out, m_w_gate, m_w_up, m_w_down, v_meta_tokens, v_mix_pre_norm, v_mix_post_norm, v_ffn_pre_norm, v_ffn_post_norm, v_w_in, v_conv_qkv, v_a_log, v_dt_bias, v_gdn_norm, v_conv_sc, v_w_out, v_w_gate, v_w_up, v_w_down):
    d = x.shape[-1]
    two_d = lambda a: a.reshape(a.shape[-2:])
    weights = dict(zip(WEIGHTS, (meta_tokens, mix_pre_norm, mix_post_norm, ffn_pre_norm, ffn_post_norm, w_in, conv_qkv, a_log,
                                 dt_bias, gdn_norm, conv_sc, w_out, w_gate, w_up, w_down)))
    m_in = dict(zip(WEIGHTS, (m_meta_tokens, m_mix_pre_norm, m_mix_post_norm, m_ffn_pre_norm, m_ffn_post_norm, m_w_in, m_conv_qkv,
                              m_a_log, m_dt_bias, m_gdn_norm, m_conv_sc, m_w_out, m_w_gate, m_w_up, m_w_down)))
    v_in = dict(zip(WEIGHTS, (v_meta_tokens, v_mix_pre_norm, v_mix_post_norm, v_ffn_pre_norm, v_ffn_post_norm, v_w_in, v_conv_qkv,
                              v_a_log, v_dt_bias, v_gdn_norm, v_conv_sc, v_w_out, v_w_gate, v_w_up, v_w_down)))
    core = lax.axis_index("c")
    chip = 2 * lax.axis_index("x") + lax.axis_index("y")
    core_arg = core.reshape(1).astype(jnp.int32)
    chip_core = jnp.stack([chip, core]).astype(jnp.int32)
    whole = lambda a: a.reshape(a.shape[:-3] + (2 * a.shape[-2], d))
    by_rows = lambda n, a: two_d(a).T if n in ("w_in", "w_gate", "w_up") else two_d(a)

    shard = {n: by_rows(n, weights[n]).astype(MXU_DTYPE) for n in MATRICES}
    shard["w_in"] = jnp.pad(shard["w_in"], ((0, IN_SHARD_PAD - IN_SHARD), (0, 0)))
    w_in_all, *small_all = _gather_weights([_halves(shard["w_in"])], [two_d(weights[n]) for n in SHARDED_SMALL])
    w_in_t = _in_to_kernel_order(whole(w_in_all))
    conv_qkv_full, conv_sc_full, meta_full = (jnp.concatenate([a[p] for p in range(N_CHIPS)], axis=1) for a in small_all)

    sq, grad_x, g, sums = _local_step(
        x, loss_target, meta_full, (mix_pre_norm, mix_post_norm, ffn_pre_norm, ffn_post_norm), w_in_t, conv_qkv_full, a_log,
        dt_bias, gdn_norm, conv_sc_full, [_halves(shard[n]) for n in LATER], core_arg)

    (packed_all,) = _exchange_siblings([], _pack_small(dict(g, loss=sq)))
    totals = [_add_chips(part, got, chip_core, n) for n, (part, got) in zip(MATRICES, sums)]
    grads = {n: whole(a) for n, a in zip(MATRICES, _share_halves(totals))}
    grads["w_in"] = grads["w_in"][:IN_SHARD]
    grads.update(_sum_devices(packed_all, chip.reshape(1).astype(jnp.int32)))
    loss = (0.5 / d) * grads.pop("loss")[0, 0]

    outs = [[], [], [], []]
    for n in WEIGHTS:
        shape = weights[n].shape
        delta, new_m, new_v = _adamw(by_rows(n, weights[n]), grads[n], by_rows(n, m_in[n]), by_rows(n, v_in[n]), n)
        for out, a in zip(outs, (grads[n], delta, new_m, new_v)):
            out.append((a.T if n in ("w_in", "w_gate", "w_up") else a).reshape(shape))
    return (loss, grad_x, *outs[0], *outs[1], *outs[2], *outs[3])
```

```python
import functools

import jax
import jax.numpy as jnp
from jax import lax
from jax.experimental import pallas as pl
from jax.experimental.pallas import tpu as pltpu

F32 = jnp.float32
BF16 = jnp.bfloat16
MXU_DTYPE = jnp.bfloat16
MESH = pl.DeviceIdType.MESH

D_MODEL = 1024
N_META = 16
HEADS = 4
HEAD_DIM = 128
GDN_WIDTH = HEADS * HEAD_DIM
GDN_CONV = 4
CHUNK = 64
SC_WIDTH = D_MODEL - GDN_WIDTH
SC_CONV = 3
D_FF = 2816
IN_WIDTH = 4 * GDN_WIDTH + 2 * HEADS + 3 * SC_WIDTH
IN_PAD = 3840
BA_COL = (4 * GDN_WIDTH + 3 * SC_WIDTH) // 128
EPS = 1e-6
LANES = 128
N_CHIPS = 4
VMEM_LIMIT = 48 * 2 ** 20

ADAM_LR = 0.001
ADAM_B1 = 0.9
ADAM_B2 = 0.999
ADAM_EPS = 1e-08
ADAM_WD = 0.01
ADAM_STEP = 10


def _pick(n, candidates):
    for c in candidates:
        if n % c == 0:
            return c
    return n


def _row_tile(n):
    return _pick(n, (352, 256, 176, 128, 64, 32, 16, 8))


def _params(*sem):
    return pltpu.CompilerParams(dimension_semantics=sem, vmem_limit_bytes=VMEM_LIMIT)


def _sigmoid(x):
    return 0.5 * jnp.tanh(0.5 * x) + 0.5


def _softplus(x):
    return jnp.maximum(x, 0.0) + jnp.log(1.0 + jnp.exp(-jnp.abs(x)))


def _dsilu(x, s):
    return s * (1.0 + x * (1.0 - s))


def _mm(a, b, mode, out_dtype, name, init=None, exchange=None):
    if mode == "tn":
        k_dim, m_dim = a.shape
    else:
        m_dim, k_dim = a.shape
    n_dim = b.shape[0] if mode == "nt" else b.shape[1]
    rows = (1056, 1024, 704, 512, 256, 128) if init is not None else (2112, 1056, 1024, 704, 512, 256, 128)
    tm = _pick(m_dim, (1408, 1280, 1024, 512, 256, 128) if mode == "tn" else rows)
    tn = _pick(n_dim, (1408, 1280, 1024, 768, 512, 256, 128))
    tk = _pick(k_dim, (1408, 1280, 1056, 1024, 512, 256, 128))
    nk = k_dim // tk
    if mode == "nn":
        a_spec = pl.BlockSpec((tm, tk), lambda i, j, k: (i, k))
        b_spec = pl.BlockSpec((tk, tn), lambda i, j, k: (k, j))
        dims = (((1,), (0,)), ((), ()))
    elif mode == "nt":
        a_spec = pl.BlockSpec((tm, tk), lambda i, j, k: (i, k))
        b_spec = pl.BlockSpec((tn, tk), lambda i, j, k: (j, k))
        dims = (((1,), (1,)), ((), ()))
    else:
        a_spec = pl.BlockSpec((tk, tm), lambda i, j, k: (k, i))
        b_spec = pl.BlockSpec((tk, tn), lambda i, j, k: (k, j))
        dims = (((0,), (0,)), ((), ()))

    out_spec = pl.BlockSpec((tm, tn), lambda i, j, k: (i, j))
    grid = (m_dim // tm, n_dim // tn, nk)
    parts = () if exchange is None else tuple(exchange)
    count = len(parts)
    first_in = 2 if init is None else 3

    assert out_dtype == F32

    def body(a_ref, b_ref, *rest):
        o_ref = rest[first_in - 2 + count]
        k = pl.program_id(2)
        step = (pl.program_id(0) * grid[1] + pl.program_id(1)) * nk + k
        if count:
            copies = _chip_copies(rest[first_in - 2:first_in - 2 + count], rest[first_in - 1 + count:first_in - 1 + 2 * count],
                                  *rest[first_in - 1 + 2 * count:])

            @pl.when(step == 0)
            def _():
                for cp in copies:
                    cp.start()

        p = lax.dot_general(a_ref[...], b_ref[...], dims, preferred_element_type=F32)
        if nk == 1:
            o_ref[...] = p if init is None else rest[0][...] + p
        else:
            @pl.when(k == 0)
            def _():
                o_ref[...] = p if init is None else rest[0][...] + p

            @pl.when(k > 0)
            def _():
                o_ref[...] += p

        if count:
            @pl.when(step == grid[0] * grid[1] * nk - 1)
            def _():
                for cp in copies:
                    cp.wait_recv()
                for cp in copies:
                    cp.wait_send()

    out = pl.pallas_call(
        body, name=name,
        out_shape=(jax.ShapeDtypeStruct((m_dim, n_dim), out_dtype),)
        + tuple(jax.ShapeDtypeStruct((3,) + p.shape[1:], p.dtype) for p in parts),
        grid=grid,
        in_specs=[a_spec, b_spec] + ([] if init is None else [out_spec]) + [_hbm()] * count,
        out_specs=(out_spec,) + (_hbm(),) * count,
        scratch_shapes=[pltpu.SemaphoreType.DMA((3 * count,)), pltpu.SemaphoreType.DMA((3 * count,))] if count else [],
        compiler_params=_params(*(("arbitrary",) * 3 if count else ("parallel", "parallel", "arbitrary"))),
    )(a, b, *(() if init is None else (init,)), *parts)
    return out[0] if not count else out


def _rms_apply(x, w):
    r = lax.rsqrt(jnp.mean(x * x, axis=-1, keepdims=True) + EPS)
    return x * r * w


def _rms_bwd(x, w, dy):
    r = lax.rsqrt(jnp.mean(x * x, axis=-1, keepdims=True) + EPS)
    xh = x * r
    dyw = dy * w
    dx = r * (dyw - xh * jnp.mean(dyw * xh, axis=-1, keepdims=True))
    return dx, jnp.sum(dy * xh, axis=0, keepdims=True)


def _accumulate(ref, first, value):
    @pl.when(first)
    def _():
        ref[...] = value

    @pl.when(jnp.logical_not(first))
    def _():
        ref[...] += value


def _rows(tr, width):
    return pl.BlockSpec((tr, width), lambda i: (i, 0))


def _vec(width):
    return pl.BlockSpec((1, width), lambda i: (0, 0))


def _rms_fwd(h, w, name):
    n, d = h.shape
    tr = _row_tile(n)

    def body(h_ref, w_ref, u_ref):
        u_ref[...] = _rms_apply(h_ref[...], w_ref[...]).astype(u_ref.dtype)

    return pl.pallas_call(
        body, name=name, out_shape=jax.ShapeDtypeStruct((n, d), MXU_DTYPE), grid=(n // tr,),
        in_specs=[_rows(tr, d), _vec(d)], out_specs=_rows(tr, d), compiler_params=_params("parallel"),
    )(h, w)


def _mix_residual(h0, mix, w_post, w_pre):
    n, d = h0.shape
    tr = _row_tile(n)

    def body(h0_ref, mix_ref, wpost_ref, wpre_ref, h1_ref, u2_ref):
        h1 = h0_ref[...] + _rms_apply(mix_ref[...], wpost_ref[...])
        h1_ref[...] = h1
        u2_ref[...] = _rms_apply(h1, wpre_ref[...]).astype(u2_ref.dtype)

    return pl.pallas_call(
        body, name="mix_residual",
        out_shape=(jax.ShapeDtypeStruct((n, d), F32), jax.ShapeDtypeStruct((n, d), MXU_DTYPE)), grid=(n // tr,),
        in_specs=[_rows(tr, d), _rows(tr, d), _vec(d), _vec(d)], out_specs=(_rows(tr, d), _rows(tr, d)),
        compiler_params=_params("parallel"),
    )(h0, mix, w_post, w_pre)


NT_DIMS = (((1,), (1,)), ((), ()))


def _ffn_tiles(n):
    return _pick(n, (704, 512, 256, 128)), _pick(D_FF, (1408, 256, 128))


def _swiglu_fwd(u, w_gate_t, w_up_t, w_next):
    n, d = u.shape
    tm, tn = _ffn_tiles(n)
    grid = (D_FF // tn, n // tm)

    def body(u_ref, wg_ref, wu_ref, wn_ref, g_ref, up_ref, act_ref, wall_ref, send_sems, recv_sems):
        gather = _gather_copies([wn_ref], [wall_ref], send_sems, recv_sems)
        step = pl.program_id(0) * grid[1] + pl.program_id(1)

        @pl.when(step == 0)
        def _():
            for cp in gather[0]:
                cp.start()

        a = u_ref[...]
        g = lax.dot_general(a, wg_ref[...], NT_DIMS, preferred_element_type=F32)
        up = lax.dot_general(a, wu_ref[...], NT_DIMS, preferred_element_type=F32)
        g_ref[...] = g
        up_ref[...] = up
        act_ref[...] = (g * _sigmoid(g) * up).astype(act_ref.dtype)

        @pl.when(step == grid[0] * grid[1] - 1)
        def _():
            _gather_finish(gather)

    tile = pl.BlockSpec((tm, tn), lambda j, i: (i, j))
    weight = pl.BlockSpec((tn, d), lambda j, i: (j, 0))
    wide = jax.ShapeDtypeStruct((n, D_FF), F32)
    return pl.pallas_call(
        body, name="swiglu_fwd",
        out_shape=(wide, wide, jax.ShapeDtypeStruct((n, D_FF), MXU_DTYPE),
                   jax.ShapeDtypeStruct((N_CHIPS,) + w_next.shape, w_next.dtype)),
        grid=grid,
        in_specs=[pl.BlockSpec((tm, d), lambda j, i: (i, 0)), weight, weight, _hbm()], out_specs=(tile, tile, tile, _hbm()),
        scratch_shapes=[pltpu.SemaphoreType.DMA((GATHER_SEMS,)), pltpu.SemaphoreType.DMA((GATHER_SEMS,))],
        compiler_params=_params("arbitrary", "arbitrary"),
    )(u, w_gate_t, w_up_t, w_next)


def _swiglu_bwd(dffn, w_down, gate, up):
    n, d = dffn.shape
    tm, tn = _ffn_tiles(n)

    def body(dy_ref, w_ref, g_ref, u_ref, dg_ref, du_ref):
        da = lax.dot_general(dy_ref[...], w_ref[...], NT_DIMS, preferred_element_type=F32)
        g = g_ref[...]
        s = _sigmoid(g)
        dg_ref[...] = (da * u_ref[...] * _dsilu(g, s)).astype(dg_ref.dtype)
        du_ref[...] = (da * g * s).astype(du_ref.dtype)

    tile = pl.BlockSpec((tm, tn), lambda j, i: (i, j))
    shape = jax.ShapeDtypeStruct((n, D_FF), MXU_DTYPE)
    return pl.pallas_call(
        body, name="swiglu_bwd", out_shape=(shape, shape), grid=(D_FF // tn, n // tm),
        in_specs=[pl.BlockSpec((tm, d), lambda j, i: (i, 0)), pl.BlockSpec((tn, d), lambda j, i: (j, 0)), tile, tile],
        out_specs=(tile, tile), compiler_params=_params("parallel", "parallel"),
    )(dffn, w_down, gate, up)


def _loss_head(h1, ffn, w_post, target, rows_per_seq, x_offset):
    n, d = h1.shape
    tr = _row_tile(rows_per_seq)
    tiles_per_seq = rows_per_seq // tr

    def body(h1_ref, ffn_ref, w_ref, t_ref, dh2_ref, dffn_ref, dw_ref, sq_ref):
        i = pl.program_id(0)
        w = w_ref[...]
        f = ffn_ref[...]
        r = lax.rsqrt(jnp.mean(f * f, axis=-1, keepdims=True) + EPS)
        fh = f * r
        row = lax.rem(i, tiles_per_seq) * tr + lax.broadcasted_iota(jnp.int32, (tr, 1), 0)
        err = jnp.where(row >= x_offset, h1_ref[...] + fh * w - t_ref[...], 0.0)
        dh2 = err * (1.0 / d)
        dh2_ref[...] = dh2
        dyw = dh2 * w
        dffn_ref[...] = (r * (dyw - fh * jnp.mean(dyw * fh, axis=-1, keepdims=True))).astype(dffn_ref.dtype)
        _accumulate(dw_ref, i == 0, jnp.sum(dh2 * fh, axis=0, keepdims=True))
        _accumulate(sq_ref, i == 0, jnp.sum(jnp.sum(err * err, axis=1, keepdims=True), axis=0, keepdims=True))

    return pl.pallas_call(
        body, name="loss_head",
        out_shape=(jax.ShapeDtypeStruct((n, d), F32), jax.ShapeDtypeStruct((n, d), MXU_DTYPE),
                   jax.ShapeDtypeStruct((1, d), F32), jax.ShapeDtypeStruct((1, 1), F32)),
        grid=(n // tr,),
        in_specs=[_rows(tr, d), _rows(tr, d), _vec(d), _rows(tr, d)],
        out_specs=(_rows(tr, d), _rows(tr, d), _vec(d), _vec(1)),
        compiler_params=_params("arbitrary"),
    )(h1, ffn, w_post, target)


def _mid_bwd(h1, mix, w_mix_post, w_ffn_pre, dh2, du2, grads):
    n, d = h1.shape
    tr = _row_tile(n)
    count = len(grads)

    def body(h1_ref, mix_ref, wpost_ref, wpre_ref, dh2_ref, du2_ref, *rest):
        g_refs, (dh1_ref, dmix_ref, dwpre_ref, dwpost_ref), got_refs = rest[:count], rest[count:count + 4], rest[count + 4:2 * count + 4]
        exchange = _sibling_copies(g_refs, got_refs, *rest[2 * count + 4:])
        i = pl.program_id(0)

        @pl.when(i == 0)
        def _():
            for cp in exchange:
                cp.start()

        dx, dwpre = _rms_bwd(h1_ref[...], wpre_ref[...], du2_ref[...])
        dh1 = dh2_ref[...] + dx
        dh1_ref[...] = dh1
        dmix, dwpost = _rms_bwd(mix_ref[...], wpost_ref[...], dh1)
        dmix_ref[...] = dmix.astype(dmix_ref.dtype)
        _accumulate(dwpre_ref, i == 0, dwpre)
        _accumulate(dwpost_ref, i == 0, dwpost)

        @pl.when(i == n // tr - 1)
        def _():
            for cp in exchange:
                cp.wait_recv()
            for cp in exchange:
                cp.wait_send()

    dh1, dmix, dwpre, dwpost, *got = pl.pallas_call(
        body, name="mid_bwd",
        out_shape=(jax.ShapeDtypeStruct((n, d), F32), jax.ShapeDtypeStruct((n, d), MXU_DTYPE),
                   jax.ShapeDtypeStruct((1, d), F32), jax.ShapeDtypeStruct((1, d), F32))
        + tuple(jax.ShapeDtypeStruct((g.shape[0],) + g.shape[2:], F32) for g in grads),
        grid=(n // tr,),
        in_specs=[_rows(tr, d), _rows(tr, d), _vec(d), _vec(d), _rows(tr, d), _rows(tr, d)] + [_hbm()] * count,
        out_specs=(_rows(tr, d), _rows(tr, d), _vec(d), _vec(d)) + (_hbm(),) * count,
        scratch_shapes=[pltpu.SemaphoreType.DMA((count,)), pltpu.SemaphoreType.DMA((count,))],
        compiler_params=_params("arbitrary"),
    )(h1, mix, w_mix_post, w_ffn_pre, dh2, du2, *grads)
    return dh1, dmix, dwpre, dwpost, got


def _in_bwd(h0, w_pre, dh1, du1):
    n, d = h0.shape
    tr = _row_tile(n)

    def body(h0_ref, w_ref, dh1_ref, du1_ref, dh0_ref, dw_ref):
        dx, dw = _rms_bwd(h0_ref[...], w_ref[...], du1_ref[...])
        dh0_ref[...] = dh1_ref[...] + dx
        _accumulate(dw_ref, pl.program_id(0) == 0, dw)

    return pl.pallas_call(
        body, name="in_bwd",
        out_shape=(jax.ShapeDtypeStruct((n, d), F32), jax.ShapeDtypeStruct((1, d), F32)), grid=(n // tr,),
        in_specs=[_rows(tr, d), _vec(d), _rows(tr, d), _rows(tr, d)], out_specs=(_rows(tr, d), _vec(d)),
        compiler_params=_params("arbitrary"),
    )(h0, w_pre, dh1, du1)


def _lane_is(lo, hi):
    lane = lax.broadcasted_iota(jnp.int32, (1, LANES), 1)
    return jnp.logical_and(lane >= lo, lane < hi)


def _gates_fwd(proj, a_log_l, dt_bias_l, rows_per_seq, pad_rows):
    n = proj.shape[0]
    tr = _row_tile(rows_per_seq)
    tiles_per_seq = rows_per_seq // tr

    def body(p_ref, a_ref, dt_ref, o_ref):
        x = p_ref[...]
        row = lax.rem(pl.program_id(0), tiles_per_seq) * tr + lax.broadcasted_iota(jnp.int32, (tr, 1), 0)
        g = -jnp.exp(a_ref[...]) * _softplus(x + dt_ref[...])
        val = jnp.where(_lane_is(0, HEADS), _sigmoid(x), jnp.where(_lane_is(HEADS, 2 * HEADS), g, 0.0))
        o_ref[...] = jnp.where(row >= pad_rows, val, 0.0)

    return pl.pallas_call(
        body, name="gates_fwd", out_shape=jax.ShapeDtypeStruct((n, LANES), F32), grid=(n // tr,),
        in_specs=[pl.BlockSpec((tr, LANES), lambda i: (i, BA_COL)), _vec(LANES), _vec(LANES)],
        out_specs=_rows(tr, LANES), compiler_params=_params("parallel"),
    )(proj, a_log_l, dt_bias_l)


def _gates_bwd(proj, dbg, a_log_l, dt_bias_l, rows_per_seq, pad_rows):
    n = proj.shape[0]
    tr = _row_tile(rows_per_seq)
    tiles_per_seq = rows_per_seq // tr

    def body(p_ref, d_ref, a_ref, dt_ref, dx_ref, da_ref, ddt_ref):
        i = pl.program_id(0)
        x = p_ref[...]
        d = d_ref[...]
        row = lax.rem(i, tiles_per_seq) * tr + lax.broadcasted_iota(jnp.int32, (tr, 1), 0)
        live = row >= pad_rows
        beta = _sigmoid(x)
        ea = jnp.exp(a_ref[...])
        xa = x + dt_ref[...]
        g = -ea * _softplus(xa)
        is_g = _lane_is(HEADS, 2 * HEADS)
        d_alogit = jnp.where(jnp.logical_and(live, is_g), d * (-ea) * _sigmoid(xa), 0.0)
        d_blogit = jnp.where(jnp.logical_and(live, _lane_is(0, HEADS)), d * beta * (1.0 - beta), 0.0)
        dx_ref[:, :LANES] = (d_alogit + d_blogit).astype(dx_ref.dtype)
        dx_ref[:, LANES:] = jnp.zeros((tr, LANES), dx_ref.dtype)
        _accumulate(da_ref, i == 0, jnp.sum(jnp.where(jnp.logical_and(live, is_g), d * g, 0.0), axis=0, keepdims=True))
        _accumulate(ddt_ref, i == 0, jnp.sum(d_alogit, axis=0, keepdims=True))

    return pl.pallas_call(
        body, name="gates_bwd",
        out_shape=(jax.ShapeDtypeStruct((n, 2 * LANES), MXU_DTYPE), jax.ShapeDtypeStruct((1, LANES), F32),
                   jax.ShapeDtypeStruct((1, LANES), F32)),
        grid=(n // tr,),
        in_specs=[pl.BlockSpec((tr, LANES), lambda i: (i, BA_COL)), _rows(tr, LANES), _vec(LANES), _vec(LANES)],
        out_specs=(_rows(tr, 2 * LANES), _vec(LANES), _vec(LANES)),
        compiler_params=_params("arbitrary"),
    )(proj, dbg, a_log_l, dt_bias_l)


HALO = 8


def _halo_scratch(rs):
    return pltpu.VMEM((rs + 2 * HALO, LANES), F32)


def _stage(ref, x):
    rs = x.shape[0]
    ref[0:HALO, :] = jnp.zeros((HALO, LANES), F32)
    ref[HALO + rs:, :] = jnp.zeros((HALO, LANES), F32)
    ref[HALO:HALO + rs, :] = x


def _shifted(ref, k, rs):
    return ref[pl.ds(HALO - k, rs), :]


def _causal_conv(x, x_staged, w, width):
    acc = w[width - 1:width, :] * x
    for i in range(width - 1):
        acc = acc + w[i:i + 1, :] * _shifted(x_staged, width - 1 - i, x.shape[0])
    return acc


def _anti_causal_conv(dy, dy_staged, w, width):
    acc = w[width - 1:width, :] * dy
    for i in range(width - 1):
        acc = acc + w[i:i + 1, :] * _shifted(dy_staged, -(width - 1 - i), dy.shape[0])
    return acc


def _conv_weight_grad(dy, x, x_staged, width):
    taps = [_shifted(x_staged, width - 1 - i, x.shape[0]) for i in range(width - 1)] + [x]
    return jnp.concatenate([jnp.sum(dy * tap, axis=0, keepdims=True) for tap in taps], axis=0)


def _seq_cols(rs, col0, heads):
    return pl.BlockSpec((rs, heads * LANES), lambda j, b: (b, col0 // heads + j))


def _tap_cols(width, col0, heads):
    return pl.BlockSpec((width, heads * LANES), lambda j, b: (0, col0 // heads + j))


def _lanes_of(h):
    return slice(h * LANES, (h + 1) * LANES)


def _qkv_fwd(proj, conv_w, kind, rs):
    n = proj.shape[0]
    col0 = {"q": 0, "k": HEADS, "v": 2 * HEADS}[kind]
    hb = HEADS

    def body(p_ref, w_ref, o_ref, staged):
        for h in range(hb):
            pre = p_ref[:, _lanes_of(h)]
            _stage(staged, pre)
            c = _causal_conv(pre, staged, w_ref[:, _lanes_of(h)], GDN_CONV)
            s = c * _sigmoid(c)
            if kind != "v":
                s = s * lax.rsqrt(jnp.sum(s * s, axis=-1, keepdims=True) + EPS)
            if kind == "q":
                s = s * (HEAD_DIM ** -0.5)
            o_ref[:, _lanes_of(h)] = s

    return pl.pallas_call(
        body, name="qkv_fwd_" + kind, out_shape=jax.ShapeDtypeStruct((n, GDN_WIDTH), F32), grid=(HEADS // hb, n // rs),
        in_specs=[_seq_cols(rs, col0, hb), _tap_cols(GDN_CONV, col0, hb)],
        out_specs=_seq_cols(rs, 0, hb), scratch_shapes=[_halo_scratch(rs)], compiler_params=_params("parallel", "parallel"),
    )(proj, conv_w)


def _qkv_bwd(dy, proj, conv_w, kind, rs):
    n = proj.shape[0]
    col0 = {"q": 0, "k": HEADS, "v": 2 * HEADS}[kind]
    hb = HEADS

    def body(dy_ref, p_ref, w_ref, dp_ref, dw_ref, pre_staged, dc_staged):
        for h in range(hb):
            lanes = _lanes_of(h)
            pre = p_ref[:, lanes]
            w = w_ref[:, lanes]
            _stage(pre_staged, pre)
            c = _causal_conv(pre, pre_staged, w, GDN_CONV)
            sg = _sigmoid(c)
            s = c * sg
            ds = dy_ref[:, lanes]
            if kind == "q":
                ds = ds * (HEAD_DIM ** -0.5)
            if kind != "v":
                r = lax.rsqrt(jnp.sum(s * s, axis=-1, keepdims=True) + EPS)
                sh = s * r
                ds = r * (ds - sh * jnp.sum(ds * sh, axis=-1, keepdims=True))
            dc = ds * _dsilu(c, sg)
            _stage(dc_staged, dc)
            dp_ref[:, lanes] = _anti_causal_conv(dc, dc_staged, w, GDN_CONV).astype(dp_ref.dtype)
            _accumulate(dw_ref.at[:, lanes], pl.program_id(1) == 0, _conv_weight_grad(dc, pre, pre_staged, GDN_CONV))

    return pl.pallas_call(
        body, name="qkv_bwd_" + kind,
        out_shape=(jax.ShapeDtypeStruct((n, GDN_WIDTH), MXU_DTYPE), jax.ShapeDtypeStruct((GDN_CONV, GDN_WIDTH), F32)),
        grid=(HEADS // hb, n // rs),
        in_specs=[_seq_cols(rs, 0, hb), _seq_cols(rs, col0, hb), _tap_cols(GDN_CONV, col0, hb)],
        out_specs=(_seq_cols(rs, 0, hb), _tap_cols(GDN_CONV, 0, hb)),
        scratch_shapes=[_halo_scratch(rs), _halo_scratch(rs)],
        compiler_params=_params("parallel", "arbitrary"),
    )(dy, proj, conv_w)


SC_COL = 4 * HEADS


def _sc_fwd(proj, conv_w, rs):
    n = proj.shape[0]

    hb = 2

    def body(x_ref, b_ref, c_ref, w_ref, y_ref, staged):
        for h in range(hb):
            lanes = _lanes_of(h)
            u = c_ref[:, lanes] * x_ref[:, lanes]
            _stage(staged, u)
            y_ref[:, lanes] = (b_ref[:, lanes] * _causal_conv(u, staged, w_ref[:, lanes], SC_CONV)).astype(y_ref.dtype)

    return pl.pallas_call(
        body, name="sc_fwd", out_shape=jax.ShapeDtypeStruct((n, SC_WIDTH), MXU_DTYPE), grid=(HEADS // hb, n // rs),
        in_specs=[_seq_cols(rs, SC_COL, hb), _seq_cols(rs, SC_COL + 4, hb), _seq_cols(rs, SC_COL + 8, hb),
                  _tap_cols(SC_CONV, 0, hb)],
        out_specs=_seq_cols(rs, 0, hb), scratch_shapes=[_halo_scratch(rs)], compiler_params=_params("parallel", "parallel"),
    )(proj, proj, proj, conv_w)


def _sc_bwd(dcat, proj, conv_w, rs):
    n = proj.shape[0]
    hb = 2

    def body(dy_ref, x_ref, b_ref, c_ref, w_ref, dx_ref, db_ref, dc_ref, dw_ref, u_staged, dcv_staged):
        for h in range(hb):
            lanes = _lanes_of(h)
            w = w_ref[:, lanes]
            x = x_ref[:, lanes]
            cc = c_ref[:, lanes]
            u = cc * x
            _stage(u_staged, u)
            dy = dy_ref[:, lanes]
            db_ref[:, lanes] = (dy * _causal_conv(u, u_staged, w, SC_CONV)).astype(db_ref.dtype)
            dcv = dy * b_ref[:, lanes]
            _stage(dcv_staged, dcv)
            du = _anti_causal_conv(dcv, dcv_staged, w, SC_CONV)
            dx_ref[:, lanes] = (du * cc).astype(dx_ref.dtype)
            dc_ref[:, lanes] = (du * x).astype(dc_ref.dtype)
            _accumulate(dw_ref.at[:, lanes], pl.program_id(1) == 0, _conv_weight_grad(dcv, u, u_staged, SC_CONV))

    piece = jax.ShapeDtypeStruct((n, SC_WIDTH), MXU_DTYPE)
    return pl.pallas_call(
        body, name="sc_bwd", out_shape=(piece, piece, piece, jax.ShapeDtypeStruct((SC_CONV, SC_WIDTH), F32)),
        grid=(HEADS // hb, n // rs),
        in_specs=[_seq_cols(rs, HEADS, hb), _seq_cols(rs, SC_COL, hb), _seq_cols(rs, SC_COL + 4, hb),
                  _seq_cols(rs, SC_COL + 8, hb), _tap_cols(SC_CONV, 0, hb)],
        out_specs=(_seq_cols(rs, 0, hb), _seq_cols(rs, 0, hb), _seq_cols(rs, 0, hb), _tap_cols(SC_CONV, 0, hb)),
        scratch_shapes=[_halo_scratch(rs), _halo_scratch(rs)],
        compiler_params=_params("parallel", "arbitrary"),
    )(dcat, proj, proj, proj, conv_w)


Z_COL = 3 * HEADS


def _gate_fwd(o, proj, gdn_norm, rs):
    n = proj.shape[0]

    hb = HEADS

    def body(o_ref, z_ref, w_ref, y_ref):
        for h in range(hb):
            lanes = _lanes_of(h)
            z = z_ref[:, lanes]
            y_ref[:, lanes] = (_rms_apply(o_ref[:, lanes], w_ref[...]) * z * _sigmoid(z)).astype(y_ref.dtype)

    return pl.pallas_call(
        body, name="gate_fwd", out_shape=jax.ShapeDtypeStruct((n, GDN_WIDTH), MXU_DTYPE), grid=(HEADS // hb, n // rs),
        in_specs=[_seq_cols(rs, 0, hb), _seq_cols(rs, Z_COL, hb), pl.BlockSpec((1, LANES), lambda j, b: (0, 0))],
        out_specs=_seq_cols(rs, 0, hb), compiler_params=_params("parallel", "parallel"),
    )(o, proj, gdn_norm)


def _gate_bwd(dcat, o, proj, gdn_norm, rs):
    n = proj.shape[0]
    hb = 2

    def body(dy_ref, o_ref, z_ref, w_ref, do_ref, dz_ref, dw_ref):
        w = w_ref[...]
        dw_step = jnp.zeros((1, LANES), F32)
        for h in range(hb):
            lanes = _lanes_of(h)
            z = z_ref[:, lanes]
            o = o_ref[:, lanes]
            dy = dy_ref[:, lanes]
            s = _sigmoid(z)
            dz_ref[:, lanes] = (dy * _rms_apply(o, w) * _dsilu(z, s)).astype(dz_ref.dtype)
            do, dw = _rms_bwd(o, w, dy * z * s)
            do_ref[:, lanes] = do
            dw_step = dw_step + dw
        _accumulate(dw_ref, jnp.logical_and(pl.program_id(0) == 0, pl.program_id(1) == 0), dw_step)

    return pl.pallas_call(
        body, name="gate_bwd",
        out_shape=(jax.ShapeDtypeStruct((n, GDN_WIDTH), F32), jax.ShapeDtypeStruct((n, GDN_WIDTH), MXU_DTYPE),
                   jax.ShapeDtypeStruct((1, LANES), F32)),
        grid=(HEADS // hb, n // rs),
        in_specs=[_seq_cols(rs, 0, hb), _seq_cols(rs, 0, hb), _seq_cols(rs, Z_COL, hb), pl.BlockSpec((1, LANES), lambda j, b: (0, 0))],
        out_specs=(_seq_cols(rs, 0, hb), _seq_cols(rs, 0, hb), pl.BlockSpec((1, LANES), lambda j, b: (0, 0))),
        compiler_params=_params("arbitrary", "arbitrary"),
    )(dcat, o, proj, gdn_norm)


def _dot(a, b):
    return jnp.dot(a.astype(MXU_DTYPE), b.astype(MXU_DTYPE), preferred_element_type=F32)


def _dot_nt(a, b):
    return lax.dot_general(a.astype(MXU_DTYPE), b.astype(MXU_DTYPE), (((1,), (1,)), ((), ())),
                           preferred_element_type=F32)


def _dot_tn(a, b):
    return lax.dot_general(a.astype(MXU_DTYPE), b.astype(MXU_DTYPE), (((0,), (0,)), ((), ())),
                           preferred_element_type=F32)


def _split(x):
    hi = x.astype(MXU_DTYPE)
    return hi, (x - hi.astype(F32)).astype(MXU_DTYPE)


def _dot_split(a, b):
    mm = functools.partial(jnp.dot, preferred_element_type=F32)
    return mm(a[0], b[0]) + (mm(a[0], b[1]) + mm(a[1], b[0]))


def _unit_lower_inverses(mats, eye):
    inv = [eye - a for a in mats]
    power = [_split(a) for a in mats]
    square = [_dot_split(p, p) for p in power]
    inv = [i + _dot_split(_split(i), _split(s)) for i, s in zip(inv, square)]
    span = 4
    while span < CHUNK:
        square = [_dot(s, s) for s in square]
        inv = [i + _dot(i, s) for i, s in zip(inv, square)]
        span *= 2
    return inv


def _chunk_masks():
    ii = lax.broadcasted_iota(jnp.int32, (CHUNK, CHUNK), 0)
    jj = lax.broadcasted_iota(jnp.int32, (CHUNK, CHUNK), 1)
    return ii, jj


def _chunk_decay(g_col, ii, jj):
    incl = ii >= jj
    g_row = jnp.sum(jnp.where(ii == jj, g_col, 0.0), axis=0, keepdims=True)
    gc_col = jnp.sum(jnp.where(incl, g_row, 0.0), axis=1, keepdims=True)
    gc_row = jnp.sum(jnp.where(ii <= jj, g_col, 0.0), axis=0, keepdims=True)
    g_total = jnp.sum(g_row, axis=1, keepdims=True)
    decay = jnp.where(incl, jnp.exp(jnp.where(incl, gc_col - gc_row, 0.0)), 0.0)
    return gc_col, g_total, decay


def _gdn_segments(rs, candidates):
    chunks = rs // CHUNK
    seg_chunks = _pick(chunks, candidates)
    return chunks, seg_chunks, chunks // seg_chunks


def _head_lanes(h):
    return slice(h * HEAD_DIM, (h + 1) * HEAD_DIM)


def _gdn_fwd(q, k, v, bg, rs, pieces):
    n = q.shape[0]
    batch = n // rs
    chunks, seg_chunks, segs = _gdn_segments(rs, (11, 8, 4, 2))
    seg_rows = seg_chunks * CHUNK
    chains = [(b, h) for b in range(batch) for h in range(HEADS)]
    each = lambda f, *lists: [f(*args) for args in zip(*lists)]
    count = len(pieces)

    def body(q_ref, k_ref, v_ref, bg_ref, *rest):
        w_refs, (o_ref, s_ref, t_ref), out_refs = rest[:count], rest[count:count + 3], rest[count + 3:2 * count + 3]
        state_ref, send_sems, recv_sems = rest[2 * count + 3:]
        gather = _gather_copies(w_refs, out_refs, send_sems, recv_sems)

        @pl.when(pl.program_id(0) == 0)
        def _():
            state_ref[...] = jnp.zeros_like(state_ref)
            for cp in gather[0]:
                cp.start()

        ii, jj = _chunk_masks()
        incl = ii >= jj
        eye = (ii == jj).astype(F32)

        def chunk(c, carry):
            rows = pl.ds(pl.multiple_of(c * CHUNK, CHUNK), CHUNK)
            bgc = [bg_ref[b, rows, :] for b in range(batch)]
            qc = [q_ref[b, rows, _head_lanes(h)] for b, h in chains]
            kc = [k_ref[b, rows, _head_lanes(h)] for b, h in chains]
            vc = [v_ref[b, rows, _head_lanes(h)] for b, h in chains]
            beta = [bgc[b][:, h:h + 1] for b, h in chains]
            state = [state_ref[b, h] for b, h in chains]
            dec = [_chunk_decay(bgc[b][:, HEADS + h:HEADS + h + 1], ii, jj) for b, h in chains]
            gc_col, g_total, decay = ([d[i] for d in dec] for i in range(3))
            kb = each(lambda x, y: x * y, kc, beta)
            a = each(lambda x, y, d: jnp.where(ii > jj, _dot_nt(x, y) * d, 0.0), kb, kc, decay)
            t_inv = _unit_lower_inverses(a, eye)
            eg = [jnp.exp(g) for g in gc_col]
            u = each(lambda t, x, y: _dot(t, x * y), t_inv, vc, beta)
            w = each(lambda t, x, e: _dot(t, x * e), t_inv, kb, eg)
            qk = each(lambda x, y, d: jnp.where(incl, _dot_nt(x, y) * d, 0.0), qc, kc, decay)
            v_new = each(lambda x, y, s: x - _dot(y, s), u, w, state)
            o = each(lambda x, e, s, m, vn: _dot(x * e, s) + _dot(m, vn), qc, eg, state, qk, v_new)
            new_state = each(lambda s, gt, x, g, vn: s * jnp.exp(gt) + _dot_tn(x * jnp.exp(gt - g), vn),
                             state, g_total, kc, gc_col, v_new)
            for i, (b, h) in enumerate(chains):
                s_ref[b, h, c] = state[i]
                t_ref[b, h, c] = t_inv[i]
                o_ref[b, rows, _head_lanes(h)] = o[i]
                state_ref[b, h] = new_state[i]
            return carry

        lax.fori_loop(0, seg_chunks, chunk, 0)

        @pl.when(pl.program_id(0) == segs - 1)
        def _():
            _gather_finish(gather)

    rows_spec = lambda width: pl.BlockSpec((batch, seg_rows, width), lambda s: (0, s, 0))
    per_chunk = lambda r, c: pl.BlockSpec((batch, HEADS, seg_chunks, r, c), lambda s: (0, 0, s, 0, 0))
    as_seqs = lambda a: a.reshape(batch, rs, a.shape[-1])
    sems = GATHER_SEMS * count
    o, states, t_invs, *gathered = pl.pallas_call(
        body, name="gdn_fwd",
        out_shape=(jax.ShapeDtypeStruct((batch, rs, GDN_WIDTH), F32),
                   jax.ShapeDtypeStruct((batch, HEADS, chunks, HEAD_DIM, HEAD_DIM), F32),
                   jax.ShapeDtypeStruct((batch, HEADS, chunks, CHUNK, CHUNK), F32))
        + tuple(jax.ShapeDtypeStruct((N_CHIPS,) + p.shape, p.dtype) for p in pieces),
        grid=(segs,),
        in_specs=[rows_spec(GDN_WIDTH), rows_spec(GDN_WIDTH), rows_spec(GDN_WIDTH), rows_spec(LANES)] + [_hbm()] * count,
        out_specs=(rows_spec(GDN_WIDTH), per_chunk(HEAD_DIM, HEAD_DIM), per_chunk(CHUNK, CHUNK)) + (_hbm(),) * count,
        scratch_shapes=[pltpu.VMEM((batch, HEADS, HEAD_DIM, HEAD_DIM), F32), pltpu.SemaphoreType.DMA((sems,)),
                        pltpu.SemaphoreType.DMA((sems,))],
        compiler_params=_params("arbitrary"),
    )(as_seqs(q), as_seqs(k), as_seqs(v), as_seqs(bg), *pieces)
    return o.reshape(n, GDN_WIDTH), states, t_invs, gathered


def _gdn_bwd(do, q, k, v, bg, states, t_invs, rs, parts):
    n = q.shape[0]
    batch = n // rs
    chunks, seg_chunks, segs = _gdn_segments(rs, (3, 4, 2))
    seg_rows = seg_chunks * CHUNK
    chains = [(b, h) for b in range(batch) for h in range(HEADS)]
    each = lambda f, *lists: [f(*args) for args in zip(*lists)]
    count = len(parts)

    def body(do_ref, q_ref, k_ref, v_ref, bg_ref, s_ref, t_ref, *rest):
        p_refs, (dq_ref, dk_ref, dv_ref, dbg_ref), got_refs = rest[:count], rest[count:count + 4], rest[count + 4:2 * count + 4]
        dstate_ref, send_sems, recv_sems = rest[2 * count + 4:]
        exchange = _chip_copies(p_refs, got_refs, send_sems, recv_sems)

        @pl.when(pl.program_id(0) == 0)
        def _():
            dstate_ref[...] = jnp.zeros_like(dstate_ref)
            for cp in exchange:
                cp.start()

        ii, jj = _chunk_masks()
        incl = ii >= jj
        strict = ii > jj
        lane = lax.broadcasted_iota(jnp.int32, (1, LANES), 1)

        def rowsum(x):
            return jnp.sum(x, axis=1, keepdims=True)

        def total(x):
            return jnp.sum(rowsum(x), axis=0, keepdims=True)

        def chunk(step, carry):
            c = seg_chunks - 1 - step
            rows = pl.ds(pl.multiple_of(c * CHUNK, CHUNK), CHUNK)
            bgc = [bg_ref[b, rows, :] for b in range(batch)]
            qc = [q_ref[b, rows, _head_lanes(h)] for b, h in chains]
            kc = [k_ref[b, rows, _head_lanes(h)] for b, h in chains]
            vc = [v_ref[b, rows, _head_lanes(h)] for b, h in chains]
            doc = [do_ref[b, rows, _head_lanes(h)] for b, h in chains]
            beta = [bgc[b][:, h:h + 1] for b, h in chains]
            state = [s_ref[b, h, c] for b, h in chains]
            t_inv = [t_ref[b, h, c] for b, h in chains]
            d_state = [dstate_ref[b, h] for b, h in chains]
            dec = [_chunk_decay(bgc[b][:, HEADS + h:HEADS + h + 1], ii, jj) for b, h in chains]
            gc_col, g_total, decay = ([d[i] for d in dec] for i in range(3))
            kb = each(lambda x, y: x * y, kc, beta)
            vb = each(lambda x, y: x * y, vc, beta)
            eg = [jnp.exp(g) for g in gc_col]
            kbg = each(lambda x, y: x * y, kb, eg)
            a = each(lambda x, y, d: jnp.where(strict, _dot_nt(x, y) * d, 0.0), kb, kc, decay)
            qk = each(lambda x, y, d: jnp.where(incl, _dot_nt(x, y) * d, 0.0), qc, kc, decay)
            w = each(_dot, t_inv, kbg)
            u = each(_dot, t_inv, vb)
            q_dec = each(lambda x, y: x * y, qc, eg)
            ek = each(lambda gt, g: jnp.exp(gt - g), g_total, gc_col)
            k_dec = each(lambda x, y: x * y, kc, ek)
            g_last = [jnp.exp(gt) for gt in g_total]
            v_new = each(lambda x, y, s: x - _dot(y, s), u, w, state)
            dv_new = each(lambda m, d, x, ds: _dot_tn(m, d) + _dot(x, ds), qk, doc, k_dec, d_state)
            dqk = each(lambda d, vn: jnp.where(incl, _dot_nt(d, vn), 0.0), doc, v_new)
            dq_dec = each(_dot_nt, doc, state)
            dk_dec = each(_dot_nt, v_new, d_state)
            dg_last = each(lambda s, ds: total(s * ds), state, d_state)
            new_d_state = each(lambda x, d, gl, ds, y, dvn: _dot_tn(x, d) + gl * ds - _dot_tn(y, dvn),
                               q_dec, doc, g_last, d_state, w, dv_new)
            dw = each(lambda dvn, s: -_dot_nt(dvn, s), dv_new, state)
            dt = each(lambda dvn, x, y, z: _dot_nt(dvn, x) + _dot_nt(y, z), dv_new, vb, dw, kbg)
            dvb = each(_dot_tn, t_inv, dv_new)
            dkbg = each(_dot_tn, t_inv, dw)
            t_dt = each(_dot_tn, t_inv, dt)
            da = each(lambda x, t: -jnp.where(strict, _dot_nt(x, t), 0.0), t_dt, t_inv)
            dm_a = each(lambda x, y: x * y, da, decay)
            dm_qk = each(lambda x, y: x * y, dqk, decay)
            e = each(lambda x, y, z, t: x * y + z * t, da, a, dqk, qk)
            dkb = each(lambda m, x, y, z: _dot(m, x) + y * z, dm_a, kc, dkbg, eg)
            dk = each(lambda m, x, m2, y, z, t, p, bt: _dot_tn(m, x) + _dot_tn(m2, y) + z * t + p * bt,
                      dm_a, kb, dm_qk, qc, dk_dec, ek, dkb, beta)
            dq = each(lambda m, x, y, z: _dot(m, x) + y * z, dm_qk, kc, dq_dec, eg)
            dbeta = each(lambda x, y, z, t: rowsum(x * y + z * t), dkb, kc, dvb, vc)
            dgc = each(lambda x, p, pd, r, rd, s, sd: rowsum(x) - rowsum(jnp.where(ii == jj, jnp.sum(x, axis=0, keepdims=True), 0.0))
                       + rowsum(p * pd - r * rd + s * sd), e, dq_dec, q_dec, dk_dec, k_dec, dkbg, kbg)
            d_total = each(lambda r, rd, x, gl: total(r * rd) + x * gl, dk_dec, k_dec, dg_last, g_last)
            dg = each(lambda x, t: rowsum(jnp.where(jj >= ii, jnp.sum(jnp.where(ii == jj, x, 0.0), axis=0, keepdims=True), 0.0)) + t,
                      dgc, d_total)
            dbg = [jnp.zeros((CHUNK, LANES), F32) for _ in range(batch)]
            for i, (b, h) in enumerate(chains):
                dstate_ref[b, h] = new_d_state[i]
                dk_ref[b, rows, _head_lanes(h)] = dk[i]
                dq_ref[b, rows, _head_lanes(h)] = dq[i]
                dv_ref[b, rows, _head_lanes(h)] = dvb[i] * beta[i]
                dbg[b] = dbg[b] + jnp.where(lane == h, dbeta[i], 0.0) + jnp.where(lane == HEADS + h, dg[i], 0.0)
            for b in range(batch):
                dbg_ref[b, rows, :] = dbg[b]
            return carry

        lax.fori_loop(0, seg_chunks, chunk, 0)

        @pl.when(pl.program_id(0) == segs - 1)
        def _():
            for cp in exchange:
                cp.wait_recv()
            for cp in exchange:
                cp.wait_send()

    rows_spec = lambda width: pl.BlockSpec((batch, seg_rows, width), lambda s: (0, segs - 1 - s, 0))
    per_chunk = lambda r, c: pl.BlockSpec((batch, HEADS, seg_chunks, r, c), lambda s: (0, 0, segs - 1 - s, 0, 0))
    as_seqs = lambda a: a.reshape(batch, rs, a.shape[-1])
    grad = jax.ShapeDtypeStruct((batch, rs, GDN_WIDTH), F32)
    wide = rows_spec(GDN_WIDTH)
    dq, dk, dv, dbg, *got = pl.pallas_call(
        body, name="gdn_bwd",
        out_shape=(grad, grad, grad, jax.ShapeDtypeStruct((batch, rs, LANES), F32))
        + tuple(jax.ShapeDtypeStruct((3,) + p.shape[1:], p.dtype) for p in parts),
        grid=(segs,),
        in_specs=[wide, wide, wide, wide, rows_spec(LANES), per_chunk(HEAD_DIM, HEAD_DIM), per_chunk(CHUNK, CHUNK)]
        + [_hbm()] * count,
        out_specs=(wide, wide, wide, rows_spec(LANES)) + (_hbm(),) * count,
        scratch_shapes=[pltpu.VMEM((batch, HEADS, HEAD_DIM, HEAD_DIM), F32), pltpu.SemaphoreType.DMA((3 * count,)),
                        pltpu.SemaphoreType.DMA((3 * count,))],
        compiler_params=_params("arbitrary"),
    )(as_seqs(do), as_seqs(q), as_seqs(k), as_seqs(v), as_seqs(bg), states, t_invs, *parts)
    return dq.reshape(n, GDN_WIDTH), dk.reshape(n, GDN_WIDTH), dv.reshape(n, GDN_WIDTH), dbg.reshape(n, LANES), got


def _lane_vec(vals, offset):
    k = vals.shape[1]
    return jnp.pad(vals, ((0, 0), (offset, LANES - offset - k)))


LATER = ("w_out", "w_gate", "w_up", "w_down")


def _halves(a):
    return a.reshape(a.shape[:-2] + (2, a.shape[-2] // 2, a.shape[-1]))


def _local_step(x, target, meta, norms, w_in_t, conv_qkv, a_log, dt_bias, gdn_norm, conv_sc, later_shards, core_arg):
    batch, seq, d = x.shape
    tokens = N_META + seq
    pad_rows = (-tokens) % CHUNK
    rs = tokens + pad_rows
    x_offset = pad_rows + N_META
    n = batch * rs
    w_mix_pre, w_mix_post, w_ffn_pre, w_ffn_post = norms

    head = jnp.concatenate([jnp.zeros((pad_rows, d), F32), meta], axis=0)
    h0 = jnp.concatenate([jnp.broadcast_to(head[None], (batch, x_offset, d)), x], axis=1).reshape(n, d)
    target_p = jnp.pad(target, ((0, 0), (x_offset, 0), (0, 0))).reshape(n, d)
    a_log_l = _lane_vec(a_log, HEADS)
    dt_bias_l = _lane_vec(dt_bias, HEADS)

    u1 = _rms_fwd(h0, w_mix_pre, "rms_mix_pre")
    proj = _mm(u1, w_in_t, "nt", F32, "mm_proj")
    q = _qkv_fwd(proj, conv_qkv, "q", rs)
    k = _qkv_fwd(proj, conv_qkv, "k", rs)
    v = _qkv_fwd(proj, conv_qkv, "v", rs)
    bg = _gates_fwd(proj, a_log_l, dt_bias_l, rs, pad_rows)
    o, states, t_invs, gathered = _gdn_fwd(q, k, v, bg, rs, later_shards[:3])
    w_out, w_gate_t, w_up_t = (a.reshape(-1, d) for a in gathered)
    o_gated = _gate_fwd(o, proj, gdn_norm, rs)
    y_sc = _sc_fwd(proj, conv_sc, rs)
    cat = jnp.concatenate([o_gated, y_sc], axis=1)
    mix = _mm(cat, w_out, "nn", F32, "mm_mix")
    h1, u2 = _mix_residual(h0, mix, w_mix_post, w_ffn_pre)
    gate, up, act, w_down = _swiglu_fwd(u2, w_gate_t, w_up_t, later_shards[3])
    w_down = w_down.reshape(-1, d)
    ffn = _mm(act, w_down, "nn", F32, "mm_down")

    dh2, dffn, d_ffn_post, sq = _loss_head(h1, ffn, w_ffn_post, target_p, rs, x_offset)
    d_w_down = _mm(act, dffn, "tn", F32, "mm_dw_down")
    dgate, dup = _swiglu_bwd(dffn, w_down, gate, up)
    d_w_gate_t = _mm(dgate, u2, "tn", F32, "mm_dw_gate")
    d_w_up_t = _mm(dup, u2, "tn", F32, "mm_dw_up")
    du2 = _mm(dup, w_up_t, "nn", F32, "mm_du2_up", init=_mm(dgate, w_gate_t, "nn", F32, "mm_du2_gate"))
    by_chip = [_halves(g.reshape(N_CHIPS, -1, d)) for g in (d_w_gate_t, d_w_up_t, d_w_down)]
    dh1, dmix, d_ffn_pre, d_mix_post, got_sibling = _mid_bwd(h1, mix, w_mix_post, w_ffn_pre, dh2, du2, by_chip)
    dcat = _mm(dmix, w_out, "nt", F32, "mm_dcat")
    d_w_out = _halves(_mm(cat, dmix, "tn", F32, "mm_dw_out").reshape(N_CHIPS, -1, d))
    by_chip, got_sibling = [d_w_out] + by_chip, list(_exchange_siblings([d_w_out])) + got_sibling
    sums = [_add_sibling(a, b, core_arg, name) for name, a, b in zip(LATER, by_chip, got_sibling)]
    do, dz, d_gdn_norm = _gate_bwd(dcat, o, proj, gdn_norm, rs)
    dscx, dscb, dscc, d_conv_sc = _sc_bwd(dcat, proj, conv_sc, rs)
    dq, dk, dv, dbg, got_chips = _gdn_bwd(do, q, k, v, bg, states, t_invs, rs, [send for _, send in sums[:3]])
    dpq, dwq = _qkv_bwd(dq, proj, conv_qkv, "q", rs)
    dpk, dwk = _qkv_bwd(dk, proj, conv_qkv, "k", rs)
    dpv, dwv = _qkv_bwd(dv, proj, conv_qkv, "v", rs)
    d_conv_qkv = jnp.concatenate([dwq, dwk, dwv], axis=1)
    dba, d_a_log_l, d_dt_bias_l = _gates_bwd(proj, dbg, a_log_l, dt_bias_l, rs, pad_rows)
    dproj = jnp.concatenate([dpq, dpk, dpv, dz, dscx, dscb, dscc, dba], axis=1)
    d_w_in_t, got_down = _mm(dproj, u1, "tn", F32, "mm_dw_in", exchange=[sums[3][1]])
    got_chips.append(got_down)
    g_in = _halves(_in_from_kernel_order(d_w_in_t))
    sums.insert(0, _add_sibling(g_in, _exchange_siblings([g_in])[0], core_arg, "w_in"))
    du1, got_in = _mm(dproj, w_in_t, "nn", F32, "mm_du1", exchange=[sums[0][1]])
    got_chips.insert(0, got_in)
    dh0, d_mix_pre = _in_bwd(h0, w_mix_pre, dh1, du1)

    dh0 = dh0.reshape(batch, rs, d)
    grads = dict(
        meta_tokens=jnp.sum(dh0[:, pad_rows:x_offset], axis=0),
        mix_pre_norm=d_mix_pre, mix_post_norm=d_mix_post, ffn_pre_norm=d_ffn_pre, ffn_post_norm=d_ffn_post,
        conv_qkv=d_conv_qkv,
        a_log=d_a_log_l[:, HEADS:2 * HEADS], dt_bias=d_dt_bias_l[:, HEADS:2 * HEADS],
        gdn_norm=d_gdn_norm, conv_sc=d_conv_sc,
    )
    return sq, dh0[:, x_offset:], grads, [(part, got) for (part, _), got in zip(sums, got_chips)]


MATRICES = ("w_in", "w_out", "w_gate", "w_up", "w_down")
IN_SHARD = IN_WIDTH // N_CHIPS
IN_SHARD_PAD = 928


def _in_to_kernel_order(by_chip):
    w_t = by_chip[:, :IN_SHARD].reshape(IN_WIDTH, by_chip.shape[-1])
    lo, hi = 4 * GDN_WIDTH, 4 * GDN_WIDTH + 2 * HEADS
    return jnp.concatenate([w_t[:lo], w_t[hi:], w_t[lo:hi], jnp.zeros((IN_PAD - IN_WIDTH, w_t.shape[1]), w_t.dtype)], axis=0)


def _in_from_kernel_order(g_t):
    lo, hi = 4 * GDN_WIDTH, IN_WIDTH - 2 * HEADS
    g = jnp.concatenate([g_t[:lo], g_t[hi:IN_WIDTH], g_t[lo:hi]], axis=0).reshape(N_CHIPS, IN_SHARD, g_t.shape[-1])
    return jnp.pad(g, ((0, 0), (0, IN_SHARD_PAD - IN_SHARD), (0, 0)))


PACK_LANES = 3 * GDN_WIDTH
PACKED = dict(mix_pre_norm=(0, 1, 0, D_MODEL), mix_post_norm=(1, 1, 0, D_MODEL), ffn_pre_norm=(2, 1, 0, D_MODEL),
              ffn_post_norm=(3, 1, 0, D_MODEL), a_log=(4, 1, 0, HEADS), dt_bias=(5, 1, 0, HEADS), loss=(6, 1, 0, 1),
              gdn_norm=(7, 1, 0, HEAD_DIM), conv_qkv=(8, GDN_CONV, 0, 3 * GDN_WIDTH), conv_sc=(0, SC_CONV, D_MODEL, SC_WIDTH),
              meta_tokens=(16, N_META, 0, D_MODEL))
PACK_ROWS = 32
SHARDED_SMALL = ("conv_qkv", "conv_sc", "meta_tokens")


def _pack_small(values):
    names = list(PACKED)

    def body(*refs):
        out_ref = refs[-1]
        out_ref[...] = jnp.zeros_like(out_ref)
        for name, ref in zip(names, refs):
            row, rows, lane0, lanes = PACKED[name]
            out_ref[row:row + rows, lane0:lane0 + lanes] = ref[...]

    return pl.pallas_call(body, name="pack_small", out_shape=jax.ShapeDtypeStruct((PACK_ROWS, PACK_LANES), F32))(
        *[values[name] for name in names])


def _sum_devices(packed_all, chip):
    names = list(PACKED)

    def body(chip_ref, all_ref, *rest):
        shard_refs, out_refs = rest[:len(SHARDED_SMALL)], rest[len(SHARDED_SMALL):]

        def total(ref, rows, lanes):
            acc = ref[0, rows, lanes]
            for k in range(1, 8):
                acc = acc + ref[k, rows, lanes]
            return acc

        for name, out in zip(names, out_refs):
            row, rows, lane0, lanes = PACKED[name]
            if name in SHARDED_SMALL:
                out[...] = total(shard_refs[SHARDED_SMALL.index(name)], slice(0, rows), slice(None))
            else:
                out[...] = total(all_ref, slice(row, row + rows), slice(lane0, lane0 + lanes))

    def shard_spec(name):
        row, rows, lane0, lanes = PACKED[name]
        height, width = max(rows, 8), lanes // N_CHIPS
        assert row % height == 0 and lane0 % width == 0
        return pl.BlockSpec((8, height, width), lambda i, chip_ref: (0, row // height, lane0 // width + chip_ref[0]))

    def out_shape(name):
        _, rows, _, lanes = PACKED[name]
        return jax.ShapeDtypeStruct((rows, lanes // N_CHIPS if name in SHARDED_SMALL else lanes), F32)

    whole = lambda shape: pl.BlockSpec(shape, lambda i, chip_ref: (0,) * len(shape))
    outs = pl.pallas_call(
        body, name="sum_devices", out_shape=tuple(out_shape(n) for n in names),
        grid_spec=pltpu.PrefetchScalarGridSpec(
            num_scalar_prefetch=1, grid=(1,),
            in_specs=[whole(packed_all.shape)] + [shard_spec(n) for n in SHARDED_SMALL],
            out_specs=tuple(whole(out_shape(n).shape) for n in names)),
    )(chip, packed_all, *[packed_all] * len(SHARDED_SMALL))
    return dict(zip(names, outs))


def _hbm():
    return pl.BlockSpec(memory_space=pl.ANY)


def _place():
    x, y, c = lax.axis_index("x"), lax.axis_index("y"), lax.axis_index("c")
    chips = ((1 - x, y), (x, 1 - y), (1 - x, 1 - y))
    return x, y, c, chips


def _remote(src, dst, send_sems, recv_sems, k, to):
    return pltpu.make_async_remote_copy(src_ref=src, dst_ref=dst, send_sem=send_sems.at[k], recv_sem=recv_sems.at[k],
                                        device_id=to, device_id_type=MESH)


GATHER_SEMS = 7


def _gather_copies(w_refs, out_refs, send_sems, recv_sems):
    x, y, c, chips = _place()
    mine = 2 * x + y
    sibling = (x, y, 1 - c)
    copy = functools.partial(_remote, send_sems=send_sems, recv_sems=recv_sems)
    direct, landed, passing, from_sibling = [], [], [], []
    for i, (w, o) in enumerate(zip(w_refs, out_refs)):
        k = GATHER_SEMS * i
        direct.append(copy(w, o.at[mine], k=k, to=sibling))
        from_sibling.append(copy(w, o.at[mine], k=k, to=sibling))
        for j, (cx, cy) in enumerate(chips):
            theirs = 2 * cx + cy
            direct.append(copy(w.at[c], o.at[mine, c], k=k + 1 + j, to=(cx, cy, c)))
            landed.append(copy(w.at[c], o.at[theirs, c], k=k + 1 + j, to=sibling))
            passing.append(copy(o.at[theirs, c], o.at[theirs, c], k=k + 4 + j, to=sibling))
            from_sibling.append(copy(w.at[c], o.at[theirs, 1 - c], k=k + 4 + j, to=sibling))
    return direct, landed, passing, from_sibling


def _gather_finish(copies):
    direct, landed, passing, from_sibling = copies
    for arrival, forward in zip(landed, passing):
        arrival.wait_recv()
        forward.start()
    for arrival in from_sibling:
        arrival.wait_recv()
    for cp in direct + passing:
        cp.wait_send()


def _gather_weights(pieces, smalls):
    count, extra = len(pieces), len(smalls)
    total = count + extra

    def body(*refs):
        w_refs, s_refs = refs[:count], refs[count:total]
        out_refs, sall_refs = refs[total:total + count], refs[total + count:2 * total]
        send_sems, recv_sems, local_sems = refs[2 * total:]
        x, y, c, chips = _place()
        mine = 2 * x + y
        own = [pltpu.make_async_copy(s, sall.at[mine], local_sems.at[i]) for i, (s, sall) in enumerate(zip(s_refs, sall_refs))]
        small = [_remote(s, sall.at[mine], send_sems, recv_sems, GATHER_SEMS * count + 3 * i + j, (cx, cy, c))
                 for i, (s, sall) in enumerate(zip(s_refs, sall_refs)) for j, (cx, cy) in enumerate(chips)]
        copies = _gather_copies(w_refs, out_refs, send_sems, recv_sems)
        for cp in own + small + copies[0]:
            cp.start()
        _gather_finish(copies)
        for cp in small:
            cp.wait_recv()
        for cp in small:
            cp.wait_send()
        for cp in own:
            cp.wait()

    sems = GATHER_SEMS * count + 3 * extra
    return pl.pallas_call(
        body, name="gather_weights",
        out_shape=tuple(jax.ShapeDtypeStruct((N_CHIPS,) + p.shape, p.dtype) for p in list(pieces) + list(smalls)),
        in_specs=[_hbm()] * total, out_specs=(_hbm(),) * total,
        scratch_shapes=[pltpu.SemaphoreType.DMA((sems,)), pltpu.SemaphoreType.DMA((sems,)), pltpu.SemaphoreType.DMA((extra,))],
    )(*pieces, *smalls)


def _sibling_copies(g_refs, got_refs, send_sems, recv_sems):
    x, y, c, _ = _place()
    return [_remote(g.at[:, 1 - c], got, send_sems, recv_sems, i, (x, y, 1 - c)) for i, (g, got) in enumerate(zip(g_refs, got_refs))]


def _exchange_siblings(grads):
    count = len(grads)

    def body(*refs):
        copies = _sibling_copies(refs[:count], refs[count:2 * count], *refs[2 * count:])
        for cp in copies:
            cp.start()
        for cp in copies:
            cp.wait_recv()
        for cp in copies:
            cp.wait_send()

    return pl.pallas_call(
        body, name="exchange_siblings",
        out_shape=tuple(jax.ShapeDtypeStruct((g.shape[0],) + g.shape[2:], F32) for g in grads),
        in_specs=[_hbm()] * count, out_specs=(_hbm(),) * count,
        scratch_shapes=[pltpu.SemaphoreType.DMA((count,)), pltpu.SemaphoreType.DMA((count,))],
    )(*grads)


def _chip_copies(p_refs, got_refs, send_sems, recv_sems):
    x, y, c, chips = _place()
    return [_remote(p.at[2 * cx + cy], got.at[j], send_sems, recv_sems, 3 * i + j, (cx, cy, c))
            for i, (p, got) in enumerate(zip(p_refs, got_refs)) for j, (cx, cy) in enumerate(chips)]


def _share_halves(halves, small):
    count = len(halves)

    def body(*refs):
        h_refs, s_ref = refs[:count], refs[count]
        full_refs, sall_ref = refs[count + 1:2 * count + 1], refs[2 * count + 1]
        send_sems, recv_sems, local_sem = refs[2 * count + 2:]
        x, y, c, _ = _place()
        me = 4 * x + 2 * y + c
        own = pltpu.make_async_copy(s_ref, sall_ref.at[me], local_sem)
        own.start()
        copies = [_remote(h.at[c], full.at[c], send_sems, recv_sems, i, (x, y, 1 - c))
                  for i, (h, full) in enumerate(zip(h_refs, full_refs))]
        for k in range(7):
            dx, dy, dc = ((k + 1) >> 2) & 1, ((k + 1) >> 1) & 1, (k + 1) & 1
            peer = (1 - x if dx else x, 1 - y if dy else y, 1 - c if dc else c)
            copies.append(_remote(s_ref, sall_ref.at[me], send_sems, recv_sems, count + k, peer))
        for cp in copies:
            cp.start()
        for cp in copies:
            cp.wait_recv()
        for cp in copies:
            cp.wait_send()
        own.wait()

    return pl.pallas_call(
        body, name="share_halves",
        out_shape=tuple(jax.ShapeDtypeStruct(h.shape, h.dtype) for h in halves) + (jax.ShapeDtypeStruct((8,) + small.shape, F32),),
        in_specs=[_hbm()] * (count + 1), out_specs=(_hbm(),) * (count + 1), input_output_aliases={i: i for i in range(count)},
        scratch_shapes=[pltpu.SemaphoreType.DMA((count + 7,)), pltpu.SemaphoreType.DMA((count + 7,)), pltpu.SemaphoreType.DMA],
    )(*halves, small)


def _add_sibling(grad, got, core, name):
    chips, _, rows, cols = grad.shape

    def body(core_ref, g_ref, r_ref, sum_ref, send_ref):
        s = g_ref[...] + r_ref[...]
        sum_ref[...] = s
        send_ref[...] = s.astype(send_ref.dtype)

    block = pl.BlockSpec((None, rows, cols), lambda p, core_ref: (p, 0, 0))
    return pl.pallas_call(
        body, name="add_sibling_" + name,
        out_shape=(jax.ShapeDtypeStruct((chips, rows, cols), F32), jax.ShapeDtypeStruct((chips, rows, cols), BF16)),
        grid_spec=pltpu.PrefetchScalarGridSpec(
            num_scalar_prefetch=1, grid=(chips,),
            in_specs=[pl.BlockSpec((None, None, rows, cols), lambda p, core_ref: (p, core_ref[0], 0, 0)), block],
            out_specs=(block, block)),
        compiler_params=_params("parallel"),
    )(core, grad, got)


def _add_chips(part, got, chip_core, name):
    _, rows, cols = part.shape
    tr = rows // 2 if rows % 32 == 0 else rows

    def body(place_ref, p_ref, r_ref, o_ref):
        o_ref[...] = ((p_ref[...] + r_ref[0].astype(F32)) + r_ref[1].astype(F32)) + r_ref[2].astype(F32)

    return pl.pallas_call(
        body, name="add_chips_" + name, out_shape=jax.ShapeDtypeStruct((2, rows, cols), F32),
        grid_spec=pltpu.PrefetchScalarGridSpec(
            num_scalar_prefetch=1, grid=(rows // tr,),
            in_specs=[pl.BlockSpec((None, tr, cols), lambda i, place_ref: (place_ref[0], i, 0)),
                      pl.BlockSpec((3, tr, cols), lambda i, place_ref: (0, i, 0))],
            out_specs=pl.BlockSpec((None, tr, cols), lambda i, place_ref: (place_ref[1], i, 0))),
        compiler_params=_params("parallel"),
    )(chip_core, part, got)


def _adamw(w, g, m, v, name):
    rows, cols = w.shape
    tr = _pick(rows, (3592, 256, 352, 176, 128, 64, 32, 16, 8))

    def body(w_ref, g_ref, m_ref, v_ref, d_ref, nm_ref, nv_ref):
        g = g_ref[...]
        m = ADAM_B1 * m_ref[...] + (1.0 - ADAM_B1) * g
        v = ADAM_B2 * v_ref[...] + (1.0 - ADAM_B2) * (g * g)
        m_hat = m / (1.0 - ADAM_B1 ** ADAM_STEP)
        v_hat = v / (1.0 - ADAM_B2 ** ADAM_STEP)
        d_ref[...] = -ADAM_LR * (m_hat / (jnp.sqrt(v_hat) + ADAM_EPS) + ADAM_WD * w_ref[...])
        nm_ref[...] = m
        nv_ref[...] = v

    block = pl.BlockSpec((tr, cols), lambda i: (i, 0))
    shape = jax.ShapeDtypeStruct((rows, cols), F32)
    return pl.pallas_call(
        body, name="adamw_" + name, out_shape=(shape, shape, shape), grid=(rows // tr,),
        in_specs=[block] * 4, out_specs=(block,) * 3, compiler_params=_params("parallel"),
    )(w, g, m, v)


WEIGHTS = ("meta_tokens", "mix_pre_norm", "mix_post_norm", "ffn_pre_norm", "ffn_post_norm", "w_in", "conv_qkv", "a_log",
           "dt_bias", "gdn_norm", "conv_sc", "w_out", "w_gate", "w_up", "w_down")


def kernel(x, meta_tokens, mix_pre_norm, mix_post_norm, ffn_pre_norm, ffn_post_norm, w_in, conv_qkv, a_log, dt_bias, gdn_norm, conv_sc, w_out, w_gate, w_up, w_down, loss_target, m_meta_tokens, m_mix_pre_norm, m_mix_post_norm, m_ffn_pre_norm, m_ffn_post_norm, m_w_in, m_conv_qkv, m_a_log, m_dt_bias, m_gdn_norm, m_conv_sc, m_w_out, m_w_gate, m_w_up, m_w_down, v_meta_tokens, v_mix_pre_norm, v_mix_post_norm, v_ffn_pre_norm, v_ffn_post_norm, v_w_in, v_conv_qkv, v_a_log, v_dt_bias, v_gdn_norm, v_conv_sc, v_w_out, v_w_gate, v_w_up, v_w_down):
    d = x.shape[-1]
    two_d = lambda a: a.reshape(a.shape[-2:])
    weights = dict(zip(WEIGHTS, (meta_tokens, mix_pre_norm, mix_post_norm, ffn_pre_norm, ffn_post_norm, w_in, conv_qkv, a_log,
                                 dt_bias, gdn_norm, conv_sc, w_out, w_gate, w_up, w_down)))
    m_in = dict(zip(WEIGHTS, (m_meta_tokens, m_mix_pre_norm, m_mix_post_norm, m_ffn_pre_norm, m_ffn_post_norm, m_w_in, m_conv_qkv,
                              m_a_log, m_dt_bias, m_gdn_norm, m_conv_sc, m_w_out, m_w_gate, m_w_up, m_w_down)))
    v_in = dict(zip(WEIGHTS, (v_meta_tokens, v_mix_pre_norm, v_mix_post_norm, v_ffn_pre_norm, v_ffn_post_norm, v_w_in, v_conv_qkv,
                              v_a_log, v_dt_bias, v_gdn_norm, v_conv_sc, v_w_out, v_w_gate, v_w_up, v_w_down)))
    core = lax.axis_index("c")
    chip = 2 * lax.axis_index("x") + lax.axis_index("y")
    core_arg = core.reshape(1).astype(jnp.int32)
    chip_core = jnp.stack([chip, core]).astype(jnp.int32)
    whole = lambda a: a.reshape(a.shape[:-3] + (2 * a.shape[-2], d))
    by_rows = lambda n, a: two_d(a).T if n in ("w_in", "w_gate", "w_up") else two_d(a)

    shard = {n: by_rows(n, weights[n]).astype(MXU_DTYPE) for n in MATRICES}
    shard["w_in"] = jnp.pad(shard["w_in"], ((0, IN_SHARD_PAD - IN_SHARD), (0, 0)))
    w_in_all, *small_all = _gather_weights([_halves(shard["w_in"])], [two_d(weights[n]) for n in SHARDED_SMALL])
    w_in_t = _in_to_kernel_order(whole(w_in_all))
    conv_qkv_full, conv_sc_full, meta_full = (jnp.concatenate([a[p] for p in range(N_CHIPS)], axis=1) for a in small_all)

    sq, grad_x, g, sums = _local_step(
        x, loss_target, meta_full, (mix_pre_norm, mix_post_norm, ffn_pre_norm, ffn_post_norm), w_in_t, conv_qkv_full, a_log,
        dt_bias, gdn_norm, conv_sc_full, [_halves(shard[n]) for n in LATER], core_arg)

    totals = [_add_chips(part, got, chip_core, n) for n, (part, got) in zip(MATRICES, sums)]
    *shared, packed_all = _share_halves(totals, _pack_small(dict(g, loss=sq)))
    grads = {n: whole(a) for n, a in zip(MATRICES, shared)}
    grads["w_in"] = grads["w_in"][:IN_SHARD]
    grads.update(_sum_devices(packed_all, chip.reshape(1).astype(jnp.int32)))
    loss = (0.5 / d) * grads.pop("loss")[0, 0]

    outs = [[], [], [], []]
    for n in WEIGHTS:
        shape = weights[n].shape
        delta, new_m, new_v = _adamw(by_rows(n, weights[n]), grads[n], by_rows(n, m_in[n]), by_rows(n, v_in[n]), n)
        for out, a in zip(outs, (grads[n], delta, new_m, new_v)):
            out.append((a.T if n in ("w_in", "w_gate", "w_up") else a).reshape(shape))
    return (loss, grad_x, *outs[0], *outs[1], *outs[2], *outs[3])
```

```python
import functools

import jax
import jax.numpy as jnp
from jax import lax
from jax.experimental import pallas as pl
from jax.experimental.pallas import tpu as pltpu

F32 = jnp.float32
BF16 = jnp.bfloat16
MXU_DTYPE = jnp.bfloat16
MESH = pl.DeviceIdType.MESH

D_MODEL = 1024
N_META = 16
HEADS = 4
HEAD_DIM = 128
GDN_WIDTH = HEADS * HEAD_DIM
GDN_CONV = 4
CHUNK = 64
SC_WIDTH = D_MODEL - GDN_WIDTH
SC_CONV = 3
D_FF = 2816
IN_WIDTH = 4 * GDN_WIDTH + 2 * HEADS + 3 * SC_WIDTH
IN_PAD = 3840
BA_COL = (4 * GDN_WIDTH + 3 * SC_WIDTH) // 128
EPS = 1e-6
LANES = 128
N_CHIPS = 4
VMEM_LIMIT = 48 * 2 ** 20

ADAM_LR = 0.001
ADAM_B1 = 0.9
ADAM_B2 = 0.999
ADAM_EPS = 1e-08
ADAM_WD = 0.01
ADAM_STEP = 10


def _pick(n, candidates):
    for c in candidates:
        if n % c == 0:
            return c
    return n


def _row_tile(n):
    return _pick(n, (352, 256, 176, 128, 64, 32, 16, 8))


def _params(*sem):
    return pltpu.CompilerParams(dimension_semantics=sem, vmem_limit_bytes=VMEM_LIMIT)


def _sigmoid(x):
    return 0.5 * jnp.tanh(0.5 * x) + 0.5


def _softplus(x):
    return jnp.maximum(x, 0.0) + jnp.log(1.0 + jnp.exp(-jnp.abs(x)))


def _dsilu(x, s):
    return s * (1.0 + x * (1.0 - s))


def _mm(a, b, mode, out_dtype, name, init=None, exchange=None):
    if mode == "tn":
        k_dim, m_dim = a.shape
    else:
        m_dim, k_dim = a.shape
    n_dim = b.shape[0] if mode == "nt" else b.shape[1]
    rows = (1056, 1024, 704, 512, 256, 128) if init is not None else (2112, 1056, 1024, 704, 512, 256, 128)
    tm = _pick(m_dim, (1408, 1280, 1024, 512, 256, 128) if mode == "tn" else rows)
    tn = _pick(n_dim, (1408, 1280, 1024, 768, 512, 256, 128))
    tk = _pick(k_dim, (1408, 1280, 1056, 1024, 512, 256, 128))
    nk = k_dim // tk
    if mode == "nn":
        a_spec = pl.BlockSpec((tm, tk), lambda i, j, k: (i, k))
        b_spec = pl.BlockSpec((tk, tn), lambda i, j, k: (k, j))
        dims = (((1,), (0,)), ((), ()))
    elif mode == "nt":
        a_spec = pl.BlockSpec((tm, tk), lambda i, j, k: (i, k))
        b_spec = pl.BlockSpec((tn, tk), lambda i, j, k: (j, k))
        dims = (((1,), (1,)), ((), ()))
    else:
        a_spec = pl.BlockSpec((tk, tm), lambda i, j, k: (k, i))
        b_spec = pl.BlockSpec((tk, tn), lambda i, j, k: (k, j))
        dims = (((0,), (0,)), ((), ()))

    out_spec = pl.BlockSpec((tm, tn), lambda i, j, k: (i, j))
    grid = (m_dim // tm, n_dim // tn, nk)
    parts = () if exchange is None else tuple(exchange)
    count = len(parts)
    first_in = 2 if init is None else 3

    assert out_dtype == F32

    def body(a_ref, b_ref, *rest):
        o_ref = rest[first_in - 2 + count]
        k = pl.program_id(2)
        step = (pl.program_id(0) * grid[1] + pl.program_id(1)) * nk + k
        if count:
            copies = _chip_copies(rest[first_in - 2:first_in - 2 + count], rest[first_in - 1 + count:first_in - 1 + 2 * count],
                                  *rest[first_in - 1 + 2 * count:])

            @pl.when(step == 0)
            def _():
                for cp in copies:
                    cp.start()

        p = lax.dot_general(a_ref[...], b_ref[...], dims, preferred_element_type=F32)
        if nk == 1:
            o_ref[...] = p if init is None else rest[0][...] + p
        else:
            @pl.when(k == 0)
            def _():
                o_ref[...] = p if init is None else rest[0][...] + p

            @pl.when(k > 0)
            def _():
                o_ref[...] += p

        if count:
            @pl.when(step == grid[0] * grid[1] * nk - 1)
            def _():
                for cp in copies:
                    cp.wait_recv()
                for cp in copies:
                    cp.wait_send()

    out = pl.pallas_call(
        body, name=name,
        out_shape=(jax.ShapeDtypeStruct((m_dim, n_dim), out_dtype),)
        + tuple(jax.ShapeDtypeStruct((3,) + p.shape[1:], p.dtype) for p in parts),
        grid=grid,
        in_specs=[a_spec, b_spec] + ([] if init is None else [out_spec]) + [_hbm()] * count,
        out_specs=(out_spec,) + (_hbm(),) * count,
        scratch_shapes=[pltpu.SemaphoreType.DMA((3 * count,)), pltpu.SemaphoreType.DMA((3 * count,))] if count else [],
        compiler_params=_params(*(("arbitrary",) * 3 if count else ("parallel", "parallel", "arbitrary"))),
    )(a, b, *(() if init is None else (init,)), *parts)
    return out[0] if not count else out


def _rms_apply(x, w):
    r = lax.rsqrt(jnp.mean(x * x, axis=-1, keepdims=True) + EPS)
    return x * r * w


def _rms_bwd(x, w, dy):
    r = lax.rsqrt(jnp.mean(x * x, axis=-1, keepdims=True) + EPS)
    xh = x * r
    dyw = dy * w
    dx = r * (dyw - xh * jnp.mean(dyw * xh, axis=-1, keepdims=True))
    return dx, jnp.sum(dy * xh, axis=0, keepdims=True)


def _accumulate(ref, first, value):
    @pl.when(first)
    def _():
        ref[...] = value

    @pl.when(jnp.logical_not(first))
    def _():
        ref[...] += value


def _rows(tr, width):
    return pl.BlockSpec((tr, width), lambda i: (i, 0))


def _vec(width):
    return pl.BlockSpec((1, width), lambda i: (0, 0))


def _rms_fwd(h, w, name):
    n, d = h.shape
    tr = _row_tile(n)

    def body(h_ref, w_ref, u_ref):
        u_ref[...] = _rms_apply(h_ref[...], w_ref[...]).astype(u_ref.dtype)

    return pl.pallas_call(
        body, name=name, out_shape=jax.ShapeDtypeStruct((n, d), MXU_DTYPE), grid=(n // tr,),
        in_specs=[_rows(tr, d), _vec(d)], out_specs=_rows(tr, d), compiler_params=_params("parallel"),
    )(h, w)


def _mix_residual(h0, mix, w_post, w_pre):
    n, d = h0.shape
    tr = _row_tile(n)

    def body(h0_ref, mix_ref, wpost_ref, wpre_ref, h1_ref, u2_ref):
        h1 = h0_ref[...] + _rms_apply(mix_ref[...], wpost_ref[...])
        h1_ref[...] = h1
        u2_ref[...] = _rms_apply(h1, wpre_ref[...]).astype(u2_ref.dtype)

    return pl.pallas_call(
        body, name="mix_residual",
        out_shape=(jax.ShapeDtypeStruct((n, d), F32), jax.ShapeDtypeStruct((n, d), MXU_DTYPE)), grid=(n // tr,),
        in_specs=[_rows(tr, d), _rows(tr, d), _vec(d), _vec(d)], out_specs=(_rows(tr, d), _rows(tr, d)),
        compiler_params=_params("parallel"),
    )(h0, mix, w_post, w_pre)


NT_DIMS = (((1,), (1,)), ((), ()))


def _ffn_tiles(n):
    return _pick(n, (1056, 704, 512, 256, 128)), _pick(D_FF, (1408, 256, 128))


def _swiglu_fwd(u, w_gate_t, w_up_t, w_next):
    n, d = u.shape
    tm, tn = _ffn_tiles(n)
    grid = (D_FF // tn, n // tm)

    def body(u_ref, wg_ref, wu_ref, wn_ref, g_ref, up_ref, act_ref, wall_ref, send_sems, recv_sems):
        gather = _gather_copies([wn_ref], [wall_ref], send_sems, recv_sems)
        step = pl.program_id(0) * grid[1] + pl.program_id(1)

        @pl.when(step == 0)
        def _():
            for cp in gather[0]:
                cp.start()

        a = u_ref[...]
        g = lax.dot_general(a, wg_ref[...], NT_DIMS, preferred_element_type=F32)
        up = lax.dot_general(a, wu_ref[...], NT_DIMS, preferred_element_type=F32)
        g_ref[...] = g.astype(g_ref.dtype)
        up_ref[...] = up.astype(up_ref.dtype)
        act_ref[...] = (g * _sigmoid(g) * up).astype(act_ref.dtype)

        @pl.when(step == grid[0] * grid[1] - 1)
        def _():
            _gather_finish(gather)

    tile = pl.BlockSpec((tm, tn), lambda j, i: (i, j))
    weight = pl.BlockSpec((tn, d), lambda j, i: (j, 0))
    wide = jax.ShapeDtypeStruct((n, D_FF), MXU_DTYPE)
    return pl.pallas_call(
        body, name="swiglu_fwd",
        out_shape=(wide, wide, jax.ShapeDtypeStruct((n, D_FF), MXU_DTYPE),
                   jax.ShapeDtypeStruct((N_CHIPS,) + w_next.shape, w_next.dtype)),
        grid=grid,
        in_specs=[pl.BlockSpec((tm, d), lambda j, i: (i, 0)), weight, weight, _hbm()], out_specs=(tile, tile, tile, _hbm()),
        scratch_shapes=[pltpu.SemaphoreType.DMA((GATHER_SEMS,)), pltpu.SemaphoreType.DMA((GATHER_SEMS,))],
        compiler_params=_params("arbitrary", "arbitrary"),
    )(u, w_gate_t, w_up_t, w_next)


def _swiglu_bwd(dffn, w_down, gate, up):
    n, d = dffn.shape
    tm, tn = _ffn_tiles(n)

    def body(dy_ref, w_ref, g_ref, u_ref, dg_ref, du_ref):
        da = lax.dot_general(dy_ref[...], w_ref[...], NT_DIMS, preferred_element_type=F32)
        g = g_ref[...].astype(F32)
        s = _sigmoid(g)
        dg_ref[...] = (da * u_ref[...].astype(F32) * _dsilu(g, s)).astype(dg_ref.dtype)
        du_ref[...] = (da * g * s).astype(du_ref.dtype)

    tile = pl.BlockSpec((tm, tn), lambda j, i: (i, j))
    shape = jax.ShapeDtypeStruct((n, D_FF), MXU_DTYPE)
    return pl.pallas_call(
        body, name="swiglu_bwd", out_shape=(shape, shape), grid=(D_FF // tn, n // tm),
        in_specs=[pl.BlockSpec((tm, d), lambda j, i: (i, 0)), pl.BlockSpec((tn, d), lambda j, i: (j, 0)), tile, tile],
        out_specs=(tile, tile), compiler_params=_params("parallel", "parallel"),
    )(dffn, w_down, gate, up)


def _loss_head(h1, ffn, w_post, target, rows_per_seq, x_offset):
    n, d = h1.shape
    tr = _row_tile(rows_per_seq)
    tiles_per_seq = rows_per_seq // tr

    def body(h1_ref, ffn_ref, w_ref, t_ref, dh2_ref, dffn_ref, dw_ref, sq_ref):
        i = pl.program_id(0)
        w = w_ref[...]
        f = ffn_ref[...]
        r = lax.rsqrt(jnp.mean(f * f, axis=-1, keepdims=True) + EPS)
        fh = f * r
        row = lax.rem(i, tiles_per_seq) * tr + lax.broadcasted_iota(jnp.int32, (tr, 1), 0)
        err = jnp.where(row >= x_offset, h1_ref[...] + fh * w - t_ref[...], 0.0)
        dh2 = err * (1.0 / d)
        dh2_ref[...] = dh2
        dyw = dh2 * w
        dffn_ref[...] = (r * (dyw - fh * jnp.mean(dyw * fh, axis=-1, keepdims=True))).astype(dffn_ref.dtype)
        _accumulate(dw_ref, i == 0, jnp.sum(dh2 * fh, axis=0, keepdims=True))
        _accumulate(sq_ref, i == 0, jnp.sum(jnp.sum(err * err, axis=1, keepdims=True), axis=0, keepdims=True))

    return pl.pallas_call(
        body, name="loss_head",
        out_shape=(jax.ShapeDtypeStruct((n, d), F32), jax.ShapeDtypeStruct((n, d), MXU_DTYPE),
                   jax.ShapeDtypeStruct((1, d), F32), jax.ShapeDtypeStruct((1, 1), F32)),
        grid=(n // tr,),
        in_specs=[_rows(tr, d), _rows(tr, d), _vec(d), _rows(tr, d)],
        out_specs=(_rows(tr, d), _rows(tr, d), _vec(d), _vec(1)),
        compiler_params=_params("arbitrary"),
    )(h1, ffn, w_post, target)


def _mid_bwd(h1, mix, w_mix_post, w_ffn_pre, dh2, du2, grads):
    n, d = h1.shape
    tr = _row_tile(n)
    count = len(grads)

    def body(h1_ref, mix_ref, wpost_ref, wpre_ref, dh2_ref, du2_ref, *rest):
        g_refs, (dh1_ref, dmix_ref, dwpre_ref, dwpost_ref), got_refs = rest[:count], rest[count:count + 4], rest[count + 4:2 * count + 4]
        exchange = _sibling_copies(g_refs, got_refs, *rest[2 * count + 4:])
        i = pl.program_id(0)

        @pl.when(i == 0)
        def _():
            for cp in exchange:
                cp.start()

        dx, dwpre = _rms_bwd(h1_ref[...], wpre_ref[...], du2_ref[...])
        dh1 = dh2_ref[...] + dx
        dh1_ref[...] = dh1
        dmix, dwpost = _rms_bwd(mix_ref[...], wpost_ref[...], dh1)
        dmix_ref[...] = dmix.astype(dmix_ref.dtype)
        _accumulate(dwpre_ref, i == 0, dwpre)
        _accumulate(dwpost_ref, i == 0, dwpost)

        @pl.when(i == n // tr - 1)
        def _():
            for cp in exchange:
                cp.wait_recv()
            for cp in exchange:
                cp.wait_send()

    dh1, dmix, dwpre, dwpost, *got = pl.pallas_call(
        body, name="mid_bwd",
        out_shape=(jax.ShapeDtypeStruct((n, d), F32), jax.ShapeDtypeStruct((n, d), MXU_DTYPE),
                   jax.ShapeDtypeStruct((1, d), F32), jax.ShapeDtypeStruct((1, d), F32))
        + tuple(jax.ShapeDtypeStruct((g.shape[0],) + g.shape[2:], F32) for g in grads),
        grid=(n // tr,),
        in_specs=[_rows(tr, d), _rows(tr, d), _vec(d), _vec(d), _rows(tr, d), _rows(tr, d)] + [_hbm()] * count,
        out_specs=(_rows(tr, d), _rows(tr, d), _vec(d), _vec(d)) + (_hbm(),) * count,
        scratch_shapes=[pltpu.SemaphoreType.DMA((count,)), pltpu.SemaphoreType.DMA((count,))],
        compiler_params=_params("arbitrary"),
    )(h1, mix, w_mix_post, w_ffn_pre, dh2, du2, *grads)
    return dh1, dmix, dwpre, dwpost, got


def _in_bwd(h0, w_pre, dh1, du1):
    n, d = h0.shape
    tr = _row_tile(n)

    def body(h0_ref, w_ref, dh1_ref, du1_ref, dh0_ref, dw_ref):
        dx, dw = _rms_bwd(h0_ref[...], w_ref[...], du1_ref[...])
        dh0_ref[...] = dh1_ref[...] + dx
        _accumulate(dw_ref, pl.program_id(0) == 0, dw)

    return pl.pallas_call(
        body, name="in_bwd",
        out_shape=(jax.ShapeDtypeStruct((n, d), F32), jax.ShapeDtypeStruct((1, d), F32)), grid=(n // tr,),
        in_specs=[_rows(tr, d), _vec(d), _rows(tr, d), _rows(tr, d)], out_specs=(_rows(tr, d), _vec(d)),
        compiler_params=_params("arbitrary"),
    )(h0, w_pre, dh1, du1)


def _lane_is(lo, hi):
    lane = lax.broadcasted_iota(jnp.int32, (1, LANES), 1)
    return jnp.logical_and(lane >= lo, lane < hi)


def _gates_fwd(proj, a_log_l, dt_bias_l, rows_per_seq, pad_rows):
    n = proj.shape[0]
    tr = _row_tile(rows_per_seq)
    tiles_per_seq = rows_per_seq // tr

    def body(p_ref, a_ref, dt_ref, o_ref):
        x = p_ref[...]
        row = lax.rem(pl.program_id(0), tiles_per_seq) * tr + lax.broadcasted_iota(jnp.int32, (tr, 1), 0)
        g = -jnp.exp(a_ref[...]) * _softplus(x + dt_ref[...])
        val = jnp.where(_lane_is(0, HEADS), _sigmoid(x), jnp.where(_lane_is(HEADS, 2 * HEADS), g, 0.0))
        o_ref[...] = jnp.where(row >= pad_rows, val, 0.0)

    return pl.pallas_call(
        body, name="gates_fwd", out_shape=jax.ShapeDtypeStruct((n, LANES), F32), grid=(n // tr,),
        in_specs=[pl.BlockSpec((tr, LANES), lambda i: (i, BA_COL)), _vec(LANES), _vec(LANES)],
        out_specs=_rows(tr, LANES), compiler_params=_params("parallel"),
    )(proj, a_log_l, dt_bias_l)


def _gates_bwd(proj, dbg, a_log_l, dt_bias_l, rows_per_seq, pad_rows):
    n = proj.shape[0]
    tr = _row_tile(rows_per_seq)
    tiles_per_seq = rows_per_seq // tr

    def body(p_ref, d_ref, a_ref, dt_ref, dx_ref, da_ref, ddt_ref):
        i = pl.program_id(0)
        x = p_ref[...]
        d = d_ref[...]
        row = lax.rem(i, tiles_per_seq) * tr + lax.broadcasted_iota(jnp.int32, (tr, 1), 0)
        live = row >= pad_rows
        beta = _sigmoid(x)
        ea = jnp.exp(a_ref[...])
        xa = x + dt_ref[...]
        g = -ea * _softplus(xa)
        is_g = _lane_is(HEADS, 2 * HEADS)
        d_alogit = jnp.where(jnp.logical_and(live, is_g), d * (-ea) * _sigmoid(xa), 0.0)
        d_blogit = jnp.where(jnp.logical_and(live, _lane_is(0, HEADS)), d * beta * (1.0 - beta), 0.0)
        dx_ref[:, :LANES] = (d_alogit + d_blogit).astype(dx_ref.dtype)
        dx_ref[:, LANES:] = jnp.zeros((tr, LANES), dx_ref.dtype)
        _accumulate(da_ref, i == 0, jnp.sum(jnp.where(jnp.logical_and(live, is_g), d * g, 0.0), axis=0, keepdims=True))
        _accumulate(ddt_ref, i == 0, jnp.sum(d_alogit, axis=0, keepdims=True))

    return pl.pallas_call(
        body, name="gates_bwd",
        out_shape=(jax.ShapeDtypeStruct((n, 2 * LANES), MXU_DTYPE), jax.ShapeDtypeStruct((1, LANES), F32),
                   jax.ShapeDtypeStruct((1, LANES), F32)),
        grid=(n // tr,),
        in_specs=[pl.BlockSpec((tr, LANES), lambda i: (i, BA_COL)), _rows(tr, LANES), _vec(LANES), _vec(LANES)],
        out_specs=(_rows(tr, 2 * LANES), _vec(LANES), _vec(LANES)),
        compiler_params=_params("arbitrary"),
    )(proj, dbg, a_log_l, dt_bias_l)


HALO = 8


def _halo_scratch(rs):
    return pltpu.VMEM((rs + 2 * HALO, LANES), F32)


def _stage(ref, x):
    rs = x.shape[0]
    ref[0:HALO, :] = jnp.zeros((HALO, LANES), F32)
    ref[HALO + rs:, :] = jnp.zeros((HALO, LANES), F32)
    ref[HALO:HALO + rs, :] = x


def _shifted(ref, k, rs):
    return ref[pl.ds(HALO - k, rs), :]


def _causal_conv(x, x_staged, w, width):
    acc = w[width - 1:width, :] * x
    for i in range(width - 1):
        acc = acc + w[i:i + 1, :] * _shifted(x_staged, width - 1 - i, x.shape[0])
    return acc


def _anti_causal_conv(dy, dy_staged, w, width):
    acc = w[width - 1:width, :] * dy
    for i in range(width - 1):
        acc = acc + w[i:i + 1, :] * _shifted(dy_staged, -(width - 1 - i), dy.shape[0])
    return acc


def _conv_weight_grad(dy, x, x_staged, width):
    taps = [_shifted(x_staged, width - 1 - i, x.shape[0]) for i in range(width - 1)] + [x]
    return jnp.concatenate([jnp.sum(dy * tap, axis=0, keepdims=True) for tap in taps], axis=0)


def _seq_cols(rs, col0, heads):
    return pl.BlockSpec((rs, heads * LANES), lambda j, b: (b, col0 // heads + j))


def _tap_cols(width, col0, heads):
    return pl.BlockSpec((width, heads * LANES), lambda j, b: (0, col0 // heads + j))


def _lanes_of(h):
    return slice(h * LANES, (h + 1) * LANES)


def _qkv_fwd(proj, conv_w, kind, rs):
    n = proj.shape[0]
    col0 = {"q": 0, "k": HEADS, "v": 2 * HEADS}[kind]
    hb = HEADS

    def body(p_ref, w_ref, o_ref, staged):
        for h in range(hb):
            pre = p_ref[:, _lanes_of(h)]
            _stage(staged, pre)
            c = _causal_conv(pre, staged, w_ref[:, _lanes_of(h)], GDN_CONV)
            s = c * _sigmoid(c)
            if kind != "v":
                s = s * lax.rsqrt(jnp.sum(s * s, axis=-1, keepdims=True) + EPS)
            if kind == "q":
                s = s * (HEAD_DIM ** -0.5)
            o_ref[:, _lanes_of(h)] = s

    return pl.pallas_call(
        body, name="qkv_fwd_" + kind, out_shape=jax.ShapeDtypeStruct((n, GDN_WIDTH), F32), grid=(HEADS // hb, n // rs),
        in_specs=[_seq_cols(rs, col0, hb), _tap_cols(GDN_CONV, col0, hb)],
        out_specs=_seq_cols(rs, 0, hb), scratch_shapes=[_halo_scratch(rs)], compiler_params=_params("parallel", "parallel"),
    )(proj, conv_w)


def _qkv_bwd(dy, proj, conv_w, kind, rs):
    n = proj.shape[0]
    col0 = {"q": 0, "k": HEADS, "v": 2 * HEADS}[kind]
    hb = HEADS

    def body(dy_ref, p_ref, w_ref, dp_ref, dw_ref, pre_staged, dc_staged):
        for h in range(hb):
            lanes = _lanes_of(h)
            pre = p_ref[:, lanes]
            w = w_ref[:, lanes]
            _stage(pre_staged, pre)
            c = _causal_conv(pre, pre_staged, w, GDN_CONV)
            sg = _sigmoid(c)
            s = c * sg
            ds = dy_ref[:, lanes]
            if kind == "q":
                ds = ds * (HEAD_DIM ** -0.5)
            if kind != "v":
                r = lax.rsqrt(jnp.sum(s * s, axis=-1, keepdims=True) + EPS)
                sh = s * r
                ds = r * (ds - sh * jnp.sum(ds * sh, axis=-1, keepdims=True))
            dc = ds * _dsilu(c, sg)
            _stage(dc_staged, dc)
            dp_ref[:, lanes] = _anti_causal_conv(dc, dc_staged, w, GDN_CONV).astype(dp_ref.dtype)
            _accumulate(dw_ref.at[:, lanes], pl.program_id(1) == 0, _conv_weight_grad(dc, pre, pre_staged, GDN_CONV))

    return pl.pallas_call(
        body, name="qkv_bwd_" + kind,
        out_shape=(jax.ShapeDtypeStruct((n, GDN_WIDTH), MXU_DTYPE), jax.ShapeDtypeStruct((GDN_CONV, GDN_WIDTH), F32)),
        grid=(HEADS // hb, n // rs),
        in_specs=[_seq_cols(rs, 0, hb), _seq_cols(rs, col0, hb), _tap_cols(GDN_CONV, col0, hb)],
        out_specs=(_seq_cols(rs, 0, hb), _tap_cols(GDN_CONV, 0, hb)),
        scratch_shapes=[_halo_scratch(rs), _halo_scratch(rs)],
        compiler_params=_params("parallel", "arbitrary"),
    )(dy, proj, conv_w)


SC_COL = 4 * HEADS


def _sc_fwd(proj, conv_w, rs):
    n = proj.shape[0]

    hb = 2

    def body(x_ref, b_ref, c_ref, w_ref, y_ref, staged):
        for h in range(hb):
            lanes = _lanes_of(h)
            u = c_ref[:, lanes] * x_ref[:, lanes]
            _stage(staged, u)
            y_ref[:, lanes] = (b_ref[:, lanes] * _causal_conv(u, staged, w_ref[:, lanes], SC_CONV)).astype(y_ref.dtype)

    return pl.pallas_call(
        body, name="sc_fwd", out_shape=jax.ShapeDtypeStruct((n, SC_WIDTH), MXU_DTYPE), grid=(HEADS // hb, n // rs),
        in_specs=[_seq_cols(rs, SC_COL, hb), _seq_cols(rs, SC_COL + 4, hb), _seq_cols(rs, SC_COL + 8, hb),
                  _tap_cols(SC_CONV, 0, hb)],
        out_specs=_seq_cols(rs, 0, hb), scratch_shapes=[_halo_scratch(rs)], compiler_params=_params("parallel", "parallel"),
    )(proj, proj, proj, conv_w)


def _sc_bwd(dcat, proj, conv_w, rs):
    n = proj.shape[0]
    hb = 2

    def body(dy_ref, x_ref, b_ref, c_ref, w_ref, dx_ref, db_ref, dc_ref, dw_ref, u_staged, dcv_staged):
        for h in range(hb):
            lanes = _lanes_of(h)
            w = w_ref[:, lanes]
            x = x_ref[:, lanes]
            cc = c_ref[:, lanes]
            u = cc * x
            _stage(u_staged, u)
            dy = dy_ref[:, lanes]
            db_ref[:, lanes] = (dy * _causal_conv(u, u_staged, w, SC_CONV)).astype(db_ref.dtype)
            dcv = dy * b_ref[:, lanes]
            _stage(dcv_staged, dcv)
            du = _anti_causal_conv(dcv, dcv_staged, w, SC_CONV)
            dx_ref[:, lanes] = (du * cc).astype(dx_ref.dtype)
            dc_ref[:, lanes] = (du * x).astype(dc_ref.dtype)
            _accumulate(dw_ref.at[:, lanes], pl.program_id(1) == 0, _conv_weight_grad(dcv, u, u_staged, SC_CONV))

    piece = jax.ShapeDtypeStruct((n, SC_WIDTH), MXU_DTYPE)
    return pl.pallas_call(
        body, name="sc_bwd", out_shape=(piece, piece, piece, jax.ShapeDtypeStruct((SC_CONV, SC_WIDTH), F32)),
        grid=(HEADS // hb, n // rs),
        in_specs=[_seq_cols(rs, HEADS, hb), _seq_cols(rs, SC_COL, hb), _seq_cols(rs, SC_COL + 4, hb),
                  _seq_cols(rs, SC_COL + 8, hb), _tap_cols(SC_CONV, 0, hb)],
        out_specs=(_seq_cols(rs, 0, hb), _seq_cols(rs, 0, hb), _seq_cols(rs, 0, hb), _tap_cols(SC_CONV, 0, hb)),
        scratch_shapes=[_halo_scratch(rs), _halo_scratch(rs)],
        compiler_params=_params("parallel", "arbitrary"),
    )(dcat, proj, proj, proj, conv_w)


Z_COL = 3 * HEADS


def _gate_fwd(o, proj, gdn_norm, rs):
    n = proj.shape[0]

    hb = HEADS

    def body(o_ref, z_ref, w_ref, y_ref):
        for h in range(hb):
            lanes = _lanes_of(h)
            z = z_ref[:, lanes]
            y_ref[:, lanes] = (_rms_apply(o_ref[:, lanes], w_ref[...]) * z * _sigmoid(z)).astype(y_ref.dtype)

    return pl.pallas_call(
        body, name="gate_fwd", out_shape=jax.ShapeDtypeStruct((n, GDN_WIDTH), MXU_DTYPE), grid=(HEADS // hb, n // rs),
        in_specs=[_seq_cols(rs, 0, hb), _seq_cols(rs, Z_COL, hb), pl.BlockSpec((1, LANES), lambda j, b: (0, 0))],
        out_specs=_seq_cols(rs, 0, hb), compiler_params=_params("parallel", "parallel"),
    )(o, proj, gdn_norm)


def _gate_bwd(dcat, o, proj, gdn_norm, rs):
    n = proj.shape[0]
    hb = 2

    def body(dy_ref, o_ref, z_ref, w_ref, do_ref, dz_ref, dw_ref):
        w = w_ref[...]
        dw_step = jnp.zeros((1, LANES), F32)
        for h in range(hb):
            lanes = _lanes_of(h)
            z = z_ref[:, lanes]
            o = o_ref[:, lanes]
            dy = dy_ref[:, lanes]
            s = _sigmoid(z)
            dz_ref[:, lanes] = (dy * _rms_apply(o, w) * _dsilu(z, s)).astype(dz_ref.dtype)
            do, dw = _rms_bwd(o, w, dy * z * s)
            do_ref[:, lanes] = do
            dw_step = dw_step + dw
        _accumulate(dw_ref, jnp.logical_and(pl.program_id(0) == 0, pl.program_id(1) == 0), dw_step)

    return pl.pallas_call(
        body, name="gate_bwd",
        out_shape=(jax.ShapeDtypeStruct((n, GDN_WIDTH), F32), jax.ShapeDtypeStruct((n, GDN_WIDTH), MXU_DTYPE),
                   jax.ShapeDtypeStruct((1, LANES), F32)),
        grid=(HEADS // hb, n // rs),
        in_specs=[_seq_cols(rs, 0, hb), _seq_cols(rs, 0, hb), _seq_cols(rs, Z_COL, hb), pl.BlockSpec((1, LANES), lambda j, b: (0, 0))],
        out_specs=(_seq_cols(rs, 0, hb), _seq_cols(rs, 0, hb), pl.BlockSpec((1, LANES), lambda j, b: (0, 0))),
        compiler_params=_params("arbitrary", "arbitrary"),
    )(dcat, o, proj, gdn_norm)


def _dot(a, b):
    return jnp.dot(a.astype(MXU_DTYPE), b.astype(MXU_DTYPE), preferred_element_type=F32)


def _dot_nt(a, b):
    return lax.dot_general(a.astype(MXU_DTYPE), b.astype(MXU_DTYPE), (((1,), (1,)), ((), ())),
                           preferred_element_type=F32)


def _dot_tn(a, b):
    return lax.dot_general(a.astype(MXU_DTYPE), b.astype(MXU_DTYPE), (((0,), (0,)), ((), ())),
                           preferred_element_type=F32)


def _split(x):
    hi = x.astype(MXU_DTYPE)
    return hi, (x - hi.astype(F32)).astype(MXU_DTYPE)


def _dot_split(a, b):
    mm = functools.partial(jnp.dot, preferred_element_type=F32)
    return mm(a[0], b[0]) + (mm(a[0], b[1]) + mm(a[1], b[0]))


def _unit_lower_inverses(mats, eye):
    inv = [eye - a for a in mats]
    power = [_split(a) for a in mats]
    square = [_dot_split(p, p) for p in power]
    inv = [i + _dot_split(_split(i), _split(s)) for i, s in zip(inv, square)]
    span = 4
    while span < CHUNK:
        square = [_dot(s, s) for s in square]
        inv = [i + _dot(i, s) for i, s in zip(inv, square)]
        span *= 2
    return inv


def _chunk_masks():
    ii = lax.broadcasted_iota(jnp.int32, (CHUNK, CHUNK), 0)
    jj = lax.broadcasted_iota(jnp.int32, (CHUNK, CHUNK), 1)
    return ii, jj


def _chunk_decay(g_col, ii, jj):
    incl = ii >= jj
    g_row = jnp.sum(jnp.where(ii == jj, g_col, 0.0), axis=0, keepdims=True)
    gc_col = jnp.sum(jnp.where(incl, g_row, 0.0), axis=1, keepdims=True)
    gc_row = jnp.sum(jnp.where(ii <= jj, g_col, 0.0), axis=0, keepdims=True)
    g_total = jnp.sum(g_row, axis=1, keepdims=True)
    decay = jnp.where(incl, jnp.exp(jnp.where(incl, gc_col - gc_row, 0.0)), 0.0)
    return gc_col, g_total, decay


def _gdn_segments(rs, candidates):
    chunks = rs // CHUNK
    seg_chunks = _pick(chunks, candidates)
    return chunks, seg_chunks, chunks // seg_chunks


def _head_lanes(h):
    return slice(h * HEAD_DIM, (h + 1) * HEAD_DIM)


def _gdn_fwd(q, k, v, bg, rs, pieces):
    n = q.shape[0]
    batch = n // rs
    chunks, seg_chunks, segs = _gdn_segments(rs, (11, 8, 4, 2))
    seg_rows = seg_chunks * CHUNK
    chains = [(b, h) for b in range(batch) for h in range(HEADS)]
    each = lambda f, *lists: [f(*args) for args in zip(*lists)]
    count = len(pieces)

    def body(q_ref, k_ref, v_ref, bg_ref, *rest):
        w_refs, (o_ref, s_ref, t_ref), out_refs = rest[:count], rest[count:count + 3], rest[count + 3:2 * count + 3]
        state_ref, send_sems, recv_sems = rest[2 * count + 3:]
        gather = _gather_copies(w_refs, out_refs, send_sems, recv_sems)

        @pl.when(pl.program_id(0) == 0)
        def _():
            state_ref[...] = jnp.zeros_like(state_ref)
            for cp in gather[0]:
                cp.start()

        ii, jj = _chunk_masks()
        incl = ii >= jj
        eye = (ii == jj).astype(F32)

        def chunk(c, carry):
            rows = pl.ds(pl.multiple_of(c * CHUNK, CHUNK), CHUNK)
            bgc = [bg_ref[b, rows, :] for b in range(batch)]
            qc = [q_ref[b, rows, _head_lanes(h)] for b, h in chains]
            kc = [k_ref[b, rows, _head_lanes(h)] for b, h in chains]
            vc = [v_ref[b, rows, _head_lanes(h)] for b, h in chains]
            beta = [bgc[b][:, h:h + 1] for b, h in chains]
            state = [state_ref[b, h] for b, h in chains]
            dec = [_chunk_decay(bgc[b][:, HEADS + h:HEADS + h + 1], ii, jj) for b, h in chains]
            gc_col, g_total, decay = ([d[i] for d in dec] for i in range(3))
            kb = each(lambda x, y: x * y, kc, beta)
            a = each(lambda x, y, d: jnp.where(ii > jj, _dot_nt(x, y) * d, 0.0), kb, kc, decay)
            t_inv = _unit_lower_inverses(a, eye)
            eg = [jnp.exp(g) for g in gc_col]
            u = each(lambda t, x, y: _dot(t, x * y), t_inv, vc, beta)
            w = each(lambda t, x, e: _dot(t, x * e), t_inv, kb, eg)
            qk = each(lambda x, y, d: jnp.where(incl, _dot_nt(x, y) * d, 0.0), qc, kc, decay)
            v_new = each(lambda x, y, s: x - _dot(y, s), u, w, state)
            o = each(lambda x, e, s, m, vn: _dot(x * e, s) + _dot(m, vn), qc, eg, state, qk, v_new)
            new_state = each(lambda s, gt, x, g, vn: s * jnp.exp(gt) + _dot_tn(x * jnp.exp(gt - g), vn),
                             state, g_total, kc, gc_col, v_new)
            for i, (b, h) in enumerate(chains):
                s_ref[b, h, c] = state[i]
                t_ref[b, h, c] = t_inv[i]
                o_ref[b, rows, _head_lanes(h)] = o[i]
                state_ref[b, h] = new_state[i]
            return carry

        lax.fori_loop(0, seg_chunks, chunk, 0)

        @pl.when(pl.program_id(0) == segs - 1)
        def _():
            _gather_finish(gather)

    rows_spec = lambda width: pl.BlockSpec((batch, seg_rows, width), lambda s: (0, s, 0))
    per_chunk = lambda r, c: pl.BlockSpec((batch, HEADS, seg_chunks, r, c), lambda s: (0, 0, s, 0, 0))
    as_seqs = lambda a: a.reshape(batch, rs, a.shape[-1])
    sems = GATHER_SEMS * count
    o, states, t_invs, *gathered = pl.pallas_call(
        body, name="gdn_fwd",
        out_shape=(jax.ShapeDtypeStruct((batch, rs, GDN_WIDTH), F32),
                   jax.ShapeDtypeStruct((batch, HEADS, chunks, HEAD_DIM, HEAD_DIM), F32),
                   jax.ShapeDtypeStruct((batch, HEADS, chunks, CHUNK, CHUNK), F32))
        + tuple(jax.ShapeDtypeStruct((N_CHIPS,) + p.shape, p.dtype) for p in pieces),
        grid=(segs,),
        in_specs=[rows_spec(GDN_WIDTH), rows_spec(GDN_WIDTH), rows_spec(GDN_WIDTH), rows_spec(LANES)] + [_hbm()] * count,
        out_specs=(rows_spec(GDN_WIDTH), per_chunk(HEAD_DIM, HEAD_DIM), per_chunk(CHUNK, CHUNK)) + (_hbm(),) * count,
        scratch_shapes=[pltpu.VMEM((batch, HEADS, HEAD_DIM, HEAD_DIM), F32), pltpu.SemaphoreType.DMA((sems,)),
                        pltpu.SemaphoreType.DMA((sems,))],
        compiler_params=_params("arbitrary"),
    )(as_seqs(q), as_seqs(k), as_seqs(v), as_seqs(bg), *pieces)
    return o.reshape(n, GDN_WIDTH), states, t_invs, gathered


def _gdn_bwd(do, q, k, v, bg, states, t_invs, rs, parts):
    n = q.shape[0]
    batch = n // rs
    chunks, seg_chunks, segs = _gdn_segments(rs, (3, 4, 2))
    seg_rows = seg_chunks * CHUNK
    chains = [(b, h) for b in range(batch) for h in range(HEADS)]
    each = lambda f, *lists: [f(*args) for args in zip(*lists)]
    count = len(parts)

    def body(do_ref, q_ref, k_ref, v_ref, bg_ref, s_ref, t_ref, *rest):
        p_refs, (dq_ref, dk_ref, dv_ref, dbg_ref), got_refs = rest[:count], rest[count:count + 4], rest[count + 4:2 * count + 4]
        dstate_ref, send_sems, recv_sems = rest[2 * count + 4:]
        exchange = _chip_copies(p_refs, got_refs, send_sems, recv_sems)

        @pl.when(pl.program_id(0) == 0)
        def _():
            dstate_ref[...] = jnp.zeros_like(dstate_ref)
            for cp in exchange:
                cp.start()

        ii, jj = _chunk_masks()
        incl = ii >= jj
        strict = ii > jj
        lane = lax.broadcasted_iota(jnp.int32, (1, LANES), 1)

        def rowsum(x):
            return jnp.sum(x, axis=1, keepdims=True)

        def total(x):
            return jnp.sum(rowsum(x), axis=0, keepdims=True)

        def chunk(step, carry):
            c = seg_chunks - 1 - step
            rows = pl.ds(pl.multiple_of(c * CHUNK, CHUNK), CHUNK)
            bgc = [bg_ref[b, rows, :] for b in range(batch)]
            qc = [q_ref[b, rows, _head_lanes(h)] for b, h in chains]
            kc = [k_ref[b, rows, _head_lanes(h)] for b, h in chains]
            vc = [v_ref[b, rows, _head_lanes(h)] for b, h in chains]
            doc = [do_ref[b, rows, _head_lanes(h)] for b, h in chains]
            beta = [bgc[b][:, h:h + 1] for b, h in chains]
            state = [s_ref[b, h, c] for b, h in chains]
            t_inv = [t_ref[b, h, c] for b, h in chains]
            d_state = [dstate_ref[b, h] for b, h in chains]
            dec = [_chunk_decay(bgc[b][:, HEADS + h:HEADS + h + 1], ii, jj) for b, h in chains]
            gc_col, g_total, decay = ([d[i] for d in dec] for i in range(3))
            kb = each(lambda x, y: x * y, kc, beta)
            vb = each(lambda x, y: x * y, vc, beta)
            eg = [jnp.exp(g) for g in gc_col]
            kbg = each(lambda x, y: x * y, kb, eg)
            a = each(lambda x, y, d: jnp.where(strict, _dot_nt(x, y) * d, 0.0), kb, kc, decay)
            qk = each(lambda x, y, d: jnp.where(incl, _dot_nt(x, y) * d, 0.0), qc, kc, decay)
            w = each(_dot, t_inv, kbg)
            u = each(_dot, t_inv, vb)
            q_dec = each(lambda x, y: x * y, qc, eg)
            ek = each(lambda gt, g: jnp.exp(gt - g), g_total, gc_col)
            k_dec = each(lambda x, y: x * y, kc, ek)
            g_last = [jnp.exp(gt) for gt in g_total]
            v_new = each(lambda x, y, s: x - _dot(y, s), u, w, state)
            dv_new = each(lambda m, d, x, ds: _dot_tn(m, d) + _dot(x, ds), qk, doc, k_dec, d_state)
            dqk = each(lambda d, vn: jnp.where(incl, _dot_nt(d, vn), 0.0), doc, v_new)
            dq_dec = each(_dot_nt, doc, state)
            dk_dec = each(_dot_nt, v_new, d_state)
            dg_last = each(lambda s, ds: total(s * ds), state, d_state)
            new_d_state = each(lambda x, d, gl, ds, y, dvn: _dot_tn(x, d) + gl * ds - _dot_tn(y, dvn),
                               q_dec, doc, g_last, d_state, w, dv_new)
            dw = each(lambda dvn, s: -_dot_nt(dvn, s), dv_new, state)
            dt = each(lambda dvn, x, y, z: _dot_nt(dvn, x) + _dot_nt(y, z), dv_new, vb, dw, kbg)
            dvb = each(_dot_tn, t_inv, dv_new)
            dkbg = each(_dot_tn, t_inv, dw)
            t_dt = each(_dot_tn, t_inv, dt)
            da = each(lambda x, t: -jnp.where(strict, _dot_nt(x, t), 0.0), t_dt, t_inv)
            dm_a = each(lambda x, y: x * y, da, decay)
            dm_qk = each(lambda x, y: x * y, dqk, decay)
            e = each(lambda x, y, z, t: x * y + z * t, da, a, dqk, qk)
            dkb = each(lambda m, x, y, z: _dot(m, x) + y * z, dm_a, kc, dkbg, eg)
            dk = each(lambda m, x, m2, y, z, t, p, bt: _dot_tn(m, x) + _dot_tn(m2, y) + z * t + p * bt,
                      dm_a, kb, dm_qk, qc, dk_dec, ek, dkb, beta)
            dq = each(lambda m, x, y, z: _dot(m, x) + y * z, dm_qk, kc, dq_dec, eg)
            dbeta = each(lambda x, y, z, t: rowsum(x * y + z * t), dkb, kc, dvb, vc)
            dgc = each(lambda x, p, pd, r, rd, s, sd: rowsum(x) - rowsum(jnp.where(ii == jj, jnp.sum(x, axis=0, keepdims=True), 0.0))
                       + rowsum(p * pd - r * rd + s * sd), e, dq_dec, q_dec, dk_dec, k_dec, dkbg, kbg)
            d_total = each(lambda r, rd, x, gl: total(r * rd) + x * gl, dk_dec, k_dec, dg_last, g_last)
            dg = each(lambda x, t: rowsum(jnp.where(jj >= ii, jnp.sum(jnp.where(ii == jj, x, 0.0), axis=0, keepdims=True), 0.0)) + t,
                      dgc, d_total)
            dbg = [jnp.zeros((CHUNK, LANES), F32) for _ in range(batch)]
            for i, (b, h) in enumerate(chains):
                dstate_ref[b, h] = new_d_state[i]
                dk_ref[b, rows, _head_lanes(h)] = dk[i]
                dq_ref[b, rows, _head_lanes(h)] = dq[i]
                dv_ref[b, rows, _head_lanes(h)] = dvb[i] * beta[i]
                dbg[b] = dbg[b] + jnp.where(lane == h, dbeta[i], 0.0) + jnp.where(lane == HEADS + h, dg[i], 0.0)
            for b in range(batch):
                dbg_ref[b, rows, :] = dbg[b]
            return carry

        lax.fori_loop(0, seg_chunks, chunk, 0)

        @pl.when(pl.program_id(0) == segs - 1)
        def _():
            for cp in exchange:
                cp.wait_recv()
            for cp in exchange:
                cp.wait_send()

    rows_spec = lambda width: pl.BlockSpec((batch, seg_rows, width), lambda s: (0, segs - 1 - s, 0))
    per_chunk = lambda r, c: pl.BlockSpec((batch, HEADS, seg_chunks, r, c), lambda s: (0, 0, segs - 1 - s, 0, 0))
    as_seqs = lambda a: a.reshape(batch, rs, a.shape[-1])
    grad = jax.ShapeDtypeStruct((batch, rs, GDN_WIDTH), F32)
    wide = rows_spec(GDN_WIDTH)
    dq, dk, dv, dbg, *got = pl.pallas_call(
        body, name="gdn_bwd",
        out_shape=(grad, grad, grad, jax.ShapeDtypeStruct((batch, rs, LANES), F32))
        + tuple(jax.ShapeDtypeStruct((3,) + p.shape[1:], p.dtype) for p in parts),
        grid=(segs,),
        in_specs=[wide, wide, wide, wide, rows_spec(LANES), per_chunk(HEAD_DIM, HEAD_DIM), per_chunk(CHUNK, CHUNK)]
        + [_hbm()] * count,
        out_specs=(wide, wide, wide, rows_spec(LANES)) + (_hbm(),) * count,
        scratch_shapes=[pltpu.VMEM((batch, HEADS, HEAD_DIM, HEAD_DIM), F32), pltpu.SemaphoreType.DMA((3 * count,)),
                        pltpu.SemaphoreType.DMA((3 * count,))],
        compiler_params=_params("arbitrary"),
    )(as_seqs(do), as_seqs(q), as_seqs(k), as_seqs(v), as_seqs(bg), states, t_invs, *parts)
    return dq.reshape(n, GDN_WIDTH), dk.reshape(n, GDN_WIDTH), dv.reshape(n, GDN_WIDTH), dbg.reshape(n, LANES), got


def _lane_vec(vals, offset):
    k = vals.shape[1]
    return jnp.pad(vals, ((0, 0), (offset, LANES - offset - k)))


LATER = ("w_out", "w_gate", "w_up", "w_down")


def _halves(a):
    return a.reshape(a.shape[:-2] + (2, a.shape[-2] // 2, a.shape[-1]))


def _local_step(x, target, meta, norms, w_in_t, conv_qkv, a_log, dt_bias, gdn_norm, conv_sc, later_shards, core_arg):
    batch, seq, d = x.shape
    tokens = N_META + seq
    pad_rows = (-tokens) % CHUNK
    rs = tokens + pad_rows
    x_offset = pad_rows + N_META
    n = batch * rs
    w_mix_pre, w_mix_post, w_ffn_pre, w_ffn_post = norms

    head = jnp.concatenate([jnp.zeros((pad_rows, d), F32), meta], axis=0)
    h0 = jnp.concatenate([jnp.broadcast_to(head[None], (batch, x_offset, d)), x], axis=1).reshape(n, d)
    target_p = jnp.pad(target, ((0, 0), (x_offset, 0), (0, 0))).reshape(n, d)
    a_log_l = _lane_vec(a_log, HEADS)
    dt_bias_l = _lane_vec(dt_bias, HEADS)

    u1 = _rms_fwd(h0, w_mix_pre, "rms_mix_pre")
    proj = _mm(u1, w_in_t, "nt", F32, "mm_proj")
    q = _qkv_fwd(proj, conv_qkv, "q", rs)
    k = _qkv_fwd(proj, conv_qkv, "k", rs)
    v = _qkv_fwd(proj, conv_qkv, "v", rs)
    bg = _gates_fwd(proj, a_log_l, dt_bias_l, rs, pad_rows)
    o, states, t_invs, gathered = _gdn_fwd(q, k, v, bg, rs, later_shards[:3])
    w_out, w_gate_t, w_up_t = (a.reshape(-1, d) for a in gathered)
    o_gated = _gate_fwd(o, proj, gdn_norm, rs)
    y_sc = _sc_fwd(proj, conv_sc, rs)
    cat = jnp.concatenate([o_gated, y_sc], axis=1)
    mix = _mm(cat, w_out, "nn", F32, "mm_mix")
    h1, u2 = _mix_residual(h0, mix, w_mix_post, w_ffn_pre)
    gate, up, act, w_down = _swiglu_fwd(u2, w_gate_t, w_up_t, later_shards[3])
    w_down = w_down.reshape(-1, d)
    ffn = _mm(act, w_down, "nn", F32, "mm_down")

    dh2, dffn, d_ffn_post, sq = _loss_head(h1, ffn, w_ffn_post, target_p, rs, x_offset)
    d_w_down = _mm(act, dffn, "tn", F32, "mm_dw_down")
    dgate, dup = _swiglu_bwd(dffn, w_down, gate, up)
    d_w_gate_t = _mm(dgate, u2, "tn", F32, "mm_dw_gate")
    d_w_up_t = _mm(dup, u2, "tn", F32, "mm_dw_up")
    du2 = _mm(dup, w_up_t, "nn", F32, "mm_du2_up", init=_mm(dgate, w_gate_t, "nn", F32, "mm_du2_gate"))
    by_chip = [_halves(g.reshape(N_CHIPS, -1, d)) for g in (d_w_gate_t, d_w_up_t, d_w_down)]
    dh1, dmix, d_ffn_pre, d_mix_post, got_sibling = _mid_bwd(h1, mix, w_mix_post, w_ffn_pre, dh2, du2, by_chip)
    dcat = _mm(dmix, w_out, "nt", F32, "mm_dcat")
    d_w_out = _halves(_mm(cat, dmix, "tn", F32, "mm_dw_out").reshape(N_CHIPS, -1, d))
    by_chip, got_sibling = [d_w_out] + by_chip, list(_exchange_siblings([d_w_out])) + got_sibling
    sums = [_add_sibling(a, b, core_arg, name) for name, a, b in zip(LATER, by_chip, got_sibling)]
    do, dz, d_gdn_norm = _gate_bwd(dcat, o, proj, gdn_norm, rs)
    dscx, dscb, dscc, d_conv_sc = _sc_bwd(dcat, proj, conv_sc, rs)
    dq, dk, dv, dbg, got_chips = _gdn_bwd(do, q, k, v, bg, states, t_invs, rs, [send for _, send in sums[:3]])
    dpq, dwq = _qkv_bwd(dq, proj, conv_qkv, "q", rs)
    dpk, dwk = _qkv_bwd(dk, proj, conv_qkv, "k", rs)
    dpv, dwv = _qkv_bwd(dv, proj, conv_qkv, "v", rs)
    d_conv_qkv = jnp.concatenate([dwq, dwk, dwv], axis=1)
    dba, d_a_log_l, d_dt_bias_l = _gates_bwd(proj, dbg, a_log_l, dt_bias_l, rs, pad_rows)
    dproj = jnp.concatenate([dpq, dpk, dpv, dz, dscx, dscb, dscc, dba], axis=1)
    d_w_in_t, got_down = _mm(dproj, u1, "tn", F32, "mm_dw_in", exchange=[sums[3][1]])
    got_chips.append(got_down)
    g_in = _halves(_in_from_kernel_order(d_w_in_t))
    sums.insert(0, _add_sibling(g_in, _exchange_siblings([g_in])[0], core_arg, "w_in"))
    du1, got_in = _mm(dproj, w_in_t, "nn", F32, "mm_du1", exchange=[sums[0][1]])
    got_chips.insert(0, got_in)
    dh0, d_mix_pre = _in_bwd(h0, w_mix_pre, dh1, du1)

    dh0 = dh0.reshape(batch, rs, d)
    grads = dict(
        meta_tokens=jnp.sum(dh0[:, pad_rows:x_offset], axis=0),
        mix_pre_norm=d_mix_pre, mix_post_norm=d_mix_post, ffn_pre_norm=d_ffn_pre, ffn_post_norm=d_ffn_post,
        conv_qkv=d_conv_qkv,
        a_log=d_a_log_l[:, HEADS:2 * HEADS], dt_bias=d_dt_bias_l[:, HEADS:2 * HEADS],
        gdn_norm=d_gdn_norm, conv_sc=d_conv_sc,
    )
    return sq, dh0[:, x_offset:], grads, [(part, got) for (part, _), got in zip(sums, got_chips)]


MATRICES = ("w_in", "w_out", "w_gate", "w_up", "w_down")
IN_SHARD = IN_WIDTH // N_CHIPS
IN_SHARD_PAD = 928


def _in_to_kernel_order(by_chip):
    w_t = by_chip[:, :IN_SHARD].reshape(IN_WIDTH, by_chip.shape[-1])
    lo, hi = 4 * GDN_WIDTH, 4 * GDN_WIDTH + 2 * HEADS
    return jnp.concatenate([w_t[:lo], w_t[hi:], w_t[lo:hi], jnp.zeros((IN_PAD - IN_WIDTH, w_t.shape[1]), w_t.dtype)], axis=0)


def _in_from_kernel_order(g_t):
    lo, hi = 4 * GDN_WIDTH, IN_WIDTH - 2 * HEADS
    g = jnp.concatenate([g_t[:lo], g_t[hi:IN_WIDTH], g_t[lo:hi]], axis=0).reshape(N_CHIPS, IN_SHARD, g_t.shape[-1])
    return jnp.pad(g, ((0, 0), (0, IN_SHARD_PAD - IN_SHARD), (0, 0)))


PACK_LANES = 3 * GDN_WIDTH
PACKED = dict(mix_pre_norm=(0, 1, 0, D_MODEL), mix_post_norm=(1, 1, 0, D_MODEL), ffn_pre_norm=(2, 1, 0, D_MODEL),
              ffn_post_norm=(3, 1, 0, D_MODEL), a_log=(4, 1, 0, HEADS), dt_bias=(5, 1, 0, HEADS), loss=(6, 1, 0, 1),
              gdn_norm=(7, 1, 0, HEAD_DIM), conv_qkv=(8, GDN_CONV, 0, 3 * GDN_WIDTH), conv_sc=(0, SC_CONV, D_MODEL, SC_WIDTH),
              meta_tokens=(16, N_META, 0, D_MODEL))
PACK_ROWS = 32
SHARDED_SMALL = ("conv_qkv", "conv_sc", "meta_tokens")


def _pack_small(values):
    names = list(PACKED)

    def body(*refs):
        out_ref = refs[-1]
        out_ref[...] = jnp.zeros_like(out_ref)
        for name, ref in zip(names, refs):
            row, rows, lane0, lanes = PACKED[name]
            out_ref[row:row + rows, lane0:lane0 + lanes] = ref[...]

    return pl.pallas_call(body, name="pack_small", out_shape=jax.ShapeDtypeStruct((PACK_ROWS, PACK_LANES), F32))(
        *[values[name] for name in names])


def _sum_devices(packed_all, chip):
    names = list(PACKED)

    def body(chip_ref, all_ref, *rest):
        shard_refs, out_refs = rest[:len(SHARDED_SMALL)], rest[len(SHARDED_SMALL):]

        def total(ref, rows, lanes):
            acc = ref[0, rows, lanes]
            for k in range(1, 8):
                acc = acc + ref[k, rows, lanes]
            return acc

        for name, out in zip(names, out_refs):
            row, rows, lane0, lanes = PACKED[name]
            if name in SHARDED_SMALL:
                out[...] = total(shard_refs[SHARDED_SMALL.index(name)], slice(0, rows), slice(None))
            else:
                out[...] = total(all_ref, slice(row, row + rows), slice(lane0, lane0 + lanes))

    def shard_spec(name):
        row, rows, lane0, lanes = PACKED[name]
        height, width = max(rows, 8), lanes // N_CHIPS
        assert row % height == 0 and lane0 % width == 0
        return pl.BlockSpec((8, height, width), lambda i, chip_ref: (0, row // height, lane0 // width + chip_ref[0]))

    def out_shape(name):
        _, rows, _, lanes = PACKED[name]
        return jax.ShapeDtypeStruct((rows, lanes // N_CHIPS if name in SHARDED_SMALL else lanes), F32)

    whole = lambda shape: pl.BlockSpec(shape, lambda i, chip_ref: (0,) * len(shape))
    outs = pl.pallas_call(
        body, name="sum_devices", out_shape=tuple(out_shape(n) for n in names),
        grid_spec=pltpu.PrefetchScalarGridSpec(
            num_scalar_prefetch=1, grid=(1,),
            in_specs=[whole(packed_all.shape)] + [shard_spec(n) for n in SHARDED_SMALL],
            out_specs=tuple(whole(out_shape(n).shape) for n in names)),
    )(chip, packed_all, *[packed_all] * len(SHARDED_SMALL))
    return dict(zip(names, outs))


def _hbm():
    return pl.BlockSpec(memory_space=pl.ANY)


def _place():
    x, y, c = lax.axis_index("x"), lax.axis_index("y"), lax.axis_index("c")
    chips = ((1 - x, y), (x, 1 - y), (1 - x, 1 - y))
    return x, y, c, chips


def _remote(src, dst, send_sems, recv_sems, k, to):
    return pltpu.make_async_remote_copy(src_ref=src, dst_ref=dst, send_sem=send_sems.at[k], recv_sem=recv_sems.at[k],
                                        device_id=to, device_id_type=MESH)


GATHER_SEMS = 7


def _gather_copies(w_refs, out_refs, send_sems, recv_sems):
    x, y, c, chips = _place()
    mine = 2 * x + y
    sibling = (x, y, 1 - c)
    copy = functools.partial(_remote, send_sems=send_sems, recv_sems=recv_sems)
    direct, landed, passing, from_sibling = [], [], [], []
    for i, (w, o) in enumerate(zip(w_refs, out_refs)):
        k = GATHER_SEMS * i
        direct.append(copy(w, o.at[mine], k=k, to=sibling))
        from_sibling.append(copy(w, o.at[mine], k=k, to=sibling))
        for j, (cx, cy) in enumerate(chips):
            theirs = 2 * cx + cy
            direct.append(copy(w.at[c], o.at[mine, c], k=k + 1 + j, to=(cx, cy, c)))
            landed.append(copy(w.at[c], o.at[theirs, c], k=k + 1 + j, to=sibling))
            passing.append(copy(o.at[theirs, c], o.at[theirs, c], k=k + 4 + j, to=sibling))
            from_sibling.append(copy(w.at[c], o.at[theirs, 1 - c], k=k + 4 + j, to=sibling))
    return direct, landed, passing, from_sibling


def _gather_finish(copies):
    direct, landed, passing, from_sibling = copies
    for arrival, forward in zip(landed, passing):
        arrival.wait_recv()
        forward.start()
    for arrival in from_sibling:
        arrival.wait_recv()
    for cp in direct + passing:
        cp.wait_send()


def _gather_weights(pieces, smalls):
    count, extra = len(pieces), len(smalls)
    total = count + extra

    def body(*refs):
        w_refs, s_refs = refs[:count], refs[count:total]
        out_refs, sall_refs = refs[total:total + count], refs[total + count:2 * total]
        send_sems, recv_sems, local_sems = refs[2 * total:]
        x, y, c, chips = _place()
        mine = 2 * x + y
        own = [pltpu.make_async_copy(s, sall.at[mine], local_sems.at[i]) for i, (s, sall) in enumerate(zip(s_refs, sall_refs))]
        small = [_remote(s, sall.at[mine], send_sems, recv_sems, GATHER_SEMS * count + 3 * i + j, (cx, cy, c))
                 for i, (s, sall) in enumerate(zip(s_refs, sall_refs)) for j, (cx, cy) in enumerate(chips)]
        copies = _gather_copies(w_refs, out_refs, send_sems, recv_sems)
        for cp in own + small + copies[0]:
            cp.start()
        _gather_finish(copies)
        for cp in small:
            cp.wait_recv()
        for cp in small:
            cp.wait_send()
        for cp in own:
            cp.wait()

    sems = GATHER_SEMS * count + 3 * extra
    return pl.pallas_call(
        body, name="gather_weights",
        out_shape=tuple(jax.ShapeDtypeStruct((N_CHIPS,) + p.shape, p.dtype) for p in list(pieces) + list(smalls)),
        in_specs=[_hbm()] * total, out_specs=(_hbm(),) * total,
        scratch_shapes=[pltpu.SemaphoreType.DMA((sems,)), pltpu.SemaphoreType.DMA((sems,)), pltpu.SemaphoreType.DMA((extra,))],
    )(*pieces, *smalls)


def _sibling_copies(g_refs, got_refs, send_sems, recv_sems):
    x, y, c, _ = _place()
    return [_remote(g.at[:, 1 - c], got, send_sems, recv_sems, i, (x, y, 1 - c)) for i, (g, got) in enumerate(zip(g_refs, got_refs))]


def _exchange_siblings(grads):
    count = len(grads)

    def body(*refs):
        copies = _sibling_copies(refs[:count], refs[count:2 * count], *refs[2 * count:])
        for cp in copies:
            cp.start()
        for cp in copies:
            cp.wait_recv()
        for cp in copies:
            cp.wait_send()

    return pl.pallas_call(
        body, name="exchange_siblings",
        out_shape=tuple(jax.ShapeDtypeStruct((g.shape[0],) + g.shape[2:], F32) for g in grads),
        in_specs=[_hbm()] * count, out_specs=(_hbm(),) * count,
        scratch_shapes=[pltpu.SemaphoreType.DMA((count,)), pltpu.SemaphoreType.DMA((count,))],
    )(*grads)


def _chip_copies(p_refs, got_refs, send_sems, recv_sems):
    x, y, c, chips = _place()
    return [_remote(p.at[2 * cx + cy], got.at[j], send_sems, recv_sems, 3 * i + j, (cx, cy, c))
            for i, (p, got) in enumerate(zip(p_refs, got_refs)) for j, (cx, cy) in enumerate(chips)]


def _share_halves(halves, small):
    count = len(halves)

    def body(*refs):
        h_refs, s_ref = refs[:count], refs[count]
        full_refs, sall_ref = refs[count + 1:2 * count + 1], refs[2 * count + 1]
        send_sems, recv_sems, local_sem = refs[2 * count + 2:]
        x, y, c, _ = _place()
        me = 4 * x + 2 * y + c
        own = pltpu.make_async_copy(s_ref, sall_ref.at[me], local_sem)
        own.start()
        copies = [_remote(h.at[c], full.at[c], send_sems, recv_sems, i, (x, y, 1 - c))
                  for i, (h, full) in enumerate(zip(h_refs, full_refs))]
        for k in range(7):
            dx, dy, dc = ((k + 1) >> 2) & 1, ((k + 1) >> 1) & 1, (k + 1) & 1
            peer = (1 - x if dx else x, 1 - y if dy else y, 1 - c if dc else c)
            copies.append(_remote(s_ref, sall_ref.at[me], send_sems, recv_sems, count + k, peer))
        for cp in copies:
            cp.start()
        for cp in copies:
            cp.wait_recv()
        for cp in copies:
            cp.wait_send()
        own.wait()

    return pl.pallas_call(
        body, name="share_halves",
        out_shape=tuple(jax.ShapeDtypeStruct(h.shape, h.dtype) for h in halves) + (jax.ShapeDtypeStruct((8,) + small.shape, F32),),
        in_specs=[_hbm()] * (count + 1), out_specs=(_hbm(),) * (count + 1), input_output_aliases={i: i for i in range(count)},
        scratch_shapes=[pltpu.SemaphoreType.DMA((count + 7,)), pltpu.SemaphoreType.DMA((count + 7,)), pltpu.SemaphoreType.DMA],
    )(*halves, small)


def _add_sibling(grad, got, core, name):
    chips, _, rows, cols = grad.shape

    def body(core_ref, g_ref, r_ref, sum_ref, send_ref):
        s = g_ref[...] + r_ref[...]
        sum_ref[...] = s
        send_ref[...] = s.astype(send_ref.dtype)

    block = pl.BlockSpec((None, rows, cols), lambda p, core_ref: (p, 0, 0))
    return pl.pallas_call(
        body, name="add_sibling_" + name,
        out_shape=(jax.ShapeDtypeStruct((chips, rows, cols), F32), jax.ShapeDtypeStruct((chips, rows, cols), BF16)),
        grid_spec=pltpu.PrefetchScalarGridSpec(
            num_scalar_prefetch=1, grid=(chips,),
            in_specs=[pl.BlockSpec((None, None, rows, cols), lambda p, core_ref: (p, core_ref[0], 0, 0)), block],
            out_specs=(block, block)),
        compiler_params=_params("parallel"),
    )(core, grad, got)


def _add_chips(part, got, chip_core, name):
    _, rows, cols = part.shape
    tr = rows // 2 if rows % 32 == 0 else rows

    def body(place_ref, p_ref, r_ref, o_ref):
        o_ref[...] = ((p_ref[...] + r_ref[0].astype(F32)) + r_ref[1].astype(F32)) + r_ref[2].astype(F32)

    return pl.pallas_call(
        body, name="add_chips_" + name, out_shape=jax.ShapeDtypeStruct((2, rows, cols), F32),
        grid_spec=pltpu.PrefetchScalarGridSpec(
            num_scalar_prefetch=1, grid=(rows // tr,),
            in_specs=[pl.BlockSpec((None, tr, cols), lambda i, place_ref: (place_ref[0], i, 0)),
                      pl.BlockSpec((3, tr, cols), lambda i, place_ref: (0, i, 0))],
            out_specs=pl.BlockSpec((None, tr, cols), lambda i, place_ref: (place_ref[1], i, 0))),
        compiler_params=_params("parallel"),
    )(chip_core, part, got)


def _adamw(w, g, m, v, name):
    rows, cols = w.shape
    tr = _pick(rows, (3592, 256, 352, 176, 128, 64, 32, 16, 8))

    def body(w_ref, g_ref, m_ref, v_ref, d_ref, nm_ref, nv_ref):
        g = g_ref[...]
        m = ADAM_B1 * m_ref[...] + (1.0 - ADAM_B1) * g
        v = ADAM_B2 * v_ref[...] + (1.0 - ADAM_B2) * (g * g)
        m_hat = m / (1.0 - ADAM_B1 ** ADAM_STEP)
        v_hat = v / (1.0 - ADAM_B2 ** ADAM_STEP)
        d_ref[...] = -ADAM_LR * (m_hat / (jnp.sqrt(v_hat) + ADAM_EPS) + ADAM_WD * w_ref[...])
        nm_ref[...] = m
        nv_ref[...] = v

    block = pl.BlockSpec((tr, cols), lambda i: (i, 0))
    shape = jax.ShapeDtypeStruct((rows, cols), F32)
    return pl.pallas_call(
        body, name="adamw_" + name, out_shape=(shape, shape, shape), grid=(rows // tr,),
        in_specs=[block] * 4, out_specs=(block,) * 3, compiler_params=_params("parallel"),
    )(w, g, m, v)


WEIGHTS = ("meta_tokens", "mix_pre_norm", "mix_post_norm", "ffn_pre_norm", "ffn_post_norm", "w_in", "conv_qkv", "a_log",
           "dt_bias", "gdn_norm", "conv_sc", "w_out", "w_gate", "w_up", "w_down")


def kernel(x, meta_tokens, mix_pre_norm, mix_post_norm, ffn_pre_norm, ffn_post_norm, w_in, conv_qkv, a_log, dt_bias, gdn_norm, conv_sc, w_out, w_gate, w_up, w_down, loss_target, m_meta_tokens, m_mix_pre_norm, m_mix_post_norm, m_ffn_pre_norm, m_ffn_post_norm, m_w_in, m_conv_qkv, m_a_log, m_dt_bias, m_gdn_norm, m_conv_sc, m_w_out, m_w_gate, m_w_up, m_w_down, v_meta_tokens, v_mix_pre_norm, v_mix_post_norm, v_ffn_pre_norm, v_ffn_post_norm, v_w_in, v_conv_qkv, v_a_log, v_dt_bias, v_gdn_norm, v_conv_sc, v_w_out, v_w_gate, v_w_up, v_w_down):
    d = x.shape[-1]
    two_d = lambda a: a.reshape(a.shape[-2:])
    weights = dict(zip(WEIGHTS, (meta_tokens, mix_pre_norm, mix_post_norm, ffn_pre_norm, ffn_post_norm, w_in, conv_qkv, a_log,
                                 dt_bias, gdn_norm, conv_sc, w_out, w_gate, w_up, w_down)))
    m_in = dict(zip(WEIGHTS, (m_meta_tokens, m_mix_pre_norm, m_mix_post_norm, m_ffn_pre_norm, m_ffn_post_norm, m_w_in, m_conv_qkv,
                              m_a_log, m_dt_bias, m_gdn_norm, m_conv_sc, m_w_out, m_w_gate, m_w_up, m_w_down)))
    v_in = dict(zip(WEIGHTS, (v_meta_tokens, v_mix_pre_norm, v_mix_post_norm, v_ffn_pre_norm, v_ffn_post_norm, v_w_in, v_conv_qkv,
                              v_a_log, v_dt_bias, v_gdn_norm, v_conv_sc, v_w_out, v_w_gate, v_w_up, v_w_down)))
    core = lax.axis_index("c")
    chip = 2 * lax.axis_index("x") + lax.axis_index("y")
    core_arg = core.reshape(1).astype(jnp.int32)
    chip_core = jnp.stack([chip, core]).astype(jnp.int32)
    whole = lambda a: a.reshape(a.shape[:-3] + (2 * a.shape[-2], d))
    by_rows = lambda n, a: two_d(a).T if n in ("w_in", "w_gate", "w_up") else two_d(a)

    shard = {n: by_rows(n, weights[n]).astype(MXU_DTYPE) for n in MATRICES}
    shard["w_in"] = jnp.pad(shard["w_in"], ((0, IN_SHARD_PAD - IN_SHARD), (0, 0)))
    w_in_all, *small_all = _gather_weights([_halves(shard["w_in"])], [two_d(weights[n]) for n in SHARDED_SMALL])
    w_in_t = _in_to_kernel_order(whole(w_in_all))
    conv_qkv_full, conv_sc_full, meta_full = (jnp.concatenate([a[p] for p in range(N_CHIPS)], axis=1) for a in small_all)

    sq, grad_x, g, sums = _local_step(
        x, loss_target, meta_full, (mix_pre_norm, mix_post_norm, ffn_pre_norm, ffn_post_norm), w_in_t, conv_qkv_full, a_log,
        dt_bias, gdn_norm, conv_sc_full, [_halves(shard[n]) for n in LATER], core_arg)

    totals = [_add_chips(part, got, chip_core, n) for n, (part, got) in zip(MATRICES, sums)]
    *shared, packed_all = _share_halves(totals, _pack_small(dict(g, loss=sq)))
    grads = {n: whole(a) for n, a in zip(MATRICES, shared)}
    grads["w_in"] = grads["w_in"][:IN_SHARD]
    grads.update(_sum_devices(packed_all, chip.reshape(1).astype(jnp.int32)))
    loss = (0.5 / d) * grads.pop("loss")[0, 0]

    outs = [[], [], [], []]
    for n in WEIGHTS:
        shape = weights[n].shape
        delta, new_m, new_v = _adamw(by_rows(n, weights[n]), grads[n], by_rows(n, m_in[n]), by_rows(n, v_in[n]), n)
        for out, a in zip(outs, (grads[n], delta, new_m, new_v)):
            out.append((a.T if n in ("w_in", "w_gate", "w_up") else a).reshape(shape))
    return (loss, grad_x, *outs[0], *outs[1], *outs[2], *outs[3])
```

```python
import functools

import jax
import jax.numpy as jnp
from jax import lax
from jax.experimental import pallas as pl
from jax.experimental.pallas import tpu as pltpu

F32 = jnp.float32
BF16 = jnp.bfloat16
MXU_DTYPE = jnp.bfloat16
MESH = pl.DeviceIdType.MESH

D_MODEL = 1024
N_META = 16
HEADS = 4
HEAD_DIM = 128
GDN_WIDTH = HEADS * HEAD_DIM
GDN_CONV = 4
CHUNK = 64
SC_WIDTH = D_MODEL - GDN_WIDTH
SC_CONV = 3
D_FF = 2816
IN_WIDTH = 4 * GDN_WIDTH + 2 * HEADS + 3 * SC_WIDTH
IN_PAD = 3840
BA_COL = (4 * GDN_WIDTH + 3 * SC_WIDTH) // 128
EPS = 1e-6
LANES = 128
N_CHIPS = 4
VMEM_LIMIT = 48 * 2 ** 20

ADAM_LR = 0.001
ADAM_B1 = 0.9
ADAM_B2 = 0.999
ADAM_EPS = 1e-08
ADAM_WD = 0.01
ADAM_STEP = 10


def _pick(n, candidates):
    for c in candidates:
        if n % c == 0:
            return c
    return n


def _row_tile(n):
    return _pick(n, (352, 256, 176, 128, 64, 32, 16, 8))


def _params(*sem):
    return pltpu.CompilerParams(dimension_semantics=sem, vmem_limit_bytes=VMEM_LIMIT)


def _sigmoid(x):
    return 0.5 * jnp.tanh(0.5 * x) + 0.5


def _softplus(x):
    return jnp.maximum(x, 0.0) + jnp.log(1.0 + jnp.exp(-jnp.abs(x)))


def _dsilu(x, s):
    return s * (1.0 + x * (1.0 - s))


def _mm(a, b, mode, out_dtype, name, init=None, exchange=None):
    if mode == "tn":
        k_dim, m_dim = a.shape
    else:
        m_dim, k_dim = a.shape
    n_dim = b.shape[0] if mode == "nt" else b.shape[1]
    rows = (1056, 1024, 704, 512, 256, 128) if init is not None else (2112, 1056, 1024, 704, 512, 256, 128)
    tm = _pick(m_dim, (1408, 1280, 1024, 512, 256, 128) if mode == "tn" else rows)
    tn = _pick(n_dim, (1408, 1280, 1024, 768, 512, 256, 128))
    tk = _pick(k_dim, (1408, 1280, 1056, 1024, 512, 256, 128))
    nk = k_dim // tk
    if mode == "nn":
        a_spec = pl.BlockSpec((tm, tk), lambda i, j, k: (i, k))
        b_spec = pl.BlockSpec((tk, tn), lambda i, j, k: (k, j))
        dims = (((1,), (0,)), ((), ()))
    elif mode == "nt":
        a_spec = pl.BlockSpec((tm, tk), lambda i, j, k: (i, k))
        b_spec = pl.BlockSpec((tn, tk), lambda i, j, k: (j, k))
        dims = (((1,), (1,)), ((), ()))
    else:
        a_spec = pl.BlockSpec((tk, tm), lambda i, j, k: (k, i))
        b_spec = pl.BlockSpec((tk, tn), lambda i, j, k: (k, j))
        dims = (((0,), (0,)), ((), ()))

    out_spec = pl.BlockSpec((tm, tn), lambda i, j, k: (i, j))
    grid = (m_dim // tm, n_dim // tn, nk)
    parts = () if exchange is None else tuple(exchange)
    count = len(parts)
    first_in = 2 if init is None else 3

    assert out_dtype == F32

    def body(a_ref, b_ref, *rest):
        o_ref = rest[first_in - 2 + count]
        k = pl.program_id(2)
        step = (pl.program_id(0) * grid[1] + pl.program_id(1)) * nk + k
        if count:
            copies = _chip_copies(rest[first_in - 2:first_in - 2 + count], rest[first_in - 1 + count:first_in - 1 + 2 * count],
                                  *rest[first_in - 1 + 2 * count:])

            @pl.when(step == 0)
            def _():
                for cp in copies:
                    cp.start()

        p = lax.dot_general(a_ref[...], b_ref[...], dims, preferred_element_type=F32)
        if nk == 1:
            o_ref[...] = p if init is None else rest[0][...] + p
        else:
            @pl.when(k == 0)
            def _():
                o_ref[...] = p if init is None else rest[0][...] + p

            @pl.when(k > 0)
            def _():
                o_ref[...] += p

        if count:
            @pl.when(step == grid[0] * grid[1] * nk - 1)
            def _():
                for cp in copies:
                    cp.wait_recv()
                for cp in copies:
                    cp.wait_send()

    out = pl.pallas_call(
        body, name=name,
        out_shape=(jax.ShapeDtypeStruct((m_dim, n_dim), out_dtype),)
        + tuple(jax.ShapeDtypeStruct((3,) + p.shape[1:], p.dtype) for p in parts),
        grid=grid,
        in_specs=[a_spec, b_spec] + ([] if init is None else [out_spec]) + [_hbm()] * count,
        out_specs=(out_spec,) + (_hbm(),) * count,
        scratch_shapes=[pltpu.SemaphoreType.DMA((3 * count,)), pltpu.SemaphoreType.DMA((3 * count,))] if count else [],
        compiler_params=_params(*(("arbitrary",) * 3 if count else ("parallel", "parallel", "arbitrary"))),
    )(a, b, *(() if init is None else (init,)), *parts)
    return out[0] if not count else out


def _rms_apply(x, w):
    r = lax.rsqrt(jnp.mean(x * x, axis=-1, keepdims=True) + EPS)
    return x * r * w


def _rms_bwd(x, w, dy):
    r = lax.rsqrt(jnp.mean(x * x, axis=-1, keepdims=True) + EPS)
    xh = x * r
    dyw = dy * w
    dx = r * (dyw - xh * jnp.mean(dyw * xh, axis=-1, keepdims=True))
    return dx, jnp.sum(dy * xh, axis=0, keepdims=True)


def _accumulate(ref, first, value):
    @pl.when(first)
    def _():
        ref[...] = value

    @pl.when(jnp.logical_not(first))
    def _():
        ref[...] += value


def _rows(tr, width):
    return pl.BlockSpec((tr, width), lambda i: (i, 0))


def _vec(width):
    return pl.BlockSpec((1, width), lambda i: (0, 0))


def _rms_fwd(h, w, name):
    n, d = h.shape
    tr = _row_tile(n)

    def body(h_ref, w_ref, u_ref):
        u_ref[...] = _rms_apply(h_ref[...], w_ref[...]).astype(u_ref.dtype)

    return pl.pallas_call(
        body, name=name, out_shape=jax.ShapeDtypeStruct((n, d), MXU_DTYPE), grid=(n // tr,),
        in_specs=[_rows(tr, d), _vec(d)], out_specs=_rows(tr, d), compiler_params=_params("parallel"),
    )(h, w)


def _mix_residual(h0, mix, w_post, w_pre):
    n, d = h0.shape
    tr = _row_tile(n)

    def body(h0_ref, mix_ref, wpost_ref, wpre_ref, h1_ref, u2_ref):
        h1 = h0_ref[...] + _rms_apply(mix_ref[...], wpost_ref[...])
        h1_ref[...] = h1
        u2_ref[...] = _rms_apply(h1, wpre_ref[...]).astype(u2_ref.dtype)

    return pl.pallas_call(
        body, name="mix_residual",
        out_shape=(jax.ShapeDtypeStruct((n, d), F32), jax.ShapeDtypeStruct((n, d), MXU_DTYPE)), grid=(n // tr,),
        in_specs=[_rows(tr, d), _rows(tr, d), _vec(d), _vec(d)], out_specs=(_rows(tr, d), _rows(tr, d)),
        compiler_params=_params("parallel"),
    )(h0, mix, w_post, w_pre)


NT_DIMS = (((1,), (1,)), ((), ()))


def _ffn_tiles(n):
    return _pick(n, (1056, 704, 512, 256, 128)), _pick(D_FF, (1408, 256, 128))


def _swiglu_fwd(u, w_gate_t, w_up_t, w_next):
    n, d = u.shape
    tm, tn = _ffn_tiles(n)
    grid = (D_FF // tn, n // tm)

    def body(u_ref, wg_ref, wu_ref, wn_ref, g_ref, up_ref, act_ref, wall_ref, send_sems, recv_sems):
        gather = _gather_copies([wn_ref], [wall_ref], send_sems, recv_sems)
        step = pl.program_id(0) * grid[1] + pl.program_id(1)

        @pl.when(step == 0)
        def _():
            for cp in gather[0]:
                cp.start()

        a = u_ref[...]
        g = lax.dot_general(a, wg_ref[...], NT_DIMS, preferred_element_type=F32)
        up = lax.dot_general(a, wu_ref[...], NT_DIMS, preferred_element_type=F32)
        g_ref[...] = g.astype(g_ref.dtype)
        up_ref[...] = up.astype(up_ref.dtype)
        act_ref[...] = (g * _sigmoid(g) * up).astype(act_ref.dtype)

        @pl.when(step == grid[0] * grid[1] - 1)
        def _():
            _gather_finish(gather)

    tile = pl.BlockSpec((tm, tn), lambda j, i: (i, j))
    weight = pl.BlockSpec((tn, d), lambda j, i: (j, 0))
    wide = jax.ShapeDtypeStruct((n, D_FF), MXU_DTYPE)
    return pl.pallas_call(
        body, name="swiglu_fwd",
        out_shape=(wide, wide, jax.ShapeDtypeStruct((n, D_FF), MXU_DTYPE),
                   jax.ShapeDtypeStruct((N_CHIPS,) + w_next.shape, w_next.dtype)),
        grid=grid,
        in_specs=[pl.BlockSpec((tm, d), lambda j, i: (i, 0)), weight, weight, _hbm()], out_specs=(tile, tile, tile, _hbm()),
        scratch_shapes=[pltpu.SemaphoreType.DMA((GATHER_SEMS,)), pltpu.SemaphoreType.DMA((GATHER_SEMS,))],
        compiler_params=_params("arbitrary", "arbitrary"),
    )(u, w_gate_t, w_up_t, w_next)


def _swiglu_bwd(dffn, w_down, gate, up):
    n, d = dffn.shape
    tm, tn = _ffn_tiles(n)

    def body(dy_ref, w_ref, g_ref, u_ref, dg_ref, du_ref):
        da = lax.dot_general(dy_ref[...], w_ref[...], NT_DIMS, preferred_element_type=F32)
        g = g_ref[...].astype(F32)
        s = _sigmoid(g)
        dg_ref[...] = (da * u_ref[...].astype(F32) * _dsilu(g, s)).astype(dg_ref.dtype)
        du_ref[...] = (da * g * s).astype(du_ref.dtype)

    tile = pl.BlockSpec((tm, tn), lambda j, i: (i, j))
    shape = jax.ShapeDtypeStruct((n, D_FF), MXU_DTYPE)
    return pl.pallas_call(
        body, name="swiglu_bwd", out_shape=(shape, shape), grid=(D_FF // tn, n // tm),
        in_specs=[pl.BlockSpec((tm, d), lambda j, i: (i, 0)), pl.BlockSpec((tn, d), lambda j, i: (j, 0)), tile, tile],
        out_specs=(tile, tile), compiler_params=_params("parallel", "parallel"),
    )(dffn, w_down, gate, up)


def _loss_head(h1, ffn, w_post, target, rows_per_seq, x_offset):
    n, d = h1.shape
    tr = _row_tile(rows_per_seq)
    tiles_per_seq = rows_per_seq // tr

    def body(h1_ref, ffn_ref, w_ref, t_ref, dh2_ref, dffn_ref, dw_ref, sq_ref):
        i = pl.program_id(0)
        w = w_ref[...]
        f = ffn_ref[...]
        r = lax.rsqrt(jnp.mean(f * f, axis=-1, keepdims=True) + EPS)
        fh = f * r
        row = lax.rem(i, tiles_per_seq) * tr + lax.broadcasted_iota(jnp.int32, (tr, 1), 0)
        err = jnp.where(row >= x_offset, h1_ref[...] + fh * w - t_ref[...], 0.0)
        dh2 = err * (1.0 / d)
        dh2_ref[...] = dh2
        dyw = dh2 * w
        dffn_ref[...] = (r * (dyw - fh * jnp.mean(dyw * fh, axis=-1, keepdims=True))).astype(dffn_ref.dtype)
        _accumulate(dw_ref, i == 0, jnp.sum(dh2 * fh, axis=0, keepdims=True))
        _accumulate(sq_ref, i == 0, jnp.sum(jnp.sum(err * err, axis=1, keepdims=True), axis=0, keepdims=True))

    return pl.pallas_call(
        body, name="loss_head",
        out_shape=(jax.ShapeDtypeStruct((n, d), F32), jax.ShapeDtypeStruct((n, d), MXU_DTYPE),
                   jax.ShapeDtypeStruct((1, d), F32), jax.ShapeDtypeStruct((1, 1), F32)),
        grid=(n // tr,),
        in_specs=[_rows(tr, d), _rows(tr, d), _vec(d), _rows(tr, d)],
        out_specs=(_rows(tr, d), _rows(tr, d), _vec(d), _vec(1)),
        compiler_params=_params("arbitrary"),
    )(h1, ffn, w_post, target)


def _mid_bwd(h1, mix, w_mix_post, w_ffn_pre, dh2, du2, grads):
    n, d = h1.shape
    tr = _row_tile(n)
    count = len(grads)

    def body(h1_ref, mix_ref, wpost_ref, wpre_ref, dh2_ref, du2_ref, *rest):
        g_refs, (dh1_ref, dmix_ref, dwpre_ref, dwpost_ref), got_refs = rest[:count], rest[count:count + 4], rest[count + 4:2 * count + 4]
        exchange = _sibling_copies(g_refs, got_refs, *rest[2 * count + 4:])
        i = pl.program_id(0)

        @pl.when(i == 0)
        def _():
            for cp in exchange:
                cp.start()

        dx, dwpre = _rms_bwd(h1_ref[...], wpre_ref[...], du2_ref[...])
        dh1 = dh2_ref[...] + dx
        dh1_ref[...] = dh1
        dmix, dwpost = _rms_bwd(mix_ref[...], wpost_ref[...], dh1)
        dmix_ref[...] = dmix.astype(dmix_ref.dtype)
        _accumulate(dwpre_ref, i == 0, dwpre)
        _accumulate(dwpost_ref, i == 0, dwpost)

        @pl.when(i == n // tr - 1)
        def _():
            for cp in exchange:
                cp.wait_recv()
            for cp in exchange:
                cp.wait_send()

    dh1, dmix, dwpre, dwpost, *got = pl.pallas_call(
        body, name="mid_bwd",
        out_shape=(jax.ShapeDtypeStruct((n, d), F32), jax.ShapeDtypeStruct((n, d), MXU_DTYPE),
                   jax.ShapeDtypeStruct((1, d), F32), jax.ShapeDtypeStruct((1, d), F32))
        + tuple(jax.ShapeDtypeStruct((g.shape[0],) + g.shape[2:], F32) for g in grads),
        grid=(n // tr,),
        in_specs=[_rows(tr, d), _rows(tr, d), _vec(d), _vec(d), _rows(tr, d), _rows(tr, d)] + [_hbm()] * count,
        out_specs=(_rows(tr, d), _rows(tr, d), _vec(d), _vec(d)) + (_hbm(),) * count,
        scratch_shapes=[pltpu.SemaphoreType.DMA((count,)), pltpu.SemaphoreType.DMA((count,))],
        compiler_params=_params("arbitrary"),
    )(h1, mix, w_mix_post, w_ffn_pre, dh2, du2, *grads)
    return dh1, dmix, dwpre, dwpost, got


def _in_bwd(h0, w_pre, dh1, du1):
    n, d = h0.shape
    tr = _row_tile(n)

    def body(h0_ref, w_ref, dh1_ref, du1_ref, dh0_ref, dw_ref):
        dx, dw = _rms_bwd(h0_ref[...], w_ref[...], du1_ref[...])
        dh0_ref[...] = dh1_ref[...] + dx
        _accumulate(dw_ref, pl.program_id(0) == 0, dw)

    return pl.pallas_call(
        body, name="in_bwd",
        out_shape=(jax.ShapeDtypeStruct((n, d), F32), jax.ShapeDtypeStruct((1, d), F32)), grid=(n // tr,),
        in_specs=[_rows(tr, d), _vec(d), _rows(tr, d), _rows(tr, d)], out_specs=(_rows(tr, d), _vec(d)),
        compiler_params=_params("arbitrary"),
    )(h0, w_pre, dh1, du1)


def _lane_is(lo, hi):
    lane = lax.broadcasted_iota(jnp.int32, (1, LANES), 1)
    return jnp.logical_and(lane >= lo, lane < hi)


def _gates_fwd(proj, a_log_l, dt_bias_l, rows_per_seq, pad_rows):
    n = proj.shape[0]
    tr = _row_tile(rows_per_seq)
    tiles_per_seq = rows_per_seq // tr

    def body(p_ref, a_ref, dt_ref, o_ref):
        x = p_ref[...]
        row = lax.rem(pl.program_id(0), tiles_per_seq) * tr + lax.broadcasted_iota(jnp.int32, (tr, 1), 0)
        g = -jnp.exp(a_ref[...]) * _softplus(x + dt_ref[...])
        val = jnp.where(_lane_is(0, HEADS), _sigmoid(x), jnp.where(_lane_is(HEADS, 2 * HEADS), g, 0.0))
        o_ref[...] = jnp.where(row >= pad_rows, val, 0.0)

    return pl.pallas_call(
        body, name="gates_fwd", out_shape=jax.ShapeDtypeStruct((n, LANES), F32), grid=(n // tr,),
        in_specs=[pl.BlockSpec((tr, LANES), lambda i: (i, BA_COL)), _vec(LANES), _vec(LANES)],
        out_specs=_rows(tr, LANES), compiler_params=_params("parallel"),
    )(proj, a_log_l, dt_bias_l)


def _gates_bwd(proj, dbg, a_log_l, dt_bias_l, rows_per_seq, pad_rows):
    n = proj.shape[0]
    tr = _row_tile(rows_per_seq)
    tiles_per_seq = rows_per_seq // tr

    def body(p_ref, d_ref, a_ref, dt_ref, dx_ref, da_ref, ddt_ref):
        i = pl.program_id(0)
        x = p_ref[...]
        d = d_ref[...]
        row = lax.rem(i, tiles_per_seq) * tr + lax.broadcasted_iota(jnp.int32, (tr, 1), 0)
        live = row >= pad_rows
        beta = _sigmoid(x)
        ea = jnp.exp(a_ref[...])
        xa = x + dt_ref[...]
        g = -ea * _softplus(xa)
        is_g = _lane_is(HEADS, 2 * HEADS)
        d_alogit = jnp.where(jnp.logical_and(live, is_g), d * (-ea) * _sigmoid(xa), 0.0)
        d_blogit = jnp.where(jnp.logical_and(live, _lane_is(0, HEADS)), d * beta * (1.0 - beta), 0.0)
        dx_ref[:, :LANES] = (d_alogit + d_blogit).astype(dx_ref.dtype)
        dx_ref[:, LANES:] = jnp.zeros((tr, LANES), dx_ref.dtype)
        _accumulate(da_ref, i == 0, jnp.sum(jnp.where(jnp.logical_and(live, is_g), d * g, 0.0), axis=0, keepdims=True))
        _accumulate(ddt_ref, i == 0, jnp.sum(d_alogit, axis=0, keepdims=True))

    return pl.pallas_call(
        body, name="gates_bwd",
        out_shape=(jax.ShapeDtypeStruct((n, 2 * LANES), MXU_DTYPE), jax.ShapeDtypeStruct((1, LANES), F32),
                   jax.ShapeDtypeStruct((1, LANES), F32)),
        grid=(n // tr,),
        in_specs=[pl.BlockSpec((tr, LANES), lambda i: (i, BA_COL)), _rows(tr, LANES), _vec(LANES), _vec(LANES)],
        out_specs=(_rows(tr, 2 * LANES), _vec(LANES), _vec(LANES)),
        compiler_params=_params("arbitrary"),
    )(proj, dbg, a_log_l, dt_bias_l)


HALO = 8


def _halo_scratch(rs):
    return pltpu.VMEM((rs + 2 * HALO, LANES), F32)


def _stage(ref, x):
    rs = x.shape[0]
    ref[0:HALO, :] = jnp.zeros((HALO, LANES), F32)
    ref[HALO + rs:, :] = jnp.zeros((HALO, LANES), F32)
    ref[HALO:HALO + rs, :] = x


def _shifted(ref, k, rs):
    return ref[pl.ds(HALO - k, rs), :]


def _causal_conv(x, x_staged, w, width):
    acc = w[width - 1:width, :] * x
    for i in range(width - 1):
        acc = acc + w[i:i + 1, :] * _shifted(x_staged, width - 1 - i, x.shape[0])
    return acc


def _anti_causal_conv(dy, dy_staged, w, width):
    acc = w[width - 1:width, :] * dy
    for i in range(width - 1):
        acc = acc + w[i:i + 1, :] * _shifted(dy_staged, -(width - 1 - i), dy.shape[0])
    return acc


def _conv_weight_grad(dy, x, x_staged, width):
    taps = [_shifted(x_staged, width - 1 - i, x.shape[0]) for i in range(width - 1)] + [x]
    return jnp.concatenate([jnp.sum(dy * tap, axis=0, keepdims=True) for tap in taps], axis=0)


def _seq_cols(rs, col0, heads):
    return pl.BlockSpec((rs, heads * LANES), lambda j, b: (b, col0 // heads + j))


def _tap_cols(width, col0, heads):
    return pl.BlockSpec((width, heads * LANES), lambda j, b: (0, col0 // heads + j))


def _lanes_of(h):
    return slice(h * LANES, (h + 1) * LANES)


def _qkv_fwd(proj, conv_w, kind, rs):
    n = proj.shape[0]
    col0 = {"q": 0, "k": HEADS, "v": 2 * HEADS}[kind]
    hb = HEADS

    def body(p_ref, w_ref, o_ref, staged):
        for h in range(hb):
            pre = p_ref[:, _lanes_of(h)]
            _stage(staged, pre)
            c = _causal_conv(pre, staged, w_ref[:, _lanes_of(h)], GDN_CONV)
            s = c * _sigmoid(c)
            if kind != "v":
                s = s * lax.rsqrt(jnp.sum(s * s, axis=-1, keepdims=True) + EPS)
            if kind == "q":
                s = s * (HEAD_DIM ** -0.5)
            o_ref[:, _lanes_of(h)] = s

    return pl.pallas_call(
        body, name="qkv_fwd_" + kind, out_shape=jax.ShapeDtypeStruct((n, GDN_WIDTH), F32), grid=(HEADS // hb, n // rs),
        in_specs=[_seq_cols(rs, col0, hb), _tap_cols(GDN_CONV, col0, hb)],
        out_specs=_seq_cols(rs, 0, hb), scratch_shapes=[_halo_scratch(rs)], compiler_params=_params("parallel", "parallel"),
    )(proj, conv_w)


def _qkv_bwd(dy, proj, conv_w, kind, rs):
    n = proj.shape[0]
    col0 = {"q": 0, "k": HEADS, "v": 2 * HEADS}[kind]
    hb = HEADS

    def body(dy_ref, p_ref, w_ref, dp_ref, dw_ref, pre_staged, dc_staged):
        for h in range(hb):
            lanes = _lanes_of(h)
            pre = p_ref[:, lanes]
            w = w_ref[:, lanes]
            _stage(pre_staged, pre)
            c = _causal_conv(pre, pre_staged, w, GDN_CONV)
            sg = _sigmoid(c)
            s = c * sg
            ds = dy_ref[:, lanes]
            if kind == "q":
                ds = ds * (HEAD_DIM ** -0.5)
            if kind != "v":
                r = lax.rsqrt(jnp.sum(s * s, axis=-1, keepdims=True) + EPS)
                sh = s * r
                ds = r * (ds - sh * jnp.sum(ds * sh, axis=-1, keepdims=True))
            dc = ds * _dsilu(c, sg)
            _stage(dc_staged, dc)
            dp_ref[:, lanes] = _anti_causal_conv(dc, dc_staged, w, GDN_CONV).astype(dp_ref.dtype)
            _accumulate(dw_ref.at[:, lanes], pl.program_id(1) == 0, _conv_weight_grad(dc, pre, pre_staged, GDN_CONV))

    return pl.pallas_call(
        body, name="qkv_bwd_" + kind,
        out_shape=(jax.ShapeDtypeStruct((n, GDN_WIDTH), MXU_DTYPE), jax.ShapeDtypeStruct((GDN_CONV, GDN_WIDTH), F32)),
        grid=(HEADS // hb, n // rs),
        in_specs=[_seq_cols(rs, 0, hb), _seq_cols(rs, col0, hb), _tap_cols(GDN_CONV, col0, hb)],
        out_specs=(_seq_cols(rs, 0, hb), _tap_cols(GDN_CONV, 0, hb)),
        scratch_shapes=[_halo_scratch(rs), _halo_scratch(rs)],
        compiler_params=_params("parallel", "arbitrary"),
    )(dy, proj, conv_w)


SC_COL = 4 * HEADS


def _sc_fwd(proj, conv_w, rs):
    n = proj.shape[0]

    hb = 2

    def body(x_ref, b_ref, c_ref, w_ref, y_ref, staged):
        for h in range(hb):
            lanes = _lanes_of(h)
            u = c_ref[:, lanes] * x_ref[:, lanes]
            _stage(staged, u)
            y_ref[:, lanes] = (b_ref[:, lanes] * _causal_conv(u, staged, w_ref[:, lanes], SC_CONV)).astype(y_ref.dtype)

    return pl.pallas_call(
        body, name="sc_fwd", out_shape=jax.ShapeDtypeStruct((n, SC_WIDTH), MXU_DTYPE), grid=(HEADS // hb, n // rs),
        in_specs=[_seq_cols(rs, SC_COL, hb), _seq_cols(rs, SC_COL + 4, hb), _seq_cols(rs, SC_COL + 8, hb),
                  _tap_cols(SC_CONV, 0, hb)],
        out_specs=_seq_cols(rs, 0, hb), scratch_shapes=[_halo_scratch(rs)], compiler_params=_params("parallel", "parallel"),
    )(proj, proj, proj, conv_w)


def _sc_bwd(dcat, proj, conv_w, rs):
    n = proj.shape[0]
    hb = 2

    def body(dy_ref, x_ref, b_ref, c_ref, w_ref, dx_ref, db_ref, dc_ref, dw_ref, u_staged, dcv_staged):
        for h in range(hb):
            lanes = _lanes_of(h)
            w = w_ref[:, lanes]
            x = x_ref[:, lanes]
            cc = c_ref[:, lanes]
            u = cc * x
            _stage(u_staged, u)
            dy = dy_ref[:, lanes]
            db_ref[:, lanes] = (dy * _causal_conv(u, u_staged, w, SC_CONV)).astype(db_ref.dtype)
            dcv = dy * b_ref[:, lanes]
            _stage(dcv_staged, dcv)
            du = _anti_causal_conv(dcv, dcv_staged, w, SC_CONV)
            dx_ref[:, lanes] = (du * cc).astype(dx_ref.dtype)
            dc_ref[:, lanes] = (du * x).astype(dc_ref.dtype)
            _accumulate(dw_ref.at[:, lanes], pl.program_id(1) == 0, _conv_weight_grad(dcv, u, u_staged, SC_CONV))

    piece = jax.ShapeDtypeStruct((n, SC_WIDTH), MXU_DTYPE)
    return pl.pallas_call(
        body, name="sc_bwd", out_shape=(piece, piece, piece, jax.ShapeDtypeStruct((SC_CONV, SC_WIDTH), F32)),
        grid=(HEADS // hb, n // rs),
        in_specs=[_seq_cols(rs, HEADS, hb), _seq_cols(rs, SC_COL, hb), _seq_cols(rs, SC_COL + 4, hb),
                  _seq_cols(rs, SC_COL + 8, hb), _tap_cols(SC_CONV, 0, hb)],
        out_specs=(_seq_cols(rs, 0, hb), _seq_cols(rs, 0, hb), _seq_cols(rs, 0, hb), _tap_cols(SC_CONV, 0, hb)),
        scratch_shapes=[_halo_scratch(rs), _halo_scratch(rs)],
        compiler_params=_params("parallel", "arbitrary"),
    )(dcat, proj, proj, proj, conv_w)


Z_COL = 3 * HEADS


def _gate_fwd(o, proj, gdn_norm, rs):
    n = proj.shape[0]

    hb = HEADS

    def body(o_ref, z_ref, w_ref, y_ref):
        for h in range(hb):
            lanes = _lanes_of(h)
            z = z_ref[:, lanes]
            y_ref[:, lanes] = (_rms_apply(o_ref[:, lanes], w_ref[...]) * z * _sigmoid(z)).astype(y_ref.dtype)

    return pl.pallas_call(
        body, name="gate_fwd", out_shape=jax.ShapeDtypeStruct((n, GDN_WIDTH), MXU_DTYPE), grid=(HEADS // hb, n // rs),
        in_specs=[_seq_cols(rs, 0, hb), _seq_cols(rs, Z_COL, hb), pl.BlockSpec((1, LANES), lambda j, b: (0, 0))],
        out_specs=_seq_cols(rs, 0, hb), compiler_params=_params("parallel", "parallel"),
    )(o, proj, gdn_norm)


def _gate_bwd(dcat, o, proj, gdn_norm, rs):
    n = proj.shape[0]
    hb = 2

    def body(dy_ref, o_ref, z_ref, w_ref, do_ref, dz_ref, dw_ref):
        w = w_ref[...]
        dw_step = jnp.zeros((1, LANES), F32)
        for h in range(hb):
            lanes = _lanes_of(h)
            z = z_ref[:, lanes]
            o = o_ref[:, lanes]
            dy = dy_ref[:, lanes]
            s = _sigmoid(z)
            dz_ref[:, lanes] = (dy * _rms_apply(o, w) * _dsilu(z, s)).astype(dz_ref.dtype)
            do, dw = _rms_bwd(o, w, dy * z * s)
            do_ref[:, lanes] = do
            dw_step = dw_step + dw
        _accumulate(dw_ref, jnp.logical_and(pl.program_id(0) == 0, pl.program_id(1) == 0), dw_step)

    return pl.pallas_call(
        body, name="gate_bwd",
        out_shape=(jax.ShapeDtypeStruct((n, GDN_WIDTH), F32), jax.ShapeDtypeStruct((n, GDN_WIDTH), MXU_DTYPE),
                   jax.ShapeDtypeStruct((1, LANES), F32)),
        grid=(HEADS // hb, n // rs),
        in_specs=[_seq_cols(rs, 0, hb), _seq_cols(rs, 0, hb), _seq_cols(rs, Z_COL, hb), pl.BlockSpec((1, LANES), lambda j, b: (0, 0))],
        out_specs=(_seq_cols(rs, 0, hb), _seq_cols(rs, 0, hb), pl.BlockSpec((1, LANES), lambda j, b: (0, 0))),
        compiler_params=_params("arbitrary", "arbitrary"),
    )(dcat, o, proj, gdn_norm)


def _dot(a, b):
    return jnp.dot(a.astype(MXU_DTYPE), b.astype(MXU_DTYPE), preferred_element_type=F32)


def _dot_nt(a, b):
    return lax.dot_general(a.astype(MXU_DTYPE), b.astype(MXU_DTYPE), (((1,), (1,)), ((), ())),
                           preferred_element_type=F32)


def _dot_tn(a, b):
    return lax.dot_general(a.astype(MXU_DTYPE), b.astype(MXU_DTYPE), (((0,), (0,)), ((), ())),
                           preferred_element_type=F32)


def _split(x):
    hi = x.astype(MXU_DTYPE)
    return hi, (x - hi.astype(F32)).astype(MXU_DTYPE)


def _dot_split(a, b):
    mm = functools.partial(jnp.dot, preferred_element_type=F32)
    return mm(a[0], b[0]) + (mm(a[0], b[1]) + mm(a[1], b[0]))


def _unit_lower_inverses(mats, eye):
    inv = [eye - a for a in mats]
    power = [_split(a) for a in mats]
    square = [_dot_split(p, p) for p in power]
    inv = [i + _dot_split(_split(i), _split(s)) for i, s in zip(inv, square)]
    span = 4
    while span < CHUNK:
        square = [_dot(s, s) for s in square]
        inv = [i + _dot(i, s) for i, s in zip(inv, square)]
        span *= 2
    return inv


def _chunk_masks():
    ii = lax.broadcasted_iota(jnp.int32, (CHUNK, CHUNK), 0)
    jj = lax.broadcasted_iota(jnp.int32, (CHUNK, CHUNK), 1)
    return ii, jj


def _chunk_decay(g_col, ii, jj):
    incl = ii >= jj
    g_row = jnp.sum(jnp.where(ii == jj, g_col, 0.0), axis=0, keepdims=True)
    gc_col = jnp.sum(jnp.where(incl, g_row, 0.0), axis=1, keepdims=True)
    gc_row = jnp.sum(jnp.where(ii <= jj, g_col, 0.0), axis=0, keepdims=True)
    g_total = jnp.sum(g_row, axis=1, keepdims=True)
    decay = jnp.where(incl, jnp.exp(jnp.where(incl, gc_col - gc_row, 0.0)), 0.0)
    return gc_col, g_total, decay


def _gdn_segments(rs, candidates):
    chunks = rs // CHUNK
    seg_chunks = _pick(chunks, candidates)
    return chunks, seg_chunks, chunks // seg_chunks


def _head_lanes(h):
    return slice(h * HEAD_DIM, (h + 1) * HEAD_DIM)


def _gdn_fwd(q, k, v, bg, rs, pieces):
    n = q.shape[0]
    batch = n // rs
    chunks, seg_chunks, segs = _gdn_segments(rs, (11, 8, 4, 2))
    seg_rows = seg_chunks * CHUNK
    chains = [(b, h) for b in range(batch) for h in range(HEADS)]
    each = lambda f, *lists: [f(*args) for args in zip(*lists)]
    count = len(pieces)

    def body(q_ref, k_ref, v_ref, bg_ref, *rest):
        w_refs, (o_ref, s_ref, t_ref), out_refs = rest[:count], rest[count:count + 3], rest[count + 3:2 * count + 3]
        state_ref, send_sems, recv_sems = rest[2 * count + 3:]
        gather = _gather_copies(w_refs, out_refs, send_sems, recv_sems)

        @pl.when(pl.program_id(0) == 0)
        def _():
            state_ref[...] = jnp.zeros_like(state_ref)
            for cp in gather[0]:
                cp.start()

        ii, jj = _chunk_masks()
        incl = ii >= jj
        eye = (ii == jj).astype(F32)

        def chunk(c, carry):
            rows = pl.ds(pl.multiple_of(c * CHUNK, CHUNK), CHUNK)
            bgc = [bg_ref[b, rows, :] for b in range(batch)]
            qc = [q_ref[b, rows, _head_lanes(h)] for b, h in chains]
            kc = [k_ref[b, rows, _head_lanes(h)] for b, h in chains]
            vc = [v_ref[b, rows, _head_lanes(h)] for b, h in chains]
            beta = [bgc[b][:, h:h + 1] for b, h in chains]
            state = [state_ref[b, h] for b, h in chains]
            dec = [_chunk_decay(bgc[b][:, HEADS + h:HEADS + h + 1], ii, jj) for b, h in chains]
            gc_col, g_total, decay = ([d[i] for d in dec] for i in range(3))
            kb = each(lambda x, y: x * y, kc, beta)
            a = each(lambda x, y, d: jnp.where(ii > jj, _dot_nt(x, y) * d, 0.0), kb, kc, decay)
            t_inv = _unit_lower_inverses(a, eye)
            eg = [jnp.exp(g) for g in gc_col]
            u = each(lambda t, x, y: _dot(t, x * y), t_inv, vc, beta)
            w = each(lambda t, x, e: _dot(t, x * e), t_inv, kb, eg)
            qk = each(lambda x, y, d: jnp.where(incl, _dot_nt(x, y) * d, 0.0), qc, kc, decay)
            v_new = each(lambda x, y, s: x - _dot(y, s), u, w, state)
            o = each(lambda x, e, s, m, vn: _dot(x * e, s) + _dot(m, vn), qc, eg, state, qk, v_new)
            new_state = each(lambda s, gt, x, g, vn: s * jnp.exp(gt) + _dot_tn(x * jnp.exp(gt - g), vn),
                             state, g_total, kc, gc_col, v_new)
            for i, (b, h) in enumerate(chains):
                s_ref[b, h, c] = state[i]
                t_ref[b, h, c] = t_inv[i]
                o_ref[b, rows, _head_lanes(h)] = o[i]
                state_ref[b, h] = new_state[i]
            return carry

        lax.fori_loop(0, seg_chunks, chunk, 0)

        @pl.when(pl.program_id(0) == segs - 1)
        def _():
            _gather_finish(gather)

    rows_spec = lambda width: pl.BlockSpec((batch, seg_rows, width), lambda s: (0, s, 0))
    per_chunk = lambda r, c: pl.BlockSpec((batch, HEADS, seg_chunks, r, c), lambda s: (0, 0, s, 0, 0))
    as_seqs = lambda a: a.reshape(batch, rs, a.shape[-1])
    sems = GATHER_SEMS * count
    o, states, t_invs, *gathered = pl.pallas_call(
        body, name="gdn_fwd",
        out_shape=(jax.ShapeDtypeStruct((batch, rs, GDN_WIDTH), F32),
                   jax.ShapeDtypeStruct((batch, HEADS, chunks, HEAD_DIM, HEAD_DIM), F32),
                   jax.ShapeDtypeStruct((batch, HEADS, chunks, CHUNK, CHUNK), F32))
        + tuple(jax.ShapeDtypeStruct((N_CHIPS,) + p.shape, p.dtype) for p in pieces),
        grid=(segs,),
        in_specs=[rows_spec(GDN_WIDTH), rows_spec(GDN_WIDTH), rows_spec(GDN_WIDTH), rows_spec(LANES)] + [_hbm()] * count,
        out_specs=(rows_spec(GDN_WIDTH), per_chunk(HEAD_DIM, HEAD_DIM), per_chunk(CHUNK, CHUNK)) + (_hbm(),) * count,
        scratch_shapes=[pltpu.VMEM((batch, HEADS, HEAD_DIM, HEAD_DIM), F32), pltpu.SemaphoreType.DMA((sems,)),
                        pltpu.SemaphoreType.DMA((sems,))],
        compiler_params=_params("arbitrary"),
    )(as_seqs(q), as_seqs(k), as_seqs(v), as_seqs(bg), *pieces)
    return o.reshape(n, GDN_WIDTH), states, t_invs, gathered


def _gdn_bwd(do, q, k, v, bg, states, t_invs, rs, parts):
    n = q.shape[0]
    batch = n // rs
    chunks, seg_chunks, segs = _gdn_segments(rs, (3, 4, 2))
    seg_rows = seg_chunks * CHUNK
    chains = [(b, h) for b in range(batch) for h in range(HEADS)]
    each = lambda f, *lists: [f(*args) for args in zip(*lists)]
    count = len(parts)

    def body(do_ref, q_ref, k_ref, v_ref, bg_ref, s_ref, t_ref, *rest):
        p_refs, (dq_ref, dk_ref, dv_ref, dbg_ref), got_refs = rest[:count], rest[count:count + 4], rest[count + 4:2 * count + 4]
        dstate_ref, send_sems, recv_sems = rest[2 * count + 4:]
        exchange = _chip_copies(p_refs, got_refs, send_sems, recv_sems)

        @pl.when(pl.program_id(0) == 0)
        def _():
            dstate_ref[...] = jnp.zeros_like(dstate_ref)
            for cp in exchange:
                cp.start()

        ii, jj = _chunk_masks()
        incl = ii >= jj
        strict = ii > jj
        lane = lax.broadcasted_iota(jnp.int32, (1, LANES), 1)

        def rowsum(x):
            return jnp.sum(x, axis=1, keepdims=True)

        def total(x):
            return jnp.sum(rowsum(x), axis=0, keepdims=True)

        def chunk(step, carry):
            c = seg_chunks - 1 - step
            rows = pl.ds(pl.multiple_of(c * CHUNK, CHUNK), CHUNK)
            bgc = [bg_ref[b, rows, :] for b in range(batch)]
            qc = [q_ref[b, rows, _head_lanes(h)] for b, h in chains]
            kc = [k_ref[b, rows, _head_lanes(h)] for b, h in chains]
            vc = [v_ref[b, rows, _head_lanes(h)] for b, h in chains]
            doc = [do_ref[b, rows, _head_lanes(h)] for b, h in chains]
            beta = [bgc[b][:, h:h + 1] for b, h in chains]
            state = [s_ref[b, h, c] for b, h in chains]
            t_inv = [t_ref[b, h, c] for b, h in chains]
            d_state = [dstate_ref[b, h] for b, h in chains]
            dec = [_chunk_decay(bgc[b][:, HEADS + h:HEADS + h + 1], ii, jj) for b, h in chains]
            gc_col, g_total, decay = ([d[i] for d in dec] for i in range(3))
            kb = each(lambda x, y: x * y, kc, beta)
            vb = each(lambda x, y: x * y, vc, beta)
            eg = [jnp.exp(g) for g in gc_col]
            kbg = each(lambda x, y: x * y, kb, eg)
            a = each(lambda x, y, d: jnp.where(strict, _dot_nt(x, y) * d, 0.0), kb, kc, decay)
            qk = each(lambda x, y, d: jnp.where(incl, _dot_nt(x, y) * d, 0.0), qc, kc, decay)
            w = each(_dot, t_inv, kbg)
            u = each(_dot, t_inv, vb)
            q_dec = each(lambda x, y: x * y, qc, eg)
            ek = each(lambda gt, g: jnp.exp(gt - g), g_total, gc_col)
            k_dec = each(lambda x, y: x * y, kc, ek)
            g_last = [jnp.exp(gt) for gt in g_total]
            v_new = each(lambda x, y, s: x - _dot(y, s), u, w, state)
            dv_new = each(lambda m, d, x, ds: _dot_tn(m, d) + _dot(x, ds), qk, doc, k_dec, d_state)
            dqk = each(lambda d, vn: jnp.where(incl, _dot_nt(d, vn), 0.0), doc, v_new)
            dq_dec = each(_dot_nt, doc, state)
            dk_dec = each(_dot_nt, v_new, d_state)
            dg_last = each(lambda s, ds: total(s * ds), state, d_state)
            new_d_state = each(lambda x, d, gl, ds, y, dvn: _dot_tn(x, d) + gl * ds - _dot_tn(y, dvn),
                               q_dec, doc, g_last, d_state, w, dv_new)
            dw = each(lambda dvn, s: -_dot_nt(dvn, s), dv_new, state)
            dt = each(lambda dvn, x, y, z: _dot_nt(dvn, x) + _dot_nt(y, z), dv_new, vb, dw, kbg)
            dvb = each(_dot_tn, t_inv, dv_new)
            dkbg = each(_dot_tn, t_inv, dw)
            t_dt = each(_dot_tn, t_inv, dt)
            da = each(lambda x, t: -jnp.where(strict, _dot_nt(x, t), 0.0), t_dt, t_inv)
            dm_a = each(lambda x, y: x * y, da, decay)
            dm_qk = each(lambda x, y: x * y, dqk, decay)
            e = each(lambda x, y, z, t: x * y + z * t, da, a, dqk, qk)
            dkb = each(lambda m, x, y, z: _dot(m, x) + y * z, dm_a, kc, dkbg, eg)
            dk = each(lambda m, x, m2, y, z, t, p, bt: _dot_tn(m, x) + _dot_tn(m2, y) + z * t + p * bt,
                      dm_a, kb, dm_qk, qc, dk_dec, ek, dkb, beta)
            dq = each(lambda m, x, y, z: _dot(m, x) + y * z, dm_qk, kc, dq_dec, eg)
            dbeta = each(lambda x, y, z, t: rowsum(x * y + z * t), dkb, kc, dvb, vc)
            dgc = each(lambda x, p, pd, r, rd, s, sd: rowsum(x) - rowsum(jnp.where(ii == jj, jnp.sum(x, axis=0, keepdims=True), 0.0))
                       + rowsum(p * pd - r * rd + s * sd), e, dq_dec, q_dec, dk_dec, k_dec, dkbg, kbg)
            d_total = each(lambda r, rd, x, gl: total(r * rd) + x * gl, dk_dec, k_dec, dg_last, g_last)
            dg = each(lambda x, t: rowsum(jnp.where(jj >= ii, jnp.sum(jnp.where(ii == jj, x, 0.0), axis=0, keepdims=True), 0.0)) + t,
                      dgc, d_total)
            dbg = [jnp.zeros((CHUNK, LANES), F32) for _ in range(batch)]
            for i, (b, h) in enumerate(chains):
                dstate_ref[b, h] = new_d_state[i]
                dk_ref[b, rows, _head_lanes(h)] = dk[i]
                dq_ref[b, rows, _head_lanes(h)] = dq[i]
                dv_ref[b, rows, _head_lanes(h)] = dvb[i] * beta[i]
                dbg[b] = dbg[b] + jnp.where(lane == h, dbeta[i], 0.0) + jnp.where(lane == HEADS + h, dg[i], 0.0)
            for b in range(batch):
                dbg_ref[b, rows, :] = dbg[b]
            return carry

        lax.fori_loop(0, seg_chunks, chunk, 0)

        @pl.when(pl.program_id(0) == segs - 1)
        def _():
            for cp in exchange:
                cp.wait_recv()
            for cp in exchange:
                cp.wait_send()

    rows_spec = lambda width: pl.BlockSpec((batch, seg_rows, width), lambda s: (0, segs - 1 - s, 0))
    per_chunk = lambda r, c: pl.BlockSpec((batch, HEADS, seg_chunks, r, c), lambda s: (0, 0, segs - 1 - s, 0, 0))
    as_seqs = lambda a: a.reshape(batch, rs, a.shape[-1])
    grad = jax.ShapeDtypeStruct((batch, rs, GDN_WIDTH), F32)
    wide = rows_spec(GDN_WIDTH)
    dq, dk, dv, dbg, *got = pl.pallas_call(
        body, name="gdn_bwd",
        out_shape=(grad, grad, grad, jax.ShapeDtypeStruct((batch, rs, LANES), F32))
        + tuple(jax.ShapeDtypeStruct((3,) + p.shape[1:], p.dtype) for p in parts),
        grid=(segs,),
        in_specs=[wide, wide, wide, wide, rows_spec(LANES), per_chunk(HEAD_DIM, HEAD_DIM), per_chunk(CHUNK, CHUNK)]
        + [_hbm()] * count,
        out_specs=(wide, wide, wide, rows_spec(LANES)) + (_hbm(),) * count,
        scratch_shapes=[pltpu.VMEM((batch, HEADS, HEAD_DIM, HEAD_DIM), F32), pltpu.SemaphoreType.DMA((3 * count,)),
                        pltpu.SemaphoreType.DMA((3 * count,))],
        compiler_params=_params("arbitrary"),
    )(as_seqs(do), as_seqs(q), as_seqs(k), as_seqs(v), as_seqs(bg), states, t_invs, *parts)
    return dq.reshape(n, GDN_WIDTH), dk.reshape(n, GDN_WIDTH), dv.reshape(n, GDN_WIDTH), dbg.reshape(n, LANES), got


def _lane_vec(vals, offset):
    k = vals.shape[1]
    return jnp.pad(vals, ((0, 0), (offset, LANES - offset - k)))


LATER = ("w_out", "w_gate", "w_up", "w_down")


def _halves(a):
    return a.reshape(a.shape[:-2] + (2, a.shape[-2] // 2, a.shape[-1]))


def _local_step(x, target, meta, norms, w_in_t, conv_qkv, a_log, dt_bias, gdn_norm, conv_sc, later_shards, core_arg):
    batch, seq, d = x.shape
    tokens = N_META + seq
    pad_rows = (-tokens) % CHUNK
    rs = tokens + pad_rows
    x_offset = pad_rows + N_META
    n = batch * rs
    w_mix_pre, w_mix_post, w_ffn_pre, w_ffn_post = norms

    head = jnp.concatenate([jnp.zeros((pad_rows, d), F32), meta], axis=0)
    h0 = jnp.concatenate([jnp.broadcast_to(head[None], (batch, x_offset, d)), x], axis=1).reshape(n, d)
    target_p = jnp.pad(target, ((0, 0), (x_offset, 0), (0, 0))).reshape(n, d)
    a_log_l = _lane_vec(a_log, HEADS)
    dt_bias_l = _lane_vec(dt_bias, HEADS)

    u1 = _rms_fwd(h0, w_mix_pre, "rms_mix_pre")
    proj = _mm(u1, w_in_t, "nt", F32, "mm_proj")
    q = _qkv_fwd(proj, conv_qkv, "q", rs)
    k = _qkv_fwd(proj, conv_qkv, "k", rs)
    v = _qkv_fwd(proj, conv_qkv, "v", rs)
    bg = _gates_fwd(proj, a_log_l, dt_bias_l, rs, pad_rows)
    o, states, t_invs, gathered = _gdn_fwd(q, k, v, bg, rs, later_shards[:3])
    w_out, w_gate_t, w_up_t = (a.reshape(-1, d) for a in gathered)
    o_gated = _gate_fwd(o, proj, gdn_norm, rs)
    y_sc = _sc_fwd(proj, conv_sc, rs)
    cat = jnp.concatenate([o_gated, y_sc], axis=1)
    mix = _mm(cat, w_out, "nn", F32, "mm_mix")
    h1, u2 = _mix_residual(h0, mix, w_mix_post, w_ffn_pre)
    gate, up, act, w_down = _swiglu_fwd(u2, w_gate_t, w_up_t, later_shards[3])
    w_down = w_down.reshape(-1, d)
    ffn = _mm(act, w_down, "nn", F32, "mm_down")

    dh2, dffn, d_ffn_post, sq = _loss_head(h1, ffn, w_ffn_post, target_p, rs, x_offset)
    d_w_down = _mm(act, dffn, "tn", F32, "mm_dw_down")
    dgate, dup = _swiglu_bwd(dffn, w_down, gate, up)
    d_w_gate_t = _mm(dgate, u2, "tn", F32, "mm_dw_gate")
    d_w_up_t = _mm(dup, u2, "tn", F32, "mm_dw_up")
    du2 = _mm(dup, w_up_t, "nn", F32, "mm_du2_up", init=_mm(dgate, w_gate_t, "nn", F32, "mm_du2_gate"))
    by_chip = [_halves(g.reshape(N_CHIPS, -1, d)) for g in (d_w_gate_t, d_w_up_t, d_w_down)]
    dh1, dmix, d_ffn_pre, d_mix_post, got_sibling = _mid_bwd(h1, mix, w_mix_post, w_ffn_pre, dh2, du2, by_chip)
    dcat = _mm(dmix, w_out, "nt", F32, "mm_dcat")
    d_w_out = _halves(_mm(cat, dmix, "tn", F32, "mm_dw_out").reshape(N_CHIPS, -1, d))
    by_chip, got_sibling = [d_w_out] + by_chip, list(_exchange_siblings([d_w_out])) + got_sibling
    sums = [_add_sibling(a, b, core_arg, name) for name, a, b in zip(LATER, by_chip, got_sibling)]
    do, dz, d_gdn_norm = _gate_bwd(dcat, o, proj, gdn_norm, rs)
    dscx, dscb, dscc, d_conv_sc = _sc_bwd(dcat, proj, conv_sc, rs)
    dq, dk, dv, dbg, got_chips = _gdn_bwd(do, q, k, v, bg, states, t_invs, rs, [send for _, send in sums[:3]])
    dpq, dwq = _qkv_bwd(dq, proj, conv_qkv, "q", rs)
    dpk, dwk = _qkv_bwd(dk, proj, conv_qkv, "k", rs)
    dpv, dwv = _qkv_bwd(dv, proj, conv_qkv, "v", rs)
    d_conv_qkv = jnp.concatenate([dwq, dwk, dwv], axis=1)
    dba, d_a_log_l, d_dt_bias_l = _gates_bwd(proj, dbg, a_log_l, dt_bias_l, rs, pad_rows)
    dproj = jnp.concatenate([dpq, dpk, dpv, dz, dscx, dscb, dscc, dba], axis=1)
    d_w_in_t, got_down = _mm(dproj, u1, "tn", F32, "mm_dw_in", exchange=[sums[3][1]])
    got_chips.append(got_down)
    g_in = _halves(_in_from_kernel_order(d_w_in_t))
    sums.insert(0, _add_sibling(g_in, _exchange_siblings([g_in])[0], core_arg, "w_in"))
    du1, got_in = _mm(dproj, w_in_t, "nn", F32, "mm_du1", exchange=[sums[0][1]])
    got_chips.insert(0, got_in)
    dh0, d_mix_pre = _in_bwd(h0, w_mix_pre, dh1, du1)

    dh0 = dh0.reshape(batch, rs, d)
    grads = dict(
        meta_tokens=jnp.sum(dh0[:, pad_rows:x_offset], axis=0),
        mix_pre_norm=d_mix_pre, mix_post_norm=d_mix_post, ffn_pre_norm=d_ffn_pre, ffn_post_norm=d_ffn_post,
        conv_qkv=d_conv_qkv,
        a_log=d_a_log_l[:, HEADS:2 * HEADS], dt_bias=d_dt_bias_l[:, HEADS:2 * HEADS],
        gdn_norm=d_gdn_norm, conv_sc=d_conv_sc,
    )
    return sq, dh0[:, x_offset:], grads, [(part, got) for (part, _), got in zip(sums, got_chips)]


MATRICES = ("w_in", "w_out", "w_gate", "w_up", "w_down")
IN_SHARD = IN_WIDTH // N_CHIPS
IN_SHARD_PAD = 928


IN_SEGMENTS = ((0, 0, 4 * GDN_WIDTH), (4 * GDN_WIDTH, IN_WIDTH - 2 * HEADS, 2 * HEADS),
               (4 * GDN_WIDTH + 2 * HEADS, 4 * GDN_WIDTH, 3 * SC_WIDTH))
SUBLANES = 8
PACKED_ROWS = 16


def _in_to_kernel_order(by_chip):
    d = by_chip.shape[-1]
    tl = _pick(d, (256, 128))
    runs = []
    for ref0, ker0, count in IN_SEGMENTS:
        row = ref0
        while row < ref0 + count:
            chip, at = divmod(row, IN_SHARD)
            take = min(ref0 + count - row, IN_SHARD - at)
            runs.append((ker0 + row - ref0, take, chip * IN_SHARD_PAD + at))
            row += take

    def body(w_ref, o_ref):
        o_ref[...] = jnp.zeros_like(o_ref)
        for out0, rows, src0 in runs:
            a0 = out0 // PACKED_ROWS * PACKED_ROWS
            a1 = -(-(out0 + rows) // PACKED_ROWS) * PACKED_ROWS
            window = w_ref[pl.ds(src0 - (out0 - a0), a1 - a0), :]
            row = a0 + lax.broadcasted_iota(jnp.int32, (a1 - a0, 1), 0)
            keep = jnp.logical_and(row >= out0, row < out0 + rows)
            o_ref[a0:a1, :] = jnp.where(keep, window, o_ref[a0:a1, :])

    return pl.pallas_call(
        body, name="in_to_kernel_order", out_shape=jax.ShapeDtypeStruct((IN_PAD, d), by_chip.dtype), grid=(d // tl,),
        in_specs=[pl.BlockSpec((N_CHIPS * IN_SHARD_PAD, tl), lambda j: (0, j))],
        out_specs=pl.BlockSpec((IN_PAD, tl), lambda j: (0, j)),
        compiler_params=_params("parallel"),
    )(by_chip.reshape(N_CHIPS * IN_SHARD_PAD, d))


def _in_from_kernel_order(g_t):
    d = g_t.shape[-1]
    tl = _pick(d, (256, 128))

    def body(g_ref, o_ref):
        row = lax.broadcasted_iota(jnp.int32, (IN_SHARD_PAD, 1), 0)
        for chip in range(N_CHIPS):
            first = chip * IN_SHARD
            runs = []
            for ref0, ker0, count in IN_SEGMENTS:
                lo, hi = max(ref0, first), min(ref0 + count, first + IN_SHARD)
                if lo < hi:
                    runs.append((lo - first, hi - lo, ker0 + lo - ref0))
            val = jnp.zeros((IN_SHARD_PAD, tl), F32)
            patches = []
            for out0, rows, src0 in runs:
                start = src0 - out0
                if 0 <= start <= IN_PAD - IN_SHARD_PAD:
                    window = g_ref[pl.ds(start, IN_SHARD_PAD), :]
                    val = jnp.where(jnp.logical_and(row >= out0, row < out0 + rows), window, val)
                else:
                    patches.append((out0, rows, src0))
            o_ref[chip] = val
            for out0, rows, src0 in patches:
                a0 = out0 // SUBLANES * SUBLANES
                a1 = -(-(out0 + rows) // SUBLANES) * SUBLANES
                window = g_ref[pl.ds(src0 - (out0 - a0), a1 - a0), :]
                keep = jnp.logical_and(row[a0:a1] >= out0, row[a0:a1] < out0 + rows)
                o_ref[chip, a0:a1, :] = jnp.where(keep, window, o_ref[chip, a0:a1, :])

    return pl.pallas_call(
        body, name="in_from_kernel_order", out_shape=jax.ShapeDtypeStruct((N_CHIPS, IN_SHARD_PAD, d), F32), grid=(d // tl,),
        in_specs=[pl.BlockSpec((IN_PAD, tl), lambda j: (0, j))],
        out_specs=pl.BlockSpec((N_CHIPS, IN_SHARD_PAD, tl), lambda j: (0, 0, j)),
        compiler_params=_params("parallel"),
    )(g_t)


PACK_LANES = 3 * GDN_WIDTH
PACKED = dict(mix_pre_norm=(0, 1, 0, D_MODEL), mix_post_norm=(1, 1, 0, D_MODEL), ffn_pre_norm=(2, 1, 0, D_MODEL),
              ffn_post_norm=(3, 1, 0, D_MODEL), a_log=(4, 1, 0, HEADS), dt_bias=(5, 1, 0, HEADS), loss=(6, 1, 0, 1),
              gdn_norm=(7, 1, 0, HEAD_DIM), conv_qkv=(8, GDN_CONV, 0, 3 * GDN_WIDTH), conv_sc=(0, SC_CONV, D_MODEL, SC_WIDTH),
              meta_tokens=(16, N_META, 0, D_MODEL))
PACK_ROWS = 32
SHARDED_SMALL = ("conv_qkv", "conv_sc", "meta_tokens")


def _pack_small(values):
    names = list(PACKED)

    def body(*refs):
        out_ref = refs[-1]
        out_ref[...] = jnp.zeros_like(out_ref)
        for name, ref in zip(names, refs):
            row, rows, lane0, lanes = PACKED[name]
            out_ref[row:row + rows, lane0:lane0 + lanes] = ref[...]

    return pl.pallas_call(body, name="pack_small", out_shape=jax.ShapeDtypeStruct((PACK_ROWS, PACK_LANES), F32))(
        *[values[name] for name in names])


def _sum_devices(packed_all, chip):
    names = list(PACKED)

    def body(chip_ref, all_ref, *rest):
        shard_refs, out_refs = rest[:len(SHARDED_SMALL)], rest[len(SHARDED_SMALL):]

        def total(ref, rows, lanes):
            acc = ref[0, rows, lanes]
            for k in range(1, 8):
                acc = acc + ref[k, rows, lanes]
            return acc

        for name, out in zip(names, out_refs):
            row, rows, lane0, lanes = PACKED[name]
            if name in SHARDED_SMALL:
                out[...] = total(shard_refs[SHARDED_SMALL.index(name)], slice(0, rows), slice(None))
            else:
                out[...] = total(all_ref, slice(row, row + rows), slice(lane0, lane0 + lanes))

    def shard_spec(name):
        row, rows, lane0, lanes = PACKED[name]
        height, width = max(rows, 8), lanes // N_CHIPS
        assert row % height == 0 and lane0 % width == 0
        return pl.BlockSpec((8, height, width), lambda i, chip_ref: (0, row // height, lane0 // width + chip_ref[0]))

    def out_shape(name):
        _, rows, _, lanes = PACKED[name]
        return jax.ShapeDtypeStruct((rows, lanes // N_CHIPS if name in SHARDED_SMALL else lanes), F32)

    whole = lambda shape: pl.BlockSpec(shape, lambda i, chip_ref: (0,) * len(shape))
    outs = pl.pallas_call(
        body, name="sum_devices", out_shape=tuple(out_shape(n) for n in names),
        grid_spec=pltpu.PrefetchScalarGridSpec(
            num_scalar_prefetch=1, grid=(1,),
            in_specs=[whole(packed_all.shape)] + [shard_spec(n) for n in SHARDED_SMALL],
            out_specs=tuple(whole(out_shape(n).shape) for n in names)),
    )(chip, packed_all, *[packed_all] * len(SHARDED_SMALL))
    return dict(zip(names, outs))


def _hbm():
    return pl.BlockSpec(memory_space=pl.ANY)


def _place():
    x, y, c = lax.axis_index("x"), lax.axis_index("y"), lax.axis_index("c")
    chips = ((1 - x, y), (x, 1 - y), (1 - x, 1 - y))
    return x, y, c, chips


def _remote(src, dst, send_sems, recv_sems, k, to):
    return pltpu.make_async_remote_copy(src_ref=src, dst_ref=dst, send_sem=send_sems.at[k], recv_sem=recv_sems.at[k],
                                        device_id=to, device_id_type=MESH)


GATHER_SEMS = 7


def _gather_copies(w_refs, out_refs, send_sems, recv_sems):
    x, y, c, chips = _place()
    mine = 2 * x + y
    sibling = (x, y, 1 - c)
    copy = functools.partial(_remote, send_sems=send_sems, recv_sems=recv_sems)
    direct, landed, passing, from_sibling = [], [], [], []
    for i, (w, o) in enumerate(zip(w_refs, out_refs)):
        k = GATHER_SEMS * i
        direct.append(copy(w, o.at[mine], k=k, to=sibling))
        from_sibling.append(copy(w, o.at[mine], k=k, to=sibling))
        for j, (cx, cy) in enumerate(chips):
            theirs = 2 * cx + cy
            direct.append(copy(w.at[c], o.at[mine, c], k=k + 1 + j, to=(cx, cy, c)))
            landed.append(copy(w.at[c], o.at[theirs, c], k=k + 1 + j, to=sibling))
            passing.append(copy(o.at[theirs, c], o.at[theirs, c], k=k + 4 + j, to=sibling))
            from_sibling.append(copy(w.at[c], o.at[theirs, 1 - c], k=k + 4 + j, to=sibling))
    return direct, landed, passing, from_sibling


def _gather_finish(copies):
    direct, landed, passing, from_sibling = copies
    for arrival, forward in zip(landed, passing):
        arrival.wait_recv()
        forward.start()
    for arrival in from_sibling:
        arrival.wait_recv()
    for cp in direct + passing:
        cp.wait_send()


def _gather_weights(pieces, smalls):
    count, extra = len(pieces), len(smalls)
    total = count + extra

    def body(*refs):
        w_refs, s_refs = refs[:count], refs[count:total]
        out_refs, sall_refs = refs[total:total + count], refs[total + count:2 * total]
        send_sems, recv_sems, local_sems = refs[2 * total:]
        x, y, c, chips = _place()
        mine = 2 * x + y
        own = [pltpu.make_async_copy(s, sall.at[mine], local_sems.at[i]) for i, (s, sall) in enumerate(zip(s_refs, sall_refs))]
        small = [_remote(s, sall.at[mine], send_sems, recv_sems, GATHER_SEMS * count + 3 * i + j, (cx, cy, c))
                 for i, (s, sall) in enumerate(zip(s_refs, sall_refs)) for j, (cx, cy) in enumerate(chips)]
        copies = _gather_copies(w_refs, out_refs, send_sems, recv_sems)
        for cp in own + small + copies[0]:
            cp.start()
        _gather_finish(copies)
        for cp in small:
            cp.wait_recv()
        for cp in small:
            cp.wait_send()
        for cp in own:
            cp.wait()

    sems = GATHER_SEMS * count + 3 * extra
    return pl.pallas_call(
        body, name="gather_weights",
        out_shape=tuple(jax.ShapeDtypeStruct((N_CHIPS,) + p.shape, p.dtype) for p in list(pieces) + list(smalls)),
        in_specs=[_hbm()] * total, out_specs=(_hbm(),) * total,
        scratch_shapes=[pltpu.SemaphoreType.DMA((sems,)), pltpu.SemaphoreType.DMA((sems,)), pltpu.SemaphoreType.DMA((extra,))],
    )(*pieces, *smalls)


def _sibling_copies(g_refs, got_refs, send_sems, recv_sems):
    x, y, c, _ = _place()
    return [_remote(g.at[:, 1 - c], got, send_sems, recv_sems, i, (x, y, 1 - c)) for i, (g, got) in enumerate(zip(g_refs, got_refs))]


def _exchange_siblings(grads):
    count = len(grads)

    def body(*refs):
        copies = _sibling_copies(refs[:count], refs[count:2 * count], *refs[2 * count:])
        for cp in copies:
            cp.start()
        for cp in copies:
            cp.wait_recv()
        for cp in copies:
            cp.wait_send()

    return pl.pallas_call(
        body, name="exchange_siblings",
        out_shape=tuple(jax.ShapeDtypeStruct((g.shape[0],) + g.shape[2:], F32) for g in grads),
        in_specs=[_hbm()] * count, out_specs=(_hbm(),) * count,
        scratch_shapes=[pltpu.SemaphoreType.DMA((count,)), pltpu.SemaphoreType.DMA((count,))],
    )(*grads)


def _chip_copies(p_refs, got_refs, send_sems, recv_sems):
    x, y, c, chips = _place()
    return [_remote(p.at[2 * cx + cy], got.at[j], send_sems, recv_sems, 3 * i + j, (cx, cy, c))
            for i, (p, got) in enumerate(zip(p_refs, got_refs)) for j, (cx, cy) in enumerate(chips)]


def _share_halves(halves, small):
    count = len(halves)

    def body(*refs):
        h_refs, s_ref = refs[:count], refs[count]
        full_refs, sall_ref = refs[count + 1:2 * count + 1], refs[2 * count + 1]
        send_sems, recv_sems, local_sem = refs[2 * count + 2:]
        x, y, c, _ = _place()
        me = 4 * x + 2 * y + c
        own = pltpu.make_async_copy(s_ref, sall_ref.at[me], local_sem)
        own.start()
        copies = [_remote(h.at[c], full.at[c], send_sems, recv_sems, i, (x, y, 1 - c))
                  for i, (h, full) in enumerate(zip(h_refs, full_refs))]
        for k in range(7):
            dx, dy, dc = ((k + 1) >> 2) & 1, ((k + 1) >> 1) & 1, (k + 1) & 1
            peer = (1 - x if dx else x, 1 - y if dy else y, 1 - c if dc else c)
            copies.append(_remote(s_ref, sall_ref.at[me], send_sems, recv_sems, count + k, peer))
        for cp in copies:
            cp.start()
        for cp in copies:
            cp.wait_recv()
        for cp in copies:
            cp.wait_send()
        own.wait()

    return pl.pallas_call(
        body, name="share_halves",
        out_shape=tuple(jax.ShapeDtypeStruct(h.shape, h.dtype) for h in halves) + (jax.ShapeDtypeStruct((8,) + small.shape, F32),),
        in_specs=[_hbm()] * (count + 1), out_specs=(_hbm(),) * (count + 1), input_output_aliases={i: i for i in range(count)},
        scratch_shapes=[pltpu.SemaphoreType.DMA((count + 7,)), pltpu.SemaphoreType.DMA((count + 7,)), pltpu.SemaphoreType.DMA],
    )(*halves, small)


def _add_sibling(grad, got, core, name):
    chips, _, rows, cols = grad.shape

    def body(core_ref, g_ref, r_ref, sum_ref, send_ref):
        s = g_ref[...] + r_ref[...]
        sum_ref[...] = s
        send_ref[...] = s.astype(send_ref.dtype)

    block = pl.BlockSpec((None, rows, cols), lambda p, core_ref: (p, 0, 0))
    return pl.pallas_call(
        body, name="add_sibling_" + name,
        out_shape=(jax.ShapeDtypeStruct((chips, rows, cols), F32), jax.ShapeDtypeStruct((chips, rows, cols), BF16)),
        grid_spec=pltpu.PrefetchScalarGridSpec(
            num_scalar_prefetch=1, grid=(chips,),
            in_specs=[pl.BlockSpec((None, None, rows, cols), lambda p, core_ref: (p, core_ref[0], 0, 0)), block],
            out_specs=(block, block)),
        compiler_params=_params("parallel"),
    )(core, grad, got)


def _add_chips(part, got, chip_core, name):
    _, rows, cols = part.shape
    tr = rows // 2 if rows % 32 == 0 else rows

    def body(place_ref, p_ref, r_ref, o_ref):
        o_ref[...] = ((p_ref[...] + r_ref[0].astype(F32)) + r_ref[1].astype(F32)) + r_ref[2].astype(F32)

    return pl.pallas_call(
        body, name="add_chips_" + name, out_shape=jax.ShapeDtypeStruct((2, rows, cols), F32),
        grid_spec=pltpu.PrefetchScalarGridSpec(
            num_scalar_prefetch=1, grid=(rows // tr,),
            in_specs=[pl.BlockSpec((None, tr, cols), lambda i, place_ref: (place_ref[0], i, 0)),
                      pl.BlockSpec((3, tr, cols), lambda i, place_ref: (0, i, 0))],
            out_specs=pl.BlockSpec((None, tr, cols), lambda i, place_ref: (place_ref[1], i, 0))),
        compiler_params=_params("parallel"),
    )(chip_core, part, got)


def _adamw(w, g, m, v, name):
    rows, cols = w.shape
    tr = _pick(rows, (3592, 256, 352, 176, 128, 64, 32, 16, 8))

    def body(w_ref, g_ref, m_ref, v_ref, d_ref, nm_ref, nv_ref):
        g = g_ref[...]
        m = ADAM_B1 * m_ref[...] + (1.0 - ADAM_B1) * g
        v = ADAM_B2 * v_ref[...] + (1.0 - ADAM_B2) * (g * g)
        m_hat = m / (1.0 - ADAM_B1 ** ADAM_STEP)
        v_hat = v / (1.0 - ADAM_B2 ** ADAM_STEP)
        d_ref[...] = -ADAM_LR * (m_hat / (jnp.sqrt(v_hat) + ADAM_EPS) + ADAM_WD * w_ref[...])
        nm_ref[...] = m
        nv_ref[...] = v

    block = pl.BlockSpec((tr, cols), lambda i: (i, 0))
    shape = jax.ShapeDtypeStruct((rows, cols), F32)
    return pl.pallas_call(
        body, name="adamw_" + name, out_shape=(shape, shape, shape), grid=(rows // tr,),
        in_specs=[block] * 4, out_specs=(block,) * 3, compiler_params=_params("parallel"),
    )(w, g, m, v)


WEIGHTS = ("meta_tokens", "mix_pre_norm", "mix_post_norm", "ffn_pre_norm", "ffn_post_norm", "w_in", "conv_qkv", "a_log",
           "dt_bias", "gdn_norm", "conv_sc", "w_out", "w_gate", "w_up", "w_down")


def kernel(x, meta_tokens, mix_pre_norm, mix_post_norm, ffn_pre_norm, ffn_post_norm, w_in, conv_qkv, a_log, dt_bias, gdn_norm, conv_sc, w_out, w_gate, w_up, w_down, loss_target, m_meta_tokens, m_mix_pre_norm, m_mix_post_norm, m_ffn_pre_norm, m_ffn_post_norm, m_w_in, m_conv_qkv, m_a_log, m_dt_bias, m_gdn_norm, m_conv_sc, m_w_out, m_w_gate, m_w_up, m_w_down, v_meta_tokens, v_mix_pre_norm, v_mix_post_norm, v_ffn_pre_norm, v_ffn_post_norm, v_w_in, v_conv_qkv, v_a_log, v_dt_bias, v_gdn_norm, v_conv_sc, v_w_out, v_w_gate, v_w_up, v_w_down):
    d = x.shape[-1]
    two_d = lambda a: a.reshape(a.shape[-2:])
    weights = dict(zip(WEIGHTS, (meta_tokens, mix_pre_norm, mix_post_norm, ffn_pre_norm, ffn_post_norm, w_in, conv_qkv, a_log,
                                 dt_bias, gdn_norm, conv_sc, w_out, w_gate, w_up, w_down)))
    m_in = dict(zip(WEIGHTS, (m_meta_tokens, m_mix_pre_norm, m_mix_post_norm, m_ffn_pre_norm, m_ffn_post_norm, m_w_in, m_conv_qkv,
                              m_a_log, m_dt_bias, m_gdn_norm, m_conv_sc, m_w_out, m_w_gate, m_w_up, m_w_down)))
    v_in = dict(zip(WEIGHTS, (v_meta_tokens, v_mix_pre_norm, v_mix_post_norm, v_ffn_pre_norm, v_ffn_post_norm, v_w_in, v_conv_qkv,
                              v_a_log, v_dt_bias, v_gdn_norm, v_conv_sc, v_w_out, v_w_gate, v_w_up, v_w_down)))
    core = lax.axis_index("c")
    chip = 2 * lax.axis_index("x") + lax.axis_index("y")
    core_arg = core.reshape(1).astype(jnp.int32)
    chip_core = jnp.stack([chip, core]).astype(jnp.int32)
    whole = lambda a: a.reshape(a.shape[:-3] + (2 * a.shape[-2], d))
    by_rows = lambda n, a: two_d(a).T if n in ("w_in", "w_gate", "w_up") else two_d(a)

    shard = {n: by_rows(n, weights[n]).astype(MXU_DTYPE) for n in MATRICES}
    shard["w_in"] = jnp.pad(shard["w_in"], ((0, IN_SHARD_PAD - IN_SHARD), (0, 0)))
    w_in_all, *small_all = _gather_weights([_halves(shard["w_in"])], [two_d(weights[n]) for n in SHARDED_SMALL])
    w_in_t = _in_to_kernel_order(whole(w_in_all))
    conv_qkv_full, conv_sc_full, meta_full = (jnp.concatenate([a[p] for p in range(N_CHIPS)], axis=1) for a in small_all)

    sq, grad_x, g, sums = _local_step(
        x, loss_target, meta_full, (mix_pre_norm, mix_post_norm, ffn_pre_norm, ffn_post_norm), w_in_t, conv_qkv_full, a_log,
        dt_bias, gdn_norm, conv_sc_full, [_halves(shard[n]) for n in LATER], core_arg)

    totals = [_add_chips(part, got, chip_core, n) for n, (part, got) in zip(MATRICES, sums)]
    *shared, packed_all = _share_halves(totals, _pack_small(dict(g, loss=sq)))
    grads = {n: whole(a) for n, a in zip(MATRICES, shared)}
    grads["w_in"] = grads["w_in"][:IN_SHARD]
    grads.update(_sum_devices(packed_all, chip.reshape(1).astype(jnp.int32)))
    loss = (0.5 / d) * grads.pop("loss")[0, 0]

    outs = [[], [], [], []]
    for n in WEIGHTS:
        shape = weights[n].shape
        delta, new_m, new_v = _adamw(by_rows(n, weights[n]), grads[n], by_rows(n, m_in[n]), by_rows(n, v_in[n]), n)
        for out, a in zip(outs, (grads[n], delta, new_m, new_v)):
            out.append((a.T if n in ("w_in", "w_gate", "w_up") else a).reshape(shape))
    return (loss, grad_x, *outs[0], *outs[1], *outs[2], *outs[3])
```

```python
import functools

import jax
import jax.numpy as jnp
from jax import lax
from jax.experimental import pallas as pl
from jax.experimental.pallas import tpu as pltpu

F32 = jnp.float32
BF16 = jnp.bfloat16
MXU_DTYPE = jnp.bfloat16
MESH = pl.DeviceIdType.MESH

D_MODEL = 1024
N_META = 16
HEADS = 4
HEAD_DIM = 128
GDN_WIDTH = HEADS * HEAD_DIM
GDN_CONV = 4
CHUNK = 64
SC_WIDTH = D_MODEL - GDN_WIDTH
SC_CONV = 3
D_FF = 2816
IN_WIDTH = 4 * GDN_WIDTH + 2 * HEADS + 3 * SC_WIDTH
IN_PAD = 3840
BA_COL = (4 * GDN_WIDTH + 3 * SC_WIDTH) // 128
EPS = 1e-6
LANES = 128
N_CHIPS = 4
VMEM_LIMIT = 48 * 2 ** 20

ADAM_LR = 0.001
ADAM_B1 = 0.9
ADAM_B2 = 0.999
ADAM_EPS = 1e-08
ADAM_WD = 0.01
ADAM_STEP = 10


def _pick(n, candidates):
    for c in candidates:
        if n % c == 0:
            return c
    return n


def _row_tile(n):
    return _pick(n, (352, 256, 176, 128, 64, 32, 16, 8))


def _params(*sem):
    return pltpu.CompilerParams(dimension_semantics=sem, vmem_limit_bytes=VMEM_LIMIT)


def _sigmoid(x):
    return 0.5 * jnp.tanh(0.5 * x) + 0.5


def _softplus(x):
    return jnp.maximum(x, 0.0) + jnp.log(1.0 + jnp.exp(-jnp.abs(x)))


def _dsilu(x, s):
    return s * (1.0 + x * (1.0 - s))


def _mm(a, b, mode, out_dtype, name, init=None, exchange=None):
    if mode == "tn":
        k_dim, m_dim = a.shape
    else:
        m_dim, k_dim = a.shape
    n_dim = b.shape[0] if mode == "nt" else b.shape[1]
    rows = (1056, 1024, 704, 512, 256, 128) if init is not None else (2112, 1056, 1024, 704, 512, 256, 128)
    tm = _pick(m_dim, (1408, 1280, 1024, 512, 256, 128) if mode == "tn" else rows)
    tn = _pick(n_dim, (1408, 1280, 1024, 768, 512, 256, 128))
    tk = _pick(k_dim, (1408, 1280, 1056, 1024, 512, 256, 128))
    nk = k_dim // tk
    if mode == "nn":
        a_spec = pl.BlockSpec((tm, tk), lambda i, j, k: (i, k))
        b_spec = pl.BlockSpec((tk, tn), lambda i, j, k: (k, j))
        dims = (((1,), (0,)), ((), ()))
    elif mode == "nt":
        a_spec = pl.BlockSpec((tm, tk), lambda i, j, k: (i, k))
        b_spec = pl.BlockSpec((tn, tk), lambda i, j, k: (j, k))
        dims = (((1,), (1,)), ((), ()))
    else:
        a_spec = pl.BlockSpec((tk, tm), lambda i, j, k: (k, i))
        b_spec = pl.BlockSpec((tk, tn), lambda i, j, k: (k, j))
        dims = (((0,), (0,)), ((), ()))

    out_spec = pl.BlockSpec((tm, tn), lambda i, j, k: (i, j))
    grid = (m_dim // tm, n_dim // tn, nk)
    parts = () if exchange is None else tuple(exchange)
    count = len(parts)
    first_in = 2 if init is None else 3

    assert out_dtype == F32

    def body(a_ref, b_ref, *rest):
        o_ref = rest[first_in - 2 + count]
        k = pl.program_id(2)
        step = (pl.program_id(0) * grid[1] + pl.program_id(1)) * nk + k
        if count:
            copies = _chip_copies(rest[first_in - 2:first_in - 2 + count], rest[first_in - 1 + count:first_in - 1 + 2 * count],
                                  *rest[first_in - 1 + 2 * count:])

            @pl.when(step == 0)
            def _():
                for cp in copies:
                    cp.start()

        p = lax.dot_general(a_ref[...], b_ref[...], dims, preferred_element_type=F32)
        if nk == 1:
            o_ref[...] = p if init is None else rest[0][...] + p
        else:
            @pl.when(k == 0)
            def _():
                o_ref[...] = p if init is None else rest[0][...] + p

            @pl.when(k > 0)
            def _():
                o_ref[...] += p

        if count:
            @pl.when(step == grid[0] * grid[1] * nk - 1)
            def _():
                for cp in copies:
                    cp.wait_recv()
                for cp in copies:
                    cp.wait_send()

    out = pl.pallas_call(
        body, name=name,
        out_shape=(jax.ShapeDtypeStruct((m_dim, n_dim), out_dtype),)
        + tuple(jax.ShapeDtypeStruct((3,) + p.shape[1:], p.dtype) for p in parts),
        grid=grid,
        in_specs=[a_spec, b_spec] + ([] if init is None else [out_spec]) + [_hbm()] * count,
        out_specs=(out_spec,) + (_hbm(),) * count,
        scratch_shapes=[pltpu.SemaphoreType.DMA((3 * count,)), pltpu.SemaphoreType.DMA((3 * count,))] if count else [],
        compiler_params=_params(*(("arbitrary",) * 3 if count else ("parallel", "parallel", "arbitrary"))),
    )(a, b, *(() if init is None else (init,)), *parts)
    return out[0] if not count else out


def _rms_apply(x, w):
    r = lax.rsqrt(jnp.mean(x * x, axis=-1, keepdims=True) + EPS)
    return x * r * w


def _rms_bwd(x, w, dy):
    r = lax.rsqrt(jnp.mean(x * x, axis=-1, keepdims=True) + EPS)
    xh = x * r
    dyw = dy * w
    dx = r * (dyw - xh * jnp.mean(dyw * xh, axis=-1, keepdims=True))
    return dx, jnp.sum(dy * xh, axis=0, keepdims=True)


def _accumulate(ref, first, value):
    @pl.when(first)
    def _():
        ref[...] = value

    @pl.when(jnp.logical_not(first))
    def _():
        ref[...] += value


def _rows(tr, width):
    return pl.BlockSpec((tr, width), lambda i: (i, 0))


def _vec(width):
    return pl.BlockSpec((1, width), lambda i: (0, 0))


def _rms_fwd(h, w, name):
    n, d = h.shape
    tr = _row_tile(n)

    def body(h_ref, w_ref, u_ref):
        u_ref[...] = _rms_apply(h_ref[...], w_ref[...]).astype(u_ref.dtype)

    return pl.pallas_call(
        body, name=name, out_shape=jax.ShapeDtypeStruct((n, d), MXU_DTYPE), grid=(n // tr,),
        in_specs=[_rows(tr, d), _vec(d)], out_specs=_rows(tr, d), compiler_params=_params("parallel"),
    )(h, w)


def _mix_residual(h0, mix, w_post, w_pre):
    n, d = h0.shape
    tr = _row_tile(n)

    def body(h0_ref, mix_ref, wpost_ref, wpre_ref, h1_ref, u2_ref):
        h1 = h0_ref[...] + _rms_apply(mix_ref[...], wpost_ref[...])
        h1_ref[...] = h1
        u2_ref[...] = _rms_apply(h1, wpre_ref[...]).astype(u2_ref.dtype)

    return pl.pallas_call(
        body, name="mix_residual",
        out_shape=(jax.ShapeDtypeStruct((n, d), F32), jax.ShapeDtypeStruct((n, d), MXU_DTYPE)), grid=(n // tr,),
        in_specs=[_rows(tr, d), _rows(tr, d), _vec(d), _vec(d)], out_specs=(_rows(tr, d), _rows(tr, d)),
        compiler_params=_params("parallel"),
    )(h0, mix, w_post, w_pre)


NT_DIMS = (((1,), (1,)), ((), ()))


def _ffn_tiles(n):
    return _pick(n, (1056, 704, 512, 256, 128)), _pick(D_FF, (1408, 256, 128))


def _swiglu_fwd(u, w_gate_t, w_up_t, w_next):
    n, d = u.shape
    tm, tn = _ffn_tiles(n)
    grid = (D_FF // tn, n // tm)

    def body(u_ref, wg_ref, wu_ref, wn_ref, g_ref, up_ref, act_ref, wall_ref, send_sems, recv_sems):
        gather = _gather_copies([wn_ref], [wall_ref], send_sems, recv_sems)
        step = pl.program_id(0) * grid[1] + pl.program_id(1)

        @pl.when(step == 0)
        def _():
            for cp in gather[0]:
                cp.start()

        a = u_ref[...]
        g = lax.dot_general(a, wg_ref[...], NT_DIMS, preferred_element_type=F32)
        up = lax.dot_general(a, wu_ref[...], NT_DIMS, preferred_element_type=F32)
        g_ref[...] = g.astype(g_ref.dtype)
        up_ref[...] = up.astype(up_ref.dtype)
        act_ref[...] = (g * _sigmoid(g) * up).astype(act_ref.dtype)

        @pl.when(step == grid[0] * grid[1] - 1)
        def _():
            _gather_finish(gather)

    tile = pl.BlockSpec((tm, tn), lambda j, i: (i, j))
    weight = pl.BlockSpec((tn, d), lambda j, i: (j, 0))
    wide = jax.ShapeDtypeStruct((n, D_FF), MXU_DTYPE)
    return pl.pallas_call(
        body, name="swiglu_fwd",
        out_shape=(wide, wide, jax.ShapeDtypeStruct((n, D_FF), MXU_DTYPE),
                   jax.ShapeDtypeStruct((N_CHIPS,) + w_next.shape, w_next.dtype)),
        grid=grid,
        in_specs=[pl.BlockSpec((tm, d), lambda j, i: (i, 0)), weight, weight, _hbm()], out_specs=(tile, tile, tile, _hbm()),
        scratch_shapes=[pltpu.SemaphoreType.DMA((GATHER_SEMS,)), pltpu.SemaphoreType.DMA((GATHER_SEMS,))],
        compiler_params=_params("arbitrary", "arbitrary"),
    )(u, w_gate_t, w_up_t, w_next)


def _swiglu_bwd(dffn, w_down, gate, up):
    n, d = dffn.shape
    tm, tn = _ffn_tiles(n)

    def body(dy_ref, w_ref, g_ref, u_ref, dg_ref, du_ref):
        da = lax.dot_general(dy_ref[...], w_ref[...], NT_DIMS, preferred_element_type=F32)
        g = g_ref[...].astype(F32)
        s = _sigmoid(g)
        dg_ref[...] = (da * u_ref[...].astype(F32) * _dsilu(g, s)).astype(dg_ref.dtype)
        du_ref[...] = (da * g * s).astype(du_ref.dtype)

    tile = pl.BlockSpec((tm, tn), lambda j, i: (i, j))
    shape = jax.ShapeDtypeStruct((n, D_FF), MXU_DTYPE)
    return pl.pallas_call(
        body, name="swiglu_bwd", out_shape=(shape, shape), grid=(D_FF // tn, n // tm),
        in_specs=[pl.BlockSpec((tm, d), lambda j, i: (i, 0)), pl.BlockSpec((tn, d), lambda j, i: (j, 0)), tile, tile],
        out_specs=(tile, tile), compiler_params=_params("parallel", "parallel"),
    )(dffn, w_down, gate, up)


def _loss_head(h1, ffn, w_post, target, rows_per_seq, x_offset):
    n, d = h1.shape
    tr = _row_tile(rows_per_seq)
    tiles_per_seq = rows_per_seq // tr

    def body(h1_ref, ffn_ref, w_ref, t_ref, dh2_ref, dffn_ref, dw_ref, sq_ref):
        i = pl.program_id(0)
        w = w_ref[...]
        f = ffn_ref[...]
        r = lax.rsqrt(jnp.mean(f * f, axis=-1, keepdims=True) + EPS)
        fh = f * r
        row = lax.rem(i, tiles_per_seq) * tr + lax.broadcasted_iota(jnp.int32, (tr, 1), 0)
        err = jnp.where(row >= x_offset, h1_ref[...] + fh * w - t_ref[...], 0.0)
        dh2 = err * (1.0 / d)
        dh2_ref[...] = dh2
        dyw = dh2 * w
        dffn_ref[...] = (r * (dyw - fh * jnp.mean(dyw * fh, axis=-1, keepdims=True))).astype(dffn_ref.dtype)
        _accumulate(dw_ref, i == 0, jnp.sum(dh2 * fh, axis=0, keepdims=True))
        _accumulate(sq_ref, i == 0, jnp.sum(jnp.sum(err * err, axis=1, keepdims=True), axis=0, keepdims=True))

    return pl.pallas_call(
        body, name="loss_head",
        out_shape=(jax.ShapeDtypeStruct((n, d), F32), jax.ShapeDtypeStruct((n, d), MXU_DTYPE),
                   jax.ShapeDtypeStruct((1, d), F32), jax.ShapeDtypeStruct((1, 1), F32)),
        grid=(n // tr,),
        in_specs=[_rows(tr, d), _rows(tr, d), _vec(d), _rows(tr, d)],
        out_specs=(_rows(tr, d), _rows(tr, d), _vec(d), _vec(1)),
        compiler_params=_params("arbitrary"),
    )(h1, ffn, w_post, target)


def _mid_bwd(h1, mix, w_mix_post, w_ffn_pre, dh2, du2, grads):
    n, d = h1.shape
    tr = _row_tile(n)
    count = len(grads)

    def body(h1_ref, mix_ref, wpost_ref, wpre_ref, dh2_ref, du2_ref, *rest):
        g_refs, (dh1_ref, dmix_ref, dwpre_ref, dwpost_ref), got_refs = rest[:count], rest[count:count + 4], rest[count + 4:2 * count + 4]
        exchange = _sibling_copies(g_refs, got_refs, *rest[2 * count + 4:])
        i = pl.program_id(0)

        @pl.when(i == 0)
        def _():
            for cp in exchange:
                cp.start()

        dx, dwpre = _rms_bwd(h1_ref[...], wpre_ref[...], du2_ref[...])
        dh1 = dh2_ref[...] + dx
        dh1_ref[...] = dh1
        dmix, dwpost = _rms_bwd(mix_ref[...], wpost_ref[...], dh1)
        dmix_ref[...] = dmix.astype(dmix_ref.dtype)
        _accumulate(dwpre_ref, i == 0, dwpre)
        _accumulate(dwpost_ref, i == 0, dwpost)

        @pl.when(i == n // tr - 1)
        def _():
            for cp in exchange:
                cp.wait_recv()
            for cp in exchange:
                cp.wait_send()

    dh1, dmix, dwpre, dwpost, *got = pl.pallas_call(
        body, name="mid_bwd",
        out_shape=(jax.ShapeDtypeStruct((n, d), F32), jax.ShapeDtypeStruct((n, d), MXU_DTYPE),
                   jax.ShapeDtypeStruct((1, d), F32), jax.ShapeDtypeStruct((1, d), F32))
        + tuple(jax.ShapeDtypeStruct((g.shape[0],) + g.shape[2:], F32) for g in grads),
        grid=(n // tr,),
        in_specs=[_rows(tr, d), _rows(tr, d), _vec(d), _vec(d), _rows(tr, d), _rows(tr, d)] + [_hbm()] * count,
        out_specs=(_rows(tr, d), _rows(tr, d), _vec(d), _vec(d)) + (_hbm(),) * count,
        scratch_shapes=[pltpu.SemaphoreType.DMA((count,)), pltpu.SemaphoreType.DMA((count,))],
        compiler_params=_params("arbitrary"),
    )(h1, mix, w_mix_post, w_ffn_pre, dh2, du2, *grads)
    return dh1, dmix, dwpre, dwpost, got


def _in_bwd(h0, w_pre, dh1, du1):
    n, d = h0.shape
    tr = _row_tile(n)

    def body(h0_ref, w_ref, dh1_ref, du1_ref, dh0_ref, dw_ref):
        dx, dw = _rms_bwd(h0_ref[...], w_ref[...], du1_ref[...])
        dh0_ref[...] = dh1_ref[...] + dx
        _accumulate(dw_ref, pl.program_id(0) == 0, dw)

    return pl.pallas_call(
        body, name="in_bwd",
        out_shape=(jax.ShapeDtypeStruct((n, d), F32), jax.ShapeDtypeStruct((1, d), F32)), grid=(n // tr,),
        in_specs=[_rows(tr, d), _vec(d), _rows(tr, d), _rows(tr, d)], out_specs=(_rows(tr, d), _vec(d)),
        compiler_params=_params("arbitrary"),
    )(h0, w_pre, dh1, du1)


def _lane_is(lo, hi):
    lane = lax.broadcasted_iota(jnp.int32, (1, LANES), 1)
    return jnp.logical_and(lane >= lo, lane < hi)


def _gates_fwd(proj, a_log_l, dt_bias_l, rows_per_seq, pad_rows):
    n = proj.shape[0]
    tr = _row_tile(rows_per_seq)
    tiles_per_seq = rows_per_seq // tr

    def body(p_ref, a_ref, dt_ref, o_ref):
        x = p_ref[...]
        row = lax.rem(pl.program_id(0), tiles_per_seq) * tr + lax.broadcasted_iota(jnp.int32, (tr, 1), 0)
        g = -jnp.exp(a_ref[...]) * _softplus(x + dt_ref[...])
        val = jnp.where(_lane_is(0, HEADS), _sigmoid(x), jnp.where(_lane_is(HEADS, 2 * HEADS), g, 0.0))
        o_ref[...] = jnp.where(row >= pad_rows, val, 0.0)

    return pl.pallas_call(
        body, name="gates_fwd", out_shape=jax.ShapeDtypeStruct((n, LANES), F32), grid=(n // tr,),
        in_specs=[pl.BlockSpec((tr, LANES), lambda i: (i, BA_COL)), _vec(LANES), _vec(LANES)],
        out_specs=_rows(tr, LANES), compiler_params=_params("parallel"),
    )(proj, a_log_l, dt_bias_l)


def _gates_bwd(proj, dbg, a_log_l, dt_bias_l, rows_per_seq, pad_rows, dproj):
    n = proj.shape[0]
    tr = _row_tile(rows_per_seq)
    tiles_per_seq = rows_per_seq // tr

    def body(p_ref, d_ref, a_ref, dt_ref, _, dx_ref, da_ref, ddt_ref):
        i = pl.program_id(0)
        x = p_ref[...]
        d = d_ref[...]
        row = lax.rem(i, tiles_per_seq) * tr + lax.broadcasted_iota(jnp.int32, (tr, 1), 0)
        live = row >= pad_rows
        beta = _sigmoid(x)
        ea = jnp.exp(a_ref[...])
        xa = x + dt_ref[...]
        g = -ea * _softplus(xa)
        is_g = _lane_is(HEADS, 2 * HEADS)
        d_alogit = jnp.where(jnp.logical_and(live, is_g), d * (-ea) * _sigmoid(xa), 0.0)
        d_blogit = jnp.where(jnp.logical_and(live, _lane_is(0, HEADS)), d * beta * (1.0 - beta), 0.0)
        dx_ref[:, :LANES] = (d_alogit + d_blogit).astype(dx_ref.dtype)
        dx_ref[:, LANES:] = jnp.zeros((tr, LANES), dx_ref.dtype)
        _accumulate(da_ref, i == 0, jnp.sum(jnp.where(jnp.logical_and(live, is_g), d * g, 0.0), axis=0, keepdims=True))
        _accumulate(ddt_ref, i == 0, jnp.sum(d_alogit, axis=0, keepdims=True))

    return pl.pallas_call(
        body, name="gates_bwd",
        out_shape=(jax.ShapeDtypeStruct(dproj.shape, dproj.dtype), jax.ShapeDtypeStruct((1, LANES), F32),
                   jax.ShapeDtypeStruct((1, LANES), F32)),
        grid=(n // tr,),
        in_specs=[pl.BlockSpec((tr, LANES), lambda i: (i, BA_COL)), _rows(tr, LANES), _vec(LANES), _vec(LANES), _hbm()],
        out_specs=(pl.BlockSpec((tr, 2 * LANES), lambda i: (i, BA_COL // 2)), _vec(LANES), _vec(LANES)),
        input_output_aliases={4: 0},
        compiler_params=_params("arbitrary"),
    )(proj, dbg, a_log_l, dt_bias_l, dproj)


HALO = 8


def _halo_scratch(rs):
    return pltpu.VMEM((rs + 2 * HALO, LANES), F32)


def _stage(ref, x):
    rs = x.shape[0]
    ref[0:HALO, :] = jnp.zeros((HALO, LANES), F32)
    ref[HALO + rs:, :] = jnp.zeros((HALO, LANES), F32)
    ref[HALO:HALO + rs, :] = x


def _shifted(ref, k, rs):
    return ref[pl.ds(HALO - k, rs), :]


def _causal_conv(x, x_staged, w, width):
    acc = w[width - 1:width, :] * x
    for i in range(width - 1):
        acc = acc + w[i:i + 1, :] * _shifted(x_staged, width - 1 - i, x.shape[0])
    return acc


def _anti_causal_conv(dy, dy_staged, w, width):
    acc = w[width - 1:width, :] * dy
    for i in range(width - 1):
        acc = acc + w[i:i + 1, :] * _shifted(dy_staged, -(width - 1 - i), dy.shape[0])
    return acc


def _conv_weight_grad(dy, x, x_staged, width):
    taps = [_shifted(x_staged, width - 1 - i, x.shape[0]) for i in range(width - 1)] + [x]
    return jnp.concatenate([jnp.sum(dy * tap, axis=0, keepdims=True) for tap in taps], axis=0)


def _seq_cols(rs, col0, heads):
    return pl.BlockSpec((rs, heads * LANES), lambda j, b: (b, col0 // heads + j))


def _tap_cols(width, col0, heads):
    return pl.BlockSpec((width, heads * LANES), lambda j, b: (0, col0 // heads + j))


def _lanes_of(h):
    return slice(h * LANES, (h + 1) * LANES)


def _qkv_fwd(proj, conv_w, kind, rs):
    n = proj.shape[0]
    col0 = {"q": 0, "k": HEADS, "v": 2 * HEADS}[kind]
    hb = HEADS

    def body(p_ref, w_ref, o_ref, staged):
        for h in range(hb):
            pre = p_ref[:, _lanes_of(h)]
            _stage(staged, pre)
            c = _causal_conv(pre, staged, w_ref[:, _lanes_of(h)], GDN_CONV)
            s = c * _sigmoid(c)
            if kind != "v":
                s = s * lax.rsqrt(jnp.sum(s * s, axis=-1, keepdims=True) + EPS)
            if kind == "q":
                s = s * (HEAD_DIM ** -0.5)
            o_ref[:, _lanes_of(h)] = s

    return pl.pallas_call(
        body, name="qkv_fwd_" + kind, out_shape=jax.ShapeDtypeStruct((n, GDN_WIDTH), F32), grid=(HEADS // hb, n // rs),
        in_specs=[_seq_cols(rs, col0, hb), _tap_cols(GDN_CONV, col0, hb)],
        out_specs=_seq_cols(rs, 0, hb), scratch_shapes=[_halo_scratch(rs)], compiler_params=_params("parallel", "parallel"),
    )(proj, conv_w)


def _qkv_bwd(dy, proj, conv_w, kind, rs, dproj):
    n = proj.shape[0]
    col0 = {"q": 0, "k": HEADS, "v": 2 * HEADS}[kind]
    hb = HEADS

    def body(dy_ref, p_ref, w_ref, _, dp_ref, dw_ref, pre_staged, dc_staged):
        for h in range(hb):
            lanes = _lanes_of(h)
            pre = p_ref[:, lanes]
            w = w_ref[:, lanes]
            _stage(pre_staged, pre)
            c = _causal_conv(pre, pre_staged, w, GDN_CONV)
            sg = _sigmoid(c)
            s = c * sg
            ds = dy_ref[:, lanes]
            if kind == "q":
                ds = ds * (HEAD_DIM ** -0.5)
            if kind != "v":
                r = lax.rsqrt(jnp.sum(s * s, axis=-1, keepdims=True) + EPS)
                sh = s * r
                ds = r * (ds - sh * jnp.sum(ds * sh, axis=-1, keepdims=True))
            dc = ds * _dsilu(c, sg)
            _stage(dc_staged, dc)
            dp_ref[:, lanes] = _anti_causal_conv(dc, dc_staged, w, GDN_CONV).astype(dp_ref.dtype)
            _accumulate(dw_ref.at[:, lanes], pl.program_id(1) == 0, _conv_weight_grad(dc, pre, pre_staged, GDN_CONV))

    return pl.pallas_call(
        body, name="qkv_bwd_" + kind,
        out_shape=(jax.ShapeDtypeStruct(dproj.shape, dproj.dtype), jax.ShapeDtypeStruct((GDN_CONV, GDN_WIDTH), F32)),
        grid=(HEADS // hb, n // rs),
        in_specs=[_seq_cols(rs, 0, hb), _seq_cols(rs, col0, hb), _tap_cols(GDN_CONV, col0, hb), _hbm()],
        out_specs=(_seq_cols(rs, col0, hb), _tap_cols(GDN_CONV, 0, hb)), input_output_aliases={3: 0},
        scratch_shapes=[_halo_scratch(rs), _halo_scratch(rs)],
        compiler_params=_params("parallel", "arbitrary"),
    )(dy, proj, conv_w, dproj)


SC_COL = 4 * HEADS


def _sc_fwd(proj, conv_w, rs, cat):
    n = proj.shape[0]

    hb = 2

    def body(x_ref, b_ref, c_ref, w_ref, _, y_ref, staged):
        for h in range(hb):
            lanes = _lanes_of(h)
            u = c_ref[:, lanes] * x_ref[:, lanes]
            _stage(staged, u)
            y_ref[:, lanes] = (b_ref[:, lanes] * _causal_conv(u, staged, w_ref[:, lanes], SC_CONV)).astype(y_ref.dtype)

    return pl.pallas_call(
        body, name="sc_fwd", out_shape=jax.ShapeDtypeStruct(cat.shape, cat.dtype), grid=(HEADS // hb, n // rs),
        in_specs=[_seq_cols(rs, SC_COL, hb), _seq_cols(rs, SC_COL + 4, hb), _seq_cols(rs, SC_COL + 8, hb),
                  _tap_cols(SC_CONV, 0, hb), _hbm()],
        out_specs=_seq_cols(rs, HEADS, hb), input_output_aliases={4: 0}, scratch_shapes=[_halo_scratch(rs)],
        compiler_params=_params("parallel", "parallel"),
    )(proj, proj, proj, conv_w, cat)


def _sc_bwd(dcat, proj, conv_w, rs, dproj):
    n = proj.shape[0]
    hb = 2

    def body(dy_ref, x_ref, b_ref, c_ref, w_ref, _, dx_ref, db_ref, dc_ref, dw_ref, u_staged, dcv_staged):
        for h in range(hb):
            lanes = _lanes_of(h)
            w = w_ref[:, lanes]
            x = x_ref[:, lanes]
            cc = c_ref[:, lanes]
            u = cc * x
            _stage(u_staged, u)
            dy = dy_ref[:, lanes]
            db_ref[:, lanes] = (dy * _causal_conv(u, u_staged, w, SC_CONV)).astype(db_ref.dtype)
            dcv = dy * b_ref[:, lanes]
            _stage(dcv_staged, dcv)
            du = _anti_causal_conv(dcv, dcv_staged, w, SC_CONV)
            dx_ref[:, lanes] = (du * cc).astype(dx_ref.dtype)
            dc_ref[:, lanes] = (du * x).astype(dc_ref.dtype)
            _accumulate(dw_ref.at[:, lanes], pl.program_id(1) == 0, _conv_weight_grad(dcv, u, u_staged, SC_CONV))

    piece = jax.ShapeDtypeStruct((n, SC_WIDTH), MXU_DTYPE)
    return pl.pallas_call(
        body, name="sc_bwd",
        out_shape=(jax.ShapeDtypeStruct(dproj.shape, dproj.dtype), piece, piece, jax.ShapeDtypeStruct((SC_CONV, SC_WIDTH), F32)),
        grid=(HEADS // hb, n // rs),
        in_specs=[_seq_cols(rs, HEADS, hb), _seq_cols(rs, SC_COL, hb), _seq_cols(rs, SC_COL + 4, hb),
                  _seq_cols(rs, SC_COL + 8, hb), _tap_cols(SC_CONV, 0, hb), _hbm()],
        out_specs=(_seq_cols(rs, SC_COL, hb), _seq_cols(rs, 0, hb), _seq_cols(rs, 0, hb), _tap_cols(SC_CONV, 0, hb)),
        input_output_aliases={5: 0},
        scratch_shapes=[_halo_scratch(rs), _halo_scratch(rs)],
        compiler_params=_params("parallel", "arbitrary"),
    )(dcat, proj, proj, proj, conv_w, dproj)


Z_COL = 3 * HEADS


def _gate_fwd(o, proj, gdn_norm, rs):
    n = proj.shape[0]

    hb = HEADS

    def body(o_ref, z_ref, w_ref, y_ref):
        for h in range(hb):
            lanes = _lanes_of(h)
            z = z_ref[:, lanes]
            y_ref[:, lanes] = (_rms_apply(o_ref[:, lanes], w_ref[...]) * z * _sigmoid(z)).astype(y_ref.dtype)

    return pl.pallas_call(
        body, name="gate_fwd", out_shape=jax.ShapeDtypeStruct((n, D_MODEL), MXU_DTYPE), grid=(HEADS // hb, n // rs),
        in_specs=[_seq_cols(rs, 0, hb), _seq_cols(rs, Z_COL, hb), pl.BlockSpec((1, LANES), lambda j, b: (0, 0))],
        out_specs=_seq_cols(rs, 0, hb), compiler_params=_params("parallel", "parallel"),
    )(o, proj, gdn_norm)


def _gate_bwd(dcat, o, proj, gdn_norm, rs):
    n = proj.shape[0]
    hb = 2

    def body(dy_ref, o_ref, z_ref, w_ref, do_ref, dz_ref, dw_ref):
        w = w_ref[...]
        dw_step = jnp.zeros((1, LANES), F32)
        for h in range(hb):
            lanes = _lanes_of(h)
            z = z_ref[:, lanes]
            o = o_ref[:, lanes]
            dy = dy_ref[:, lanes]
            s = _sigmoid(z)
            dz_ref[:, lanes] = (dy * _rms_apply(o, w) * _dsilu(z, s)).astype(dz_ref.dtype)
            do, dw = _rms_bwd(o, w, dy * z * s)
            do_ref[:, lanes] = do
            dw_step = dw_step + dw
        _accumulate(dw_ref, jnp.logical_and(pl.program_id(0) == 0, pl.program_id(1) == 0), dw_step)

    return pl.pallas_call(
        body, name="gate_bwd",
        out_shape=(jax.ShapeDtypeStruct((n, GDN_WIDTH), F32), jax.ShapeDtypeStruct((n, IN_PAD), MXU_DTYPE),
                   jax.ShapeDtypeStruct((1, LANES), F32)),
        grid=(HEADS // hb, n // rs),
        in_specs=[_seq_cols(rs, 0, hb), _seq_cols(rs, 0, hb), _seq_cols(rs, Z_COL, hb), pl.BlockSpec((1, LANES), lambda j, b: (0, 0))],
        out_specs=(_seq_cols(rs, 0, hb), _seq_cols(rs, Z_COL, hb), pl.BlockSpec((1, LANES), lambda j, b: (0, 0))),
        compiler_params=_params("arbitrary", "arbitrary"),
    )(dcat, o, proj, gdn_norm)


def _dot(a, b):
    return jnp.dot(a.astype(MXU_DTYPE), b.astype(MXU_DTYPE), preferred_element_type=F32)


def _dot_nt(a, b):
    return lax.dot_general(a.astype(MXU_DTYPE), b.astype(MXU_DTYPE), (((1,), (1,)), ((), ())),
                           preferred_element_type=F32)


def _dot_tn(a, b):
    return lax.dot_general(a.astype(MXU_DTYPE), b.astype(MXU_DTYPE), (((0,), (0,)), ((), ())),
                           preferred_element_type=F32)


def _split(x):
    hi = x.astype(MXU_DTYPE)
    return hi, (x - hi.astype(F32)).astype(MXU_DTYPE)


def _dot_split(a, b):
    mm = functools.partial(jnp.dot, preferred_element_type=F32)
    return mm(a[0], b[0]) + (mm(a[0], b[1]) + mm(a[1], b[0]))


def _unit_lower_inverses(mats, eye):
    inv = [eye - a for a in mats]
    power = [_split(a) for a in mats]
    square = [_dot_split(p, p) for p in power]
    inv = [i + _dot_split(_split(i), _split(s)) for i, s in zip(inv, square)]
    span = 4
    while span < CHUNK:
        square = [_dot(s, s) for s in square]
        inv = [i + _dot(i, s) for i, s in zip(inv, square)]
        span *= 2
    return inv


def _chunk_masks():
    ii = lax.broadcasted_iota(jnp.int32, (CHUNK, CHUNK), 0)
    jj = lax.broadcasted_iota(jnp.int32, (CHUNK, CHUNK), 1)
    return ii, jj


def _chunk_decay(g_col, ii, jj):
    incl = ii >= jj
    g_row = jnp.sum(jnp.where(ii == jj, g_col, 0.0), axis=0, keepdims=True)
    gc_col = jnp.sum(jnp.where(incl, g_row, 0.0), axis=1, keepdims=True)
    gc_row = jnp.sum(jnp.where(ii <= jj, g_col, 0.0), axis=0, keepdims=True)
    g_total = jnp.sum(g_row, axis=1, keepdims=True)
    decay = jnp.where(incl, jnp.exp(jnp.where(incl, gc_col - gc_row, 0.0)), 0.0)
    return gc_col, g_total, decay


def _gdn_segments(rs, candidates):
    chunks = rs // CHUNK
    seg_chunks = _pick(chunks, candidates)
    return chunks, seg_chunks, chunks // seg_chunks


def _head_lanes(h):
    return slice(h * HEAD_DIM, (h + 1) * HEAD_DIM)


def _gdn_fwd(q, k, v, bg, rs, pieces):
    n = q.shape[0]
    batch = n // rs
    chunks, seg_chunks, segs = _gdn_segments(rs, (11, 8, 4, 2))
    seg_rows = seg_chunks * CHUNK
    chains = [(b, h) for b in range(batch) for h in range(HEADS)]
    each = lambda f, *lists: [f(*args) for args in zip(*lists)]
    count = len(pieces)

    def body(q_ref, k_ref, v_ref, bg_ref, *rest):
        w_refs, (o_ref, s_ref, t_ref), out_refs = rest[:count], rest[count:count + 3], rest[count + 3:2 * count + 3]
        state_ref, send_sems, recv_sems = rest[2 * count + 3:]
        gather = _gather_copies(w_refs, out_refs, send_sems, recv_sems)

        @pl.when(pl.program_id(0) == 0)
        def _():
            state_ref[...] = jnp.zeros_like(state_ref)
            for cp in gather[0]:
                cp.start()

        ii, jj = _chunk_masks()
        incl = ii >= jj
        eye = (ii == jj).astype(F32)

        def chunk(c, carry):
            rows = pl.ds(pl.multiple_of(c * CHUNK, CHUNK), CHUNK)
            bgc = [bg_ref[b, rows, :] for b in range(batch)]
            qc = [q_ref[b, rows, _head_lanes(h)] for b, h in chains]
            kc = [k_ref[b, rows, _head_lanes(h)] for b, h in chains]
            vc = [v_ref[b, rows, _head_lanes(h)] for b, h in chains]
            beta = [bgc[b][:, h:h + 1] for b, h in chains]
            state = [state_ref[b, h] for b, h in chains]
            dec = [_chunk_decay(bgc[b][:, HEADS + h:HEADS + h + 1], ii, jj) for b, h in chains]
            gc_col, g_total, decay = ([d[i] for d in dec] for i in range(3))
            kb = each(lambda x, y: x * y, kc, beta)
            a = each(lambda x, y, d: jnp.where(ii > jj, _dot_nt(x, y) * d, 0.0), kb, kc, decay)
            t_inv = _unit_lower_inverses(a, eye)
            eg = [jnp.exp(g) for g in gc_col]
            u = each(lambda t, x, y: _dot(t, x * y), t_inv, vc, beta)
            w = each(lambda t, x, e: _dot(t, x * e), t_inv, kb, eg)
            qk = each(lambda x, y, d: jnp.where(incl, _dot_nt(x, y) * d, 0.0), qc, kc, decay)
            v_new = each(lambda x, y, s: x - _dot(y, s), u, w, state)
            o = each(lambda x, e, s, m, vn: _dot(x * e, s) + _dot(m, vn), qc, eg, state, qk, v_new)
            new_state = each(lambda s, gt, x, g, vn: s * jnp.exp(gt) + _dot_tn(x * jnp.exp(gt - g), vn),
                             state, g_total, kc, gc_col, v_new)
            for i, (b, h) in enumerate(chains):
                s_ref[b, h, c] = state[i]
                t_ref[b, h, c] = t_inv[i]
                o_ref[b, rows, _head_lanes(h)] = o[i]
                state_ref[b, h] = new_state[i]
            return carry

        lax.fori_loop(0, seg_chunks, chunk, 0)

        @pl.when(pl.program_id(0) == segs - 1)
        def _():
            _gather_finish(gather)

    rows_spec = lambda width: pl.BlockSpec((batch, seg_rows, width), lambda s: (0, s, 0))
    per_chunk = lambda r, c: pl.BlockSpec((batch, HEADS, seg_chunks, r, c), lambda s: (0, 0, s, 0, 0))
    as_seqs = lambda a: a.reshape(batch, rs, a.shape[-1])
    sems = GATHER_SEMS * count
    o, states, t_invs, *gathered = pl.pallas_call(
        body, name="gdn_fwd",
        out_shape=(jax.ShapeDtypeStruct((batch, rs, GDN_WIDTH), F32),
                   jax.ShapeDtypeStruct((batch, HEADS, chunks, HEAD_DIM, HEAD_DIM), F32),
                   jax.ShapeDtypeStruct((batch, HEADS, chunks, CHUNK, CHUNK), F32))
        + tuple(jax.ShapeDtypeStruct((N_CHIPS,) + p.shape, p.dtype) for p in pieces),
        grid=(segs,),
        in_specs=[rows_spec(GDN_WIDTH), rows_spec(GDN_WIDTH), rows_spec(GDN_WIDTH), rows_spec(LANES)] + [_hbm()] * count,
        out_specs=(rows_spec(GDN_WIDTH), per_chunk(HEAD_DIM, HEAD_DIM), per_chunk(CHUNK, CHUNK)) + (_hbm(),) * count,
        scratch_shapes=[pltpu.VMEM((batch, HEADS, HEAD_DIM, HEAD_DIM), F32), pltpu.SemaphoreType.DMA((sems,)),
                        pltpu.SemaphoreType.DMA((sems,))],
        compiler_params=_params("arbitrary"),
    )(as_seqs(q), as_seqs(k), as_seqs(v), as_seqs(bg), *pieces)
    return o.reshape(n, GDN_WIDTH), states, t_invs, gathered


def _gdn_bwd(do, q, k, v, bg, states, t_invs, rs, parts):
    n = q.shape[0]
    batch = n // rs
    chunks, seg_chunks, segs = _gdn_segments(rs, (3, 4, 2))
    seg_rows = seg_chunks * CHUNK
    chains = [(b, h) for b in range(batch) for h in range(HEADS)]
    each = lambda f, *lists: [f(*args) for args in zip(*lists)]
    count = len(parts)

    def body(do_ref, q_ref, k_ref, v_ref, bg_ref, s_ref, t_ref, *rest):
        p_refs, (dq_ref, dk_ref, dv_ref, dbg_ref), got_refs = rest[:count], rest[count:count + 4], rest[count + 4:2 * count + 4]
        dstate_ref, send_sems, recv_sems = rest[2 * count + 4:]
        exchange = _chip_copies(p_refs, got_refs, send_sems, recv_sems)

        @pl.when(pl.program_id(0) == 0)
        def _():
            dstate_ref[...] = jnp.zeros_like(dstate_ref)
            for cp in exchange:
                cp.start()

        ii, jj = _chunk_masks()
        incl = ii >= jj
        strict = ii > jj
        lane = lax.broadcasted_iota(jnp.int32, (1, LANES), 1)

        def rowsum(x):
            return jnp.sum(x, axis=1, keepdims=True)

        def total(x):
            return jnp.sum(rowsum(x), axis=0, keepdims=True)

        def chunk(step, carry):
            c = seg_chunks - 1 - step
            rows = pl.ds(pl.multiple_of(c * CHUNK, CHUNK), CHUNK)
            bgc = [bg_ref[b, rows, :] for b in range(batch)]
            qc = [q_ref[b, rows, _head_lanes(h)] for b, h in chains]
            kc = [k_ref[b, rows, _head_lanes(h)] for b, h in chains]
            vc = [v_ref[b, rows, _head_lanes(h)] for b, h in chains]
            doc = [do_ref[b, rows, _head_lanes(h)] for b, h in chains]
            beta = [bgc[b][:, h:h + 1] for b, h in chains]
            state = [s_ref[b, h, c] for b, h in chains]
            t_inv = [t_ref[b, h, c] for b, h in chains]
            d_state = [dstate_ref[b, h] for b, h in chains]
            dec = [_chunk_decay(bgc[b][:, HEADS + h:HEADS + h + 1], ii, jj) for b, h in chains]
            gc_col, g_total, decay = ([d[i] for d in dec] for i in range(3))
            kb = each(lambda x, y: x * y, kc, beta)
            vb = each(lambda x, y: x * y, vc, beta)
            eg = [jnp.exp(g) for g in gc_col]
            kbg = each(lambda x, y: x * y, kb, eg)
            a = each(lambda x, y, d: jnp.where(strict, _dot_nt(x, y) * d, 0.0), kb, kc, decay)
            qk = each(lambda x, y, d: jnp.where(incl, _dot_nt(x, y) * d, 0.0), qc, kc, decay)
            w = each(_dot, t_inv, kbg)
            u = each(_dot, t_inv, vb)
            q_dec = each(lambda x, y: x * y, qc, eg)
            ek = each(lambda gt, g: jnp.exp(gt - g), g_total, gc_col)
            k_dec = each(lambda x, y: x * y, kc, ek)
            g_last = [jnp.exp(gt) for gt in g_total]
            v_new = each(lambda x, y, s: x - _dot(y, s), u, w, state)
            dv_new = each(lambda m, d, x, ds: _dot_tn(m, d) + _dot(x, ds), qk, doc, k_dec, d_state)
            dqk = each(lambda d, vn: jnp.where(incl, _dot_nt(d, vn), 0.0), doc, v_new)
            dq_dec = each(_dot_nt, doc, state)
            dk_dec = each(_dot_nt, v_new, d_state)
            dg_last = each(lambda s, ds: total(s * ds), state, d_state)
            new_d_state = each(lambda x, d, gl, ds, y, dvn: _dot_tn(x, d) + gl * ds - _dot_tn(y, dvn),
                               q_dec, doc, g_last, d_state, w, dv_new)
            dw = each(lambda dvn, s: -_dot_nt(dvn, s), dv_new, state)
            dt = each(lambda dvn, x, y, z: _dot_nt(dvn, x) + _dot_nt(y, z), dv_new, vb, dw, kbg)
            dvb = each(_dot_tn, t_inv, dv_new)
            dkbg = each(_dot_tn, t_inv, dw)
            t_dt = each(_dot_tn, t_inv, dt)
            da = each(lambda x, t: -jnp.where(strict, _dot_nt(x, t), 0.0), t_dt, t_inv)
            dm_a = each(lambda x, y: x * y, da, decay)
            dm_qk = each(lambda x, y: x * y, dqk, decay)
            e = each(lambda x, y, z, t: x * y + z * t, da, a, dqk, qk)
            dkb = each(lambda m, x, y, z: _dot(m, x) + y * z, dm_a, kc, dkbg, eg)
            dk = each(lambda m, x, m2, y, z, t, p, bt: _dot_tn(m, x) + _dot_tn(m2, y) + z * t + p * bt,
                      dm_a, kb, dm_qk, qc, dk_dec, ek, dkb, beta)
            dq = each(lambda m, x, y, z: _dot(m, x) + y * z, dm_qk, kc, dq_dec, eg)
            dbeta = each(lambda x, y, z, t: rowsum(x * y + z * t), dkb, kc, dvb, vc)
            dgc = each(lambda x, p, pd, r, rd, s, sd: rowsum(x) - rowsum(jnp.where(ii == jj, jnp.sum(x, axis=0, keepdims=True), 0.0))
                       + rowsum(p * pd - r * rd + s * sd), e, dq_dec, q_dec, dk_dec, k_dec, dkbg, kbg)
            d_total = each(lambda r, rd, x, gl: total(r * rd) + x * gl, dk_dec, k_dec, dg_last, g_last)
            dg = each(lambda x, t: rowsum(jnp.where(jj >= ii, jnp.sum(jnp.where(ii == jj, x, 0.0), axis=0, keepdims=True), 0.0)) + t,
                      dgc, d_total)
            dbg = [jnp.zeros((CHUNK, LANES), F32) for _ in range(batch)]
            for i, (b, h) in enumerate(chains):
                dstate_ref[b, h] = new_d_state[i]
                dk_ref[b, rows, _head_lanes(h)] = dk[i]
                dq_ref[b, rows, _head_lanes(h)] = dq[i]
                dv_ref[b, rows, _head_lanes(h)] = dvb[i] * beta[i]
                dbg[b] = dbg[b] + jnp.where(lane == h, dbeta[i], 0.0) + jnp.where(lane == HEADS + h, dg[i], 0.0)
            for b in range(batch):
                dbg_ref[b, rows, :] = dbg[b]
            return carry

        lax.fori_loop(0, seg_chunks, chunk, 0)

        @pl.when(pl.program_id(0) == segs - 1)
        def _():
            for cp in exchange:
                cp.wait_recv()
            for cp in exchange:
                cp.wait_send()

    rows_spec = lambda width: pl.BlockSpec((batch, seg_rows, width), lambda s: (0, segs - 1 - s, 0))
    per_chunk = lambda r, c: pl.BlockSpec((batch, HEADS, seg_chunks, r, c), lambda s: (0, 0, segs - 1 - s, 0, 0))
    as_seqs = lambda a: a.reshape(batch, rs, a.shape[-1])
    grad = jax.ShapeDtypeStruct((batch, rs, GDN_WIDTH), F32)
    wide = rows_spec(GDN_WIDTH)
    dq, dk, dv, dbg, *got = pl.pallas_call(
        body, name="gdn_bwd",
        out_shape=(grad, grad, grad, jax.ShapeDtypeStruct((batch, rs, LANES), F32))
        + tuple(jax.ShapeDtypeStruct((3,) + p.shape[1:], p.dtype) for p in parts),
        grid=(segs,),
        in_specs=[wide, wide, wide, wide, rows_spec(LANES), per_chunk(HEAD_DIM, HEAD_DIM), per_chunk(CHUNK, CHUNK)]
        + [_hbm()] * count,
        out_specs=(wide, wide, wide, rows_spec(LANES)) + (_hbm(),) * count,
        scratch_shapes=[pltpu.VMEM((batch, HEADS, HEAD_DIM, HEAD_DIM), F32), pltpu.SemaphoreType.DMA((3 * count,)),
                        pltpu.SemaphoreType.DMA((3 * count,))],
        compiler_params=_params("arbitrary"),
    )(as_seqs(do), as_seqs(q), as_seqs(k), as_seqs(v), as_seqs(bg), states, t_invs, *parts)
    return dq.reshape(n, GDN_WIDTH), dk.reshape(n, GDN_WIDTH), dv.reshape(n, GDN_WIDTH), dbg.reshape(n, LANES), got


def _lane_vec(vals, offset):
    k = vals.shape[1]
    return jnp.pad(vals, ((0, 0), (offset, LANES - offset - k)))


LATER = ("w_out", "w_gate", "w_up", "w_down")


def _halves(a):
    return a.reshape(a.shape[:-2] + (2, a.shape[-2] // 2, a.shape[-1]))


def _local_step(x, target, meta, norms, w_in_t, conv_qkv, a_log, dt_bias, gdn_norm, conv_sc, later_shards, core_arg):
    batch, seq, d = x.shape
    tokens = N_META + seq
    pad_rows = (-tokens) % CHUNK
    rs = tokens + pad_rows
    x_offset = pad_rows + N_META
    n = batch * rs
    w_mix_pre, w_mix_post, w_ffn_pre, w_ffn_post = norms

    head = jnp.concatenate([jnp.zeros((pad_rows, d), F32), meta], axis=0)
    h0 = jnp.concatenate([jnp.broadcast_to(head[None], (batch, x_offset, d)), x], axis=1).reshape(n, d)
    target_p = jnp.pad(target, ((0, 0), (x_offset, 0), (0, 0))).reshape(n, d)
    a_log_l = _lane_vec(a_log, HEADS)
    dt_bias_l = _lane_vec(dt_bias, HEADS)

    u1 = _rms_fwd(h0, w_mix_pre, "rms_mix_pre")
    proj = _mm(u1, w_in_t, "nt", F32, "mm_proj")
    q = _qkv_fwd(proj, conv_qkv, "q", rs)
    k = _qkv_fwd(proj, conv_qkv, "k", rs)
    v = _qkv_fwd(proj, conv_qkv, "v", rs)
    bg = _gates_fwd(proj, a_log_l, dt_bias_l, rs, pad_rows)
    o, states, t_invs, gathered = _gdn_fwd(q, k, v, bg, rs, later_shards[:3])
    w_out, w_gate_t, w_up_t = (a.reshape(-1, d) for a in gathered)
    cat = _sc_fwd(proj, conv_sc, rs, _gate_fwd(o, proj, gdn_norm, rs))
    mix = _mm(cat, w_out, "nn", F32, "mm_mix")
    h1, u2 = _mix_residual(h0, mix, w_mix_post, w_ffn_pre)
    gate, up, act, w_down = _swiglu_fwd(u2, w_gate_t, w_up_t, later_shards[3])
    w_down = w_down.reshape(-1, d)
    ffn = _mm(act, w_down, "nn", F32, "mm_down")

    dh2, dffn, d_ffn_post, sq = _loss_head(h1, ffn, w_ffn_post, target_p, rs, x_offset)
    d_w_down = _mm(act, dffn, "tn", F32, "mm_dw_down")
    dgate, dup = _swiglu_bwd(dffn, w_down, gate, up)
    d_w_gate_t = _mm(dgate, u2, "tn", F32, "mm_dw_gate")
    d_w_up_t = _mm(dup, u2, "tn", F32, "mm_dw_up")
    du2 = _mm(dup, w_up_t, "nn", F32, "mm_du2_up", init=_mm(dgate, w_gate_t, "nn", F32, "mm_du2_gate"))
    by_chip = [_halves(g.reshape(N_CHIPS, -1, d)) for g in (d_w_gate_t, d_w_up_t, d_w_down)]
    dh1, dmix, d_ffn_pre, d_mix_post, got_sibling = _mid_bwd(h1, mix, w_mix_post, w_ffn_pre, dh2, du2, by_chip)
    dcat = _mm(dmix, w_out, "nt", F32, "mm_dcat")
    d_w_out = _halves(_mm(cat, dmix, "tn", F32, "mm_dw_out").reshape(N_CHIPS, -1, d))
    by_chip, got_sibling = [d_w_out] + by_chip, list(_exchange_siblings([d_w_out])) + got_sibling
    sums = [_add_sibling(a, b, core_arg, name) for name, a, b in zip(LATER, by_chip, got_sibling)]
    do, dproj, d_gdn_norm = _gate_bwd(dcat, o, proj, gdn_norm, rs)
    dproj, dscb, dscc, d_conv_sc = _sc_bwd(dcat, proj, conv_sc, rs, dproj)
    dq, dk, dv, dbg, got_chips = _gdn_bwd(do, q, k, v, bg, states, t_invs, rs, [send for _, send in sums[:3]])
    dproj, dwq = _qkv_bwd(dq, proj, conv_qkv, "q", rs, dproj)
    dproj, dwk = _qkv_bwd(dk, proj, conv_qkv, "k", rs, dproj)
    dproj, dwv = _qkv_bwd(dv, proj, conv_qkv, "v", rs, dproj)
    d_conv_qkv = jnp.concatenate([dwq, dwk, dwv], axis=1)
    dproj, d_a_log_l, d_dt_bias_l = _gates_bwd(proj, dbg, a_log_l, dt_bias_l, rs, pad_rows, dproj)
    dproj = lax.dynamic_update_slice(dproj, dscb, (0, (SC_COL + HEADS) * LANES))
    dproj = lax.dynamic_update_slice(dproj, dscc, (0, (SC_COL + 2 * HEADS) * LANES))
    d_w_in_t, got_down = _mm(dproj, u1, "tn", F32, "mm_dw_in", exchange=[sums[3][1]])
    got_chips.append(got_down)
    g_in = _halves(_in_from_kernel_order(d_w_in_t))
    sums.insert(0, _add_sibling(g_in, _exchange_siblings([g_in])[0], core_arg, "w_in"))
    du1, got_in = _mm(dproj, w_in_t, "nn", F32, "mm_du1", exchange=[sums[0][1]])
    got_chips.insert(0, got_in)
    dh0, d_mix_pre = _in_bwd(h0, w_mix_pre, dh1, du1)

    dh0 = dh0.reshape(batch, rs, d)
    grads = dict(
        meta_tokens=jnp.sum(dh0[:, pad_rows:x_offset], axis=0),
        mix_pre_norm=d_mix_pre, mix_post_norm=d_mix_post, ffn_pre_norm=d_ffn_pre, ffn_post_norm=d_ffn_post,
        conv_qkv=d_conv_qkv,
        a_log=d_a_log_l[:, HEADS:2 * HEADS], dt_bias=d_dt_bias_l[:, HEADS:2 * HEADS],
        gdn_norm=d_gdn_norm, conv_sc=d_conv_sc,
    )
    return sq, dh0[:, x_offset:], grads, [(part, got) for (part, _), got in zip(sums, got_chips)]


MATRICES = ("w_in", "w_out", "w_gate", "w_up", "w_down")
IN_SHARD = IN_WIDTH // N_CHIPS
IN_SHARD_PAD = 928


IN_SEGMENTS = ((0, 0, 4 * GDN_WIDTH), (4 * GDN_WIDTH, IN_WIDTH - 2 * HEADS, 2 * HEADS),
               (4 * GDN_WIDTH + 2 * HEADS, 4 * GDN_WIDTH, 3 * SC_WIDTH))
SUBLANES = 8
PACKED_ROWS = 16


def _in_to_kernel_order(by_chip):
    d = by_chip.shape[-1]
    tl = _pick(d, (256, 128))
    runs = []
    for ref0, ker0, count in IN_SEGMENTS:
        row = ref0
        while row < ref0 + count:
            chip, at = divmod(row, IN_SHARD)
            take = min(ref0 + count - row, IN_SHARD - at)
            runs.append((ker0 + row - ref0, take, chip * IN_SHARD_PAD + at))
            row += take

    def body(w_ref, o_ref):
        o_ref[...] = jnp.zeros_like(o_ref)
        for out0, rows, src0 in runs:
            a0 = out0 // PACKED_ROWS * PACKED_ROWS
            a1 = -(-(out0 + rows) // PACKED_ROWS) * PACKED_ROWS
            window = w_ref[pl.ds(src0 - (out0 - a0), a1 - a0), :]
            row = a0 + lax.broadcasted_iota(jnp.int32, (a1 - a0, 1), 0)
            keep = jnp.logical_and(row >= out0, row < out0 + rows)
            o_ref[a0:a1, :] = jnp.where(keep, window, o_ref[a0:a1, :])

    return pl.pallas_call(
        body, name="in_to_kernel_order", out_shape=jax.ShapeDtypeStruct((IN_PAD, d), by_chip.dtype), grid=(d // tl,),
        in_specs=[pl.BlockSpec((N_CHIPS * IN_SHARD_PAD, tl), lambda j: (0, j))],
        out_specs=pl.BlockSpec((IN_PAD, tl), lambda j: (0, j)),
        compiler_params=_params("parallel"),
    )(by_chip.reshape(N_CHIPS * IN_SHARD_PAD, d))


def _in_from_kernel_order(g_t):
    d = g_t.shape[-1]
    tl = _pick(d, (256, 128))

    def body(g_ref, o_ref):
        row = lax.broadcasted_iota(jnp.int32, (IN_SHARD_PAD, 1), 0)
        for chip in range(N_CHIPS):
            first = chip * IN_SHARD
            runs = []
            for ref0, ker0, count in IN_SEGMENTS:
                lo, hi = max(ref0, first), min(ref0 + count, first + IN_SHARD)
                if lo < hi:
                    runs.append((lo - first, hi - lo, ker0 + lo - ref0))
            val = jnp.zeros((IN_SHARD_PAD, tl), F32)
            patches = []
            for out0, rows, src0 in runs:
                start = src0 - out0
                if 0 <= start <= IN_PAD - IN_SHARD_PAD:
                    window = g_ref[pl.ds(start, IN_SHARD_PAD), :]
                    val = jnp.where(jnp.logical_and(row >= out0, row < out0 + rows), window, val)
                else:
                    patches.append((out0, rows, src0))
            o_ref[chip] = val
            for out0, rows, src0 in patches:
                a0 = out0 // SUBLANES * SUBLANES
                a1 = -(-(out0 + rows) // SUBLANES) * SUBLANES
                window = g_ref[pl.ds(src0 - (out0 - a0), a1 - a0), :]
                keep = jnp.logical_and(row[a0:a1] >= out0, row[a0:a1] < out0 + rows)
                o_ref[chip, a0:a1, :] = jnp.where(keep, window, o_ref[chip, a0:a1, :])

    return pl.pallas_call(
        body, name="in_from_kernel_order", out_shape=jax.ShapeDtypeStruct((N_CHIPS, IN_SHARD_PAD, d), F32), grid=(d // tl,),
        in_specs=[pl.BlockSpec((IN_PAD, tl), lambda j: (0, j))],
        out_specs=pl.BlockSpec((N_CHIPS, IN_SHARD_PAD, tl), lambda j: (0, 0, j)),
        compiler_params=_params("parallel"),
    )(g_t)


PACK_LANES = 3 * GDN_WIDTH
PACKED = dict(mix_pre_norm=(0, 1, 0, D_MODEL), mix_post_norm=(1, 1, 0, D_MODEL), ffn_pre_norm=(2, 1, 0, D_MODEL),
              ffn_post_norm=(3, 1, 0, D_MODEL), a_log=(4, 1, 0, HEADS), dt_bias=(5, 1, 0, HEADS), loss=(6, 1, 0, 1),
              gdn_norm=(7, 1, 0, HEAD_DIM), conv_qkv=(8, GDN_CONV, 0, 3 * GDN_WIDTH), conv_sc=(0, SC_CONV, D_MODEL, SC_WIDTH),
              meta_tokens=(16, N_META, 0, D_MODEL))
PACK_ROWS = 32
SHARDED_SMALL = ("conv_qkv", "conv_sc", "meta_tokens")


def _pack_small(values):
    names = list(PACKED)

    def body(*refs):
        out_ref = refs[-1]
        out_ref[...] = jnp.zeros_like(out_ref)
        for name, ref in zip(names, refs):
            row, rows, lane0, lanes = PACKED[name]
            out_ref[row:row + rows, lane0:lane0 + lanes] = ref[...]

    return pl.pallas_call(body, name="pack_small", out_shape=jax.ShapeDtypeStruct((PACK_ROWS, PACK_LANES), F32))(
        *[values[name] for name in names])


def _sum_devices(packed_all, chip):
    names = list(PACKED)

    def body(chip_ref, all_ref, *rest):
        shard_refs, out_refs = rest[:len(SHARDED_SMALL)], rest[len(SHARDED_SMALL):]

        def total(ref, rows, lanes):
            acc = ref[0, rows, lanes]
            for k in range(1, 8):
                acc = acc + ref[k, rows, lanes]
            return acc

        for name, out in zip(names, out_refs):
            row, rows, lane0, lanes = PACKED[name]
            if name in SHARDED_SMALL:
                out[...] = total(shard_refs[SHARDED_SMALL.index(name)], slice(0, rows), slice(None))
            else:
                out[...] = total(all_ref, slice(row, row + rows), slice(lane0, lane0 + lanes))

    def shard_spec(name):
        row, rows, lane0, lanes = PACKED[name]
        height, width = max(rows, 8), lanes // N_CHIPS
        assert row % height == 0 and lane0 % width == 0
        return pl.BlockSpec((8, height, width), lambda i, chip_ref: (0, row // height, lane0 // width + chip_ref[0]))

    def out_shape(name):
        _, rows, _, lanes = PACKED[name]
        return jax.ShapeDtypeStruct((rows, lanes // N_CHIPS if name in SHARDED_SMALL else lanes), F32)

    whole = lambda shape: pl.BlockSpec(shape, lambda i, chip_ref: (0,) * len(shape))
    outs = pl.pallas_call(
        body, name="sum_devices", out_shape=tuple(out_shape(n) for n in names),
        grid_spec=pltpu.PrefetchScalarGridSpec(
            num_scalar_prefetch=1, grid=(1,),
            in_specs=[whole(packed_all.shape)] + [shard_spec(n) for n in SHARDED_SMALL],
            out_specs=tuple(whole(out_shape(n).shape) for n in names)),
    )(chip, packed_all, *[packed_all] * len(SHARDED_SMALL))
    return dict(zip(names, outs))


def _hbm():
    return pl.BlockSpec(memory_space=pl.ANY)


def _place():
    x, y, c = lax.axis_index("x"), lax.axis_index("y"), lax.axis_index("c")
    chips = ((1 - x, y), (x, 1 - y), (1 - x, 1 - y))
    return x, y, c, chips


def _remote(src, dst, send_sems, recv_sems, k, to):
    return pltpu.make_async_remote_copy(src_ref=src, dst_ref=dst, send_sem=send_sems.at[k], recv_sem=recv_sems.at[k],
                                        device_id=to, device_id_type=MESH)


GATHER_SEMS = 7


def _gather_copies(w_refs, out_refs, send_sems, recv_sems):
    x, y, c, chips = _place()
    mine = 2 * x + y
    sibling = (x, y, 1 - c)
    copy = functools.partial(_remote, send_sems=send_sems, recv_sems=recv_sems)
    direct, landed, passing, from_sibling = [], [], [], []
    for i, (w, o) in enumerate(zip(w_refs, out_refs)):
        k = GATHER_SEMS * i
        direct.append(copy(w, o.at[mine], k=k, to=sibling))
        from_sibling.append(copy(w, o.at[mine], k=k, to=sibling))
        for j, (cx, cy) in enumerate(chips):
            theirs = 2 * cx + cy
            direct.append(copy(w.at[c], o.at[mine, c], k=k + 1 + j, to=(cx, cy, c)))
            landed.append(copy(w.at[c], o.at[theirs, c], k=k + 1 + j, to=sibling))
            passing.append(copy(o.at[theirs, c], o.at[theirs, c], k=k + 4 + j, to=sibling))
            from_sibling.append(copy(w.at[c], o.at[theirs, 1 - c], k=k + 4 + j, to=sibling))
    return direct, landed, passing, from_sibling


def _gather_finish(copies):
    direct, landed, passing, from_sibling = copies
    for arrival, forward in zip(landed, passing):
        arrival.wait_recv()
        forward.start()
    for arrival in from_sibling:
        arrival.wait_recv()
    for cp in direct + passing:
        cp.wait_send()


def _gather_weights(pieces, smalls):
    count, extra = len(pieces), len(smalls)
    total = count + extra

    def body(*refs):
        w_refs, s_refs = refs[:count], refs[count:total]
        out_refs, sall_refs = refs[total:total + count], refs[total + count:2 * total]
        send_sems, recv_sems, local_sems = refs[2 * total:]
        x, y, c, chips = _place()
        mine = 2 * x + y
        own = [pltpu.make_async_copy(s, sall.at[mine], local_sems.at[i]) for i, (s, sall) in enumerate(zip(s_refs, sall_refs))]
        small = [_remote(s, sall.at[mine], send_sems, recv_sems, GATHER_SEMS * count + 3 * i + j, (cx, cy, c))
                 for i, (s, sall) in enumerate(zip(s_refs, sall_refs)) for j, (cx, cy) in enumerate(chips)]
        copies = _gather_copies(w_refs, out_refs, send_sems, recv_sems)
        for cp in own + small + copies[0]:
            cp.start()
        _gather_finish(copies)
        for cp in small:
            cp.wait_recv()
        for cp in small:
            cp.wait_send()
        for cp in own:
            cp.wait()

    sems = GATHER_SEMS * count + 3 * extra
    return pl.pallas_call(
        body, name="gather_weights",
        out_shape=tuple(jax.ShapeDtypeStruct((N_CHIPS,) + p.shape, p.dtype) for p in list(pieces) + list(smalls)),
        in_specs=[_hbm()] * total, out_specs=(_hbm(),) * total,
        scratch_shapes=[pltpu.SemaphoreType.DMA((sems,)), pltpu.SemaphoreType.DMA((sems,)), pltpu.SemaphoreType.DMA((extra,))],
    )(*pieces, *smalls)


def _sibling_copies(g_refs, got_refs, send_sems, recv_sems):
    x, y, c, _ = _place()
    return [_remote(g.at[:, 1 - c], got, send_sems, recv_sems, i, (x, y, 1 - c)) for i, (g, got) in enumerate(zip(g_refs, got_refs))]


def _exchange_siblings(grads):
    count = len(grads)

    def body(*refs):
        copies = _sibling_copies(refs[:count], refs[count:2 * count], *refs[2 * count:])
        for cp in copies:
            cp.start()
        for cp in copies:
            cp.wait_recv()
        for cp in copies:
            cp.wait_send()

    return pl.pallas_call(
        body, name="exchange_siblings",
        out_shape=tuple(jax.ShapeDtypeStruct((g.shape[0],) + g.shape[2:], F32) for g in grads),
        in_specs=[_hbm()] * count, out_specs=(_hbm(),) * count,
        scratch_shapes=[pltpu.SemaphoreType.DMA((count,)), pltpu.SemaphoreType.DMA((count,))],
    )(*grads)


def _chip_copies(p_refs, got_refs, send_sems, recv_sems):
    x, y, c, chips = _place()
    return [_remote(p.at[2 * cx + cy], got.at[j], send_sems, recv_sems, 3 * i + j, (cx, cy, c))
            for i, (p, got) in enumerate(zip(p_refs, got_refs)) for j, (cx, cy) in enumerate(chips)]


def _share_halves(halves, small):
    count = len(halves)

    def body(*refs):
        h_refs, s_ref = refs[:count], refs[count]
        full_refs, sall_ref = refs[count + 1:2 * count + 1], refs[2 * count + 1]
        send_sems, recv_sems, local_sem = refs[2 * count + 2:]
        x, y, c, _ = _place()
        me = 4 * x + 2 * y + c
        own = pltpu.make_async_copy(s_ref, sall_ref.at[me], local_sem)
        own.start()
        copies = [_remote(h.at[c], full.at[c], send_sems, recv_sems, i, (x, y, 1 - c))
                  for i, (h, full) in enumerate(zip(h_refs, full_refs))]
        for k in range(7):
            dx, dy, dc = ((k + 1) >> 2) & 1, ((k + 1) >> 1) & 1, (k + 1) & 1
            peer = (1 - x if dx else x, 1 - y if dy else y, 1 - c if dc else c)
            copies.append(_remote(s_ref, sall_ref.at[me], send_sems, recv_sems, count + k, peer))
        for cp in copies:
            cp.start()
        for cp in copies:
            cp.wait_recv()
        for cp in copies:
            cp.wait_send()
        own.wait()

    return pl.pallas_call(
        body, name="share_halves",
        out_shape=tuple(jax.ShapeDtypeStruct(h.shape, h.dtype) for h in halves) + (jax.ShapeDtypeStruct((8,) + small.shape, F32),),
        in_specs=[_hbm()] * (count + 1), out_specs=(_hbm(),) * (count + 1), input_output_aliases={i: i for i in range(count)},
        scratch_shapes=[pltpu.SemaphoreType.DMA((count + 7,)), pltpu.SemaphoreType.DMA((count + 7,)), pltpu.SemaphoreType.DMA],
    )(*halves, small)


def _add_sibling(grad, got, core, name):
    chips, _, rows, cols = grad.shape

    def body(core_ref, g_ref, r_ref, sum_ref, send_ref):
        s = g_ref[...] + r_ref[...]
        sum_ref[...] = s
        send_ref[...] = s.astype(send_ref.dtype)

    block = pl.BlockSpec((None, rows, cols), lambda p, core_ref: (p, 0, 0))
    return pl.pallas_call(
        body, name="add_sibling_" + name,
        out_shape=(jax.ShapeDtypeStruct((chips, rows, cols), F32), jax.ShapeDtypeStruct((chips, rows, cols), BF16)),
        grid_spec=pltpu.PrefetchScalarGridSpec(
            num_scalar_prefetch=1, grid=(chips,),
            in_specs=[pl.BlockSpec((None, None, rows, cols), lambda p, core_ref: (p, core_ref[0], 0, 0)), block],
            out_specs=(block, block)),
        compiler_params=_params("parallel"),
    )(core, grad, got)


def _add_chips(part, got, chip_core, name):
    _, rows, cols = part.shape
    tr = rows // 2 if rows % 32 == 0 else rows

    def body(place_ref, p_ref, r_ref, o_ref):
        o_ref[...] = ((p_ref[...] + r_ref[0].astype(F32)) + r_ref[1].astype(F32)) + r_ref[2].astype(F32)

    return pl.pallas_call(
        body, name="add_chips_" + name, out_shape=jax.ShapeDtypeStruct((2, rows, cols), F32),
        grid_spec=pltpu.PrefetchScalarGridSpec(
            num_scalar_prefetch=1, grid=(rows // tr,),
            in_specs=[pl.BlockSpec((None, tr, cols), lambda i, place_ref: (place_ref[0], i, 0)),
                      pl.BlockSpec((3, tr, cols), lambda i, place_ref: (0, i, 0))],
            out_specs=pl.BlockSpec((None, tr, cols), lambda i, place_ref: (place_ref[1], i, 0))),
        compiler_params=_params("parallel"),
    )(chip_core, part, got)


def _adamw(w, g, m, v, name):
    rows, cols = w.shape
    tr = _pick(rows, (3592, 256, 352, 176, 128, 64, 32, 16, 8))

    def body(w_ref, g_ref, m_ref, v_ref, d_ref, nm_ref, nv_ref):
        d_ref[...], nm_ref[...], nv_ref[...] = _adamw_math(w_ref[...], g_ref[...], m_ref[...], v_ref[...])

    block = pl.BlockSpec((tr, cols), lambda i: (i, 0))
    shape = jax.ShapeDtypeStruct((rows, cols), F32)
    return pl.pallas_call(
        body, name="adamw_" + name, out_shape=(shape, shape, shape), grid=(rows // tr,),
        in_specs=[block] * 4, out_specs=(block,) * 3, compiler_params=_params("parallel"),
    )(w, g, m, v)


def _adamw_math(w, g, m, v):
    m = ADAM_B1 * m + (1.0 - ADAM_B1) * g
    v = ADAM_B2 * v + (1.0 - ADAM_B2) * (g * g)
    m_hat = m / (1.0 - ADAM_B1 ** ADAM_STEP)
    v_hat = v / (1.0 - ADAM_B2 ** ADAM_STEP)
    return -ADAM_LR * (m_hat / (jnp.sqrt(v_hat) + ADAM_EPS) + ADAM_WD * w), m, v


def _adamw_small(ws, gs, ms, vs):
    count = len(ws)

    def body(*refs):
        ins, outs = refs[:4 * count], refs[4 * count:]
        for i in range(count):
            outs[i][...], outs[count + i][...], outs[2 * count + i][...] = _adamw_math(
                ins[i][...], ins[count + i][...], ins[2 * count + i][...], ins[3 * count + i][...])

    shapes = tuple(jax.ShapeDtypeStruct(w.shape, F32) for w in ws)
    out = pl.pallas_call(body, name="adamw_small", out_shape=shapes * 3)(*ws, *gs, *ms, *vs)
    return out[:count], out[count:2 * count], out[2 * count:]


WEIGHTS = ("meta_tokens", "mix_pre_norm", "mix_post_norm", "ffn_pre_norm", "ffn_post_norm", "w_in", "conv_qkv", "a_log",
           "dt_bias", "gdn_norm", "conv_sc", "w_out", "w_gate", "w_up", "w_down")


def kernel(x, meta_tokens, mix_pre_norm, mix_post_norm, ffn_pre_norm, ffn_post_norm, w_in, conv_qkv, a_log, dt_bias, gdn_norm, conv_sc, w_out, w_gate, w_up, w_down, loss_target, m_meta_tokens, m_mix_pre_norm, m_mix_post_norm, m_ffn_pre_norm, m_ffn_post_norm, m_w_in, m_conv_qkv, m_a_log, m_dt_bias, m_gdn_norm, m_conv_sc, m_w_out, m_w_gate, m_w_up, m_w_down, v_meta_tokens, v_mix_pre_norm, v_mix_post_norm, v_ffn_pre_norm, v_ffn_post_norm, v_w_in, v_conv_qkv, v_a_log, v_dt_bias, v_gdn_norm, v_conv_sc, v_w_out, v_w_gate, v_w_up, v_w_down):
    d = x.shape[-1]
    two_d = lambda a: a.reshape(a.shape[-2:])
    weights = dict(zip(WEIGHTS, (meta_tokens, mix_pre_norm, mix_post_norm, ffn_pre_norm, ffn_post_norm, w_in, conv_qkv, a_log,
                                 dt_bias, gdn_norm, conv_sc, w_out, w_gate, w_up, w_down)))
    m_in = dict(zip(WEIGHTS, (m_meta_tokens, m_mix_pre_norm, m_mix_post_norm, m_ffn_pre_norm, m_ffn_post_norm, m_w_in, m_conv_qkv,
                              m_a_log, m_dt_bias, m_gdn_norm, m_conv_sc, m_w_out, m_w_gate, m_w_up, m_w_down)))
    v_in = dict(zip(WEIGHTS, (v_meta_tokens, v_mix_pre_norm, v_mix_post_norm, v_ffn_pre_norm, v_ffn_post_norm, v_w_in, v_conv_qkv,
                              v_a_log, v_dt_bias, v_gdn_norm, v_conv_sc, v_w_out, v_w_gate, v_w_up, v_w_down)))
    core = lax.axis_index("c")
    chip = 2 * lax.axis_index("x") + lax.axis_index("y")
    core_arg = core.reshape(1).astype(jnp.int32)
    chip_core = jnp.stack([chip, core]).astype(jnp.int32)
    whole = lambda a: a.reshape(a.shape[:-3] + (2 * a.shape[-2], d))
    by_rows = lambda n, a: two_d(a).T if n in ("w_in", "w_gate", "w_up") else two_d(a)

    shard = {n: by_rows(n, weights[n]).astype(MXU_DTYPE) for n in MATRICES}
    shard["w_in"] = jnp.pad(shard["w_in"], ((0, IN_SHARD_PAD - IN_SHARD), (0, 0)))
    w_in_all, *small_all = _gather_weights([_halves(shard["w_in"])], [two_d(weights[n]) for n in SHARDED_SMALL])
    w_in_t = _in_to_kernel_order(whole(w_in_all))
    conv_qkv_full, conv_sc_full, meta_full = (jnp.concatenate([a[p] for p in range(N_CHIPS)], axis=1) for a in small_all)

    sq, grad_x, g, sums = _local_step(
        x, loss_target, meta_full, (mix_pre_norm, mix_post_norm, ffn_pre_norm, ffn_post_norm), w_in_t, conv_qkv_full, a_log,
        dt_bias, gdn_norm, conv_sc_full, [_halves(shard[n]) for n in LATER], core_arg)

    totals = [_add_chips(part, got, chip_core, n) for n, (part, got) in zip(MATRICES, sums)]
    *shared, packed_all = _share_halves(totals, _pack_small(dict(g, loss=sq)))
    grads = {n: whole(a) for n, a in zip(MATRICES, shared)}
    grads["w_in"] = grads["w_in"][:IN_SHARD]
    grads.update(_sum_devices(packed_all, chip.reshape(1).astype(jnp.int32)))
    loss = (0.5 / d) * grads.pop("loss")[0, 0]

    small = [n for n in WEIGHTS if n not in MATRICES]
    updates = dict(zip(small, zip(*_adamw_small(*([by_rows(n, params[n]) for n in small] for params in (weights, grads, m_in, v_in))))))
    outs = [[], [], [], []]
    for n in WEIGHTS:
        shape = weights[n].shape
        if n in MATRICES:
            updates[n] = _adamw(by_rows(n, weights[n]), grads[n], by_rows(n, m_in[n]), by_rows(n, v_in[n]), n)
        for out, a in zip(outs, (grads[n], *updates[n])):
            out.append((a.T if n in ("w_in", "w_gate", "w_up") else a).reshape(shape))
    return (loss, grad_x, *outs[0], *outs[1], *outs[2], *outs[3])
```

```python
import functools

import jax
import jax.numpy as jnp
from jax import lax
from jax.experimental import pallas as pl
from jax.experimental.pallas import tpu as pltpu

F32 = jnp.float32
BF16 = jnp.bfloat16
MXU_DTYPE = jnp.bfloat16
MESH = pl.DeviceIdType.MESH

D_MODEL = 1024
N_META = 16
HEADS = 4
HEAD_DIM = 128
GDN_WIDTH = HEADS * HEAD_DIM
GDN_CONV = 4
CHUNK = 64
SC_WIDTH = D_MODEL - GDN_WIDTH
SC_CONV = 3
D_FF = 2816
IN_WIDTH = 4 * GDN_WIDTH + 2 * HEADS + 3 * SC_WIDTH
IN_PAD = 3840
BA_COL = (4 * GDN_WIDTH + 3 * SC_WIDTH) // 128
EPS = 1e-6
LANES = 128
N_CHIPS = 4
VMEM_LIMIT = 48 * 2 ** 20

ADAM_LR = 0.001
ADAM_B1 = 0.9
ADAM_B2 = 0.999
ADAM_EPS = 1e-08
ADAM_WD = 0.01
ADAM_STEP = 10


def _pick(n, candidates):
    for c in candidates:
        if n % c == 0:
            return c
    return n


def _row_tile(n):
    return _pick(n, (352, 256, 176, 128, 64, 32, 16, 8))


def _params(*sem):
    return pltpu.CompilerParams(dimension_semantics=sem, vmem_limit_bytes=VMEM_LIMIT)


def _sigmoid(x):
    return 0.5 * jnp.tanh(0.5 * x) + 0.5


def _softplus(x):
    return jnp.maximum(x, 0.0) + jnp.log(1.0 + jnp.exp(-jnp.abs(x)))


def _dsilu(x, s):
    return s * (1.0 + x * (1.0 - s))


def _mm(a, b, mode, out_dtype, name, init=None, exchange=None):
    if mode == "tn":
        k_dim, m_dim = a.shape
    else:
        m_dim, k_dim = a.shape
    n_dim = b.shape[0] if mode == "nt" else b.shape[1]
    rows = (1056, 1024, 704, 512, 256, 128) if init is not None else (2112, 1056, 1024, 704, 512, 256, 128)
    tm = _pick(m_dim, (1408, 1280, 1024, 512, 256, 128) if mode == "tn" else rows)
    tn = _pick(n_dim, (1408, 1280, 1024, 768, 512, 256, 128))
    tk = _pick(k_dim, (1408, 1280, 1056, 1024, 512, 256, 128))
    nk = k_dim // tk
    if mode == "nn":
        a_spec = pl.BlockSpec((tm, tk), lambda i, j, k: (i, k))
        b_spec = pl.BlockSpec((tk, tn), lambda i, j, k: (k, j))
        dims = (((1,), (0,)), ((), ()))
    elif mode == "nt":
        a_spec = pl.BlockSpec((tm, tk), lambda i, j, k: (i, k))
        b_spec = pl.BlockSpec((tn, tk), lambda i, j, k: (j, k))
        dims = (((1,), (1,)), ((), ()))
    else:
        a_spec = pl.BlockSpec((tk, tm), lambda i, j, k: (k, i))
        b_spec = pl.BlockSpec((tk, tn), lambda i, j, k: (k, j))
        dims = (((0,), (0,)), ((), ()))

    out_spec = pl.BlockSpec((tm, tn), lambda i, j, k: (i, j))
    grid = (m_dim // tm, n_dim // tn, nk)
    parts = () if exchange is None else tuple(exchange)
    count = len(parts)
    first_in = 2 if init is None else 3

    assert out_dtype == F32

    def body(a_ref, b_ref, *rest):
        o_ref = rest[first_in - 2 + count]
        k = pl.program_id(2)
        step = (pl.program_id(0) * grid[1] + pl.program_id(1)) * nk + k
        if count:
            copies = _chip_copies(rest[first_in - 2:first_in - 2 + count], rest[first_in - 1 + count:first_in - 1 + 2 * count],
                                  *rest[first_in - 1 + 2 * count:])

            @pl.when(step == 0)
            def _():
                for cp in copies:
                    cp.start()

        p = lax.dot_general(a_ref[...], b_ref[...], dims, preferred_element_type=F32)
        if nk == 1:
            o_ref[...] = p if init is None else rest[0][...] + p
        else:
            @pl.when(k == 0)
            def _():
                o_ref[...] = p if init is None else rest[0][...] + p

            @pl.when(k > 0)
            def _():
                o_ref[...] += p

        if count:
            @pl.when(step == grid[0] * grid[1] * nk - 1)
            def _():
                for cp in copies:
                    cp.wait_recv()
                for cp in copies:
                    cp.wait_send()

    out = pl.pallas_call(
        body, name=name,
        out_shape=(jax.ShapeDtypeStruct((m_dim, n_dim), out_dtype),)
        + tuple(jax.ShapeDtypeStruct((3,) + p.shape[1:], p.dtype) for p in parts),
        grid=grid,
        in_specs=[a_spec, b_spec] + ([] if init is None else [out_spec]) + [_hbm()] * count,
        out_specs=(out_spec,) + (_hbm(),) * count,
        scratch_shapes=[pltpu.SemaphoreType.DMA((3 * count,)), pltpu.SemaphoreType.DMA((3 * count,))] if count else [],
        compiler_params=_params(*(("arbitrary",) * 3 if count else ("parallel", "parallel", "arbitrary"))),
    )(a, b, *(() if init is None else (init,)), *parts)
    return out[0] if not count else out


def _rms_apply(x, w):
    r = lax.rsqrt(jnp.mean(x * x, axis=-1, keepdims=True) + EPS)
    return x * r * w


def _rms_bwd(x, w, dy):
    r = lax.rsqrt(jnp.mean(x * x, axis=-1, keepdims=True) + EPS)
    xh = x * r
    dyw = dy * w
    dx = r * (dyw - xh * jnp.mean(dyw * xh, axis=-1, keepdims=True))
    return dx, jnp.sum(dy * xh, axis=0, keepdims=True)


def _accumulate(ref, first, value):
    @pl.when(first)
    def _():
        ref[...] = value

    @pl.when(jnp.logical_not(first))
    def _():
        ref[...] += value


def _rows(tr, width):
    return pl.BlockSpec((tr, width), lambda i: (i, 0))


def _vec(width):
    return pl.BlockSpec((1, width), lambda i: (0, 0))


def _rms_fwd(h, w, name):
    n, d = h.shape
    tr = _row_tile(n)

    def body(h_ref, w_ref, u_ref):
        u_ref[...] = _rms_apply(h_ref[...], w_ref[...]).astype(u_ref.dtype)

    return pl.pallas_call(
        body, name=name, out_shape=jax.ShapeDtypeStruct((n, d), MXU_DTYPE), grid=(n // tr,),
        in_specs=[_rows(tr, d), _vec(d)], out_specs=_rows(tr, d), compiler_params=_params("parallel"),
    )(h, w)


def _mix_residual(h0, mix, w_post, w_pre):
    n, d = h0.shape
    tr = _row_tile(n)

    def body(h0_ref, mix_ref, wpost_ref, wpre_ref, h1_ref, u2_ref):
        h1 = h0_ref[...] + _rms_apply(mix_ref[...], wpost_ref[...])
        h1_ref[...] = h1
        u2_ref[...] = _rms_apply(h1, wpre_ref[...]).astype(u2_ref.dtype)

    return pl.pallas_call(
        body, name="mix_residual",
        out_shape=(jax.ShapeDtypeStruct((n, d), F32), jax.ShapeDtypeStruct((n, d), MXU_DTYPE)), grid=(n // tr,),
        in_specs=[_rows(tr, d), _rows(tr, d), _vec(d), _vec(d)], out_specs=(_rows(tr, d), _rows(tr, d)),
        compiler_params=_params("parallel"),
    )(h0, mix, w_post, w_pre)


NT_DIMS = (((1,), (1,)), ((), ()))


def _ffn_tiles(n):
    return _pick(n, (1056, 704, 512, 256, 128)), _pick(D_FF, (1408, 256, 128))


def _swiglu_fwd(u, w_gate_t, w_up_t, w_next):
    n, d = u.shape
    tm, tn = _ffn_tiles(n)
    grid = (D_FF // tn, n // tm)

    def body(u_ref, wg_ref, wu_ref, wn_ref, g_ref, up_ref, act_ref, wall_ref, send_sems, recv_sems):
        gather = _gather_copies([wn_ref], [wall_ref], send_sems, recv_sems)
        step = pl.program_id(0) * grid[1] + pl.program_id(1)

        @pl.when(step == 0)
        def _():
            for cp in gather[0]:
                cp.start()

        a = u_ref[...]
        g = lax.dot_general(a, wg_ref[...], NT_DIMS, preferred_element_type=F32)
        up = lax.dot_general(a, wu_ref[...], NT_DIMS, preferred_element_type=F32)
        g_ref[...] = g.astype(g_ref.dtype)
        up_ref[...] = up.astype(up_ref.dtype)
        act_ref[...] = (g * _sigmoid(g) * up).astype(act_ref.dtype)

        @pl.when(step == grid[0] * grid[1] - 1)
        def _():
            _gather_finish(gather)

    tile = pl.BlockSpec((tm, tn), lambda j, i: (i, j))
    weight = pl.BlockSpec((tn, d), lambda j, i: (j, 0))
    wide = jax.ShapeDtypeStruct((n, D_FF), MXU_DTYPE)
    return pl.pallas_call(
        body, name="swiglu_fwd",
        out_shape=(wide, wide, jax.ShapeDtypeStruct((n, D_FF), MXU_DTYPE),
                   jax.ShapeDtypeStruct((N_CHIPS,) + w_next.shape, w_next.dtype)),
        grid=grid,
        in_specs=[pl.BlockSpec((tm, d), lambda j, i: (i, 0)), weight, weight, _hbm()], out_specs=(tile, tile, tile, _hbm()),
        scratch_shapes=[pltpu.SemaphoreType.DMA((GATHER_SEMS,)), pltpu.SemaphoreType.DMA((GATHER_SEMS,))],
        compiler_params=_params("arbitrary", "arbitrary"),
    )(u, w_gate_t, w_up_t, w_next)


def _swiglu_bwd(dffn, w_down, gate, up):
    n, d = dffn.shape
    tm, tn = _ffn_tiles(n)

    def body(dy_ref, w_ref, g_ref, u_ref, dg_ref, du_ref):
        da = lax.dot_general(dy_ref[...], w_ref[...], NT_DIMS, preferred_element_type=F32)
        g = g_ref[...].astype(F32)
        s = _sigmoid(g)
        dg_ref[...] = (da * u_ref[...].astype(F32) * _dsilu(g, s)).astype(dg_ref.dtype)
        du_ref[...] = (da * g * s).astype(du_ref.dtype)

    tile = pl.BlockSpec((tm, tn), lambda j, i: (i, j))
    shape = jax.ShapeDtypeStruct((n, D_FF), MXU_DTYPE)
    return pl.pallas_call(
        body, name="swiglu_bwd", out_shape=(shape, shape), grid=(D_FF // tn, n // tm),
        in_specs=[pl.BlockSpec((tm, d), lambda j, i: (i, 0)), pl.BlockSpec((tn, d), lambda j, i: (j, 0)), tile, tile],
        out_specs=(tile, tile), compiler_params=_params("parallel", "parallel"),
    )(dffn, w_down, gate, up)


def _loss_head(h1, ffn, w_post, target, rows_per_seq, x_offset):
    n, d = h1.shape
    tr = _row_tile(rows_per_seq)
    tiles_per_seq = rows_per_seq // tr
    seq = target.shape[1]

    def seq_rows(t_ref, tile):
        first = jnp.concatenate([jnp.zeros((x_offset, d), F32), t_ref[0:tr - x_offset, :]], axis=0)
        if tiles_per_seq == 1:
            return first
        start = pl.multiple_of(jnp.maximum(tile * tr - x_offset, 0), SUBLANES)
        return jnp.where(tile == 0, first, t_ref[pl.ds(start, tr), :])

    def body(h1_ref, ffn_ref, w_ref, t_ref, dh2_ref, dffn_ref, dw_ref, sq_ref):
        i = pl.program_id(0)
        tile = lax.rem(i, tiles_per_seq)
        w = w_ref[...]
        f = ffn_ref[...]
        r = lax.rsqrt(jnp.mean(f * f, axis=-1, keepdims=True) + EPS)
        fh = f * r
        row = tile * tr + lax.broadcasted_iota(jnp.int32, (tr, 1), 0)
        err = jnp.where(row >= x_offset, h1_ref[...] + fh * w - seq_rows(t_ref, tile), 0.0)
        dh2 = err * (1.0 / d)
        dh2_ref[...] = dh2
        dyw = dh2 * w
        dffn_ref[...] = (r * (dyw - fh * jnp.mean(dyw * fh, axis=-1, keepdims=True))).astype(dffn_ref.dtype)
        _accumulate(dw_ref, i == 0, jnp.sum(dh2 * fh, axis=0, keepdims=True))
        _accumulate(sq_ref, i == 0, jnp.sum(jnp.sum(err * err, axis=1, keepdims=True), axis=0, keepdims=True))

    return pl.pallas_call(
        body, name="loss_head",
        out_shape=(jax.ShapeDtypeStruct((n, d), F32), jax.ShapeDtypeStruct((n, d), MXU_DTYPE),
                   jax.ShapeDtypeStruct((1, d), F32), jax.ShapeDtypeStruct((1, 1), F32)),
        grid=(n // tr,),
        in_specs=[_rows(tr, d), _rows(tr, d), _vec(d), pl.BlockSpec((None, seq, d), lambda i: (i // tiles_per_seq, 0, 0))],
        out_specs=(_rows(tr, d), _rows(tr, d), _vec(d), _vec(1)),
        compiler_params=_params("arbitrary"),
    )(h1, ffn, w_post, target)


def _mid_bwd(h1, mix, w_mix_post, w_ffn_pre, dh2, du2, grads):
    n, d = h1.shape
    tr = _row_tile(n)
    count = len(grads)

    def body(h1_ref, mix_ref, wpost_ref, wpre_ref, dh2_ref, du2_ref, *rest):
        g_refs, (dh1_ref, dmix_ref, dwpre_ref, dwpost_ref), got_refs = rest[:count], rest[count:count + 4], rest[count + 4:2 * count + 4]
        exchange = _sibling_copies(g_refs, got_refs, *rest[2 * count + 4:])
        i = pl.program_id(0)

        @pl.when(i == 0)
        def _():
            for cp in exchange:
                cp.start()

        dx, dwpre = _rms_bwd(h1_ref[...], wpre_ref[...], du2_ref[...])
        dh1 = dh2_ref[...] + dx
        dh1_ref[...] = dh1
        dmix, dwpost = _rms_bwd(mix_ref[...], wpost_ref[...], dh1)
        dmix_ref[...] = dmix.astype(dmix_ref.dtype)
        _accumulate(dwpre_ref, i == 0, dwpre)
        _accumulate(dwpost_ref, i == 0, dwpost)

        @pl.when(i == n // tr - 1)
        def _():
            for cp in exchange:
                cp.wait_recv()
            for cp in exchange:
                cp.wait_send()

    dh1, dmix, dwpre, dwpost, *got = pl.pallas_call(
        body, name="mid_bwd",
        out_shape=(jax.ShapeDtypeStruct((n, d), F32), jax.ShapeDtypeStruct((n, d), MXU_DTYPE),
                   jax.ShapeDtypeStruct((1, d), F32), jax.ShapeDtypeStruct((1, d), F32))
        + tuple(jax.ShapeDtypeStruct((g.shape[0],) + g.shape[2:], F32) for g in grads),
        grid=(n // tr,),
        in_specs=[_rows(tr, d), _rows(tr, d), _vec(d), _vec(d), _rows(tr, d), _rows(tr, d)] + [_hbm()] * count,
        out_specs=(_rows(tr, d), _rows(tr, d), _vec(d), _vec(d)) + (_hbm(),) * count,
        scratch_shapes=[pltpu.SemaphoreType.DMA((count,)), pltpu.SemaphoreType.DMA((count,))],
        compiler_params=_params("arbitrary"),
    )(h1, mix, w_mix_post, w_ffn_pre, dh2, du2, *grads)
    return dh1, dmix, dwpre, dwpost, got


def _in_bwd(h0, w_pre, dh1, du1, rows_per_seq, pad_rows, x_offset):
    n, d = h0.shape
    tr = _row_tile(rows_per_seq)
    tiles_per_seq = rows_per_seq // tr
    seq = rows_per_seq - x_offset

    def body(h0_ref, w_ref, dh1_ref, du1_ref, gx_ref, dmeta_ref, dw_ref):
        i = pl.program_id(0)
        tile = lax.rem(i, tiles_per_seq)
        dx, dw = _rms_bwd(h0_ref[...], w_ref[...], du1_ref[...])
        dh0 = dh1_ref[...] + dx
        _accumulate(dw_ref, i == 0, dw)

        @pl.when(tile == 0)
        def _():
            gx_ref[0:tr - x_offset, :] = dh0[x_offset:, :]
            _accumulate(dmeta_ref, i == 0, dh0[pad_rows:x_offset, :])

        if tiles_per_seq > 1:
            @pl.when(tile > 0)
            def _():
                gx_ref[pl.ds(pl.multiple_of(tile * tr - x_offset, SUBLANES), tr), :] = dh0

    return pl.pallas_call(
        body, name="in_bwd",
        out_shape=(jax.ShapeDtypeStruct((n // rows_per_seq, seq, d), F32), jax.ShapeDtypeStruct((x_offset - pad_rows, d), F32),
                   jax.ShapeDtypeStruct((1, d), F32)),
        grid=(n // tr,),
        in_specs=[_rows(tr, d), _vec(d), _rows(tr, d), _rows(tr, d)],
        out_specs=(pl.BlockSpec((None, seq, d), lambda i: (i // tiles_per_seq, 0, 0)),
                   pl.BlockSpec((x_offset - pad_rows, d), lambda i: (0, 0)), _vec(d)),
        compiler_params=_params("arbitrary"),
    )(h0, w_pre, dh1, du1)


def _lane_is(lo, hi):
    lane = lax.broadcasted_iota(jnp.int32, (1, LANES), 1)
    return jnp.logical_and(lane >= lo, lane < hi)


def _gates_fwd(proj, a_log_l, dt_bias_l, rows_per_seq, pad_rows):
    n = proj.shape[0]
    tr = _row_tile(rows_per_seq)
    tiles_per_seq = rows_per_seq // tr

    def body(p_ref, a_ref, dt_ref, o_ref):
        x = p_ref[...]
        row = lax.rem(pl.program_id(0), tiles_per_seq) * tr + lax.broadcasted_iota(jnp.int32, (tr, 1), 0)
        g = -jnp.exp(a_ref[...]) * _softplus(x + dt_ref[...])
        val = jnp.where(_lane_is(0, HEADS), _sigmoid(x), jnp.where(_lane_is(HEADS, 2 * HEADS), g, 0.0))
        o_ref[...] = jnp.where(row >= pad_rows, val, 0.0)

    return pl.pallas_call(
        body, name="gates_fwd", out_shape=jax.ShapeDtypeStruct((n, LANES), F32), grid=(n // tr,),
        in_specs=[pl.BlockSpec((tr, LANES), lambda i: (i, BA_COL)), _vec(LANES), _vec(LANES)],
        out_specs=_rows(tr, LANES), compiler_params=_params("parallel"),
    )(proj, a_log_l, dt_bias_l)


def _gates_bwd(proj, dbg, a_log_l, dt_bias_l, rows_per_seq, pad_rows, dproj):
    n = proj.shape[0]
    tr = _row_tile(rows_per_seq)
    tiles_per_seq = rows_per_seq // tr

    def body(p_ref, d_ref, a_ref, dt_ref, _, dx_ref, da_ref, ddt_ref):
        i = pl.program_id(0)
        x = p_ref[...]
        d = d_ref[...]
        row = lax.rem(i, tiles_per_seq) * tr + lax.broadcasted_iota(jnp.int32, (tr, 1), 0)
        live = row >= pad_rows
        beta = _sigmoid(x)
        ea = jnp.exp(a_ref[...])
        xa = x + dt_ref[...]
        g = -ea * _softplus(xa)
        is_g = _lane_is(HEADS, 2 * HEADS)
        d_alogit = jnp.where(jnp.logical_and(live, is_g), d * (-ea) * _sigmoid(xa), 0.0)
        d_blogit = jnp.where(jnp.logical_and(live, _lane_is(0, HEADS)), d * beta * (1.0 - beta), 0.0)
        dx_ref[:, :LANES] = (d_alogit + d_blogit).astype(dx_ref.dtype)
        dx_ref[:, LANES:] = jnp.zeros((tr, LANES), dx_ref.dtype)
        _accumulate(da_ref, i == 0, jnp.sum(jnp.where(jnp.logical_and(live, is_g), d * g, 0.0), axis=0, keepdims=True))
        _accumulate(ddt_ref, i == 0, jnp.sum(d_alogit, axis=0, keepdims=True))

    return pl.pallas_call(
        body, name="gates_bwd",
        out_shape=(jax.ShapeDtypeStruct(dproj.shape, dproj.dtype), jax.ShapeDtypeStruct((1, LANES), F32),
                   jax.ShapeDtypeStruct((1, LANES), F32)),
        grid=(n // tr,),
        in_specs=[pl.BlockSpec((tr, LANES), lambda i: (i, BA_COL)), _rows(tr, LANES), _vec(LANES), _vec(LANES), _hbm()],
        out_specs=(pl.BlockSpec((tr, 2 * LANES), lambda i: (i, BA_COL // 2)), _vec(LANES), _vec(LANES)),
        input_output_aliases={4: 0},
        compiler_params=_params("arbitrary"),
    )(proj, dbg, a_log_l, dt_bias_l, dproj)


HALO = 8


def _halo_scratch(rs):
    return pltpu.VMEM((rs + 2 * HALO, LANES), F32)


def _stage(ref, x):
    rs = x.shape[0]
    ref[0:HALO, :] = jnp.zeros((HALO, LANES), F32)
    ref[HALO + rs:, :] = jnp.zeros((HALO, LANES), F32)
    ref[HALO:HALO + rs, :] = x


def _shifted(ref, k, rs):
    return ref[pl.ds(HALO - k, rs), :]


def _causal_conv(x, x_staged, w, width):
    acc = w[width - 1:width, :] * x
    for i in range(width - 1):
        acc = acc + w[i:i + 1, :] * _shifted(x_staged, width - 1 - i, x.shape[0])
    return acc


def _anti_causal_conv(dy, dy_staged, w, width):
    acc = w[width - 1:width, :] * dy
    for i in range(width - 1):
        acc = acc + w[i:i + 1, :] * _shifted(dy_staged, -(width - 1 - i), dy.shape[0])
    return acc


def _conv_weight_grad(dy, x, x_staged, width):
    taps = [_shifted(x_staged, width - 1 - i, x.shape[0]) for i in range(width - 1)] + [x]
    return jnp.concatenate([jnp.sum(dy * tap, axis=0, keepdims=True) for tap in taps], axis=0)


def _seq_cols(rs, col0, heads):
    return pl.BlockSpec((rs, heads * LANES), lambda j, b: (b, col0 // heads + j))


def _tap_cols(width, col0, heads):
    return pl.BlockSpec((width, heads * LANES), lambda j, b: (0, col0 // heads + j))


def _lanes_of(h):
    return slice(h * LANES, (h + 1) * LANES)


def _qkv_fwd(proj, conv_w, kind, rs):
    n = proj.shape[0]
    col0 = {"q": 0, "k": HEADS, "v": 2 * HEADS}[kind]
    hb = HEADS

    def body(p_ref, w_ref, o_ref, staged):
        for h in range(hb):
            pre = p_ref[:, _lanes_of(h)]
            _stage(staged, pre)
            c = _causal_conv(pre, staged, w_ref[:, _lanes_of(h)], GDN_CONV)
            s = c * _sigmoid(c)
            if kind != "v":
                s = s * lax.rsqrt(jnp.sum(s * s, axis=-1, keepdims=True) + EPS)
            if kind == "q":
                s = s * (HEAD_DIM ** -0.5)
            o_ref[:, _lanes_of(h)] = s

    return pl.pallas_call(
        body, name="qkv_fwd_" + kind, out_shape=jax.ShapeDtypeStruct((n, GDN_WIDTH), F32), grid=(HEADS // hb, n // rs),
        in_specs=[_seq_cols(rs, col0, hb), _tap_cols(GDN_CONV, col0, hb)],
        out_specs=_seq_cols(rs, 0, hb), scratch_shapes=[_halo_scratch(rs)], compiler_params=_params("parallel", "parallel"),
    )(proj, conv_w)


def _qkv_bwd(dy, proj, conv_w, kind, rs, dproj):
    n = proj.shape[0]
    col0 = {"q": 0, "k": HEADS, "v": 2 * HEADS}[kind]
    hb = HEADS

    def body(dy_ref, p_ref, w_ref, _, dp_ref, dw_ref, pre_staged, dc_staged):
        for h in range(hb):
            lanes = _lanes_of(h)
            pre = p_ref[:, lanes]
            w = w_ref[:, lanes]
            _stage(pre_staged, pre)
            c = _causal_conv(pre, pre_staged, w, GDN_CONV)
            sg = _sigmoid(c)
            s = c * sg
            ds = dy_ref[:, lanes]
            if kind == "q":
                ds = ds * (HEAD_DIM ** -0.5)
            if kind != "v":
                r = lax.rsqrt(jnp.sum(s * s, axis=-1, keepdims=True) + EPS)
                sh = s * r
                ds = r * (ds - sh * jnp.sum(ds * sh, axis=-1, keepdims=True))
            dc = ds * _dsilu(c, sg)
            _stage(dc_staged, dc)
            dp_ref[:, lanes] = _anti_causal_conv(dc, dc_staged, w, GDN_CONV).astype(dp_ref.dtype)
            _accumulate(dw_ref.at[:, lanes], pl.program_id(1) == 0, _conv_weight_grad(dc, pre, pre_staged, GDN_CONV))

    return pl.pallas_call(
        body, name="qkv_bwd_" + kind,
        out_shape=(jax.ShapeDtypeStruct(dproj.shape, dproj.dtype), jax.ShapeDtypeStruct((GDN_CONV, GDN_WIDTH), F32)),
        grid=(HEADS // hb, n // rs),
        in_specs=[_seq_cols(rs, 0, hb), _seq_cols(rs, col0, hb), _tap_cols(GDN_CONV, col0, hb), _hbm()],
        out_specs=(_seq_cols(rs, col0, hb), _tap_cols(GDN_CONV, 0, hb)), input_output_aliases={3: 0},
        scratch_shapes=[_halo_scratch(rs), _halo_scratch(rs)],
        compiler_params=_params("parallel", "arbitrary"),
    )(dy, proj, conv_w, dproj)


SC_COL = 4 * HEADS


def _sc_fwd(proj, conv_w, rs, cat):
    n = proj.shape[0]

    hb = 2

    def body(x_ref, b_ref, c_ref, w_ref, _, y_ref, staged):
        for h in range(hb):
            lanes = _lanes_of(h)
            u = c_ref[:, lanes] * x_ref[:, lanes]
            _stage(staged, u)
            y_ref[:, lanes] = (b_ref[:, lanes] * _causal_conv(u, staged, w_ref[:, lanes], SC_CONV)).astype(y_ref.dtype)

    return pl.pallas_call(
        body, name="sc_fwd", out_shape=jax.ShapeDtypeStruct(cat.shape, cat.dtype), grid=(HEADS // hb, n // rs),
        in_specs=[_seq_cols(rs, SC_COL, hb), _seq_cols(rs, SC_COL + 4, hb), _seq_cols(rs, SC_COL + 8, hb),
                  _tap_cols(SC_CONV, 0, hb), _hbm()],
        out_specs=_seq_cols(rs, HEADS, hb), input_output_aliases={4: 0}, scratch_shapes=[_halo_scratch(rs)],
        compiler_params=_params("parallel", "parallel"),
    )(proj, proj, proj, conv_w, cat)


def _sc_bwd(dcat, proj, conv_w, rs, dproj):
    n = proj.shape[0]
    hb = 2

    def body(dy_ref, x_ref, b_ref, c_ref, w_ref, _, dx_ref, db_ref, dc_ref, dw_ref, u_staged, dcv_staged):
        for h in range(hb):
            lanes = _lanes_of(h)
            w = w_ref[:, lanes]
            x = x_ref[:, lanes]
            cc = c_ref[:, lanes]
            u = cc * x
            _stage(u_staged, u)
            dy = dy_ref[:, lanes]
            db_ref[:, lanes] = (dy * _causal_conv(u, u_staged, w, SC_CONV)).astype(db_ref.dtype)
            dcv = dy * b_ref[:, lanes]
            _stage(dcv_staged, dcv)
            du = _anti_causal_conv(dcv, dcv_staged, w, SC_CONV)
            dx_ref[:, lanes] = (du * cc).astype(dx_ref.dtype)
            dc_ref[:, lanes] = (du * x).astype(dc_ref.dtype)
            _accumulate(dw_ref.at[:, lanes], pl.program_id(1) == 0, _conv_weight_grad(dcv, u, u_staged, SC_CONV))

    piece = jax.ShapeDtypeStruct((n, SC_WIDTH), MXU_DTYPE)
    return pl.pallas_call(
        body, name="sc_bwd",
        out_shape=(jax.ShapeDtypeStruct(dproj.shape, dproj.dtype), piece, piece, jax.ShapeDtypeStruct((SC_CONV, SC_WIDTH), F32)),
        grid=(HEADS // hb, n // rs),
        in_specs=[_seq_cols(rs, HEADS, hb), _seq_cols(rs, SC_COL, hb), _seq_cols(rs, SC_COL + 4, hb),
                  _seq_cols(rs, SC_COL + 8, hb), _tap_cols(SC_CONV, 0, hb), _hbm()],
        out_specs=(_seq_cols(rs, SC_COL, hb), _seq_cols(rs, 0, hb), _seq_cols(rs, 0, hb), _tap_cols(SC_CONV, 0, hb)),
        input_output_aliases={5: 0},
        scratch_shapes=[_halo_scratch(rs), _halo_scratch(rs)],
        compiler_params=_params("parallel", "arbitrary"),
    )(dcat, proj, proj, proj, conv_w, dproj)


Z_COL = 3 * HEADS


def _gate_fwd(o, proj, gdn_norm, rs):
    n = proj.shape[0]

    hb = HEADS

    def body(o_ref, z_ref, w_ref, y_ref):
        for h in range(hb):
            lanes = _lanes_of(h)
            z = z_ref[:, lanes]
            y_ref[:, lanes] = (_rms_apply(o_ref[:, lanes], w_ref[...]) * z * _sigmoid(z)).astype(y_ref.dtype)

    return pl.pallas_call(
        body, name="gate_fwd", out_shape=jax.ShapeDtypeStruct((n, D_MODEL), MXU_DTYPE), grid=(HEADS // hb, n // rs),
        in_specs=[_seq_cols(rs, 0, hb), _seq_cols(rs, Z_COL, hb), pl.BlockSpec((1, LANES), lambda j, b: (0, 0))],
        out_specs=_seq_cols(rs, 0, hb), compiler_params=_params("parallel", "parallel"),
    )(o, proj, gdn_norm)


def _gate_bwd(dcat, o, proj, gdn_norm, rs):
    n = proj.shape[0]
    hb = 2

    def body(dy_ref, o_ref, z_ref, w_ref, do_ref, dz_ref, dw_ref):
        w = w_ref[...]
        dw_step = jnp.zeros((1, LANES), F32)
        for h in range(hb):
            lanes = _lanes_of(h)
            z = z_ref[:, lanes]
            o = o_ref[:, lanes]
            dy = dy_ref[:, lanes]
            s = _sigmoid(z)
            dz_ref[:, lanes] = (dy * _rms_apply(o, w) * _dsilu(z, s)).astype(dz_ref.dtype)
            do, dw = _rms_bwd(o, w, dy * z * s)
            do_ref[:, lanes] = do
            dw_step = dw_step + dw
        _accumulate(dw_ref, jnp.logical_and(pl.program_id(0) == 0, pl.program_id(1) == 0), dw_step)

    return pl.pallas_call(
        body, name="gate_bwd",
        out_shape=(jax.ShapeDtypeStruct((n, GDN_WIDTH), F32), jax.ShapeDtypeStruct((n, IN_PAD), MXU_DTYPE),
                   jax.ShapeDtypeStruct((1, LANES), F32)),
        grid=(HEADS // hb, n // rs),
        in_specs=[_seq_cols(rs, 0, hb), _seq_cols(rs, 0, hb), _seq_cols(rs, Z_COL, hb), pl.BlockSpec((1, LANES), lambda j, b: (0, 0))],
        out_specs=(_seq_cols(rs, 0, hb), _seq_cols(rs, Z_COL, hb), pl.BlockSpec((1, LANES), lambda j, b: (0, 0))),
        compiler_params=_params("arbitrary", "arbitrary"),
    )(dcat, o, proj, gdn_norm)


def _dot(a, b):
    return jnp.dot(a.astype(MXU_DTYPE), b.astype(MXU_DTYPE), preferred_element_type=F32)


def _dot_nt(a, b):
    return lax.dot_general(a.astype(MXU_DTYPE), b.astype(MXU_DTYPE), (((1,), (1,)), ((), ())),
                           preferred_element_type=F32)


def _dot_tn(a, b):
    return lax.dot_general(a.astype(MXU_DTYPE), b.astype(MXU_DTYPE), (((0,), (0,)), ((), ())),
                           preferred_element_type=F32)


def _split(x):
    hi = x.astype(MXU_DTYPE)
    return hi, (x - hi.astype(F32)).astype(MXU_DTYPE)


def _dot_split(a, b):
    mm = functools.partial(jnp.dot, preferred_element_type=F32)
    return mm(a[0], b[0]) + (mm(a[0], b[1]) + mm(a[1], b[0]))


def _unit_lower_inverses(mats, eye):
    inv = [eye - a for a in mats]
    power = [_split(a) for a in mats]
    square = [_dot_split(p, p) for p in power]
    inv = [i + _dot_split(_split(i), _split(s)) for i, s in zip(inv, square)]
    span = 4
    while span < CHUNK:
        square = [_dot(s, s) for s in square]
        inv = [i + _dot(i, s) for i, s in zip(inv, square)]
        span *= 2
    return inv


def _chunk_masks():
    ii = lax.broadcasted_iota(jnp.int32, (CHUNK, CHUNK), 0)
    jj = lax.broadcasted_iota(jnp.int32, (CHUNK, CHUNK), 1)
    return ii, jj


def _chunk_decay(g_col, ii, jj):
    incl = ii >= jj
    g_row = jnp.sum(jnp.where(ii == jj, g_col, 0.0), axis=0, keepdims=True)
    gc_col = jnp.sum(jnp.where(incl, g_row, 0.0), axis=1, keepdims=True)
    gc_row = jnp.sum(jnp.where(ii <= jj, g_col, 0.0), axis=0, keepdims=True)
    g_total = jnp.sum(g_row, axis=1, keepdims=True)
    decay = jnp.where(incl, jnp.exp(jnp.where(incl, gc_col - gc_row, 0.0)), 0.0)
    return gc_col, g_total, decay


def _gdn_segments(rs, candidates):
    chunks = rs // CHUNK
    seg_chunks = _pick(chunks, candidates)
    return chunks, seg_chunks, chunks // seg_chunks


def _head_lanes(h):
    return slice(h * HEAD_DIM, (h + 1) * HEAD_DIM)


def _gdn_fwd(q, k, v, bg, rs, pieces):
    n = q.shape[0]
    batch = n // rs
    chunks, seg_chunks, segs = _gdn_segments(rs, (11, 8, 4, 2))
    seg_rows = seg_chunks * CHUNK
    chains = [(b, h) for b in range(batch) for h in range(HEADS)]
    each = lambda f, *lists: [f(*args) for args in zip(*lists)]
    count = len(pieces)

    def body(q_ref, k_ref, v_ref, bg_ref, *rest):
        w_refs, (o_ref, s_ref, t_ref), out_refs = rest[:count], rest[count:count + 3], rest[count + 3:2 * count + 3]
        state_ref, send_sems, recv_sems = rest[2 * count + 3:]
        gather = _gather_copies(w_refs, out_refs, send_sems, recv_sems)

        @pl.when(pl.program_id(0) == 0)
        def _():
            state_ref[...] = jnp.zeros_like(state_ref)
            for cp in gather[0]:
                cp.start()

        ii, jj = _chunk_masks()
        incl = ii >= jj
        eye = (ii == jj).astype(F32)

        def chunk(c, carry):
            rows = pl.ds(pl.multiple_of(c * CHUNK, CHUNK), CHUNK)
            bgc = [bg_ref[b, rows, :] for b in range(batch)]
            qc = [q_ref[b, rows, _head_lanes(h)] for b, h in chains]
            kc = [k_ref[b, rows, _head_lanes(h)] for b, h in chains]
            vc = [v_ref[b, rows, _head_lanes(h)] for b, h in chains]
            beta = [bgc[b][:, h:h + 1] for b, h in chains]
            state = [state_ref[b, h] for b, h in chains]
            dec = [_chunk_decay(bgc[b][:, HEADS + h:HEADS + h + 1], ii, jj) for b, h in chains]
            gc_col, g_total, decay = ([d[i] for d in dec] for i in range(3))
            kb = each(lambda x, y: x * y, kc, beta)
            a = each(lambda x, y, d: jnp.where(ii > jj, _dot_nt(x, y) * d, 0.0), kb, kc, decay)
            t_inv = _unit_lower_inverses(a, eye)
            eg = [jnp.exp(g) for g in gc_col]
            u = each(lambda t, x, y: _dot(t, x * y), t_inv, vc, beta)
            w = each(lambda t, x, e: _dot(t, x * e), t_inv, kb, eg)
            qk = each(lambda x, y, d: jnp.where(incl, _dot_nt(x, y) * d, 0.0), qc, kc, decay)
            v_new = each(lambda x, y, s: x - _dot(y, s), u, w, state)
            o = each(lambda x, e, s, m, vn: _dot(x * e, s) + _dot(m, vn), qc, eg, state, qk, v_new)
            new_state = each(lambda s, gt, x, g, vn: s * jnp.exp(gt) + _dot_tn(x * jnp.exp(gt - g), vn),
                             state, g_total, kc, gc_col, v_new)
            for i, (b, h) in enumerate(chains):
                s_ref[b, h, c] = state[i]
                t_ref[b, h, c] = t_inv[i]
                o_ref[b, rows, _head_lanes(h)] = o[i]
                state_ref[b, h] = new_state[i]
            return carry

        lax.fori_loop(0, seg_chunks, chunk, 0)

        @pl.when(pl.program_id(0) == segs - 1)
        def _():
            _gather_finish(gather)

    rows_spec = lambda width: pl.BlockSpec((batch, seg_rows, width), lambda s: (0, s, 0))
    per_chunk = lambda r, c: pl.BlockSpec((batch, HEADS, seg_chunks, r, c), lambda s: (0, 0, s, 0, 0))
    as_seqs = lambda a: a.reshape(batch, rs, a.shape[-1])
    sems = GATHER_SEMS * count
    o, states, t_invs, *gathered = pl.pallas_call(
        body, name="gdn_fwd",
        out_shape=(jax.ShapeDtypeStruct((batch, rs, GDN_WIDTH), F32),
                   jax.ShapeDtypeStruct((batch, HEADS, chunks, HEAD_DIM, HEAD_DIM), F32),
                   jax.ShapeDtypeStruct((batch, HEADS, chunks, CHUNK, CHUNK), F32))
        + tuple(jax.ShapeDtypeStruct((N_CHIPS,) + p.shape, p.dtype) for p in pieces),
        grid=(segs,),
        in_specs=[rows_spec(GDN_WIDTH), rows_spec(GDN_WIDTH), rows_spec(GDN_WIDTH), rows_spec(LANES)] + [_hbm()] * count,
        out_specs=(rows_spec(GDN_WIDTH), per_chunk(HEAD_DIM, HEAD_DIM), per_chunk(CHUNK, CHUNK)) + (_hbm(),) * count,
        scratch_shapes=[pltpu.VMEM((batch, HEADS, HEAD_DIM, HEAD_DIM), F32), pltpu.SemaphoreType.DMA((sems,)),
                        pltpu.SemaphoreType.DMA((sems,))],
        compiler_params=_params("arbitrary"),
    )(as_seqs(q), as_seqs(k), as_seqs(v), as_seqs(bg), *pieces)
    return o.reshape(n, GDN_WIDTH), states, t_invs, gathered


def _gdn_bwd(do, q, k, v, bg, states, t_invs, rs, parts):
    n = q.shape[0]
    batch = n // rs
    chunks, seg_chunks, segs = _gdn_segments(rs, (3, 4, 2))
    seg_rows = seg_chunks * CHUNK
    chains = [(b, h) for b in range(batch) for h in range(HEADS)]
    each = lambda f, *lists: [f(*args) for args in zip(*lists)]
    count = len(parts)

    def body(do_ref, q_ref, k_ref, v_ref, bg_ref, s_ref, t_ref, *rest):
        p_refs, (dq_ref, dk_ref, dv_ref, dbg_ref), got_refs = rest[:count], rest[count:count + 4], rest[count + 4:2 * count + 4]
        dstate_ref, send_sems, recv_sems = rest[2 * count + 4:]
        exchange = _chip_copies(p_refs, got_refs, send_sems, recv_sems)

        @pl.when(pl.program_id(0) == 0)
        def _():
            dstate_ref[...] = jnp.zeros_like(dstate_ref)
            for cp in exchange:
                cp.start()

        ii, jj = _chunk_masks()
        incl = ii >= jj
        strict = ii > jj
        lane = lax.broadcasted_iota(jnp.int32, (1, LANES), 1)

        def rowsum(x):
            return jnp.sum(x, axis=1, keepdims=True)

        def total(x):
            return jnp.sum(rowsum(x), axis=0, keepdims=True)

        def chunk(step, carry):
            c = seg_chunks - 1 - step
            rows = pl.ds(pl.multiple_of(c * CHUNK, CHUNK), CHUNK)
            bgc = [bg_ref[b, rows, :] for b in range(batch)]
            qc = [q_ref[b, rows, _head_lanes(h)] for b, h in chains]
            kc = [k_ref[b, rows, _head_lanes(h)] for b, h in chains]
            vc = [v_ref[b, rows, _head_lanes(h)] for b, h in chains]
            doc = [do_ref[b, rows, _head_lanes(h)] for b, h in chains]
            beta = [bgc[b][:, h:h + 1] for b, h in chains]
            state = [s_ref[b, h, c] for b, h in chains]
            t_inv = [t_ref[b, h, c] for b, h in chains]
            d_state = [dstate_ref[b, h] for b, h in chains]
            dec = [_chunk_decay(bgc[b][:, HEADS + h:HEADS + h + 1], ii, jj) for b, h in chains]
            gc_col, g_total, decay = ([d[i] for d in dec] for i in range(3))
            kb = each(lambda x, y: x * y, kc, beta)
            vb = each(lambda x, y: x * y, vc, beta)
            eg = [jnp.exp(g) for g in gc_col]
            kbg = each(lambda x, y: x * y, kb, eg)
            a = each(lambda x, y, d: jnp.where(strict, _dot_nt(x, y) * d, 0.0), kb, kc, decay)
            qk = each(lambda x, y, d: jnp.where(incl, _dot_nt(x, y) * d, 0.0), qc, kc, decay)
            w = each(_dot, t_inv, kbg)
            u = each(_dot, t_inv, vb)
            q_dec = each(lambda x, y: x * y, qc, eg)
            ek = each(lambda gt, g: jnp.exp(gt - g), g_total, gc_col)
            k_dec = each(lambda x, y: x * y, kc, ek)
            g_last = [jnp.exp(gt) for gt in g_total]
            v_new = each(lambda x, y, s: x - _dot(y, s), u, w, state)
            dv_new = each(lambda m, d, x, ds: _dot_tn(m, d) + _dot(x, ds), qk, doc, k_dec, d_state)
            dqk = each(lambda d, vn: jnp.where(incl, _dot_nt(d, vn), 0.0), doc, v_new)
            dq_dec = each(_dot_nt, doc, state)
            dk_dec = each(_dot_nt, v_new, d_state)
            dg_last = each(lambda s, ds: total(s * ds), state, d_state)
            new_d_state = each(lambda x, d, gl, ds, y, dvn: _dot_tn(x, d) + gl * ds - _dot_tn(y, dvn),
                               q_dec, doc, g_last, d_state, w, dv_new)
            dw = each(lambda dvn, s: -_dot_nt(dvn, s), dv_new, state)
            dt = each(lambda dvn, x, y, z: _dot_nt(dvn, x) + _dot_nt(y, z), dv_new, vb, dw, kbg)
            dvb = each(_dot_tn, t_inv, dv_new)
            dkbg = each(_dot_tn, t_inv, dw)
            t_dt = each(_dot_tn, t_inv, dt)
            da = each(lambda x, t: -jnp.where(strict, _dot_nt(x, t), 0.0), t_dt, t_inv)
            dm_a = each(lambda x, y: x * y, da, decay)
            dm_qk = each(lambda x, y: x * y, dqk, decay)
            e = each(lambda x, y, z, t: x * y + z * t, da, a, dqk, qk)
            dkb = each(lambda m, x, y, z: _dot(m, x) + y * z, dm_a, kc, dkbg, eg)
            dk = each(lambda m, x, m2, y, z, t, p, bt: _dot_tn(m, x) + _dot_tn(m2, y) + z * t + p * bt,
                      dm_a, kb, dm_qk, qc, dk_dec, ek, dkb, beta)
            dq = each(lambda m, x, y, z: _dot(m, x) + y * z, dm_qk, kc, dq_dec, eg)
            dbeta = each(lambda x, y, z, t: rowsum(x * y + z * t), dkb, kc, dvb, vc)
            dgc = each(lambda x, p, pd, r, rd, s, sd: rowsum(x) - rowsum(jnp.where(ii == jj, jnp.sum(x, axis=0, keepdims=True), 0.0))
                       + rowsum(p * pd - r * rd + s * sd), e, dq_dec, q_dec, dk_dec, k_dec, dkbg, kbg)
            d_total = each(lambda r, rd, x, gl: total(r * rd) + x * gl, dk_dec, k_dec, dg_last, g_last)
            dg = each(lambda x, t: rowsum(jnp.where(jj >= ii, jnp.sum(jnp.where(ii == jj, x, 0.0), axis=0, keepdims=True), 0.0)) + t,
                      dgc, d_total)
            dbg = [jnp.zeros((CHUNK, LANES), F32) for _ in range(batch)]
            for i, (b, h) in enumerate(chains):
                dstate_ref[b, h] = new_d_state[i]
                dk_ref[b, rows, _head_lanes(h)] = dk[i]
                dq_ref[b, rows, _head_lanes(h)] = dq[i]
                dv_ref[b, rows, _head_lanes(h)] = dvb[i] * beta[i]
                dbg[b] = dbg[b] + jnp.where(lane == h, dbeta[i], 0.0) + jnp.where(lane == HEADS + h, dg[i], 0.0)
            for b in range(batch):
                dbg_ref[b, rows, :] = dbg[b]
            return carry

        lax.fori_loop(0, seg_chunks, chunk, 0)

        @pl.when(pl.program_id(0) == segs - 1)
        def _():
            for cp in exchange:
                cp.wait_recv()
            for cp in exchange:
                cp.wait_send()

    rows_spec = lambda width: pl.BlockSpec((batch, seg_rows, width), lambda s: (0, segs - 1 - s, 0))
    per_chunk = lambda r, c: pl.BlockSpec((batch, HEADS, seg_chunks, r, c), lambda s: (0, 0, segs - 1 - s, 0, 0))
    as_seqs = lambda a: a.reshape(batch, rs, a.shape[-1])
    grad = jax.ShapeDtypeStruct((batch, rs, GDN_WIDTH), F32)
    wide = rows_spec(GDN_WIDTH)
    dq, dk, dv, dbg, *got = pl.pallas_call(
        body, name="gdn_bwd",
        out_shape=(grad, grad, grad, jax.ShapeDtypeStruct((batch, rs, LANES), F32))
        + tuple(jax.ShapeDtypeStruct((3,) + p.shape[1:], p.dtype) for p in parts),
        grid=(segs,),
        in_specs=[wide, wide, wide, wide, rows_spec(LANES), per_chunk(HEAD_DIM, HEAD_DIM), per_chunk(CHUNK, CHUNK)]
        + [_hbm()] * count,
        out_specs=(wide, wide, wide, rows_spec(LANES)) + (_hbm(),) * count,
        scratch_shapes=[pltpu.VMEM((batch, HEADS, HEAD_DIM, HEAD_DIM), F32), pltpu.SemaphoreType.DMA((3 * count,)),
                        pltpu.SemaphoreType.DMA((3 * count,))],
        compiler_params=_params("arbitrary"),
    )(as_seqs(do), as_seqs(q), as_seqs(k), as_seqs(v), as_seqs(bg), states, t_invs, *parts)
    return dq.reshape(n, GDN_WIDTH), dk.reshape(n, GDN_WIDTH), dv.reshape(n, GDN_WIDTH), dbg.reshape(n, LANES), got


def _lane_vec(vals, offset):
    k = vals.shape[1]
    return jnp.pad(vals, ((0, 0), (offset, LANES - offset - k)))


LATER = ("w_out", "w_gate", "w_up", "w_down")


def _halves(a):
    return a.reshape(a.shape[:-2] + (2, a.shape[-2] // 2, a.shape[-1]))


def _local_step(x, target, meta, norms, w_in_t, conv_qkv, a_log, dt_bias, gdn_norm, conv_sc, later_shards, core_arg):
    batch, seq, d = x.shape
    tokens = N_META + seq
    pad_rows = (-tokens) % CHUNK
    rs = tokens + pad_rows
    x_offset = pad_rows + N_META
    n = batch * rs
    w_mix_pre, w_mix_post, w_ffn_pre, w_ffn_post = norms

    head = jnp.concatenate([jnp.zeros((pad_rows, d), F32), meta], axis=0)
    h0 = jnp.concatenate([jnp.broadcast_to(head[None], (batch, x_offset, d)), x], axis=1).reshape(n, d)
    a_log_l = _lane_vec(a_log, HEADS)
    dt_bias_l = _lane_vec(dt_bias, HEADS)

    u1 = _rms_fwd(h0, w_mix_pre, "rms_mix_pre")
    proj = _mm(u1, w_in_t, "nt", F32, "mm_proj")
    q = _qkv_fwd(proj, conv_qkv, "q", rs)
    k = _qkv_fwd(proj, conv_qkv, "k", rs)
    v = _qkv_fwd(proj, conv_qkv, "v", rs)
    bg = _gates_fwd(proj, a_log_l, dt_bias_l, rs, pad_rows)
    o, states, t_invs, gathered = _gdn_fwd(q, k, v, bg, rs, later_shards[:3])
    w_out, w_gate_t, w_up_t = (a.reshape(-1, d) for a in gathered)
    cat = _sc_fwd(proj, conv_sc, rs, _gate_fwd(o, proj, gdn_norm, rs))
    mix = _mm(cat, w_out, "nn", F32, "mm_mix")
    h1, u2 = _mix_residual(h0, mix, w_mix_post, w_ffn_pre)
    gate, up, act, w_down = _swiglu_fwd(u2, w_gate_t, w_up_t, later_shards[3])
    w_down = w_down.reshape(-1, d)
    ffn = _mm(act, w_down, "nn", F32, "mm_down")

    dh2, dffn, d_ffn_post, sq = _loss_head(h1, ffn, w_ffn_post, target, rs, x_offset)
    d_w_down = _mm(act, dffn, "tn", F32, "mm_dw_down")
    dgate, dup = _swiglu_bwd(dffn, w_down, gate, up)
    d_w_gate_t = _mm(dgate, u2, "tn", F32, "mm_dw_gate")
    d_w_up_t = _mm(dup, u2, "tn", F32, "mm_dw_up")
    du2 = _mm(dup, w_up_t, "nn", F32, "mm_du2_up", init=_mm(dgate, w_gate_t, "nn", F32, "mm_du2_gate"))
    by_chip = [_halves(g.reshape(N_CHIPS, -1, d)) for g in (d_w_gate_t, d_w_up_t, d_w_down)]
    dh1, dmix, d_ffn_pre, d_mix_post, got_sibling = _mid_bwd(h1, mix, w_mix_post, w_ffn_pre, dh2, du2, by_chip)
    dcat = _mm(dmix, w_out, "nt", F32, "mm_dcat")
    d_w_out = _halves(_mm(cat, dmix, "tn", F32, "mm_dw_out").reshape(N_CHIPS, -1, d))
    by_chip, got_sibling = [d_w_out] + by_chip, list(_exchange_siblings([d_w_out])) + got_sibling
    sums = [_add_sibling(a, b, core_arg, name) for name, a, b in zip(LATER, by_chip, got_sibling)]
    do, dproj, d_gdn_norm = _gate_bwd(dcat, o, proj, gdn_norm, rs)
    dproj, dscb, dscc, d_conv_sc = _sc_bwd(dcat, proj, conv_sc, rs, dproj)
    dq, dk, dv, dbg, got_chips = _gdn_bwd(do, q, k, v, bg, states, t_invs, rs, [send for _, send in sums[:3]])
    dproj, dwq = _qkv_bwd(dq, proj, conv_qkv, "q", rs, dproj)
    dproj, dwk = _qkv_bwd(dk, proj, conv_qkv, "k", rs, dproj)
    dproj, dwv = _qkv_bwd(dv, proj, conv_qkv, "v", rs, dproj)
    d_conv_qkv = jnp.concatenate([dwq, dwk, dwv], axis=1)
    dproj, d_a_log_l, d_dt_bias_l = _gates_bwd(proj, dbg, a_log_l, dt_bias_l, rs, pad_rows, dproj)
    dproj = lax.dynamic_update_slice(dproj, dscb, (0, (SC_COL + HEADS) * LANES))
    dproj = lax.dynamic_update_slice(dproj, dscc, (0, (SC_COL + 2 * HEADS) * LANES))
    d_w_in_t, got_down = _mm(dproj, u1, "tn", F32, "mm_dw_in", exchange=[sums[3][1]])
    got_chips.append(got_down)
    g_in = _halves(_in_from_kernel_order(d_w_in_t))
    sums.insert(0, _add_sibling(g_in, _exchange_siblings([g_in])[0], core_arg, "w_in"))
    du1, got_in = _mm(dproj, w_in_t, "nn", F32, "mm_du1", exchange=[sums[0][1]])
    got_chips.insert(0, got_in)
    grad_x, d_meta, d_mix_pre = _in_bwd(h0, w_mix_pre, dh1, du1, rs, pad_rows, x_offset)

    grads = dict(
        meta_tokens=d_meta,
        mix_pre_norm=d_mix_pre, mix_post_norm=d_mix_post, ffn_pre_norm=d_ffn_pre, ffn_post_norm=d_ffn_post,
        conv_qkv=d_conv_qkv,
        a_log=d_a_log_l[:, HEADS:2 * HEADS], dt_bias=d_dt_bias_l[:, HEADS:2 * HEADS],
        gdn_norm=d_gdn_norm, conv_sc=d_conv_sc,
    )
    return sq, grad_x, grads, [(part, got) for (part, _), got in zip(sums, got_chips)]


MATRICES = ("w_in", "w_out", "w_gate", "w_up", "w_down")
IN_SHARD = IN_WIDTH // N_CHIPS
IN_SHARD_PAD = 928


IN_SEGMENTS = ((0, 0, 4 * GDN_WIDTH), (4 * GDN_WIDTH, IN_WIDTH - 2 * HEADS, 2 * HEADS),
               (4 * GDN_WIDTH + 2 * HEADS, 4 * GDN_WIDTH, 3 * SC_WIDTH))
SUBLANES = 8
PACKED_ROWS = 16


def _in_to_kernel_order(by_chip):
    d = by_chip.shape[-1]
    tl = _pick(d, (256, 128))
    runs = []
    for ref0, ker0, count in IN_SEGMENTS:
        row = ref0
        while row < ref0 + count:
            chip, at = divmod(row, IN_SHARD)
            take = min(ref0 + count - row, IN_SHARD - at)
            runs.append((ker0 + row - ref0, take, chip * IN_SHARD_PAD + at))
            row += take

    def body(w_ref, o_ref):
        o_ref[...] = jnp.zeros_like(o_ref)
        for out0, rows, src0 in runs:
            a0 = out0 // PACKED_ROWS * PACKED_ROWS
            a1 = -(-(out0 + rows) // PACKED_ROWS) * PACKED_ROWS
            window = w_ref[pl.ds(src0 - (out0 - a0), a1 - a0), :]
            row = a0 + lax.broadcasted_iota(jnp.int32, (a1 - a0, 1), 0)
            keep = jnp.logical_and(row >= out0, row < out0 + rows)
            o_ref[a0:a1, :] = jnp.where(keep, window, o_ref[a0:a1, :])

    return pl.pallas_call(
        body, name="in_to_kernel_order", out_shape=jax.ShapeDtypeStruct((IN_PAD, d), by_chip.dtype), grid=(d // tl,),
        in_specs=[pl.BlockSpec((N_CHIPS * IN_SHARD_PAD, tl), lambda j: (0, j))],
        out_specs=pl.BlockSpec((IN_PAD, tl), lambda j: (0, j)),
        compiler_params=_params("parallel"),
    )(by_chip.reshape(N_CHIPS * IN_SHARD_PAD, d))


def _in_from_kernel_order(g_t):
    d = g_t.shape[-1]
    tl = _pick(d, (256, 128))

    def body(g_ref, o_ref):
        row = lax.broadcasted_iota(jnp.int32, (IN_SHARD_PAD, 1), 0)
        for chip in range(N_CHIPS):
            first = chip * IN_SHARD
            runs = []
            for ref0, ker0, count in IN_SEGMENTS:
                lo, hi = max(ref0, first), min(ref0 + count, first + IN_SHARD)
                if lo < hi:
                    runs.append((lo - first, hi - lo, ker0 + lo - ref0))
            val = jnp.zeros((IN_SHARD_PAD, tl), F32)
            patches = []
            for out0, rows, src0 in runs:
                start = src0 - out0
                if 0 <= start <= IN_PAD - IN_SHARD_PAD:
                    window = g_ref[pl.ds(start, IN_SHARD_PAD), :]
                    val = jnp.where(jnp.logical_and(row >= out0, row < out0 + rows), window, val)
                else:
                    patches.append((out0, rows, src0))
            o_ref[chip] = val
            for out0, rows, src0 in patches:
                a0 = out0 // SUBLANES * SUBLANES
                a1 = -(-(out0 + rows) // SUBLANES) * SUBLANES
                window = g_ref[pl.ds(src0 - (out0 - a0), a1 - a0), :]
                keep = jnp.logical_and(row[a0:a1] >= out0, row[a0:a1] < out0 + rows)
                o_ref[chip, a0:a1, :] = jnp.where(keep, window, o_ref[chip, a0:a1, :])

    return pl.pallas_call(
        body, name="in_from_kernel_order", out_shape=jax.ShapeDtypeStruct((N_CHIPS, IN_SHARD_PAD, d), F32), grid=(d // tl,),
        in_specs=[pl.BlockSpec((IN_PAD, tl), lambda j: (0, j))],
        out_specs=pl.BlockSpec((N_CHIPS, IN_SHARD_PAD, tl), lambda j: (0, 0, j)),
        compiler_params=_params("parallel"),
    )(g_t)


PACK_LANES = 3 * GDN_WIDTH
PACKED = dict(mix_pre_norm=(0, 1, 0, D_MODEL), mix_post_norm=(1, 1, 0, D_MODEL), ffn_pre_norm=(2, 1, 0, D_MODEL),
              ffn_post_norm=(3, 1, 0, D_MODEL), a_log=(4, 1, 0, HEADS), dt_bias=(5, 1, 0, HEADS), loss=(6, 1, 0, 1),
              gdn_norm=(7, 1, 0, HEAD_DIM), conv_qkv=(8, GDN_CONV, 0, 3 * GDN_WIDTH), conv_sc=(0, SC_CONV, D_MODEL, SC_WIDTH),
              meta_tokens=(16, N_META, 0, D_MODEL))
PACK_ROWS = 32
SHARDED_SMALL = ("conv_qkv", "conv_sc", "meta_tokens")


def _pack_small(values):
    names = list(PACKED)

    def body(*refs):
        out_ref = refs[-1]
        out_ref[...] = jnp.zeros_like(out_ref)
        for name, ref in zip(names, refs):
            row, rows, lane0, lanes = PACKED[name]
            out_ref[row:row + rows, lane0:lane0 + lanes] = ref[...]

    return pl.pallas_call(body, name="pack_small", out_shape=jax.ShapeDtypeStruct((PACK_ROWS, PACK_LANES), F32))(
        *[values[name] for name in names])


def _sum_devices(packed_all, chip):
    names = list(PACKED)

    def body(chip_ref, all_ref, *rest):
        shard_refs, out_refs = rest[:len(SHARDED_SMALL)], rest[len(SHARDED_SMALL):]

        def total(ref, rows, lanes):
            acc = ref[0, rows, lanes]
            for k in range(1, 8):
                acc = acc + ref[k, rows, lanes]
            return acc

        for name, out in zip(names, out_refs):
            row, rows, lane0, lanes = PACKED[name]
            if name in SHARDED_SMALL:
                out[...] = total(shard_refs[SHARDED_SMALL.index(name)], slice(0, rows), slice(None))
            else:
                out[...] = total(all_ref, slice(row, row + rows), slice(lane0, lane0 + lanes))

    def shard_spec(name):
        row, rows, lane0, lanes = PACKED[name]
        height, width = max(rows, 8), lanes // N_CHIPS
        assert row % height == 0 and lane0 % width == 0
        return pl.BlockSpec((8, height, width), lambda i, chip_ref: (0, row // height, lane0 // width + chip_ref[0]))

    def out_shape(name):
        _, rows, _, lanes = PACKED[name]
        return jax.ShapeDtypeStruct((rows, lanes // N_CHIPS if name in SHARDED_SMALL else lanes), F32)

    whole = lambda shape: pl.BlockSpec(shape, lambda i, chip_ref: (0,) * len(shape))
    outs = pl.pallas_call(
        body, name="sum_devices", out_shape=tuple(out_shape(n) for n in names),
        grid_spec=pltpu.PrefetchScalarGridSpec(
            num_scalar_prefetch=1, grid=(1,),
            in_specs=[whole(packed_all.shape)] + [shard_spec(n) for n in SHARDED_SMALL],
            out_specs=tuple(whole(out_shape(n).shape) for n in names)),
    )(chip, packed_all, *[packed_all] * len(SHARDED_SMALL))
    return dict(zip(names, outs))


def _hbm():
    return pl.BlockSpec(memory_space=pl.ANY)


def _place():
    x, y, c = lax.axis_index("x"), lax.axis_index("y"), lax.axis_index("c")
    chips = ((1 - x, y), (x, 1 - y), (1 - x, 1 - y))
    return x, y, c, chips


def _remote(src, dst, send_sems, recv_sems, k, to):
    return pltpu.make_async_remote_copy(src_ref=src, dst_ref=dst, send_sem=send_sems.at[k], recv_sem=recv_sems.at[k],
                                        device_id=to, device_id_type=MESH)


GATHER_SEMS = 7


def _gather_copies(w_refs, out_refs, send_sems, recv_sems):
    x, y, c, chips = _place()
    mine = 2 * x + y
    sibling = (x, y, 1 - c)
    copy = functools.partial(_remote, send_sems=send_sems, recv_sems=recv_sems)
    direct, landed, passing, from_sibling = [], [], [], []
    for i, (w, o) in enumerate(zip(w_refs, out_refs)):
        k = GATHER_SEMS * i
        direct.append(copy(w, o.at[mine], k=k, to=sibling))
        from_sibling.append(copy(w, o.at[mine], k=k, to=sibling))
        for j, (cx, cy) in enumerate(chips):
            theirs = 2 * cx + cy
            direct.append(copy(w.at[c], o.at[mine, c], k=k + 1 + j, to=(cx, cy, c)))
            landed.append(copy(w.at[c], o.at[theirs, c], k=k + 1 + j, to=sibling))
            passing.append(copy(o.at[theirs, c], o.at[theirs, c], k=k + 4 + j, to=sibling))
            from_sibling.append(copy(w.at[c], o.at[theirs, 1 - c], k=k + 4 + j, to=sibling))
    return direct, landed, passing, from_sibling


def _gather_finish(copies):
    direct, landed, passing, from_sibling = copies
    for arrival, forward in zip(landed, passing):
        arrival.wait_recv()
        forward.start()
    for arrival in from_sibling:
        arrival.wait_recv()
    for cp in direct + passing:
        cp.wait_send()


def _gather_weights(pieces, smalls):
    count, extra = len(pieces), len(smalls)
    total = count + extra

    def body(*refs):
        w_refs, s_refs = refs[:count], refs[count:total]
        out_refs, sall_refs = refs[total:total + count], refs[total + count:2 * total]
        send_sems, recv_sems, local_sems = refs[2 * total:]
        x, y, c, chips = _place()
        mine = 2 * x + y
        own = [pltpu.make_async_copy(s, sall.at[mine], local_sems.at[i]) for i, (s, sall) in enumerate(zip(s_refs, sall_refs))]
        small = [_remote(s, sall.at[mine], send_sems, recv_sems, GATHER_SEMS * count + 3 * i + j, (cx, cy, c))
                 for i, (s, sall) in enumerate(zip(s_refs, sall_refs)) for j, (cx, cy) in enumerate(chips)]
        copies = _gather_copies(w_refs, out_refs, send_sems, recv_sems)
        for cp in own + small + copies[0]:
            cp.start()
        _gather_finish(copies)
        for cp in small:
            cp.wait_recv()
        for cp in small:
            cp.wait_send()
        for cp in own:
            cp.wait()

    sems = GATHER_SEMS * count + 3 * extra
    return pl.pallas_call(
        body, name="gather_weights",
        out_shape=tuple(jax.ShapeDtypeStruct((N_CHIPS,) + p.shape, p.dtype) for p in list(pieces) + list(smalls)),
        in_specs=[_hbm()] * total, out_specs=(_hbm(),) * total,
        scratch_shapes=[pltpu.SemaphoreType.DMA((sems,)), pltpu.SemaphoreType.DMA((sems,)), pltpu.SemaphoreType.DMA((extra,))],
    )(*pieces, *smalls)


def _sibling_copies(g_refs, got_refs, send_sems, recv_sems):
    x, y, c, _ = _place()
    return [_remote(g.at[:, 1 - c], got, send_sems, recv_sems, i, (x, y, 1 - c)) for i, (g, got) in enumerate(zip(g_refs, got_refs))]


def _exchange_siblings(grads):
    count = len(grads)

    def body(*refs):
        copies = _sibling_copies(refs[:count], refs[count:2 * count], *refs[2 * count:])
        for cp in copies:
            cp.start()
        for cp in copies:
            cp.wait_recv()
        for cp in copies:
            cp.wait_send()

    return pl.pallas_call(
        body, name="exchange_siblings",
        out_shape=tuple(jax.ShapeDtypeStruct((g.shape[0],) + g.shape[2:], F32) for g in grads),
        in_specs=[_hbm()] * count, out_specs=(_hbm(),) * count,
        scratch_shapes=[pltpu.SemaphoreType.DMA((count,)), pltpu.SemaphoreType.DMA((count,))],
    )(*grads)


def _chip_copies(p_refs, got_refs, send_sems, recv_sems):
    x, y, c, chips = _place()
    return [_remote(p.at[2 * cx + cy], got.at[j], send_sems, recv_sems, 3 * i + j, (cx, cy, c))
            for i, (p, got) in enumerate(zip(p_refs, got_refs)) for j, (cx, cy) in enumerate(chips)]


def _share_halves(halves, small):
    count = len(halves)

    def body(*refs):
        h_refs, s_ref = refs[:count], refs[count]
        full_refs, sall_ref = refs[count + 1:2 * count + 1], refs[2 * count + 1]
        send_sems, recv_sems, local_sem = refs[2 * count + 2:]
        x, y, c, _ = _place()
        me = 4 * x + 2 * y + c
        own = pltpu.make_async_copy(s_ref, sall_ref.at[me], local_sem)
        own.start()
        copies = [_remote(h.at[c], full.at[c], send_sems, recv_sems, i, (x, y, 1 - c))
                  for i, (h, full) in enumerate(zip(h_refs, full_refs))]
        for k in range(7):
            dx, dy, dc = ((k + 1) >> 2) & 1, ((k + 1) >> 1) & 1, (k + 1) & 1
            peer = (1 - x if dx else x, 1 - y if dy else y, 1 - c if dc else c)
            copies.append(_remote(s_ref, sall_ref.at[me], send_sems, recv_sems, count + k, peer))
        for cp in copies:
            cp.start()
        for cp in copies:
            cp.wait_recv()
        for cp in copies:
            cp.wait_send()
        own.wait()

    return pl.pallas_call(
        body, name="share_halves",
        out_shape=tuple(jax.ShapeDtypeStruct(h.shape, h.dtype) for h in halves) + (jax.ShapeDtypeStruct((8,) + small.shape, F32),),
        in_specs=[_hbm()] * (count + 1), out_specs=(_hbm(),) * (count + 1), input_output_aliases={i: i for i in range(count)},
        scratch_shapes=[pltpu.SemaphoreType.DMA((count + 7,)), pltpu.SemaphoreType.DMA((count + 7,)), pltpu.SemaphoreType.DMA],
    )(*halves, small)


def _add_sibling(grad, got, core, name):
    chips, _, rows, cols = grad.shape

    def body(core_ref, g_ref, r_ref, sum_ref, send_ref):
        s = g_ref[...] + r_ref[...]
        sum_ref[...] = s
        send_ref[...] = s.astype(send_ref.dtype)

    block = pl.BlockSpec((None, rows, cols), lambda p, core_ref: (p, 0, 0))
    return pl.pallas_call(
        body, name="add_sibling_" + name,
        out_shape=(jax.ShapeDtypeStruct((chips, rows, cols), F32), jax.ShapeDtypeStruct((chips, rows, cols), BF16)),
        grid_spec=pltpu.PrefetchScalarGridSpec(
            num_scalar_prefetch=1, grid=(chips,),
            in_specs=[pl.BlockSpec((None, None, rows, cols), lambda p, core_ref: (p, core_ref[0], 0, 0)), block],
            out_specs=(block, block)),
        compiler_params=_params("parallel"),
    )(core, grad, got)


def _add_chips(part, got, chip_core, name):
    _, rows, cols = part.shape
    tr = rows // 2 if rows % 32 == 0 else rows

    def body(place_ref, p_ref, r_ref, o_ref):
        o_ref[...] = ((p_ref[...] + r_ref[0].astype(F32)) + r_ref[1].astype(F32)) + r_ref[2].astype(F32)

    return pl.pallas_call(
        body, name="add_chips_" + name, out_shape=jax.ShapeDtypeStruct((2, rows, cols), F32),
        grid_spec=pltpu.PrefetchScalarGridSpec(
            num_scalar_prefetch=1, grid=(rows // tr,),
            in_specs=[pl.BlockSpec((None, tr, cols), lambda i, place_ref: (place_ref[0], i, 0)),
                      pl.BlockSpec((3, tr, cols), lambda i, place_ref: (0, i, 0))],
            out_specs=pl.BlockSpec((None, tr, cols), lambda i, place_ref: (place_ref[1], i, 0))),
        compiler_params=_params("parallel"),
    )(chip_core, part, got)


def _adamw(w, g, m, v, name):
    rows, cols = w.shape
    tr = _pick(rows, (3592, 256, 352, 176, 128, 64, 32, 16, 8))

    def body(w_ref, g_ref, m_ref, v_ref, d_ref, nm_ref, nv_ref):
        d_ref[...], nm_ref[...], nv_ref[...] = _adamw_math(w_ref[...], g_ref[...], m_ref[...], v_ref[...])

    block = pl.BlockSpec((tr, cols), lambda i: (i, 0))
    shape = jax.ShapeDtypeStruct((rows, cols), F32)
    return pl.pallas_call(
        body, name="adamw_" + name, out_shape=(shape, shape, shape), grid=(rows // tr,),
        in_specs=[block] * 4, out_specs=(block,) * 3, compiler_params=_params("parallel"),
    )(w, g, m, v)


def _adamw_math(w, g, m, v):
    m = ADAM_B1 * m + (1.0 - ADAM_B1) * g
    v = ADAM_B2 * v + (1.0 - ADAM_B2) * (g * g)
    m_hat = m / (1.0 - ADAM_B1 ** ADAM_STEP)
    v_hat = v / (1.0 - ADAM_B2 ** ADAM_STEP)
    return -ADAM_LR * (m_hat / (jnp.sqrt(v_hat) + ADAM_EPS) + ADAM_WD * w), m, v


def _adamw_small(ws, gs, ms, vs):
    count = len(ws)

    def body(*refs):
        ins, outs = refs[:4 * count], refs[4 * count:]
        for i in range(count):
            outs[i][...], outs[count + i][...], outs[2 * count + i][...] = _adamw_math(
                ins[i][...], ins[count + i][...], ins[2 * count + i][...], ins[3 * count + i][...])

    shapes = tuple(jax.ShapeDtypeStruct(w.shape, F32) for w in ws)
    out = pl.pallas_call(body, name="adamw_small", out_shape=shapes * 3)(*ws, *gs, *ms, *vs)
    return out[:count], out[count:2 * count], out[2 * count:]


WEIGHTS = ("meta_tokens", "mix_pre_norm", "mix_post_norm", "ffn_pre_norm", "ffn_post_norm", "w_in", "conv_qkv", "a_log",
           "dt_bias", "gdn_norm", "conv_sc", "w_out", "w_gate", "w_up", "w_down")


def kernel(x, meta_tokens, mix_pre_norm, mix_post_norm, ffn_pre_norm, ffn_post_norm, w_in, conv_qkv, a_log, dt_bias, gdn_norm, conv_sc, w_out, w_gate, w_up, w_down, loss_target, m_meta_tokens, m_mix_pre_norm, m_mix_post_norm, m_ffn_pre_norm, m_ffn_post_norm, m_w_in, m_conv_qkv, m_a_log, m_dt_bias, m_gdn_norm, m_conv_sc, m_w_out, m_w_gate, m_w_up, m_w_down, v_meta_tokens, v_mix_pre_norm, v_mix_post_norm, v_ffn_pre_norm, v_ffn_post_norm, v_w_in, v_conv_qkv, v_a_log, v_dt_bias, v_gdn_norm, v_conv_sc, v_w_out, v_w_gate, v_w_up, v_w_down):
    d = x.shape[-1]
    two_d = lambda a: a.reshape(a.shape[-2:])
    weights = dict(zip(WEIGHTS, (meta_tokens, mix_pre_norm, mix_post_norm, ffn_pre_norm, ffn_post_norm, w_in, conv_qkv, a_log,
                                 dt_bias, gdn_norm, conv_sc, w_out, w_gate, w_up, w_down)))
    m_in = dict(zip(WEIGHTS, (m_meta_tokens, m_mix_pre_norm, m_mix_post_norm, m_ffn_pre_norm, m_ffn_post_norm, m_w_in, m_conv_qkv,
                              m_a_log, m_dt_bias, m_gdn_norm, m_conv_sc, m_w_out, m_w_gate, m_w_up, m_w_down)))
    v_in = dict(zip(WEIGHTS, (v_meta_tokens, v_mix_pre_norm, v_mix_post_norm, v_ffn_pre_norm, v_ffn_post_norm, v_w_in, v_conv_qkv,
                              v_a_log, v_dt_bias, v_gdn_norm, v_conv_sc, v_w_out, v_w_gate, v_w_up, v_w_down)))
    core = lax.axis_index("c")
    chip = 2 * lax.axis_index("x") + lax.axis_index("y")
    core_arg = core.reshape(1).astype(jnp.int32)
    chip_core = jnp.stack([chip, core]).astype(jnp.int32)
    whole = lambda a: a.reshape(a.shape[:-3] + (2 * a.shape[-2], d))
    by_rows = lambda n, a: two_d(a).T if n in ("w_in", "w_gate", "w_up") else two_d(a)

    shard = {n: by_rows(n, weights[n]).astype(MXU_DTYPE) for n in MATRICES}
    shard["w_in"] = jnp.pad(shard["w_in"], ((0, IN_SHARD_PAD - IN_SHARD), (0, 0)))
    w_in_all, *small_all = _gather_weights([_halves(shard["w_in"])], [two_d(weights[n]) for n in SHARDED_SMALL])
    w_in_t = _in_to_kernel_order(whole(w_in_all))
    conv_qkv_full, conv_sc_full, meta_full = (jnp.concatenate([a[p] for p in range(N_CHIPS)], axis=1) for a in small_all)

    sq, grad_x, g, sums = _local_step(
        x, loss_target, meta_full, (mix_pre_norm, mix_post_norm, ffn_pre_norm, ffn_post_norm), w_in_t, conv_qkv_full, a_log,
        dt_bias, gdn_norm, conv_sc_full, [_halves(shard[n]) for n in LATER], core_arg)

    totals = [_add_chips(part, got, chip_core, n) for n, (part, got) in zip(MATRICES, sums)]
    *shared, packed_all = _share_halves(totals, _pack_small(dict(g, loss=sq)))
    grads = {n: whole(a) for n, a in zip(MATRICES, shared)}
    grads["w_in"] = grads["w_in"][:IN_SHARD]
    grads.update(_sum_devices(packed_all, chip.reshape(1).astype(jnp.int32)))
    loss = (0.5 / d) * grads.pop("loss")[0, 0]

    small = [n for n in WEIGHTS if n not in MATRICES]
    updates = dict(zip(small, zip(*_adamw_small(*([by_rows(n, params[n]) for n in small] for params in (weights, grads, m_in, v_in))))))
    outs = [[], [], [], []]
    for n in WEIGHTS:
        shape = weights[n].shape
        if n in MATRICES:
            updates[n] = _adamw(by_rows(n, weights[n]), grads[n], by_rows(n, m_in[n]), by_rows(n, v_in[n]), n)
        for out, a in zip(outs, (grads[n], *updates[n])):
            out.append((a.T if n in ("w_in", "w_gate", "w_up") else a).reshape(shape))
    return (loss, grad_x, *outs[0], *outs[1], *outs[2], *outs[3])
```

```python
import functools

import jax
import jax.numpy as jnp
from jax import lax
from jax.experimental import pallas as pl
from jax.experimental.pallas import tpu as pltpu

F32 = jnp.float32
BF16 = jnp.bfloat16
MXU_DTYPE = jnp.bfloat16
MESH = pl.DeviceIdType.MESH

D_MODEL = 1024
N_META = 16
HEADS = 4
HEAD_DIM = 128
GDN_WIDTH = HEADS * HEAD_DIM
GDN_CONV = 4
CHUNK = 64
SC_WIDTH = D_MODEL - GDN_WIDTH
SC_CONV = 3
D_FF = 2816
IN_WIDTH = 4 * GDN_WIDTH + 2 * HEADS + 3 * SC_WIDTH
IN_PAD = 3840
BA_COL = (4 * GDN_WIDTH + 3 * SC_WIDTH) // 128
EPS = 1e-6
LANES = 128
N_CHIPS = 4
VMEM_LIMIT = 48 * 2 ** 20

ADAM_LR = 0.001
ADAM_B1 = 0.9
ADAM_B2 = 0.999
ADAM_EPS = 1e-08
ADAM_WD = 0.01
ADAM_STEP = 10


def _pick(n, candidates):
    for c in candidates:
        if n % c == 0:
            return c
    return n


def _row_tile(n):
    return _pick(n, (352, 256, 176, 128, 64, 32, 16, 8))


def _params(*sem):
    return pltpu.CompilerParams(dimension_semantics=sem, vmem_limit_bytes=VMEM_LIMIT)


def _sigmoid(x):
    return 0.5 * jnp.tanh(0.5 * x) + 0.5


def _softplus(x):
    return jnp.maximum(x, 0.0) + jnp.log(1.0 + jnp.exp(-jnp.abs(x)))


def _dsilu(x, s):
    return s * (1.0 + x * (1.0 - s))


def _mm(a, b, mode, out_dtype, name, init=None, exchange=None):
    if mode == "tn":
        k_dim, m_dim = a.shape
    else:
        m_dim, k_dim = a.shape
    n_dim = b.shape[0] if mode == "nt" else b.shape[1]
    rows = (1056, 1024, 704, 512, 256, 128) if init is not None else (2112, 1056, 1024, 704, 512, 256, 128)
    tm = _pick(m_dim, (1408, 1280, 1024, 512, 256, 128) if mode == "tn" else rows)
    tn = _pick(n_dim, (1408, 1280, 1024, 768, 512, 256, 128))
    tk = _pick(k_dim, (1408, 1280, 1056, 1024, 512, 256, 128))
    nk = k_dim // tk
    if mode == "nn":
        a_spec = pl.BlockSpec((tm, tk), lambda i, j, k: (i, k))
        b_spec = pl.BlockSpec((tk, tn), lambda i, j, k: (k, j))
        dims = (((1,), (0,)), ((), ()))
    elif mode == "nt":
        a_spec = pl.BlockSpec((tm, tk), lambda i, j, k: (i, k))
        b_spec = pl.BlockSpec((tn, tk), lambda i, j, k: (j, k))
        dims = (((1,), (1,)), ((), ()))
    else:
        a_spec = pl.BlockSpec((tk, tm), lambda i, j, k: (k, i))
        b_spec = pl.BlockSpec((tk, tn), lambda i, j, k: (k, j))
        dims = (((0,), (0,)), ((), ()))

    out_spec = pl.BlockSpec((tm, tn), lambda i, j, k: (i, j))
    grid = (m_dim // tm, n_dim // tn, nk)
    parts = () if exchange is None else tuple(exchange)
    count = len(parts)
    first_in = 2 if init is None else 3

    assert out_dtype == F32

    def body(a_ref, b_ref, *rest):
        o_ref = rest[first_in - 2 + count]
        k = pl.program_id(2)
        step = (pl.program_id(0) * grid[1] + pl.program_id(1)) * nk + k
        if count:
            copies = _chip_copies(rest[first_in - 2:first_in - 2 + count], rest[first_in - 1 + count:first_in - 1 + 2 * count],
                                  *rest[first_in - 1 + 2 * count:])

            @pl.when(step == 0)
            def _():
                for cp in copies:
                    cp.start()

        p = lax.dot_general(a_ref[...], b_ref[...], dims, preferred_element_type=F32)
        if nk == 1:
            o_ref[...] = p if init is None else rest[0][...] + p
        else:
            @pl.when(k == 0)
            def _():
                o_ref[...] = p if init is None else rest[0][...] + p

            @pl.when(k > 0)
            def _():
                o_ref[...] += p

        if count:
            @pl.when(step == grid[0] * grid[1] * nk - 1)
            def _():
                for cp in copies:
                    cp.wait_recv()
                for cp in copies:
                    cp.wait_send()

    out = pl.pallas_call(
        body, name=name,
        out_shape=(jax.ShapeDtypeStruct((m_dim, n_dim), out_dtype),)
        + tuple(jax.ShapeDtypeStruct((3,) + p.shape[1:], p.dtype) for p in parts),
        grid=grid,
        in_specs=[a_spec, b_spec] + ([] if init is None else [out_spec]) + [_hbm()] * count,
        out_specs=(out_spec,) + (_hbm(),) * count,
        scratch_shapes=[pltpu.SemaphoreType.DMA((3 * count,)), pltpu.SemaphoreType.DMA((3 * count,))] if count else [],
        compiler_params=_params(*(("arbitrary",) * 3 if count else ("parallel", "parallel", "arbitrary"))),
    )(a, b, *(() if init is None else (init,)), *parts)
    return out[0] if not count else out


def _rms_apply(x, w):
    r = lax.rsqrt(jnp.mean(x * x, axis=-1, keepdims=True) + EPS)
    return x * r * w


def _rms_bwd(x, w, dy):
    r = lax.rsqrt(jnp.mean(x * x, axis=-1, keepdims=True) + EPS)
    xh = x * r
    dyw = dy * w
    dx = r * (dyw - xh * jnp.mean(dyw * xh, axis=-1, keepdims=True))
    return dx, jnp.sum(dy * xh, axis=0, keepdims=True)


def _accumulate(ref, first, value):
    @pl.when(first)
    def _():
        ref[...] = value

    @pl.when(jnp.logical_not(first))
    def _():
        ref[...] += value


def _rows(tr, width):
    return pl.BlockSpec((tr, width), lambda i: (i, 0))


def _vec(width):
    return pl.BlockSpec((1, width), lambda i: (0, 0))


def _embed(x, head, w_pre, w_shard, rows_per_seq):
    batch, seq, d = x.shape
    x_offset = head.shape[0]
    tr = _row_tile(rows_per_seq)
    tiles_per_seq = rows_per_seq // tr
    n = batch * rows_per_seq

    def body(x_ref, head_ref, w_ref, ws_ref, h0_ref, u_ref, wall_ref, send_sems, recv_sems):
        gather = _gather_copies([ws_ref], [wall_ref], send_sems, recv_sems)
        i = pl.program_id(0)
        tile = lax.rem(i, tiles_per_seq)

        @pl.when(i == 0)
        def _():
            for cp in gather[0]:
                cp.start()

        rows = jnp.concatenate([head_ref[...], x_ref[0:tr - x_offset, :]], axis=0)
        if tiles_per_seq > 1:
            start = pl.multiple_of(jnp.maximum(tile * tr - x_offset, 0), SUBLANES)
            rows = jnp.where(tile == 0, rows, x_ref[pl.ds(start, tr), :])
        h0_ref[...] = rows
        u_ref[...] = _rms_apply(rows, w_ref[...]).astype(u_ref.dtype)

        @pl.when(i == n // tr - 1)
        def _():
            _gather_finish(gather)

    return pl.pallas_call(
        body, name="embed",
        out_shape=(jax.ShapeDtypeStruct((n, d), F32), jax.ShapeDtypeStruct((n, d), MXU_DTYPE),
                   jax.ShapeDtypeStruct((N_CHIPS,) + w_shard.shape, w_shard.dtype)),
        grid=(n // tr,),
        in_specs=[pl.BlockSpec((None, seq, d), lambda i: (i // tiles_per_seq, 0, 0)),
                  pl.BlockSpec((x_offset, d), lambda i: (0, 0)), _vec(d), _hbm()],
        out_specs=(_rows(tr, d), _rows(tr, d), _hbm()),
        scratch_shapes=[pltpu.SemaphoreType.DMA((GATHER_SEMS,)), pltpu.SemaphoreType.DMA((GATHER_SEMS,))],
        compiler_params=_params("arbitrary"),
    )(x, head, w_pre, w_shard)


def _mix_residual(h0, mix, w_post, w_pre):
    n, d = h0.shape
    tr = _row_tile(n)

    def body(h0_ref, mix_ref, wpost_ref, wpre_ref, h1_ref, u2_ref):
        h1 = h0_ref[...] + _rms_apply(mix_ref[...], wpost_ref[...])
        h1_ref[...] = h1
        u2_ref[...] = _rms_apply(h1, wpre_ref[...]).astype(u2_ref.dtype)

    return pl.pallas_call(
        body, name="mix_residual",
        out_shape=(jax.ShapeDtypeStruct((n, d), F32), jax.ShapeDtypeStruct((n, d), MXU_DTYPE)), grid=(n // tr,),
        in_specs=[_rows(tr, d), _rows(tr, d), _vec(d), _vec(d)], out_specs=(_rows(tr, d), _rows(tr, d)),
        compiler_params=_params("parallel"),
    )(h0, mix, w_post, w_pre)


NT_DIMS = (((1,), (1,)), ((), ()))


def _ffn_tiles(n):
    return _pick(n, (1056, 704, 512, 256, 128)), _pick(D_FF, (1408, 256, 128))


def _swiglu_fwd(u, w_gate_t, w_up_t, w_next):
    n, d = u.shape
    tm, tn = _ffn_tiles(n)
    grid = (D_FF // tn, n // tm)

    def body(u_ref, wg_ref, wu_ref, wn_ref, g_ref, up_ref, act_ref, wall_ref, send_sems, recv_sems):
        gather = _gather_copies([wn_ref], [wall_ref], send_sems, recv_sems)
        step = pl.program_id(0) * grid[1] + pl.program_id(1)

        @pl.when(step == 0)
        def _():
            for cp in gather[0]:
                cp.start()

        a = u_ref[...]
        g = lax.dot_general(a, wg_ref[...], NT_DIMS, preferred_element_type=F32)
        up = lax.dot_general(a, wu_ref[...], NT_DIMS, preferred_element_type=F32)
        g_ref[...] = g.astype(g_ref.dtype)
        up_ref[...] = up.astype(up_ref.dtype)
        act_ref[...] = (g * _sigmoid(g) * up).astype(act_ref.dtype)

        @pl.when(step == grid[0] * grid[1] - 1)
        def _():
            _gather_finish(gather)

    tile = pl.BlockSpec((tm, tn), lambda j, i: (i, j))
    weight = pl.BlockSpec((tn, d), lambda j, i: (j, 0))
    wide = jax.ShapeDtypeStruct((n, D_FF), MXU_DTYPE)
    return pl.pallas_call(
        body, name="swiglu_fwd",
        out_shape=(wide, wide, jax.ShapeDtypeStruct((n, D_FF), MXU_DTYPE),
                   jax.ShapeDtypeStruct((N_CHIPS,) + w_next.shape, w_next.dtype)),
        grid=grid,
        in_specs=[pl.BlockSpec((tm, d), lambda j, i: (i, 0)), weight, weight, _hbm()], out_specs=(tile, tile, tile, _hbm()),
        scratch_shapes=[pltpu.SemaphoreType.DMA((GATHER_SEMS,)), pltpu.SemaphoreType.DMA((GATHER_SEMS,))],
        compiler_params=_params("arbitrary", "arbitrary"),
    )(u, w_gate_t, w_up_t, w_next)


def _swiglu_bwd(dffn, w_down, gate, up):
    n, d = dffn.shape
    tm, tn = _ffn_tiles(n)

    def body(dy_ref, w_ref, g_ref, u_ref, dg_ref, du_ref):
        da = lax.dot_general(dy_ref[...], w_ref[...], NT_DIMS, preferred_element_type=F32)
        g = g_ref[...].astype(F32)
        s = _sigmoid(g)
        dg_ref[...] = (da * u_ref[...].astype(F32) * _dsilu(g, s)).astype(dg_ref.dtype)
        du_ref[...] = (da * g * s).astype(du_ref.dtype)

    tile = pl.BlockSpec((tm, tn), lambda j, i: (i, j))
    shape = jax.ShapeDtypeStruct((n, D_FF), MXU_DTYPE)
    return pl.pallas_call(
        body, name="swiglu_bwd", out_shape=(shape, shape), grid=(D_FF // tn, n // tm),
        in_specs=[pl.BlockSpec((tm, d), lambda j, i: (i, 0)), pl.BlockSpec((tn, d), lambda j, i: (j, 0)), tile, tile],
        out_specs=(tile, tile), compiler_params=_params("parallel", "parallel"),
    )(dffn, w_down, gate, up)


def _loss_head(h1, ffn, w_post, target, rows_per_seq, x_offset):
    n, d = h1.shape
    tr = _row_tile(rows_per_seq)
    tiles_per_seq = rows_per_seq // tr
    seq = target.shape[1]

    def seq_rows(t_ref, tile):
        first = jnp.concatenate([jnp.zeros((x_offset, d), F32), t_ref[0:tr - x_offset, :]], axis=0)
        if tiles_per_seq == 1:
            return first
        start = pl.multiple_of(jnp.maximum(tile * tr - x_offset, 0), SUBLANES)
        return jnp.where(tile == 0, first, t_ref[pl.ds(start, tr), :])

    def body(h1_ref, ffn_ref, w_ref, t_ref, dh2_ref, dffn_ref, dw_ref, sq_ref):
        i = pl.program_id(0)
        tile = lax.rem(i, tiles_per_seq)
        w = w_ref[...]
        f = ffn_ref[...]
        r = lax.rsqrt(jnp.mean(f * f, axis=-1, keepdims=True) + EPS)
        fh = f * r
        row = tile * tr + lax.broadcasted_iota(jnp.int32, (tr, 1), 0)
        err = jnp.where(row >= x_offset, h1_ref[...] + fh * w - seq_rows(t_ref, tile), 0.0)
        dh2 = err * (1.0 / d)
        dh2_ref[...] = dh2
        dyw = dh2 * w
        dffn_ref[...] = (r * (dyw - fh * jnp.mean(dyw * fh, axis=-1, keepdims=True))).astype(dffn_ref.dtype)
        _accumulate(dw_ref, i == 0, jnp.sum(dh2 * fh, axis=0, keepdims=True))
        _accumulate(sq_ref, i == 0, jnp.sum(jnp.sum(err * err, axis=1, keepdims=True), axis=0, keepdims=True))

    return pl.pallas_call(
        body, name="loss_head",
        out_shape=(jax.ShapeDtypeStruct((n, d), F32), jax.ShapeDtypeStruct((n, d), MXU_DTYPE),
                   jax.ShapeDtypeStruct((1, d), F32), jax.ShapeDtypeStruct((1, 1), F32)),
        grid=(n // tr,),
        in_specs=[_rows(tr, d), _rows(tr, d), _vec(d), pl.BlockSpec((None, seq, d), lambda i: (i // tiles_per_seq, 0, 0))],
        out_specs=(_rows(tr, d), _rows(tr, d), _vec(d), _vec(1)),
        compiler_params=_params("arbitrary"),
    )(h1, ffn, w_post, target)


def _mid_bwd(h1, mix, w_mix_post, w_ffn_pre, dh2, du2, grads):
    n, d = h1.shape
    tr = _row_tile(n)
    count = len(grads)

    def body(h1_ref, mix_ref, wpost_ref, wpre_ref, dh2_ref, du2_ref, *rest):
        g_refs, (dh1_ref, dmix_ref, dwpre_ref, dwpost_ref), got_refs = rest[:count], rest[count:count + 4], rest[count + 4:2 * count + 4]
        exchange = _sibling_copies(g_refs, got_refs, *rest[2 * count + 4:])
        i = pl.program_id(0)

        @pl.when(i == 0)
        def _():
            for cp in exchange:
                cp.start()

        dx, dwpre = _rms_bwd(h1_ref[...], wpre_ref[...], du2_ref[...])
        dh1 = dh2_ref[...] + dx
        dh1_ref[...] = dh1
        dmix, dwpost = _rms_bwd(mix_ref[...], wpost_ref[...], dh1)
        dmix_ref[...] = dmix.astype(dmix_ref.dtype)
        _accumulate(dwpre_ref, i == 0, dwpre)
        _accumulate(dwpost_ref, i == 0, dwpost)

        @pl.when(i == n // tr - 1)
        def _():
            for cp in exchange:
                cp.wait_recv()
            for cp in exchange:
                cp.wait_send()

    dh1, dmix, dwpre, dwpost, *got = pl.pallas_call(
        body, name="mid_bwd",
        out_shape=(jax.ShapeDtypeStruct((n, d), F32), jax.ShapeDtypeStruct((n, d), MXU_DTYPE),
                   jax.ShapeDtypeStruct((1, d), F32), jax.ShapeDtypeStruct((1, d), F32))
        + tuple(jax.ShapeDtypeStruct((g.shape[0],) + g.shape[2:], F32) for g in grads),
        grid=(n // tr,),
        in_specs=[_rows(tr, d), _rows(tr, d), _vec(d), _vec(d), _rows(tr, d), _rows(tr, d)] + [_hbm()] * count,
        out_specs=(_rows(tr, d), _rows(tr, d), _vec(d), _vec(d)) + (_hbm(),) * count,
        scratch_shapes=[pltpu.SemaphoreType.DMA((count,)), pltpu.SemaphoreType.DMA((count,))],
        compiler_params=_params("arbitrary"),
    )(h1, mix, w_mix_post, w_ffn_pre, dh2, du2, *grads)
    return dh1, dmix, dwpre, dwpost, got


def _in_bwd(h0, w_pre, dh1, du1, rows_per_seq, pad_rows, x_offset):
    n, d = h0.shape
    tr = _row_tile(rows_per_seq)
    tiles_per_seq = rows_per_seq // tr
    seq = rows_per_seq - x_offset

    def body(h0_ref, w_ref, dh1_ref, du1_ref, gx_ref, dmeta_ref, dw_ref):
        i = pl.program_id(0)
        tile = lax.rem(i, tiles_per_seq)
        dx, dw = _rms_bwd(h0_ref[...], w_ref[...], du1_ref[...])
        dh0 = dh1_ref[...] + dx
        _accumulate(dw_ref, i == 0, dw)

        @pl.when(tile == 0)
        def _():
            gx_ref[0:tr - x_offset, :] = dh0[x_offset:, :]
            _accumulate(dmeta_ref, i == 0, dh0[pad_rows:x_offset, :])

        if tiles_per_seq > 1:
            @pl.when(tile > 0)
            def _():
                gx_ref[pl.ds(pl.multiple_of(tile * tr - x_offset, SUBLANES), tr), :] = dh0

    return pl.pallas_call(
        body, name="in_bwd",
        out_shape=(jax.ShapeDtypeStruct((n // rows_per_seq, seq, d), F32), jax.ShapeDtypeStruct((x_offset - pad_rows, d), F32),
                   jax.ShapeDtypeStruct((1, d), F32)),
        grid=(n // tr,),
        in_specs=[_rows(tr, d), _vec(d), _rows(tr, d), _rows(tr, d)],
        out_specs=(pl.BlockSpec((None, seq, d), lambda i: (i // tiles_per_seq, 0, 0)),
                   pl.BlockSpec((x_offset - pad_rows, d), lambda i: (0, 0)), _vec(d)),
        compiler_params=_params("arbitrary"),
    )(h0, w_pre, dh1, du1)


def _lane_is(lo, hi):
    lane = lax.broadcasted_iota(jnp.int32, (1, LANES), 1)
    return jnp.logical_and(lane >= lo, lane < hi)


def _gates_fwd(proj, a_log_l, dt_bias_l, rows_per_seq, pad_rows):
    n = proj.shape[0]
    tr = _row_tile(rows_per_seq)
    tiles_per_seq = rows_per_seq // tr

    def body(p_ref, a_ref, dt_ref, o_ref):
        x = p_ref[...]
        row = lax.rem(pl.program_id(0), tiles_per_seq) * tr + lax.broadcasted_iota(jnp.int32, (tr, 1), 0)
        g = -jnp.exp(a_ref[...]) * _softplus(x + dt_ref[...])
        val = jnp.where(_lane_is(0, HEADS), _sigmoid(x), jnp.where(_lane_is(HEADS, 2 * HEADS), g, 0.0))
        o_ref[...] = jnp.where(row >= pad_rows, val, 0.0)

    return pl.pallas_call(
        body, name="gates_fwd", out_shape=jax.ShapeDtypeStruct((n, LANES), F32), grid=(n // tr,),
        in_specs=[pl.BlockSpec((tr, LANES), lambda i: (i, BA_COL)), _vec(LANES), _vec(LANES)],
        out_specs=_rows(tr, LANES), compiler_params=_params("parallel"),
    )(proj, a_log_l, dt_bias_l)


def _gates_bwd(proj, dbg, a_log_l, dt_bias_l, rows_per_seq, pad_rows, dproj):
    n = proj.shape[0]
    tr = _row_tile(rows_per_seq)
    tiles_per_seq = rows_per_seq // tr

    def body(p_ref, d_ref, a_ref, dt_ref, _, dx_ref, da_ref, ddt_ref):
        i = pl.program_id(0)
        x = p_ref[...]
        d = d_ref[...]
        row = lax.rem(i, tiles_per_seq) * tr + lax.broadcasted_iota(jnp.int32, (tr, 1), 0)
        live = row >= pad_rows
        beta = _sigmoid(x)
        ea = jnp.exp(a_ref[...])
        xa = x + dt_ref[...]
        g = -ea * _softplus(xa)
        is_g = _lane_is(HEADS, 2 * HEADS)
        d_alogit = jnp.where(jnp.logical_and(live, is_g), d * (-ea) * _sigmoid(xa), 0.0)
        d_blogit = jnp.where(jnp.logical_and(live, _lane_is(0, HEADS)), d * beta * (1.0 - beta), 0.0)
        dx_ref[:, :LANES] = (d_alogit + d_blogit).astype(dx_ref.dtype)
        dx_ref[:, LANES:] = jnp.zeros((tr, LANES), dx_ref.dtype)
        _accumulate(da_ref, i == 0, jnp.sum(jnp.where(jnp.logical_and(live, is_g), d * g, 0.0), axis=0, keepdims=True))
        _accumulate(ddt_ref, i == 0, jnp.sum(d_alogit, axis=0, keepdims=True))

    return pl.pallas_call(
        body, name="gates_bwd",
        out_shape=(jax.ShapeDtypeStruct(dproj.shape, dproj.dtype), jax.ShapeDtypeStruct((1, LANES), F32),
                   jax.ShapeDtypeStruct((1, LANES), F32)),
        grid=(n // tr,),
        in_specs=[pl.BlockSpec((tr, LANES), lambda i: (i, BA_COL)), _rows(tr, LANES), _vec(LANES), _vec(LANES), _hbm()],
        out_specs=(pl.BlockSpec((tr, 2 * LANES), lambda i: (i, BA_COL // 2)), _vec(LANES), _vec(LANES)),
        input_output_aliases={4: 0},
        compiler_params=_params("arbitrary"),
    )(proj, dbg, a_log_l, dt_bias_l, dproj)


HALO = 8


def _halo_scratch(rs):
    return pltpu.VMEM((rs + 2 * HALO, LANES), F32)


def _stage(ref, x):
    rs = x.shape[0]
    ref[0:HALO, :] = jnp.zeros((HALO, LANES), F32)
    ref[HALO + rs:, :] = jnp.zeros((HALO, LANES), F32)
    ref[HALO:HALO + rs, :] = x


def _shifted(ref, k, rs):
    return ref[pl.ds(HALO - k, rs), :]


def _causal_conv(x, x_staged, w, width):
    acc = w[width - 1:width, :] * x
    for i in range(width - 1):
        acc = acc + w[i:i + 1, :] * _shifted(x_staged, width - 1 - i, x.shape[0])
    return acc


def _anti_causal_conv(dy, dy_staged, w, width):
    acc = w[width - 1:width, :] * dy
    for i in range(width - 1):
        acc = acc + w[i:i + 1, :] * _shifted(dy_staged, -(width - 1 - i), dy.shape[0])
    return acc


def _conv_weight_grad(dy, x, x_staged, width):
    taps = [_shifted(x_staged, width - 1 - i, x.shape[0]) for i in range(width - 1)] + [x]
    return jnp.concatenate([jnp.sum(dy * tap, axis=0, keepdims=True) for tap in taps], axis=0)


def _seq_cols(rs, col0, heads):
    return pl.BlockSpec((rs, heads * LANES), lambda j, b: (b, col0 // heads + j))


def _tap_cols(width, col0, heads):
    return pl.BlockSpec((width, heads * LANES), lambda j, b: (0, col0 // heads + j))


def _lanes_of(h):
    return slice(h * LANES, (h + 1) * LANES)


def _qkv_fwd(proj, conv_w, kind, rs):
    n = proj.shape[0]
    col0 = {"q": 0, "k": HEADS, "v": 2 * HEADS}[kind]
    hb = HEADS

    def body(p_ref, w_ref, o_ref, staged):
        for h in range(hb):
            pre = p_ref[:, _lanes_of(h)]
            _stage(staged, pre)
            c = _causal_conv(pre, staged, w_ref[:, _lanes_of(h)], GDN_CONV)
            s = c * _sigmoid(c)
            if kind != "v":
                s = s * lax.rsqrt(jnp.sum(s * s, axis=-1, keepdims=True) + EPS)
            if kind == "q":
                s = s * (HEAD_DIM ** -0.5)
            o_ref[:, _lanes_of(h)] = s

    return pl.pallas_call(
        body, name="qkv_fwd_" + kind, out_shape=jax.ShapeDtypeStruct((n, GDN_WIDTH), F32), grid=(HEADS // hb, n // rs),
        in_specs=[_seq_cols(rs, col0, hb), _tap_cols(GDN_CONV, col0, hb)],
        out_specs=_seq_cols(rs, 0, hb), scratch_shapes=[_halo_scratch(rs)], compiler_params=_params("parallel", "parallel"),
    )(proj, conv_w)


def _qkv_bwd(dy, proj, conv_w, kind, rs, dproj):
    n = proj.shape[0]
    col0 = {"q": 0, "k": HEADS, "v": 2 * HEADS}[kind]
    hb = HEADS

    def body(dy_ref, p_ref, w_ref, _, dp_ref, dw_ref, pre_staged, dc_staged):
        for h in range(hb):
            lanes = _lanes_of(h)
            pre = p_ref[:, lanes]
            w = w_ref[:, lanes]
            _stage(pre_staged, pre)
            c = _causal_conv(pre, pre_staged, w, GDN_CONV)
            sg = _sigmoid(c)
            s = c * sg
            ds = dy_ref[:, lanes]
            if kind == "q":
                ds = ds * (HEAD_DIM ** -0.5)
            if kind != "v":
                r = lax.rsqrt(jnp.sum(s * s, axis=-1, keepdims=True) + EPS)
                sh = s * r
                ds = r * (ds - sh * jnp.sum(ds * sh, axis=-1, keepdims=True))
            dc = ds * _dsilu(c, sg)
            _stage(dc_staged, dc)
            dp_ref[:, lanes] = _anti_causal_conv(dc, dc_staged, w, GDN_CONV).astype(dp_ref.dtype)
            _accumulate(dw_ref.at[:, lanes], pl.program_id(1) == 0, _conv_weight_grad(dc, pre, pre_staged, GDN_CONV))

    return pl.pallas_call(
        body, name="qkv_bwd_" + kind,
        out_shape=(jax.ShapeDtypeStruct(dproj.shape, dproj.dtype), jax.ShapeDtypeStruct((GDN_CONV, GDN_WIDTH), F32)),
        grid=(HEADS // hb, n // rs),
        in_specs=[_seq_cols(rs, 0, hb), _seq_cols(rs, col0, hb), _tap_cols(GDN_CONV, col0, hb), _hbm()],
        out_specs=(_seq_cols(rs, col0, hb), _tap_cols(GDN_CONV, 0, hb)), input_output_aliases={3: 0},
        scratch_shapes=[_halo_scratch(rs), _halo_scratch(rs)],
        compiler_params=_params("parallel", "arbitrary"),
    )(dy, proj, conv_w, dproj)


SC_COL = 4 * HEADS


def _sc_fwd(proj, conv_w, rs, cat):
    n = proj.shape[0]

    hb = 2

    def body(x_ref, b_ref, c_ref, w_ref, _, y_ref, staged):
        for h in range(hb):
            lanes = _lanes_of(h)
            u = c_ref[:, lanes] * x_ref[:, lanes]
            _stage(staged, u)
            y_ref[:, lanes] = (b_ref[:, lanes] * _causal_conv(u, staged, w_ref[:, lanes], SC_CONV)).astype(y_ref.dtype)

    return pl.pallas_call(
        body, name="sc_fwd", out_shape=jax.ShapeDtypeStruct(cat.shape, cat.dtype), grid=(HEADS // hb, n // rs),
        in_specs=[_seq_cols(rs, SC_COL, hb), _seq_cols(rs, SC_COL + 4, hb), _seq_cols(rs, SC_COL + 8, hb),
                  _tap_cols(SC_CONV, 0, hb), _hbm()],
        out_specs=_seq_cols(rs, HEADS, hb), input_output_aliases={4: 0}, scratch_shapes=[_halo_scratch(rs)],
        compiler_params=_params("parallel", "parallel"),
    )(proj, proj, proj, conv_w, cat)


def _sc_bwd(dcat, proj, conv_w, rs, dproj):
    n = proj.shape[0]
    hb = 2

    def body(dy_ref, x_ref, b_ref, c_ref, w_ref, _, dx_ref, db_ref, dc_ref, dw_ref, u_staged, dcv_staged):
        for h in range(hb):
            lanes = _lanes_of(h)
            w = w_ref[:, lanes]
            x = x_ref[:, lanes]
            cc = c_ref[:, lanes]
            u = cc * x
            _stage(u_staged, u)
            dy = dy_ref[:, lanes]
            db_ref[:, lanes] = (dy * _causal_conv(u, u_staged, w, SC_CONV)).astype(db_ref.dtype)
            dcv = dy * b_ref[:, lanes]
            _stage(dcv_staged, dcv)
            du = _anti_causal_conv(dcv, dcv_staged, w, SC_CONV)
            dx_ref[:, lanes] = (du * cc).astype(dx_ref.dtype)
            dc_ref[:, lanes] = (du * x).astype(dc_ref.dtype)
            _accumulate(dw_ref.at[:, lanes], pl.program_id(1) == 0, _conv_weight_grad(dcv, u, u_staged, SC_CONV))

    piece = jax.ShapeDtypeStruct((n, SC_WIDTH), MXU_DTYPE)
    return pl.pallas_call(
        body, name="sc_bwd",
        out_shape=(jax.ShapeDtypeStruct(dproj.shape, dproj.dtype), piece, piece, jax.ShapeDtypeStruct((SC_CONV, SC_WIDTH), F32)),
        grid=(HEADS // hb, n // rs),
        in_specs=[_seq_cols(rs, HEADS, hb), _seq_cols(rs, SC_COL, hb), _seq_cols(rs, SC_COL + 4, hb),
                  _seq_cols(rs, SC_COL + 8, hb), _tap_cols(SC_CONV, 0, hb), _hbm()],
        out_specs=(_seq_cols(rs, SC_COL, hb), _seq_cols(rs, 0, hb), _seq_cols(rs, 0, hb), _tap_cols(SC_CONV, 0, hb)),
        input_output_aliases={5: 0},
        scratch_shapes=[_halo_scratch(rs), _halo_scratch(rs)],
        compiler_params=_params("parallel", "arbitrary"),
    )(dcat, proj, proj, proj, conv_w, dproj)


Z_COL = 3 * HEADS


def _gate_fwd(o, proj, gdn_norm, rs):
    n = proj.shape[0]

    hb = HEADS

    def body(o_ref, z_ref, w_ref, y_ref):
        for h in range(hb):
            lanes = _lanes_of(h)
            z = z_ref[:, lanes]
            y_ref[:, lanes] = (_rms_apply(o_ref[:, lanes], w_ref[...]) * z * _sigmoid(z)).astype(y_ref.dtype)

    return pl.pallas_call(
        body, name="gate_fwd", out_shape=jax.ShapeDtypeStruct((n, D_MODEL), MXU_DTYPE), grid=(HEADS // hb, n // rs),
        in_specs=[_seq_cols(rs, 0, hb), _seq_cols(rs, Z_COL, hb), pl.BlockSpec((1, LANES), lambda j, b: (0, 0))],
        out_specs=_seq_cols(rs, 0, hb), compiler_params=_params("parallel", "parallel"),
    )(o, proj, gdn_norm)


def _gate_bwd(dcat, o, proj, gdn_norm, rs):
    n = proj.shape[0]
    hb = 2

    def body(dy_ref, o_ref, z_ref, w_ref, do_ref, dz_ref, dw_ref):
        w = w_ref[...]
        dw_step = jnp.zeros((1, LANES), F32)
        for h in range(hb):
            lanes = _lanes_of(h)
            z = z_ref[:, lanes]
            o = o_ref[:, lanes]
            dy = dy_ref[:, lanes]
            s = _sigmoid(z)
            dz_ref[:, lanes] = (dy * _rms_apply(o, w) * _dsilu(z, s)).astype(dz_ref.dtype)
            do, dw = _rms_bwd(o, w, dy * z * s)
            do_ref[:, lanes] = do
            dw_step = dw_step + dw
        _accumulate(dw_ref, jnp.logical_and(pl.program_id(0) == 0, pl.program_id(1) == 0), dw_step)

    return pl.pallas_call(
        body, name="gate_bwd",
        out_shape=(jax.ShapeDtypeStruct((n, GDN_WIDTH), F32), jax.ShapeDtypeStruct((n, IN_PAD), MXU_DTYPE),
                   jax.ShapeDtypeStruct((1, LANES), F32)),
        grid=(HEADS // hb, n // rs),
        in_specs=[_seq_cols(rs, 0, hb), _seq_cols(rs, 0, hb), _seq_cols(rs, Z_COL, hb), pl.BlockSpec((1, LANES), lambda j, b: (0, 0))],
        out_specs=(_seq_cols(rs, 0, hb), _seq_cols(rs, Z_COL, hb), pl.BlockSpec((1, LANES), lambda j, b: (0, 0))),
        compiler_params=_params("arbitrary", "arbitrary"),
    )(dcat, o, proj, gdn_norm)


def _dot(a, b):
    return jnp.dot(a.astype(MXU_DTYPE), b.astype(MXU_DTYPE), preferred_element_type=F32)


def _dot_nt(a, b):
    return lax.dot_general(a.astype(MXU_DTYPE), b.astype(MXU_DTYPE), (((1,), (1,)), ((), ())),
                           preferred_element_type=F32)


def _dot_tn(a, b):
    return lax.dot_general(a.astype(MXU_DTYPE), b.astype(MXU_DTYPE), (((0,), (0,)), ((), ())),
                           preferred_element_type=F32)


def _split(x):
    hi = x.astype(MXU_DTYPE)
    return hi, (x - hi.astype(F32)).astype(MXU_DTYPE)


def _dot_split(a, b):
    mm = functools.partial(jnp.dot, preferred_element_type=F32)
    return mm(a[0], b[0]) + (mm(a[0], b[1]) + mm(a[1], b[0]))


def _unit_lower_inverses(mats, eye):
    inv = [eye - a for a in mats]
    power = [_split(a) for a in mats]
    square = [_dot_split(p, p) for p in power]
    inv = [i + _dot_split(_split(i), _split(s)) for i, s in zip(inv, square)]
    span = 4
    while span < CHUNK:
        square = [_dot(s, s) for s in square]
        inv = [i + _dot(i, s) for i, s in zip(inv, square)]
        span *= 2
    return inv


def _chunk_masks():
    ii = lax.broadcasted_iota(jnp.int32, (CHUNK, CHUNK), 0)
    jj = lax.broadcasted_iota(jnp.int32, (CHUNK, CHUNK), 1)
    return ii, jj


def _chunk_decay(g_col, ii, jj):
    incl = ii >= jj
    g_row = jnp.sum(jnp.where(ii == jj, g_col, 0.0), axis=0, keepdims=True)
    gc_col = jnp.sum(jnp.where(incl, g_row, 0.0), axis=1, keepdims=True)
    gc_row = jnp.sum(jnp.where(ii <= jj, g_col, 0.0), axis=0, keepdims=True)
    g_total = jnp.sum(g_row, axis=1, keepdims=True)
    decay = jnp.where(incl, jnp.exp(jnp.where(incl, gc_col - gc_row, 0.0)), 0.0)
    return gc_col, g_total, decay


def _gdn_segments(rs, candidates):
    chunks = rs // CHUNK
    seg_chunks = _pick(chunks, candidates)
    return chunks, seg_chunks, chunks // seg_chunks


def _head_lanes(h):
    return slice(h * HEAD_DIM, (h + 1) * HEAD_DIM)


def _gdn_fwd(q, k, v, bg, rs, pieces):
    n = q.shape[0]
    batch = n // rs
    chunks, seg_chunks, segs = _gdn_segments(rs, (11, 8, 4, 2))
    seg_rows = seg_chunks * CHUNK
    chains = [(b, h) for b in range(batch) for h in range(HEADS)]
    each = lambda f, *lists: [f(*args) for args in zip(*lists)]
    count = len(pieces)

    def body(q_ref, k_ref, v_ref, bg_ref, *rest):
        w_refs, (o_ref, s_ref, t_ref), out_refs = rest[:count], rest[count:count + 3], rest[count + 3:2 * count + 3]
        state_ref, send_sems, recv_sems = rest[2 * count + 3:]
        gather = _gather_copies(w_refs, out_refs, send_sems, recv_sems)

        @pl.when(pl.program_id(0) == 0)
        def _():
            state_ref[...] = jnp.zeros_like(state_ref)
            for cp in gather[0]:
                cp.start()

        ii, jj = _chunk_masks()
        incl = ii >= jj
        eye = (ii == jj).astype(F32)

        def chunk(c, carry):
            rows = pl.ds(pl.multiple_of(c * CHUNK, CHUNK), CHUNK)
            bgc = [bg_ref[b, rows, :] for b in range(batch)]
            qc = [q_ref[b, rows, _head_lanes(h)] for b, h in chains]
            kc = [k_ref[b, rows, _head_lanes(h)] for b, h in chains]
            vc = [v_ref[b, rows, _head_lanes(h)] for b, h in chains]
            beta = [bgc[b][:, h:h + 1] for b, h in chains]
            state = [state_ref[b, h] for b, h in chains]
            dec = [_chunk_decay(bgc[b][:, HEADS + h:HEADS + h + 1], ii, jj) for b, h in chains]
            gc_col, g_total, decay = ([d[i] for d in dec] for i in range(3))
            kb = each(lambda x, y: x * y, kc, beta)
            a = each(lambda x, y, d: jnp.where(ii > jj, _dot_nt(x, y) * d, 0.0), kb, kc, decay)
            t_inv = _unit_lower_inverses(a, eye)
            eg = [jnp.exp(g) for g in gc_col]
            u = each(lambda t, x, y: _dot(t, x * y), t_inv, vc, beta)
            w = each(lambda t, x, e: _dot(t, x * e), t_inv, kb, eg)
            qk = each(lambda x, y, d: jnp.where(incl, _dot_nt(x, y) * d, 0.0), qc, kc, decay)
            v_new = each(lambda x, y, s: x - _dot(y, s), u, w, state)
            o = each(lambda x, e, s, m, vn: _dot(x * e, s) + _dot(m, vn), qc, eg, state, qk, v_new)
            new_state = each(lambda s, gt, x, g, vn: s * jnp.exp(gt) + _dot_tn(x * jnp.exp(gt - g), vn),
                             state, g_total, kc, gc_col, v_new)
            for i, (b, h) in enumerate(chains):
                s_ref[b, h, c] = state[i]
                t_ref[b, h, c] = t_inv[i]
                o_ref[b, rows, _head_lanes(h)] = o[i]
                state_ref[b, h] = new_state[i]
            return carry

        lax.fori_loop(0, seg_chunks, chunk, 0)

        @pl.when(pl.program_id(0) == segs - 1)
        def _():
            _gather_finish(gather)

    rows_spec = lambda width: pl.BlockSpec((batch, seg_rows, width), lambda s: (0, s, 0))
    per_chunk = lambda r, c: pl.BlockSpec((batch, HEADS, seg_chunks, r, c), lambda s: (0, 0, s, 0, 0))
    as_seqs = lambda a: a.reshape(batch, rs, a.shape[-1])
    sems = GATHER_SEMS * count
    o, states, t_invs, *gathered = pl.pallas_call(
        body, name="gdn_fwd",
        out_shape=(jax.ShapeDtypeStruct((batch, rs, GDN_WIDTH), F32),
                   jax.ShapeDtypeStruct((batch, HEADS, chunks, HEAD_DIM, HEAD_DIM), F32),
                   jax.ShapeDtypeStruct((batch, HEADS, chunks, CHUNK, CHUNK), F32))
        + tuple(jax.ShapeDtypeStruct((N_CHIPS,) + p.shape, p.dtype) for p in pieces),
        grid=(segs,),
        in_specs=[rows_spec(GDN_WIDTH), rows_spec(GDN_WIDTH), rows_spec(GDN_WIDTH), rows_spec(LANES)] + [_hbm()] * count,
        out_specs=(rows_spec(GDN_WIDTH), per_chunk(HEAD_DIM, HEAD_DIM), per_chunk(CHUNK, CHUNK)) + (_hbm(),) * count,
        scratch_shapes=[pltpu.VMEM((batch, HEADS, HEAD_DIM, HEAD_DIM), F32), pltpu.SemaphoreType.DMA((sems,)),
                        pltpu.SemaphoreType.DMA((sems,))],
        compiler_params=_params("arbitrary"),
    )(as_seqs(q), as_seqs(k), as_seqs(v), as_seqs(bg), *pieces)
    return o.reshape(n, GDN_WIDTH), states, t_invs, gathered


def _gdn_bwd(do, q, k, v, bg, states, t_invs, rs, parts):
    n = q.shape[0]
    batch = n // rs
    chunks, seg_chunks, segs = _gdn_segments(rs, (3, 4, 2))
    seg_rows = seg_chunks * CHUNK
    chains = [(b, h) for b in range(batch) for h in range(HEADS)]
    each = lambda f, *lists: [f(*args) for args in zip(*lists)]
    count = len(parts)

    def body(do_ref, q_ref, k_ref, v_ref, bg_ref, s_ref, t_ref, *rest):
        p_refs, (dq_ref, dk_ref, dv_ref, dbg_ref), got_refs = rest[:count], rest[count:count + 4], rest[count + 4:2 * count + 4]
        dstate_ref, send_sems, recv_sems = rest[2 * count + 4:]
        exchange = _chip_copies(p_refs, got_refs, send_sems, recv_sems)

        @pl.when(pl.program_id(0) == 0)
        def _():
            dstate_ref[...] = jnp.zeros_like(dstate_ref)
            for cp in exchange:
                cp.start()

        ii, jj = _chunk_masks()
        incl = ii >= jj
        strict = ii > jj
        lane = lax.broadcasted_iota(jnp.int32, (1, LANES), 1)

        def rowsum(x):
            return jnp.sum(x, axis=1, keepdims=True)

        def total(x):
            return jnp.sum(rowsum(x), axis=0, keepdims=True)

        def chunk(step, carry):
            c = seg_chunks - 1 - step
            rows = pl.ds(pl.multiple_of(c * CHUNK, CHUNK), CHUNK)
            bgc = [bg_ref[b, rows, :] for b in range(batch)]
            qc = [q_ref[b, rows, _head_lanes(h)] for b, h in chains]
            kc = [k_ref[b, rows, _head_lanes(h)] for b, h in chains]
            vc = [v_ref[b, rows, _head_lanes(h)] for b, h in chains]
            doc = [do_ref[b, rows, _head_lanes(h)] for b, h in chains]
            beta = [bgc[b][:, h:h + 1] for b, h in chains]
            state = [s_ref[b, h, c] for b, h in chains]
            t_inv = [t_ref[b, h, c] for b, h in chains]
            d_state = [dstate_ref[b, h] for b, h in chains]
            dec = [_chunk_decay(bgc[b][:, HEADS + h:HEADS + h + 1], ii, jj) for b, h in chains]
            gc_col, g_total, decay = ([d[i] for d in dec] for i in range(3))
            kb = each(lambda x, y: x * y, kc, beta)
            vb = each(lambda x, y: x * y, vc, beta)
            eg = [jnp.exp(g) for g in gc_col]
            kbg = each(lambda x, y: x * y, kb, eg)
            a = each(lambda x, y, d: jnp.where(strict, _dot_nt(x, y) * d, 0.0), kb, kc, decay)
            qk = each(lambda x, y, d: jnp.where(incl, _dot_nt(x, y) * d, 0.0), qc, kc, decay)
            w = each(_dot, t_inv, kbg)
            u = each(_dot, t_inv, vb)
            q_dec = each(lambda x, y: x * y, qc, eg)
            ek = each(lambda gt, g: jnp.exp(gt - g), g_total, gc_col)
            k_dec = each(lambda x, y: x * y, kc, ek)
            g_last = [jnp.exp(gt) for gt in g_total]
            v_new = each(lambda x, y, s: x - _dot(y, s), u, w, state)
            dv_new = each(lambda m, d, x, ds: _dot_tn(m, d) + _dot(x, ds), qk, doc, k_dec, d_state)
            dqk = each(lambda d, vn: jnp.where(incl, _dot_nt(d, vn), 0.0), doc, v_new)
            dq_dec = each(_dot_nt, doc, state)
            dk_dec = each(_dot_nt, v_new, d_state)
            dg_last = each(lambda s, ds: total(s * ds), state, d_state)
            new_d_state = each(lambda x, d, gl, ds, y, dvn: _dot_tn(x, d) + gl * ds - _dot_tn(y, dvn),
                               q_dec, doc, g_last, d_state, w, dv_new)
            dw = each(lambda dvn, s: -_dot_nt(dvn, s), dv_new, state)
            dt = each(lambda dvn, x, y, z: _dot_nt(dvn, x) + _dot_nt(y, z), dv_new, vb, dw, kbg)
            dvb = each(_dot_tn, t_inv, dv_new)
            dkbg = each(_dot_tn, t_inv, dw)
            t_dt = each(_dot_tn, t_inv, dt)
            da = each(lambda x, t: -jnp.where(strict, _dot_nt(x, t), 0.0), t_dt, t_inv)
            dm_a = each(lambda x, y: x * y, da, decay)
            dm_qk = each(lambda x, y: x * y, dqk, decay)
            e = each(lambda x, y, z, t: x * y + z * t, da, a, dqk, qk)
            dkb = each(lambda m, x, y, z: _dot(m, x) + y * z, dm_a, kc, dkbg, eg)
            dk = each(lambda m, x, m2, y, z, t, p, bt: _dot_tn(m, x) + _dot_tn(m2, y) + z * t + p * bt,
                      dm_a, kb, dm_qk, qc, dk_dec, ek, dkb, beta)
            dq = each(lambda m, x, y, z: _dot(m, x) + y * z, dm_qk, kc, dq_dec, eg)
            dbeta = each(lambda x, y, z, t: rowsum(x * y + z * t), dkb, kc, dvb, vc)
            dgc = each(lambda x, p, pd, r, rd, s, sd: rowsum(x) - rowsum(jnp.where(ii == jj, jnp.sum(x, axis=0, keepdims=True), 0.0))
                       + rowsum(p * pd - r * rd + s * sd), e, dq_dec, q_dec, dk_dec, k_dec, dkbg, kbg)
            d_total = each(lambda r, rd, x, gl: total(r * rd) + x * gl, dk_dec, k_dec, dg_last, g_last)
            dg = each(lambda x, t: rowsum(jnp.where(jj >= ii, jnp.sum(jnp.where(ii == jj, x, 0.0), axis=0, keepdims=True), 0.0)) + t,
                      dgc, d_total)
            dbg = [jnp.zeros((CHUNK, LANES), F32) for _ in range(batch)]
            for i, (b, h) in enumerate(chains):
                dstate_ref[b, h] = new_d_state[i]
                dk_ref[b, rows, _head_lanes(h)] = dk[i]
                dq_ref[b, rows, _head_lanes(h)] = dq[i]
                dv_ref[b, rows, _head_lanes(h)] = dvb[i] * beta[i]
                dbg[b] = dbg[b] + jnp.where(lane == h, dbeta[i], 0.0) + jnp.where(lane == HEADS + h, dg[i], 0.0)
            for b in range(batch):
                dbg_ref[b, rows, :] = dbg[b]
            return carry

        lax.fori_loop(0, seg_chunks, chunk, 0)

        @pl.when(pl.program_id(0) == segs - 1)
        def _():
            for cp in exchange:
                cp.wait_recv()
            for cp in exchange:
                cp.wait_send()

    rows_spec = lambda width: pl.BlockSpec((batch, seg_rows, width), lambda s: (0, segs - 1 - s, 0))
    per_chunk = lambda r, c: pl.BlockSpec((batch, HEADS, seg_chunks, r, c), lambda s: (0, 0, segs - 1 - s, 0, 0))
    as_seqs = lambda a: a.reshape(batch, rs, a.shape[-1])
    grad = jax.ShapeDtypeStruct((batch, rs, GDN_WIDTH), F32)
    wide = rows_spec(GDN_WIDTH)
    dq, dk, dv, dbg, *got = pl.pallas_call(
        body, name="gdn_bwd",
        out_shape=(grad, grad, grad, jax.ShapeDtypeStruct((batch, rs, LANES), F32))
        + tuple(jax.ShapeDtypeStruct((3,) + p.shape[1:], p.dtype) for p in parts),
        grid=(segs,),
        in_specs=[wide, wide, wide, wide, rows_spec(LANES), per_chunk(HEAD_DIM, HEAD_DIM), per_chunk(CHUNK, CHUNK)]
        + [_hbm()] * count,
        out_specs=(wide, wide, wide, rows_spec(LANES)) + (_hbm(),) * count,
        scratch_shapes=[pltpu.VMEM((batch, HEADS, HEAD_DIM, HEAD_DIM), F32), pltpu.SemaphoreType.DMA((3 * count,)),
                        pltpu.SemaphoreType.DMA((3 * count,))],
        compiler_params=_params("arbitrary"),
    )(as_seqs(do), as_seqs(q), as_seqs(k), as_seqs(v), as_seqs(bg), states, t_invs, *parts)
    return dq.reshape(n, GDN_WIDTH), dk.reshape(n, GDN_WIDTH), dv.reshape(n, GDN_WIDTH), dbg.reshape(n, LANES), got


def _lane_vec(vals, offset):
    k = vals.shape[1]
    return jnp.pad(vals, ((0, 0), (offset, LANES - offset - k)))


LATER = ("w_out", "w_gate", "w_up", "w_down")


def _halves(a):
    return a.reshape(a.shape[:-2] + (2, a.shape[-2] // 2, a.shape[-1]))


def _local_step(x, target, meta, norms, w_in_shard, conv_qkv, a_log, dt_bias, gdn_norm, conv_sc, later_shards, core_arg):
    batch, seq, d = x.shape
    tokens = N_META + seq
    pad_rows = (-tokens) % CHUNK
    rs = tokens + pad_rows
    x_offset = pad_rows + N_META
    n = batch * rs
    w_mix_pre, w_mix_post, w_ffn_pre, w_ffn_post = norms

    head = jnp.concatenate([jnp.zeros((pad_rows, d), F32), meta], axis=0)
    a_log_l = _lane_vec(a_log, HEADS)
    dt_bias_l = _lane_vec(dt_bias, HEADS)

    h0, u1, w_in_all = _embed(x, head, w_mix_pre, w_in_shard, rs)
    w_in_t = _in_to_kernel_order(w_in_all.reshape(N_CHIPS, -1, d))
    proj = _mm(u1, w_in_t, "nt", F32, "mm_proj")
    q = _qkv_fwd(proj, conv_qkv, "q", rs)
    k = _qkv_fwd(proj, conv_qkv, "k", rs)
    v = _qkv_fwd(proj, conv_qkv, "v", rs)
    bg = _gates_fwd(proj, a_log_l, dt_bias_l, rs, pad_rows)
    o, states, t_invs, gathered = _gdn_fwd(q, k, v, bg, rs, later_shards[:3])
    w_out, w_gate_t, w_up_t = (a.reshape(-1, d) for a in gathered)
    cat = _sc_fwd(proj, conv_sc, rs, _gate_fwd(o, proj, gdn_norm, rs))
    mix = _mm(cat, w_out, "nn", F32, "mm_mix")
    h1, u2 = _mix_residual(h0, mix, w_mix_post, w_ffn_pre)
    gate, up, act, w_down = _swiglu_fwd(u2, w_gate_t, w_up_t, later_shards[3])
    w_down = w_down.reshape(-1, d)
    ffn = _mm(act, w_down, "nn", F32, "mm_down")

    dh2, dffn, d_ffn_post, sq = _loss_head(h1, ffn, w_ffn_post, target, rs, x_offset)
    d_w_down = _mm(act, dffn, "tn", F32, "mm_dw_down")
    dgate, dup = _swiglu_bwd(dffn, w_down, gate, up)
    d_w_gate_t = _mm(dgate, u2, "tn", F32, "mm_dw_gate")
    d_w_up_t = _mm(dup, u2, "tn", F32, "mm_dw_up")
    du2 = _mm(dup, w_up_t, "nn", F32, "mm_du2_up", init=_mm(dgate, w_gate_t, "nn", F32, "mm_du2_gate"))
    by_chip = [_halves(g.reshape(N_CHIPS, -1, d)) for g in (d_w_gate_t, d_w_up_t, d_w_down)]
    dh1, dmix, d_ffn_pre, d_mix_post, got_sibling = _mid_bwd(h1, mix, w_mix_post, w_ffn_pre, dh2, du2, by_chip)
    dcat = _mm(dmix, w_out, "nt", F32, "mm_dcat")
    d_w_out = _halves(_mm(cat, dmix, "tn", F32, "mm_dw_out").reshape(N_CHIPS, -1, d))
    by_chip, got_sibling = [d_w_out] + by_chip, list(_exchange_siblings([d_w_out])) + got_sibling
    sums = [_add_sibling(a, b, core_arg, name) for name, a, b in zip(LATER, by_chip, got_sibling)]
    do, dproj, d_gdn_norm = _gate_bwd(dcat, o, proj, gdn_norm, rs)
    dproj, dscb, dscc, d_conv_sc = _sc_bwd(dcat, proj, conv_sc, rs, dproj)
    dq, dk, dv, dbg, got_chips = _gdn_bwd(do, q, k, v, bg, states, t_invs, rs, [send for _, send in sums[:3]])
    dproj, dwq = _qkv_bwd(dq, proj, conv_qkv, "q", rs, dproj)
    dproj, dwk = _qkv_bwd(dk, proj, conv_qkv, "k", rs, dproj)
    dproj, dwv = _qkv_bwd(dv, proj, conv_qkv, "v", rs, dproj)
    d_conv_qkv = jnp.concatenate([dwq, dwk, dwv], axis=1)
    dproj, d_a_log_l, d_dt_bias_l = _gates_bwd(proj, dbg, a_log_l, dt_bias_l, rs, pad_rows, dproj)
    dproj = lax.dynamic_update_slice(dproj, dscb, (0, (SC_COL + HEADS) * LANES))
    dproj = lax.dynamic_update_slice(dproj, dscc, (0, (SC_COL + 2 * HEADS) * LANES))
    d_w_in_t, got_down = _mm(dproj, u1, "tn", F32, "mm_dw_in", exchange=[sums[3][1]])
    got_chips.append(got_down)
    g_in = _halves(_in_from_kernel_order(d_w_in_t))
    sums.insert(0, _add_sibling(g_in, _exchange_siblings([g_in])[0], core_arg, "w_in"))
    du1, got_in = _mm(dproj, w_in_t, "nn", F32, "mm_du1", exchange=[sums[0][1]])
    got_chips.insert(0, got_in)
    grad_x, d_meta, d_mix_pre = _in_bwd(h0, w_mix_pre, dh1, du1, rs, pad_rows, x_offset)

    grads = dict(
        meta_tokens=d_meta,
        mix_pre_norm=d_mix_pre, mix_post_norm=d_mix_post, ffn_pre_norm=d_ffn_pre, ffn_post_norm=d_ffn_post,
        conv_qkv=d_conv_qkv,
        a_log=d_a_log_l[:, HEADS:2 * HEADS], dt_bias=d_dt_bias_l[:, HEADS:2 * HEADS],
        gdn_norm=d_gdn_norm, conv_sc=d_conv_sc,
    )
    return sq, grad_x, grads, [(part, got) for (part, _), got in zip(sums, got_chips)]


MATRICES = ("w_in", "w_out", "w_gate", "w_up", "w_down")
IN_SHARD = IN_WIDTH // N_CHIPS
IN_SHARD_PAD = 928


IN_SEGMENTS = ((0, 0, 4 * GDN_WIDTH), (4 * GDN_WIDTH, IN_WIDTH - 2 * HEADS, 2 * HEADS),
               (4 * GDN_WIDTH + 2 * HEADS, 4 * GDN_WIDTH, 3 * SC_WIDTH))
SUBLANES = 8
PACKED_ROWS = 16


def _in_to_kernel_order(by_chip):
    d = by_chip.shape[-1]
    tl = _pick(d, (256, 128))
    runs = []
    for ref0, ker0, count in IN_SEGMENTS:
        row = ref0
        while row < ref0 + count:
            chip, at = divmod(row, IN_SHARD)
            take = min(ref0 + count - row, IN_SHARD - at)
            runs.append((ker0 + row - ref0, take, chip * IN_SHARD_PAD + at))
            row += take

    def body(w_ref, o_ref):
        o_ref[...] = jnp.zeros_like(o_ref)
        for out0, rows, src0 in runs:
            a0 = out0 // PACKED_ROWS * PACKED_ROWS
            a1 = -(-(out0 + rows) // PACKED_ROWS) * PACKED_ROWS
            window = w_ref[pl.ds(src0 - (out0 - a0), a1 - a0), :]
            row = a0 + lax.broadcasted_iota(jnp.int32, (a1 - a0, 1), 0)
            keep = jnp.logical_and(row >= out0, row < out0 + rows)
            o_ref[a0:a1, :] = jnp.where(keep, window, o_ref[a0:a1, :])

    return pl.pallas_call(
        body, name="in_to_kernel_order", out_shape=jax.ShapeDtypeStruct((IN_PAD, d), by_chip.dtype), grid=(d // tl,),
        in_specs=[pl.BlockSpec((N_CHIPS * IN_SHARD_PAD, tl), lambda j: (0, j))],
        out_specs=pl.BlockSpec((IN_PAD, tl), lambda j: (0, j)),
        compiler_params=_params("parallel"),
    )(by_chip.reshape(N_CHIPS * IN_SHARD_PAD, d))


def _in_from_kernel_order(g_t):
    d = g_t.shape[-1]
    tl = _pick(d, (256, 128))

    def body(g_ref, o_ref):
        row = lax.broadcasted_iota(jnp.int32, (IN_SHARD_PAD, 1), 0)
        for chip in range(N_CHIPS):
            first = chip * IN_SHARD
            runs = []
            for ref0, ker0, count in IN_SEGMENTS:
                lo, hi = max(ref0, first), min(ref0 + count, first + IN_SHARD)
                if lo < hi:
                    runs.append((lo - first, hi - lo, ker0 + lo - ref0))
            val = jnp.zeros((IN_SHARD_PAD, tl), F32)
            patches = []
            for out0, rows, src0 in runs:
                start = src0 - out0
                if 0 <= start <= IN_PAD - IN_SHARD_PAD:
                    window = g_ref[pl.ds(start, IN_SHARD_PAD), :]
                    val = jnp.where(jnp.logical_and(row >= out0, row < out0 + rows), window, val)
                else:
                    patches.append((out0, rows, src0))
            o_ref[chip] = val
            for out0, rows, src0 in patches:
                a0 = out0 // SUBLANES * SUBLANES
                a1 = -(-(out0 + rows) // SUBLANES) * SUBLANES
                window = g_ref[pl.ds(src0 - (out0 - a0), a1 - a0), :]
                keep = jnp.logical_and(row[a0:a1] >= out0, row[a0:a1] < out0 + rows)
                o_ref[chip, a0:a1, :] = jnp.where(keep, window, o_ref[chip, a0:a1, :])

    return pl.pallas_call(
        body, name="in_from_kernel_order", out_shape=jax.ShapeDtypeStruct((N_CHIPS, IN_SHARD_PAD, d), F32), grid=(d // tl,),
        in_specs=[pl.BlockSpec((IN_PAD, tl), lambda j: (0, j))],
        out_specs=pl.BlockSpec((N_CHIPS, IN_SHARD_PAD, tl), lambda j: (0, 0, j)),
        compiler_params=_params("parallel"),
    )(g_t)


PACK_LANES = 3 * GDN_WIDTH
PACKED = dict(mix_pre_norm=(0, 1, 0, D_MODEL), mix_post_norm=(1, 1, 0, D_MODEL), ffn_pre_norm=(2, 1, 0, D_MODEL),
              ffn_post_norm=(3, 1, 0, D_MODEL), a_log=(4, 1, 0, HEADS), dt_bias=(5, 1, 0, HEADS), loss=(6, 1, 0, 1),
              gdn_norm=(7, 1, 0, HEAD_DIM), conv_qkv=(8, GDN_CONV, 0, 3 * GDN_WIDTH), conv_sc=(0, SC_CONV, D_MODEL, SC_WIDTH),
              meta_tokens=(16, N_META, 0, D_MODEL))
PACK_ROWS = 32
SHARDED_SMALL = ("conv_qkv", "conv_sc", "meta_tokens")


def _pack_small(values):
    names = list(PACKED)

    def body(*refs):
        out_ref = refs[-1]
        out_ref[...] = jnp.zeros_like(out_ref)
        for name, ref in zip(names, refs):
            row, rows, lane0, lanes = PACKED[name]
            out_ref[row:row + rows, lane0:lane0 + lanes] = ref[...]

    return pl.pallas_call(body, name="pack_small", out_shape=jax.ShapeDtypeStruct((PACK_ROWS, PACK_LANES), F32))(
        *[values[name] for name in names])


def _sum_devices(packed_all, chip):
    names = list(PACKED)

    def body(chip_ref, all_ref, *rest):
        shard_refs, out_refs = rest[:len(SHARDED_SMALL)], rest[len(SHARDED_SMALL):]

        def total(ref, rows, lanes):
            acc = ref[0, rows, lanes]
            for k in range(1, 8):
                acc = acc + ref[k, rows, lanes]
            return acc

        for name, out in zip(names, out_refs):
            row, rows, lane0, lanes = PACKED[name]
            if name in SHARDED_SMALL:
                out[...] = total(shard_refs[SHARDED_SMALL.index(name)], slice(0, rows), slice(None))
            else:
                out[...] = total(all_ref, slice(row, row + rows), slice(lane0, lane0 + lanes))

    def shard_spec(name):
        row, rows, lane0, lanes = PACKED[name]
        height, width = max(rows, 8), lanes // N_CHIPS
        assert row % height == 0 and lane0 % width == 0
        return pl.BlockSpec((8, height, width), lambda i, chip_ref: (0, row // height, lane0 // width + chip_ref[0]))

    def out_shape(name):
        _, rows, _, lanes = PACKED[name]
        return jax.ShapeDtypeStruct((rows, lanes // N_CHIPS if name in SHARDED_SMALL else lanes), F32)

    whole = lambda shape: pl.BlockSpec(shape, lambda i, chip_ref: (0,) * len(shape))
    outs = pl.pallas_call(
        body, name="sum_devices", out_shape=tuple(out_shape(n) for n in names),
        grid_spec=pltpu.PrefetchScalarGridSpec(
            num_scalar_prefetch=1, grid=(1,),
            in_specs=[whole(packed_all.shape)] + [shard_spec(n) for n in SHARDED_SMALL],
            out_specs=tuple(whole(out_shape(n).shape) for n in names)),
    )(chip, packed_all, *[packed_all] * len(SHARDED_SMALL))
    return dict(zip(names, outs))


def _hbm():
    return pl.BlockSpec(memory_space=pl.ANY)


def _place():
    x, y, c = lax.axis_index("x"), lax.axis_index("y"), lax.axis_index("c")
    chips = ((1 - x, y), (x, 1 - y), (1 - x, 1 - y))
    return x, y, c, chips


def _remote(src, dst, send_sems, recv_sems, k, to):
    return pltpu.make_async_remote_copy(src_ref=src, dst_ref=dst, send_sem=send_sems.at[k], recv_sem=recv_sems.at[k],
                                        device_id=to, device_id_type=MESH)


GATHER_SEMS = 7


def _gather_copies(w_refs, out_refs, send_sems, recv_sems):
    x, y, c, chips = _place()
    mine = 2 * x + y
    sibling = (x, y, 1 - c)
    copy = functools.partial(_remote, send_sems=send_sems, recv_sems=recv_sems)
    direct, landed, passing, from_sibling = [], [], [], []
    for i, (w, o) in enumerate(zip(w_refs, out_refs)):
        k = GATHER_SEMS * i
        direct.append(copy(w, o.at[mine], k=k, to=sibling))
        from_sibling.append(copy(w, o.at[mine], k=k, to=sibling))
        for j, (cx, cy) in enumerate(chips):
            theirs = 2 * cx + cy
            direct.append(copy(w.at[c], o.at[mine, c], k=k + 1 + j, to=(cx, cy, c)))
            landed.append(copy(w.at[c], o.at[theirs, c], k=k + 1 + j, to=sibling))
            passing.append(copy(o.at[theirs, c], o.at[theirs, c], k=k + 4 + j, to=sibling))
            from_sibling.append(copy(w.at[c], o.at[theirs, 1 - c], k=k + 4 + j, to=sibling))
    return direct, landed, passing, from_sibling


def _gather_finish(copies):
    direct, landed, passing, from_sibling = copies
    for arrival, forward in zip(landed, passing):
        arrival.wait_recv()
        forward.start()
    for arrival in from_sibling:
        arrival.wait_recv()
    for cp in direct + passing:
        cp.wait_send()


def _gather_weights(pieces, smalls):
    count, extra = len(pieces), len(smalls)
    total = count + extra

    def body(*refs):
        w_refs, s_refs = refs[:count], refs[count:total]
        out_refs, sall_refs = refs[total:total + count], refs[total + count:2 * total]
        send_sems, recv_sems, local_sems = refs[2 * total:]
        x, y, c, chips = _place()
        mine = 2 * x + y
        own = [pltpu.make_async_copy(s, sall.at[mine], local_sems.at[i]) for i, (s, sall) in enumerate(zip(s_refs, sall_refs))]
        small = [_remote(s, sall.at[mine], send_sems, recv_sems, GATHER_SEMS * count + 3 * i + j, (cx, cy, c))
                 for i, (s, sall) in enumerate(zip(s_refs, sall_refs)) for j, (cx, cy) in enumerate(chips)]
        copies = _gather_copies(w_refs, out_refs, send_sems, recv_sems)
        for cp in own + small + copies[0]:
            cp.start()
        _gather_finish(copies)
        for cp in small:
            cp.wait_recv()
        for cp in small:
            cp.wait_send()
        for cp in own:
            cp.wait()

    sems = GATHER_SEMS * count + 3 * extra
    return pl.pallas_call(
        body, name="gather_weights",
        out_shape=tuple(jax.ShapeDtypeStruct((N_CHIPS,) + p.shape, p.dtype) for p in list(pieces) + list(smalls)),
        in_specs=[_hbm()] * total, out_specs=(_hbm(),) * total,
        scratch_shapes=[pltpu.SemaphoreType.DMA((sems,)), pltpu.SemaphoreType.DMA((sems,)), pltpu.SemaphoreType.DMA((extra,))],
    )(*pieces, *smalls)


def _sibling_copies(g_refs, got_refs, send_sems, recv_sems):
    x, y, c, _ = _place()
    return [_remote(g.at[:, 1 - c], got, send_sems, recv_sems, i, (x, y, 1 - c)) for i, (g, got) in enumerate(zip(g_refs, got_refs))]


def _exchange_siblings(grads):
    count = len(grads)

    def body(*refs):
        copies = _sibling_copies(refs[:count], refs[count:2 * count], *refs[2 * count:])
        for cp in copies:
            cp.start()
        for cp in copies:
            cp.wait_recv()
        for cp in copies:
            cp.wait_send()

    return pl.pallas_call(
        body, name="exchange_siblings",
        out_shape=tuple(jax.ShapeDtypeStruct((g.shape[0],) + g.shape[2:], F32) for g in grads),
        in_specs=[_hbm()] * count, out_specs=(_hbm(),) * count,
        scratch_shapes=[pltpu.SemaphoreType.DMA((count,)), pltpu.SemaphoreType.DMA((count,))],
    )(*grads)


def _chip_copies(p_refs, got_refs, send_sems, recv_sems):
    x, y, c, chips = _place()
    return [_remote(p.at[2 * cx + cy], got.at[j], send_sems, recv_sems, 3 * i + j, (cx, cy, c))
            for i, (p, got) in enumerate(zip(p_refs, got_refs)) for j, (cx, cy) in enumerate(chips)]


def _share_halves(halves, small):
    count = len(halves)

    def body(*refs):
        h_refs, s_ref = refs[:count], refs[count]
        full_refs, sall_ref = refs[count + 1:2 * count + 1], refs[2 * count + 1]
        send_sems, recv_sems, local_sem = refs[2 * count + 2:]
        x, y, c, _ = _place()
        me = 4 * x + 2 * y + c
        own = pltpu.make_async_copy(s_ref, sall_ref.at[me], local_sem)
        own.start()
        copies = [_remote(h.at[c], full.at[c], send_sems, recv_sems, i, (x, y, 1 - c))
                  for i, (h, full) in enumerate(zip(h_refs, full_refs))]
        for k in range(7):
            dx, dy, dc = ((k + 1) >> 2) & 1, ((k + 1) >> 1) & 1, (k + 1) & 1
            peer = (1 - x if dx else x, 1 - y if dy else y, 1 - c if dc else c)
            copies.append(_remote(s_ref, sall_ref.at[me], send_sems, recv_sems, count + k, peer))
        for cp in copies:
            cp.start()
        for cp in copies:
            cp.wait_recv()
        for cp in copies:
            cp.wait_send()
        own.wait()

    return pl.pallas_call(
        body, name="share_halves",
        out_shape=tuple(jax.ShapeDtypeStruct(h.shape, h.dtype) for h in halves) + (jax.ShapeDtypeStruct((8,) + small.shape, F32),),
        in_specs=[_hbm()] * (count + 1), out_specs=(_hbm(),) * (count + 1), input_output_aliases={i: i for i in range(count)},
        scratch_shapes=[pltpu.SemaphoreType.DMA((count + 7,)), pltpu.SemaphoreType.DMA((count + 7,)), pltpu.SemaphoreType.DMA],
    )(*halves, small)


def _add_sibling(grad, got, core, name):
    chips, _, rows, cols = grad.shape

    def body(core_ref, g_ref, r_ref, sum_ref, send_ref):
        s = g_ref[...] + r_ref[...]
        sum_ref[...] = s
        send_ref[...] = s.astype(send_ref.dtype)

    block = pl.BlockSpec((None, rows, cols), lambda p, core_ref: (p, 0, 0))
    return pl.pallas_call(
        body, name="add_sibling_" + name,
        out_shape=(jax.ShapeDtypeStruct((chips, rows, cols), F32), jax.ShapeDtypeStruct((chips, rows, cols), BF16)),
        grid_spec=pltpu.PrefetchScalarGridSpec(
            num_scalar_prefetch=1, grid=(chips,),
            in_specs=[pl.BlockSpec((None, None, rows, cols), lambda p, core_ref: (p, core_ref[0], 0, 0)), block],
            out_specs=(block, block)),
        compiler_params=_params("parallel"),
    )(core, grad, got)


def _add_chips(part, got, chip_core, name):
    _, rows, cols = part.shape
    tr = rows // 2 if rows % 32 == 0 else rows

    def body(place_ref, p_ref, r_ref, o_ref):
        o_ref[...] = ((p_ref[...] + r_ref[0].astype(F32)) + r_ref[1].astype(F32)) + r_ref[2].astype(F32)

    return pl.pallas_call(
        body, name="add_chips_" + name, out_shape=jax.ShapeDtypeStruct((2, rows, cols), F32),
        grid_spec=pltpu.PrefetchScalarGridSpec(
            num_scalar_prefetch=1, grid=(rows // tr,),
            in_specs=[pl.BlockSpec((None, tr, cols), lambda i, place_ref: (place_ref[0], i, 0)),
                      pl.BlockSpec((3, tr, cols), lambda i, place_ref: (0, i, 0))],
            out_specs=pl.BlockSpec((None, tr, cols), lambda i, place_ref: (place_ref[1], i, 0))),
        compiler_params=_params("parallel"),
    )(chip_core, part, got)


def _adamw(w, g, m, v, name):
    rows, cols = w.shape
    tr = _pick(rows, (3592, 256, 352, 176, 128, 64, 32, 16, 8))

    def body(w_ref, g_ref, m_ref, v_ref, d_ref, nm_ref, nv_ref):
        d_ref[...], nm_ref[...], nv_ref[...] = _adamw_math(w_ref[...], g_ref[...], m_ref[...], v_ref[...])

    block = pl.BlockSpec((tr, cols), lambda i: (i, 0))
    shape = jax.ShapeDtypeStruct((rows, cols), F32)
    return pl.pallas_call(
        body, name="adamw_" + name, out_shape=(shape, shape, shape), grid=(rows // tr,),
        in_specs=[block] * 4, out_specs=(block,) * 3, compiler_params=_params("parallel"),
    )(w, g, m, v)


def _adamw_math(w, g, m, v):
    m = ADAM_B1 * m + (1.0 - ADAM_B1) * g
    v = ADAM_B2 * v + (1.0 - ADAM_B2) * (g * g)
    m_hat = m / (1.0 - ADAM_B1 ** ADAM_STEP)
    v_hat = v / (1.0 - ADAM_B2 ** ADAM_STEP)
    return -ADAM_LR * (m_hat / (jnp.sqrt(v_hat) + ADAM_EPS) + ADAM_WD * w), m, v


def _adamw_small(ws, gs, ms, vs):
    count = len(ws)

    def body(*refs):
        ins, outs = refs[:4 * count], refs[4 * count:]
        for i in range(count):
            outs[i][...], outs[count + i][...], outs[2 * count + i][...] = _adamw_math(
                ins[i][...], ins[count + i][...], ins[2 * count + i][...], ins[3 * count + i][...])

    shapes = tuple(jax.ShapeDtypeStruct(w.shape, F32) for w in ws)
    out = pl.pallas_call(body, name="adamw_small", out_shape=shapes * 3)(*ws, *gs, *ms, *vs)
    return out[:count], out[count:2 * count], out[2 * count:]


WEIGHTS = ("meta_tokens", "mix_pre_norm", "mix_post_norm", "ffn_pre_norm", "ffn_post_norm", "w_in", "conv_qkv", "a_log",
           "dt_bias", "gdn_norm", "conv_sc", "w_out", "w_gate", "w_up", "w_down")


def kernel(x, meta_tokens, mix_pre_norm, mix_post_norm, ffn_pre_norm, ffn_post_norm, w_in, conv_qkv, a_log, dt_bias, gdn_norm, conv_sc, w_out, w_gate, w_up, w_down, loss_target, m_meta_tokens, m_mix_pre_norm, m_mix_post_norm, m_ffn_pre_norm, m_ffn_post_norm, m_w_in, m_conv_qkv, m_a_log, m_dt_bias, m_gdn_norm, m_conv_sc, m_w_out, m_w_gate, m_w_up, m_w_down, v_meta_tokens, v_mix_pre_norm, v_mix_post_norm, v_ffn_pre_norm, v_ffn_post_norm, v_w_in, v_conv_qkv, v_a_log, v_dt_bias, v_gdn_norm, v_conv_sc, v_w_out, v_w_gate, v_w_up, v_w_down):
    d = x.shape[-1]
    two_d = lambda a: a.reshape(a.shape[-2:])
    weights = dict(zip(WEIGHTS, (meta_tokens, mix_pre_norm, mix_post_norm, ffn_pre_norm, ffn_post_norm, w_in, conv_qkv, a_log,
                                 dt_bias, gdn_norm, conv_sc, w_out, w_gate, w_up, w_down)))
    m_in = dict(zip(WEIGHTS, (m_meta_tokens, m_mix_pre_norm, m_mix_post_norm, m_ffn_pre_norm, m_ffn_post_norm, m_w_in, m_conv_qkv,
                              m_a_log, m_dt_bias, m_gdn_norm, m_conv_sc, m_w_out, m_w_gate, m_w_up, m_w_down)))
    v_in = dict(zip(WEIGHTS, (v_meta_tokens, v_mix_pre_norm, v_mix_post_norm, v_ffn_pre_norm, v_ffn_post_norm, v_w_in, v_conv_qkv,
                              v_a_log, v_dt_bias, v_gdn_norm, v_conv_sc, v_w_out, v_w_gate, v_w_up, v_w_down)))
    core = lax.axis_index("c")
    chip = 2 * lax.axis_index("x") + lax.axis_index("y")
    core_arg = core.reshape(1).astype(jnp.int32)
    chip_core = jnp.stack([chip, core]).astype(jnp.int32)
    whole = lambda a: a.reshape(a.shape[:-3] + (2 * a.shape[-2], d))
    by_rows = lambda n, a: two_d(a).T if n in ("w_in", "w_gate", "w_up") else two_d(a)

    shard = {n: by_rows(n, weights[n]).astype(MXU_DTYPE) for n in MATRICES}
    shard["w_in"] = jnp.pad(shard["w_in"], ((0, IN_SHARD_PAD - IN_SHARD), (0, 0)))
    small_all = _gather_weights([], [two_d(weights[n]) for n in SHARDED_SMALL])
    conv_qkv_full, conv_sc_full, meta_full = (jnp.concatenate([a[p] for p in range(N_CHIPS)], axis=1) for a in small_all)

    sq, grad_x, g, sums = _local_step(
        x, loss_target, meta_full, (mix_pre_norm, mix_post_norm, ffn_pre_norm, ffn_post_norm), _halves(shard["w_in"]),
        conv_qkv_full, a_log, dt_bias, gdn_norm, conv_sc_full, [_halves(shard[n]) for n in LATER], core_arg)

    totals = [_add_chips(part, got, chip_core, n) for n, (part, got) in zip(MATRICES, sums)]
    *shared, packed_all = _share_halves(totals, _pack_small(dict(g, loss=sq)))
    grads = {n: whole(a) for n, a in zip(MATRICES, shared)}
    grads["w_in"] = grads["w_in"][:IN_SHARD]
    grads.update(_sum_devices(packed_all, chip.reshape(1).astype(jnp.int32)))
    loss = (0.5 / d) * grads.pop("loss")[0, 0]

    small = [n for n in WEIGHTS if n not in MATRICES]
    updates = dict(zip(small, zip(*_adamw_small(*([by_rows(n, params[n]) for n in small] for params in (weights, grads, m_in, v_in))))))
    outs = [[], [], [], []]
    for n in WEIGHTS:
        shape = weights[n].shape
        if n in MATRICES:
            updates[n] = _adamw(by_rows(n, weights[n]), grads[n], by_rows(n, m_in[n]), by_rows(n, v_in[n]), n)
        for out, a in zip(outs, (grads[n], *updates[n])):
            out.append((a.T if n in ("w_in", "w_gate", "w_up") else a).reshape(shape))
    return (loss, grad_x, *outs[0], *outs[1], *outs[2], *outs[3])
```

```python
import functools

import jax
import jax.numpy as jnp
from jax import lax
from jax.experimental import pallas as pl
from jax.experimental.pallas import tpu as pltpu

F32 = jnp.float32
BF16 = jnp.bfloat16
MXU_DTYPE = jnp.bfloat16
MESH = pl.DeviceIdType.MESH

D_MODEL = 1024
N_META = 16
HEADS = 4
HEAD_DIM = 128
GDN_WIDTH = HEADS * HEAD_DIM
GDN_CONV = 4
CHUNK = 64
SC_WIDTH = D_MODEL - GDN_WIDTH
SC_CONV = 3
D_FF = 2816
IN_WIDTH = 4 * GDN_WIDTH + 2 * HEADS + 3 * SC_WIDTH
IN_PAD = 3840
BA_COL = (4 * GDN_WIDTH + 3 * SC_WIDTH) // 128
EPS = 1e-6
LANES = 128
N_CHIPS = 4
VMEM_LIMIT = 48 * 2 ** 20
MM_VMEM_BUDGET = 42 * 2 ** 20

ADAM_LR = 0.001
ADAM_B1 = 0.9
ADAM_B2 = 0.999
ADAM_EPS = 1e-08
ADAM_WD = 0.01
ADAM_STEP = 10


def _pick(n, candidates):
    for c in candidates:
        if n % c == 0:
            return c
    return n


def _row_tile(n):
    return _pick(n, (352, 256, 176, 128, 64, 32, 16, 8))


def _params(*sem):
    return pltpu.CompilerParams(dimension_semantics=sem, vmem_limit_bytes=VMEM_LIMIT)


def _sigmoid(x):
    return 0.5 * jnp.tanh(0.5 * x) + 0.5


def _softplus(x):
    return jnp.maximum(x, 0.0) + jnp.log(1.0 + jnp.exp(-jnp.abs(x)))


def _dsilu(x, s):
    return s * (1.0 + x * (1.0 - s))


def _mm(a, b, mode, out_dtype, name, init=None, exchange=None):
    if mode == "tn":
        k_dim, m_dim = a.shape
    else:
        m_dim, k_dim = a.shape
    n_dim = b.shape[0] if mode == "nt" else b.shape[1]
    tn = _pick(n_dim, (1408, 1280, 1024, 768, 512, 256, 128))
    if mode == "tn":
        tm = _pick(m_dim, (1408, 1280, 1024, 512, 256, 128))
        tk = _pick(k_dim, (2112, 1408, 1280, 1056, 1024, 512, 256, 128))
    else:
        tk = k_dim
        blocks = lambda rows: 2 * (2 * rows * tk + 2 * tk * tn + 4 * rows * tn * (1 if init is None else 2))
        tm = next((t for t in (2112, 1056, 1024, 704, 512, 256, 128) if m_dim % t == 0 and blocks(t) <= MM_VMEM_BUDGET), m_dim)
    nk = k_dim // tk
    if mode == "nn":
        a_spec = pl.BlockSpec((tm, tk), lambda i, j, k: (i, k))
        b_spec = pl.BlockSpec((tk, tn), lambda i, j, k: (k, j))
        dims = (((1,), (0,)), ((), ()))
    elif mode == "nt":
        a_spec = pl.BlockSpec((tm, tk), lambda i, j, k: (i, k))
        b_spec = pl.BlockSpec((tn, tk), lambda i, j, k: (j, k))
        dims = (((1,), (1,)), ((), ()))
    else:
        a_spec = pl.BlockSpec((tk, tm), lambda i, j, k: (k, i))
        b_spec = pl.BlockSpec((tk, tn), lambda i, j, k: (k, j))
        dims = (((0,), (0,)), ((), ()))

    out_spec = pl.BlockSpec((tm, tn), lambda i, j, k: (i, j))
    grid = (m_dim // tm, n_dim // tn, nk)
    parts = () if exchange is None else tuple(exchange)
    count = len(parts)
    first_in = 2 if init is None else 3

    assert out_dtype == F32

    def body(a_ref, b_ref, *rest):
        o_ref = rest[first_in - 2 + count]
        k = pl.program_id(2)
        step = (pl.program_id(0) * grid[1] + pl.program_id(1)) * nk + k
        if count:
            copies = _chip_copies(rest[first_in - 2:first_in - 2 + count], rest[first_in - 1 + count:first_in - 1 + 2 * count],
                                  *rest[first_in - 1 + 2 * count:])

            @pl.when(step == 0)
            def _():
                for cp in copies:
                    cp.start()

        p = lax.dot_general(a_ref[...], b_ref[...], dims, preferred_element_type=F32)
        if nk == 1:
            o_ref[...] = p if init is None else rest[0][...] + p
        else:
            @pl.when(k == 0)
            def _():
                o_ref[...] = p if init is None else rest[0][...] + p

            @pl.when(k > 0)
            def _():
                o_ref[...] += p

        if count:
            @pl.when(step == grid[0] * grid[1] * nk - 1)
            def _():
                for cp in copies:
                    cp.wait_recv()
                for cp in copies:
                    cp.wait_send()

    out = pl.pallas_call(
        body, name=name,
        out_shape=(jax.ShapeDtypeStruct((m_dim, n_dim), out_dtype),)
        + tuple(jax.ShapeDtypeStruct((3,) + p.shape[1:], p.dtype) for p in parts),
        grid=grid,
        in_specs=[a_spec, b_spec] + ([] if init is None else [out_spec]) + [_hbm()] * count,
        out_specs=(out_spec,) + (_hbm(),) * count,
        scratch_shapes=[pltpu.SemaphoreType.DMA((3 * count,)), pltpu.SemaphoreType.DMA((3 * count,))] if count else [],
        compiler_params=_params(*(("arbitrary",) * 3 if count else ("parallel", "parallel", "arbitrary"))),
    )(a, b, *(() if init is None else (init,)), *parts)
    return out[0] if not count else out


def _rms_apply(x, w):
    r = lax.rsqrt(jnp.mean(x * x, axis=-1, keepdims=True) + EPS)
    return x * r * w


def _rms_bwd(x, w, dy):
    r = lax.rsqrt(jnp.mean(x * x, axis=-1, keepdims=True) + EPS)
    xh = x * r
    dyw = dy * w
    dx = r * (dyw - xh * jnp.mean(dyw * xh, axis=-1, keepdims=True))
    return dx, jnp.sum(dy * xh, axis=0, keepdims=True)


def _accumulate(ref, first, value):
    @pl.when(first)
    def _():
        ref[...] = value

    @pl.when(jnp.logical_not(first))
    def _():
        ref[...] += value


def _rows(tr, width):
    return pl.BlockSpec((tr, width), lambda i: (i, 0))


def _vec(width):
    return pl.BlockSpec((1, width), lambda i: (0, 0))


def _embed(x, head, w_pre, w_shard, rows_per_seq):
    batch, seq, d = x.shape
    x_offset = head.shape[0]
    tr = _row_tile(rows_per_seq)
    tiles_per_seq = rows_per_seq // tr
    n = batch * rows_per_seq

    def body(x_ref, head_ref, w_ref, ws_ref, h0_ref, u_ref, wall_ref, send_sems, recv_sems):
        gather = _gather_copies([ws_ref], [wall_ref], send_sems, recv_sems)
        i = pl.program_id(0)
        tile = lax.rem(i, tiles_per_seq)

        @pl.when(i == 0)
        def _():
            for cp in gather[0]:
                cp.start()

        rows = jnp.concatenate([head_ref[...], x_ref[0:tr - x_offset, :]], axis=0)
        if tiles_per_seq > 1:
            start = pl.multiple_of(jnp.maximum(tile * tr - x_offset, 0), SUBLANES)
            rows = jnp.where(tile == 0, rows, x_ref[pl.ds(start, tr), :])
        h0_ref[...] = rows
        u_ref[...] = _rms_apply(rows, w_ref[...]).astype(u_ref.dtype)

        @pl.when(i == n // tr - 1)
        def _():
            _gather_finish(gather)

    return pl.pallas_call(
        body, name="embed",
        out_shape=(jax.ShapeDtypeStruct((n, d), F32), jax.ShapeDtypeStruct((n, d), MXU_DTYPE),
                   jax.ShapeDtypeStruct((N_CHIPS,) + w_shard.shape, w_shard.dtype)),
        grid=(n // tr,),
        in_specs=[pl.BlockSpec((None, seq, d), lambda i: (i // tiles_per_seq, 0, 0)),
                  pl.BlockSpec((x_offset, d), lambda i: (0, 0)), _vec(d), _hbm()],
        out_specs=(_rows(tr, d), _rows(tr, d), _hbm()),
        scratch_shapes=[pltpu.SemaphoreType.DMA((GATHER_SEMS,)), pltpu.SemaphoreType.DMA((GATHER_SEMS,))],
        compiler_params=_params("arbitrary"),
    )(x, head, w_pre, w_shard)


def _mix_residual(h0, mix, w_post, w_pre):
    n, d = h0.shape
    tr = _row_tile(n)

    def body(h0_ref, mix_ref, wpost_ref, wpre_ref, h1_ref, u2_ref):
        h1 = h0_ref[...] + _rms_apply(mix_ref[...], wpost_ref[...])
        h1_ref[...] = h1
        u2_ref[...] = _rms_apply(h1, wpre_ref[...]).astype(u2_ref.dtype)

    return pl.pallas_call(
        body, name="mix_residual",
        out_shape=(jax.ShapeDtypeStruct((n, d), F32), jax.ShapeDtypeStruct((n, d), MXU_DTYPE)), grid=(n // tr,),
        in_specs=[_rows(tr, d), _rows(tr, d), _vec(d), _vec(d)], out_specs=(_rows(tr, d), _rows(tr, d)),
        compiler_params=_params("parallel"),
    )(h0, mix, w_post, w_pre)


NT_DIMS = (((1,), (1,)), ((), ()))


def _ffn_tiles(n):
    return _pick(n, (1056, 704, 512, 256, 128)), _pick(D_FF, (1408, 256, 128))


def _swiglu_fwd(u, w_gate_t, w_up_t, w_next):
    n, d = u.shape
    tm, tn = _ffn_tiles(n)
    grid = (D_FF // tn, n // tm)

    def body(u_ref, wg_ref, wu_ref, wn_ref, g_ref, up_ref, act_ref, wall_ref, send_sems, recv_sems):
        gather = _gather_copies([wn_ref], [wall_ref], send_sems, recv_sems)
        step = pl.program_id(0) * grid[1] + pl.program_id(1)

        @pl.when(step == 0)
        def _():
            for cp in gather[0]:
                cp.start()

        a = u_ref[...]
        g = lax.dot_general(a, wg_ref[...], NT_DIMS, preferred_element_type=F32)
        up = lax.dot_general(a, wu_ref[...], NT_DIMS, preferred_element_type=F32)
        g_ref[...] = g.astype(g_ref.dtype)
        up_ref[...] = up.astype(up_ref.dtype)
        act_ref[...] = (g * _sigmoid(g) * up).astype(act_ref.dtype)

        @pl.when(step == grid[0] * grid[1] - 1)
        def _():
            _gather_finish(gather)

    tile = pl.BlockSpec((tm, tn), lambda j, i: (i, j))
    weight = pl.BlockSpec((tn, d), lambda j, i: (j, 0))
    wide = jax.ShapeDtypeStruct((n, D_FF), MXU_DTYPE)
    return pl.pallas_call(
        body, name="swiglu_fwd",
        out_shape=(wide, wide, jax.ShapeDtypeStruct((n, D_FF), MXU_DTYPE),
                   jax.ShapeDtypeStruct((N_CHIPS,) + w_next.shape, w_next.dtype)),
        grid=grid,
        in_specs=[pl.BlockSpec((tm, d), lambda j, i: (i, 0)), weight, weight, _hbm()], out_specs=(tile, tile, tile, _hbm()),
        scratch_shapes=[pltpu.SemaphoreType.DMA((GATHER_SEMS,)), pltpu.SemaphoreType.DMA((GATHER_SEMS,))],
        compiler_params=_params("arbitrary", "arbitrary"),
    )(u, w_gate_t, w_up_t, w_next)


def _swiglu_bwd(dffn, w_down, gate, up):
    n, d = dffn.shape
    tm, tn = _ffn_tiles(n)

    def body(dy_ref, w_ref, g_ref, u_ref, dg_ref, du_ref):
        da = lax.dot_general(dy_ref[...], w_ref[...], NT_DIMS, preferred_element_type=F32)
        g = g_ref[...].astype(F32)
        s = _sigmoid(g)
        dg_ref[...] = (da * u_ref[...].astype(F32) * _dsilu(g, s)).astype(dg_ref.dtype)
        du_ref[...] = (da * g * s).astype(du_ref.dtype)

    tile = pl.BlockSpec((tm, tn), lambda j, i: (i, j))
    shape = jax.ShapeDtypeStruct((n, D_FF), MXU_DTYPE)
    return pl.pallas_call(
        body, name="swiglu_bwd", out_shape=(shape, shape), grid=(D_FF // tn, n // tm),
        in_specs=[pl.BlockSpec((tm, d), lambda j, i: (i, 0)), pl.BlockSpec((tn, d), lambda j, i: (j, 0)), tile, tile],
        out_specs=(tile, tile), compiler_params=_params("parallel", "parallel"),
    )(dffn, w_down, gate, up)


def _loss_head(h1, ffn, w_post, target, rows_per_seq, x_offset):
    n, d = h1.shape
    tr = _row_tile(rows_per_seq)
    tiles_per_seq = rows_per_seq // tr
    seq = target.shape[1]

    def seq_rows(t_ref, tile):
        first = jnp.concatenate([jnp.zeros((x_offset, d), F32), t_ref[0:tr - x_offset, :]], axis=0)
        if tiles_per_seq == 1:
            return first
        start = pl.multiple_of(jnp.maximum(tile * tr - x_offset, 0), SUBLANES)
        return jnp.where(tile == 0, first, t_ref[pl.ds(start, tr), :])

    def body(h1_ref, ffn_ref, w_ref, t_ref, dh2_ref, dffn_ref, dw_ref, sq_ref):
        i = pl.program_id(0)
        tile = lax.rem(i, tiles_per_seq)
        w = w_ref[...]
        f = ffn_ref[...]
        r = lax.rsqrt(jnp.mean(f * f, axis=-1, keepdims=True) + EPS)
        fh = f * r
        row = tile * tr + lax.broadcasted_iota(jnp.int32, (tr, 1), 0)
        err = jnp.where(row >= x_offset, h1_ref[...] + fh * w - seq_rows(t_ref, tile), 0.0)
        dh2 = err * (1.0 / d)
        dh2_ref[...] = dh2
        dyw = dh2 * w
        dffn_ref[...] = (r * (dyw - fh * jnp.mean(dyw * fh, axis=-1, keepdims=True))).astype(dffn_ref.dtype)
        _accumulate(dw_ref, i == 0, jnp.sum(dh2 * fh, axis=0, keepdims=True))
        _accumulate(sq_ref, i == 0, jnp.sum(jnp.sum(err * err, axis=1, keepdims=True), axis=0, keepdims=True))

    return pl.pallas_call(
        body, name="loss_head",
        out_shape=(jax.ShapeDtypeStruct((n, d), F32), jax.ShapeDtypeStruct((n, d), MXU_DTYPE),
                   jax.ShapeDtypeStruct((1, d), F32), jax.ShapeDtypeStruct((1, 1), F32)),
        grid=(n // tr,),
        in_specs=[_rows(tr, d), _rows(tr, d), _vec(d), pl.BlockSpec((None, seq, d), lambda i: (i // tiles_per_seq, 0, 0))],
        out_specs=(_rows(tr, d), _rows(tr, d), _vec(d), _vec(1)),
        compiler_params=_params("arbitrary"),
    )(h1, ffn, w_post, target)


def _mid_bwd(h1, mix, w_mix_post, w_ffn_pre, dh2, du2, grads):
    n, d = h1.shape
    tr = _row_tile(n)
    count = len(grads)

    def body(h1_ref, mix_ref, wpost_ref, wpre_ref, dh2_ref, du2_ref, *rest):
        g_refs, (dh1_ref, dmix_ref, dwpre_ref, dwpost_ref), got_refs = rest[:count], rest[count:count + 4], rest[count + 4:2 * count + 4]
        exchange = _sibling_copies(g_refs, got_refs, *rest[2 * count + 4:])
        i = pl.program_id(0)

        @pl.when(i == 0)
        def _():
            for cp in exchange:
                cp.start()

        dx, dwpre = _rms_bwd(h1_ref[...], wpre_ref[...], du2_ref[...])
        dh1 = dh2_ref[...] + dx
        dh1_ref[...] = dh1
        dmix, dwpost = _rms_bwd(mix_ref[...], wpost_ref[...], dh1)
        dmix_ref[...] = dmix.astype(dmix_ref.dtype)
        _accumulate(dwpre_ref, i == 0, dwpre)
        _accumulate(dwpost_ref, i == 0, dwpost)

        @pl.when(i == n // tr - 1)
        def _():
            for cp in exchange:
                cp.wait_recv()
            for cp in exchange:
                cp.wait_send()

    dh1, dmix, dwpre, dwpost, *got = pl.pallas_call(
        body, name="mid_bwd",
        out_shape=(jax.ShapeDtypeStruct((n, d), F32), jax.ShapeDtypeStruct((n, d), MXU_DTYPE),
                   jax.ShapeDtypeStruct((1, d), F32), jax.ShapeDtypeStruct((1, d), F32))
        + tuple(jax.ShapeDtypeStruct((g.shape[0],) + g.shape[2:], F32) for g in grads),
        grid=(n // tr,),
        in_specs=[_rows(tr, d), _rows(tr, d), _vec(d), _vec(d), _rows(tr, d), _rows(tr, d)] + [_hbm()] * count,
        out_specs=(_rows(tr, d), _rows(tr, d), _vec(d), _vec(d)) + (_hbm(),) * count,
        scratch_shapes=[pltpu.SemaphoreType.DMA((count,)), pltpu.SemaphoreType.DMA((count,))],
        compiler_params=_params("arbitrary"),
    )(h1, mix, w_mix_post, w_ffn_pre, dh2, du2, *grads)
    return dh1, dmix, dwpre, dwpost, got


def _in_bwd(h0, w_pre, dh1, du1, rows_per_seq, pad_rows, x_offset):
    n, d = h0.shape
    tr = _row_tile(rows_per_seq)
    tiles_per_seq = rows_per_seq // tr
    seq = rows_per_seq - x_offset

    def body(h0_ref, w_ref, dh1_ref, du1_ref, gx_ref, dmeta_ref, dw_ref):
        i = pl.program_id(0)
        tile = lax.rem(i, tiles_per_seq)
        dx, dw = _rms_bwd(h0_ref[...], w_ref[...], du1_ref[...])
        dh0 = dh1_ref[...] + dx
        _accumulate(dw_ref, i == 0, dw)

        @pl.when(tile == 0)
        def _():
            gx_ref[0:tr - x_offset, :] = dh0[x_offset:, :]
            _accumulate(dmeta_ref, i == 0, dh0[pad_rows:x_offset, :])

        if tiles_per_seq > 1:
            @pl.when(tile > 0)
            def _():
                gx_ref[pl.ds(pl.multiple_of(tile * tr - x_offset, SUBLANES), tr), :] = dh0

    return pl.pallas_call(
        body, name="in_bwd",
        out_shape=(jax.ShapeDtypeStruct((n // rows_per_seq, seq, d), F32), jax.ShapeDtypeStruct((x_offset - pad_rows, d), F32),
                   jax.ShapeDtypeStruct((1, d), F32)),
        grid=(n // tr,),
        in_specs=[_rows(tr, d), _vec(d), _rows(tr, d), _rows(tr, d)],
        out_specs=(pl.BlockSpec((None, seq, d), lambda i: (i // tiles_per_seq, 0, 0)),
                   pl.BlockSpec((x_offset - pad_rows, d), lambda i: (0, 0)), _vec(d)),
        compiler_params=_params("arbitrary"),
    )(h0, w_pre, dh1, du1)


def _lane_is(lo, hi):
    lane = lax.broadcasted_iota(jnp.int32, (1, LANES), 1)
    return jnp.logical_and(lane >= lo, lane < hi)


def _gates_fwd(proj, a_log_l, dt_bias_l, rows_per_seq, pad_rows):
    n = proj.shape[0]
    tr = _row_tile(rows_per_seq)
    tiles_per_seq = rows_per_seq // tr

    def body(p_ref, a_ref, dt_ref, o_ref):
        x = p_ref[...]
        row = lax.rem(pl.program_id(0), tiles_per_seq) * tr + lax.broadcasted_iota(jnp.int32, (tr, 1), 0)
        g = -jnp.exp(a_ref[...]) * _softplus(x + dt_ref[...])
        val = jnp.where(_lane_is(0, HEADS), _sigmoid(x), jnp.where(_lane_is(HEADS, 2 * HEADS), g, 0.0))
        o_ref[...] = jnp.where(row >= pad_rows, val, 0.0)

    return pl.pallas_call(
        body, name="gates_fwd", out_shape=jax.ShapeDtypeStruct((n, LANES), F32), grid=(n // tr,),
        in_specs=[pl.BlockSpec((tr, LANES), lambda i: (i, BA_COL)), _vec(LANES), _vec(LANES)],
        out_specs=_rows(tr, LANES), compiler_params=_params("parallel"),
    )(proj, a_log_l, dt_bias_l)


def _gates_bwd(proj, dbg, a_log_l, dt_bias_l, rows_per_seq, pad_rows, dproj):
    n = proj.shape[0]
    tr = _row_tile(rows_per_seq)
    tiles_per_seq = rows_per_seq // tr

    def body(p_ref, d_ref, a_ref, dt_ref, _, dx_ref, da_ref, ddt_ref):
        i = pl.program_id(0)
        x = p_ref[...]
        d = d_ref[...]
        row = lax.rem(i, tiles_per_seq) * tr + lax.broadcasted_iota(jnp.int32, (tr, 1), 0)
        live = row >= pad_rows
        beta = _sigmoid(x)
        ea = jnp.exp(a_ref[...])
        xa = x + dt_ref[...]
        g = -ea * _softplus(xa)
        is_g = _lane_is(HEADS, 2 * HEADS)
        d_alogit = jnp.where(jnp.logical_and(live, is_g), d * (-ea) * _sigmoid(xa), 0.0)
        d_blogit = jnp.where(jnp.logical_and(live, _lane_is(0, HEADS)), d * beta * (1.0 - beta), 0.0)
        dx_ref[:, :LANES] = (d_alogit + d_blogit).astype(dx_ref.dtype)
        dx_ref[:, LANES:] = jnp.zeros((tr, LANES), dx_ref.dtype)
        _accumulate(da_ref, i == 0, jnp.sum(jnp.where(jnp.logical_and(live, is_g), d * g, 0.0), axis=0, keepdims=True))
        _accumulate(ddt_ref, i == 0, jnp.sum(d_alogit, axis=0, keepdims=True))

    return pl.pallas_call(
        body, name="gates_bwd",
        out_shape=(jax.ShapeDtypeStruct(dproj.shape, dproj.dtype), jax.ShapeDtypeStruct((1, LANES), F32),
                   jax.ShapeDtypeStruct((1, LANES), F32)),
        grid=(n // tr,),
        in_specs=[pl.BlockSpec((tr, LANES), lambda i: (i, BA_COL)), _rows(tr, LANES), _vec(LANES), _vec(LANES), _hbm()],
        out_specs=(pl.BlockSpec((tr, 2 * LANES), lambda i: (i, BA_COL // 2)), _vec(LANES), _vec(LANES)),
        input_output_aliases={4: 0},
        compiler_params=_params("arbitrary"),
    )(proj, dbg, a_log_l, dt_bias_l, dproj)


HALO = 8


def _halo_scratch(rs):
    return pltpu.VMEM((rs + 2 * HALO, LANES), F32)


def _stage(ref, x):
    rs = x.shape[0]
    ref[0:HALO, :] = jnp.zeros((HALO, LANES), F32)
    ref[HALO + rs:, :] = jnp.zeros((HALO, LANES), F32)
    ref[HALO:HALO + rs, :] = x


def _shifted(ref, k, rs):
    return ref[pl.ds(HALO - k, rs), :]


def _causal_conv(x, x_staged, w, width):
    acc = w[width - 1:width, :] * x
    for i in range(width - 1):
        acc = acc + w[i:i + 1, :] * _shifted(x_staged, width - 1 - i, x.shape[0])
    return acc


def _anti_causal_conv(dy, dy_staged, w, width):
    acc = w[width - 1:width, :] * dy
    for i in range(width - 1):
        acc = acc + w[i:i + 1, :] * _shifted(dy_staged, -(width - 1 - i), dy.shape[0])
    return acc


def _conv_weight_grad(dy, x, x_staged, width):
    taps = [_shifted(x_staged, width - 1 - i, x.shape[0]) for i in range(width - 1)] + [x]
    return jnp.concatenate([jnp.sum(dy * tap, axis=0, keepdims=True) for tap in taps], axis=0)


def _seq_cols(rs, col0, heads):
    return pl.BlockSpec((rs, heads * LANES), lambda j, b: (b, col0 // heads + j))


def _tap_cols(width, col0, heads):
    return pl.BlockSpec((width, heads * LANES), lambda j, b: (0, col0 // heads + j))


def _lanes_of(h):
    return slice(h * LANES, (h + 1) * LANES)


def _qkv_fwd(proj, conv_w, kind, rs):
    n = proj.shape[0]
    col0 = {"q": 0, "k": HEADS, "v": 2 * HEADS}[kind]
    hb = HEADS

    def body(p_ref, w_ref, o_ref, staged):
        for h in range(hb):
            pre = p_ref[:, _lanes_of(h)]
            _stage(staged, pre)
            c = _causal_conv(pre, staged, w_ref[:, _lanes_of(h)], GDN_CONV)
            s = c * _sigmoid(c)
            if kind != "v":
                s = s * lax.rsqrt(jnp.sum(s * s, axis=-1, keepdims=True) + EPS)
            if kind == "q":
                s = s * (HEAD_DIM ** -0.5)
            o_ref[:, _lanes_of(h)] = s

    return pl.pallas_call(
        body, name="qkv_fwd_" + kind, out_shape=jax.ShapeDtypeStruct((n, GDN_WIDTH), F32), grid=(HEADS // hb, n // rs),
        in_specs=[_seq_cols(rs, col0, hb), _tap_cols(GDN_CONV, col0, hb)],
        out_specs=_seq_cols(rs, 0, hb), scratch_shapes=[_halo_scratch(rs)], compiler_params=_params("parallel", "parallel"),
    )(proj, conv_w)


def _qkv_bwd(dy, proj, conv_w, kind, rs, dproj):
    n = proj.shape[0]
    col0 = {"q": 0, "k": HEADS, "v": 2 * HEADS}[kind]
    hb = HEADS

    def body(dy_ref, p_ref, w_ref, _, dp_ref, dw_ref, pre_staged, dc_staged):
        for h in range(hb):
            lanes = _lanes_of(h)
            pre = p_ref[:, lanes]
            w = w_ref[:, lanes]
            _stage(pre_staged, pre)
            c = _causal_conv(pre, pre_staged, w, GDN_CONV)
            sg = _sigmoid(c)
            s = c * sg
            ds = dy_ref[:, lanes]
            if kind == "q":
                ds = ds * (HEAD_DIM ** -0.5)
            if kind != "v":
                r = lax.rsqrt(jnp.sum(s * s, axis=-1, keepdims=True) + EPS)
                sh = s * r
                ds = r * (ds - sh * jnp.sum(ds * sh, axis=-1, keepdims=True))
            dc = ds * _dsilu(c, sg)
            _stage(dc_staged, dc)
            dp_ref[:, lanes] = _anti_causal_conv(dc, dc_staged, w, GDN_CONV).astype(dp_ref.dtype)
            _accumulate(dw_ref.at[:, lanes], pl.program_id(1) == 0, _conv_weight_grad(dc, pre, pre_staged, GDN_CONV))

    return pl.pallas_call(
        body, name="qkv_bwd_" + kind,
        out_shape=(jax.ShapeDtypeStruct(dproj.shape, dproj.dtype), jax.ShapeDtypeStruct((GDN_CONV, GDN_WIDTH), F32)),
        grid=(HEADS // hb, n // rs),
        in_specs=[_seq_cols(rs, 0, hb), _seq_cols(rs, col0, hb), _tap_cols(GDN_CONV, col0, hb), _hbm()],
        out_specs=(_seq_cols(rs, col0, hb), _tap_cols(GDN_CONV, 0, hb)), input_output_aliases={3: 0},
        scratch_shapes=[_halo_scratch(rs), _halo_scratch(rs)],
        compiler_params=_params("parallel", "arbitrary"),
    )(dy, proj, conv_w, dproj)


SC_COL = 4 * HEADS


def _sc_fwd(proj, conv_w, rs, cat):
    n = proj.shape[0]

    hb = 2

    def body(x_ref, b_ref, c_ref, w_ref, _, y_ref, staged):
        for h in range(hb):
            lanes = _lanes_of(h)
            u = c_ref[:, lanes] * x_ref[:, lanes]
            _stage(staged, u)
            y_ref[:, lanes] = (b_ref[:, lanes] * _causal_conv(u, staged, w_ref[:, lanes], SC_CONV)).astype(y_ref.dtype)

    return pl.pallas_call(
        body, name="sc_fwd", out_shape=jax.ShapeDtypeStruct(cat.shape, cat.dtype), grid=(HEADS // hb, n // rs),
        in_specs=[_seq_cols(rs, SC_COL, hb), _seq_cols(rs, SC_COL + 4, hb), _seq_cols(rs, SC_COL + 8, hb),
                  _tap_cols(SC_CONV, 0, hb), _hbm()],
        out_specs=_seq_cols(rs, HEADS, hb), input_output_aliases={4: 0}, scratch_shapes=[_halo_scratch(rs)],
        compiler_params=_params("parallel", "parallel"),
    )(proj, proj, proj, conv_w, cat)


def _sc_bwd(dcat, proj, conv_w, rs, dproj):
    n = proj.shape[0]
    hb = 2

    def body(dy_ref, x_ref, b_ref, c_ref, w_ref, _, dx_ref, db_ref, dc_ref, dw_ref, u_staged, dcv_staged):
        for h in range(hb):
            lanes = _lanes_of(h)
            w = w_ref[:, lanes]
            x = x_ref[:, lanes]
            cc = c_ref[:, lanes]
            u = cc * x
            _stage(u_staged, u)
            dy = dy_ref[:, lanes]
            db_ref[:, lanes] = (dy * _causal_conv(u, u_staged, w, SC_CONV)).astype(db_ref.dtype)
            dcv = dy * b_ref[:, lanes]
            _stage(dcv_staged, dcv)
            du = _anti_causal_conv(dcv, dcv_staged, w, SC_CONV)
            dx_ref[:, lanes] = (du * cc).astype(dx_ref.dtype)
            dc_ref[:, lanes] = (du * x).astype(dc_ref.dtype)
            _accumulate(dw_ref.at[:, lanes], pl.program_id(1) == 0, _conv_weight_grad(dcv, u, u_staged, SC_CONV))

    piece = jax.ShapeDtypeStruct((n, SC_WIDTH), MXU_DTYPE)
    return pl.pallas_call(
        body, name="sc_bwd",
        out_shape=(jax.ShapeDtypeStruct(dproj.shape, dproj.dtype), piece, piece, jax.ShapeDtypeStruct((SC_CONV, SC_WIDTH), F32)),
        grid=(HEADS // hb, n // rs),
        in_specs=[_seq_cols(rs, HEADS, hb), _seq_cols(rs, SC_COL, hb), _seq_cols(rs, SC_COL + 4, hb),
                  _seq_cols(rs, SC_COL + 8, hb), _tap_cols(SC_CONV, 0, hb), _hbm()],
        out_specs=(_seq_cols(rs, SC_COL, hb), _seq_cols(rs, 0, hb), _seq_cols(rs, 0, hb), _tap_cols(SC_CONV, 0, hb)),
        input_output_aliases={5: 0},
        scratch_shapes=[_halo_scratch(rs), _halo_scratch(rs)],
        compiler_params=_params("parallel", "arbitrary"),
    )(dcat, proj, proj, proj, conv_w, dproj)


Z_COL = 3 * HEADS


def _gate_fwd(o, proj, gdn_norm, rs):
    n = proj.shape[0]

    hb = HEADS

    def body(o_ref, z_ref, w_ref, y_ref):
        for h in range(hb):
            lanes = _lanes_of(h)
            z = z_ref[:, lanes]
            y_ref[:, lanes] = (_rms_apply(o_ref[:, lanes], w_ref[...]) * z * _sigmoid(z)).astype(y_ref.dtype)

    return pl.pallas_call(
        body, name="gate_fwd", out_shape=jax.ShapeDtypeStruct((n, D_MODEL), MXU_DTYPE), grid=(HEADS // hb, n // rs),
        in_specs=[_seq_cols(rs, 0, hb), _seq_cols(rs, Z_COL, hb), pl.BlockSpec((1, LANES), lambda j, b: (0, 0))],
        out_specs=_seq_cols(rs, 0, hb), compiler_params=_params("parallel", "parallel"),
    )(o, proj, gdn_norm)


def _gate_bwd(dcat, o, proj, gdn_norm, rs):
    n = proj.shape[0]
    hb = 2

    def body(dy_ref, o_ref, z_ref, w_ref, do_ref, dz_ref, dw_ref):
        w = w_ref[...]
        dw_step = jnp.zeros((1, LANES), F32)
        for h in range(hb):
            lanes = _lanes_of(h)
            z = z_ref[:, lanes]
            o = o_ref[:, lanes]
            dy = dy_ref[:, lanes]
            s = _sigmoid(z)
            dz_ref[:, lanes] = (dy * _rms_apply(o, w) * _dsilu(z, s)).astype(dz_ref.dtype)
            do, dw = _rms_bwd(o, w, dy * z * s)
            do_ref[:, lanes] = do
            dw_step = dw_step + dw
        _accumulate(dw_ref, jnp.logical_and(pl.program_id(0) == 0, pl.program_id(1) == 0), dw_step)

    return pl.pallas_call(
        body, name="gate_bwd",
        out_shape=(jax.ShapeDtypeStruct((n, GDN_WIDTH), F32), jax.ShapeDtypeStruct((n, IN_PAD), MXU_DTYPE),
                   jax.ShapeDtypeStruct((1, LANES), F32)),
        grid=(HEADS // hb, n // rs),
        in_specs=[_seq_cols(rs, 0, hb), _seq_cols(rs, 0, hb), _seq_cols(rs, Z_COL, hb), pl.BlockSpec((1, LANES), lambda j, b: (0, 0))],
        out_specs=(_seq_cols(rs, 0, hb), _seq_cols(rs, Z_COL, hb), pl.BlockSpec((1, LANES), lambda j, b: (0, 0))),
        compiler_params=_params("arbitrary", "arbitrary"),
    )(dcat, o, proj, gdn_norm)


def _dot(a, b):
    return jnp.dot(a.astype(MXU_DTYPE), b.astype(MXU_DTYPE), preferred_element_type=F32)


def _dot_nt(a, b):
    return lax.dot_general(a.astype(MXU_DTYPE), b.astype(MXU_DTYPE), (((1,), (1,)), ((), ())),
                           preferred_element_type=F32)


def _dot_tn(a, b):
    return lax.dot_general(a.astype(MXU_DTYPE), b.astype(MXU_DTYPE), (((0,), (0,)), ((), ())),
                           preferred_element_type=F32)


def _split(x):
    hi = x.astype(MXU_DTYPE)
    return hi, (x - hi.astype(F32)).astype(MXU_DTYPE)


def _dot_split(a, b):
    mm = functools.partial(jnp.dot, preferred_element_type=F32)
    return mm(a[0], b[0]) + (mm(a[0], b[1]) + mm(a[1], b[0]))


def _unit_lower_inverses(mats, eye):
    inv = [eye - a for a in mats]
    power = [_split(a) for a in mats]
    square = [_dot_split(p, p) for p in power]
    inv = [i + _dot_split(_split(i), _split(s)) for i, s in zip(inv, square)]
    span = 4
    while span < CHUNK:
        square = [_dot(s, s) for s in square]
        inv = [i + _dot(i, s) for i, s in zip(inv, square)]
        span *= 2
    return inv


def _chunk_masks():
    ii = lax.broadcasted_iota(jnp.int32, (CHUNK, CHUNK), 0)
    jj = lax.broadcasted_iota(jnp.int32, (CHUNK, CHUNK), 1)
    return ii, jj


def _chunk_decay(g_col, ii, jj):
    incl = ii >= jj
    g_row = jnp.sum(jnp.where(ii == jj, g_col, 0.0), axis=0, keepdims=True)
    gc_col = jnp.sum(jnp.where(incl, g_row, 0.0), axis=1, keepdims=True)
    gc_row = jnp.sum(jnp.where(ii <= jj, g_col, 0.0), axis=0, keepdims=True)
    g_total = jnp.sum(g_row, axis=1, keepdims=True)
    decay = jnp.where(incl, jnp.exp(jnp.where(incl, gc_col - gc_row, 0.0)), 0.0)
    return gc_col, g_total, decay


def _gdn_segments(rs, candidates):
    chunks = rs // CHUNK
    seg_chunks = _pick(chunks, candidates)
    return chunks, seg_chunks, chunks // seg_chunks


def _head_lanes(h):
    return slice(h * HEAD_DIM, (h + 1) * HEAD_DIM)


def _gdn_fwd(q, k, v, bg, rs, pieces):
    n = q.shape[0]
    batch = n // rs
    chunks, seg_chunks, segs = _gdn_segments(rs, (11, 8, 4, 2))
    seg_rows = seg_chunks * CHUNK
    chains = [(b, h) for b in range(batch) for h in range(HEADS)]
    each = lambda f, *lists: [f(*args) for args in zip(*lists)]
    count = len(pieces)

    def body(q_ref, k_ref, v_ref, bg_ref, *rest):
        w_refs, (o_ref, s_ref, t_ref), out_refs = rest[:count], rest[count:count + 3], rest[count + 3:2 * count + 3]
        state_ref, send_sems, recv_sems = rest[2 * count + 3:]
        gather = _gather_copies(w_refs, out_refs, send_sems, recv_sems)

        @pl.when(pl.program_id(0) == 0)
        def _():
            state_ref[...] = jnp.zeros_like(state_ref)
            for cp in gather[0]:
                cp.start()

        ii, jj = _chunk_masks()
        incl = ii >= jj
        eye = (ii == jj).astype(F32)

        def chunk(c, carry):
            rows = pl.ds(pl.multiple_of(c * CHUNK, CHUNK), CHUNK)
            bgc = [bg_ref[b, rows, :] for b in range(batch)]
            qc = [q_ref[b, rows, _head_lanes(h)] for b, h in chains]
            kc = [k_ref[b, rows, _head_lanes(h)] for b, h in chains]
            vc = [v_ref[b, rows, _head_lanes(h)] for b, h in chains]
            beta = [bgc[b][:, h:h + 1] for b, h in chains]
            state = [state_ref[b, h] for b, h in chains]
            dec = [_chunk_decay(bgc[b][:, HEADS + h:HEADS + h + 1], ii, jj) for b, h in chains]
            gc_col, g_total, decay = ([d[i] for d in dec] for i in range(3))
            kb = each(lambda x, y: x * y, kc, beta)
            a = each(lambda x, y, d: jnp.where(ii > jj, _dot_nt(x, y) * d, 0.0), kb, kc, decay)
            t_inv = _unit_lower_inverses(a, eye)
            eg = [jnp.exp(g) for g in gc_col]
            u = each(lambda t, x, y: _dot(t, x * y), t_inv, vc, beta)
            w = each(lambda t, x, e: _dot(t, x * e), t_inv, kb, eg)
            qk = each(lambda x, y, d: jnp.where(incl, _dot_nt(x, y) * d, 0.0), qc, kc, decay)
            v_new = each(lambda x, y, s: x - _dot(y, s), u, w, state)
            o = each(lambda x, e, s, m, vn: _dot(x * e, s) + _dot(m, vn), qc, eg, state, qk, v_new)
            new_state = each(lambda s, gt, x, g, vn: s * jnp.exp(gt) + _dot_tn(x * jnp.exp(gt - g), vn),
                             state, g_total, kc, gc_col, v_new)
            for i, (b, h) in enumerate(chains):
                s_ref[b, h, c] = state[i]
                t_ref[b, h, c] = t_inv[i]
                o_ref[b, rows, _head_lanes(h)] = o[i]
                state_ref[b, h] = new_state[i]
            return carry

        lax.fori_loop(0, seg_chunks, chunk, 0)

        @pl.when(pl.program_id(0) == segs - 1)
        def _():
            _gather_finish(gather)

    rows_spec = lambda width: pl.BlockSpec((batch, seg_rows, width), lambda s: (0, s, 0))
    per_chunk = lambda r, c: pl.BlockSpec((batch, HEADS, seg_chunks, r, c), lambda s: (0, 0, s, 0, 0))
    as_seqs = lambda a: a.reshape(batch, rs, a.shape[-1])
    sems = GATHER_SEMS * count
    o, states, t_invs, *gathered = pl.pallas_call(
        body, name="gdn_fwd",
        out_shape=(jax.ShapeDtypeStruct((batch, rs, GDN_WIDTH), F32),
                   jax.ShapeDtypeStruct((batch, HEADS, chunks, HEAD_DIM, HEAD_DIM), F32),
                   jax.ShapeDtypeStruct((batch, HEADS, chunks, CHUNK, CHUNK), F32))
        + tuple(jax.ShapeDtypeStruct((N_CHIPS,) + p.shape, p.dtype) for p in pieces),
        grid=(segs,),
        in_specs=[rows_spec(GDN_WIDTH), rows_spec(GDN_WIDTH), rows_spec(GDN_WIDTH), rows_spec(LANES)] + [_hbm()] * count,
        out_specs=(rows_spec(GDN_WIDTH), per_chunk(HEAD_DIM, HEAD_DIM), per_chunk(CHUNK, CHUNK)) + (_hbm(),) * count,
        scratch_shapes=[pltpu.VMEM((batch, HEADS, HEAD_DIM, HEAD_DIM), F32), pltpu.SemaphoreType.DMA((sems,)),
                        pltpu.SemaphoreType.DMA((sems,))],
        compiler_params=_params("arbitrary"),
    )(as_seqs(q), as_seqs(k), as_seqs(v), as_seqs(bg), *pieces)
    return o.reshape(n, GDN_WIDTH), states, t_invs, gathered


def _gdn_bwd(do, q, k, v, bg, states, t_invs, rs, parts):
    n = q.shape[0]
    batch = n // rs
    chunks, seg_chunks, segs = _gdn_segments(rs, (3, 4, 2))
    seg_rows = seg_chunks * CHUNK
    chains = [(b, h) for b in range(batch) for h in range(HEADS)]
    each = lambda f, *lists: [f(*args) for args in zip(*lists)]
    count = len(parts)

    def body(do_ref, q_ref, k_ref, v_ref, bg_ref, s_ref, t_ref, *rest):
        p_refs, (dq_ref, dk_ref, dv_ref, dbg_ref), got_refs = rest[:count], rest[count:count + 4], rest[count + 4:2 * count + 4]
        dstate_ref, send_sems, recv_sems = rest[2 * count + 4:]
        exchange = _chip_copies(p_refs, got_refs, send_sems, recv_sems)

        @pl.when(pl.program_id(0) == 0)
        def _():
            dstate_ref[...] = jnp.zeros_like(dstate_ref)
            for cp in exchange:
                cp.start()

        ii, jj = _chunk_masks()
        incl = ii >= jj
        strict = ii > jj
        lane = lax.broadcasted_iota(jnp.int32, (1, LANES), 1)

        def rowsum(x):
            return jnp.sum(x, axis=1, keepdims=True)

        def total(x):
            return jnp.sum(rowsum(x), axis=0, keepdims=True)

        def chunk(step, carry):
            c = seg_chunks - 1 - step
            rows = pl.ds(pl.multiple_of(c * CHUNK, CHUNK), CHUNK)
            bgc = [bg_ref[b, rows, :] for b in range(batch)]
            qc = [q_ref[b, rows, _head_lanes(h)] for b, h in chains]
            kc = [k_ref[b, rows, _head_lanes(h)] for b, h in chains]
            vc = [v_ref[b, rows, _head_lanes(h)] for b, h in chains]
            doc = [do_ref[b, rows, _head_lanes(h)] for b, h in chains]
            beta = [bgc[b][:, h:h + 1] for b, h in chains]
            state = [s_ref[b, h, c] for b, h in chains]
            t_inv = [t_ref[b, h, c] for b, h in chains]
            d_state = [dstate_ref[b, h] for b, h in chains]
            dec = [_chunk_decay(bgc[b][:, HEADS + h:HEADS + h + 1], ii, jj) for b, h in chains]
            gc_col, g_total, decay = ([d[i] for d in dec] for i in range(3))
            kb = each(lambda x, y: x * y, kc, beta)
            vb = each(lambda x, y: x * y, vc, beta)
            eg = [jnp.exp(g) for g in gc_col]
            kbg = each(lambda x, y: x * y, kb, eg)
            a = each(lambda x, y, d: jnp.where(strict, _dot_nt(x, y) * d, 0.0), kb, kc, decay)
            qk = each(lambda x, y, d: jnp.where(incl, _dot_nt(x, y) * d, 0.0), qc, kc, decay)
            w = each(_dot, t_inv, kbg)
            u = each(_dot, t_inv, vb)
            q_dec = each(lambda x, y: x * y, qc, eg)
            ek = each(lambda gt, g: jnp.exp(gt - g), g_total, gc_col)
            k_dec = each(lambda x, y: x * y, kc, ek)
            g_last = [jnp.exp(gt) for gt in g_total]
            v_new = each(lambda x, y, s: x - _dot(y, s), u, w, state)
            dv_new = each(lambda m, d, x, ds: _dot_tn(m, d) + _dot(x, ds), qk, doc, k_dec, d_state)
            dqk = each(lambda d, vn: jnp.where(incl, _dot_nt(d, vn), 0.0), doc, v_new)
            dq_dec = each(_dot_nt, doc, state)
            dk_dec = each(_dot_nt, v_new, d_state)
            dg_last = each(lambda s, ds: total(s * ds), state, d_state)
            new_d_state = each(lambda x, d, gl, ds, y, dvn: _dot_tn(x, d) + gl * ds - _dot_tn(y, dvn),
                               q_dec, doc, g_last, d_state, w, dv_new)
            dw = each(lambda dvn, s: -_dot_nt(dvn, s), dv_new, state)
            dt = each(lambda dvn, x, y, z: _dot_nt(dvn, x) + _dot_nt(y, z), dv_new, vb, dw, kbg)
            dvb = each(_dot_tn, t_inv, dv_new)
            dkbg = each(_dot_tn, t_inv, dw)
            t_dt = each(_dot_tn, t_inv, dt)
            da = each(lambda x, t: -jnp.where(strict, _dot_nt(x, t), 0.0), t_dt, t_inv)
            dm_a = each(lambda x, y: x * y, da, decay)
            dm_qk = each(lambda x, y: x * y, dqk, decay)
            e = each(lambda x, y, z, t: x * y + z * t, da, a, dqk, qk)
            dkb = each(lambda m, x, y, z: _dot(m, x) + y * z, dm_a, kc, dkbg, eg)
            dk = each(lambda m, x, m2, y, z, t, p, bt: _dot_tn(m, x) + _dot_tn(m2, y) + z * t + p * bt,
                      dm_a, kb, dm_qk, qc, dk_dec, ek, dkb, beta)
            dq = each(lambda m, x, y, z: _dot(m, x) + y * z, dm_qk, kc, dq_dec, eg)
            dbeta = each(lambda x, y, z, t: rowsum(x * y + z * t), dkb, kc, dvb, vc)
            dgc = each(lambda x, p, pd, r, rd, s, sd: rowsum(x) - rowsum(jnp.where(ii == jj, jnp.sum(x, axis=0, keepdims=True), 0.0))
                       + rowsum(p * pd - r * rd + s * sd), e, dq_dec, q_dec, dk_dec, k_dec, dkbg, kbg)
            d_total = each(lambda r, rd, x, gl: total(r * rd) + x * gl, dk_dec, k_dec, dg_last, g_last)
            dg = each(lambda x, t: rowsum(jnp.where(jj >= ii, jnp.sum(jnp.where(ii == jj, x, 0.0), axis=0, keepdims=True), 0.0)) + t,
                      dgc, d_total)
            dbg = [jnp.zeros((CHUNK, LANES), F32) for _ in range(batch)]
            for i, (b, h) in enumerate(chains):
                dstate_ref[b, h] = new_d_state[i]
                dk_ref[b, rows, _head_lanes(h)] = dk[i]
                dq_ref[b, rows, _head_lanes(h)] = dq[i]
                dv_ref[b, rows, _head_lanes(h)] = dvb[i] * beta[i]
                dbg[b] = dbg[b] + jnp.where(lane == h, dbeta[i], 0.0) + jnp.where(lane == HEADS + h, dg[i], 0.0)
            for b in range(batch):
                dbg_ref[b, rows, :] = dbg[b]
            return carry

        lax.fori_loop(0, seg_chunks, chunk, 0)

        @pl.when(pl.program_id(0) == segs - 1)
        def _():
            for cp in exchange:
                cp.wait_recv()
            for cp in exchange:
                cp.wait_send()

    rows_spec = lambda width: pl.BlockSpec((batch, seg_rows, width), lambda s: (0, segs - 1 - s, 0))
    per_chunk = lambda r, c: pl.BlockSpec((batch, HEADS, seg_chunks, r, c), lambda s: (0, 0, segs - 1 - s, 0, 0))
    as_seqs = lambda a: a.reshape(batch, rs, a.shape[-1])
    grad = jax.ShapeDtypeStruct((batch, rs, GDN_WIDTH), F32)
    wide = rows_spec(GDN_WIDTH)
    dq, dk, dv, dbg, *got = pl.pallas_call(
        body, name="gdn_bwd",
        out_shape=(grad, grad, grad, jax.ShapeDtypeStruct((batch, rs, LANES), F32))
        + tuple(jax.ShapeDtypeStruct((3,) + p.shape[1:], p.dtype) for p in parts),
        grid=(segs,),
        in_specs=[wide, wide, wide, wide, rows_spec(LANES), per_chunk(HEAD_DIM, HEAD_DIM), per_chunk(CHUNK, CHUNK)]
        + [_hbm()] * count,
        out_specs=(wide, wide, wide, rows_spec(LANES)) + (_hbm(),) * count,
        scratch_shapes=[pltpu.VMEM((batch, HEADS, HEAD_DIM, HEAD_DIM), F32), pltpu.SemaphoreType.DMA((3 * count,)),
                        pltpu.SemaphoreType.DMA((3 * count,))],
        compiler_params=_params("arbitrary"),
    )(as_seqs(do), as_seqs(q), as_seqs(k), as_seqs(v), as_seqs(bg), states, t_invs, *parts)
    return dq.reshape(n, GDN_WIDTH), dk.reshape(n, GDN_WIDTH), dv.reshape(n, GDN_WIDTH), dbg.reshape(n, LANES), got


def _lane_vec(vals, offset):
    k = vals.shape[1]
    return jnp.pad(vals, ((0, 0), (offset, LANES - offset - k)))


LATER = ("w_out", "w_gate", "w_up", "w_down")


def _halves(a):
    return a.reshape(a.shape[:-2] + (2, a.shape[-2] // 2, a.shape[-1]))


def _local_step(x, target, meta, norms, w_in_shard, conv_qkv, a_log, dt_bias, gdn_norm, conv_sc, later_shards, core_arg):
    batch, seq, d = x.shape
    tokens = N_META + seq
    pad_rows = (-tokens) % CHUNK
    rs = tokens + pad_rows
    x_offset = pad_rows + N_META
    n = batch * rs
    w_mix_pre, w_mix_post, w_ffn_pre, w_ffn_post = norms

    head = jnp.concatenate([jnp.zeros((pad_rows, d), F32), meta], axis=0)
    a_log_l = _lane_vec(a_log, HEADS)
    dt_bias_l = _lane_vec(dt_bias, HEADS)

    h0, u1, w_in_all = _embed(x, head, w_mix_pre, w_in_shard, rs)
    w_in_t = _in_to_kernel_order(w_in_all.reshape(N_CHIPS, -1, d))
    proj = _mm(u1, w_in_t, "nt", F32, "mm_proj")
    q = _qkv_fwd(proj, conv_qkv, "q", rs)
    k = _qkv_fwd(proj, conv_qkv, "k", rs)
    v = _qkv_fwd(proj, conv_qkv, "v", rs)
    bg = _gates_fwd(proj, a_log_l, dt_bias_l, rs, pad_rows)
    o, states, t_invs, gathered = _gdn_fwd(q, k, v, bg, rs, later_shards[:3])
    w_out, w_gate_t, w_up_t = (a.reshape(-1, d) for a in gathered)
    cat = _sc_fwd(proj, conv_sc, rs, _gate_fwd(o, proj, gdn_norm, rs))
    mix = _mm(cat, w_out, "nn", F32, "mm_mix")
    h1, u2 = _mix_residual(h0, mix, w_mix_post, w_ffn_pre)
    gate, up, act, w_down = _swiglu_fwd(u2, w_gate_t, w_up_t, later_shards[3])
    w_down = w_down.reshape(-1, d)
    ffn = _mm(act, w_down, "nn", F32, "mm_down")

    dh2, dffn, d_ffn_post, sq = _loss_head(h1, ffn, w_ffn_post, target, rs, x_offset)
    d_w_down = _mm(act, dffn, "tn", F32, "mm_dw_down")
    dgate, dup = _swiglu_bwd(dffn, w_down, gate, up)
    d_w_gate_t = _mm(dgate, u2, "tn", F32, "mm_dw_gate")
    d_w_up_t = _mm(dup, u2, "tn", F32, "mm_dw_up")
    du2 = _mm(dup, w_up_t, "nn", F32, "mm_du2_up", init=_mm(dgate, w_gate_t, "nn", F32, "mm_du2_gate"))
    by_chip = [_halves(g.reshape(N_CHIPS, -1, d)) for g in (d_w_gate_t, d_w_up_t, d_w_down)]
    dh1, dmix, d_ffn_pre, d_mix_post, got_sibling = _mid_bwd(h1, mix, w_mix_post, w_ffn_pre, dh2, du2, by_chip)
    dcat = _mm(dmix, w_out, "nt", F32, "mm_dcat")
    d_w_out = _halves(_mm(cat, dmix, "tn", F32, "mm_dw_out").reshape(N_CHIPS, -1, d))
    by_chip, got_sibling = [d_w_out] + by_chip, list(_exchange_siblings([d_w_out])) + got_sibling
    sums = [_add_sibling(a, b, core_arg, name) for name, a, b in zip(LATER, by_chip, got_sibling)]
    do, dproj, d_gdn_norm = _gate_bwd(dcat, o, proj, gdn_norm, rs)
    dproj, dscb, dscc, d_conv_sc = _sc_bwd(dcat, proj, conv_sc, rs, dproj)
    dq, dk, dv, dbg, got_chips = _gdn_bwd(do, q, k, v, bg, states, t_invs, rs, [send for _, send in sums[:3]])
    dproj, dwq = _qkv_bwd(dq, proj, conv_qkv, "q", rs, dproj)
    dproj, dwk = _qkv_bwd(dk, proj, conv_qkv, "k", rs, dproj)
    dproj, dwv = _qkv_bwd(dv, proj, conv_qkv, "v", rs, dproj)
    d_conv_qkv = jnp.concatenate([dwq, dwk, dwv], axis=1)
    dproj, d_a_log_l, d_dt_bias_l = _gates_bwd(proj, dbg, a_log_l, dt_bias_l, rs, pad_rows, dproj)
    dproj = lax.dynamic_update_slice(dproj, dscb, (0, (SC_COL + HEADS) * LANES))
    dproj = lax.dynamic_update_slice(dproj, dscc, (0, (SC_COL + 2 * HEADS) * LANES))
    d_w_in_t, got_down = _mm(dproj, u1, "tn", F32, "mm_dw_in", exchange=[sums[3][1]])
    got_chips.append(got_down)
    g_in = _halves(_in_from_kernel_order(d_w_in_t))
    sums.insert(0, _add_sibling(g_in, _exchange_siblings([g_in])[0], core_arg, "w_in"))
    du1, got_in = _mm(dproj, w_in_t, "nn", F32, "mm_du1", exchange=[sums[0][1]])
    got_chips.insert(0, got_in)
    grad_x, d_meta, d_mix_pre = _in_bwd(h0, w_mix_pre, dh1, du1, rs, pad_rows, x_offset)

    grads = dict(
        meta_tokens=d_meta,
        mix_pre_norm=d_mix_pre, mix_post_norm=d_mix_post, ffn_pre_norm=d_ffn_pre, ffn_post_norm=d_ffn_post,
        conv_qkv=d_conv_qkv,
        a_log=d_a_log_l[:, HEADS:2 * HEADS], dt_bias=d_dt_bias_l[:, HEADS:2 * HEADS],
        gdn_norm=d_gdn_norm, conv_sc=d_conv_sc,
    )
    return sq, grad_x, grads, [(part, got) for (part, _), got in zip(sums, got_chips)]


MATRICES = ("w_in", "w_out", "w_gate", "w_up", "w_down")
IN_SHARD = IN_WIDTH // N_CHIPS
IN_SHARD_PAD = 928


IN_SEGMENTS = ((0, 0, 4 * GDN_WIDTH), (4 * GDN_WIDTH, IN_WIDTH - 2 * HEADS, 2 * HEADS),
               (4 * GDN_WIDTH + 2 * HEADS, 4 * GDN_WIDTH, 3 * SC_WIDTH))
SUBLANES = 8
PACKED_ROWS = 16


def _in_to_kernel_order(by_chip):
    d = by_chip.shape[-1]
    tl = _pick(d, (256, 128))
    runs = []
    for ref0, ker0, count in IN_SEGMENTS:
        row = ref0
        while row < ref0 + count:
            chip, at = divmod(row, IN_SHARD)
            take = min(ref0 + count - row, IN_SHARD - at)
            runs.append((ker0 + row - ref0, take, chip * IN_SHARD_PAD + at))
            row += take

    def body(w_ref, o_ref):
        o_ref[...] = jnp.zeros_like(o_ref)
        for out0, rows, src0 in runs:
            a0 = out0 // PACKED_ROWS * PACKED_ROWS
            a1 = -(-(out0 + rows) // PACKED_ROWS) * PACKED_ROWS
            window = w_ref[pl.ds(src0 - (out0 - a0), a1 - a0), :]
            row = a0 + lax.broadcasted_iota(jnp.int32, (a1 - a0, 1), 0)
            keep = jnp.logical_and(row >= out0, row < out0 + rows)
            o_ref[a0:a1, :] = jnp.where(keep, window, o_ref[a0:a1, :])

    return pl.pallas_call(
        body, name="in_to_kernel_order", out_shape=jax.ShapeDtypeStruct((IN_PAD, d), by_chip.dtype), grid=(d // tl,),
        in_specs=[pl.BlockSpec((N_CHIPS * IN_SHARD_PAD, tl), lambda j: (0, j))],
        out_specs=pl.BlockSpec((IN_PAD, tl), lambda j: (0, j)),
        compiler_params=_params("parallel"),
    )(by_chip.reshape(N_CHIPS * IN_SHARD_PAD, d))


def _in_from_kernel_order(g_t):
    d = g_t.shape[-1]
    tl = _pick(d, (256, 128))

    def body(g_ref, o_ref):
        row = lax.broadcasted_iota(jnp.int32, (IN_SHARD_PAD, 1), 0)
        for chip in range(N_CHIPS):
            first = chip * IN_SHARD
            runs = []
            for ref0, ker0, count in IN_SEGMENTS:
                lo, hi = max(ref0, first), min(ref0 + count, first + IN_SHARD)
                if lo < hi:
                    runs.append((lo - first, hi - lo, ker0 + lo - ref0))
            val = jnp.zeros((IN_SHARD_PAD, tl), F32)
            patches = []
            for out0, rows, src0 in runs:
                start = src0 - out0
                if 0 <= start <= IN_PAD - IN_SHARD_PAD:
                    window = g_ref[pl.ds(start, IN_SHARD_PAD), :]
                    val = jnp.where(jnp.logical_and(row >= out0, row < out0 + rows), window, val)
                else:
                    patches.append((out0, rows, src0))
            o_ref[chip] = val
            for out0, rows, src0 in patches:
                a0 = out0 // SUBLANES * SUBLANES
                a1 = -(-(out0 + rows) // SUBLANES) * SUBLANES
                window = g_ref[pl.ds(src0 - (out0 - a0), a1 - a0), :]
                keep = jnp.logical_and(row[a0:a1] >= out0, row[a0:a1] < out0 + rows)
                o_ref[chip, a0:a1, :] = jnp.where(keep, window, o_ref[chip, a0:a1, :])

    return pl.pallas_call(
        body, name="in_from_kernel_order", out_shape=jax.ShapeDtypeStruct((N_CHIPS, IN_SHARD_PAD, d), F32), grid=(d // tl,),
        in_specs=[pl.BlockSpec((IN_PAD, tl), lambda j: (0, j))],
        out_specs=pl.BlockSpec((N_CHIPS, IN_SHARD_PAD, tl), lambda j: (0, 0, j)),
        compiler_params=_params("parallel"),
    )(g_t)


PACK_LANES = 3 * GDN_WIDTH
PACKED = dict(mix_pre_norm=(0, 1, 0, D_MODEL), mix_post_norm=(1, 1, 0, D_MODEL), ffn_pre_norm=(2, 1, 0, D_MODEL),
              ffn_post_norm=(3, 1, 0, D_MODEL), a_log=(4, 1, 0, HEADS), dt_bias=(5, 1, 0, HEADS), loss=(6, 1, 0, 1),
              gdn_norm=(7, 1, 0, HEAD_DIM), conv_qkv=(8, GDN_CONV, 0, 3 * GDN_WIDTH), conv_sc=(0, SC_CONV, D_MODEL, SC_WIDTH),
              meta_tokens=(16, N_META, 0, D_MODEL))
PACK_ROWS = 32
SHARDED_SMALL = ("conv_qkv", "conv_sc", "meta_tokens")


def _pack_small(values):
    names = list(PACKED)

    def body(*refs):
        out_ref = refs[-1]
        out_ref[...] = jnp.zeros_like(out_ref)
        for name, ref in zip(names, refs):
            row, rows, lane0, lanes = PACKED[name]
            out_ref[row:row + rows, lane0:lane0 + lanes] = ref[...]

    return pl.pallas_call(body, name="pack_small", out_shape=jax.ShapeDtypeStruct((PACK_ROWS, PACK_LANES), F32))(
        *[values[name] for name in names])


def _sum_devices(packed_all, chip):
    names = list(PACKED)

    def body(chip_ref, all_ref, *rest):
        shard_refs, out_refs = rest[:len(SHARDED_SMALL)], rest[len(SHARDED_SMALL):]

        def total(ref, rows, lanes):
            acc = ref[0, rows, lanes]
            for k in range(1, 8):
                acc = acc + ref[k, rows, lanes]
            return acc

        for name, out in zip(names, out_refs):
            row, rows, lane0, lanes = PACKED[name]
            if name in SHARDED_SMALL:
                out[...] = total(shard_refs[SHARDED_SMALL.index(name)], slice(0, rows), slice(None))
            else:
                out[...] = total(all_ref, slice(row, row + rows), slice(lane0, lane0 + lanes))

    def shard_spec(name):
        row, rows, lane0, lanes = PACKED[name]
        height, width = max(rows, 8), lanes // N_CHIPS
        assert row % height == 0 and lane0 % width == 0
        return pl.BlockSpec((8, height, width), lambda i, chip_ref: (0, row // height, lane0 // width + chip_ref[0]))

    def out_shape(name):
        _, rows, _, lanes = PACKED[name]
        return jax.ShapeDtypeStruct((rows, lanes // N_CHIPS if name in SHARDED_SMALL else lanes), F32)

    whole = lambda shape: pl.BlockSpec(shape, lambda i, chip_ref: (0,) * len(shape))
    outs = pl.pallas_call(
        body, name="sum_devices", out_shape=tuple(out_shape(n) for n in names),
        grid_spec=pltpu.PrefetchScalarGridSpec(
            num_scalar_prefetch=1, grid=(1,),
            in_specs=[whole(packed_all.shape)] + [shard_spec(n) for n in SHARDED_SMALL],
            out_specs=tuple(whole(out_shape(n).shape) for n in names)),
    )(chip, packed_all, *[packed_all] * len(SHARDED_SMALL))
    return dict(zip(names, outs))


def _hbm():
    return pl.BlockSpec(memory_space=pl.ANY)


def _place():
    x, y, c = lax.axis_index("x"), lax.axis_index("y"), lax.axis_index("c")
    chips = ((1 - x, y), (x, 1 - y), (1 - x, 1 - y))
    return x, y, c, chips


def _remote(src, dst, send_sems, recv_sems, k, to):
    return pltpu.make_async_remote_copy(src_ref=src, dst_ref=dst, send_sem=send_sems.at[k], recv_sem=recv_sems.at[k],
                                        device_id=to, device_id_type=MESH)


GATHER_SEMS = 7


def _gather_copies(w_refs, out_refs, send_sems, recv_sems):
    x, y, c, chips = _place()
    mine = 2 * x + y
    sibling = (x, y, 1 - c)
    copy = functools.partial(_remote, send_sems=send_sems, recv_sems=recv_sems)
    direct, landed, passing, from_sibling = [], [], [], []
    for i, (w, o) in enumerate(zip(w_refs, out_refs)):
        k = GATHER_SEMS * i
        direct.append(copy(w, o.at[mine], k=k, to=sibling))
        from_sibling.append(copy(w, o.at[mine], k=k, to=sibling))
        for j, (cx, cy) in enumerate(chips):
            theirs = 2 * cx + cy
            direct.append(copy(w.at[c], o.at[mine, c], k=k + 1 + j, to=(cx, cy, c)))
            landed.append(copy(w.at[c], o.at[theirs, c], k=k + 1 + j, to=sibling))
            passing.append(copy(o.at[theirs, c], o.at[theirs, c], k=k + 4 + j, to=sibling))
            from_sibling.append(copy(w.at[c], o.at[theirs, 1 - c], k=k + 4 + j, to=sibling))
    return direct, landed, passing, from_sibling


def _gather_finish(copies):
    direct, landed, passing, from_sibling = copies
    for arrival, forward in zip(landed, passing):
        arrival.wait_recv()
        forward.start()
    for arrival in from_sibling:
        arrival.wait_recv()
    for cp in direct + passing:
        cp.wait_send()


def _gather_weights(pieces, smalls):
    count, extra = len(pieces), len(smalls)
    total = count + extra

    def body(*refs):
        w_refs, s_refs = refs[:count], refs[count:total]
        out_refs, sall_refs = refs[total:total + count], refs[total + count:2 * total]
        send_sems, recv_sems, local_sems = refs[2 * total:]
        x, y, c, chips = _place()
        mine = 2 * x + y
        own = [pltpu.make_async_copy(s, sall.at[mine], local_sems.at[i]) for i, (s, sall) in enumerate(zip(s_refs, sall_refs))]
        small = [_remote(s, sall.at[mine], send_sems, recv_sems, GATHER_SEMS * count + 3 * i + j, (cx, cy, c))
                 for i, (s, sall) in enumerate(zip(s_refs, sall_refs)) for j, (cx, cy) in enumerate(chips)]
        copies = _gather_copies(w_refs, out_refs, send_sems, recv_sems)
        for cp in own + small + copies[0]:
            cp.start()
        _gather_finish(copies)
        for cp in small:
            cp.wait_recv()
        for cp in small:
            cp.wait_send()
        for cp in own:
            cp.wait()

    sems = GATHER_SEMS * count + 3 * extra
    return pl.pallas_call(
        body, name="gather_weights",
        out_shape=tuple(jax.ShapeDtypeStruct((N_CHIPS,) + p.shape, p.dtype) for p in list(pieces) + list(smalls)),
        in_specs=[_hbm()] * total, out_specs=(_hbm(),) * total,
        scratch_shapes=[pltpu.SemaphoreType.DMA((sems,)), pltpu.SemaphoreType.DMA((sems,)), pltpu.SemaphoreType.DMA((extra,))],
    )(*pieces, *smalls)


def _sibling_copies(g_refs, got_refs, send_sems, recv_sems):
    x, y, c, _ = _place()
    return [_remote(g.at[:, 1 - c], got, send_sems, recv_sems, i, (x, y, 1 - c)) for i, (g, got) in enumerate(zip(g_refs, got_refs))]


def _exchange_siblings(grads):
    count = len(grads)

    def body(*refs):
        copies = _sibling_copies(refs[:count], refs[count:2 * count], *refs[2 * count:])
        for cp in copies:
            cp.start()
        for cp in copies:
            cp.wait_recv()
        for cp in copies:
            cp.wait_send()

    return pl.pallas_call(
        body, name="exchange_siblings",
        out_shape=tuple(jax.ShapeDtypeStruct((g.shape[0],) + g.shape[2:], F32) for g in grads),
        in_specs=[_hbm()] * count, out_specs=(_hbm(),) * count,
        scratch_shapes=[pltpu.SemaphoreType.DMA((count,)), pltpu.SemaphoreType.DMA((count,))],
    )(*grads)


def _chip_copies(p_refs, got_refs, send_sems, recv_sems):
    x, y, c, chips = _place()
    return [_remote(p.at[2 * cx + cy], got.at[j], send_sems, recv_sems, 3 * i + j, (cx, cy, c))
            for i, (p, got) in enumerate(zip(p_refs, got_refs)) for j, (cx, cy) in enumerate(chips)]


def _share_halves(halves, small):
    count = len(halves)

    def body(*refs):
        h_refs, s_ref = refs[:count], refs[count]
        full_refs, sall_ref = refs[count + 1:2 * count + 1], refs[2 * count + 1]
        send_sems, recv_sems, local_sem = refs[2 * count + 2:]
        x, y, c, _ = _place()
        me = 4 * x + 2 * y + c
        own = pltpu.make_async_copy(s_ref, sall_ref.at[me], local_sem)
        own.start()
        copies = [_remote(h.at[c], full.at[c], send_sems, recv_sems, i, (x, y, 1 - c))
                  for i, (h, full) in enumerate(zip(h_refs, full_refs))]
        for k in range(7):
            dx, dy, dc = ((k + 1) >> 2) & 1, ((k + 1) >> 1) & 1, (k + 1) & 1
            peer = (1 - x if dx else x, 1 - y if dy else y, 1 - c if dc else c)
            copies.append(_remote(s_ref, sall_ref.at[me], send_sems, recv_sems, count + k, peer))
        for cp in copies:
            cp.start()
        for cp in copies:
            cp.wait_recv()
        for cp in copies:
            cp.wait_send()
        own.wait()

    return pl.pallas_call(
        body, name="share_halves",
        out_shape=tuple(jax.ShapeDtypeStruct(h.shape, h.dtype) for h in halves) + (jax.ShapeDtypeStruct((8,) + small.shape, F32),),
        in_specs=[_hbm()] * (count + 1), out_specs=(_hbm(),) * (count + 1), input_output_aliases={i: i for i in range(count)},
        scratch_shapes=[pltpu.SemaphoreType.DMA((count + 7,)), pltpu.SemaphoreType.DMA((count + 7,)), pltpu.SemaphoreType.DMA],
    )(*halves, small)


def _add_sibling(grad, got, core, name):
    chips, _, rows, cols = grad.shape

    def body(core_ref, g_ref, r_ref, sum_ref, send_ref):
        s = g_ref[...] + r_ref[...]
        sum_ref[...] = s
        send_ref[...] = s.astype(send_ref.dtype)

    block = pl.BlockSpec((None, rows, cols), lambda p, core_ref: (p, 0, 0))
    return pl.pallas_call(
        body, name="add_sibling_" + name,
        out_shape=(jax.ShapeDtypeStruct((chips, rows, cols), F32), jax.ShapeDtypeStruct((chips, rows, cols), BF16)),
        grid_spec=pltpu.PrefetchScalarGridSpec(
            num_scalar_prefetch=1, grid=(chips,),
            in_specs=[pl.BlockSpec((None, None, rows, cols), lambda p, core_ref: (p, core_ref[0], 0, 0)), block],
            out_specs=(block, block)),
        compiler_params=_params("parallel"),
    )(core, grad, got)


def _add_chips(part, got, chip_core, name):
    _, rows, cols = part.shape
    tr = rows // 2 if rows % 32 == 0 else rows

    def body(place_ref, p_ref, r_ref, o_ref):
        o_ref[...] = ((p_ref[...] + r_ref[0].astype(F32)) + r_ref[1].astype(F32)) + r_ref[2].astype(F32)

    return pl.pallas_call(
        body, name="add_chips_" + name, out_shape=jax.ShapeDtypeStruct((2, rows, cols), F32),
        grid_spec=pltpu.PrefetchScalarGridSpec(
            num_scalar_prefetch=1, grid=(rows // tr,),
            in_specs=[pl.BlockSpec((None, tr, cols), lambda i, place_ref: (place_ref[0], i, 0)),
                      pl.BlockSpec((3, tr, cols), lambda i, place_ref: (0, i, 0))],
            out_specs=pl.BlockSpec((None, tr, cols), lambda i, place_ref: (place_ref[1], i, 0))),
        compiler_params=_params("parallel"),
    )(chip_core, part, got)


def _adamw(w, g, m, v, name):
    rows, cols = w.shape
    tr = _pick(rows, (3592, 256, 352, 176, 128, 64, 32, 16, 8))

    def body(w_ref, g_ref, m_ref, v_ref, d_ref, nm_ref, nv_ref):
        d_ref[...], nm_ref[...], nv_ref[...] = _adamw_math(w_ref[...], g_ref[...], m_ref[...], v_ref[...])

    block = pl.BlockSpec((tr, cols), lambda i: (i, 0))
    shape = jax.ShapeDtypeStruct((rows, cols), F32)
    return pl.pallas_call(
        body, name="adamw_" + name, out_shape=(shape, shape, shape), grid=(rows // tr,),
        in_specs=[block] * 4, out_specs=(block,) * 3, compiler_params=_params("parallel"),
    )(w, g, m, v)


def _adamw_math(w, g, m, v):
    m = ADAM_B1 * m + (1.0 - ADAM_B1) * g
    v = ADAM_B2 * v + (1.0 - ADAM_B2) * (g * g)
    m_hat = m / (1.0 - ADAM_B1 ** ADAM_STEP)
    v_hat = v / (1.0 - ADAM_B2 ** ADAM_STEP)
    return -ADAM_LR * (m_hat / (jnp.sqrt(v_hat) + ADAM_EPS) + ADAM_WD * w), m, v


def _adamw_small(ws, gs, ms, vs):
    count = len(ws)

    def body(*refs):
        ins, outs = refs[:4 * count], refs[4 * count:]
        for i in range(count):
            outs[i][...], outs[count + i][...], outs[2 * count + i][...] = _adamw_math(
                ins[i][...], ins[count + i][...], ins[2 * count + i][...], ins[3 * count + i][...])

    shapes = tuple(jax.ShapeDtypeStruct(w.shape, F32) for w in ws)
    out = pl.pallas_call(body, name="adamw_small", out_shape=shapes * 3)(*ws, *gs, *ms, *vs)
    return out[:count], out[count:2 * count], out[2 * count:]


WEIGHTS = ("meta_tokens", "mix_pre_norm", "mix_post_norm", "ffn_pre_norm", "ffn_post_norm", "w_in", "conv_qkv", "a_log",
           "dt_bias", "gdn_norm", "conv_sc", "w_out", "w_gate", "w_up", "w_down")


def kernel(x, meta_tokens, mix_pre_norm, mix_post_norm, ffn_pre_norm, ffn_post_norm, w_in, conv_qkv, a_log, dt_bias, gdn_norm, conv_sc, w_out, w_gate, w_up, w_down, loss_target, m_meta_tokens, m_mix_pre_norm, m_mix_post_norm, m_ffn_pre_norm, m_ffn_post_norm, m_w_in, m_conv_qkv, m_a_log, m_dt_bias, m_gdn_norm, m_conv_sc, m_w_out, m_w_gate, m_w_up, m_w_down, v_meta_tokens, v_mix_pre_norm, v_mix_post_norm, v_ffn_pre_norm, v_ffn_post_norm, v_w_in, v_conv_qkv, v_a_log, v_dt_bias, v_gdn_norm, v_conv_sc, v_w_out, v_w_gate, v_w_up, v_w_down):
    d = x.shape[-1]
    two_d = lambda a: a.reshape(a.shape[-2:])
    weights = dict(zip(WEIGHTS, (meta_tokens, mix_pre_norm, mix_post_norm, ffn_pre_norm, ffn_post_norm, w_in, conv_qkv, a_log,
                                 dt_bias, gdn_norm, conv_sc, w_out, w_gate, w_up, w_down)))
    m_in = dict(zip(WEIGHTS, (m_meta_tokens, m_mix_pre_norm, m_mix_post_norm, m_ffn_pre_norm, m_ffn_post_norm, m_w_in, m_conv_qkv,
                              m_a_log, m_dt_bias, m_gdn_norm, m_conv_sc, m_w_out, m_w_gate, m_w_up, m_w_down)))
    v_in = dict(zip(WEIGHTS, (v_meta_tokens, v_mix_pre_norm, v_mix_post_norm, v_ffn_pre_norm, v_ffn_post_norm, v_w_in, v_conv_qkv,
                              v_a_log, v_dt_bias, v_gdn_norm, v_conv_sc, v_w_out, v_w_gate, v_w_up, v_w_down)))
    core = lax.axis_index("c")
    chip = 2 * lax.axis_index("x") + lax.axis_index("y")
    core_arg = core.reshape(1).astype(jnp.int32)
    chip_core = jnp.stack([chip, core]).astype(jnp.int32)
    whole = lambda a: a.reshape(a.shape[:-3] + (2 * a.shape[-2], d))
    by_rows = lambda n, a: two_d(a).T if n in ("w_in", "w_gate", "w_up") else two_d(a)

    shard = {n: by_rows(n, weights[n]).astype(MXU_DTYPE) for n in MATRICES}
    shard["w_in"] = jnp.pad(shard["w_in"], ((0, IN_SHARD_PAD - IN_SHARD), (0, 0)))
    small_all = _gather_weights([], [two_d(weights[n]) for n in SHARDED_SMALL])
    conv_qkv_full, conv_sc_full, meta_full = (jnp.concatenate([a[p] for p in range(N_CHIPS)], axis=1) for a in small_all)

    sq, grad_x, g, sums = _local_step(
        x, loss_target, meta_full, (mix_pre_norm, mix_post_norm, ffn_pre_norm, ffn_post_norm), _halves(shard["w_in"]),
        conv_qkv_full, a_log, dt_bias, gdn_norm, conv_sc_full, [_halves(shard[n]) for n in LATER], core_arg)

    totals = [_add_chips(part, got, chip_core, n) for n, (part, got) in zip(MATRICES, sums)]
    *shared, packed_all = _share_halves(totals, _pack_small(dict(g, loss=sq)))
    grads = {n: whole(a) for n, a in zip(MATRICES, shared)}
    grads["w_in"] = grads["w_in"][:IN_SHARD]
    grads.update(_sum_devices(packed_all, chip.reshape(1).astype(jnp.int32)))
    loss = (0.5 / d) * grads.pop("loss")[0, 0]

    small = [n for n in WEIGHTS if n not in MATRICES]
    updates = dict(zip(small, zip(*_adamw_small(*([by_rows(n, params[n]) for n in small] for params in (weights, grads, m_in, v_in))))))
    outs = [[], [], [], []]
    for n in WEIGHTS:
        shape = weights[n].shape
        if n in MATRICES:
            updates[n] = _adamw(by_rows(n, weights[n]), grads[n], by_rows(n, m_in[n]), by_rows(n, v_in[n]), n)
        for out, a in zip(outs, (grads[n], *updates[n])):
            out.append((a.T if n in ("w_in", "w_gate", "w_up") else a).reshape(shape))
    return (loss, grad_x, *outs[0], *outs[1], *outs[2], *outs[3])
```

```python
import functools

import jax
import jax.numpy as jnp
from jax import lax
from jax.experimental import pallas as pl
from jax.experimental.pallas import tpu as pltpu

F32 = jnp.float32
BF16 = jnp.bfloat16
MXU_DTYPE = jnp.bfloat16
MESH = pl.DeviceIdType.MESH

D_MODEL = 1024
N_META = 16
HEADS = 4
HEAD_DIM = 128
GDN_WIDTH = HEADS * HEAD_DIM
GDN_CONV = 4
CHUNK = 64
SC_WIDTH = D_MODEL - GDN_WIDTH
SC_CONV = 3
D_FF = 2816
IN_WIDTH = 4 * GDN_WIDTH + 2 * HEADS + 3 * SC_WIDTH
IN_PAD = 3840
BA_COL = (4 * GDN_WIDTH + 3 * SC_WIDTH) // 128
EPS = 1e-6
LANES = 128
N_CHIPS = 4
VMEM_LIMIT = 48 * 2 ** 20
MM_VMEM_BUDGET = 42 * 2 ** 20

ADAM_LR = 0.001
ADAM_B1 = 0.9
ADAM_B2 = 0.999
ADAM_EPS = 1e-08
ADAM_WD = 0.01
ADAM_STEP = 10


def _pick(n, candidates):
    for c in candidates:
        if n % c == 0:
            return c
    return n


def _row_tile(n):
    return _pick(n, (352, 256, 176, 128, 64, 32, 16, 8))


def _params(*sem):
    return pltpu.CompilerParams(dimension_semantics=sem, vmem_limit_bytes=VMEM_LIMIT)


def _sigmoid(x):
    return 0.5 * jnp.tanh(0.5 * x) + 0.5


def _softplus(x):
    return jnp.maximum(x, 0.0) + jnp.log(1.0 + jnp.exp(-jnp.abs(x)))


def _dsilu(x, s):
    return s * (1.0 + x * (1.0 - s))


def _mm(a, b, mode, out_dtype, name, init=None, exchange=None):
    if mode == "tn":
        k_dim, m_dim = a.shape
    else:
        m_dim, k_dim = a.shape
    n_dim = b.shape[0] if mode == "nt" else b.shape[1]
    tn = _pick(n_dim, (1408, 1280, 1024, 768, 512, 256, 128))
    if mode == "tn":
        tm = _pick(m_dim, (1408, 1280, 1024, 512, 256, 128))
        tk = _pick(k_dim, (2112, 1408, 1280, 1056, 1024, 512, 256, 128))
    else:
        tk = k_dim
        blocks = lambda rows: 2 * (2 * rows * tk + 2 * tk * tn + 4 * rows * tn * (1 if init is None else 2))
        tm = next((t for t in (2112, 1056, 1024, 704, 512, 256, 128) if m_dim % t == 0 and blocks(t) <= MM_VMEM_BUDGET), m_dim)
    nk = k_dim // tk
    if mode == "nn":
        a_spec = pl.BlockSpec((tm, tk), lambda i, j, k: (i, k))
        b_spec = pl.BlockSpec((tk, tn), lambda i, j, k: (k, j))
        dims = (((1,), (0,)), ((), ()))
    elif mode == "nt":
        a_spec = pl.BlockSpec((tm, tk), lambda i, j, k: (i, k))
        b_spec = pl.BlockSpec((tn, tk), lambda i, j, k: (j, k))
        dims = (((1,), (1,)), ((), ()))
    else:
        a_spec = pl.BlockSpec((tk, tm), lambda i, j, k: (k, i))
        b_spec = pl.BlockSpec((tk, tn), lambda i, j, k: (k, j))
        dims = (((0,), (0,)), ((), ()))

    out_spec = pl.BlockSpec((tm, tn), lambda i, j, k: (i, j))
    grid = (m_dim // tm, n_dim // tn, nk)
    parts = () if exchange is None else tuple(exchange)
    count = len(parts)
    first_in = 2 if init is None else 3

    assert out_dtype == F32

    def body(a_ref, b_ref, *rest):
        o_ref = rest[first_in - 2 + count]
        k = pl.program_id(2)
        step = (pl.program_id(0) * grid[1] + pl.program_id(1)) * nk + k
        if count:
            copies = _chip_copies(rest[first_in - 2:first_in - 2 + count], rest[first_in - 1 + count:first_in - 1 + 2 * count],
                                  *rest[first_in - 1 + 2 * count:])

            @pl.when(step == 0)
            def _():
                for cp in copies:
                    cp.start()

        p = lax.dot_general(a_ref[...], b_ref[...], dims, preferred_element_type=F32)
        if nk == 1:
            o_ref[...] = p if init is None else rest[0][...] + p
        else:
            @pl.when(k == 0)
            def _():
                o_ref[...] = p if init is None else rest[0][...] + p

            @pl.when(k > 0)
            def _():
                o_ref[...] += p

        if count:
            @pl.when(step == grid[0] * grid[1] * nk - 1)
            def _():
                for cp in copies:
                    cp.wait_recv()
                for cp in copies:
                    cp.wait_send()

    out = pl.pallas_call(
        body, name=name,
        out_shape=(jax.ShapeDtypeStruct((m_dim, n_dim), out_dtype),)
        + tuple(jax.ShapeDtypeStruct((3,) + p.shape[1:], p.dtype) for p in parts),
        grid=grid,
        in_specs=[a_spec, b_spec] + ([] if init is None else [out_spec]) + [_hbm()] * count,
        out_specs=(out_spec,) + (_hbm(),) * count,
        scratch_shapes=[pltpu.SemaphoreType.DMA((3 * count,)), pltpu.SemaphoreType.DMA((3 * count,))] if count else [],
        compiler_params=_params(*(("arbitrary",) * 3 if count else ("parallel", "parallel", "arbitrary"))),
    )(a, b, *(() if init is None else (init,)), *parts)
    return out[0] if not count else out


def _rms_apply(x, w):
    r = lax.rsqrt(jnp.mean(x * x, axis=-1, keepdims=True) + EPS)
    return x * r * w


def _rms_bwd(x, w, dy):
    r = lax.rsqrt(jnp.mean(x * x, axis=-1, keepdims=True) + EPS)
    xh = x * r
    dyw = dy * w
    dx = r * (dyw - xh * jnp.mean(dyw * xh, axis=-1, keepdims=True))
    return dx, jnp.sum(dy * xh, axis=0, keepdims=True)


def _accumulate(ref, first, value):
    @pl.when(first)
    def _():
        ref[...] = value

    @pl.when(jnp.logical_not(first))
    def _():
        ref[...] += value


def _rows(tr, width):
    return pl.BlockSpec((tr, width), lambda i: (i, 0))


def _vec(width):
    return pl.BlockSpec((1, width), lambda i: (0, 0))


def _embed(x, head, w_pre, w_shard, rows_per_seq):
    batch, seq, d = x.shape
    x_offset = head.shape[0]
    tr = _row_tile(rows_per_seq)
    tiles_per_seq = rows_per_seq // tr
    n = batch * rows_per_seq

    def body(x_ref, head_ref, w_ref, ws_ref, h0_ref, u_ref, wall_ref, send_sems, recv_sems):
        gather = _gather_copies([ws_ref], [wall_ref], send_sems, recv_sems)
        i = pl.program_id(0)
        tile = lax.rem(i, tiles_per_seq)

        @pl.when(i == 0)
        def _():
            for cp in gather[0]:
                cp.start()

        rows = jnp.concatenate([head_ref[...], x_ref[0:tr - x_offset, :]], axis=0)
        if tiles_per_seq > 1:
            start = pl.multiple_of(jnp.maximum(tile * tr - x_offset, 0), SUBLANES)
            rows = jnp.where(tile == 0, rows, x_ref[pl.ds(start, tr), :])
        h0_ref[...] = rows
        u_ref[...] = _rms_apply(rows, w_ref[...]).astype(u_ref.dtype)

        @pl.when(i == n // tr - 1)
        def _():
            _gather_finish(gather)

    return pl.pallas_call(
        body, name="embed",
        out_shape=(jax.ShapeDtypeStruct((n, d), F32), jax.ShapeDtypeStruct((n, d), MXU_DTYPE),
                   jax.ShapeDtypeStruct((N_CHIPS,) + w_shard.shape, w_shard.dtype)),
        grid=(n // tr,),
        in_specs=[pl.BlockSpec((None, seq, d), lambda i: (i // tiles_per_seq, 0, 0)),
                  pl.BlockSpec((x_offset, d), lambda i: (0, 0)), _vec(d), _hbm()],
        out_specs=(_rows(tr, d), _rows(tr, d), _hbm()),
        scratch_shapes=[pltpu.SemaphoreType.DMA((GATHER_SEMS,)), pltpu.SemaphoreType.DMA((GATHER_SEMS,))],
        compiler_params=_params("arbitrary"),
    )(x, head, w_pre, w_shard)


def _mix_residual(h0, mix, w_post, w_pre):
    n, d = h0.shape
    tr = _row_tile(n)

    def body(h0_ref, mix_ref, wpost_ref, wpre_ref, h1_ref, u2_ref):
        h1 = h0_ref[...] + _rms_apply(mix_ref[...], wpost_ref[...])
        h1_ref[...] = h1
        u2_ref[...] = _rms_apply(h1, wpre_ref[...]).astype(u2_ref.dtype)

    return pl.pallas_call(
        body, name="mix_residual",
        out_shape=(jax.ShapeDtypeStruct((n, d), F32), jax.ShapeDtypeStruct((n, d), MXU_DTYPE)), grid=(n // tr,),
        in_specs=[_rows(tr, d), _rows(tr, d), _vec(d), _vec(d)], out_specs=(_rows(tr, d), _rows(tr, d)),
        compiler_params=_params("parallel"),
    )(h0, mix, w_post, w_pre)


NT_DIMS = (((1,), (1,)), ((), ()))


def _ffn_tiles(n):
    return _pick(n, (1056, 704, 512, 256, 128)), _pick(D_FF, (1408, 256, 128))


def _swiglu_fwd(u, w_gate_t, w_up_t, w_next):
    n, d = u.shape
    tm, tn = _ffn_tiles(n)
    grid = (D_FF // tn, n // tm)

    def body(u_ref, wg_ref, wu_ref, wn_ref, g_ref, up_ref, act_ref, wall_ref, send_sems, recv_sems):
        gather = _gather_copies([wn_ref], [wall_ref], send_sems, recv_sems)
        step = pl.program_id(0) * grid[1] + pl.program_id(1)

        @pl.when(step == 0)
        def _():
            for cp in gather[0]:
                cp.start()

        a = u_ref[...]
        g = lax.dot_general(a, wg_ref[...], NT_DIMS, preferred_element_type=F32)
        up = lax.dot_general(a, wu_ref[...], NT_DIMS, preferred_element_type=F32)
        g_ref[...] = g.astype(g_ref.dtype)
        up_ref[...] = up.astype(up_ref.dtype)
        act_ref[...] = (g * _sigmoid(g) * up).astype(act_ref.dtype)

        @pl.when(step == grid[0] * grid[1] - 1)
        def _():
            _gather_finish(gather)

    tile = pl.BlockSpec((tm, tn), lambda j, i: (i, j))
    weight = pl.BlockSpec((tn, d), lambda j, i: (j, 0))
    wide = jax.ShapeDtypeStruct((n, D_FF), MXU_DTYPE)
    return pl.pallas_call(
        body, name="swiglu_fwd",
        out_shape=(wide, wide, jax.ShapeDtypeStruct((n, D_FF), MXU_DTYPE),
                   jax.ShapeDtypeStruct((N_CHIPS,) + w_next.shape, w_next.dtype)),
        grid=grid,
        in_specs=[pl.BlockSpec((tm, d), lambda j, i: (i, 0)), weight, weight, _hbm()], out_specs=(tile, tile, tile, _hbm()),
        scratch_shapes=[pltpu.SemaphoreType.DMA((GATHER_SEMS,)), pltpu.SemaphoreType.DMA((GATHER_SEMS,))],
        compiler_params=_params("arbitrary", "arbitrary"),
    )(u, w_gate_t, w_up_t, w_next)


def _swiglu_bwd(dffn, w_down, gate, up):
    n, d = dffn.shape
    tm, tn = _ffn_tiles(n)

    def body(dy_ref, w_ref, g_ref, u_ref, dg_ref, du_ref):
        da = lax.dot_general(dy_ref[...], w_ref[...], NT_DIMS, preferred_element_type=F32)
        g = g_ref[...].astype(F32)
        s = _sigmoid(g)
        dg_ref[...] = (da * u_ref[...].astype(F32) * _dsilu(g, s)).astype(dg_ref.dtype)
        du_ref[...] = (da * g * s).astype(du_ref.dtype)

    tile = pl.BlockSpec((tm, tn), lambda j, i: (i, j))
    shape = jax.ShapeDtypeStruct((n, D_FF), MXU_DTYPE)
    return pl.pallas_call(
        body, name="swiglu_bwd", out_shape=(shape, shape), grid=(D_FF // tn, n // tm),
        in_specs=[pl.BlockSpec((tm, d), lambda j, i: (i, 0)), pl.BlockSpec((tn, d), lambda j, i: (j, 0)), tile, tile],
        out_specs=(tile, tile), compiler_params=_params("parallel", "parallel"),
    )(dffn, w_down, gate, up)


def _loss_head(h1, ffn, w_post, target, rows_per_seq, x_offset):
    n, d = h1.shape
    tr = _row_tile(rows_per_seq)
    tiles_per_seq = rows_per_seq // tr
    seq = target.shape[1]

    def seq_rows(t_ref, tile):
        first = jnp.concatenate([jnp.zeros((x_offset, d), F32), t_ref[0:tr - x_offset, :]], axis=0)
        if tiles_per_seq == 1:
            return first
        start = pl.multiple_of(jnp.maximum(tile * tr - x_offset, 0), SUBLANES)
        return jnp.where(tile == 0, first, t_ref[pl.ds(start, tr), :])

    def body(h1_ref, ffn_ref, w_ref, t_ref, dh2_ref, dffn_ref, dw_ref, sq_ref):
        i = pl.program_id(0)
        tile = lax.rem(i, tiles_per_seq)
        w = w_ref[...]
        f = ffn_ref[...]
        r = lax.rsqrt(jnp.mean(f * f, axis=-1, keepdims=True) + EPS)
        fh = f * r
        row = tile * tr + lax.broadcasted_iota(jnp.int32, (tr, 1), 0)
        err = jnp.where(row >= x_offset, h1_ref[...] + fh * w - seq_rows(t_ref, tile), 0.0)
        dh2 = err * (1.0 / d)
        dh2_ref[...] = dh2
        dyw = dh2 * w
        dffn_ref[...] = (r * (dyw - fh * jnp.mean(dyw * fh, axis=-1, keepdims=True))).astype(dffn_ref.dtype)
        _accumulate(dw_ref, i == 0, jnp.sum(dh2 * fh, axis=0, keepdims=True))
        _accumulate(sq_ref, i == 0, jnp.sum(jnp.sum(err * err, axis=1, keepdims=True), axis=0, keepdims=True))

    return pl.pallas_call(
        body, name="loss_head",
        out_shape=(jax.ShapeDtypeStruct((n, d), F32), jax.ShapeDtypeStruct((n, d), MXU_DTYPE),
                   jax.ShapeDtypeStruct((1, d), F32), jax.ShapeDtypeStruct((1, 1), F32)),
        grid=(n // tr,),
        in_specs=[_rows(tr, d), _rows(tr, d), _vec(d), pl.BlockSpec((None, seq, d), lambda i: (i // tiles_per_seq, 0, 0))],
        out_specs=(_rows(tr, d), _rows(tr, d), _vec(d), _vec(1)),
        compiler_params=_params("arbitrary"),
    )(h1, ffn, w_post, target)


def _mid_bwd(h1, mix, w_mix_post, w_ffn_pre, dh2, du2, grads):
    n, d = h1.shape
    tr = _row_tile(n)
    count = len(grads)

    def body(h1_ref, mix_ref, wpost_ref, wpre_ref, dh2_ref, du2_ref, *rest):
        g_refs, (dh1_ref, dmix_ref, dwpre_ref, dwpost_ref), got_refs = rest[:count], rest[count:count + 4], rest[count + 4:2 * count + 4]
        exchange = _sibling_copies(g_refs, got_refs, *rest[2 * count + 4:])
        i = pl.program_id(0)

        @pl.when(i == 0)
        def _():
            for cp in exchange:
                cp.start()

        dx, dwpre = _rms_bwd(h1_ref[...], wpre_ref[...], du2_ref[...])
        dh1 = dh2_ref[...] + dx
        dh1_ref[...] = dh1
        dmix, dwpost = _rms_bwd(mix_ref[...], wpost_ref[...], dh1)
        dmix_ref[...] = dmix.astype(dmix_ref.dtype)
        _accumulate(dwpre_ref, i == 0, dwpre)
        _accumulate(dwpost_ref, i == 0, dwpost)

        @pl.when(i == n // tr - 1)
        def _():
            for cp in exchange:
                cp.wait_recv()
            for cp in exchange:
                cp.wait_send()

    dh1, dmix, dwpre, dwpost, *got = pl.pallas_call(
        body, name="mid_bwd",
        out_shape=(jax.ShapeDtypeStruct((n, d), F32), jax.ShapeDtypeStruct((n, d), MXU_DTYPE),
                   jax.ShapeDtypeStruct((1, d), F32), jax.ShapeDtypeStruct((1, d), F32))
        + tuple(jax.ShapeDtypeStruct((g.shape[0],) + g.shape[2:], F32) for g in grads),
        grid=(n // tr,),
        in_specs=[_rows(tr, d), _rows(tr, d), _vec(d), _vec(d), _rows(tr, d), _rows(tr, d)] + [_hbm()] * count,
        out_specs=(_rows(tr, d), _rows(tr, d), _vec(d), _vec(d)) + (_hbm(),) * count,
        scratch_shapes=[pltpu.SemaphoreType.DMA((count,)), pltpu.SemaphoreType.DMA((count,))],
        compiler_params=_params("arbitrary"),
    )(h1, mix, w_mix_post, w_ffn_pre, dh2, du2, *grads)
    return dh1, dmix, dwpre, dwpost, got


def _in_bwd(h0, w_pre, dh1, du1, rows_per_seq, pad_rows, x_offset):
    n, d = h0.shape
    tr = _row_tile(rows_per_seq)
    tiles_per_seq = rows_per_seq // tr
    seq = rows_per_seq - x_offset

    def body(h0_ref, w_ref, dh1_ref, du1_ref, gx_ref, dmeta_ref, dw_ref):
        i = pl.program_id(0)
        tile = lax.rem(i, tiles_per_seq)
        dx, dw = _rms_bwd(h0_ref[...], w_ref[...], du1_ref[...])
        dh0 = dh1_ref[...] + dx
        _accumulate(dw_ref, i == 0, dw)

        @pl.when(tile == 0)
        def _():
            gx_ref[0:tr - x_offset, :] = dh0[x_offset:, :]
            _accumulate(dmeta_ref, i == 0, dh0[pad_rows:x_offset, :])

        if tiles_per_seq > 1:
            @pl.when(tile > 0)
            def _():
                gx_ref[pl.ds(pl.multiple_of(tile * tr - x_offset, SUBLANES), tr), :] = dh0

    return pl.pallas_call(
        body, name="in_bwd",
        out_shape=(jax.ShapeDtypeStruct((n // rows_per_seq, seq, d), F32), jax.ShapeDtypeStruct((x_offset - pad_rows, d), F32),
                   jax.ShapeDtypeStruct((1, d), F32)),
        grid=(n // tr,),
        in_specs=[_rows(tr, d), _vec(d), _rows(tr, d), _rows(tr, d)],
        out_specs=(pl.BlockSpec((None, seq, d), lambda i: (i // tiles_per_seq, 0, 0)),
                   pl.BlockSpec((x_offset - pad_rows, d), lambda i: (0, 0)), _vec(d)),
        compiler_params=_params("arbitrary"),
    )(h0, w_pre, dh1, du1)


def _lane_is(lo, hi):
    lane = lax.broadcasted_iota(jnp.int32, (1, LANES), 1)
    return jnp.logical_and(lane >= lo, lane < hi)


def _gates_fwd(proj, a_log_l, dt_bias_l, rows_per_seq, pad_rows):
    n = proj.shape[0]
    tr = _row_tile(rows_per_seq)
    tiles_per_seq = rows_per_seq // tr

    def body(p_ref, a_ref, dt_ref, o_ref):
        x = p_ref[...]
        row = lax.rem(pl.program_id(0), tiles_per_seq) * tr + lax.broadcasted_iota(jnp.int32, (tr, 1), 0)
        g = -jnp.exp(a_ref[...]) * _softplus(x + dt_ref[...])
        val = jnp.where(_lane_is(0, HEADS), _sigmoid(x), jnp.where(_lane_is(HEADS, 2 * HEADS), g, 0.0))
        o_ref[...] = jnp.where(row >= pad_rows, val, 0.0)

    return pl.pallas_call(
        body, name="gates_fwd", out_shape=jax.ShapeDtypeStruct((n, LANES), F32), grid=(n // tr,),
        in_specs=[pl.BlockSpec((tr, LANES), lambda i: (i, BA_COL)), _vec(LANES), _vec(LANES)],
        out_specs=_rows(tr, LANES), compiler_params=_params("parallel"),
    )(proj, a_log_l, dt_bias_l)


def _gates_bwd(proj, dbg, a_log_l, dt_bias_l, rows_per_seq, pad_rows, dproj):
    n = proj.shape[0]
    tr = _row_tile(rows_per_seq)
    tiles_per_seq = rows_per_seq // tr

    def body(p_ref, d_ref, a_ref, dt_ref, _, dx_ref, da_ref, ddt_ref):
        i = pl.program_id(0)
        x = p_ref[...]
        d = d_ref[...]
        row = lax.rem(i, tiles_per_seq) * tr + lax.broadcasted_iota(jnp.int32, (tr, 1), 0)
        live = row >= pad_rows
        beta = _sigmoid(x)
        ea = jnp.exp(a_ref[...])
        xa = x + dt_ref[...]
        g = -ea * _softplus(xa)
        is_g = _lane_is(HEADS, 2 * HEADS)
        d_alogit = jnp.where(jnp.logical_and(live, is_g), d * (-ea) * _sigmoid(xa), 0.0)
        d_blogit = jnp.where(jnp.logical_and(live, _lane_is(0, HEADS)), d * beta * (1.0 - beta), 0.0)
        dx_ref[:, :LANES] = (d_alogit + d_blogit).astype(dx_ref.dtype)
        dx_ref[:, LANES:] = jnp.zeros((tr, LANES), dx_ref.dtype)
        _accumulate(da_ref, i == 0, jnp.sum(jnp.where(jnp.logical_and(live, is_g), d * g, 0.0), axis=0, keepdims=True))
        _accumulate(ddt_ref, i == 0, jnp.sum(d_alogit, axis=0, keepdims=True))

    return pl.pallas_call(
        body, name="gates_bwd",
        out_shape=(jax.ShapeDtypeStruct(dproj.shape, dproj.dtype), jax.ShapeDtypeStruct((1, LANES), F32),
                   jax.ShapeDtypeStruct((1, LANES), F32)),
        grid=(n // tr,),
        in_specs=[pl.BlockSpec((tr, LANES), lambda i: (i, BA_COL)), _rows(tr, LANES), _vec(LANES), _vec(LANES), _hbm()],
        out_specs=(pl.BlockSpec((tr, 2 * LANES), lambda i: (i, BA_COL // 2)), _vec(LANES), _vec(LANES)),
        input_output_aliases={4: 0},
        compiler_params=_params("arbitrary"),
    )(proj, dbg, a_log_l, dt_bias_l, dproj)


HALO = 8


def _halo_scratch(rs):
    return pltpu.VMEM((rs + 2 * HALO, LANES), F32)


def _stage(ref, x):
    rs = x.shape[0]
    ref[0:HALO, :] = jnp.zeros((HALO, LANES), F32)
    ref[HALO + rs:, :] = jnp.zeros((HALO, LANES), F32)
    ref[HALO:HALO + rs, :] = x


def _shifted(ref, k, rs):
    return ref[pl.ds(HALO - k, rs), :]


def _causal_conv(x, x_staged, w, width):
    acc = w[width - 1:width, :] * x
    for i in range(width - 1):
        acc = acc + w[i:i + 1, :] * _shifted(x_staged, width - 1 - i, x.shape[0])
    return acc


def _anti_causal_conv(dy, dy_staged, w, width):
    acc = w[width - 1:width, :] * dy
    for i in range(width - 1):
        acc = acc + w[i:i + 1, :] * _shifted(dy_staged, -(width - 1 - i), dy.shape[0])
    return acc


def _conv_weight_grad(dy, x, x_staged, width):
    taps = [_shifted(x_staged, width - 1 - i, x.shape[0]) for i in range(width - 1)] + [x]
    return jnp.concatenate([jnp.sum(dy * tap, axis=0, keepdims=True) for tap in taps], axis=0)


def _seq_cols(rs, col0, heads):
    return pl.BlockSpec((rs, heads * LANES), lambda j, b: (b, col0 // heads + j))


def _tap_cols(width, col0, heads):
    return pl.BlockSpec((width, heads * LANES), lambda j, b: (0, col0 // heads + j))


def _lanes_of(h):
    return slice(h * LANES, (h + 1) * LANES)


def _qkv_fwd(proj, conv_w, kind, rs):
    n = proj.shape[0]
    col0 = {"q": 0, "k": HEADS, "v": 2 * HEADS}[kind]
    hb = HEADS

    def body(p_ref, w_ref, o_ref, staged):
        for h in range(hb):
            pre = p_ref[:, _lanes_of(h)]
            _stage(staged, pre)
            c = _causal_conv(pre, staged, w_ref[:, _lanes_of(h)], GDN_CONV)
            s = c * _sigmoid(c)
            if kind != "v":
                s = s * lax.rsqrt(jnp.sum(s * s, axis=-1, keepdims=True) + EPS)
            if kind == "q":
                s = s * (HEAD_DIM ** -0.5)
            o_ref[:, _lanes_of(h)] = s

    return pl.pallas_call(
        body, name="qkv_fwd_" + kind, out_shape=jax.ShapeDtypeStruct((n, GDN_WIDTH), F32), grid=(HEADS // hb, n // rs),
        in_specs=[_seq_cols(rs, col0, hb), _tap_cols(GDN_CONV, col0, hb)],
        out_specs=_seq_cols(rs, 0, hb), scratch_shapes=[_halo_scratch(rs)], compiler_params=_params("parallel", "parallel"),
    )(proj, conv_w)


def _qkv_bwd(dy, proj, conv_w, kind, rs, dproj):
    n = proj.shape[0]
    col0 = {"q": 0, "k": HEADS, "v": 2 * HEADS}[kind]
    hb = HEADS

    def body(dy_ref, p_ref, w_ref, _, dp_ref, dw_ref, pre_staged, dc_staged):
        for h in range(hb):
            lanes = _lanes_of(h)
            pre = p_ref[:, lanes]
            w = w_ref[:, lanes]
            _stage(pre_staged, pre)
            c = _causal_conv(pre, pre_staged, w, GDN_CONV)
            sg = _sigmoid(c)
            s = c * sg
            ds = dy_ref[:, lanes]
            if kind == "q":
                ds = ds * (HEAD_DIM ** -0.5)
            if kind != "v":
                r = lax.rsqrt(jnp.sum(s * s, axis=-1, keepdims=True) + EPS)
                sh = s * r
                ds = r * (ds - sh * jnp.sum(ds * sh, axis=-1, keepdims=True))
            dc = ds * _dsilu(c, sg)
            _stage(dc_staged, dc)
            dp_ref[:, lanes] = _anti_causal_conv(dc, dc_staged, w, GDN_CONV).astype(dp_ref.dtype)
            _accumulate(dw_ref.at[:, lanes], pl.program_id(1) == 0, _conv_weight_grad(dc, pre, pre_staged, GDN_CONV))

    return pl.pallas_call(
        body, name="qkv_bwd_" + kind,
        out_shape=(jax.ShapeDtypeStruct(dproj.shape, dproj.dtype), jax.ShapeDtypeStruct((GDN_CONV, GDN_WIDTH), F32)),
        grid=(HEADS // hb, n // rs),
        in_specs=[_seq_cols(rs, 0, hb), _seq_cols(rs, col0, hb), _tap_cols(GDN_CONV, col0, hb), _hbm()],
        out_specs=(_seq_cols(rs, col0, hb), _tap_cols(GDN_CONV, 0, hb)), input_output_aliases={3: 0},
        scratch_shapes=[_halo_scratch(rs), _halo_scratch(rs)],
        compiler_params=_params("parallel", "arbitrary"),
    )(dy, proj, conv_w, dproj)


SC_COL = 4 * HEADS


def _sc_fwd(proj, conv_w, rs, cat):
    n = proj.shape[0]

    hb = 2

    def body(x_ref, b_ref, c_ref, w_ref, _, y_ref, staged):
        for h in range(hb):
            lanes = _lanes_of(h)
            u = c_ref[:, lanes] * x_ref[:, lanes]
            _stage(staged, u)
            y_ref[:, lanes] = (b_ref[:, lanes] * _causal_conv(u, staged, w_ref[:, lanes], SC_CONV)).astype(y_ref.dtype)

    return pl.pallas_call(
        body, name="sc_fwd", out_shape=jax.ShapeDtypeStruct(cat.shape, cat.dtype), grid=(HEADS // hb, n // rs),
        in_specs=[_seq_cols(rs, SC_COL, hb), _seq_cols(rs, SC_COL + 4, hb), _seq_cols(rs, SC_COL + 8, hb),
                  _tap_cols(SC_CONV, 0, hb), _hbm()],
        out_specs=_seq_cols(rs, HEADS, hb), input_output_aliases={4: 0}, scratch_shapes=[_halo_scratch(rs)],
        compiler_params=_params("parallel", "parallel"),
    )(proj, proj, proj, conv_w, cat)


def _sc_bwd(dcat, proj, conv_w, rs, dproj):
    n = proj.shape[0]
    hb = 2

    def body(dy_ref, x_ref, b_ref, c_ref, w_ref, _, dx_ref, db_ref, dc_ref, dw_ref, u_staged, dcv_staged):
        for h in range(hb):
            lanes = _lanes_of(h)
            w = w_ref[:, lanes]
            x = x_ref[:, lanes]
            cc = c_ref[:, lanes]
            u = cc * x
            _stage(u_staged, u)
            dy = dy_ref[:, lanes]
            db_ref[:, lanes] = (dy * _causal_conv(u, u_staged, w, SC_CONV)).astype(db_ref.dtype)
            dcv = dy * b_ref[:, lanes]
            _stage(dcv_staged, dcv)
            du = _anti_causal_conv(dcv, dcv_staged, w, SC_CONV)
            dx_ref[:, lanes] = (du * cc).astype(dx_ref.dtype)
            dc_ref[:, lanes] = (du * x).astype(dc_ref.dtype)
            _accumulate(dw_ref.at[:, lanes], pl.program_id(1) == 0, _conv_weight_grad(dcv, u, u_staged, SC_CONV))

    piece = jax.ShapeDtypeStruct((n, SC_WIDTH), MXU_DTYPE)
    return pl.pallas_call(
        body, name="sc_bwd",
        out_shape=(jax.ShapeDtypeStruct(dproj.shape, dproj.dtype), piece, piece, jax.ShapeDtypeStruct((SC_CONV, SC_WIDTH), F32)),
        grid=(HEADS // hb, n // rs),
        in_specs=[_seq_cols(rs, HEADS, hb), _seq_cols(rs, SC_COL, hb), _seq_cols(rs, SC_COL + 4, hb),
                  _seq_cols(rs, SC_COL + 8, hb), _tap_cols(SC_CONV, 0, hb), _hbm()],
        out_specs=(_seq_cols(rs, SC_COL, hb), _seq_cols(rs, 0, hb), _seq_cols(rs, 0, hb), _tap_cols(SC_CONV, 0, hb)),
        input_output_aliases={5: 0},
        scratch_shapes=[_halo_scratch(rs), _halo_scratch(rs)],
        compiler_params=_params("parallel", "arbitrary"),
    )(dcat, proj, proj, proj, conv_w, dproj)


Z_COL = 3 * HEADS


def _gate_fwd(o, proj, gdn_norm, rs):
    n = proj.shape[0]

    hb = HEADS

    def body(o_ref, z_ref, w_ref, y_ref):
        for h in range(hb):
            lanes = _lanes_of(h)
            z = z_ref[:, lanes]
            y_ref[:, lanes] = (_rms_apply(o_ref[:, lanes], w_ref[...]) * z * _sigmoid(z)).astype(y_ref.dtype)

    return pl.pallas_call(
        body, name="gate_fwd", out_shape=jax.ShapeDtypeStruct((n, D_MODEL), MXU_DTYPE), grid=(HEADS // hb, n // rs),
        in_specs=[_seq_cols(rs, 0, hb), _seq_cols(rs, Z_COL, hb), pl.BlockSpec((1, LANES), lambda j, b: (0, 0))],
        out_specs=_seq_cols(rs, 0, hb), compiler_params=_params("parallel", "parallel"),
    )(o, proj, gdn_norm)


def _gate_bwd(dcat, o, proj, gdn_norm, rs):
    n = proj.shape[0]
    hb = 2

    def body(dy_ref, o_ref, z_ref, w_ref, do_ref, dz_ref, dw_ref):
        w = w_ref[...]
        dw_step = jnp.zeros((1, LANES), F32)
        for h in range(hb):
            lanes = _lanes_of(h)
            z = z_ref[:, lanes]
            o = o_ref[:, lanes]
            dy = dy_ref[:, lanes]
            s = _sigmoid(z)
            dz_ref[:, lanes] = (dy * _rms_apply(o, w) * _dsilu(z, s)).astype(dz_ref.dtype)
            do, dw = _rms_bwd(o, w, dy * z * s)
            do_ref[:, lanes] = do
            dw_step = dw_step + dw
        _accumulate(dw_ref, jnp.logical_and(pl.program_id(0) == 0, pl.program_id(1) == 0), dw_step)

    return pl.pallas_call(
        body, name="gate_bwd",
        out_shape=(jax.ShapeDtypeStruct((n, GDN_WIDTH), F32), jax.ShapeDtypeStruct((n, IN_PAD), MXU_DTYPE),
                   jax.ShapeDtypeStruct((1, LANES), F32)),
        grid=(HEADS // hb, n // rs),
        in_specs=[_seq_cols(rs, 0, hb), _seq_cols(rs, 0, hb), _seq_cols(rs, Z_COL, hb), pl.BlockSpec((1, LANES), lambda j, b: (0, 0))],
        out_specs=(_seq_cols(rs, 0, hb), _seq_cols(rs, Z_COL, hb), pl.BlockSpec((1, LANES), lambda j, b: (0, 0))),
        compiler_params=_params("arbitrary", "arbitrary"),
    )(dcat, o, proj, gdn_norm)


def _dot(a, b):
    return jnp.dot(a.astype(MXU_DTYPE), b.astype(MXU_DTYPE), preferred_element_type=F32)


def _dot_nt(a, b):
    return lax.dot_general(a.astype(MXU_DTYPE), b.astype(MXU_DTYPE), (((1,), (1,)), ((), ())),
                           preferred_element_type=F32)


def _dot_tn(a, b):
    return lax.dot_general(a.astype(MXU_DTYPE), b.astype(MXU_DTYPE), (((0,), (0,)), ((), ())),
                           preferred_element_type=F32)


def _split(x):
    hi = x.astype(MXU_DTYPE)
    return hi, (x - hi.astype(F32)).astype(MXU_DTYPE)


def _dot_split(a, b):
    mm = functools.partial(jnp.dot, preferred_element_type=F32)
    return mm(a[0], b[0]) + (mm(a[0], b[1]) + mm(a[1], b[0]))


def _unit_lower_inverses(mats, eye):
    inv = [eye - a for a in mats]
    power = [_split(a) for a in mats]
    span = 2
    while span < CHUNK:
        power = [_split(_dot_split(p, p)) for p in power]
        inv = [i + _dot_split(_split(i), p) for i, p in zip(inv, power)]
        span *= 2
    return inv


def _chunk_masks():
    ii = lax.broadcasted_iota(jnp.int32, (CHUNK, CHUNK), 0)
    jj = lax.broadcasted_iota(jnp.int32, (CHUNK, CHUNK), 1)
    return ii, jj


def _chunk_decay(g_col, ii, jj):
    incl = ii >= jj
    g_row = jnp.sum(jnp.where(ii == jj, g_col, 0.0), axis=0, keepdims=True)
    gc_col = jnp.sum(jnp.where(incl, g_row, 0.0), axis=1, keepdims=True)
    gc_row = jnp.sum(jnp.where(ii <= jj, g_col, 0.0), axis=0, keepdims=True)
    g_total = jnp.sum(g_row, axis=1, keepdims=True)
    decay = jnp.where(incl, jnp.exp(jnp.where(incl, gc_col - gc_row, 0.0)), 0.0)
    return gc_col, g_total, decay


def _gdn_segments(rs, candidates):
    chunks = rs // CHUNK
    seg_chunks = _pick(chunks, candidates)
    return chunks, seg_chunks, chunks // seg_chunks


def _head_lanes(h):
    return slice(h * HEAD_DIM, (h + 1) * HEAD_DIM)


def _gdn_fwd(q, k, v, bg, rs, pieces):
    n = q.shape[0]
    batch = n // rs
    chunks, seg_chunks, segs = _gdn_segments(rs, (11, 8, 4, 2))
    seg_rows = seg_chunks * CHUNK
    chains = [(b, h) for b in range(batch) for h in range(HEADS)]
    each = lambda f, *lists: [f(*args) for args in zip(*lists)]
    count = len(pieces)

    def body(q_ref, k_ref, v_ref, bg_ref, *rest):
        w_refs, (o_ref, s_ref, t_ref), out_refs = rest[:count], rest[count:count + 3], rest[count + 3:2 * count + 3]
        state_ref, send_sems, recv_sems = rest[2 * count + 3:]
        gather = _gather_copies(w_refs, out_refs, send_sems, recv_sems)

        @pl.when(pl.program_id(0) == 0)
        def _():
            state_ref[...] = jnp.zeros_like(state_ref)
            for cp in gather[0]:
                cp.start()

        ii, jj = _chunk_masks()
        incl = ii >= jj
        eye = (ii == jj).astype(F32)

        def chunk(c, carry):
            rows = pl.ds(pl.multiple_of(c * CHUNK, CHUNK), CHUNK)
            bgc = [bg_ref[b, rows, :] for b in range(batch)]
            qc = [q_ref[b, rows, _head_lanes(h)] for b, h in chains]
            kc = [k_ref[b, rows, _head_lanes(h)] for b, h in chains]
            vc = [v_ref[b, rows, _head_lanes(h)] for b, h in chains]
            beta = [bgc[b][:, h:h + 1] for b, h in chains]
            state = [state_ref[b, h] for b, h in chains]
            dec = [_chunk_decay(bgc[b][:, HEADS + h:HEADS + h + 1], ii, jj) for b, h in chains]
            gc_col, g_total, decay = ([d[i] for d in dec] for i in range(3))
            kb = each(lambda x, y: x * y, kc, beta)
            a = each(lambda x, y, d: jnp.where(ii > jj, _dot_nt(x, y) * d, 0.0), kb, kc, decay)
            t_inv = _unit_lower_inverses(a, eye)
            eg = [jnp.exp(g) for g in gc_col]
            u = each(lambda t, x, y: _dot(t, x * y), t_inv, vc, beta)
            w = each(lambda t, x, e: _dot(t, x * e), t_inv, kb, eg)
            qk = each(lambda x, y, d: jnp.where(incl, _dot_nt(x, y) * d, 0.0), qc, kc, decay)
            v_new = each(lambda x, y, s: x - _dot(y, s), u, w, state)
            o = each(lambda x, e, s, m, vn: _dot(x * e, s) + _dot(m, vn), qc, eg, state, qk, v_new)
            new_state = each(lambda s, gt, x, g, vn: s * jnp.exp(gt) + _dot_tn(x * jnp.exp(gt - g), vn),
                             state, g_total, kc, gc_col, v_new)
            for i, (b, h) in enumerate(chains):
                s_ref[b, h, c] = state[i]
                t_ref[b, h, c] = t_inv[i]
                o_ref[b, rows, _head_lanes(h)] = o[i]
                state_ref[b, h] = new_state[i]
            return carry

        lax.fori_loop(0, seg_chunks, chunk, 0)

        @pl.when(pl.program_id(0) == segs - 1)
        def _():
            _gather_finish(gather)

    rows_spec = lambda width: pl.BlockSpec((batch, seg_rows, width), lambda s: (0, s, 0))
    per_chunk = lambda r, c: pl.BlockSpec((batch, HEADS, seg_chunks, r, c), lambda s: (0, 0, s, 0, 0))
    as_seqs = lambda a: a.reshape(batch, rs, a.shape[-1])
    sems = GATHER_SEMS * count
    o, states, t_invs, *gathered = pl.pallas_call(
        body, name="gdn_fwd",
        out_shape=(jax.ShapeDtypeStruct((batch, rs, GDN_WIDTH), F32),
                   jax.ShapeDtypeStruct((batch, HEADS, chunks, HEAD_DIM, HEAD_DIM), F32),
                   jax.ShapeDtypeStruct((batch, HEADS, chunks, CHUNK, CHUNK), F32))
        + tuple(jax.ShapeDtypeStruct((N_CHIPS,) + p.shape, p.dtype) for p in pieces),
        grid=(segs,),
        in_specs=[rows_spec(GDN_WIDTH), rows_spec(GDN_WIDTH), rows_spec(GDN_WIDTH), rows_spec(LANES)] + [_hbm()] * count,
        out_specs=(rows_spec(GDN_WIDTH), per_chunk(HEAD_DIM, HEAD_DIM), per_chunk(CHUNK, CHUNK)) + (_hbm(),) * count,
        scratch_shapes=[pltpu.VMEM((batch, HEADS, HEAD_DIM, HEAD_DIM), F32), pltpu.SemaphoreType.DMA((sems,)),
                        pltpu.SemaphoreType.DMA((sems,))],
        compiler_params=_params("arbitrary"),
    )(as_seqs(q), as_seqs(k), as_seqs(v), as_seqs(bg), *pieces)
    return o.reshape(n, GDN_WIDTH), states, t_invs, gathered


def _gdn_bwd(do, q, k, v, bg, states, t_invs, rs, parts):
    n = q.shape[0]
    batch = n // rs
    chunks, seg_chunks, segs = _gdn_segments(rs, (3, 4, 2))
    seg_rows = seg_chunks * CHUNK
    chains = [(b, h) for b in range(batch) for h in range(HEADS)]
    each = lambda f, *lists: [f(*args) for args in zip(*lists)]
    count = len(parts)

    def body(do_ref, q_ref, k_ref, v_ref, bg_ref, s_ref, t_ref, *rest):
        p_refs, (dq_ref, dk_ref, dv_ref, dbg_ref), got_refs = rest[:count], rest[count:count + 4], rest[count + 4:2 * count + 4]
        dstate_ref, send_sems, recv_sems = rest[2 * count + 4:]
        exchange = _chip_copies(p_refs, got_refs, send_sems, recv_sems)

        @pl.when(pl.program_id(0) == 0)
        def _():
            dstate_ref[...] = jnp.zeros_like(dstate_ref)
            for cp in exchange:
                cp.start()

        ii, jj = _chunk_masks()
        incl = ii >= jj
        strict = ii > jj
        lane = lax.broadcasted_iota(jnp.int32, (1, LANES), 1)

        def rowsum(x):
            return jnp.sum(x, axis=1, keepdims=True)

        def total(x):
            return jnp.sum(rowsum(x), axis=0, keepdims=True)

        def chunk(step, carry):
            c = seg_chunks - 1 - step
            rows = pl.ds(pl.multiple_of(c * CHUNK, CHUNK), CHUNK)
            bgc = [bg_ref[b, rows, :] for b in range(batch)]
            qc = [q_ref[b, rows, _head_lanes(h)] for b, h in chains]
            kc = [k_ref[b, rows, _head_lanes(h)] for b, h in chains]
            vc = [v_ref[b, rows, _head_lanes(h)] for b, h in chains]
            doc = [do_ref[b, rows, _head_lanes(h)] for b, h in chains]
            beta = [bgc[b][:, h:h + 1] for b, h in chains]
            state = [s_ref[b, h, c] for b, h in chains]
            t_inv = [t_ref[b, h, c] for b, h in chains]
            d_state = [dstate_ref[b, h] for b, h in chains]
            dec = [_chunk_decay(bgc[b][:, HEADS + h:HEADS + h + 1], ii, jj) for b, h in chains]
            gc_col, g_total, decay = ([d[i] for d in dec] for i in range(3))
            kb = each(lambda x, y: x * y, kc, beta)
            vb = each(lambda x, y: x * y, vc, beta)
            eg = [jnp.exp(g) for g in gc_col]
            kbg = each(lambda x, y: x * y, kb, eg)
            a = each(lambda x, y, d: jnp.where(strict, _dot_nt(x, y) * d, 0.0), kb, kc, decay)
            qk = each(lambda x, y, d: jnp.where(incl, _dot_nt(x, y) * d, 0.0), qc, kc, decay)
            w = each(_dot, t_inv, kbg)
            u = each(_dot, t_inv, vb)
            q_dec = each(lambda x, y: x * y, qc, eg)
            ek = each(lambda gt, g: jnp.exp(gt - g), g_total, gc_col)
            k_dec = each(lambda x, y: x * y, kc, ek)
            g_last = [jnp.exp(gt) for gt in g_total]
            v_new = each(lambda x, y, s: x - _dot(y, s), u, w, state)
            dv_new = each(lambda m, d, x, ds: _dot_tn(m, d) + _dot(x, ds), qk, doc, k_dec, d_state)
            dqk = each(lambda d, vn: jnp.where(incl, _dot_nt(d, vn), 0.0), doc, v_new)
            dq_dec = each(_dot_nt, doc, state)
            dk_dec = each(_dot_nt, v_new, d_state)
            dg_last = each(lambda s, ds: total(s * ds), state, d_state)
            new_d_state = each(lambda x, d, gl, ds, y, dvn: _dot_tn(x, d) + gl * ds - _dot_tn(y, dvn),
                               q_dec, doc, g_last, d_state, w, dv_new)
            dw = each(lambda dvn, s: -_dot_nt(dvn, s), dv_new, state)
            dt = each(lambda dvn, x, y, z: _dot_nt(dvn, x) + _dot_nt(y, z), dv_new, vb, dw, kbg)
            dvb = each(_dot_tn, t_inv, dv_new)
            dkbg = each(_dot_tn, t_inv, dw)
            t_dt = each(_dot_tn, t_inv, dt)
            da = each(lambda x, t: -jnp.where(strict, _dot_nt(x, t), 0.0), t_dt, t_inv)
            dm_a = each(lambda x, y: x * y, da, decay)
            dm_qk = each(lambda x, y: x * y, dqk, decay)
            e = each(lambda x, y, z, t: x * y + z * t, da, a, dqk, qk)
            dkb = each(lambda m, x, y, z: _dot(m, x) + y * z, dm_a, kc, dkbg, eg)
            dk = each(lambda m, x, m2, y, z, t, p, bt: _dot_tn(m, x) + _dot_tn(m2, y) + z * t + p * bt,
                      dm_a, kb, dm_qk, qc, dk_dec, ek, dkb, beta)
            dq = each(lambda m, x, y, z: _dot(m, x) + y * z, dm_qk, kc, dq_dec, eg)
            dbeta = each(lambda x, y, z, t: rowsum(x * y + z * t), dkb, kc, dvb, vc)
            dgc = each(lambda x, p, pd, r, rd, s, sd: rowsum(x) - rowsum(jnp.where(ii == jj, jnp.sum(x, axis=0, keepdims=True), 0.0))
                       + rowsum(p * pd - r * rd + s * sd), e, dq_dec, q_dec, dk_dec, k_dec, dkbg, kbg)
            d_total = each(lambda r, rd, x, gl: total(r * rd) + x * gl, dk_dec, k_dec, dg_last, g_last)
            dg = each(lambda x, t: rowsum(jnp.where(jj >= ii, jnp.sum(jnp.where(ii == jj, x, 0.0), axis=0, keepdims=True), 0.0)) + t,
                      dgc, d_total)
            dbg = [jnp.zeros((CHUNK, LANES), F32) for _ in range(batch)]
            for i, (b, h) in enumerate(chains):
                dstate_ref[b, h] = new_d_state[i]
                dk_ref[b, rows, _head_lanes(h)] = dk[i]
                dq_ref[b, rows, _head_lanes(h)] = dq[i]
                dv_ref[b, rows, _head_lanes(h)] = dvb[i] * beta[i]
                dbg[b] = dbg[b] + jnp.where(lane == h, dbeta[i], 0.0) + jnp.where(lane == HEADS + h, dg[i], 0.0)
            for b in range(batch):
                dbg_ref[b, rows, :] = dbg[b]
            return carry

        lax.fori_loop(0, seg_chunks, chunk, 0)

        @pl.when(pl.program_id(0) == segs - 1)
        def _():
            for cp in exchange:
                cp.wait_recv()
            for cp in exchange:
                cp.wait_send()

    rows_spec = lambda width: pl.BlockSpec((batch, seg_rows, width), lambda s: (0, segs - 1 - s, 0))
    per_chunk = lambda r, c: pl.BlockSpec((batch, HEADS, seg_chunks, r, c), lambda s: (0, 0, segs - 1 - s, 0, 0))
    as_seqs = lambda a: a.reshape(batch, rs, a.shape[-1])
    grad = jax.ShapeDtypeStruct((batch, rs, GDN_WIDTH), F32)
    wide = rows_spec(GDN_WIDTH)
    dq, dk, dv, dbg, *got = pl.pallas_call(
        body, name="gdn_bwd",
        out_shape=(grad, grad, grad, jax.ShapeDtypeStruct((batch, rs, LANES), F32))
        + tuple(jax.ShapeDtypeStruct((3,) + p.shape[1:], p.dtype) for p in parts),
        grid=(segs,),
        in_specs=[wide, wide, wide, wide, rows_spec(LANES), per_chunk(HEAD_DIM, HEAD_DIM), per_chunk(CHUNK, CHUNK)]
        + [_hbm()] * count,
        out_specs=(wide, wide, wide, rows_spec(LANES)) + (_hbm(),) * count,
        scratch_shapes=[pltpu.VMEM((batch, HEADS, HEAD_DIM, HEAD_DIM), F32), pltpu.SemaphoreType.DMA((3 * count,)),
                        pltpu.SemaphoreType.DMA((3 * count,))],
        compiler_params=_params("arbitrary"),
    )(as_seqs(do), as_seqs(q), as_seqs(k), as_seqs(v), as_seqs(bg), states, t_invs, *parts)
    return dq.reshape(n, GDN_WIDTH), dk.reshape(n, GDN_WIDTH), dv.reshape(n, GDN_WIDTH), dbg.reshape(n, LANES), got


def _lane_vec(vals, offset):
    k = vals.shape[1]
    return jnp.pad(vals, ((0, 0), (offset, LANES - offset - k)))


LATER = ("w_out", "w_gate", "w_up", "w_down")


def _halves(a):
    return a.reshape(a.shape[:-2] + (2, a.shape[-2] // 2, a.shape[-1]))


def _local_step(x, target, meta, norms, w_in_shard, conv_qkv, a_log, dt_bias, gdn_norm, conv_sc, later_shards, core_arg):
    batch, seq, d = x.shape
    tokens = N_META + seq
    pad_rows = (-tokens) % CHUNK
    rs = tokens + pad_rows
    x_offset = pad_rows + N_META
    n = batch * rs
    w_mix_pre, w_mix_post, w_ffn_pre, w_ffn_post = norms

    head = jnp.concatenate([jnp.zeros((pad_rows, d), F32), meta], axis=0)
    a_log_l = _lane_vec(a_log, HEADS)
    dt_bias_l = _lane_vec(dt_bias, HEADS)

    h0, u1, w_in_all = _embed(x, head, w_mix_pre, w_in_shard, rs)
    w_in_t = _in_to_kernel_order(w_in_all.reshape(N_CHIPS, -1, d))
    proj = _mm(u1, w_in_t, "nt", F32, "mm_proj")
    q = _qkv_fwd(proj, conv_qkv, "q", rs)
    k = _qkv_fwd(proj, conv_qkv, "k", rs)
    v = _qkv_fwd(proj, conv_qkv, "v", rs)
    bg = _gates_fwd(proj, a_log_l, dt_bias_l, rs, pad_rows)
    o, states, t_invs, gathered = _gdn_fwd(q, k, v, bg, rs, later_shards[:3])
    w_out, w_gate_t, w_up_t = (a.reshape(-1, d) for a in gathered)
    cat = _sc_fwd(proj, conv_sc, rs, _gate_fwd(o, proj, gdn_norm, rs))
    mix = _mm(cat, w_out, "nn", F32, "mm_mix")
    h1, u2 = _mix_residual(h0, mix, w_mix_post, w_ffn_pre)
    gate, up, act, w_down = _swiglu_fwd(u2, w_gate_t, w_up_t, later_shards[3])
    w_down = w_down.reshape(-1, d)
    ffn = _mm(act, w_down, "nn", F32, "mm_down")

    dh2, dffn, d_ffn_post, sq = _loss_head(h1, ffn, w_ffn_post, target, rs, x_offset)
    d_w_down = _mm(act, dffn, "tn", F32, "mm_dw_down")
    dgate, dup = _swiglu_bwd(dffn, w_down, gate, up)
    d_w_gate_t = _mm(dgate, u2, "tn", F32, "mm_dw_gate")
    d_w_up_t = _mm(dup, u2, "tn", F32, "mm_dw_up")
    du2 = _mm(dup, w_up_t, "nn", F32, "mm_du2_up", init=_mm(dgate, w_gate_t, "nn", F32, "mm_du2_gate"))
    by_chip = [_halves(g.reshape(N_CHIPS, -1, d)) for g in (d_w_gate_t, d_w_up_t, d_w_down)]
    dh1, dmix, d_ffn_pre, d_mix_post, got_sibling = _mid_bwd(h1, mix, w_mix_post, w_ffn_pre, dh2, du2, by_chip)
    dcat = _mm(dmix, w_out, "nt", F32, "mm_dcat")
    d_w_out = _halves(_mm(cat, dmix, "tn", F32, "mm_dw_out").reshape(N_CHIPS, -1, d))
    by_chip, got_sibling = [d_w_out] + by_chip, list(_exchange_siblings([d_w_out])) + got_sibling
    sums = [_add_sibling(a, b, core_arg, name) for name, a, b in zip(LATER, by_chip, got_sibling)]
    do, dproj, d_gdn_norm = _gate_bwd(dcat, o, proj, gdn_norm, rs)
    dproj, dscb, dscc, d_conv_sc = _sc_bwd(dcat, proj, conv_sc, rs, dproj)
    dq, dk, dv, dbg, got_chips = _gdn_bwd(do, q, k, v, bg, states, t_invs, rs, [send for _, send in sums[:3]])
    dproj, dwq = _qkv_bwd(dq, proj, conv_qkv, "q", rs, dproj)
    dproj, dwk = _qkv_bwd(dk, proj, conv_qkv, "k", rs, dproj)
    dproj, dwv = _qkv_bwd(dv, proj, conv_qkv, "v", rs, dproj)
    d_conv_qkv = jnp.concatenate([dwq, dwk, dwv], axis=1)
    dproj, d_a_log_l, d_dt_bias_l = _gates_bwd(proj, dbg, a_log_l, dt_bias_l, rs, pad_rows, dproj)
    dproj = lax.dynamic_update_slice(dproj, dscb, (0, (SC_COL + HEADS) * LANES))
    dproj = lax.dynamic_update_slice(dproj, dscc, (0, (SC_COL + 2 * HEADS) * LANES))
    d_w_in_t, got_down = _mm(dproj, u1, "tn", F32, "mm_dw_in", exchange=[sums[3][1]])
    got_chips.append(got_down)
    g_in = _halves(_in_from_kernel_order(d_w_in_t))
    sums.insert(0, _add_sibling(g_in, _exchange_siblings([g_in])[0], core_arg, "w_in"))
    du1, got_in = _mm(dproj, w_in_t, "nn", F32, "mm_du1", exchange=[sums[0][1]])
    got_chips.insert(0, got_in)
    grad_x, d_meta, d_mix_pre = _in_bwd(h0, w_mix_pre, dh1, du1, rs, pad_rows, x_offset)

    grads = dict(
        meta_tokens=d_meta,
        mix_pre_norm=d_mix_pre, mix_post_norm=d_mix_post, ffn_pre_norm=d_ffn_pre, ffn_post_norm=d_ffn_post,
        conv_qkv=d_conv_qkv,
        a_log=d_a_log_l[:, HEADS:2 * HEADS], dt_bias=d_dt_bias_l[:, HEADS:2 * HEADS],
        gdn_norm=d_gdn_norm, conv_sc=d_conv_sc,
    )
    return sq, grad_x, grads, [(part, got) for (part, _), got in zip(sums, got_chips)]


MATRICES = ("w_in", "w_out", "w_gate", "w_up", "w_down")
IN_SHARD = IN_WIDTH // N_CHIPS
IN_SHARD_PAD = 928


IN_SEGMENTS = ((0, 0, 4 * GDN_WIDTH), (4 * GDN_WIDTH, IN_WIDTH - 2 * HEADS, 2 * HEADS),
               (4 * GDN_WIDTH + 2 * HEADS, 4 * GDN_WIDTH, 3 * SC_WIDTH))
SUBLANES = 8
PACKED_ROWS = 16


def _in_to_kernel_order(by_chip):
    d = by_chip.shape[-1]
    tl = _pick(d, (256, 128))
    runs = []
    for ref0, ker0, count in IN_SEGMENTS:
        row = ref0
        while row < ref0 + count:
            chip, at = divmod(row, IN_SHARD)
            take = min(ref0 + count - row, IN_SHARD - at)
            runs.append((ker0 + row - ref0, take, chip * IN_SHARD_PAD + at))
            row += take

    def body(w_ref, o_ref):
        o_ref[...] = jnp.zeros_like(o_ref)
        for out0, rows, src0 in runs:
            a0 = out0 // PACKED_ROWS * PACKED_ROWS
            a1 = -(-(out0 + rows) // PACKED_ROWS) * PACKED_ROWS
            window = w_ref[pl.ds(src0 - (out0 - a0), a1 - a0), :]
            row = a0 + lax.broadcasted_iota(jnp.int32, (a1 - a0, 1), 0)
            keep = jnp.logical_and(row >= out0, row < out0 + rows)
            o_ref[a0:a1, :] = jnp.where(keep, window, o_ref[a0:a1, :])

    return pl.pallas_call(
        body, name="in_to_kernel_order", out_shape=jax.ShapeDtypeStruct((IN_PAD, d), by_chip.dtype), grid=(d // tl,),
        in_specs=[pl.BlockSpec((N_CHIPS * IN_SHARD_PAD, tl), lambda j: (0, j))],
        out_specs=pl.BlockSpec((IN_PAD, tl), lambda j: (0, j)),
        compiler_params=_params("parallel"),
    )(by_chip.reshape(N_CHIPS * IN_SHARD_PAD, d))


def _in_from_kernel_order(g_t):
    d = g_t.shape[-1]
    tl = _pick(d, (256, 128))

    def body(g_ref, o_ref):
        row = lax.broadcasted_iota(jnp.int32, (IN_SHARD_PAD, 1), 0)
        for chip in range(N_CHIPS):
            first = chip * IN_SHARD
            runs = []
            for ref0, ker0, count in IN_SEGMENTS:
                lo, hi = max(ref0, first), min(ref0 + count, first + IN_SHARD)
                if lo < hi:
                    runs.append((lo - first, hi - lo, ker0 + lo - ref0))
            val = jnp.zeros((IN_SHARD_PAD, tl), F32)
            patches = []
            for out0, rows, src0 in runs:
                start = src0 - out0
                if 0 <= start <= IN_PAD - IN_SHARD_PAD:
                    window = g_ref[pl.ds(start, IN_SHARD_PAD), :]
                    val = jnp.where(jnp.logical_and(row >= out0, row < out0 + rows), window, val)
                else:
                    patches.append((out0, rows, src0))
            o_ref[chip] = val
            for out0, rows, src0 in patches:
                a0 = out0 // SUBLANES * SUBLANES
                a1 = -(-(out0 + rows) // SUBLANES) * SUBLANES
                window = g_ref[pl.ds(src0 - (out0 - a0), a1 - a0), :]
                keep = jnp.logical_and(row[a0:a1] >= out0, row[a0:a1] < out0 + rows)
                o_ref[chip, a0:a1, :] = jnp.where(keep, window, o_ref[chip, a0:a1, :])

    return pl.pallas_call(
        body, name="in_from_kernel_order", out_shape=jax.ShapeDtypeStruct((N_CHIPS, IN_SHARD_PAD, d), F32), grid=(d // tl,),
        in_specs=[pl.BlockSpec((IN_PAD, tl), lambda j: (0, j))],
        out_specs=pl.BlockSpec((N_CHIPS, IN_SHARD_PAD, tl), lambda j: (0, 0, j)),
        compiler_params=_params("parallel"),
    )(g_t)


PACK_LANES = 3 * GDN_WIDTH
PACKED = dict(mix_pre_norm=(0, 1, 0, D_MODEL), mix_post_norm=(1, 1, 0, D_MODEL), ffn_pre_norm=(2, 1, 0, D_MODEL),
              ffn_post_norm=(3, 1, 0, D_MODEL), a_log=(4, 1, 0, HEADS), dt_bias=(5, 1, 0, HEADS), loss=(6, 1, 0, 1),
              gdn_norm=(7, 1, 0, HEAD_DIM), conv_qkv=(8, GDN_CONV, 0, 3 * GDN_WIDTH), conv_sc=(0, SC_CONV, D_MODEL, SC_WIDTH),
              meta_tokens=(16, N_META, 0, D_MODEL))
PACK_ROWS = 32
SHARDED_SMALL = ("conv_qkv", "conv_sc", "meta_tokens")


def _pack_small(values):
    names = list(PACKED)

    def body(*refs):
        out_ref = refs[-1]
        out_ref[...] = jnp.zeros_like(out_ref)
        for name, ref in zip(names, refs):
            row, rows, lane0, lanes = PACKED[name]
            out_ref[row:row + rows, lane0:lane0 + lanes] = ref[...]

    return pl.pallas_call(body, name="pack_small", out_shape=jax.ShapeDtypeStruct((PACK_ROWS, PACK_LANES), F32))(
        *[values[name] for name in names])


def _sum_devices(packed_all, chip):
    names = list(PACKED)

    def body(chip_ref, all_ref, *rest):
        shard_refs, out_refs = rest[:len(SHARDED_SMALL)], rest[len(SHARDED_SMALL):]

        def total(ref, rows, lanes):
            acc = ref[0, rows, lanes]
            for k in range(1, 8):
                acc = acc + ref[k, rows, lanes]
            return acc

        for name, out in zip(names, out_refs):
            row, rows, lane0, lanes = PACKED[name]
            if name in SHARDED_SMALL:
                out[...] = total(shard_refs[SHARDED_SMALL.index(name)], slice(0, rows), slice(None))
            else:
                out[...] = total(all_ref, slice(row, row + rows), slice(lane0, lane0 + lanes))

    def shard_spec(name):
        row, rows, lane0, lanes = PACKED[name]
        height, width = max(rows, 8), lanes // N_CHIPS
        assert row % height == 0 and lane0 % width == 0
        return pl.BlockSpec((8, height, width), lambda i, chip_ref: (0, row // height, lane0 // width + chip_ref[0]))

    def out_shape(name):
        _, rows, _, lanes = PACKED[name]
        return jax.ShapeDtypeStruct((rows, lanes // N_CHIPS if name in SHARDED_SMALL else lanes), F32)

    whole = lambda shape: pl.BlockSpec(shape, lambda i, chip_ref: (0,) * len(shape))
    outs = pl.pallas_call(
        body, name="sum_devices", out_shape=tuple(out_shape(n) for n in names),
        grid_spec=pltpu.PrefetchScalarGridSpec(
            num_scalar_prefetch=1, grid=(1,),
            in_specs=[whole(packed_all.shape)] + [shard_spec(n) for n in SHARDED_SMALL],
            out_specs=tuple(whole(out_shape(n).shape) for n in names)),
    )(chip, packed_all, *[packed_all] * len(SHARDED_SMALL))
    return dict(zip(names, outs))


def _hbm():
    return pl.BlockSpec(memory_space=pl.ANY)


def _place():
    x, y, c = lax.axis_index("x"), lax.axis_index("y"), lax.axis_index("c")
    chips = ((1 - x, y), (x, 1 - y), (1 - x, 1 - y))
    return x, y, c, chips


def _remote(src, dst, send_sems, recv_sems, k, to):
    return pltpu.make_async_remote_copy(src_ref=src, dst_ref=dst, send_sem=send_sems.at[k], recv_sem=recv_sems.at[k],
                                        device_id=to, device_id_type=MESH)


GATHER_SEMS = 7


def _gather_copies(w_refs, out_refs, send_sems, recv_sems):
    x, y, c, chips = _place()
    mine = 2 * x + y
    sibling = (x, y, 1 - c)
    copy = functools.partial(_remote, send_sems=send_sems, recv_sems=recv_sems)
    direct, landed, passing, from_sibling = [], [], [], []
    for i, (w, o) in enumerate(zip(w_refs, out_refs)):
        k = GATHER_SEMS * i
        direct.append(copy(w, o.at[mine], k=k, to=sibling))
        from_sibling.append(copy(w, o.at[mine], k=k, to=sibling))
        for j, (cx, cy) in enumerate(chips):
            theirs = 2 * cx + cy
            direct.append(copy(w.at[c], o.at[mine, c], k=k + 1 + j, to=(cx, cy, c)))
            landed.append(copy(w.at[c], o.at[theirs, c], k=k + 1 + j, to=sibling))
            passing.append(copy(o.at[theirs, c], o.at[theirs, c], k=k + 4 + j, to=sibling))
            from_sibling.append(copy(w.at[c], o.at[theirs, 1 - c], k=k + 4 + j, to=sibling))
    return direct, landed, passing, from_sibling


def _gather_finish(copies):
    direct, landed, passing, from_sibling = copies
    for arrival, forward in zip(landed, passing):
        arrival.wait_recv()
        forward.start()
    for arrival in from_sibling:
        arrival.wait_recv()
    for cp in direct + passing:
        cp.wait_send()


def _gather_weights(pieces, smalls):
    count, extra = len(pieces), len(smalls)
    total = count + extra

    def body(*refs):
        w_refs, s_refs = refs[:count], refs[count:total]
        out_refs, sall_refs = refs[total:total + count], refs[total + count:2 * total]
        send_sems, recv_sems, local_sems = refs[2 * total:]
        x, y, c, chips = _place()
        mine = 2 * x + y
        own = [pltpu.make_async_copy(s, sall.at[mine], local_sems.at[i]) for i, (s, sall) in enumerate(zip(s_refs, sall_refs))]
        small = [_remote(s, sall.at[mine], send_sems, recv_sems, GATHER_SEMS * count + 3 * i + j, (cx, cy, c))
                 for i, (s, sall) in enumerate(zip(s_refs, sall_refs)) for j, (cx, cy) in enumerate(chips)]
        copies = _gather_copies(w_refs, out_refs, send_sems, recv_sems)
        for cp in own + small + copies[0]:
            cp.start()
        _gather_finish(copies)
        for cp in small:
            cp.wait_recv()
        for cp in small:
            cp.wait_send()
        for cp in own:
            cp.wait()

    sems = GATHER_SEMS * count + 3 * extra
    return pl.pallas_call(
        body, name="gather_weights",
        out_shape=tuple(jax.ShapeDtypeStruct((N_CHIPS,) + p.shape, p.dtype) for p in list(pieces) + list(smalls)),
        in_specs=[_hbm()] * total, out_specs=(_hbm(),) * total,
        scratch_shapes=[pltpu.SemaphoreType.DMA((sems,)), pltpu.SemaphoreType.DMA((sems,)), pltpu.SemaphoreType.DMA((extra,))],
    )(*pieces, *smalls)


def _sibling_copies(g_refs, got_refs, send_sems, recv_sems):
    x, y, c, _ = _place()
    return [_remote(g.at[:, 1 - c], got, send_sems, recv_sems, i, (x, y, 1 - c)) for i, (g, got) in enumerate(zip(g_refs, got_refs))]


def _exchange_siblings(grads):
    count = len(grads)

    def body(*refs):
        copies = _sibling_copies(refs[:count], refs[count:2 * count], *refs[2 * count:])
        for cp in copies:
            cp.start()
        for cp in copies:
            cp.wait_recv()
        for cp in copies:
            cp.wait_send()

    return pl.pallas_call(
        body, name="exchange_siblings",
        out_shape=tuple(jax.ShapeDtypeStruct((g.shape[0],) + g.shape[2:], F32) for g in grads),
        in_specs=[_hbm()] * count, out_specs=(_hbm(),) * count,
        scratch_shapes=[pltpu.SemaphoreType.DMA((count,)), pltpu.SemaphoreType.DMA((count,))],
    )(*grads)


def _chip_copies(p_refs, got_refs, send_sems, recv_sems):
    x, y, c, chips = _place()
    return [_remote(p.at[2 * cx + cy], got.at[j], send_sems, recv_sems, 3 * i + j, (cx, cy, c))
            for i, (p, got) in enumerate(zip(p_refs, got_refs)) for j, (cx, cy) in enumerate(chips)]


def _share_halves(halves, small):
    count = len(halves)

    def body(*refs):
        h_refs, s_ref = refs[:count], refs[count]
        full_refs, sall_ref = refs[count + 1:2 * count + 1], refs[2 * count + 1]
        send_sems, recv_sems, local_sem = refs[2 * count + 2:]
        x, y, c, _ = _place()
        me = 4 * x + 2 * y + c
        own = pltpu.make_async_copy(s_ref, sall_ref.at[me], local_sem)
        own.start()
        copies = [_remote(h.at[c], full.at[c], send_sems, recv_sems, i, (x, y, 1 - c))
                  for i, (h, full) in enumerate(zip(h_refs, full_refs))]
        for k in range(7):
            dx, dy, dc = ((k + 1) >> 2) & 1, ((k + 1) >> 1) & 1, (k + 1) & 1
            peer = (1 - x if dx else x, 1 - y if dy else y, 1 - c if dc else c)
            copies.append(_remote(s_ref, sall_ref.at[me], send_sems, recv_sems, count + k, peer))
        for cp in copies:
            cp.start()
        for cp in copies:
            cp.wait_recv()
        for cp in copies:
            cp.wait_send()
        own.wait()

    return pl.pallas_call(
        body, name="share_halves",
        out_shape=tuple(jax.ShapeDtypeStruct(h.shape, h.dtype) for h in halves) + (jax.ShapeDtypeStruct((8,) + small.shape, F32),),
        in_specs=[_hbm()] * (count + 1), out_specs=(_hbm(),) * (count + 1), input_output_aliases={i: i for i in range(count)},
        scratch_shapes=[pltpu.SemaphoreType.DMA((count + 7,)), pltpu.SemaphoreType.DMA((count + 7,)), pltpu.SemaphoreType.DMA],
    )(*halves, small)


def _add_sibling(grad, got, core, name):
    chips, _, rows, cols = grad.shape

    def body(core_ref, g_ref, r_ref, sum_ref, send_ref):
        s = g_ref[...] + r_ref[...]
        sum_ref[...] = s
        send_ref[...] = s.astype(send_ref.dtype)

    block = pl.BlockSpec((None, rows, cols), lambda p, core_ref: (p, 0, 0))
    return pl.pallas_call(
        body, name="add_sibling_" + name,
        out_shape=(jax.ShapeDtypeStruct((chips, rows, cols), F32), jax.ShapeDtypeStruct((chips, rows, cols), BF16)),
        grid_spec=pltpu.PrefetchScalarGridSpec(
            num_scalar_prefetch=1, grid=(chips,),
            in_specs=[pl.BlockSpec((None, None, rows, cols), lambda p, core_ref: (p, core_ref[0], 0, 0)), block],
            out_specs=(block, block)),
        compiler_params=_params("parallel"),
    )(core, grad, got)


def _add_chips(part, got, chip_core, name):
    _, rows, cols = part.shape
    tr = rows // 2 if rows % 32 == 0 else rows

    def body(place_ref, p_ref, r_ref, o_ref):
        o_ref[...] = ((p_ref[...] + r_ref[0].astype(F32)) + r_ref[1].astype(F32)) + r_ref[2].astype(F32)

    return pl.pallas_call(
        body, name="add_chips_" + name, out_shape=jax.ShapeDtypeStruct((2, rows, cols), F32),
        grid_spec=pltpu.PrefetchScalarGridSpec(
            num_scalar_prefetch=1, grid=(rows // tr,),
            in_specs=[pl.BlockSpec((None, tr, cols), lambda i, place_ref: (place_ref[0], i, 0)),
                      pl.BlockSpec((3, tr, cols), lambda i, place_ref: (0, i, 0))],
            out_specs=pl.BlockSpec((None, tr, cols), lambda i, place_ref: (place_ref[1], i, 0))),
        compiler_params=_params("parallel"),
    )(chip_core, part, got)


def _adamw(w, g, m, v, name):
    rows, cols = w.shape
    tr = _pick(rows, (3592, 256, 352, 176, 128, 64, 32, 16, 8))

    def body(w_ref, g_ref, m_ref, v_ref, d_ref, nm_ref, nv_ref):
        d_ref[...], nm_ref[...], nv_ref[...] = _adamw_math(w_ref[...], g_ref[...], m_ref[...], v_ref[...])

    block = pl.BlockSpec((tr, cols), lambda i: (i, 0))
    shape = jax.ShapeDtypeStruct((rows, cols), F32)
    return pl.pallas_call(
        body, name="adamw_" + name, out_shape=(shape, shape, shape), grid=(rows // tr,),
        in_specs=[block] * 4, out_specs=(block,) * 3, compiler_params=_params("parallel"),
    )(w, g, m, v)


def _adamw_math(w, g, m, v):
    m = ADAM_B1 * m + (1.0 - ADAM_B1) * g
    v = ADAM_B2 * v + (1.0 - ADAM_B2) * (g * g)
    m_hat = m / (1.0 - ADAM_B1 ** ADAM_STEP)
    v_hat = v / (1.0 - ADAM_B2 ** ADAM_STEP)
    return -ADAM_LR * (m_hat / (jnp.sqrt(v_hat) + ADAM_EPS) + ADAM_WD * w), m, v


def _adamw_small(ws, gs, ms, vs):
    count = len(ws)

    def body(*refs):
        ins, outs = refs[:4 * count], refs[4 * count:]
        for i in range(count):
            outs[i][...], outs[count + i][...], outs[2 * count + i][...] = _adamw_math(
                ins[i][...], ins[count + i][...], ins[2 * count + i][...], ins[3 * count + i][...])

    shapes = tuple(jax.ShapeDtypeStruct(w.shape, F32) for w in ws)
    out = pl.pallas_call(body, name="adamw_small", out_shape=shapes * 3)(*ws, *gs, *ms, *vs)
    return out[:count], out[count:2 * count], out[2 * count:]


WEIGHTS = ("meta_tokens", "mix_pre_norm", "mix_post_norm", "ffn_pre_norm", "ffn_post_norm", "w_in", "conv_qkv", "a_log",
           "dt_bias", "gdn_norm", "conv_sc", "w_out", "w_gate", "w_up", "w_down")


def kernel(x, meta_tokens, mix_pre_norm, mix_post_norm, ffn_pre_norm, ffn_post_norm, w_in, conv_qkv, a_log, dt_bias, gdn_norm, conv_sc, w_out, w_gate, w_up, w_down, loss_target, m_meta_tokens, m_mix_pre_norm, m_mix_post_norm, m_ffn_pre_norm, m_ffn_post_norm, m_w_in, m_conv_qkv, m_a_log, m_dt_bias, m_gdn_norm, m_conv_sc, m_w_out, m_w_gate, m_w_up, m_w_down, v_meta_tokens, v_mix_pre_norm, v_mix_post_norm, v_ffn_pre_norm, v_ffn_post_norm, v_w_in, v_conv_qkv, v_a_log, v_dt_bias, v_gdn_norm, v_conv_sc, v_w_out, v_w_gate, v_w_up, v_w_down):
    d = x.shape[-1]
    two_d = lambda a: a.reshape(a.shape[-2:])
    weights = dict(zip(WEIGHTS, (meta_tokens, mix_pre_norm, mix_post_norm, ffn_pre_norm, ffn_post_norm, w_in, conv_qkv, a_log,
                                 dt_bias, gdn_norm, conv_sc, w_out, w_gate, w_up, w_down)))
    m_in = dict(zip(WEIGHTS, (m_meta_tokens, m_mix_pre_norm, m_mix_post_norm, m_ffn_pre_norm, m_ffn_post_norm, m_w_in, m_conv_qkv,
                              m_a_log, m_dt_bias, m_gdn_norm, m_conv_sc, m_w_out, m_w_gate, m_w_up, m_w_down)))
    v_in = dict(zip(WEIGHTS, (v_meta_tokens, v_mix_pre_norm, v_mix_post_norm, v_ffn_pre_norm, v_ffn_post_norm, v_w_in, v_conv_qkv,
                              v_a_log, v_dt_bias, v_gdn_norm, v_conv_sc, v_w_out, v_w_gate, v_w_up, v_w_down)))
    core = lax.axis_index("c")
    chip = 2 * lax.axis_index("x") + lax.axis_index("y")
    core_arg = core.reshape(1).astype(jnp.int32)
    chip_core = jnp.stack([chip, core]).astype(jnp.int32)
    whole = lambda a: a.reshape(a.shape[:-3] + (2 * a.shape[-2], d))
    by_rows = lambda n, a: two_d(a).T if n in ("w_in", "w_gate", "w_up") else two_d(a)

    shard = {n: by_rows(n, weights[n]).astype(MXU_DTYPE) for n in MATRICES}
    shard["w_in"] = jnp.pad(shard["w_in"], ((0, IN_SHARD_PAD - IN_SHARD), (0, 0)))
    small_all = _gather_weights([], [two_d(weights[n]) for n in SHARDED_SMALL])
    conv_qkv_full, conv_sc_full, meta_full = (jnp.concatenate([a[p] for p in range(N_CHIPS)], axis=1) for a in small_all)

    sq, grad_x, g, sums = _local_step(
        x, loss_target, meta_full, (mix_pre_norm, mix_post_norm, ffn_pre_norm, ffn_post_norm), _halves(shard["w_in"]),
        conv_qkv_full, a_log, dt_bias, gdn_norm, conv_sc_full, [_halves(shard[n]) for n in LATER], core_arg)

    totals = [_add_chips(part, got, chip_core, n) for n, (part, got) in zip(MATRICES, sums)]
    *shared, packed_all = _share_halves(totals, _pack_small(dict(g, loss=sq)))
    grads = {n: whole(a) for n, a in zip(MATRICES, shared)}
    grads["w_in"] = grads["w_in"][:IN_SHARD]
    grads.update(_sum_devices(packed_all, chip.reshape(1).astype(jnp.int32)))
    loss = (0.5 / d) * grads.pop("loss")[0, 0]

    small = [n for n in WEIGHTS if n not in MATRICES]
    updates = dict(zip(small, zip(*_adamw_small(*([by_rows(n, params[n]) for n in small] for params in (weights, grads, m_in, v_in))))))
    outs = [[], [], [], []]
    for n in WEIGHTS:
        shape = weights[n].shape
        if n in MATRICES:
            updates[n] = _adamw(by_rows(n, weights[n]), grads[n], by_rows(n, m_in[n]), by_rows(n, v_in[n]), n)
        for out, a in zip(outs, (grads[n], *updates[n])):
            out.append((a.T if n in ("w_in", "w_gate", "w_up") else a).reshape(shape))
    return (loss, grad_x, *outs[0], *outs[1], *outs[2], *outs[3])
```

```python
import functools

import jax
import jax.numpy as jnp
from jax import lax
from jax.experimental import pallas as pl
from jax.experimental.pallas import tpu as pltpu

F32 = jnp.float32
BF16 = jnp.bfloat16
MXU_DTYPE = jnp.bfloat16
MESH = pl.DeviceIdType.MESH

D_MODEL = 1024
N_META = 16
HEADS = 4
HEAD_DIM = 128
GDN_WIDTH = HEADS * HEAD_DIM
GDN_CONV = 4
CHUNK = 64
SC_WIDTH = D_MODEL - GDN_WIDTH
SC_CONV = 3
D_FF = 2816
IN_WIDTH = 4 * GDN_WIDTH + 2 * HEADS + 3 * SC_WIDTH
IN_PAD = 3840
BA_COL = (4 * GDN_WIDTH + 3 * SC_WIDTH) // 128
EPS = 1e-6
LANES = 128
N_CHIPS = 4
VMEM_LIMIT = 48 * 2 ** 20
MM_VMEM_BUDGET = 42 * 2 ** 20

ADAM_LR = 0.001
ADAM_B1 = 0.9
ADAM_B2 = 0.999
ADAM_EPS = 1e-08
ADAM_WD = 0.01
ADAM_STEP = 10


def _pick(n, candidates):
    for c in candidates:
        if n % c == 0:
            return c
    return n


def _row_tile(n):
    return _pick(n, (352, 256, 176, 128, 64, 32, 16, 8))


def _params(*sem):
    return pltpu.CompilerParams(dimension_semantics=sem, vmem_limit_bytes=VMEM_LIMIT)


def _sigmoid(x):
    return 0.5 * jnp.tanh(0.5 * x) + 0.5


def _softplus(x):
    return jnp.maximum(x, 0.0) + jnp.log(1.0 + jnp.exp(-jnp.abs(x)))


def _dsilu(x, s):
    return s * (1.0 + x * (1.0 - s))


def _mm(a, b, mode, out_dtype, name, init=None, exchange=None):
    if mode == "tn":
        k_dim, m_dim = a.shape
    else:
        m_dim, k_dim = a.shape
    n_dim = b.shape[0] if mode == "nt" else b.shape[1]
    tn = _pick(n_dim, (1408, 1280, 1024, 768, 512, 256, 128))
    if mode == "tn":
        tm = _pick(m_dim, (1408, 1280, 1024, 512, 256, 128))
        tk = _pick(k_dim, (2112, 1408, 1280, 1056, 1024, 512, 256, 128))
    else:
        tk = k_dim
        blocks = lambda rows: 2 * (2 * rows * tk + 2 * tk * tn + 4 * rows * tn * (1 if init is None else 2))
        tm = next((t for t in (2112, 1056, 1024, 704, 512, 256, 128) if m_dim % t == 0 and blocks(t) <= MM_VMEM_BUDGET), m_dim)
    nk = k_dim // tk
    if mode == "nn":
        a_spec = pl.BlockSpec((tm, tk), lambda i, j, k: (i, k))
        b_spec = pl.BlockSpec((tk, tn), lambda i, j, k: (k, j))
        dims = (((1,), (0,)), ((), ()))
    elif mode == "nt":
        a_spec = pl.BlockSpec((tm, tk), lambda i, j, k: (i, k))
        b_spec = pl.BlockSpec((tn, tk), lambda i, j, k: (j, k))
        dims = (((1,), (1,)), ((), ()))
    else:
        a_spec = pl.BlockSpec((tk, tm), lambda i, j, k: (k, i))
        b_spec = pl.BlockSpec((tk, tn), lambda i, j, k: (k, j))
        dims = (((0,), (0,)), ((), ()))

    out_spec = pl.BlockSpec((tm, tn), lambda i, j, k: (i, j))
    grid = (m_dim // tm, n_dim // tn, nk)
    parts = () if exchange is None else tuple(exchange)
    count = len(parts)
    first_in = 2 if init is None else 3

    assert out_dtype == F32

    def body(a_ref, b_ref, *rest):
        o_ref = rest[first_in - 2 + count]
        k = pl.program_id(2)
        step = (pl.program_id(0) * grid[1] + pl.program_id(1)) * nk + k
        if count:
            copies = _chip_copies(rest[first_in - 2:first_in - 2 + count], rest[first_in - 1 + count:first_in - 1 + 2 * count],
                                  *rest[first_in - 1 + 2 * count:])

            @pl.when(step == 0)
            def _():
                for cp in copies:
                    cp.start()

        p = lax.dot_general(a_ref[...], b_ref[...], dims, preferred_element_type=F32)
        if nk == 1:
            o_ref[...] = p if init is None else rest[0][...] + p
        else:
            @pl.when(k == 0)
            def _():
                o_ref[...] = p if init is None else rest[0][...] + p

            @pl.when(k > 0)
            def _():
                o_ref[...] += p

        if count:
            @pl.when(step == grid[0] * grid[1] * nk - 1)
            def _():
                for cp in copies:
                    cp.wait_recv()
                for cp in copies:
                    cp.wait_send()

    out = pl.pallas_call(
        body, name=name,
        out_shape=(jax.ShapeDtypeStruct((m_dim, n_dim), out_dtype),)
        + tuple(jax.ShapeDtypeStruct((3,) + p.shape[1:], p.dtype) for p in parts),
        grid=grid,
        in_specs=[a_spec, b_spec] + ([] if init is None else [out_spec]) + [_hbm()] * count,
        out_specs=(out_spec,) + (_hbm(),) * count,
        scratch_shapes=[pltpu.SemaphoreType.DMA((3 * count,)), pltpu.SemaphoreType.DMA((3 * count,))] if count else [],
        compiler_params=_params(*(("arbitrary",) * 3 if count else ("parallel", "parallel", "arbitrary"))),
    )(a, b, *(() if init is None else (init,)), *parts)
    return out[0] if not count else out


def _rms_apply(x, w):
    r = lax.rsqrt(jnp.mean(x * x, axis=-1, keepdims=True) + EPS)
    return x * r * w


def _rms_bwd(x, w, dy):
    r = lax.rsqrt(jnp.mean(x * x, axis=-1, keepdims=True) + EPS)
    xh = x * r
    dyw = dy * w
    dx = r * (dyw - xh * jnp.mean(dyw * xh, axis=-1, keepdims=True))
    return dx, jnp.sum(dy * xh, axis=0, keepdims=True)


def _accumulate(ref, first, value):
    @pl.when(first)
    def _():
        ref[...] = value

    @pl.when(jnp.logical_not(first))
    def _():
        ref[...] += value


def _rows(tr, width):
    return pl.BlockSpec((tr, width), lambda i: (i, 0))


def _vec(width):
    return pl.BlockSpec((1, width), lambda i: (0, 0))


def _embed(x, head, w_pre, w_shard, rows_per_seq):
    batch, seq, d = x.shape
    x_offset = head.shape[0]
    tr = _row_tile(rows_per_seq)
    tiles_per_seq = rows_per_seq // tr
    n = batch * rows_per_seq

    def body(x_ref, head_ref, w_ref, ws_ref, h0_ref, u_ref, wall_ref, send_sems, recv_sems):
        gather = _gather_copies([ws_ref], [wall_ref], send_sems, recv_sems)
        i = pl.program_id(0)
        tile = lax.rem(i, tiles_per_seq)

        @pl.when(i == 0)
        def _():
            for cp in gather[0]:
                cp.start()

        rows = jnp.concatenate([head_ref[...], x_ref[0:tr - x_offset, :]], axis=0)
        if tiles_per_seq > 1:
            start = pl.multiple_of(jnp.maximum(tile * tr - x_offset, 0), SUBLANES)
            rows = jnp.where(tile == 0, rows, x_ref[pl.ds(start, tr), :])
        h0_ref[...] = rows
        u_ref[...] = _rms_apply(rows, w_ref[...]).astype(u_ref.dtype)

        @pl.when(i == n // tr - 1)
        def _():
            _gather_finish(gather)

    return pl.pallas_call(
        body, name="embed",
        out_shape=(jax.ShapeDtypeStruct((n, d), F32), jax.ShapeDtypeStruct((n, d), MXU_DTYPE),
                   jax.ShapeDtypeStruct((N_CHIPS,) + w_shard.shape, w_shard.dtype)),
        grid=(n // tr,),
        in_specs=[pl.BlockSpec((None, seq, d), lambda i: (i // tiles_per_seq, 0, 0)),
                  pl.BlockSpec((x_offset, d), lambda i: (0, 0)), _vec(d), _hbm()],
        out_specs=(_rows(tr, d), _rows(tr, d), _hbm()),
        scratch_shapes=[pltpu.SemaphoreType.DMA((GATHER_SEMS,)), pltpu.SemaphoreType.DMA((GATHER_SEMS,))],
        compiler_params=_params("arbitrary"),
    )(x, head, w_pre, w_shard)


def _mix_residual(h0, mix, w_post, w_pre):
    n, d = h0.shape
    tr = _row_tile(n)

    def body(h0_ref, mix_ref, wpost_ref, wpre_ref, h1_ref, u2_ref):
        h1 = h0_ref[...] + _rms_apply(mix_ref[...], wpost_ref[...])
        h1_ref[...] = h1
        u2_ref[...] = _rms_apply(h1, wpre_ref[...]).astype(u2_ref.dtype)

    return pl.pallas_call(
        body, name="mix_residual",
        out_shape=(jax.ShapeDtypeStruct((n, d), F32), jax.ShapeDtypeStruct((n, d), MXU_DTYPE)), grid=(n // tr,),
        in_specs=[_rows(tr, d), _rows(tr, d), _vec(d), _vec(d)], out_specs=(_rows(tr, d), _rows(tr, d)),
        compiler_params=_params("parallel"),
    )(h0, mix, w_post, w_pre)


NT_DIMS = (((1,), (1,)), ((), ()))


def _ffn_tiles(n):
    return _pick(n, (1056, 704, 512, 256, 128)), _pick(D_FF, (1408, 256, 128))


def _swiglu_fwd(u, w_gate_t, w_up_t, w_next):
    n, d = u.shape
    tm, tn = _ffn_tiles(n)
    grid = (D_FF // tn, n // tm)

    def body(u_ref, wg_ref, wu_ref, wn_ref, g_ref, up_ref, act_ref, wall_ref, send_sems, recv_sems):
        gather = _gather_copies([wn_ref], [wall_ref], send_sems, recv_sems)
        step = pl.program_id(0) * grid[1] + pl.program_id(1)

        @pl.when(step == 0)
        def _():
            for cp in gather[0]:
                cp.start()

        a = u_ref[...]
        g = lax.dot_general(a, wg_ref[...], NT_DIMS, preferred_element_type=F32)
        up = lax.dot_general(a, wu_ref[...], NT_DIMS, preferred_element_type=F32)
        g_ref[...] = g.astype(g_ref.dtype)
        up_ref[...] = up.astype(up_ref.dtype)
        act_ref[...] = (g * _sigmoid(g) * up).astype(act_ref.dtype)

        @pl.when(step == grid[0] * grid[1] - 1)
        def _():
            _gather_finish(gather)

    tile = pl.BlockSpec((tm, tn), lambda j, i: (i, j))
    weight = pl.BlockSpec((tn, d), lambda j, i: (j, 0))
    wide = jax.ShapeDtypeStruct((n, D_FF), MXU_DTYPE)
    return pl.pallas_call(
        body, name="swiglu_fwd",
        out_shape=(wide, wide, jax.ShapeDtypeStruct((n, D_FF), MXU_DTYPE),
                   jax.ShapeDtypeStruct((N_CHIPS,) + w_next.shape, w_next.dtype)),
        grid=grid,
        in_specs=[pl.BlockSpec((tm, d), lambda j, i: (i, 0)), weight, weight, _hbm()], out_specs=(tile, tile, tile, _hbm()),
        scratch_shapes=[pltpu.SemaphoreType.DMA((GATHER_SEMS,)), pltpu.SemaphoreType.DMA((GATHER_SEMS,))],
        compiler_params=_params("arbitrary", "arbitrary"),
    )(u, w_gate_t, w_up_t, w_next)


def _swiglu_bwd(dffn, w_down, gate, up):
    n, d = dffn.shape
    tm, tn = _ffn_tiles(n)

    def body(dy_ref, w_ref, g_ref, u_ref, dg_ref, du_ref):
        da = lax.dot_general(dy_ref[...], w_ref[...], NT_DIMS, preferred_element_type=F32)
        g = g_ref[...].astype(F32)
        s = _sigmoid(g)
        dg_ref[...] = (da * u_ref[...].astype(F32) * _dsilu(g, s)).astype(dg_ref.dtype)
        du_ref[...] = (da * g * s).astype(du_ref.dtype)

    tile = pl.BlockSpec((tm, tn), lambda j, i: (i, j))
    shape = jax.ShapeDtypeStruct((n, D_FF), MXU_DTYPE)
    return pl.pallas_call(
        body, name="swiglu_bwd", out_shape=(shape, shape), grid=(D_FF // tn, n // tm),
        in_specs=[pl.BlockSpec((tm, d), lambda j, i: (i, 0)), pl.BlockSpec((tn, d), lambda j, i: (j, 0)), tile, tile],
        out_specs=(tile, tile), compiler_params=_params("parallel", "parallel"),
    )(dffn, w_down, gate, up)


def _loss_head(h1, ffn, w_post, target, rows_per_seq, x_offset):
    n, d = h1.shape
    tr = _row_tile(rows_per_seq)
    tiles_per_seq = rows_per_seq // tr
    seq = target.shape[1]

    def seq_rows(t_ref, tile):
        first = jnp.concatenate([jnp.zeros((x_offset, d), F32), t_ref[0:tr - x_offset, :]], axis=0)
        if tiles_per_seq == 1:
            return first
        start = pl.multiple_of(jnp.maximum(tile * tr - x_offset, 0), SUBLANES)
        return jnp.where(tile == 0, first, t_ref[pl.ds(start, tr), :])

    def body(h1_ref, ffn_ref, w_ref, t_ref, dh2_ref, dffn_ref, dw_ref, sq_ref):
        i = pl.program_id(0)
        tile = lax.rem(i, tiles_per_seq)
        w = w_ref[...]
        f = ffn_ref[...]
        r = lax.rsqrt(jnp.mean(f * f, axis=-1, keepdims=True) + EPS)
        fh = f * r
        row = tile * tr + lax.broadcasted_iota(jnp.int32, (tr, 1), 0)
        err = jnp.where(row >= x_offset, h1_ref[...] + fh * w - seq_rows(t_ref, tile), 0.0)
        dh2 = err * (1.0 / d)
        dh2_ref[...] = dh2
        dyw = dh2 * w
        dffn_ref[...] = (r * (dyw - fh * jnp.mean(dyw * fh, axis=-1, keepdims=True))).astype(dffn_ref.dtype)
        _accumulate(dw_ref, i == 0, jnp.sum(dh2 * fh, axis=0, keepdims=True))
        _accumulate(sq_ref, i == 0, jnp.sum(jnp.sum(err * err, axis=1, keepdims=True), axis=0, keepdims=True))

    return pl.pallas_call(
        body, name="loss_head",
        out_shape=(jax.ShapeDtypeStruct((n, d), F32), jax.ShapeDtypeStruct((n, d), MXU_DTYPE),
                   jax.ShapeDtypeStruct((1, d), F32), jax.ShapeDtypeStruct((1, 1), F32)),
        grid=(n // tr,),
        in_specs=[_rows(tr, d), _rows(tr, d), _vec(d), pl.BlockSpec((None, seq, d), lambda i: (i // tiles_per_seq, 0, 0))],
        out_specs=(_rows(tr, d), _rows(tr, d), _vec(d), _vec(1)),
        compiler_params=_params("arbitrary"),
    )(h1, ffn, w_post, target)


def _mid_bwd(h1, mix, w_mix_post, w_ffn_pre, dh2, du2, grads):
    n, d = h1.shape
    tr = _row_tile(n)
    count = len(grads)

    def body(h1_ref, mix_ref, wpost_ref, wpre_ref, dh2_ref, du2_ref, *rest):
        g_refs, (dh1_ref, dmix_ref, dwpre_ref, dwpost_ref), got_refs = rest[:count], rest[count:count + 4], rest[count + 4:2 * count + 4]
        exchange = _sibling_copies(g_refs, got_refs, *rest[2 * count + 4:])
        i = pl.program_id(0)

        @pl.when(i == 0)
        def _():
            for cp in exchange:
                cp.start()

        dx, dwpre = _rms_bwd(h1_ref[...], wpre_ref[...], du2_ref[...])
        dh1 = dh2_ref[...] + dx
        dh1_ref[...] = dh1
        dmix, dwpost = _rms_bwd(mix_ref[...], wpost_ref[...], dh1)
        dmix_ref[...] = dmix.astype(dmix_ref.dtype)
        _accumulate(dwpre_ref, i == 0, dwpre)
        _accumulate(dwpost_ref, i == 0, dwpost)

        @pl.when(i == n // tr - 1)
        def _():
            for cp in exchange:
                cp.wait_recv()
            for cp in exchange:
                cp.wait_send()

    dh1, dmix, dwpre, dwpost, *got = pl.pallas_call(
        body, name="mid_bwd",
        out_shape=(jax.ShapeDtypeStruct((n, d), F32), jax.ShapeDtypeStruct((n, d), MXU_DTYPE),
                   jax.ShapeDtypeStruct((1, d), F32), jax.ShapeDtypeStruct((1, d), F32))
        + tuple(jax.ShapeDtypeStruct((g.shape[0],) + g.shape[2:], F32) for g in grads),
        grid=(n // tr,),
        in_specs=[_rows(tr, d), _rows(tr, d), _vec(d), _vec(d), _rows(tr, d), _rows(tr, d)] + [_hbm()] * count,
        out_specs=(_rows(tr, d), _rows(tr, d), _vec(d), _vec(d)) + (_hbm(),) * count,
        scratch_shapes=[pltpu.SemaphoreType.DMA((count,)), pltpu.SemaphoreType.DMA((count,))],
        compiler_params=_params("arbitrary"),
    )(h1, mix, w_mix_post, w_ffn_pre, dh2, du2, *grads)
    return dh1, dmix, dwpre, dwpost, got


def _in_bwd(h0, w_pre, dh1, du1, rows_per_seq, pad_rows, x_offset):
    n, d = h0.shape
    tr = _row_tile(rows_per_seq)
    tiles_per_seq = rows_per_seq // tr
    seq = rows_per_seq - x_offset

    def body(h0_ref, w_ref, dh1_ref, du1_ref, gx_ref, dmeta_ref, dw_ref):
        i = pl.program_id(0)
        tile = lax.rem(i, tiles_per_seq)
        dx, dw = _rms_bwd(h0_ref[...], w_ref[...], du1_ref[...])
        dh0 = dh1_ref[...] + dx
        _accumulate(dw_ref, i == 0, dw)

        @pl.when(tile == 0)
        def _():
            gx_ref[0:tr - x_offset, :] = dh0[x_offset:, :]
            _accumulate(dmeta_ref, i == 0, dh0[pad_rows:x_offset, :])

        if tiles_per_seq > 1:
            @pl.when(tile > 0)
            def _():
                gx_ref[pl.ds(pl.multiple_of(tile * tr - x_offset, SUBLANES), tr), :] = dh0

    return pl.pallas_call(
        body, name="in_bwd",
        out_shape=(jax.ShapeDtypeStruct((n // rows_per_seq, seq, d), F32), jax.ShapeDtypeStruct((x_offset - pad_rows, d), F32),
                   jax.ShapeDtypeStruct((1, d), F32)),
        grid=(n // tr,),
        in_specs=[_rows(tr, d), _vec(d), _rows(tr, d), _rows(tr, d)],
        out_specs=(pl.BlockSpec((None, seq, d), lambda i: (i // tiles_per_seq, 0, 0)),
                   pl.BlockSpec((x_offset - pad_rows, d), lambda i: (0, 0)), _vec(d)),
        compiler_params=_params("arbitrary"),
    )(h0, w_pre, dh1, du1)


def _lane_is(lo, hi):
    lane = lax.broadcasted_iota(jnp.int32, (1, LANES), 1)
    return jnp.logical_and(lane >= lo, lane < hi)


def _gates_fwd(proj, a_log_l, dt_bias_l, rows_per_seq, pad_rows):
    n = proj.shape[0]
    tr = _row_tile(rows_per_seq)
    tiles_per_seq = rows_per_seq // tr

    def body(p_ref, a_ref, dt_ref, o_ref):
        x = p_ref[...]
        row = lax.rem(pl.program_id(0), tiles_per_seq) * tr + lax.broadcasted_iota(jnp.int32, (tr, 1), 0)
        g = -jnp.exp(a_ref[...]) * _softplus(x + dt_ref[...])
        val = jnp.where(_lane_is(0, HEADS), _sigmoid(x), jnp.where(_lane_is(HEADS, 2 * HEADS), g, 0.0))
        o_ref[...] = jnp.where(row >= pad_rows, val, 0.0)

    return pl.pallas_call(
        body, name="gates_fwd", out_shape=jax.ShapeDtypeStruct((n, LANES), F32), grid=(n // tr,),
        in_specs=[pl.BlockSpec((tr, LANES), lambda i: (i, BA_COL)), _vec(LANES), _vec(LANES)],
        out_specs=_rows(tr, LANES), compiler_params=_params("parallel"),
    )(proj, a_log_l, dt_bias_l)


def _gates_bwd(proj, dbg, a_log_l, dt_bias_l, rows_per_seq, pad_rows, dproj):
    n = proj.shape[0]
    tr = _row_tile(rows_per_seq)
    tiles_per_seq = rows_per_seq // tr

    def body(p_ref, d_ref, a_ref, dt_ref, _, dx_ref, da_ref, ddt_ref):
        i = pl.program_id(0)
        x = p_ref[...]
        d = d_ref[...]
        row = lax.rem(i, tiles_per_seq) * tr + lax.broadcasted_iota(jnp.int32, (tr, 1), 0)
        live = row >= pad_rows
        beta = _sigmoid(x)
        ea = jnp.exp(a_ref[...])
        xa = x + dt_ref[...]
        g = -ea * _softplus(xa)
        is_g = _lane_is(HEADS, 2 * HEADS)
        d_alogit = jnp.where(jnp.logical_and(live, is_g), d * (-ea) * _sigmoid(xa), 0.0)
        d_blogit = jnp.where(jnp.logical_and(live, _lane_is(0, HEADS)), d * beta * (1.0 - beta), 0.0)
        dx_ref[:, :LANES] = (d_alogit + d_blogit).astype(dx_ref.dtype)
        dx_ref[:, LANES:] = jnp.zeros((tr, LANES), dx_ref.dtype)
        _accumulate(da_ref, i == 0, jnp.sum(jnp.where(jnp.logical_and(live, is_g), d * g, 0.0), axis=0, keepdims=True))
        _accumulate(ddt_ref, i == 0, jnp.sum(d_alogit, axis=0, keepdims=True))

    return pl.pallas_call(
        body, name="gates_bwd",
        out_shape=(jax.ShapeDtypeStruct(dproj.shape, dproj.dtype), jax.ShapeDtypeStruct((1, LANES), F32),
                   jax.ShapeDtypeStruct((1, LANES), F32)),
        grid=(n // tr,),
        in_specs=[pl.BlockSpec((tr, LANES), lambda i: (i, BA_COL)), _rows(tr, LANES), _vec(LANES), _vec(LANES), _hbm()],
        out_specs=(pl.BlockSpec((tr, 2 * LANES), lambda i: (i, BA_COL // 2)), _vec(LANES), _vec(LANES)),
        input_output_aliases={4: 0},
        compiler_params=_params("arbitrary"),
    )(proj, dbg, a_log_l, dt_bias_l, dproj)


HALO = 8


def _halo_scratch(rs):
    return pltpu.VMEM((rs + 2 * HALO, LANES), F32)


def _stage(ref, x):
    rs = x.shape[0]
    ref[0:HALO, :] = jnp.zeros((HALO, LANES), F32)
    ref[HALO + rs:, :] = jnp.zeros((HALO, LANES), F32)
    ref[HALO:HALO + rs, :] = x


def _shifted(ref, k, rs):
    return ref[pl.ds(HALO - k, rs), :]


def _causal_conv(x, x_staged, w, width):
    acc = w[width - 1:width, :] * x
    for i in range(width - 1):
        acc = acc + w[i:i + 1, :] * _shifted(x_staged, width - 1 - i, x.shape[0])
    return acc


def _anti_causal_conv(dy, dy_staged, w, width):
    acc = w[width - 1:width, :] * dy
    for i in range(width - 1):
        acc = acc + w[i:i + 1, :] * _shifted(dy_staged, -(width - 1 - i), dy.shape[0])
    return acc


def _conv_weight_grad(dy, x, x_staged, width):
    taps = [_shifted(x_staged, width - 1 - i, x.shape[0]) for i in range(width - 1)] + [x]
    return jnp.concatenate([jnp.sum(dy * tap, axis=0, keepdims=True) for tap in taps], axis=0)


def _seq_cols(rs, col0, heads):
    return pl.BlockSpec((rs, heads * LANES), lambda j, b: (b, col0 // heads + j))


def _tap_cols(width, col0, heads):
    return pl.BlockSpec((width, heads * LANES), lambda j, b: (0, col0 // heads + j))


def _lanes_of(h):
    return slice(h * LANES, (h + 1) * LANES)


def _qkv_fwd(proj, conv_w, kind, rs):
    n = proj.shape[0]
    col0 = {"q": 0, "k": HEADS, "v": 2 * HEADS}[kind]
    hb = HEADS

    def body(p_ref, w_ref, o_ref, staged):
        for h in range(hb):
            pre = p_ref[:, _lanes_of(h)]
            _stage(staged, pre)
            c = _causal_conv(pre, staged, w_ref[:, _lanes_of(h)], GDN_CONV)
            s = c * _sigmoid(c)
            if kind != "v":
                s = s * lax.rsqrt(jnp.sum(s * s, axis=-1, keepdims=True) + EPS)
            if kind == "q":
                s = s * (HEAD_DIM ** -0.5)
            o_ref[:, _lanes_of(h)] = s

    return pl.pallas_call(
        body, name="qkv_fwd_" + kind, out_shape=jax.ShapeDtypeStruct((n, GDN_WIDTH), F32), grid=(HEADS // hb, n // rs),
        in_specs=[_seq_cols(rs, col0, hb), _tap_cols(GDN_CONV, col0, hb)],
        out_specs=_seq_cols(rs, 0, hb), scratch_shapes=[_halo_scratch(rs)], compiler_params=_params("parallel", "parallel"),
    )(proj, conv_w)


def _qkv_bwd(dy, proj, conv_w, kind, rs, dproj):
    n = proj.shape[0]
    col0 = {"q": 0, "k": HEADS, "v": 2 * HEADS}[kind]
    hb = HEADS

    def body(dy_ref, p_ref, w_ref, _, dp_ref, dw_ref, pre_staged, dc_staged):
        for h in range(hb):
            lanes = _lanes_of(h)
            pre = p_ref[:, lanes]
            w = w_ref[:, lanes]
            _stage(pre_staged, pre)
            c = _causal_conv(pre, pre_staged, w, GDN_CONV)
            sg = _sigmoid(c)
            s = c * sg
            ds = dy_ref[:, lanes]
            if kind == "q":
                ds = ds * (HEAD_DIM ** -0.5)
            if kind != "v":
                r = lax.rsqrt(jnp.sum(s * s, axis=-1, keepdims=True) + EPS)
                sh = s * r
                ds = r * (ds - sh * jnp.sum(ds * sh, axis=-1, keepdims=True))
            dc = ds * _dsilu(c, sg)
            _stage(dc_staged, dc)
            dp_ref[:, lanes] = _anti_causal_conv(dc, dc_staged, w, GDN_CONV).astype(dp_ref.dtype)
            _accumulate(dw_ref.at[:, lanes], pl.program_id(1) == 0, _conv_weight_grad(dc, pre, pre_staged, GDN_CONV))

    return pl.pallas_call(
        body, name="qkv_bwd_" + kind,
        out_shape=(jax.ShapeDtypeStruct(dproj.shape, dproj.dtype), jax.ShapeDtypeStruct((GDN_CONV, GDN_WIDTH), F32)),
        grid=(HEADS // hb, n // rs),
        in_specs=[_seq_cols(rs, 0, hb), _seq_cols(rs, col0, hb), _tap_cols(GDN_CONV, col0, hb), _hbm()],
        out_specs=(_seq_cols(rs, col0, hb), _tap_cols(GDN_CONV, 0, hb)), input_output_aliases={3: 0},
        scratch_shapes=[_halo_scratch(rs), _halo_scratch(rs)],
        compiler_params=_params("parallel", "arbitrary"),
    )(dy, proj, conv_w, dproj)


SC_COL = 4 * HEADS


def _sc_fwd(proj, conv_w, rs, cat):
    n = proj.shape[0]

    hb = 2

    def body(x_ref, b_ref, c_ref, w_ref, _, y_ref, staged):
        for h in range(hb):
            lanes = _lanes_of(h)
            u = c_ref[:, lanes] * x_ref[:, lanes]
            _stage(staged, u)
            y_ref[:, lanes] = (b_ref[:, lanes] * _causal_conv(u, staged, w_ref[:, lanes], SC_CONV)).astype(y_ref.dtype)

    return pl.pallas_call(
        body, name="sc_fwd", out_shape=jax.ShapeDtypeStruct(cat.shape, cat.dtype), grid=(HEADS // hb, n // rs),
        in_specs=[_seq_cols(rs, SC_COL, hb), _seq_cols(rs, SC_COL + 4, hb), _seq_cols(rs, SC_COL + 8, hb),
                  _tap_cols(SC_CONV, 0, hb), _hbm()],
        out_specs=_seq_cols(rs, HEADS, hb), input_output_aliases={4: 0}, scratch_shapes=[_halo_scratch(rs)],
        compiler_params=_params("parallel", "parallel"),
    )(proj, proj, proj, conv_w, cat)


def _sc_bwd(dcat, proj, conv_w, rs, dproj):
    n = proj.shape[0]
    hb = 2

    def body(dy_ref, x_ref, b_ref, c_ref, w_ref, _, dx_ref, db_ref, dc_ref, dw_ref, u_staged, dcv_staged):
        for h in range(hb):
            lanes = _lanes_of(h)
            w = w_ref[:, lanes]
            x = x_ref[:, lanes]
            cc = c_ref[:, lanes]
            u = cc * x
            _stage(u_staged, u)
            dy = dy_ref[:, lanes]
            db_ref[:, lanes] = (dy * _causal_conv(u, u_staged, w, SC_CONV)).astype(db_ref.dtype)
            dcv = dy * b_ref[:, lanes]
            _stage(dcv_staged, dcv)
            du = _anti_causal_conv(dcv, dcv_staged, w, SC_CONV)
            dx_ref[:, lanes] = (du * cc).astype(dx_ref.dtype)
            dc_ref[:, lanes] = (du * x).astype(dc_ref.dtype)
            _accumulate(dw_ref.at[:, lanes], pl.program_id(1) == 0, _conv_weight_grad(dcv, u, u_staged, SC_CONV))

    piece = jax.ShapeDtypeStruct((n, SC_WIDTH), MXU_DTYPE)
    return pl.pallas_call(
        body, name="sc_bwd",
        out_shape=(jax.ShapeDtypeStruct(dproj.shape, dproj.dtype), piece, piece, jax.ShapeDtypeStruct((SC_CONV, SC_WIDTH), F32)),
        grid=(HEADS // hb, n // rs),
        in_specs=[_seq_cols(rs, HEADS, hb), _seq_cols(rs, SC_COL, hb), _seq_cols(rs, SC_COL + 4, hb),
                  _seq_cols(rs, SC_COL + 8, hb), _tap_cols(SC_CONV, 0, hb), _hbm()],
        out_specs=(_seq_cols(rs, SC_COL, hb), _seq_cols(rs, 0, hb), _seq_cols(rs, 0, hb), _tap_cols(SC_CONV, 0, hb)),
        input_output_aliases={5: 0},
        scratch_shapes=[_halo_scratch(rs), _halo_scratch(rs)],
        compiler_params=_params("parallel", "arbitrary"),
    )(dcat, proj, proj, proj, conv_w, dproj)


Z_COL = 3 * HEADS


def _gate_fwd(o, proj, gdn_norm, rs):
    n = proj.shape[0]

    hb = HEADS

    def body(o_ref, z_ref, w_ref, y_ref):
        for h in range(hb):
            lanes = _lanes_of(h)
            z = z_ref[:, lanes]
            y_ref[:, lanes] = (_rms_apply(o_ref[:, lanes], w_ref[...]) * z * _sigmoid(z)).astype(y_ref.dtype)

    return pl.pallas_call(
        body, name="gate_fwd", out_shape=jax.ShapeDtypeStruct((n, D_MODEL), MXU_DTYPE), grid=(HEADS // hb, n // rs),
        in_specs=[_seq_cols(rs, 0, hb), _seq_cols(rs, Z_COL, hb), pl.BlockSpec((1, LANES), lambda j, b: (0, 0))],
        out_specs=_seq_cols(rs, 0, hb), compiler_params=_params("parallel", "parallel"),
    )(o, proj, gdn_norm)


def _gate_bwd(dcat, o, proj, gdn_norm, rs):
    n = proj.shape[0]
    hb = 2

    def body(dy_ref, o_ref, z_ref, w_ref, do_ref, dz_ref, dw_ref):
        w = w_ref[...]
        dw_step = jnp.zeros((1, LANES), F32)
        for h in range(hb):
            lanes = _lanes_of(h)
            z = z_ref[:, lanes]
            o = o_ref[:, lanes]
            dy = dy_ref[:, lanes]
            s = _sigmoid(z)
            dz_ref[:, lanes] = (dy * _rms_apply(o, w) * _dsilu(z, s)).astype(dz_ref.dtype)
            do, dw = _rms_bwd(o, w, dy * z * s)
            do_ref[:, lanes] = do
            dw_step = dw_step + dw
        _accumulate(dw_ref, jnp.logical_and(pl.program_id(0) == 0, pl.program_id(1) == 0), dw_step)

    return pl.pallas_call(
        body, name="gate_bwd",
        out_shape=(jax.ShapeDtypeStruct((n, GDN_WIDTH), F32), jax.ShapeDtypeStruct((n, IN_PAD), MXU_DTYPE),
                   jax.ShapeDtypeStruct((1, LANES), F32)),
        grid=(HEADS // hb, n // rs),
        in_specs=[_seq_cols(rs, 0, hb), _seq_cols(rs, 0, hb), _seq_cols(rs, Z_COL, hb), pl.BlockSpec((1, LANES), lambda j, b: (0, 0))],
        out_specs=(_seq_cols(rs, 0, hb), _seq_cols(rs, Z_COL, hb), pl.BlockSpec((1, LANES), lambda j, b: (0, 0))),
        compiler_params=_params("arbitrary", "arbitrary"),
    )(dcat, o, proj, gdn_norm)


def _dot(a, b):
    return jnp.dot(a.astype(MXU_DTYPE), b.astype(MXU_DTYPE), preferred_element_type=F32)


def _dot_nt(a, b):
    return lax.dot_general(a.astype(MXU_DTYPE), b.astype(MXU_DTYPE), (((1,), (1,)), ((), ())),
                           preferred_element_type=F32)


def _dot_tn(a, b):
    return lax.dot_general(a.astype(MXU_DTYPE), b.astype(MXU_DTYPE), (((0,), (0,)), ((), ())),
                           preferred_element_type=F32)


def _split(x):
    hi = x.astype(MXU_DTYPE)
    return hi, (x - hi.astype(F32)).astype(MXU_DTYPE)


def _dot_split(a, b):
    mm = functools.partial(jnp.dot, preferred_element_type=F32)
    return mm(a[0], b[0]) + (mm(a[0], b[1]) + mm(a[1], b[0]))


def _unit_lower_inverses(mats, eye):
    inv = [eye - a for a in mats]
    power = [_split(a) for a in mats]
    span = 2
    while span < CHUNK:
        power = [_split(_dot_split(p, p)) for p in power]
        inv = [i + _dot_split(_split(i), p) for i, p in zip(inv, power)]
        span *= 2
    return inv


def _chunk_masks():
    ii = lax.broadcasted_iota(jnp.int32, (CHUNK, CHUNK), 0)
    jj = lax.broadcasted_iota(jnp.int32, (CHUNK, CHUNK), 1)
    return ii, jj


def _chunk_decay(g_col, ii, jj):
    incl = ii >= jj
    g_row = jnp.sum(jnp.where(ii == jj, g_col, 0.0), axis=0, keepdims=True)
    gc_col = jnp.sum(jnp.where(incl, g_row, 0.0), axis=1, keepdims=True)
    gc_row = jnp.sum(jnp.where(ii <= jj, g_col, 0.0), axis=0, keepdims=True)
    g_total = jnp.sum(g_row, axis=1, keepdims=True)
    decay = jnp.where(incl, jnp.exp(jnp.where(incl, gc_col - gc_row, 0.0)), 0.0)
    return gc_col, g_total, decay


def _gdn_segments(rs, candidates):
    chunks = rs // CHUNK
    seg_chunks = _pick(chunks, candidates)
    return chunks, seg_chunks, chunks // seg_chunks


def _head_lanes(h):
    return slice(h * HEAD_DIM, (h + 1) * HEAD_DIM)


def _gdn_fwd(q, k, v, bg, rs, pieces):
    n = q.shape[0]
    batch = n // rs
    chunks, seg_chunks, segs = _gdn_segments(rs, (11, 8, 4, 2))
    seg_rows = seg_chunks * CHUNK
    chains = [(b, h) for b in range(batch) for h in range(HEADS)]
    each = lambda f, *lists: [f(*args) for args in zip(*lists)]
    count = len(pieces)

    def body(q_ref, k_ref, v_ref, bg_ref, *rest):
        w_refs, (o_ref, s_ref, t_ref), out_refs = rest[:count], rest[count:count + 3], rest[count + 3:2 * count + 3]
        state_ref, send_sems, recv_sems = rest[2 * count + 3:]
        gather = _gather_copies(w_refs, out_refs, send_sems, recv_sems)

        @pl.when(pl.program_id(0) == 0)
        def _():
            state_ref[...] = jnp.zeros_like(state_ref)
            for cp in gather[0]:
                cp.start()

        ii, jj = _chunk_masks()
        incl = ii >= jj
        eye = (ii == jj).astype(F32)

        def chunk(c, carry):
            rows = pl.ds(pl.multiple_of(c * CHUNK, CHUNK), CHUNK)
            bgc = [bg_ref[b, rows, :] for b in range(batch)]
            qc = [q_ref[b, rows, _head_lanes(h)] for b, h in chains]
            kc = [k_ref[b, rows, _head_lanes(h)] for b, h in chains]
            vc = [v_ref[b, rows, _head_lanes(h)] for b, h in chains]
            beta = [bgc[b][:, h:h + 1] for b, h in chains]
            state = [state_ref[b, h] for b, h in chains]
            dec = [_chunk_decay(bgc[b][:, HEADS + h:HEADS + h + 1], ii, jj) for b, h in chains]
            gc_col, g_total, decay = ([d[i] for d in dec] for i in range(3))
            kb = each(lambda x, y: x * y, kc, beta)
            a = each(lambda x, y, d: jnp.where(ii > jj, _dot_nt(x, y) * d, 0.0), kb, kc, decay)
            t_inv = _unit_lower_inverses(a, eye)
            eg = [jnp.exp(g) for g in gc_col]
            u = each(lambda t, x, y: _dot(t, x * y), t_inv, vc, beta)
            w = each(lambda t, x, e: _dot(t, x * e), t_inv, kb, eg)
            qk = each(lambda x, y, d: jnp.where(incl, _dot_nt(x, y) * d, 0.0), qc, kc, decay)
            v_new = each(lambda x, y, s: x - _dot(y, s), u, w, state)
            o = each(lambda x, e, s, m, vn: _dot(x * e, s) + _dot(m, vn), qc, eg, state, qk, v_new)
            new_state = each(lambda s, gt, x, g, vn: s * jnp.exp(gt) + _dot_tn(x * jnp.exp(gt - g), vn),
                             state, g_total, kc, gc_col, v_new)
            for i, (b, h) in enumerate(chains):
                s_ref[b, h, c] = state[i]
                t_ref[b, h, c] = t_inv[i]
                o_ref[b, rows, _head_lanes(h)] = o[i]
                state_ref[b, h] = new_state[i]
            return carry

        lax.fori_loop(0, seg_chunks, chunk, 0)

        @pl.when(pl.program_id(0) == segs - 1)
        def _():
            _gather_finish(gather)

    rows_spec = lambda width: pl.BlockSpec((batch, seg_rows, width), lambda s: (0, s, 0))
    per_chunk = lambda r, c: pl.BlockSpec((batch, HEADS, seg_chunks, r, c), lambda s: (0, 0, s, 0, 0))
    as_seqs = lambda a: a.reshape(batch, rs, a.shape[-1])
    sems = GATHER_SEMS * count
    o, states, t_invs, *gathered = pl.pallas_call(
        body, name="gdn_fwd",
        out_shape=(jax.ShapeDtypeStruct((batch, rs, GDN_WIDTH), F32),
                   jax.ShapeDtypeStruct((batch, HEADS, chunks, HEAD_DIM, HEAD_DIM), F32),
                   jax.ShapeDtypeStruct((batch, HEADS, chunks, CHUNK, CHUNK), F32))
        + tuple(jax.ShapeDtypeStruct((N_CHIPS,) + p.shape, p.dtype) for p in pieces),
        grid=(segs,),
        in_specs=[rows_spec(GDN_WIDTH), rows_spec(GDN_WIDTH), rows_spec(GDN_WIDTH), rows_spec(LANES)] + [_hbm()] * count,
        out_specs=(rows_spec(GDN_WIDTH), per_chunk(HEAD_DIM, HEAD_DIM), per_chunk(CHUNK, CHUNK)) + (_hbm(),) * count,
        scratch_shapes=[pltpu.VMEM((batch, HEADS, HEAD_DIM, HEAD_DIM), F32), pltpu.SemaphoreType.DMA((sems,)),
                        pltpu.SemaphoreType.DMA((sems,))],
        compiler_params=_params("arbitrary"),
    )(as_seqs(q), as_seqs(k), as_seqs(v), as_seqs(bg), *pieces)
    return o.reshape(n, GDN_WIDTH), states, t_invs, gathered


def _gdn_bwd(do, q, k, v, bg, states, t_invs, rs, parts):
    n = q.shape[0]
    batch = n // rs
    chunks, seg_chunks, segs = _gdn_segments(rs, (3, 4, 2))
    seg_rows = seg_chunks * CHUNK
    chains = [(b, h) for b in range(batch) for h in range(HEADS)]
    each = lambda f, *lists: [f(*args) for args in zip(*lists)]
    count = len(parts)

    def body(do_ref, q_ref, k_ref, v_ref, bg_ref, s_ref, t_ref, *rest):
        p_refs, (dq_ref, dk_ref, dv_ref, dbg_ref), got_refs = rest[:count], rest[count:count + 4], rest[count + 4:2 * count + 4]
        dstate_ref, send_sems, recv_sems = rest[2 * count + 4:]
        exchange = _chip_copies(p_refs, got_refs, send_sems, recv_sems)

        @pl.when(pl.program_id(0) == 0)
        def _():
            dstate_ref[...] = jnp.zeros_like(dstate_ref)
            for cp in exchange:
                cp.start()

        ii, jj = _chunk_masks()
        incl = ii >= jj
        strict = ii > jj
        lane = lax.broadcasted_iota(jnp.int32, (1, LANES), 1)

        def rowsum(x):
            return jnp.sum(x, axis=1, keepdims=True)

        def total(x):
            return jnp.sum(rowsum(x), axis=0, keepdims=True)

        def chunk(step, carry):
            c = seg_chunks - 1 - step
            rows = pl.ds(pl.multiple_of(c * CHUNK, CHUNK), CHUNK)
            bgc = [bg_ref[b, rows, :] for b in range(batch)]
            qc = [q_ref[b, rows, _head_lanes(h)] for b, h in chains]
            kc = [k_ref[b, rows, _head_lanes(h)] for b, h in chains]
            vc = [v_ref[b, rows, _head_lanes(h)] for b, h in chains]
            doc = [do_ref[b, rows, _head_lanes(h)] for b, h in chains]
            beta = [bgc[b][:, h:h + 1] for b, h in chains]
            state = [s_ref[b, h, c] for b, h in chains]
            t_inv = [t_ref[b, h, c] for b, h in chains]
            d_state = [dstate_ref[b, h] for b, h in chains]
            dec = [_chunk_decay(bgc[b][:, HEADS + h:HEADS + h + 1], ii, jj) for b, h in chains]
            gc_col, g_total, decay = ([d[i] for d in dec] for i in range(3))
            kb = each(lambda x, y: x * y, kc, beta)
            vb = each(lambda x, y: x * y, vc, beta)
            eg = [jnp.exp(g) for g in gc_col]
            kbg = each(lambda x, y: x * y, kb, eg)
            a = each(lambda x, y, d: jnp.where(strict, _dot_nt(x, y) * d, 0.0), kb, kc, decay)
            qk = each(lambda x, y, d: jnp.where(incl, _dot_nt(x, y) * d, 0.0), qc, kc, decay)
            w = each(_dot, t_inv, kbg)
            u = each(_dot, t_inv, vb)
            q_dec = each(lambda x, y: x * y, qc, eg)
            ek = each(lambda gt, g: jnp.exp(gt - g), g_total, gc_col)
            k_dec = each(lambda x, y: x * y, kc, ek)
            g_last = [jnp.exp(gt) for gt in g_total]
            v_new = each(lambda x, y, s: x - _dot(y, s), u, w, state)
            dv_new = each(lambda m, d, x, ds: _dot_tn(m, d) + _dot(x, ds), qk, doc, k_dec, d_state)
            dqk = each(lambda d, vn: jnp.where(incl, _dot_nt(d, vn), 0.0), doc, v_new)
            dq_dec = each(_dot_nt, doc, state)
            dk_dec = each(_dot_nt, v_new, d_state)
            dg_last = each(lambda s, ds: total(s * ds), state, d_state)
            new_d_state = each(lambda x, d, gl, ds, y, dvn: _dot_tn(x, d) + gl * ds - _dot_tn(y, dvn),
                               q_dec, doc, g_last, d_state, w, dv_new)
            dw = each(lambda dvn, s: -_dot_nt(dvn, s), dv_new, state)
            dt = each(lambda dvn, x, y, z: _dot_nt(dvn, x) + _dot_nt(y, z), dv_new, vb, dw, kbg)
            dvb = each(_dot_tn, t_inv, dv_new)
            dkbg = each(_dot_tn, t_inv, dw)
            t_dt = each(_dot_tn, t_inv, dt)
            da = each(lambda x, t: -jnp.where(strict, _dot_nt(x, t), 0.0), t_dt, t_inv)
            dm_a = each(lambda x, y: x * y, da, decay)
            dm_qk = each(lambda x, y: x * y, dqk, decay)
            e = each(lambda x, y, z, t: x * y + z * t, da, a, dqk, qk)
            dkb = each(lambda m, x, y, z: _dot(m, x) + y * z, dm_a, kc, dkbg, eg)
            dk = each(lambda m, x, m2, y, z, t, p, bt: _dot_tn(m, x) + _dot_tn(m2, y) + z * t + p * bt,
                      dm_a, kb, dm_qk, qc, dk_dec, ek, dkb, beta)
            dq = each(lambda m, x, y, z: _dot(m, x) + y * z, dm_qk, kc, dq_dec, eg)
            dbeta = each(lambda x, y, z, t: rowsum(x * y + z * t), dkb, kc, dvb, vc)
            dgc = each(lambda x, p, pd, r, rd, s, sd: rowsum(x) - rowsum(jnp.where(ii == jj, jnp.sum(x, axis=0, keepdims=True), 0.0))
                       + rowsum(p * pd - r * rd + s * sd), e, dq_dec, q_dec, dk_dec, k_dec, dkbg, kbg)
            d_total = each(lambda r, rd, x, gl: total(r * rd) + x * gl, dk_dec, k_dec, dg_last, g_last)
            dg = each(lambda x, t: rowsum(jnp.where(jj >= ii, jnp.sum(jnp.where(ii == jj, x, 0.0), axis=0, keepdims=True), 0.0)) + t,
                      dgc, d_total)
            dbg = [jnp.zeros((CHUNK, LANES), F32) for _ in range(batch)]
            for i, (b, h) in enumerate(chains):
                dstate_ref[b, h] = new_d_state[i]
                dk_ref[b, rows, _head_lanes(h)] = dk[i]
                dq_ref[b, rows, _head_lanes(h)] = dq[i]
                dv_ref[b, rows, _head_lanes(h)] = dvb[i] * beta[i]
                dbg[b] = dbg[b] + jnp.where(lane == h, dbeta[i], 0.0) + jnp.where(lane == HEADS + h, dg[i], 0.0)
            for b in range(batch):
                dbg_ref[b, rows, :] = dbg[b]
            return carry

        lax.fori_loop(0, seg_chunks, chunk, 0)

        @pl.when(pl.program_id(0) == segs - 1)
        def _():
            for cp in exchange:
                cp.wait_recv()
            for cp in exchange:
                cp.wait_send()

    rows_spec = lambda width: pl.BlockSpec((batch, seg_rows, width), lambda s: (0, segs - 1 - s, 0))
    per_chunk = lambda r, c: pl.BlockSpec((batch, HEADS, seg_chunks, r, c), lambda s: (0, 0, segs - 1 - s, 0, 0))
    as_seqs = lambda a: a.reshape(batch, rs, a.shape[-1])
    grad = jax.ShapeDtypeStruct((batch, rs, GDN_WIDTH), F32)
    wide = rows_spec(GDN_WIDTH)
    dq, dk, dv, dbg, *got = pl.pallas_call(
        body, name="gdn_bwd",
        out_shape=(grad, grad, grad, jax.ShapeDtypeStruct((batch, rs, LANES), F32))
        + tuple(jax.ShapeDtypeStruct((3,) + p.shape[1:], p.dtype) for p in parts),
        grid=(segs,),
        in_specs=[wide, wide, wide, wide, rows_spec(LANES), per_chunk(HEAD_DIM, HEAD_DIM), per_chunk(CHUNK, CHUNK)]
        + [_hbm()] * count,
        out_specs=(wide, wide, wide, rows_spec(LANES)) + (_hbm(),) * count,
        scratch_shapes=[pltpu.VMEM((batch, HEADS, HEAD_DIM, HEAD_DIM), F32), pltpu.SemaphoreType.DMA((3 * count,)),
                        pltpu.SemaphoreType.DMA((3 * count,))],
        compiler_params=_params("arbitrary"),
    )(as_seqs(do), as_seqs(q), as_seqs(k), as_seqs(v), as_seqs(bg), states, t_invs, *parts)
    return dq.reshape(n, GDN_WIDTH), dk.reshape(n, GDN_WIDTH), dv.reshape(n, GDN_WIDTH), dbg.reshape(n, LANES), got


def _lane_vec(vals, offset):
    k = vals.shape[1]
    return jnp.pad(vals, ((0, 0), (offset, LANES - offset - k)))


LATER = ("w_out", "w_gate", "w_up", "w_down")


def _halves(a):
    return a.reshape(a.shape[:-2] + (2, a.shape[-2] // 2, a.shape[-1]))


def _local_step(x, target, meta, norms, w_in_shard, conv_qkv, a_log, dt_bias, gdn_norm, conv_sc, later_shards, core_arg):
    batch, seq, d = x.shape
    tokens = N_META + seq
    pad_rows = (-tokens) % CHUNK
    rs = tokens + pad_rows
    x_offset = pad_rows + N_META
    n = batch * rs
    w_mix_pre, w_mix_post, w_ffn_pre, w_ffn_post = norms

    head = jnp.concatenate([jnp.zeros((pad_rows, d), F32), meta], axis=0)
    a_log_l = _lane_vec(a_log, HEADS)
    dt_bias_l = _lane_vec(dt_bias, HEADS)

    h0, u1, w_in_all = _embed(x, head, w_mix_pre, w_in_shard, rs)
    w_in_t = _in_to_kernel_order(w_in_all.reshape(N_CHIPS, -1, d))
    proj = _mm(u1, w_in_t, "nt", F32, "mm_proj")
    q = _qkv_fwd(proj, conv_qkv, "q", rs)
    k = _qkv_fwd(proj, conv_qkv, "k", rs)
    v = _qkv_fwd(proj, conv_qkv, "v", rs)
    bg = _gates_fwd(proj, a_log_l, dt_bias_l, rs, pad_rows)
    o, states, t_invs, gathered = _gdn_fwd(q, k, v, bg, rs, later_shards[:3])
    w_out, w_gate_t, w_up_t = (a.reshape(-1, d) for a in gathered)
    cat = _sc_fwd(proj, conv_sc, rs, _gate_fwd(o, proj, gdn_norm, rs))
    mix = _mm(cat, w_out, "nn", F32, "mm_mix")
    h1, u2 = _mix_residual(h0, mix, w_mix_post, w_ffn_pre)
    gate, up, act, w_down = _swiglu_fwd(u2, w_gate_t, w_up_t, later_shards[3])
    w_down = w_down.reshape(-1, d)
    ffn = _mm(act, w_down, "nn", F32, "mm_down")

    dh2, dffn, d_ffn_post, sq = _loss_head(h1, ffn, w_ffn_post, target, rs, x_offset)
    d_w_down = _mm(act, dffn, "tn", F32, "mm_dw_down")
    dgate, dup = _swiglu_bwd(dffn, w_down, gate, up)
    d_w_gate_t = _mm(dgate, u2, "tn", F32, "mm_dw_gate")
    d_w_up_t = _mm(dup, u2, "tn", F32, "mm_dw_up")
    du2 = _mm(dup, w_up_t, "nn", F32, "mm_du2_up", init=_mm(dgate, w_gate_t, "nn", F32, "mm_du2_gate"))
    by_chip = [_halves(g.reshape(N_CHIPS, -1, d)) for g in (d_w_gate_t, d_w_up_t, d_w_down)]
    dh1, dmix, d_ffn_pre, d_mix_post, got_sibling = _mid_bwd(h1, mix, w_mix_post, w_ffn_pre, dh2, du2, by_chip)
    dcat = _mm(dmix, w_out, "nt", F32, "mm_dcat")
    d_w_out = _halves(_mm(cat, dmix, "tn", F32, "mm_dw_out").reshape(N_CHIPS, -1, d))
    sums = (_add_sibling([d_w_out], _exchange_siblings([d_w_out]), core_arg, "w_out")
            + _add_sibling(by_chip, got_sibling, core_arg, "ffn"))
    do, dproj, d_gdn_norm = _gate_bwd(dcat, o, proj, gdn_norm, rs)
    dproj, dscb, dscc, d_conv_sc = _sc_bwd(dcat, proj, conv_sc, rs, dproj)
    dq, dk, dv, dbg, got_chips = _gdn_bwd(do, q, k, v, bg, states, t_invs, rs, [send for _, send in sums[:3]])
    dproj, dwq = _qkv_bwd(dq, proj, conv_qkv, "q", rs, dproj)
    dproj, dwk = _qkv_bwd(dk, proj, conv_qkv, "k", rs, dproj)
    dproj, dwv = _qkv_bwd(dv, proj, conv_qkv, "v", rs, dproj)
    d_conv_qkv = jnp.concatenate([dwq, dwk, dwv], axis=1)
    dproj, d_a_log_l, d_dt_bias_l = _gates_bwd(proj, dbg, a_log_l, dt_bias_l, rs, pad_rows, dproj)
    dproj = lax.dynamic_update_slice(dproj, dscb, (0, (SC_COL + HEADS) * LANES))
    dproj = lax.dynamic_update_slice(dproj, dscc, (0, (SC_COL + 2 * HEADS) * LANES))
    d_w_in_t, got_down = _mm(dproj, u1, "tn", F32, "mm_dw_in", exchange=[sums[3][1]])
    got_chips.append(got_down)
    g_in = _halves(_in_from_kernel_order(d_w_in_t))
    sums = _add_sibling([g_in], _exchange_siblings([g_in]), core_arg, "w_in") + sums
    du1, got_in = _mm(dproj, w_in_t, "nn", F32, "mm_du1", exchange=[sums[0][1]])
    got_chips.insert(0, got_in)
    grad_x, d_meta, d_mix_pre = _in_bwd(h0, w_mix_pre, dh1, du1, rs, pad_rows, x_offset)

    grads = dict(
        meta_tokens=d_meta,
        mix_pre_norm=d_mix_pre, mix_post_norm=d_mix_post, ffn_pre_norm=d_ffn_pre, ffn_post_norm=d_ffn_post,
        conv_qkv=d_conv_qkv,
        a_log=d_a_log_l[:, HEADS:2 * HEADS], dt_bias=d_dt_bias_l[:, HEADS:2 * HEADS],
        gdn_norm=d_gdn_norm, conv_sc=d_conv_sc,
    )
    return sq, grad_x, grads, [(part, got) for (part, _), got in zip(sums, got_chips)]


MATRICES = ("w_in", "w_out", "w_gate", "w_up", "w_down")
IN_SHARD = IN_WIDTH // N_CHIPS
IN_SHARD_PAD = 928


IN_SEGMENTS = ((0, 0, 4 * GDN_WIDTH), (4 * GDN_WIDTH, IN_WIDTH - 2 * HEADS, 2 * HEADS),
               (4 * GDN_WIDTH + 2 * HEADS, 4 * GDN_WIDTH, 3 * SC_WIDTH))
SUBLANES = 8
PACKED_ROWS = 16


def _in_to_kernel_order(by_chip):
    d = by_chip.shape[-1]
    tl = _pick(d, (256, 128))
    runs = []
    for ref0, ker0, count in IN_SEGMENTS:
        row = ref0
        while row < ref0 + count:
            chip, at = divmod(row, IN_SHARD)
            take = min(ref0 + count - row, IN_SHARD - at)
            runs.append((ker0 + row - ref0, take, chip * IN_SHARD_PAD + at))
            row += take

    def body(w_ref, o_ref):
        o_ref[...] = jnp.zeros_like(o_ref)
        for out0, rows, src0 in runs:
            a0 = out0 // PACKED_ROWS * PACKED_ROWS
            a1 = -(-(out0 + rows) // PACKED_ROWS) * PACKED_ROWS
            window = w_ref[pl.ds(src0 - (out0 - a0), a1 - a0), :]
            row = a0 + lax.broadcasted_iota(jnp.int32, (a1 - a0, 1), 0)
            keep = jnp.logical_and(row >= out0, row < out0 + rows)
            o_ref[a0:a1, :] = jnp.where(keep, window, o_ref[a0:a1, :])

    return pl.pallas_call(
        body, name="in_to_kernel_order", out_shape=jax.ShapeDtypeStruct((IN_PAD, d), by_chip.dtype), grid=(d // tl,),
        in_specs=[pl.BlockSpec((N_CHIPS * IN_SHARD_PAD, tl), lambda j: (0, j))],
        out_specs=pl.BlockSpec((IN_PAD, tl), lambda j: (0, j)),
        compiler_params=_params("parallel"),
    )(by_chip.reshape(N_CHIPS * IN_SHARD_PAD, d))


def _in_from_kernel_order(g_t):
    d = g_t.shape[-1]
    tl = _pick(d, (256, 128))

    def body(g_ref, o_ref):
        row = lax.broadcasted_iota(jnp.int32, (IN_SHARD_PAD, 1), 0)
        for chip in range(N_CHIPS):
            first = chip * IN_SHARD
            runs = []
            for ref0, ker0, count in IN_SEGMENTS:
                lo, hi = max(ref0, first), min(ref0 + count, first + IN_SHARD)
                if lo < hi:
                    runs.append((lo - first, hi - lo, ker0 + lo - ref0))
            val = jnp.zeros((IN_SHARD_PAD, tl), F32)
            patches = []
            for out0, rows, src0 in runs:
                start = src0 - out0
                if 0 <= start <= IN_PAD - IN_SHARD_PAD:
                    window = g_ref[pl.ds(start, IN_SHARD_PAD), :]
                    val = jnp.where(jnp.logical_and(row >= out0, row < out0 + rows), window, val)
                else:
                    patches.append((out0, rows, src0))
            o_ref[chip] = val
            for out0, rows, src0 in patches:
                a0 = out0 // SUBLANES * SUBLANES
                a1 = -(-(out0 + rows) // SUBLANES) * SUBLANES
                window = g_ref[pl.ds(src0 - (out0 - a0), a1 - a0), :]
                keep = jnp.logical_and(row[a0:a1] >= out0, row[a0:a1] < out0 + rows)
                o_ref[chip, a0:a1, :] = jnp.where(keep, window, o_ref[chip, a0:a1, :])

    return pl.pallas_call(
        body, name="in_from_kernel_order", out_shape=jax.ShapeDtypeStruct((N_CHIPS, IN_SHARD_PAD, d), F32), grid=(d // tl,),
        in_specs=[pl.BlockSpec((IN_PAD, tl), lambda j: (0, j))],
        out_specs=pl.BlockSpec((N_CHIPS, IN_SHARD_PAD, tl), lambda j: (0, 0, j)),
        compiler_params=_params("parallel"),
    )(g_t)


PACK_LANES = 3 * GDN_WIDTH
PACKED = dict(mix_pre_norm=(0, 1, 0, D_MODEL), mix_post_norm=(1, 1, 0, D_MODEL), ffn_pre_norm=(2, 1, 0, D_MODEL),
              ffn_post_norm=(3, 1, 0, D_MODEL), a_log=(4, 1, 0, HEADS), dt_bias=(5, 1, 0, HEADS), loss=(6, 1, 0, 1),
              gdn_norm=(7, 1, 0, HEAD_DIM), conv_qkv=(8, GDN_CONV, 0, 3 * GDN_WIDTH), conv_sc=(0, SC_CONV, D_MODEL, SC_WIDTH),
              meta_tokens=(16, N_META, 0, D_MODEL))
PACK_ROWS = 32
SHARDED_SMALL = ("conv_qkv", "conv_sc", "meta_tokens")


def _pack_small(values):
    names = list(PACKED)

    def body(*refs):
        out_ref = refs[-1]
        out_ref[...] = jnp.zeros_like(out_ref)
        for name, ref in zip(names, refs):
            row, rows, lane0, lanes = PACKED[name]
            out_ref[row:row + rows, lane0:lane0 + lanes] = ref[...]

    return pl.pallas_call(body, name="pack_small", out_shape=jax.ShapeDtypeStruct((PACK_ROWS, PACK_LANES), F32))(
        *[values[name] for name in names])


def _sum_devices(packed_all, chip):
    names = list(PACKED)

    def body(chip_ref, all_ref, *rest):
        shard_refs, out_refs = rest[:len(SHARDED_SMALL)], rest[len(SHARDED_SMALL):]

        def total(ref, rows, lanes):
            acc = ref[0, rows, lanes]
            for k in range(1, 8):
                acc = acc + ref[k, rows, lanes]
            return acc

        for name, out in zip(names, out_refs):
            row, rows, lane0, lanes = PACKED[name]
            if name in SHARDED_SMALL:
                out[...] = total(shard_refs[SHARDED_SMALL.index(name)], slice(0, rows), slice(None))
            else:
                out[...] = total(all_ref, slice(row, row + rows), slice(lane0, lane0 + lanes))

    def shard_spec(name):
        row, rows, lane0, lanes = PACKED[name]
        height, width = max(rows, 8), lanes // N_CHIPS
        assert row % height == 0 and lane0 % width == 0
        return pl.BlockSpec((8, height, width), lambda i, chip_ref: (0, row // height, lane0 // width + chip_ref[0]))

    def out_shape(name):
        _, rows, _, lanes = PACKED[name]
        return jax.ShapeDtypeStruct((rows, lanes // N_CHIPS if name in SHARDED_SMALL else lanes), F32)

    whole = lambda shape: pl.BlockSpec(shape, lambda i, chip_ref: (0,) * len(shape))
    outs = pl.pallas_call(
        body, name="sum_devices", out_shape=tuple(out_shape(n) for n in names),
        grid_spec=pltpu.PrefetchScalarGridSpec(
            num_scalar_prefetch=1, grid=(1,),
            in_specs=[whole(packed_all.shape)] + [shard_spec(n) for n in SHARDED_SMALL],
            out_specs=tuple(whole(out_shape(n).shape) for n in names)),
    )(chip, packed_all, *[packed_all] * len(SHARDED_SMALL))
    return dict(zip(names, outs))


def _hbm():
    return pl.BlockSpec(memory_space=pl.ANY)


def _place():
    x, y, c = lax.axis_index("x"), lax.axis_index("y"), lax.axis_index("c")
    chips = ((1 - x, y), (x, 1 - y), (1 - x, 1 - y))
    return x, y, c, chips


def _remote(src, dst, send_sems, recv_sems, k, to):
    return pltpu.make_async_remote_copy(src_ref=src, dst_ref=dst, send_sem=send_sems.at[k], recv_sem=recv_sems.at[k],
                                        device_id=to, device_id_type=MESH)


GATHER_SEMS = 7


def _gather_copies(w_refs, out_refs, send_sems, recv_sems):
    x, y, c, chips = _place()
    mine = 2 * x + y
    sibling = (x, y, 1 - c)
    copy = functools.partial(_remote, send_sems=send_sems, recv_sems=recv_sems)
    direct, landed, passing, from_sibling = [], [], [], []
    for i, (w, o) in enumerate(zip(w_refs, out_refs)):
        k = GATHER_SEMS * i
        direct.append(copy(w, o.at[mine], k=k, to=sibling))
        from_sibling.append(copy(w, o.at[mine], k=k, to=sibling))
        for j, (cx, cy) in enumerate(chips):
            theirs = 2 * cx + cy
            direct.append(copy(w.at[c], o.at[mine, c], k=k + 1 + j, to=(cx, cy, c)))
            landed.append(copy(w.at[c], o.at[theirs, c], k=k + 1 + j, to=sibling))
            passing.append(copy(o.at[theirs, c], o.at[theirs, c], k=k + 4 + j, to=sibling))
            from_sibling.append(copy(w.at[c], o.at[theirs, 1 - c], k=k + 4 + j, to=sibling))
    return direct, landed, passing, from_sibling


def _gather_finish(copies):
    direct, landed, passing, from_sibling = copies
    for arrival, forward in zip(landed, passing):
        arrival.wait_recv()
        forward.start()
    for arrival in from_sibling:
        arrival.wait_recv()
    for cp in direct + passing:
        cp.wait_send()


def _gather_weights(pieces, smalls):
    count, extra = len(pieces), len(smalls)
    total = count + extra

    def body(*refs):
        w_refs, s_refs = refs[:count], refs[count:total]
        out_refs, sall_refs = refs[total:total + count], refs[total + count:2 * total]
        send_sems, recv_sems, local_sems = refs[2 * total:]
        x, y, c, chips = _place()
        mine = 2 * x + y
        own = [pltpu.make_async_copy(s, sall.at[mine], local_sems.at[i]) for i, (s, sall) in enumerate(zip(s_refs, sall_refs))]
        small = [_remote(s, sall.at[mine], send_sems, recv_sems, GATHER_SEMS * count + 3 * i + j, (cx, cy, c))
                 for i, (s, sall) in enumerate(zip(s_refs, sall_refs)) for j, (cx, cy) in enumerate(chips)]
        copies = _gather_copies(w_refs, out_refs, send_sems, recv_sems)
        for cp in own + small + copies[0]:
            cp.start()
        _gather_finish(copies)
        for cp in small:
            cp.wait_recv()
        for cp in small:
            cp.wait_send()
        for cp in own:
            cp.wait()

    sems = GATHER_SEMS * count + 3 * extra
    return pl.pallas_call(
        body, name="gather_weights",
        out_shape=tuple(jax.ShapeDtypeStruct((N_CHIPS,) + p.shape, p.dtype) for p in list(pieces) + list(smalls)),
        in_specs=[_hbm()] * total, out_specs=(_hbm(),) * total,
        scratch_shapes=[pltpu.SemaphoreType.DMA((sems,)), pltpu.SemaphoreType.DMA((sems,)), pltpu.SemaphoreType.DMA((extra,))],
    )(*pieces, *smalls)


def _sibling_copies(g_refs, got_refs, send_sems, recv_sems):
    x, y, c, _ = _place()
    return [_remote(g.at[:, 1 - c], got, send_sems, recv_sems, i, (x, y, 1 - c)) for i, (g, got) in enumerate(zip(g_refs, got_refs))]


def _exchange_siblings(grads):
    count = len(grads)

    def body(*refs):
        copies = _sibling_copies(refs[:count], refs[count:2 * count], *refs[2 * count:])
        for cp in copies:
            cp.start()
        for cp in copies:
            cp.wait_recv()
        for cp in copies:
            cp.wait_send()

    return pl.pallas_call(
        body, name="exchange_siblings",
        out_shape=tuple(jax.ShapeDtypeStruct((g.shape[0],) + g.shape[2:], F32) for g in grads),
        in_specs=[_hbm()] * count, out_specs=(_hbm(),) * count,
        scratch_shapes=[pltpu.SemaphoreType.DMA((count,)), pltpu.SemaphoreType.DMA((count,))],
    )(*grads)


def _chip_copies(p_refs, got_refs, send_sems, recv_sems):
    x, y, c, chips = _place()
    return [_remote(p.at[2 * cx + cy], got.at[j], send_sems, recv_sems, 3 * i + j, (cx, cy, c))
            for i, (p, got) in enumerate(zip(p_refs, got_refs)) for j, (cx, cy) in enumerate(chips)]


def _share_halves(halves, small):
    count = len(halves)

    def body(*refs):
        h_refs, s_ref = refs[:count], refs[count]
        full_refs, sall_ref = refs[count + 1:2 * count + 1], refs[2 * count + 1]
        send_sems, recv_sems, local_sem = refs[2 * count + 2:]
        x, y, c, _ = _place()
        me = 4 * x + 2 * y + c
        own = pltpu.make_async_copy(s_ref, sall_ref.at[me], local_sem)
        own.start()
        copies = [_remote(h.at[c], full.at[c], send_sems, recv_sems, i, (x, y, 1 - c))
                  for i, (h, full) in enumerate(zip(h_refs, full_refs))]
        for k in range(7):
            dx, dy, dc = ((k + 1) >> 2) & 1, ((k + 1) >> 1) & 1, (k + 1) & 1
            peer = (1 - x if dx else x, 1 - y if dy else y, 1 - c if dc else c)
            copies.append(_remote(s_ref, sall_ref.at[me], send_sems, recv_sems, count + k, peer))
        for cp in copies:
            cp.start()
        for cp in copies:
            cp.wait_recv()
        for cp in copies:
            cp.wait_send()
        own.wait()

    return pl.pallas_call(
        body, name="share_halves",
        out_shape=tuple(jax.ShapeDtypeStruct(h.shape, h.dtype) for h in halves) + (jax.ShapeDtypeStruct((8,) + small.shape, F32),),
        in_specs=[_hbm()] * (count + 1), out_specs=(_hbm(),) * (count + 1), input_output_aliases={i: i for i in range(count)},
        scratch_shapes=[pltpu.SemaphoreType.DMA((count + 7,)), pltpu.SemaphoreType.DMA((count + 7,)), pltpu.SemaphoreType.DMA],
    )(*halves, small)


def _add_sibling(grads, gots, core, name):
    count = len(grads)
    chips, _, rows, cols = grads[0].shape

    def body(core_ref, *refs):
        for i in range(count):
            s = refs[i][...] + refs[count + i][...]
            refs[2 * count + 2 * i][...] = s
            refs[2 * count + 2 * i + 1][...] = s.astype(BF16)

    block = pl.BlockSpec((None, rows, cols), lambda p, core_ref: (p, 0, 0))
    own = pl.BlockSpec((None, None, rows, cols), lambda p, core_ref: (p, core_ref[0], 0, 0))
    out = pl.pallas_call(
        body, name="add_sibling_" + name,
        out_shape=(jax.ShapeDtypeStruct((chips, rows, cols), F32), jax.ShapeDtypeStruct((chips, rows, cols), BF16)) * count,
        grid_spec=pltpu.PrefetchScalarGridSpec(
            num_scalar_prefetch=1, grid=(chips,), in_specs=[own] * count + [block] * count, out_specs=(block, block) * count),
        compiler_params=_params("parallel"),
    )(core, *grads, *gots)
    return [(out[2 * i], out[2 * i + 1]) for i in range(count)]


def _add_chips(parts, gots, chip_core, name):
    count = len(parts)
    _, rows, cols = parts[0].shape
    tr = rows // 2 if rows % 32 == 0 else rows

    def body(place_ref, *refs):
        for i in range(count):
            r_ref = refs[count + i]
            refs[2 * count + i][...] = ((refs[i][...] + r_ref[0].astype(F32)) + r_ref[1].astype(F32)) + r_ref[2].astype(F32)

    return pl.pallas_call(
        body, name="add_chips_" + name, out_shape=(jax.ShapeDtypeStruct((2, rows, cols), F32),) * count,
        grid_spec=pltpu.PrefetchScalarGridSpec(
            num_scalar_prefetch=1, grid=(rows // tr,),
            in_specs=[pl.BlockSpec((None, tr, cols), lambda i, place_ref: (place_ref[0], i, 0))] * count
            + [pl.BlockSpec((3, tr, cols), lambda i, place_ref: (0, i, 0))] * count,
            out_specs=(pl.BlockSpec((None, tr, cols), lambda i, place_ref: (place_ref[1], i, 0)),) * count),
        compiler_params=_params("parallel"),
    )(chip_core, *parts, *gots)


def _adamw(w, g, m, v, name):
    rows, cols = w.shape
    tr = _pick(rows, (3592, 256, 352, 176, 128, 64, 32, 16, 8))

    def body(w_ref, g_ref, m_ref, v_ref, d_ref, nm_ref, nv_ref):
        d_ref[...], nm_ref[...], nv_ref[...] = _adamw_math(w_ref[...], g_ref[...], m_ref[...], v_ref[...])

    block = pl.BlockSpec((tr, cols), lambda i: (i, 0))
    shape = jax.ShapeDtypeStruct((rows, cols), F32)
    return pl.pallas_call(
        body, name="adamw_" + name, out_shape=(shape, shape, shape), grid=(rows // tr,),
        in_specs=[block] * 4, out_specs=(block,) * 3, compiler_params=_params("parallel"),
    )(w, g, m, v)


def _adamw_math(w, g, m, v):
    m = ADAM_B1 * m + (1.0 - ADAM_B1) * g
    v = ADAM_B2 * v + (1.0 - ADAM_B2) * (g * g)
    m_hat = m / (1.0 - ADAM_B1 ** ADAM_STEP)
    v_hat = v / (1.0 - ADAM_B2 ** ADAM_STEP)
    return -ADAM_LR * (m_hat / (jnp.sqrt(v_hat) + ADAM_EPS) + ADAM_WD * w), m, v


def _adamw_small(ws, gs, ms, vs):
    count = len(ws)

    def body(*refs):
        ins, outs = refs[:4 * count], refs[4 * count:]
        for i in range(count):
            outs[i][...], outs[count + i][...], outs[2 * count + i][...] = _adamw_math(
                ins[i][...], ins[count + i][...], ins[2 * count + i][...], ins[3 * count + i][...])

    shapes = tuple(jax.ShapeDtypeStruct(w.shape, F32) for w in ws)
    out = pl.pallas_call(body, name="adamw_small", out_shape=shapes * 3)(*ws, *gs, *ms, *vs)
    return out[:count], out[count:2 * count], out[2 * count:]


WEIGHTS = ("meta_tokens", "mix_pre_norm", "mix_post_norm", "ffn_pre_norm", "ffn_post_norm", "w_in", "conv_qkv", "a_log",
           "dt_bias", "gdn_norm", "conv_sc", "w_out", "w_gate", "w_up", "w_down")


def kernel(x, meta_tokens, mix_pre_norm, mix_post_norm, ffn_pre_norm, ffn_post_norm, w_in, conv_qkv, a_log, dt_bias, gdn_norm, conv_sc, w_out, w_gate, w_up, w_down, loss_target, m_meta_tokens, m_mix_pre_norm, m_mix_post_norm, m_ffn_pre_norm, m_ffn_post_norm, m_w_in, m_conv_qkv, m_a_log, m_dt_bias, m_gdn_norm, m_conv_sc, m_w_out, m_w_gate, m_w_up, m_w_down, v_meta_tokens, v_mix_pre_norm, v_mix_post_norm, v_ffn_pre_norm, v_ffn_post_norm, v_w_in, v_conv_qkv, v_a_log, v_dt_bias, v_gdn_norm, v_conv_sc, v_w_out, v_w_gate, v_w_up, v_w_down):
    d = x.shape[-1]
    two_d = lambda a: a.reshape(a.shape[-2:])
    weights = dict(zip(WEIGHTS, (meta_tokens, mix_pre_norm, mix_post_norm, ffn_pre_norm, ffn_post_norm, w_in, conv_qkv, a_log,
                                 dt_bias, gdn_norm, conv_sc, w_out, w_gate, w_up, w_down)))
    m_in = dict(zip(WEIGHTS, (m_meta_tokens, m_mix_pre_norm, m_mix_post_norm, m_ffn_pre_norm, m_ffn_post_norm, m_w_in, m_conv_qkv,
                              m_a_log, m_dt_bias, m_gdn_norm, m_conv_sc, m_w_out, m_w_gate, m_w_up, m_w_down)))
    v_in = dict(zip(WEIGHTS, (v_meta_tokens, v_mix_pre_norm, v_mix_post_norm, v_ffn_pre_norm, v_ffn_post_norm, v_w_in, v_conv_qkv,
                              v_a_log, v_dt_bias, v_gdn_norm, v_conv_sc, v_w_out, v_w_gate, v_w_up, v_w_down)))
    core = lax.axis_index("c")
    chip = 2 * lax.axis_index("x") + lax.axis_index("y")
    core_arg = core.reshape(1).astype(jnp.int32)
    chip_core = jnp.stack([chip, core]).astype(jnp.int32)
    whole = lambda a: a.reshape(a.shape[:-3] + (2 * a.shape[-2], d))
    by_rows = lambda n, a: two_d(a).T if n in ("w_in", "w_gate", "w_up") else two_d(a)

    shard = {n: by_rows(n, weights[n]).astype(MXU_DTYPE) for n in MATRICES}
    shard["w_in"] = jnp.pad(shard["w_in"], ((0, IN_SHARD_PAD - IN_SHARD), (0, 0)))
    small_all = _gather_weights([], [two_d(weights[n]) for n in SHARDED_SMALL])
    conv_qkv_full, conv_sc_full, meta_full = (jnp.concatenate([a[p] for p in range(N_CHIPS)], axis=1) for a in small_all)

    sq, grad_x, g, sums = _local_step(
        x, loss_target, meta_full, (mix_pre_norm, mix_post_norm, ffn_pre_norm, ffn_post_norm), _halves(shard["w_in"]),
        conv_qkv_full, a_log, dt_bias, gdn_norm, conv_sc_full, [_halves(shard[n]) for n in LATER], core_arg)

    parts, gots = zip(*sums)
    totals = [_add_chips(parts[i:i + 1], gots[i:i + 1], chip_core, MATRICES[i])[0] for i in range(2)]
    totals += _add_chips(parts[2:], gots[2:], chip_core, "ffn")
    *shared, packed_all = _share_halves(totals, _pack_small(dict(g, loss=sq)))
    grads = {n: whole(a) for n, a in zip(MATRICES, shared)}
    grads["w_in"] = grads["w_in"][:IN_SHARD]
    grads.update(_sum_devices(packed_all, chip.reshape(1).astype(jnp.int32)))
    loss = (0.5 / d) * grads.pop("loss")[0, 0]

    small = [n for n in WEIGHTS if n not in MATRICES]
    updates = dict(zip(small, zip(*_adamw_small(*([by_rows(n, params[n]) for n in small] for params in (weights, grads, m_in, v_in))))))
    outs = [[], [], [], []]
    for n in WEIGHTS:
        shape = weights[n].shape
        if n in MATRICES:
            updates[n] = _adamw(by_rows(n, weights[n]), grads[n], by_rows(n, m_in[n]), by_rows(n, v_in[n]), n)
        for out, a in zip(outs, (grads[n], *updates[n])):
            out.append((a.T if n in ("w_in", "w_gate", "w_up") else a).reshape(shape))
    return (loss, grad_x, *outs[0], *outs[1], *outs[2], *outs[3])
```

```python
import functools

import jax
import jax.numpy as jnp
from jax import lax
from jax.experimental import pallas as pl
from jax.experimental.pallas import tpu as pltpu

F32 = jnp.float32
BF16 = jnp.bfloat16
MXU_DTYPE = jnp.bfloat16
MESH = pl.DeviceIdType.MESH

D_MODEL = 1024
N_META = 16
HEADS = 4
HEAD_DIM = 128
GDN_WIDTH = HEADS * HEAD_DIM
GDN_CONV = 4
CHUNK = 64
SC_WIDTH = D_MODEL - GDN_WIDTH
SC_CONV = 3
D_FF = 2816
IN_WIDTH = 4 * GDN_WIDTH + 2 * HEADS + 3 * SC_WIDTH
IN_PAD = 3840
BA_COL = (4 * GDN_WIDTH + 3 * SC_WIDTH) // 128
EPS = 1e-6
LANES = 128
N_CHIPS = 4
VMEM_LIMIT = 48 * 2 ** 20
MM_VMEM_BUDGET = 42 * 2 ** 20

ADAM_LR = 0.001
ADAM_B1 = 0.9
ADAM_B2 = 0.999
ADAM_EPS = 1e-08
ADAM_WD = 0.01
ADAM_STEP = 10


def _pick(n, candidates):
    for c in candidates:
        if n % c == 0:
            return c
    return n


def _row_tile(n):
    return _pick(n, (352, 256, 176, 128, 64, 32, 16, 8))


def _params(*sem):
    return pltpu.CompilerParams(dimension_semantics=sem, vmem_limit_bytes=VMEM_LIMIT)


def _sigmoid(x):
    return 0.5 * jnp.tanh(0.5 * x) + 0.5


def _softplus(x):
    return jnp.maximum(x, 0.0) + jnp.log(1.0 + jnp.exp(-jnp.abs(x)))


def _dsilu(x, s):
    return s * (1.0 + x * (1.0 - s))


def _mm(a, b, mode, out_dtype, name, init=None, exchange=None):
    if mode == "tn":
        k_dim, m_dim = a.shape
    else:
        m_dim, k_dim = a.shape
    n_dim = b.shape[0] if mode == "nt" else b.shape[1]
    tn = _pick(n_dim, (1408, 1280, 1024, 768, 512, 256, 128))
    if mode == "tn":
        tm = _pick(m_dim, (1408, 1280, 1024, 512, 256, 128))
        tk = _pick(k_dim, (2112, 1408, 1280, 1056, 1024, 512, 256, 128))
    else:
        tk = k_dim
        blocks = lambda rows: 2 * (2 * rows * tk + 2 * tk * tn + 4 * rows * tn * (1 if init is None else 2))
        tm = next((t for t in (2112, 1056, 1024, 704, 512, 256, 128) if m_dim % t == 0 and blocks(t) <= MM_VMEM_BUDGET), m_dim)
    nk = k_dim // tk
    if mode == "nn":
        a_spec = pl.BlockSpec((tm, tk), lambda i, j, k: (i, k))
        b_spec = pl.BlockSpec((tk, tn), lambda i, j, k: (k, j))
        dims = (((1,), (0,)), ((), ()))
    elif mode == "nt":
        a_spec = pl.BlockSpec((tm, tk), lambda i, j, k: (i, k))
        b_spec = pl.BlockSpec((tn, tk), lambda i, j, k: (j, k))
        dims = (((1,), (1,)), ((), ()))
    else:
        a_spec = pl.BlockSpec((tk, tm), lambda i, j, k: (k, i))
        b_spec = pl.BlockSpec((tk, tn), lambda i, j, k: (k, j))
        dims = (((0,), (0,)), ((), ()))

    out_spec = pl.BlockSpec((tm, tn), lambda i, j, k: (i, j))
    grid = (m_dim // tm, n_dim // tn, nk)
    parts = () if exchange is None else tuple(exchange)
    count = len(parts)
    first_in = 2 if init is None else 3

    assert out_dtype == F32

    def body(a_ref, b_ref, *rest):
        o_ref = rest[first_in - 2 + count]
        k = pl.program_id(2)
        step = (pl.program_id(0) * grid[1] + pl.program_id(1)) * nk + k
        if count:
            copies = _chip_copies(rest[first_in - 2:first_in - 2 + count], rest[first_in - 1 + count:first_in - 1 + 2 * count],
                                  *rest[first_in - 1 + 2 * count:])

            @pl.when(step == 0)
            def _():
                for cp in copies:
                    cp.start()

        p = lax.dot_general(a_ref[...], b_ref[...], dims, preferred_element_type=F32)
        if nk == 1:
            o_ref[...] = p if init is None else rest[0][...] + p
        else:
            @pl.when(k == 0)
            def _():
                o_ref[...] = p if init is None else rest[0][...] + p

            @pl.when(k > 0)
            def _():
                o_ref[...] += p

        if count:
            @pl.when(step == grid[0] * grid[1] * nk - 1)
            def _():
                for cp in copies:
                    cp.wait_recv()
                for cp in copies:
                    cp.wait_send()

    out = pl.pallas_call(
        body, name=name,
        out_shape=(jax.ShapeDtypeStruct((m_dim, n_dim), out_dtype),)
        + tuple(jax.ShapeDtypeStruct((3,) + p.shape[1:], p.dtype) for p in parts),
        grid=grid,
        in_specs=[a_spec, b_spec] + ([] if init is None else [out_spec]) + [_hbm()] * count,
        out_specs=(out_spec,) + (_hbm(),) * count,
        scratch_shapes=[pltpu.SemaphoreType.DMA((3 * count,)), pltpu.SemaphoreType.DMA((3 * count,))] if count else [],
        compiler_params=_params(*(("arbitrary",) * 3 if count else ("parallel", "parallel", "arbitrary"))),
    )(a, b, *(() if init is None else (init,)), *parts)
    return out[0] if not count else out


def _rms_apply(x, w):
    r = lax.rsqrt(jnp.mean(x * x, axis=-1, keepdims=True) + EPS)
    return x * r * w


def _rms_bwd(x, w, dy):
    r = lax.rsqrt(jnp.mean(x * x, axis=-1, keepdims=True) + EPS)
    xh = x * r
    dyw = dy * w
    dx = r * (dyw - xh * jnp.mean(dyw * xh, axis=-1, keepdims=True))
    return dx, jnp.sum(dy * xh, axis=0, keepdims=True)


def _accumulate(ref, first, value):
    @pl.when(first)
    def _():
        ref[...] = value

    @pl.when(jnp.logical_not(first))
    def _():
        ref[...] += value


def _rows(tr, width):
    return pl.BlockSpec((tr, width), lambda i: (i, 0))


def _vec(width):
    return pl.BlockSpec((1, width), lambda i: (0, 0))


def _embed(x, head, w_pre, w_shard, rows_per_seq):
    batch, seq, d = x.shape
    x_offset = head.shape[0]
    tr = _row_tile(rows_per_seq)
    tiles_per_seq = rows_per_seq // tr
    n = batch * rows_per_seq

    def body(x_ref, head_ref, w_ref, ws_ref, h0_ref, u_ref, wall_ref, send_sems, recv_sems):
        gather = _gather_copies([ws_ref], [wall_ref], send_sems, recv_sems)
        i = pl.program_id(0)
        tile = lax.rem(i, tiles_per_seq)

        @pl.when(i == 0)
        def _():
            for cp in gather[0]:
                cp.start()

        rows = jnp.concatenate([head_ref[...], x_ref[0:tr - x_offset, :]], axis=0)
        if tiles_per_seq > 1:
            start = pl.multiple_of(jnp.maximum(tile * tr - x_offset, 0), SUBLANES)
            rows = jnp.where(tile == 0, rows, x_ref[pl.ds(start, tr), :])
        h0_ref[...] = rows
        u_ref[...] = _rms_apply(rows, w_ref[...]).astype(u_ref.dtype)

        @pl.when(i == n // tr - 1)
        def _():
            _gather_finish(gather)

    return pl.pallas_call(
        body, name="embed",
        out_shape=(jax.ShapeDtypeStruct((n, d), F32), jax.ShapeDtypeStruct((n, d), MXU_DTYPE),
                   jax.ShapeDtypeStruct((N_CHIPS,) + w_shard.shape, w_shard.dtype)),
        grid=(n // tr,),
        in_specs=[pl.BlockSpec((None, seq, d), lambda i: (i // tiles_per_seq, 0, 0)),
                  pl.BlockSpec((x_offset, d), lambda i: (0, 0)), _vec(d), _hbm()],
        out_specs=(_rows(tr, d), _rows(tr, d), _hbm()),
        scratch_shapes=[pltpu.SemaphoreType.DMA((GATHER_SEMS,)), pltpu.SemaphoreType.DMA((GATHER_SEMS,))],
        compiler_params=_params("arbitrary"),
    )(x, head, w_pre, w_shard)


def _mix_residual(h0, mix, w_post, w_pre):
    n, d = h0.shape
    tr = _row_tile(n)

    def body(h0_ref, mix_ref, wpost_ref, wpre_ref, h1_ref, u2_ref):
        h1 = h0_ref[...] + _rms_apply(mix_ref[...], wpost_ref[...])
        h1_ref[...] = h1
        u2_ref[...] = _rms_apply(h1, wpre_ref[...]).astype(u2_ref.dtype)

    return pl.pallas_call(
        body, name="mix_residual",
        out_shape=(jax.ShapeDtypeStruct((n, d), F32), jax.ShapeDtypeStruct((n, d), MXU_DTYPE)), grid=(n // tr,),
        in_specs=[_rows(tr, d), _rows(tr, d), _vec(d), _vec(d)], out_specs=(_rows(tr, d), _rows(tr, d)),
        compiler_params=_params("parallel"),
    )(h0, mix, w_post, w_pre)


NT_DIMS = (((1,), (1,)), ((), ()))


def _ffn_tiles(n):
    return _pick(n, (1056, 704, 512, 256, 128)), _pick(D_FF, (1408, 256, 128))


def _swiglu_fwd(u, w_gate_t, w_up_t, w_next):
    n, d = u.shape
    tm, tn = _ffn_tiles(n)
    grid = (D_FF // tn, n // tm)

    def body(u_ref, wg_ref, wu_ref, wn_ref, g_ref, up_ref, act_ref, wall_ref, send_sems, recv_sems):
        gather = _gather_copies([wn_ref], [wall_ref], send_sems, recv_sems)
        step = pl.program_id(0) * grid[1] + pl.program_id(1)

        @pl.when(step == 0)
        def _():
            for cp in gather[0]:
                cp.start()

        a = u_ref[...]
        g = lax.dot_general(a, wg_ref[...], NT_DIMS, preferred_element_type=F32)
        up = lax.dot_general(a, wu_ref[...], NT_DIMS, preferred_element_type=F32)
        g_ref[...] = g.astype(g_ref.dtype)
        up_ref[...] = up.astype(up_ref.dtype)
        act_ref[...] = (g * _sigmoid(g) * up).astype(act_ref.dtype)

        @pl.when(step == grid[0] * grid[1] - 1)
        def _():
            _gather_finish(gather)

    tile = pl.BlockSpec((tm, tn), lambda j, i: (i, j))
    weight = pl.BlockSpec((tn, d), lambda j, i: (j, 0))
    wide = jax.ShapeDtypeStruct((n, D_FF), MXU_DTYPE)
    return pl.pallas_call(
        body, name="swiglu_fwd",
        out_shape=(wide, wide, jax.ShapeDtypeStruct((n, D_FF), MXU_DTYPE),
                   jax.ShapeDtypeStruct((N_CHIPS,) + w_next.shape, w_next.dtype)),
        grid=grid,
        in_specs=[pl.BlockSpec((tm, d), lambda j, i: (i, 0)), weight, weight, _hbm()], out_specs=(tile, tile, tile, _hbm()),
        scratch_shapes=[pltpu.SemaphoreType.DMA((GATHER_SEMS,)), pltpu.SemaphoreType.DMA((GATHER_SEMS,))],
        compiler_params=_params("arbitrary", "arbitrary"),
    )(u, w_gate_t, w_up_t, w_next)


def _swiglu_bwd(dffn, w_down, gate, up):
    n, d = dffn.shape
    tm, tn = _ffn_tiles(n)

    def body(dy_ref, w_ref, g_ref, u_ref, dg_ref, du_ref):
        da = lax.dot_general(dy_ref[...], w_ref[...], NT_DIMS, preferred_element_type=F32)
        g = g_ref[...].astype(F32)
        s = _sigmoid(g)
        dg_ref[...] = (da * u_ref[...].astype(F32) * _dsilu(g, s)).astype(dg_ref.dtype)
        du_ref[...] = (da * g * s).astype(du_ref.dtype)

    tile = pl.BlockSpec((tm, tn), lambda j, i: (i, j))
    shape = jax.ShapeDtypeStruct((n, D_FF), MXU_DTYPE)
    return pl.pallas_call(
        body, name="swiglu_bwd", out_shape=(shape, shape), grid=(D_FF // tn, n // tm),
        in_specs=[pl.BlockSpec((tm, d), lambda j, i: (i, 0)), pl.BlockSpec((tn, d), lambda j, i: (j, 0)), tile, tile],
        out_specs=(tile, tile), compiler_params=_params("parallel", "parallel"),
    )(dffn, w_down, gate, up)


def _loss_head(h1, ffn, w_post, target, rows_per_seq, x_offset):
    n, d = h1.shape
    tr = _row_tile(rows_per_seq)
    tiles_per_seq = rows_per_seq // tr
    seq = target.shape[1]

    def seq_rows(t_ref, tile):
        first = jnp.concatenate([jnp.zeros((x_offset, d), F32), t_ref[0:tr - x_offset, :]], axis=0)
        if tiles_per_seq == 1:
            return first
        start = pl.multiple_of(jnp.maximum(tile * tr - x_offset, 0), SUBLANES)
        return jnp.where(tile == 0, first, t_ref[pl.ds(start, tr), :])

    def body(h1_ref, ffn_ref, w_ref, t_ref, dh2_ref, dffn_ref, dw_ref, sq_ref):
        i = pl.program_id(0)
        tile = lax.rem(i, tiles_per_seq)
        w = w_ref[...]
        f = ffn_ref[...]
        r = lax.rsqrt(jnp.mean(f * f, axis=-1, keepdims=True) + EPS)
        fh = f * r
        row = tile * tr + lax.broadcasted_iota(jnp.int32, (tr, 1), 0)
        err = jnp.where(row >= x_offset, h1_ref[...] + fh * w - seq_rows(t_ref, tile), 0.0)
        dh2 = err * (1.0 / d)
        dh2_ref[...] = dh2
        dyw = dh2 * w
        dffn_ref[...] = (r * (dyw - fh * jnp.mean(dyw * fh, axis=-1, keepdims=True))).astype(dffn_ref.dtype)
        _accumulate(dw_ref, i == 0, jnp.sum(dh2 * fh, axis=0, keepdims=True))
        _accumulate(sq_ref, i == 0, jnp.sum(jnp.sum(err * err, axis=1, keepdims=True), axis=0, keepdims=True))

    return pl.pallas_call(
        body, name="loss_head",
        out_shape=(jax.ShapeDtypeStruct((n, d), F32), jax.ShapeDtypeStruct((n, d), MXU_DTYPE),
                   jax.ShapeDtypeStruct((1, d), F32), jax.ShapeDtypeStruct((1, 1), F32)),
        grid=(n // tr,),
        in_specs=[_rows(tr, d), _rows(tr, d), _vec(d), pl.BlockSpec((None, seq, d), lambda i: (i // tiles_per_seq, 0, 0))],
        out_specs=(_rows(tr, d), _rows(tr, d), _vec(d), _vec(1)),
        compiler_params=_params("arbitrary"),
    )(h1, ffn, w_post, target)


def _mid_bwd(h1, mix, w_mix_post, w_ffn_pre, dh2, du2, grads):
    n, d = h1.shape
    tr = _row_tile(n)
    count = len(grads)

    def body(h1_ref, mix_ref, wpost_ref, wpre_ref, dh2_ref, du2_ref, *rest):
        g_refs, (dh1_ref, dmix_ref, dwpre_ref, dwpost_ref), got_refs = rest[:count], rest[count:count + 4], rest[count + 4:2 * count + 4]
        exchange = _sibling_copies(g_refs, got_refs, *rest[2 * count + 4:])
        i = pl.program_id(0)

        @pl.when(i == 0)
        def _():
            for cp in exchange:
                cp.start()

        dx, dwpre = _rms_bwd(h1_ref[...], wpre_ref[...], du2_ref[...])
        dh1 = dh2_ref[...] + dx
        dh1_ref[...] = dh1
        dmix, dwpost = _rms_bwd(mix_ref[...], wpost_ref[...], dh1)
        dmix_ref[...] = dmix.astype(dmix_ref.dtype)
        _accumulate(dwpre_ref, i == 0, dwpre)
        _accumulate(dwpost_ref, i == 0, dwpost)

        @pl.when(i == n // tr - 1)
        def _():
            for cp in exchange:
                cp.wait_recv()
            for cp in exchange:
                cp.wait_send()

    dh1, dmix, dwpre, dwpost, *got = pl.pallas_call(
        body, name="mid_bwd",
        out_shape=(jax.ShapeDtypeStruct((n, d), F32), jax.ShapeDtypeStruct((n, d), MXU_DTYPE),
                   jax.ShapeDtypeStruct((1, d), F32), jax.ShapeDtypeStruct((1, d), F32))
        + tuple(jax.ShapeDtypeStruct((g.shape[0],) + g.shape[2:], F32) for g in grads),
        grid=(n // tr,),
        in_specs=[_rows(tr, d), _rows(tr, d), _vec(d), _vec(d), _rows(tr, d), _rows(tr, d)] + [_hbm()] * count,
        out_specs=(_rows(tr, d), _rows(tr, d), _vec(d), _vec(d)) + (_hbm(),) * count,
        scratch_shapes=[pltpu.SemaphoreType.DMA((count,)), pltpu.SemaphoreType.DMA((count,))],
        compiler_params=_params("arbitrary"),
    )(h1, mix, w_mix_post, w_ffn_pre, dh2, du2, *grads)
    return dh1, dmix, dwpre, dwpost, got


def _in_bwd(h0, w_pre, dh1, du1, rows_per_seq, pad_rows, x_offset):
    n, d = h0.shape
    tr = _row_tile(rows_per_seq)
    tiles_per_seq = rows_per_seq // tr
    seq = rows_per_seq - x_offset

    def body(h0_ref, w_ref, dh1_ref, du1_ref, gx_ref, dmeta_ref, dw_ref):
        i = pl.program_id(0)
        tile = lax.rem(i, tiles_per_seq)
        dx, dw = _rms_bwd(h0_ref[...], w_ref[...], du1_ref[...])
        dh0 = dh1_ref[...] + dx
        _accumulate(dw_ref, i == 0, dw)

        @pl.when(tile == 0)
        def _():
            gx_ref[0:tr - x_offset, :] = dh0[x_offset:, :]
            _accumulate(dmeta_ref, i == 0, dh0[pad_rows:x_offset, :])

        if tiles_per_seq > 1:
            @pl.when(tile > 0)
            def _():
                gx_ref[pl.ds(pl.multiple_of(tile * tr - x_offset, SUBLANES), tr), :] = dh0

    return pl.pallas_call(
        body, name="in_bwd",
        out_shape=(jax.ShapeDtypeStruct((n // rows_per_seq, seq, d), F32), jax.ShapeDtypeStruct((x_offset - pad_rows, d), F32),
                   jax.ShapeDtypeStruct((1, d), F32)),
        grid=(n // tr,),
        in_specs=[_rows(tr, d), _vec(d), _rows(tr, d), _rows(tr, d)],
        out_specs=(pl.BlockSpec((None, seq, d), lambda i: (i // tiles_per_seq, 0, 0)),
                   pl.BlockSpec((x_offset - pad_rows, d), lambda i: (0, 0)), _vec(d)),
        compiler_params=_params("arbitrary"),
    )(h0, w_pre, dh1, du1)


def _lane_is(lo, hi):
    lane = lax.broadcasted_iota(jnp.int32, (1, LANES), 1)
    return jnp.logical_and(lane >= lo, lane < hi)


def _gates_fwd(proj, a_log_l, dt_bias_l, rows_per_seq, pad_rows):
    n = proj.shape[0]
    tr = _row_tile(rows_per_seq)
    tiles_per_seq = rows_per_seq // tr

    def body(p_ref, a_ref, dt_ref, o_ref):
        x = p_ref[...]
        row = lax.rem(pl.program_id(0), tiles_per_seq) * tr + lax.broadcasted_iota(jnp.int32, (tr, 1), 0)
        g = -jnp.exp(a_ref[...]) * _softplus(x + dt_ref[...])
        val = jnp.where(_lane_is(0, HEADS), _sigmoid(x), jnp.where(_lane_is(HEADS, 2 * HEADS), g, 0.0))
        o_ref[...] = jnp.where(row >= pad_rows, val, 0.0)

    return pl.pallas_call(
        body, name="gates_fwd", out_shape=jax.ShapeDtypeStruct((n, LANES), F32), grid=(n // tr,),
        in_specs=[pl.BlockSpec((tr, LANES), lambda i: (i, BA_COL)), _vec(LANES), _vec(LANES)],
        out_specs=_rows(tr, LANES), compiler_params=_params("parallel"),
    )(proj, a_log_l, dt_bias_l)


def _gates_bwd(proj, dbg, a_log_l, dt_bias_l, rows_per_seq, pad_rows, dproj):
    n = proj.shape[0]
    tr = _row_tile(rows_per_seq)
    tiles_per_seq = rows_per_seq // tr

    def body(p_ref, d_ref, a_ref, dt_ref, _, dx_ref, da_ref, ddt_ref):
        i = pl.program_id(0)
        x = p_ref[...]
        d = d_ref[...]
        row = lax.rem(i, tiles_per_seq) * tr + lax.broadcasted_iota(jnp.int32, (tr, 1), 0)
        live = row >= pad_rows
        beta = _sigmoid(x)
        ea = jnp.exp(a_ref[...])
        xa = x + dt_ref[...]
        g = -ea * _softplus(xa)
        is_g = _lane_is(HEADS, 2 * HEADS)
        d_alogit = jnp.where(jnp.logical_and(live, is_g), d * (-ea) * _sigmoid(xa), 0.0)
        d_blogit = jnp.where(jnp.logical_and(live, _lane_is(0, HEADS)), d * beta * (1.0 - beta), 0.0)
        dx_ref[:, :LANES] = (d_alogit + d_blogit).astype(dx_ref.dtype)
        dx_ref[:, LANES:] = jnp.zeros((tr, LANES), dx_ref.dtype)
        _accumulate(da_ref, i == 0, jnp.sum(jnp.where(jnp.logical_and(live, is_g), d * g, 0.0), axis=0, keepdims=True))
        _accumulate(ddt_ref, i == 0, jnp.sum(d_alogit, axis=0, keepdims=True))

    return pl.pallas_call(
        body, name="gates_bwd",
        out_shape=(jax.ShapeDtypeStruct(dproj.shape, dproj.dtype), jax.ShapeDtypeStruct((1, LANES), F32),
                   jax.ShapeDtypeStruct((1, LANES), F32)),
        grid=(n // tr,),
        in_specs=[pl.BlockSpec((tr, LANES), lambda i: (i, BA_COL)), _rows(tr, LANES), _vec(LANES), _vec(LANES), _hbm()],
        out_specs=(pl.BlockSpec((tr, 2 * LANES), lambda i: (i, BA_COL // 2)), _vec(LANES), _vec(LANES)),
        input_output_aliases={4: 0},
        compiler_params=_params("arbitrary"),
    )(proj, dbg, a_log_l, dt_bias_l, dproj)


HALO = 8


def _halo_scratch(rs):
    return pltpu.VMEM((rs + 2 * HALO, LANES), F32)


def _stage(ref, x):
    rs = x.shape[0]
    ref[0:HALO, :] = jnp.zeros((HALO, LANES), F32)
    ref[HALO + rs:, :] = jnp.zeros((HALO, LANES), F32)
    ref[HALO:HALO + rs, :] = x


def _shifted(ref, k, rs):
    return ref[pl.ds(HALO - k, rs), :]


def _causal_conv(x, x_staged, w, width):
    acc = w[width - 1:width, :] * x
    for i in range(width - 1):
        acc = acc + w[i:i + 1, :] * _shifted(x_staged, width - 1 - i, x.shape[0])
    return acc


def _anti_causal_conv(dy, dy_staged, w, width):
    acc = w[width - 1:width, :] * dy
    for i in range(width - 1):
        acc = acc + w[i:i + 1, :] * _shifted(dy_staged, -(width - 1 - i), dy.shape[0])
    return acc


def _conv_weight_grad(dy, x, x_staged, width):
    taps = [_shifted(x_staged, width - 1 - i, x.shape[0]) for i in range(width - 1)] + [x]
    return jnp.concatenate([jnp.sum(dy * tap, axis=0, keepdims=True) for tap in taps], axis=0)


def _seq_cols(rs, col0, heads):
    return pl.BlockSpec((rs, heads * LANES), lambda j, b: (b, col0 // heads + j))


def _tap_cols(width, col0, heads):
    return pl.BlockSpec((width, heads * LANES), lambda j, b: (0, col0 // heads + j))


def _lanes_of(h):
    return slice(h * LANES, (h + 1) * LANES)


def _qkv_fwd(proj, conv_w, kind, rs):
    n = proj.shape[0]
    col0 = {"q": 0, "k": HEADS, "v": 2 * HEADS}[kind]
    hb = HEADS

    def body(p_ref, w_ref, o_ref, staged):
        for h in range(hb):
            pre = p_ref[:, _lanes_of(h)]
            _stage(staged, pre)
            c = _causal_conv(pre, staged, w_ref[:, _lanes_of(h)], GDN_CONV)
            s = c * _sigmoid(c)
            if kind != "v":
                s = s * lax.rsqrt(jnp.sum(s * s, axis=-1, keepdims=True) + EPS)
            if kind == "q":
                s = s * (HEAD_DIM ** -0.5)
            o_ref[:, _lanes_of(h)] = s

    return pl.pallas_call(
        body, name="qkv_fwd_" + kind, out_shape=jax.ShapeDtypeStruct((n, GDN_WIDTH), F32), grid=(HEADS // hb, n // rs),
        in_specs=[_seq_cols(rs, col0, hb), _tap_cols(GDN_CONV, col0, hb)],
        out_specs=_seq_cols(rs, 0, hb), scratch_shapes=[_halo_scratch(rs)], compiler_params=_params("parallel", "parallel"),
    )(proj, conv_w)


def _qkv_bwd(dy, proj, conv_w, kind, rs, dproj):
    n = proj.shape[0]
    col0 = {"q": 0, "k": HEADS, "v": 2 * HEADS}[kind]
    hb = HEADS

    def body(dy_ref, p_ref, w_ref, _, dp_ref, dw_ref, pre_staged, dc_staged):
        for h in range(hb):
            lanes = _lanes_of(h)
            pre = p_ref[:, lanes]
            w = w_ref[:, lanes]
            _stage(pre_staged, pre)
            c = _causal_conv(pre, pre_staged, w, GDN_CONV)
            sg = _sigmoid(c)
            s = c * sg
            ds = dy_ref[:, lanes]
            if kind == "q":
                ds = ds * (HEAD_DIM ** -0.5)
            if kind != "v":
                r = lax.rsqrt(jnp.sum(s * s, axis=-1, keepdims=True) + EPS)
                sh = s * r
                ds = r * (ds - sh * jnp.sum(ds * sh, axis=-1, keepdims=True))
            dc = ds * _dsilu(c, sg)
            _stage(dc_staged, dc)
            dp_ref[:, lanes] = _anti_causal_conv(dc, dc_staged, w, GDN_CONV).astype(dp_ref.dtype)
            _accumulate(dw_ref.at[:, lanes], pl.program_id(1) == 0, _conv_weight_grad(dc, pre, pre_staged, GDN_CONV))

    return pl.pallas_call(
        body, name="qkv_bwd_" + kind,
        out_shape=(jax.ShapeDtypeStruct(dproj.shape, dproj.dtype), jax.ShapeDtypeStruct((GDN_CONV, GDN_WIDTH), F32)),
        grid=(HEADS // hb, n // rs),
        in_specs=[_seq_cols(rs, 0, hb), _seq_cols(rs, col0, hb), _tap_cols(GDN_CONV, col0, hb), _hbm()],
        out_specs=(_seq_cols(rs, col0, hb), _tap_cols(GDN_CONV, 0, hb)), input_output_aliases={3: 0},
        scratch_shapes=[_halo_scratch(rs), _halo_scratch(rs)],
        compiler_params=_params("parallel", "arbitrary"),
    )(dy, proj, conv_w, dproj)


SC_COL = 4 * HEADS


def _sc_fwd(proj, conv_w, rs, cat):
    n = proj.shape[0]

    hb = 2

    def body(x_ref, b_ref, c_ref, w_ref, _, y_ref, staged):
        for h in range(hb):
            lanes = _lanes_of(h)
            u = c_ref[:, lanes] * x_ref[:, lanes]
            _stage(staged, u)
            y_ref[:, lanes] = (b_ref[:, lanes] * _causal_conv(u, staged, w_ref[:, lanes], SC_CONV)).astype(y_ref.dtype)

    return pl.pallas_call(
        body, name="sc_fwd", out_shape=jax.ShapeDtypeStruct(cat.shape, cat.dtype), grid=(HEADS // hb, n // rs),
        in_specs=[_seq_cols(rs, SC_COL, hb), _seq_cols(rs, SC_COL + 4, hb), _seq_cols(rs, SC_COL + 8, hb),
                  _tap_cols(SC_CONV, 0, hb), _hbm()],
        out_specs=_seq_cols(rs, HEADS, hb), input_output_aliases={4: 0}, scratch_shapes=[_halo_scratch(rs)],
        compiler_params=_params("parallel", "parallel"),
    )(proj, proj, proj, conv_w, cat)


def _sc_bwd(dcat, proj, conv_w, rs, dproj):
    n = proj.shape[0]
    hb = 2

    def body(dy_ref, x_ref, b_ref, c_ref, w_ref, _, dx_ref, db_ref, dc_ref, dw_ref, u_staged, dcv_staged):
        for h in range(hb):
            lanes = _lanes_of(h)
            w = w_ref[:, lanes]
            x = x_ref[:, lanes]
            cc = c_ref[:, lanes]
            u = cc * x
            _stage(u_staged, u)
            dy = dy_ref[:, lanes]
            db_ref[:, lanes] = (dy * _causal_conv(u, u_staged, w, SC_CONV)).astype(db_ref.dtype)
            dcv = dy * b_ref[:, lanes]
            _stage(dcv_staged, dcv)
            du = _anti_causal_conv(dcv, dcv_staged, w, SC_CONV)
            dx_ref[:, lanes] = (du * cc).astype(dx_ref.dtype)
            dc_ref[:, lanes] = (du * x).astype(dc_ref.dtype)
            _accumulate(dw_ref.at[:, lanes], pl.program_id(1) == 0, _conv_weight_grad(dcv, u, u_staged, SC_CONV))

    piece = jax.ShapeDtypeStruct((n, SC_WIDTH), MXU_DTYPE)
    return pl.pallas_call(
        body, name="sc_bwd",
        out_shape=(jax.ShapeDtypeStruct(dproj.shape, dproj.dtype), piece, piece, jax.ShapeDtypeStruct((SC_CONV, SC_WIDTH), F32)),
        grid=(HEADS // hb, n // rs),
        in_specs=[_seq_cols(rs, HEADS, hb), _seq_cols(rs, SC_COL, hb), _seq_cols(rs, SC_COL + 4, hb),
                  _seq_cols(rs, SC_COL + 8, hb), _tap_cols(SC_CONV, 0, hb), _hbm()],
        out_specs=(_seq_cols(rs, SC_COL, hb), _seq_cols(rs, 0, hb), _seq_cols(rs, 0, hb), _tap_cols(SC_CONV, 0, hb)),
        input_output_aliases={5: 0},
        scratch_shapes=[_halo_scratch(rs), _halo_scratch(rs)],
        compiler_params=_params("parallel", "arbitrary"),
    )(dcat, proj, proj, proj, conv_w, dproj)


Z_COL = 3 * HEADS


def _gate_fwd(o, proj, gdn_norm, rs):
    n = proj.shape[0]

    hb = HEADS

    def body(o_ref, z_ref, w_ref, y_ref):
        for h in range(hb):
            lanes = _lanes_of(h)
            z = z_ref[:, lanes]
            y_ref[:, lanes] = (_rms_apply(o_ref[:, lanes], w_ref[...]) * z * _sigmoid(z)).astype(y_ref.dtype)

    return pl.pallas_call(
        body, name="gate_fwd", out_shape=jax.ShapeDtypeStruct((n, D_MODEL), MXU_DTYPE), grid=(HEADS // hb, n // rs),
        in_specs=[_seq_cols(rs, 0, hb), _seq_cols(rs, Z_COL, hb), pl.BlockSpec((1, LANES), lambda j, b: (0, 0))],
        out_specs=_seq_cols(rs, 0, hb), compiler_params=_params("parallel", "parallel"),
    )(o, proj, gdn_norm)


def _gate_bwd(dcat, o, proj, gdn_norm, rs):
    n = proj.shape[0]
    hb = 2

    def body(dy_ref, o_ref, z_ref, w_ref, do_ref, dz_ref, dw_ref):
        w = w_ref[...]
        dw_step = jnp.zeros((1, LANES), F32)
        for h in range(hb):
            lanes = _lanes_of(h)
            z = z_ref[:, lanes]
            o = o_ref[:, lanes]
            dy = dy_ref[:, lanes]
            s = _sigmoid(z)
            dz_ref[:, lanes] = (dy * _rms_apply(o, w) * _dsilu(z, s)).astype(dz_ref.dtype)
            do, dw = _rms_bwd(o, w, dy * z * s)
            do_ref[:, lanes] = do
            dw_step = dw_step + dw
        _accumulate(dw_ref, jnp.logical_and(pl.program_id(0) == 0, pl.program_id(1) == 0), dw_step)

    return pl.pallas_call(
        body, name="gate_bwd",
        out_shape=(jax.ShapeDtypeStruct((n, GDN_WIDTH), F32), jax.ShapeDtypeStruct((n, IN_PAD), MXU_DTYPE),
                   jax.ShapeDtypeStruct((1, LANES), F32)),
        grid=(HEADS // hb, n // rs),
        in_specs=[_seq_cols(rs, 0, hb), _seq_cols(rs, 0, hb), _seq_cols(rs, Z_COL, hb), pl.BlockSpec((1, LANES), lambda j, b: (0, 0))],
        out_specs=(_seq_cols(rs, 0, hb), _seq_cols(rs, Z_COL, hb), pl.BlockSpec((1, LANES), lambda j, b: (0, 0))),
        compiler_params=_params("arbitrary", "arbitrary"),
    )(dcat, o, proj, gdn_norm)


def _dot(a, b):
    return jnp.dot(a.astype(MXU_DTYPE), b.astype(MXU_DTYPE), preferred_element_type=F32)


def _dot_nt(a, b):
    return lax.dot_general(a.astype(MXU_DTYPE), b.astype(MXU_DTYPE), (((1,), (1,)), ((), ())),
                           preferred_element_type=F32)


def _dot_tn(a, b):
    return lax.dot_general(a.astype(MXU_DTYPE), b.astype(MXU_DTYPE), (((0,), (0,)), ((), ())),
                           preferred_element_type=F32)


def _split(x):
    hi = x.astype(MXU_DTYPE)
    return hi, (x - hi.astype(F32)).astype(MXU_DTYPE)


def _dot_split(a, b):
    mm = functools.partial(jnp.dot, preferred_element_type=F32)
    return mm(a[0], b[0]) + (mm(a[0], b[1]) + mm(a[1], b[0]))


def _unit_lower_inverses(mats, eye):
    inv = [eye - a for a in mats]
    power = [_split(a) for a in mats]
    span = 2
    while span < CHUNK:
        power = [_split(_dot_split(p, p)) for p in power]
        inv = [i + _dot_split(_split(i), p) for i, p in zip(inv, power)]
        span *= 2
    return inv


def _chunk_masks():
    ii = lax.broadcasted_iota(jnp.int32, (CHUNK, CHUNK), 0)
    jj = lax.broadcasted_iota(jnp.int32, (CHUNK, CHUNK), 1)
    return ii, jj


def _chunk_decay(g_col, ii, jj):
    incl = ii >= jj
    g_row = jnp.sum(jnp.where(ii == jj, g_col, 0.0), axis=0, keepdims=True)
    gc_col = jnp.sum(jnp.where(incl, g_row, 0.0), axis=1, keepdims=True)
    gc_row = jnp.sum(jnp.where(ii <= jj, g_col, 0.0), axis=0, keepdims=True)
    g_total = jnp.sum(g_row, axis=1, keepdims=True)
    decay = jnp.where(incl, jnp.exp(jnp.where(incl, gc_col - gc_row, 0.0)), 0.0)
    return gc_col, g_total, decay


def _gdn_segments(rs, candidates):
    chunks = rs // CHUNK
    seg_chunks = _pick(chunks, candidates)
    return chunks, seg_chunks, chunks // seg_chunks


def _gdn_fwd(q, k, v, bg, rs, pieces):
    n = q.shape[0]
    batch = n // rs
    chunks, seg_chunks, segs = _gdn_segments(rs, (11, 8, 4, 2))
    seg_rows = seg_chunks * CHUNK
    chains = [(b, h) for b in range(batch) for h in range(HEADS)]
    each = lambda f, *lists: [f(*args) for args in zip(*lists)]
    count = len(pieces)

    def body(q_ref, k_ref, v_ref, bg_ref, *rest):
        w_refs, (o_ref, s_ref, t_ref), out_refs = rest[:count], rest[count:count + 3], rest[count + 3:2 * count + 3]
        state_ref, send_sems, recv_sems = rest[2 * count + 3:]
        gather = _gather_copies(w_refs, out_refs, send_sems, recv_sems)

        @pl.when(pl.program_id(0) == 0)
        def _():
            state_ref[...] = jnp.zeros_like(state_ref)
            for cp in gather[0]:
                cp.start()

        ii, jj = _chunk_masks()
        incl = ii >= jj
        eye = (ii == jj).astype(F32)

        def chunk(c, carry):
            rows = pl.ds(pl.multiple_of(c * CHUNK, CHUNK), CHUNK)
            bgc = [bg_ref[b, rows, :] for b in range(batch)]
            qc = [q_ref[b, rows, _lanes_of(h)] for b, h in chains]
            kc = [k_ref[b, rows, _lanes_of(h)] for b, h in chains]
            vc = [v_ref[b, rows, _lanes_of(h)] for b, h in chains]
            beta = [bgc[b][:, h:h + 1] for b, h in chains]
            state = [state_ref[b, h] for b, h in chains]
            dec = [_chunk_decay(bgc[b][:, HEADS + h:HEADS + h + 1], ii, jj) for b, h in chains]
            gc_col, g_total, decay = ([d[i] for d in dec] for i in range(3))
            kb = each(lambda x, y: x * y, kc, beta)
            a = each(lambda x, y, d: jnp.where(ii > jj, _dot_nt(x, y) * d, 0.0), kb, kc, decay)
            t_inv = _unit_lower_inverses(a, eye)
            eg = [jnp.exp(g) for g in gc_col]
            u = each(lambda t, x, y: _dot(t, x * y), t_inv, vc, beta)
            w = each(lambda t, x, e: _dot(t, x * e), t_inv, kb, eg)
            qk = each(lambda x, y, d: jnp.where(incl, _dot_nt(x, y) * d, 0.0), qc, kc, decay)
            v_new = each(lambda x, y, s: x - _dot(y, s), u, w, state)
            o = each(lambda x, e, s, m, vn: _dot(x * e, s) + _dot(m, vn), qc, eg, state, qk, v_new)
            new_state = each(lambda s, gt, x, g, vn: s * jnp.exp(gt) + _dot_tn(x * jnp.exp(gt - g), vn),
                             state, g_total, kc, gc_col, v_new)
            for i, (b, h) in enumerate(chains):
                s_ref[b, h, c] = state[i]
                t_ref[b, h, c] = t_inv[i]
                o_ref[b, rows, _lanes_of(h)] = o[i]
                state_ref[b, h] = new_state[i]
            return carry

        lax.fori_loop(0, seg_chunks, chunk, 0)

        @pl.when(pl.program_id(0) == segs - 1)
        def _():
            _gather_finish(gather)

    rows_spec = lambda width: pl.BlockSpec((batch, seg_rows, width), lambda s: (0, s, 0))
    per_chunk = lambda r, c: pl.BlockSpec((batch, HEADS, seg_chunks, r, c), lambda s: (0, 0, s, 0, 0))
    as_seqs = lambda a: a.reshape(batch, rs, a.shape[-1])
    sems = GATHER_SEMS * count
    o, states, t_invs, *gathered = pl.pallas_call(
        body, name="gdn_fwd",
        out_shape=(jax.ShapeDtypeStruct((batch, rs, GDN_WIDTH), F32),
                   jax.ShapeDtypeStruct((batch, HEADS, chunks, HEAD_DIM, HEAD_DIM), F32),
                   jax.ShapeDtypeStruct((batch, HEADS, chunks, CHUNK, CHUNK), F32))
        + tuple(jax.ShapeDtypeStruct((N_CHIPS,) + p.shape, p.dtype) for p in pieces),
        grid=(segs,),
        in_specs=[rows_spec(GDN_WIDTH), rows_spec(GDN_WIDTH), rows_spec(GDN_WIDTH), rows_spec(LANES)] + [_hbm()] * count,
        out_specs=(rows_spec(GDN_WIDTH), per_chunk(HEAD_DIM, HEAD_DIM), per_chunk(CHUNK, CHUNK)) + (_hbm(),) * count,
        scratch_shapes=[pltpu.VMEM((batch, HEADS, HEAD_DIM, HEAD_DIM), F32), pltpu.SemaphoreType.DMA((sems,)),
                        pltpu.SemaphoreType.DMA((sems,))],
        compiler_params=_params("arbitrary"),
    )(as_seqs(q), as_seqs(k), as_seqs(v), as_seqs(bg), *pieces)
    return o.reshape(n, GDN_WIDTH), states, t_invs, gathered


def _gdn_bwd(do, q, k, v, bg, states, t_invs, rs, parts):
    n = q.shape[0]
    batch = n // rs
    chunks, seg_chunks, segs = _gdn_segments(rs, (3, 4, 2))
    seg_rows = seg_chunks * CHUNK
    chains = [(b, h) for b in range(batch) for h in range(HEADS)]
    each = lambda f, *lists: [f(*args) for args in zip(*lists)]
    count = len(parts)

    def body(do_ref, q_ref, k_ref, v_ref, bg_ref, s_ref, t_ref, *rest):
        p_refs, (dq_ref, dk_ref, dv_ref, dbg_ref), got_refs = rest[:count], rest[count:count + 4], rest[count + 4:2 * count + 4]
        dstate_ref, send_sems, recv_sems = rest[2 * count + 4:]
        exchange = _chip_copies(p_refs, got_refs, send_sems, recv_sems)

        @pl.when(pl.program_id(0) == 0)
        def _():
            dstate_ref[...] = jnp.zeros_like(dstate_ref)
            for cp in exchange:
                cp.start()

        ii, jj = _chunk_masks()
        incl = ii >= jj
        strict = ii > jj
        lane = lax.broadcasted_iota(jnp.int32, (1, LANES), 1)

        def rowsum(x):
            return jnp.sum(x, axis=1, keepdims=True)

        def total(x):
            return jnp.sum(rowsum(x), axis=0, keepdims=True)

        def chunk(step, carry):
            c = seg_chunks - 1 - step
            rows = pl.ds(pl.multiple_of(c * CHUNK, CHUNK), CHUNK)
            bgc = [bg_ref[b, rows, :] for b in range(batch)]
            qc = [q_ref[b, rows, _lanes_of(h)] for b, h in chains]
            kc = [k_ref[b, rows, _lanes_of(h)] for b, h in chains]
            vc = [v_ref[b, rows, _lanes_of(h)] for b, h in chains]
            doc = [do_ref[b, rows, _lanes_of(h)] for b, h in chains]
            beta = [bgc[b][:, h:h + 1] for b, h in chains]
            state = [s_ref[b, h, c] for b, h in chains]
            t_inv = [t_ref[b, h, c] for b, h in chains]
            d_state = [dstate_ref[b, h] for b, h in chains]
            dec = [_chunk_decay(bgc[b][:, HEADS + h:HEADS + h + 1], ii, jj) for b, h in chains]
            gc_col, g_total, decay = ([d[i] for d in dec] for i in range(3))
            kb = each(lambda x, y: x * y, kc, beta)
            vb = each(lambda x, y: x * y, vc, beta)
            eg = [jnp.exp(g) for g in gc_col]
            kbg = each(lambda x, y: x * y, kb, eg)
            a = each(lambda x, y, d: jnp.where(strict, _dot_nt(x, y) * d, 0.0), kb, kc, decay)
            qk = each(lambda x, y, d: jnp.where(incl, _dot_nt(x, y) * d, 0.0), qc, kc, decay)
            w = each(_dot, t_inv, kbg)
            u = each(_dot, t_inv, vb)
            q_dec = each(lambda x, y: x * y, qc, eg)
            ek = each(lambda gt, g: jnp.exp(gt - g), g_total, gc_col)
            k_dec = each(lambda x, y: x * y, kc, ek)
            g_last = [jnp.exp(gt) for gt in g_total]
            v_new = each(lambda x, y, s: x - _dot(y, s), u, w, state)
            dv_new = each(lambda m, d, x, ds: _dot_tn(m, d) + _dot(x, ds), qk, doc, k_dec, d_state)
            dqk = each(lambda d, vn: jnp.where(incl, _dot_nt(d, vn), 0.0), doc, v_new)
            dq_dec = each(_dot_nt, doc, state)
            dk_dec = each(_dot_nt, v_new, d_state)
            dg_last = each(lambda s, ds: total(s * ds), state, d_state)
            new_d_state = each(lambda x, d, gl, ds, y, dvn: _dot_tn(x, d) + gl * ds - _dot_tn(y, dvn),
                               q_dec, doc, g_last, d_state, w, dv_new)
            dw = each(lambda dvn, s: -_dot_nt(dvn, s), dv_new, state)
            dt = each(lambda dvn, x, y, z: _dot_nt(dvn, x) + _dot_nt(y, z), dv_new, vb, dw, kbg)
            dvb = each(_dot_tn, t_inv, dv_new)
            dkbg = each(_dot_tn, t_inv, dw)
            t_dt = each(_dot_tn, t_inv, dt)
            da = each(lambda x, t: -jnp.where(strict, _dot_nt(x, t), 0.0), t_dt, t_inv)
            dm_a = each(lambda x, y: x * y, da, decay)
            dm_qk = each(lambda x, y: x * y, dqk, decay)
            e = each(lambda x, y, z, t: x * y + z * t, da, a, dqk, qk)
            dkb = each(lambda m, x, y, z: _dot(m, x) + y * z, dm_a, kc, dkbg, eg)
            dk = each(lambda m, x, m2, y, z, t, p, bt: _dot_tn(m, x) + _dot_tn(m2, y) + z * t + p * bt,
                      dm_a, kb, dm_qk, qc, dk_dec, ek, dkb, beta)
            dq = each(lambda m, x, y, z: _dot(m, x) + y * z, dm_qk, kc, dq_dec, eg)
            dbeta = each(lambda x, y, z, t: rowsum(x * y + z * t), dkb, kc, dvb, vc)
            dgc = each(lambda x, p, pd, r, rd, s, sd: rowsum(x) - rowsum(jnp.where(ii == jj, jnp.sum(x, axis=0, keepdims=True), 0.0))
                       + rowsum(p * pd - r * rd + s * sd), e, dq_dec, q_dec, dk_dec, k_dec, dkbg, kbg)
            d_total = each(lambda r, rd, x, gl: total(r * rd) + x * gl, dk_dec, k_dec, dg_last, g_last)
            dg = each(lambda x, t: rowsum(jnp.where(jj >= ii, jnp.sum(jnp.where(ii == jj, x, 0.0), axis=0, keepdims=True), 0.0)) + t,
                      dgc, d_total)
            dbg = [jnp.zeros((CHUNK, LANES), F32) for _ in range(batch)]
            for i, (b, h) in enumerate(chains):
                dstate_ref[b, h] = new_d_state[i]
                dk_ref[b, rows, _lanes_of(h)] = dk[i]
                dq_ref[b, rows, _lanes_of(h)] = dq[i]
                dv_ref[b, rows, _lanes_of(h)] = dvb[i] * beta[i]
                dbg[b] = dbg[b] + jnp.where(lane == h, dbeta[i], 0.0) + jnp.where(lane == HEADS + h, dg[i], 0.0)
            for b in range(batch):
                dbg_ref[b, rows, :] = dbg[b]
            return carry

        lax.fori_loop(0, seg_chunks, chunk, 0)

        @pl.when(pl.program_id(0) == segs - 1)
        def _():
            for cp in exchange:
                cp.wait_recv()
            for cp in exchange:
                cp.wait_send()

    rows_spec = lambda width: pl.BlockSpec((batch, seg_rows, width), lambda s: (0, segs - 1 - s, 0))
    per_chunk = lambda r, c: pl.BlockSpec((batch, HEADS, seg_chunks, r, c), lambda s: (0, 0, segs - 1 - s, 0, 0))
    as_seqs = lambda a: a.reshape(batch, rs, a.shape[-1])
    grad = jax.ShapeDtypeStruct((batch, rs, GDN_WIDTH), F32)
    wide = rows_spec(GDN_WIDTH)
    dq, dk, dv, dbg, *got = pl.pallas_call(
        body, name="gdn_bwd",
        out_shape=(grad, grad, grad, jax.ShapeDtypeStruct((batch, rs, LANES), F32))
        + tuple(jax.ShapeDtypeStruct((3,) + p.shape[1:], p.dtype) for p in parts),
        grid=(segs,),
        in_specs=[wide, wide, wide, wide, rows_spec(LANES), per_chunk(HEAD_DIM, HEAD_DIM), per_chunk(CHUNK, CHUNK)]
        + [_hbm()] * count,
        out_specs=(wide, wide, wide, rows_spec(LANES)) + (_hbm(),) * count,
        scratch_shapes=[pltpu.VMEM((batch, HEADS, HEAD_DIM, HEAD_DIM), F32), pltpu.SemaphoreType.DMA((3 * count,)),
                        pltpu.SemaphoreType.DMA((3 * count,))],
        compiler_params=_params("arbitrary"),
    )(as_seqs(do), as_seqs(q), as_seqs(k), as_seqs(v), as_seqs(bg), states, t_invs, *parts)
    return dq.reshape(n, GDN_WIDTH), dk.reshape(n, GDN_WIDTH), dv.reshape(n, GDN_WIDTH), dbg.reshape(n, LANES), got


def _lane_vec(vals, offset):
    k = vals.shape[1]
    return jnp.pad(vals, ((0, 0), (offset, LANES - offset - k)))


LATER = ("w_out", "w_gate", "w_up", "w_down")


def _halves(a):
    return a.reshape(a.shape[:-2] + (2, a.shape[-2] // 2, a.shape[-1]))


def _local_step(x, target, meta, norms, w_in_shard, conv_qkv, a_log, dt_bias, gdn_norm, conv_sc, later_shards, core_arg):
    batch, seq, d = x.shape
    tokens = N_META + seq
    pad_rows = (-tokens) % CHUNK
    rs = tokens + pad_rows
    x_offset = pad_rows + N_META
    n = batch * rs
    w_mix_pre, w_mix_post, w_ffn_pre, w_ffn_post = norms

    head = jnp.concatenate([jnp.zeros((pad_rows, d), F32), meta], axis=0)
    a_log_l = _lane_vec(a_log, HEADS)
    dt_bias_l = _lane_vec(dt_bias, HEADS)

    h0, u1, w_in_all = _embed(x, head, w_mix_pre, w_in_shard, rs)
    w_in_t = _in_to_kernel_order(w_in_all.reshape(N_CHIPS, -1, d))
    proj = _mm(u1, w_in_t, "nt", F32, "mm_proj")
    q = _qkv_fwd(proj, conv_qkv, "q", rs)
    k = _qkv_fwd(proj, conv_qkv, "k", rs)
    v = _qkv_fwd(proj, conv_qkv, "v", rs)
    bg = _gates_fwd(proj, a_log_l, dt_bias_l, rs, pad_rows)
    o, states, t_invs, gathered = _gdn_fwd(q, k, v, bg, rs, later_shards[:3])
    w_out, w_gate_t, w_up_t = (a.reshape(-1, d) for a in gathered)
    cat = _sc_fwd(proj, conv_sc, rs, _gate_fwd(o, proj, gdn_norm, rs))
    mix = _mm(cat, w_out, "nn", F32, "mm_mix")
    h1, u2 = _mix_residual(h0, mix, w_mix_post, w_ffn_pre)
    gate, up, act, w_down = _swiglu_fwd(u2, w_gate_t, w_up_t, later_shards[3])
    w_down = w_down.reshape(-1, d)
    ffn = _mm(act, w_down, "nn", F32, "mm_down")

    dh2, dffn, d_ffn_post, sq = _loss_head(h1, ffn, w_ffn_post, target, rs, x_offset)
    d_w_down = _mm(act, dffn, "tn", F32, "mm_dw_down")
    dgate, dup = _swiglu_bwd(dffn, w_down, gate, up)
    d_w_gate_t = _mm(dgate, u2, "tn", F32, "mm_dw_gate")
    d_w_up_t = _mm(dup, u2, "tn", F32, "mm_dw_up")
    du2 = _mm(dup, w_up_t, "nn", F32, "mm_du2_up", init=_mm(dgate, w_gate_t, "nn", F32, "mm_du2_gate"))
    by_chip = [_halves(g.reshape(N_CHIPS, -1, d)) for g in (d_w_gate_t, d_w_up_t, d_w_down)]
    dh1, dmix, d_ffn_pre, d_mix_post, got_sibling = _mid_bwd(h1, mix, w_mix_post, w_ffn_pre, dh2, du2, by_chip)
    dcat = _mm(dmix, w_out, "nt", F32, "mm_dcat")
    d_w_out = _halves(_mm(cat, dmix, "tn", F32, "mm_dw_out").reshape(N_CHIPS, -1, d))
    sums = (_add_sibling([d_w_out], _exchange_siblings([d_w_out]), core_arg, "w_out")
            + _add_sibling(by_chip, got_sibling, core_arg, "ffn"))
    do, dproj, d_gdn_norm = _gate_bwd(dcat, o, proj, gdn_norm, rs)
    dproj, dscb, dscc, d_conv_sc = _sc_bwd(dcat, proj, conv_sc, rs, dproj)
    dq, dk, dv, dbg, got_chips = _gdn_bwd(do, q, k, v, bg, states, t_invs, rs, [send for _, send in sums[:3]])
    dproj, dwq = _qkv_bwd(dq, proj, conv_qkv, "q", rs, dproj)
    dproj, dwk = _qkv_bwd(dk, proj, conv_qkv, "k", rs, dproj)
    dproj, dwv = _qkv_bwd(dv, proj, conv_qkv, "v", rs, dproj)
    d_conv_qkv = jnp.concatenate([dwq, dwk, dwv], axis=1)
    dproj, d_a_log_l, d_dt_bias_l = _gates_bwd(proj, dbg, a_log_l, dt_bias_l, rs, pad_rows, dproj)
    dproj = lax.dynamic_update_slice(dproj, dscb, (0, (SC_COL + HEADS) * LANES))
    dproj = lax.dynamic_update_slice(dproj, dscc, (0, (SC_COL + 2 * HEADS) * LANES))
    d_w_in_t, got_down = _mm(dproj, u1, "tn", F32, "mm_dw_in", exchange=[sums[3][1]])
    got_chips.append(got_down)
    g_in = _halves(_in_from_kernel_order(d_w_in_t))
    sums = _add_sibling([g_in], _exchange_siblings([g_in]), core_arg, "w_in") + sums
    du1, got_in = _mm(dproj, w_in_t, "nn", F32, "mm_du1", exchange=[sums[0][1]])
    got_chips.insert(0, got_in)
    grad_x, d_meta, d_mix_pre = _in_bwd(h0, w_mix_pre, dh1, du1, rs, pad_rows, x_offset)

    grads = dict(
        meta_tokens=d_meta,
        mix_pre_norm=d_mix_pre, mix_post_norm=d_mix_post, ffn_pre_norm=d_ffn_pre, ffn_post_norm=d_ffn_post,
        conv_qkv=d_conv_qkv,
        a_log=d_a_log_l[:, HEADS:2 * HEADS], dt_bias=d_dt_bias_l[:, HEADS:2 * HEADS],
        gdn_norm=d_gdn_norm, conv_sc=d_conv_sc,
    )
    return sq, grad_x, grads, [(part, got) for (part, _), got in zip(sums, got_chips)]


MATRICES = ("w_in", "w_out", "w_gate", "w_up", "w_down")
IN_SHARD = IN_WIDTH // N_CHIPS
IN_SHARD_PAD = 928


IN_SEGMENTS = ((0, 0, 4 * GDN_WIDTH), (4 * GDN_WIDTH, IN_WIDTH - 2 * HEADS, 2 * HEADS),
               (4 * GDN_WIDTH + 2 * HEADS, 4 * GDN_WIDTH, 3 * SC_WIDTH))
SUBLANES = 8
PACKED_ROWS = 16


def _in_to_kernel_order(by_chip):
    d = by_chip.shape[-1]
    tl = _pick(d, (256, 128))
    runs = []
    for ref0, ker0, count in IN_SEGMENTS:
        row = ref0
        while row < ref0 + count:
            chip, at = divmod(row, IN_SHARD)
            take = min(ref0 + count - row, IN_SHARD - at)
            runs.append((ker0 + row - ref0, take, chip * IN_SHARD_PAD + at))
            row += take

    def body(w_ref, o_ref):
        o_ref[...] = jnp.zeros_like(o_ref)
        for out0, rows, src0 in runs:
            a0 = out0 // PACKED_ROWS * PACKED_ROWS
            a1 = -(-(out0 + rows) // PACKED_ROWS) * PACKED_ROWS
            window = w_ref[pl.ds(src0 - (out0 - a0), a1 - a0), :]
            row = a0 + lax.broadcasted_iota(jnp.int32, (a1 - a0, 1), 0)
            keep = jnp.logical_and(row >= out0, row < out0 + rows)
            o_ref[a0:a1, :] = jnp.where(keep, window, o_ref[a0:a1, :])

    return pl.pallas_call(
        body, name="in_to_kernel_order", out_shape=jax.ShapeDtypeStruct((IN_PAD, d), by_chip.dtype), grid=(d // tl,),
        in_specs=[pl.BlockSpec((N_CHIPS * IN_SHARD_PAD, tl), lambda j: (0, j))],
        out_specs=pl.BlockSpec((IN_PAD, tl), lambda j: (0, j)),
        compiler_params=_params("parallel"),
    )(by_chip.reshape(N_CHIPS * IN_SHARD_PAD, d))


def _in_from_kernel_order(g_t):
    d = g_t.shape[-1]
    tl = _pick(d, (256, 128))

    def body(g_ref, o_ref):
        row = lax.broadcasted_iota(jnp.int32, (IN_SHARD_PAD, 1), 0)
        for chip in range(N_CHIPS):
            first = chip * IN_SHARD
            runs = []
            for ref0, ker0, count in IN_SEGMENTS:
                lo, hi = max(ref0, first), min(ref0 + count, first + IN_SHARD)
                if lo < hi:
                    runs.append((lo - first, hi - lo, ker0 + lo - ref0))
            val = jnp.zeros((IN_SHARD_PAD, tl), F32)
            patches = []
            for out0, rows, src0 in runs:
                start = src0 - out0
                if 0 <= start <= IN_PAD - IN_SHARD_PAD:
                    window = g_ref[pl.ds(start, IN_SHARD_PAD), :]
                    val = jnp.where(jnp.logical_and(row >= out0, row < out0 + rows), window, val)
                else:
                    patches.append((out0, rows, src0))
            o_ref[chip] = val
            for out0, rows, src0 in patches:
                a0 = out0 // SUBLANES * SUBLANES
                a1 = -(-(out0 + rows) // SUBLANES) * SUBLANES
                window = g_ref[pl.ds(src0 - (out0 - a0), a1 - a0), :]
                keep = jnp.logical_and(row[a0:a1] >= out0, row[a0:a1] < out0 + rows)
                o_ref[chip, a0:a1, :] = jnp.where(keep, window, o_ref[chip, a0:a1, :])

    return pl.pallas_call(
        body, name="in_from_kernel_order", out_shape=jax.ShapeDtypeStruct((N_CHIPS, IN_SHARD_PAD, d), F32), grid=(d // tl,),
        in_specs=[pl.BlockSpec((IN_PAD, tl), lambda j: (0, j))],
        out_specs=pl.BlockSpec((N_CHIPS, IN_SHARD_PAD, tl), lambda j: (0, 0, j)),
        compiler_params=_params("parallel"),
    )(g_t)


PACK_LANES = 3 * GDN_WIDTH
PACKED = dict(mix_pre_norm=(0, 1, 0, D_MODEL), mix_post_norm=(1, 1, 0, D_MODEL), ffn_pre_norm=(2, 1, 0, D_MODEL),
              ffn_post_norm=(3, 1, 0, D_MODEL), a_log=(4, 1, 0, HEADS), dt_bias=(5, 1, 0, HEADS), loss=(6, 1, 0, 1),
              gdn_norm=(7, 1, 0, HEAD_DIM), conv_qkv=(8, GDN_CONV, 0, 3 * GDN_WIDTH), conv_sc=(0, SC_CONV, D_MODEL, SC_WIDTH),
              meta_tokens=(16, N_META, 0, D_MODEL))
PACK_ROWS = 32
SHARDED_SMALL = ("conv_qkv", "conv_sc", "meta_tokens")


def _pack_small(values):
    names = list(PACKED)

    def body(*refs):
        out_ref = refs[-1]
        out_ref[...] = jnp.zeros_like(out_ref)
        for name, ref in zip(names, refs):
            row, rows, lane0, lanes = PACKED[name]
            out_ref[row:row + rows, lane0:lane0 + lanes] = ref[...]

    return pl.pallas_call(body, name="pack_small", out_shape=jax.ShapeDtypeStruct((PACK_ROWS, PACK_LANES), F32))(
        *[values[name] for name in names])


def _sum_devices(packed_all, chip):
    names = list(PACKED)

    def body(chip_ref, all_ref, *rest):
        shard_refs, out_refs = rest[:len(SHARDED_SMALL)], rest[len(SHARDED_SMALL):]

        def total(ref, rows, lanes):
            acc = ref[0, rows, lanes]
            for k in range(1, 8):
                acc = acc + ref[k, rows, lanes]
            return acc

        for name, out in zip(names, out_refs):
            row, rows, lane0, lanes = PACKED[name]
            if name in SHARDED_SMALL:
                out[...] = total(shard_refs[SHARDED_SMALL.index(name)], slice(0, rows), slice(None))
            else:
                out[...] = total(all_ref, slice(row, row + rows), slice(lane0, lane0 + lanes))

    def shard_spec(name):
        row, rows, lane0, lanes = PACKED[name]
        height, width = max(rows, 8), lanes // N_CHIPS
        assert row % height == 0 and lane0 % width == 0
        return pl.BlockSpec((8, height, width), lambda i, chip_ref: (0, row // height, lane0 // width + chip_ref[0]))

    def out_shape(name):
        _, rows, _, lanes = PACKED[name]
        return jax.ShapeDtypeStruct((rows, lanes // N_CHIPS if name in SHARDED_SMALL else lanes), F32)

    whole = lambda shape: pl.BlockSpec(shape, lambda i, chip_ref: (0,) * len(shape))
    outs = pl.pallas_call(
        body, name="sum_devices", out_shape=tuple(out_shape(n) for n in names),
        grid_spec=pltpu.PrefetchScalarGridSpec(
            num_scalar_prefetch=1, grid=(1,),
            in_specs=[whole(packed_all.shape)] + [shard_spec(n) for n in SHARDED_SMALL],
            out_specs=tuple(whole(out_shape(n).shape) for n in names)),
    )(chip, packed_all, *[packed_all] * len(SHARDED_SMALL))
    return dict(zip(names, outs))


def _hbm():
    return pl.BlockSpec(memory_space=pl.ANY)


def _place():
    x, y, c = lax.axis_index("x"), lax.axis_index("y"), lax.axis_index("c")
    chips = ((1 - x, y), (x, 1 - y), (1 - x, 1 - y))
    return x, y, c, chips


def _remote(src, dst, send_sems, recv_sems, k, to):
    return pltpu.make_async_remote_copy(src_ref=src, dst_ref=dst, send_sem=send_sems.at[k], recv_sem=recv_sems.at[k],
                                        device_id=to, device_id_type=MESH)


GATHER_SEMS = 7


def _gather_copies(w_refs, out_refs, send_sems, recv_sems):
    x, y, c, chips = _place()
    mine = 2 * x + y
    sibling = (x, y, 1 - c)
    copy = functools.partial(_remote, send_sems=send_sems, recv_sems=recv_sems)
    direct, landed, passing, from_sibling = [], [], [], []
    for i, (w, o) in enumerate(zip(w_refs, out_refs)):
        k = GATHER_SEMS * i
        direct.append(copy(w, o.at[mine], k=k, to=sibling))
        from_sibling.append(copy(w, o.at[mine], k=k, to=sibling))
        for j, (cx, cy) in enumerate(chips):
            theirs = 2 * cx + cy
            direct.append(copy(w.at[c], o.at[mine, c], k=k + 1 + j, to=(cx, cy, c)))
            landed.append(copy(w.at[c], o.at[theirs, c], k=k + 1 + j, to=sibling))
            passing.append(copy(o.at[theirs, c], o.at[theirs, c], k=k + 4 + j, to=sibling))
            from_sibling.append(copy(w.at[c], o.at[theirs, 1 - c], k=k + 4 + j, to=sibling))
    return direct, landed, passing, from_sibling


def _gather_finish(copies):
    direct, landed, passing, from_sibling = copies
    for arrival, forward in zip(landed, passing):
        arrival.wait_recv()
        forward.start()
    for arrival in from_sibling:
        arrival.wait_recv()
    for cp in direct + passing:
        cp.wait_send()


def _gather_weights(pieces, smalls):
    count, extra = len(pieces), len(smalls)
    total = count + extra

    def body(*refs):
        w_refs, s_refs = refs[:count], refs[count:total]
        out_refs, sall_refs = refs[total:total + count], refs[total + count:2 * total]
        send_sems, recv_sems, local_sems = refs[2 * total:]
        x, y, c, chips = _place()
        mine = 2 * x + y
        own = [pltpu.make_async_copy(s, sall.at[mine], local_sems.at[i]) for i, (s, sall) in enumerate(zip(s_refs, sall_refs))]
        small = [_remote(s, sall.at[mine], send_sems, recv_sems, GATHER_SEMS * count + 3 * i + j, (cx, cy, c))
                 for i, (s, sall) in enumerate(zip(s_refs, sall_refs)) for j, (cx, cy) in enumerate(chips)]
        copies = _gather_copies(w_refs, out_refs, send_sems, recv_sems)
        for cp in own + small + copies[0]:
            cp.start()
        _gather_finish(copies)
        for cp in small:
            cp.wait_recv()
        for cp in small:
            cp.wait_send()
        for cp in own:
            cp.wait()

    sems = GATHER_SEMS * count + 3 * extra
    return pl.pallas_call(
        body, name="gather_weights",
        out_shape=tuple(jax.ShapeDtypeStruct((N_CHIPS,) + p.shape, p.dtype) for p in list(pieces) + list(smalls)),
        in_specs=[_hbm()] * total, out_specs=(_hbm(),) * total,
        scratch_shapes=[pltpu.SemaphoreType.DMA((sems,)), pltpu.SemaphoreType.DMA((sems,)), pltpu.SemaphoreType.DMA((extra,))],
    )(*pieces, *smalls)


def _sibling_copies(g_refs, got_refs, send_sems, recv_sems):
    x, y, c, _ = _place()
    return [_remote(g.at[:, 1 - c], got, send_sems, recv_sems, i, (x, y, 1 - c)) for i, (g, got) in enumerate(zip(g_refs, got_refs))]


def _exchange_siblings(grads):
    count = len(grads)

    def body(*refs):
        copies = _sibling_copies(refs[:count], refs[count:2 * count], *refs[2 * count:])
        for cp in copies:
            cp.start()
        for cp in copies:
            cp.wait_recv()
        for cp in copies:
            cp.wait_send()

    return pl.pallas_call(
        body, name="exchange_siblings",
        out_shape=tuple(jax.ShapeDtypeStruct((g.shape[0],) + g.shape[2:], F32) for g in grads),
        in_specs=[_hbm()] * count, out_specs=(_hbm(),) * count,
        scratch_shapes=[pltpu.SemaphoreType.DMA((count,)), pltpu.SemaphoreType.DMA((count,))],
    )(*grads)


def _chip_copies(p_refs, got_refs, send_sems, recv_sems):
    x, y, c, chips = _place()
    return [_remote(p.at[2 * cx + cy], got.at[j], send_sems, recv_sems, 3 * i + j, (cx, cy, c))
            for i, (p, got) in enumerate(zip(p_refs, got_refs)) for j, (cx, cy) in enumerate(chips)]


def _share_halves(halves, small):
    count = len(halves)

    def body(*refs):
        h_refs, s_ref = refs[:count], refs[count]
        full_refs, sall_ref = refs[count + 1:2 * count + 1], refs[2 * count + 1]
        send_sems, recv_sems, local_sem = refs[2 * count + 2:]
        x, y, c, _ = _place()
        me = 4 * x + 2 * y + c
        own = pltpu.make_async_copy(s_ref, sall_ref.at[me], local_sem)
        own.start()
        copies = [_remote(h.at[c], full.at[c], send_sems, recv_sems, i, (x, y, 1 - c))
                  for i, (h, full) in enumerate(zip(h_refs, full_refs))]
        for k in range(7):
            dx, dy, dc = ((k + 1) >> 2) & 1, ((k + 1) >> 1) & 1, (k + 1) & 1
            peer = (1 - x if dx else x, 1 - y if dy else y, 1 - c if dc else c)
            copies.append(_remote(s_ref, sall_ref.at[me], send_sems, recv_sems, count + k, peer))
        for cp in copies:
            cp.start()
        for cp in copies:
            cp.wait_recv()
        for cp in copies:
            cp.wait_send()
        own.wait()

    return pl.pallas_call(
        body, name="share_halves",
        out_shape=tuple(jax.ShapeDtypeStruct(h.shape, h.dtype) for h in halves) + (jax.ShapeDtypeStruct((8,) + small.shape, F32),),
        in_specs=[_hbm()] * (count + 1), out_specs=(_hbm(),) * (count + 1), input_output_aliases={i: i for i in range(count)},
        scratch_shapes=[pltpu.SemaphoreType.DMA((count + 7,)), pltpu.SemaphoreType.DMA((count + 7,)), pltpu.SemaphoreType.DMA],
    )(*halves, small)


def _add_sibling(grads, gots, core, name):
    count = len(grads)
    chips, _, rows, cols = grads[0].shape

    def body(core_ref, *refs):
        for i in range(count):
            s = refs[i][...] + refs[count + i][...]
            refs[2 * count + 2 * i][...] = s
            refs[2 * count + 2 * i + 1][...] = s.astype(BF16)

    block = pl.BlockSpec((None, rows, cols), lambda p, core_ref: (p, 0, 0))
    own = pl.BlockSpec((None, None, rows, cols), lambda p, core_ref: (p, core_ref[0], 0, 0))
    out = pl.pallas_call(
        body, name="add_sibling_" + name,
        out_shape=(jax.ShapeDtypeStruct((chips, rows, cols), F32), jax.ShapeDtypeStruct((chips, rows, cols), BF16)) * count,
        grid_spec=pltpu.PrefetchScalarGridSpec(
            num_scalar_prefetch=1, grid=(chips,), in_specs=[own] * count + [block] * count, out_specs=(block, block) * count),
        compiler_params=_params("parallel"),
    )(core, *grads, *gots)
    return [(out[2 * i], out[2 * i + 1]) for i in range(count)]


def _add_chips(parts, gots, chip_core, name):
    count = len(parts)
    _, rows, cols = parts[0].shape
    tr = rows // 2 if rows % 32 == 0 else rows

    def body(place_ref, *refs):
        for i in range(count):
            r_ref = refs[count + i]
            refs[2 * count + i][...] = ((refs[i][...] + r_ref[0].astype(F32)) + r_ref[1].astype(F32)) + r_ref[2].astype(F32)

    return pl.pallas_call(
        body, name="add_chips_" + name, out_shape=(jax.ShapeDtypeStruct((2, rows, cols), F32),) * count,
        grid_spec=pltpu.PrefetchScalarGridSpec(
            num_scalar_prefetch=1, grid=(rows // tr,),
            in_specs=[pl.BlockSpec((None, tr, cols), lambda i, place_ref: (place_ref[0], i, 0))] * count
            + [pl.BlockSpec((3, tr, cols), lambda i, place_ref: (0, i, 0))] * count,
            out_specs=(pl.BlockSpec((None, tr, cols), lambda i, place_ref: (place_ref[1], i, 0)),) * count),
        compiler_params=_params("parallel"),
    )(chip_core, *parts, *gots)


def _adamw(ws, gs, ms, vs, name):
    count = len(ws)
    rows, cols = ws[0].shape
    tr = _pick(rows, (256, 352, 176, 128, 64, 32, 16, 8) if count == 1 else (176, 128, 64, 32, 16, 8))

    def body(*refs):
        ins, outs = refs[:4 * count], refs[4 * count:]
        for i in range(count):
            outs[3 * i][...], outs[3 * i + 1][...], outs[3 * i + 2][...] = _adamw_math(
                ins[i][...], ins[count + i][...], ins[2 * count + i][...], ins[3 * count + i][...])

    block = pl.BlockSpec((tr, cols), lambda i: (i, 0))
    shape = jax.ShapeDtypeStruct((rows, cols), F32)
    out = pl.pallas_call(
        body, name="adamw_" + name, out_shape=(shape,) * (3 * count), grid=(rows // tr,),
        in_specs=[block] * (4 * count), out_specs=(block,) * (3 * count), compiler_params=_params("parallel"),
    )(*ws, *gs, *ms, *vs)
    return [tuple(out[3 * i:3 * i + 3]) for i in range(count)]


def _adamw_math(w, g, m, v):
    m = ADAM_B1 * m + (1.0 - ADAM_B1) * g
    v = ADAM_B2 * v + (1.0 - ADAM_B2) * (g * g)
    m_hat = m / (1.0 - ADAM_B1 ** ADAM_STEP)
    v_hat = v / (1.0 - ADAM_B2 ** ADAM_STEP)
    return -ADAM_LR * (m_hat / (jnp.sqrt(v_hat) + ADAM_EPS) + ADAM_WD * w), m, v


def _adamw_small(ws, gs, ms, vs):
    count = len(ws)

    def body(*refs):
        ins, outs = refs[:4 * count], refs[4 * count:]
        for i in range(count):
            outs[i][...], outs[count + i][...], outs[2 * count + i][...] = _adamw_math(
                ins[i][...], ins[count + i][...], ins[2 * count + i][...], ins[3 * count + i][...])

    shapes = tuple(jax.ShapeDtypeStruct(w.shape, F32) for w in ws)
    out = pl.pallas_call(body, name="adamw_small", out_shape=shapes * 3)(*ws, *gs, *ms, *vs)
    return out[:count], out[count:2 * count], out[2 * count:]


WEIGHTS = ("meta_tokens", "mix_pre_norm", "mix_post_norm", "ffn_pre_norm", "ffn_post_norm", "w_in", "conv_qkv", "a_log",
           "dt_bias", "gdn_norm", "conv_sc", "w_out", "w_gate", "w_up", "w_down")


def kernel(x, meta_tokens, mix_pre_norm, mix_post_norm, ffn_pre_norm, ffn_post_norm, w_in, conv_qkv, a_log, dt_bias, gdn_norm, conv_sc, w_out, w_gate, w_up, w_down, loss_target, m_meta_tokens, m_mix_pre_norm, m_mix_post_norm, m_ffn_pre_norm, m_ffn_post_norm, m_w_in, m_conv_qkv, m_a_log, m_dt_bias, m_gdn_norm, m_conv_sc, m_w_out, m_w_gate, m_w_up, m_w_down, v_meta_tokens, v_mix_pre_norm, v_mix_post_norm, v_ffn_pre_norm, v_ffn_post_norm, v_w_in, v_conv_qkv, v_a_log, v_dt_bias, v_gdn_norm, v_conv_sc, v_w_out, v_w_gate, v_w_up, v_w_down):
    d = x.shape[-1]
    two_d = lambda a: a.reshape(a.shape[-2:])
    weights = dict(zip(WEIGHTS, (meta_tokens, mix_pre_norm, mix_post_norm, ffn_pre_norm, ffn_post_norm, w_in, conv_qkv, a_log,
                                 dt_bias, gdn_norm, conv_sc, w_out, w_gate, w_up, w_down)))
    m_in = dict(zip(WEIGHTS, (m_meta_tokens, m_mix_pre_norm, m_mix_post_norm, m_ffn_pre_norm, m_ffn_post_norm, m_w_in, m_conv_qkv,
                              m_a_log, m_dt_bias, m_gdn_norm, m_conv_sc, m_w_out, m_w_gate, m_w_up, m_w_down)))
    v_in = dict(zip(WEIGHTS, (v_meta_tokens, v_mix_pre_norm, v_mix_post_norm, v_ffn_pre_norm, v_ffn_post_norm, v_w_in, v_conv_qkv,
                              v_a_log, v_dt_bias, v_gdn_norm, v_conv_sc, v_w_out, v_w_gate, v_w_up, v_w_down)))
    core = lax.axis_index("c")
    chip = 2 * lax.axis_index("x") + lax.axis_index("y")
    core_arg = core.reshape(1).astype(jnp.int32)
    chip_core = jnp.stack([chip, core]).astype(jnp.int32)
    whole = lambda a: a.reshape(a.shape[:-3] + (2 * a.shape[-2], d))
    by_rows = lambda n, a: two_d(a).T if n in ("w_in", "w_gate", "w_up") else two_d(a)

    shard = {n: by_rows(n, weights[n]).astype(MXU_DTYPE) for n in MATRICES}
    shard["w_in"] = jnp.pad(shard["w_in"], ((0, IN_SHARD_PAD - IN_SHARD), (0, 0)))
    small_all = _gather_weights([], [two_d(weights[n]) for n in SHARDED_SMALL])
    conv_qkv_full, conv_sc_full, meta_full = (jnp.concatenate([a[p] for p in range(N_CHIPS)], axis=1) for a in small_all)

    sq, grad_x, g, sums = _local_step(
        x, loss_target, meta_full, (mix_pre_norm, mix_post_norm, ffn_pre_norm, ffn_post_norm), _halves(shard["w_in"]),
        conv_qkv_full, a_log, dt_bias, gdn_norm, conv_sc_full, [_halves(shard[n]) for n in LATER], core_arg)

    parts, gots = zip(*sums)
    totals = [_add_chips(parts[i:i + 1], gots[i:i + 1], chip_core, MATRICES[i])[0] for i in range(2)]
    totals += _add_chips(parts[2:], gots[2:], chip_core, "ffn")
    *shared, packed_all = _share_halves(totals, _pack_small(dict(g, loss=sq)))
    grads = {n: whole(a) for n, a in zip(MATRICES, shared)}
    grads["w_in"] = grads["w_in"][:IN_SHARD]
    grads.update(_sum_devices(packed_all, chip.reshape(1).astype(jnp.int32)))
    loss = (0.5 / d) * grads.pop("loss")[0, 0]

    small = [n for n in WEIGHTS if n not in MATRICES]
    updates = dict(zip(small, zip(*_adamw_small(*([by_rows(n, params[n]) for n in small] for params in (weights, grads, m_in, v_in))))))
    for group, name in ((("w_in",), "w_in"), (("w_out",), "w_out"), (("w_gate", "w_up", "w_down"), "ffn")):
        updates.update(zip(group, _adamw([by_rows(n, weights[n]) for n in group], [grads[n] for n in group],
                                         [by_rows(n, m_in[n]) for n in group], [by_rows(n, v_in[n]) for n in group], name)))
    outs = [[], [], [], []]
    for n in WEIGHTS:
        shape = weights[n].shape
        for out, a in zip(outs, (grads[n], *updates[n])):
            out.append((a.T if n in ("w_in", "w_gate", "w_up") else a).reshape(shape))
    return (loss, grad_x, *outs[0], *outs[1], *outs[2], *outs[3])
```

```python
import functools

import jax
import jax.numpy as jnp
from jax import lax
from jax.experimental import pallas as pl
from jax.experimental.pallas import tpu as pltpu

F32 = jnp.float32
BF16 = jnp.bfloat16
MXU_DTYPE = jnp.bfloat16
MESH = pl.DeviceIdType.MESH

D_MODEL = 1024
N_META = 16
HEADS = 4
HEAD_DIM = 128
GDN_WIDTH = HEADS * HEAD_DIM
GDN_CONV = 4
CHUNK = 64
SC_WIDTH = D_MODEL - GDN_WIDTH
SC_CONV = 3
D_FF = 2816
IN_WIDTH = 4 * GDN_WIDTH + 2 * HEADS + 3 * SC_WIDTH
IN_PAD = 3840
BA_COL = (4 * GDN_WIDTH + 3 * SC_WIDTH) // 128
EPS = 1e-6
LANES = 128
N_CHIPS = 4
VMEM_LIMIT = 48 * 2 ** 20
MM_VMEM_BUDGET = 42 * 2 ** 20

ADAM_LR = 0.001
ADAM_B1 = 0.9
ADAM_B2 = 0.999
ADAM_EPS = 1e-08
ADAM_WD = 0.01
ADAM_STEP = 10


def _pick(n, candidates):
    for c in candidates:
        if n % c == 0:
            return c
    return n


def _row_tile(n):
    return _pick(n, (352, 256, 176, 128, 64, 32, 16, 8))


def _params(*sem):
    return pltpu.CompilerParams(dimension_semantics=sem, vmem_limit_bytes=VMEM_LIMIT)


def _sigmoid(x):
    return 0.5 * jnp.tanh(0.5 * x) + 0.5


def _softplus(x):
    return jnp.maximum(x, 0.0) + jnp.log(1.0 + jnp.exp(-jnp.abs(x)))


def _dsilu(x, s):
    return s * (1.0 + x * (1.0 - s))


def _mm(a, b, mode, out_dtype, name, init=None, exchange=None):
    if mode == "tn":
        k_dim, m_dim = a.shape
    else:
        m_dim, k_dim = a.shape
    n_dim = b.shape[0] if mode == "nt" else b.shape[1]
    tn = _pick(n_dim, (1408, 1280, 1024, 768, 512, 256, 128))
    if mode == "tn":
        tm = _pick(m_dim, (1408, 1280, 1024, 512, 256, 128))
        tk = _pick(k_dim, (2112, 1408, 1280, 1056, 1024, 512, 256, 128))
    else:
        tk = k_dim
        blocks = lambda rows: 2 * (2 * rows * tk + 2 * tk * tn + 4 * rows * tn * (1 if init is None else 2))
        tm = next((t for t in (2112, 1056, 1024, 704, 512, 256, 128) if m_dim % t == 0 and blocks(t) <= MM_VMEM_BUDGET), m_dim)
    nk = k_dim // tk
    if mode == "nn":
        a_spec = pl.BlockSpec((tm, tk), lambda i, j, k: (i, k))
        b_spec = pl.BlockSpec((tk, tn), lambda i, j, k: (k, j))
        dims = (((1,), (0,)), ((), ()))
    elif mode == "nt":
        a_spec = pl.BlockSpec((tm, tk), lambda i, j, k: (i, k))
        b_spec = pl.BlockSpec((tn, tk), lambda i, j, k: (j, k))
        dims = (((1,), (1,)), ((), ()))
    else:
        a_spec = pl.BlockSpec((tk, tm), lambda i, j, k: (k, i))
        b_spec = pl.BlockSpec((tk, tn), lambda i, j, k: (k, j))
        dims = (((0,), (0,)), ((), ()))

    out_spec = pl.BlockSpec((tm, tn), lambda i, j, k: (i, j))
    grid = (m_dim // tm, n_dim // tn, nk)
    parts = () if exchange is None else tuple(exchange)
    count = len(parts)
    first_in = 2 if init is None else 3

    assert out_dtype == F32

    def body(a_ref, b_ref, *rest):
        o_ref = rest[first_in - 2 + count]
        k = pl.program_id(2)
        step = (pl.program_id(0) * grid[1] + pl.program_id(1)) * nk + k
        if count:
            copies = _chip_copies(rest[first_in - 2:first_in - 2 + count], rest[first_in - 1 + count:first_in - 1 + 2 * count],
                                  *rest[first_in - 1 + 2 * count:])

            @pl.when(step == 0)
            def _():
                for cp in copies:
                    cp.start()

        p = lax.dot_general(a_ref[...], b_ref[...], dims, preferred_element_type=F32)
        if nk == 1:
            o_ref[...] = p if init is None else rest[0][...] + p
        else:
            @pl.when(k == 0)
            def _():
                o_ref[...] = p if init is None else rest[0][...] + p

            @pl.when(k > 0)
            def _():
                o_ref[...] += p

        if count:
            @pl.when(step == grid[0] * grid[1] * nk - 1)
            def _():
                for cp in copies:
                    cp.wait_recv()
                for cp in copies:
                    cp.wait_send()

    out = pl.pallas_call(
        body, name=name,
        out_shape=(jax.ShapeDtypeStruct((m_dim, n_dim), out_dtype),)
        + tuple(jax.ShapeDtypeStruct((3,) + p.shape[1:], p.dtype) for p in parts),
        grid=grid,
        in_specs=[a_spec, b_spec] + ([] if init is None else [out_spec]) + [_hbm()] * count,
        out_specs=(out_spec,) + (_hbm(),) * count,
        scratch_shapes=[pltpu.SemaphoreType.DMA((3 * count,)), pltpu.SemaphoreType.DMA((3 * count,))] if count else [],
        compiler_params=_params(*(("arbitrary",) * 3 if count else ("parallel", "parallel", "arbitrary"))),
    )(a, b, *(() if init is None else (init,)), *parts)
    return out[0] if not count else out


def _rms_apply(x, w):
    r = lax.rsqrt(jnp.mean(x * x, axis=-1, keepdims=True) + EPS)
    return x * r * w


def _rms_bwd(x, w, dy):
    r = lax.rsqrt(jnp.mean(x * x, axis=-1, keepdims=True) + EPS)
    xh = x * r
    dyw = dy * w
    dx = r * (dyw - xh * jnp.mean(dyw * xh, axis=-1, keepdims=True))
    return dx, jnp.sum(dy * xh, axis=0, keepdims=True)


def _accumulate(ref, first, value):
    @pl.when(first)
    def _():
        ref[...] = value

    @pl.when(jnp.logical_not(first))
    def _():
        ref[...] += value


def _rows(tr, width):
    return pl.BlockSpec((tr, width), lambda i: (i, 0))


def _vec(width):
    return pl.BlockSpec((1, width), lambda i: (0, 0))


def _embed(x, head, w_pre, w_shard, rows_per_seq):
    batch, seq, d = x.shape
    x_offset = head.shape[0]
    tr = _row_tile(rows_per_seq)
    tiles_per_seq = rows_per_seq // tr
    n = batch * rows_per_seq

    def body(x_ref, head_ref, w_ref, ws_ref, h0_ref, u_ref, wall_ref, send_sems, recv_sems):
        gather = _gather_copies([ws_ref], [wall_ref], send_sems, recv_sems)
        i = pl.program_id(0)
        tile = lax.rem(i, tiles_per_seq)

        @pl.when(i == 0)
        def _():
            for cp in gather[0]:
                cp.start()

        rows = jnp.concatenate([head_ref[...], x_ref[0:tr - x_offset, :]], axis=0)
        if tiles_per_seq > 1:
            start = pl.multiple_of(jnp.maximum(tile * tr - x_offset, 0), SUBLANES)
            rows = jnp.where(tile == 0, rows, x_ref[pl.ds(start, tr), :])
        h0_ref[...] = rows
        u_ref[...] = _rms_apply(rows, w_ref[...]).astype(u_ref.dtype)

        @pl.when(i == n // tr - 1)
        def _():
            _gather_finish(gather)

    return pl.pallas_call(
        body, name="embed",
        out_shape=(jax.ShapeDtypeStruct((n, d), F32), jax.ShapeDtypeStruct((n, d), MXU_DTYPE),
                   jax.ShapeDtypeStruct((N_CHIPS,) + w_shard.shape, w_shard.dtype)),
        grid=(n // tr,),
        in_specs=[pl.BlockSpec((None, seq, d), lambda i: (i // tiles_per_seq, 0, 0)),
                  pl.BlockSpec((x_offset, d), lambda i: (0, 0)), _vec(d), _hbm()],
        out_specs=(_rows(tr, d), _rows(tr, d), _hbm()),
        scratch_shapes=[pltpu.SemaphoreType.DMA((GATHER_SEMS,)), pltpu.SemaphoreType.DMA((GATHER_SEMS,))],
        compiler_params=_params("arbitrary"),
    )(x, head, w_pre, w_shard)


def _mix_residual(cat, w_out, h0, w_post, w_pre):
    n, d = h0.shape
    tr = _pick(n, (1056, 1024, 704, 512, 256, 128))

    def body(cat_ref, w_ref, h0_ref, wpost_ref, wpre_ref, mix_ref, h1_ref, u2_ref):
        mix = jnp.dot(cat_ref[...], w_ref[...], preferred_element_type=F32)
        mix_ref[...] = mix
        h1 = h0_ref[...] + _rms_apply(mix, wpost_ref[...])
        h1_ref[...] = h1
        u2_ref[...] = _rms_apply(h1, wpre_ref[...]).astype(u2_ref.dtype)

    wide = jax.ShapeDtypeStruct((n, d), F32)
    return pl.pallas_call(
        body, name="mix_residual", out_shape=(wide, wide, jax.ShapeDtypeStruct((n, d), MXU_DTYPE)), grid=(n // tr,),
        in_specs=[_rows(tr, cat.shape[1]), pl.BlockSpec(w_out.shape, lambda i: (0, 0)), _rows(tr, d), _vec(d), _vec(d)],
        out_specs=(_rows(tr, d), _rows(tr, d), _rows(tr, d)), compiler_params=_params("parallel"),
    )(cat, w_out, h0, w_post, w_pre)


NT_DIMS = (((1,), (1,)), ((), ()))


def _ffn_tiles(n):
    return _pick(n, (1056, 704, 512, 256, 128)), _pick(D_FF, (1408, 256, 128))


def _swiglu_fwd(u, w_gate_t, w_up_t, w_next):
    n, d = u.shape
    tm, tn = _ffn_tiles(n)
    grid = (D_FF // tn, n // tm)

    def body(u_ref, wg_ref, wu_ref, wn_ref, g_ref, up_ref, act_ref, wall_ref, send_sems, recv_sems):
        gather = _gather_copies([wn_ref], [wall_ref], send_sems, recv_sems)
        step = pl.program_id(0) * grid[1] + pl.program_id(1)

        @pl.when(step == 0)
        def _():
            for cp in gather[0]:
                cp.start()

        a = u_ref[...]
        g = lax.dot_general(a, wg_ref[...], NT_DIMS, preferred_element_type=F32)
        up = lax.dot_general(a, wu_ref[...], NT_DIMS, preferred_element_type=F32)
        g_ref[...] = g.astype(g_ref.dtype)
        up_ref[...] = up.astype(up_ref.dtype)
        act_ref[...] = (g * _sigmoid(g) * up).astype(act_ref.dtype)

        @pl.when(step == grid[0] * grid[1] - 1)
        def _():
            _gather_finish(gather)

    tile = pl.BlockSpec((tm, tn), lambda j, i: (i, j))
    weight = pl.BlockSpec((tn, d), lambda j, i: (j, 0))
    wide = jax.ShapeDtypeStruct((n, D_FF), MXU_DTYPE)
    return pl.pallas_call(
        body, name="swiglu_fwd",
        out_shape=(wide, wide, jax.ShapeDtypeStruct((n, D_FF), MXU_DTYPE),
                   jax.ShapeDtypeStruct((N_CHIPS,) + w_next.shape, w_next.dtype)),
        grid=grid,
        in_specs=[pl.BlockSpec((tm, d), lambda j, i: (i, 0)), weight, weight, _hbm()], out_specs=(tile, tile, tile, _hbm()),
        scratch_shapes=[pltpu.SemaphoreType.DMA((GATHER_SEMS,)), pltpu.SemaphoreType.DMA((GATHER_SEMS,))],
        compiler_params=_params("arbitrary", "arbitrary"),
    )(u, w_gate_t, w_up_t, w_next)


def _swiglu_bwd(dffn, w_down, gate, up):
    n, d = dffn.shape
    tm, tn = _ffn_tiles(n)

    def body(dy_ref, w_ref, g_ref, u_ref, dg_ref, du_ref):
        da = lax.dot_general(dy_ref[...], w_ref[...], NT_DIMS, preferred_element_type=F32)
        g = g_ref[...].astype(F32)
        s = _sigmoid(g)
        dg_ref[...] = (da * u_ref[...].astype(F32) * _dsilu(g, s)).astype(dg_ref.dtype)
        du_ref[...] = (da * g * s).astype(du_ref.dtype)

    tile = pl.BlockSpec((tm, tn), lambda j, i: (i, j))
    shape = jax.ShapeDtypeStruct((n, D_FF), MXU_DTYPE)
    return pl.pallas_call(
        body, name="swiglu_bwd", out_shape=(shape, shape), grid=(D_FF // tn, n // tm),
        in_specs=[pl.BlockSpec((tm, d), lambda j, i: (i, 0)), pl.BlockSpec((tn, d), lambda j, i: (j, 0)), tile, tile],
        out_specs=(tile, tile), compiler_params=_params("parallel", "parallel"),
    )(dffn, w_down, gate, up)


def _loss_head(h1, ffn, w_post, target, rows_per_seq, x_offset):
    n, d = h1.shape
    tr = _row_tile(rows_per_seq)
    tiles_per_seq = rows_per_seq // tr
    seq = target.shape[1]

    def seq_rows(t_ref, tile):
        first = jnp.concatenate([jnp.zeros((x_offset, d), F32), t_ref[0:tr - x_offset, :]], axis=0)
        if tiles_per_seq == 1:
            return first
        start = pl.multiple_of(jnp.maximum(tile * tr - x_offset, 0), SUBLANES)
        return jnp.where(tile == 0, first, t_ref[pl.ds(start, tr), :])

    def body(h1_ref, ffn_ref, w_ref, t_ref, dh2_ref, dffn_ref, dw_ref, sq_ref):
        i = pl.program_id(0)
        tile = lax.rem(i, tiles_per_seq)
        w = w_ref[...]
        f = ffn_ref[...]
        r = lax.rsqrt(jnp.mean(f * f, axis=-1, keepdims=True) + EPS)
        fh = f * r
        row = tile * tr + lax.broadcasted_iota(jnp.int32, (tr, 1), 0)
        err = jnp.where(row >= x_offset, h1_ref[...] + fh * w - seq_rows(t_ref, tile), 0.0)
        dh2 = err * (1.0 / d)
        dh2_ref[...] = dh2
        dyw = dh2 * w
        dffn_ref[...] = (r * (dyw - fh * jnp.mean(dyw * fh, axis=-1, keepdims=True))).astype(dffn_ref.dtype)
        _accumulate(dw_ref, i == 0, jnp.sum(dh2 * fh, axis=0, keepdims=True))
        _accumulate(sq_ref, i == 0, jnp.sum(jnp.sum(err * err, axis=1, keepdims=True), axis=0, keepdims=True))

    return pl.pallas_call(
        body, name="loss_head",
        out_shape=(jax.ShapeDtypeStruct((n, d), F32), jax.ShapeDtypeStruct((n, d), MXU_DTYPE),
                   jax.ShapeDtypeStruct((1, d), F32), jax.ShapeDtypeStruct((1, 1), F32)),
        grid=(n // tr,),
        in_specs=[_rows(tr, d), _rows(tr, d), _vec(d), pl.BlockSpec((None, seq, d), lambda i: (i // tiles_per_seq, 0, 0))],
        out_specs=(_rows(tr, d), _rows(tr, d), _vec(d), _vec(1)),
        compiler_params=_params("arbitrary"),
    )(h1, ffn, w_post, target)


def _mid_bwd(h1, mix, w_mix_post, w_ffn_pre, dh2, du2, grads):
    n, d = h1.shape
    tr = _row_tile(n)
    count = len(grads)

    def body(h1_ref, mix_ref, wpost_ref, wpre_ref, dh2_ref, du2_ref, *rest):
        g_refs, (dh1_ref, dmix_ref, dwpre_ref, dwpost_ref), got_refs = rest[:count], rest[count:count + 4], rest[count + 4:2 * count + 4]
        exchange = _sibling_copies(g_refs, got_refs, *rest[2 * count + 4:])
        i = pl.program_id(0)

        @pl.when(i == 0)
        def _():
            for cp in exchange:
                cp.start()

        dx, dwpre = _rms_bwd(h1_ref[...], wpre_ref[...], du2_ref[...])
        dh1 = dh2_ref[...] + dx
        dh1_ref[...] = dh1
        dmix, dwpost = _rms_bwd(mix_ref[...], wpost_ref[...], dh1)
        dmix_ref[...] = dmix.astype(dmix_ref.dtype)
        _accumulate(dwpre_ref, i == 0, dwpre)
        _accumulate(dwpost_ref, i == 0, dwpost)

        @pl.when(i == n // tr - 1)
        def _():
            for cp in exchange:
                cp.wait_recv()
            for cp in exchange:
                cp.wait_send()

    dh1, dmix, dwpre, dwpost, *got = pl.pallas_call(
        body, name="mid_bwd",
        out_shape=(jax.ShapeDtypeStruct((n, d), F32), jax.ShapeDtypeStruct((n, d), MXU_DTYPE),
                   jax.ShapeDtypeStruct((1, d), F32), jax.ShapeDtypeStruct((1, d), F32))
        + tuple(jax.ShapeDtypeStruct((g.shape[0],) + g.shape[2:], F32) for g in grads),
        grid=(n // tr,),
        in_specs=[_rows(tr, d), _rows(tr, d), _vec(d), _vec(d), _rows(tr, d), _rows(tr, d)] + [_hbm()] * count,
        out_specs=(_rows(tr, d), _rows(tr, d), _vec(d), _vec(d)) + (_hbm(),) * count,
        scratch_shapes=[pltpu.SemaphoreType.DMA((count,)), pltpu.SemaphoreType.DMA((count,))],
        compiler_params=_params("arbitrary"),
    )(h1, mix, w_mix_post, w_ffn_pre, dh2, du2, *grads)
    return dh1, dmix, dwpre, dwpost, got


def _in_bwd(h0, w_pre, dh1, du1, rows_per_seq, pad_rows, x_offset):
    n, d = h0.shape
    tr = _row_tile(rows_per_seq)
    tiles_per_seq = rows_per_seq // tr
    seq = rows_per_seq - x_offset

    def body(h0_ref, w_ref, dh1_ref, du1_ref, gx_ref, dmeta_ref, dw_ref):
        i = pl.program_id(0)
        tile = lax.rem(i, tiles_per_seq)
        dx, dw = _rms_bwd(h0_ref[...], w_ref[...], du1_ref[...])
        dh0 = dh1_ref[...] + dx
        _accumulate(dw_ref, i == 0, dw)

        @pl.when(tile == 0)
        def _():
            gx_ref[0:tr - x_offset, :] = dh0[x_offset:, :]
            _accumulate(dmeta_ref, i == 0, dh0[pad_rows:x_offset, :])

        if tiles_per_seq > 1:
            @pl.when(tile > 0)
            def _():
                gx_ref[pl.ds(pl.multiple_of(tile * tr - x_offset, SUBLANES), tr), :] = dh0

    return pl.pallas_call(
        body, name="in_bwd",
        out_shape=(jax.ShapeDtypeStruct((n // rows_per_seq, seq, d), F32), jax.ShapeDtypeStruct((x_offset - pad_rows, d), F32),
                   jax.ShapeDtypeStruct((1, d), F32)),
        grid=(n // tr,),
        in_specs=[_rows(tr, d), _vec(d), _rows(tr, d), _rows(tr, d)],
        out_specs=(pl.BlockSpec((None, seq, d), lambda i: (i // tiles_per_seq, 0, 0)),
                   pl.BlockSpec((x_offset - pad_rows, d), lambda i: (0, 0)), _vec(d)),
        compiler_params=_params("arbitrary"),
    )(h0, w_pre, dh1, du1)


def _lane_is(lo, hi):
    lane = lax.broadcasted_iota(jnp.int32, (1, LANES), 1)
    return jnp.logical_and(lane >= lo, lane < hi)


def _gates_fwd(proj, a_log_l, dt_bias_l, rows_per_seq, pad_rows):
    n = proj.shape[0]
    tr = _row_tile(rows_per_seq)
    tiles_per_seq = rows_per_seq // tr

    def body(p_ref, a_ref, dt_ref, o_ref):
        x = p_ref[...]
        row = lax.rem(pl.program_id(0), tiles_per_seq) * tr + lax.broadcasted_iota(jnp.int32, (tr, 1), 0)
        g = -jnp.exp(a_ref[...]) * _softplus(x + dt_ref[...])
        val = jnp.where(_lane_is(0, HEADS), _sigmoid(x), jnp.where(_lane_is(HEADS, 2 * HEADS), g, 0.0))
        o_ref[...] = jnp.where(row >= pad_rows, val, 0.0)

    return pl.pallas_call(
        body, name="gates_fwd", out_shape=jax.ShapeDtypeStruct((n, LANES), F32), grid=(n // tr,),
        in_specs=[pl.BlockSpec((tr, LANES), lambda i: (i, BA_COL)), _vec(LANES), _vec(LANES)],
        out_specs=_rows(tr, LANES), compiler_params=_params("parallel"),
    )(proj, a_log_l, dt_bias_l)


def _gates_bwd(proj, dbg, a_log_l, dt_bias_l, rows_per_seq, pad_rows, dproj):
    n = proj.shape[0]
    tr = _row_tile(rows_per_seq)
    tiles_per_seq = rows_per_seq // tr

    def body(p_ref, d_ref, a_ref, dt_ref, _, dx_ref, da_ref, ddt_ref):
        i = pl.program_id(0)
        x = p_ref[...]
        d = d_ref[...]
        row = lax.rem(i, tiles_per_seq) * tr + lax.broadcasted_iota(jnp.int32, (tr, 1), 0)
        live = row >= pad_rows
        beta = _sigmoid(x)
        ea = jnp.exp(a_ref[...])
        xa = x + dt_ref[...]
        g = -ea * _softplus(xa)
        is_g = _lane_is(HEADS, 2 * HEADS)
        d_alogit = jnp.where(jnp.logical_and(live, is_g), d * (-ea) * _sigmoid(xa), 0.0)
        d_blogit = jnp.where(jnp.logical_and(live, _lane_is(0, HEADS)), d * beta * (1.0 - beta), 0.0)
        dx_ref[:, :LANES] = (d_alogit + d_blogit).astype(dx_ref.dtype)
        dx_ref[:, LANES:] = jnp.zeros((tr, LANES), dx_ref.dtype)
        _accumulate(da_ref, i == 0, jnp.sum(jnp.where(jnp.logical_and(live, is_g), d * g, 0.0), axis=0, keepdims=True))
        _accumulate(ddt_ref, i == 0, jnp.sum(d_alogit, axis=0, keepdims=True))

    return pl.pallas_call(
        body, name="gates_bwd",
        out_shape=(jax.ShapeDtypeStruct(dproj.shape, dproj.dtype), jax.ShapeDtypeStruct((1, LANES), F32),
                   jax.ShapeDtypeStruct((1, LANES), F32)),
        grid=(n // tr,),
        in_specs=[pl.BlockSpec((tr, LANES), lambda i: (i, BA_COL)), _rows(tr, LANES), _vec(LANES), _vec(LANES), _hbm()],
        out_specs=(pl.BlockSpec((tr, 2 * LANES), lambda i: (i, BA_COL // 2)), _vec(LANES), _vec(LANES)),
        input_output_aliases={4: 0},
        compiler_params=_params("arbitrary"),
    )(proj, dbg, a_log_l, dt_bias_l, dproj)


HALO = 8


def _halo_scratch(rs):
    return pltpu.VMEM((rs + 2 * HALO, LANES), F32)


def _stage(ref, x):
    rs = x.shape[0]
    ref[0:HALO, :] = jnp.zeros((HALO, LANES), F32)
    ref[HALO + rs:, :] = jnp.zeros((HALO, LANES), F32)
    ref[HALO:HALO + rs, :] = x


def _shifted(ref, k, rs):
    return ref[pl.ds(HALO - k, rs), :]


def _causal_conv(x, x_staged, w, width):
    acc = w[width - 1:width, :] * x
    for i in range(width - 1):
        acc = acc + w[i:i + 1, :] * _shifted(x_staged, width - 1 - i, x.shape[0])
    return acc


def _anti_causal_conv(dy, dy_staged, w, width):
    acc = w[width - 1:width, :] * dy
    for i in range(width - 1):
        acc = acc + w[i:i + 1, :] * _shifted(dy_staged, -(width - 1 - i), dy.shape[0])
    return acc


def _conv_weight_grad(dy, x, x_staged, width):
    taps = [_shifted(x_staged, width - 1 - i, x.shape[0]) for i in range(width - 1)] + [x]
    return jnp.concatenate([jnp.sum(dy * tap, axis=0, keepdims=True) for tap in taps], axis=0)


def _seq_cols(rs, col0, heads):
    return pl.BlockSpec((rs, heads * LANES), lambda j, b: (b, col0 // heads + j))


def _tap_cols(width, col0, heads):
    return pl.BlockSpec((width, heads * LANES), lambda j, b: (0, col0 // heads + j))


def _lanes_of(h):
    return slice(h * LANES, (h + 1) * LANES)


def _qkv_fwd(proj, conv_w, kind, rs):
    n = proj.shape[0]
    col0 = {"q": 0, "k": HEADS, "v": 2 * HEADS}[kind]
    hb = HEADS

    def body(p_ref, w_ref, o_ref, staged):
        for h in range(hb):
            pre = p_ref[:, _lanes_of(h)]
            _stage(staged, pre)
            c = _causal_conv(pre, staged, w_ref[:, _lanes_of(h)], GDN_CONV)
            s = c * _sigmoid(c)
            if kind != "v":
                s = s * lax.rsqrt(jnp.sum(s * s, axis=-1, keepdims=True) + EPS)
            if kind == "q":
                s = s * (HEAD_DIM ** -0.5)
            o_ref[:, _lanes_of(h)] = s

    return pl.pallas_call(
        body, name="qkv_fwd_" + kind, out_shape=jax.ShapeDtypeStruct((n, GDN_WIDTH), F32), grid=(HEADS // hb, n // rs),
        in_specs=[_seq_cols(rs, col0, hb), _tap_cols(GDN_CONV, col0, hb)],
        out_specs=_seq_cols(rs, 0, hb), scratch_shapes=[_halo_scratch(rs)], compiler_params=_params("parallel", "parallel"),
    )(proj, conv_w)


def _qkv_bwd(dy, proj, conv_w, kind, rs, dproj):
    n = proj.shape[0]
    col0 = {"q": 0, "k": HEADS, "v": 2 * HEADS}[kind]
    hb = HEADS

    def body(dy_ref, p_ref, w_ref, _, dp_ref, dw_ref, pre_staged, dc_staged):
        for h in range(hb):
            lanes = _lanes_of(h)
            pre = p_ref[:, lanes]
            w = w_ref[:, lanes]
            _stage(pre_staged, pre)
            c = _causal_conv(pre, pre_staged, w, GDN_CONV)
            sg = _sigmoid(c)
            s = c * sg
            ds = dy_ref[:, lanes]
            if kind == "q":
                ds = ds * (HEAD_DIM ** -0.5)
            if kind != "v":
                r = lax.rsqrt(jnp.sum(s * s, axis=-1, keepdims=True) + EPS)
                sh = s * r
                ds = r * (ds - sh * jnp.sum(ds * sh, axis=-1, keepdims=True))
            dc = ds * _dsilu(c, sg)
            _stage(dc_staged, dc)
            dp_ref[:, lanes] = _anti_causal_conv(dc, dc_staged, w, GDN_CONV).astype(dp_ref.dtype)
            _accumulate(dw_ref.at[:, lanes], pl.program_id(1) == 0, _conv_weight_grad(dc, pre, pre_staged, GDN_CONV))

    return pl.pallas_call(
        body, name="qkv_bwd_" + kind,
        out_shape=(jax.ShapeDtypeStruct(dproj.shape, dproj.dtype), jax.ShapeDtypeStruct((GDN_CONV, GDN_WIDTH), F32)),
        grid=(HEADS // hb, n // rs),
        in_specs=[_seq_cols(rs, 0, hb), _seq_cols(rs, col0, hb), _tap_cols(GDN_CONV, col0, hb), _hbm()],
        out_specs=(_seq_cols(rs, col0, hb), _tap_cols(GDN_CONV, 0, hb)), input_output_aliases={3: 0},
        scratch_shapes=[_halo_scratch(rs), _halo_scratch(rs)],
        compiler_params=_params("parallel", "arbitrary"),
    )(dy, proj, conv_w, dproj)


SC_COL = 4 * HEADS


def _sc_fwd(proj, conv_w, rs, cat):
    n = proj.shape[0]

    hb = 2

    def body(x_ref, b_ref, c_ref, w_ref, _, y_ref, staged):
        for h in range(hb):
            lanes = _lanes_of(h)
            u = c_ref[:, lanes] * x_ref[:, lanes]
            _stage(staged, u)
            y_ref[:, lanes] = (b_ref[:, lanes] * _causal_conv(u, staged, w_ref[:, lanes], SC_CONV)).astype(y_ref.dtype)

    return pl.pallas_call(
        body, name="sc_fwd", out_shape=jax.ShapeDtypeStruct(cat.shape, cat.dtype), grid=(HEADS // hb, n // rs),
        in_specs=[_seq_cols(rs, SC_COL, hb), _seq_cols(rs, SC_COL + 4, hb), _seq_cols(rs, SC_COL + 8, hb),
                  _tap_cols(SC_CONV, 0, hb), _hbm()],
        out_specs=_seq_cols(rs, HEADS, hb), input_output_aliases={4: 0}, scratch_shapes=[_halo_scratch(rs)],
        compiler_params=_params("parallel", "parallel"),
    )(proj, proj, proj, conv_w, cat)


def _sc_bwd(dcat, proj, conv_w, rs, dproj):
    n = proj.shape[0]
    hb = 2

    def body(dy_ref, x_ref, b_ref, c_ref, w_ref, _, dx_ref, db_ref, dc_ref, dw_ref, u_staged, dcv_staged):
        for h in range(hb):
            lanes = _lanes_of(h)
            w = w_ref[:, lanes]
            x = x_ref[:, lanes]
            cc = c_ref[:, lanes]
            u = cc * x
            _stage(u_staged, u)
            dy = dy_ref[:, lanes]
            db_ref[:, lanes] = (dy * _causal_conv(u, u_staged, w, SC_CONV)).astype(db_ref.dtype)
            dcv = dy * b_ref[:, lanes]
            _stage(dcv_staged, dcv)
            du = _anti_causal_conv(dcv, dcv_staged, w, SC_CONV)
            dx_ref[:, lanes] = (du * cc).astype(dx_ref.dtype)
            dc_ref[:, lanes] = (du * x).astype(dc_ref.dtype)
            _accumulate(dw_ref.at[:, lanes], pl.program_id(1) == 0, _conv_weight_grad(dcv, u, u_staged, SC_CONV))

    piece = jax.ShapeDtypeStruct((n, SC_WIDTH), MXU_DTYPE)
    return pl.pallas_call(
        body, name="sc_bwd",
        out_shape=(jax.ShapeDtypeStruct(dproj.shape, dproj.dtype), piece, piece, jax.ShapeDtypeStruct((SC_CONV, SC_WIDTH), F32)),
        grid=(HEADS // hb, n // rs),
        in_specs=[_seq_cols(rs, HEADS, hb), _seq_cols(rs, SC_COL, hb), _seq_cols(rs, SC_COL + 4, hb),
                  _seq_cols(rs, SC_COL + 8, hb), _tap_cols(SC_CONV, 0, hb), _hbm()],
        out_specs=(_seq_cols(rs, SC_COL, hb), _seq_cols(rs, 0, hb), _seq_cols(rs, 0, hb), _tap_cols(SC_CONV, 0, hb)),
        input_output_aliases={5: 0},
        scratch_shapes=[_halo_scratch(rs), _halo_scratch(rs)],
        compiler_params=_params("parallel", "arbitrary"),
    )(dcat, proj, proj, proj, conv_w, dproj)


Z_COL = 3 * HEADS


def _gate_fwd(o, proj, gdn_norm, rs):
    n = proj.shape[0]

    hb = HEADS

    def body(o_ref, z_ref, w_ref, y_ref):
        for h in range(hb):
            lanes = _lanes_of(h)
            z = z_ref[:, lanes]
            y_ref[:, lanes] = (_rms_apply(o_ref[:, lanes], w_ref[...]) * z * _sigmoid(z)).astype(y_ref.dtype)

    return pl.pallas_call(
        body, name="gate_fwd", out_shape=jax.ShapeDtypeStruct((n, D_MODEL), MXU_DTYPE), grid=(HEADS // hb, n // rs),
        in_specs=[_seq_cols(rs, 0, hb), _seq_cols(rs, Z_COL, hb), pl.BlockSpec((1, LANES), lambda j, b: (0, 0))],
        out_specs=_seq_cols(rs, 0, hb), compiler_params=_params("parallel", "parallel"),
    )(o, proj, gdn_norm)


def _gate_bwd(dcat, o, proj, gdn_norm, rs):
    n = proj.shape[0]
    hb = 2

    def body(dy_ref, o_ref, z_ref, w_ref, do_ref, dz_ref, dw_ref):
        w = w_ref[...]
        dw_step = jnp.zeros((1, LANES), F32)
        for h in range(hb):
            lanes = _lanes_of(h)
            z = z_ref[:, lanes]
            o = o_ref[:, lanes]
            dy = dy_ref[:, lanes]
            s = _sigmoid(z)
            dz_ref[:, lanes] = (dy * _rms_apply(o, w) * _dsilu(z, s)).astype(dz_ref.dtype)
            do, dw = _rms_bwd(o, w, dy * z * s)
            do_ref[:, lanes] = do
            dw_step = dw_step + dw
        _accumulate(dw_ref, jnp.logical_and(pl.program_id(0) == 0, pl.program_id(1) == 0), dw_step)

    return pl.pallas_call(
        body, name="gate_bwd",
        out_shape=(jax.ShapeDtypeStruct((n, GDN_WIDTH), F32), jax.ShapeDtypeStruct((n, IN_PAD), MXU_DTYPE),
                   jax.ShapeDtypeStruct((1, LANES), F32)),
        grid=(HEADS // hb, n // rs),
        in_specs=[_seq_cols(rs, 0, hb), _seq_cols(rs, 0, hb), _seq_cols(rs, Z_COL, hb), pl.BlockSpec((1, LANES), lambda j, b: (0, 0))],
        out_specs=(_seq_cols(rs, 0, hb), _seq_cols(rs, Z_COL, hb), pl.BlockSpec((1, LANES), lambda j, b: (0, 0))),
        compiler_params=_params("arbitrary", "arbitrary"),
    )(dcat, o, proj, gdn_norm)


def _dot(a, b):
    return jnp.dot(a.astype(MXU_DTYPE), b.astype(MXU_DTYPE), preferred_element_type=F32)


def _dot_nt(a, b):
    return lax.dot_general(a.astype(MXU_DTYPE), b.astype(MXU_DTYPE), (((1,), (1,)), ((), ())),
                           preferred_element_type=F32)


def _dot_tn(a, b):
    return lax.dot_general(a.astype(MXU_DTYPE), b.astype(MXU_DTYPE), (((0,), (0,)), ((), ())),
                           preferred_element_type=F32)


def _split(x):
    hi = x.astype(MXU_DTYPE)
    return hi, (x - hi.astype(F32)).astype(MXU_DTYPE)


def _dot_split(a, b):
    mm = functools.partial(jnp.dot, preferred_element_type=F32)
    return mm(a[0], b[0]) + (mm(a[0], b[1]) + mm(a[1], b[0]))


def _unit_lower_inverses(mats, eye):
    inv = [eye - a for a in mats]
    power = [_split(a) for a in mats]
    span = 2
    while span < CHUNK:
        power = [_split(_dot_split(p, p)) for p in power]
        inv = [i + _dot_split(_split(i), p) for i, p in zip(inv, power)]
        span *= 2
    return inv


def _chunk_masks():
    ii = lax.broadcasted_iota(jnp.int32, (CHUNK, CHUNK), 0)
    jj = lax.broadcasted_iota(jnp.int32, (CHUNK, CHUNK), 1)
    return ii, jj


def _chunk_decay(g_col, ii, jj):
    incl = ii >= jj
    g_row = jnp.sum(jnp.where(ii == jj, g_col, 0.0), axis=0, keepdims=True)
    gc_col = jnp.sum(jnp.where(incl, g_row, 0.0), axis=1, keepdims=True)
    gc_row = jnp.sum(jnp.where(ii <= jj, g_col, 0.0), axis=0, keepdims=True)
    g_total = jnp.sum(g_row, axis=1, keepdims=True)
    decay = jnp.where(incl, jnp.exp(jnp.where(incl, gc_col - gc_row, 0.0)), 0.0)
    return gc_col, g_total, decay


def _gdn_segments(rs, candidates):
    chunks = rs // CHUNK
    seg_chunks = _pick(chunks, candidates)
    return chunks, seg_chunks, chunks // seg_chunks


def _gdn_fwd(q, k, v, bg, rs, pieces):
    n = q.shape[0]
    batch = n // rs
    chunks, seg_chunks, segs = _gdn_segments(rs, (11, 8, 4, 2))
    seg_rows = seg_chunks * CHUNK
    chains = [(b, h) for b in range(batch) for h in range(HEADS)]
    each = lambda f, *lists: [f(*args) for args in zip(*lists)]
    count = len(pieces)

    def body(q_ref, k_ref, v_ref, bg_ref, *rest):
        w_refs, (o_ref, s_ref, t_ref), out_refs = rest[:count], rest[count:count + 3], rest[count + 3:2 * count + 3]
        state_ref, send_sems, recv_sems = rest[2 * count + 3:]
        gather = _gather_copies(w_refs, out_refs, send_sems, recv_sems)

        @pl.when(pl.program_id(0) == 0)
        def _():
            state_ref[...] = jnp.zeros_like(state_ref)
            for cp in gather[0]:
                cp.start()

        ii, jj = _chunk_masks()
        incl = ii >= jj
        eye = (ii == jj).astype(F32)

        def chunk(c, carry):
            rows = pl.ds(pl.multiple_of(c * CHUNK, CHUNK), CHUNK)
            bgc = [bg_ref[b, rows, :] for b in range(batch)]
            qc = [q_ref[b, rows, _lanes_of(h)] for b, h in chains]
            kc = [k_ref[b, rows, _lanes_of(h)] for b, h in chains]
            vc = [v_ref[b, rows, _lanes_of(h)] for b, h in chains]
            beta = [bgc[b][:, h:h + 1] for b, h in chains]
            state = [state_ref[b, h] for b, h in chains]
            dec = [_chunk_decay(bgc[b][:, HEADS + h:HEADS + h + 1], ii, jj) for b, h in chains]
            gc_col, g_total, decay = ([d[i] for d in dec] for i in range(3))
            kb = each(lambda x, y: x * y, kc, beta)
            a = each(lambda x, y, d: jnp.where(ii > jj, _dot_nt(x, y) * d, 0.0), kb, kc, decay)
            t_inv = _unit_lower_inverses(a, eye)
            eg = [jnp.exp(g) for g in gc_col]
            u = each(lambda t, x, y: _dot(t, x * y), t_inv, vc, beta)
            w = each(lambda t, x, e: _dot(t, x * e), t_inv, kb, eg)
            qk = each(lambda x, y, d: jnp.where(incl, _dot_nt(x, y) * d, 0.0), qc, kc, decay)
            v_new = each(lambda x, y, s: x - _dot(y, s), u, w, state)
            o = each(lambda x, e, s, m, vn: _dot(x * e, s) + _dot(m, vn), qc, eg, state, qk, v_new)
            new_state = each(lambda s, gt, x, g, vn: s * jnp.exp(gt) + _dot_tn(x * jnp.exp(gt - g), vn),
                             state, g_total, kc, gc_col, v_new)
            for i, (b, h) in enumerate(chains):
                s_ref[b, h, c] = state[i]
                t_ref[b, h, c] = t_inv[i]
                o_ref[b, rows, _lanes_of(h)] = o[i]
                state_ref[b, h] = new_state[i]
            return carry

        lax.fori_loop(0, seg_chunks, chunk, 0)

        @pl.when(pl.program_id(0) == segs - 1)
        def _():
            _gather_finish(gather)

    rows_spec = lambda width: pl.BlockSpec((batch, seg_rows, width), lambda s: (0, s, 0))
    per_chunk = lambda r, c: pl.BlockSpec((batch, HEADS, seg_chunks, r, c), lambda s: (0, 0, s, 0, 0))
    as_seqs = lambda a: a.reshape(batch, rs, a.shape[-1])
    sems = GATHER_SEMS * count
    o, states, t_invs, *gathered = pl.pallas_call(
        body, name="gdn_fwd",
        out_shape=(jax.ShapeDtypeStruct((batch, rs, GDN_WIDTH), F32),
                   jax.ShapeDtypeStruct((batch, HEADS, chunks, HEAD_DIM, HEAD_DIM), F32),
                   jax.ShapeDtypeStruct((batch, HEADS, chunks, CHUNK, CHUNK), F32))
        + tuple(jax.ShapeDtypeStruct((N_CHIPS,) + p.shape, p.dtype) for p in pieces),
        grid=(segs,),
        in_specs=[rows_spec(GDN_WIDTH), rows_spec(GDN_WIDTH), rows_spec(GDN_WIDTH), rows_spec(LANES)] + [_hbm()] * count,
        out_specs=(rows_spec(GDN_WIDTH), per_chunk(HEAD_DIM, HEAD_DIM), per_chunk(CHUNK, CHUNK)) + (_hbm(),) * count,
        scratch_shapes=[pltpu.VMEM((batch, HEADS, HEAD_DIM, HEAD_DIM), F32), pltpu.SemaphoreType.DMA((sems,)),
                        pltpu.SemaphoreType.DMA((sems,))],
        compiler_params=_params("arbitrary"),
    )(as_seqs(q), as_seqs(k), as_seqs(v), as_seqs(bg), *pieces)
    return o.reshape(n, GDN_WIDTH), states, t_invs, gathered


def _gdn_bwd(do, q, k, v, bg, states, t_invs, rs, parts):
    n = q.shape[0]
    batch = n // rs
    chunks, seg_chunks, segs = _gdn_segments(rs, (3, 4, 2))
    seg_rows = seg_chunks * CHUNK
    chains = [(b, h) for b in range(batch) for h in range(HEADS)]
    each = lambda f, *lists: [f(*args) for args in zip(*lists)]
    count = len(parts)

    def body(do_ref, q_ref, k_ref, v_ref, bg_ref, s_ref, t_ref, *rest):
        p_refs, (dq_ref, dk_ref, dv_ref, dbg_ref), got_refs = rest[:count], rest[count:count + 4], rest[count + 4:2 * count + 4]
        dstate_ref, send_sems, recv_sems = rest[2 * count + 4:]
        exchange = _chip_copies(p_refs, got_refs, send_sems, recv_sems)

        @pl.when(pl.program_id(0) == 0)
        def _():
            dstate_ref[...] = jnp.zeros_like(dstate_ref)
            for cp in exchange:
                cp.start()

        ii, jj = _chunk_masks()
        incl = ii >= jj
        strict = ii > jj
        lane = lax.broadcasted_iota(jnp.int32, (1, LANES), 1)

        def rowsum(x):
            return jnp.sum(x, axis=1, keepdims=True)

        def total(x):
            return jnp.sum(rowsum(x), axis=0, keepdims=True)

        def chunk(step, carry):
            c = seg_chunks - 1 - step
            rows = pl.ds(pl.multiple_of(c * CHUNK, CHUNK), CHUNK)
            bgc = [bg_ref[b, rows, :] for b in range(batch)]
            qc = [q_ref[b, rows, _lanes_of(h)] for b, h in chains]
            kc = [k_ref[b, rows, _lanes_of(h)] for b, h in chains]
            vc = [v_ref[b, rows, _lanes_of(h)] for b, h in chains]
            doc = [do_ref[b, rows, _lanes_of(h)] for b, h in chains]
            beta = [bgc[b][:, h:h + 1] for b, h in chains]
            state = [s_ref[b, h, c] for b, h in chains]
            t_inv = [t_ref[b, h, c] for b, h in chains]
            d_state = [dstate_ref[b, h] for b, h in chains]
            dec = [_chunk_decay(bgc[b][:, HEADS + h:HEADS + h + 1], ii, jj) for b, h in chains]
            gc_col, g_total, decay = ([d[i] for d in dec] for i in range(3))
            kb = each(lambda x, y: x * y, kc, beta)
            vb = each(lambda x, y: x * y, vc, beta)
            eg = [jnp.exp(g) for g in gc_col]
            kbg = each(lambda x, y: x * y, kb, eg)
            a = each(lambda x, y, d: jnp.where(strict, _dot_nt(x, y) * d, 0.0), kb, kc, decay)
            qk = each(lambda x, y, d: jnp.where(incl, _dot_nt(x, y) * d, 0.0), qc, kc, decay)
            w = each(_dot, t_inv, kbg)
            u = each(_dot, t_inv, vb)
            q_dec = each(lambda x, y: x * y, qc, eg)
            ek = each(lambda gt, g: jnp.exp(gt - g), g_total, gc_col)
            k_dec = each(lambda x, y: x * y, kc, ek)
            g_last = [jnp.exp(gt) for gt in g_total]
            v_new = each(lambda x, y, s: x - _dot(y, s), u, w, state)
            dv_new = each(lambda m, d, x, ds: _dot_tn(m, d) + _dot(x, ds), qk, doc, k_dec, d_state)
            dqk = each(lambda d, vn: jnp.where(incl, _dot_nt(d, vn), 0.0), doc, v_new)
            dq_dec = each(_dot_nt, doc, state)
            dk_dec = each(_dot_nt, v_new, d_state)
            dg_last = each(lambda s, ds: total(s * ds), state, d_state)
            new_d_state = each(lambda x, d, gl, ds, y, dvn: _dot_tn(x, d) + gl * ds - _dot_tn(y, dvn),
                               q_dec, doc, g_last, d_state, w, dv_new)
            dw = each(lambda dvn, s: -_dot_nt(dvn, s), dv_new, state)
            dt = each(lambda dvn, x, y, z: _dot_nt(dvn, x) + _dot_nt(y, z), dv_new, vb, dw, kbg)
            dvb = each(_dot_tn, t_inv, dv_new)
            dkbg = each(_dot_tn, t_inv, dw)
            t_dt = each(_dot_tn, t_inv, dt)
            da = each(lambda x, t: -jnp.where(strict, _dot_nt(x, t), 0.0), t_dt, t_inv)
            dm_a = each(lambda x, y: x * y, da, decay)
            dm_qk = each(lambda x, y: x * y, dqk, decay)
            e = each(lambda x, y, z, t: x * y + z * t, da, a, dqk, qk)
            dkb = each(lambda m, x, y, z: _dot(m, x) + y * z, dm_a, kc, dkbg, eg)
            dk = each(lambda m, x, m2, y, z, t, p, bt: _dot_tn(m, x) + _dot_tn(m2, y) + z * t + p * bt,
                      dm_a, kb, dm_qk, qc, dk_dec, ek, dkb, beta)
            dq = each(lambda m, x, y, z: _dot(m, x) + y * z, dm_qk, kc, dq_dec, eg)
            dbeta = each(lambda x, y, z, t: rowsum(x * y + z * t), dkb, kc, dvb, vc)
            dgc = each(lambda x, p, pd, r, rd, s, sd: rowsum(x) - rowsum(jnp.where(ii == jj, jnp.sum(x, axis=0, keepdims=True), 0.0))
                       + rowsum(p * pd - r * rd + s * sd), e, dq_dec, q_dec, dk_dec, k_dec, dkbg, kbg)
            d_total = each(lambda r, rd, x, gl: total(r * rd) + x * gl, dk_dec, k_dec, dg_last, g_last)
            dg = each(lambda x, t: rowsum(jnp.where(jj >= ii, jnp.sum(jnp.where(ii == jj, x, 0.0), axis=0, keepdims=True), 0.0)) + t,
                      dgc, d_total)
            dbg = [jnp.zeros((CHUNK, LANES), F32) for _ in range(batch)]
            for i, (b, h) in enumerate(chains):
                dstate_ref[b, h] = new_d_state[i]
                dk_ref[b, rows, _lanes_of(h)] = dk[i]
                dq_ref[b, rows, _lanes_of(h)] = dq[i]
                dv_ref[b, rows, _lanes_of(h)] = dvb[i] * beta[i]
                dbg[b] = dbg[b] + jnp.where(lane == h, dbeta[i], 0.0) + jnp.where(lane == HEADS + h, dg[i], 0.0)
            for b in range(batch):
                dbg_ref[b, rows, :] = dbg[b]
            return carry

        lax.fori_loop(0, seg_chunks, chunk, 0)

        @pl.when(pl.program_id(0) == segs - 1)
        def _():
            for cp in exchange:
                cp.wait_recv()
            for cp in exchange:
                cp.wait_send()

    rows_spec = lambda width: pl.BlockSpec((batch, seg_rows, width), lambda s: (0, segs - 1 - s, 0))
    per_chunk = lambda r, c: pl.BlockSpec((batch, HEADS, seg_chunks, r, c), lambda s: (0, 0, segs - 1 - s, 0, 0))
    as_seqs = lambda a: a.reshape(batch, rs, a.shape[-1])
    grad = jax.ShapeDtypeStruct((batch, rs, GDN_WIDTH), F32)
    wide = rows_spec(GDN_WIDTH)
    dq, dk, dv, dbg, *got = pl.pallas_call(
        body, name="gdn_bwd",
        out_shape=(grad, grad, grad, jax.ShapeDtypeStruct((batch, rs, LANES), F32))
        + tuple(jax.ShapeDtypeStruct((3,) + p.shape[1:], p.dtype) for p in parts),
        grid=(segs,),
        in_specs=[wide, wide, wide, wide, rows_spec(LANES), per_chunk(HEAD_DIM, HEAD_DIM), per_chunk(CHUNK, CHUNK)]
        + [_hbm()] * count,
        out_specs=(wide, wide, wide, rows_spec(LANES)) + (_hbm(),) * count,
        scratch_shapes=[pltpu.VMEM((batch, HEADS, HEAD_DIM, HEAD_DIM), F32), pltpu.SemaphoreType.DMA((3 * count,)),
                        pltpu.SemaphoreType.DMA((3 * count,))],
        compiler_params=_params("arbitrary"),
    )(as_seqs(do), as_seqs(q), as_seqs(k), as_seqs(v), as_seqs(bg), states, t_invs, *parts)
    return dq.reshape(n, GDN_WIDTH), dk.reshape(n, GDN_WIDTH), dv.reshape(n, GDN_WIDTH), dbg.reshape(n, LANES), got


def _lane_vec(vals, offset):
    k = vals.shape[1]
    return jnp.pad(vals, ((0, 0), (offset, LANES - offset - k)))


LATER = ("w_out", "w_gate", "w_up", "w_down")


def _halves(a):
    return a.reshape(a.shape[:-2] + (2, a.shape[-2] // 2, a.shape[-1]))


def _local_step(x, target, meta, norms, w_in_shard, conv_qkv, a_log, dt_bias, gdn_norm, conv_sc, later_shards, core_arg):
    batch, seq, d = x.shape
    tokens = N_META + seq
    pad_rows = (-tokens) % CHUNK
    rs = tokens + pad_rows
    x_offset = pad_rows + N_META
    n = batch * rs
    w_mix_pre, w_mix_post, w_ffn_pre, w_ffn_post = norms

    head = jnp.concatenate([jnp.zeros((pad_rows, d), F32), meta], axis=0)
    a_log_l = _lane_vec(a_log, HEADS)
    dt_bias_l = _lane_vec(dt_bias, HEADS)

    h0, u1, w_in_all = _embed(x, head, w_mix_pre, w_in_shard, rs)
    w_in_t = _in_to_kernel_order(w_in_all.reshape(N_CHIPS, -1, d))
    proj = _mm(u1, w_in_t, "nt", F32, "mm_proj")
    q = _qkv_fwd(proj, conv_qkv, "q", rs)
    k = _qkv_fwd(proj, conv_qkv, "k", rs)
    v = _qkv_fwd(proj, conv_qkv, "v", rs)
    bg = _gates_fwd(proj, a_log_l, dt_bias_l, rs, pad_rows)
    o, states, t_invs, gathered = _gdn_fwd(q, k, v, bg, rs, later_shards[:3])
    w_out, w_gate_t, w_up_t = (a.reshape(-1, d) for a in gathered)
    cat = _sc_fwd(proj, conv_sc, rs, _gate_fwd(o, proj, gdn_norm, rs))
    mix, h1, u2 = _mix_residual(cat, w_out, h0, w_mix_post, w_ffn_pre)
    gate, up, act, w_down = _swiglu_fwd(u2, w_gate_t, w_up_t, later_shards[3])
    w_down = w_down.reshape(-1, d)
    ffn = _mm(act, w_down, "nn", F32, "mm_down")

    dh2, dffn, d_ffn_post, sq = _loss_head(h1, ffn, w_ffn_post, target, rs, x_offset)
    d_w_down = _mm(act, dffn, "tn", F32, "mm_dw_down")
    dgate, dup = _swiglu_bwd(dffn, w_down, gate, up)
    d_w_gate_t = _mm(dgate, u2, "tn", F32, "mm_dw_gate")
    d_w_up_t = _mm(dup, u2, "tn", F32, "mm_dw_up")
    du2 = _mm(dup, w_up_t, "nn", F32, "mm_du2_up", init=_mm(dgate, w_gate_t, "nn", F32, "mm_du2_gate"))
    by_chip = [_halves(g.reshape(N_CHIPS, -1, d)) for g in (d_w_gate_t, d_w_up_t, d_w_down)]
    dh1, dmix, d_ffn_pre, d_mix_post, got_sibling = _mid_bwd(h1, mix, w_mix_post, w_ffn_pre, dh2, du2, by_chip)
    dcat = _mm(dmix, w_out, "nt", F32, "mm_dcat")
    d_w_out = _halves(_mm(cat, dmix, "tn", F32, "mm_dw_out").reshape(N_CHIPS, -1, d))
    sums = (_add_sibling([d_w_out], _exchange_siblings([d_w_out]), core_arg, "w_out")
            + _add_sibling(by_chip, got_sibling, core_arg, "ffn"))
    do, dproj, d_gdn_norm = _gate_bwd(dcat, o, proj, gdn_norm, rs)
    dproj, dscb, dscc, d_conv_sc = _sc_bwd(dcat, proj, conv_sc, rs, dproj)
    dq, dk, dv, dbg, got_chips = _gdn_bwd(do, q, k, v, bg, states, t_invs, rs, [send for _, send in sums[:3]])
    dproj, dwq = _qkv_bwd(dq, proj, conv_qkv, "q", rs, dproj)
    dproj, dwk = _qkv_bwd(dk, proj, conv_qkv, "k", rs, dproj)
    dproj, dwv = _qkv_bwd(dv, proj, conv_qkv, "v", rs, dproj)
    d_conv_qkv = jnp.concatenate([dwq, dwk, dwv], axis=1)
    dproj, d_a_log_l, d_dt_bias_l = _gates_bwd(proj, dbg, a_log_l, dt_bias_l, rs, pad_rows, dproj)
    dproj = lax.dynamic_update_slice(dproj, dscb, (0, (SC_COL + HEADS) * LANES))
    dproj = lax.dynamic_update_slice(dproj, dscc, (0, (SC_COL + 2 * HEADS) * LANES))
    d_w_in_t, got_down = _mm(dproj, u1, "tn", F32, "mm_dw_in", exchange=[sums[3][1]])
    got_chips.append(got_down)
    g_in = _halves(_in_from_kernel_order(d_w_in_t))
    sums = _add_sibling([g_in], _exchange_siblings([g_in]), core_arg, "w_in") + sums
    du1, got_in = _mm(dproj, w_in_t, "nn", F32, "mm_du1", exchange=[sums[0][1]])
    got_chips.insert(0, got_in)
    grad_x, d_meta, d_mix_pre = _in_bwd(h0, w_mix_pre, dh1, du1, rs, pad_rows, x_offset)

    grads = dict(
        meta_tokens=d_meta,
        mix_pre_norm=d_mix_pre, mix_post_norm=d_mix_post, ffn_pre_norm=d_ffn_pre, ffn_post_norm=d_ffn_post,
        conv_qkv=d_conv_qkv,
        a_log=d_a_log_l[:, HEADS:2 * HEADS], dt_bias=d_dt_bias_l[:, HEADS:2 * HEADS],
        gdn_norm=d_gdn_norm, conv_sc=d_conv_sc,
    )
    return sq, grad_x, grads, [(part, got) for (part, _), got in zip(sums, got_chips)]


MATRICES = ("w_in", "w_out", "w_gate", "w_up", "w_down")
IN_SHARD = IN_WIDTH // N_CHIPS
IN_SHARD_PAD = 928


IN_SEGMENTS = ((0, 0, 4 * GDN_WIDTH), (4 * GDN_WIDTH, IN_WIDTH - 2 * HEADS, 2 * HEADS),
               (4 * GDN_WIDTH + 2 * HEADS, 4 * GDN_WIDTH, 3 * SC_WIDTH))
SUBLANES = 8
PACKED_ROWS = 16


def _in_to_kernel_order(by_chip):
    d = by_chip.shape[-1]
    tl = _pick(d, (256, 128))
    runs = []
    for ref0, ker0, count in IN_SEGMENTS:
        row = ref0
        while row < ref0 + count:
            chip, at = divmod(row, IN_SHARD)
            take = min(ref0 + count - row, IN_SHARD - at)
            runs.append((ker0 + row - ref0, take, chip * IN_SHARD_PAD + at))
            row += take

    def body(w_ref, o_ref):
        o_ref[...] = jnp.zeros_like(o_ref)
        for out0, rows, src0 in runs:
            a0 = out0 // PACKED_ROWS * PACKED_ROWS
            a1 = -(-(out0 + rows) // PACKED_ROWS) * PACKED_ROWS
            window = w_ref[pl.ds(src0 - (out0 - a0), a1 - a0), :]
            row = a0 + lax.broadcasted_iota(jnp.int32, (a1 - a0, 1), 0)
            keep = jnp.logical_and(row >= out0, row < out0 + rows)
            o_ref[a0:a1, :] = jnp.where(keep, window, o_ref[a0:a1, :])

    return pl.pallas_call(
        body, name="in_to_kernel_order", out_shape=jax.ShapeDtypeStruct((IN_PAD, d), by_chip.dtype), grid=(d // tl,),
        in_specs=[pl.BlockSpec((N_CHIPS * IN_SHARD_PAD, tl), lambda j: (0, j))],
        out_specs=pl.BlockSpec((IN_PAD, tl), lambda j: (0, j)),
        compiler_params=_params("parallel"),
    )(by_chip.reshape(N_CHIPS * IN_SHARD_PAD, d))


def _in_from_kernel_order(g_t):
    d = g_t.shape[-1]
    tl = _pick(d, (256, 128))

    def body(g_ref, o_ref):
        row = lax.broadcasted_iota(jnp.int32, (IN_SHARD_PAD, 1), 0)
        for chip in range(N_CHIPS):
            first = chip * IN_SHARD
            runs = []
            for ref0, ker0, count in IN_SEGMENTS:
                lo, hi = max(ref0, first), min(ref0 + count, first + IN_SHARD)
                if lo < hi:
                    runs.append((lo - first, hi - lo, ker0 + lo - ref0))
            val = jnp.zeros((IN_SHARD_PAD, tl), F32)
            patches = []
            for out0, rows, src0 in runs:
                start = src0 - out0
                if 0 <= start <= IN_PAD - IN_SHARD_PAD:
                    window = g_ref[pl.ds(start, IN_SHARD_PAD), :]
                    val = jnp.where(jnp.logical_and(row >= out0, row < out0 + rows), window, val)
                else:
                    patches.append((out0, rows, src0))
            o_ref[chip] = val
            for out0, rows, src0 in patches:
                a0 = out0 // SUBLANES * SUBLANES
                a1 = -(-(out0 + rows) // SUBLANES) * SUBLANES
                window = g_ref[pl.ds(src0 - (out0 - a0), a1 - a0), :]
                keep = jnp.logical_and(row[a0:a1] >= out0, row[a0:a1] < out0 + rows)
                o_ref[chip, a0:a1, :] = jnp.where(keep, window, o_ref[chip, a0:a1, :])

    return pl.pallas_call(
        body, name="in_from_kernel_order", out_shape=jax.ShapeDtypeStruct((N_CHIPS, IN_SHARD_PAD, d), F32), grid=(d // tl,),
        in_specs=[pl.BlockSpec((IN_PAD, tl), lambda j: (0, j))],
        out_specs=pl.BlockSpec((N_CHIPS, IN_SHARD_PAD, tl), lambda j: (0, 0, j)),
        compiler_params=_params("parallel"),
    )(g_t)


PACK_LANES = 3 * GDN_WIDTH
PACKED = dict(mix_pre_norm=(0, 1, 0, D_MODEL), mix_post_norm=(1, 1, 0, D_MODEL), ffn_pre_norm=(2, 1, 0, D_MODEL),
              ffn_post_norm=(3, 1, 0, D_MODEL), a_log=(4, 1, 0, HEADS), dt_bias=(5, 1, 0, HEADS), loss=(6, 1, 0, 1),
              gdn_norm=(7, 1, 0, HEAD_DIM), conv_qkv=(8, GDN_CONV, 0, 3 * GDN_WIDTH), conv_sc=(0, SC_CONV, D_MODEL, SC_WIDTH),
              meta_tokens=(16, N_META, 0, D_MODEL))
PACK_ROWS = 32
SHARDED_SMALL = ("conv_qkv", "conv_sc", "meta_tokens")


def _pack_small(values):
    names = list(PACKED)

    def body(*refs):
        out_ref = refs[-1]
        out_ref[...] = jnp.zeros_like(out_ref)
        for name, ref in zip(names, refs):
            row, rows, lane0, lanes = PACKED[name]
            out_ref[row:row + rows, lane0:lane0 + lanes] = ref[...]

    return pl.pallas_call(body, name="pack_small", out_shape=jax.ShapeDtypeStruct((PACK_ROWS, PACK_LANES), F32))(
        *[values[name] for name in names])


def _sum_devices(packed_all, chip):
    names = list(PACKED)

    def body(chip_ref, all_ref, *rest):
        shard_refs, out_refs = rest[:len(SHARDED_SMALL)], rest[len(SHARDED_SMALL):]

        def total(ref, rows, lanes):
            acc = ref[0, rows, lanes]
            for k in range(1, 8):
                acc = acc + ref[k, rows, lanes]
            return acc

        for name, out in zip(names, out_refs):
            row, rows, lane0, lanes = PACKED[name]
            if name in SHARDED_SMALL:
                out[...] = total(shard_refs[SHARDED_SMALL.index(name)], slice(0, rows), slice(None))
            else:
                out[...] = total(all_ref, slice(row, row + rows), slice(lane0, lane0 + lanes))

    def shard_spec(name):
        row, rows, lane0, lanes = PACKED[name]
        height, width = max(rows, 8), lanes // N_CHIPS
        assert row % height == 0 and lane0 % width == 0
        return pl.BlockSpec((8, height, width), lambda i, chip_ref: (0, row // height, lane0 // width + chip_ref[0]))

    def out_shape(name):
        _, rows, _, lanes = PACKED[name]
        return jax.ShapeDtypeStruct((rows, lanes // N_CHIPS if name in SHARDED_SMALL else lanes), F32)

    whole = lambda shape: pl.BlockSpec(shape, lambda i, chip_ref: (0,) * len(shape))
    outs = pl.pallas_call(
        body, name="sum_devices", out_shape=tuple(out_shape(n) for n in names),
        grid_spec=pltpu.PrefetchScalarGridSpec(
            num_scalar_prefetch=1, grid=(1,),
            in_specs=[whole(packed_all.shape)] + [shard_spec(n) for n in SHARDED_SMALL],
            out_specs=tuple(whole(out_shape(n).shape) for n in names)),
    )(chip, packed_all, *[packed_all] * len(SHARDED_SMALL))
    return dict(zip(names, outs))


def _hbm():
    return pl.BlockSpec(memory_space=pl.ANY)


def _place():
    x, y, c = lax.axis_index("x"), lax.axis_index("y"), lax.axis_index("c")
    chips = ((1 - x, y), (x, 1 - y), (1 - x, 1 - y))
    return x, y, c, chips


def _remote(src, dst, send_sems, recv_sems, k, to):
    return pltpu.make_async_remote_copy(src_ref=src, dst_ref=dst, send_sem=send_sems.at[k], recv_sem=recv_sems.at[k],
                                        device_id=to, device_id_type=MESH)


GATHER_SEMS = 7


def _gather_copies(w_refs, out_refs, send_sems, recv_sems):
    x, y, c, chips = _place()
    mine = 2 * x + y
    sibling = (x, y, 1 - c)
    copy = functools.partial(_remote, send_sems=send_sems, recv_sems=recv_sems)
    direct, landed, passing, from_sibling = [], [], [], []
    for i, (w, o) in enumerate(zip(w_refs, out_refs)):
        k = GATHER_SEMS * i
        direct.append(copy(w, o.at[mine], k=k, to=sibling))
        from_sibling.append(copy(w, o.at[mine], k=k, to=sibling))
        for j, (cx, cy) in enumerate(chips):
            theirs = 2 * cx + cy
            direct.append(copy(w.at[c], o.at[mine, c], k=k + 1 + j, to=(cx, cy, c)))
            landed.append(copy(w.at[c], o.at[theirs, c], k=k + 1 + j, to=sibling))
            passing.append(copy(o.at[theirs, c], o.at[theirs, c], k=k + 4 + j, to=sibling))
            from_sibling.append(copy(w.at[c], o.at[theirs, 1 - c], k=k + 4 + j, to=sibling))
    return direct, landed, passing, from_sibling


def _gather_finish(copies):
    direct, landed, passing, from_sibling = copies
    for arrival, forward in zip(landed, passing):
        arrival.wait_recv()
        forward.start()
    for arrival in from_sibling:
        arrival.wait_recv()
    for cp in direct + passing:
        cp.wait_send()


def _gather_weights(pieces, smalls):
    count, extra = len(pieces), len(smalls)
    total = count + extra

    def body(*refs):
        w_refs, s_refs = refs[:count], refs[count:total]
        out_refs, sall_refs = refs[total:total + count], refs[total + count:2 * total]
        send_sems, recv_sems, local_sems = refs[2 * total:]
        x, y, c, chips = _place()
        mine = 2 * x + y
        own = [pltpu.make_async_copy(s, sall.at[mine], local_sems.at[i]) for i, (s, sall) in enumerate(zip(s_refs, sall_refs))]
        small = [_remote(s, sall.at[mine], send_sems, recv_sems, GATHER_SEMS * count + 3 * i + j, (cx, cy, c))
                 for i, (s, sall) in enumerate(zip(s_refs, sall_refs)) for j, (cx, cy) in enumerate(chips)]
        copies = _gather_copies(w_refs, out_refs, send_sems, recv_sems)
        for cp in own + small + copies[0]:
            cp.start()
        _gather_finish(copies)
        for cp in small:
            cp.wait_recv()
        for cp in small:
            cp.wait_send()
        for cp in own:
            cp.wait()

    sems = GATHER_SEMS * count + 3 * extra
    return pl.pallas_call(
        body, name="gather_weights",
        out_shape=tuple(jax.ShapeDtypeStruct((N_CHIPS,) + p.shape, p.dtype) for p in list(pieces) + list(smalls)),
        in_specs=[_hbm()] * total, out_specs=(_hbm(),) * total,
        scratch_shapes=[pltpu.SemaphoreType.DMA((sems,)), pltpu.SemaphoreType.DMA((sems,)), pltpu.SemaphoreType.DMA((extra,))],
    )(*pieces, *smalls)


def _sibling_copies(g_refs, got_refs, send_sems, recv_sems):
    x, y, c, _ = _place()
    return [_remote(g.at[:, 1 - c], got, send_sems, recv_sems, i, (x, y, 1 - c)) for i, (g, got) in enumerate(zip(g_refs, got_refs))]


def _exchange_siblings(grads):
    count = len(grads)

    def body(*refs):
        copies = _sibling_copies(refs[:count], refs[count:2 * count], *refs[2 * count:])
        for cp in copies:
            cp.start()
        for cp in copies:
            cp.wait_recv()
        for cp in copies:
            cp.wait_send()

    return pl.pallas_call(
        body, name="exchange_siblings",
        out_shape=tuple(jax.ShapeDtypeStruct((g.shape[0],) + g.shape[2:], F32) for g in grads),
        in_specs=[_hbm()] * count, out_specs=(_hbm(),) * count,
        scratch_shapes=[pltpu.SemaphoreType.DMA((count,)), pltpu.SemaphoreType.DMA((count,))],
    )(*grads)


def _chip_copies(p_refs, got_refs, send_sems, recv_sems):
    x, y, c, chips = _place()
    return [_remote(p.at[2 * cx + cy], got.at[j], send_sems, recv_sems, 3 * i + j, (cx, cy, c))
            for i, (p, got) in enumerate(zip(p_refs, got_refs)) for j, (cx, cy) in enumerate(chips)]


def _share_halves(halves, small):
    count = len(halves)

    def body(*refs):
        h_refs, s_ref = refs[:count], refs[count]
        full_refs, sall_ref = refs[count + 1:2 * count + 1], refs[2 * count + 1]
        send_sems, recv_sems, local_sem = refs[2 * count + 2:]
        x, y, c, _ = _place()
        me = 4 * x + 2 * y + c
        own = pltpu.make_async_copy(s_ref, sall_ref.at[me], local_sem)
        own.start()
        copies = [_remote(h.at[c], full.at[c], send_sems, recv_sems, i, (x, y, 1 - c))
                  for i, (h, full) in enumerate(zip(h_refs, full_refs))]
        for k in range(7):
            dx, dy, dc = ((k + 1) >> 2) & 1, ((k + 1) >> 1) & 1, (k + 1) & 1
            peer = (1 - x if dx else x, 1 - y if dy else y, 1 - c if dc else c)
            copies.append(_remote(s_ref, sall_ref.at[me], send_sems, recv_sems, count + k, peer))
        for cp in copies:
            cp.start()
        for cp in copies:
            cp.wait_recv()
        for cp in copies:
            cp.wait_send()
        own.wait()

    return pl.pallas_call(
        body, name="share_halves",
        out_shape=tuple(jax.ShapeDtypeStruct(h.shape, h.dtype) for h in halves) + (jax.ShapeDtypeStruct((8,) + small.shape, F32),),
        in_specs=[_hbm()] * (count + 1), out_specs=(_hbm(),) * (count + 1), input_output_aliases={i: i for i in range(count)},
        scratch_shapes=[pltpu.SemaphoreType.DMA((count + 7,)), pltpu.SemaphoreType.DMA((count + 7,)), pltpu.SemaphoreType.DMA],
    )(*halves, small)


def _add_sibling(grads, gots, core, name):
    count = len(grads)
    chips, _, rows, cols = grads[0].shape

    def body(core_ref, *refs):
        for i in range(count):
            s = refs[i][...] + refs[count + i][...]
            refs[2 * count + 2 * i][...] = s
            refs[2 * count + 2 * i + 1][...] = s.astype(BF16)

    block = pl.BlockSpec((None, rows, cols), lambda p, core_ref: (p, 0, 0))
    own = pl.BlockSpec((None, None, rows, cols), lambda p, core_ref: (p, core_ref[0], 0, 0))
    out = pl.pallas_call(
        body, name="add_sibling_" + name,
        out_shape=(jax.ShapeDtypeStruct((chips, rows, cols), F32), jax.ShapeDtypeStruct((chips, rows, cols), BF16)) * count,
        grid_spec=pltpu.PrefetchScalarGridSpec(
            num_scalar_prefetch=1, grid=(chips,), in_specs=[own] * count + [block] * count, out_specs=(block, block) * count),
        compiler_params=_params("parallel"),
    )(core, *grads, *gots)
    return [(out[2 * i], out[2 * i + 1]) for i in range(count)]


def _add_chips(parts, gots, chip_core, name):
    count = len(parts)
    _, rows, cols = parts[0].shape
    tr = rows // 2 if rows % 32 == 0 else rows

    def body(place_ref, *refs):
        for i in range(count):
            r_ref = refs[count + i]
            refs[2 * count + i][...] = ((refs[i][...] + r_ref[0].astype(F32)) + r_ref[1].astype(F32)) + r_ref[2].astype(F32)

    return pl.pallas_call(
        body, name="add_chips_" + name, out_shape=(jax.ShapeDtypeStruct((2, rows, cols), F32),) * count,
        grid_spec=pltpu.PrefetchScalarGridSpec(
            num_scalar_prefetch=1, grid=(rows // tr,),
            in_specs=[pl.BlockSpec((None, tr, cols), lambda i, place_ref: (place_ref[0], i, 0))] * count
            + [pl.BlockSpec((3, tr, cols), lambda i, place_ref: (0, i, 0))] * count,
            out_specs=(pl.BlockSpec((None, tr, cols), lambda i, place_ref: (place_ref[1], i, 0)),) * count),
        compiler_params=_params("parallel"),
    )(chip_core, *parts, *gots)


def _adamw(w, g, m, v, name):
    rows, cols = w.shape
    tr = _pick(rows, (256, 352, 176, 128, 64, 32, 16, 8))

    def body(w_ref, g_ref, m_ref, v_ref, d_ref, nm_ref, nv_ref):
        d_ref[...], nm_ref[...], nv_ref[...] = _adamw_math(w_ref[...], g_ref[...], m_ref[...], v_ref[...])

    block = pl.BlockSpec((tr, cols), lambda i: (i, 0))
    shape = jax.ShapeDtypeStruct((rows, cols), F32)
    return pl.pallas_call(
        body, name="adamw_" + name, out_shape=(shape, shape, shape), grid=(rows // tr,),
        in_specs=[block] * 4, out_specs=(block,) * 3, compiler_params=_params("parallel"),
    )(w, g, m, v)


def _adamw_math(w, g, m, v):
    m = ADAM_B1 * m + (1.0 - ADAM_B1) * g
    v = ADAM_B2 * v + (1.0 - ADAM_B2) * (g * g)
    m_hat = m / (1.0 - ADAM_B1 ** ADAM_STEP)
    v_hat = v / (1.0 - ADAM_B2 ** ADAM_STEP)
    return -ADAM_LR * (m_hat / (jnp.sqrt(v_hat) + ADAM_EPS) + ADAM_WD * w), m, v


def _adamw_small(ws, gs, ms, vs):
    count = len(ws)

    def body(*refs):
        ins, outs = refs[:4 * count], refs[4 * count:]
        for i in range(count):
            outs[i][...], outs[count + i][...], outs[2 * count + i][...] = _adamw_math(
                ins[i][...], ins[count + i][...], ins[2 * count + i][...], ins[3 * count + i][...])

    shapes = tuple(jax.ShapeDtypeStruct(w.shape, F32) for w in ws)
    out = pl.pallas_call(body, name="adamw_small", out_shape=shapes * 3)(*ws, *gs, *ms, *vs)
    return out[:count], out[count:2 * count], out[2 * count:]


WEIGHTS = ("meta_tokens", "mix_pre_norm", "mix_post_norm", "ffn_pre_norm", "ffn_post_norm", "w_in", "conv_qkv", "a_log",
           "dt_bias", "gdn_norm", "conv_sc", "w_out", "w_gate", "w_up", "w_down")


def kernel(x, meta_tokens, mix_pre_norm, mix_post_norm, ffn_pre_norm, ffn_post_norm, w_in, conv_qkv, a_log, dt_bias, gdn_norm, conv_sc, w_out, w_gate, w_up, w_down, loss_target, m_meta_tokens, m_mix_pre_norm, m_mix_post_norm, m_ffn_pre_norm, m_ffn_post_norm, m_w_in, m_conv_qkv, m_a_log, m_dt_bias, m_gdn_norm, m_conv_sc, m_w_out, m_w_gate, m_w_up, m_w_down, v_meta_tokens, v_mix_pre_norm, v_mix_post_norm, v_ffn_pre_norm, v_ffn_post_norm, v_w_in, v_conv_qkv, v_a_log, v_dt_bias, v_gdn_norm, v_conv_sc, v_w_out, v_w_gate, v_w_up, v_w_down):
    d = x.shape[-1]
    two_d = lambda a: a.reshape(a.shape[-2:])
    weights = dict(zip(WEIGHTS, (meta_tokens, mix_pre_norm, mix_post_norm, ffn_pre_norm, ffn_post_norm, w_in, conv_qkv, a_log,
                                 dt_bias, gdn_norm, conv_sc, w_out, w_gate, w_up, w_down)))
    m_in = dict(zip(WEIGHTS, (m_meta_tokens, m_mix_pre_norm, m_mix_post_norm, m_ffn_pre_norm, m_ffn_post_norm, m_w_in, m_conv_qkv,
                              m_a_log, m_dt_bias, m_gdn_norm, m_conv_sc, m_w_out, m_w_gate, m_w_up, m_w_down)))
    v_in = dict(zip(WEIGHTS, (v_meta_tokens, v_mix_pre_norm, v_mix_post_norm, v_ffn_pre_norm, v_ffn_post_norm, v_w_in, v_conv_qkv,
                              v_a_log, v_dt_bias, v_gdn_norm, v_conv_sc, v_w_out, v_w_gate, v_w_up, v_w_down)))
    core = lax.axis_index("c")
    chip = 2 * lax.axis_index("x") + lax.axis_index("y")
    core_arg = core.reshape(1).astype(jnp.int32)
    chip_core = jnp.stack([chip, core]).astype(jnp.int32)
    whole = lambda a: a.reshape(a.shape[:-3] + (2 * a.shape[-2], d))
    by_rows = lambda n, a: two_d(a).T if n in ("w_in", "w_gate", "w_up") else two_d(a)

    shard = {n: by_rows(n, weights[n]).astype(MXU_DTYPE) for n in MATRICES}
    shard["w_in"] = jnp.pad(shard["w_in"], ((0, IN_SHARD_PAD - IN_SHARD), (0, 0)))
    small_all = _gather_weights([], [two_d(weights[n]) for n in SHARDED_SMALL])
    conv_qkv_full, conv_sc_full, meta_full = (jnp.concatenate([a[p] for p in range(N_CHIPS)], axis=1) for a in small_all)

    sq, grad_x, g, sums = _local_step(
        x, loss_target, meta_full, (mix_pre_norm, mix_post_norm, ffn_pre_norm, ffn_post_norm), _halves(shard["w_in"]),
        conv_qkv_full, a_log, dt_bias, gdn_norm, conv_sc_full, [_halves(shard[n]) for n in LATER], core_arg)

    parts, gots = zip(*sums)
    totals = [_add_chips(parts[i:i + 1], gots[i:i + 1], chip_core, MATRICES[i])[0] for i in range(2)]
    totals += _add_chips(parts[2:], gots[2:], chip_core, "ffn")
    *shared, packed_all = _share_halves(totals, _pack_small(dict(g, loss=sq)))
    grads = {n: whole(a) for n, a in zip(MATRICES, shared)}
    grads["w_in"] = grads["w_in"][:IN_SHARD]
    grads.update(_sum_devices(packed_all, chip.reshape(1).astype(jnp.int32)))
    loss = (0.5 / d) * grads.pop("loss")[0, 0]

    small = [n for n in WEIGHTS if n not in MATRICES]
    updates = dict(zip(small, zip(*_adamw_small(*([by_rows(n, params[n]) for n in small] for params in (weights, grads, m_in, v_in))))))
    outs = [[], [], [], []]
    for n in WEIGHTS:
        shape = weights[n].shape
        if n in MATRICES:
            updates[n] = _adamw(by_rows(n, weights[n]), grads[n], by_rows(n, m_in[n]), by_rows(n, v_in[n]), n)
        for out, a in zip(outs, (grads[n], *updates[n])):
            out.append((a.T if n in ("w_in", "w_gate", "w_up") else a).reshape(shape))
    return (loss, grad_x, *outs[0], *outs[1], *outs[2], *outs[3])
```

```python
import functools

import jax
import jax.numpy as jnp
from jax import lax
from jax.experimental import pallas as pl
from jax.experimental.pallas import tpu as pltpu

F32 = jnp.float32
BF16 = jnp.bfloat16
MXU_DTYPE = jnp.bfloat16
MESH = pl.DeviceIdType.MESH

D_MODEL = 1024
N_META = 16
HEADS = 4
HEAD_DIM = 128
GDN_WIDTH = HEADS * HEAD_DIM
GDN_CONV = 4
CHUNK = 64
SC_WIDTH = D_MODEL - GDN_WIDTH
SC_CONV = 3
D_FF = 2816
IN_WIDTH = 4 * GDN_WIDTH + 2 * HEADS + 3 * SC_WIDTH
IN_PAD = 3840
BA_COL = (4 * GDN_WIDTH + 3 * SC_WIDTH) // 128
EPS = 1e-6
LANES = 128
N_CHIPS = 4
VMEM_LIMIT = 48 * 2 ** 20
MM_VMEM_BUDGET = 42 * 2 ** 20

ADAM_LR = 0.001
ADAM_B1 = 0.9
ADAM_B2 = 0.999
ADAM_EPS = 1e-08
ADAM_WD = 0.01
ADAM_STEP = 10


def _pick(n, candidates):
    for c in candidates:
        if n % c == 0:
            return c
    return n


def _row_tile(n):
    return _pick(n, (352, 256, 176, 128, 64, 32, 16, 8))


def _params(*sem):
    return pltpu.CompilerParams(dimension_semantics=sem, vmem_limit_bytes=VMEM_LIMIT)


def _sigmoid(x):
    return 0.5 * jnp.tanh(0.5 * x) + 0.5


def _softplus(x):
    return jnp.maximum(x, 0.0) + jnp.log(1.0 + jnp.exp(-jnp.abs(x)))


def _dsilu(x, s):
    return s * (1.0 + x * (1.0 - s))


def _mm(a, b, mode, out_dtype, name, init=None, exchange=None):
    if mode == "tn":
        k_dim, m_dim = a.shape
    else:
        m_dim, k_dim = a.shape
    n_dim = b.shape[0] if mode == "nt" else b.shape[1]
    tn = _pick(n_dim, (1408, 1280, 1024, 768, 512, 256, 128))
    if mode == "tn":
        tm = _pick(m_dim, (1408, 1280, 1024, 512, 256, 128))
        tk = _pick(k_dim, (2112, 1408, 1280, 1056, 1024, 512, 256, 128))
    else:
        tk = k_dim
        blocks = lambda rows: 2 * (2 * rows * tk + 2 * tk * tn + 4 * rows * tn * (1 if init is None else 2))
        tm = next((t for t in (2112, 1056, 1024, 704, 512, 256, 128) if m_dim % t == 0 and blocks(t) <= MM_VMEM_BUDGET), m_dim)
    nk = k_dim // tk
    if mode == "nn":
        a_spec = pl.BlockSpec((tm, tk), lambda i, j, k: (i, k))
        b_spec = pl.BlockSpec((tk, tn), lambda i, j, k: (k, j))
        dims = (((1,), (0,)), ((), ()))
    elif mode == "nt":
        a_spec = pl.BlockSpec((tm, tk), lambda i, j, k: (i, k))
        b_spec = pl.BlockSpec((tn, tk), lambda i, j, k: (j, k))
        dims = (((1,), (1,)), ((), ()))
    else:
        a_spec = pl.BlockSpec((tk, tm), lambda i, j, k: (k, i))
        b_spec = pl.BlockSpec((tk, tn), lambda i, j, k: (k, j))
        dims = (((0,), (0,)), ((), ()))

    out_spec = pl.BlockSpec((tm, tn), lambda i, j, k: (i, j))
    grid = (m_dim // tm, n_dim // tn, nk)
    parts = () if exchange is None else tuple(exchange)
    count = len(parts)
    first_in = 2 if init is None else 3

    assert out_dtype == F32

    def body(a_ref, b_ref, *rest):
        o_ref = rest[first_in - 2 + count]
        k = pl.program_id(2)
        step = (pl.program_id(0) * grid[1] + pl.program_id(1)) * nk + k
        if count:
            copies = _chip_copies(rest[first_in - 2:first_in - 2 + count], rest[first_in - 1 + count:first_in - 1 + 2 * count],
                                  *rest[first_in - 1 + 2 * count:])

            @pl.when(step == 0)
            def _():
                for cp in copies:
                    cp.start()

        p = lax.dot_general(a_ref[...], b_ref[...], dims, preferred_element_type=F32)
        if nk == 1:
            o_ref[...] = p if init is None else rest[0][...] + p
        else:
            @pl.when(k == 0)
            def _():
                o_ref[...] = p if init is None else rest[0][...] + p

            @pl.when(k > 0)
            def _():
                o_ref[...] += p

        if count:
            @pl.when(step == grid[0] * grid[1] * nk - 1)
            def _():
                for cp in copies:
                    cp.wait_recv()
                for cp in copies:
                    cp.wait_send()

    out = pl.pallas_call(
        body, name=name,
        out_shape=(jax.ShapeDtypeStruct((m_dim, n_dim), out_dtype),)
        + tuple(jax.ShapeDtypeStruct((3,) + p.shape[1:], p.dtype) for p in parts),
        grid=grid,
        in_specs=[a_spec, b_spec] + ([] if init is None else [out_spec]) + [_hbm()] * count,
        out_specs=(out_spec,) + (_hbm(),) * count,
        scratch_shapes=[pltpu.SemaphoreType.DMA((3 * count,)), pltpu.SemaphoreType.DMA((3 * count,))] if count else [],
        compiler_params=_params(*(("arbitrary",) * 3 if count else ("parallel", "parallel", "arbitrary"))),
    )(a, b, *(() if init is None else (init,)), *parts)
    return out[0] if not count else out


def _rms_apply(x, w):
    r = lax.rsqrt(jnp.mean(x * x, axis=-1, keepdims=True) + EPS)
    return x * r * w


def _rms_bwd(x, w, dy):
    r = lax.rsqrt(jnp.mean(x * x, axis=-1, keepdims=True) + EPS)
    xh = x * r
    dyw = dy * w
    dx = r * (dyw - xh * jnp.mean(dyw * xh, axis=-1, keepdims=True))
    return dx, jnp.sum(dy * xh, axis=0, keepdims=True)


def _accumulate(ref, first, value):
    @pl.when(first)
    def _():
        ref[...] = value

    @pl.when(jnp.logical_not(first))
    def _():
        ref[...] += value


def _rows(tr, width):
    return pl.BlockSpec((tr, width), lambda i: (i, 0))


def _vec(width):
    return pl.BlockSpec((1, width), lambda i: (0, 0))


def _embed(x, head, w_pre, w_shard, rows_per_seq):
    batch, seq, d = x.shape
    x_offset = head.shape[0]
    tr = _row_tile(rows_per_seq)
    tiles_per_seq = rows_per_seq // tr
    n = batch * rows_per_seq

    def body(x_ref, head_ref, w_ref, ws_ref, h0_ref, u_ref, wall_ref, send_sems, recv_sems):
        gather = _gather_copies([ws_ref], [wall_ref], send_sems, recv_sems)
        i = pl.program_id(0)
        tile = lax.rem(i, tiles_per_seq)

        @pl.when(i == 0)
        def _():
            for cp in gather[0]:
                cp.start()

        rows = jnp.concatenate([head_ref[...], x_ref[0:tr - x_offset, :]], axis=0)
        if tiles_per_seq > 1:
            start = pl.multiple_of(jnp.maximum(tile * tr - x_offset, 0), SUBLANES)
            rows = jnp.where(tile == 0, rows, x_ref[pl.ds(start, tr), :])
        h0_ref[...] = rows
        u_ref[...] = _rms_apply(rows, w_ref[...]).astype(u_ref.dtype)

        @pl.when(i == n // tr - 1)
        def _():
            _gather_finish(gather)

    return pl.pallas_call(
        body, name="embed",
        out_shape=(jax.ShapeDtypeStruct((n, d), F32), jax.ShapeDtypeStruct((n, d), MXU_DTYPE),
                   jax.ShapeDtypeStruct((N_CHIPS,) + w_shard.shape, w_shard.dtype)),
        grid=(n // tr,),
        in_specs=[pl.BlockSpec((None, seq, d), lambda i: (i // tiles_per_seq, 0, 0)),
                  pl.BlockSpec((x_offset, d), lambda i: (0, 0)), _vec(d), _hbm()],
        out_specs=(_rows(tr, d), _rows(tr, d), _hbm()),
        scratch_shapes=[pltpu.SemaphoreType.DMA((GATHER_SEMS,)), pltpu.SemaphoreType.DMA((GATHER_SEMS,))],
        compiler_params=_params("arbitrary"),
    )(x, head, w_pre, w_shard)


def _mix_residual(cat, w_out, h0, w_post, w_pre):
    n, d = h0.shape
    tr = _pick(n, (1056, 1024, 704, 512, 256, 128))

    def body(cat_ref, w_ref, h0_ref, wpost_ref, wpre_ref, mix_ref, h1_ref, u2_ref):
        mix = jnp.dot(cat_ref[...], w_ref[...], preferred_element_type=F32)
        mix_ref[...] = mix
        h1 = h0_ref[...] + _rms_apply(mix, wpost_ref[...])
        h1_ref[...] = h1
        u2_ref[...] = _rms_apply(h1, wpre_ref[...]).astype(u2_ref.dtype)

    wide = jax.ShapeDtypeStruct((n, d), F32)
    return pl.pallas_call(
        body, name="mix_residual", out_shape=(wide, wide, jax.ShapeDtypeStruct((n, d), MXU_DTYPE)), grid=(n // tr,),
        in_specs=[_rows(tr, cat.shape[1]), pl.BlockSpec(w_out.shape, lambda i: (0, 0)), _rows(tr, d), _vec(d), _vec(d)],
        out_specs=(_rows(tr, d), _rows(tr, d), _rows(tr, d)), compiler_params=_params("parallel"),
    )(cat, w_out, h0, w_post, w_pre)


NT_DIMS = (((1,), (1,)), ((), ()))


def _ffn_tiles(n):
    return _pick(n, (1056, 704, 512, 256, 128)), _pick(D_FF, (1408, 256, 128))


def _swiglu_fwd(u, w_gate_t, w_up_t, w_next):
    n, d = u.shape
    tm, tn = _ffn_tiles(n)
    grid = (D_FF // tn, n // tm)

    def body(u_ref, wg_ref, wu_ref, wn_ref, g_ref, up_ref, act_ref, wall_ref, send_sems, recv_sems):
        gather = _gather_copies([wn_ref], [wall_ref], send_sems, recv_sems)
        step = pl.program_id(0) * grid[1] + pl.program_id(1)

        @pl.when(step == 0)
        def _():
            for cp in gather[0]:
                cp.start()

        a = u_ref[...]
        g = lax.dot_general(a, wg_ref[...], NT_DIMS, preferred_element_type=F32)
        up = lax.dot_general(a, wu_ref[...], NT_DIMS, preferred_element_type=F32)
        g_ref[...] = g.astype(g_ref.dtype)
        up_ref[...] = up.astype(up_ref.dtype)
        act_ref[...] = (g * _sigmoid(g) * up).astype(act_ref.dtype)

        @pl.when(step == grid[0] * grid[1] - 1)
        def _():
            _gather_finish(gather)

    tile = pl.BlockSpec((tm, tn), lambda j, i: (i, j))
    weight = pl.BlockSpec((tn, d), lambda j, i: (j, 0))
    wide = jax.ShapeDtypeStruct((n, D_FF), MXU_DTYPE)
    return pl.pallas_call(
        body, name="swiglu_fwd",
        out_shape=(wide, wide, jax.ShapeDtypeStruct((n, D_FF), MXU_DTYPE),
                   jax.ShapeDtypeStruct((N_CHIPS,) + w_next.shape, w_next.dtype)),
        grid=grid,
        in_specs=[pl.BlockSpec((tm, d), lambda j, i: (i, 0)), weight, weight, _hbm()], out_specs=(tile, tile, tile, _hbm()),
        scratch_shapes=[pltpu.SemaphoreType.DMA((GATHER_SEMS,)), pltpu.SemaphoreType.DMA((GATHER_SEMS,))],
        compiler_params=_params("arbitrary", "arbitrary"),
    )(u, w_gate_t, w_up_t, w_next)


def _swiglu_bwd(dffn, w_down, gate, up):
    n, d = dffn.shape
    tm, tn = _ffn_tiles(n)

    def body(dy_ref, w_ref, g_ref, u_ref, dg_ref, du_ref):
        da = lax.dot_general(dy_ref[...], w_ref[...], NT_DIMS, preferred_element_type=F32)
        g = g_ref[...].astype(F32)
        s = _sigmoid(g)
        dg_ref[...] = (da * u_ref[...].astype(F32) * _dsilu(g, s)).astype(dg_ref.dtype)
        du_ref[...] = (da * g * s).astype(du_ref.dtype)

    tile = pl.BlockSpec((tm, tn), lambda j, i: (i, j))
    shape = jax.ShapeDtypeStruct((n, D_FF), MXU_DTYPE)
    return pl.pallas_call(
        body, name="swiglu_bwd", out_shape=(shape, shape), grid=(D_FF // tn, n // tm),
        in_specs=[pl.BlockSpec((tm, d), lambda j, i: (i, 0)), pl.BlockSpec((tn, d), lambda j, i: (j, 0)), tile, tile],
        out_specs=(tile, tile), compiler_params=_params("parallel", "parallel"),
    )(dffn, w_down, gate, up)


def _loss_head(h1, ffn, w_post, target, rows_per_seq, x_offset):
    n, d = h1.shape
    tr = _row_tile(rows_per_seq)
    tiles_per_seq = rows_per_seq // tr
    seq = target.shape[1]

    def seq_rows(t_ref, tile):
        first = jnp.concatenate([jnp.zeros((x_offset, d), F32), t_ref[0:tr - x_offset, :]], axis=0)
        if tiles_per_seq == 1:
            return first
        start = pl.multiple_of(jnp.maximum(tile * tr - x_offset, 0), SUBLANES)
        return jnp.where(tile == 0, first, t_ref[pl.ds(start, tr), :])

    def body(h1_ref, ffn_ref, w_ref, t_ref, dh2_ref, dffn_ref, dw_ref, sq_ref):
        i = pl.program_id(0)
        tile = lax.rem(i, tiles_per_seq)
        w = w_ref[...]
        f = ffn_ref[...]
        r = lax.rsqrt(jnp.mean(f * f, axis=-1, keepdims=True) + EPS)
        fh = f * r
        row = tile * tr + lax.broadcasted_iota(jnp.int32, (tr, 1), 0)
        err = jnp.where(row >= x_offset, h1_ref[...] + fh * w - seq_rows(t_ref, tile), 0.0)
        dh2 = err * (1.0 / d)
        dh2_ref[...] = dh2
        dyw = dh2 * w
        dffn_ref[...] = (r * (dyw - fh * jnp.mean(dyw * fh, axis=-1, keepdims=True))).astype(dffn_ref.dtype)
        _accumulate(dw_ref, i == 0, jnp.sum(dh2 * fh, axis=0, keepdims=True))
        _accumulate(sq_ref, i == 0, jnp.sum(jnp.sum(err * err, axis=1, keepdims=True), axis=0, keepdims=True))

    return pl.pallas_call(
        body, name="loss_head",
        out_shape=(jax.ShapeDtypeStruct((n, d), F32), jax.ShapeDtypeStruct((n, d), MXU_DTYPE),
                   jax.ShapeDtypeStruct((1, d), F32), jax.ShapeDtypeStruct((1, 1), F32)),
        grid=(n // tr,),
        in_specs=[_rows(tr, d), _rows(tr, d), _vec(d), pl.BlockSpec((None, seq, d), lambda i: (i // tiles_per_seq, 0, 0))],
        out_specs=(_rows(tr, d), _rows(tr, d), _vec(d), _vec(1)),
        compiler_params=_params("arbitrary"),
    )(h1, ffn, w_post, target)


def _mid_bwd(h1, mix, w_mix_post, w_ffn_pre, dh2, du2_gate, dup, w_up_t, grads):
    n, d = h1.shape
    tr = _pick(n, (528, 512, 352, 256, 128, 64, 32, 16, 8))
    count = len(grads)

    def body(h1_ref, mix_ref, wpost_ref, wpre_ref, dh2_ref, du2_ref, dup_ref, wup_ref, *rest):
        g_refs, (dh1_ref, dmix_ref, dwpre_ref, dwpost_ref), got_refs = rest[:count], rest[count:count + 4], rest[count + 4:2 * count + 4]
        exchange = _sibling_copies(g_refs, got_refs, *rest[2 * count + 4:])
        i = pl.program_id(0)

        @pl.when(i == 0)
        def _():
            for cp in exchange:
                cp.start()

        du2 = du2_ref[...] + jnp.dot(dup_ref[...], wup_ref[...], preferred_element_type=F32)
        dx, dwpre = _rms_bwd(h1_ref[...], wpre_ref[...], du2)
        dh1 = dh2_ref[...] + dx
        dh1_ref[...] = dh1
        dmix, dwpost = _rms_bwd(mix_ref[...], wpost_ref[...], dh1)
        dmix_ref[...] = dmix.astype(dmix_ref.dtype)
        _accumulate(dwpre_ref, i == 0, dwpre)
        _accumulate(dwpost_ref, i == 0, dwpost)

        @pl.when(i == n // tr - 1)
        def _():
            for cp in exchange:
                cp.wait_recv()
            for cp in exchange:
                cp.wait_send()

    dh1, dmix, dwpre, dwpost, *got = pl.pallas_call(
        body, name="mid_bwd",
        out_shape=(jax.ShapeDtypeStruct((n, d), F32), jax.ShapeDtypeStruct((n, d), MXU_DTYPE),
                   jax.ShapeDtypeStruct((1, d), F32), jax.ShapeDtypeStruct((1, d), F32))
        + tuple(jax.ShapeDtypeStruct((g.shape[0],) + g.shape[2:], F32) for g in grads),
        grid=(n // tr,),
        in_specs=[_rows(tr, d), _rows(tr, d), _vec(d), _vec(d), _rows(tr, d), _rows(tr, d), _rows(tr, dup.shape[1]),
                  pl.BlockSpec(w_up_t.shape, lambda i: (0, 0))] + [_hbm()] * count,
        out_specs=(_rows(tr, d), _rows(tr, d), _vec(d), _vec(d)) + (_hbm(),) * count,
        scratch_shapes=[pltpu.SemaphoreType.DMA((count,)), pltpu.SemaphoreType.DMA((count,))],
        compiler_params=_params("arbitrary"),
    )(h1, mix, w_mix_post, w_ffn_pre, dh2, du2_gate, dup, w_up_t, *grads)
    return dh1, dmix, dwpre, dwpost, got


def _in_bwd(h0, w_pre, dh1, du1, rows_per_seq, pad_rows, x_offset):
    n, d = h0.shape
    tr = _row_tile(rows_per_seq)
    tiles_per_seq = rows_per_seq // tr
    seq = rows_per_seq - x_offset

    def body(h0_ref, w_ref, dh1_ref, du1_ref, gx_ref, dmeta_ref, dw_ref):
        i = pl.program_id(0)
        tile = lax.rem(i, tiles_per_seq)
        dx, dw = _rms_bwd(h0_ref[...], w_ref[...], du1_ref[...])
        dh0 = dh1_ref[...] + dx
        _accumulate(dw_ref, i == 0, dw)

        @pl.when(tile == 0)
        def _():
            gx_ref[0:tr - x_offset, :] = dh0[x_offset:, :]
            _accumulate(dmeta_ref, i == 0, dh0[pad_rows:x_offset, :])

        if tiles_per_seq > 1:
            @pl.when(tile > 0)
            def _():
                gx_ref[pl.ds(pl.multiple_of(tile * tr - x_offset, SUBLANES), tr), :] = dh0

    return pl.pallas_call(
        body, name="in_bwd",
        out_shape=(jax.ShapeDtypeStruct((n // rows_per_seq, seq, d), F32), jax.ShapeDtypeStruct((x_offset - pad_rows, d), F32),
                   jax.ShapeDtypeStruct((1, d), F32)),
        grid=(n // tr,),
        in_specs=[_rows(tr, d), _vec(d), _rows(tr, d), _rows(tr, d)],
        out_specs=(pl.BlockSpec((None, seq, d), lambda i: (i // tiles_per_seq, 0, 0)),
                   pl.BlockSpec((x_offset - pad_rows, d), lambda i: (0, 0)), _vec(d)),
        compiler_params=_params("arbitrary"),
    )(h0, w_pre, dh1, du1)


def _lane_is(lo, hi):
    lane = lax.broadcasted_iota(jnp.int32, (1, LANES), 1)
    return jnp.logical_and(lane >= lo, lane < hi)


def _gates_fwd(proj, a_log_l, dt_bias_l, rows_per_seq, pad_rows):
    n = proj.shape[0]
    tr = _row_tile(rows_per_seq)
    tiles_per_seq = rows_per_seq // tr

    def body(p_ref, a_ref, dt_ref, o_ref):
        x = p_ref[...]
        row = lax.rem(pl.program_id(0), tiles_per_seq) * tr + lax.broadcasted_iota(jnp.int32, (tr, 1), 0)
        g = -jnp.exp(a_ref[...]) * _softplus(x + dt_ref[...])
        val = jnp.where(_lane_is(0, HEADS), _sigmoid(x), jnp.where(_lane_is(HEADS, 2 * HEADS), g, 0.0))
        o_ref[...] = jnp.where(row >= pad_rows, val, 0.0)

    return pl.pallas_call(
        body, name="gates_fwd", out_shape=jax.ShapeDtypeStruct((n, LANES), F32), grid=(n // tr,),
        in_specs=[pl.BlockSpec((tr, LANES), lambda i: (i, BA_COL)), _vec(LANES), _vec(LANES)],
        out_specs=_rows(tr, LANES), compiler_params=_params("parallel"),
    )(proj, a_log_l, dt_bias_l)


def _gates_bwd(proj, dbg, a_log_l, dt_bias_l, rows_per_seq, pad_rows, dproj):
    n = proj.shape[0]
    tr = _row_tile(rows_per_seq)
    tiles_per_seq = rows_per_seq // tr

    def body(p_ref, d_ref, a_ref, dt_ref, _, dx_ref, da_ref, ddt_ref):
        i = pl.program_id(0)
        x = p_ref[...]
        d = d_ref[...]
        row = lax.rem(i, tiles_per_seq) * tr + lax.broadcasted_iota(jnp.int32, (tr, 1), 0)
        live = row >= pad_rows
        beta = _sigmoid(x)
        ea = jnp.exp(a_ref[...])
        xa = x + dt_ref[...]
        g = -ea * _softplus(xa)
        is_g = _lane_is(HEADS, 2 * HEADS)
        d_alogit = jnp.where(jnp.logical_and(live, is_g), d * (-ea) * _sigmoid(xa), 0.0)
        d_blogit = jnp.where(jnp.logical_and(live, _lane_is(0, HEADS)), d * beta * (1.0 - beta), 0.0)
        dx_ref[:, :LANES] = (d_alogit + d_blogit).astype(dx_ref.dtype)
        dx_ref[:, LANES:] = jnp.zeros((tr, LANES), dx_ref.dtype)
        _accumulate(da_ref, i == 0, jnp.sum(jnp.where(jnp.logical_and(live, is_g), d * g, 0.0), axis=0, keepdims=True))
        _accumulate(ddt_ref, i == 0, jnp.sum(d_alogit, axis=0, keepdims=True))

    return pl.pallas_call(
        body, name="gates_bwd",
        out_shape=(jax.ShapeDtypeStruct(dproj.shape, dproj.dtype), jax.ShapeDtypeStruct((1, LANES), F32),
                   jax.ShapeDtypeStruct((1, LANES), F32)),
        grid=(n // tr,),
        in_specs=[pl.BlockSpec((tr, LANES), lambda i: (i, BA_COL)), _rows(tr, LANES), _vec(LANES), _vec(LANES), _hbm()],
        out_specs=(pl.BlockSpec((tr, 2 * LANES), lambda i: (i, BA_COL // 2)), _vec(LANES), _vec(LANES)),
        input_output_aliases={4: 0},
        compiler_params=_params("arbitrary"),
    )(proj, dbg, a_log_l, dt_bias_l, dproj)


HALO = 8


def _halo_scratch(rs):
    return pltpu.VMEM((rs + 2 * HALO, LANES), F32)


def _stage(ref, x):
    rs = x.shape[0]
    ref[0:HALO, :] = jnp.zeros((HALO, LANES), F32)
    ref[HALO + rs:, :] = jnp.zeros((HALO, LANES), F32)
    ref[HALO:HALO + rs, :] = x


def _shifted(ref, k, rs):
    return ref[pl.ds(HALO - k, rs), :]


def _causal_conv(x, x_staged, w, width):
    acc = w[width - 1:width, :] * x
    for i in range(width - 1):
        acc = acc + w[i:i + 1, :] * _shifted(x_staged, width - 1 - i, x.shape[0])
    return acc


def _anti_causal_conv(dy, dy_staged, w, width):
    acc = w[width - 1:width, :] * dy
    for i in range(width - 1):
        acc = acc + w[i:i + 1, :] * _shifted(dy_staged, -(width - 1 - i), dy.shape[0])
    return acc


def _conv_weight_grad(dy, x, x_staged, width):
    taps = [_shifted(x_staged, width - 1 - i, x.shape[0]) for i in range(width - 1)] + [x]
    return jnp.concatenate([jnp.sum(dy * tap, axis=0, keepdims=True) for tap in taps], axis=0)


def _seq_cols(rs, col0, heads):
    return pl.BlockSpec((rs, heads * LANES), lambda j, b: (b, col0 // heads + j))


def _tap_cols(width, col0, heads):
    return pl.BlockSpec((width, heads * LANES), lambda j, b: (0, col0 // heads + j))


def _lanes_of(h):
    return slice(h * LANES, (h + 1) * LANES)


def _qkv_fwd(proj, conv_w, kind, rs):
    n = proj.shape[0]
    col0 = {"q": 0, "k": HEADS, "v": 2 * HEADS}[kind]
    hb = HEADS

    def body(p_ref, w_ref, o_ref, staged):
        for h in range(hb):
            pre = p_ref[:, _lanes_of(h)]
            _stage(staged, pre)
            c = _causal_conv(pre, staged, w_ref[:, _lanes_of(h)], GDN_CONV)
            s = c * _sigmoid(c)
            if kind != "v":
                s = s * lax.rsqrt(jnp.sum(s * s, axis=-1, keepdims=True) + EPS)
            if kind == "q":
                s = s * (HEAD_DIM ** -0.5)
            o_ref[:, _lanes_of(h)] = s

    return pl.pallas_call(
        body, name="qkv_fwd_" + kind, out_shape=jax.ShapeDtypeStruct((n, GDN_WIDTH), F32), grid=(HEADS // hb, n // rs),
        in_specs=[_seq_cols(rs, col0, hb), _tap_cols(GDN_CONV, col0, hb)],
        out_specs=_seq_cols(rs, 0, hb), scratch_shapes=[_halo_scratch(rs)], compiler_params=_params("parallel", "parallel"),
    )(proj, conv_w)


def _qkv_bwd(dy, proj, conv_w, kind, rs, dproj):
    n = proj.shape[0]
    col0 = {"q": 0, "k": HEADS, "v": 2 * HEADS}[kind]
    hb = HEADS

    def body(dy_ref, p_ref, w_ref, _, dp_ref, dw_ref, pre_staged, dc_staged):
        for h in range(hb):
            lanes = _lanes_of(h)
            pre = p_ref[:, lanes]
            w = w_ref[:, lanes]
            _stage(pre_staged, pre)
            c = _causal_conv(pre, pre_staged, w, GDN_CONV)
            sg = _sigmoid(c)
            s = c * sg
            ds = dy_ref[:, lanes]
            if kind == "q":
                ds = ds * (HEAD_DIM ** -0.5)
            if kind != "v":
                r = lax.rsqrt(jnp.sum(s * s, axis=-1, keepdims=True) + EPS)
                sh = s * r
                ds = r * (ds - sh * jnp.sum(ds * sh, axis=-1, keepdims=True))
            dc = ds * _dsilu(c, sg)
            _stage(dc_staged, dc)
            dp_ref[:, lanes] = _anti_causal_conv(dc, dc_staged, w, GDN_CONV).astype(dp_ref.dtype)
            _accumulate(dw_ref.at[:, lanes], pl.program_id(1) == 0, _conv_weight_grad(dc, pre, pre_staged, GDN_CONV))

    return pl.pallas_call(
        body, name="qkv_bwd_" + kind,
        out_shape=(jax.ShapeDtypeStruct(dproj.shape, dproj.dtype), jax.ShapeDtypeStruct((GDN_CONV, GDN_WIDTH), F32)),
        grid=(HEADS // hb, n // rs),
        in_specs=[_seq_cols(rs, 0, hb), _seq_cols(rs, col0, hb), _tap_cols(GDN_CONV, col0, hb), _hbm()],
        out_specs=(_seq_cols(rs, col0, hb), _tap_cols(GDN_CONV, 0, hb)), input_output_aliases={3: 0},
        scratch_shapes=[_halo_scratch(rs), _halo_scratch(rs)],
        compiler_params=_params("parallel", "arbitrary"),
    )(dy, proj, conv_w, dproj)


SC_COL = 4 * HEADS


def _sc_fwd(proj, conv_w, rs, cat):
    n = proj.shape[0]

    hb = 2

    def body(x_ref, b_ref, c_ref, w_ref, _, y_ref, staged):
        for h in range(hb):
            lanes = _lanes_of(h)
            u = c_ref[:, lanes] * x_ref[:, lanes]
            _stage(staged, u)
            y_ref[:, lanes] = (b_ref[:, lanes] * _causal_conv(u, staged, w_ref[:, lanes], SC_CONV)).astype(y_ref.dtype)

    return pl.pallas_call(
        body, name="sc_fwd", out_shape=jax.ShapeDtypeStruct(cat.shape, cat.dtype), grid=(HEADS // hb, n // rs),
        in_specs=[_seq_cols(rs, SC_COL, hb), _seq_cols(rs, SC_COL + 4, hb), _seq_cols(rs, SC_COL + 8, hb),
                  _tap_cols(SC_CONV, 0, hb), _hbm()],
        out_specs=_seq_cols(rs, HEADS, hb), input_output_aliases={4: 0}, scratch_shapes=[_halo_scratch(rs)],
        compiler_params=_params("parallel", "parallel"),
    )(proj, proj, proj, conv_w, cat)


def _sc_bwd(dcat, proj, conv_w, rs, dproj):
    n = proj.shape[0]
    hb = 2

    def body(dy_ref, x_ref, b_ref, c_ref, w_ref, _, dx_ref, db_ref, dc_ref, dw_ref, u_staged, dcv_staged):
        for h in range(hb):
            lanes = _lanes_of(h)
            w = w_ref[:, lanes]
            x = x_ref[:, lanes]
            cc = c_ref[:, lanes]
            u = cc * x
            _stage(u_staged, u)
            dy = dy_ref[:, lanes]
            db_ref[:, lanes] = (dy * _causal_conv(u, u_staged, w, SC_CONV)).astype(db_ref.dtype)
            dcv = dy * b_ref[:, lanes]
            _stage(dcv_staged, dcv)
            du = _anti_causal_conv(dcv, dcv_staged, w, SC_CONV)
            dx_ref[:, lanes] = (du * cc).astype(dx_ref.dtype)
            dc_ref[:, lanes] = (du * x).astype(dc_ref.dtype)
            _accumulate(dw_ref.at[:, lanes], pl.program_id(1) == 0, _conv_weight_grad(dcv, u, u_staged, SC_CONV))

    piece = jax.ShapeDtypeStruct((n, SC_WIDTH), MXU_DTYPE)
    return pl.pallas_call(
        body, name="sc_bwd",
        out_shape=(jax.ShapeDtypeStruct(dproj.shape, dproj.dtype), piece, piece, jax.ShapeDtypeStruct((SC_CONV, SC_WIDTH), F32)),
        grid=(HEADS // hb, n // rs),
        in_specs=[_seq_cols(rs, HEADS, hb), _seq_cols(rs, SC_COL, hb), _seq_cols(rs, SC_COL + 4, hb),
                  _seq_cols(rs, SC_COL + 8, hb), _tap_cols(SC_CONV, 0, hb), _hbm()],
        out_specs=(_seq_cols(rs, SC_COL, hb), _seq_cols(rs, 0, hb), _seq_cols(rs, 0, hb), _tap_cols(SC_CONV, 0, hb)),
        input_output_aliases={5: 0},
        scratch_shapes=[_halo_scratch(rs), _halo_scratch(rs)],
        compiler_params=_params("parallel", "arbitrary"),
    )(dcat, proj, proj, proj, conv_w, dproj)


Z_COL = 3 * HEADS


def _gate_fwd(o, proj, gdn_norm, rs):
    n = proj.shape[0]

    hb = HEADS

    def body(o_ref, z_ref, w_ref, y_ref):
        for h in range(hb):
            lanes = _lanes_of(h)
            z = z_ref[:, lanes]
            y_ref[:, lanes] = (_rms_apply(o_ref[:, lanes], w_ref[...]) * z * _sigmoid(z)).astype(y_ref.dtype)

    return pl.pallas_call(
        body, name="gate_fwd", out_shape=jax.ShapeDtypeStruct((n, D_MODEL), MXU_DTYPE), grid=(HEADS // hb, n // rs),
        in_specs=[_seq_cols(rs, 0, hb), _seq_cols(rs, Z_COL, hb), pl.BlockSpec((1, LANES), lambda j, b: (0, 0))],
        out_specs=_seq_cols(rs, 0, hb), compiler_params=_params("parallel", "parallel"),
    )(o, proj, gdn_norm)


def _gate_bwd(dcat, o, proj, gdn_norm, rs):
    n = proj.shape[0]
    hb = 2

    def body(dy_ref, o_ref, z_ref, w_ref, do_ref, dz_ref, dw_ref):
        w = w_ref[...]
        dw_step = jnp.zeros((1, LANES), F32)
        for h in range(hb):
            lanes = _lanes_of(h)
            z = z_ref[:, lanes]
            o = o_ref[:, lanes]
            dy = dy_ref[:, lanes]
            s = _sigmoid(z)
            dz_ref[:, lanes] = (dy * _rms_apply(o, w) * _dsilu(z, s)).astype(dz_ref.dtype)
            do, dw = _rms_bwd(o, w, dy * z * s)
            do_ref[:, lanes] = do
            dw_step = dw_step + dw
        _accumulate(dw_ref, jnp.logical_and(pl.program_id(0) == 0, pl.program_id(1) == 0), dw_step)

    return pl.pallas_call(
        body, name="gate_bwd",
        out_shape=(jax.ShapeDtypeStruct((n, GDN_WIDTH), F32), jax.ShapeDtypeStruct((n, IN_PAD), MXU_DTYPE),
                   jax.ShapeDtypeStruct((1, LANES), F32)),
        grid=(HEADS // hb, n // rs),
        in_specs=[_seq_cols(rs, 0, hb), _seq_cols(rs, 0, hb), _seq_cols(rs, Z_COL, hb), pl.BlockSpec((1, LANES), lambda j, b: (0, 0))],
        out_specs=(_seq_cols(rs, 0, hb), _seq_cols(rs, Z_COL, hb), pl.BlockSpec((1, LANES), lambda j, b: (0, 0))),
        compiler_params=_params("arbitrary", "arbitrary"),
    )(dcat, o, proj, gdn_norm)


def _dot(a, b):
    return jnp.dot(a.astype(MXU_DTYPE), b.astype(MXU_DTYPE), preferred_element_type=F32)


def _dot_nt(a, b):
    return lax.dot_general(a.astype(MXU_DTYPE), b.astype(MXU_DTYPE), (((1,), (1,)), ((), ())),
                           preferred_element_type=F32)


def _dot_tn(a, b):
    return lax.dot_general(a.astype(MXU_DTYPE), b.astype(MXU_DTYPE), (((0,), (0,)), ((), ())),
                           preferred_element_type=F32)


def _split(x):
    hi = x.astype(MXU_DTYPE)
    return hi, (x - hi.astype(F32)).astype(MXU_DTYPE)


def _dot_split(a, b):
    mm = functools.partial(jnp.dot, preferred_element_type=F32)
    return mm(a[0], b[0]) + (mm(a[0], b[1]) + mm(a[1], b[0]))


def _unit_lower_inverses(mats, eye):
    inv = [eye - a for a in mats]
    power = [_split(a) for a in mats]
    span = 2
    while span < CHUNK:
        power = [_split(_dot_split(p, p)) for p in power]
        inv = [i + _dot_split(_split(i), p) for i, p in zip(inv, power)]
        span *= 2
    return inv


def _chunk_masks():
    ii = lax.broadcasted_iota(jnp.int32, (CHUNK, CHUNK), 0)
    jj = lax.broadcasted_iota(jnp.int32, (CHUNK, CHUNK), 1)
    return ii, jj


def _chunk_decay(g_col, ii, jj):
    incl = ii >= jj
    g_row = jnp.sum(jnp.where(ii == jj, g_col, 0.0), axis=0, keepdims=True)
    gc_col = jnp.sum(jnp.where(incl, g_row, 0.0), axis=1, keepdims=True)
    gc_row = jnp.sum(jnp.where(ii <= jj, g_col, 0.0), axis=0, keepdims=True)
    g_total = jnp.sum(g_row, axis=1, keepdims=True)
    decay = jnp.where(incl, jnp.exp(jnp.where(incl, gc_col - gc_row, 0.0)), 0.0)
    return gc_col, g_total, decay


def _gdn_segments(rs, candidates):
    chunks = rs // CHUNK
    seg_chunks = _pick(chunks, candidates)
    return chunks, seg_chunks, chunks // seg_chunks


def _gdn_fwd(q, k, v, bg, rs, pieces):
    n = q.shape[0]
    batch = n // rs
    chunks, seg_chunks, segs = _gdn_segments(rs, (11, 8, 4, 2))
    seg_rows = seg_chunks * CHUNK
    chains = [(b, h) for b in range(batch) for h in range(HEADS)]
    each = lambda f, *lists: [f(*args) for args in zip(*lists)]
    count = len(pieces)

    def body(q_ref, k_ref, v_ref, bg_ref, *rest):
        w_refs, (o_ref, s_ref, t_ref), out_refs = rest[:count], rest[count:count + 3], rest[count + 3:2 * count + 3]
        state_ref, send_sems, recv_sems = rest[2 * count + 3:]
        gather = _gather_copies(w_refs, out_refs, send_sems, recv_sems)

        @pl.when(pl.program_id(0) == 0)
        def _():
            state_ref[...] = jnp.zeros_like(state_ref)
            for cp in gather[0]:
                cp.start()

        ii, jj = _chunk_masks()
        incl = ii >= jj
        eye = (ii == jj).astype(F32)

        def chunk(c, carry):
            rows = pl.ds(pl.multiple_of(c * CHUNK, CHUNK), CHUNK)
            bgc = [bg_ref[b, rows, :] for b in range(batch)]
            qc = [q_ref[b, rows, _lanes_of(h)] for b, h in chains]
            kc = [k_ref[b, rows, _lanes_of(h)] for b, h in chains]
            vc = [v_ref[b, rows, _lanes_of(h)] for b, h in chains]
            beta = [bgc[b][:, h:h + 1] for b, h in chains]
            state = [state_ref[b, h] for b, h in chains]
            dec = [_chunk_decay(bgc[b][:, HEADS + h:HEADS + h + 1], ii, jj) for b, h in chains]
            gc_col, g_total, decay = ([d[i] for d in dec] for i in range(3))
            kb = each(lambda x, y: x * y, kc, beta)
            a = each(lambda x, y, d: jnp.where(ii > jj, _dot_nt(x, y) * d, 0.0), kb, kc, decay)
            t_inv = _unit_lower_inverses(a, eye)
            eg = [jnp.exp(g) for g in gc_col]
            u = each(lambda t, x, y: _dot(t, x * y), t_inv, vc, beta)
            w = each(lambda t, x, e: _dot(t, x * e), t_inv, kb, eg)
            qk = each(lambda x, y, d: jnp.where(incl, _dot_nt(x, y) * d, 0.0), qc, kc, decay)
            v_new = each(lambda x, y, s: x - _dot(y, s), u, w, state)
            o = each(lambda x, e, s, m, vn: _dot(x * e, s) + _dot(m, vn), qc, eg, state, qk, v_new)
            new_state = each(lambda s, gt, x, g, vn: s * jnp.exp(gt) + _dot_tn(x * jnp.exp(gt - g), vn),
                             state, g_total, kc, gc_col, v_new)
            for i, (b, h) in enumerate(chains):
                s_ref[b, h, c] = state[i]
                t_ref[b, h, c] = t_inv[i]
                o_ref[b, rows, _lanes_of(h)] = o[i]
                state_ref[b, h] = new_state[i]
            return carry

        lax.fori_loop(0, seg_chunks, chunk, 0)

        @pl.when(pl.program_id(0) == segs - 1)
        def _():
            _gather_finish(gather)

    rows_spec = lambda width: pl.BlockSpec((batch, seg_rows, width), lambda s: (0, s, 0))
    per_chunk = lambda r, c: pl.BlockSpec((batch, HEADS, seg_chunks, r, c), lambda s: (0, 0, s, 0, 0))
    as_seqs = lambda a: a.reshape(batch, rs, a.shape[-1])
    sems = GATHER_SEMS * count
    o, states, t_invs, *gathered = pl.pallas_call(
        body, name="gdn_fwd",
        out_shape=(jax.ShapeDtypeStruct((batch, rs, GDN_WIDTH), F32),
                   jax.ShapeDtypeStruct((batch, HEADS, chunks, HEAD_DIM, HEAD_DIM), F32),
                   jax.ShapeDtypeStruct((batch, HEADS, chunks, CHUNK, CHUNK), F32))
        + tuple(jax.ShapeDtypeStruct((N_CHIPS,) + p.shape, p.dtype) for p in pieces),
        grid=(segs,),
        in_specs=[rows_spec(GDN_WIDTH), rows_spec(GDN_WIDTH), rows_spec(GDN_WIDTH), rows_spec(LANES)] + [_hbm()] * count,
        out_specs=(rows_spec(GDN_WIDTH), per_chunk(HEAD_DIM, HEAD_DIM), per_chunk(CHUNK, CHUNK)) + (_hbm(),) * count,
        scratch_shapes=[pltpu.VMEM((batch, HEADS, HEAD_DIM, HEAD_DIM), F32), pltpu.SemaphoreType.DMA((sems,)),
                        pltpu.SemaphoreType.DMA((sems,))],
        compiler_params=_params("arbitrary"),
    )(as_seqs(q), as_seqs(k), as_seqs(v), as_seqs(bg), *pieces)
    return o.reshape(n, GDN_WIDTH), states, t_invs, gathered


def _gdn_bwd(do, q, k, v, bg, states, t_invs, rs, parts):
    n = q.shape[0]
    batch = n // rs
    chunks, seg_chunks, segs = _gdn_segments(rs, (3, 4, 2))
    seg_rows = seg_chunks * CHUNK
    chains = [(b, h) for b in range(batch) for h in range(HEADS)]
    each = lambda f, *lists: [f(*args) for args in zip(*lists)]
    count = len(parts)

    def body(do_ref, q_ref, k_ref, v_ref, bg_ref, s_ref, t_ref, *rest):
        p_refs, (dq_ref, dk_ref, dv_ref, dbg_ref), got_refs = rest[:count], rest[count:count + 4], rest[count + 4:2 * count + 4]
        dstate_ref, send_sems, recv_sems = rest[2 * count + 4:]
        exchange = _chip_copies(p_refs, got_refs, send_sems, recv_sems)

        @pl.when(pl.program_id(0) == 0)
        def _():
            dstate_ref[...] = jnp.zeros_like(dstate_ref)
            for cp in exchange:
                cp.start()

        ii, jj = _chunk_masks()
        incl = ii >= jj
        strict = ii > jj
        lane = lax.broadcasted_iota(jnp.int32, (1, LANES), 1)

        def rowsum(x):
            return jnp.sum(x, axis=1, keepdims=True)

        def total(x):
            return jnp.sum(rowsum(x), axis=0, keepdims=True)

        def chunk(step, carry):
            c = seg_chunks - 1 - step
            rows = pl.ds(pl.multiple_of(c * CHUNK, CHUNK), CHUNK)
            bgc = [bg_ref[b, rows, :] for b in range(batch)]
            qc = [q_ref[b, rows, _lanes_of(h)] for b, h in chains]
            kc = [k_ref[b, rows, _lanes_of(h)] for b, h in chains]
            vc = [v_ref[b, rows, _lanes_of(h)] for b, h in chains]
            doc = [do_ref[b, rows, _lanes_of(h)] for b, h in chains]
            beta = [bgc[b][:, h:h + 1] for b, h in chains]
            state = [s_ref[b, h, c] for b, h in chains]
            t_inv = [t_ref[b, h, c] for b, h in chains]
            d_state = [dstate_ref[b, h] for b, h in chains]
            dec = [_chunk_decay(bgc[b][:, HEADS + h:HEADS + h + 1], ii, jj) for b, h in chains]
            gc_col, g_total, decay = ([d[i] for d in dec] for i in range(3))
            kb = each(lambda x, y: x * y, kc, beta)
            vb = each(lambda x, y: x * y, vc, beta)
            eg = [jnp.exp(g) for g in gc_col]
            kbg = each(lambda x, y: x * y, kb, eg)
            a = each(lambda x, y, d: jnp.where(strict, _dot_nt(x, y) * d, 0.0), kb, kc, decay)
            qk = each(lambda x, y, d: jnp.where(incl, _dot_nt(x, y) * d, 0.0), qc, kc, decay)
            w = each(_dot, t_inv, kbg)
            u = each(_dot, t_inv, vb)
            q_dec = each(lambda x, y: x * y, qc, eg)
            ek = each(lambda gt, g: jnp.exp(gt - g), g_total, gc_col)
            k_dec = each(lambda x, y: x * y, kc, ek)
            g_last = [jnp.exp(gt) for gt in g_total]
            v_new = each(lambda x, y, s: x - _dot(y, s), u, w, state)
            dv_new = each(lambda m, d, x, ds: _dot_tn(m, d) + _dot(x, ds), qk, doc, k_dec, d_state)
            dqk = each(lambda d, vn: jnp.where(incl, _dot_nt(d, vn), 0.0), doc, v_new)
            dq_dec = each(_dot_nt, doc, state)
            dk_dec = each(_dot_nt, v_new, d_state)
            dg_last = each(lambda s, ds: total(s * ds), state, d_state)
            new_d_state = each(lambda x, d, gl, ds, y, dvn: _dot_tn(x, d) + gl * ds - _dot_tn(y, dvn),
                               q_dec, doc, g_last, d_state, w, dv_new)
            dw = each(lambda dvn, s: -_dot_nt(dvn, s), dv_new, state)
            dt = each(lambda dvn, x, y, z: _dot_nt(dvn, x) + _dot_nt(y, z), dv_new, vb, dw, kbg)
            dvb = each(_dot_tn, t_inv, dv_new)
            dkbg = each(_dot_tn, t_inv, dw)
            t_dt = each(_dot_tn, t_inv, dt)
            da = each(lambda x, t: -jnp.where(strict, _dot_nt(x, t), 0.0), t_dt, t_inv)
            dm_a = each(lambda x, y: x * y, da, decay)
            dm_qk = each(lambda x, y: x * y, dqk, decay)
            e = each(lambda x, y, z, t: x * y + z * t, da, a, dqk, qk)
            dkb = each(lambda m, x, y, z: _dot(m, x) + y * z, dm_a, kc, dkbg, eg)
            dk = each(lambda m, x, m2, y, z, t, p, bt: _dot_tn(m, x) + _dot_tn(m2, y) + z * t + p * bt,
                      dm_a, kb, dm_qk, qc, dk_dec, ek, dkb, beta)
            dq = each(lambda m, x, y, z: _dot(m, x) + y * z, dm_qk, kc, dq_dec, eg)
            dbeta = each(lambda x, y, z, t: rowsum(x * y + z * t), dkb, kc, dvb, vc)
            dgc = each(lambda x, p, pd, r, rd, s, sd: rowsum(x) - rowsum(jnp.where(ii == jj, jnp.sum(x, axis=0, keepdims=True), 0.0))
                       + rowsum(p * pd - r * rd + s * sd), e, dq_dec, q_dec, dk_dec, k_dec, dkbg, kbg)
            d_total = each(lambda r, rd, x, gl: total(r * rd) + x * gl, dk_dec, k_dec, dg_last, g_last)
            dg = each(lambda x, t: rowsum(jnp.where(jj >= ii, jnp.sum(jnp.where(ii == jj, x, 0.0), axis=0, keepdims=True), 0.0)) + t,
                      dgc, d_total)
            dbg = [jnp.zeros((CHUNK, LANES), F32) for _ in range(batch)]
            for i, (b, h) in enumerate(chains):
                dstate_ref[b, h] = new_d_state[i]
                dk_ref[b, rows, _lanes_of(h)] = dk[i]
                dq_ref[b, rows, _lanes_of(h)] = dq[i]
                dv_ref[b, rows, _lanes_of(h)] = dvb[i] * beta[i]
                dbg[b] = dbg[b] + jnp.where(lane == h, dbeta[i], 0.0) + jnp.where(lane == HEADS + h, dg[i], 0.0)
            for b in range(batch):
                dbg_ref[b, rows, :] = dbg[b]
            return carry

        lax.fori_loop(0, seg_chunks, chunk, 0)

        @pl.when(pl.program_id(0) == segs - 1)
        def _():
            for cp in exchange:
                cp.wait_recv()
            for cp in exchange:
                cp.wait_send()

    rows_spec = lambda width: pl.BlockSpec((batch, seg_rows, width), lambda s: (0, segs - 1 - s, 0))
    per_chunk = lambda r, c: pl.BlockSpec((batch, HEADS, seg_chunks, r, c), lambda s: (0, 0, segs - 1 - s, 0, 0))
    as_seqs = lambda a: a.reshape(batch, rs, a.shape[-1])
    grad = jax.ShapeDtypeStruct((batch, rs, GDN_WIDTH), F32)
    wide = rows_spec(GDN_WIDTH)
    dq, dk, dv, dbg, *got = pl.pallas_call(
        body, name="gdn_bwd",
        out_shape=(grad, grad, grad, jax.ShapeDtypeStruct((batch, rs, LANES), F32))
        + tuple(jax.ShapeDtypeStruct((3,) + p.shape[1:], p.dtype) for p in parts),
        grid=(segs,),
        in_specs=[wide, wide, wide, wide, rows_spec(LANES), per_chunk(HEAD_DIM, HEAD_DIM), per_chunk(CHUNK, CHUNK)]
        + [_hbm()] * count,
        out_specs=(wide, wide, wide, rows_spec(LANES)) + (_hbm(),) * count,
        scratch_shapes=[pltpu.VMEM((batch, HEADS, HEAD_DIM, HEAD_DIM), F32), pltpu.SemaphoreType.DMA((3 * count,)),
                        pltpu.SemaphoreType.DMA((3 * count,))],
        compiler_params=_params("arbitrary"),
    )(as_seqs(do), as_seqs(q), as_seqs(k), as_seqs(v), as_seqs(bg), states, t_invs, *parts)
    return dq.reshape(n, GDN_WIDTH), dk.reshape(n, GDN_WIDTH), dv.reshape(n, GDN_WIDTH), dbg.reshape(n, LANES), got


def _lane_vec(vals, offset):
    k = vals.shape[1]
    return jnp.pad(vals, ((0, 0), (offset, LANES - offset - k)))


LATER = ("w_out", "w_gate", "w_up", "w_down")


def _halves(a):
    return a.reshape(a.shape[:-2] + (2, a.shape[-2] // 2, a.shape[-1]))


def _local_step(x, target, meta, norms, w_in_shard, conv_qkv, a_log, dt_bias, gdn_norm, conv_sc, later_shards, core_arg):
    batch, seq, d = x.shape
    tokens = N_META + seq
    pad_rows = (-tokens) % CHUNK
    rs = tokens + pad_rows
    x_offset = pad_rows + N_META
    n = batch * rs
    w_mix_pre, w_mix_post, w_ffn_pre, w_ffn_post = norms

    head = jnp.concatenate([jnp.zeros((pad_rows, d), F32), meta], axis=0)
    a_log_l = _lane_vec(a_log, HEADS)
    dt_bias_l = _lane_vec(dt_bias, HEADS)

    h0, u1, w_in_all = _embed(x, head, w_mix_pre, w_in_shard, rs)
    w_in_t = _in_to_kernel_order(w_in_all.reshape(N_CHIPS, -1, d))
    proj = _mm(u1, w_in_t, "nt", F32, "mm_proj")
    q = _qkv_fwd(proj, conv_qkv, "q", rs)
    k = _qkv_fwd(proj, conv_qkv, "k", rs)
    v = _qkv_fwd(proj, conv_qkv, "v", rs)
    bg = _gates_fwd(proj, a_log_l, dt_bias_l, rs, pad_rows)
    o, states, t_invs, gathered = _gdn_fwd(q, k, v, bg, rs, later_shards[:3])
    w_out, w_gate_t, w_up_t = (a.reshape(-1, d) for a in gathered)
    cat = _sc_fwd(proj, conv_sc, rs, _gate_fwd(o, proj, gdn_norm, rs))
    mix, h1, u2 = _mix_residual(cat, w_out, h0, w_mix_post, w_ffn_pre)
    gate, up, act, w_down = _swiglu_fwd(u2, w_gate_t, w_up_t, later_shards[3])
    w_down = w_down.reshape(-1, d)
    ffn = _mm(act, w_down, "nn", F32, "mm_down")

    dh2, dffn, d_ffn_post, sq = _loss_head(h1, ffn, w_ffn_post, target, rs, x_offset)
    d_w_down = _mm(act, dffn, "tn", F32, "mm_dw_down")
    dgate, dup = _swiglu_bwd(dffn, w_down, gate, up)
    d_w_gate_t = _mm(dgate, u2, "tn", F32, "mm_dw_gate")
    d_w_up_t = _mm(dup, u2, "tn", F32, "mm_dw_up")
    du2_gate = _mm(dgate, w_gate_t, "nn", F32, "mm_du2_gate")
    by_chip = [_halves(g.reshape(N_CHIPS, -1, d)) for g in (d_w_gate_t, d_w_up_t, d_w_down)]
    dh1, dmix, d_ffn_pre, d_mix_post, got_sibling = _mid_bwd(h1, mix, w_mix_post, w_ffn_pre, dh2, du2_gate, dup, w_up_t, by_chip)
    dcat = _mm(dmix, w_out, "nt", F32, "mm_dcat")
    d_w_out = _halves(_mm(cat, dmix, "tn", F32, "mm_dw_out").reshape(N_CHIPS, -1, d))
    sums = (_add_sibling([d_w_out], _exchange_siblings([d_w_out]), core_arg, "w_out")
            + _add_sibling(by_chip, got_sibling, core_arg, "ffn"))
    do, dproj, d_gdn_norm = _gate_bwd(dcat, o, proj, gdn_norm, rs)
    dproj, dscb, dscc, d_conv_sc = _sc_bwd(dcat, proj, conv_sc, rs, dproj)
    dq, dk, dv, dbg, got_chips = _gdn_bwd(do, q, k, v, bg, states, t_invs, rs, [send for _, send in sums[:3]])
    dproj, dwq = _qkv_bwd(dq, proj, conv_qkv, "q", rs, dproj)
    dproj, dwk = _qkv_bwd(dk, proj, conv_qkv, "k", rs, dproj)
    dproj, dwv = _qkv_bwd(dv, proj, conv_qkv, "v", rs, dproj)
    d_conv_qkv = jnp.concatenate([dwq, dwk, dwv], axis=1)
    dproj, d_a_log_l, d_dt_bias_l = _gates_bwd(proj, dbg, a_log_l, dt_bias_l, rs, pad_rows, dproj)
    dproj = lax.dynamic_update_slice(dproj, dscb, (0, (SC_COL + HEADS) * LANES))
    dproj = lax.dynamic_update_slice(dproj, dscc, (0, (SC_COL + 2 * HEADS) * LANES))
    d_w_in_t, got_down = _mm(dproj, u1, "tn", F32, "mm_dw_in", exchange=[sums[3][1]])
    got_chips.append(got_down)
    g_in = _halves(_in_from_kernel_order(d_w_in_t))
    sums = _add_sibling([g_in], _exchange_siblings([g_in]), core_arg, "w_in") + sums
    du1, got_in = _mm(dproj, w_in_t, "nn", F32, "mm_du1", exchange=[sums[0][1]])
    got_chips.insert(0, got_in)
    grad_x, d_meta, d_mix_pre = _in_bwd(h0, w_mix_pre, dh1, du1, rs, pad_rows, x_offset)

    grads = dict(
        meta_tokens=d_meta,
        mix_pre_norm=d_mix_pre, mix_post_norm=d_mix_post, ffn_pre_norm=d_ffn_pre, ffn_post_norm=d_ffn_post,
        conv_qkv=d_conv_qkv,
        a_log=d_a_log_l[:, HEADS:2 * HEADS], dt_bias=d_dt_bias_l[:, HEADS:2 * HEADS],
        gdn_norm=d_gdn_norm, conv_sc=d_conv_sc,
    )
    return sq, grad_x, grads, [(part, got) for (part, _), got in zip(sums, got_chips)]


MATRICES = ("w_in", "w_out", "w_gate", "w_up", "w_down")
IN_SHARD = IN_WIDTH // N_CHIPS
IN_SHARD_PAD = 928


IN_SEGMENTS = ((0, 0, 4 * GDN_WIDTH), (4 * GDN_WIDTH, IN_WIDTH - 2 * HEADS, 2 * HEADS),
               (4 * GDN_WIDTH + 2 * HEADS, 4 * GDN_WIDTH, 3 * SC_WIDTH))
SUBLANES = 8
PACKED_ROWS = 16


def _in_to_kernel_order(by_chip):
    d = by_chip.shape[-1]
    tl = _pick(d, (256, 128))
    runs = []
    for ref0, ker0, count in IN_SEGMENTS:
        row = ref0
        while row < ref0 + count:
            chip, at = divmod(row, IN_SHARD)
            take = min(ref0 + count - row, IN_SHARD - at)
            runs.append((ker0 + row - ref0, take, chip * IN_SHARD_PAD + at))
            row += take

    def body(w_ref, o_ref):
        o_ref[...] = jnp.zeros_like(o_ref)
        for out0, rows, src0 in runs:
            a0 = out0 // PACKED_ROWS * PACKED_ROWS
            a1 = -(-(out0 + rows) // PACKED_ROWS) * PACKED_ROWS
            window = w_ref[pl.ds(src0 - (out0 - a0), a1 - a0), :]
            row = a0 + lax.broadcasted_iota(jnp.int32, (a1 - a0, 1), 0)
            keep = jnp.logical_and(row >= out0, row < out0 + rows)
            o_ref[a0:a1, :] = jnp.where(keep, window, o_ref[a0:a1, :])

    return pl.pallas_call(
        body, name="in_to_kernel_order", out_shape=jax.ShapeDtypeStruct((IN_PAD, d), by_chip.dtype), grid=(d // tl,),
        in_specs=[pl.BlockSpec((N_CHIPS * IN_SHARD_PAD, tl), lambda j: (0, j))],
        out_specs=pl.BlockSpec((IN_PAD, tl), lambda j: (0, j)),
        compiler_params=_params("parallel"),
    )(by_chip.reshape(N_CHIPS * IN_SHARD_PAD, d))


def _in_from_kernel_order(g_t):
    d = g_t.shape[-1]
    tl = _pick(d, (256, 128))

    def body(g_ref, o_ref):
        row = lax.broadcasted_iota(jnp.int32, (IN_SHARD_PAD, 1), 0)
        for chip in range(N_CHIPS):
            first = chip * IN_SHARD
            runs = []
            for ref0, ker0, count in IN_SEGMENTS:
                lo, hi = max(ref0, first), min(ref0 + count, first + IN_SHARD)
                if lo < hi:
                    runs.append((lo - first, hi - lo, ker0 + lo - ref0))
            val = jnp.zeros((IN_SHARD_PAD, tl), F32)
            patches = []
            for out0, rows, src0 in runs:
                start = src0 - out0
                if 0 <= start <= IN_PAD - IN_SHARD_PAD:
                    window = g_ref[pl.ds(start, IN_SHARD_PAD), :]
                    val = jnp.where(jnp.logical_and(row >= out0, row < out0 + rows), window, val)
                else:
                    patches.append((out0, rows, src0))
            o_ref[chip] = val
            for out0, rows, src0 in patches:
                a0 = out0 // SUBLANES * SUBLANES
                a1 = -(-(out0 + rows) // SUBLANES) * SUBLANES
                window = g_ref[pl.ds(src0 - (out0 - a0), a1 - a0), :]
                keep = jnp.logical_and(row[a0:a1] >= out0, row[a0:a1] < out0 + rows)
                o_ref[chip, a0:a1, :] = jnp.where(keep, window, o_ref[chip, a0:a1, :])

    return pl.pallas_call(
        body, name="in_from_kernel_order", out_shape=jax.ShapeDtypeStruct((N_CHIPS, IN_SHARD_PAD, d), F32), grid=(d // tl,),
        in_specs=[pl.BlockSpec((IN_PAD, tl), lambda j: (0, j))],
        out_specs=pl.BlockSpec((N_CHIPS, IN_SHARD_PAD, tl), lambda j: (0, 0, j)),
        compiler_params=_params("parallel"),
    )(g_t)


PACK_LANES = 3 * GDN_WIDTH
PACKED = dict(mix_pre_norm=(0, 1, 0, D_MODEL), mix_post_norm=(1, 1, 0, D_MODEL), ffn_pre_norm=(2, 1, 0, D_MODEL),
              ffn_post_norm=(3, 1, 0, D_MODEL), a_log=(4, 1, 0, HEADS), dt_bias=(5, 1, 0, HEADS), loss=(6, 1, 0, 1),
              gdn_norm=(7, 1, 0, HEAD_DIM), conv_qkv=(8, GDN_CONV, 0, 3 * GDN_WIDTH), conv_sc=(0, SC_CONV, D_MODEL, SC_WIDTH),
              meta_tokens=(16, N_META, 0, D_MODEL))
PACK_ROWS = 32
SHARDED_SMALL = ("conv_qkv", "conv_sc", "meta_tokens")


def _pack_small(values):
    names = list(PACKED)

    def body(*refs):
        out_ref = refs[-1]
        out_ref[...] = jnp.zeros_like(out_ref)
        for name, ref in zip(names, refs):
            row, rows, lane0, lanes = PACKED[name]
            out_ref[row:row + rows, lane0:lane0 + lanes] = ref[...]

    return pl.pallas_call(body, name="pack_small", out_shape=jax.ShapeDtypeStruct((PACK_ROWS, PACK_LANES), F32))(
        *[values[name] for name in names])


def _sum_devices(packed_all, chip):
    names = list(PACKED)

    def body(chip_ref, all_ref, *rest):
        shard_refs, out_refs = rest[:len(SHARDED_SMALL)], rest[len(SHARDED_SMALL):]

        def total(ref, rows, lanes):
            acc = ref[0, rows, lanes]
            for k in range(1, 8):
                acc = acc + ref[k, rows, lanes]
            return acc

        for name, out in zip(names, out_refs):
            row, rows, lane0, lanes = PACKED[name]
            if name in SHARDED_SMALL:
                out[...] = total(shard_refs[SHARDED_SMALL.index(name)], slice(0, rows), slice(None))
            else:
                out[...] = total(all_ref, slice(row, row + rows), slice(lane0, lane0 + lanes))

    def shard_spec(name):
        row, rows, lane0, lanes = PACKED[name]
        height, width = max(rows, 8), lanes // N_CHIPS
        assert row % height == 0 and lane0 % width == 0
        return pl.BlockSpec((8, height, width), lambda i, chip_ref: (0, row // height, lane0 // width + chip_ref[0]))

    def out_shape(name):
        _, rows, _, lanes = PACKED[name]
        return jax.ShapeDtypeStruct((rows, lanes // N_CHIPS if name in SHARDED_SMALL else lanes), F32)

    whole = lambda shape: pl.BlockSpec(shape, lambda i, chip_ref: (0,) * len(shape))
    outs = pl.pallas_call(
        body, name="sum_devices", out_shape=tuple(out_shape(n) for n in names),
        grid_spec=pltpu.PrefetchScalarGridSpec(
            num_scalar_prefetch=1, grid=(1,),
            in_specs=[whole(packed_all.shape)] + [shard_spec(n) for n in SHARDED_SMALL],
            out_specs=tuple(whole(out_shape(n).shape) for n in names)),
    )(chip, packed_all, *[packed_all] * len(SHARDED_SMALL))
    return dict(zip(names, outs))


def _hbm():
    return pl.BlockSpec(memory_space=pl.ANY)


def _place():
    x, y, c = lax.axis_index("x"), lax.axis_index("y"), lax.axis_index("c")
    chips = ((1 - x, y), (x, 1 - y), (1 - x, 1 - y))
    return x, y, c, chips


def _remote(src, dst, send_sems, recv_sems, k, to):
    return pltpu.make_async_remote_copy(src_ref=src, dst_ref=dst, send_sem=send_sems.at[k], recv_sem=recv_sems.at[k],
                                        device_id=to, device_id_type=MESH)


GATHER_SEMS = 7


def _gather_copies(w_refs, out_refs, send_sems, recv_sems):
    x, y, c, chips = _place()
    mine = 2 * x + y
    sibling = (x, y, 1 - c)
    copy = functools.partial(_remote, send_sems=send_sems, recv_sems=recv_sems)
    direct, landed, passing, from_sibling = [], [], [], []
    for i, (w, o) in enumerate(zip(w_refs, out_refs)):
        k = GATHER_SEMS * i
        direct.append(copy(w, o.at[mine], k=k, to=sibling))
        from_sibling.append(copy(w, o.at[mine], k=k, to=sibling))
        for j, (cx, cy) in enumerate(chips):
            theirs = 2 * cx + cy
            direct.append(copy(w.at[c], o.at[mine, c], k=k + 1 + j, to=(cx, cy, c)))
            landed.append(copy(w.at[c], o.at[theirs, c], k=k + 1 + j, to=sibling))
            passing.append(copy(o.at[theirs, c], o.at[theirs, c], k=k + 4 + j, to=sibling))
            from_sibling.append(copy(w.at[c], o.at[theirs, 1 - c], k=k + 4 + j, to=sibling))
    return direct, landed, passing, from_sibling


def _gather_finish(copies):
    direct, landed, passing, from_sibling = copies
    for arrival, forward in zip(landed, passing):
        arrival.wait_recv()
        forward.start()
    for arrival in from_sibling:
        arrival.wait_recv()
    for cp in direct + passing:
        cp.wait_send()


def _gather_weights(pieces, smalls):
    count, extra = len(pieces), len(smalls)
    total = count + extra

    def body(*refs):
        w_refs, s_refs = refs[:count], refs[count:total]
        out_refs, sall_refs = refs[total:total + count], refs[total + count:2 * total]
        send_sems, recv_sems, local_sems = refs[2 * total:]
        x, y, c, chips = _place()
        mine = 2 * x + y
        own = [pltpu.make_async_copy(s, sall.at[mine], local_sems.at[i]) for i, (s, sall) in enumerate(zip(s_refs, sall_refs))]
        small = [_remote(s, sall.at[mine], send_sems, recv_sems, GATHER_SEMS * count + 3 * i + j, (cx, cy, c))
                 for i, (s, sall) in enumerate(zip(s_refs, sall_refs)) for j, (cx, cy) in enumerate(chips)]
        copies = _gather_copies(w_refs, out_refs, send_sems, recv_sems)
        for cp in own + small + copies[0]:
            cp.start()
        _gather_finish(copies)
        for cp in small:
            cp.wait_recv()
        for cp in small:
            cp.wait_send()
        for cp in own:
            cp.wait()

    sems = GATHER_SEMS * count + 3 * extra
    return pl.pallas_call(
        body, name="gather_weights",
        out_shape=tuple(jax.ShapeDtypeStruct((N_CHIPS,) + p.shape, p.dtype) for p in list(pieces) + list(smalls)),
        in_specs=[_hbm()] * total, out_specs=(_hbm(),) * total,
        scratch_shapes=[pltpu.SemaphoreType.DMA((sems,)), pltpu.SemaphoreType.DMA((sems,)), pltpu.SemaphoreType.DMA((extra,))],
    )(*pieces, *smalls)


def _sibling_copies(g_refs, got_refs, send_sems, recv_sems):
    x, y, c, _ = _place()
    return [_remote(g.at[:, 1 - c], got, send_sems, recv_sems, i, (x, y, 1 - c)) for i, (g, got) in enumerate(zip(g_refs, got_refs))]


def _exchange_siblings(grads):
    count = len(grads)

    def body(*refs):
        copies = _sibling_copies(refs[:count], refs[count:2 * count], *refs[2 * count:])
        for cp in copies:
            cp.start()
        for cp in copies:
            cp.wait_recv()
        for cp in copies:
            cp.wait_send()

    return pl.pallas_call(
        body, name="exchange_siblings",
        out_shape=tuple(jax.ShapeDtypeStruct((g.shape[0],) + g.shape[2:], F32) for g in grads),
        in_specs=[_hbm()] * count, out_specs=(_hbm(),) * count,
        scratch_shapes=[pltpu.SemaphoreType.DMA((count,)), pltpu.SemaphoreType.DMA((count,))],
    )(*grads)


def _chip_copies(p_refs, got_refs, send_sems, recv_sems):
    x, y, c, chips = _place()
    return [_remote(p.at[2 * cx + cy], got.at[j], send_sems, recv_sems, 3 * i + j, (cx, cy, c))
            for i, (p, got) in enumerate(zip(p_refs, got_refs)) for j, (cx, cy) in enumerate(chips)]


def _share_halves(halves, small):
    count = len(halves)

    def body(*refs):
        h_refs, s_ref = refs[:count], refs[count]
        full_refs, sall_ref = refs[count + 1:2 * count + 1], refs[2 * count + 1]
        send_sems, recv_sems, local_sem = refs[2 * count + 2:]
        x, y, c, _ = _place()
        me = 4 * x + 2 * y + c
        own = pltpu.make_async_copy(s_ref, sall_ref.at[me], local_sem)
        own.start()
        copies = [_remote(h.at[c], full.at[c], send_sems, recv_sems, i, (x, y, 1 - c))
                  for i, (h, full) in enumerate(zip(h_refs, full_refs))]
        for k in range(7):
            dx, dy, dc = ((k + 1) >> 2) & 1, ((k + 1) >> 1) & 1, (k + 1) & 1
            peer = (1 - x if dx else x, 1 - y if dy else y, 1 - c if dc else c)
            copies.append(_remote(s_ref, sall_ref.at[me], send_sems, recv_sems, count + k, peer))
        for cp in copies:
            cp.start()
        for cp in copies:
            cp.wait_recv()
        for cp in copies:
            cp.wait_send()
        own.wait()

    return pl.pallas_call(
        body, name="share_halves",
        out_shape=tuple(jax.ShapeDtypeStruct(h.shape, h.dtype) for h in halves) + (jax.ShapeDtypeStruct((8,) + small.shape, F32),),
        in_specs=[_hbm()] * (count + 1), out_specs=(_hbm(),) * (count + 1), input_output_aliases={i: i for i in range(count)},
        scratch_shapes=[pltpu.SemaphoreType.DMA((count + 7,)), pltpu.SemaphoreType.DMA((count + 7,)), pltpu.SemaphoreType.DMA],
    )(*halves, small)


def _add_sibling(grads, gots, core, name):
    count = len(grads)
    chips, _, rows, cols = grads[0].shape

    def body(core_ref, *refs):
        for i in range(count):
            s = refs[i][...] + refs[count + i][...]
            refs[2 * count + 2 * i][...] = s
            refs[2 * count + 2 * i + 1][...] = s.astype(BF16)

    block = pl.BlockSpec((None, rows, cols), lambda p, core_ref: (p, 0, 0))
    own = pl.BlockSpec((None, None, rows, cols), lambda p, core_ref: (p, core_ref[0], 0, 0))
    out = pl.pallas_call(
        body, name="add_sibling_" + name,
        out_shape=(jax.ShapeDtypeStruct((chips, rows, cols), F32), jax.ShapeDtypeStruct((chips, rows, cols), BF16)) * count,
        grid_spec=pltpu.PrefetchScalarGridSpec(
            num_scalar_prefetch=1, grid=(chips,), in_specs=[own] * count + [block] * count, out_specs=(block, block) * count),
        compiler_params=_params("parallel"),
    )(core, *grads, *gots)
    return [(out[2 * i], out[2 * i + 1]) for i in range(count)]


def _add_chips(parts, gots, chip_core, name):
    count = len(parts)
    _, rows, cols = parts[0].shape
    tr = rows // 2 if rows % 32 == 0 else rows

    def body(place_ref, *refs):
        for i in range(count):
            r_ref = refs[count + i]
            refs[2 * count + i][...] = ((refs[i][...] + r_ref[0].astype(F32)) + r_ref[1].astype(F32)) + r_ref[2].astype(F32)

    return pl.pallas_call(
        body, name="add_chips_" + name, out_shape=(jax.ShapeDtypeStruct((2, rows, cols), F32),) * count,
        grid_spec=pltpu.PrefetchScalarGridSpec(
            num_scalar_prefetch=1, grid=(rows // tr,),
            in_specs=[pl.BlockSpec((None, tr, cols), lambda i, place_ref: (place_ref[0], i, 0))] * count
            + [pl.BlockSpec((3, tr, cols), lambda i, place_ref: (0, i, 0))] * count,
            out_specs=(pl.BlockSpec((None, tr, cols), lambda i, place_ref: (place_ref[1], i, 0)),) * count),
        compiler_params=_params("parallel"),
    )(chip_core, *parts, *gots)


def _adamw(w, g, m, v, name):
    rows, cols = w.shape
    tr = _pick(rows, (256, 352, 176, 128, 64, 32, 16, 8))

    def body(w_ref, g_ref, m_ref, v_ref, d_ref, nm_ref, nv_ref):
        d_ref[...], nm_ref[...], nv_ref[...] = _adamw_math(w_ref[...], g_ref[...], m_ref[...], v_ref[...])

    block = pl.BlockSpec((tr, cols), lambda i: (i, 0))
    shape = jax.ShapeDtypeStruct((rows, cols), F32)
    return pl.pallas_call(
        body, name="adamw_" + name, out_shape=(shape, shape, shape), grid=(rows // tr,),
        in_specs=[block] * 4, out_specs=(block,) * 3, compiler_params=_params("parallel"),
    )(w, g, m, v)


def _adamw_math(w, g, m, v):
    m = ADAM_B1 * m + (1.0 - ADAM_B1) * g
    v = ADAM_B2 * v + (1.0 - ADAM_B2) * (g * g)
    m_hat = m / (1.0 - ADAM_B1 ** ADAM_STEP)
    v_hat = v / (1.0 - ADAM_B2 ** ADAM_STEP)
    return -ADAM_LR * (m_hat / (jnp.sqrt(v_hat) + ADAM_EPS) + ADAM_WD * w), m, v


def _adamw_small(ws, gs, ms, vs):
    count = len(ws)

    def body(*refs):
        ins, outs = refs[:4 * count], refs[4 * count:]
        for i in range(count):
            outs[i][...], outs[count + i][...], outs[2 * count + i][...] = _adamw_math(
                ins[i][...], ins[count + i][...], ins[2 * count + i][...], ins[3 * count + i][...])

    shapes = tuple(jax.ShapeDtypeStruct(w.shape, F32) for w in ws)
    out = pl.pallas_call(body, name="adamw_small", out_shape=shapes * 3)(*ws, *gs, *ms, *vs)
    return out[:count], out[count:2 * count], out[2 * count:]


WEIGHTS = ("meta_tokens", "mix_pre_norm", "mix_post_norm", "ffn_pre_norm", "ffn_post_norm", "w_in", "conv_qkv", "a_log",
           "dt_bias", "gdn_norm", "conv_sc", "w_out", "w_gate", "w_up", "w_down")


def kernel(x, meta_tokens, mix_pre_norm, mix_post_norm, ffn_pre_norm, ffn_post_norm, w_in, conv_qkv, a_log, dt_bias, gdn_norm, conv_sc, w_out, w_gate, w_up, w_down, loss_target, m_meta_tokens, m_mix_pre_norm, m_mix_post_norm, m_ffn_pre_norm, m_ffn_post_norm, m_w_in, m_conv_qkv, m_a_log, m_dt_bias, m_gdn_norm, m_conv_sc, m_w_out, m_w_gate, m_w_up, m_w_down, v_meta_tokens, v_mix_pre_norm, v_mix_post_norm, v_ffn_pre_norm, v_ffn_post_norm, v_w_in, v_conv_qkv, v_a_log, v_dt_bias, v_gdn_norm, v_conv_sc, v_w_out, v_w_gate, v_w_up, v_w_down):
    d = x.shape[-1]
    two_d = lambda a: a.reshape(a.shape[-2:])
    weights = dict(zip(WEIGHTS, (meta_tokens, mix_pre_norm, mix_post_norm, ffn_pre_norm, ffn_post_norm, w_in, conv_qkv, a_log,
                                 dt_bias, gdn_norm, conv_sc, w_out, w_gate, w_up, w_down)))
    m_in = dict(zip(WEIGHTS, (m_meta_tokens, m_mix_pre_norm, m_mix_post_norm, m_ffn_pre_norm, m_ffn_post_norm, m_w_in, m_conv_qkv,
                              m_a_log, m_dt_bias, m_gdn_norm, m_conv_sc, m_w_out, m_w_gate, m_w_up, m_w_down)))
    v_in = dict(zip(WEIGHTS, (v_meta_tokens, v_mix_pre_norm, v_mix_post_norm, v_ffn_pre_norm, v_ffn_post_norm, v_w_in, v_conv_qkv,
                              v_a_log, v_dt_bias, v_gdn_norm, v_conv_sc, v_w_out, v_w_gate, v_w_up, v_w_down)))
    core = lax.axis_index("c")
    chip = 2 * lax.axis_index("x") + lax.axis_index("y")
    core_arg = core.reshape(1).astype(jnp.int32)
    chip_core = jnp.stack([chip, core]).astype(jnp.int32)
    whole = lambda a: a.reshape(a.shape[:-3] + (2 * a.shape[-2], d))
    by_rows = lambda n, a: two_d(a).T if n in ("w_in", "w_gate", "w_up") else two_d(a)

    shard = {n: by_rows(n, weights[n]).astype(MXU_DTYPE) for n in MATRICES}
    shard["w_in"] = jnp.pad(shard["w_in"], ((0, IN_SHARD_PAD - IN_SHARD), (0, 0)))
    small_all = _gather_weights([], [two_d(weights[n]) for n in SHARDED_SMALL])
    conv_qkv_full, conv_sc_full, meta_full = (jnp.concatenate([a[p] for p in range(N_CHIPS)], axis=1) for a in small_all)

    sq, grad_x, g, sums = _local_step(
        x, loss_target, meta_full, (mix_pre_norm, mix_post_norm, ffn_pre_norm, ffn_post_norm), _halves(shard["w_in"]),
        conv_qkv_full, a_log, dt_bias, gdn_norm, conv_sc_full, [_halves(shard[n]) for n in LATER], core_arg)

    parts, gots = zip(*sums)
    totals = [_add_chips(parts[i:i + 1], gots[i:i + 1], chip_core, MATRICES[i])[0] for i in range(2)]
    totals += _add_chips(parts[2:], gots[2:], chip_core, "ffn")
    *shared, packed_all = _share_halves(totals, _pack_small(dict(g, loss=sq)))
    grads = {n: whole(a) for n, a in zip(MATRICES, shared)}
    grads["w_in"] = grads["w_in"][:IN_SHARD]
    grads.update(_sum_devices(packed_all, chip.reshape(1).astype(jnp.int32)))
    loss = (0.5 / d) * grads.pop("loss")[0, 0]

    small = [n for n in WEIGHTS if n not in MATRICES]
    updates = dict(zip(small, zip(*_adamw_small(*([by_rows(n, params[n]) for n in small] for params in (weights, grads, m_in, v_in))))))
    outs = [[], [], [], []]
    for n in WEIGHTS:
        shape = weights[n].shape
        if n in MATRICES:
            updates[n] = _adamw(by_rows(n, weights[n]), grads[n], by_rows(n, m_in[n]), by_rows(n, v_in[n]), n)
        for out, a in zip(outs, (grads[n], *updates[n])):
            out.append((a.T if n in ("w_in", "w_gate", "w_up") else a).reshape(shape))
    return (loss, grad_x, *outs[0], *outs[1], *outs[2], *outs[3])
```

```python
import functools

import jax
import jax.numpy as jnp
from jax import lax
from jax.experimental import pallas as pl
from jax.experimental.pallas import tpu as pltpu

F32 = jnp.float32
BF16 = jnp.bfloat16
MXU_DTYPE = jnp.bfloat16
MESH = pl.DeviceIdType.MESH

D_MODEL = 1024
N_META = 16
HEADS = 4
HEAD_DIM = 128
GDN_WIDTH = HEADS * HEAD_DIM
GDN_CONV = 4
CHUNK = 64
SC_WIDTH = D_MODEL - GDN_WIDTH
SC_CONV = 3
D_FF = 2816
IN_WIDTH = 4 * GDN_WIDTH + 2 * HEADS + 3 * SC_WIDTH
IN_PAD = 3840
BA_COL = (4 * GDN_WIDTH + 3 * SC_WIDTH) // 128
EPS = 1e-6
LANES = 128
N_CHIPS = 4
VMEM_LIMIT = 48 * 2 ** 20
MM_VMEM_BUDGET = 42 * 2 ** 20

ADAM_LR = 0.001
ADAM_B1 = 0.9
ADAM_B2 = 0.999
ADAM_EPS = 1e-08
ADAM_WD = 0.01
ADAM_STEP = 10


def _pick(n, candidates):
    for c in candidates:
        if n % c == 0:
            return c
    return n


def _row_tile(n):
    return _pick(n, (352, 256, 176, 128, 64, 32, 16, 8))


def _params(*sem):
    return pltpu.CompilerParams(dimension_semantics=sem, vmem_limit_bytes=VMEM_LIMIT)


def _sigmoid(x):
    return 0.5 * jnp.tanh(0.5 * x) + 0.5


def _softplus(x):
    return jnp.maximum(x, 0.0) + jnp.log(1.0 + jnp.exp(-jnp.abs(x)))


def _dsilu(x, s):
    return s * (1.0 + x * (1.0 - s))


def _mm(a, b, mode, out_dtype, name, exchange=None):
    if mode == "tn":
        k_dim, m_dim = a.shape
    else:
        m_dim, k_dim = a.shape
    n_dim = b.shape[0] if mode == "nt" else b.shape[1]
    tn = _pick(n_dim, (1408, 1280, 1024, 768, 512, 256, 128))
    if mode == "tn":
        tm = _pick(m_dim, (1408, 1280, 1024, 512, 256, 128))
        tk = _pick(k_dim, (2112, 1408, 1280, 1056, 1024, 512, 256, 128))
    else:
        tk = k_dim
        blocks = lambda rows: 2 * (2 * rows * tk + 2 * tk * tn + 4 * rows * tn)
        tm = next((t for t in (2112, 1056, 1024, 704, 512, 256, 128) if m_dim % t == 0 and blocks(t) <= MM_VMEM_BUDGET), m_dim)
    nk = k_dim // tk
    if mode == "nn":
        a_spec = pl.BlockSpec((tm, tk), lambda i, j, k: (i, k))
        b_spec = pl.BlockSpec((tk, tn), lambda i, j, k: (k, j))
        dims = (((1,), (0,)), ((), ()))
    elif mode == "nt":
        a_spec = pl.BlockSpec((tm, tk), lambda i, j, k: (i, k))
        b_spec = pl.BlockSpec((tn, tk), lambda i, j, k: (j, k))
        dims = (((1,), (1,)), ((), ()))
    else:
        a_spec = pl.BlockSpec((tk, tm), lambda i, j, k: (k, i))
        b_spec = pl.BlockSpec((tk, tn), lambda i, j, k: (k, j))
        dims = (((0,), (0,)), ((), ()))

    out_spec = pl.BlockSpec((tm, tn), lambda i, j, k: (i, j))
    grid = (m_dim // tm, n_dim // tn, nk)
    parts = () if exchange is None else tuple(exchange)
    count = len(parts)

    assert out_dtype == F32

    def body(a_ref, b_ref, *rest):
        o_ref = rest[count]
        k = pl.program_id(2)
        step = (pl.program_id(0) * grid[1] + pl.program_id(1)) * nk + k
        if count:
            copies = _chip_copies(rest[:count], rest[count + 1:2 * count + 1], *rest[2 * count + 1:])

            @pl.when(step == 0)
            def _():
                for cp in copies:
                    cp.start()

        p = lax.dot_general(a_ref[...], b_ref[...], dims, preferred_element_type=F32)
        if nk == 1:
            o_ref[...] = p
        else:
            @pl.when(k == 0)
            def _():
                o_ref[...] = p

            @pl.when(k > 0)
            def _():
                o_ref[...] += p

        if count:
            @pl.when(step == grid[0] * grid[1] * nk - 1)
            def _():
                for cp in copies:
                    cp.wait_recv()
                for cp in copies:
                    cp.wait_send()

    out = pl.pallas_call(
        body, name=name,
        out_shape=(jax.ShapeDtypeStruct((m_dim, n_dim), out_dtype),)
        + tuple(jax.ShapeDtypeStruct((3,) + p.shape[1:], p.dtype) for p in parts),
        grid=grid,
        in_specs=[a_spec, b_spec] + [_hbm()] * count,
        out_specs=(out_spec,) + (_hbm(),) * count,
        scratch_shapes=[pltpu.SemaphoreType.DMA((3 * count,)), pltpu.SemaphoreType.DMA((3 * count,))] if count else [],
        compiler_params=_params(*(("arbitrary",) * 3 if count else ("parallel", "parallel", "arbitrary"))),
    )(a, b, *parts)
    return out[0] if not count else out


def _rms_apply(x, w):
    r = lax.rsqrt(jnp.mean(x * x, axis=-1, keepdims=True) + EPS)
    return x * r * w


def _rms_bwd(x, w, dy):
    r = lax.rsqrt(jnp.mean(x * x, axis=-1, keepdims=True) + EPS)
    xh = x * r
    dyw = dy * w
    dx = r * (dyw - xh * jnp.mean(dyw * xh, axis=-1, keepdims=True))
    return dx, jnp.sum(dy * xh, axis=0, keepdims=True)


def _accumulate(ref, first, value):
    @pl.when(first)
    def _():
        ref[...] = value

    @pl.when(jnp.logical_not(first))
    def _():
        ref[...] += value


def _rows(tr, width):
    return pl.BlockSpec((tr, width), lambda i: (i, 0))


def _vec(width):
    return pl.BlockSpec((1, width), lambda i: (0, 0))


def _embed(x, head, w_pre, w_shard, rows_per_seq):
    batch, seq, d = x.shape
    x_offset = head.shape[0]
    tr = _row_tile(rows_per_seq)
    tiles_per_seq = rows_per_seq // tr
    n = batch * rows_per_seq

    def body(x_ref, head_ref, w_ref, ws_ref, h0_ref, u_ref, wall_ref, send_sems, recv_sems):
        gather = _gather_copies([ws_ref], [wall_ref], send_sems, recv_sems)
        i = pl.program_id(0)
        tile = lax.rem(i, tiles_per_seq)

        @pl.when(i == 0)
        def _():
            for cp in gather[0]:
                cp.start()

        rows = jnp.concatenate([head_ref[...], x_ref[0:tr - x_offset, :]], axis=0)
        if tiles_per_seq > 1:
            start = pl.multiple_of(jnp.maximum(tile * tr - x_offset, 0), SUBLANES)
            rows = jnp.where(tile == 0, rows, x_ref[pl.ds(start, tr), :])
        h0_ref[...] = rows
        u_ref[...] = _rms_apply(rows, w_ref[...]).astype(u_ref.dtype)

        @pl.when(i == n // tr - 1)
        def _():
            _gather_finish(gather)

    return pl.pallas_call(
        body, name="embed",
        out_shape=(jax.ShapeDtypeStruct((n, d), F32), jax.ShapeDtypeStruct((n, d), MXU_DTYPE),
                   jax.ShapeDtypeStruct((N_CHIPS,) + w_shard.shape, w_shard.dtype)),
        grid=(n // tr,),
        in_specs=[pl.BlockSpec((None, seq, d), lambda i: (i // tiles_per_seq, 0, 0)),
                  pl.BlockSpec((x_offset, d), lambda i: (0, 0)), _vec(d), _hbm()],
        out_specs=(_rows(tr, d), _rows(tr, d), _hbm()),
        scratch_shapes=[pltpu.SemaphoreType.DMA((GATHER_SEMS,)), pltpu.SemaphoreType.DMA((GATHER_SEMS,))],
        compiler_params=_params("arbitrary"),
    )(x, head, w_pre, w_shard)


def _mix_residual(cat, w_out, h0, w_post, w_pre):
    n, d = h0.shape
    tr = _pick(n, (1056, 1024, 704, 512, 256, 128))

    def body(cat_ref, w_ref, h0_ref, wpost_ref, wpre_ref, mix_ref, h1_ref, u2_ref):
        mix = jnp.dot(cat_ref[...], w_ref[...], preferred_element_type=F32)
        mix_ref[...] = mix
        h1 = h0_ref[...] + _rms_apply(mix, wpost_ref[...])
        h1_ref[...] = h1
        u2_ref[...] = _rms_apply(h1, wpre_ref[...]).astype(u2_ref.dtype)

    wide = jax.ShapeDtypeStruct((n, d), F32)
    return pl.pallas_call(
        body, name="mix_residual", out_shape=(wide, wide, jax.ShapeDtypeStruct((n, d), MXU_DTYPE)), grid=(n // tr,),
        in_specs=[_rows(tr, cat.shape[1]), pl.BlockSpec(w_out.shape, lambda i: (0, 0)), _rows(tr, d), _vec(d), _vec(d)],
        out_specs=(_rows(tr, d), _rows(tr, d), _rows(tr, d)), compiler_params=_params("parallel"),
    )(cat, w_out, h0, w_post, w_pre)


NT_DIMS = (((1,), (1,)), ((), ()))


def _ffn_tiles(n):
    return _pick(n, (1056, 704, 512, 256, 128)), _pick(D_FF, (1408, 256, 128))


def _swiglu_fwd(u, w_gate_t, w_up_t, w_next):
    n, d = u.shape
    tm, tn = _ffn_tiles(n)
    grid = (D_FF // tn, n // tm)

    def body(u_ref, wg_ref, wu_ref, wn_ref, g_ref, up_ref, act_ref, wall_ref, send_sems, recv_sems):
        gather = _gather_copies([wn_ref], [wall_ref], send_sems, recv_sems)
        step = pl.program_id(0) * grid[1] + pl.program_id(1)

        @pl.when(step == 0)
        def _():
            for cp in gather[0]:
                cp.start()

        a = u_ref[...]
        g = lax.dot_general(a, wg_ref[...], NT_DIMS, preferred_element_type=F32)
        up = lax.dot_general(a, wu_ref[...], NT_DIMS, preferred_element_type=F32)
        g_ref[...] = g.astype(g_ref.dtype)
        up_ref[...] = up.astype(up_ref.dtype)
        act_ref[...] = (g * _sigmoid(g) * up).astype(act_ref.dtype)

        @pl.when(step == grid[0] * grid[1] - 1)
        def _():
            _gather_finish(gather)

    tile = pl.BlockSpec((tm, tn), lambda j, i: (i, j))
    weight = pl.BlockSpec((tn, d), lambda j, i: (j, 0))
    wide = jax.ShapeDtypeStruct((n, D_FF), MXU_DTYPE)
    return pl.pallas_call(
        body, name="swiglu_fwd",
        out_shape=(wide, wide, jax.ShapeDtypeStruct((n, D_FF), MXU_DTYPE),
                   jax.ShapeDtypeStruct((N_CHIPS,) + w_next.shape, w_next.dtype)),
        grid=grid,
        in_specs=[pl.BlockSpec((tm, d), lambda j, i: (i, 0)), weight, weight, _hbm()], out_specs=(tile, tile, tile, _hbm()),
        scratch_shapes=[pltpu.SemaphoreType.DMA((GATHER_SEMS,)), pltpu.SemaphoreType.DMA((GATHER_SEMS,))],
        compiler_params=_params("arbitrary", "arbitrary"),
    )(u, w_gate_t, w_up_t, w_next)


def _swiglu_bwd(dffn, w_down, gate, up):
    n, d = dffn.shape
    tm, tn = _ffn_tiles(n)

    def body(dy_ref, w_ref, g_ref, u_ref, dg_ref, du_ref):
        da = lax.dot_general(dy_ref[...], w_ref[...], NT_DIMS, preferred_element_type=F32)
        g = g_ref[...].astype(F32)
        s = _sigmoid(g)
        dg_ref[...] = (da * u_ref[...].astype(F32) * _dsilu(g, s)).astype(dg_ref.dtype)
        du_ref[...] = (da * g * s).astype(du_ref.dtype)

    tile = pl.BlockSpec((tm, tn), lambda j, i: (i, j))
    shape = jax.ShapeDtypeStruct((n, D_FF), MXU_DTYPE)
    return pl.pallas_call(
        body, name="swiglu_bwd", out_shape=(shape, shape), grid=(D_FF // tn, n // tm),
        in_specs=[pl.BlockSpec((tm, d), lambda j, i: (i, 0)), pl.BlockSpec((tn, d), lambda j, i: (j, 0)), tile, tile],
        out_specs=(tile, tile), compiler_params=_params("parallel", "parallel"),
    )(dffn, w_down, gate, up)


def _loss_head(h1, ffn, w_post, target, rows_per_seq, x_offset):
    n, d = h1.shape
    tr = _row_tile(rows_per_seq)
    tiles_per_seq = rows_per_seq // tr
    seq = target.shape[1]

    def seq_rows(t_ref, tile):
        first = jnp.concatenate([jnp.zeros((x_offset, d), F32), t_ref[0:tr - x_offset, :]], axis=0)
        if tiles_per_seq == 1:
            return first
        start = pl.multiple_of(jnp.maximum(tile * tr - x_offset, 0), SUBLANES)
        return jnp.where(tile == 0, first, t_ref[pl.ds(start, tr), :])

    def body(h1_ref, ffn_ref, w_ref, t_ref, dh2_ref, dffn_ref, dw_ref, sq_ref):
        i = pl.program_id(0)
        tile = lax.rem(i, tiles_per_seq)
        w = w_ref[...]
        f = ffn_ref[...]
        r = lax.rsqrt(jnp.mean(f * f, axis=-1, keepdims=True) + EPS)
        fh = f * r
        row = tile * tr + lax.broadcasted_iota(jnp.int32, (tr, 1), 0)
        err = jnp.where(row >= x_offset, h1_ref[...] + fh * w - seq_rows(t_ref, tile), 0.0)
        dh2 = err * (1.0 / d)
        dh2_ref[...] = dh2
        dyw = dh2 * w
        dffn_ref[...] = (r * (dyw - fh * jnp.mean(dyw * fh, axis=-1, keepdims=True))).astype(dffn_ref.dtype)
        _accumulate(dw_ref, i == 0, jnp.sum(dh2 * fh, axis=0, keepdims=True))
        _accumulate(sq_ref, i == 0, jnp.sum(jnp.sum(err * err, axis=1, keepdims=True), axis=0, keepdims=True))

    return pl.pallas_call(
        body, name="loss_head",
        out_shape=(jax.ShapeDtypeStruct((n, d), F32), jax.ShapeDtypeStruct((n, d), MXU_DTYPE),
                   jax.ShapeDtypeStruct((1, d), F32), jax.ShapeDtypeStruct((1, 1), F32)),
        grid=(n // tr,),
        in_specs=[_rows(tr, d), _rows(tr, d), _vec(d), pl.BlockSpec((None, seq, d), lambda i: (i // tiles_per_seq, 0, 0))],
        out_specs=(_rows(tr, d), _rows(tr, d), _vec(d), _vec(1)),
        compiler_params=_params("arbitrary"),
    )(h1, ffn, w_post, target)


def _mid_bwd(h1, mix, w_mix_post, w_ffn_pre, dh2, du2_gate, dup, w_up_t, grads):
    n, d = h1.shape
    tr = _pick(n, (528, 512, 352, 256, 128, 64, 32, 16, 8))
    count = len(grads)

    def body(h1_ref, mix_ref, wpost_ref, wpre_ref, dh2_ref, du2_ref, dup_ref, wup_ref, *rest):
        g_refs, (dh1_ref, dmix_ref, dwpre_ref, dwpost_ref), got_refs = rest[:count], rest[count:count + 4], rest[count + 4:2 * count + 4]
        exchange = _sibling_copies(g_refs, got_refs, *rest[2 * count + 4:])
        i = pl.program_id(0)

        @pl.when(i == 0)
        def _():
            for cp in exchange:
                cp.start()

        du2 = du2_ref[...] + jnp.dot(dup_ref[...], wup_ref[...], preferred_element_type=F32)
        dx, dwpre = _rms_bwd(h1_ref[...], wpre_ref[...], du2)
        dh1 = dh2_ref[...] + dx
        dh1_ref[...] = dh1
        dmix, dwpost = _rms_bwd(mix_ref[...], wpost_ref[...], dh1)
        dmix_ref[...] = dmix.astype(dmix_ref.dtype)
        _accumulate(dwpre_ref, i == 0, dwpre)
        _accumulate(dwpost_ref, i == 0, dwpost)

        @pl.when(i == n // tr - 1)
        def _():
            for cp in exchange:
                cp.wait_recv()
            for cp in exchange:
                cp.wait_send()

    dh1, dmix, dwpre, dwpost, *got = pl.pallas_call(
        body, name="mid_bwd",
        out_shape=(jax.ShapeDtypeStruct((n, d), F32), jax.ShapeDtypeStruct((n, d), MXU_DTYPE),
                   jax.ShapeDtypeStruct((1, d), F32), jax.ShapeDtypeStruct((1, d), F32))
        + tuple(jax.ShapeDtypeStruct((g.shape[0],) + g.shape[2:], F32) for g in grads),
        grid=(n // tr,),
        in_specs=[_rows(tr, d), _rows(tr, d), _vec(d), _vec(d), _rows(tr, d), _rows(tr, d), _rows(tr, dup.shape[1]),
                  pl.BlockSpec(w_up_t.shape, lambda i: (0, 0))] + [_hbm()] * count,
        out_specs=(_rows(tr, d), _rows(tr, d), _vec(d), _vec(d)) + (_hbm(),) * count,
        scratch_shapes=[pltpu.SemaphoreType.DMA((count,)), pltpu.SemaphoreType.DMA((count,))],
        compiler_params=_params("arbitrary"),
    )(h1, mix, w_mix_post, w_ffn_pre, dh2, du2_gate, dup, w_up_t, *grads)
    return dh1, dmix, dwpre, dwpost, got


def _in_bwd(h0, w_pre, dh1, du1, rows_per_seq, pad_rows, x_offset):
    n, d = h0.shape
    tr = _row_tile(rows_per_seq)
    tiles_per_seq = rows_per_seq // tr
    seq = rows_per_seq - x_offset

    def body(h0_ref, w_ref, dh1_ref, du1_ref, gx_ref, dmeta_ref, dw_ref):
        i = pl.program_id(0)
        tile = lax.rem(i, tiles_per_seq)
        dx, dw = _rms_bwd(h0_ref[...], w_ref[...], du1_ref[...])
        dh0 = dh1_ref[...] + dx
        _accumulate(dw_ref, i == 0, dw)

        @pl.when(tile == 0)
        def _():
            gx_ref[0:tr - x_offset, :] = dh0[x_offset:, :]
            _accumulate(dmeta_ref, i == 0, dh0[pad_rows:x_offset, :])

        if tiles_per_seq > 1:
            @pl.when(tile > 0)
            def _():
                gx_ref[pl.ds(pl.multiple_of(tile * tr - x_offset, SUBLANES), tr), :] = dh0

    return pl.pallas_call(
        body, name="in_bwd",
        out_shape=(jax.ShapeDtypeStruct((n // rows_per_seq, seq, d), F32), jax.ShapeDtypeStruct((x_offset - pad_rows, d), F32),
                   jax.ShapeDtypeStruct((1, d), F32)),
        grid=(n // tr,),
        in_specs=[_rows(tr, d), _vec(d), _rows(tr, d), _rows(tr, d)],
        out_specs=(pl.BlockSpec((None, seq, d), lambda i: (i // tiles_per_seq, 0, 0)),
                   pl.BlockSpec((x_offset - pad_rows, d), lambda i: (0, 0)), _vec(d)),
        compiler_params=_params("arbitrary"),
    )(h0, w_pre, dh1, du1)


def _lane_is(lo, hi):
    lane = lax.broadcasted_iota(jnp.int32, (1, LANES), 1)
    return jnp.logical_and(lane >= lo, lane < hi)


def _gates_fwd(proj, a_log_l, dt_bias_l, rows_per_seq, pad_rows):
    n = proj.shape[0]
    tr = _row_tile(rows_per_seq)
    tiles_per_seq = rows_per_seq // tr

    def body(p_ref, a_ref, dt_ref, o_ref):
        x = p_ref[...]
        row = lax.rem(pl.program_id(0), tiles_per_seq) * tr + lax.broadcasted_iota(jnp.int32, (tr, 1), 0)
        g = -jnp.exp(a_ref[...]) * _softplus(x + dt_ref[...])
        val = jnp.where(_lane_is(0, HEADS), _sigmoid(x), jnp.where(_lane_is(HEADS, 2 * HEADS), g, 0.0))
        o_ref[...] = jnp.where(row >= pad_rows, val, 0.0)

    return pl.pallas_call(
        body, name="gates_fwd", out_shape=jax.ShapeDtypeStruct((n, LANES), F32), grid=(n // tr,),
        in_specs=[pl.BlockSpec((tr, LANES), lambda i: (i, BA_COL)), _vec(LANES), _vec(LANES)],
        out_specs=_rows(tr, LANES), compiler_params=_params("parallel"),
    )(proj, a_log_l, dt_bias_l)


def _gates_bwd(proj, dbg, a_log_l, dt_bias_l, rows_per_seq, pad_rows, dproj):
    n = proj.shape[0]
    tr = _row_tile(rows_per_seq)
    tiles_per_seq = rows_per_seq // tr

    def body(p_ref, d_ref, a_ref, dt_ref, _, dx_ref, da_ref, ddt_ref):
        i = pl.program_id(0)
        x = p_ref[...]
        d = d_ref[...]
        row = lax.rem(i, tiles_per_seq) * tr + lax.broadcasted_iota(jnp.int32, (tr, 1), 0)
        live = row >= pad_rows
        beta = _sigmoid(x)
        ea = jnp.exp(a_ref[...])
        xa = x + dt_ref[...]
        g = -ea * _softplus(xa)
        is_g = _lane_is(HEADS, 2 * HEADS)
        d_alogit = jnp.where(jnp.logical_and(live, is_g), d * (-ea) * _sigmoid(xa), 0.0)
        d_blogit = jnp.where(jnp.logical_and(live, _lane_is(0, HEADS)), d * beta * (1.0 - beta), 0.0)
        dx_ref[:, :LANES] = (d_alogit + d_blogit).astype(dx_ref.dtype)
        dx_ref[:, LANES:] = jnp.zeros((tr, LANES), dx_ref.dtype)
        _accumulate(da_ref, i == 0, jnp.sum(jnp.where(jnp.logical_and(live, is_g), d * g, 0.0), axis=0, keepdims=True))
        _accumulate(ddt_ref, i == 0, jnp.sum(d_alogit, axis=0, keepdims=True))

    return pl.pallas_call(
        body, name="gates_bwd",
        out_shape=(jax.ShapeDtypeStruct(dproj.shape, dproj.dtype), jax.ShapeDtypeStruct((1, LANES), F32),
                   jax.ShapeDtypeStruct((1, LANES), F32)),
        grid=(n // tr,),
        in_specs=[pl.BlockSpec((tr, LANES), lambda i: (i, BA_COL)), _rows(tr, LANES), _vec(LANES), _vec(LANES), _hbm()],
        out_specs=(pl.BlockSpec((tr, 2 * LANES), lambda i: (i, BA_COL // 2)), _vec(LANES), _vec(LANES)),
        input_output_aliases={4: 0},
        compiler_params=_params("arbitrary"),
    )(proj, dbg, a_log_l, dt_bias_l, dproj)


HALO = 8


def _halo_scratch(rs):
    return pltpu.VMEM((rs + 2 * HALO, LANES), F32)


def _stage(ref, x):
    rs = x.shape[0]
    ref[0:HALO, :] = jnp.zeros((HALO, LANES), F32)
    ref[HALO + rs:, :] = jnp.zeros((HALO, LANES), F32)
    ref[HALO:HALO + rs, :] = x


def _shifted(ref, k, rs):
    return ref[pl.ds(HALO - k, rs), :]


def _causal_conv(x, x_staged, w, width):
    acc = w[width - 1:width, :] * x
    for i in range(width - 1):
        acc = acc + w[i:i + 1, :] * _shifted(x_staged, width - 1 - i, x.shape[0])
    return acc


def _anti_causal_conv(dy, dy_staged, w, width):
    acc = w[width - 1:width, :] * dy
    for i in range(width - 1):
        acc = acc + w[i:i + 1, :] * _shifted(dy_staged, -(width - 1 - i), dy.shape[0])
    return acc


def _conv_weight_grad(dy, x, x_staged, width):
    taps = [_shifted(x_staged, width - 1 - i, x.shape[0]) for i in range(width - 1)] + [x]
    return jnp.concatenate([jnp.sum(dy * tap, axis=0, keepdims=True) for tap in taps], axis=0)


def _seq_cols(rs, col0, heads):
    return pl.BlockSpec((rs, heads * LANES), lambda j, b: (b, col0 // heads + j))


def _tap_cols(width, col0, heads):
    return pl.BlockSpec((width, heads * LANES), lambda j, b: (0, col0 // heads + j))


def _lanes_of(h):
    return slice(h * LANES, (h + 1) * LANES)


def _qkv_fwd(proj, conv_w, kind, rs):
    n = proj.shape[0]
    col0 = {"q": 0, "k": HEADS, "v": 2 * HEADS}[kind]
    hb = HEADS

    def body(p_ref, w_ref, o_ref, staged):
        for h in range(hb):
            pre = p_ref[:, _lanes_of(h)]
            _stage(staged, pre)
            c = _causal_conv(pre, staged, w_ref[:, _lanes_of(h)], GDN_CONV)
            s = c * _sigmoid(c)
            if kind != "v":
                s = s * lax.rsqrt(jnp.sum(s * s, axis=-1, keepdims=True) + EPS)
            if kind == "q":
                s = s * (HEAD_DIM ** -0.5)
            o_ref[:, _lanes_of(h)] = s

    return pl.pallas_call(
        body, name="qkv_fwd_" + kind, out_shape=jax.ShapeDtypeStruct((n, GDN_WIDTH), F32), grid=(HEADS // hb, n // rs),
        in_specs=[_seq_cols(rs, col0, hb), _tap_cols(GDN_CONV, col0, hb)],
        out_specs=_seq_cols(rs, 0, hb), scratch_shapes=[_halo_scratch(rs)], compiler_params=_params("parallel", "parallel"),
    )(proj, conv_w)


def _qkv_bwd(dy, proj, conv_w, kind, rs, dproj):
    n = proj.shape[0]
    col0 = {"q": 0, "k": HEADS, "v": 2 * HEADS}[kind]
    hb = HEADS

    def body(dy_ref, p_ref, w_ref, _, dp_ref, dw_ref, pre_staged, dc_staged):
        for h in range(hb):
            lanes = _lanes_of(h)
            pre = p_ref[:, lanes]
            w = w_ref[:, lanes]
            _stage(pre_staged, pre)
            c = _causal_conv(pre, pre_staged, w, GDN_CONV)
            sg = _sigmoid(c)
            s = c * sg
            ds = dy_ref[:, lanes]
            if kind == "q":
                ds = ds * (HEAD_DIM ** -0.5)
            if kind != "v":
                r = lax.rsqrt(jnp.sum(s * s, axis=-1, keepdims=True) + EPS)
                sh = s * r
                ds = r * (ds - sh * jnp.sum(ds * sh, axis=-1, keepdims=True))
            dc = ds * _dsilu(c, sg)
            _stage(dc_staged, dc)
            dp_ref[:, lanes] = _anti_causal_conv(dc, dc_staged, w, GDN_CONV).astype(dp_ref.dtype)
            _accumulate(dw_ref.at[:, lanes], pl.program_id(1) == 0, _conv_weight_grad(dc, pre, pre_staged, GDN_CONV))

    return pl.pallas_call(
        body, name="qkv_bwd_" + kind,
        out_shape=(jax.ShapeDtypeStruct(dproj.shape, dproj.dtype), jax.ShapeDtypeStruct((GDN_CONV, GDN_WIDTH), F32)),
        grid=(HEADS // hb, n // rs),
        in_specs=[_seq_cols(rs, 0, hb), _seq_cols(rs, col0, hb), _tap_cols(GDN_CONV, col0, hb), _hbm()],
        out_specs=(_seq_cols(rs, col0, hb), _tap_cols(GDN_CONV, 0, hb)), input_output_aliases={3: 0},
        scratch_shapes=[_halo_scratch(rs), _halo_scratch(rs)],
        compiler_params=_params("parallel", "arbitrary"),
    )(dy, proj, conv_w, dproj)


SC_COL = 4 * HEADS


def _sc_fwd(proj, conv_w, rs, cat):
    n = proj.shape[0]

    hb = 2

    def body(x_ref, b_ref, c_ref, w_ref, _, y_ref, staged):
        for h in range(hb):
            lanes = _lanes_of(h)
            u = c_ref[:, lanes] * x_ref[:, lanes]
            _stage(staged, u)
            y_ref[:, lanes] = (b_ref[:, lanes] * _causal_conv(u, staged, w_ref[:, lanes], SC_CONV)).astype(y_ref.dtype)

    return pl.pallas_call(
        body, name="sc_fwd", out_shape=jax.ShapeDtypeStruct(cat.shape, cat.dtype), grid=(HEADS // hb, n // rs),
        in_specs=[_seq_cols(rs, SC_COL, hb), _seq_cols(rs, SC_COL + 4, hb), _seq_cols(rs, SC_COL + 8, hb),
                  _tap_cols(SC_CONV, 0, hb), _hbm()],
        out_specs=_seq_cols(rs, HEADS, hb), input_output_aliases={4: 0}, scratch_shapes=[_halo_scratch(rs)],
        compiler_params=_params("parallel", "parallel"),
    )(proj, proj, proj, conv_w, cat)


def _sc_bwd(dcat, proj, conv_w, rs, dproj):
    n = proj.shape[0]
    hb = 2

    def body(dy_ref, x_ref, b_ref, c_ref, w_ref, _, dx_ref, db_ref, dc_ref, dw_ref, u_staged, dcv_staged):
        for h in range(hb):
            lanes = _lanes_of(h)
            w = w_ref[:, lanes]
            x = x_ref[:, lanes]
            cc = c_ref[:, lanes]
            u = cc * x
            _stage(u_staged, u)
            dy = dy_ref[:, lanes]
            db_ref[:, lanes] = (dy * _causal_conv(u, u_staged, w, SC_CONV)).astype(db_ref.dtype)
            dcv = dy * b_ref[:, lanes]
            _stage(dcv_staged, dcv)
            du = _anti_causal_conv(dcv, dcv_staged, w, SC_CONV)
            dx_ref[:, lanes] = (du * cc).astype(dx_ref.dtype)
            dc_ref[:, lanes] = (du * x).astype(dc_ref.dtype)
            _accumulate(dw_ref.at[:, lanes], pl.program_id(1) == 0, _conv_weight_grad(dcv, u, u_staged, SC_CONV))

    piece = jax.ShapeDtypeStruct((n, SC_WIDTH), MXU_DTYPE)
    return pl.pallas_call(
        body, name="sc_bwd",
        out_shape=(jax.ShapeDtypeStruct(dproj.shape, dproj.dtype), piece, piece, jax.ShapeDtypeStruct((SC_CONV, SC_WIDTH), F32)),
        grid=(HEADS // hb, n // rs),
        in_specs=[_seq_cols(rs, HEADS, hb), _seq_cols(rs, SC_COL, hb), _seq_cols(rs, SC_COL + 4, hb),
                  _seq_cols(rs, SC_COL + 8, hb), _tap_cols(SC_CONV, 0, hb), _hbm()],
        out_specs=(_seq_cols(rs, SC_COL, hb), _seq_cols(rs, 0, hb), _seq_cols(rs, 0, hb), _tap_cols(SC_CONV, 0, hb)),
        input_output_aliases={5: 0},
        scratch_shapes=[_halo_scratch(rs), _halo_scratch(rs)],
        compiler_params=_params("parallel", "arbitrary"),
    )(dcat, proj, proj, proj, conv_w, dproj)


Z_COL = 3 * HEADS


def _gate_fwd(o, proj, gdn_norm, rs):
    n = proj.shape[0]

    hb = HEADS

    def body(o_ref, z_ref, w_ref, y_ref):
        for h in range(hb):
            lanes = _lanes_of(h)
            z = z_ref[:, lanes]
            y_ref[:, lanes] = (_rms_apply(o_ref[:, lanes], w_ref[...]) * z * _sigmoid(z)).astype(y_ref.dtype)

    return pl.pallas_call(
        body, name="gate_fwd", out_shape=jax.ShapeDtypeStruct((n, D_MODEL), MXU_DTYPE), grid=(HEADS // hb, n // rs),
        in_specs=[_seq_cols(rs, 0, hb), _seq_cols(rs, Z_COL, hb), pl.BlockSpec((1, LANES), lambda j, b: (0, 0))],
        out_specs=_seq_cols(rs, 0, hb), compiler_params=_params("parallel", "parallel"),
    )(o, proj, gdn_norm)


def _gate_bwd(dcat, o, proj, gdn_norm, rs):
    n = proj.shape[0]
    hb = 2

    def body(dy_ref, o_ref, z_ref, w_ref, do_ref, dz_ref, dw_ref):
        w = w_ref[...]
        dw_step = jnp.zeros((1, LANES), F32)
        for h in range(hb):
            lanes = _lanes_of(h)
            z = z_ref[:, lanes]
            o = o_ref[:, lanes]
            dy = dy_ref[:, lanes]
            s = _sigmoid(z)
            dz_ref[:, lanes] = (dy * _rms_apply(o, w) * _dsilu(z, s)).astype(dz_ref.dtype)
            do, dw = _rms_bwd(o, w, dy * z * s)
            do_ref[:, lanes] = do
            dw_step = dw_step + dw
        _accumulate(dw_ref, jnp.logical_and(pl.program_id(0) == 0, pl.program_id(1) == 0), dw_step)

    return pl.pallas_call(
        body, name="gate_bwd",
        out_shape=(jax.ShapeDtypeStruct((n, GDN_WIDTH), F32), jax.ShapeDtypeStruct((n, IN_PAD), MXU_DTYPE),
                   jax.ShapeDtypeStruct((1, LANES), F32)),
        grid=(HEADS // hb, n // rs),
        in_specs=[_seq_cols(rs, 0, hb), _seq_cols(rs, 0, hb), _seq_cols(rs, Z_COL, hb), pl.BlockSpec((1, LANES), lambda j, b: (0, 0))],
        out_specs=(_seq_cols(rs, 0, hb), _seq_cols(rs, Z_COL, hb), pl.BlockSpec((1, LANES), lambda j, b: (0, 0))),
        compiler_params=_params("arbitrary", "arbitrary"),
    )(dcat, o, proj, gdn_norm)


def _dot(a, b):
    return jnp.dot(a.astype(MXU_DTYPE), b.astype(MXU_DTYPE), preferred_element_type=F32)


def _dot_nt(a, b):
    return lax.dot_general(a.astype(MXU_DTYPE), b.astype(MXU_DTYPE), (((1,), (1,)), ((), ())),
                           preferred_element_type=F32)


def _dot_tn(a, b):
    return lax.dot_general(a.astype(MXU_DTYPE), b.astype(MXU_DTYPE), (((0,), (0,)), ((), ())),
                           preferred_element_type=F32)


def _split(x):
    hi = x.astype(MXU_DTYPE)
    return hi, (x - hi.astype(F32)).astype(MXU_DTYPE)


def _dot_split(a, b):
    mm = functools.partial(jnp.dot, preferred_element_type=F32)
    return mm(a[0], b[0]) + (mm(a[0], b[1]) + mm(a[1], b[0]))


def _unit_lower_inverses(mats, eye):
    inv = [eye - a for a in mats]
    power = [_split(a) for a in mats]
    span = 2
    while span < CHUNK:
        power = [_split(_dot_split(p, p)) for p in power]
        inv = [i + _dot_split(_split(i), p) for i, p in zip(inv, power)]
        span *= 2
    return inv


def _chunk_masks():
    ii = lax.broadcasted_iota(jnp.int32, (CHUNK, CHUNK), 0)
    jj = lax.broadcasted_iota(jnp.int32, (CHUNK, CHUNK), 1)
    return ii, jj


def _chunk_decay(g_col, ii, jj):
    incl = ii >= jj
    g_row = jnp.sum(jnp.where(ii == jj, g_col, 0.0), axis=0, keepdims=True)
    gc_col = jnp.sum(jnp.where(incl, g_row, 0.0), axis=1, keepdims=True)
    gc_row = jnp.sum(jnp.where(ii <= jj, g_col, 0.0), axis=0, keepdims=True)
    g_total = jnp.sum(g_row, axis=1, keepdims=True)
    decay = jnp.where(incl, jnp.exp(jnp.where(incl, gc_col - gc_row, 0.0)), 0.0)
    return gc_col, g_total, decay


def _gdn_segments(rs, candidates):
    chunks = rs // CHUNK
    seg_chunks = _pick(chunks, candidates)
    return chunks, seg_chunks, chunks // seg_chunks


def _gdn_fwd(q, k, v, bg, rs, pieces):
    n = q.shape[0]
    batch = n // rs
    chunks, seg_chunks, segs = _gdn_segments(rs, (11, 8, 4, 2))
    seg_rows = seg_chunks * CHUNK
    chains = [(b, h) for b in range(batch) for h in range(HEADS)]
    each = lambda f, *lists: [f(*args) for args in zip(*lists)]
    count = len(pieces)

    def body(q_ref, k_ref, v_ref, bg_ref, *rest):
        w_refs, (o_ref, s_ref, t_ref), out_refs = rest[:count], rest[count:count + 3], rest[count + 3:2 * count + 3]
        state_ref, send_sems, recv_sems = rest[2 * count + 3:]
        gather = _gather_copies(w_refs, out_refs, send_sems, recv_sems)

        @pl.when(pl.program_id(0) == 0)
        def _():
            state_ref[...] = jnp.zeros_like(state_ref)
            for cp in gather[0]:
                cp.start()

        ii, jj = _chunk_masks()
        incl = ii >= jj
        eye = (ii == jj).astype(F32)

        def chunk(c, carry):
            rows = pl.ds(pl.multiple_of(c * CHUNK, CHUNK), CHUNK)
            bgc = [bg_ref[b, rows, :] for b in range(batch)]
            qc = [q_ref[b, rows, _lanes_of(h)] for b, h in chains]
            kc = [k_ref[b, rows, _lanes_of(h)] for b, h in chains]
            vc = [v_ref[b, rows, _lanes_of(h)] for b, h in chains]
            beta = [bgc[b][:, h:h + 1] for b, h in chains]
            state = [state_ref[b, h] for b, h in chains]
            dec = [_chunk_decay(bgc[b][:, HEADS + h:HEADS + h + 1], ii, jj) for b, h in chains]
            gc_col, g_total, decay = ([d[i] for d in dec] for i in range(3))
            kb = each(lambda x, y: x * y, kc, beta)
            a = each(lambda x, y, d: jnp.where(ii > jj, _dot_nt(x, y) * d, 0.0), kb, kc, decay)
            t_inv = _unit_lower_inverses(a, eye)
            eg = [jnp.exp(g) for g in gc_col]
            u = each(lambda t, x, y: _dot(t, x * y), t_inv, vc, beta)
            w = each(lambda t, x, e: _dot(t, x * e), t_inv, kb, eg)
            qk = each(lambda x, y, d: jnp.where(incl, _dot_nt(x, y) * d, 0.0), qc, kc, decay)
            v_new = each(lambda x, y, s: x - _dot(y, s), u, w, state)
            o = each(lambda x, e, s, m, vn: _dot(x * e, s) + _dot(m, vn), qc, eg, state, qk, v_new)
            new_state = each(lambda s, gt, x, g, vn: s * jnp.exp(gt) + _dot_tn(x * jnp.exp(gt - g), vn),
                             state, g_total, kc, gc_col, v_new)
            for i, (b, h) in enumerate(chains):
                s_ref[b, h, c] = state[i]
                t_ref[b, h, c] = t_inv[i]
                o_ref[b, rows, _lanes_of(h)] = o[i]
                state_ref[b, h] = new_state[i]
            return carry

        lax.fori_loop(0, seg_chunks, chunk, 0)

        @pl.when(pl.program_id(0) == segs - 1)
        def _():
            _gather_finish(gather)

    rows_spec = lambda width: pl.BlockSpec((batch, seg_rows, width), lambda s: (0, s, 0))
    per_chunk = lambda r, c: pl.BlockSpec((batch, HEADS, seg_chunks, r, c), lambda s: (0, 0, s, 0, 0))
    as_seqs = lambda a: a.reshape(batch, rs, a.shape[-1])
    sems = GATHER_SEMS * count
    o, states, t_invs, *gathered = pl.pallas_call(
        body, name="gdn_fwd",
        out_shape=(jax.ShapeDtypeStruct((batch, rs, GDN_WIDTH), F32),
                   jax.ShapeDtypeStruct((batch, HEADS, chunks, HEAD_DIM, HEAD_DIM), F32),
                   jax.ShapeDtypeStruct((batch, HEADS, chunks, CHUNK, CHUNK), F32))
        + tuple(jax.ShapeDtypeStruct((N_CHIPS,) + p.shape, p.dtype) for p in pieces),
        grid=(segs,),
        in_specs=[rows_spec(GDN_WIDTH), rows_spec(GDN_WIDTH), rows_spec(GDN_WIDTH), rows_spec(LANES)] + [_hbm()] * count,
        out_specs=(rows_spec(GDN_WIDTH), per_chunk(HEAD_DIM, HEAD_DIM), per_chunk(CHUNK, CHUNK)) + (_hbm(),) * count,
        scratch_shapes=[pltpu.VMEM((batch, HEADS, HEAD_DIM, HEAD_DIM), F32), pltpu.SemaphoreType.DMA((sems,)),
                        pltpu.SemaphoreType.DMA((sems,))],
        compiler_params=_params("arbitrary"),
    )(as_seqs(q), as_seqs(k), as_seqs(v), as_seqs(bg), *pieces)
    return o.reshape(n, GDN_WIDTH), states, t_invs, gathered


def _gdn_bwd(do, q, k, v, bg, states, t_invs, rs, parts):
    n = q.shape[0]
    batch = n // rs
    chunks, seg_chunks, segs = _gdn_segments(rs, (3, 4, 2))
    seg_rows = seg_chunks * CHUNK
    chains = [(b, h) for b in range(batch) for h in range(HEADS)]
    each = lambda f, *lists: [f(*args) for args in zip(*lists)]
    count = len(parts)

    def body(do_ref, q_ref, k_ref, v_ref, bg_ref, s_ref, t_ref, *rest):
        p_refs, (dq_ref, dk_ref, dv_ref, dbg_ref), got_refs = rest[:count], rest[count:count + 4], rest[count + 4:2 * count + 4]
        dstate_ref, send_sems, recv_sems = rest[2 * count + 4:]
        exchange = _chip_copies(p_refs, got_refs, send_sems, recv_sems)

        @pl.when(pl.program_id(0) == 0)
        def _():
            dstate_ref[...] = jnp.zeros_like(dstate_ref)
            for cp in exchange:
                cp.start()

        ii, jj = _chunk_masks()
        incl = ii >= jj
        strict = ii > jj
        lane = lax.broadcasted_iota(jnp.int32, (1, LANES), 1)

        def rowsum(x):
            return jnp.sum(x, axis=1, keepdims=True)

        def total(x):
            return jnp.sum(rowsum(x), axis=0, keepdims=True)

        def chunk(step, carry):
            c = seg_chunks - 1 - step
            rows = pl.ds(pl.multiple_of(c * CHUNK, CHUNK), CHUNK)
            bgc = [bg_ref[b, rows, :] for b in range(batch)]
            qc = [q_ref[b, rows, _lanes_of(h)] for b, h in chains]
            kc = [k_ref[b, rows, _lanes_of(h)] for b, h in chains]
            vc = [v_ref[b, rows, _lanes_of(h)] for b, h in chains]
            doc = [do_ref[b, rows, _lanes_of(h)] for b, h in chains]
            beta = [bgc[b][:, h:h + 1] for b, h in chains]
            state = [s_ref[b, h, c] for b, h in chains]
            t_inv = [t_ref[b, h, c] for b, h in chains]
            d_state = [dstate_ref[b, h] for b, h in chains]
            dec = [_chunk_decay(bgc[b][:, HEADS + h:HEADS + h + 1], ii, jj) for b, h in chains]
            gc_col, g_total, decay = ([d[i] for d in dec] for i in range(3))
            kb = each(lambda x, y: x * y, kc, beta)
            vb = each(lambda x, y: x * y, vc, beta)
            eg = [jnp.exp(g) for g in gc_col]
            kbg = each(lambda x, y: x * y, kb, eg)
            a = each(lambda x, y, d: jnp.where(strict, _dot_nt(x, y) * d, 0.0), kb, kc, decay)
            qk = each(lambda x, y, d: jnp.where(incl, _dot_nt(x, y) * d, 0.0), qc, kc, decay)
            w = each(_dot, t_inv, kbg)
            u = each(_dot, t_inv, vb)
            q_dec = each(lambda x, y: x * y, qc, eg)
            ek = each(lambda gt, g: jnp.exp(gt - g), g_total, gc_col)
            k_dec = each(lambda x, y: x * y, kc, ek)
            g_last = [jnp.exp(gt) for gt in g_total]
            v_new = each(lambda x, y, s: x - _dot(y, s), u, w, state)
            dv_new = each(lambda m, d, x, ds: _dot_tn(m, d) + _dot(x, ds), qk, doc, k_dec, d_state)
            dqk = each(lambda d, vn: jnp.where(incl, _dot_nt(d, vn), 0.0), doc, v_new)
            dq_dec = each(_dot_nt, doc, state)
            dk_dec = each(_dot_nt, v_new, d_state)
            dg_last = each(lambda s, ds: total(s * ds), state, d_state)
            new_d_state = each(lambda x, d, gl, ds, y, dvn: _dot_tn(x, d) + gl * ds - _dot_tn(y, dvn),
                               q_dec, doc, g_last, d_state, w, dv_new)
            dw = each(lambda dvn, s: -_dot_nt(dvn, s), dv_new, state)
            dt = each(lambda dvn, x, y, z: _dot_nt(dvn, x) + _dot_nt(y, z), dv_new, vb, dw, kbg)
            dvb = each(_dot_tn, t_inv, dv_new)
            dkbg = each(_dot_tn, t_inv, dw)
            t_dt = each(_dot_tn, t_inv, dt)
            da = each(lambda x, t: -jnp.where(strict, _dot_nt(x, t), 0.0), t_dt, t_inv)
            dm_a = each(lambda x, y: x * y, da, decay)
            dm_qk = each(lambda x, y: x * y, dqk, decay)
            e = each(lambda x, y, z, t: x * y + z * t, da, a, dqk, qk)
            dkb = each(lambda m, x, y, z: _dot(m, x) + y * z, dm_a, kc, dkbg, eg)
            dk = each(lambda m, x, m2, y, z, t, p, bt: _dot_tn(m, x) + _dot_tn(m2, y) + z * t + p * bt,
                      dm_a, kb, dm_qk, qc, dk_dec, ek, dkb, beta)
            dq = each(lambda m, x, y, z: _dot(m, x) + y * z, dm_qk, kc, dq_dec, eg)
            dbeta = each(lambda x, y, z, t: rowsum(x * y + z * t), dkb, kc, dvb, vc)
            dgc = each(lambda x, p, pd, r, rd, s, sd: rowsum(x) - rowsum(jnp.where(ii == jj, jnp.sum(x, axis=0, keepdims=True), 0.0))
                       + rowsum(p * pd - r * rd + s * sd), e, dq_dec, q_dec, dk_dec, k_dec, dkbg, kbg)
            d_total = each(lambda r, rd, x, gl: total(r * rd) + x * gl, dk_dec, k_dec, dg_last, g_last)
            dg = each(lambda x, t: rowsum(jnp.where(jj >= ii, jnp.sum(jnp.where(ii == jj, x, 0.0), axis=0, keepdims=True), 0.0)) + t,
                      dgc, d_total)
            dbg = [jnp.zeros((CHUNK, LANES), F32) for _ in range(batch)]
            for i, (b, h) in enumerate(chains):
                dstate_ref[b, h] = new_d_state[i]
                dk_ref[b, rows, _lanes_of(h)] = dk[i]
                dq_ref[b, rows, _lanes_of(h)] = dq[i]
                dv_ref[b, rows, _lanes_of(h)] = dvb[i] * beta[i]
                dbg[b] = dbg[b] + jnp.where(lane == h, dbeta[i], 0.0) + jnp.where(lane == HEADS + h, dg[i], 0.0)
            for b in range(batch):
                dbg_ref[b, rows, :] = dbg[b]
            return carry

        lax.fori_loop(0, seg_chunks, chunk, 0)

        @pl.when(pl.program_id(0) == segs - 1)
        def _():
            for cp in exchange:
                cp.wait_recv()
            for cp in exchange:
                cp.wait_send()

    rows_spec = lambda width: pl.BlockSpec((batch, seg_rows, width), lambda s: (0, segs - 1 - s, 0))
    per_chunk = lambda r, c: pl.BlockSpec((batch, HEADS, seg_chunks, r, c), lambda s: (0, 0, segs - 1 - s, 0, 0))
    as_seqs = lambda a: a.reshape(batch, rs, a.shape[-1])
    grad = jax.ShapeDtypeStruct((batch, rs, GDN_WIDTH), F32)
    wide = rows_spec(GDN_WIDTH)
    dq, dk, dv, dbg, *got = pl.pallas_call(
        body, name="gdn_bwd",
        out_shape=(grad, grad, grad, jax.ShapeDtypeStruct((batch, rs, LANES), F32))
        + tuple(jax.ShapeDtypeStruct((3,) + p.shape[1:], p.dtype) for p in parts),
        grid=(segs,),
        in_specs=[wide, wide, wide, wide, rows_spec(LANES), per_chunk(HEAD_DIM, HEAD_DIM), per_chunk(CHUNK, CHUNK)]
        + [_hbm()] * count,
        out_specs=(wide, wide, wide, rows_spec(LANES)) + (_hbm(),) * count,
        scratch_shapes=[pltpu.VMEM((batch, HEADS, HEAD_DIM, HEAD_DIM), F32), pltpu.SemaphoreType.DMA((3 * count,)),
                        pltpu.SemaphoreType.DMA((3 * count,))],
        compiler_params=_params("arbitrary"),
    )(as_seqs(do), as_seqs(q), as_seqs(k), as_seqs(v), as_seqs(bg), states, t_invs, *parts)
    return dq.reshape(n, GDN_WIDTH), dk.reshape(n, GDN_WIDTH), dv.reshape(n, GDN_WIDTH), dbg.reshape(n, LANES), got


def _lane_vec(vals, offset):
    k = vals.shape[1]
    return jnp.pad(vals, ((0, 0), (offset, LANES - offset - k)))


LATER = ("w_out", "w_gate", "w_up", "w_down")


def _halves(a):
    return a.reshape(a.shape[:-2] + (2, a.shape[-2] // 2, a.shape[-1]))


def _local_step(x, target, meta, norms, w_in_shard, conv_qkv, a_log, dt_bias, gdn_norm, conv_sc, later_shards, core_arg):
    batch, seq, d = x.shape
    tokens = N_META + seq
    pad_rows = (-tokens) % CHUNK
    rs = tokens + pad_rows
    x_offset = pad_rows + N_META
    n = batch * rs
    w_mix_pre, w_mix_post, w_ffn_pre, w_ffn_post = norms

    head = jnp.concatenate([jnp.zeros((pad_rows, d), F32), meta], axis=0)
    a_log_l = _lane_vec(a_log, HEADS)
    dt_bias_l = _lane_vec(dt_bias, HEADS)

    h0, u1, w_in_all = _embed(x, head, w_mix_pre, w_in_shard, rs)
    w_in_t = _in_to_kernel_order(w_in_all.reshape(N_CHIPS, -1, d))
    proj = _mm(u1, w_in_t, "nt", F32, "mm_proj")
    q = _qkv_fwd(proj, conv_qkv, "q", rs)
    k = _qkv_fwd(proj, conv_qkv, "k", rs)
    v = _qkv_fwd(proj, conv_qkv, "v", rs)
    bg = _gates_fwd(proj, a_log_l, dt_bias_l, rs, pad_rows)
    o, states, t_invs, gathered = _gdn_fwd(q, k, v, bg, rs, later_shards[:3])
    w_out, w_gate_t, w_up_t = (a.reshape(-1, d) for a in gathered)
    cat = _sc_fwd(proj, conv_sc, rs, _gate_fwd(o, proj, gdn_norm, rs))
    mix, h1, u2 = _mix_residual(cat, w_out, h0, w_mix_post, w_ffn_pre)
    gate, up, act, w_down = _swiglu_fwd(u2, w_gate_t, w_up_t, later_shards[3])
    w_down = w_down.reshape(-1, d)
    ffn = _mm(act, w_down, "nn", F32, "mm_down")

    dh2, dffn, d_ffn_post, sq = _loss_head(h1, ffn, w_ffn_post, target, rs, x_offset)
    d_w_down = _mm(act, dffn, "tn", F32, "mm_dw_down")
    dgate, dup = _swiglu_bwd(dffn, w_down, gate, up)
    d_w_gate_t = _mm(dgate, u2, "tn", F32, "mm_dw_gate")
    d_w_up_t = _mm(dup, u2, "tn", F32, "mm_dw_up")
    du2_gate = _mm(dgate, w_gate_t, "nn", F32, "mm_du2_gate")
    by_chip = [_halves(g.reshape(N_CHIPS, -1, d)) for g in (d_w_gate_t, d_w_up_t, d_w_down)]
    dh1, dmix, d_ffn_pre, d_mix_post, got_sibling = _mid_bwd(h1, mix, w_mix_post, w_ffn_pre, dh2, du2_gate, dup, w_up_t, by_chip)
    dcat = _mm(dmix, w_out, "nt", F32, "mm_dcat")
    d_w_out = _halves(_mm(cat, dmix, "tn", F32, "mm_dw_out").reshape(N_CHIPS, -1, d))
    sums = (_add_sibling([d_w_out], _exchange_siblings([d_w_out]), core_arg, "w_out")
            + _add_sibling(by_chip, got_sibling, core_arg, "ffn"))
    do, dproj, d_gdn_norm = _gate_bwd(dcat, o, proj, gdn_norm, rs)
    dproj, dscb, dscc, d_conv_sc = _sc_bwd(dcat, proj, conv_sc, rs, dproj)
    dq, dk, dv, dbg, got_chips = _gdn_bwd(do, q, k, v, bg, states, t_invs, rs, [send for _, send in sums[:3]])
    dproj, dwq = _qkv_bwd(dq, proj, conv_qkv, "q", rs, dproj)
    dproj, dwk = _qkv_bwd(dk, proj, conv_qkv, "k", rs, dproj)
    dproj, dwv = _qkv_bwd(dv, proj, conv_qkv, "v", rs, dproj)
    d_conv_qkv = jnp.concatenate([dwq, dwk, dwv], axis=1)
    dproj, d_a_log_l, d_dt_bias_l = _gates_bwd(proj, dbg, a_log_l, dt_bias_l, rs, pad_rows, dproj)
    dproj = lax.dynamic_update_slice(dproj, dscb, (0, (SC_COL + HEADS) * LANES))
    dproj = lax.dynamic_update_slice(dproj, dscc, (0, (SC_COL + 2 * HEADS) * LANES))
    d_w_in_t, got_down = _mm(dproj, u1, "tn", F32, "mm_dw_in", exchange=[sums[3][1]])
    got_chips.append(got_down)
    g_in = _halves(_in_from_kernel_order(d_w_in_t))
    sums = _add_sibling([g_in], _exchange_siblings([g_in]), core_arg, "w_in") + sums
    du1, got_in = _mm(dproj, w_in_t, "nn", F32, "mm_du1", exchange=[sums[0][1]])
    got_chips.insert(0, got_in)
    grad_x, d_meta, d_mix_pre = _in_bwd(h0, w_mix_pre, dh1, du1, rs, pad_rows, x_offset)

    grads = dict(
        meta_tokens=d_meta,
        mix_pre_norm=d_mix_pre, mix_post_norm=d_mix_post, ffn_pre_norm=d_ffn_pre, ffn_post_norm=d_ffn_post,
        conv_qkv=d_conv_qkv,
        a_log=d_a_log_l[:, HEADS:2 * HEADS], dt_bias=d_dt_bias_l[:, HEADS:2 * HEADS],
        gdn_norm=d_gdn_norm, conv_sc=d_conv_sc,
    )
    return sq, grad_x, grads, [(part, got) for (part, _), got in zip(sums, got_chips)]


MATRICES = ("w_in", "w_out", "w_gate", "w_up", "w_down")
IN_SHARD = IN_WIDTH // N_CHIPS
IN_SHARD_PAD = 928


IN_SEGMENTS = ((0, 0, 4 * GDN_WIDTH), (4 * GDN_WIDTH, IN_WIDTH - 2 * HEADS, 2 * HEADS),
               (4 * GDN_WIDTH + 2 * HEADS, 4 * GDN_WIDTH, 3 * SC_WIDTH))
SUBLANES = 8
PACKED_ROWS = 16


def _in_to_kernel_order(by_chip):
    d = by_chip.shape[-1]
    tl = _pick(d, (256, 128))
    runs = []
    for ref0, ker0, count in IN_SEGMENTS:
        row = ref0
        while row < ref0 + count:
            chip, at = divmod(row, IN_SHARD)
            take = min(ref0 + count - row, IN_SHARD - at)
            runs.append((ker0 + row - ref0, take, chip * IN_SHARD_PAD + at))
            row += take

    def body(w_ref, o_ref):
        o_ref[...] = jnp.zeros_like(o_ref)
        for out0, rows, src0 in runs:
            a0 = out0 // PACKED_ROWS * PACKED_ROWS
            a1 = -(-(out0 + rows) // PACKED_ROWS) * PACKED_ROWS
            window = w_ref[pl.ds(src0 - (out0 - a0), a1 - a0), :]
            row = a0 + lax.broadcasted_iota(jnp.int32, (a1 - a0, 1), 0)
            keep = jnp.logical_and(row >= out0, row < out0 + rows)
            o_ref[a0:a1, :] = jnp.where(keep, window, o_ref[a0:a1, :])

    return pl.pallas_call(
        body, name="in_to_kernel_order", out_shape=jax.ShapeDtypeStruct((IN_PAD, d), by_chip.dtype), grid=(d // tl,),
        in_specs=[pl.BlockSpec((N_CHIPS * IN_SHARD_PAD, tl), lambda j: (0, j))],
        out_specs=pl.BlockSpec((IN_PAD, tl), lambda j: (0, j)),
        compiler_params=_params("parallel"),
    )(by_chip.reshape(N_CHIPS * IN_SHARD_PAD, d))


def _in_from_kernel_order(g_t):
    d = g_t.shape[-1]
    tl = _pick(d, (256, 128))

    def body(g_ref, o_ref):
        row = lax.broadcasted_iota(jnp.int32, (IN_SHARD_PAD, 1), 0)
        for chip in range(N_CHIPS):
            first = chip * IN_SHARD
            runs = []
            for ref0, ker0, count in IN_SEGMENTS:
                lo, hi = max(ref0, first), min(ref0 + count, first + IN_SHARD)
                if lo < hi:
                    runs.append((lo - first, hi - lo, ker0 + lo - ref0))
            val = jnp.zeros((IN_SHARD_PAD, tl), F32)
            patches = []
            for out0, rows, src0 in runs:
                start = src0 - out0
                if 0 <= start <= IN_PAD - IN_SHARD_PAD:
                    window = g_ref[pl.ds(start, IN_SHARD_PAD), :]
                    val = jnp.where(jnp.logical_and(row >= out0, row < out0 + rows), window, val)
                else:
                    patches.append((out0, rows, src0))
            o_ref[chip] = val
            for out0, rows, src0 in patches:
                a0 = out0 // SUBLANES * SUBLANES
                a1 = -(-(out0 + rows) // SUBLANES) * SUBLANES
                window = g_ref[pl.ds(src0 - (out0 - a0), a1 - a0), :]
                keep = jnp.logical_and(row[a0:a1] >= out0, row[a0:a1] < out0 + rows)
                o_ref[chip, a0:a1, :] = jnp.where(keep, window, o_ref[chip, a0:a1, :])

    return pl.pallas_call(
        body, name="in_from_kernel_order", out_shape=jax.ShapeDtypeStruct((N_CHIPS, IN_SHARD_PAD, d), F32), grid=(d // tl,),
        in_specs=[pl.BlockSpec((IN_PAD, tl), lambda j: (0, j))],
        out_specs=pl.BlockSpec((N_CHIPS, IN_SHARD_PAD, tl), lambda j: (0, 0, j)),
        compiler_params=_params("parallel"),
    )(g_t)


PACK_LANES = 3 * GDN_WIDTH
PACKED = dict(mix_pre_norm=(0, 1, 0, D_MODEL), mix_post_norm=(1, 1, 0, D_MODEL), ffn_pre_norm=(2, 1, 0, D_MODEL),
              ffn_post_norm=(3, 1, 0, D_MODEL), a_log=(4, 1, 0, HEADS), dt_bias=(5, 1, 0, HEADS), loss=(6, 1, 0, 1),
              gdn_norm=(7, 1, 0, HEAD_DIM), conv_qkv=(8, GDN_CONV, 0, 3 * GDN_WIDTH), conv_sc=(0, SC_CONV, D_MODEL, SC_WIDTH),
              meta_tokens=(16, N_META, 0, D_MODEL))
PACK_ROWS = 32
SHARDED_SMALL = ("conv_qkv", "conv_sc", "meta_tokens")


def _pack_small(values):
    names = list(PACKED)

    def body(*refs):
        out_ref = refs[-1]
        out_ref[...] = jnp.zeros_like(out_ref)
        for name, ref in zip(names, refs):
            row, rows, lane0, lanes = PACKED[name]
            out_ref[row:row + rows, lane0:lane0 + lanes] = ref[...]

    return pl.pallas_call(body, name="pack_small", out_shape=jax.ShapeDtypeStruct((PACK_ROWS, PACK_LANES), F32))(
        *[values[name] for name in names])


def _sum_devices(packed_all, chip):
    names = list(PACKED)

    def body(chip_ref, all_ref, *rest):
        shard_refs, out_refs = rest[:len(SHARDED_SMALL)], rest[len(SHARDED_SMALL):]

        def total(ref, rows, lanes):
            acc = ref[0, rows, lanes]
            for k in range(1, N_CHIPS):
                acc = acc + ref[k, rows, lanes]
            return acc

        for name, out in zip(names, out_refs):
            row, rows, lane0, lanes = PACKED[name]
            if name in SHARDED_SMALL:
                out[...] = total(shard_refs[SHARDED_SMALL.index(name)], slice(0, rows), slice(None))
            else:
                out[...] = total(all_ref, slice(row, row + rows), slice(lane0, lane0 + lanes))

    def shard_spec(name):
        row, rows, lane0, lanes = PACKED[name]
        height, width = max(rows, 8), lanes // N_CHIPS
        assert row % height == 0 and lane0 % width == 0
        return pl.BlockSpec((N_CHIPS, height, width), lambda i, chip_ref: (0, row // height, lane0 // width + chip_ref[0]))

    def out_shape(name):
        _, rows, _, lanes = PACKED[name]
        return jax.ShapeDtypeStruct((rows, lanes // N_CHIPS if name in SHARDED_SMALL else lanes), F32)

    whole = lambda shape: pl.BlockSpec(shape, lambda i, chip_ref: (0,) * len(shape))
    outs = pl.pallas_call(
        body, name="sum_devices", out_shape=tuple(out_shape(n) for n in names),
        grid_spec=pltpu.PrefetchScalarGridSpec(
            num_scalar_prefetch=1, grid=(1,),
            in_specs=[whole(packed_all.shape)] + [shard_spec(n) for n in SHARDED_SMALL],
            out_specs=tuple(whole(out_shape(n).shape) for n in names)),
    )(chip, packed_all, *[packed_all] * len(SHARDED_SMALL))
    return dict(zip(names, outs))


def _hbm():
    return pl.BlockSpec(memory_space=pl.ANY)


def _place():
    x, y, c = lax.axis_index("x"), lax.axis_index("y"), lax.axis_index("c")
    chips = ((1 - x, y), (x, 1 - y), (1 - x, 1 - y))
    return x, y, c, chips


def _remote(src, dst, send_sems, recv_sems, k, to):
    return pltpu.make_async_remote_copy(src_ref=src, dst_ref=dst, send_sem=send_sems.at[k], recv_sem=recv_sems.at[k],
                                        device_id=to, device_id_type=MESH)


GATHER_SEMS = 7


def _gather_copies(w_refs, out_refs, send_sems, recv_sems):
    x, y, c, chips = _place()
    mine = 2 * x + y
    sibling = (x, y, 1 - c)
    copy = functools.partial(_remote, send_sems=send_sems, recv_sems=recv_sems)
    direct, landed, passing, from_sibling = [], [], [], []
    for i, (w, o) in enumerate(zip(w_refs, out_refs)):
        k = GATHER_SEMS * i
        direct.append(copy(w, o.at[mine], k=k, to=sibling))
        from_sibling.append(copy(w, o.at[mine], k=k, to=sibling))
        for j, (cx, cy) in enumerate(chips):
            theirs = 2 * cx + cy
            direct.append(copy(w.at[c], o.at[mine, c], k=k + 1 + j, to=(cx, cy, c)))
            landed.append(copy(w.at[c], o.at[theirs, c], k=k + 1 + j, to=sibling))
            passing.append(copy(o.at[theirs, c], o.at[theirs, c], k=k + 4 + j, to=sibling))
            from_sibling.append(copy(w.at[c], o.at[theirs, 1 - c], k=k + 4 + j, to=sibling))
    return direct, landed, passing, from_sibling


def _gather_finish(copies):
    direct, landed, passing, from_sibling = copies
    for arrival, forward in zip(landed, passing):
        arrival.wait_recv()
        forward.start()
    for arrival in from_sibling:
        arrival.wait_recv()
    for cp in direct + passing:
        cp.wait_send()


def _gather_weights(pieces, smalls):
    count, extra = len(pieces), len(smalls)
    total = count + extra

    def body(*refs):
        w_refs, s_refs = refs[:count], refs[count:total]
        out_refs, sall_refs = refs[total:total + count], refs[total + count:2 * total]
        send_sems, recv_sems, local_sems = refs[2 * total:]
        x, y, c, chips = _place()
        mine = 2 * x + y
        own = [pltpu.make_async_copy(s, sall.at[mine], local_sems.at[i]) for i, (s, sall) in enumerate(zip(s_refs, sall_refs))]
        small = [_remote(s, sall.at[mine], send_sems, recv_sems, GATHER_SEMS * count + 3 * i + j, (cx, cy, c))
                 for i, (s, sall) in enumerate(zip(s_refs, sall_refs)) for j, (cx, cy) in enumerate(chips)]
        copies = _gather_copies(w_refs, out_refs, send_sems, recv_sems)
        for cp in own + small + copies[0]:
            cp.start()
        _gather_finish(copies)
        for cp in small:
            cp.wait_recv()
        for cp in small:
            cp.wait_send()
        for cp in own:
            cp.wait()

    sems = GATHER_SEMS * count + 3 * extra
    return pl.pallas_call(
        body, name="gather_weights",
        out_shape=tuple(jax.ShapeDtypeStruct((N_CHIPS,) + p.shape, p.dtype) for p in list(pieces) + list(smalls)),
        in_specs=[_hbm()] * total, out_specs=(_hbm(),) * total,
        scratch_shapes=[pltpu.SemaphoreType.DMA((sems,)), pltpu.SemaphoreType.DMA((sems,)), pltpu.SemaphoreType.DMA((extra,))],
    )(*pieces, *smalls)


def _sibling_copies(g_refs, got_refs, send_sems, recv_sems):
    x, y, c, _ = _place()
    return [_remote(g.at[:, 1 - c], got, send_sems, recv_sems, i, (x, y, 1 - c)) for i, (g, got) in enumerate(zip(g_refs, got_refs))]


def _exchange_siblings(grads):
    count = len(grads)

    def body(*refs):
        copies = _sibling_copies(refs[:count], refs[count:2 * count], *refs[2 * count:])
        for cp in copies:
            cp.start()
        for cp in copies:
            cp.wait_recv()
        for cp in copies:
            cp.wait_send()

    return pl.pallas_call(
        body, name="exchange_siblings",
        out_shape=tuple(jax.ShapeDtypeStruct((g.shape[0],) + g.shape[2:], F32) for g in grads),
        in_specs=[_hbm()] * count, out_specs=(_hbm(),) * count,
        scratch_shapes=[pltpu.SemaphoreType.DMA((count,)), pltpu.SemaphoreType.DMA((count,))],
    )(*grads)


def _chip_copies(p_refs, got_refs, send_sems, recv_sems):
    x, y, c, chips = _place()
    return [_remote(p.at[2 * cx + cy], got.at[j], send_sems, recv_sems, 3 * i + j, (cx, cy, c))
            for i, (p, got) in enumerate(zip(p_refs, got_refs)) for j, (cx, cy) in enumerate(chips)]


def _share_halves(halves, small):
    count = len(halves)

    def body(*refs):
        h_refs, s_ref = refs[:count], refs[count]
        full_refs, sall_ref = refs[count + 1:2 * count + 1], refs[2 * count + 1]
        send_sems, recv_sems, local_sems, mine_v, theirs_v, pair_v = refs[2 * count + 2:]
        x, y, c, chips = _place()
        mine = 2 * x + y
        sibling = (x, y, 1 - c)
        swaps = [_remote(h.at[c], full.at[c], send_sems, recv_sems, i, sibling) for i, (h, full) in enumerate(zip(h_refs, full_refs))]
        swaps.append(_remote(s_ref, theirs_v, send_sems, recv_sems, count, sibling))
        load = pltpu.make_async_copy(s_ref, mine_v, local_sems.at[0])
        load.start()
        for cp in swaps:
            cp.start()
        load.wait()
        swaps[count].wait_recv()
        pair_v[...] = mine_v[...] + theirs_v[...]
        store = pltpu.make_async_copy(pair_v, sall_ref.at[mine], local_sems.at[1])
        store.start()
        spread = [_remote(pair_v, sall_ref.at[mine], send_sems, recv_sems, count + 1 + j, (cx, cy, c))
                  for j, (cx, cy) in enumerate(chips)]
        for cp in spread:
            cp.start()
        for cp in spread + swaps[:count]:
            cp.wait_recv()
        for cp in spread + swaps:
            cp.wait_send()
        store.wait()

    return pl.pallas_call(
        body, name="share_halves",
        out_shape=tuple(jax.ShapeDtypeStruct(h.shape, h.dtype) for h in halves)
        + (jax.ShapeDtypeStruct((N_CHIPS,) + small.shape, F32),),
        in_specs=[_hbm()] * (count + 1), out_specs=(_hbm(),) * (count + 1), input_output_aliases={i: i for i in range(count)},
        scratch_shapes=[pltpu.SemaphoreType.DMA((count + 4,)), pltpu.SemaphoreType.DMA((count + 4,)), pltpu.SemaphoreType.DMA((2,))]
        + [pltpu.VMEM(small.shape, F32)] * 3,
    )(*halves, small)


def _add_sibling(grads, gots, core, name):
    count = len(grads)
    chips, _, rows, cols = grads[0].shape

    def body(core_ref, *refs):
        for i in range(count):
            s = refs[i][...] + refs[count + i][...]
            refs[2 * count + 2 * i][...] = s
            refs[2 * count + 2 * i + 1][...] = s.astype(BF16)

    block = pl.BlockSpec((None, rows, cols), lambda p, core_ref: (p, 0, 0))
    own = pl.BlockSpec((None, None, rows, cols), lambda p, core_ref: (p, core_ref[0], 0, 0))
    out = pl.pallas_call(
        body, name="add_sibling_" + name,
        out_shape=(jax.ShapeDtypeStruct((chips, rows, cols), F32), jax.ShapeDtypeStruct((chips, rows, cols), BF16)) * count,
        grid_spec=pltpu.PrefetchScalarGridSpec(
            num_scalar_prefetch=1, grid=(chips,), in_specs=[own] * count + [block] * count, out_specs=(block, block) * count),
        compiler_params=_params("parallel"),
    )(core, *grads, *gots)
    return [(out[2 * i], out[2 * i + 1]) for i in range(count)]


def _add_chips(parts, gots, chip_core, name):
    count = len(parts)
    _, rows, cols = parts[0].shape
    tr = rows // 2 if rows % 32 == 0 else rows

    def body(place_ref, *refs):
        for i in range(count):
            r_ref = refs[count + i]
            refs[2 * count + i][...] = ((refs[i][...] + r_ref[0].astype(F32)) + r_ref[1].astype(F32)) + r_ref[2].astype(F32)

    return pl.pallas_call(
        body, name="add_chips_" + name, out_shape=(jax.ShapeDtypeStruct((2, rows, cols), F32),) * count,
        grid_spec=pltpu.PrefetchScalarGridSpec(
            num_scalar_prefetch=1, grid=(rows // tr,),
            in_specs=[pl.BlockSpec((None, tr, cols), lambda i, place_ref: (place_ref[0], i, 0))] * count
            + [pl.BlockSpec((3, tr, cols), lambda i, place_ref: (0, i, 0))] * count,
            out_specs=(pl.BlockSpec((None, tr, cols), lambda i, place_ref: (place_ref[1], i, 0)),) * count),
        compiler_params=_params("parallel"),
    )(chip_core, *parts, *gots)


def _adamw(w, g, m, v, name):
    rows, cols = w.shape
    tr = _pick(rows, (256, 352, 176, 128, 64, 32, 16, 8))

    def body(w_ref, g_ref, m_ref, v_ref, d_ref, nm_ref, nv_ref):
        d_ref[...], nm_ref[...], nv_ref[...] = _adamw_math(w_ref[...], g_ref[...], m_ref[...], v_ref[...])

    block = pl.BlockSpec((tr, cols), lambda i: (i, 0))
    shape = jax.ShapeDtypeStruct((rows, cols), F32)
    return pl.pallas_call(
        body, name="adamw_" + name, out_shape=(shape, shape, shape), grid=(rows // tr,),
        in_specs=[block] * 4, out_specs=(block,) * 3, compiler_params=_params("parallel"),
    )(w, g, m, v)


def _adamw_math(w, g, m, v):
    m = ADAM_B1 * m + (1.0 - ADAM_B1) * g
    v = ADAM_B2 * v + (1.0 - ADAM_B2) * (g * g)
    m_hat = m / (1.0 - ADAM_B1 ** ADAM_STEP)
    v_hat = v / (1.0 - ADAM_B2 ** ADAM_STEP)
    return -ADAM_LR * (m_hat / (jnp.sqrt(v_hat) + ADAM_EPS) + ADAM_WD * w), m, v


def _adamw_small(ws, gs, ms, vs):
    count = len(ws)

    def body(*refs):
        ins, outs = refs[:4 * count], refs[4 * count:]
        for i in range(count):
            outs[i][...], outs[count + i][...], outs[2 * count + i][...] = _adamw_math(
                ins[i][...], ins[count + i][...], ins[2 * count + i][...], ins[3 * count + i][...])

    shapes = tuple(jax.ShapeDtypeStruct(w.shape, F32) for w in ws)
    out = pl.pallas_call(body, name="adamw_small", out_shape=shapes * 3)(*ws, *gs, *ms, *vs)
    return out[:count], out[count:2 * count], out[2 * count:]


WEIGHTS = ("meta_tokens", "mix_pre_norm", "mix_post_norm", "ffn_pre_norm", "ffn_post_norm", "w_in", "conv_qkv", "a_log",
           "dt_bias", "gdn_norm", "conv_sc", "w_out", "w_gate", "w_up", "w_down")


def kernel(x, meta_tokens, mix_pre_norm, mix_post_norm, ffn_pre_norm, ffn_post_norm, w_in, conv_qkv, a_log, dt_bias, gdn_norm, conv_sc, w_out, w_gate, w_up, w_down, loss_target, m_meta_tokens, m_mix_pre_norm, m_mix_post_norm, m_ffn_pre_norm, m_ffn_post_norm, m_w_in, m_conv_qkv, m_a_log, m_dt_bias, m_gdn_norm, m_conv_sc, m_w_out, m_w_gate, m_w_up, m_w_down, v_meta_tokens, v_mix_pre_norm, v_mix_post_norm, v_ffn_pre_norm, v_ffn_post_norm, v_w_in, v_conv_qkv, v_a_log, v_dt_bias, v_gdn_norm, v_conv_sc, v_w_out, v_w_gate, v_w_up, v_w_down):
    d = x.shape[-1]
    two_d = lambda a: a.reshape(a.shape[-2:])
    weights = dict(zip(WEIGHTS, (meta_tokens, mix_pre_norm, mix_post_norm, ffn_pre_norm, ffn_post_norm, w_in, conv_qkv, a_log,
                                 dt_bias, gdn_norm, conv_sc, w_out, w_gate, w_up, w_down)))
    m_in = dict(zip(WEIGHTS, (m_meta_tokens, m_mix_pre_norm, m_mix_post_norm, m_ffn_pre_norm, m_ffn_post_norm, m_w_in, m_conv_qkv,
                              m_a_log, m_dt_bias, m_gdn_norm, m_conv_sc, m_w_out, m_w_gate, m_w_up, m_w_down)))
    v_in = dict(zip(WEIGHTS, (v_meta_tokens, v_mix_pre_norm, v_mix_post_norm, v_ffn_pre_norm, v_ffn_post_norm, v_w_in, v_conv_qkv,
                              v_a_log, v_dt_bias, v_gdn_norm, v_conv_sc, v_w_out, v_w_gate, v_w_up, v_w_down)))
    core = lax.axis_index("c")
    chip = 2 * lax.axis_index("x") + lax.axis_index("y")
    core_arg = core.reshape(1).astype(jnp.int32)
    chip_core = jnp.stack([chip, core]).astype(jnp.int32)
    whole = lambda a: a.reshape(a.shape[:-3] + (2 * a.shape[-2], d))
    by_rows = lambda n, a: two_d(a).T if n in ("w_in", "w_gate", "w_up") else two_d(a)

    shard = {n: by_rows(n, weights[n]).astype(MXU_DTYPE) for n in MATRICES}
    shard["w_in"] = jnp.pad(shard["w_in"], ((0, IN_SHARD_PAD - IN_SHARD), (0, 0)))
    small_all = _gather_weights([], [two_d(weights[n]) for n in SHARDED_SMALL])
    conv_qkv_full, conv_sc_full, meta_full = (jnp.concatenate([a[p] for p in range(N_CHIPS)], axis=1) for a in small_all)

    sq, grad_x, g, sums = _local_step(
        x, loss_target, meta_full, (mix_pre_norm, mix_post_norm, ffn_pre_norm, ffn_post_norm), _halves(shard["w_in"]),
        conv_qkv_full, a_log, dt_bias, gdn_norm, conv_sc_full, [_halves(shard[n]) for n in LATER], core_arg)

    parts, gots = zip(*sums)
    totals = [_add_chips(parts[i:i + 1], gots[i:i + 1], chip_core, MATRICES[i])[0] for i in range(2)]
    totals += _add_chips(parts[2:], gots[2:], chip_core, "ffn")
    *shared, packed_all = _share_halves(totals, _pack_small(dict(g, loss=sq)))
    grads = {n: whole(a) for n, a in zip(MATRICES, shared)}
    grads["w_in"] = grads["w_in"][:IN_SHARD]
    grads.update(_sum_devices(packed_all, chip.reshape(1).astype(jnp.int32)))
    loss = (0.5 / d) * grads.pop("loss")[0, 0]

    small = [n for n in WEIGHTS if n not in MATRICES]
    updates = dict(zip(small, zip(*_adamw_small(*([by_rows(n, params[n]) for n in small] for params in (weights, grads, m_in, v_in))))))
    outs = [[], [], [], []]
    for n in WEIGHTS:
        shape = weights[n].shape
        if n in MATRICES:
            updates[n] = _adamw(by_rows(n, weights[n]), grads[n], by_rows(n, m_in[n]), by_rows(n, v_in[n]), n)
        for out, a in zip(outs, (grads[n], *updates[n])):
            out.append((a.T if n in ("w_in", "w_gate", "w_up") else a).reshape(shape))
    return (loss, grad_x, *outs[0], *outs[1], *outs[2], *outs[3])
```

```python
import functools

import jax
import jax.numpy as jnp
from jax import lax
from jax.experimental import pallas as pl
from jax.experimental.pallas import tpu as pltpu

F32 = jnp.float32
BF16 = jnp.bfloat16
MXU_DTYPE = jnp.bfloat16
MESH = pl.DeviceIdType.MESH

D_MODEL = 1024
N_META = 16
HEADS = 4
HEAD_DIM = 128
GDN_WIDTH = HEADS * HEAD_DIM
GDN_CONV = 4
CHUNK = 64
SC_WIDTH = D_MODEL - GDN_WIDTH
SC_CONV = 3
D_FF = 2816
IN_WIDTH = 4 * GDN_WIDTH + 2 * HEADS + 3 * SC_WIDTH
IN_PAD = 3840
BA_COL = (4 * GDN_WIDTH + 3 * SC_WIDTH) // 128
EPS = 1e-6
LANES = 128
N_CHIPS = 4
VMEM_LIMIT = 48 * 2 ** 20
MM_VMEM_BUDGET = 42 * 2 ** 20

ADAM_LR = 0.001
ADAM_B1 = 0.9
ADAM_B2 = 0.999
ADAM_EPS = 1e-08
ADAM_WD = 0.01
ADAM_STEP = 10


def _pick(n, candidates):
    for c in candidates:
        if n % c == 0:
            return c
    return n


def _row_tile(n):
    return _pick(n, (352, 256, 176, 128, 64, 32, 16, 8))


def _params(*sem):
    return pltpu.CompilerParams(dimension_semantics=sem, vmem_limit_bytes=VMEM_LIMIT)


def _sigmoid(x):
    return 0.5 * jnp.tanh(0.5 * x) + 0.5


def _softplus(x):
    return jnp.maximum(x, 0.0) + jnp.log(1.0 + jnp.exp(-jnp.abs(x)))


def _dsilu(x, s):
    return s * (1.0 + x * (1.0 - s))


def _mm(a, b, mode, out_dtype, name, exchange=None):
    if mode == "tn":
        k_dim, m_dim = a.shape
    else:
        m_dim, k_dim = a.shape
    n_dim = b.shape[0] if mode == "nt" else b.shape[1]
    tn = _pick(n_dim, (1408, 1280, 1024, 768, 512, 256, 128))
    if mode == "tn":
        tm = _pick(m_dim, (1408, 1280, 1024, 512, 256, 128))
        tk = _pick(k_dim, (2112, 1408, 1280, 1056, 1024, 512, 256, 128))
    else:
        tk = k_dim
        blocks = lambda rows: 2 * (2 * rows * tk + 2 * tk * tn + 4 * rows * tn)
        tm = next((t for t in (2112, 1056, 1024, 704, 512, 256, 128) if m_dim % t == 0 and blocks(t) <= MM_VMEM_BUDGET), m_dim)
    nk = k_dim // tk
    if mode == "nn":
        a_spec = pl.BlockSpec((tm, tk), lambda i, j, k: (i, k))
        b_spec = pl.BlockSpec((tk, tn), lambda i, j, k: (k, j))
        dims = (((1,), (0,)), ((), ()))
    elif mode == "nt":
        a_spec = pl.BlockSpec((tm, tk), lambda i, j, k: (i, k))
        b_spec = pl.BlockSpec((tn, tk), lambda i, j, k: (j, k))
        dims = (((1,), (1,)), ((), ()))
    else:
        a_spec = pl.BlockSpec((tk, tm), lambda i, j, k: (k, i))
        b_spec = pl.BlockSpec((tk, tn), lambda i, j, k: (k, j))
        dims = (((0,), (0,)), ((), ()))

    out_spec = pl.BlockSpec((tm, tn), lambda i, j, k: (i, j))
    grid = (m_dim // tm, n_dim // tn, nk)
    parts = () if exchange is None else tuple(exchange)
    count = len(parts)

    assert out_dtype == F32

    def body(a_ref, b_ref, *rest):
        o_ref = rest[count]
        k = pl.program_id(2)
        step = (pl.program_id(0) * grid[1] + pl.program_id(1)) * nk + k
        if count:
            copies = _chip_copies(rest[:count], rest[count + 1:2 * count + 1], *rest[2 * count + 1:])

            @pl.when(step == 0)
            def _():
                for cp in copies:
                    cp.start()

        p = lax.dot_general(a_ref[...], b_ref[...], dims, preferred_element_type=F32)
        if nk == 1:
            o_ref[...] = p
        else:
            @pl.when(k == 0)
            def _():
                o_ref[...] = p

            @pl.when(k > 0)
            def _():
                o_ref[...] += p

        if count:
            @pl.when(step == grid[0] * grid[1] * nk - 1)
            def _():
                for cp in copies:
                    cp.wait_recv()
                for cp in copies:
                    cp.wait_send()

    out = pl.pallas_call(
        body, name=name,
        out_shape=(jax.ShapeDtypeStruct((m_dim, n_dim), out_dtype),)
        + tuple(jax.ShapeDtypeStruct((3,) + p.shape[1:], p.dtype) for p in parts),
        grid=grid,
        in_specs=[a_spec, b_spec] + [_hbm()] * count,
        out_specs=(out_spec,) + (_hbm(),) * count,
        scratch_shapes=[pltpu.SemaphoreType.DMA((3 * count,)), pltpu.SemaphoreType.DMA((3 * count,))] if count else [],
        compiler_params=_params(*(("arbitrary",) * 3 if count else ("parallel", "parallel", "arbitrary"))),
    )(a, b, *parts)
    return out[0] if not count else out


def _rms_apply(x, w):
    r = lax.rsqrt(jnp.mean(x * x, axis=-1, keepdims=True) + EPS)
    return x * r * w


def _rms_bwd(x, w, dy):
    r = lax.rsqrt(jnp.mean(x * x, axis=-1, keepdims=True) + EPS)
    xh = x * r
    dyw = dy * w
    dx = r * (dyw - xh * jnp.mean(dyw * xh, axis=-1, keepdims=True))
    return dx, jnp.sum(dy * xh, axis=0, keepdims=True)


def _accumulate(ref, first, value):
    @pl.when(first)
    def _():
        ref[...] = value

    @pl.when(jnp.logical_not(first))
    def _():
        ref[...] += value


def _rows(tr, width):
    return pl.BlockSpec((tr, width), lambda i: (i, 0))


def _vec(width):
    return pl.BlockSpec((1, width), lambda i: (0, 0))


def _embed(x, head, w_pre, w_shard, rows_per_seq):
    batch, seq, d = x.shape
    x_offset = head.shape[0]
    tr = _row_tile(rows_per_seq)
    tiles_per_seq = rows_per_seq // tr
    n = batch * rows_per_seq

    def body(x_ref, head_ref, w_ref, ws_ref, h0_ref, u_ref, wall_ref, send_sems, recv_sems):
        gather = _gather_copies([ws_ref], [wall_ref], send_sems, recv_sems)
        i = pl.program_id(0)
        tile = lax.rem(i, tiles_per_seq)

        @pl.when(i == 0)
        def _():
            for cp in gather[0]:
                cp.start()

        rows = jnp.concatenate([head_ref[...], x_ref[0:tr - x_offset, :]], axis=0)
        if tiles_per_seq > 1:
            start = pl.multiple_of(jnp.maximum(tile * tr - x_offset, 0), SUBLANES)
            rows = jnp.where(tile == 0, rows, x_ref[pl.ds(start, tr), :])
        h0_ref[...] = rows
        u_ref[...] = _rms_apply(rows, w_ref[...]).astype(u_ref.dtype)

        @pl.when(i == n // tr - 1)
        def _():
            _gather_finish(gather)

    return pl.pallas_call(
        body, name="embed",
        out_shape=(jax.ShapeDtypeStruct((n, d), F32), jax.ShapeDtypeStruct((n, d), MXU_DTYPE),
                   jax.ShapeDtypeStruct((N_CHIPS,) + w_shard.shape, w_shard.dtype)),
        grid=(n // tr,),
        in_specs=[pl.BlockSpec((None, seq, d), lambda i: (i // tiles_per_seq, 0, 0)),
                  pl.BlockSpec((x_offset, d), lambda i: (0, 0)), _vec(d), _hbm()],
        out_specs=(_rows(tr, d), _rows(tr, d), _hbm()),
        scratch_shapes=[pltpu.SemaphoreType.DMA((GATHER_SEMS,)), pltpu.SemaphoreType.DMA((GATHER_SEMS,))],
        compiler_params=_params("arbitrary"),
    )(x, head, w_pre, w_shard)


def _mix_residual(cat, w_out, h0, w_post, w_pre):
    n, d = h0.shape
    tr = _pick(n, (1056, 1024, 704, 512, 256, 128))

    def body(cat_ref, w_ref, h0_ref, wpost_ref, wpre_ref, mix_ref, h1_ref, u2_ref):
        mix = jnp.dot(cat_ref[...], w_ref[...], preferred_element_type=F32)
        mix_ref[...] = mix
        h1 = h0_ref[...] + _rms_apply(mix, wpost_ref[...])
        h1_ref[...] = h1
        u2_ref[...] = _rms_apply(h1, wpre_ref[...]).astype(u2_ref.dtype)

    wide = jax.ShapeDtypeStruct((n, d), F32)
    return pl.pallas_call(
        body, name="mix_residual", out_shape=(wide, wide, jax.ShapeDtypeStruct((n, d), MXU_DTYPE)), grid=(n // tr,),
        in_specs=[_rows(tr, cat.shape[1]), pl.BlockSpec(w_out.shape, lambda i: (0, 0)), _rows(tr, d), _vec(d), _vec(d)],
        out_specs=(_rows(tr, d), _rows(tr, d), _rows(tr, d)), compiler_params=_params("parallel"),
    )(cat, w_out, h0, w_post, w_pre)


NT_DIMS = (((1,), (1,)), ((), ()))


def _ffn_tiles(n):
    return _pick(n, (1056, 704, 512, 256, 128)), _pick(D_FF, (1408, 256, 128))


def _swiglu_fwd(u, w_gate_t, w_up_t, w_next):
    n, d = u.shape
    tm, tn = _ffn_tiles(n)
    grid = (D_FF // tn, n // tm)

    def body(u_ref, wg_ref, wu_ref, wn_ref, g_ref, up_ref, act_ref, wall_ref, send_sems, recv_sems):
        gather = _gather_copies([wn_ref], [wall_ref], send_sems, recv_sems)
        step = pl.program_id(0) * grid[1] + pl.program_id(1)

        @pl.when(step == 0)
        def _():
            for cp in gather[0]:
                cp.start()

        a = u_ref[...]
        g = lax.dot_general(a, wg_ref[...], NT_DIMS, preferred_element_type=F32)
        up = lax.dot_general(a, wu_ref[...], NT_DIMS, preferred_element_type=F32)
        g_ref[...] = g.astype(g_ref.dtype)
        up_ref[...] = up.astype(up_ref.dtype)
        act_ref[...] = (g * _sigmoid(g) * up).astype(act_ref.dtype)

        @pl.when(step == grid[0] * grid[1] - 1)
        def _():
            _gather_finish(gather)

    tile = pl.BlockSpec((tm, tn), lambda j, i: (i, j))
    weight = pl.BlockSpec((tn, d), lambda j, i: (j, 0))
    wide = jax.ShapeDtypeStruct((n, D_FF), MXU_DTYPE)
    return pl.pallas_call(
        body, name="swiglu_fwd",
        out_shape=(wide, wide, jax.ShapeDtypeStruct((n, D_FF), MXU_DTYPE),
                   jax.ShapeDtypeStruct((N_CHIPS,) + w_next.shape, w_next.dtype)),
        grid=grid,
        in_specs=[pl.BlockSpec((tm, d), lambda j, i: (i, 0)), weight, weight, _hbm()], out_specs=(tile, tile, tile, _hbm()),
        scratch_shapes=[pltpu.SemaphoreType.DMA((GATHER_SEMS,)), pltpu.SemaphoreType.DMA((GATHER_SEMS,))],
        compiler_params=_params("arbitrary", "arbitrary"),
    )(u, w_gate_t, w_up_t, w_next)


def _swiglu_bwd(dffn, w_down, gate, up):
    n, d = dffn.shape
    tm, tn = _ffn_tiles(n)

    def body(dy_ref, w_ref, g_ref, u_ref, dg_ref, du_ref):
        da = lax.dot_general(dy_ref[...], w_ref[...], NT_DIMS, preferred_element_type=F32)
        g = g_ref[...].astype(F32)
        s = _sigmoid(g)
        dg_ref[...] = (da * u_ref[...].astype(F32) * _dsilu(g, s)).astype(dg_ref.dtype)
        du_ref[...] = (da * g * s).astype(du_ref.dtype)

    tile = pl.BlockSpec((tm, tn), lambda j, i: (i, j))
    shape = jax.ShapeDtypeStruct((n, D_FF), MXU_DTYPE)
    return pl.pallas_call(
        body, name="swiglu_bwd", out_shape=(shape, shape), grid=(D_FF // tn, n // tm),
        in_specs=[pl.BlockSpec((tm, d), lambda j, i: (i, 0)), pl.BlockSpec((tn, d), lambda j, i: (j, 0)), tile, tile],
        out_specs=(tile, tile), compiler_params=_params("parallel", "parallel"),
    )(dffn, w_down, gate, up)


def _loss_head(h1, ffn, w_post, target, rows_per_seq, x_offset):
    n, d = h1.shape
    tr = _row_tile(rows_per_seq)
    tiles_per_seq = rows_per_seq // tr
    seq = target.shape[1]

    def seq_rows(t_ref, tile):
        first = jnp.concatenate([jnp.zeros((x_offset, d), F32), t_ref[0:tr - x_offset, :]], axis=0)
        if tiles_per_seq == 1:
            return first
        start = pl.multiple_of(jnp.maximum(tile * tr - x_offset, 0), SUBLANES)
        return jnp.where(tile == 0, first, t_ref[pl.ds(start, tr), :])

    def body(h1_ref, ffn_ref, w_ref, t_ref, dh2_ref, dffn_ref, dw_ref, sq_ref):
        i = pl.program_id(0)
        tile = lax.rem(i, tiles_per_seq)
        w = w_ref[...]
        f = ffn_ref[...]
        r = lax.rsqrt(jnp.mean(f * f, axis=-1, keepdims=True) + EPS)
        fh = f * r
        row = tile * tr + lax.broadcasted_iota(jnp.int32, (tr, 1), 0)
        err = jnp.where(row >= x_offset, h1_ref[...] + fh * w - seq_rows(t_ref, tile), 0.0)
        dh2 = err * (1.0 / d)
        dh2_ref[...] = dh2
        dyw = dh2 * w
        dffn_ref[...] = (r * (dyw - fh * jnp.mean(dyw * fh, axis=-1, keepdims=True))).astype(dffn_ref.dtype)
        _accumulate(dw_ref, i == 0, jnp.sum(dh2 * fh, axis=0, keepdims=True))
        _accumulate(sq_ref, i == 0, jnp.sum(jnp.sum(err * err, axis=1, keepdims=True), axis=0, keepdims=True))

    return pl.pallas_call(
        body, name="loss_head",
        out_shape=(jax.ShapeDtypeStruct((n, d), F32), jax.ShapeDtypeStruct((n, d), MXU_DTYPE),
                   jax.ShapeDtypeStruct((1, d), F32), jax.ShapeDtypeStruct((1, 1), F32)),
        grid=(n // tr,),
        in_specs=[_rows(tr, d), _rows(tr, d), _vec(d), pl.BlockSpec((None, seq, d), lambda i: (i // tiles_per_seq, 0, 0))],
        out_specs=(_rows(tr, d), _rows(tr, d), _vec(d), _vec(1)),
        compiler_params=_params("arbitrary"),
    )(h1, ffn, w_post, target)


def _mid_bwd(h1, mix, w_mix_post, w_ffn_pre, dh2, du2_gate, dup, w_up_t, grads):
    n, d = h1.shape
    tr = _pick(n, (528, 512, 352, 256, 128, 64, 32, 16, 8))
    count = len(grads)

    def body(h1_ref, mix_ref, wpost_ref, wpre_ref, dh2_ref, du2_ref, dup_ref, wup_ref, *rest):
        g_refs, (dh1_ref, dmix_ref, dwpre_ref, dwpost_ref), got_refs = rest[:count], rest[count:count + 4], rest[count + 4:2 * count + 4]
        exchange = _sibling_copies(g_refs, got_refs, *rest[2 * count + 4:])
        i = pl.program_id(0)

        @pl.when(i == 0)
        def _():
            for cp in exchange:
                cp.start()

        du2 = du2_ref[...] + jnp.dot(dup_ref[...], wup_ref[...], preferred_element_type=F32)
        dx, dwpre = _rms_bwd(h1_ref[...], wpre_ref[...], du2)
        dh1 = dh2_ref[...] + dx
        dh1_ref[...] = dh1
        dmix, dwpost = _rms_bwd(mix_ref[...], wpost_ref[...], dh1)
        dmix_ref[...] = dmix.astype(dmix_ref.dtype)
        _accumulate(dwpre_ref, i == 0, dwpre)
        _accumulate(dwpost_ref, i == 0, dwpost)

        @pl.when(i == n // tr - 1)
        def _():
            for cp in exchange:
                cp.wait_recv()
            for cp in exchange:
                cp.wait_send()

    dh1, dmix, dwpre, dwpost, *got = pl.pallas_call(
        body, name="mid_bwd",
        out_shape=(jax.ShapeDtypeStruct((n, d), F32), jax.ShapeDtypeStruct((n, d), MXU_DTYPE),
                   jax.ShapeDtypeStruct((1, d), F32), jax.ShapeDtypeStruct((1, d), F32))
        + tuple(jax.ShapeDtypeStruct((g.shape[0],) + g.shape[2:], F32) for g in grads),
        grid=(n // tr,),
        in_specs=[_rows(tr, d), _rows(tr, d), _vec(d), _vec(d), _rows(tr, d), _rows(tr, d), _rows(tr, dup.shape[1]),
                  pl.BlockSpec(w_up_t.shape, lambda i: (0, 0))] + [_hbm()] * count,
        out_specs=(_rows(tr, d), _rows(tr, d), _vec(d), _vec(d)) + (_hbm(),) * count,
        scratch_shapes=[pltpu.SemaphoreType.DMA((count,)), pltpu.SemaphoreType.DMA((count,))],
        compiler_params=_params("arbitrary"),
    )(h1, mix, w_mix_post, w_ffn_pre, dh2, du2_gate, dup, w_up_t, *grads)
    return dh1, dmix, dwpre, dwpost, got


def _in_bwd(h0, w_pre, dh1, du1, rows_per_seq, pad_rows, x_offset):
    n, d = h0.shape
    tr = _row_tile(rows_per_seq)
    tiles_per_seq = rows_per_seq // tr
    seq = rows_per_seq - x_offset

    def body(h0_ref, w_ref, dh1_ref, du1_ref, gx_ref, dmeta_ref, dw_ref):
        i = pl.program_id(0)
        tile = lax.rem(i, tiles_per_seq)
        dx, dw = _rms_bwd(h0_ref[...], w_ref[...], du1_ref[...])
        dh0 = dh1_ref[...] + dx
        _accumulate(dw_ref, i == 0, dw)

        @pl.when(tile == 0)
        def _():
            gx_ref[0:tr - x_offset, :] = dh0[x_offset:, :]
            _accumulate(dmeta_ref, i == 0, dh0[pad_rows:x_offset, :])

        if tiles_per_seq > 1:
            @pl.when(tile > 0)
            def _():
                gx_ref[pl.ds(pl.multiple_of(tile * tr - x_offset, SUBLANES), tr), :] = dh0

    return pl.pallas_call(
        body, name="in_bwd",
        out_shape=(jax.ShapeDtypeStruct((n // rows_per_seq, seq, d), F32), jax.ShapeDtypeStruct((x_offset - pad_rows, d), F32),
                   jax.ShapeDtypeStruct((1, d), F32)),
        grid=(n // tr,),
        in_specs=[_rows(tr, d), _vec(d), _rows(tr, d), _rows(tr, d)],
        out_specs=(pl.BlockSpec((None, seq, d), lambda i: (i // tiles_per_seq, 0, 0)),
                   pl.BlockSpec((x_offset - pad_rows, d), lambda i: (0, 0)), _vec(d)),
        compiler_params=_params("arbitrary"),
    )(h0, w_pre, dh1, du1)


def _lane_is(lo, hi):
    lane = lax.broadcasted_iota(jnp.int32, (1, LANES), 1)
    return jnp.logical_and(lane >= lo, lane < hi)


def _gates_fwd(proj, a_log_l, dt_bias_l, rows_per_seq, pad_rows):
    n = proj.shape[0]
    tr = _row_tile(rows_per_seq)
    tiles_per_seq = rows_per_seq // tr

    def body(p_ref, a_ref, dt_ref, o_ref):
        x = p_ref[...]
        row = lax.rem(pl.program_id(0), tiles_per_seq) * tr + lax.broadcasted_iota(jnp.int32, (tr, 1), 0)
        g = -jnp.exp(a_ref[...]) * _softplus(x + dt_ref[...])
        val = jnp.where(_lane_is(0, HEADS), _sigmoid(x), jnp.where(_lane_is(HEADS, 2 * HEADS), g, 0.0))
        o_ref[...] = jnp.where(row >= pad_rows, val, 0.0)

    return pl.pallas_call(
        body, name="gates_fwd", out_shape=jax.ShapeDtypeStruct((n, LANES), F32), grid=(n // tr,),
        in_specs=[pl.BlockSpec((tr, LANES), lambda i: (i, BA_COL)), _vec(LANES), _vec(LANES)],
        out_specs=_rows(tr, LANES), compiler_params=_params("parallel"),
    )(proj, a_log_l, dt_bias_l)


def _gates_bwd(proj, dbg, a_log_l, dt_bias_l, rows_per_seq, pad_rows, dproj):
    n = proj.shape[0]
    tr = _row_tile(rows_per_seq)
    tiles_per_seq = rows_per_seq // tr

    def body(p_ref, d_ref, a_ref, dt_ref, _, dx_ref, da_ref, ddt_ref):
        i = pl.program_id(0)
        x = p_ref[...]
        d = d_ref[...]
        row = lax.rem(i, tiles_per_seq) * tr + lax.broadcasted_iota(jnp.int32, (tr, 1), 0)
        live = row >= pad_rows
        beta = _sigmoid(x)
        ea = jnp.exp(a_ref[...])
        xa = x + dt_ref[...]
        g = -ea * _softplus(xa)
        is_g = _lane_is(HEADS, 2 * HEADS)
        d_alogit = jnp.where(jnp.logical_and(live, is_g), d * (-ea) * _sigmoid(xa), 0.0)
        d_blogit = jnp.where(jnp.logical_and(live, _lane_is(0, HEADS)), d * beta * (1.0 - beta), 0.0)
        dx_ref[:, :LANES] = (d_alogit + d_blogit).astype(dx_ref.dtype)
        dx_ref[:, LANES:] = jnp.zeros((tr, LANES), dx_ref.dtype)
        _accumulate(da_ref, i == 0, jnp.sum(jnp.where(jnp.logical_and(live, is_g), d * g, 0.0), axis=0, keepdims=True))
        _accumulate(ddt_ref, i == 0, jnp.sum(d_alogit, axis=0, keepdims=True))

    return pl.pallas_call(
        body, name="gates_bwd",
        out_shape=(jax.ShapeDtypeStruct(dproj.shape, dproj.dtype), jax.ShapeDtypeStruct((1, LANES), F32),
                   jax.ShapeDtypeStruct((1, LANES), F32)),
        grid=(n // tr,),
        in_specs=[pl.BlockSpec((tr, LANES), lambda i: (i, BA_COL)), _rows(tr, LANES), _vec(LANES), _vec(LANES), _hbm()],
        out_specs=(pl.BlockSpec((tr, 2 * LANES), lambda i: (i, BA_COL // 2)), _vec(LANES), _vec(LANES)),
        input_output_aliases={4: 0},
        compiler_params=_params("arbitrary"),
    )(proj, dbg, a_log_l, dt_bias_l, dproj)


HALO = 8


def _halo_scratch(rs):
    return pltpu.VMEM((rs + 2 * HALO, LANES), F32)


def _stage(ref, x):
    rs = x.shape[0]
    ref[0:HALO, :] = jnp.zeros((HALO, LANES), F32)
    ref[HALO + rs:, :] = jnp.zeros((HALO, LANES), F32)
    ref[HALO:HALO + rs, :] = x


def _shifted(ref, k, rs):
    return ref[pl.ds(HALO - k, rs), :]


def _causal_conv(x, x_staged, w, width):
    acc = w[width - 1:width, :] * x
    for i in range(width - 1):
        acc = acc + w[i:i + 1, :] * _shifted(x_staged, width - 1 - i, x.shape[0])
    return acc


def _anti_causal_conv(dy, dy_staged, w, width):
    acc = w[width - 1:width, :] * dy
    for i in range(width - 1):
        acc = acc + w[i:i + 1, :] * _shifted(dy_staged, -(width - 1 - i), dy.shape[0])
    return acc


def _conv_weight_grad(dy, x, x_staged, width):
    taps = [_shifted(x_staged, width - 1 - i, x.shape[0]) for i in range(width - 1)] + [x]
    return jnp.concatenate([jnp.sum(dy * tap, axis=0, keepdims=True) for tap in taps], axis=0)


def _seq_cols(rs, col0, heads):
    return pl.BlockSpec((rs, heads * LANES), lambda j, b: (b, col0 // heads + j))


def _tap_cols(width, col0, heads):
    return pl.BlockSpec((width, heads * LANES), lambda j, b: (0, col0 // heads + j))


def _lanes_of(h):
    return slice(h * LANES, (h + 1) * LANES)


def _qkv_fwd(proj, conv_w, kind, rs):
    n = proj.shape[0]
    col0 = {"q": 0, "k": HEADS, "v": 2 * HEADS}[kind]
    hb = HEADS

    def body(p_ref, w_ref, o_ref, staged):
        for h in range(hb):
            pre = p_ref[:, _lanes_of(h)]
            _stage(staged, pre)
            c = _causal_conv(pre, staged, w_ref[:, _lanes_of(h)], GDN_CONV)
            s = c * _sigmoid(c)
            if kind != "v":
                s = s * lax.rsqrt(jnp.sum(s * s, axis=-1, keepdims=True) + EPS)
            if kind == "q":
                s = s * (HEAD_DIM ** -0.5)
            o_ref[:, _lanes_of(h)] = s

    return pl.pallas_call(
        body, name="qkv_fwd_" + kind, out_shape=jax.ShapeDtypeStruct((n, GDN_WIDTH), F32), grid=(HEADS // hb, n // rs),
        in_specs=[_seq_cols(rs, col0, hb), _tap_cols(GDN_CONV, col0, hb)],
        out_specs=_seq_cols(rs, 0, hb), scratch_shapes=[_halo_scratch(rs)], compiler_params=_params("parallel", "parallel"),
    )(proj, conv_w)


def _qkv_bwd(dy, proj, conv_w, kind, rs, dproj):
    n = proj.shape[0]
    col0 = {"q": 0, "k": HEADS, "v": 2 * HEADS}[kind]
    hb = HEADS

    def body(dy_ref, p_ref, w_ref, _, dp_ref, dw_ref, pre_staged, dc_staged):
        for h in range(hb):
            lanes = _lanes_of(h)
            pre = p_ref[:, lanes]
            w = w_ref[:, lanes]
            _stage(pre_staged, pre)
            c = _causal_conv(pre, pre_staged, w, GDN_CONV)
            sg = _sigmoid(c)
            s = c * sg
            ds = dy_ref[:, lanes]
            if kind == "q":
                ds = ds * (HEAD_DIM ** -0.5)
            if kind != "v":
                r = lax.rsqrt(jnp.sum(s * s, axis=-1, keepdims=True) + EPS)
                sh = s * r
                ds = r * (ds - sh * jnp.sum(ds * sh, axis=-1, keepdims=True))
            dc = ds * _dsilu(c, sg)
            _stage(dc_staged, dc)
            dp_ref[:, lanes] = _anti_causal_conv(dc, dc_staged, w, GDN_CONV).astype(dp_ref.dtype)
            _accumulate(dw_ref.at[:, lanes], pl.program_id(1) == 0, _conv_weight_grad(dc, pre, pre_staged, GDN_CONV))

    return pl.pallas_call(
        body, name="qkv_bwd_" + kind,
        out_shape=(jax.ShapeDtypeStruct(dproj.shape, dproj.dtype), jax.ShapeDtypeStruct((GDN_CONV, GDN_WIDTH), F32)),
        grid=(HEADS // hb, n // rs),
        in_specs=[_seq_cols(rs, 0, hb), _seq_cols(rs, col0, hb), _tap_cols(GDN_CONV, col0, hb), _hbm()],
        out_specs=(_seq_cols(rs, col0, hb), _tap_cols(GDN_CONV, 0, hb)), input_output_aliases={3: 0},
        scratch_shapes=[_halo_scratch(rs), _halo_scratch(rs)],
        compiler_params=_params("parallel", "arbitrary"),
    )(dy, proj, conv_w, dproj)


SC_COL = 4 * HEADS


def _sc_fwd(proj, conv_w, rs, cat):
    n = proj.shape[0]

    hb = 2

    def body(x_ref, b_ref, c_ref, w_ref, _, y_ref, staged):
        for h in range(hb):
            lanes = _lanes_of(h)
            u = c_ref[:, lanes] * x_ref[:, lanes]
            _stage(staged, u)
            y_ref[:, lanes] = (b_ref[:, lanes] * _causal_conv(u, staged, w_ref[:, lanes], SC_CONV)).astype(y_ref.dtype)

    return pl.pallas_call(
        body, name="sc_fwd", out_shape=jax.ShapeDtypeStruct(cat.shape, cat.dtype), grid=(HEADS // hb, n // rs),
        in_specs=[_seq_cols(rs, SC_COL, hb), _seq_cols(rs, SC_COL + 4, hb), _seq_cols(rs, SC_COL + 8, hb),
                  _tap_cols(SC_CONV, 0, hb), _hbm()],
        out_specs=_seq_cols(rs, HEADS, hb), input_output_aliases={4: 0}, scratch_shapes=[_halo_scratch(rs)],
        compiler_params=_params("parallel", "parallel"),
    )(proj, proj, proj, conv_w, cat)


def _sc_bwd(dcat, proj, conv_w, rs, dproj):
    n = proj.shape[0]
    hb = 2

    def body(dy_ref, x_ref, b_ref, c_ref, w_ref, _, dx_ref, db_ref, dc_ref, dw_ref, u_staged, dcv_staged):
        for h in range(hb):
            lanes = _lanes_of(h)
            w = w_ref[:, lanes]
            x = x_ref[:, lanes]
            cc = c_ref[:, lanes]
            u = cc * x
            _stage(u_staged, u)
            dy = dy_ref[:, lanes]
            db_ref[:, lanes] = (dy * _causal_conv(u, u_staged, w, SC_CONV)).astype(db_ref.dtype)
            dcv = dy * b_ref[:, lanes]
            _stage(dcv_staged, dcv)
            du = _anti_causal_conv(dcv, dcv_staged, w, SC_CONV)
            dx_ref[:, lanes] = (du * cc).astype(dx_ref.dtype)
            dc_ref[:, lanes] = (du * x).astype(dc_ref.dtype)
            _accumulate(dw_ref.at[:, lanes], pl.program_id(1) == 0, _conv_weight_grad(dcv, u, u_staged, SC_CONV))

    piece = jax.ShapeDtypeStruct((n, SC_WIDTH), MXU_DTYPE)
    return pl.pallas_call(
        body, name="sc_bwd",
        out_shape=(jax.ShapeDtypeStruct(dproj.shape, dproj.dtype), piece, piece, jax.ShapeDtypeStruct((SC_CONV, SC_WIDTH), F32)),
        grid=(HEADS // hb, n // rs),
        in_specs=[_seq_cols(rs, HEADS, hb), _seq_cols(rs, SC_COL, hb), _seq_cols(rs, SC_COL + 4, hb),
                  _seq_cols(rs, SC_COL + 8, hb), _tap_cols(SC_CONV, 0, hb), _hbm()],
        out_specs=(_seq_cols(rs, SC_COL, hb), _seq_cols(rs, 0, hb), _seq_cols(rs, 0, hb), _tap_cols(SC_CONV, 0, hb)),
        input_output_aliases={5: 0},
        scratch_shapes=[_halo_scratch(rs), _halo_scratch(rs)],
        compiler_params=_params("parallel", "arbitrary"),
    )(dcat, proj, proj, proj, conv_w, dproj)


Z_COL = 3 * HEADS


def _gate_fwd(o, proj, gdn_norm, rs):
    n = proj.shape[0]

    hb = HEADS

    def body(o_ref, z_ref, w_ref, y_ref):
        for h in range(hb):
            lanes = _lanes_of(h)
            z = z_ref[:, lanes]
            y_ref[:, lanes] = (_rms_apply(o_ref[:, lanes], w_ref[...]) * z * _sigmoid(z)).astype(y_ref.dtype)

    return pl.pallas_call(
        body, name="gate_fwd", out_shape=jax.ShapeDtypeStruct((n, D_MODEL), MXU_DTYPE), grid=(HEADS // hb, n // rs),
        in_specs=[_seq_cols(rs, 0, hb), _seq_cols(rs, Z_COL, hb), pl.BlockSpec((1, LANES), lambda j, b: (0, 0))],
        out_specs=_seq_cols(rs, 0, hb), compiler_params=_params("parallel", "parallel"),
    )(o, proj, gdn_norm)


def _gate_bwd(dcat, o, proj, gdn_norm, rs, grad):
    n = proj.shape[0]
    hb = 2
    grid = (HEADS // hb, n // rs)

    def body(dy_ref, o_ref, z_ref, w_ref, g_ref, do_ref, dz_ref, dw_ref, got_ref, send_sems, recv_sems):
        exchange = _sibling_copies([g_ref], [got_ref], send_sems, recv_sems)
        step = pl.program_id(0) * grid[1] + pl.program_id(1)

        @pl.when(step == 0)
        def _():
            for cp in exchange:
                cp.start()

        w = w_ref[...]
        dw_step = jnp.zeros((1, LANES), F32)
        for h in range(hb):
            lanes = _lanes_of(h)
            z = z_ref[:, lanes]
            o = o_ref[:, lanes]
            dy = dy_ref[:, lanes]
            s = _sigmoid(z)
            dz_ref[:, lanes] = (dy * _rms_apply(o, w) * _dsilu(z, s)).astype(dz_ref.dtype)
            do, dw = _rms_bwd(o, w, dy * z * s)
            do_ref[:, lanes] = do
            dw_step = dw_step + dw
        _accumulate(dw_ref, step == 0, dw_step)

        @pl.when(step == grid[0] * grid[1] - 1)
        def _():
            for cp in exchange:
                cp.wait_recv()
            for cp in exchange:
                cp.wait_send()

    return pl.pallas_call(
        body, name="gate_bwd",
        out_shape=(jax.ShapeDtypeStruct((n, GDN_WIDTH), F32), jax.ShapeDtypeStruct((n, IN_PAD), MXU_DTYPE),
                   jax.ShapeDtypeStruct((1, LANES), F32), jax.ShapeDtypeStruct((grad.shape[0],) + grad.shape[2:], F32)),
        grid=grid,
        in_specs=[_seq_cols(rs, 0, hb), _seq_cols(rs, 0, hb), _seq_cols(rs, Z_COL, hb), pl.BlockSpec((1, LANES), lambda j, b: (0, 0)),
                  _hbm()],
        out_specs=(_seq_cols(rs, 0, hb), _seq_cols(rs, Z_COL, hb), pl.BlockSpec((1, LANES), lambda j, b: (0, 0)), _hbm()),
        scratch_shapes=[pltpu.SemaphoreType.DMA((1,)), pltpu.SemaphoreType.DMA((1,))],
        compiler_params=_params("arbitrary", "arbitrary"),
    )(dcat, o, proj, gdn_norm, grad)


def _dot(a, b):
    return jnp.dot(a.astype(MXU_DTYPE), b.astype(MXU_DTYPE), preferred_element_type=F32)


def _dot_nt(a, b):
    return lax.dot_general(a.astype(MXU_DTYPE), b.astype(MXU_DTYPE), (((1,), (1,)), ((), ())),
                           preferred_element_type=F32)


def _dot_tn(a, b):
    return lax.dot_general(a.astype(MXU_DTYPE), b.astype(MXU_DTYPE), (((0,), (0,)), ((), ())),
                           preferred_element_type=F32)


def _split(x):
    hi = x.astype(MXU_DTYPE)
    return hi, (x - hi.astype(F32)).astype(MXU_DTYPE)


def _dot_split(a, b):
    mm = functools.partial(jnp.dot, preferred_element_type=F32)
    return mm(a[0], b[0]) + (mm(a[0], b[1]) + mm(a[1], b[0]))


def _unit_lower_inverses(mats, eye):
    inv = [eye - a for a in mats]
    power = [_split(a) for a in mats]
    span = 2
    while span < CHUNK:
        power = [_split(_dot_split(p, p)) for p in power]
        inv = [i + _dot_split(_split(i), p) for i, p in zip(inv, power)]
        span *= 2
    return inv


def _chunk_masks():
    ii = lax.broadcasted_iota(jnp.int32, (CHUNK, CHUNK), 0)
    jj = lax.broadcasted_iota(jnp.int32, (CHUNK, CHUNK), 1)
    return ii, jj


def _chunk_decay(g_col, ii, jj):
    incl = ii >= jj
    g_row = jnp.sum(jnp.where(ii == jj, g_col, 0.0), axis=0, keepdims=True)
    gc_col = jnp.sum(jnp.where(incl, g_row, 0.0), axis=1, keepdims=True)
    gc_row = jnp.sum(jnp.where(ii <= jj, g_col, 0.0), axis=0, keepdims=True)
    g_total = jnp.sum(g_row, axis=1, keepdims=True)
    decay = jnp.where(incl, jnp.exp(jnp.where(incl, gc_col - gc_row, 0.0)), 0.0)
    return gc_col, g_total, decay


def _gdn_segments(rs, candidates):
    chunks = rs // CHUNK
    seg_chunks = _pick(chunks, candidates)
    return chunks, seg_chunks, chunks // seg_chunks


def _gdn_fwd(q, k, v, bg, rs, pieces):
    n = q.shape[0]
    batch = n // rs
    chunks, seg_chunks, segs = _gdn_segments(rs, (11, 8, 4, 2))
    seg_rows = seg_chunks * CHUNK
    chains = [(b, h) for b in range(batch) for h in range(HEADS)]
    each = lambda f, *lists: [f(*args) for args in zip(*lists)]
    count = len(pieces)

    def body(q_ref, k_ref, v_ref, bg_ref, *rest):
        w_refs, (o_ref, s_ref, t_ref), out_refs = rest[:count], rest[count:count + 3], rest[count + 3:2 * count + 3]
        state_ref, send_sems, recv_sems = rest[2 * count + 3:]
        gather = _gather_copies(w_refs, out_refs, send_sems, recv_sems)

        @pl.when(pl.program_id(0) == 0)
        def _():
            state_ref[...] = jnp.zeros_like(state_ref)
            for cp in gather[0]:
                cp.start()

        ii, jj = _chunk_masks()
        incl = ii >= jj
        eye = (ii == jj).astype(F32)

        def chunk(c, carry):
            rows = pl.ds(pl.multiple_of(c * CHUNK, CHUNK), CHUNK)
            bgc = [bg_ref[b, rows, :] for b in range(batch)]
            qc = [q_ref[b, rows, _lanes_of(h)] for b, h in chains]
            kc = [k_ref[b, rows, _lanes_of(h)] for b, h in chains]
            vc = [v_ref[b, rows, _lanes_of(h)] for b, h in chains]
            beta = [bgc[b][:, h:h + 1] for b, h in chains]
            state = [state_ref[b, h] for b, h in chains]
            dec = [_chunk_decay(bgc[b][:, HEADS + h:HEADS + h + 1], ii, jj) for b, h in chains]
            gc_col, g_total, decay = ([d[i] for d in dec] for i in range(3))
            kb = each(lambda x, y: x * y, kc, beta)
            a = each(lambda x, y, d: jnp.where(ii > jj, _dot_nt(x, y) * d, 0.0), kb, kc, decay)
            t_inv = _unit_lower_inverses(a, eye)
            eg = [jnp.exp(g) for g in gc_col]
            u = each(lambda t, x, y: _dot(t, x * y), t_inv, vc, beta)
            w = each(lambda t, x, e: _dot(t, x * e), t_inv, kb, eg)
            qk = each(lambda x, y, d: jnp.where(incl, _dot_nt(x, y) * d, 0.0), qc, kc, decay)
            v_new = each(lambda x, y, s: x - _dot(y, s), u, w, state)
            o = each(lambda x, e, s, m, vn: _dot(x * e, s) + _dot(m, vn), qc, eg, state, qk, v_new)
            new_state = each(lambda s, gt, x, g, vn: s * jnp.exp(gt) + _dot_tn(x * jnp.exp(gt - g), vn),
                             state, g_total, kc, gc_col, v_new)
            for i, (b, h) in enumerate(chains):
                s_ref[b, h, c] = state[i]
                t_ref[b, h, c] = t_inv[i]
                o_ref[b, rows, _lanes_of(h)] = o[i]
                state_ref[b, h] = new_state[i]
            return carry

        lax.fori_loop(0, seg_chunks, chunk, 0)

        @pl.when(pl.program_id(0) == segs - 1)
        def _():
            _gather_finish(gather)

    rows_spec = lambda width: pl.BlockSpec((batch, seg_rows, width), lambda s: (0, s, 0))
    per_chunk = lambda r, c: pl.BlockSpec((batch, HEADS, seg_chunks, r, c), lambda s: (0, 0, s, 0, 0))
    as_seqs = lambda a: a.reshape(batch, rs, a.shape[-1])
    sems = GATHER_SEMS * count
    o, states, t_invs, *gathered = pl.pallas_call(
        body, name="gdn_fwd",
        out_shape=(jax.ShapeDtypeStruct((batch, rs, GDN_WIDTH), F32),
                   jax.ShapeDtypeStruct((batch, HEADS, chunks, HEAD_DIM, HEAD_DIM), F32),
                   jax.ShapeDtypeStruct((batch, HEADS, chunks, CHUNK, CHUNK), F32))
        + tuple(jax.ShapeDtypeStruct((N_CHIPS,) + p.shape, p.dtype) for p in pieces),
        grid=(segs,),
        in_specs=[rows_spec(GDN_WIDTH), rows_spec(GDN_WIDTH), rows_spec(GDN_WIDTH), rows_spec(LANES)] + [_hbm()] * count,
        out_specs=(rows_spec(GDN_WIDTH), per_chunk(HEAD_DIM, HEAD_DIM), per_chunk(CHUNK, CHUNK)) + (_hbm(),) * count,
        scratch_shapes=[pltpu.VMEM((batch, HEADS, HEAD_DIM, HEAD_DIM), F32), pltpu.SemaphoreType.DMA((sems,)),
                        pltpu.SemaphoreType.DMA((sems,))],
        compiler_params=_params("arbitrary"),
    )(as_seqs(q), as_seqs(k), as_seqs(v), as_seqs(bg), *pieces)
    return o.reshape(n, GDN_WIDTH), states, t_invs, gathered


def _gdn_bwd(do, q, k, v, bg, states, t_invs, rs, parts):
    n = q.shape[0]
    batch = n // rs
    chunks, seg_chunks, segs = _gdn_segments(rs, (3, 4, 2))
    seg_rows = seg_chunks * CHUNK
    chains = [(b, h) for b in range(batch) for h in range(HEADS)]
    each = lambda f, *lists: [f(*args) for args in zip(*lists)]
    count = len(parts)

    def body(do_ref, q_ref, k_ref, v_ref, bg_ref, s_ref, t_ref, *rest):
        p_refs, (dq_ref, dk_ref, dv_ref, dbg_ref), got_refs = rest[:count], rest[count:count + 4], rest[count + 4:2 * count + 4]
        dstate_ref, send_sems, recv_sems = rest[2 * count + 4:]
        exchange = _chip_copies(p_refs, got_refs, send_sems, recv_sems)

        @pl.when(pl.program_id(0) == 0)
        def _():
            dstate_ref[...] = jnp.zeros_like(dstate_ref)
            for cp in exchange:
                cp.start()

        ii, jj = _chunk_masks()
        incl = ii >= jj
        strict = ii > jj
        lane = lax.broadcasted_iota(jnp.int32, (1, LANES), 1)

        def rowsum(x):
            return jnp.sum(x, axis=1, keepdims=True)

        def total(x):
            return jnp.sum(rowsum(x), axis=0, keepdims=True)

        def chunk(step, carry):
            c = seg_chunks - 1 - step
            rows = pl.ds(pl.multiple_of(c * CHUNK, CHUNK), CHUNK)
            bgc = [bg_ref[b, rows, :] for b in range(batch)]
            qc = [q_ref[b, rows, _lanes_of(h)] for b, h in chains]
            kc = [k_ref[b, rows, _lanes_of(h)] for b, h in chains]
            vc = [v_ref[b, rows, _lanes_of(h)] for b, h in chains]
            doc = [do_ref[b, rows, _lanes_of(h)] for b, h in chains]
            beta = [bgc[b][:, h:h + 1] for b, h in chains]
            state = [s_ref[b, h, c] for b, h in chains]
            t_inv = [t_ref[b, h, c] for b, h in chains]
            d_state = [dstate_ref[b, h] for b, h in chains]
            dec = [_chunk_decay(bgc[b][:, HEADS + h:HEADS + h + 1], ii, jj) for b, h in chains]
            gc_col, g_total, decay = ([d[i] for d in dec] for i in range(3))
            kb = each(lambda x, y: x * y, kc, beta)
            vb = each(lambda x, y: x * y, vc, beta)
            eg = [jnp.exp(g) for g in gc_col]
            kbg = each(lambda x, y: x * y, kb, eg)
            a = each(lambda x, y, d: jnp.where(strict, _dot_nt(x, y) * d, 0.0), kb, kc, decay)
            qk = each(lambda x, y, d: jnp.where(incl, _dot_nt(x, y) * d, 0.0), qc, kc, decay)
            w = each(_dot, t_inv, kbg)
            u = each(_dot, t_inv, vb)
            q_dec = each(lambda x, y: x * y, qc, eg)
            ek = each(lambda gt, g: jnp.exp(gt - g), g_total, gc_col)
            k_dec = each(lambda x, y: x * y, kc, ek)
            g_last = [jnp.exp(gt) for gt in g_total]
            v_new = each(lambda x, y, s: x - _dot(y, s), u, w, state)
            dv_new = each(lambda m, d, x, ds: _dot_tn(m, d) + _dot(x, ds), qk, doc, k_dec, d_state)
            dqk = each(lambda d, vn: jnp.where(incl, _dot_nt(d, vn), 0.0), doc, v_new)
            dq_dec = each(_dot_nt, doc, state)
            dk_dec = each(_dot_nt, v_new, d_state)
            dg_last = each(lambda s, ds: total(s * ds), state, d_state)
            new_d_state = each(lambda x, d, gl, ds, y, dvn: _dot_tn(x, d) + gl * ds - _dot_tn(y, dvn),
                               q_dec, doc, g_last, d_state, w, dv_new)
            dw = each(lambda dvn, s: -_dot_nt(dvn, s), dv_new, state)
            dt = each(lambda dvn, x, y, z: _dot_nt(dvn, x) + _dot_nt(y, z), dv_new, vb, dw, kbg)
            dvb = each(_dot_tn, t_inv, dv_new)
            dkbg = each(_dot_tn, t_inv, dw)
            t_dt = each(_dot_tn, t_inv, dt)
            da = each(lambda x, t: -jnp.where(strict, _dot_nt(x, t), 0.0), t_dt, t_inv)
            dm_a = each(lambda x, y: x * y, da, decay)
            dm_qk = each(lambda x, y: x * y, dqk, decay)
            e = each(lambda x, y, z, t: x * y + z * t, da, a, dqk, qk)
            dkb = each(lambda m, x, y, z: _dot(m, x) + y * z, dm_a, kc, dkbg, eg)
            dk = each(lambda m, x, m2, y, z, t, p, bt: _dot_tn(m, x) + _dot_tn(m2, y) + z * t + p * bt,
                      dm_a, kb, dm_qk, qc, dk_dec, ek, dkb, beta)
            dq = each(lambda m, x, y, z: _dot(m, x) + y * z, dm_qk, kc, dq_dec, eg)
            dbeta = each(lambda x, y, z, t: rowsum(x * y + z * t), dkb, kc, dvb, vc)
            dgc = each(lambda x, p, pd, r, rd, s, sd: rowsum(x) - rowsum(jnp.where(ii == jj, jnp.sum(x, axis=0, keepdims=True), 0.0))
                       + rowsum(p * pd - r * rd + s * sd), e, dq_dec, q_dec, dk_dec, k_dec, dkbg, kbg)
            d_total = each(lambda r, rd, x, gl: total(r * rd) + x * gl, dk_dec, k_dec, dg_last, g_last)
            dg = each(lambda x, t: rowsum(jnp.where(jj >= ii, jnp.sum(jnp.where(ii == jj, x, 0.0), axis=0, keepdims=True), 0.0)) + t,
                      dgc, d_total)
            dbg = [jnp.zeros((CHUNK, LANES), F32) for _ in range(batch)]
            for i, (b, h) in enumerate(chains):
                dstate_ref[b, h] = new_d_state[i]
                dk_ref[b, rows, _lanes_of(h)] = dk[i]
                dq_ref[b, rows, _lanes_of(h)] = dq[i]
                dv_ref[b, rows, _lanes_of(h)] = dvb[i] * beta[i]
                dbg[b] = dbg[b] + jnp.where(lane == h, dbeta[i], 0.0) + jnp.where(lane == HEADS + h, dg[i], 0.0)
            for b in range(batch):
                dbg_ref[b, rows, :] = dbg[b]
            return carry

        lax.fori_loop(0, seg_chunks, chunk, 0)

        @pl.when(pl.program_id(0) == segs - 1)
        def _():
            for cp in exchange:
                cp.wait_recv()
            for cp in exchange:
                cp.wait_send()

    rows_spec = lambda width: pl.BlockSpec((batch, seg_rows, width), lambda s: (0, segs - 1 - s, 0))
    per_chunk = lambda r, c: pl.BlockSpec((batch, HEADS, seg_chunks, r, c), lambda s: (0, 0, segs - 1 - s, 0, 0))
    as_seqs = lambda a: a.reshape(batch, rs, a.shape[-1])
    grad = jax.ShapeDtypeStruct((batch, rs, GDN_WIDTH), F32)
    wide = rows_spec(GDN_WIDTH)
    dq, dk, dv, dbg, *got = pl.pallas_call(
        body, name="gdn_bwd",
        out_shape=(grad, grad, grad, jax.ShapeDtypeStruct((batch, rs, LANES), F32))
        + tuple(jax.ShapeDtypeStruct((3,) + p.shape[1:], p.dtype) for p in parts),
        grid=(segs,),
        in_specs=[wide, wide, wide, wide, rows_spec(LANES), per_chunk(HEAD_DIM, HEAD_DIM), per_chunk(CHUNK, CHUNK)]
        + [_hbm()] * count,
        out_specs=(wide, wide, wide, rows_spec(LANES)) + (_hbm(),) * count,
        scratch_shapes=[pltpu.VMEM((batch, HEADS, HEAD_DIM, HEAD_DIM), F32), pltpu.SemaphoreType.DMA((3 * count,)),
                        pltpu.SemaphoreType.DMA((3 * count,))],
        compiler_params=_params("arbitrary"),
    )(as_seqs(do), as_seqs(q), as_seqs(k), as_seqs(v), as_seqs(bg), states, t_invs, *parts)
    return dq.reshape(n, GDN_WIDTH), dk.reshape(n, GDN_WIDTH), dv.reshape(n, GDN_WIDTH), dbg.reshape(n, LANES), got


def _lane_vec(vals, offset):
    k = vals.shape[1]
    return jnp.pad(vals, ((0, 0), (offset, LANES - offset - k)))


LATER = ("w_out", "w_gate", "w_up", "w_down")


def _halves(a):
    return a.reshape(a.shape[:-2] + (2, a.shape[-2] // 2, a.shape[-1]))


def _local_step(x, target, meta, norms, w_in_shard, conv_qkv, a_log, dt_bias, gdn_norm, conv_sc, later_shards, core_arg):
    batch, seq, d = x.shape
    tokens = N_META + seq
    pad_rows = (-tokens) % CHUNK
    rs = tokens + pad_rows
    x_offset = pad_rows + N_META
    n = batch * rs
    w_mix_pre, w_mix_post, w_ffn_pre, w_ffn_post = norms

    head = jnp.concatenate([jnp.zeros((pad_rows, d), F32), meta], axis=0)
    a_log_l = _lane_vec(a_log, HEADS)
    dt_bias_l = _lane_vec(dt_bias, HEADS)

    h0, u1, w_in_all = _embed(x, head, w_mix_pre, w_in_shard, rs)
    w_in_t = _in_to_kernel_order(w_in_all.reshape(N_CHIPS, -1, d))
    proj = _mm(u1, w_in_t, "nt", F32, "mm_proj")
    q = _qkv_fwd(proj, conv_qkv, "q", rs)
    k = _qkv_fwd(proj, conv_qkv, "k", rs)
    v = _qkv_fwd(proj, conv_qkv, "v", rs)
    bg = _gates_fwd(proj, a_log_l, dt_bias_l, rs, pad_rows)
    o, states, t_invs, gathered = _gdn_fwd(q, k, v, bg, rs, later_shards[:3])
    w_out, w_gate_t, w_up_t = (a.reshape(-1, d) for a in gathered)
    cat = _sc_fwd(proj, conv_sc, rs, _gate_fwd(o, proj, gdn_norm, rs))
    mix, h1, u2 = _mix_residual(cat, w_out, h0, w_mix_post, w_ffn_pre)
    gate, up, act, w_down = _swiglu_fwd(u2, w_gate_t, w_up_t, later_shards[3])
    w_down = w_down.reshape(-1, d)
    ffn = _mm(act, w_down, "nn", F32, "mm_down")

    dh2, dffn, d_ffn_post, sq = _loss_head(h1, ffn, w_ffn_post, target, rs, x_offset)
    d_w_down = _mm(act, dffn, "tn", F32, "mm_dw_down")
    dgate, dup = _swiglu_bwd(dffn, w_down, gate, up)
    d_w_gate_t = _mm(dgate, u2, "tn", F32, "mm_dw_gate")
    d_w_up_t = _mm(dup, u2, "tn", F32, "mm_dw_up")
    du2_gate = _mm(dgate, w_gate_t, "nn", F32, "mm_du2_gate")
    by_chip = [_halves(g.reshape(N_CHIPS, -1, d)) for g in (d_w_gate_t, d_w_up_t, d_w_down)]
    dh1, dmix, d_ffn_pre, d_mix_post, got_sibling = _mid_bwd(h1, mix, w_mix_post, w_ffn_pre, dh2, du2_gate, dup, w_up_t, by_chip)
    dcat = _mm(dmix, w_out, "nt", F32, "mm_dcat")
    d_w_out = _halves(_mm(cat, dmix, "tn", F32, "mm_dw_out").reshape(N_CHIPS, -1, d))
    do, dproj, d_gdn_norm, got_out = _gate_bwd(dcat, o, proj, gdn_norm, rs, d_w_out)
    sums = (_add_sibling([d_w_out], [got_out], core_arg, "w_out")
            + _add_sibling(by_chip, got_sibling, core_arg, "ffn"))
    dproj, dscb, dscc, d_conv_sc = _sc_bwd(dcat, proj, conv_sc, rs, dproj)
    dq, dk, dv, dbg, got_chips = _gdn_bwd(do, q, k, v, bg, states, t_invs, rs, [send for _, send in sums[:3]])
    dproj, dwq = _qkv_bwd(dq, proj, conv_qkv, "q", rs, dproj)
    dproj, dwk = _qkv_bwd(dk, proj, conv_qkv, "k", rs, dproj)
    dproj, dwv = _qkv_bwd(dv, proj, conv_qkv, "v", rs, dproj)
    d_conv_qkv = jnp.concatenate([dwq, dwk, dwv], axis=1)
    dproj, d_a_log_l, d_dt_bias_l = _gates_bwd(proj, dbg, a_log_l, dt_bias_l, rs, pad_rows, dproj)
    dproj = lax.dynamic_update_slice(dproj, dscb, (0, (SC_COL + HEADS) * LANES))
    dproj = lax.dynamic_update_slice(dproj, dscc, (0, (SC_COL + 2 * HEADS) * LANES))
    d_w_in_t, got_down = _mm(dproj, u1, "tn", F32, "mm_dw_in", exchange=[sums[3][1]])
    got_chips.append(got_down)
    g_in = _halves(_in_from_kernel_order(d_w_in_t))
    sums = _add_sibling([g_in], _exchange_siblings([g_in]), core_arg, "w_in") + sums
    du1, got_in = _mm(dproj, w_in_t, "nn", F32, "mm_du1", exchange=[sums[0][1]])
    got_chips.insert(0, got_in)
    grad_x, d_meta, d_mix_pre = _in_bwd(h0, w_mix_pre, dh1, du1, rs, pad_rows, x_offset)

    grads = dict(
        meta_tokens=d_meta,
        mix_pre_norm=d_mix_pre, mix_post_norm=d_mix_post, ffn_pre_norm=d_ffn_pre, ffn_post_norm=d_ffn_post,
        conv_qkv=d_conv_qkv,
        a_log=d_a_log_l[:, HEADS:2 * HEADS], dt_bias=d_dt_bias_l[:, HEADS:2 * HEADS],
        gdn_norm=d_gdn_norm, conv_sc=d_conv_sc,
    )
    return sq, grad_x, grads, [(part, got) for (part, _), got in zip(sums, got_chips)]


MATRICES = ("w_in", "w_out", "w_gate", "w_up", "w_down")
IN_SHARD = IN_WIDTH // N_CHIPS
IN_SHARD_PAD = 928


IN_SEGMENTS = ((0, 0, 4 * GDN_WIDTH), (4 * GDN_WIDTH, IN_WIDTH - 2 * HEADS, 2 * HEADS),
               (4 * GDN_WIDTH + 2 * HEADS, 4 * GDN_WIDTH, 3 * SC_WIDTH))
SUBLANES = 8
PACKED_ROWS = 16


def _in_to_kernel_order(by_chip):
    d = by_chip.shape[-1]
    tl = _pick(d, (256, 128))
    runs = []
    for ref0, ker0, count in IN_SEGMENTS:
        row = ref0
        while row < ref0 + count:
            chip, at = divmod(row, IN_SHARD)
            take = min(ref0 + count - row, IN_SHARD - at)
            runs.append((ker0 + row - ref0, take, chip * IN_SHARD_PAD + at))
            row += take

    def body(w_ref, o_ref):
        o_ref[...] = jnp.zeros_like(o_ref)
        for out0, rows, src0 in runs:
            a0 = out0 // PACKED_ROWS * PACKED_ROWS
            a1 = -(-(out0 + rows) // PACKED_ROWS) * PACKED_ROWS
            window = w_ref[pl.ds(src0 - (out0 - a0), a1 - a0), :]
            row = a0 + lax.broadcasted_iota(jnp.int32, (a1 - a0, 1), 0)
            keep = jnp.logical_and(row >= out0, row < out0 + rows)
            o_ref[a0:a1, :] = jnp.where(keep, window, o_ref[a0:a1, :])

    return pl.pallas_call(
        body, name="in_to_kernel_order", out_shape=jax.ShapeDtypeStruct((IN_PAD, d), by_chip.dtype), grid=(d // tl,),
        in_specs=[pl.BlockSpec((N_CHIPS * IN_SHARD_PAD, tl), lambda j: (0, j))],
        out_specs=pl.BlockSpec((IN_PAD, tl), lambda j: (0, j)),
        compiler_params=_params("parallel"),
    )(by_chip.reshape(N_CHIPS * IN_SHARD_PAD, d))


def _in_from_kernel_order(g_t):
    d = g_t.shape[-1]
    tl = _pick(d, (256, 128))

    def body(g_ref, o_ref):
        row = lax.broadcasted_iota(jnp.int32, (IN_SHARD_PAD, 1), 0)
        for chip in range(N_CHIPS):
            first = chip * IN_SHARD
            runs = []
            for ref0, ker0, count in IN_SEGMENTS:
                lo, hi = max(ref0, first), min(ref0 + count, first + IN_SHARD)
                if lo < hi:
                    runs.append((lo - first, hi - lo, ker0 + lo - ref0))
            val = jnp.zeros((IN_SHARD_PAD, tl), F32)
            patches = []
            for out0, rows, src0 in runs:
                start = src0 - out0
                if 0 <= start <= IN_PAD - IN_SHARD_PAD:
                    window = g_ref[pl.ds(start, IN_SHARD_PAD), :]
                    val = jnp.where(jnp.logical_and(row >= out0, row < out0 + rows), window, val)
                else:
                    patches.append((out0, rows, src0))
            o_ref[chip] = val
            for out0, rows, src0 in patches:
                a0 = out0 // SUBLANES * SUBLANES
                a1 = -(-(out0 + rows) // SUBLANES) * SUBLANES
                window = g_ref[pl.ds(src0 - (out0 - a0), a1 - a0), :]
                keep = jnp.logical_and(row[a0:a1] >= out0, row[a0:a1] < out0 + rows)
                o_ref[chip, a0:a1, :] = jnp.where(keep, window, o_ref[chip, a0:a1, :])

    return pl.pallas_call(
        body, name="in_from_kernel_order", out_shape=jax.ShapeDtypeStruct((N_CHIPS, IN_SHARD_PAD, d), F32), grid=(d // tl,),
        in_specs=[pl.BlockSpec((IN_PAD, tl), lambda j: (0, j))],
        out_specs=pl.BlockSpec((N_CHIPS, IN_SHARD_PAD, tl), lambda j: (0, 0, j)),
        compiler_params=_params("parallel"),
    )(g_t)


PACK_LANES = 3 * GDN_WIDTH
PACKED = dict(mix_pre_norm=(0, 1, 0, D_MODEL), mix_post_norm=(1, 1, 0, D_MODEL), ffn_pre_norm=(2, 1, 0, D_MODEL),
              ffn_post_norm=(3, 1, 0, D_MODEL), a_log=(4, 1, 0, HEADS), dt_bias=(5, 1, 0, HEADS), loss=(6, 1, 0, 1),
              gdn_norm=(7, 1, 0, HEAD_DIM), conv_qkv=(8, GDN_CONV, 0, 3 * GDN_WIDTH), conv_sc=(0, SC_CONV, D_MODEL, SC_WIDTH),
              meta_tokens=(16, N_META, 0, D_MODEL))
PACK_ROWS = 32
SHARDED_SMALL = ("conv_qkv", "conv_sc", "meta_tokens")


def _pack_small(values):
    names = list(PACKED)

    def body(*refs):
        out_ref = refs[-1]
        out_ref[...] = jnp.zeros_like(out_ref)
        for name, ref in zip(names, refs):
            row, rows, lane0, lanes = PACKED[name]
            out_ref[row:row + rows, lane0:lane0 + lanes] = ref[...]

    return pl.pallas_call(body, name="pack_small", out_shape=jax.ShapeDtypeStruct((PACK_ROWS, PACK_LANES), F32))(
        *[values[name] for name in names])


def _sum_devices(packed_all, chip):
    names = list(PACKED)

    def body(chip_ref, all_ref, *rest):
        shard_refs, out_refs = rest[:len(SHARDED_SMALL)], rest[len(SHARDED_SMALL):]

        def total(ref, rows, lanes):
            acc = ref[0, rows, lanes]
            for k in range(1, N_CHIPS):
                acc = acc + ref[k, rows, lanes]
            return acc

        for name, out in zip(names, out_refs):
            row, rows, lane0, lanes = PACKED[name]
            if name in SHARDED_SMALL:
                out[...] = total(shard_refs[SHARDED_SMALL.index(name)], slice(0, rows), slice(None))
            else:
                out[...] = total(all_ref, slice(row, row + rows), slice(lane0, lane0 + lanes))

    def shard_spec(name):
        row, rows, lane0, lanes = PACKED[name]
        height, width = max(rows, 8), lanes // N_CHIPS
        assert row % height == 0 and lane0 % width == 0
        return pl.BlockSpec((N_CHIPS, height, width), lambda i, chip_ref: (0, row // height, lane0 // width + chip_ref[0]))

    def out_shape(name):
        _, rows, _, lanes = PACKED[name]
        return jax.ShapeDtypeStruct((rows, lanes // N_CHIPS if name in SHARDED_SMALL else lanes), F32)

    whole = lambda shape: pl.BlockSpec(shape, lambda i, chip_ref: (0,) * len(shape))
    outs = pl.pallas_call(
        body, name="sum_devices", out_shape=tuple(out_shape(n) for n in names),
        grid_spec=pltpu.PrefetchScalarGridSpec(
            num_scalar_prefetch=1, grid=(1,),
            in_specs=[whole(packed_all.shape)] + [shard_spec(n) for n in SHARDED_SMALL],
            out_specs=tuple(whole(out_shape(n).shape) for n in names)),
    )(chip, packed_all, *[packed_all] * len(SHARDED_SMALL))
    return dict(zip(names, outs))


def _hbm():
    return pl.BlockSpec(memory_space=pl.ANY)


def _place():
    x, y, c = lax.axis_index("x"), lax.axis_index("y"), lax.axis_index("c")
    chips = ((1 - x, y), (x, 1 - y), (1 - x, 1 - y))
    return x, y, c, chips


def _remote(src, dst, send_sems, recv_sems, k, to):
    return pltpu.make_async_remote_copy(src_ref=src, dst_ref=dst, send_sem=send_sems.at[k], recv_sem=recv_sems.at[k],
                                        device_id=to, device_id_type=MESH)


GATHER_SEMS = 7


def _gather_copies(w_refs, out_refs, send_sems, recv_sems):
    x, y, c, chips = _place()
    mine = 2 * x + y
    sibling = (x, y, 1 - c)
    copy = functools.partial(_remote, send_sems=send_sems, recv_sems=recv_sems)
    direct, landed, passing, from_sibling = [], [], [], []
    for i, (w, o) in enumerate(zip(w_refs, out_refs)):
        k = GATHER_SEMS * i
        direct.append(copy(w, o.at[mine], k=k, to=sibling))
        from_sibling.append(copy(w, o.at[mine], k=k, to=sibling))
        for j, (cx, cy) in enumerate(chips):
            theirs = 2 * cx + cy
            direct.append(copy(w.at[c], o.at[mine, c], k=k + 1 + j, to=(cx, cy, c)))
            landed.append(copy(w.at[c], o.at[theirs, c], k=k + 1 + j, to=sibling))
            passing.append(copy(o.at[theirs, c], o.at[theirs, c], k=k + 4 + j, to=sibling))
            from_sibling.append(copy(w.at[c], o.at[theirs, 1 - c], k=k + 4 + j, to=sibling))
    return direct, landed, passing, from_sibling


def _gather_finish(copies):
    direct, landed, passing, from_sibling = copies
    for arrival, forward in zip(landed, passing):
        arrival.wait_recv()
        forward.start()
    for arrival in from_sibling:
        arrival.wait_recv()
    for cp in direct + passing:
        cp.wait_send()


def _gather_weights(pieces, smalls):
    count, extra = len(pieces), len(smalls)
    total = count + extra

    def body(*refs):
        w_refs, s_refs = refs[:count], refs[count:total]
        out_refs, sall_refs = refs[total:total + count], refs[total + count:2 * total]
        send_sems, recv_sems, local_sems = refs[2 * total:]
        x, y, c, chips = _place()
        mine = 2 * x + y
        own = [pltpu.make_async_copy(s, sall.at[mine], local_sems.at[i]) for i, (s, sall) in enumerate(zip(s_refs, sall_refs))]
        small = [_remote(s, sall.at[mine], send_sems, recv_sems, GATHER_SEMS * count + 3 * i + j, (cx, cy, c))
                 for i, (s, sall) in enumerate(zip(s_refs, sall_refs)) for j, (cx, cy) in enumerate(chips)]
        copies = _gather_copies(w_refs, out_refs, send_sems, recv_sems)
        for cp in own + small + copies[0]:
            cp.start()
        _gather_finish(copies)
        for cp in small:
            cp.wait_recv()
        for cp in small:
            cp.wait_send()
        for cp in own:
            cp.wait()

    sems = GATHER_SEMS * count + 3 * extra
    return pl.pallas_call(
        body, name="gather_weights",
        out_shape=tuple(jax.ShapeDtypeStruct((N_CHIPS,) + p.shape, p.dtype) for p in list(pieces) + list(smalls)),
        in_specs=[_hbm()] * total, out_specs=(_hbm(),) * total,
        scratch_shapes=[pltpu.SemaphoreType.DMA((sems,)), pltpu.SemaphoreType.DMA((sems,)), pltpu.SemaphoreType.DMA((extra,))],
    )(*pieces, *smalls)


def _sibling_copies(g_refs, got_refs, send_sems, recv_sems):
    x, y, c, _ = _place()
    return [_remote(g.at[:, 1 - c], got, send_sems, recv_sems, i, (x, y, 1 - c)) for i, (g, got) in enumerate(zip(g_refs, got_refs))]


def _exchange_siblings(grads):
    count = len(grads)

    def body(*refs):
        copies = _sibling_copies(refs[:count], refs[count:2 * count], *refs[2 * count:])
        for cp in copies:
            cp.start()
        for cp in copies:
            cp.wait_recv()
        for cp in copies:
            cp.wait_send()

    return pl.pallas_call(
        body, name="exchange_siblings",
        out_shape=tuple(jax.ShapeDtypeStruct((g.shape[0],) + g.shape[2:], F32) for g in grads),
        in_specs=[_hbm()] * count, out_specs=(_hbm(),) * count,
        scratch_shapes=[pltpu.SemaphoreType.DMA((count,)), pltpu.SemaphoreType.DMA((count,))],
    )(*grads)


def _chip_copies(p_refs, got_refs, send_sems, recv_sems):
    x, y, c, chips = _place()
    return [_remote(p.at[2 * cx + cy], got.at[j], send_sems, recv_sems, 3 * i + j, (cx, cy, c))
            for i, (p, got) in enumerate(zip(p_refs, got_refs)) for j, (cx, cy) in enumerate(chips)]


def _share_halves(halves, small):
    count = len(halves)

    def body(*refs):
        h_refs, s_ref = refs[:count], refs[count]
        full_refs, sall_ref = refs[count + 1:2 * count + 1], refs[2 * count + 1]
        send_sems, recv_sems, local_sems, mine_v, theirs_v, pair_v = refs[2 * count + 2:]
        x, y, c, chips = _place()
        mine = 2 * x + y
        sibling = (x, y, 1 - c)
        swaps = [_remote(h.at[c], full.at[c], send_sems, recv_sems, i, sibling) for i, (h, full) in enumerate(zip(h_refs, full_refs))]
        swaps.append(_remote(s_ref, theirs_v, send_sems, recv_sems, count, sibling))
        load = pltpu.make_async_copy(s_ref, mine_v, local_sems.at[0])
        load.start()
        for cp in swaps:
            cp.start()
        load.wait()
        swaps[count].wait_recv()
        pair_v[...] = mine_v[...] + theirs_v[...]
        store = pltpu.make_async_copy(pair_v, sall_ref.at[mine], local_sems.at[1])
        store.start()
        spread = [_remote(pair_v, sall_ref.at[mine], send_sems, recv_sems, count + 1 + j, (cx, cy, c))
                  for j, (cx, cy) in enumerate(chips)]
        for cp in spread:
            cp.start()
        for cp in spread + swaps[:count]:
            cp.wait_recv()
        for cp in spread + swaps:
            cp.wait_send()
        store.wait()

    return pl.pallas_call(
        body, name="share_halves",
        out_shape=tuple(jax.ShapeDtypeStruct(h.shape, h.dtype) for h in halves)
        + (jax.ShapeDtypeStruct((N_CHIPS,) + small.shape, F32),),
        in_specs=[_hbm()] * (count + 1), out_specs=(_hbm(),) * (count + 1), input_output_aliases={i: i for i in range(count)},
        scratch_shapes=[pltpu.SemaphoreType.DMA((count + 4,)), pltpu.SemaphoreType.DMA((count + 4,)), pltpu.SemaphoreType.DMA((2,))]
        + [pltpu.VMEM(small.shape, F32)] * 3,
    )(*halves, small)


def _add_sibling(grads, gots, core, name):
    count = len(grads)
    chips, _, rows, cols = grads[0].shape

    def body(core_ref, *refs):
        for i in range(count):
            s = refs[i][...] + refs[count + i][...]
            refs[2 * count + 2 * i][...] = s
            refs[2 * count + 2 * i + 1][...] = s.astype(BF16)

    block = pl.BlockSpec((None, rows, cols), lambda p, core_ref: (p, 0, 0))
    own = pl.BlockSpec((None, None, rows, cols), lambda p, core_ref: (p, core_ref[0], 0, 0))
    out = pl.pallas_call(
        body, name="add_sibling_" + name,
        out_shape=(jax.ShapeDtypeStruct((chips, rows, cols), F32), jax.ShapeDtypeStruct((chips, rows, cols), BF16)) * count,
        grid_spec=pltpu.PrefetchScalarGridSpec(
            num_scalar_prefetch=1, grid=(chips,), in_specs=[own] * count + [block] * count, out_specs=(block, block) * count),
        compiler_params=_params("parallel"),
    )(core, *grads, *gots)
    return [(out[2 * i], out[2 * i + 1]) for i in range(count)]


def _add_chips(parts, gots, chip_core, name):
    count = len(parts)
    _, rows, cols = parts[0].shape
    tr = rows // 2 if rows % 32 == 0 else rows

    def body(place_ref, *refs):
        for i in range(count):
            r_ref = refs[count + i]
            refs[2 * count + i][...] = ((refs[i][...] + r_ref[0].astype(F32)) + r_ref[1].astype(F32)) + r_ref[2].astype(F32)

    return pl.pallas_call(
        body, name="add_chips_" + name, out_shape=(jax.ShapeDtypeStruct((2, rows, cols), F32),) * count,
        grid_spec=pltpu.PrefetchScalarGridSpec(
            num_scalar_prefetch=1, grid=(rows // tr,),
            in_specs=[pl.BlockSpec((None, tr, cols), lambda i, place_ref: (place_ref[0], i, 0))] * count
            + [pl.BlockSpec((3, tr, cols), lambda i, place_ref: (0, i, 0))] * count,
            out_specs=(pl.BlockSpec((None, tr, cols), lambda i, place_ref: (place_ref[1], i, 0)),) * count),
        compiler_params=_params("parallel"),
    )(chip_core, *parts, *gots)


def _adamw(w, g, m, v, name):
    rows, cols = w.shape
    tr = _pick(rows, (256, 352, 176, 128, 64, 32, 16, 8))

    def body(w_ref, g_ref, m_ref, v_ref, d_ref, nm_ref, nv_ref):
        d_ref[...], nm_ref[...], nv_ref[...] = _adamw_math(w_ref[...], g_ref[...], m_ref[...], v_ref[...])

    block = pl.BlockSpec((tr, cols), lambda i: (i, 0))
    shape = jax.ShapeDtypeStruct((rows, cols), F32)
    return pl.pallas_call(
        body, name="adamw_" + name, out_shape=(shape, shape, shape), grid=(rows // tr,),
        in_specs=[block] * 4, out_specs=(block,) * 3, compiler_params=_params("parallel"),
    )(w, g, m, v)


def _adamw_math(w, g, m, v):
    m = ADAM_B1 * m + (1.0 - ADAM_B1) * g
    v = ADAM_B2 * v + (1.0 - ADAM_B2) * (g * g)
    m_hat = m / (1.0 - ADAM_B1 ** ADAM_STEP)
    v_hat = v / (1.0 - ADAM_B2 ** ADAM_STEP)
    return -ADAM_LR * (m_hat / (jnp.sqrt(v_hat) + ADAM_EPS) + ADAM_WD * w), m, v


def _adamw_small(ws, gs, ms, vs):
    count = len(ws)

    def body(*refs):
        ins, outs = refs[:4 * count], refs[4 * count:]
        for i in range(count):
            outs[i][...], outs[count + i][...], outs[2 * count + i][...] = _adamw_math(
                ins[i][...], ins[count + i][...], ins[2 * count + i][...], ins[3 * count + i][...])

    shapes = tuple(jax.ShapeDtypeStruct(w.shape, F32) for w in ws)
    out = pl.pallas_call(body, name="adamw_small", out_shape=shapes * 3)(*ws, *gs, *ms, *vs)
    return out[:count], out[count:2 * count], out[2 * count:]


WEIGHTS = ("meta_tokens", "mix_pre_norm", "mix_post_norm", "ffn_pre_norm", "ffn_post_norm", "w_in", "conv_qkv", "a_log",
           "dt_bias", "gdn_norm", "conv_sc", "w_out", "w_gate", "w_up", "w_down")


def kernel(x, meta_tokens, mix_pre_norm, mix_post_norm, ffn_pre_norm, ffn_post_norm, w_in, conv_qkv, a_log, dt_bias, gdn_norm, conv_sc, w_out, w_gate, w_up, w_down, loss_target, m_meta_tokens, m_mix_pre_norm, m_mix_post_norm, m_ffn_pre_norm, m_ffn_post_norm, m_w_in, m_conv_qkv, m_a_log, m_dt_bias, m_gdn_norm, m_conv_sc, m_w_out, m_w_gate, m_w_up, m_w_down, v_meta_tokens, v_mix_pre_norm, v_mix_post_norm, v_ffn_pre_norm, v_ffn_post_norm, v_w_in, v_conv_qkv, v_a_log, v_dt_bias, v_gdn_norm, v_conv_sc, v_w_out, v_w_gate, v_w_up, v_w_down):
    d = x.shape[-1]
    two_d = lambda a: a.reshape(a.shape[-2:])
    weights = dict(zip(WEIGHTS, (meta_tokens, mix_pre_norm, mix_post_norm, ffn_pre_norm, ffn_post_norm, w_in, conv_qkv, a_log,
                                 dt_bias, gdn_norm, conv_sc, w_out, w_gate, w_up, w_down)))
    m_in = dict(zip(WEIGHTS, (m_meta_tokens, m_mix_pre_norm, m_mix_post_norm, m_ffn_pre_norm, m_ffn_post_norm, m_w_in, m_conv_qkv,
                              m_a_log, m_dt_bias, m_gdn_norm, m_conv_sc, m_w_out, m_w_gate, m_w_up, m_w_down)))
    v_in = dict(zip(WEIGHTS, (v_meta_tokens, v_mix_pre_norm, v_mix_post_norm, v_ffn_pre_norm, v_ffn_post_norm, v_w_in, v_conv_qkv,
                              v_a_log, v_dt_bias, v_gdn_norm, v_conv_sc, v_w_out, v_w_gate, v_w_up, v_w_down)))
    core = lax.axis_index("c")
    chip = 2 * lax.axis_index("x") + lax.axis_index("y")
    core_arg = core.reshape(1).astype(jnp.int32)
    chip_core = jnp.stack([chip, core]).astype(jnp.int32)
    whole = lambda a: a.reshape(a.shape[:-3] + (2 * a.shape[-2], d))
    by_rows = lambda n, a: two_d(a).T if n in ("w_in", "w_gate", "w_up") else two_d(a)

    shard = {n: by_rows(n, weights[n]).astype(MXU_DTYPE) for n in MATRICES}
    shard["w_in"] = jnp.pad(shard["w_in"], ((0, IN_SHARD_PAD - IN_SHARD), (0, 0)))
    small_all = _gather_weights([], [two_d(weights[n]) for n in SHARDED_SMALL])
    conv_qkv_full, conv_sc_full, meta_full = (jnp.concatenate([a[p] for p in range(N_CHIPS)], axis=1) for a in small_all)

    sq, grad_x, g, sums = _local_step(
        x, loss_target, meta_full, (mix_pre_norm, mix_post_norm, ffn_pre_norm, ffn_post_norm), _halves(shard["w_in"]),
        conv_qkv_full, a_log, dt_bias, gdn_norm, conv_sc_full, [_halves(shard[n]) for n in LATER], core_arg)

    parts, gots = zip(*sums)
    totals = [_add_chips(parts[i:i + 1], gots[i:i + 1], chip_core, MATRICES[i])[0] for i in range(2)]
    totals += _add_chips(parts[2:], gots[2:], chip_core, "ffn")
    *shared, packed_all = _share_halves(totals, _pack_small(dict(g, loss=sq)))
    grads = {n: whole(a) for n, a in zip(MATRICES, shared)}
    grads["w_in"] = grads["w_in"][:IN_SHARD]
    grads.update(_sum_devices(packed_all, chip.reshape(1).astype(jnp.int32)))
    loss = (0.5 / d) * grads.pop("loss")[0, 0]

    small = [n for n in WEIGHTS if n not in MATRICES]
    updates = dict(zip(small, zip(*_adamw_small(*([by_rows(n, params[n]) for n in small] for params in (weights, grads, m_in, v_in))))))
    outs = [[], [], [], []]
    for n in WEIGHTS:
        shape = weights[n].shape
        if n in MATRICES:
            updates[n] = _adamw(by_rows(n, weights[n]), grads[n], by_rows(n, m_in[n]), by_rows(n, v_in[n]), n)
        for out, a in zip(outs, (grads[n], *updates[n])):
            out.append((a.T if n in ("w_in", "w_gate", "w_up") else a).reshape(shape))
    return (loss, grad_x, *outs[0], *outs[1], *outs[2], *outs[3])
```
